```python
import math
import jax, jax.numpy as jnp
from jax import lax
import numpy as np

D_MODEL = 1024
BATCH = 8
SEQ = 2048
DEPTH = 2

EPS = 1e-6
SSM_WIDTH = D_MODEL // 2
SSM_GROUP_SIZE = 16
SSM_GROUPS = SSM_WIDTH // SSM_GROUP_SIZE
SSM_STATE = 64
DT_MIN = 1e-3
DT_MAX = 1e-1
POOL_WIDTH = D_MODEL // 2
POOL_WINDOWS = (2, 4, 8, 16)
POOL_GROUPS = len(POOL_WINDOWS)
POOL_GROUP = POOL_WIDTH // POOL_GROUPS
N_IN = 2 * SSM_WIDTH + 2 * POOL_WIDTH + 2 * D_MODEL
SPLITS = (SSM_WIDTH, 2 * SSM_WIDTH, 2 * SSM_WIDTH + POOL_WIDTH,
          2 * SSM_WIDTH + 2 * POOL_WIDTH, 2 * SSM_WIDTH + 2 * POOL_WIDTH + D_MODEL)

kernel_name = "hawk_merge_s5_pool_hybrid"


def _rmsnorm(x, g):
    x32 = x.astype(jnp.float32)
    y = x32 * lax.rsqrt(jnp.mean(x32 * x32, axis=-1, keepdims=True) + EPS)
    return y.astype(x.dtype) * g


def _complex_linear_combine(left, right):
    a1r, a1i, b1r, b1i = left
    a2r, a2i, b2r, b2i = right
    ar = a2r * a1r - a2i * a1i
    ai = a2r * a1i + a2i * a1r
    br = a2r * b1r - a2i * b1i + b2r
    bi = a2r * b1i + a2i * b1r + b2i
    return ar, ai, br, bi


def _s5_branch(u, log_dt, lam_re, lam_im, b_re, b_im, c_re, c_im, d_skip, w_glu, b_glu):
    bsz, seq, _ = u.shape
    ug = u.reshape(bsz, seq, SSM_GROUPS, SSM_GROUP_SIZE)
    dt = jnp.exp(log_dt)[:, None]
    mag = jnp.exp(lam_re * dt)
    ang = lam_im * dt
    abar_re = mag * jnp.cos(ang)
    abar_im = mag * jnp.sin(ang)
    num_re = abar_re - 1.0
    num_im = abar_im
    den = lam_re * lam_re + lam_im * lam_im
    coef_re = (num_re * lam_re + num_im * lam_im) / den
    coef_im = (num_im * lam_re - num_re * lam_im) / den
    bbar_re = coef_re[..., None] * b_re - coef_im[..., None] * b_im
    bbar_im = coef_re[..., None] * b_im + coef_im[..., None] * b_re
    bu_re = jnp.einsum('blgc,gpc->blgp', ug, bbar_re)
    bu_im = jnp.einsum('blgc,gpc->blgp', ug, bbar_im)
    a_re = jnp.broadcast_to(abar_re, bu_re.shape)
    a_im = jnp.broadcast_to(abar_im, bu_im.shape)
    _, _, s_re, s_im = lax.associative_scan(_complex_linear_combine,
                                            (a_re, a_im, bu_re, bu_im), axis=1)
    y = (jnp.einsum('blgp,gcp->blgc', s_re, c_re)
         - jnp.einsum('blgp,gcp->blgc', s_im, c_im))
    y = y.reshape(bsz, seq, SSM_WIDTH) + d_skip * u
    y = jax.nn.gelu(y)
    return y * jax.nn.sigmoid(y @ w_glu + b_glu)


def _pool_branch(u, w_group, scale):
    bsz, seq, _ = u.shape
    u32 = u.astype(jnp.float32)
    cs = jnp.cumsum(u32, axis=1)
    pos = jnp.arange(seq)
    outs = []
    for gi, win in enumerate(POOL_WINDOWS):
        csg = cs[:, :, gi * POOL_GROUP:(gi + 1) * POOL_GROUP]
        shifted = jnp.pad(csg, ((0, 0), (win, 0), (0, 0)))[:, :seq]
        count = jnp.minimum(pos + 1, win).astype(jnp.float32)[None, :, None]
        mean = (csg - shifted) / count
        outs.append(mean - u32[:, :, gi * POOL_GROUP:(gi + 1) * POOL_GROUP])
    pooled = jnp.stack(outs, axis=2).astype(u.dtype)
    mixed = jnp.einsum('blgc,gcd->blgd', pooled, w_group).reshape(bsz, seq, POOL_WIDTH)
    return mixed * scale


def _fwd_setup_inputs(seed: int = 0) -> dict:
    key = jax.random.key(seed)
    ks = jax.random.split(key, 24)
    f32 = jnp.float32
    nrm = lambda k, shape, std: (jax.random.normal(k, shape, f32) * std)
    x = jax.random.normal(ks[0], (BATCH, SEQ, D_MODEL), f32)
    norm_g = 1.0 + nrm(ks[1], (DEPTH, D_MODEL), 0.05)
    w_in = nrm(ks[2], (DEPTH, D_MODEL, N_IN), D_MODEL ** -0.5)
    b_in = nrm(ks[3], (DEPTH, N_IN), 0.02)
    ssm_log_dt = jax.random.uniform(ks[4], (DEPTH, SSM_GROUPS), f32,
                                    math.log(DT_MIN), math.log(DT_MAX))
    n_idx = jnp.arange(SSM_STATE, dtype=f32)
    ssm_lam_re = -0.5 + nrm(ks[5], (DEPTH, SSM_GROUPS, SSM_STATE), 0.01)
    ssm_lam_im = math.pi * n_idx[None, None, :] + nrm(ks[6], (DEPTH, SSM_GROUPS, SSM_STATE), 0.01)
    b_std = (2.0 * SSM_GROUP_SIZE) ** -0.5
    ssm_b_re = nrm(ks[7], (DEPTH, SSM_GROUPS, SSM_STATE, SSM_GROUP_SIZE), b_std)
    ssm_b_im = nrm(ks[8], (DEPTH, SSM_GROUPS, SSM_STATE, SSM_GROUP_SIZE), b_std)
    c_std = SSM_STATE ** -0.5
    ssm_c_re = nrm(ks[9], (DEPTH, SSM_GROUPS, SSM_GROUP_SIZE, SSM_STATE), c_std)
    ssm_c_im = nrm(ks[10], (DEPTH, SSM_GROUPS, SSM_GROUP_SIZE, SSM_STATE), c_std)
    ssm_d = nrm(ks[11], (DEPTH, SSM_WIDTH), 1.0)
    ssm_w_glu = nrm(ks[12], (DEPTH, SSM_WIDTH, SSM_WIDTH), SSM_WIDTH ** -0.5)
    ssm_b_glu = nrm(ks[13], (DEPTH, SSM_WIDTH), 0.02)
    pool_w = nrm(ks[14], (DEPTH, POOL_GROUPS, POOL_GROUP, POOL_GROUP), POOL_GROUP ** -0.5)
    pool_scale = 1.0 + nrm(ks[15], (DEPTH, POOL_WIDTH), 0.1)
    w_branch_a = nrm(ks[16], (DEPTH, SSM_WIDTH, D_MODEL), SSM_WIDTH ** -0.5)
    w_branch_b = nrm(ks[17], (DEPTH, POOL_WIDTH, D_MODEL), POOL_WIDTH ** -0.5)
    w_out = nrm(ks[18], (DEPTH, D_MODEL, D_MODEL), D_MODEL ** -0.5)
    final_norm_g = 1.0 + nrm(ks[19], (D_MODEL,), 0.05)
    return {"x": x, "norm_g": norm_g, "w_in": w_in, "b_in": b_in,
            "ssm_log_dt": ssm_log_dt, "ssm_lam_re": ssm_lam_re, "ssm_lam_im": ssm_lam_im,
            "ssm_b_re": ssm_b_re, "ssm_b_im": ssm_b_im, "ssm_c_re": ssm_c_re, "ssm_c_im": ssm_c_im,
            "ssm_d": ssm_d, "ssm_w_glu": ssm_w_glu, "ssm_b_glu": ssm_b_glu,
            "pool_w": pool_w, "pool_scale": pool_scale,
            "w_branch_a": w_branch_a, "w_branch_b": w_branch_b, "w_out": w_out,
            "final_norm_g": final_norm_g}


def _fwd_reference(x, norm_g, w_in, b_in, ssm_log_dt, ssm_lam_re, ssm_lam_im, ssm_b_re, ssm_b_im,
              ssm_c_re, ssm_c_im, ssm_d, ssm_w_glu, ssm_b_glu, pool_w, pool_scale,
              w_branch_a, w_branch_b, w_out, final_norm_g):
    for l in range(DEPTH):
        h = _rmsnorm(x, norm_g[l])
        proj = h @ w_in[l] + b_in[l]
        ua, za, ub, zb, ga, gb = jnp.split(proj, SPLITS, axis=-1)
        ya = _s5_branch(ua, ssm_log_dt[l], ssm_lam_re[l], ssm_lam_im[l], ssm_b_re[l], ssm_b_im[l],
                        ssm_c_re[l], ssm_c_im[l], ssm_d[l], ssm_w_glu[l], ssm_b_glu[l])
        ya = ya * jax.nn.silu(za)
        yb = _pool_branch(ub, pool_w[l], pool_scale[l]) * jax.nn.silu(zb)
        merged = (jax.nn.sigmoid(ga) * (ya @ w_branch_a[l])
                  + jax.nn.sigmoid(gb) * (yb @ w_branch_b[l]))
        x = x + merged @ w_out[l]
    return _rmsnorm(x, final_norm_g)


import jax as _jax
import jax.numpy as _jnp

TWIN_FORMAT = 'train_step'
FWD_PARAMS = ['x', 'norm_g', 'w_in', 'b_in', 'ssm_log_dt', 'ssm_lam_re', 'ssm_lam_im', 'ssm_b_re', 'ssm_b_im', 'ssm_c_re', 'ssm_c_im', 'ssm_d', 'ssm_w_glu', 'ssm_b_glu', 'pool_w', 'pool_scale', 'w_branch_a', 'w_branch_b', 'w_out', 'final_norm_g']
TWIN_WEIGHTS = ['norm_g', 'w_in', 'b_in', 'ssm_log_dt', 'ssm_lam_re', 'ssm_lam_im', 'ssm_b_re', 'ssm_b_im', 'ssm_c_re', 'ssm_c_im', 'ssm_d', 'ssm_w_glu', 'ssm_b_glu', 'pool_w', 'pool_scale', 'w_branch_a', 'w_branch_b', 'w_out', 'final_norm_g']
TWIN_DIFF_INPUT = 'x'
TWIN_INPUTS = ['x', 'norm_g', 'w_in', 'b_in', 'ssm_log_dt', 'ssm_lam_re', 'ssm_lam_im', 'ssm_b_re', 'ssm_b_im', 'ssm_c_re', 'ssm_c_im', 'ssm_d', 'ssm_w_glu', 'ssm_b_glu', 'pool_w', 'pool_scale', 'w_branch_a', 'w_branch_b', 'w_out', 'final_norm_g', 'loss_target', 'm_norm_g', 'm_w_in', 'm_b_in', 'm_ssm_log_dt', 'm_ssm_lam_re', 'm_ssm_lam_im', 'm_ssm_b_re', 'm_ssm_b_im', 'm_ssm_c_re', 'm_ssm_c_im', 'm_ssm_d', 'm_ssm_w_glu', 'm_ssm_b_glu', 'm_pool_w', 'm_pool_scale', 'm_w_branch_a', 'm_w_branch_b', 'm_w_out', 'm_final_norm_g', 'v_norm_g', 'v_w_in', 'v_b_in', 'v_ssm_log_dt', 'v_ssm_lam_re', 'v_ssm_lam_im', 'v_ssm_b_re', 'v_ssm_b_im', 'v_ssm_c_re', 'v_ssm_c_im', 'v_ssm_d', 'v_ssm_w_glu', 'v_ssm_b_glu', 'v_pool_w', 'v_pool_scale', 'v_w_branch_a', 'v_w_branch_b', 'v_w_out', 'v_final_norm_g']
TWIN_OUTPUTS = ['loss', 'grad_x', 'grad_norm_g', 'grad_w_in', 'grad_b_in', 'grad_ssm_log_dt', 'grad_ssm_lam_re', 'grad_ssm_lam_im', 'grad_ssm_b_re', 'grad_ssm_b_im', 'grad_ssm_c_re', 'grad_ssm_c_im', 'grad_ssm_d', 'grad_ssm_w_glu', 'grad_ssm_b_glu', 'grad_pool_w', 'grad_pool_scale', 'grad_w_branch_a', 'grad_w_branch_b', 'grad_w_out', 'grad_final_norm_g', 'delta_norm_g', 'delta_w_in', 'delta_b_in', 'delta_ssm_log_dt', 'delta_ssm_lam_re', 'delta_ssm_lam_im', 'delta_ssm_b_re', 'delta_ssm_b_im', 'delta_ssm_c_re', 'delta_ssm_c_im', 'delta_ssm_d', 'delta_ssm_w_glu', 'delta_ssm_b_glu', 'delta_pool_w', 'delta_pool_scale', 'delta_w_branch_a', 'delta_w_branch_b', 'delta_w_out', 'delta_final_norm_g', 'new_m_norm_g', 'new_m_w_in', 'new_m_b_in', 'new_m_ssm_log_dt', 'new_m_ssm_lam_re', 'new_m_ssm_lam_im', 'new_m_ssm_b_re', 'new_m_ssm_b_im', 'new_m_ssm_c_re', 'new_m_ssm_c_im', 'new_m_ssm_d', 'new_m_ssm_w_glu', 'new_m_ssm_b_glu', 'new_m_pool_w', 'new_m_pool_scale', 'new_m_w_branch_a', 'new_m_w_branch_b', 'new_m_w_out', 'new_m_final_norm_g', 'new_v_norm_g', 'new_v_w_in', 'new_v_b_in', 'new_v_ssm_log_dt', 'new_v_ssm_lam_re', 'new_v_ssm_lam_im', 'new_v_ssm_b_re', 'new_v_ssm_b_im', 'new_v_ssm_c_re', 'new_v_ssm_c_im', 'new_v_ssm_d', 'new_v_ssm_w_glu', 'new_v_ssm_b_glu', 'new_v_pool_w', 'new_v_pool_scale', 'new_v_w_branch_a', 'new_v_w_branch_b', 'new_v_w_out', 'new_v_final_norm_g']
TWIN_LEAF_KINDS = {'loss': 'loss', 'grad_x': 'grad_x', 'grad_norm_g': 'grad_w', 'grad_w_in': 'grad_w', 'grad_b_in': 'grad_w', 'grad_ssm_log_dt': 'grad_w', 'grad_ssm_lam_re': 'grad_w', 'grad_ssm_lam_im': 'grad_w', 'grad_ssm_b_re': 'grad_w', 'grad_ssm_b_im': 'grad_w', 'grad_ssm_c_re': 'grad_w', 'grad_ssm_c_im': 'grad_w', 'grad_ssm_d': 'grad_w', 'grad_ssm_w_glu': 'grad_w', 'grad_ssm_b_glu': 'grad_w', 'grad_pool_w': 'grad_w', 'grad_pool_scale': 'grad_w', 'grad_w_branch_a': 'grad_w', 'grad_w_branch_b': 'grad_w', 'grad_w_out': 'grad_w', 'grad_final_norm_g': 'grad_w', 'delta_norm_g': 'delta_w', 'delta_w_in': 'delta_w', 'delta_b_in': 'delta_w', 'delta_ssm_log_dt': 'delta_w', 'delta_ssm_lam_re': 'delta_w', 'delta_ssm_lam_im': 'delta_w', 'delta_ssm_b_re': 'delta_w', 'delta_ssm_b_im': 'delta_w', 'delta_ssm_c_re': 'delta_w', 'delta_ssm_c_im': 'delta_w', 'delta_ssm_d': 'delta_w', 'delta_ssm_w_glu': 'delta_w', 'delta_ssm_b_glu': 'delta_w', 'delta_pool_w': 'delta_w', 'delta_pool_scale': 'delta_w', 'delta_w_branch_a': 'delta_w', 'delta_w_branch_b': 'delta_w', 'delta_w_out': 'delta_w', 'delta_final_norm_g': 'delta_w', 'new_m_norm_g': 'new_m', 'new_m_w_in': 'new_m', 'new_m_b_in': 'new_m', 'new_m_ssm_log_dt': 'new_m', 'new_m_ssm_lam_re': 'new_m', 'new_m_ssm_lam_im': 'new_m', 'new_m_ssm_b_re': 'new_m', 'new_m_ssm_b_im': 'new_m', 'new_m_ssm_c_re': 'new_m', 'new_m_ssm_c_im': 'new_m', 'new_m_ssm_d': 'new_m', 'new_m_ssm_w_glu': 'new_m', 'new_m_ssm_b_glu': 'new_m', 'new_m_pool_w': 'new_m', 'new_m_pool_scale': 'new_m', 'new_m_w_branch_a': 'new_m', 'new_m_w_branch_b': 'new_m', 'new_m_w_out': 'new_m', 'new_m_final_norm_g': 'new_m', 'new_v_norm_g': 'new_v', 'new_v_w_in': 'new_v', 'new_v_b_in': 'new_v', 'new_v_ssm_log_dt': 'new_v', 'new_v_ssm_lam_re': 'new_v', 'new_v_ssm_lam_im': 'new_v', 'new_v_ssm_b_re': 'new_v', 'new_v_ssm_b_im': 'new_v', 'new_v_ssm_c_re': 'new_v', 'new_v_ssm_c_im': 'new_v', 'new_v_ssm_d': 'new_v', 'new_v_ssm_w_glu': 'new_v', 'new_v_ssm_b_glu': 'new_v', 'new_v_pool_w': 'new_v', 'new_v_pool_scale': 'new_v', 'new_v_w_branch_a': 'new_v', 'new_v_w_branch_b': 'new_v', 'new_v_w_out': 'new_v', 'new_v_final_norm_g': 'new_v'}


def _forward(args):
    return _fwd_reference(*[args[k] for k in FWD_PARAMS])


def _output_shape():
    out = _jax.eval_shape(lambda: _forward(_fwd_setup_inputs(0)))
    return out.shape, out.dtype

N_MICROBATCH = 1
ADAM_LR = 0.001
ADAM_B1 = 0.9
ADAM_B2 = 0.999
ADAM_EPS = 1e-08
ADAM_WD = 0.01
ADAM_STEP = 10
PER_EXAMPLE_BATCH_AXIS = {'x': 0, 'loss_target': 0}
SHARED_INPUTS = []
_WEIGHT_DTYPES = {'norm_g': _jnp.float32, 'w_in': _jnp.float32, 'b_in': _jnp.float32, 'ssm_log_dt': _jnp.float32, 'ssm_lam_re': _jnp.float32, 'ssm_lam_im': _jnp.float32, 'ssm_b_re': _jnp.float32, 'ssm_b_im': _jnp.float32, 'ssm_c_re': _jnp.float32, 'ssm_c_im': _jnp.float32, 'ssm_d': _jnp.float32, 'ssm_w_glu': _jnp.float32, 'ssm_b_glu': _jnp.float32, 'pool_w': _jnp.float32, 'pool_scale': _jnp.float32, 'w_branch_a': _jnp.float32, 'w_branch_b': _jnp.float32, 'w_out': _jnp.float32, 'final_norm_g': _jnp.float32}
MOMENT_SCALE = {'norm_g': 5.537921e-02, 'w_in': 2.753182e-02, 'b_in': 2.286294e-02, 'ssm_log_dt': 1.195252e+00, 'ssm_lam_re': 1.394943e-03, 'ssm_lam_im': 1.321869e-03, 'ssm_b_re': 8.972127e-04, 'ssm_b_im': 8.951182e-04, 'ssm_c_re': 1.249456e-03, 'ssm_c_im': 1.251200e-03, 'ssm_d': 2.109677e-02, 'ssm_w_glu': 5.803262e-03, 'ssm_b_glu': 8.938468e-03, 'pool_w': 4.862920e-02, 'pool_scale': 4.902267e-02, 'w_branch_a': 1.358045e-02, 'w_branch_b': 3.441708e-02, 'w_out': 3.705689e-02, 'final_norm_g': 1.598412e+01}


def _to_microbatches(a, axis):
    t = _jnp.moveaxis(a, axis, 0)
    t = t.reshape((N_MICROBATCH, t.shape[0] // N_MICROBATCH) + t.shape[1:])
    return _jnp.moveaxis(t, 1, axis + 1)


def setup_inputs(seed: int = 0) -> dict:
    inp = _fwd_setup_inputs(seed)
    key = _jax.random.fold_in(_jax.random.key(seed), 7919)
    shape, _ = _output_shape()
    out = dict(inp)
    out["loss_target"] = _jax.random.normal(_jax.random.fold_in(key, 0), shape, _jnp.float32)
    for i, name in enumerate(TWIN_WEIGHTS):
        w = inp[name].astype(_jnp.float32)
        if MOMENT_SCALE is None:
            s = _jnp.sqrt(_jnp.mean(_jnp.square(w)) + 1e-30)
        else:
            s = MOMENT_SCALE[name]
        km, kv = _jax.random.split(_jax.random.fold_in(key, i + 1))
        out[name] = w
        out["m_" + name] = s * _jax.random.normal(km, w.shape, _jnp.float32)
        out["v_" + name] = (s * s) * _jax.random.uniform(kv, w.shape, _jnp.float32, 0.5, 1.5)
    if N_MICROBATCH > 1:
        for name, axis in PER_EXAMPLE_BATCH_AXIS.items():
            out[name] = _to_microbatches(out[name], axis)
    return {'x': out['x'], 'norm_g': out['norm_g'], 'w_in': out['w_in'], 'b_in': out['b_in'], 'ssm_log_dt': out['ssm_log_dt'], 'ssm_lam_re': out['ssm_lam_re'], 'ssm_lam_im': out['ssm_lam_im'], 'ssm_b_re': out['ssm_b_re'], 'ssm_b_im': out['ssm_b_im'], 'ssm_c_re': out['ssm_c_re'], 'ssm_c_im': out['ssm_c_im'], 'ssm_d': out['ssm_d'], 'ssm_w_glu': out['ssm_w_glu'], 'ssm_b_glu': out['ssm_b_glu'], 'pool_w': out['pool_w'], 'pool_scale': out['pool_scale'], 'w_branch_a': out['w_branch_a'], 'w_branch_b': out['w_branch_b'], 'w_out': out['w_out'], 'final_norm_g': out['final_norm_g'], 'loss_target': out['loss_target'], 'm_norm_g': out['m_norm_g'], 'm_w_in': out['m_w_in'], 'm_b_in': out['m_b_in'], 'm_ssm_log_dt': out['m_ssm_log_dt'], 'm_ssm_lam_re': out['m_ssm_lam_re'], 'm_ssm_lam_im': out['m_ssm_lam_im'], 'm_ssm_b_re': out['m_ssm_b_re'], 'm_ssm_b_im': out['m_ssm_b_im'], 'm_ssm_c_re': out['m_ssm_c_re'], 'm_ssm_c_im': out['m_ssm_c_im'], 'm_ssm_d': out['m_ssm_d'], 'm_ssm_w_glu': out['m_ssm_w_glu'], 'm_ssm_b_glu': out['m_ssm_b_glu'], 'm_pool_w': out['m_pool_w'], 'm_pool_scale': out['m_pool_scale'], 'm_w_branch_a': out['m_w_branch_a'], 'm_w_branch_b': out['m_w_branch_b'], 'm_w_out': out['m_w_out'], 'm_final_norm_g': out['m_final_norm_g'], 'v_norm_g': out['v_norm_g'], 'v_w_in': out['v_w_in'], 'v_b_in': out['v_b_in'], 'v_ssm_log_dt': out['v_ssm_log_dt'], 'v_ssm_lam_re': out['v_ssm_lam_re'], 'v_ssm_lam_im': out['v_ssm_lam_im'], 'v_ssm_b_re': out['v_ssm_b_re'], 'v_ssm_b_im': out['v_ssm_b_im'], 'v_ssm_c_re': out['v_ssm_c_re'], 'v_ssm_c_im': out['v_ssm_c_im'], 'v_ssm_d': out['v_ssm_d'], 'v_ssm_w_glu': out['v_ssm_w_glu'], 'v_ssm_b_glu': out['v_ssm_b_glu'], 'v_pool_w': out['v_pool_w'], 'v_pool_scale': out['v_pool_scale'], 'v_w_branch_a': out['v_w_branch_a'], 'v_w_branch_b': out['v_w_branch_b'], 'v_w_out': out['v_w_out'], 'v_final_norm_g': out['v_final_norm_g']}


def _loss(weights, diff, rest, loss_target):
    with _jax.named_scope("forward"):
        args = {**rest, TWIN_DIFF_INPUT: diff, **{k: w.astype(_WEIGHT_DTYPES[k]) for k, w in weights.items()}}
        y = _forward(args)
    with _jax.named_scope("loss_head"):
        err = _jnp.square(y.astype(_jnp.float32) - loss_target)
        return 0.5 * _jnp.sum(_jnp.mean(err, axis=-1)) if err.ndim else 0.5 * err


def _adamw(w, g, m, v):
    m = ADAM_B1 * m + (1.0 - ADAM_B1) * g
    v = ADAM_B2 * v + (1.0 - ADAM_B2) * _jnp.square(g)
    m_hat = m / (1.0 - ADAM_B1 ** ADAM_STEP)
    v_hat = v / (1.0 - ADAM_B2 ** ADAM_STEP)
    delta = -ADAM_LR * (m_hat / (_jnp.sqrt(v_hat) + ADAM_EPS) + ADAM_WD * w)
    return delta, m, v


def reference(x, norm_g, w_in, b_in, ssm_log_dt, ssm_lam_re, ssm_lam_im, ssm_b_re, ssm_b_im, ssm_c_re, ssm_c_im, ssm_d, ssm_w_glu, ssm_b_glu, pool_w, pool_scale, w_branch_a, w_branch_b, w_out, final_norm_g, loss_target, m_norm_g, m_w_in, m_b_in, m_ssm_log_dt, m_ssm_lam_re, m_ssm_lam_im, m_ssm_b_re, m_ssm_b_im, m_ssm_c_re, m_ssm_c_im, m_ssm_d, m_ssm_w_glu, m_ssm_b_glu, m_pool_w, m_pool_scale, m_w_branch_a, m_w_branch_b, m_w_out, m_final_norm_g, v_norm_g, v_w_in, v_b_in, v_ssm_log_dt, v_ssm_lam_re, v_ssm_lam_im, v_ssm_b_re, v_ssm_b_im, v_ssm_c_re, v_ssm_c_im, v_ssm_d, v_ssm_w_glu, v_ssm_b_glu, v_pool_w, v_pool_scale, v_w_branch_a, v_w_branch_b, v_w_out, v_final_norm_g):
    given = dict(x=x, norm_g=norm_g, w_in=w_in, b_in=b_in, ssm_log_dt=ssm_log_dt, ssm_lam_re=ssm_lam_re, ssm_lam_im=ssm_lam_im, ssm_b_re=ssm_b_re, ssm_b_im=ssm_b_im, ssm_c_re=ssm_c_re, ssm_c_im=ssm_c_im, ssm_d=ssm_d, ssm_w_glu=ssm_w_glu, ssm_b_glu=ssm_b_glu, pool_w=pool_w, pool_scale=pool_scale, w_branch_a=w_branch_a, w_branch_b=w_branch_b, w_out=w_out, final_norm_g=final_norm_g, loss_target=loss_target, m_norm_g=m_norm_g, m_w_in=m_w_in, m_b_in=m_b_in, m_ssm_log_dt=m_ssm_log_dt, m_ssm_lam_re=m_ssm_lam_re, m_ssm_lam_im=m_ssm_lam_im, m_ssm_b_re=m_ssm_b_re, m_ssm_b_im=m_ssm_b_im, m_ssm_c_re=m_ssm_c_re, m_ssm_c_im=m_ssm_c_im, m_ssm_d=m_ssm_d, m_ssm_w_glu=m_ssm_w_glu, m_ssm_b_glu=m_ssm_b_glu, m_pool_w=m_pool_w, m_pool_scale=m_pool_scale, m_w_branch_a=m_w_branch_a, m_w_branch_b=m_w_branch_b, m_w_out=m_w_out, m_final_norm_g=m_final_norm_g, v_norm_g=v_norm_g, v_w_in=v_w_in, v_b_in=v_b_in, v_ssm_log_dt=v_ssm_log_dt, v_ssm_lam_re=v_ssm_lam_re, v_ssm_lam_im=v_ssm_lam_im, v_ssm_b_re=v_ssm_b_re, v_ssm_b_im=v_ssm_b_im, v_ssm_c_re=v_ssm_c_re, v_ssm_c_im=v_ssm_c_im, v_ssm_d=v_ssm_d, v_ssm_w_glu=v_ssm_w_glu, v_ssm_b_glu=v_ssm_b_glu, v_pool_w=v_pool_w, v_pool_scale=v_pool_scale, v_w_branch_a=v_w_branch_a, v_w_branch_b=v_w_branch_b, v_w_out=v_w_out, v_final_norm_g=v_final_norm_g)
    weights = {n: given[n] for n in TWIN_WEIGHTS}
    shared = {n: given[n] for n in SHARED_INPUTS}
    per_example = {n: given[n] for n in ['x']}
    grad_fn = _jax.value_and_grad(_loss, argnums=(0, 1))

    def one_microbatch(ex, loss_target):
        ex = dict(ex)
        diff = ex.pop(TWIN_DIFF_INPUT)
        return grad_fn(weights, diff, {**shared, **ex}, loss_target)

    if N_MICROBATCH == 1:
        loss, (grad_w, grad_x) = one_microbatch(per_example, given["loss_target"])
    else:
        def body(carry, xs):
            loss_sum, grad_sum = carry
            l_k, (gw_k, gx_k) = one_microbatch(xs[0], xs[1])
            with _jax.named_scope("update"):
                return (loss_sum + l_k, _jax.tree.map(_jnp.add, grad_sum, gw_k)), gx_k

        init = (_jnp.zeros((), _jnp.float32), _jax.tree.map(_jnp.zeros_like, weights))
        (loss, grad_w), grad_x = _jax.lax.scan(body, init, (per_example, given["loss_target"]))
    with _jax.named_scope("update"):
        delta_w, new_m, new_v = {}, {}, {}
        for n in TWIN_WEIGHTS:
            delta_w[n], new_m[n], new_v[n] = _adamw(weights[n], grad_w[n], given["m_" + n], given["v_" + n])
    return (loss, grad_x, *[grad_w[n] for n in TWIN_WEIGHTS], *[delta_w[n] for n in TWIN_WEIGHTS],
            *[new_m[n] for n in TWIN_WEIGHTS], *[new_v[n] for n in TWIN_WEIGHTS])
```

```python
import functools
import math

import jax
import jax.numpy as jnp
from jax import lax
from jax.experimental import pallas as pl
from jax.experimental.pallas import tpu as pltpu

F32 = jnp.float32
BF16 = jnp.bfloat16

SEQ = 2048
D_MODEL = 1024
N_IN = 4096
WIDTH = 512
N_STATE = 2048
N_CHUNK = 4
CH_W = WIDTH // N_CHUNK
CH_S = N_STATE // N_CHUNK
N_DEV = 8
N_CHIP = 4
POOL_WINDOWS = (2, 4, 8, 16)
POOL_GROUP = 128
EPS = 1e-6
DEPTH = 2

ADAM_LR = 0.001
ADAM_B1 = 0.9
ADAM_B2 = 0.999
ADAM_EPS = 1e-08
ADAM_WD = 0.01
ADAM_STEP = 10

TILE_M = 256
ROW_BLK = 512
VMEM_LIMIT = 48 * 1024 * 1024
VMEM_LIMIT_BIG = 60 * 1024 * 1024
MESH = pl.DeviceIdType.MESH
ANY = pl.BlockSpec(memory_space=pl.ANY)

GELU_C = math.sqrt(2.0 / math.pi)
GELU_A = 0.044715


def _cp(sem=None, limit=VMEM_LIMIT):
    return pltpu.CompilerParams(dimension_semantics=sem, vmem_limit_bytes=limit)


def _dot(a, b):
    return jnp.dot(a, b, preferred_element_type=F32)


def _dot_nt(a, b):
    return lax.dot_general(a, b, (((1,), (1,)), ((), ())), preferred_element_type=F32)


def _dot_tn(a, b):
    return lax.dot_general(a, b, (((0,), (0,)), ((), ())), preferred_element_type=F32)


def _sig(x):
    return jax.nn.sigmoid(x)


def _rms(x):
    rs = lax.rsqrt(jnp.mean(x * x, axis=-1, keepdims=True) + EPS)
    return rs, x * rs


def _slot(n):
    return 4 * (n % 2) + n // 2


def _const(shape):
    n = len(shape)
    return pl.BlockSpec(shape, lambda *_: (0,) * n)


def _s5_param_fn(log_dt, lam_re, lam_im, bt_re, bt_im):
    dt = jnp.exp(log_dt)
    mag = jnp.exp(lam_re * dt)
    ang = lam_im * dt
    abar_re = mag * jnp.cos(ang)
    abar_im = mag * jnp.sin(ang)
    num_re = abar_re - 1.0
    num_im = abar_im
    den = lam_re * lam_re + lam_im * lam_im
    coef_re = (num_re * lam_re + num_im * lam_im) / den
    coef_im = (num_im * lam_re - num_re * lam_im) / den
    bbar_re = coef_re[:, None] * bt_re - coef_im[:, None] * bt_im
    bbar_im = coef_re[:, None] * bt_im + coef_im[:, None] * bt_re
    return abar_re, abar_im, bbar_re, bbar_im


def _s5_params(log_dt, lam_re, lam_im, bt_re, bt_im):
    def body(ld, lr, li, br, bi, o_ar, o_ai, o_br, o_bi):
        ar, ai, bbr, bbi = _s5_param_fn(ld[...], lr[...], li[...], br[...], bi[...])
        o_ar[...] = ar
        o_ai[...] = ai
        o_br[...] = bbr
        o_bi[...] = bbi

    s = jax.ShapeDtypeStruct
    return pl.pallas_call(
        body, name="s5_params",
        out_shape=(s(lam_re.shape, F32), s(lam_re.shape, F32), s(bt_re.shape, F32), s(bt_re.shape, F32)),
    )(log_dt, lam_re, lam_im, bt_re, bt_im)


def _s5_params_bwd(log_dt, lam_re, lam_im, bt_re, bt_im, g_ar, g_ai, g_br, g_bi):
    def body(ld, lr, li, br, bi, car, cai, cbr, cbi, o_ld, o_lr, o_li, o_br, o_bi):
        _, vjp = jax.vjp(_s5_param_fn, ld[...], lr[...], li[...], br[...], bi[...])
        d_ld, d_lr, d_li, d_br, d_bi = vjp((car[...], cai[...], cbr[...], cbi[...]))
        o_ld[...] = d_ld
        o_lr[...] = d_lr
        o_li[...] = d_li
        o_br[...] = d_br
        o_bi[...] = d_bi

    s = jax.ShapeDtypeStruct
    return pl.pallas_call(
        body, name="s5_params_bwd",
        out_shape=(s(log_dt.shape, F32), s(lam_re.shape, F32), s(lam_re.shape, F32),
                   s(bt_re.shape, F32), s(bt_re.shape, F32)),
    )(log_dt, lam_re, lam_im, bt_re, bt_im, g_ar, g_ai, g_br, g_bi)


def _norm_proj(layer, x, norm_g, wg_in, b_in):
    def body(x_ref, g_ref, w_ref, b_ref, o_ref):
        _, xn = _rms(x_ref[...])
        h = (xn * g_ref[...]).astype(BF16)
        for k in range(N_DEV):
            cols = slice(k * WIDTH, (k + 1) * WIDTH)
            o_ref[:, cols] = _dot(h, w_ref[k]) + b_ref[:, cols]

    return pl.pallas_call(
        body, name=f"norm_proj_l{layer}",
        out_shape=jax.ShapeDtypeStruct((SEQ, N_IN), F32),
        grid=(SEQ // TILE_M,),
        in_specs=[pl.BlockSpec((TILE_M, D_MODEL), lambda i: (i, 0)),
                  pl.BlockSpec((None, 1, D_MODEL), lambda i: (layer, 0, 0)),
                  pl.BlockSpec((N_DEV, None, D_MODEL, WIDTH), lambda i: (0, layer, 0, 0)),
                  pl.BlockSpec((None, 1, N_IN), lambda i: (layer, 0, 0))],
        out_specs=pl.BlockSpec((TILE_M, N_IN), lambda i: (i, 0)),
        compiler_params=_cp(("parallel",)),
    )(x, norm_g, wg_in, b_in)


def _scan_tile_rows():
    return lax.broadcasted_iota(jnp.int32, (8, CH_S), 0)


def _s5_fwd(layer, proj, bd_re, bd_im, cd_re, cd_im, a_re, a_im, d_skip):
    def body(u_ref, bdre_ref, bdim_ref, cdre_ref, cdim_ref, are_ref, aim_ref, d_ref,
             sre_ref, sim_ref, y_ref):
        for rb in range(SEQ // ROW_BLK):
            rows = pl.ds(rb * ROW_BLK, ROW_BLK)
            ub = u_ref[rows, :].astype(BF16)
            sre_ref[rows, :] = _dot(ub, bdre_ref[...])
            sim_ref[rows, :] = _dot(ub, bdim_ref[...])
        ar = are_ref[...]
        ai = aim_ref[...]
        row_id = _scan_tile_rows()

        def step(i, carry):
            sr, si = carry
            base = pl.multiple_of(i * 8, 8)
            tr = sre_ref[pl.ds(base, 8), :]
            ti = sim_ref[pl.ds(base, 8), :]
            outr, outi = tr, ti
            for r in range(8):
                nsr = ar * sr - ai * si + tr[r:r + 1, :]
                nsi = ar * si + ai * sr + ti[r:r + 1, :]
                outr = jnp.where(row_id == r, nsr, outr)
                outi = jnp.where(row_id == r, nsi, outi)
                sr, si = nsr, nsi
            sre_ref[pl.ds(base, 8), :] = outr
            sim_ref[pl.ds(base, 8), :] = outi
            return sr, si

        zero = jnp.zeros((1, CH_S), F32)
        lax.fori_loop(0, SEQ // 8, step, (zero, zero))
        for rb in range(SEQ // ROW_BLK):
            rows = pl.ds(rb * ROW_BLK, ROW_BLK)
            y = (_dot(sre_ref[rows, :].astype(BF16), cdre_ref[...])
                 - _dot(sim_ref[rows, :].astype(BF16), cdim_ref[...]))
            y_ref[rows, :] = y + d_ref[...] * u_ref[rows, :]

    s = jax.ShapeDtypeStruct
    return pl.pallas_call(
        body, name=f"s5_fwd_l{layer}",
        out_shape=(s((SEQ, N_STATE), F32), s((SEQ, N_STATE), F32), s((SEQ, WIDTH), F32)),
        grid=(N_CHUNK,),
        in_specs=[pl.BlockSpec((SEQ, CH_W), lambda j: (0, j)),
                  pl.BlockSpec((None, CH_W, CH_S), lambda j: (j, 0, 0)),
                  pl.BlockSpec((None, CH_W, CH_S), lambda j: (j, 0, 0)),
                  pl.BlockSpec((None, CH_S, CH_W), lambda j: (j, 0, 0)),
                  pl.BlockSpec((None, CH_S, CH_W), lambda j: (j, 0, 0)),
                  pl.BlockSpec((1, CH_S), lambda j: (0, j)),
                  pl.BlockSpec((1, CH_S), lambda j: (0, j)),
                  pl.BlockSpec((1, CH_W), lambda j: (0, j))],
        out_specs=(pl.BlockSpec((SEQ, CH_S), lambda j: (0, j)),
                   pl.BlockSpec((SEQ, CH_S), lambda j: (0, j)),
                   pl.BlockSpec((SEQ, CH_W), lambda j: (0, j))),
        compiler_params=_cp(("parallel",)),
    )(proj, bd_re, bd_im, cd_re, cd_im, a_re, a_im, d_skip)


def _pool_counts(win):
    t = lax.broadcasted_iota(jnp.int32, (SEQ, POOL_GROUP), 0)
    return t, jnp.minimum(t + 1, win).astype(F32)


def _pool_fwd(layer, proj):
    def body(u_ref, o_ref):
        for gi, win in enumerate(POOL_WINDOWS):
            cols = slice(gi * POOL_GROUP, (gi + 1) * POOL_GROUP)
            u = u_ref[:, cols]
            t, count = _pool_counts(win)
            acc = u
            k = 1
            while k < win:
                acc = acc + jnp.where(t >= k, pltpu.roll(acc, k, 0), 0.0)
                k *= 2
            o_ref[:, cols] = acc / count - u

    return pl.pallas_call(
        body, name=f"pool_fwd_l{layer}",
        out_shape=jax.ShapeDtypeStruct((SEQ, WIDTH), F32),
        grid=(1,),
        in_specs=[pl.BlockSpec((SEQ, WIDTH), lambda i: (0, 2))],
        out_specs=pl.BlockSpec((SEQ, WIDTH), lambda i: (0, 0)),
        compiler_params=_cp(("arbitrary",)),
    )(proj)


def _gelu_parts(y0):
    t = jnp.tanh(GELU_C * (y0 + GELU_A * (y0 * y0 * y0)))
    return t, 0.5 * y0 * (1.0 + t)


def _mix_forward(p_ref, y0_ref, pooled_ref, wglu_ref, bglu_ref, pw_ref, scale_ref, wa_ref, wb_ref):
    za = p_ref[:, WIDTH:2 * WIDTH]
    zb = p_ref[:, 3 * WIDTH:4 * WIDTH]
    ga = p_ref[:, 4 * WIDTH:4 * WIDTH + D_MODEL]
    gb = p_ref[:, 4 * WIDTH + D_MODEL:]
    y0 = y0_ref[...]
    t, y1 = _gelu_parts(y0)
    y1b = y1.astype(BF16)
    q = _dot(y1b, wglu_ref[...].reshape(WIDTH, WIDTH)) + bglu_ref[...]
    sq = _sig(q)
    y2 = y1 * sq
    sza = _sig(za)
    silu_za = za * sza
    ya = y2 * silu_za
    pooled = pooled_ref[...]
    mixed = jnp.concatenate(
        [_dot(pooled[:, g * POOL_GROUP:(g + 1) * POOL_GROUP].astype(BF16), pw_ref[g].astype(BF16))
         for g in range(len(POOL_WINDOWS))], axis=1)
    szb = _sig(zb)
    silu_zb = zb * szb
    ms = mixed * scale_ref[...]
    yb = ms * silu_zb
    yab = ya.astype(BF16)
    ybb = yb.astype(BF16)
    ma = jnp.concatenate([_dot(yab, wa_ref[k]) for k in range(N_DEV)], axis=1)
    mb = jnp.concatenate([_dot(ybb, wb_ref[k]) for k in range(N_DEV)], axis=1)
    sga = _sig(ga)
    sgb = _sig(gb)
    merged = sga * ma + sgb * mb
    return dict(za=za, zb=zb, y0=y0, t=t, y1=y1, y1b=y1b, sq=sq, y2=y2, sza=sza, silu_za=silu_za,
                pooled=pooled, mixed=mixed, szb=szb, silu_zb=silu_zb, ms=ms, yab=yab, ybb=ybb,
                ma=ma, mb=mb, sga=sga, sgb=sgb, merged=merged)


def _mix_weight_specs(layer):
    return [pl.BlockSpec((N_DEV, None, WIDTH // N_DEV, WIDTH), lambda i: (0, layer, 0, 0)),
            pl.BlockSpec((None, 1, WIDTH), lambda i: (layer, 0, 0)),
            pl.BlockSpec((None, 4, POOL_GROUP, POOL_GROUP), lambda i: (layer, 0, 0, 0)),
            pl.BlockSpec((None, 1, WIDTH), lambda i: (layer, 0, 0)),
            pl.BlockSpec((N_DEV, None, WIDTH, D_MODEL // N_DEV), lambda i: (0, layer, 0, 0)),
            pl.BlockSpec((N_DEV, None, WIDTH, D_MODEL // N_DEV), lambda i: (0, layer, 0, 0)),
            pl.BlockSpec((N_DEV, None, D_MODEL // N_DEV, D_MODEL), lambda i: (0, layer, 0, 0))]


def _mix_fwd(layer, x, proj, y0, pooled, wg_glu, b_glu, pool_w, pool_scale, wg_a, wg_b, wg_out):
    def body(x_ref, p_ref, y0_ref, pooled_ref, wglu_ref, bglu_ref, pw_ref, scale_ref, wa_ref, wb_ref,
             wout_ref, o_ref):
        f = _mix_forward(p_ref, y0_ref, pooled_ref, wglu_ref, bglu_ref, pw_ref, scale_ref, wa_ref, wb_ref)
        wout = wout_ref[...].reshape(D_MODEL, D_MODEL)
        o_ref[...] = x_ref[...] + _dot(f["merged"].astype(BF16), wout)

    return pl.pallas_call(
        body, name=f"mix_fwd_l{layer}",
        out_shape=jax.ShapeDtypeStruct((SEQ, D_MODEL), F32),
        grid=(SEQ // TILE_M,),
        in_specs=[pl.BlockSpec((TILE_M, D_MODEL), lambda i: (i, 0)),
                  pl.BlockSpec((TILE_M, N_IN), lambda i: (i, 0)),
                  pl.BlockSpec((TILE_M, WIDTH), lambda i: (i, 0)),
                  pl.BlockSpec((TILE_M, WIDTH), lambda i: (i, 0))] + _mix_weight_specs(layer),
        out_specs=pl.BlockSpec((TILE_M, D_MODEL), lambda i: (i, 0)),
        compiler_params=_cp(("parallel",)),
    )(x, proj, y0, pooled, wg_glu, b_glu, pool_w, pool_scale, wg_a, wg_b, wg_out)


def _loss_head(x, target, final_g):
    def body(x_ref, t_ref, g_ref, dx_ref, loss_ref, gg_ref):
        @pl.when(pl.program_id(0) == 0)
        def _():
            loss_ref[...] = jnp.zeros_like(loss_ref)
            gg_ref[...] = jnp.zeros_like(gg_ref)

        g = g_ref[...]
        rs, xn = _rms(x_ref[...])
        err = xn * g - t_ref[...]
        loss_ref[...] += 0.5 * jnp.sum(jnp.mean(err * err, axis=-1, keepdims=True), axis=0, keepdims=True)
        dy = err * (1.0 / D_MODEL)
        gg_ref[...] += jnp.sum(dy * xn, axis=0, keepdims=True)
        dxn = dy * g
        dx_ref[...] = rs * (dxn - xn * jnp.mean(dxn * xn, axis=-1, keepdims=True))

    s = jax.ShapeDtypeStruct
    return pl.pallas_call(
        body, name="loss_head",
        out_shape=(s((SEQ, D_MODEL), F32), s((1, 1), F32), s((1, D_MODEL), F32)),
        grid=(SEQ // TILE_M,),
        in_specs=[pl.BlockSpec((TILE_M, D_MODEL), lambda i: (i, 0)),
                  pl.BlockSpec((TILE_M, D_MODEL), lambda i: (i, 0)),
                  _const((1, D_MODEL))],
        out_specs=(pl.BlockSpec((TILE_M, D_MODEL), lambda i: (i, 0)), _const((1, 1)), _const((1, D_MODEL))),
        compiler_params=_cp(("arbitrary",)),
    )(x, target, final_g)


def _mix_bwd(layer, dx_next, proj, y0, pooled, wg_glu, b_glu, pool_w, pool_scale, wg_a, wg_b, wg_out):
    n_k = N_DEV

    def body(dx_ref, p_ref, y0_ref, pooled_ref, wglu_ref, bglu_ref, pw_ref, scale_ref, wa_ref, wb_ref,
             wout_ref, dproj_ref, dy0_ref, dpooled_ref, gwout_ref, gwa_ref, gwb_ref, gwglu_ref, gpw_ref,
             gscale_ref, gbglu_ref):
        @pl.when(pl.program_id(0) == 0)
        def _():
            for r in (gwout_ref, gwa_ref, gwb_ref, gwglu_ref, gpw_ref, gscale_ref, gbglu_ref):
                r[...] = jnp.zeros_like(r)

        f = _mix_forward(p_ref, y0_ref, pooled_ref, wglu_ref, bglu_ref, pw_ref, scale_ref, wa_ref, wb_ref)
        wglu = wglu_ref[...].reshape(WIDTH, WIDTH)
        wout = wout_ref[...].reshape(D_MODEL, D_MODEL)
        blk = D_MODEL // n_k
        dxb = dx_ref[...].astype(BF16)
        dmerged = _dot_nt(dxb, wout)
        gwout = _dot_tn(f["merged"].astype(BF16), dxb)
        for k in range(n_k):
            gwout_ref[_slot(k)] += gwout[k * blk:(k + 1) * blk, :]
        dma = dmerged * f["sga"]
        dmb = dmerged * f["sgb"]
        dga = dmerged * f["ma"] * f["sga"] * (1.0 - f["sga"])
        dgb = dmerged * f["mb"] * f["sgb"] * (1.0 - f["sgb"])
        dmab = dma.astype(BF16)
        dmbb = dmb.astype(BF16)
        dya = jnp.zeros((TILE_M, WIDTH), F32)
        dyb = jnp.zeros((TILE_M, WIDTH), F32)
        for k in range(n_k):
            da_k = dmab[:, k * blk:(k + 1) * blk]
            db_k = dmbb[:, k * blk:(k + 1) * blk]
            dya = dya + _dot_nt(da_k, wa_ref[k])
            dyb = dyb + _dot_nt(db_k, wb_ref[k])
            gwa_ref[_slot(k)] += _dot_tn(f["yab"], da_k)
            gwb_ref[_slot(k)] += _dot_tn(f["ybb"], db_k)
        zb, szb = f["zb"], f["szb"]
        dzb = dyb * f["ms"] * (szb * (1.0 + zb * (1.0 - szb)))
        dms = dyb * f["silu_zb"]
        gscale_ref[...] += jnp.sum(dms * f["mixed"], axis=0, keepdims=True)
        dmixed = (dms * scale_ref[...]).astype(BF16)
        pooled = f["pooled"]
        for g in range(len(POOL_WINDOWS)):
            cols = slice(g * POOL_GROUP, (g + 1) * POOL_GROUP)
            dpooled_ref[:, cols] = _dot_nt(dmixed[:, cols], pw_ref[g].astype(BF16))
            gpw_ref[g] += _dot_tn(pooled[:, cols].astype(BF16), dmixed[:, cols])
        za, sza = f["za"], f["sza"]
        dza = dya * f["y2"] * (sza * (1.0 + za * (1.0 - sza)))
        dy2 = dya * f["silu_za"]
        sq = f["sq"]
        dq = dy2 * f["y1"] * sq * (1.0 - sq)
        dqb = dq.astype(BF16)
        dy1 = dy2 * sq + _dot_nt(dqb, wglu)
        gwglu = _dot_tn(f["y1b"], dqb)
        rblk = WIDTH // n_k
        for k in range(n_k):
            gwglu_ref[_slot(k)] += gwglu[k * rblk:(k + 1) * rblk, :]
        gbglu_ref[...] += jnp.sum(dq, axis=0, keepdims=True)
        y0, t = f["y0"], f["t"]
        dgelu = 0.5 * (1.0 + t) + 0.5 * y0 * (1.0 - t * t) * (GELU_C * (1.0 + 3.0 * GELU_A * y0 * y0))
        dy0_ref[...] = dy1 * dgelu
        zeros = jnp.zeros((TILE_M, WIDTH), BF16)
        dproj_ref[:, 0:WIDTH] = zeros
        dproj_ref[:, WIDTH:2 * WIDTH] = dza.astype(BF16)
        dproj_ref[:, 2 * WIDTH:3 * WIDTH] = zeros
        dproj_ref[:, 3 * WIDTH:4 * WIDTH] = dzb.astype(BF16)
        dproj_ref[:, 4 * WIDTH:4 * WIDTH + D_MODEL] = dga.astype(BF16)
        dproj_ref[:, 4 * WIDTH + D_MODEL:] = dgb.astype(BF16)

    s = jax.ShapeDtypeStruct
    tile = lambda w: pl.BlockSpec((TILE_M, w), lambda i: (i, 0))
    return pl.pallas_call(
        body, name=f"mix_bwd_l{layer}",
        out_shape=(s((SEQ, N_IN), BF16), s((SEQ, WIDTH), F32), s((SEQ, WIDTH), F32),
                   s((N_DEV, D_MODEL // N_DEV, D_MODEL), F32),
                   s((N_DEV, WIDTH, D_MODEL // N_DEV), F32),
                   s((N_DEV, WIDTH, D_MODEL // N_DEV), F32),
                   s((N_DEV, WIDTH // N_DEV, WIDTH), F32),
                   s((4, POOL_GROUP, POOL_GROUP), F32), s((1, WIDTH), F32), s((1, WIDTH), F32)),
        grid=(SEQ // TILE_M,),
        in_specs=[tile(D_MODEL), tile(N_IN), tile(WIDTH), tile(WIDTH)] + _mix_weight_specs(layer),
        out_specs=(tile(N_IN), tile(WIDTH), tile(WIDTH),
                   _const((N_DEV, D_MODEL // N_DEV, D_MODEL)), _const((N_DEV, WIDTH, D_MODEL // N_DEV)),
                   _const((N_DEV, WIDTH, D_MODEL // N_DEV)), _const((N_DEV, WIDTH // N_DEV, WIDTH)),
                   _const((4, POOL_GROUP, POOL_GROUP)), _const((1, WIDTH)), _const((1, WIDTH))),
        compiler_params=_cp(("arbitrary",), VMEM_LIMIT_BIG),
    )(dx_next, proj, y0, pooled, wg_glu, b_glu, pool_w, pool_scale, wg_a, wg_b, wg_out)


def _pool_bwd(layer, dpooled, dproj):
    def body(dp_ref, _, o_ref):
        for gi, win in enumerate(POOL_WINDOWS):
            cols = slice(gi * POOL_GROUP, (gi + 1) * POOL_GROUP)
            dp = dp_ref[:, cols]
            t, count = _pool_counts(win)
            e = dp / count
            acc = e
            k = 1
            while k < win:
                acc = acc + jnp.where(t < SEQ - k, pltpu.roll(acc, SEQ - k, 0), 0.0)
                k *= 2
            o_ref[:, cols] = (acc - dp).astype(BF16)

    return pl.pallas_call(
        body, name=f"pool_bwd_l{layer}",
        out_shape=jax.ShapeDtypeStruct((SEQ, N_IN), BF16),
        grid=(1,),
        in_specs=[pl.BlockSpec((SEQ, WIDTH), lambda i: (0, 0)), ANY],
        out_specs=pl.BlockSpec((SEQ, WIDTH), lambda i: (0, 2)),
        input_output_aliases={1: 0},
        compiler_params=_cp(("arbitrary",)),
    )(dpooled, dproj)


def _s5_bwd(layer, dy0, proj, s_re, s_im, bd_re, bd_im, cd_re, cd_im, a_re, a_im, d_skip, dproj):
    def body(dy_ref, u_ref, sre_ref, sim_ref, bdre_ref, bdim_ref, cdre_ref, cdim_ref, are_ref, aim_ref,
             d_ref, _, du_ref, gbre_ref, gbim_ref, gcre_ref, gcim_ref, gare_ref, gaim_ref, gd_ref,
             lre_ref, lim_ref):
        n_rb = SEQ // ROW_BLK
        gcre = jnp.zeros((CH_S, CH_W), F32)
        gcim = jnp.zeros((CH_S, CH_W), F32)
        for rb in range(n_rb):
            rows = pl.ds(rb * ROW_BLK, ROW_BLK)
            dyb = dy_ref[rows, :].astype(BF16)
            lre_ref[rows, :] = _dot_nt(dyb, cdre_ref[...])
            lim_ref[rows, :] = -_dot_nt(dyb, cdim_ref[...])
            gcre = gcre + _dot_tn(sre_ref[rows, :].astype(BF16), dyb)
            gcim = gcim - _dot_tn(sim_ref[rows, :].astype(BF16), dyb)
        gcre_ref[...] = gcre
        gcim_ref[...] = gcim
        ar = are_ref[...]
        ai = aim_ref[...]
        row_id = _scan_tile_rows()

        def step(n, carry):
            lr, li = carry
            base = pl.multiple_of((SEQ // 8 - 1 - n) * 8, 8)
            tr = lre_ref[pl.ds(base, 8), :]
            ti = lim_ref[pl.ds(base, 8), :]
            outr, outi = tr, ti
            for r in range(7, -1, -1):
                nlr = ar * lr + ai * li + tr[r:r + 1, :]
                nli = ar * li - ai * lr + ti[r:r + 1, :]
                outr = jnp.where(row_id == r, nlr, outr)
                outi = jnp.where(row_id == r, nli, outi)
                lr, li = nlr, nli
            lre_ref[pl.ds(base, 8), :] = outr
            lim_ref[pl.ds(base, 8), :] = outi
            return lr, li

        zero = jnp.zeros((1, CH_S), F32)
        lax.fori_loop(0, SEQ // 8, step, (zero, zero))

        gare = jnp.zeros((1, CH_S), F32)
        gaim = jnp.zeros((1, CH_S), F32)
        gbre = jnp.zeros((CH_W, CH_S), F32)
        gbim = jnp.zeros((CH_W, CH_S), F32)
        gd = jnp.zeros((1, CH_W), F32)
        first = lax.broadcasted_iota(jnp.int32, (ROW_BLK, CH_S), 0) == 0
        for rb in range(n_rb):
            rows = pl.ds(rb * ROW_BLK, ROW_BLK)
            lr = lre_ref[rows, :]
            li = lim_ref[rows, :]
            if rb == 0:
                prev_r = jnp.zeros((1, CH_S), F32)
                prev_i = jnp.zeros((1, CH_S), F32)
            else:
                prev_r = sre_ref[pl.ds(rb * ROW_BLK - 1, 1), :]
                prev_i = sim_ref[pl.ds(rb * ROW_BLK - 1, 1), :]
            spr = jnp.where(first, prev_r, pltpu.roll(sre_ref[rows, :], 1, 0))
            spi = jnp.where(first, prev_i, pltpu.roll(sim_ref[rows, :], 1, 0))
            gare = gare + jnp.sum(lr * spr + li * spi, axis=0, keepdims=True)
            gaim = gaim + jnp.sum(li * spr - lr * spi, axis=0, keepdims=True)
            lrb = lr.astype(BF16)
            lib = li.astype(BF16)
            u = u_ref[rows, :]
            ub = u.astype(BF16)
            dy = dy_ref[rows, :]
            du = dy * d_ref[...] + _dot_nt(lrb, bdre_ref[...]) + _dot_nt(lib, bdim_ref[...])
            du_ref[rows, :] = du.astype(BF16)
            gbre = gbre + _dot_tn(ub, lrb)
            gbim = gbim + _dot_tn(ub, lib)
            gd = gd + jnp.sum(dy * u, axis=0, keepdims=True)
        gare_ref[...] = gare
        gaim_ref[...] = gaim
        gbre_ref[...] = gbre
        gbim_ref[...] = gbim
        gd_ref[...] = gd

    s = jax.ShapeDtypeStruct
    chunk_w = lambda: pl.BlockSpec((SEQ, CH_W), lambda j: (0, j))
    chunk_s = lambda: pl.BlockSpec((SEQ, CH_S), lambda j: (0, j))
    bspec = lambda: pl.BlockSpec((None, CH_W, CH_S), lambda j: (j, 0, 0))
    cspec = lambda: pl.BlockSpec((None, CH_S, CH_W), lambda j: (j, 0, 0))
    return pl.pallas_call(
        body, name=f"s5_bwd_l{layer}",
        out_shape=(s((SEQ, N_IN), BF16),
                   s((N_CHUNK, CH_W, CH_S), F32), s((N_CHUNK, CH_W, CH_S), F32),
                   s((N_CHUNK, CH_S, CH_W), F32), s((N_CHUNK, CH_S, CH_W), F32),
                   s((1, N_STATE), F32), s((1, N_STATE), F32), s((1, WIDTH), F32)),
        grid=(N_CHUNK,),
        in_specs=[chunk_w(), chunk_w(), chunk_s(), chunk_s(), bspec(), bspec(), cspec(), cspec(),
                  pl.BlockSpec((1, CH_S), lambda j: (0, j)), pl.BlockSpec((1, CH_S), lambda j: (0, j)),
                  pl.BlockSpec((1, CH_W), lambda j: (0, j)), ANY],
        out_specs=(chunk_w(), bspec(), bspec(), cspec(), cspec(),
                   pl.BlockSpec((1, CH_S), lambda j: (0, j)), pl.BlockSpec((1, CH_S), lambda j: (0, j)),
                   pl.BlockSpec((1, CH_W), lambda j: (0, j))),
        scratch_shapes=[pltpu.VMEM((SEQ, CH_S), F32), pltpu.VMEM((SEQ, CH_S), F32)],
        input_output_aliases={11: 0},
        compiler_params=_cp(("arbitrary",), VMEM_LIMIT_BIG),
    )(dy0, proj, s_re, s_im, bd_re, bd_im, cd_re, cd_im, a_re, a_im, d_skip, dproj)


def _proj_wgrad(layer, x, norm_g, dproj):
    tm = 512

    def body(x_ref, g_ref, dp_ref, gw_ref, gb_ref):
        @pl.when(pl.program_id(1) == 0)
        def _():
            gw_ref[...] = jnp.zeros_like(gw_ref)
            gb_ref[...] = jnp.zeros_like(gb_ref)

        _, xn = _rms(x_ref[...])
        h = (xn * g_ref[...]).astype(BF16)
        dp = dp_ref[...]
        gw_ref[...] += _dot_tn(h, dp)
        gb_ref[...] += jnp.sum(dp.astype(F32), axis=0, keepdims=True)

    s = jax.ShapeDtypeStruct
    return pl.pallas_call(
        body, name=f"proj_wgrad_l{layer}",
        out_shape=(s((N_DEV, D_MODEL, WIDTH), F32), s((1, N_IN), F32)),
        grid=(N_DEV, SEQ // tm),
        in_specs=[pl.BlockSpec((tm, D_MODEL), lambda n, t: (t, 0)),
                  pl.BlockSpec((None, 1, D_MODEL), lambda n, t: (layer, 0, 0)),
                  pl.BlockSpec((tm, WIDTH), lambda n, t: (t, n))],
        out_specs=(pl.BlockSpec((None, D_MODEL, WIDTH), lambda n, t: (_slot(n), 0, 0)),
                   pl.BlockSpec((1, WIDTH), lambda n, t: (0, n))),
        compiler_params=_cp(("parallel", "arbitrary")),
    )(x, norm_g, dproj)


def _proj_dgrad(layer, dx_next, x, norm_g, dproj, wg_in):
    def body(dxn_ref, x_ref, g_ref, dp_ref, w_ref, dx_ref, gg_ref):
        @pl.when(pl.program_id(0) == 0)
        def _():
            gg_ref[...] = jnp.zeros_like(gg_ref)

        dh = jnp.zeros((TILE_M, D_MODEL), F32)
        for k in range(N_DEV):
            dh = dh + _dot_nt(dp_ref[:, k * WIDTH:(k + 1) * WIDTH], w_ref[k])
        rs, xn = _rms(x_ref[...])
        gg_ref[...] += jnp.sum(dh * xn, axis=0, keepdims=True)
        dxn = dh * g_ref[...]
        dx_ref[...] = dxn_ref[...] + rs * (dxn - xn * jnp.mean(dxn * xn, axis=-1, keepdims=True))

    s = jax.ShapeDtypeStruct
    return pl.pallas_call(
        body, name=f"proj_dgrad_l{layer}",
        out_shape=(s((SEQ, D_MODEL), F32), s((1, D_MODEL), F32)),
        grid=(SEQ // TILE_M,),
        in_specs=[pl.BlockSpec((TILE_M, D_MODEL), lambda i: (i, 0)),
                  pl.BlockSpec((TILE_M, D_MODEL), lambda i: (i, 0)),
                  pl.BlockSpec((None, 1, D_MODEL), lambda i: (layer, 0, 0)),
                  pl.BlockSpec((TILE_M, N_IN), lambda i: (i, 0)),
                  pl.BlockSpec((N_DEV, None, D_MODEL, WIDTH), lambda i: (0, layer, 0, 0))],
        out_specs=(pl.BlockSpec((TILE_M, D_MODEL), lambda i: (i, 0)), _const((1, D_MODEL))),
        compiler_params=_cp(("arbitrary",)),
    )(dx_next, x, norm_g, dproj, wg_in)


def _my_place():
    return lax.axis_index("x"), lax.axis_index("y"), lax.axis_index("c")


def _gather_weights(shards):
    n = len(shards)

    def body(*refs):
        src = refs[:n]
        out = refs[n:2 * n]
        send_sems, recv_sems, local_sems = refs[2 * n:]
        x, y, c = _my_place()
        me, sibling = (x, y, c), (x, y, 1 - c)
        chips = [(1 - x, y), (x, 1 - y), (1 - x, 1 - y)]

        def rows(t, place):
            px, py, pc = place
            return out[t].at[pl.ds(4 * px + 2 * py + pc, 1)]

        def copy(t, k, block, to, from_src=False):
            return pltpu.make_async_remote_copy(
                src_ref=src[t] if from_src else rows(t, block), dst_ref=rows(t, block),
                send_sem=send_sems.at[7 * t + k], recv_sem=recv_sems.at[7 * t + k], device_id=to,
                device_id_type=MESH)

        mine = [pltpu.make_async_copy(src[t], rows(t, me), local_sems.at[t]) for t in range(n)]
        for cp in mine:
            cp.start()
        first = []
        for t in range(n):
            first.append(copy(t, 0, me, sibling, from_src=True))
            first += [copy(t, 1 + j, me, (*chip, c), from_src=True) for j, chip in enumerate(chips)]
        for cp in first:
            cp.start()
        passed = []
        for t in range(n):
            for j, chip in enumerate(chips):
                copy(t, 1 + j, (*chip, c), me).wait_recv()
                fwd = copy(t, 4 + j, (*chip, c), sibling)
                fwd.start()
                passed.append(fwd)
        for t in range(n):
            copy(t, 0, sibling, me).wait_recv()
            for j, chip in enumerate(chips):
                copy(t, 4 + j, (*chip, 1 - c), me).wait_recv()
        for cp in first + passed:
            cp.wait_send()
        for cp in mine:
            cp.wait()

    return pl.pallas_call(
        body, name="gather_weights",
        out_shape=tuple(jax.ShapeDtypeStruct((N_DEV,) + a.shape[1:], a.dtype) for a in shards),
        in_specs=[ANY] * n, out_specs=tuple([ANY] * n),
        scratch_shapes=[pltpu.SemaphoreType.DMA((7 * n,)), pltpu.SemaphoreType.DMA((7 * n,)),
                        pltpu.SemaphoreType.DMA((n,))],
    )(*shards)


def _exchange_sibling(big, small):
    n = len(big)

    def body(*refs):
        g = refs[:n]
        sg = refs[n]
        got = refs[n + 1:2 * n + 1]
        own = refs[2 * n + 1:3 * n + 1]
        got_small = refs[3 * n + 1]
        send_sems, recv_sems, local_sems = refs[3 * n + 2:]
        x, y, c = _my_place()
        sibling = (x, y, 1 - c)
        remote, local = [], []
        for t in range(n):
            remote.append(pltpu.make_async_remote_copy(
                src_ref=g[t].at[:, pl.ds(4 * (1 - c), 4)], dst_ref=got[t],
                send_sem=send_sems.at[t], recv_sem=recv_sems.at[t], device_id=sibling, device_id_type=MESH))
            local.append(pltpu.make_async_copy(g[t].at[:, pl.ds(4 * c, 4)], own[t], local_sems.at[t]))
        remote.append(pltpu.make_async_remote_copy(
            src_ref=sg, dst_ref=got_small, send_sem=send_sems.at[n], recv_sem=recv_sems.at[n],
            device_id=sibling, device_id_type=MESH))
        for cp in remote + local:
            cp.start()
        for cp in remote + local:
            cp.wait()

    half = lambda a: jax.ShapeDtypeStruct((a.shape[0], 4) + a.shape[2:], a.dtype)
    out_shape = tuple([half(a) for a in big] + [half(a) for a in big]
                      + [jax.ShapeDtypeStruct(small.shape, small.dtype)])
    res = pl.pallas_call(
        body, name="exchange_sibling", out_shape=out_shape,
        in_specs=[ANY] * (n + 1), out_specs=tuple([ANY] * (2 * n + 1)),
        scratch_shapes=[pltpu.SemaphoreType.DMA((n + 1,)), pltpu.SemaphoreType.DMA((n + 1,)),
                        pltpu.SemaphoreType.DMA((n,))],
    )(*big, small)
    return res[:n], res[n:2 * n], res[2 * n]


def _exchange_chips(big, small):
    n = len(big)

    def body(*refs):
        cp_in = refs[:n]
        sm_in = refs[n]
        slots = refs[n + 1:2 * n + 1]
        sm_slots = refs[2 * n + 1]
        send_sems, recv_sems, local_sems = refs[2 * n + 2:]
        x, y, c = _my_place()
        my_chip = 2 * x + y
        chips = [(1 - x, y), (x, 1 - y), (1 - x, 1 - y)]
        remote, local = [], []
        for t in range(n):
            for j, (px, py) in enumerate(chips):
                remote.append(pltpu.make_async_remote_copy(
                    src_ref=cp_in[t].at[:, pl.ds(2 * px + py, 1)], dst_ref=slots[t].at[:, pl.ds(my_chip, 1)],
                    send_sem=send_sems.at[3 * t + j], recv_sem=recv_sems.at[3 * t + j],
                    device_id=(px, py, c), device_id_type=MESH))
            local.append(pltpu.make_async_copy(
                cp_in[t].at[:, pl.ds(my_chip, 1)], slots[t].at[:, pl.ds(my_chip, 1)], local_sems.at[t]))
        for j, (px, py) in enumerate(chips):
            remote.append(pltpu.make_async_remote_copy(
                src_ref=sm_in, dst_ref=sm_slots.at[pl.ds(my_chip, 1)],
                send_sem=send_sems.at[3 * n + j], recv_sem=recv_sems.at[3 * n + j],
                device_id=(px, py, c), device_id_type=MESH))
        local.append(pltpu.make_async_copy(sm_in, sm_slots.at[pl.ds(my_chip, 1)], local_sems.at[n]))
        for cp in remote + local:
            cp.start()
        for t in range(n):
            for j, (px, py) in enumerate(chips):
                pltpu.make_async_remote_copy(
                    src_ref=cp_in[t].at[:, pl.ds(2 * px + py, 1)], dst_ref=slots[t].at[:, pl.ds(2 * px + py, 1)],
                    send_sem=send_sems.at[3 * t + j], recv_sem=recv_sems.at[3 * t + j],
                    device_id=(px, py, c), device_id_type=MESH).wait()
        for j, (px, py) in enumerate(chips):
            pltpu.make_async_remote_copy(
                src_ref=sm_in, dst_ref=sm_slots.at[pl.ds(2 * px + py, 1)],
                send_sem=send_sems.at[3 * n + j], recv_sem=recv_sems.at[3 * n + j],
                device_id=(px, py, c), device_id_type=MESH).wait()
        for cp in local:
            cp.wait()

    out_shape = tuple([jax.ShapeDtypeStruct(a.shape, a.dtype) for a in big]
                      + [jax.ShapeDtypeStruct((N_CHIP,) + small.shape[1:], small.dtype)])
    res = pl.pallas_call(
        body, name="exchange_chips", out_shape=out_shape,
        in_specs=[ANY] * (n + 1), out_specs=tuple([ANY] * (n + 1)),
        scratch_shapes=[pltpu.SemaphoreType.DMA((3 * n + 3,)), pltpu.SemaphoreType.DMA((3 * n + 3,)),
                        pltpu.SemaphoreType.DMA((n + 1,))],
    )(*big, small)
    return res[:n], res[n]


def _row_block(rows):
    return rows if rows <= 256 else 256


def _add_pairs(tag, a, b):
    shape = a.shape
    r, c = shape[-2], shape[-1]
    lead = int(math.prod(shape[:-2]))
    rb = _row_block(r)

    def body(a_ref, b_ref, o_ref):
        o_ref[...] = a_ref[...] + b_ref[...]

    spec = pl.BlockSpec((None, rb, c), lambda i, j: (i, j, 0))
    out = pl.pallas_call(
        body, name=f"add_{tag}", out_shape=jax.ShapeDtypeStruct((lead, r, c), a.dtype),
        grid=(lead, r // rb), in_specs=[spec, spec], out_specs=spec,
        compiler_params=_cp(("parallel", "parallel")),
    )(a.reshape(lead, r, c), b.reshape(lead, r, c))
    return out.reshape(shape)


def _adamw_math(w, g, m, v):
    m = ADAM_B1 * m + (1.0 - ADAM_B1) * g
    v = ADAM_B2 * v + (1.0 - ADAM_B2) * (g * g)
    m_hat = m / (1.0 - ADAM_B1 ** ADAM_STEP)
    v_hat = v / (1.0 - ADAM_B2 ** ADAM_STEP)
    delta = -ADAM_LR * (m_hat / (jnp.sqrt(v_hat) + ADAM_EPS) + ADAM_WD * w)
    return delta, m, v


def _sum_slots_adamw(tag, slots, w, m, v):
    b, _, r, c = slots.shape
    rb = _row_block(r)

    def body(s_ref, w_ref, m_ref, v_ref, g_ref, d_ref, nm_ref, nv_ref):
        g = (s_ref[0] + s_ref[1]) + (s_ref[2] + s_ref[3])
        delta, nm, nv = _adamw_math(w_ref[...], g, m_ref[...], v_ref[...])
        g_ref[...] = g
        d_ref[...] = delta
        nm_ref[...] = nm
        nv_ref[...] = nv

    spec = pl.BlockSpec((None, rb, c), lambda i, j: (i, j, 0))
    sspec = pl.BlockSpec((None, N_CHIP, rb, c), lambda i, j: (i, 0, j, 0))
    s = jax.ShapeDtypeStruct((b, r, c), F32)
    return pl.pallas_call(
        body, name=f"adamw_{tag}", out_shape=(s, s, s, s),
        grid=(b, r // rb), in_specs=[sspec, spec, spec, spec], out_specs=(spec, spec, spec, spec),
        compiler_params=_cp(("parallel", "parallel")),
    )(slots, w, m, v)


SMALL = (("norm_g", (2, 1024)), ("b_in", (2, 4096)), ("ssm_log_dt", (2, 32)), ("ssm_lam_re", (2, 32, 64)),
         ("ssm_lam_im", (2, 32, 64)), ("ssm_b_re", (2, 32, 64, 16)), ("ssm_b_im", (2, 32, 64, 16)),
         ("ssm_c_re", (2, 32, 16, 64)), ("ssm_c_im", (2, 32, 16, 64)), ("ssm_d", (2, 512)),
         ("ssm_b_glu", (2, 512)), ("pool_w", (2, 4, 128, 128)), ("pool_scale", (2, 512)),
         ("final_norm_g", (1024,)))
PACK_ALIGN = 1024
PACK_ROWS = 3328


def _padded(n):
    return -(-n // PACK_ALIGN) * PACK_ALIGN


def _pack_small(vals):
    parts = []
    for name, shape in SMALL:
        flat = vals[name].reshape(-1)
        parts.append(jnp.pad(flat, (0, _padded(flat.shape[0]) - flat.shape[0])))
    flat = jnp.concatenate(parts)
    flat = jnp.pad(flat, (0, PACK_ROWS * 128 - flat.shape[0]))
    return flat.reshape(PACK_ROWS, 128)


def _unpack_small(packed):
    flat = packed.reshape(-1)
    out, off = {}, 0
    for name, shape in SMALL:
        n = int(math.prod(shape))
        out[name] = flat[off:off + n].reshape(shape)
        off += _padded(n)
    return out


def _block_diag_b(bbar_t):
    bb = bbar_t.reshape(16, N_CHUNK, 8, 64)
    eye = jnp.eye(8, dtype=bbar_t.dtype)
    return jnp.einsum("cjgp,gh->jgchp", bb, eye).reshape(N_CHUNK, CH_W, CH_S)


def _block_diag_c(c):
    cc = c.reshape(N_CHUNK, 8, 16, 64)
    eye = jnp.eye(8, dtype=c.dtype)
    return jnp.einsum("jhcp,gh->jgphc", cc, eye).reshape(N_CHUNK, CH_S, CH_W)


def _diag_b_grad(g):
    d = jnp.einsum("jgcgp->jgcp", g.reshape(N_CHUNK, 8, 16, 8, 64))
    return d.transpose(2, 0, 1, 3).reshape(16, 32, 64)


def _diag_c_grad(g):
    d = jnp.einsum("jgpgc->jgpc", g.reshape(N_CHUNK, 8, 64, 8, 16))
    return d.reshape(32, 64, 16).transpose(0, 2, 1)


def kernel(x, norm_g, w_in, b_in, ssm_log_dt, ssm_lam_re, ssm_lam_im, ssm_b_re, ssm_b_im, ssm_c_re, ssm_c_im, ssm_d, ssm_w_glu, ssm_b_glu, pool_w, pool_scale, w_branch_a, w_branch_b, w_out, final_norm_g, loss_target, m_norm_g, m_w_in, m_b_in, m_ssm_log_dt, m_ssm_lam_re, m_ssm_lam_im, m_ssm_b_re, m_ssm_b_im, m_ssm_c_re, m_ssm_c_im, m_ssm_d, m_ssm_w_glu, m_ssm_b_glu, m_pool_w, m_pool_scale, m_w_branch_a, m_w_branch_b, m_w_out, m_final_norm_g, v_norm_g, v_w_in, v_b_in, v_ssm_log_dt, v_ssm_lam_re, v_ssm_lam_im, v_ssm_b_re, v_ssm_b_im, v_ssm_c_re, v_ssm_c_im, v_ssm_d, v_ssm_w_glu, v_ssm_b_glu, v_pool_w, v_pool_scale, v_w_branch_a, v_w_branch_b, v_w_out, v_final_norm_g):
    weights = dict(norm_g=norm_g, w_in=w_in, b_in=b_in, ssm_log_dt=ssm_log_dt, ssm_lam_re=ssm_lam_re,
                   ssm_lam_im=ssm_lam_im, ssm_b_re=ssm_b_re, ssm_b_im=ssm_b_im, ssm_c_re=ssm_c_re,
                   ssm_c_im=ssm_c_im, ssm_d=ssm_d, ssm_w_glu=ssm_w_glu, ssm_b_glu=ssm_b_glu, pool_w=pool_w,
                   pool_scale=pool_scale, w_branch_a=w_branch_a, w_branch_b=w_branch_b, w_out=w_out,
                   final_norm_g=final_norm_g)
    mom_m = dict(norm_g=m_norm_g, w_in=m_w_in, b_in=m_b_in, ssm_log_dt=m_ssm_log_dt, ssm_lam_re=m_ssm_lam_re,
                 ssm_lam_im=m_ssm_lam_im, ssm_b_re=m_ssm_b_re, ssm_b_im=m_ssm_b_im, ssm_c_re=m_ssm_c_re,
                 ssm_c_im=m_ssm_c_im, ssm_d=m_ssm_d, ssm_w_glu=m_ssm_w_glu, ssm_b_glu=m_ssm_b_glu,
                 pool_w=m_pool_w, pool_scale=m_pool_scale, w_branch_a=m_w_branch_a, w_branch_b=m_w_branch_b,
                 w_out=m_w_out, final_norm_g=m_final_norm_g)
    mom_v = dict(norm_g=v_norm_g, w_in=v_w_in, b_in=v_b_in, ssm_log_dt=v_ssm_log_dt, ssm_lam_re=v_ssm_lam_re,
                 ssm_lam_im=v_ssm_lam_im, ssm_b_re=v_ssm_b_re, ssm_b_im=v_ssm_b_im, ssm_c_re=v_ssm_c_re,
                 ssm_c_im=v_ssm_c_im, ssm_d=v_ssm_d, ssm_w_glu=v_ssm_w_glu, ssm_b_glu=v_ssm_b_glu,
                 pool_w=v_pool_w, pool_scale=v_pool_scale, w_branch_a=v_w_branch_a, w_branch_b=v_w_branch_b,
                 w_out=v_w_out, final_norm_g=v_final_norm_g)
    order = ["norm_g", "w_in", "b_in", "ssm_log_dt", "ssm_lam_re", "ssm_lam_im", "ssm_b_re", "ssm_b_im",
             "ssm_c_re", "ssm_c_im", "ssm_d", "ssm_w_glu", "ssm_b_glu", "pool_w", "pool_scale", "w_branch_a",
             "w_branch_b", "w_out", "final_norm_g"]
    big_names = ["w_in", "ssm_w_glu", "w_branch_a", "w_branch_b", "w_out"]

    wg_in, wg_glu, wg_a, wg_b, wg_out = _gather_weights([weights[n].astype(BF16)[None] for n in big_names])

    log_dt3 = ssm_log_dt.reshape(DEPTH, 32, 1)
    bt_re = ssm_b_re.transpose(0, 3, 1, 2)
    bt_im = ssm_b_im.transpose(0, 3, 1, 2)
    abar_re, abar_im, bbar_re, bbar_im = _s5_params(log_dt3, ssm_lam_re, ssm_lam_im, bt_re, bt_im)
    s5 = []
    for l in range(DEPTH):
        s5.append(dict(
            a_re=abar_re[l].reshape(1, N_STATE), a_im=abar_im[l].reshape(1, N_STATE),
            bd_re=_block_diag_b(bbar_re[l]).astype(BF16), bd_im=_block_diag_b(bbar_im[l]).astype(BF16),
            cd_re=_block_diag_c(ssm_c_re[l]).astype(BF16), cd_im=_block_diag_c(ssm_c_im[l]).astype(BF16)))

    norm_g3 = norm_g.reshape(DEPTH, 1, D_MODEL)
    b_in3 = b_in.reshape(DEPTH, 1, N_IN)
    b_glu3 = ssm_b_glu.reshape(DEPTH, 1, WIDTH)
    scale3 = pool_scale.reshape(DEPTH, 1, WIDTH)
    d_skip = ssm_d.reshape(DEPTH, 1, WIDTH)

    xs = [x.reshape(SEQ, D_MODEL)]
    saved = []
    for l in range(DEPTH):
        p = s5[l]
        proj = _norm_proj(l, xs[l], norm_g3, wg_in, b_in3)
        s_re, s_im, y0 = _s5_fwd(l, proj, p["bd_re"], p["bd_im"], p["cd_re"], p["cd_im"], p["a_re"], p["a_im"],
                                 d_skip[l])
        pooled = _pool_fwd(l, proj)
        xs.append(_mix_fwd(l, xs[l], proj, y0, pooled, wg_glu, b_glu3, pool_w, scale3, wg_a, wg_b, wg_out))
        saved.append((proj, s_re, s_im, y0, pooled))

    dx, loss_part, g_final = _loss_head(xs[DEPTH], loss_target.reshape(SEQ, D_MODEL),
                                        final_norm_g.reshape(1, D_MODEL))
    loss = lax.psum(loss_part[0, 0], ("x", "y", "c"))

    big_g = {n: [None] * DEPTH for n in big_names}
    sm = {n: [None] * DEPTH for n, _ in SMALL}
    g_abar_re, g_abar_im, g_bbar_re, g_bbar_im = ([None] * DEPTH for _ in range(4))
    for l in reversed(range(DEPTH)):
        p = s5[l]
        proj, s_re, s_im, y0, pooled = saved[l]
        (dproj, dy0, dpooled, gw_out, gw_a, gw_b, gw_glu, g_pw, g_scale, g_bglu) = _mix_bwd(
            l, dx, proj, y0, pooled, wg_glu, b_glu3, pool_w, scale3, wg_a, wg_b, wg_out)
        dproj = _pool_bwd(l, dpooled, dproj)
        (dproj, g_bd_re, g_bd_im, g_cd_re, g_cd_im, g_a_re, g_a_im, g_d) = _s5_bwd(
            l, dy0, proj, s_re, s_im, p["bd_re"], p["bd_im"], p["cd_re"], p["cd_im"], p["a_re"], p["a_im"],
            d_skip[l], dproj)
        gw_in, g_bin = _proj_wgrad(l, xs[l], norm_g3, dproj)
        dx, g_norm = _proj_dgrad(l, dx, xs[l], norm_g3, dproj, wg_in)
        big_g["w_in"][l] = gw_in
        big_g["ssm_w_glu"][l] = gw_glu
        big_g["w_branch_a"][l] = gw_a
        big_g["w_branch_b"][l] = gw_b
        big_g["w_out"][l] = gw_out
        sm["norm_g"][l] = g_norm.reshape(D_MODEL)
        sm["b_in"][l] = g_bin.reshape(N_IN)
        sm["ssm_c_re"][l] = _diag_c_grad(g_cd_re)
        sm["ssm_c_im"][l] = _diag_c_grad(g_cd_im)
        sm["ssm_d"][l] = g_d.reshape(WIDTH)
        sm["ssm_b_glu"][l] = g_bglu.reshape(WIDTH)
        sm["pool_w"][l] = g_pw
        sm["pool_scale"][l] = g_scale.reshape(WIDTH)
        g_abar_re[l] = g_a_re.reshape(32, 64)
        g_abar_im[l] = g_a_im.reshape(32, 64)
        g_bbar_re[l] = _diag_b_grad(g_bd_re)
        g_bbar_im[l] = _diag_b_grad(g_bd_im)
    grad_x = dx.reshape(1, SEQ, D_MODEL)

    g_ld, g_lr, g_li, g_btr, g_bti = _s5_params_bwd(
        log_dt3, ssm_lam_re, ssm_lam_im, bt_re, bt_im,
        jnp.stack(g_abar_re), jnp.stack(g_abar_im), jnp.stack(g_bbar_re), jnp.stack(g_bbar_im))
    small_g = {n: (jnp.stack(v) if v[0] is not None else None) for n, v in sm.items()}
    small_g["ssm_log_dt"] = g_ld.reshape(DEPTH, 32)
    small_g["ssm_lam_re"] = g_lr
    small_g["ssm_lam_im"] = g_li
    small_g["ssm_b_re"] = g_btr.transpose(0, 2, 3, 1)
    small_g["ssm_b_im"] = g_bti.transpose(0, 2, 3, 1)
    small_g["final_norm_g"] = g_final.reshape(D_MODEL)

    big_part = [jnp.stack(big_g[n]) for n in big_names]
    small_part = _pack_small(small_g)
    got, own, got_small = _exchange_sibling(big_part, small_part)
    chip_big = [_add_pairs(f"chip_{n}", a, b) for n, a, b in zip(big_names, own, got)]
    chip_small = _add_pairs("chip_small", small_part[None], got_small[None])
    slots_big, slots_small = _exchange_chips(chip_big, chip_small)

    res = {}
    for n, slots in zip(big_names, slots_big):
        res[n] = _sum_slots_adamw(n, slots, weights[n], mom_m[n], mom_v[n])
    packed = _sum_slots_adamw("small", slots_small[None], _pack_small(weights)[None], _pack_small(mom_m)[None],
                              _pack_small(mom_v)[None])
    unpacked = [_unpack_small(a[0]) for a in packed]
    for n, _ in SMALL:
        res[n] = tuple(u[n] for u in unpacked)

    outs = [loss, grad_x]
    for i in range(4):
        outs += [res[n][i] for n in order]
    return tuple(outs)
```

```python
import functools
import math

import jax
import jax.numpy as jnp
from jax import lax
from jax.experimental import pallas as pl
from jax.experimental.pallas import tpu as pltpu

F32 = jnp.float32
BF16 = jnp.bfloat16

SEQ = 2048
D_MODEL = 1024
N_IN = 4096
WIDTH = 512
N_STATE = 2048
N_CHUNK = 4
CH_W = WIDTH // N_CHUNK
CH_S = N_STATE // N_CHUNK
N_DEV = 8
N_CHIP = 4
POOL_WINDOWS = (2, 4, 8, 16)
POOL_GROUP = 128
EPS = 1e-6
DEPTH = 2

ADAM_LR = 0.001
ADAM_B1 = 0.9
ADAM_B2 = 0.999
ADAM_EPS = 1e-08
ADAM_WD = 0.01
ADAM_STEP = 10

TILE_M = 256
ROW_BLK = 512
VMEM_LIMIT = 48 * 1024 * 1024
VMEM_LIMIT_BIG = 60 * 1024 * 1024
MESH = pl.DeviceIdType.MESH
ANY = pl.BlockSpec(memory_space=pl.ANY)

GELU_C = math.sqrt(2.0 / math.pi)
GELU_A = 0.044715


def _cp(sem=None, limit=VMEM_LIMIT):
    return pltpu.CompilerParams(dimension_semantics=sem, vmem_limit_bytes=limit)


def _dot(a, b):
    return jnp.dot(a, b, preferred_element_type=F32)


def _dot_nt(a, b):
    return lax.dot_general(a, b, (((1,), (1,)), ((), ())), preferred_element_type=F32)


def _dot_tn(a, b):
    return lax.dot_general(a, b, (((0,), (0,)), ((), ())), preferred_element_type=F32)


def _sig(x):
    return jax.nn.sigmoid(x)


def _rms(x):
    rs = lax.rsqrt(jnp.mean(x * x, axis=-1, keepdims=True) + EPS)
    return rs, x * rs


def _slot(n):
    return 4 * (n % 2) + n // 2


def _const(shape):
    n = len(shape)
    return pl.BlockSpec(shape, lambda *_: (0,) * n)


def _s5_param_fn(log_dt, lam_re, lam_im, bt_re, bt_im):
    dt = jnp.exp(log_dt)
    mag = jnp.exp(lam_re * dt)
    ang = lam_im * dt
    abar_re = mag * jnp.cos(ang)
    abar_im = mag * jnp.sin(ang)
    num_re = abar_re - 1.0
    num_im = abar_im
    den = lam_re * lam_re + lam_im * lam_im
    coef_re = (num_re * lam_re + num_im * lam_im) / den
    coef_im = (num_im * lam_re - num_re * lam_im) / den
    bbar_re = coef_re[:, None] * bt_re - coef_im[:, None] * bt_im
    bbar_im = coef_re[:, None] * bt_im + coef_im[:, None] * bt_re
    return abar_re, abar_im, bbar_re, bbar_im


def _s5_params(log_dt, lam_re, lam_im, bt_re, bt_im):
    def body(ld, lr, li, br, bi, o_ar, o_ai, o_br, o_bi):
        ar, ai, bbr, bbi = _s5_param_fn(ld[...], lr[...], li[...], br[...], bi[...])
        o_ar[...] = ar
        o_ai[...] = ai
        o_br[...] = bbr
        o_bi[...] = bbi

    s = jax.ShapeDtypeStruct
    return pl.pallas_call(
        body, name="s5_params",
        out_shape=(s(lam_re.shape, F32), s(lam_re.shape, F32), s(bt_re.shape, F32), s(bt_re.shape, F32)),
    )(log_dt, lam_re, lam_im, bt_re, bt_im)


def _s5_params_bwd(log_dt, lam_re, lam_im, bt_re, bt_im, g_ar, g_ai, g_br, g_bi):
    def body(ld, lr, li, br, bi, car, cai, cbr, cbi, o_ld, o_lr, o_li, o_br, o_bi):
        _, vjp = jax.vjp(_s5_param_fn, ld[...], lr[...], li[...], br[...], bi[...])
        d_ld, d_lr, d_li, d_br, d_bi = vjp((car[...], cai[...], cbr[...], cbi[...]))
        o_ld[...] = d_ld
        o_lr[...] = d_lr
        o_li[...] = d_li
        o_br[...] = d_br
        o_bi[...] = d_bi

    s = jax.ShapeDtypeStruct
    return pl.pallas_call(
        body, name="s5_params_bwd",
        out_shape=(s(log_dt.shape, F32), s(lam_re.shape, F32), s(lam_re.shape, F32),
                   s(bt_re.shape, F32), s(bt_re.shape, F32)),
    )(log_dt, lam_re, lam_im, bt_re, bt_im, g_ar, g_ai, g_br, g_bi)


def _norm_proj(layer, x, norm_g, wg_in, b_in):
    def body(x_ref, g_ref, w_ref, b_ref, o_ref):
        _, xn = _rms(x_ref[...])
        h = (xn * g_ref[...]).astype(BF16)
        for k in range(N_DEV):
            cols = slice(k * WIDTH, (k + 1) * WIDTH)
            o_ref[:, cols] = _dot(h, w_ref[k]) + b_ref[:, cols]

    return pl.pallas_call(
        body, name=f"norm_proj_l{layer}",
        out_shape=jax.ShapeDtypeStruct((SEQ, N_IN), F32),
        grid=(SEQ // TILE_M,),
        in_specs=[pl.BlockSpec((TILE_M, D_MODEL), lambda i: (i, 0)),
                  pl.BlockSpec((None, 1, D_MODEL), lambda i: (layer, 0, 0)),
                  pl.BlockSpec((N_DEV, None, D_MODEL, WIDTH), lambda i: (0, layer, 0, 0)),
                  pl.BlockSpec((None, 1, N_IN), lambda i: (layer, 0, 0))],
        out_specs=pl.BlockSpec((TILE_M, N_IN), lambda i: (i, 0)),
        compiler_params=_cp(("parallel",)),
    )(x, norm_g, wg_in, b_in)


def _scan_tile_rows():
    return lax.broadcasted_iota(jnp.int32, (8, CH_S), 0)


def _s5_fwd(layer, proj, bd_re, bd_im, cd_re, cd_im, a_re, a_im, d_skip):
    def body(u_ref, bdre_ref, bdim_ref, cdre_ref, cdim_ref, are_ref, aim_ref, d_ref,
             sre_ref, sim_ref, y_ref):
        for rb in range(SEQ // ROW_BLK):
            rows = pl.ds(rb * ROW_BLK, ROW_BLK)
            ub = u_ref[rows, :].astype(BF16)
            sre_ref[rows, :] = _dot(ub, bdre_ref[...])
            sim_ref[rows, :] = _dot(ub, bdim_ref[...])
        ar = are_ref[...]
        ai = aim_ref[...]
        row_id = _scan_tile_rows()

        def step(i, carry):
            sr, si = carry
            base = pl.multiple_of(i * 8, 8)
            tr = sre_ref[pl.ds(base, 8), :]
            ti = sim_ref[pl.ds(base, 8), :]
            outr, outi = tr, ti
            for r in range(8):
                nsr = ar * sr - ai * si + tr[r:r + 1, :]
                nsi = ar * si + ai * sr + ti[r:r + 1, :]
                outr = jnp.where(row_id == r, nsr, outr)
                outi = jnp.where(row_id == r, nsi, outi)
                sr, si = nsr, nsi
            sre_ref[pl.ds(base, 8), :] = outr
            sim_ref[pl.ds(base, 8), :] = outi
            return sr, si

        zero = jnp.zeros((1, CH_S), F32)
        lax.fori_loop(0, SEQ // 8, step, (zero, zero))
        for rb in range(SEQ // ROW_BLK):
            rows = pl.ds(rb * ROW_BLK, ROW_BLK)
            y = (_dot(sre_ref[rows, :].astype(BF16), cdre_ref[...])
                 - _dot(sim_ref[rows, :].astype(BF16), cdim_ref[...]))
            y_ref[rows, :] = y + d_ref[...] * u_ref[rows, :]

    s = jax.ShapeDtypeStruct
    return pl.pallas_call(
        body, name=f"s5_fwd_l{layer}",
        out_shape=(s((SEQ, N_STATE), F32), s((SEQ, N_STATE), F32), s((SEQ, WIDTH), F32)),
        grid=(N_CHUNK,),
        in_specs=[pl.BlockSpec((SEQ, CH_W), lambda j: (0, j)),
                  pl.BlockSpec((None, CH_W, CH_S), lambda j: (j, 0, 0)),
                  pl.BlockSpec((None, CH_W, CH_S), lambda j: (j, 0, 0)),
                  pl.BlockSpec((None, CH_S, CH_W), lambda j: (j, 0, 0)),
                  pl.BlockSpec((None, CH_S, CH_W), lambda j: (j, 0, 0)),
                  pl.BlockSpec((1, CH_S), lambda j: (0, j)),
                  pl.BlockSpec((1, CH_S), lambda j: (0, j)),
                  pl.BlockSpec((1, CH_W), lambda j: (0, j))],
        out_specs=(pl.BlockSpec((SEQ, CH_S), lambda j: (0, j)),
                   pl.BlockSpec((SEQ, CH_S), lambda j: (0, j)),
                   pl.BlockSpec((SEQ, CH_W), lambda j: (0, j))),
        compiler_params=_cp(("parallel",)),
    )(proj, bd_re, bd_im, cd_re, cd_im, a_re, a_im, d_skip)


def _pool_counts(win):
    t = lax.broadcasted_iota(jnp.int32, (SEQ, POOL_GROUP), 0)
    return t, jnp.minimum(t + 1, win).astype(F32)


def _pool_fwd(layer, proj):
    def body(u_ref, o_ref):
        for gi, win in enumerate(POOL_WINDOWS):
            cols = slice(gi * POOL_GROUP, (gi + 1) * POOL_GROUP)
            u = u_ref[:, cols]
            t, count = _pool_counts(win)
            acc = u
            k = 1
            while k < win:
                acc = acc + jnp.where(t >= k, pltpu.roll(acc, k, 0), 0.0)
                k *= 2
            o_ref[:, cols] = acc / count - u

    return pl.pallas_call(
        body, name=f"pool_fwd_l{layer}",
        out_shape=jax.ShapeDtypeStruct((SEQ, WIDTH), F32),
        grid=(1,),
        in_specs=[pl.BlockSpec((SEQ, WIDTH), lambda i: (0, 2))],
        out_specs=pl.BlockSpec((SEQ, WIDTH), lambda i: (0, 0)),
        compiler_params=_cp(("arbitrary",)),
    )(proj)


def _gelu_parts(y0):
    t = jnp.tanh(GELU_C * (y0 + GELU_A * (y0 * y0 * y0)))
    return t, 0.5 * y0 * (1.0 + t)


def _mix_forward(p_ref, y0_ref, pooled_ref, wglu_ref, bglu_ref, pw_ref, scale_ref, wa_ref, wb_ref):
    za = p_ref[:, WIDTH:2 * WIDTH]
    zb = p_ref[:, 3 * WIDTH:4 * WIDTH]
    ga = p_ref[:, 4 * WIDTH:4 * WIDTH + D_MODEL]
    gb = p_ref[:, 4 * WIDTH + D_MODEL:]
    y0 = y0_ref[...]
    t, y1 = _gelu_parts(y0)
    y1b = y1.astype(BF16)
    q = _dot(y1b, wglu_ref[...].reshape(WIDTH, WIDTH)) + bglu_ref[...]
    sq = _sig(q)
    y2 = y1 * sq
    sza = _sig(za)
    silu_za = za * sza
    ya = y2 * silu_za
    pooled = pooled_ref[...]
    mixed = jnp.concatenate(
        [_dot(pooled[:, g * POOL_GROUP:(g + 1) * POOL_GROUP].astype(BF16), pw_ref[g].astype(BF16))
         for g in range(len(POOL_WINDOWS))], axis=1)
    szb = _sig(zb)
    silu_zb = zb * szb
    ms = mixed * scale_ref[...]
    yb = ms * silu_zb
    yab = ya.astype(BF16)
    ybb = yb.astype(BF16)
    ma = jnp.concatenate([_dot(yab, wa_ref[k]) for k in range(N_DEV)], axis=1)
    mb = jnp.concatenate([_dot(ybb, wb_ref[k]) for k in range(N_DEV)], axis=1)
    sga = _sig(ga)
    sgb = _sig(gb)
    merged = sga * ma + sgb * mb
    return dict(za=za, zb=zb, y0=y0, t=t, y1=y1, y1b=y1b, sq=sq, y2=y2, sza=sza, silu_za=silu_za,
                pooled=pooled, mixed=mixed, szb=szb, silu_zb=silu_zb, ms=ms, yab=yab, ybb=ybb,
                ma=ma, mb=mb, sga=sga, sgb=sgb, merged=merged)


def _mix_weight_specs(layer):
    return [pl.BlockSpec((N_DEV, None, WIDTH // N_DEV, WIDTH), lambda i: (0, layer, 0, 0)),
            pl.BlockSpec((None, 1, WIDTH), lambda i: (layer, 0, 0)),
            pl.BlockSpec((None, 4, POOL_GROUP, POOL_GROUP), lambda i: (layer, 0, 0, 0)),
            pl.BlockSpec((None, 1, WIDTH), lambda i: (layer, 0, 0)),
            pl.BlockSpec((N_DEV, None, WIDTH, D_MODEL // N_DEV), lambda i: (0, layer, 0, 0)),
            pl.BlockSpec((N_DEV, None, WIDTH, D_MODEL // N_DEV), lambda i: (0, layer, 0, 0)),
            pl.BlockSpec((N_DEV, None, D_MODEL // N_DEV, D_MODEL), lambda i: (0, layer, 0, 0))]


def _mix_fwd(layer, x, proj, y0, pooled, wg_glu, b_glu, pool_w, pool_scale, wg_a, wg_b, wg_out):
    def body(x_ref, p_ref, y0_ref, pooled_ref, wglu_ref, bglu_ref, pw_ref, scale_ref, wa_ref, wb_ref,
             wout_ref, o_ref):
        f = _mix_forward(p_ref, y0_ref, pooled_ref, wglu_ref, bglu_ref, pw_ref, scale_ref, wa_ref, wb_ref)
        wout = wout_ref[...].reshape(D_MODEL, D_MODEL)
        o_ref[...] = x_ref[...] + _dot(f["merged"].astype(BF16), wout)

    return pl.pallas_call(
        body, name=f"mix_fwd_l{layer}",
        out_shape=jax.ShapeDtypeStruct((SEQ, D_MODEL), F32),
        grid=(SEQ // TILE_M,),
        in_specs=[pl.BlockSpec((TILE_M, D_MODEL), lambda i: (i, 0)),
                  pl.BlockSpec((TILE_M, N_IN), lambda i: (i, 0)),
                  pl.BlockSpec((TILE_M, WIDTH), lambda i: (i, 0)),
                  pl.BlockSpec((TILE_M, WIDTH), lambda i: (i, 0))] + _mix_weight_specs(layer),
        out_specs=pl.BlockSpec((TILE_M, D_MODEL), lambda i: (i, 0)),
        compiler_params=_cp(("parallel",)),
    )(x, proj, y0, pooled, wg_glu, b_glu, pool_w, pool_scale, wg_a, wg_b, wg_out)


def _loss_head(x, target, final_g):
    def body(x_ref, t_ref, g_ref, dx_ref, loss_ref, gg_ref):
        @pl.when(pl.program_id(0) == 0)
        def _():
            loss_ref[...] = jnp.zeros_like(loss_ref)
            gg_ref[...] = jnp.zeros_like(gg_ref)

        g = g_ref[...]
        rs, xn = _rms(x_ref[...])
        err = xn * g - t_ref[...]
        loss_ref[...] += 0.5 * jnp.sum(jnp.mean(err * err, axis=-1, keepdims=True), axis=0, keepdims=True)
        dy = err * (1.0 / D_MODEL)
        gg_ref[...] += jnp.sum(dy * xn, axis=0, keepdims=True)
        dxn = dy * g
        dx_ref[...] = rs * (dxn - xn * jnp.mean(dxn * xn, axis=-1, keepdims=True))

    s = jax.ShapeDtypeStruct
    return pl.pallas_call(
        body, name="loss_head",
        out_shape=(s((SEQ, D_MODEL), F32), s((1, 1), F32), s((1, D_MODEL), F32)),
        grid=(SEQ // TILE_M,),
        in_specs=[pl.BlockSpec((TILE_M, D_MODEL), lambda i: (i, 0)),
                  pl.BlockSpec((TILE_M, D_MODEL), lambda i: (i, 0)),
                  _const((1, D_MODEL))],
        out_specs=(pl.BlockSpec((TILE_M, D_MODEL), lambda i: (i, 0)), _const((1, 1)), _const((1, D_MODEL))),
        compiler_params=_cp(("arbitrary",)),
    )(x, target, final_g)


def _mix_bwd(layer, dx_next, proj, y0, pooled, wg_glu, b_glu, pool_w, pool_scale, wg_a, wg_b, wg_out):
    n_k = N_DEV

    def body(dx_ref, p_ref, y0_ref, pooled_ref, wglu_ref, bglu_ref, pw_ref, scale_ref, wa_ref, wb_ref,
             wout_ref, dproj_ref, dy0_ref, dpooled_ref, gwout_ref, gwa_ref, gwb_ref, gwglu_ref, gpw_ref,
             gscale_ref, gbglu_ref):
        @pl.when(pl.program_id(0) == 0)
        def _():
            for r in (gwout_ref, gwa_ref, gwb_ref, gwglu_ref, gpw_ref, gscale_ref, gbglu_ref):
                r[...] = jnp.zeros_like(r)

        f = _mix_forward(p_ref, y0_ref, pooled_ref, wglu_ref, bglu_ref, pw_ref, scale_ref, wa_ref, wb_ref)
        wglu = wglu_ref[...].reshape(WIDTH, WIDTH)
        wout = wout_ref[...].reshape(D_MODEL, D_MODEL)
        blk = D_MODEL // n_k
        dxb = dx_ref[...].astype(BF16)
        dmerged = _dot_nt(dxb, wout)
        gwout = _dot_tn(f["merged"].astype(BF16), dxb)
        for k in range(n_k):
            gwout_ref[_slot(k)] += gwout[k * blk:(k + 1) * blk, :]
        dma = dmerged * f["sga"]
        dmb = dmerged * f["sgb"]
        dga = dmerged * f["ma"] * f["sga"] * (1.0 - f["sga"])
        dgb = dmerged * f["mb"] * f["sgb"] * (1.0 - f["sgb"])
        dmab = dma.astype(BF16)
        dmbb = dmb.astype(BF16)
        dya = jnp.zeros((TILE_M, WIDTH), F32)
        dyb = jnp.zeros((TILE_M, WIDTH), F32)
        for k in range(n_k):
            da_k = dmab[:, k * blk:(k + 1) * blk]
            db_k = dmbb[:, k * blk:(k + 1) * blk]
            dya = dya + _dot_nt(da_k, wa_ref[k])
            dyb = dyb + _dot_nt(db_k, wb_ref[k])
            gwa_ref[_slot(k)] += _dot_tn(f["yab"], da_k)
            gwb_ref[_slot(k)] += _dot_tn(f["ybb"], db_k)
        zb, szb = f["zb"], f["szb"]
        dzb = dyb * f["ms"] * (szb * (1.0 + zb * (1.0 - szb)))
        dms = dyb * f["silu_zb"]
        gscale_ref[...] += jnp.sum(dms * f["mixed"], axis=0, keepdims=True)
        dmixed = (dms * scale_ref[...]).astype(BF16)
        pooled = f["pooled"]
        for g in range(len(POOL_WINDOWS)):
            cols = slice(g * POOL_GROUP, (g + 1) * POOL_GROUP)
            dpooled_ref[:, cols] = _dot_nt(dmixed[:, cols], pw_ref[g].astype(BF16))
            gpw_ref[g] += _dot_tn(pooled[:, cols].astype(BF16), dmixed[:, cols])
        za, sza = f["za"], f["sza"]
        dza = dya * f["y2"] * (sza * (1.0 + za * (1.0 - sza)))
        dy2 = dya * f["silu_za"]
        sq = f["sq"]
        dq = dy2 * f["y1"] * sq * (1.0 - sq)
        dqb = dq.astype(BF16)
        dy1 = dy2 * sq + _dot_nt(dqb, wglu)
        gwglu = _dot_tn(f["y1b"], dqb)
        rblk = WIDTH // n_k
        for k in range(n_k):
            gwglu_ref[_slot(k)] += gwglu[k * rblk:(k + 1) * rblk, :]
        gbglu_ref[...] += jnp.sum(dq, axis=0, keepdims=True)
        y0, t = f["y0"], f["t"]
        dgelu = 0.5 * (1.0 + t) + 0.5 * y0 * (1.0 - t * t) * (GELU_C * (1.0 + 3.0 * GELU_A * y0 * y0))
        dy0_ref[...] = dy1 * dgelu
        zeros = jnp.zeros((TILE_M, WIDTH), BF16)
        dproj_ref[:, 0:WIDTH] = zeros
        dproj_ref[:, WIDTH:2 * WIDTH] = dza.astype(BF16)
        dproj_ref[:, 2 * WIDTH:3 * WIDTH] = zeros
        dproj_ref[:, 3 * WIDTH:4 * WIDTH] = dzb.astype(BF16)
        dproj_ref[:, 4 * WIDTH:4 * WIDTH + D_MODEL] = dga.astype(BF16)
        dproj_ref[:, 4 * WIDTH + D_MODEL:] = dgb.astype(BF16)

    s = jax.ShapeDtypeStruct
    tile = lambda w: pl.BlockSpec((TILE_M, w), lambda i: (i, 0))
    return pl.pallas_call(
        body, name=f"mix_bwd_l{layer}",
        out_shape=(s((SEQ, N_IN), BF16), s((SEQ, WIDTH), F32), s((SEQ, WIDTH), F32),
                   s((N_DEV, D_MODEL // N_DEV, D_MODEL), F32),
                   s((N_DEV, WIDTH, D_MODEL // N_DEV), F32),
                   s((N_DEV, WIDTH, D_MODEL // N_DEV), F32),
                   s((N_DEV, WIDTH // N_DEV, WIDTH), F32),
                   s((4, POOL_GROUP, POOL_GROUP), F32), s((1, WIDTH), F32), s((1, WIDTH), F32)),
        grid=(SEQ // TILE_M,),
        in_specs=[tile(D_MODEL), tile(N_IN), tile(WIDTH), tile(WIDTH)] + _mix_weight_specs(layer),
        out_specs=(tile(N_IN), tile(WIDTH), tile(WIDTH),
                   _const((N_DEV, D_MODEL // N_DEV, D_MODEL)), _const((N_DEV, WIDTH, D_MODEL // N_DEV)),
                   _const((N_DEV, WIDTH, D_MODEL // N_DEV)), _const((N_DEV, WIDTH // N_DEV, WIDTH)),
                   _const((4, POOL_GROUP, POOL_GROUP)), _const((1, WIDTH)), _const((1, WIDTH))),
        compiler_params=_cp(("arbitrary",), VMEM_LIMIT_BIG),
    )(dx_next, proj, y0, pooled, wg_glu, b_glu, pool_w, pool_scale, wg_a, wg_b, wg_out)


def _pool_bwd(layer, dpooled, dproj):
    def body(dp_ref, _, o_ref):
        for gi, win in enumerate(POOL_WINDOWS):
            cols = slice(gi * POOL_GROUP, (gi + 1) * POOL_GROUP)
            dp = dp_ref[:, cols]
            t, count = _pool_counts(win)
            e = dp / count
            acc = e
            k = 1
            while k < win:
                acc = acc + jnp.where(t < SEQ - k, pltpu.roll(acc, SEQ - k, 0), 0.0)
                k *= 2
            o_ref[:, cols] = (acc - dp).astype(BF16)

    return pl.pallas_call(
        body, name=f"pool_bwd_l{layer}",
        out_shape=jax.ShapeDtypeStruct((SEQ, N_IN), BF16),
        grid=(1,),
        in_specs=[pl.BlockSpec((SEQ, WIDTH), lambda i: (0, 0)), ANY],
        out_specs=pl.BlockSpec((SEQ, WIDTH), lambda i: (0, 2)),
        input_output_aliases={1: 0},
        compiler_params=_cp(("arbitrary",)),
    )(dpooled, dproj)


def _s5_bwd(layer, dy0, proj, s_re, s_im, bd_re, bd_im, cd_re, cd_im, a_re, a_im, d_skip, dproj):
    def body(dy_ref, u_ref, sre_ref, sim_ref, bdre_ref, bdim_ref, cdre_ref, cdim_ref, are_ref, aim_ref,
             d_ref, _, du_ref, gbre_ref, gbim_ref, gcre_ref, gcim_ref, gare_ref, gaim_ref, gd_ref,
             lre_ref, lim_ref):
        n_rb = SEQ // ROW_BLK
        gcre = jnp.zeros((CH_S, CH_W), F32)
        gcim = jnp.zeros((CH_S, CH_W), F32)
        for rb in range(n_rb):
            rows = pl.ds(rb * ROW_BLK, ROW_BLK)
            dyb = dy_ref[rows, :].astype(BF16)
            lre_ref[rows, :] = _dot_nt(dyb, cdre_ref[...])
            lim_ref[rows, :] = -_dot_nt(dyb, cdim_ref[...])
            gcre = gcre + _dot_tn(sre_ref[rows, :].astype(BF16), dyb)
            gcim = gcim - _dot_tn(sim_ref[rows, :].astype(BF16), dyb)
        gcre_ref[...] = gcre
        gcim_ref[...] = gcim
        ar = are_ref[...]
        ai = aim_ref[...]
        row_id = _scan_tile_rows()

        def step(n, carry):
            lr, li = carry
            base = pl.multiple_of((SEQ // 8 - 1 - n) * 8, 8)
            tr = lre_ref[pl.ds(base, 8), :]
            ti = lim_ref[pl.ds(base, 8), :]
            outr, outi = tr, ti
            for r in range(7, -1, -1):
                nlr = ar * lr + ai * li + tr[r:r + 1, :]
                nli = ar * li - ai * lr + ti[r:r + 1, :]
                outr = jnp.where(row_id == r, nlr, outr)
                outi = jnp.where(row_id == r, nli, outi)
                lr, li = nlr, nli
            lre_ref[pl.ds(base, 8), :] = outr
            lim_ref[pl.ds(base, 8), :] = outi
            return lr, li

        zero = jnp.zeros((1, CH_S), F32)
        lax.fori_loop(0, SEQ // 8, step, (zero, zero))

        gare = jnp.zeros((1, CH_S), F32)
        gaim = jnp.zeros((1, CH_S), F32)
        gbre = jnp.zeros((CH_W, CH_S), F32)
        gbim = jnp.zeros((CH_W, CH_S), F32)
        gd = jnp.zeros((1, CH_W), F32)
        first = lax.broadcasted_iota(jnp.int32, (ROW_BLK, CH_S), 0) == 0
        for rb in range(n_rb):
            rows = pl.ds(rb * ROW_BLK, ROW_BLK)
            lr = lre_ref[rows, :]
            li = lim_ref[rows, :]
            if rb == 0:
                prev_r = jnp.zeros((1, CH_S), F32)
                prev_i = jnp.zeros((1, CH_S), F32)
            else:
                prev_r = sre_ref[pl.ds(rb * ROW_BLK - 1, 1), :]
                prev_i = sim_ref[pl.ds(rb * ROW_BLK - 1, 1), :]
            spr = jnp.where(first, prev_r, pltpu.roll(sre_ref[rows, :], 1, 0))
            spi = jnp.where(first, prev_i, pltpu.roll(sim_ref[rows, :], 1, 0))
            gare = gare + jnp.sum(lr * spr + li * spi, axis=0, keepdims=True)
            gaim = gaim + jnp.sum(li * spr - lr * spi, axis=0, keepdims=True)
            lrb = lr.astype(BF16)
            lib = li.astype(BF16)
            u = u_ref[rows, :]
            ub = u.astype(BF16)
            dy = dy_ref[rows, :]
            du = dy * d_ref[...] + _dot_nt(lrb, bdre_ref[...]) + _dot_nt(lib, bdim_ref[...])
            du_ref[rows, :] = du.astype(BF16)
            gbre = gbre + _dot_tn(ub, lrb)
            gbim = gbim + _dot_tn(ub, lib)
            gd = gd + jnp.sum(dy * u, axis=0, keepdims=True)
        gare_ref[...] = gare
        gaim_ref[...] = gaim
        gbre_ref[...] = gbre
        gbim_ref[...] = gbim
        gd_ref[...] = gd

    s = jax.ShapeDtypeStruct
    chunk_w = lambda: pl.BlockSpec((SEQ, CH_W), lambda j: (0, j))
    chunk_s = lambda: pl.BlockSpec((SEQ, CH_S), lambda j: (0, j))
    bspec = lambda: pl.BlockSpec((None, CH_W, CH_S), lambda j: (j, 0, 0))
    cspec = lambda: pl.BlockSpec((None, CH_S, CH_W), lambda j: (j, 0, 0))
    return pl.pallas_call(
        body, name=f"s5_bwd_l{layer}",
        out_shape=(s((SEQ, N_IN), BF16),
                   s((N_CHUNK, CH_W, CH_S), F32), s((N_CHUNK, CH_W, CH_S), F32),
                   s((N_CHUNK, CH_S, CH_W), F32), s((N_CHUNK, CH_S, CH_W), F32),
                   s((1, N_STATE), F32), s((1, N_STATE), F32), s((1, WIDTH), F32)),
        grid=(N_CHUNK,),
        in_specs=[chunk_w(), chunk_w(), chunk_s(), chunk_s(), bspec(), bspec(), cspec(), cspec(),
                  pl.BlockSpec((1, CH_S), lambda j: (0, j)), pl.BlockSpec((1, CH_S), lambda j: (0, j)),
                  pl.BlockSpec((1, CH_W), lambda j: (0, j)), ANY],
        out_specs=(chunk_w(), bspec(), bspec(), cspec(), cspec(),
                   pl.BlockSpec((1, CH_S), lambda j: (0, j)), pl.BlockSpec((1, CH_S), lambda j: (0, j)),
                   pl.BlockSpec((1, CH_W), lambda j: (0, j))),
        scratch_shapes=[pltpu.VMEM((SEQ, CH_S), F32), pltpu.VMEM((SEQ, CH_S), F32)],
        input_output_aliases={11: 0},
        compiler_params=_cp(("arbitrary",), VMEM_LIMIT_BIG),
    )(dy0, proj, s_re, s_im, bd_re, bd_im, cd_re, cd_im, a_re, a_im, d_skip, dproj)


def _proj_wgrad(layer, x, norm_g, dproj):
    tm = 512

    def body(x_ref, g_ref, dp_ref, gw_ref, gb_ref):
        @pl.when(pl.program_id(1) == 0)
        def _():
            gw_ref[...] = jnp.zeros_like(gw_ref)
            gb_ref[...] = jnp.zeros_like(gb_ref)

        _, xn = _rms(x_ref[...])
        h = (xn * g_ref[...]).astype(BF16)
        dp = dp_ref[...]
        gw_ref[...] += _dot_tn(h, dp)
        gb_ref[...] += jnp.sum(dp.astype(F32), axis=0, keepdims=True)

    s = jax.ShapeDtypeStruct
    return pl.pallas_call(
        body, name=f"proj_wgrad_l{layer}",
        out_shape=(s((N_DEV, D_MODEL, WIDTH), F32), s((1, N_IN), F32)),
        grid=(N_DEV, SEQ // tm),
        in_specs=[pl.BlockSpec((tm, D_MODEL), lambda n, t: (t, 0)),
                  pl.BlockSpec((None, 1, D_MODEL), lambda n, t: (layer, 0, 0)),
                  pl.BlockSpec((tm, WIDTH), lambda n, t: (t, n))],
        out_specs=(pl.BlockSpec((None, D_MODEL, WIDTH), lambda n, t: (_slot(n), 0, 0)),
                   pl.BlockSpec((1, WIDTH), lambda n, t: (0, n))),
        compiler_params=_cp(("parallel", "arbitrary")),
    )(x, norm_g, dproj)


def _proj_dgrad(layer, dx_next, x, norm_g, dproj, wg_in):
    def body(dxn_ref, x_ref, g_ref, dp_ref, w_ref, dx_ref, gg_ref):
        @pl.when(pl.program_id(0) == 0)
        def _():
            gg_ref[...] = jnp.zeros_like(gg_ref)

        dh = jnp.zeros((TILE_M, D_MODEL), F32)
        for k in range(N_DEV):
            dh = dh + _dot_nt(dp_ref[:, k * WIDTH:(k + 1) * WIDTH], w_ref[k])
        rs, xn = _rms(x_ref[...])
        gg_ref[...] += jnp.sum(dh * xn, axis=0, keepdims=True)
        dxn = dh * g_ref[...]
        dx_ref[...] = dxn_ref[...] + rs * (dxn - xn * jnp.mean(dxn * xn, axis=-1, keepdims=True))

    s = jax.ShapeDtypeStruct
    return pl.pallas_call(
        body, name=f"proj_dgrad_l{layer}",
        out_shape=(s((SEQ, D_MODEL), F32), s((1, D_MODEL), F32)),
        grid=(SEQ // TILE_M,),
        in_specs=[pl.BlockSpec((TILE_M, D_MODEL), lambda i: (i, 0)),
                  pl.BlockSpec((TILE_M, D_MODEL), lambda i: (i, 0)),
                  pl.BlockSpec((None, 1, D_MODEL), lambda i: (layer, 0, 0)),
                  pl.BlockSpec((TILE_M, N_IN), lambda i: (i, 0)),
                  pl.BlockSpec((N_DEV, None, D_MODEL, WIDTH), lambda i: (0, layer, 0, 0))],
        out_specs=(pl.BlockSpec((TILE_M, D_MODEL), lambda i: (i, 0)), _const((1, D_MODEL))),
        compiler_params=_cp(("arbitrary",)),
    )(dx_next, x, norm_g, dproj, wg_in)


def _my_place():
    return lax.axis_index("x"), lax.axis_index("y"), lax.axis_index("c")


def _gather_weights(shards):
    n = len(shards)

    def body(*refs):
        src = refs[:n]
        out = refs[n:2 * n]
        send_sems, recv_sems, local_sems = refs[2 * n:]
        x, y, c = _my_place()
        me, sibling = (x, y, c), (x, y, 1 - c)
        chips = [(1 - x, y), (x, 1 - y), (1 - x, 1 - y)]

        def rows(t, place):
            px, py, pc = place
            return out[t].at[pl.ds(4 * px + 2 * py + pc, 1)]

        def copy(t, k, block, to, from_src=False):
            return pltpu.make_async_remote_copy(
                src_ref=src[t] if from_src else rows(t, block), dst_ref=rows(t, block),
                send_sem=send_sems.at[7 * t + k], recv_sem=recv_sems.at[7 * t + k], device_id=to,
                device_id_type=MESH)

        mine = [pltpu.make_async_copy(src[t], rows(t, me), local_sems.at[t]) for t in range(n)]
        for cp in mine:
            cp.start()
        first = []
        for t in range(n):
            first.append(copy(t, 0, me, sibling, from_src=True))
            first += [copy(t, 1 + j, me, (*chip, c), from_src=True) for j, chip in enumerate(chips)]
        for cp in first:
            cp.start()
        passed = []
        for t in range(n):
            for j, chip in enumerate(chips):
                copy(t, 1 + j, (*chip, c), me).wait_recv()
                fwd = copy(t, 4 + j, (*chip, c), sibling)
                fwd.start()
                passed.append(fwd)
        for t in range(n):
            copy(t, 0, sibling, me).wait_recv()
            for j, chip in enumerate(chips):
                copy(t, 4 + j, (*chip, 1 - c), me).wait_recv()
        for cp in first + passed:
            cp.wait_send()
        for cp in mine:
            cp.wait()

    return pl.pallas_call(
        body, name="gather_weights",
        out_shape=tuple(jax.ShapeDtypeStruct((N_DEV,) + a.shape[1:], a.dtype) for a in shards),
        in_specs=[ANY] * n, out_specs=tuple([ANY] * n),
        scratch_shapes=[pltpu.SemaphoreType.DMA((7 * n,)), pltpu.SemaphoreType.DMA((7 * n,)),
                        pltpu.SemaphoreType.DMA((n,))],
    )(*shards)


def _exchange_sibling(big, small):
    n = len(big)
    n_small = 4
    rows_small = small.shape[0] // n_small
    n_copies = n * DEPTH * 4 + n_small

    def body(*refs):
        g = refs[:n]
        sg = refs[n]
        got = refs[n + 1:2 * n + 1]
        got_small = refs[2 * n + 1]
        send_sems, recv_sems = refs[2 * n + 2:]
        x, y, c = _my_place()
        sibling = (x, y, 1 - c)
        pairs = []
        for t in range(n):
            for l in range(DEPTH):
                for s in range(4):
                    pairs.append((g[t].at[l, pl.ds(4 * (1 - c) + s, 1)], got[t].at[l, pl.ds(s, 1)]))
        for s in range(n_small):
            rows = pl.ds(s * rows_small, rows_small)
            pairs.append((sg.at[rows], got_small.at[rows]))
        copies = [pltpu.make_async_remote_copy(
            src_ref=src, dst_ref=dst, send_sem=send_sems.at[k], recv_sem=recv_sems.at[k],
            device_id=sibling, device_id_type=MESH) for k, (src, dst) in enumerate(pairs)]
        for cp in copies:
            cp.start()
        for cp in copies:
            cp.wait()

    half = lambda a: jax.ShapeDtypeStruct((a.shape[0], 4) + a.shape[2:], a.dtype)
    out_shape = tuple([half(a) for a in big] + [jax.ShapeDtypeStruct(small.shape, small.dtype)])
    res = pl.pallas_call(
        body, name="exchange_sibling", out_shape=out_shape,
        in_specs=[ANY] * (n + 1), out_specs=tuple([ANY] * (n + 1)),
        scratch_shapes=[pltpu.SemaphoreType.DMA((n_copies,)), pltpu.SemaphoreType.DMA((n_copies,))],
    )(*big, small)
    return res[:n], res[n]


def _exchange_chips(big, small):
    n = len(big)

    def body(*refs):
        cp_in = refs[:n]
        sm_in = refs[n]
        slots = refs[n + 1:2 * n + 1]
        sm_slots = refs[2 * n + 1]
        send_sems, recv_sems, local_sems = refs[2 * n + 2:]
        x, y, c = _my_place()
        my_chip = 2 * x + y
        chips = [(1 - x, y), (x, 1 - y), (1 - x, 1 - y)]

        def remote(k, src, dst, chip):
            return pltpu.make_async_remote_copy(
                src_ref=src, dst_ref=dst, send_sem=send_sems.at[k], recv_sem=recv_sems.at[k],
                device_id=(*chip, c), device_id_type=MESH)

        sends, waits, local = [], [], []
        k = 0
        for j, chip in enumerate(chips):
            to = 2 * chip[0] + chip[1]
            for t in range(n):
                for l in range(DEPTH):
                    src = cp_in[t].at[l, pl.ds(to, 1)]
                    sends.append(remote(k, src, slots[t].at[l, pl.ds(my_chip, 1)], chip))
                    waits.append(remote(k, src, slots[t].at[l, pl.ds(to, 1)], chip))
                    k += 1
            sends.append(remote(k, sm_in, sm_slots.at[pl.ds(my_chip, 1)], chip))
            waits.append(remote(k, sm_in, sm_slots.at[pl.ds(to, 1)], chip))
            k += 1
        for t in range(n):
            local.append(pltpu.make_async_copy(
                cp_in[t].at[:, pl.ds(my_chip, 1)], slots[t].at[:, pl.ds(my_chip, 1)], local_sems.at[t]))
        local.append(pltpu.make_async_copy(sm_in, sm_slots.at[pl.ds(my_chip, 1)], local_sems.at[n]))
        for cp in sends + local:
            cp.start()
        for cp in waits + local:
            cp.wait()

    n_copies = 3 * (n * DEPTH + 1)
    out_shape = tuple([jax.ShapeDtypeStruct(a.shape, a.dtype) for a in big]
                      + [jax.ShapeDtypeStruct((N_CHIP,) + small.shape[1:], small.dtype)])
    res = pl.pallas_call(
        body, name="exchange_chips", out_shape=out_shape,
        in_specs=[ANY] * (n + 1), out_specs=tuple([ANY] * (n + 1)),
        scratch_shapes=[pltpu.SemaphoreType.DMA((n_copies,)), pltpu.SemaphoreType.DMA((n_copies,)),
                        pltpu.SemaphoreType.DMA((n + 1,))],
    )(*big, small)
    return res[:n], res[n]


def _row_block(rows):
    return rows if rows <= 256 else 256


def _add_pairs(tag, a, b):
    shape = a.shape
    r, c = shape[-2], shape[-1]
    lead = int(math.prod(shape[:-2]))
    rb = _row_block(r)

    def body(a_ref, b_ref, o_ref):
        o_ref[...] = a_ref[...] + b_ref[...]

    spec = pl.BlockSpec((None, rb, c), lambda i, j: (i, j, 0))
    out = pl.pallas_call(
        body, name=f"add_{tag}", out_shape=jax.ShapeDtypeStruct((lead, r, c), a.dtype),
        grid=(lead, r // rb), in_specs=[spec, spec], out_specs=spec,
        compiler_params=_cp(("parallel", "parallel")),
    )(a.reshape(lead, r, c), b.reshape(lead, r, c))
    return out.reshape(shape)


def _add_own(tag, core, g, got):
    _, _, r, c = got.shape
    rb = _row_block(r)

    def body(core_ref, a_ref, b_ref, o_ref):
        o_ref[...] = (a_ref[...] + b_ref[...]).astype(o_ref.dtype)

    blk = (None, None, rb, c)
    return pl.pallas_call(
        body, name=f"add_{tag}", out_shape=jax.ShapeDtypeStruct(got.shape, BF16),
        grid_spec=pltpu.PrefetchScalarGridSpec(
            num_scalar_prefetch=1, grid=(DEPTH, 4, r // rb),
            in_specs=[pl.BlockSpec(blk, lambda l, s, j, core: (l, 4 * core[0] + s, j, 0)),
                      pl.BlockSpec(blk, lambda l, s, j, core: (l, s, j, 0))],
            out_specs=pl.BlockSpec(blk, lambda l, s, j, core: (l, s, j, 0))),
        compiler_params=_cp(("parallel", "parallel", "parallel")),
    )(core, g, got)


def _adamw_math(w, g, m, v):
    m = ADAM_B1 * m + (1.0 - ADAM_B1) * g
    v = ADAM_B2 * v + (1.0 - ADAM_B2) * (g * g)
    m_hat = m / (1.0 - ADAM_B1 ** ADAM_STEP)
    v_hat = v / (1.0 - ADAM_B2 ** ADAM_STEP)
    delta = -ADAM_LR * (m_hat / (jnp.sqrt(v_hat) + ADAM_EPS) + ADAM_WD * w)
    return delta, m, v


def _sum_slots_adamw(tag, slots, w, m, v):
    b, _, r, c = slots.shape
    rb = _row_block(r)

    def body(s_ref, w_ref, m_ref, v_ref, g_ref, d_ref, nm_ref, nv_ref):
        p = [s_ref[k].astype(F32) for k in range(N_CHIP)]
        g = (p[0] + p[1]) + (p[2] + p[3])
        delta, nm, nv = _adamw_math(w_ref[...], g, m_ref[...], v_ref[...])
        g_ref[...] = g
        d_ref[...] = delta
        nm_ref[...] = nm
        nv_ref[...] = nv

    spec = pl.BlockSpec((None, rb, c), lambda i, j: (i, j, 0))
    sspec = pl.BlockSpec((None, N_CHIP, rb, c), lambda i, j: (i, 0, j, 0))
    s = jax.ShapeDtypeStruct((b, r, c), F32)
    return pl.pallas_call(
        body, name=f"adamw_{tag}", out_shape=(s, s, s, s),
        grid=(b, r // rb), in_specs=[sspec, spec, spec, spec], out_specs=(spec, spec, spec, spec),
        compiler_params=_cp(("parallel", "parallel")),
    )(slots, w, m, v)


SMALL = (("norm_g", (2, 1024)), ("b_in", (2, 4096)), ("ssm_log_dt", (2, 32)), ("ssm_lam_re", (2, 32, 64)),
         ("ssm_lam_im", (2, 32, 64)), ("ssm_b_re", (2, 32, 64, 16)), ("ssm_b_im", (2, 32, 64, 16)),
         ("ssm_c_re", (2, 32, 16, 64)), ("ssm_c_im", (2, 32, 16, 64)), ("ssm_d", (2, 512)),
         ("ssm_b_glu", (2, 512)), ("pool_w", (2, 4, 128, 128)), ("pool_scale", (2, 512)),
         ("final_norm_g", (1024,)))
PACK_ALIGN = 1024
PACK_ROWS = 3328


def _padded(n):
    return -(-n // PACK_ALIGN) * PACK_ALIGN


def _pack_small(vals):
    parts = []
    for name, shape in SMALL:
        flat = vals[name].reshape(-1)
        parts.append(jnp.pad(flat, (0, _padded(flat.shape[0]) - flat.shape[0])))
    flat = jnp.concatenate(parts)
    flat = jnp.pad(flat, (0, PACK_ROWS * 128 - flat.shape[0]))
    return flat.reshape(PACK_ROWS, 128)


def _unpack_small(packed):
    flat = packed.reshape(-1)
    out, off = {}, 0
    for name, shape in SMALL:
        n = int(math.prod(shape))
        out[name] = flat[off:off + n].reshape(shape)
        off += _padded(n)
    return out


def _block_diag_b(bbar_t):
    bb = bbar_t.reshape(16, N_CHUNK, 8, 64)
    eye = jnp.eye(8, dtype=bbar_t.dtype)
    return jnp.einsum("cjgp,gh->jgchp", bb, eye).reshape(N_CHUNK, CH_W, CH_S)


def _block_diag_c(c):
    cc = c.reshape(N_CHUNK, 8, 16, 64)
    eye = jnp.eye(8, dtype=c.dtype)
    return jnp.einsum("jhcp,gh->jgphc", cc, eye).reshape(N_CHUNK, CH_S, CH_W)


def _diag_b_grad(g):
    d = jnp.einsum("jgcgp->jgcp", g.reshape(N_CHUNK, 8, 16, 8, 64))
    return d.transpose(2, 0, 1, 3).reshape(16, 32, 64)


def _diag_c_grad(g):
    d = jnp.einsum("jgpgc->jgpc", g.reshape(N_CHUNK, 8, 64, 8, 16))
    return d.reshape(32, 64, 16).transpose(0, 2, 1)


def kernel(x, norm_g, w_in, b_in, ssm_log_dt, ssm_lam_re, ssm_lam_im, ssm_b_re, ssm_b_im, ssm_c_re, ssm_c_im, ssm_d, ssm_w_glu, ssm_b_glu, pool_w, pool_scale, w_branch_a, w_branch_b, w_out, final_norm_g, loss_target, m_norm_g, m_w_in, m_b_in, m_ssm_log_dt, m_ssm_lam_re, m_ssm_lam_im, m_ssm_b_re, m_ssm_b_im, m_ssm_c_re, m_ssm_c_im, m_ssm_d, m_ssm_w_glu, m_ssm_b_glu, m_pool_w, m_pool_scale, m_w_branch_a, m_w_branch_b, m_w_out, m_final_norm_g, v_norm_g, v_w_in, v_b_in, v_ssm_log_dt, v_ssm_lam_re, v_ssm_lam_im, v_ssm_b_re, v_ssm_b_im, v_ssm_c_re, v_ssm_c_im, v_ssm_d, v_ssm_w_glu, v_ssm_b_glu, v_pool_w, v_pool_scale, v_w_branch_a, v_w_branch_b, v_w_out, v_final_norm_g):
    weights = dict(norm_g=norm_g, w_in=w_in, b_in=b_in, ssm_log_dt=ssm_log_dt, ssm_lam_re=ssm_lam_re,
                   ssm_lam_im=ssm_lam_im, ssm_b_re=ssm_b_re, ssm_b_im=ssm_b_im, ssm_c_re=ssm_c_re,
                   ssm_c_im=ssm_c_im, ssm_d=ssm_d, ssm_w_glu=ssm_w_glu, ssm_b_glu=ssm_b_glu, pool_w=pool_w,
                   pool_scale=pool_scale, w_branch_a=w_branch_a, w_branch_b=w_branch_b, w_out=w_out,
                   final_norm_g=final_norm_g)
    mom_m = dict(norm_g=m_norm_g, w_in=m_w_in, b_in=m_b_in, ssm_log_dt=m_ssm_log_dt, ssm_lam_re=m_ssm_lam_re,
                 ssm_lam_im=m_ssm_lam_im, ssm_b_re=m_ssm_b_re, ssm_b_im=m_ssm_b_im, ssm_c_re=m_ssm_c_re,
                 ssm_c_im=m_ssm_c_im, ssm_d=m_ssm_d, ssm_w_glu=m_ssm_w_glu, ssm_b_glu=m_ssm_b_glu,
                 pool_w=m_pool_w, pool_scale=m_pool_scale, w_branch_a=m_w_branch_a, w_branch_b=m_w_branch_b,
                 w_out=m_w_out, final_norm_g=m_final_norm_g)
    mom_v = dict(norm_g=v_norm_g, w_in=v_w_in, b_in=v_b_in, ssm_log_dt=v_ssm_log_dt, ssm_lam_re=v_ssm_lam_re,
                 ssm_lam_im=v_ssm_lam_im, ssm_b_re=v_ssm_b_re, ssm_b_im=v_ssm_b_im, ssm_c_re=v_ssm_c_re,
                 ssm_c_im=v_ssm_c_im, ssm_d=v_ssm_d, ssm_w_glu=v_ssm_w_glu, ssm_b_glu=v_ssm_b_glu,
                 pool_w=v_pool_w, pool_scale=v_pool_scale, w_branch_a=v_w_branch_a, w_branch_b=v_w_branch_b,
                 w_out=v_w_out, final_norm_g=v_final_norm_g)
    order = ["norm_g", "w_in", "b_in", "ssm_log_dt", "ssm_lam_re", "ssm_lam_im", "ssm_b_re", "ssm_b_im",
             "ssm_c_re", "ssm_c_im", "ssm_d", "ssm_w_glu", "ssm_b_glu", "pool_w", "pool_scale", "w_branch_a",
             "w_branch_b", "w_out", "final_norm_g"]
    big_names = ["w_in", "ssm_w_glu", "w_branch_a", "w_branch_b", "w_out"]

    wg_in, wg_glu, wg_a, wg_b, wg_out = _gather_weights([weights[n].astype(BF16)[None] for n in big_names])

    log_dt3 = ssm_log_dt.reshape(DEPTH, 32, 1)
    bt_re = ssm_b_re.transpose(0, 3, 1, 2)
    bt_im = ssm_b_im.transpose(0, 3, 1, 2)
    abar_re, abar_im, bbar_re, bbar_im = _s5_params(log_dt3, ssm_lam_re, ssm_lam_im, bt_re, bt_im)
    s5 = []
    for l in range(DEPTH):
        s5.append(dict(
            a_re=abar_re[l].reshape(1, N_STATE), a_im=abar_im[l].reshape(1, N_STATE),
            bd_re=_block_diag_b(bbar_re[l]).astype(BF16), bd_im=_block_diag_b(bbar_im[l]).astype(BF16),
            cd_re=_block_diag_c(ssm_c_re[l]).astype(BF16), cd_im=_block_diag_c(ssm_c_im[l]).astype(BF16)))

    norm_g3 = norm_g.reshape(DEPTH, 1, D_MODEL)
    b_in3 = b_in.reshape(DEPTH, 1, N_IN)
    b_glu3 = ssm_b_glu.reshape(DEPTH, 1, WIDTH)
    scale3 = pool_scale.reshape(DEPTH, 1, WIDTH)
    d_skip = ssm_d.reshape(DEPTH, 1, WIDTH)

    xs = [x.reshape(SEQ, D_MODEL)]
    saved = []
    for l in range(DEPTH):
        p = s5[l]
        proj = _norm_proj(l, xs[l], norm_g3, wg_in, b_in3)
        s_re, s_im, y0 = _s5_fwd(l, proj, p["bd_re"], p["bd_im"], p["cd_re"], p["cd_im"], p["a_re"], p["a_im"],
                                 d_skip[l])
        pooled = _pool_fwd(l, proj)
        xs.append(_mix_fwd(l, xs[l], proj, y0, pooled, wg_glu, b_glu3, pool_w, scale3, wg_a, wg_b, wg_out))
        saved.append((proj, s_re, s_im, y0, pooled))

    dx, loss_part, g_final = _loss_head(xs[DEPTH], loss_target.reshape(SEQ, D_MODEL),
                                        final_norm_g.reshape(1, D_MODEL))
    loss = lax.psum(loss_part[0, 0], ("x", "y", "c"))

    big_g = {n: [None] * DEPTH for n in big_names}
    sm = {n: [None] * DEPTH for n, _ in SMALL}
    g_abar_re, g_abar_im, g_bbar_re, g_bbar_im = ([None] * DEPTH for _ in range(4))
    for l in reversed(range(DEPTH)):
        p = s5[l]
        proj, s_re, s_im, y0, pooled = saved[l]
        (dproj, dy0, dpooled, gw_out, gw_a, gw_b, gw_glu, g_pw, g_scale, g_bglu) = _mix_bwd(
            l, dx, proj, y0, pooled, wg_glu, b_glu3, pool_w, scale3, wg_a, wg_b, wg_out)
        dproj = _pool_bwd(l, dpooled, dproj)
        (dproj, g_bd_re, g_bd_im, g_cd_re, g_cd_im, g_a_re, g_a_im, g_d) = _s5_bwd(
            l, dy0, proj, s_re, s_im, p["bd_re"], p["bd_im"], p["cd_re"], p["cd_im"], p["a_re"], p["a_im"],
            d_skip[l], dproj)
        gw_in, g_bin = _proj_wgrad(l, xs[l], norm_g3, dproj)
        dx, g_norm = _proj_dgrad(l, dx, xs[l], norm_g3, dproj, wg_in)
        big_g["w_in"][l] = gw_in
        big_g["ssm_w_glu"][l] = gw_glu
        big_g["w_branch_a"][l] = gw_a
        big_g["w_branch_b"][l] = gw_b
        big_g["w_out"][l] = gw_out
        sm["norm_g"][l] = g_norm.reshape(D_MODEL)
        sm["b_in"][l] = g_bin.reshape(N_IN)
        sm["ssm_c_re"][l] = _diag_c_grad(g_cd_re)
        sm["ssm_c_im"][l] = _diag_c_grad(g_cd_im)
        sm["ssm_d"][l] = g_d.reshape(WIDTH)
        sm["ssm_b_glu"][l] = g_bglu.reshape(WIDTH)
        sm["pool_w"][l] = g_pw
        sm["pool_scale"][l] = g_scale.reshape(WIDTH)
        g_abar_re[l] = g_a_re.reshape(32, 64)
        g_abar_im[l] = g_a_im.reshape(32, 64)
        g_bbar_re[l] = _diag_b_grad(g_bd_re)
        g_bbar_im[l] = _diag_b_grad(g_bd_im)
    grad_x = dx.reshape(1, SEQ, D_MODEL)

    g_ld, g_lr, g_li, g_btr, g_bti = _s5_params_bwd(
        log_dt3, ssm_lam_re, ssm_lam_im, bt_re, bt_im,
        jnp.stack(g_abar_re), jnp.stack(g_abar_im), jnp.stack(g_bbar_re), jnp.stack(g_bbar_im))
    small_g = {n: (jnp.stack(v) if v[0] is not None else None) for n, v in sm.items()}
    small_g["ssm_log_dt"] = g_ld.reshape(DEPTH, 32)
    small_g["ssm_lam_re"] = g_lr
    small_g["ssm_lam_im"] = g_li
    small_g["ssm_b_re"] = g_btr.transpose(0, 2, 3, 1)
    small_g["ssm_b_im"] = g_bti.transpose(0, 2, 3, 1)
    small_g["final_norm_g"] = g_final.reshape(D_MODEL)

    big_part = [jnp.stack(big_g[n]) for n in big_names]
    small_part = _pack_small(small_g)
    got, got_small = _exchange_sibling(big_part, small_part)
    core = lax.axis_index("c").astype(jnp.int32).reshape(1)
    chip_big = [_add_own(f"chip_{n}", core, a, b) for n, a, b in zip(big_names, big_part, got)]
    chip_small = _add_pairs("chip_small", small_part[None], got_small[None])
    slots_big, slots_small = _exchange_chips(chip_big, chip_small)

    res = {}
    for n, slots in zip(big_names, slots_big):
        res[n] = _sum_slots_adamw(n, slots, weights[n], mom_m[n], mom_v[n])
    packed = _sum_slots_adamw("small", slots_small[None], _pack_small(weights)[None], _pack_small(mom_m)[None],
                              _pack_small(mom_v)[None])
    unpacked = [_unpack_small(a[0]) for a in packed]
    for n, _ in SMALL:
        res[n] = tuple(u[n] for u in unpacked)

    outs = [loss, grad_x]
    for i in range(4):
        outs += [res[n][i] for n in order]
    return tuple(outs)
```

```python
import math

import jax
import jax.numpy as jnp
from jax import lax
from jax.experimental import pallas as pl
from jax.experimental.pallas import tpu as pltpu

F32 = jnp.float32
BF16 = jnp.bfloat16

SEQ = 2048
D_MODEL = 1024
N_IN = 4096
WIDTH = 512
N_GROUP = 32
GROUP_W = 16
STATE = 64
N_STATE = N_GROUP * STATE
N_CHUNK = 4
CH_G = N_GROUP // N_CHUNK
CH_W = WIDTH // N_CHUNK
CH_S = N_STATE // N_CHUNK
N_DEV = 8
N_CHIP = 4
POOL_WINDOWS = (2, 4, 8, 16)
POOL_GROUP = 128
EPS = 1e-6
DEPTH = 2

ADAM_LR = 0.001
ADAM_B1 = 0.9
ADAM_B2 = 0.999
ADAM_EPS = 1e-08
ADAM_WD = 0.01
ADAM_STEP = 10

TILE_M = 256
ROW_BLK = 512
VMEM_LIMIT = 48 * 1024 * 1024
VMEM_LIMIT_BIG = 60 * 1024 * 1024
MESH = pl.DeviceIdType.MESH
ANY = pl.BlockSpec(memory_space=pl.ANY)

GELU_C = math.sqrt(2.0 / math.pi)
GELU_A = 0.044715

SDS = jax.ShapeDtypeStruct


def _cp(sem=None, limit=VMEM_LIMIT):
    return pltpu.CompilerParams(dimension_semantics=sem, vmem_limit_bytes=limit)


def _dot(a, b):
    return jnp.dot(a, b, preferred_element_type=F32)


def _dot_nt(a, b):
    return lax.dot_general(a, b, (((1,), (1,)), ((), ())), preferred_element_type=F32)


def _dot_tn(a, b):
    return lax.dot_general(a, b, (((0,), (0,)), ((), ())), preferred_element_type=F32)


def _sig(x):
    return jax.nn.sigmoid(x)


def _rms(x):
    rs = lax.rsqrt(jnp.mean(x * x, axis=-1, keepdims=True) + EPS)
    return rs, x * rs


def _slot(n):
    return 4 * (n % 2) + n // 2


def _const(shape):
    n = len(shape)
    return pl.BlockSpec(shape, lambda *_: (0,) * n)


def _sum4(p):
    return (p[0] + p[1]) + (p[2] + p[3])


def _s5_param_fn(log_dt, lam_re, lam_im, bt_re, bt_im):
    dt = jnp.exp(log_dt)
    mag = jnp.exp(lam_re * dt)
    ang = lam_im * dt
    abar_re = mag * jnp.cos(ang)
    abar_im = mag * jnp.sin(ang)
    num_re = abar_re - 1.0
    num_im = abar_im
    den = lam_re * lam_re + lam_im * lam_im
    coef_re = (num_re * lam_re + num_im * lam_im) / den
    coef_im = (num_im * lam_re - num_re * lam_im) / den
    bbar_re = coef_re[:, None] * bt_re - coef_im[:, None] * bt_im
    bbar_im = coef_re[:, None] * bt_im + coef_im[:, None] * bt_re
    return abar_re, abar_im, bbar_re, bbar_im


def _s5_params(log_dt, lam_re, lam_im, bt_re, bt_im):
    def body(ld, lr, li, br, bi, o_ar, o_ai, o_br, o_bi):
        ar, ai, bbr, bbi = _s5_param_fn(ld[...], lr[...], li[...], br[...], bi[...])
        o_ar[...] = ar
        o_ai[...] = ai
        o_br[...] = bbr
        o_bi[...] = bbi

    return pl.pallas_call(
        body, name="s5_params",
        out_shape=(SDS(lam_re.shape, F32), SDS(lam_re.shape, F32), SDS(bt_re.shape, F32), SDS(bt_re.shape, F32)),
    )(log_dt, lam_re, lam_im, bt_re, bt_im)


def _s5_params_bwd(log_dt, lam_re, lam_im, bt_re, bt_im, g_ar, g_ai, g_br, g_bi):
    def body(ld, lr, li, br, bi, car0, car1, cai0, cai1, cbr0, cbr1, cbi0, cbi1, o_ld, o_lr, o_li, o_br, o_bi):
        _, vjp = jax.vjp(_s5_param_fn, ld[...], lr[...], li[...], br[...], bi[...])
        both = lambda a, b: jnp.stack([a[...], b[...]], axis=0)
        d_ld, d_lr, d_li, d_br, d_bi = vjp((both(car0, car1), both(cai0, cai1), both(cbr0, cbr1), both(cbi0, cbi1)))
        o_ld[...] = d_ld
        o_lr[...] = d_lr
        o_li[...] = d_li
        o_br[...] = d_br
        o_bi[...] = d_bi

    return pl.pallas_call(
        body, name="s5_params_bwd",
        out_shape=(SDS(log_dt.shape, F32), SDS(lam_re.shape, F32), SDS(lam_re.shape, F32),
                   SDS(bt_re.shape, F32), SDS(bt_re.shape, F32)),
    )(log_dt, lam_re, lam_im, bt_re, bt_im, *g_ar, *g_ai, *g_br, *g_bi)


def _norm_proj(layer, x, norm_g, wg_in, b_in):
    def body(x_ref, g_ref, w_ref, b_ref, o_ref):
        _, xn = _rms(x_ref[...])
        h = (xn * g_ref[layer:layer + 1, :]).astype(BF16)
        for k in range(N_DEV):
            cols = slice(k * WIDTH, (k + 1) * WIDTH)
            o_ref[:, cols] = _dot(h, w_ref[k]) + b_ref[layer:layer + 1, cols]

    return pl.pallas_call(
        body, name=f"norm_proj_l{layer}",
        out_shape=SDS((SEQ, N_IN), F32),
        grid=(SEQ // TILE_M,),
        in_specs=[pl.BlockSpec((TILE_M, D_MODEL), lambda i: (i, 0)),
                  _const((DEPTH, D_MODEL)),
                  pl.BlockSpec((N_DEV, None, D_MODEL, WIDTH), lambda i: (0, layer, 0, 0)),
                  _const((DEPTH, N_IN))],
        out_specs=pl.BlockSpec((TILE_M, N_IN), lambda i: (i, 0)),
        compiler_params=_cp(("parallel",)),
    )(x, norm_g, wg_in, b_in)


def _scan_tile_rows():
    return lax.broadcasted_iota(jnp.int32, (8, CH_S), 0)


def _s5_param_specs(layer):
    bt = lambda: pl.BlockSpec((None, GROUP_W, CH_G, STATE), lambda j: (layer, 0, j, 0))
    cc = lambda: pl.BlockSpec((None, CH_G, GROUP_W, STATE), lambda j: (layer, j, 0, 0))
    ab = lambda: pl.BlockSpec((None, CH_G, STATE), lambda j: (layer, j, 0))
    return [bt(), bt(), cc(), cc(), ab(), ab(), pl.BlockSpec((DEPTH, CH_W), lambda j: (0, j))]


def _s5_param_scratch():
    return [pltpu.VMEM((CH_W, CH_S), BF16)] * 4 + [pltpu.VMEM((1, CH_S), F32)] * 2


def _s5_fill(btre_ref, btim_ref, cre_ref, cim_ref, are_ref, aim_ref, bdre, bdim, ctre, ctim, arow, airow):
    for m in (bdre, bdim, ctre, ctim):
        m[...] = jnp.zeros_like(m)
    for g in range(CH_G):
        rows = slice(g * GROUP_W, (g + 1) * GROUP_W)
        cols = slice(g * STATE, (g + 1) * STATE)
        bdre[rows, cols] = btre_ref[:, g, :].astype(BF16)
        bdim[rows, cols] = btim_ref[:, g, :].astype(BF16)
        ctre[rows, cols] = cre_ref[g].astype(BF16)
        ctim[rows, cols] = cim_ref[g].astype(BF16)
        arow[:, cols] = are_ref[g:g + 1, :]
        airow[:, cols] = aim_ref[g:g + 1, :]


def _s5_fwd(layer, proj, bbt_re, bbt_im, c_re, c_im, abar_re, abar_im, d_skip):
    def body(u_ref, btre_ref, btim_ref, cre_ref, cim_ref, are_ref, aim_ref, d_ref,
             sre_ref, sim_ref, y_ref, bdre, bdim, ctre, ctim, arow, airow):
        _s5_fill(btre_ref, btim_ref, cre_ref, cim_ref, are_ref, aim_ref, bdre, bdim, ctre, ctim, arow, airow)
        for rb in range(SEQ // ROW_BLK):
            rows = pl.ds(rb * ROW_BLK, ROW_BLK)
            ub = u_ref[rows, :].astype(BF16)
            sre_ref[rows, :] = _dot(ub, bdre[...])
            sim_ref[rows, :] = _dot(ub, bdim[...])
        ar = arow[...]
        ai = airow[...]
        row_id = _scan_tile_rows()

        def step(i, carry):
            sr, si = carry
            base = pl.multiple_of(i * 8, 8)
            tr = sre_ref[pl.ds(base, 8), :]
            ti = sim_ref[pl.ds(base, 8), :]
            outr, outi = tr, ti
            for r in range(8):
                nsr = ar * sr - ai * si + tr[r:r + 1, :]
                nsi = ar * si + ai * sr + ti[r:r + 1, :]
                outr = jnp.where(row_id == r, nsr, outr)
                outi = jnp.where(row_id == r, nsi, outi)
                sr, si = nsr, nsi
            sre_ref[pl.ds(base, 8), :] = outr
            sim_ref[pl.ds(base, 8), :] = outi
            return sr, si

        zero = jnp.zeros((1, CH_S), F32)
        lax.fori_loop(0, SEQ // 8, step, (zero, zero))
        d = d_ref[layer:layer + 1, :]
        for rb in range(SEQ // ROW_BLK):
            rows = pl.ds(rb * ROW_BLK, ROW_BLK)
            y = (_dot_nt(sre_ref[rows, :].astype(BF16), ctre[...])
                 - _dot_nt(sim_ref[rows, :].astype(BF16), ctim[...]))
            y_ref[rows, :] = y + d * u_ref[rows, :]

    return pl.pallas_call(
        body, name=f"s5_fwd_l{layer}",
        out_shape=(SDS((SEQ, N_STATE), F32), SDS((SEQ, N_STATE), F32), SDS((SEQ, WIDTH), F32)),
        grid=(N_CHUNK,),
        in_specs=[pl.BlockSpec((SEQ, CH_W), lambda j: (0, j))] + _s5_param_specs(layer),
        out_specs=(pl.BlockSpec((SEQ, CH_S), lambda j: (0, j)),
                   pl.BlockSpec((SEQ, CH_S), lambda j: (0, j)),
                   pl.BlockSpec((SEQ, CH_W), lambda j: (0, j))),
        scratch_shapes=_s5_param_scratch(),
        compiler_params=_cp(("parallel",)),
    )(proj, bbt_re, bbt_im, c_re, c_im, abar_re, abar_im, d_skip)


def _pool_counts(win):
    t = lax.broadcasted_iota(jnp.int32, (SEQ, POOL_GROUP), 0)
    return t, jnp.minimum(t + 1, win).astype(F32)


def _pool_fwd(layer, proj):
    def body(u_ref, o_ref):
        for gi, win in enumerate(POOL_WINDOWS):
            cols = slice(gi * POOL_GROUP, (gi + 1) * POOL_GROUP)
            u = u_ref[:, cols]
            t, count = _pool_counts(win)
            acc = u
            k = 1
            while k < win:
                acc = acc + jnp.where(t >= k, pltpu.roll(acc, k, 0), 0.0)
                k *= 2
            o_ref[:, cols] = acc / count - u

    return pl.pallas_call(
        body, name=f"pool_fwd_l{layer}",
        out_shape=SDS((SEQ, WIDTH), F32),
        grid=(1,),
        in_specs=[pl.BlockSpec((SEQ, WIDTH), lambda i: (0, 2))],
        out_specs=pl.BlockSpec((SEQ, WIDTH), lambda i: (0, 0)),
        compiler_params=_cp(("arbitrary",)),
    )(proj)


def _gelu_parts(y0):
    t = jnp.tanh(GELU_C * (y0 + GELU_A * (y0 * y0 * y0)))
    return t, 0.5 * y0 * (1.0 + t)


def _mix_forward(layer, p_ref, y0_ref, pooled_ref, wglu_ref, bglu_ref, pw_ref, scale_ref, wa_ref, wb_ref):
    za = p_ref[:, WIDTH:2 * WIDTH]
    zb = p_ref[:, 3 * WIDTH:4 * WIDTH]
    ga = p_ref[:, 4 * WIDTH:4 * WIDTH + D_MODEL]
    gb = p_ref[:, 4 * WIDTH + D_MODEL:]
    y0 = y0_ref[...]
    t, y1 = _gelu_parts(y0)
    y1b = y1.astype(BF16)
    q = _dot(y1b, wglu_ref[...].reshape(WIDTH, WIDTH)) + bglu_ref[layer:layer + 1, :]
    sq = _sig(q)
    y2 = y1 * sq
    sza = _sig(za)
    silu_za = za * sza
    ya = y2 * silu_za
    pooled = pooled_ref[...]
    mixed = jnp.concatenate(
        [_dot(pooled[:, g * POOL_GROUP:(g + 1) * POOL_GROUP].astype(BF16), pw_ref[g].astype(BF16))
         for g in range(len(POOL_WINDOWS))], axis=1)
    szb = _sig(zb)
    silu_zb = zb * szb
    scale = scale_ref[layer:layer + 1, :]
    ms = mixed * scale
    yb = ms * silu_zb
    yab = ya.astype(BF16)
    ybb = yb.astype(BF16)
    ma = jnp.concatenate([_dot(yab, wa_ref[k]) for k in range(N_DEV)], axis=1)
    mb = jnp.concatenate([_dot(ybb, wb_ref[k]) for k in range(N_DEV)], axis=1)
    sga = _sig(ga)
    sgb = _sig(gb)
    merged = sga * ma + sgb * mb
    return dict(za=za, zb=zb, y0=y0, t=t, y1=y1, y1b=y1b, sq=sq, y2=y2, sza=sza, silu_za=silu_za,
                pooled=pooled, mixed=mixed, szb=szb, silu_zb=silu_zb, scale=scale, ms=ms, yab=yab, ybb=ybb,
                ma=ma, mb=mb, sga=sga, sgb=sgb, merged=merged)


def _mix_weight_specs(layer):
    return [pl.BlockSpec((N_DEV, None, WIDTH // N_DEV, WIDTH), lambda i: (0, layer, 0, 0)),
            _const((DEPTH, WIDTH)),
            pl.BlockSpec((None, 4, POOL_GROUP, POOL_GROUP), lambda i: (layer, 0, 0, 0)),
            _const((DEPTH, WIDTH)),
            pl.BlockSpec((N_DEV, None, WIDTH, D_MODEL // N_DEV), lambda i: (0, layer, 0, 0)),
            pl.BlockSpec((N_DEV, None, WIDTH, D_MODEL // N_DEV), lambda i: (0, layer, 0, 0)),
            pl.BlockSpec((N_DEV, None, D_MODEL // N_DEV, D_MODEL), lambda i: (0, layer, 0, 0))]


def _mix_fwd(layer, x, proj, y0, pooled, wg_glu, b_glu, pool_w, pool_scale, wg_a, wg_b, wg_out):
    def body(x_ref, p_ref, y0_ref, pooled_ref, wglu_ref, bglu_ref, pw_ref, scale_ref, wa_ref, wb_ref,
             wout_ref, o_ref):
        f = _mix_forward(layer, p_ref, y0_ref, pooled_ref, wglu_ref, bglu_ref, pw_ref, scale_ref, wa_ref, wb_ref)
        wout = wout_ref[...].reshape(D_MODEL, D_MODEL)
        o_ref[...] = x_ref[...] + _dot(f["merged"].astype(BF16), wout)

    return pl.pallas_call(
        body, name=f"mix_fwd_l{layer}",
        out_shape=SDS((SEQ, D_MODEL), F32),
        grid=(SEQ // TILE_M,),
        in_specs=[pl.BlockSpec((TILE_M, D_MODEL), lambda i: (i, 0)),
                  pl.BlockSpec((TILE_M, N_IN), lambda i: (i, 0)),
                  pl.BlockSpec((TILE_M, WIDTH), lambda i: (i, 0)),
                  pl.BlockSpec((TILE_M, WIDTH), lambda i: (i, 0))] + _mix_weight_specs(layer),
        out_specs=pl.BlockSpec((TILE_M, D_MODEL), lambda i: (i, 0)),
        compiler_params=_cp(("parallel",)),
    )(x, proj, y0, pooled, wg_glu, b_glu, pool_w, pool_scale, wg_a, wg_b, wg_out)


def _loss_head(x, target, final_g):
    def body(x_ref, t_ref, g_ref, dx_ref, loss_ref, gg_ref):
        @pl.when(pl.program_id(0) == 0)
        def _():
            loss_ref[...] = jnp.zeros_like(loss_ref)
            gg_ref[...] = jnp.zeros_like(gg_ref)

        g = g_ref[...]
        rs, xn = _rms(x_ref[...])
        err = xn * g - t_ref[...]
        loss_ref[...] += 0.5 * jnp.sum(jnp.mean(err * err, axis=-1, keepdims=True), axis=0, keepdims=True)
        dy = err * (1.0 / D_MODEL)
        gg_ref[...] += jnp.sum(dy * xn, axis=0, keepdims=True)
        dxn = dy * g
        dx_ref[...] = rs * (dxn - xn * jnp.mean(dxn * xn, axis=-1, keepdims=True))

    return pl.pallas_call(
        body, name="loss_head",
        out_shape=(SDS((SEQ, D_MODEL), F32), SDS((1, 1), F32), SDS((1, D_MODEL), F32)),
        grid=(SEQ // TILE_M,),
        in_specs=[pl.BlockSpec((TILE_M, D_MODEL), lambda i: (i, 0)),
                  pl.BlockSpec((TILE_M, D_MODEL), lambda i: (i, 0)),
                  _const((1, D_MODEL))],
        out_specs=(pl.BlockSpec((TILE_M, D_MODEL), lambda i: (i, 0)), _const((1, 1)), _const((1, D_MODEL))),
        compiler_params=_cp(("arbitrary",)),
    )(x, target, final_g)


def _big_shapes():
    return dict(w_out=(DEPTH, N_DEV, D_MODEL // N_DEV, D_MODEL), w_branch_a=(DEPTH, N_DEV, WIDTH, D_MODEL // N_DEV),
                w_branch_b=(DEPTH, N_DEV, WIDTH, D_MODEL // N_DEV), ssm_w_glu=(DEPTH, N_DEV, WIDTH // N_DEV, WIDTH),
                w_in=(DEPTH, N_DEV, D_MODEL, WIDTH))


def _mix_bwd(layer, dx_next, proj, y0, pooled, wg_glu, b_glu, pool_w, pool_scale, wg_a, wg_b, wg_out, prev):
    n_k = N_DEV
    n_prev = 0 if prev is None else len(prev)

    def body(*refs):
        (dx_ref, p_ref, y0_ref, pooled_ref, wglu_ref, bglu_ref, pw_ref, scale_ref, wa_ref, wb_ref,
         wout_ref) = refs[:11]
        (dproj_ref, dy0_ref, dpooled_ref, gwout_ref, gwa_ref, gwb_ref, gwglu_ref, gpw_ref,
         gscale_ref, gbglu_ref) = refs[11 + n_prev:]

        @pl.when(pl.program_id(0) == 0)
        def _():
            for r in (gwout_ref, gwa_ref, gwb_ref, gwglu_ref, gpw_ref, gscale_ref, gbglu_ref):
                r[...] = jnp.zeros_like(r)

        f = _mix_forward(layer, p_ref, y0_ref, pooled_ref, wglu_ref, bglu_ref, pw_ref, scale_ref, wa_ref, wb_ref)
        wglu = wglu_ref[...].reshape(WIDTH, WIDTH)
        wout = wout_ref[...].reshape(D_MODEL, D_MODEL)
        blk = D_MODEL // n_k
        dxb = dx_ref[...].astype(BF16)
        dmerged = _dot_nt(dxb, wout)
        gwout = _dot_tn(f["merged"].astype(BF16), dxb)
        for k in range(n_k):
            gwout_ref[_slot(k)] += gwout[k * blk:(k + 1) * blk, :]
        dma = dmerged * f["sga"]
        dmb = dmerged * f["sgb"]
        dga = dmerged * f["ma"] * f["sga"] * (1.0 - f["sga"])
        dgb = dmerged * f["mb"] * f["sgb"] * (1.0 - f["sgb"])
        dmab = dma.astype(BF16)
        dmbb = dmb.astype(BF16)
        dya = jnp.zeros((TILE_M, WIDTH), F32)
        dyb = jnp.zeros((TILE_M, WIDTH), F32)
        for k in range(n_k):
            da_k = dmab[:, k * blk:(k + 1) * blk]
            db_k = dmbb[:, k * blk:(k + 1) * blk]
            dya = dya + _dot_nt(da_k, wa_ref[k])
            dyb = dyb + _dot_nt(db_k, wb_ref[k])
            gwa_ref[_slot(k)] += _dot_tn(f["yab"], da_k)
            gwb_ref[_slot(k)] += _dot_tn(f["ybb"], db_k)
        zb, szb = f["zb"], f["szb"]
        dzb = dyb * f["ms"] * (szb * (1.0 + zb * (1.0 - szb)))
        dms = dyb * f["silu_zb"]
        gscale_ref[...] += jnp.sum(dms * f["mixed"], axis=0, keepdims=True)
        dmixed = (dms * f["scale"]).astype(BF16)
        pooled = f["pooled"]
        for g in range(len(POOL_WINDOWS)):
            cols = slice(g * POOL_GROUP, (g + 1) * POOL_GROUP)
            dpooled_ref[:, cols] = _dot_nt(dmixed[:, cols], pw_ref[g].astype(BF16))
            gpw_ref[g] += _dot_tn(pooled[:, cols].astype(BF16), dmixed[:, cols])
        za, sza = f["za"], f["sza"]
        dza = dya * f["y2"] * (sza * (1.0 + za * (1.0 - sza)))
        dy2 = dya * f["silu_za"]
        sq = f["sq"]
        dq = dy2 * f["y1"] * sq * (1.0 - sq)
        dqb = dq.astype(BF16)
        dy1 = dy2 * sq + _dot_nt(dqb, wglu)
        gwglu = _dot_tn(f["y1b"], dqb)
        rblk = WIDTH // n_k
        for k in range(n_k):
            gwglu_ref[_slot(k)] += gwglu[k * rblk:(k + 1) * rblk, :]
        gbglu_ref[...] += jnp.sum(dq, axis=0, keepdims=True)
        y0, t = f["y0"], f["t"]
        dgelu = 0.5 * (1.0 + t) + 0.5 * y0 * (1.0 - t * t) * (GELU_C * (1.0 + 3.0 * GELU_A * y0 * y0))
        dy0_ref[...] = dy1 * dgelu
        zeros = jnp.zeros((TILE_M, WIDTH), BF16)
        dproj_ref[:, 0:WIDTH] = zeros
        dproj_ref[:, WIDTH:2 * WIDTH] = dza.astype(BF16)
        dproj_ref[:, 2 * WIDTH:3 * WIDTH] = zeros
        dproj_ref[:, 3 * WIDTH:4 * WIDTH] = dzb.astype(BF16)
        dproj_ref[:, 4 * WIDTH:4 * WIDTH + D_MODEL] = dga.astype(BF16)
        dproj_ref[:, 4 * WIDTH + D_MODEL:] = dgb.astype(BF16)

    tile = lambda w: pl.BlockSpec((TILE_M, w), lambda i: (i, 0))
    shapes = _big_shapes()
    big = ["w_out", "w_branch_a", "w_branch_b", "ssm_w_glu"]
    slab = lambda n: pl.BlockSpec((None,) + shapes[n][1:], lambda i: (layer, 0, 0, 0))
    args = [dx_next, proj, y0, pooled, wg_glu, b_glu, pool_w, pool_scale, wg_a, wg_b, wg_out]
    return pl.pallas_call(
        body, name=f"mix_bwd_l{layer}",
        out_shape=(SDS((SEQ, N_IN), BF16), SDS((SEQ, WIDTH), F32), SDS((SEQ, WIDTH), F32))
        + tuple(SDS(shapes[n], F32) for n in big)
        + (SDS((4, POOL_GROUP, POOL_GROUP), F32), SDS((1, WIDTH), F32), SDS((1, WIDTH), F32)),
        grid=(SEQ // TILE_M,),
        in_specs=[tile(D_MODEL), tile(N_IN), tile(WIDTH), tile(WIDTH)] + _mix_weight_specs(layer) + [ANY] * n_prev,
        out_specs=(tile(N_IN), tile(WIDTH), tile(WIDTH)) + tuple(slab(n) for n in big)
        + (_const((4, POOL_GROUP, POOL_GROUP)), _const((1, WIDTH)), _const((1, WIDTH))),
        input_output_aliases={len(args) + i: 3 + i for i in range(n_prev)},
        compiler_params=_cp(("arbitrary",), VMEM_LIMIT_BIG),
    )(*args, *(prev or ()))


def _pool_bwd(layer, dpooled, dproj):
    def body(dp_ref, _, o_ref):
        for gi, win in enumerate(POOL_WINDOWS):
            cols = slice(gi * POOL_GROUP, (gi + 1) * POOL_GROUP)
            dp = dp_ref[:, cols]
            t, count = _pool_counts(win)
            e = dp / count
            acc = e
            k = 1
            while k < win:
                acc = acc + jnp.where(t < SEQ - k, pltpu.roll(acc, SEQ - k, 0), 0.0)
                k *= 2
            o_ref[:, cols] = (acc - dp).astype(BF16)

    return pl.pallas_call(
        body, name=f"pool_bwd_l{layer}",
        out_shape=SDS((SEQ, N_IN), BF16),
        grid=(1,),
        in_specs=[pl.BlockSpec((SEQ, WIDTH), lambda i: (0, 0)), ANY],
        out_specs=pl.BlockSpec((SEQ, WIDTH), lambda i: (0, 2)),
        input_output_aliases={1: 0},
        compiler_params=_cp(("arbitrary",)),
    )(dpooled, dproj)


def _s5_bwd(layer, dy0, proj, s_re, s_im, bbt_re, bbt_im, c_re, c_im, abar_re, abar_im, d_skip, dproj):
    def body(dy_ref, u_ref, sre_ref, sim_ref, btre_ref, btim_ref, cre_ref, cim_ref, are_ref, aim_ref,
             d_ref, _, du_ref, gbre_ref, gbim_ref, gcre_ref, gcim_ref, gare_ref, gaim_ref, gd_ref,
             lre_ref, lim_ref, bdre, bdim, ctre, ctim, arow, airow):
        _s5_fill(btre_ref, btim_ref, cre_ref, cim_ref, are_ref, aim_ref, bdre, bdim, ctre, ctim, arow, airow)
        n_rb = SEQ // ROW_BLK
        gcre = jnp.zeros((CH_W, CH_S), F32)
        gcim = jnp.zeros((CH_W, CH_S), F32)
        for rb in range(n_rb):
            rows = pl.ds(rb * ROW_BLK, ROW_BLK)
            dyb = dy_ref[rows, :].astype(BF16)
            lre_ref[rows, :] = _dot(dyb, ctre[...])
            lim_ref[rows, :] = -_dot(dyb, ctim[...])
            gcre = gcre + _dot_tn(dyb, sre_ref[rows, :].astype(BF16))
            gcim = gcim - _dot_tn(dyb, sim_ref[rows, :].astype(BF16))
        ar = arow[...]
        ai = airow[...]
        row_id = _scan_tile_rows()

        def step(n, carry):
            lr, li = carry
            base = pl.multiple_of((SEQ // 8 - 1 - n) * 8, 8)
            tr = lre_ref[pl.ds(base, 8), :]
            ti = lim_ref[pl.ds(base, 8), :]
            outr, outi = tr, ti
            for r in range(7, -1, -1):
                nlr = ar * lr + ai * li + tr[r:r + 1, :]
                nli = ar * li - ai * lr + ti[r:r + 1, :]
                outr = jnp.where(row_id == r, nlr, outr)
                outi = jnp.where(row_id == r, nli, outi)
                lr, li = nlr, nli
            lre_ref[pl.ds(base, 8), :] = outr
            lim_ref[pl.ds(base, 8), :] = outi
            return lr, li

        zero = jnp.zeros((1, CH_S), F32)
        lax.fori_loop(0, SEQ // 8, step, (zero, zero))

        gare = jnp.zeros((1, CH_S), F32)
        gaim = jnp.zeros((1, CH_S), F32)
        gbre = jnp.zeros((CH_W, CH_S), F32)
        gbim = jnp.zeros((CH_W, CH_S), F32)
        gd = jnp.zeros((1, CH_W), F32)
        d = d_ref[layer:layer + 1, :]
        first = lax.broadcasted_iota(jnp.int32, (ROW_BLK, CH_S), 0) == 0
        for rb in range(n_rb):
            rows = pl.ds(rb * ROW_BLK, ROW_BLK)
            lr = lre_ref[rows, :]
            li = lim_ref[rows, :]
            if rb == 0:
                prev_r = jnp.zeros((1, CH_S), F32)
                prev_i = jnp.zeros((1, CH_S), F32)
            else:
                prev_r = sre_ref[pl.ds(rb * ROW_BLK - 1, 1), :]
                prev_i = sim_ref[pl.ds(rb * ROW_BLK - 1, 1), :]
            spr = jnp.where(first, prev_r, pltpu.roll(sre_ref[rows, :], 1, 0))
            spi = jnp.where(first, prev_i, pltpu.roll(sim_ref[rows, :], 1, 0))
            gare = gare + jnp.sum(lr * spr + li * spi, axis=0, keepdims=True)
            gaim = gaim + jnp.sum(li * spr - lr * spi, axis=0, keepdims=True)
            lrb = lr.astype(BF16)
            lib = li.astype(BF16)
            u = u_ref[rows, :]
            ub = u.astype(BF16)
            dy = dy_ref[rows, :]
            du = dy * d + _dot_nt(lrb, bdre[...]) + _dot_nt(lib, bdim[...])
            du_ref[rows, :] = du.astype(BF16)
            gbre = gbre + _dot_tn(ub, lrb)
            gbim = gbim + _dot_tn(ub, lib)
            gd = gd + jnp.sum(dy * u, axis=0, keepdims=True)
        gd_ref[...] = gd
        for g in range(CH_G):
            rows = slice(g * GROUP_W, (g + 1) * GROUP_W)
            cols = slice(g * STATE, (g + 1) * STATE)
            gcre_ref[g] = gcre[rows, cols]
            gcim_ref[g] = gcim[rows, cols]
            gbre_ref[:, g, :] = gbre[rows, cols]
            gbim_ref[:, g, :] = gbim[rows, cols]
            gare_ref[g:g + 1, :] = gare[:, cols]
            gaim_ref[g:g + 1, :] = gaim[:, cols]

    chunk_w = lambda: pl.BlockSpec((SEQ, CH_W), lambda j: (0, j))
    chunk_s = lambda: pl.BlockSpec((SEQ, CH_S), lambda j: (0, j))
    gbt = lambda: pl.BlockSpec((GROUP_W, CH_G, STATE), lambda j: (0, j, 0))
    gcc = lambda: pl.BlockSpec((CH_G, GROUP_W, STATE), lambda j: (j, 0, 0))
    gab = lambda: pl.BlockSpec((CH_G, STATE), lambda j: (j, 0))
    return pl.pallas_call(
        body, name=f"s5_bwd_l{layer}",
        out_shape=(SDS((SEQ, N_IN), BF16),
                   SDS((GROUP_W, N_GROUP, STATE), F32), SDS((GROUP_W, N_GROUP, STATE), F32),
                   SDS((N_GROUP, GROUP_W, STATE), F32), SDS((N_GROUP, GROUP_W, STATE), F32),
                   SDS((N_GROUP, STATE), F32), SDS((N_GROUP, STATE), F32), SDS((1, WIDTH), F32)),
        grid=(N_CHUNK,),
        in_specs=[chunk_w(), chunk_w(), chunk_s(), chunk_s()] + _s5_param_specs(layer) + [ANY],
        out_specs=(chunk_w(), gbt(), gbt(), gcc(), gcc(), gab(), gab(),
                   pl.BlockSpec((1, CH_W), lambda j: (0, j))),
        scratch_shapes=[pltpu.VMEM((SEQ, CH_S), F32), pltpu.VMEM((SEQ, CH_S), F32)] + _s5_param_scratch(),
        input_output_aliases={11: 0},
        compiler_params=_cp(("arbitrary",), VMEM_LIMIT_BIG),
    )(dy0, proj, s_re, s_im, bbt_re, bbt_im, c_re, c_im, abar_re, abar_im, d_skip, dproj)


def _proj_wgrad(layer, x, norm_g, dproj, prev):
    tm = 512
    n_prev = 0 if prev is None else 1

    def body(*refs):
        x_ref, g_ref, dp_ref = refs[:3]
        gw_ref, gb_ref = refs[3 + n_prev:]

        @pl.when(pl.program_id(1) == 0)
        def _():
            gw_ref[...] = jnp.zeros_like(gw_ref)
            gb_ref[...] = jnp.zeros_like(gb_ref)

        _, xn = _rms(x_ref[...])
        h = (xn * g_ref[layer:layer + 1, :]).astype(BF16)
        dp = dp_ref[...]
        gw_ref[...] += _dot_tn(h, dp)
        gb_ref[...] += jnp.sum(dp.astype(F32), axis=0, keepdims=True)

    return pl.pallas_call(
        body, name=f"proj_wgrad_l{layer}",
        out_shape=(SDS(_big_shapes()["w_in"], F32), SDS((1, N_IN), F32)),
        grid=(N_DEV, SEQ // tm),
        in_specs=[pl.BlockSpec((tm, D_MODEL), lambda n, t: (t, 0)),
                  _const((DEPTH, D_MODEL)),
                  pl.BlockSpec((tm, WIDTH), lambda n, t: (t, n))] + [ANY] * n_prev,
        out_specs=(pl.BlockSpec((None, None, D_MODEL, WIDTH), lambda n, t: (layer, _slot(n), 0, 0)),
                   pl.BlockSpec((1, WIDTH), lambda n, t: (0, n))),
        input_output_aliases={3: 0} if n_prev else {},
        compiler_params=_cp(("parallel", "arbitrary")),
    )(x, norm_g, dproj, *([prev] if n_prev else []))


def _proj_dgrad(layer, dx_next, x, norm_g, dproj, wg_in):
    def body(dxn_ref, x_ref, g_ref, dp_ref, w_ref, dx_ref, gg_ref):
        @pl.when(pl.program_id(0) == 0)
        def _():
            gg_ref[...] = jnp.zeros_like(gg_ref)

        dh = jnp.zeros((TILE_M, D_MODEL), F32)
        for k in range(N_DEV):
            dh = dh + _dot_nt(dp_ref[:, k * WIDTH:(k + 1) * WIDTH], w_ref[k])
        rs, xn = _rms(x_ref[...])
        gg_ref[...] += jnp.sum(dh * xn, axis=0, keepdims=True)
        dxn = dh * g_ref[layer:layer + 1, :]
        dx_ref[...] = dxn_ref[...] + rs * (dxn - xn * jnp.mean(dxn * xn, axis=-1, keepdims=True))

    return pl.pallas_call(
        body, name=f"proj_dgrad_l{layer}",
        out_shape=(SDS((SEQ, D_MODEL), F32), SDS((1, D_MODEL), F32)),
        grid=(SEQ // TILE_M,),
        in_specs=[pl.BlockSpec((TILE_M, D_MODEL), lambda i: (i, 0)),
                  pl.BlockSpec((TILE_M, D_MODEL), lambda i: (i, 0)),
                  _const((DEPTH, D_MODEL)),
                  pl.BlockSpec((TILE_M, N_IN), lambda i: (i, 0)),
                  pl.BlockSpec((N_DEV, None, D_MODEL, WIDTH), lambda i: (0, layer, 0, 0))],
        out_specs=(pl.BlockSpec((TILE_M, D_MODEL), lambda i: (i, 0)), _const((1, D_MODEL))),
        compiler_params=_cp(("arbitrary",)),
    )(dx_next, x, norm_g, dproj, wg_in)


def _my_place():
    return lax.axis_index("x"), lax.axis_index("y"), lax.axis_index("c")


def _gather_weights(shards):
    n = len(shards)

    def body(*refs):
        src = refs[:n]
        out = refs[n:2 * n]
        send_sems, recv_sems, local_sems = refs[2 * n:]
        x, y, c = _my_place()
        me, sibling = (x, y, c), (x, y, 1 - c)
        chips = [(1 - x, y), (x, 1 - y), (1 - x, 1 - y)]

        def rows(t, place):
            px, py, pc = place
            return out[t].at[pl.ds(4 * px + 2 * py + pc, 1)]

        def copy(t, k, block, to, from_src=False):
            return pltpu.make_async_remote_copy(
                src_ref=src[t] if from_src else rows(t, block), dst_ref=rows(t, block),
                send_sem=send_sems.at[7 * t + k], recv_sem=recv_sems.at[7 * t + k], device_id=to,
                device_id_type=MESH)

        mine = [pltpu.make_async_copy(src[t], rows(t, me), local_sems.at[t]) for t in range(n)]
        for cp in mine:
            cp.start()
        first = []
        for t in range(n):
            first.append(copy(t, 0, me, sibling, from_src=True))
            first += [copy(t, 1 + j, me, (*chip, c), from_src=True) for j, chip in enumerate(chips)]
        for cp in first:
            cp.start()
        passed = []
        for t in range(n):
            for j, chip in enumerate(chips):
                copy(t, 1 + j, (*chip, c), me).wait_recv()
                fwd = copy(t, 4 + j, (*chip, c), sibling)
                fwd.start()
                passed.append(fwd)
        for t in range(n):
            copy(t, 0, sibling, me).wait_recv()
            for j, chip in enumerate(chips):
                copy(t, 4 + j, (*chip, 1 - c), me).wait_recv()
        for cp in first + passed:
            cp.wait_send()
        for cp in mine:
            cp.wait()

    return pl.pallas_call(
        body, name="gather_weights",
        out_shape=tuple(SDS((N_DEV,) + a.shape[1:], a.dtype) for a in shards),
        in_specs=[ANY] * n, out_specs=tuple([ANY] * n),
        scratch_shapes=[pltpu.SemaphoreType.DMA((7 * n,)), pltpu.SemaphoreType.DMA((7 * n,)),
                        pltpu.SemaphoreType.DMA((n,))],
    )(*shards)


def _exchange_sibling(big, small):
    n = len(big)
    n_small = len(small)
    n_copies = n * DEPTH * 4 + n_small

    def body(*refs):
        g = refs[:n]
        sg = refs[n:n + n_small]
        got = refs[n + n_small:2 * n + n_small]
        got_small = refs[2 * n + n_small:2 * (n + n_small)]
        send_sems, recv_sems = refs[2 * (n + n_small):]
        x, y, c = _my_place()
        sibling = (x, y, 1 - c)
        pairs = []
        for t in range(n):
            for l in range(DEPTH):
                for s in range(4):
                    pairs.append((g[t].at[l, pl.ds(4 * (1 - c) + s, 1)], got[t].at[l, pl.ds(s, 1)]))
        pairs += list(zip(sg, got_small))
        copies = [pltpu.make_async_remote_copy(
            src_ref=src, dst_ref=dst, send_sem=send_sems.at[k], recv_sem=recv_sems.at[k],
            device_id=sibling, device_id_type=MESH) for k, (src, dst) in enumerate(pairs)]
        for cp in copies:
            cp.start()
        for cp in copies:
            cp.wait()

    half = lambda a: SDS((a.shape[0], 4) + a.shape[2:], a.dtype)
    out_shape = tuple([half(a) for a in big] + [SDS(a.shape, a.dtype) for a in small])
    res = pl.pallas_call(
        body, name="exchange_sibling", out_shape=out_shape,
        in_specs=[ANY] * (n + n_small), out_specs=tuple([ANY] * (n + n_small)),
        scratch_shapes=[pltpu.SemaphoreType.DMA((n_copies,)), pltpu.SemaphoreType.DMA((n_copies,))],
    )(*big, *small)
    return res[:n], res[n:]


def _exchange_chips(big, small):
    n = len(big)
    n_small = len(small)

    def body(*refs):
        cp_in = refs[:n]
        sm_in = refs[n:n + n_small]
        slots = refs[n + n_small:2 * n + n_small]
        sm_slots = refs[2 * n + n_small:2 * (n + n_small)]
        send_sems, recv_sems, local_sems = refs[2 * (n + n_small):]
        x, y, c = _my_place()
        my_chip = 2 * x + y
        chips = [(1 - x, y), (x, 1 - y), (1 - x, 1 - y)]

        def remote(k, src, dst, chip):
            return pltpu.make_async_remote_copy(
                src_ref=src, dst_ref=dst, send_sem=send_sems.at[k], recv_sem=recv_sems.at[k],
                device_id=(*chip, c), device_id_type=MESH)

        sends, waits, local = [], [], []
        k = 0
        for chip in chips:
            to = 2 * chip[0] + chip[1]
            for t in range(n):
                for l in range(DEPTH):
                    src = cp_in[t].at[l, pl.ds(to, 1)]
                    sends.append(remote(k, src, slots[t].at[l, pl.ds(my_chip, 1)], chip))
                    waits.append(remote(k, src, slots[t].at[l, pl.ds(to, 1)], chip))
                    k += 1
            for t in range(n_small):
                sends.append(remote(k, sm_in[t], sm_slots[t].at[my_chip], chip))
                waits.append(remote(k, sm_in[t], sm_slots[t].at[to], chip))
                k += 1
        for t in range(n):
            local.append(pltpu.make_async_copy(
                cp_in[t].at[:, pl.ds(my_chip, 1)], slots[t].at[:, pl.ds(my_chip, 1)], local_sems.at[t]))
        for t in range(n_small):
            local.append(pltpu.make_async_copy(sm_in[t], sm_slots[t].at[my_chip], local_sems.at[n + t]))
        for cp in sends + local:
            cp.start()
        for cp in waits + local:
            cp.wait()

    n_copies = 3 * (n * DEPTH + n_small)
    out_shape = tuple([SDS(a.shape, a.dtype) for a in big] + [SDS((N_CHIP,) + a.shape, a.dtype) for a in small])
    res = pl.pallas_call(
        body, name="exchange_chips", out_shape=out_shape,
        in_specs=[ANY] * (n + n_small), out_specs=tuple([ANY] * (n + n_small)),
        scratch_shapes=[pltpu.SemaphoreType.DMA((n_copies,)), pltpu.SemaphoreType.DMA((n_copies,)),
                        pltpu.SemaphoreType.DMA((n + n_small,))],
    )(*big, *small)
    return res[:n], res[n:]


def _row_block(rows):
    return rows if rows <= 256 else 256


def _add_own(tag, core, g, got):
    _, _, r, c = got.shape
    rb = _row_block(r)

    def body(core_ref, a_ref, b_ref, o_ref):
        o_ref[...] = (a_ref[...] + b_ref[...]).astype(o_ref.dtype)

    blk = (None, None, rb, c)
    return pl.pallas_call(
        body, name=f"add_{tag}", out_shape=SDS(got.shape, BF16),
        grid_spec=pltpu.PrefetchScalarGridSpec(
            num_scalar_prefetch=1, grid=(DEPTH, 4, r // rb),
            in_specs=[pl.BlockSpec(blk, lambda l, s, j, core: (l, 4 * core[0] + s, j, 0)),
                      pl.BlockSpec(blk, lambda l, s, j, core: (l, s, j, 0))],
            out_specs=pl.BlockSpec(blk, lambda l, s, j, core: (l, s, j, 0))),
        compiler_params=_cp(("parallel", "parallel", "parallel")),
    )(core, g, got)


def _add_lists(tag, own, got, grid=None, specs=None):
    n = len(own)

    def body(*refs):
        for a, b, o in zip(refs[:n], refs[n:2 * n], refs[2 * n:]):
            o[...] = a[...] + b[...]

    kw = {}
    if grid is not None:
        kw = dict(grid=grid, in_specs=list(specs) * 2, out_specs=tuple(specs),
                  compiler_params=_cp(("parallel",) * len(grid)))
    return pl.pallas_call(
        body, name=f"add_{tag}", out_shape=tuple(SDS(a.shape, a.dtype) for a in own), **kw)(*own, *got)


def _adamw_math(w, g, m, v):
    m = ADAM_B1 * m + (1.0 - ADAM_B1) * g
    v = ADAM_B2 * v + (1.0 - ADAM_B2) * (g * g)
    m_hat = m / (1.0 - ADAM_B1 ** ADAM_STEP)
    v_hat = v / (1.0 - ADAM_B2 ** ADAM_STEP)
    delta = -ADAM_LR * (m_hat / (jnp.sqrt(v_hat) + ADAM_EPS) + ADAM_WD * w)
    return delta, m, v


def _sum_slots_adamw(tag, slots, w, m, v):
    b, _, r, c = slots.shape
    rb = _row_block(r)

    def body(s_ref, w_ref, m_ref, v_ref, g_ref, d_ref, nm_ref, nv_ref):
        g = _sum4([s_ref[k].astype(F32) for k in range(N_CHIP)])
        delta, nm, nv = _adamw_math(w_ref[...], g, m_ref[...], v_ref[...])
        g_ref[...] = g
        d_ref[...] = delta
        nm_ref[...] = nm
        nv_ref[...] = nv

    spec = pl.BlockSpec((None, rb, c), lambda i, j: (i, j, 0))
    sspec = pl.BlockSpec((None, N_CHIP, rb, c), lambda i, j: (i, 0, j, 0))
    s = SDS((b, r, c), F32)
    return pl.pallas_call(
        body, name=f"adamw_{tag}", out_shape=(s, s, s, s),
        grid=(b, r // rb), in_specs=[sspec, spec, spec, spec], out_specs=(spec, spec, spec, spec),
        compiler_params=_cp(("parallel", "parallel")),
    )(slots, w, m, v)


def _adamw_small(tag, entries, grid=None):
    flat_in, in_specs, out_shape, out_specs, layout = [], [], [], [], []
    for slots, w, m, v, slot_spec, w_spec in entries:
        per_layer = isinstance(slots, (list, tuple))
        n_slot = len(slots) if per_layer else 1
        flat_in += (list(slots) if per_layer else [slots]) + [w, m, v]
        in_specs += [slot_spec] * n_slot + [w_spec] * 3
        out_shape += [SDS(w.shape, F32)] * 4
        out_specs += [w_spec] * 4
        layout.append((per_layer, n_slot))
    n_in = len(flat_in)

    def body(*refs):
        i, o = 0, n_in
        for per_layer, n_slot in layout:
            s_refs = refs[i:i + n_slot]
            w_ref, m_ref, v_ref = refs[i + n_slot:i + n_slot + 3]
            outs = refs[o:o + 4]
            if per_layer:
                for l, s_ref in enumerate(s_refs):
                    at = (slice(l, l + 1),) if len(w_ref.shape) == 2 else (l,)
                    g = _sum4([s_ref[k] for k in range(N_CHIP)])
                    res = (g,) + _adamw_math(w_ref[at], g, m_ref[at], v_ref[at])
                    for o_ref, val in zip(outs, res):
                        o_ref[at] = val
            else:
                g = _sum4([s_refs[0][k] for k in range(N_CHIP)])
                res = (g,) + _adamw_math(w_ref[...], g, m_ref[...], v_ref[...])
                for o_ref, val in zip(outs, res):
                    o_ref[...] = val
            i += n_slot + 3
            o += 4

    kw = {}
    if grid is not None:
        kw = dict(grid=grid, in_specs=in_specs, out_specs=tuple(out_specs),
                  compiler_params=_cp(("parallel",) * len(grid)))
    res = pl.pallas_call(body, name=f"adamw_{tag}", out_shape=tuple(out_shape), **kw)(*flat_in)
    return [tuple(res[4 * e:4 * e + 4]) for e in range(len(entries))]


def kernel(x, norm_g, w_in, b_in, ssm_log_dt, ssm_lam_re, ssm_lam_im, ssm_b_re, ssm_b_im, ssm_c_re, ssm_c_im, ssm_d, ssm_w_glu, ssm_b_glu, pool_w, pool_scale, w_branch_a, w_branch_b, w_out, final_norm_g, loss_target, m_norm_g, m_w_in, m_b_in, m_ssm_log_dt, m_ssm_lam_re, m_ssm_lam_im, m_ssm_b_re, m_ssm_b_im, m_ssm_c_re, m_ssm_c_im, m_ssm_d, m_ssm_w_glu, m_ssm_b_glu, m_pool_w, m_pool_scale, m_w_branch_a, m_w_branch_b, m_w_out, m_final_norm_g, v_norm_g, v_w_in, v_b_in, v_ssm_log_dt, v_ssm_lam_re, v_ssm_lam_im, v_ssm_b_re, v_ssm_b_im, v_ssm_c_re, v_ssm_c_im, v_ssm_d, v_ssm_w_glu, v_ssm_b_glu, v_pool_w, v_pool_scale, v_w_branch_a, v_w_branch_b, v_w_out, v_final_norm_g):
    weights = dict(norm_g=norm_g, w_in=w_in, b_in=b_in, ssm_log_dt=ssm_log_dt, ssm_lam_re=ssm_lam_re,
                   ssm_lam_im=ssm_lam_im, ssm_b_re=ssm_b_re, ssm_b_im=ssm_b_im, ssm_c_re=ssm_c_re,
                   ssm_c_im=ssm_c_im, ssm_d=ssm_d, ssm_w_glu=ssm_w_glu, ssm_b_glu=ssm_b_glu, pool_w=pool_w,
                   pool_scale=pool_scale, w_branch_a=w_branch_a, w_branch_b=w_branch_b, w_out=w_out,
                   final_norm_g=final_norm_g.reshape(1, D_MODEL))
    mom_m = dict(norm_g=m_norm_g, w_in=m_w_in, b_in=m_b_in, ssm_log_dt=m_ssm_log_dt, ssm_lam_re=m_ssm_lam_re,
                 ssm_lam_im=m_ssm_lam_im, ssm_b_re=m_ssm_b_re, ssm_b_im=m_ssm_b_im, ssm_c_re=m_ssm_c_re,
                 ssm_c_im=m_ssm_c_im, ssm_d=m_ssm_d, ssm_w_glu=m_ssm_w_glu, ssm_b_glu=m_ssm_b_glu,
                 pool_w=m_pool_w, pool_scale=m_pool_scale, w_branch_a=m_w_branch_a, w_branch_b=m_w_branch_b,
                 w_out=m_w_out, final_norm_g=m_final_norm_g.reshape(1, D_MODEL))
    mom_v = dict(norm_g=v_norm_g, w_in=v_w_in, b_in=v_b_in, ssm_log_dt=v_ssm_log_dt, ssm_lam_re=v_ssm_lam_re,
                 ssm_lam_im=v_ssm_lam_im, ssm_b_re=v_ssm_b_re, ssm_b_im=v_ssm_b_im, ssm_c_re=v_ssm_c_re,
                 ssm_c_im=v_ssm_c_im, ssm_d=v_ssm_d, ssm_w_glu=v_ssm_w_glu, ssm_b_glu=v_ssm_b_glu,
                 pool_w=v_pool_w, pool_scale=v_pool_scale, w_branch_a=v_w_branch_a, w_branch_b=v_w_branch_b,
                 w_out=v_w_out, final_norm_g=v_final_norm_g.reshape(1, D_MODEL))
    order = ["norm_g", "w_in", "b_in", "ssm_log_dt", "ssm_lam_re", "ssm_lam_im", "ssm_b_re", "ssm_b_im",
             "ssm_c_re", "ssm_c_im", "ssm_d", "ssm_w_glu", "ssm_b_glu", "pool_w", "pool_scale", "w_branch_a",
             "w_branch_b", "w_out", "final_norm_g"]
    big_names = ["w_in", "ssm_w_glu", "w_branch_a", "w_branch_b", "w_out"]

    wg_in, wg_glu, wg_a, wg_b, wg_out = _gather_weights([weights[n].astype(BF16)[None] for n in big_names])

    log_dt3 = ssm_log_dt.reshape(DEPTH, N_GROUP, 1)
    bt_re = ssm_b_re.transpose(0, 3, 1, 2)
    bt_im = ssm_b_im.transpose(0, 3, 1, 2)
    abar_re, abar_im, bbt_re, bbt_im = _s5_params(log_dt3, ssm_lam_re, ssm_lam_im, bt_re, bt_im)
    s5_args = (bbt_re, bbt_im, ssm_c_re, ssm_c_im, abar_re, abar_im, ssm_d)

    xs = [x.reshape(SEQ, D_MODEL)]
    saved = []
    for l in range(DEPTH):
        proj = _norm_proj(l, xs[l], norm_g, wg_in, b_in)
        s_re, s_im, y0 = _s5_fwd(l, proj, *s5_args)
        pooled = _pool_fwd(l, proj)
        xs.append(_mix_fwd(l, xs[l], proj, y0, pooled, wg_glu, ssm_b_glu, pool_w, pool_scale, wg_a, wg_b, wg_out))
        saved.append((proj, s_re, s_im, y0, pooled))

    dx, loss_part, g_final = _loss_head(xs[DEPTH], loss_target.reshape(SEQ, D_MODEL), weights["final_norm_g"])
    loss = lax.psum(loss_part[0, 0], ("x", "y", "c"))

    sm = [dict() for _ in range(DEPTH)]
    g_abar_re, g_abar_im, g_bbt_re, g_bbt_im = ([None] * DEPTH for _ in range(4))
    mix_big, gw_in = None, None
    for l in reversed(range(DEPTH)):
        proj, s_re, s_im, y0, pooled = saved[l]
        res = _mix_bwd(l, dx, proj, y0, pooled, wg_glu, ssm_b_glu, pool_w, pool_scale, wg_a, wg_b, wg_out, mix_big)
        dproj, dy0, dpooled = res[:3]
        mix_big = list(res[3:7])
        sm[l]["pool_w"], sm[l]["pool_scale"], sm[l]["ssm_b_glu"] = res[7:]
        dproj = _pool_bwd(l, dpooled, dproj)
        (dproj, g_bbt_re[l], g_bbt_im[l], sm[l]["ssm_c_re"], sm[l]["ssm_c_im"], g_abar_re[l], g_abar_im[l],
         sm[l]["ssm_d"]) = _s5_bwd(l, dy0, proj, s_re, s_im, *s5_args, dproj)
        gw_in, sm[l]["b_in"] = _proj_wgrad(l, xs[l], norm_g, dproj, gw_in)
        dx, sm[l]["norm_g"] = _proj_dgrad(l, dx, xs[l], norm_g, dproj, wg_in)
    grad_x = dx.reshape(1, SEQ, D_MODEL)
    gw_out, gw_a, gw_b, gw_glu = mix_big

    g_ld, g_lr, g_li, g_btr, g_bti = _s5_params_bwd(
        log_dt3, ssm_lam_re, ssm_lam_im, bt_re, bt_im, g_abar_re, g_abar_im, g_bbt_re, g_bbt_im)
    stacked = dict(ssm_log_dt=g_ld.reshape(DEPTH, N_GROUP), ssm_lam_re=g_lr, ssm_lam_im=g_li,
                   ssm_b_re=g_btr.transpose(0, 2, 3, 1), ssm_b_im=g_bti.transpose(0, 2, 3, 1),
                   final_norm_g=g_final)

    vec_names = ["norm_g", "b_in", "ssm_d", "ssm_b_glu", "pool_scale"]
    flat_names = ["final_norm_g", "ssm_log_dt", "ssm_lam_re", "ssm_lam_im"]
    mat_names = ["pool_w", "ssm_c_re", "ssm_c_im"]
    small_a = [sm[l][n] for n in vec_names for l in range(DEPTH)] + [stacked[n] for n in flat_names]
    small_b = [sm[l][n] for n in mat_names for l in range(DEPTH)] + [stacked["ssm_b_re"], stacked["ssm_b_im"]]
    n_a = len(small_a)
    big_part = [gw_in, gw_glu, gw_a, gw_b, gw_out]
    got, got_small = _exchange_sibling(big_part, small_a + small_b)
    core = lax.axis_index("c").astype(jnp.int32).reshape(1)
    chip_big = [_add_own(f"chip_{n}", core, a, b) for n, a, b in zip(big_names, big_part, got)]
    chip_a = _add_lists("chip_small_a", small_a, got_small[:n_a])
    pw_spec = pl.BlockSpec((1, POOL_GROUP, POOL_GROUP), lambda j: (j, 0, 0))
    c_spec = pl.BlockSpec((CH_G, GROUP_W, STATE), lambda j: (j, 0, 0))
    b_spec = pl.BlockSpec((DEPTH, CH_G, STATE, GROUP_W), lambda j: (0, j, 0, 0))
    chip_b = _add_lists("chip_small_b", small_b, got_small[n_a:], grid=(N_CHUNK,),
                        specs=[pw_spec] * 2 + [c_spec] * 4 + [b_spec] * 2)
    slots_big, slots_small = _exchange_chips(chip_big, list(chip_a) + list(chip_b))

    res = {}
    for n, slots in zip(big_names, slots_big):
        res[n] = _sum_slots_adamw(n, slots, weights[n], mom_m[n], mom_v[n])
    entries_a, k = [], 0
    for n in vec_names:
        entries_a.append((slots_small[k:k + DEPTH], weights[n], mom_m[n], mom_v[n], None, None))
        k += DEPTH
    for n in flat_names:
        entries_a.append((slots_small[k], weights[n], mom_m[n], mom_v[n], None, None))
        k += 1
    out_a = _adamw_small("small_a", entries_a)
    for n, r in zip(vec_names + flat_names, out_a):
        res[n] = r
    res["final_norm_g"] = tuple(a.reshape(D_MODEL) for a in res["final_norm_g"])
    pw_s = pl.BlockSpec((N_CHIP, 1, POOL_GROUP, POOL_GROUP), lambda j: (0, j, 0, 0))
    pw_w = pl.BlockSpec((DEPTH, 1, POOL_GROUP, POOL_GROUP), lambda j: (0, j, 0, 0))
    c_s = pl.BlockSpec((N_CHIP, CH_G, GROUP_W, STATE), lambda j: (0, j, 0, 0))
    c_w = pl.BlockSpec((DEPTH, CH_G, GROUP_W, STATE), lambda j: (0, j, 0, 0))
    b_s = pl.BlockSpec((N_CHIP, DEPTH, CH_G, STATE, GROUP_W), lambda j: (0, 0, j, 0, 0))
    b_w = pl.BlockSpec((DEPTH, CH_G, STATE, GROUP_W), lambda j: (0, j, 0, 0))
    entries_b = []
    for n, s_spec, w_spec in (("pool_w", pw_s, pw_w), ("ssm_c_re", c_s, c_w), ("ssm_c_im", c_s, c_w)):
        entries_b.append((slots_small[k:k + DEPTH], weights[n], mom_m[n], mom_v[n], s_spec, w_spec))
        k += DEPTH
    for n in ("ssm_b_re", "ssm_b_im"):
        entries_b.append((slots_small[k], weights[n], mom_m[n], mom_v[n], b_s, b_w))
        k += 1
    out_b = _adamw_small("small_b", entries_b, grid=(N_CHUNK,))
    for n, r in zip(mat_names + ["ssm_b_re", "ssm_b_im"], out_b):
        res[n] = r

    outs = [loss, grad_x]
    for i in range(4):
        outs += [res[n][i] for n in order]
    return tuple(outs)
```

```python
import math

import jax
import jax.numpy as jnp
from jax import lax
from jax.experimental import pallas as pl
from jax.experimental.pallas import tpu as pltpu

F32 = jnp.float32
BF16 = jnp.bfloat16

SEQ = 2048
D_MODEL = 1024
N_IN = 4096
WIDTH = 512
N_GROUP = 32
GROUP_W = 16
STATE = 64
N_STATE = N_GROUP * STATE
N_CHUNK = 4
CH_G = N_GROUP // N_CHUNK
CH_W = WIDTH // N_CHUNK
CH_S = N_STATE // N_CHUNK
N_DEV = 8
N_CHIP = 4
POOL_WINDOWS = (2, 4, 8, 16)
POOL_GROUP = 128
EPS = 1e-6
DEPTH = 2

ADAM_LR = 0.001
ADAM_B1 = 0.9
ADAM_B2 = 0.999
ADAM_EPS = 1e-08
ADAM_WD = 0.01
ADAM_STEP = 10

TILE_M = 256
ROW_BLK = 512
VMEM_LIMIT = 48 * 1024 * 1024
VMEM_LIMIT_BIG = 60 * 1024 * 1024
MESH = pl.DeviceIdType.MESH
ANY = pl.BlockSpec(memory_space=pl.ANY)

GELU_C = math.sqrt(2.0 / math.pi)
GELU_A = 0.044715

SDS = jax.ShapeDtypeStruct


def _cp(sem=None, limit=VMEM_LIMIT):
    return pltpu.CompilerParams(dimension_semantics=sem, vmem_limit_bytes=limit)


def _dot(a, b):
    return jnp.dot(a, b, preferred_element_type=F32)


def _dot_nt(a, b):
    return lax.dot_general(a, b, (((1,), (1,)), ((), ())), preferred_element_type=F32)


def _dot_tn(a, b):
    return lax.dot_general(a, b, (((0,), (0,)), ((), ())), preferred_element_type=F32)


def _sig(x):
    return jax.nn.sigmoid(x)


def _rms(x):
    rs = lax.rsqrt(jnp.mean(x * x, axis=-1, keepdims=True) + EPS)
    return rs, x * rs


def _slot(n):
    return 4 * (n % 2) + n // 2


def _const(shape):
    n = len(shape)
    return pl.BlockSpec(shape, lambda *_: (0,) * n)


def _sum4(p):
    return (p[0] + p[1]) + (p[2] + p[3])


def _s5_param_fn(log_dt, lam_re, lam_im, bt_re, bt_im):
    dt = jnp.exp(log_dt)
    mag = jnp.exp(lam_re * dt)
    ang = lam_im * dt
    abar_re = mag * jnp.cos(ang)
    abar_im = mag * jnp.sin(ang)
    num_re = abar_re - 1.0
    num_im = abar_im
    den = lam_re * lam_re + lam_im * lam_im
    coef_re = (num_re * lam_re + num_im * lam_im) / den
    coef_im = (num_im * lam_re - num_re * lam_im) / den
    bbar_re = coef_re[:, None] * bt_re - coef_im[:, None] * bt_im
    bbar_im = coef_re[:, None] * bt_im + coef_im[:, None] * bt_re
    return abar_re, abar_im, bbar_re, bbar_im


def _s5_params(log_dt, lam_re, lam_im, bt_re, bt_im):
    def body(ld, lr, li, br, bi, o_ar, o_ai, o_br, o_bi):
        ar, ai, bbr, bbi = _s5_param_fn(ld[...], lr[...], li[...], br[...], bi[...])
        o_ar[...] = ar
        o_ai[...] = ai
        o_br[...] = bbr
        o_bi[...] = bbi

    return pl.pallas_call(
        body, name="s5_params",
        out_shape=(SDS(lam_re.shape, F32), SDS(lam_re.shape, F32), SDS(bt_re.shape, F32), SDS(bt_re.shape, F32)),
    )(log_dt, lam_re, lam_im, bt_re, bt_im)


def _s5_params_bwd(log_dt, lam_re, lam_im, bt_re, bt_im, g_ar, g_ai, g_br, g_bi):
    def body(ld, lr, li, br, bi, car0, car1, cai0, cai1, cbr0, cbr1, cbi0, cbi1, o_ld, o_lr, o_li, o_br, o_bi):
        _, vjp = jax.vjp(_s5_param_fn, ld[...], lr[...], li[...], br[...], bi[...])
        both = lambda a, b: jnp.stack([a[...], b[...]], axis=0)
        d_ld, d_lr, d_li, d_br, d_bi = vjp((both(car0, car1), both(cai0, cai1), both(cbr0, cbr1), both(cbi0, cbi1)))
        o_ld[...] = d_ld
        o_lr[...] = d_lr
        o_li[...] = d_li
        o_br[...] = d_br
        o_bi[...] = d_bi

    return pl.pallas_call(
        body, name="s5_params_bwd",
        out_shape=(SDS(log_dt.shape, F32), SDS(lam_re.shape, F32), SDS(lam_re.shape, F32),
                   SDS(bt_re.shape, F32), SDS(bt_re.shape, F32)),
    )(log_dt, lam_re, lam_im, bt_re, bt_im, *g_ar, *g_ai, *g_br, *g_bi)


def _norm_proj(layer, x, norm_g, wg_in, b_in):
    def body(x_ref, g_ref, w_ref, b_ref, o_ref):
        _, xn = _rms(x_ref[...])
        h = (xn * g_ref[layer:layer + 1, :]).astype(BF16)
        for k in range(N_DEV):
            cols = slice(k * WIDTH, (k + 1) * WIDTH)
            o_ref[:, cols] = _dot(h, w_ref[k]) + b_ref[layer:layer + 1, cols]

    return pl.pallas_call(
        body, name=f"norm_proj_l{layer}",
        out_shape=SDS((SEQ, N_IN), F32),
        grid=(SEQ // TILE_M,),
        in_specs=[pl.BlockSpec((TILE_M, D_MODEL), lambda i: (i, 0)),
                  _const((DEPTH, D_MODEL)),
                  pl.BlockSpec((N_DEV, None, D_MODEL, WIDTH), lambda i: (0, layer, 0, 0)),
                  _const((DEPTH, N_IN))],
        out_specs=pl.BlockSpec((TILE_M, N_IN), lambda i: (i, 0)),
        compiler_params=_cp(("parallel",)),
    )(x, norm_g, wg_in, b_in)


def _scan_tile_rows():
    return lax.broadcasted_iota(jnp.int32, (8, CH_S), 0)


def _s5_param_specs(layer):
    bt = lambda: pl.BlockSpec((None, GROUP_W, CH_G, STATE), lambda j: (layer, 0, j, 0))
    cc = lambda: pl.BlockSpec((None, CH_G, GROUP_W, STATE), lambda j: (layer, j, 0, 0))
    ab = lambda: pl.BlockSpec((None, CH_G, STATE), lambda j: (layer, j, 0))
    return [bt(), bt(), cc(), cc(), ab(), ab(), pl.BlockSpec((DEPTH, CH_W), lambda j: (0, j))]


def _s5_param_scratch():
    return [pltpu.VMEM((CH_W, CH_S), BF16)] * 4 + [pltpu.VMEM((1, CH_S), F32)] * 2


def _s5_fill(btre_ref, btim_ref, cre_ref, cim_ref, are_ref, aim_ref, bdre, bdim, ctre, ctim, arow, airow):
    for m in (bdre, bdim, ctre, ctim):
        m[...] = jnp.zeros_like(m)
    for g in range(CH_G):
        rows = slice(g * GROUP_W, (g + 1) * GROUP_W)
        cols = slice(g * STATE, (g + 1) * STATE)
        bdre[rows, cols] = btre_ref[:, g, :].astype(BF16)
        bdim[rows, cols] = btim_ref[:, g, :].astype(BF16)
        ctre[rows, cols] = cre_ref[g].astype(BF16)
        ctim[rows, cols] = cim_ref[g].astype(BF16)
        arow[:, cols] = are_ref[g:g + 1, :]
        airow[:, cols] = aim_ref[g:g + 1, :]


def _s5_fwd(layer, proj, bbt_re, bbt_im, c_re, c_im, abar_re, abar_im, d_skip):
    def body(u_ref, btre_ref, btim_ref, cre_ref, cim_ref, are_ref, aim_ref, d_ref,
             sre_ref, sim_ref, y_ref, bdre, bdim, ctre, ctim, arow, airow):
        _s5_fill(btre_ref, btim_ref, cre_ref, cim_ref, are_ref, aim_ref, bdre, bdim, ctre, ctim, arow, airow)
        for rb in range(SEQ // ROW_BLK):
            rows = pl.ds(rb * ROW_BLK, ROW_BLK)
            ub = u_ref[rows, :].astype(BF16)
            sre_ref[rows, :] = _dot(ub, bdre[...])
            sim_ref[rows, :] = _dot(ub, bdim[...])
        ar = arow[...]
        ai = airow[...]
        row_id = _scan_tile_rows()

        def step(i, carry):
            sr, si = carry
            base = pl.multiple_of(i * 8, 8)
            tr = sre_ref[pl.ds(base, 8), :]
            ti = sim_ref[pl.ds(base, 8), :]
            outr, outi = tr, ti
            for r in range(8):
                nsr = ar * sr - ai * si + tr[r:r + 1, :]
                nsi = ar * si + ai * sr + ti[r:r + 1, :]
                outr = jnp.where(row_id == r, nsr, outr)
                outi = jnp.where(row_id == r, nsi, outi)
                sr, si = nsr, nsi
            sre_ref[pl.ds(base, 8), :] = outr
            sim_ref[pl.ds(base, 8), :] = outi
            return sr, si

        zero = jnp.zeros((1, CH_S), F32)
        lax.fori_loop(0, SEQ // 8, step, (zero, zero))
        d = d_ref[layer:layer + 1, :]
        for rb in range(SEQ // ROW_BLK):
            rows = pl.ds(rb * ROW_BLK, ROW_BLK)
            y = (_dot_nt(sre_ref[rows, :].astype(BF16), ctre[...])
                 - _dot_nt(sim_ref[rows, :].astype(BF16), ctim[...]))
            y_ref[rows, :] = y + d * u_ref[rows, :]

    return pl.pallas_call(
        body, name=f"s5_fwd_l{layer}",
        out_shape=(SDS((SEQ, N_STATE), F32), SDS((SEQ, N_STATE), F32), SDS((SEQ, WIDTH), F32)),
        grid=(N_CHUNK,),
        in_specs=[pl.BlockSpec((SEQ, CH_W), lambda j: (0, j))] + _s5_param_specs(layer),
        out_specs=(pl.BlockSpec((SEQ, CH_S), lambda j: (0, j)),
                   pl.BlockSpec((SEQ, CH_S), lambda j: (0, j)),
                   pl.BlockSpec((SEQ, CH_W), lambda j: (0, j))),
        scratch_shapes=_s5_param_scratch(),
        compiler_params=_cp(("parallel",)),
    )(proj, bbt_re, bbt_im, c_re, c_im, abar_re, abar_im, d_skip)


def _pool_counts(win):
    t = lax.broadcasted_iota(jnp.int32, (SEQ, POOL_GROUP), 0)
    return t, jnp.minimum(t + 1, win).astype(F32)


def _pool_fwd(layer, proj):
    def body(u_ref, o_ref):
        for gi, win in enumerate(POOL_WINDOWS):
            cols = slice(gi * POOL_GROUP, (gi + 1) * POOL_GROUP)
            u = u_ref[:, cols]
            t, count = _pool_counts(win)
            acc = u
            k = 1
            while k < win:
                acc = acc + jnp.where(t >= k, pltpu.roll(acc, k, 0), 0.0)
                k *= 2
            o_ref[:, cols] = acc / count - u

    return pl.pallas_call(
        body, name=f"pool_fwd_l{layer}",
        out_shape=SDS((SEQ, WIDTH), F32),
        grid=(1,),
        in_specs=[pl.BlockSpec((SEQ, WIDTH), lambda i: (0, 2))],
        out_specs=pl.BlockSpec((SEQ, WIDTH), lambda i: (0, 0)),
        compiler_params=_cp(("arbitrary",)),
    )(proj)


def _gelu_parts(y0):
    t = jnp.tanh(GELU_C * (y0 + GELU_A * (y0 * y0 * y0)))
    return t, 0.5 * y0 * (1.0 + t)


def _mix_forward(layer, p_ref, y0_ref, pooled_ref, wglu_ref, bglu_ref, pw_ref, scale_ref, wa_ref, wb_ref):
    za = p_ref[:, WIDTH:2 * WIDTH]
    zb = p_ref[:, 3 * WIDTH:4 * WIDTH]
    ga = p_ref[:, 4 * WIDTH:4 * WIDTH + D_MODEL]
    gb = p_ref[:, 4 * WIDTH + D_MODEL:]
    y0 = y0_ref[...]
    t, y1 = _gelu_parts(y0)
    y1b = y1.astype(BF16)
    q = _dot(y1b, wglu_ref[...].reshape(WIDTH, WIDTH)) + bglu_ref[layer:layer + 1, :]
    sq = _sig(q)
    y2 = y1 * sq
    sza = _sig(za)
    silu_za = za * sza
    ya = y2 * silu_za
    pooled = pooled_ref[...]
    mixed = jnp.concatenate(
        [_dot(pooled[:, g * POOL_GROUP:(g + 1) * POOL_GROUP].astype(BF16), pw_ref[g].astype(BF16))
         for g in range(len(POOL_WINDOWS))], axis=1)
    szb = _sig(zb)
    silu_zb = zb * szb
    scale = scale_ref[layer:layer + 1, :]
    ms = mixed * scale
    yb = ms * silu_zb
    yab = ya.astype(BF16)
    ybb = yb.astype(BF16)
    ma = jnp.concatenate([_dot(yab, wa_ref[k]) for k in range(N_DEV)], axis=1)
    mb = jnp.concatenate([_dot(ybb, wb_ref[k]) for k in range(N_DEV)], axis=1)
    sga = _sig(ga)
    sgb = _sig(gb)
    merged = sga * ma + sgb * mb
    return dict(za=za, zb=zb, y0=y0, t=t, y1=y1, y1b=y1b, sq=sq, y2=y2, sza=sza, silu_za=silu_za,
                pooled=pooled, mixed=mixed, szb=szb, silu_zb=silu_zb, scale=scale, ms=ms, yab=yab, ybb=ybb,
                ma=ma, mb=mb, sga=sga, sgb=sgb, merged=merged)


def _mix_weight_specs(layer):
    return [pl.BlockSpec((N_DEV, None, WIDTH // N_DEV, WIDTH), lambda i: (0, layer, 0, 0)),
            _const((DEPTH, WIDTH)),
            pl.BlockSpec((None, 4, POOL_GROUP, POOL_GROUP), lambda i: (layer, 0, 0, 0)),
            _const((DEPTH, WIDTH)),
            pl.BlockSpec((N_DEV, None, WIDTH, D_MODEL // N_DEV), lambda i: (0, layer, 0, 0)),
            pl.BlockSpec((N_DEV, None, WIDTH, D_MODEL // N_DEV), lambda i: (0, layer, 0, 0)),
            pl.BlockSpec((N_DEV, None, D_MODEL // N_DEV, D_MODEL), lambda i: (0, layer, 0, 0))]


def _mix_fwd(layer, x, proj, y0, pooled, wg_glu, b_glu, pool_w, pool_scale, wg_a, wg_b, wg_out):
    def body(x_ref, p_ref, y0_ref, pooled_ref, wglu_ref, bglu_ref, pw_ref, scale_ref, wa_ref, wb_ref,
             wout_ref, o_ref):
        f = _mix_forward(layer, p_ref, y0_ref, pooled_ref, wglu_ref, bglu_ref, pw_ref, scale_ref, wa_ref, wb_ref)
        wout = wout_ref[...].reshape(D_MODEL, D_MODEL)
        o_ref[...] = x_ref[...] + _dot(f["merged"].astype(BF16), wout)

    return pl.pallas_call(
        body, name=f"mix_fwd_l{layer}",
        out_shape=SDS((SEQ, D_MODEL), F32),
        grid=(SEQ // TILE_M,),
        in_specs=[pl.BlockSpec((TILE_M, D_MODEL), lambda i: (i, 0)),
                  pl.BlockSpec((TILE_M, N_IN), lambda i: (i, 0)),
                  pl.BlockSpec((TILE_M, WIDTH), lambda i: (i, 0)),
                  pl.BlockSpec((TILE_M, WIDTH), lambda i: (i, 0))] + _mix_weight_specs(layer),
        out_specs=pl.BlockSpec((TILE_M, D_MODEL), lambda i: (i, 0)),
        compiler_params=_cp(("parallel",)),
    )(x, proj, y0, pooled, wg_glu, b_glu, pool_w, pool_scale, wg_a, wg_b, wg_out)


def _loss_head(x, target, final_g):
    def body(x_ref, t_ref, g_ref, dx_ref, loss_ref, gg_ref):
        @pl.when(pl.program_id(0) == 0)
        def _():
            loss_ref[...] = jnp.zeros_like(loss_ref)
            gg_ref[...] = jnp.zeros_like(gg_ref)

        g = g_ref[...]
        rs, xn = _rms(x_ref[...])
        err = xn * g - t_ref[...]
        loss_ref[...] += 0.5 * jnp.sum(jnp.mean(err * err, axis=-1, keepdims=True), axis=0, keepdims=True)
        dy = err * (1.0 / D_MODEL)
        gg_ref[...] += jnp.sum(dy * xn, axis=0, keepdims=True)
        dxn = dy * g
        dx_ref[...] = rs * (dxn - xn * jnp.mean(dxn * xn, axis=-1, keepdims=True))

    return pl.pallas_call(
        body, name="loss_head",
        out_shape=(SDS((SEQ, D_MODEL), F32), SDS((1, 1), F32), SDS((1, D_MODEL), F32)),
        grid=(SEQ // TILE_M,),
        in_specs=[pl.BlockSpec((TILE_M, D_MODEL), lambda i: (i, 0)),
                  pl.BlockSpec((TILE_M, D_MODEL), lambda i: (i, 0)),
                  _const((1, D_MODEL))],
        out_specs=(pl.BlockSpec((TILE_M, D_MODEL), lambda i: (i, 0)), _const((1, 1)), _const((1, D_MODEL))),
        compiler_params=_cp(("arbitrary",)),
    )(x, target, final_g)


def _big_shapes():
    return dict(w_out=(DEPTH, N_DEV, D_MODEL // N_DEV, D_MODEL), w_branch_a=(DEPTH, N_DEV, WIDTH, D_MODEL // N_DEV),
                w_branch_b=(DEPTH, N_DEV, WIDTH, D_MODEL // N_DEV), ssm_w_glu=(DEPTH, N_DEV, WIDTH // N_DEV, WIDTH),
                w_in=(DEPTH, N_DEV, D_MODEL, WIDTH))


def _mix_bwd(layer, dx_next, proj, y0, pooled, wg_glu, b_glu, pool_w, pool_scale, wg_a, wg_b, wg_out, prev):
    n_k = N_DEV
    n_prev = 0 if prev is None else len(prev)

    def body(*refs):
        (dx_ref, p_ref, y0_ref, pooled_ref, wglu_ref, bglu_ref, pw_ref, scale_ref, wa_ref, wb_ref,
         wout_ref) = refs[:11]
        (dproj_ref, dy0_ref, dpooled_ref, gwout_ref, gwa_ref, gwb_ref, gwglu_ref, gpw_ref,
         gscale_ref, gbglu_ref) = refs[11 + n_prev:]

        @pl.when(pl.program_id(0) == 0)
        def _():
            for r in (gwout_ref, gwa_ref, gwb_ref, gwglu_ref, gpw_ref, gscale_ref, gbglu_ref):
                r[...] = jnp.zeros_like(r)

        f = _mix_forward(layer, p_ref, y0_ref, pooled_ref, wglu_ref, bglu_ref, pw_ref, scale_ref, wa_ref, wb_ref)
        wglu = wglu_ref[...].reshape(WIDTH, WIDTH)
        wout = wout_ref[...].reshape(D_MODEL, D_MODEL)
        blk = D_MODEL // n_k
        dxb = dx_ref[...].astype(BF16)
        dmerged = _dot_nt(dxb, wout)
        gwout = _dot_tn(f["merged"].astype(BF16), dxb)
        for k in range(n_k):
            gwout_ref[_slot(k)] += gwout[k * blk:(k + 1) * blk, :]
        dma = dmerged * f["sga"]
        dmb = dmerged * f["sgb"]
        dga = dmerged * f["ma"] * f["sga"] * (1.0 - f["sga"])
        dgb = dmerged * f["mb"] * f["sgb"] * (1.0 - f["sgb"])
        dmab = dma.astype(BF16)
        dmbb = dmb.astype(BF16)
        dya = jnp.zeros((TILE_M, WIDTH), F32)
        dyb = jnp.zeros((TILE_M, WIDTH), F32)
        for k in range(n_k):
            da_k = dmab[:, k * blk:(k + 1) * blk]
            db_k = dmbb[:, k * blk:(k + 1) * blk]
            dya = dya + _dot_nt(da_k, wa_ref[k])
            dyb = dyb + _dot_nt(db_k, wb_ref[k])
            gwa_ref[_slot(k)] += _dot_tn(f["yab"], da_k)
            gwb_ref[_slot(k)] += _dot_tn(f["ybb"], db_k)
        zb, szb = f["zb"], f["szb"]
        dzb = dyb * f["ms"] * (szb * (1.0 + zb * (1.0 - szb)))
        dms = dyb * f["silu_zb"]
        gscale_ref[...] += jnp.sum(dms * f["mixed"], axis=0, keepdims=True)
        dmixed = (dms * f["scale"]).astype(BF16)
        pooled = f["pooled"]
        for g in range(len(POOL_WINDOWS)):
            cols = slice(g * POOL_GROUP, (g + 1) * POOL_GROUP)
            dpooled_ref[:, cols] = _dot_nt(dmixed[:, cols], pw_ref[g].astype(BF16))
            gpw_ref[g] += _dot_tn(pooled[:, cols].astype(BF16), dmixed[:, cols])
        za, sza = f["za"], f["sza"]
        dza = dya * f["y2"] * (sza * (1.0 + za * (1.0 - sza)))
        dy2 = dya * f["silu_za"]
        sq = f["sq"]
        dq = dy2 * f["y1"] * sq * (1.0 - sq)
        dqb = dq.astype(BF16)
        dy1 = dy2 * sq + _dot_nt(dqb, wglu)
        gwglu = _dot_tn(f["y1b"], dqb)
        rblk = WIDTH // n_k
        for k in range(n_k):
            gwglu_ref[_slot(k)] += gwglu[k * rblk:(k + 1) * rblk, :]
        gbglu_ref[...] += jnp.sum(dq, axis=0, keepdims=True)
        y0, t = f["y0"], f["t"]
        dgelu = 0.5 * (1.0 + t) + 0.5 * y0 * (1.0 - t * t) * (GELU_C * (1.0 + 3.0 * GELU_A * y0 * y0))
        dy0_ref[...] = dy1 * dgelu
        zeros = jnp.zeros((TILE_M, WIDTH), BF16)
        dproj_ref[:, 0:WIDTH] = zeros
        dproj_ref[:, WIDTH:2 * WIDTH] = dza.astype(BF16)
        dproj_ref[:, 2 * WIDTH:3 * WIDTH] = zeros
        dproj_ref[:, 3 * WIDTH:4 * WIDTH] = dzb.astype(BF16)
        dproj_ref[:, 4 * WIDTH:4 * WIDTH + D_MODEL] = dga.astype(BF16)
        dproj_ref[:, 4 * WIDTH + D_MODEL:] = dgb.astype(BF16)

    tile = lambda w: pl.BlockSpec((TILE_M, w), lambda i: (i, 0))
    shapes = _big_shapes()
    big = ["w_out", "w_branch_a", "w_branch_b", "ssm_w_glu"]
    slab = lambda n: pl.BlockSpec((None,) + shapes[n][1:], lambda i: (layer, 0, 0, 0))
    args = [dx_next, proj, y0, pooled, wg_glu, b_glu, pool_w, pool_scale, wg_a, wg_b, wg_out]
    return pl.pallas_call(
        body, name=f"mix_bwd_l{layer}",
        out_shape=(SDS((SEQ, N_IN), BF16), SDS((SEQ, WIDTH), F32), SDS((SEQ, WIDTH), F32))
        + tuple(SDS(shapes[n], F32) for n in big)
        + (SDS((4, POOL_GROUP, POOL_GROUP), F32), SDS((1, WIDTH), F32), SDS((1, WIDTH), F32)),
        grid=(SEQ // TILE_M,),
        in_specs=[tile(D_MODEL), tile(N_IN), tile(WIDTH), tile(WIDTH)] + _mix_weight_specs(layer) + [ANY] * n_prev,
        out_specs=(tile(N_IN), tile(WIDTH), tile(WIDTH)) + tuple(slab(n) for n in big)
        + (_const((4, POOL_GROUP, POOL_GROUP)), _const((1, WIDTH)), _const((1, WIDTH))),
        input_output_aliases={len(args) + i: 3 + i for i in range(n_prev)},
        compiler_params=_cp(("arbitrary",), VMEM_LIMIT_BIG),
    )(*args, *(prev or ()))


def _pool_bwd(layer, dpooled, dproj):
    def body(dp_ref, _, o_ref):
        for gi, win in enumerate(POOL_WINDOWS):
            cols = slice(gi * POOL_GROUP, (gi + 1) * POOL_GROUP)
            dp = dp_ref[:, cols]
            t, count = _pool_counts(win)
            e = dp / count
            acc = e
            k = 1
            while k < win:
                acc = acc + jnp.where(t < SEQ - k, pltpu.roll(acc, SEQ - k, 0), 0.0)
                k *= 2
            o_ref[:, cols] = (acc - dp).astype(BF16)

    return pl.pallas_call(
        body, name=f"pool_bwd_l{layer}",
        out_shape=SDS((SEQ, N_IN), BF16),
        grid=(1,),
        in_specs=[pl.BlockSpec((SEQ, WIDTH), lambda i: (0, 0)), ANY],
        out_specs=pl.BlockSpec((SEQ, WIDTH), lambda i: (0, 2)),
        input_output_aliases={1: 0},
        compiler_params=_cp(("arbitrary",)),
    )(dpooled, dproj)


def _s5_bwd(layer, dy0, proj, s_re, s_im, bbt_re, bbt_im, c_re, c_im, abar_re, abar_im, d_skip, dproj):
    def body(dy_ref, u_ref, sre_ref, sim_ref, btre_ref, btim_ref, cre_ref, cim_ref, are_ref, aim_ref,
             d_ref, _, du_ref, gbre_ref, gbim_ref, gcre_ref, gcim_ref, gare_ref, gaim_ref, gd_ref,
             lre_ref, lim_ref, bdre, bdim, ctre, ctim, arow, airow):
        _s5_fill(btre_ref, btim_ref, cre_ref, cim_ref, are_ref, aim_ref, bdre, bdim, ctre, ctim, arow, airow)
        n_rb = SEQ // ROW_BLK
        gcre = jnp.zeros((CH_W, CH_S), F32)
        gcim = jnp.zeros((CH_W, CH_S), F32)
        for rb in range(n_rb):
            rows = pl.ds(rb * ROW_BLK, ROW_BLK)
            dyb = dy_ref[rows, :].astype(BF16)
            lre_ref[rows, :] = _dot(dyb, ctre[...])
            lim_ref[rows, :] = -_dot(dyb, ctim[...])
            gcre = gcre + _dot_tn(dyb, sre_ref[rows, :].astype(BF16))
            gcim = gcim - _dot_tn(dyb, sim_ref[rows, :].astype(BF16))
        ar = arow[...]
        ai = airow[...]
        row_id = _scan_tile_rows()

        def step(n, carry):
            lr, li = carry
            base = pl.multiple_of((SEQ // 8 - 1 - n) * 8, 8)
            tr = lre_ref[pl.ds(base, 8), :]
            ti = lim_ref[pl.ds(base, 8), :]
            outr, outi = tr, ti
            for r in range(7, -1, -1):
                nlr = ar * lr + ai * li + tr[r:r + 1, :]
                nli = ar * li - ai * lr + ti[r:r + 1, :]
                outr = jnp.where(row_id == r, nlr, outr)
                outi = jnp.where(row_id == r, nli, outi)
                lr, li = nlr, nli
            lre_ref[pl.ds(base, 8), :] = outr
            lim_ref[pl.ds(base, 8), :] = outi
            return lr, li

        zero = jnp.zeros((1, CH_S), F32)
        lax.fori_loop(0, SEQ // 8, step, (zero, zero))

        gare = jnp.zeros((1, CH_S), F32)
        gaim = jnp.zeros((1, CH_S), F32)
        gbre = jnp.zeros((CH_W, CH_S), F32)
        gbim = jnp.zeros((CH_W, CH_S), F32)
        gd = jnp.zeros((1, CH_W), F32)
        d = d_ref[layer:layer + 1, :]
        first = lax.broadcasted_iota(jnp.int32, (ROW_BLK, CH_S), 0) == 0
        for rb in range(n_rb):
            rows = pl.ds(rb * ROW_BLK, ROW_BLK)
            lr = lre_ref[rows, :]
            li = lim_ref[rows, :]
            if rb == 0:
                prev_r = jnp.zeros((1, CH_S), F32)
                prev_i = jnp.zeros((1, CH_S), F32)
            else:
                prev_r = sre_ref[pl.ds(rb * ROW_BLK - 1, 1), :]
                prev_i = sim_ref[pl.ds(rb * ROW_BLK - 1, 1), :]
            spr = jnp.where(first, prev_r, pltpu.roll(sre_ref[rows, :], 1, 0))
            spi = jnp.where(first, prev_i, pltpu.roll(sim_ref[rows, :], 1, 0))
            gare = gare + jnp.sum(lr * spr + li * spi, axis=0, keepdims=True)
            gaim = gaim + jnp.sum(li * spr - lr * spi, axis=0, keepdims=True)
            lrb = lr.astype(BF16)
            lib = li.astype(BF16)
            u = u_ref[rows, :]
            ub = u.astype(BF16)
            dy = dy_ref[rows, :]
            du = dy * d + _dot_nt(lrb, bdre[...]) + _dot_nt(lib, bdim[...])
            du_ref[rows, :] = du.astype(BF16)
            gbre = gbre + _dot_tn(ub, lrb)
            gbim = gbim + _dot_tn(ub, lib)
            gd = gd + jnp.sum(dy * u, axis=0, keepdims=True)
        gd_ref[...] = gd
        for g in range(CH_G):
            rows = slice(g * GROUP_W, (g + 1) * GROUP_W)
            cols = slice(g * STATE, (g + 1) * STATE)
            gcre_ref[g] = gcre[rows, cols]
            gcim_ref[g] = gcim[rows, cols]
            gbre_ref[:, g, :] = gbre[rows, cols]
            gbim_ref[:, g, :] = gbim[rows, cols]
            gare_ref[g:g + 1, :] = gare[:, cols]
            gaim_ref[g:g + 1, :] = gaim[:, cols]

    chunk_w = lambda: pl.BlockSpec((SEQ, CH_W), lambda j: (0, j))
    chunk_s = lambda: pl.BlockSpec((SEQ, CH_S), lambda j: (0, j))
    gbt = lambda: pl.BlockSpec((GROUP_W, CH_G, STATE), lambda j: (0, j, 0))
    gcc = lambda: pl.BlockSpec((CH_G, GROUP_W, STATE), lambda j: (j, 0, 0))
    gab = lambda: pl.BlockSpec((CH_G, STATE), lambda j: (j, 0))
    return pl.pallas_call(
        body, name=f"s5_bwd_l{layer}",
        out_shape=(SDS((SEQ, N_IN), BF16),
                   SDS((GROUP_W, N_GROUP, STATE), F32), SDS((GROUP_W, N_GROUP, STATE), F32),
                   SDS((N_GROUP, GROUP_W, STATE), F32), SDS((N_GROUP, GROUP_W, STATE), F32),
                   SDS((N_GROUP, STATE), F32), SDS((N_GROUP, STATE), F32), SDS((1, WIDTH), F32)),
        grid=(N_CHUNK,),
        in_specs=[chunk_w(), chunk_w(), chunk_s(), chunk_s()] + _s5_param_specs(layer) + [ANY],
        out_specs=(chunk_w(), gbt(), gbt(), gcc(), gcc(), gab(), gab(),
                   pl.BlockSpec((1, CH_W), lambda j: (0, j))),
        scratch_shapes=[pltpu.VMEM((SEQ, CH_S), F32), pltpu.VMEM((SEQ, CH_S), F32)] + _s5_param_scratch(),
        input_output_aliases={11: 0},
        compiler_params=_cp(("arbitrary",), VMEM_LIMIT_BIG),
    )(dy0, proj, s_re, s_im, bbt_re, bbt_im, c_re, c_im, abar_re, abar_im, d_skip, dproj)


def _proj_wgrad(layer, x, norm_g, dproj, prev):
    tm = 512
    n_prev = 0 if prev is None else 1

    def body(*refs):
        x_ref, g_ref, dp_ref = refs[:3]
        gw_ref, gb_ref = refs[3 + n_prev:]

        @pl.when(pl.program_id(1) == 0)
        def _():
            gw_ref[...] = jnp.zeros_like(gw_ref)
            gb_ref[...] = jnp.zeros_like(gb_ref)

        _, xn = _rms(x_ref[...])
        h = (xn * g_ref[layer:layer + 1, :]).astype(BF16)
        dp = dp_ref[...]
        gw_ref[...] += _dot_tn(h, dp)
        gb_ref[...] += jnp.sum(dp.astype(F32), axis=0, keepdims=True)

    return pl.pallas_call(
        body, name=f"proj_wgrad_l{layer}",
        out_shape=(SDS(_big_shapes()["w_in"], F32), SDS((1, N_IN), F32)),
        grid=(N_DEV, SEQ // tm),
        in_specs=[pl.BlockSpec((tm, D_MODEL), lambda n, t: (t, 0)),
                  _const((DEPTH, D_MODEL)),
                  pl.BlockSpec((tm, WIDTH), lambda n, t: (t, n))] + [ANY] * n_prev,
        out_specs=(pl.BlockSpec((None, None, D_MODEL, WIDTH), lambda n, t: (layer, _slot(n), 0, 0)),
                   pl.BlockSpec((1, WIDTH), lambda n, t: (0, n))),
        input_output_aliases={3: 0} if n_prev else {},
        compiler_params=_cp(("parallel", "arbitrary")),
    )(x, norm_g, dproj, *([prev] if n_prev else []))


def _proj_dgrad(layer, dx_next, x, norm_g, dproj, wg_in):
    def body(dxn_ref, x_ref, g_ref, dp_ref, w_ref, dx_ref, gg_ref):
        @pl.when(pl.program_id(0) == 0)
        def _():
            gg_ref[...] = jnp.zeros_like(gg_ref)

        dh = jnp.zeros((TILE_M, D_MODEL), F32)
        for k in range(N_DEV):
            dh = dh + _dot_nt(dp_ref[:, k * WIDTH:(k + 1) * WIDTH], w_ref[k])
        rs, xn = _rms(x_ref[...])
        gg_ref[...] += jnp.sum(dh * xn, axis=0, keepdims=True)
        dxn = dh * g_ref[layer:layer + 1, :]
        dx_ref[...] = dxn_ref[...] + rs * (dxn - xn * jnp.mean(dxn * xn, axis=-1, keepdims=True))

    return pl.pallas_call(
        body, name=f"proj_dgrad_l{layer}",
        out_shape=(SDS((SEQ, D_MODEL), F32), SDS((1, D_MODEL), F32)),
        grid=(SEQ // TILE_M,),
        in_specs=[pl.BlockSpec((TILE_M, D_MODEL), lambda i: (i, 0)),
                  pl.BlockSpec((TILE_M, D_MODEL), lambda i: (i, 0)),
                  _const((DEPTH, D_MODEL)),
                  pl.BlockSpec((TILE_M, N_IN), lambda i: (i, 0)),
                  pl.BlockSpec((N_DEV, None, D_MODEL, WIDTH), lambda i: (0, layer, 0, 0))],
        out_specs=(pl.BlockSpec((TILE_M, D_MODEL), lambda i: (i, 0)), _const((1, D_MODEL))),
        compiler_params=_cp(("arbitrary",)),
    )(dx_next, x, norm_g, dproj, wg_in)


def _my_place():
    return lax.axis_index("x"), lax.axis_index("y"), lax.axis_index("c")


def _gather_weights(shards):
    n = len(shards)

    def body(*refs):
        src = refs[:n]
        out = refs[n:2 * n]
        send_sems, recv_sems, local_sems = refs[2 * n:]
        x, y, c = _my_place()
        me, sibling = (x, y, c), (x, y, 1 - c)
        chips = [(1 - x, y), (x, 1 - y), (1 - x, 1 - y)]

        def rows(t, place):
            px, py, pc = place
            return out[t].at[pl.ds(4 * px + 2 * py + pc, 1)]

        def copy(t, k, block, to, from_src=False):
            return pltpu.make_async_remote_copy(
                src_ref=src[t] if from_src else rows(t, block), dst_ref=rows(t, block),
                send_sem=send_sems.at[7 * t + k], recv_sem=recv_sems.at[7 * t + k], device_id=to,
                device_id_type=MESH)

        mine = [pltpu.make_async_copy(src[t], rows(t, me), local_sems.at[t]) for t in range(n)]
        for cp in mine:
            cp.start()
        first = []
        for t in range(n):
            first.append(copy(t, 0, me, sibling, from_src=True))
            first += [copy(t, 1 + j, me, (*chip, c), from_src=True) for j, chip in enumerate(chips)]
        for cp in first:
            cp.start()
        passed = []
        for t in range(n):
            for j, chip in enumerate(chips):
                copy(t, 1 + j, (*chip, c), me).wait_recv()
                fwd = copy(t, 4 + j, (*chip, c), sibling)
                fwd.start()
                passed.append(fwd)
        for t in range(n):
            copy(t, 0, sibling, me).wait_recv()
            for j, chip in enumerate(chips):
                copy(t, 4 + j, (*chip, 1 - c), me).wait_recv()
        for cp in first + passed:
            cp.wait_send()
        for cp in mine:
            cp.wait()

    return pl.pallas_call(
        body, name="gather_weights",
        out_shape=tuple(SDS((N_DEV,) + a.shape[1:], a.dtype) for a in shards),
        in_specs=[ANY] * n, out_specs=tuple([ANY] * n),
        scratch_shapes=[pltpu.SemaphoreType.DMA((7 * n,)), pltpu.SemaphoreType.DMA((7 * n,)),
                        pltpu.SemaphoreType.DMA((n,))],
    )(*shards)


def _exchange_sibling(big, small):
    n = len(big)
    n_small = len(small)
    n_copies = n * DEPTH * 4 + n_small

    def body(*refs):
        g = refs[:n]
        sg = refs[n:n + n_small]
        got = refs[n + n_small:2 * n + n_small]
        got_small = refs[2 * n + n_small:2 * (n + n_small)]
        send_sems, recv_sems = refs[2 * (n + n_small):]
        x, y, c = _my_place()
        sibling = (x, y, 1 - c)
        pairs = []
        for t in range(n):
            for l in range(DEPTH):
                for s in range(4):
                    pairs.append((g[t].at[l, pl.ds(4 * (1 - c) + s, 1)], got[t].at[l, pl.ds(s, 1)]))
        pairs += list(zip(sg, got_small))
        copies = [pltpu.make_async_remote_copy(
            src_ref=src, dst_ref=dst, send_sem=send_sems.at[k], recv_sem=recv_sems.at[k],
            device_id=sibling, device_id_type=MESH) for k, (src, dst) in enumerate(pairs)]
        for cp in copies:
            cp.start()
        for cp in copies:
            cp.wait()

    half = lambda a: SDS((a.shape[0], 4) + a.shape[2:], a.dtype)
    out_shape = tuple([half(a) for a in big] + [SDS(a.shape, a.dtype) for a in small])
    res = pl.pallas_call(
        body, name="exchange_sibling", out_shape=out_shape,
        in_specs=[ANY] * (n + n_small), out_specs=tuple([ANY] * (n + n_small)),
        scratch_shapes=[pltpu.SemaphoreType.DMA((n_copies,)), pltpu.SemaphoreType.DMA((n_copies,))],
    )(*big, *small)
    return res[:n], res[n:]


def _exchange_chips(big, small):
    n = len(big)
    n_small = len(small)

    def body(*refs):
        cp_in = refs[:n]
        sm_in = refs[n:n + n_small]
        slots = refs[n + n_small:2 * n + n_small]
        sm_slots = refs[2 * n + n_small:2 * (n + n_small)]
        send_sems, recv_sems, local_sems = refs[2 * (n + n_small):]
        x, y, c = _my_place()
        my_chip = 2 * x + y
        chips = [(1 - x, y), (x, 1 - y), (1 - x, 1 - y)]

        def remote(k, src, dst, chip):
            return pltpu.make_async_remote_copy(
                src_ref=src, dst_ref=dst, send_sem=send_sems.at[k], recv_sem=recv_sems.at[k],
                device_id=(*chip, c), device_id_type=MESH)

        sends, waits, local = [], [], []
        k = 0
        for chip in chips:
            to = 2 * chip[0] + chip[1]
            for t in range(n):
                for l in range(DEPTH):
                    src = cp_in[t].at[l, pl.ds(to, 1)]
                    sends.append(remote(k, src, slots[t].at[l, pl.ds(my_chip, 1)], chip))
                    waits.append(remote(k, src, slots[t].at[l, pl.ds(to, 1)], chip))
                    k += 1
            for t in range(n_small):
                sends.append(remote(k, sm_in[t], sm_slots[t].at[my_chip], chip))
                waits.append(remote(k, sm_in[t], sm_slots[t].at[to], chip))
                k += 1
        for t in range(n):
            local.append(pltpu.make_async_copy(
                cp_in[t].at[:, pl.ds(my_chip, 1)], slots[t].at[:, pl.ds(my_chip, 1)], local_sems.at[t]))
        for t in range(n_small):
            local.append(pltpu.make_async_copy(sm_in[t], sm_slots[t].at[my_chip], local_sems.at[n + t]))
        for cp in sends + local:
            cp.start()
        for cp in waits + local:
            cp.wait()

    n_copies = 3 * (n * DEPTH + n_small)
    out_shape = tuple([SDS(a.shape, a.dtype) for a in big] + [SDS((N_CHIP,) + a.shape, a.dtype) for a in small])
    res = pl.pallas_call(
        body, name="exchange_chips", out_shape=out_shape,
        in_specs=[ANY] * (n + n_small), out_specs=tuple([ANY] * (n + n_small)),
        scratch_shapes=[pltpu.SemaphoreType.DMA((n_copies,)), pltpu.SemaphoreType.DMA((n_copies,)),
                        pltpu.SemaphoreType.DMA((n + n_small,))],
    )(*big, *small)
    return res[:n], res[n:]


def _row_block(rows):
    return rows if rows <= 256 else 256


def _add_own(tag, core, g, got):
    _, _, r, c = got.shape
    rb = _row_block(r)

    def body(core_ref, a_ref, b_ref, o_ref):
        o_ref[...] = (a_ref[...] + b_ref[...]).astype(o_ref.dtype)

    blk = (None, None, rb, c)
    return pl.pallas_call(
        body, name=f"add_{tag}", out_shape=SDS(got.shape, BF16),
        grid_spec=pltpu.PrefetchScalarGridSpec(
            num_scalar_prefetch=1, grid=(DEPTH, 4, r // rb),
            in_specs=[pl.BlockSpec(blk, lambda l, s, j, core: (l, 4 * core[0] + s, j, 0)),
                      pl.BlockSpec(blk, lambda l, s, j, core: (l, s, j, 0))],
            out_specs=pl.BlockSpec(blk, lambda l, s, j, core: (l, s, j, 0))),
        compiler_params=_cp(("parallel", "parallel", "parallel")),
    )(core, g, got)


def _add_lists(tag, own, got, grid=None, specs=None):
    n = len(own)

    def body(*refs):
        for a, b, o in zip(refs[:n], refs[n:2 * n], refs[2 * n:]):
            o[...] = a[...] + b[...]

    kw = {}
    if grid is not None:
        kw = dict(grid=grid, in_specs=list(specs) * 2, out_specs=tuple(specs),
                  compiler_params=_cp(("parallel",) * len(grid)))
    return pl.pallas_call(
        body, name=f"add_{tag}", out_shape=tuple(SDS(a.shape, a.dtype) for a in own), **kw)(*own, *got)


def _adamw_math(w, g, m, v):
    m = ADAM_B1 * m + (1.0 - ADAM_B1) * g
    v = ADAM_B2 * v + (1.0 - ADAM_B2) * (g * g)
    m_hat = m / (1.0 - ADAM_B1 ** ADAM_STEP)
    v_hat = v / (1.0 - ADAM_B2 ** ADAM_STEP)
    delta = -ADAM_LR * (m_hat / (jnp.sqrt(v_hat) + ADAM_EPS) + ADAM_WD * w)
    return delta, m, v


def _sum_slots_adamw(tag, slots, w, m, v):
    b, _, r, c = slots.shape
    rb = _row_block(r)

    def body(s_ref, w_ref, m_ref, v_ref, g_ref, d_ref, nm_ref, nv_ref):
        g = _sum4([s_ref[k].astype(F32) for k in range(N_CHIP)])
        delta, nm, nv = _adamw_math(w_ref[...], g, m_ref[...], v_ref[...])
        g_ref[...] = g
        d_ref[...] = delta
        nm_ref[...] = nm
        nv_ref[...] = nv

    spec = pl.BlockSpec((None, rb, c), lambda i, j: (i, j, 0))
    sspec = pl.BlockSpec((None, N_CHIP, rb, c), lambda i, j: (i, 0, j, 0))
    s = SDS((b, r, c), F32)
    return pl.pallas_call(
        body, name=f"adamw_{tag}", out_shape=(s, s, s, s),
        grid=(b, r // rb), in_specs=[sspec, spec, spec, spec], out_specs=(spec, spec, spec, spec),
        compiler_params=_cp(("parallel", "parallel")),
    )(slots, w, m, v)


def _adamw_small(tag, entries, grid=None):
    flat_in, in_specs, out_shape, out_specs, layout = [], [], [], [], []
    for slots, w, m, v, slot_spec, w_spec in entries:
        per_layer = isinstance(slots, (list, tuple))
        n_slot = len(slots) if per_layer else 1
        flat_in += (list(slots) if per_layer else [slots]) + [w, m, v]
        in_specs += [slot_spec] * n_slot + [w_spec] * 3
        out_shape += [SDS(w.shape, F32)] * 4
        out_specs += [w_spec] * 4
        layout.append((per_layer, n_slot))
    n_in = len(flat_in)

    def body(*refs):
        i, o = 0, n_in
        for per_layer, n_slot in layout:
            s_refs = refs[i:i + n_slot]
            w_ref, m_ref, v_ref = refs[i + n_slot:i + n_slot + 3]
            outs = refs[o:o + 4]
            if per_layer:
                for l, s_ref in enumerate(s_refs):
                    at = (slice(l, l + 1),) if len(w_ref.shape) == 2 else (l,)
                    g = _sum4([s_ref[k] for k in range(N_CHIP)])
                    res = (g,) + _adamw_math(w_ref[at], g, m_ref[at], v_ref[at])
                    for o_ref, val in zip(outs, res):
                        o_ref[at] = val
            else:
                g = _sum4([s_refs[0][k] for k in range(N_CHIP)])
                res = (g,) + _adamw_math(w_ref[...], g, m_ref[...], v_ref[...])
                for o_ref, val in zip(outs, res):
                    o_ref[...] = val
            i += n_slot + 3
            o += 4

    kw = {}
    if grid is not None:
        kw = dict(grid=grid, in_specs=in_specs, out_specs=tuple(out_specs),
                  compiler_params=_cp(("parallel",) * len(grid)))
    res = pl.pallas_call(body, name=f"adamw_{tag}", out_shape=tuple(out_shape), **kw)(*flat_in)
    return [tuple(res[4 * e:4 * e + 4]) for e in range(len(entries))]


def kernel(x, norm_g, w_in, b_in, ssm_log_dt, ssm_lam_re, ssm_lam_im, ssm_b_re, ssm_b_im, ssm_c_re, ssm_c_im, ssm_d, ssm_w_glu, ssm_b_glu, pool_w, pool_scale, w_branch_a, w_branch_b, w_out, final_norm_g, loss_target, m_norm_g, m_w_in, m_b_in, m_ssm_log_dt, m_ssm_lam_re, m_ssm_lam_im, m_ssm_b_re, m_ssm_b_im, m_ssm_c_re, m_ssm_c_im, m_ssm_d, m_ssm_w_glu, m_ssm_b_glu, m_pool_w, m_pool_scale, m_w_branch_a, m_w_branch_b, m_w_out, m_final_norm_g, v_norm_g, v_w_in, v_b_in, v_ssm_log_dt, v_ssm_lam_re, v_ssm_lam_im, v_ssm_b_re, v_ssm_b_im, v_ssm_c_re, v_ssm_c_im, v_ssm_d, v_ssm_w_glu, v_ssm_b_glu, v_pool_w, v_pool_scale, v_w_branch_a, v_w_branch_b, v_w_out, v_final_norm_g):
    weights = dict(norm_g=norm_g, w_in=w_in, b_in=b_in, ssm_log_dt=ssm_log_dt, ssm_lam_re=ssm_lam_re,
                   ssm_lam_im=ssm_lam_im, ssm_b_re=ssm_b_re, ssm_b_im=ssm_b_im, ssm_c_re=ssm_c_re,
                   ssm_c_im=ssm_c_im, ssm_d=ssm_d, ssm_w_glu=ssm_w_glu, ssm_b_glu=ssm_b_glu, pool_w=pool_w,
                   pool_scale=pool_scale, w_branch_a=w_branch_a, w_branch_b=w_branch_b, w_out=w_out,
                   final_norm_g=final_norm_g.reshape(1, D_MODEL))
    mom_m = dict(norm_g=m_norm_g, w_in=m_w_in, b_in=m_b_in, ssm_log_dt=m_ssm_log_dt, ssm_lam_re=m_ssm_lam_re,
                 ssm_lam_im=m_ssm_lam_im, ssm_b_re=m_ssm_b_re, ssm_b_im=m_ssm_b_im, ssm_c_re=m_ssm_c_re,
                 ssm_c_im=m_ssm_c_im, ssm_d=m_ssm_d, ssm_w_glu=m_ssm_w_glu, ssm_b_glu=m_ssm_b_glu,
                 pool_w=m_pool_w, pool_scale=m_pool_scale, w_branch_a=m_w_branch_a, w_branch_b=m_w_branch_b,
                 w_out=m_w_out, final_norm_g=m_final_norm_g.reshape(1, D_MODEL))
    mom_v = dict(norm_g=v_norm_g, w_in=v_w_in, b_in=v_b_in, ssm_log_dt=v_ssm_log_dt, ssm_lam_re=v_ssm_lam_re,
                 ssm_lam_im=v_ssm_lam_im, ssm_b_re=v_ssm_b_re, ssm_b_im=v_ssm_b_im, ssm_c_re=v_ssm_c_re,
                 ssm_c_im=v_ssm_c_im, ssm_d=v_ssm_d, ssm_w_glu=v_ssm_w_glu, ssm_b_glu=v_ssm_b_glu,
                 pool_w=v_pool_w, pool_scale=v_pool_scale, w_branch_a=v_w_branch_a, w_branch_b=v_w_branch_b,
                 w_out=v_w_out, final_norm_g=v_final_norm_g.reshape(1, D_MODEL))
    order = ["norm_g", "w_in", "b_in", "ssm_log_dt", "ssm_lam_re", "ssm_lam_im", "ssm_b_re", "ssm_b_im",
             "ssm_c_re", "ssm_c_im", "ssm_d", "ssm_w_glu", "ssm_b_glu", "pool_w", "pool_scale", "w_branch_a",
             "w_branch_b", "w_out", "final_norm_g"]
    big_names = ["w_in", "ssm_w_glu", "w_branch_a", "w_branch_b", "w_out"]

    wg_in, wg_glu, wg_a, wg_b, wg_out = _gather_weights([weights[n].astype(BF16)[None] for n in big_names])

    log_dt3 = ssm_log_dt.reshape(DEPTH, N_GROUP, 1)
    bt_re = ssm_b_re.transpose(0, 3, 1, 2)
    bt_im = ssm_b_im.transpose(0, 3, 1, 2)
    abar_re, abar_im, bbt_re, bbt_im = _s5_params(log_dt3, ssm_lam_re, ssm_lam_im, bt_re, bt_im)
    s5_args = (bbt_re, bbt_im, ssm_c_re, ssm_c_im, abar_re, abar_im, ssm_d)

    xs = [x.reshape(SEQ, D_MODEL)]
    saved = []
    for l in range(DEPTH):
        proj = _norm_proj(l, xs[l], norm_g, wg_in, b_in)
        s_re, s_im, y0 = _s5_fwd(l, proj, *s5_args)
        pooled = _pool_fwd(l, proj)
        xs.append(_mix_fwd(l, xs[l], proj, y0, pooled, wg_glu, ssm_b_glu, pool_w, pool_scale, wg_a, wg_b, wg_out))
        saved.append((proj, s_re, s_im, y0, pooled))

    dx, loss_part, g_final = _loss_head(xs[DEPTH], loss_target.reshape(SEQ, D_MODEL), weights["final_norm_g"])
    loss = lax.psum(loss_part[0, 0], ("x", "y", "c"))

    sm = [dict() for _ in range(DEPTH)]
    g_abar_re, g_abar_im, g_bbt_re, g_bbt_im = ([None] * DEPTH for _ in range(4))
    mix_big, gw_in = None, None
    for l in reversed(range(DEPTH)):
        proj, s_re, s_im, y0, pooled = saved[l]
        res = _mix_bwd(l, dx, proj, y0, pooled, wg_glu, ssm_b_glu, pool_w, pool_scale, wg_a, wg_b, wg_out, mix_big)
        dproj, dy0, dpooled = res[:3]
        mix_big = list(res[3:7])
        sm[l]["pool_w"], sm[l]["pool_scale"], sm[l]["ssm_b_glu"] = res[7:]
        dproj = _pool_bwd(l, dpooled, dproj)
        (dproj, g_bbt_re[l], g_bbt_im[l], sm[l]["ssm_c_re"], sm[l]["ssm_c_im"], g_abar_re[l], g_abar_im[l],
         sm[l]["ssm_d"]) = _s5_bwd(l, dy0, proj, s_re, s_im, *s5_args, dproj)
        gw_in, sm[l]["b_in"] = _proj_wgrad(l, xs[l], norm_g, dproj, gw_in)
        dx, sm[l]["norm_g"] = _proj_dgrad(l, dx, xs[l], norm_g, dproj, wg_in)
    grad_x = dx.reshape(1, SEQ, D_MODEL)
    gw_out, gw_a, gw_b, gw_glu = mix_big

    g_ld, g_lr, g_li, g_btr, g_bti = _s5_params_bwd(
        log_dt3, ssm_lam_re, ssm_lam_im, bt_re, bt_im, g_abar_re, g_abar_im, g_bbt_re, g_bbt_im)
    stacked = dict(ssm_log_dt=g_ld.reshape(DEPTH, N_GROUP), ssm_lam_re=g_lr, ssm_lam_im=g_li,
                   ssm_b_re=g_btr.transpose(0, 2, 3, 1), ssm_b_im=g_bti.transpose(0, 2, 3, 1),
                   final_norm_g=g_final)

    vec_names = ["norm_g", "b_in", "ssm_d", "ssm_b_glu", "pool_scale"]
    flat_names = ["final_norm_g", "ssm_log_dt", "ssm_lam_re", "ssm_lam_im"]
    mat_names = ["pool_w", "ssm_c_re", "ssm_c_im"]
    small_a = [sm[l][n] for n in vec_names for l in range(DEPTH)] + [stacked[n] for n in flat_names]
    small_b = ([sm[l]["pool_w"] for l in range(DEPTH)]
               + [sm[l][n].reshape(-1, 128) for n in ("ssm_c_re", "ssm_c_im") for l in range(DEPTH)]
               + [stacked["ssm_b_re"].reshape(-1, 128), stacked["ssm_b_im"].reshape(-1, 128)])
    n_a = len(small_a)
    big_part = [gw_in, gw_glu, gw_a, gw_b, gw_out]
    got, got_small = _exchange_sibling(big_part, small_a + small_b)
    core = lax.axis_index("c").astype(jnp.int32).reshape(1)
    chip_big = [_add_own(f"chip_{n}", core, a, b) for n, a, b in zip(big_names, big_part, got)]
    chip_a = _add_lists("chip_small_a", small_a, got_small[:n_a])
    pw_spec = pl.BlockSpec((1, POOL_GROUP, POOL_GROUP), lambda j: (j, 0, 0))
    c_spec = pl.BlockSpec((small_b[2].shape[0] // N_CHUNK, 128), lambda j: (j, 0))
    b_spec = pl.BlockSpec((small_b[6].shape[0] // N_CHUNK, 128), lambda j: (j, 0))
    chip_b = _add_lists("chip_small_b", small_b, got_small[n_a:], grid=(N_CHUNK,),
                        specs=[pw_spec] * 2 + [c_spec] * 4 + [b_spec] * 2)
    slots_big, slots_small = _exchange_chips(chip_big, list(chip_a) + list(chip_b))
    slots_small = list(slots_small)
    for i in range(4):
        slots_small[n_a + 2 + i] = slots_small[n_a + 2 + i].reshape((N_CHIP,) + ssm_c_re.shape[1:])
    for i in range(2):
        slots_small[n_a + 6 + i] = slots_small[n_a + 6 + i].reshape((N_CHIP,) + ssm_b_re.shape)

    res = {}
    for n, slots in zip(big_names, slots_big):
        res[n] = _sum_slots_adamw(n, slots, weights[n], mom_m[n], mom_v[n])
    entries_a, k = [], 0
    for n in vec_names:
        entries_a.append((slots_small[k:k + DEPTH], weights[n], mom_m[n], mom_v[n], None, None))
        k += DEPTH
    for n in flat_names:
        entries_a.append((slots_small[k], weights[n], mom_m[n], mom_v[n], None, None))
        k += 1
    out_a = _adamw_small("small_a", entries_a)
    for n, r in zip(vec_names + flat_names, out_a):
        res[n] = r
    res["final_norm_g"] = tuple(a.reshape(D_MODEL) for a in res["final_norm_g"])
    pw_s = pl.BlockSpec((N_CHIP, 1, POOL_GROUP, POOL_GROUP), lambda j: (0, j, 0, 0))
    pw_w = pl.BlockSpec((DEPTH, 1, POOL_GROUP, POOL_GROUP), lambda j: (0, j, 0, 0))
    c_s = pl.BlockSpec((N_CHIP, CH_G, GROUP_W, STATE), lambda j: (0, j, 0, 0))
    c_w = pl.BlockSpec((DEPTH, CH_G, GROUP_W, STATE), lambda j: (0, j, 0, 0))
    b_s = pl.BlockSpec((N_CHIP, DEPTH, CH_G, STATE, GROUP_W), lambda j: (0, 0, j, 0, 0))
    b_w = pl.BlockSpec((DEPTH, CH_G, STATE, GROUP_W), lambda j: (0, j, 0, 0))
    entries_b = []
    for n, s_spec, w_spec in (("pool_w", pw_s, pw_w), ("ssm_c_re", c_s, c_w), ("ssm_c_im", c_s, c_w)):
        entries_b.append((slots_small[k:k + DEPTH], weights[n], mom_m[n], mom_v[n], s_spec, w_spec))
        k += DEPTH
    for n in ("ssm_b_re", "ssm_b_im"):
        entries_b.append((slots_small[k], weights[n], mom_m[n], mom_v[n], b_s, b_w))
        k += 1
    out_b = _adamw_small("small_b", entries_b, grid=(N_CHUNK,))
    for n, r in zip(mat_names + ["ssm_b_re", "ssm_b_im"], out_b):
        res[n] = r

    outs = [loss, grad_x]
    for i in range(4):
        outs += [res[n][i] for n in order]
    return tuple(outs)
```

```python
import math

import jax
import jax.numpy as jnp
from jax import lax
from jax.experimental import pallas as pl
from jax.experimental.pallas import tpu as pltpu

F32 = jnp.float32
BF16 = jnp.bfloat16

SEQ = 2048
D_MODEL = 1024
N_IN = 4096
WIDTH = 512
N_GROUP = 32
GROUP_W = 16
STATE = 64
N_STATE = N_GROUP * STATE
N_CHUNK = 4
CH_G = N_GROUP // N_CHUNK
CH_W = WIDTH // N_CHUNK
CH_S = N_STATE // N_CHUNK
N_DEV = 8
N_CHIP = 4
POOL_WINDOWS = (2, 4, 8, 16)
POOL_GROUP = 128
EPS = 1e-6
DEPTH = 2

ADAM_LR = 0.001
ADAM_B1 = 0.9
ADAM_B2 = 0.999
ADAM_EPS = 1e-08
ADAM_WD = 0.01
ADAM_STEP = 10

TILE_M = 256
ROW_BLK = 512
VMEM_LIMIT = 48 * 1024 * 1024
VMEM_LIMIT_BIG = 60 * 1024 * 1024
MESH = pl.DeviceIdType.MESH
ANY = pl.BlockSpec(memory_space=pl.ANY)

GELU_C = math.sqrt(2.0 / math.pi)
GELU_A = 0.044715

SDS = jax.ShapeDtypeStruct


def _cp(sem=None, limit=VMEM_LIMIT):
    return pltpu.CompilerParams(dimension_semantics=sem, vmem_limit_bytes=limit)


def _dot(a, b):
    return jnp.dot(a, b, preferred_element_type=F32)


def _dot_nt(a, b):
    return lax.dot_general(a, b, (((1,), (1,)), ((), ())), preferred_element_type=F32)


def _dot_tn(a, b):
    return lax.dot_general(a, b, (((0,), (0,)), ((), ())), preferred_element_type=F32)


def _sig(x):
    return jax.nn.sigmoid(x)


def _rms(x):
    rs = lax.rsqrt(jnp.mean(x * x, axis=-1, keepdims=True) + EPS)
    return rs, x * rs


def _slot(n):
    return 4 * (n % 2) + n // 2


def _const(shape):
    n = len(shape)
    return pl.BlockSpec(shape, lambda *_: (0,) * n)


def _sum4(p):
    return (p[0] + p[1]) + (p[2] + p[3])


def _s5_param_fn(log_dt, lam_re, lam_im, bt_re, bt_im):
    dt = jnp.exp(log_dt)
    mag = jnp.exp(lam_re * dt)
    ang = lam_im * dt
    abar_re = mag * jnp.cos(ang)
    abar_im = mag * jnp.sin(ang)
    num_re = abar_re - 1.0
    num_im = abar_im
    den = lam_re * lam_re + lam_im * lam_im
    coef_re = (num_re * lam_re + num_im * lam_im) / den
    coef_im = (num_im * lam_re - num_re * lam_im) / den
    bbar_re = coef_re[:, None] * bt_re - coef_im[:, None] * bt_im
    bbar_im = coef_re[:, None] * bt_im + coef_im[:, None] * bt_re
    return abar_re, abar_im, bbar_re, bbar_im


def _s5_params(log_dt, lam_re, lam_im, bt_re, bt_im):
    def body(ld, lr, li, br, bi, o_ar, o_ai, o_br, o_bi):
        ar, ai, bbr, bbi = _s5_param_fn(ld[...], lr[...], li[...], br[...], bi[...])
        o_ar[...] = ar
        o_ai[...] = ai
        o_br[...] = bbr
        o_bi[...] = bbi

    return pl.pallas_call(
        body, name="s5_params",
        out_shape=(SDS(lam_re.shape, F32), SDS(lam_re.shape, F32), SDS(bt_re.shape, F32), SDS(bt_re.shape, F32)),
    )(log_dt, lam_re, lam_im, bt_re, bt_im)


def _s5_params_bwd(log_dt, lam_re, lam_im, bt_re, bt_im, g_ar, g_ai, g_br, g_bi):
    def body(ld, lr, li, br, bi, car0, car1, cai0, cai1, cbr0, cbr1, cbi0, cbi1, o_ld, o_lr, o_li, o_br, o_bi):
        _, vjp = jax.vjp(_s5_param_fn, ld[...], lr[...], li[...], br[...], bi[...])
        both = lambda a, b: jnp.stack([a[...], b[...]], axis=0)
        d_ld, d_lr, d_li, d_br, d_bi = vjp((both(car0, car1), both(cai0, cai1), both(cbr0, cbr1), both(cbi0, cbi1)))
        o_ld[...] = d_ld
        o_lr[...] = d_lr
        o_li[...] = d_li
        o_br[...] = d_br
        o_bi[...] = d_bi

    return pl.pallas_call(
        body, name="s5_params_bwd",
        out_shape=(SDS(log_dt.shape, F32), SDS(lam_re.shape, F32), SDS(lam_re.shape, F32),
                   SDS(bt_re.shape, F32), SDS(bt_re.shape, F32)),
    )(log_dt, lam_re, lam_im, bt_re, bt_im, *g_ar, *g_ai, *g_br, *g_bi)


def _norm_proj(layer, x, norm_g, wg_in, b_in):
    def body(x_ref, g_ref, w_ref, b_ref, o_ref):
        _, xn = _rms(x_ref[...])
        h = (xn * g_ref[layer:layer + 1, :]).astype(BF16)
        for k in range(N_DEV):
            cols = slice(k * WIDTH, (k + 1) * WIDTH)
            o_ref[:, cols] = _dot(h, w_ref[k]) + b_ref[layer:layer + 1, cols]

    return pl.pallas_call(
        body, name=f"norm_proj_l{layer}",
        out_shape=SDS((SEQ, N_IN), F32),
        grid=(SEQ // TILE_M,),
        in_specs=[pl.BlockSpec((TILE_M, D_MODEL), lambda i: (i, 0)),
                  _const((DEPTH, D_MODEL)),
                  pl.BlockSpec((N_DEV, None, D_MODEL, WIDTH), lambda i: (0, layer, 0, 0)),
                  _const((DEPTH, N_IN))],
        out_specs=pl.BlockSpec((TILE_M, N_IN), lambda i: (i, 0)),
        compiler_params=_cp(("parallel",)),
    )(x, norm_g, wg_in, b_in)


def _scan_tile_rows():
    return lax.broadcasted_iota(jnp.int32, (8, CH_S), 0)


def _s5_param_specs(layer):
    bt = lambda: pl.BlockSpec((None, GROUP_W, CH_G, STATE), lambda j: (layer, 0, j, 0))
    cc = lambda: pl.BlockSpec((None, CH_G, GROUP_W, STATE), lambda j: (layer, j, 0, 0))
    ab = lambda: pl.BlockSpec((None, CH_G, STATE), lambda j: (layer, j, 0))
    return [bt(), bt(), cc(), cc(), ab(), ab(), pl.BlockSpec((DEPTH, CH_W), lambda j: (0, j))]


def _s5_param_scratch():
    return [pltpu.VMEM((CH_W, CH_S), BF16)] * 4 + [pltpu.VMEM((1, CH_S), F32)] * 2


def _s5_fill(btre_ref, btim_ref, cre_ref, cim_ref, are_ref, aim_ref, bdre, bdim, ctre, ctim, arow, airow):
    for m in (bdre, bdim, ctre, ctim):
        m[...] = jnp.zeros_like(m)
    for g in range(CH_G):
        rows = slice(g * GROUP_W, (g + 1) * GROUP_W)
        cols = slice(g * STATE, (g + 1) * STATE)
        bdre[rows, cols] = btre_ref[:, g, :].astype(BF16)
        bdim[rows, cols] = btim_ref[:, g, :].astype(BF16)
        ctre[rows, cols] = cre_ref[g].astype(BF16)
        ctim[rows, cols] = cim_ref[g].astype(BF16)
        arow[:, cols] = are_ref[g:g + 1, :]
        airow[:, cols] = aim_ref[g:g + 1, :]


def _s5_fwd(layer, proj, bbt_re, bbt_im, c_re, c_im, abar_re, abar_im, d_skip):
    def body(u_ref, btre_ref, btim_ref, cre_ref, cim_ref, are_ref, aim_ref, d_ref,
             sre_ref, sim_ref, y_ref, bdre, bdim, ctre, ctim, arow, airow):
        _s5_fill(btre_ref, btim_ref, cre_ref, cim_ref, are_ref, aim_ref, bdre, bdim, ctre, ctim, arow, airow)
        for rb in range(SEQ // ROW_BLK):
            rows = pl.ds(rb * ROW_BLK, ROW_BLK)
            ub = u_ref[rows, :].astype(BF16)
            sre_ref[rows, :] = _dot(ub, bdre[...])
            sim_ref[rows, :] = _dot(ub, bdim[...])
        ar = arow[...]
        ai = airow[...]
        row_id = _scan_tile_rows()

        def step(i, carry):
            sr, si = carry
            base = pl.multiple_of(i * 8, 8)
            tr = sre_ref[pl.ds(base, 8), :]
            ti = sim_ref[pl.ds(base, 8), :]
            outr, outi = tr, ti
            for r in range(8):
                nsr = ar * sr - ai * si + tr[r:r + 1, :]
                nsi = ar * si + ai * sr + ti[r:r + 1, :]
                outr = jnp.where(row_id == r, nsr, outr)
                outi = jnp.where(row_id == r, nsi, outi)
                sr, si = nsr, nsi
            sre_ref[pl.ds(base, 8), :] = outr
            sim_ref[pl.ds(base, 8), :] = outi
            return sr, si

        zero = jnp.zeros((1, CH_S), F32)
        lax.fori_loop(0, SEQ // 8, step, (zero, zero))
        d = d_ref[layer:layer + 1, :]
        for rb in range(SEQ // ROW_BLK):
            rows = pl.ds(rb * ROW_BLK, ROW_BLK)
            y = (_dot_nt(sre_ref[rows, :].astype(BF16), ctre[...])
                 - _dot_nt(sim_ref[rows, :].astype(BF16), ctim[...]))
            y_ref[rows, :] = y + d * u_ref[rows, :]

    return pl.pallas_call(
        body, name=f"s5_fwd_l{layer}",
        out_shape=(SDS((SEQ, N_STATE), F32), SDS((SEQ, N_STATE), F32), SDS((SEQ, WIDTH), F32)),
        grid=(N_CHUNK,),
        in_specs=[pl.BlockSpec((SEQ, CH_W), lambda j: (0, j))] + _s5_param_specs(layer),
        out_specs=(pl.BlockSpec((SEQ, CH_S), lambda j: (0, j)),
                   pl.BlockSpec((SEQ, CH_S), lambda j: (0, j)),
                   pl.BlockSpec((SEQ, CH_W), lambda j: (0, j))),
        scratch_shapes=_s5_param_scratch(),
        compiler_params=_cp(("parallel",)),
    )(proj, bbt_re, bbt_im, c_re, c_im, abar_re, abar_im, d_skip)


def _pool_counts(win):
    t = lax.broadcasted_iota(jnp.int32, (SEQ, POOL_GROUP), 0)
    return t, jnp.minimum(t + 1, win).astype(F32)


def _pool_fwd(layer, proj):
    def body(u_ref, o_ref):
        for gi, win in enumerate(POOL_WINDOWS):
            cols = slice(gi * POOL_GROUP, (gi + 1) * POOL_GROUP)
            u = u_ref[:, cols]
            t, count = _pool_counts(win)
            acc = u
            k = 1
            while k < win:
                acc = acc + jnp.where(t >= k, pltpu.roll(acc, k, 0), 0.0)
                k *= 2
            o_ref[:, cols] = acc / count - u

    return pl.pallas_call(
        body, name=f"pool_fwd_l{layer}",
        out_shape=SDS((SEQ, WIDTH), F32),
        grid=(1,),
        in_specs=[pl.BlockSpec((SEQ, WIDTH), lambda i: (0, 2))],
        out_specs=pl.BlockSpec((SEQ, WIDTH), lambda i: (0, 0)),
        compiler_params=_cp(("arbitrary",)),
    )(proj)


def _gelu_parts(y0):
    t = jnp.tanh(GELU_C * (y0 + GELU_A * (y0 * y0 * y0)))
    return t, 0.5 * y0 * (1.0 + t)


def _mix_forward(layer, p_ref, y0_ref, pooled_ref, wglu_ref, bglu_ref, pw_ref, scale_ref, wa_ref, wb_ref):
    za = p_ref[:, WIDTH:2 * WIDTH]
    zb = p_ref[:, 3 * WIDTH:4 * WIDTH]
    ga = p_ref[:, 4 * WIDTH:4 * WIDTH + D_MODEL]
    gb = p_ref[:, 4 * WIDTH + D_MODEL:]
    y0 = y0_ref[...]
    t, y1 = _gelu_parts(y0)
    y1b = y1.astype(BF16)
    q = _dot(y1b, wglu_ref[...].reshape(WIDTH, WIDTH)) + bglu_ref[layer:layer + 1, :]
    sq = _sig(q)
    y2 = y1 * sq
    sza = _sig(za)
    silu_za = za * sza
    ya = y2 * silu_za
    pooled = pooled_ref[...]
    mixed = jnp.concatenate(
        [_dot(pooled[:, g * POOL_GROUP:(g + 1) * POOL_GROUP].astype(BF16), pw_ref[g].astype(BF16))
         for g in range(len(POOL_WINDOWS))], axis=1)
    szb = _sig(zb)
    silu_zb = zb * szb
    scale = scale_ref[layer:layer + 1, :]
    ms = mixed * scale
    yb = ms * silu_zb
    yab = ya.astype(BF16)
    ybb = yb.astype(BF16)
    ma = jnp.concatenate([_dot(yab, wa_ref[k]) for k in range(N_DEV)], axis=1)
    mb = jnp.concatenate([_dot(ybb, wb_ref[k]) for k in range(N_DEV)], axis=1)
    sga = _sig(ga)
    sgb = _sig(gb)
    merged = sga * ma + sgb * mb
    return dict(za=za, zb=zb, y0=y0, t=t, y1=y1, y1b=y1b, sq=sq, y2=y2, sza=sza, silu_za=silu_za,
                pooled=pooled, mixed=mixed, szb=szb, silu_zb=silu_zb, scale=scale, ms=ms, yab=yab, ybb=ybb,
                ma=ma, mb=mb, sga=sga, sgb=sgb, merged=merged)


def _mix_weight_specs(layer):
    return [pl.BlockSpec((N_DEV, None, WIDTH // N_DEV, WIDTH), lambda i: (0, layer, 0, 0)),
            _const((DEPTH, WIDTH)),
            pl.BlockSpec((None, 4, POOL_GROUP, POOL_GROUP), lambda i: (layer, 0, 0, 0)),
            _const((DEPTH, WIDTH)),
            pl.BlockSpec((N_DEV, None, WIDTH, D_MODEL // N_DEV), lambda i: (0, layer, 0, 0)),
            pl.BlockSpec((N_DEV, None, WIDTH, D_MODEL // N_DEV), lambda i: (0, layer, 0, 0)),
            pl.BlockSpec((N_DEV, None, D_MODEL // N_DEV, D_MODEL), lambda i: (0, layer, 0, 0))]


def _mix_fwd(layer, x, proj, y0, pooled, wg_glu, b_glu, pool_w, pool_scale, wg_a, wg_b, wg_out):
    def body(x_ref, p_ref, y0_ref, pooled_ref, wglu_ref, bglu_ref, pw_ref, scale_ref, wa_ref, wb_ref,
             wout_ref, o_ref):
        f = _mix_forward(layer, p_ref, y0_ref, pooled_ref, wglu_ref, bglu_ref, pw_ref, scale_ref, wa_ref, wb_ref)
        wout = wout_ref[...].reshape(D_MODEL, D_MODEL)
        o_ref[...] = x_ref[...] + _dot(f["merged"].astype(BF16), wout)

    return pl.pallas_call(
        body, name=f"mix_fwd_l{layer}",
        out_shape=SDS((SEQ, D_MODEL), F32),
        grid=(SEQ // TILE_M,),
        in_specs=[pl.BlockSpec((TILE_M, D_MODEL), lambda i: (i, 0)),
                  pl.BlockSpec((TILE_M, N_IN), lambda i: (i, 0)),
                  pl.BlockSpec((TILE_M, WIDTH), lambda i: (i, 0)),
                  pl.BlockSpec((TILE_M, WIDTH), lambda i: (i, 0))] + _mix_weight_specs(layer),
        out_specs=pl.BlockSpec((TILE_M, D_MODEL), lambda i: (i, 0)),
        compiler_params=_cp(("parallel",)),
    )(x, proj, y0, pooled, wg_glu, b_glu, pool_w, pool_scale, wg_a, wg_b, wg_out)


def _loss_head(x, target, final_g):
    def body(x_ref, t_ref, g_ref, dx_ref, loss_ref, gg_ref):
        @pl.when(pl.program_id(0) == 0)
        def _():
            loss_ref[...] = jnp.zeros_like(loss_ref)
            gg_ref[...] = jnp.zeros_like(gg_ref)

        g = g_ref[...]
        rs, xn = _rms(x_ref[...])
        err = xn * g - t_ref[...]
        loss_ref[...] += 0.5 * jnp.sum(jnp.mean(err * err, axis=-1, keepdims=True), axis=0, keepdims=True)
        dy = err * (1.0 / D_MODEL)
        gg_ref[...] += jnp.sum(dy * xn, axis=0, keepdims=True)
        dxn = dy * g
        dx_ref[...] = rs * (dxn - xn * jnp.mean(dxn * xn, axis=-1, keepdims=True))

    return pl.pallas_call(
        body, name="loss_head",
        out_shape=(SDS((SEQ, D_MODEL), F32), SDS((1, 1), F32), SDS((1, D_MODEL), F32)),
        grid=(SEQ // TILE_M,),
        in_specs=[pl.BlockSpec((TILE_M, D_MODEL), lambda i: (i, 0)),
                  pl.BlockSpec((TILE_M, D_MODEL), lambda i: (i, 0)),
                  _const((1, D_MODEL))],
        out_specs=(pl.BlockSpec((TILE_M, D_MODEL), lambda i: (i, 0)), _const((1, 1)), _const((1, D_MODEL))),
        compiler_params=_cp(("arbitrary",)),
    )(x, target, final_g)


def _big_shapes():
    return dict(w_out=(DEPTH, N_DEV, D_MODEL // N_DEV, D_MODEL), w_branch_a=(DEPTH, N_DEV, WIDTH, D_MODEL // N_DEV),
                w_branch_b=(DEPTH, N_DEV, WIDTH, D_MODEL // N_DEV), ssm_w_glu=(DEPTH, N_DEV, WIDTH // N_DEV, WIDTH),
                w_in=(DEPTH, N_DEV, D_MODEL, WIDTH))


def _mix_bwd(layer, dx_next, proj, y0, pooled, wg_glu, b_glu, pool_w, pool_scale, wg_a, wg_b, wg_out, prev,
             carry=None):
    n_k = N_DEV
    n_prev = 0 if prev is None else len(prev)

    def body(*refs):
        (dx_ref, p_ref, y0_ref, pooled_ref, wglu_ref, bglu_ref, pw_ref, scale_ref, wa_ref, wb_ref,
         wout_ref) = refs[:11]
        (dproj_ref, dy0_ref, dpooled_ref, gwout_ref, gwa_ref, gwb_ref, gwglu_ref, gpw_ref,
         gscale_ref, gbglu_ref) = refs[11 + n_prev:]

        @pl.when(pl.program_id(0) == 0)
        def _():
            for r in (gwout_ref, gwa_ref, gwb_ref, gwglu_ref, gpw_ref, gscale_ref, gbglu_ref):
                r[...] = jnp.zeros_like(r)

        f = _mix_forward(layer, p_ref, y0_ref, pooled_ref, wglu_ref, bglu_ref, pw_ref, scale_ref, wa_ref, wb_ref)
        wglu = wglu_ref[...].reshape(WIDTH, WIDTH)
        wout = wout_ref[...].reshape(D_MODEL, D_MODEL)
        blk = D_MODEL // n_k
        dxb = dx_ref[...].astype(BF16)
        dmerged = _dot_nt(dxb, wout)
        gwout = _dot_tn(f["merged"].astype(BF16), dxb)
        for k in range(n_k):
            gwout_ref[_slot(k)] += gwout[k * blk:(k + 1) * blk, :]
        dma = dmerged * f["sga"]
        dmb = dmerged * f["sgb"]
        dga = dmerged * f["ma"] * f["sga"] * (1.0 - f["sga"])
        dgb = dmerged * f["mb"] * f["sgb"] * (1.0 - f["sgb"])
        dmab = dma.astype(BF16)
        dmbb = dmb.astype(BF16)
        dya = jnp.zeros((TILE_M, WIDTH), F32)
        dyb = jnp.zeros((TILE_M, WIDTH), F32)
        for k in range(n_k):
            da_k = dmab[:, k * blk:(k + 1) * blk]
            db_k = dmbb[:, k * blk:(k + 1) * blk]
            dya = dya + _dot_nt(da_k, wa_ref[k])
            dyb = dyb + _dot_nt(db_k, wb_ref[k])
            gwa_ref[_slot(k)] += _dot_tn(f["yab"], da_k)
            gwb_ref[_slot(k)] += _dot_tn(f["ybb"], db_k)
        zb, szb = f["zb"], f["szb"]
        dzb = dyb * f["ms"] * (szb * (1.0 + zb * (1.0 - szb)))
        dms = dyb * f["silu_zb"]
        gscale_ref[...] += jnp.sum(dms * f["mixed"], axis=0, keepdims=True)
        dmixed = (dms * f["scale"]).astype(BF16)
        pooled = f["pooled"]
        for g in range(len(POOL_WINDOWS)):
            cols = slice(g * POOL_GROUP, (g + 1) * POOL_GROUP)
            dpooled_ref[:, cols] = _dot_nt(dmixed[:, cols], pw_ref[g].astype(BF16))
            gpw_ref[g] += _dot_tn(pooled[:, cols].astype(BF16), dmixed[:, cols])
        za, sza = f["za"], f["sza"]
        dza = dya * f["y2"] * (sza * (1.0 + za * (1.0 - sza)))
        dy2 = dya * f["silu_za"]
        sq = f["sq"]
        dq = dy2 * f["y1"] * sq * (1.0 - sq)
        dqb = dq.astype(BF16)
        dy1 = dy2 * sq + _dot_nt(dqb, wglu)
        gwglu = _dot_tn(f["y1b"], dqb)
        rblk = WIDTH // n_k
        for k in range(n_k):
            gwglu_ref[_slot(k)] += gwglu[k * rblk:(k + 1) * rblk, :]
        gbglu_ref[...] += jnp.sum(dq, axis=0, keepdims=True)
        y0, t = f["y0"], f["t"]
        dgelu = 0.5 * (1.0 + t) + 0.5 * y0 * (1.0 - t * t) * (GELU_C * (1.0 + 3.0 * GELU_A * y0 * y0))
        dy0_ref[...] = dy1 * dgelu
        zeros = jnp.zeros((TILE_M, WIDTH), BF16)
        dproj_ref[:, 0:WIDTH] = zeros
        dproj_ref[:, WIDTH:2 * WIDTH] = dza.astype(BF16)
        dproj_ref[:, 2 * WIDTH:3 * WIDTH] = zeros
        dproj_ref[:, 3 * WIDTH:4 * WIDTH] = dzb.astype(BF16)
        dproj_ref[:, 4 * WIDTH:4 * WIDTH + D_MODEL] = dga.astype(BF16)
        dproj_ref[:, 4 * WIDTH + D_MODEL:] = dgb.astype(BF16)

    tile = lambda w: pl.BlockSpec((TILE_M, w), lambda i: (i, 0))
    shapes = _big_shapes()
    big = ["w_out", "w_branch_a", "w_branch_b", "ssm_w_glu"]
    slab = lambda n: pl.BlockSpec((None,) + shapes[n][1:], lambda i: (layer, 0, 0, 0))
    args = [dx_next, proj, y0, pooled, wg_glu, b_glu, pool_w, pool_scale, wg_a, wg_b, wg_out]
    return _pcall(
        body, name=f"mix_bwd_l{layer}",
        out_shape=(SDS((SEQ, N_IN), BF16), SDS((SEQ, WIDTH), F32), SDS((SEQ, WIDTH), F32))
        + tuple(SDS(shapes[n], F32) for n in big)
        + (SDS((4, POOL_GROUP, POOL_GROUP), F32), SDS((1, WIDTH), F32), SDS((1, WIDTH), F32)),
        grid=(SEQ // TILE_M,),
        in_specs=[tile(D_MODEL), tile(N_IN), tile(WIDTH), tile(WIDTH)] + _mix_weight_specs(layer) + [ANY] * n_prev,
        out_specs=(tile(N_IN), tile(WIDTH), tile(WIDTH)) + tuple(slab(n) for n in big)
        + (_const((4, POOL_GROUP, POOL_GROUP)), _const((1, WIDTH)), _const((1, WIDTH))),
        args=args + list(prev or ()),
        aliases={len(args) + i: 3 + i for i in range(n_prev)},
        sem=("arbitrary",), limit=VMEM_LIMIT_BIG, carry=carry)


def _pool_bwd(layer, dpooled, dproj):
    def body(dp_ref, _, o_ref):
        for gi, win in enumerate(POOL_WINDOWS):
            cols = slice(gi * POOL_GROUP, (gi + 1) * POOL_GROUP)
            dp = dp_ref[:, cols]
            t, count = _pool_counts(win)
            e = dp / count
            acc = e
            k = 1
            while k < win:
                acc = acc + jnp.where(t < SEQ - k, pltpu.roll(acc, SEQ - k, 0), 0.0)
                k *= 2
            o_ref[:, cols] = (acc - dp).astype(BF16)

    return pl.pallas_call(
        body, name=f"pool_bwd_l{layer}",
        out_shape=SDS((SEQ, N_IN), BF16),
        grid=(1,),
        in_specs=[pl.BlockSpec((SEQ, WIDTH), lambda i: (0, 0)), ANY],
        out_specs=pl.BlockSpec((SEQ, WIDTH), lambda i: (0, 2)),
        input_output_aliases={1: 0},
        compiler_params=_cp(("arbitrary",)),
    )(dpooled, dproj)


def _s5_bwd(layer, dy0, proj, s_re, s_im, bbt_re, bbt_im, c_re, c_im, abar_re, abar_im, d_skip, dproj,
            carry=None):
    def body(dy_ref, u_ref, sre_ref, sim_ref, btre_ref, btim_ref, cre_ref, cim_ref, are_ref, aim_ref,
             d_ref, _, du_ref, gbre_ref, gbim_ref, gcre_ref, gcim_ref, gare_ref, gaim_ref, gd_ref,
             lre_ref, lim_ref, bdre, bdim, ctre, ctim, arow, airow):
        _s5_fill(btre_ref, btim_ref, cre_ref, cim_ref, are_ref, aim_ref, bdre, bdim, ctre, ctim, arow, airow)
        n_rb = SEQ // ROW_BLK
        gcre = jnp.zeros((CH_W, CH_S), F32)
        gcim = jnp.zeros((CH_W, CH_S), F32)
        for rb in range(n_rb):
            rows = pl.ds(rb * ROW_BLK, ROW_BLK)
            dyb = dy_ref[rows, :].astype(BF16)
            lre_ref[rows, :] = _dot(dyb, ctre[...])
            lim_ref[rows, :] = -_dot(dyb, ctim[...])
            gcre = gcre + _dot_tn(dyb, sre_ref[rows, :].astype(BF16))
            gcim = gcim - _dot_tn(dyb, sim_ref[rows, :].astype(BF16))
        ar = arow[...]
        ai = airow[...]
        row_id = _scan_tile_rows()

        def step(n, carry):
            lr, li = carry
            base = pl.multiple_of((SEQ // 8 - 1 - n) * 8, 8)
            tr = lre_ref[pl.ds(base, 8), :]
            ti = lim_ref[pl.ds(base, 8), :]
            outr, outi = tr, ti
            for r in range(7, -1, -1):
                nlr = ar * lr + ai * li + tr[r:r + 1, :]
                nli = ar * li - ai * lr + ti[r:r + 1, :]
                outr = jnp.where(row_id == r, nlr, outr)
                outi = jnp.where(row_id == r, nli, outi)
                lr, li = nlr, nli
            lre_ref[pl.ds(base, 8), :] = outr
            lim_ref[pl.ds(base, 8), :] = outi
            return lr, li

        zero = jnp.zeros((1, CH_S), F32)
        lax.fori_loop(0, SEQ // 8, step, (zero, zero))

        gare = jnp.zeros((1, CH_S), F32)
        gaim = jnp.zeros((1, CH_S), F32)
        gbre = jnp.zeros((CH_W, CH_S), F32)
        gbim = jnp.zeros((CH_W, CH_S), F32)
        gd = jnp.zeros((1, CH_W), F32)
        d = d_ref[layer:layer + 1, :]
        first = lax.broadcasted_iota(jnp.int32, (ROW_BLK, CH_S), 0) == 0
        for rb in range(n_rb):
            rows = pl.ds(rb * ROW_BLK, ROW_BLK)
            lr = lre_ref[rows, :]
            li = lim_ref[rows, :]
            if rb == 0:
                prev_r = jnp.zeros((1, CH_S), F32)
                prev_i = jnp.zeros((1, CH_S), F32)
            else:
                prev_r = sre_ref[pl.ds(rb * ROW_BLK - 1, 1), :]
                prev_i = sim_ref[pl.ds(rb * ROW_BLK - 1, 1), :]
            spr = jnp.where(first, prev_r, pltpu.roll(sre_ref[rows, :], 1, 0))
            spi = jnp.where(first, prev_i, pltpu.roll(sim_ref[rows, :], 1, 0))
            gare = gare + jnp.sum(lr * spr + li * spi, axis=0, keepdims=True)
            gaim = gaim + jnp.sum(li * spr - lr * spi, axis=0, keepdims=True)
            lrb = lr.astype(BF16)
            lib = li.astype(BF16)
            u = u_ref[rows, :]
            ub = u.astype(BF16)
            dy = dy_ref[rows, :]
            du = dy * d + _dot_nt(lrb, bdre[...]) + _dot_nt(lib, bdim[...])
            du_ref[rows, :] = du.astype(BF16)
            gbre = gbre + _dot_tn(ub, lrb)
            gbim = gbim + _dot_tn(ub, lib)
            gd = gd + jnp.sum(dy * u, axis=0, keepdims=True)
        gd_ref[...] = gd
        for g in range(CH_G):
            rows = slice(g * GROUP_W, (g + 1) * GROUP_W)
            cols = slice(g * STATE, (g + 1) * STATE)
            gcre_ref[g] = gcre[rows, cols]
            gcim_ref[g] = gcim[rows, cols]
            gbre_ref[:, g, :] = gbre[rows, cols]
            gbim_ref[:, g, :] = gbim[rows, cols]
            gare_ref[g:g + 1, :] = gare[:, cols]
            gaim_ref[g:g + 1, :] = gaim[:, cols]

    chunk_w = lambda: pl.BlockSpec((SEQ, CH_W), lambda j: (0, j))
    chunk_s = lambda: pl.BlockSpec((SEQ, CH_S), lambda j: (0, j))
    gbt = lambda: pl.BlockSpec((GROUP_W, CH_G, STATE), lambda j: (0, j, 0))
    gcc = lambda: pl.BlockSpec((CH_G, GROUP_W, STATE), lambda j: (j, 0, 0))
    gab = lambda: pl.BlockSpec((CH_G, STATE), lambda j: (j, 0))
    return _pcall(
        body, name=f"s5_bwd_l{layer}",
        out_shape=(SDS((SEQ, N_IN), BF16),
                   SDS((GROUP_W, N_GROUP, STATE), F32), SDS((GROUP_W, N_GROUP, STATE), F32),
                   SDS((N_GROUP, GROUP_W, STATE), F32), SDS((N_GROUP, GROUP_W, STATE), F32),
                   SDS((N_GROUP, STATE), F32), SDS((N_GROUP, STATE), F32), SDS((1, WIDTH), F32)),
        grid=(N_CHUNK,),
        in_specs=[chunk_w(), chunk_w(), chunk_s(), chunk_s()] + _s5_param_specs(layer) + [ANY],
        out_specs=(chunk_w(), gbt(), gbt(), gcc(), gcc(), gab(), gab(),
                   pl.BlockSpec((1, CH_W), lambda j: (0, j))),
        scratch_shapes=[pltpu.VMEM((SEQ, CH_S), F32), pltpu.VMEM((SEQ, CH_S), F32)] + _s5_param_scratch(),
        args=[dy0, proj, s_re, s_im, bbt_re, bbt_im, c_re, c_im, abar_re, abar_im, d_skip, dproj],
        aliases={11: 0}, sem=("arbitrary",), limit=VMEM_LIMIT_BIG, carry=carry)


def _proj_wgrad(layer, x, norm_g, dproj, prev):
    tm = 512
    n_prev = 0 if prev is None else 1

    def body(*refs):
        x_ref, g_ref, dp_ref = refs[:3]
        gw_ref, gb_ref = refs[3 + n_prev:]

        @pl.when(pl.program_id(1) == 0)
        def _():
            gw_ref[...] = jnp.zeros_like(gw_ref)
            gb_ref[...] = jnp.zeros_like(gb_ref)

        _, xn = _rms(x_ref[...])
        h = (xn * g_ref[layer:layer + 1, :]).astype(BF16)
        dp = dp_ref[...]
        gw_ref[...] += _dot_tn(h, dp)
        gb_ref[...] += jnp.sum(dp.astype(F32), axis=0, keepdims=True)

    return pl.pallas_call(
        body, name=f"proj_wgrad_l{layer}",
        out_shape=(SDS(_big_shapes()["w_in"], F32), SDS((1, N_IN), F32)),
        grid=(N_DEV, SEQ // tm),
        in_specs=[pl.BlockSpec((tm, D_MODEL), lambda n, t: (t, 0)),
                  _const((DEPTH, D_MODEL)),
                  pl.BlockSpec((tm, WIDTH), lambda n, t: (t, n))] + [ANY] * n_prev,
        out_specs=(pl.BlockSpec((None, None, D_MODEL, WIDTH), lambda n, t: (layer, _slot(n), 0, 0)),
                   pl.BlockSpec((1, WIDTH), lambda n, t: (0, n))),
        input_output_aliases={3: 0} if n_prev else {},
        compiler_params=_cp(("parallel", "arbitrary")),
    )(x, norm_g, dproj, *([prev] if n_prev else []))


def _proj_dgrad(layer, dx_next, x, norm_g, dproj, wg_in, carry=None):
    def body(dxn_ref, x_ref, g_ref, dp_ref, w_ref, dx_ref, gg_ref):
        @pl.when(pl.program_id(0) == 0)
        def _():
            gg_ref[...] = jnp.zeros_like(gg_ref)

        dh = jnp.zeros((TILE_M, D_MODEL), F32)
        for k in range(N_DEV):
            dh = dh + _dot_nt(dp_ref[:, k * WIDTH:(k + 1) * WIDTH], w_ref[k])
        rs, xn = _rms(x_ref[...])
        gg_ref[...] += jnp.sum(dh * xn, axis=0, keepdims=True)
        dxn = dh * g_ref[layer:layer + 1, :]
        dx_ref[...] = dxn_ref[...] + rs * (dxn - xn * jnp.mean(dxn * xn, axis=-1, keepdims=True))

    return _pcall(
        body, name=f"proj_dgrad_l{layer}",
        out_shape=(SDS((SEQ, D_MODEL), F32), SDS((1, D_MODEL), F32)),
        grid=(SEQ // TILE_M,),
        in_specs=[pl.BlockSpec((TILE_M, D_MODEL), lambda i: (i, 0)),
                  pl.BlockSpec((TILE_M, D_MODEL), lambda i: (i, 0)),
                  _const((DEPTH, D_MODEL)),
                  pl.BlockSpec((TILE_M, N_IN), lambda i: (i, 0)),
                  pl.BlockSpec((N_DEV, None, D_MODEL, WIDTH), lambda i: (0, layer, 0, 0))],
        out_specs=(pl.BlockSpec((TILE_M, D_MODEL), lambda i: (i, 0)), _const((1, D_MODEL))),
        args=[dx_next, x, norm_g, dproj, wg_in], sem=("arbitrary",), carry=carry)


def _my_place():
    return lax.axis_index("x"), lax.axis_index("y"), lax.axis_index("c")


def _gather_weights(shards):
    n = len(shards)

    def body(*refs):
        src = refs[:n]
        out = refs[n:2 * n]
        send_sems, recv_sems, local_sems = refs[2 * n:]
        x, y, c = _my_place()
        me, sibling = (x, y, c), (x, y, 1 - c)
        chips = [(1 - x, y), (x, 1 - y), (1 - x, 1 - y)]

        def rows(t, place):
            px, py, pc = place
            return out[t].at[pl.ds(4 * px + 2 * py + pc, 1)]

        def copy(t, k, block, to, from_src=False):
            return pltpu.make_async_remote_copy(
                src_ref=src[t] if from_src else rows(t, block), dst_ref=rows(t, block),
                send_sem=send_sems.at[7 * t + k], recv_sem=recv_sems.at[7 * t + k], device_id=to,
                device_id_type=MESH)

        mine = [pltpu.make_async_copy(src[t], rows(t, me), local_sems.at[t]) for t in range(n)]
        for cp in mine:
            cp.start()
        first = []
        for t in range(n):
            first.append(copy(t, 0, me, sibling, from_src=True))
            first += [copy(t, 1 + j, me, (*chip, c), from_src=True) for j, chip in enumerate(chips)]
        for cp in first:
            cp.start()
        passed = []
        for t in range(n):
            for j, chip in enumerate(chips):
                copy(t, 1 + j, (*chip, c), me).wait_recv()
                fwd = copy(t, 4 + j, (*chip, c), sibling)
                fwd.start()
                passed.append(fwd)
        for t in range(n):
            copy(t, 0, sibling, me).wait_recv()
            for j, chip in enumerate(chips):
                copy(t, 4 + j, (*chip, 1 - c), me).wait_recv()
        for cp in first + passed:
            cp.wait_send()
        for cp in mine:
            cp.wait()

    return pl.pallas_call(
        body, name="gather_weights",
        out_shape=tuple(SDS((N_DEV,) + a.shape[1:], a.dtype) for a in shards),
        in_specs=[ANY] * n, out_specs=tuple([ANY] * n),
        scratch_shapes=[pltpu.SemaphoreType.DMA((7 * n,)), pltpu.SemaphoreType.DMA((7 * n,)),
                        pltpu.SemaphoreType.DMA((n,))],
    )(*shards)


class _Carried:
    def __init__(self, ins, out_shape, sems, start, finish):
        self.ins, self.out_shape, self.sems = list(ins), list(out_shape), list(sems)
        self.start, self.finish = start, finish


def _pcall(body, *, name, grid, in_specs, out_specs, out_shape, args, scratch_shapes=(), aliases=None,
           sem=None, limit=VMEM_LIMIT, carry=None):
    out_shape, out_specs, scratch_shapes = list(out_shape), list(out_specs), list(scratch_shapes)
    n_in, n_out, n_scr = len(args), len(out_shape), len(scratch_shapes)
    if carry is None:
        kern, c_ins, c_out, c_sems = body, [], [], []
    else:
        c_ins, c_out, c_sems = carry.ins, carry.out_shape, carry.sems
        ci, co = len(c_ins), len(c_out)
        steps = tuple(grid)

        def kern(*refs):
            o0 = n_in + ci
            s0 = o0 + n_out + co
            mine = refs[:n_in] + refs[o0:o0 + n_out] + refs[s0:s0 + n_scr]
            theirs = (refs[n_in:o0], refs[o0 + n_out:s0], refs[s0 + n_scr:])
            first = pl.program_id(0) == 0
            last = pl.program_id(0) == steps[0] - 1
            for a in range(1, len(steps)):
                first = jnp.logical_and(first, pl.program_id(a) == 0)
                last = jnp.logical_and(last, pl.program_id(a) == steps[a] - 1)

            @pl.when(first)
            def _():
                carry.start(*theirs)

            body(*mine)

            @pl.when(last)
            def _():
                carry.finish(*theirs)

        sem = ("arbitrary",) * len(steps)
    res = pl.pallas_call(
        kern, name=name, grid=tuple(grid),
        in_specs=list(in_specs) + [ANY] * len(c_ins),
        out_specs=tuple(out_specs + [ANY] * len(c_out)),
        out_shape=tuple(out_shape + c_out),
        scratch_shapes=scratch_shapes + c_sems,
        input_output_aliases=aliases or {},
        compiler_params=_cp(sem, limit),
    )(*args, *c_ins)
    return res[:n_out], res[n_out:]


def _run_carried(name, carry):
    ci, co = len(carry.ins), len(carry.out_shape)

    def body(*refs):
        parts = (refs[:ci], refs[ci:ci + co], refs[ci + co:])
        carry.start(*parts)
        carry.finish(*parts)

    return pl.pallas_call(
        body, name=name, out_shape=tuple(carry.out_shape),
        in_specs=[ANY] * ci, out_specs=tuple([ANY] * co), scratch_shapes=carry.sems,
    )(*carry.ins)


def _sibling_plan(big, small):
    n = len(big)
    n_copies = 4 * n + len(small)

    def copies(ins, outs, sems):
        send_sems, recv_sems = sems
        x, y, c = _my_place()
        pairs = []
        for t, (_, layer) in enumerate(big):
            for s in range(4):
                pairs.append((ins[t].at[layer, pl.ds(4 * (1 - c) + s, 1)], outs[t].at[pl.ds(s, 1)]))
        pairs += list(zip(ins[n:], outs[n:]))
        return [pltpu.make_async_remote_copy(
            src_ref=src, dst_ref=dst, send_sem=send_sems.at[k], recv_sem=recv_sems.at[k],
            device_id=(x, y, 1 - c), device_id_type=MESH) for k, (src, dst) in enumerate(pairs)]

    def start(ins, outs, sems):
        for cp in copies(ins, outs, sems):
            cp.start()

    def finish(ins, outs, sems):
        for cp in copies(ins, outs, sems):
            cp.wait()

    out_shape = [SDS((4,) + a.shape[2:], a.dtype) for a, _ in big] + [SDS(a.shape, a.dtype) for a in small]
    sems = [pltpu.SemaphoreType.DMA((n_copies,)), pltpu.SemaphoreType.DMA((n_copies,))]
    return _Carried([a for a, _ in big] + list(small), out_shape, sems, start, finish)


def _chips_plan(big, small):
    n, n_small = len(big), len(small)
    max_rows = 512
    parts = [max(1, a.shape[1] // max_rows) for a in big]
    n_copies = 3 * (sum(parts) + n_small)

    def copies(ins, outs, sems, landing):
        send_sems, recv_sems, local_sems = sems
        x, y, c = _my_place()
        my_chip = 2 * x + y
        chips = [(1 - x, y), (x, 1 - y), (1 - x, 1 - y)]
        remote, local = [], []
        for chip in chips:
            to = 2 * chip[0] + chip[1]
            slot = to if landing else my_chip
            pairs = []
            for t in range(n):
                rows_per = big[t].shape[1] // parts[t]
                for p in range(parts[t]):
                    rows = pl.ds(p * rows_per, rows_per)
                    pairs.append((ins[t].at[to, rows], outs[t].at[slot, rows]))
            pairs += [(ins[t], outs[t].at[slot]) for t in range(n, n + n_small)]
            for src, dst in pairs:
                k = len(remote)
                remote.append(pltpu.make_async_remote_copy(
                    src_ref=src, dst_ref=dst, send_sem=send_sems.at[k], recv_sem=recv_sems.at[k],
                    device_id=(*chip, c), device_id_type=MESH))
        for t in range(n):
            local.append(pltpu.make_async_copy(ins[t].at[my_chip], outs[t].at[my_chip], local_sems.at[t]))
        for t in range(n, n + n_small):
            local.append(pltpu.make_async_copy(ins[t], outs[t].at[my_chip], local_sems.at[t]))
        return remote + local

    def start(ins, outs, sems):
        for cp in copies(ins, outs, sems, landing=False):
            cp.start()

    def finish(ins, outs, sems):
        for cp in copies(ins, outs, sems, landing=True):
            cp.wait()

    out_shape = [SDS(a.shape, a.dtype) for a in big] + [SDS((N_CHIP,) + a.shape, a.dtype) for a in small]
    sems = [pltpu.SemaphoreType.DMA((n_copies,)), pltpu.SemaphoreType.DMA((n_copies,)),
            pltpu.SemaphoreType.DMA((n + n_small,))]
    return _Carried(list(big) + list(small), out_shape, sems, start, finish)


def _row_block(rows):
    return rows if rows <= 256 else 256


def _add_own(tag, core, g, layer, got):
    _, r, c = got.shape
    rb = _row_block(r)

    def body(core_ref, a_ref, b_ref, o_ref):
        o_ref[...] = (a_ref[...] + b_ref[...]).astype(o_ref.dtype)

    return pl.pallas_call(
        body, name=f"add_{tag}", out_shape=SDS(got.shape, BF16),
        grid_spec=pltpu.PrefetchScalarGridSpec(
            num_scalar_prefetch=1, grid=(4, r // rb),
            in_specs=[pl.BlockSpec((None, None, rb, c), lambda s, j, core: (layer, 4 * core[0] + s, j, 0)),
                      pl.BlockSpec((None, rb, c), lambda s, j, core: (s, j, 0))],
            out_specs=pl.BlockSpec((None, rb, c), lambda s, j, core: (s, j, 0))),
        compiler_params=_cp(("parallel", "parallel")),
    )(core, g, got)


def _add_lists(tag, own, got, grid=None, specs=None):
    n = len(own)

    def body(*refs):
        for a, b, o in zip(refs[:n], refs[n:2 * n], refs[2 * n:]):
            o[...] = a[...] + b[...]

    kw = {}
    if grid is not None:
        kw = dict(grid=grid, in_specs=list(specs) * 2, out_specs=tuple(specs),
                  compiler_params=_cp(("parallel",) * len(grid)))
    return pl.pallas_call(
        body, name=f"add_{tag}", out_shape=tuple(SDS(a.shape, a.dtype) for a in own), **kw)(*own, *got)


def _adamw_math(w, g, m, v):
    m = ADAM_B1 * m + (1.0 - ADAM_B1) * g
    v = ADAM_B2 * v + (1.0 - ADAM_B2) * (g * g)
    m_hat = m / (1.0 - ADAM_B1 ** ADAM_STEP)
    v_hat = v / (1.0 - ADAM_B2 ** ADAM_STEP)
    delta = -ADAM_LR * (m_hat / (jnp.sqrt(v_hat) + ADAM_EPS) + ADAM_WD * w)
    return delta, m, v


def _sum_slots_adamw(tag, slots, w, m, v):
    _, r, c = slots[0].shape
    rb = _row_block(r)

    def body(s0_ref, s1_ref, w_ref, m_ref, v_ref, g_ref, d_ref, nm_ref, nv_ref):
        first = pl.program_id(1) == 0
        g = _sum4([jnp.where(first, s0_ref[k], s1_ref[k]).astype(F32) for k in range(N_CHIP)])
        delta, nm, nv = _adamw_math(w_ref[...], g, m_ref[...], v_ref[...])
        g_ref[...] = g
        d_ref[...] = delta
        nm_ref[...] = nm
        nv_ref[...] = nv

    spec = pl.BlockSpec((None, rb, c), lambda j, l: (l, j, 0))
    sspec = pl.BlockSpec((N_CHIP, rb, c), lambda j, l: (0, j, 0))
    s = SDS((DEPTH, r, c), F32)
    return pl.pallas_call(
        body, name=f"adamw_{tag}", out_shape=(s, s, s, s),
        grid=(r // rb, DEPTH), in_specs=[sspec, sspec, spec, spec, spec], out_specs=(spec, spec, spec, spec),
        compiler_params=_cp(("parallel", "arbitrary")),
    )(*slots, w, m, v)


def _adamw_small(tag, entries, grid=None):
    flat_in, in_specs, out_shape, out_specs, layout = [], [], [], [], []
    for slots, w, m, v, slot_spec, w_spec in entries:
        per_layer = isinstance(slots, (list, tuple))
        n_slot = len(slots) if per_layer else 1
        flat_in += (list(slots) if per_layer else [slots]) + [w, m, v]
        in_specs += [slot_spec] * n_slot + [w_spec] * 3
        out_shape += [SDS(w.shape, F32)] * 4
        out_specs += [w_spec] * 4
        layout.append((per_layer, n_slot))
    n_in = len(flat_in)

    def body(*refs):
        i, o = 0, n_in
        for per_layer, n_slot in layout:
            s_refs = refs[i:i + n_slot]
            w_ref, m_ref, v_ref = refs[i + n_slot:i + n_slot + 3]
            outs = refs[o:o + 4]
            if per_layer:
                for l, s_ref in enumerate(s_refs):
                    at = (slice(l, l + 1),) if len(w_ref.shape) == 2 else (l,)
                    g = _sum4([s_ref[k] for k in range(N_CHIP)])
                    res = (g,) + _adamw_math(w_ref[at], g, m_ref[at], v_ref[at])
                    for o_ref, val in zip(outs, res):
                        o_ref[at] = val
            else:
                g = _sum4([s_refs[0][k] for k in range(N_CHIP)])
                res = (g,) + _adamw_math(w_ref[...], g, m_ref[...], v_ref[...])
                for o_ref, val in zip(outs, res):
                    o_ref[...] = val
            i += n_slot + 3
            o += 4

    kw = {}
    if grid is not None:
        kw = dict(grid=grid, in_specs=in_specs, out_specs=tuple(out_specs),
                  compiler_params=_cp(("parallel",) * len(grid)))
    res = pl.pallas_call(body, name=f"adamw_{tag}", out_shape=tuple(out_shape), **kw)(*flat_in)
    return [tuple(res[4 * e:4 * e + 4]) for e in range(len(entries))]


def kernel(x, norm_g, w_in, b_in, ssm_log_dt, ssm_lam_re, ssm_lam_im, ssm_b_re, ssm_b_im, ssm_c_re, ssm_c_im, ssm_d, ssm_w_glu, ssm_b_glu, pool_w, pool_scale, w_branch_a, w_branch_b, w_out, final_norm_g, loss_target, m_norm_g, m_w_in, m_b_in, m_ssm_log_dt, m_ssm_lam_re, m_ssm_lam_im, m_ssm_b_re, m_ssm_b_im, m_ssm_c_re, m_ssm_c_im, m_ssm_d, m_ssm_w_glu, m_ssm_b_glu, m_pool_w, m_pool_scale, m_w_branch_a, m_w_branch_b, m_w_out, m_final_norm_g, v_norm_g, v_w_in, v_b_in, v_ssm_log_dt, v_ssm_lam_re, v_ssm_lam_im, v_ssm_b_re, v_ssm_b_im, v_ssm_c_re, v_ssm_c_im, v_ssm_d, v_ssm_w_glu, v_ssm_b_glu, v_pool_w, v_pool_scale, v_w_branch_a, v_w_branch_b, v_w_out, v_final_norm_g):
    weights = dict(norm_g=norm_g, w_in=w_in, b_in=b_in, ssm_log_dt=ssm_log_dt, ssm_lam_re=ssm_lam_re,
                   ssm_lam_im=ssm_lam_im, ssm_b_re=ssm_b_re, ssm_b_im=ssm_b_im, ssm_c_re=ssm_c_re,
                   ssm_c_im=ssm_c_im, ssm_d=ssm_d, ssm_w_glu=ssm_w_glu, ssm_b_glu=ssm_b_glu, pool_w=pool_w,
                   pool_scale=pool_scale, w_branch_a=w_branch_a, w_branch_b=w_branch_b, w_out=w_out,
                   final_norm_g=final_norm_g.reshape(1, D_MODEL))
    mom_m = dict(norm_g=m_norm_g, w_in=m_w_in, b_in=m_b_in, ssm_log_dt=m_ssm_log_dt, ssm_lam_re=m_ssm_lam_re,
                 ssm_lam_im=m_ssm_lam_im, ssm_b_re=m_ssm_b_re, ssm_b_im=m_ssm_b_im, ssm_c_re=m_ssm_c_re,
                 ssm_c_im=m_ssm_c_im, ssm_d=m_ssm_d, ssm_w_glu=m_ssm_w_glu, ssm_b_glu=m_ssm_b_glu,
                 pool_w=m_pool_w, pool_scale=m_pool_scale, w_branch_a=m_w_branch_a, w_branch_b=m_w_branch_b,
                 w_out=m_w_out, final_norm_g=m_final_norm_g.reshape(1, D_MODEL))
    mom_v = dict(norm_g=v_norm_g, w_in=v_w_in, b_in=v_b_in, ssm_log_dt=v_ssm_log_dt, ssm_lam_re=v_ssm_lam_re,
                 ssm_lam_im=v_ssm_lam_im, ssm_b_re=v_ssm_b_re, ssm_b_im=v_ssm_b_im, ssm_c_re=v_ssm_c_re,
                 ssm_c_im=v_ssm_c_im, ssm_d=v_ssm_d, ssm_w_glu=v_ssm_w_glu, ssm_b_glu=v_ssm_b_glu,
                 pool_w=v_pool_w, pool_scale=v_pool_scale, w_branch_a=v_w_branch_a, w_branch_b=v_w_branch_b,
                 w_out=v_w_out, final_norm_g=v_final_norm_g.reshape(1, D_MODEL))
    order = ["norm_g", "w_in", "b_in", "ssm_log_dt", "ssm_lam_re", "ssm_lam_im", "ssm_b_re", "ssm_b_im",
             "ssm_c_re", "ssm_c_im", "ssm_d", "ssm_w_glu", "ssm_b_glu", "pool_w", "pool_scale", "w_branch_a",
             "w_branch_b", "w_out", "final_norm_g"]
    big_names = ["w_in", "ssm_w_glu", "w_branch_a", "w_branch_b", "w_out"]

    wg_in, wg_glu, wg_a, wg_b, wg_out = _gather_weights([weights[n].astype(BF16)[None] for n in big_names])

    log_dt3 = ssm_log_dt.reshape(DEPTH, N_GROUP, 1)
    bt_re = ssm_b_re.transpose(0, 3, 1, 2)
    bt_im = ssm_b_im.transpose(0, 3, 1, 2)
    abar_re, abar_im, bbt_re, bbt_im = _s5_params(log_dt3, ssm_lam_re, ssm_lam_im, bt_re, bt_im)
    s5_args = (bbt_re, bbt_im, ssm_c_re, ssm_c_im, abar_re, abar_im, ssm_d)

    xs = [x.reshape(SEQ, D_MODEL)]
    saved = []
    for l in range(DEPTH):
        proj = _norm_proj(l, xs[l], norm_g, wg_in, b_in)
        s_re, s_im, y0 = _s5_fwd(l, proj, *s5_args)
        pooled = _pool_fwd(l, proj)
        xs.append(_mix_fwd(l, xs[l], proj, y0, pooled, wg_glu, ssm_b_glu, pool_w, pool_scale, wg_a, wg_b, wg_out))
        saved.append((proj, s_re, s_im, y0, pooled))

    dx, loss_part, g_final = _loss_head(xs[DEPTH], loss_target.reshape(SEQ, D_MODEL), weights["final_norm_g"])
    loss = lax.psum(loss_part[0, 0], ("x", "y", "c"))

    core = lax.axis_index("c").astype(jnp.int32).reshape(1)
    vec_names = ["norm_g", "b_in", "ssm_d", "ssm_b_glu", "pool_scale"]
    flat_names = ["final_norm_g", "ssm_log_dt", "ssm_lam_re", "ssm_lam_im"]
    mat_names = ["pool_w", "ssm_c_re", "ssm_c_im"]
    lane_sparse = {"ssm_c_re": ssm_c_re.shape[1:], "ssm_c_im": ssm_c_im.shape[1:],
                   "ssm_b_re": ssm_b_re.shape, "ssm_b_im": ssm_b_im.shape}
    gridded = set(mat_names) | {"ssm_b_re", "ssm_b_im"}

    def dense(key, a):
        return a.reshape(-1, 128) if key[0] in lane_sparse else a

    def undense(key, slots):
        return slots.reshape((N_CHIP,) + lane_sparse[key[0]]) if key[0] in lane_sparse else slots

    def add_small(tag, keys, own, got):
        out = [None] * len(keys)
        whole = [i for i, k in enumerate(keys) if k[0] not in gridded]
        tiled = [i for i, k in enumerate(keys) if k[0] in gridded]
        for i, r in zip(whole, _add_lists(f"{tag}_a", [own[i] for i in whole], [got[i] for i in whole])):
            out[i] = r
        specs = [pl.BlockSpec((1, POOL_GROUP, POOL_GROUP), lambda j: (j, 0, 0)) if keys[i][0] == "pool_w"
                 else pl.BlockSpec((own[i].shape[0] // N_CHUNK, 128), lambda j: (j, 0)) for i in tiled]
        for i, r in zip(tiled, _add_lists(f"{tag}_b", [own[i] for i in tiled], [got[i] for i in tiled],
                                          grid=(N_CHUNK,), specs=specs)):
            out[i] = r
        return out

    sm = {("final_norm_g", None): g_final}
    g_abar_re, g_abar_im, g_bbt_re, g_bbt_im = ([None] * DEPTH for _ in range(4))
    mix_big, gw_in = None, None
    keys1 = ([(n, 1) for n in vec_names[1:]] + [(n, 1) for n in mat_names] + [("final_norm_g", None)])
    chip1_big = chip1_small = slots1_big = slots1_small = None
    for l in reversed(range(DEPTH)):
        proj, s_re, s_im, y0, pooled = saved[l]
        carry = None if l == 1 else _chips_plan(chip1_big, [])
        res, moved = _mix_bwd(l, dx, proj, y0, pooled, wg_glu, ssm_b_glu, pool_w, pool_scale, wg_a, wg_b, wg_out,
                              mix_big, carry=carry)
        if l == 0:
            slots1_big = moved
        dproj, dy0, dpooled = res[:3]
        mix_big = list(res[3:7])
        sm[("pool_w", l)], sm[("pool_scale", l)], sm[("ssm_b_glu", l)] = res[7:]
        dproj = _pool_bwd(l, dpooled, dproj)
        carry = None if l == 1 else _chips_plan([], chip1_small)
        res, moved = _s5_bwd(l, dy0, proj, s_re, s_im, *s5_args, dproj, carry=carry)
        if l == 0:
            slots1_small = moved
        (dproj, g_bbt_re[l], g_bbt_im[l], sm[("ssm_c_re", l)], sm[("ssm_c_im", l)], g_abar_re[l], g_abar_im[l],
         sm[("ssm_d", l)]) = res
        gw_in, sm[("b_in", l)] = _proj_wgrad(l, xs[l], norm_g, dproj, gw_in)
        gw_out, gw_a, gw_b, gw_glu = mix_big
        big_part = [gw_in, gw_glu, gw_a, gw_b, gw_out]
        carry = None
        if l == 1:
            own1 = [dense(k, sm[k]) for k in keys1]
            carry = _sibling_plan([(a, 1) for a in big_part], own1)
        (dx, sm[("norm_g", l)]), moved = _proj_dgrad(l, dx, xs[l], norm_g, dproj, wg_in, carry=carry)
        if l == 1:
            chip1_big = [_add_own(f"chip1_{n}", core, a, 1, b) for n, a, b in zip(big_names, big_part, moved[:5])]
            chip1_small = add_small("chip1_small", keys1, own1, moved[5:])
    grad_x = dx.reshape(1, SEQ, D_MODEL)

    g_ld, g_lr, g_li, g_btr, g_bti = _s5_params_bwd(
        log_dt3, ssm_lam_re, ssm_lam_im, bt_re, bt_im, g_abar_re, g_abar_im, g_bbt_re, g_bbt_im)
    sm[("ssm_log_dt", None)] = g_ld.reshape(DEPTH, N_GROUP)
    sm[("ssm_lam_re", None)] = g_lr
    sm[("ssm_lam_im", None)] = g_li
    sm[("ssm_b_re", None)] = g_btr.transpose(0, 2, 3, 1)
    sm[("ssm_b_im", None)] = g_bti.transpose(0, 2, 3, 1)

    keys0 = ([(n, 0) for n in vec_names] + [("norm_g", 1)] + [(n, 0) for n in mat_names]
             + [(n, None) for n in ("ssm_log_dt", "ssm_lam_re", "ssm_lam_im", "ssm_b_re", "ssm_b_im")])
    own0 = [dense(k, sm[k]) for k in keys0]
    moved = _run_carried("exchange_sibling_l0", _sibling_plan([(a, 0) for a in big_part], own0))
    chip0_big = [_add_own(f"chip0_{n}", core, a, 0, b) for n, a, b in zip(big_names, big_part, moved[:5])]
    chip0_small = add_small("chip0_small", keys0, own0, moved[5:])
    moved = _run_carried("exchange_chips_l0", _chips_plan(chip0_big, chip0_small))
    slots0_big = moved[:5]
    slots = {k: undense(k, s) for k, s in zip(keys0, moved[5:])}
    slots.update({k: undense(k, s) for k, s in zip(keys1, slots1_small)})

    res = {}
    for i, n in enumerate(big_names):
        res[n] = _sum_slots_adamw(n, [slots0_big[i], slots1_big[i]], weights[n], mom_m[n], mom_v[n])
    entries_a = []
    for n in vec_names:
        entries_a.append(([slots[(n, l)] for l in range(DEPTH)], weights[n], mom_m[n], mom_v[n], None, None))
    for n in flat_names:
        entries_a.append((slots[(n, None)], weights[n], mom_m[n], mom_v[n], None, None))
    out_a = _adamw_small("small_a", entries_a)
    for n, r in zip(vec_names + flat_names, out_a):
        res[n] = r
    res["final_norm_g"] = tuple(a.reshape(D_MODEL) for a in res["final_norm_g"])
    pw_s = pl.BlockSpec((N_CHIP, 1, POOL_GROUP, POOL_GROUP), lambda j: (0, j, 0, 0))
    pw_w = pl.BlockSpec((DEPTH, 1, POOL_GROUP, POOL_GROUP), lambda j: (0, j, 0, 0))
    c_s = pl.BlockSpec((N_CHIP, CH_G, GROUP_W, STATE), lambda j: (0, j, 0, 0))
    c_w = pl.BlockSpec((DEPTH, CH_G, GROUP_W, STATE), lambda j: (0, j, 0, 0))
    b_s = pl.BlockSpec((N_CHIP, DEPTH, CH_G, STATE, GROUP_W), lambda j: (0, 0, j, 0, 0))
    b_w = pl.BlockSpec((DEPTH, CH_G, STATE, GROUP_W), lambda j: (0, j, 0, 0))
    entries_b = []
    for n, s_spec, w_spec in (("pool_w", pw_s, pw_w), ("ssm_c_re", c_s, c_w), ("ssm_c_im", c_s, c_w)):
        entries_b.append(([slots[(n, l)] for l in range(DEPTH)], weights[n], mom_m[n], mom_v[n], s_spec, w_spec))
    for n in ("ssm_b_re", "ssm_b_im"):
        entries_b.append((slots[(n, None)], weights[n], mom_m[n], mom_v[n], b_s, b_w))
    out_b = _adamw_small("small_b", entries_b, grid=(N_CHUNK,))
    for n, r in zip(mat_names + ["ssm_b_re", "ssm_b_im"], out_b):
        res[n] = r

    outs = [loss, grad_x]
    for i in range(4):
        outs += [res[n][i] for n in order]
    return tuple(outs)
```

```python
import math

import jax
import jax.numpy as jnp
from jax import lax
from jax.experimental import pallas as pl
from jax.experimental.pallas import tpu as pltpu

F32 = jnp.float32
BF16 = jnp.bfloat16

SEQ = 2048
D_MODEL = 1024
N_IN = 4096
WIDTH = 512
N_GROUP = 32
GROUP_W = 16
STATE = 64
N_STATE = N_GROUP * STATE
N_CHUNK = 4
CH_G = N_GROUP // N_CHUNK
CH_W = WIDTH // N_CHUNK
CH_S = N_STATE // N_CHUNK
N_DEV = 8
N_CHIP = 4
POOL_WINDOWS = (2, 4, 8, 16)
POOL_GROUP = 128
EPS = 1e-6
DEPTH = 2

ADAM_LR = 0.001
ADAM_B1 = 0.9
ADAM_B2 = 0.999
ADAM_EPS = 1e-08
ADAM_WD = 0.01
ADAM_STEP = 10

TILE_M = 256
ROW_BLK = 512
VMEM_LIMIT = 48 * 1024 * 1024
VMEM_LIMIT_BIG = 60 * 1024 * 1024
MESH = pl.DeviceIdType.MESH
ANY = pl.BlockSpec(memory_space=pl.ANY)

GELU_C = math.sqrt(2.0 / math.pi)
GELU_A = 0.044715

SDS = jax.ShapeDtypeStruct


def _cp(sem=None, limit=VMEM_LIMIT):
    return pltpu.CompilerParams(dimension_semantics=sem, vmem_limit_bytes=limit)


def _dot(a, b):
    return jnp.dot(a, b, preferred_element_type=F32)


def _dot_nt(a, b):
    return lax.dot_general(a, b, (((1,), (1,)), ((), ())), preferred_element_type=F32)


def _dot_tn(a, b):
    return lax.dot_general(a, b, (((0,), (0,)), ((), ())), preferred_element_type=F32)


def _sig(x):
    return jax.nn.sigmoid(x)


def _rms(x):
    rs = lax.rsqrt(jnp.mean(x * x, axis=-1, keepdims=True) + EPS)
    return rs, x * rs


def _slot(n):
    return 4 * (n % 2) + n // 2


def _const(shape):
    n = len(shape)
    return pl.BlockSpec(shape, lambda *_: (0,) * n)


def _sum4(p):
    return (p[0] + p[1]) + (p[2] + p[3])


def _s5_param_fn(log_dt, lam_re, lam_im, bt_re, bt_im):
    dt = jnp.exp(log_dt)
    mag = jnp.exp(lam_re * dt)
    ang = lam_im * dt
    abar_re = mag * jnp.cos(ang)
    abar_im = mag * jnp.sin(ang)
    num_re = abar_re - 1.0
    num_im = abar_im
    den = lam_re * lam_re + lam_im * lam_im
    coef_re = (num_re * lam_re + num_im * lam_im) / den
    coef_im = (num_im * lam_re - num_re * lam_im) / den
    bbar_re = coef_re[:, None] * bt_re - coef_im[:, None] * bt_im
    bbar_im = coef_re[:, None] * bt_im + coef_im[:, None] * bt_re
    return abar_re, abar_im, bbar_re, bbar_im


def _s5_params(log_dt, lam_re, lam_im, bt_re, bt_im):
    def body(ld, lr, li, br, bi, o_ar, o_ai, o_br, o_bi):
        ar, ai, bbr, bbi = _s5_param_fn(ld[...], lr[...], li[...], br[...], bi[...])
        o_ar[...] = ar
        o_ai[...] = ai
        o_br[...] = bbr
        o_bi[...] = bbi

    return pl.pallas_call(
        body, name="s5_params",
        out_shape=(SDS(lam_re.shape, F32), SDS(lam_re.shape, F32), SDS(bt_re.shape, F32), SDS(bt_re.shape, F32)),
    )(log_dt, lam_re, lam_im, bt_re, bt_im)


def _s5_params_bwd(log_dt, lam_re, lam_im, bt_re, bt_im, g_ar, g_ai, g_br, g_bi):
    def body(ld, lr, li, br, bi, car0, car1, cai0, cai1, cbr0, cbr1, cbi0, cbi1, o_ld, o_lr, o_li, o_br, o_bi):
        _, vjp = jax.vjp(_s5_param_fn, ld[...], lr[...], li[...], br[...], bi[...])
        both = lambda a, b: jnp.stack([a[...], b[...]], axis=0)
        d_ld, d_lr, d_li, d_br, d_bi = vjp((both(car0, car1), both(cai0, cai1), both(cbr0, cbr1), both(cbi0, cbi1)))
        o_ld[...] = d_ld
        o_lr[...] = d_lr
        o_li[...] = d_li
        o_br[...] = d_br
        o_bi[...] = d_bi

    return pl.pallas_call(
        body, name="s5_params_bwd",
        out_shape=(SDS(log_dt.shape, F32), SDS(lam_re.shape, F32), SDS(lam_re.shape, F32),
                   SDS(bt_re.shape, F32), SDS(bt_re.shape, F32)),
    )(log_dt, lam_re, lam_im, bt_re, bt_im, *g_ar, *g_ai, *g_br, *g_bi)


def _norm_proj(layer, x, norm_g, wg_in, b_in, carry=None):
    def body(x_ref, g_ref, w_ref, b_ref, o_ref):
        _, xn = _rms(x_ref[...])
        h = (xn * g_ref[layer:layer + 1, :]).astype(BF16)
        for k in range(N_DEV):
            cols = slice(k * WIDTH, (k + 1) * WIDTH)
            o_ref[:, cols] = _dot(h, w_ref[k]) + b_ref[layer:layer + 1, cols]

    (proj,), moved = _pcall(
        body, name=f"norm_proj_l{layer}",
        out_shape=[SDS((SEQ, N_IN), F32)],
        grid=(SEQ // TILE_M,),
        in_specs=[pl.BlockSpec((TILE_M, D_MODEL), lambda i: (i, 0)),
                  _const((DEPTH, D_MODEL)),
                  _const((N_DEV, D_MODEL, WIDTH)),
                  _const((DEPTH, N_IN))],
        out_specs=[pl.BlockSpec((TILE_M, N_IN), lambda i: (i, 0))],
        args=[x, norm_g, wg_in, b_in], sem=("parallel",), carry=carry)
    return proj, moved


def _scan_tile_rows():
    return lax.broadcasted_iota(jnp.int32, (8, CH_S), 0)


def _s5_param_specs(layer):
    bt = lambda: pl.BlockSpec((None, GROUP_W, CH_G, STATE), lambda j: (layer, 0, j, 0))
    cc = lambda: pl.BlockSpec((None, CH_G, GROUP_W, STATE), lambda j: (layer, j, 0, 0))
    ab = lambda: pl.BlockSpec((None, CH_G, STATE), lambda j: (layer, j, 0))
    return [bt(), bt(), cc(), cc(), ab(), ab(), pl.BlockSpec((DEPTH, CH_W), lambda j: (0, j))]


def _s5_param_scratch():
    return [pltpu.VMEM((CH_W, CH_S), BF16)] * 4 + [pltpu.VMEM((1, CH_S), F32)] * 2


def _s5_fill(btre_ref, btim_ref, cre_ref, cim_ref, are_ref, aim_ref, bdre, bdim, ctre, ctim, arow, airow):
    for m in (bdre, bdim, ctre, ctim):
        m[...] = jnp.zeros_like(m)
    for g in range(CH_G):
        rows = slice(g * GROUP_W, (g + 1) * GROUP_W)
        cols = slice(g * STATE, (g + 1) * STATE)
        bdre[rows, cols] = btre_ref[:, g, :].astype(BF16)
        bdim[rows, cols] = btim_ref[:, g, :].astype(BF16)
        ctre[rows, cols] = cre_ref[g].astype(BF16)
        ctim[rows, cols] = cim_ref[g].astype(BF16)
        arow[:, cols] = are_ref[g:g + 1, :]
        airow[:, cols] = aim_ref[g:g + 1, :]


def _s5_fwd(layer, proj, bbt_re, bbt_im, c_re, c_im, abar_re, abar_im, d_skip, carry=None):
    def body(u_ref, btre_ref, btim_ref, cre_ref, cim_ref, are_ref, aim_ref, d_ref,
             sre_ref, sim_ref, y_ref, bdre, bdim, ctre, ctim, arow, airow):
        _s5_fill(btre_ref, btim_ref, cre_ref, cim_ref, are_ref, aim_ref, bdre, bdim, ctre, ctim, arow, airow)
        for rb in range(SEQ // ROW_BLK):
            rows = pl.ds(rb * ROW_BLK, ROW_BLK)
            ub = u_ref[rows, :].astype(BF16)
            sre_ref[rows, :] = _dot(ub, bdre[...])
            sim_ref[rows, :] = _dot(ub, bdim[...])
        ar = arow[...]
        ai = airow[...]
        row_id = _scan_tile_rows()

        def step(i, carry):
            sr, si = carry
            base = pl.multiple_of(i * 8, 8)
            tr = sre_ref[pl.ds(base, 8), :]
            ti = sim_ref[pl.ds(base, 8), :]
            outr, outi = tr, ti
            for r in range(8):
                nsr = ar * sr - ai * si + tr[r:r + 1, :]
                nsi = ar * si + ai * sr + ti[r:r + 1, :]
                outr = jnp.where(row_id == r, nsr, outr)
                outi = jnp.where(row_id == r, nsi, outi)
                sr, si = nsr, nsi
            sre_ref[pl.ds(base, 8), :] = outr
            sim_ref[pl.ds(base, 8), :] = outi
            return sr, si

        zero = jnp.zeros((1, CH_S), F32)
        lax.fori_loop(0, SEQ // 8, step, (zero, zero))
        d = d_ref[layer:layer + 1, :]
        for rb in range(SEQ // ROW_BLK):
            rows = pl.ds(rb * ROW_BLK, ROW_BLK)
            y = (_dot_nt(sre_ref[rows, :].astype(BF16), ctre[...])
                 - _dot_nt(sim_ref[rows, :].astype(BF16), ctim[...]))
            y_ref[rows, :] = y + d * u_ref[rows, :]

    return _pcall(
        body, name=f"s5_fwd_l{layer}",
        out_shape=(SDS((SEQ, N_STATE), F32), SDS((SEQ, N_STATE), F32), SDS((SEQ, WIDTH), F32)),
        grid=(N_CHUNK,),
        in_specs=[pl.BlockSpec((SEQ, CH_W), lambda j: (0, j))] + _s5_param_specs(layer),
        out_specs=(pl.BlockSpec((SEQ, CH_S), lambda j: (0, j)),
                   pl.BlockSpec((SEQ, CH_S), lambda j: (0, j)),
                   pl.BlockSpec((SEQ, CH_W), lambda j: (0, j))),
        scratch_shapes=_s5_param_scratch(),
        args=[proj, bbt_re, bbt_im, c_re, c_im, abar_re, abar_im, d_skip], sem=("parallel",), carry=carry)


def _pool_counts(win):
    t = lax.broadcasted_iota(jnp.int32, (SEQ, POOL_GROUP), 0)
    return t, jnp.minimum(t + 1, win).astype(F32)


def _pool_fwd(layer, proj):
    def body(u_ref, o_ref):
        for gi, win in enumerate(POOL_WINDOWS):
            cols = slice(gi * POOL_GROUP, (gi + 1) * POOL_GROUP)
            u = u_ref[:, cols]
            t, count = _pool_counts(win)
            acc = u
            k = 1
            while k < win:
                acc = acc + jnp.where(t >= k, pltpu.roll(acc, k, 0), 0.0)
                k *= 2
            o_ref[:, cols] = acc / count - u

    return pl.pallas_call(
        body, name=f"pool_fwd_l{layer}",
        out_shape=SDS((SEQ, WIDTH), F32),
        grid=(1,),
        in_specs=[pl.BlockSpec((SEQ, WIDTH), lambda i: (0, 2))],
        out_specs=pl.BlockSpec((SEQ, WIDTH), lambda i: (0, 0)),
        compiler_params=_cp(("arbitrary",)),
    )(proj)


def _gelu_parts(y0):
    t = jnp.tanh(GELU_C * (y0 + GELU_A * (y0 * y0 * y0)))
    return t, 0.5 * y0 * (1.0 + t)


def _mix_forward(layer, p_ref, y0_ref, pooled_ref, wglu_ref, bglu_ref, pw_ref, scale_ref, wa_ref, wb_ref):
    za = p_ref[:, WIDTH:2 * WIDTH]
    zb = p_ref[:, 3 * WIDTH:4 * WIDTH]
    ga = p_ref[:, 4 * WIDTH:4 * WIDTH + D_MODEL]
    gb = p_ref[:, 4 * WIDTH + D_MODEL:]
    y0 = y0_ref[...]
    t, y1 = _gelu_parts(y0)
    y1b = y1.astype(BF16)
    q = _dot(y1b, wglu_ref[...].reshape(WIDTH, WIDTH)) + bglu_ref[layer:layer + 1, :]
    sq = _sig(q)
    y2 = y1 * sq
    sza = _sig(za)
    silu_za = za * sza
    ya = y2 * silu_za
    pooled = pooled_ref[...]
    mixed = jnp.concatenate(
        [_dot(pooled[:, g * POOL_GROUP:(g + 1) * POOL_GROUP].astype(BF16), pw_ref[g].astype(BF16))
         for g in range(len(POOL_WINDOWS))], axis=1)
    szb = _sig(zb)
    silu_zb = zb * szb
    scale = scale_ref[layer:layer + 1, :]
    ms = mixed * scale
    yb = ms * silu_zb
    yab = ya.astype(BF16)
    ybb = yb.astype(BF16)
    ma = jnp.concatenate([_dot(yab, wa_ref[k]) for k in range(N_DEV)], axis=1)
    mb = jnp.concatenate([_dot(ybb, wb_ref[k]) for k in range(N_DEV)], axis=1)
    sga = _sig(ga)
    sgb = _sig(gb)
    merged = sga * ma + sgb * mb
    return dict(za=za, zb=zb, y0=y0, t=t, y1=y1, y1b=y1b, sq=sq, y2=y2, sza=sza, silu_za=silu_za,
                pooled=pooled, mixed=mixed, szb=szb, silu_zb=silu_zb, scale=scale, ms=ms, yab=yab, ybb=ybb,
                ma=ma, mb=mb, sga=sga, sgb=sgb, merged=merged)


def _mix_weight_specs(layer):
    return [_const((N_DEV, WIDTH // N_DEV, WIDTH)),
            _const((DEPTH, WIDTH)),
            pl.BlockSpec((None, 4, POOL_GROUP, POOL_GROUP), lambda i: (layer, 0, 0, 0)),
            _const((DEPTH, WIDTH)),
            _const((N_DEV, WIDTH, D_MODEL // N_DEV)),
            _const((N_DEV, WIDTH, D_MODEL // N_DEV)),
            _const((N_DEV, D_MODEL // N_DEV, D_MODEL))]


def _mix_fwd(layer, x, proj, y0, pooled, wg_glu, b_glu, pool_w, pool_scale, wg_a, wg_b, wg_out, carry=None):
    def body(x_ref, p_ref, y0_ref, pooled_ref, wglu_ref, bglu_ref, pw_ref, scale_ref, wa_ref, wb_ref,
             wout_ref, o_ref):
        f = _mix_forward(layer, p_ref, y0_ref, pooled_ref, wglu_ref, bglu_ref, pw_ref, scale_ref, wa_ref, wb_ref)
        wout = wout_ref[...].reshape(D_MODEL, D_MODEL)
        o_ref[...] = x_ref[...] + _dot(f["merged"].astype(BF16), wout)

    (x_next,), moved = _pcall(
        body, name=f"mix_fwd_l{layer}",
        out_shape=[SDS((SEQ, D_MODEL), F32)],
        grid=(SEQ // TILE_M,),
        in_specs=[pl.BlockSpec((TILE_M, D_MODEL), lambda i: (i, 0)),
                  pl.BlockSpec((TILE_M, N_IN), lambda i: (i, 0)),
                  pl.BlockSpec((TILE_M, WIDTH), lambda i: (i, 0)),
                  pl.BlockSpec((TILE_M, WIDTH), lambda i: (i, 0))] + _mix_weight_specs(layer),
        out_specs=[pl.BlockSpec((TILE_M, D_MODEL), lambda i: (i, 0))],
        args=[x, proj, y0, pooled, wg_glu, b_glu, pool_w, pool_scale, wg_a, wg_b, wg_out],
        sem=("parallel",), carry=carry)
    return x_next, moved


def _loss_head(x, target, final_g):
    def body(x_ref, t_ref, g_ref, dx_ref, loss_ref, gg_ref):
        @pl.when(pl.program_id(0) == 0)
        def _():
            loss_ref[...] = jnp.zeros_like(loss_ref)
            gg_ref[...] = jnp.zeros_like(gg_ref)

        g = g_ref[...]
        rs, xn = _rms(x_ref[...])
        err = xn * g - t_ref[...]
        loss_ref[...] += 0.5 * jnp.sum(jnp.mean(err * err, axis=-1, keepdims=True), axis=0, keepdims=True)
        dy = err * (1.0 / D_MODEL)
        gg_ref[...] += jnp.sum(dy * xn, axis=0, keepdims=True)
        dxn = dy * g
        dx_ref[...] = rs * (dxn - xn * jnp.mean(dxn * xn, axis=-1, keepdims=True))

    return pl.pallas_call(
        body, name="loss_head",
        out_shape=(SDS((SEQ, D_MODEL), F32), SDS((1, 1), F32), SDS((1, D_MODEL), F32)),
        grid=(SEQ // TILE_M,),
        in_specs=[pl.BlockSpec((TILE_M, D_MODEL), lambda i: (i, 0)),
                  pl.BlockSpec((TILE_M, D_MODEL), lambda i: (i, 0)),
                  _const((1, D_MODEL))],
        out_specs=(pl.BlockSpec((TILE_M, D_MODEL), lambda i: (i, 0)), _const((1, 1)), _const((1, D_MODEL))),
        compiler_params=_cp(("arbitrary",)),
    )(x, target, final_g)


def _big_shapes():
    return dict(w_out=(DEPTH, N_DEV, D_MODEL // N_DEV, D_MODEL), w_branch_a=(DEPTH, N_DEV, WIDTH, D_MODEL // N_DEV),
                w_branch_b=(DEPTH, N_DEV, WIDTH, D_MODEL // N_DEV), ssm_w_glu=(DEPTH, N_DEV, WIDTH // N_DEV, WIDTH),
                w_in=(DEPTH, N_DEV, D_MODEL, WIDTH))


def _mix_bwd(layer, dx_next, proj, y0, pooled, wg_glu, b_glu, pool_w, pool_scale, wg_a, wg_b, wg_out, prev,
             carry=None):
    n_k = N_DEV
    n_prev = 0 if prev is None else len(prev)

    def body(*refs):
        (dx_ref, p_ref, y0_ref, pooled_ref, wglu_ref, bglu_ref, pw_ref, scale_ref, wa_ref, wb_ref,
         wout_ref) = refs[:11]
        (dproj_ref, dy0_ref, dpooled_ref, gwout_ref, gwa_ref, gwb_ref, gwglu_ref, gpw_ref,
         gscale_ref, gbglu_ref) = refs[11 + n_prev:]

        @pl.when(pl.program_id(0) == 0)
        def _():
            for r in (gwout_ref, gwa_ref, gwb_ref, gwglu_ref, gpw_ref, gscale_ref, gbglu_ref):
                r[...] = jnp.zeros_like(r)

        f = _mix_forward(layer, p_ref, y0_ref, pooled_ref, wglu_ref, bglu_ref, pw_ref, scale_ref, wa_ref, wb_ref)
        wglu = wglu_ref[...].reshape(WIDTH, WIDTH)
        wout = wout_ref[...].reshape(D_MODEL, D_MODEL)
        blk = D_MODEL // n_k
        dxb = dx_ref[...].astype(BF16)
        dmerged = _dot_nt(dxb, wout)
        gwout = _dot_tn(f["merged"].astype(BF16), dxb)
        for k in range(n_k):
            gwout_ref[_slot(k)] += gwout[k * blk:(k + 1) * blk, :]
        dma = dmerged * f["sga"]
        dmb = dmerged * f["sgb"]
        dga = dmerged * f["ma"] * f["sga"] * (1.0 - f["sga"])
        dgb = dmerged * f["mb"] * f["sgb"] * (1.0 - f["sgb"])
        dmab = dma.astype(BF16)
        dmbb = dmb.astype(BF16)
        dya = jnp.zeros((TILE_M, WIDTH), F32)
        dyb = jnp.zeros((TILE_M, WIDTH), F32)
        for k in range(n_k):
            da_k = dmab[:, k * blk:(k + 1) * blk]
            db_k = dmbb[:, k * blk:(k + 1) * blk]
            dya = dya + _dot_nt(da_k, wa_ref[k])
            dyb = dyb + _dot_nt(db_k, wb_ref[k])
            gwa_ref[_slot(k)] += _dot_tn(f["yab"], da_k)
            gwb_ref[_slot(k)] += _dot_tn(f["ybb"], db_k)
        zb, szb = f["zb"], f["szb"]
        dzb = dyb * f["ms"] * (szb * (1.0 + zb * (1.0 - szb)))
        dms = dyb * f["silu_zb"]
        gscale_ref[...] += jnp.sum(dms * f["mixed"], axis=0, keepdims=True)
        dmixed = (dms * f["scale"]).astype(BF16)
        pooled = f["pooled"]
        for g in range(len(POOL_WINDOWS)):
            cols = slice(g * POOL_GROUP, (g + 1) * POOL_GROUP)
            dpooled_ref[:, cols] = _dot_nt(dmixed[:, cols], pw_ref[g].astype(BF16))
            gpw_ref[g] += _dot_tn(pooled[:, cols].astype(BF16), dmixed[:, cols])
        za, sza = f["za"], f["sza"]
        dza = dya * f["y2"] * (sza * (1.0 + za * (1.0 - sza)))
        dy2 = dya * f["silu_za"]
        sq = f["sq"]
        dq = dy2 * f["y1"] * sq * (1.0 - sq)
        dqb = dq.astype(BF16)
        dy1 = dy2 * sq + _dot_nt(dqb, wglu)
        gwglu = _dot_tn(f["y1b"], dqb)
        rblk = WIDTH // n_k
        for k in range(n_k):
            gwglu_ref[_slot(k)] += gwglu[k * rblk:(k + 1) * rblk, :]
        gbglu_ref[...] += jnp.sum(dq, axis=0, keepdims=True)
        y0, t = f["y0"], f["t"]
        dgelu = 0.5 * (1.0 + t) + 0.5 * y0 * (1.0 - t * t) * (GELU_C * (1.0 + 3.0 * GELU_A * y0 * y0))
        dy0_ref[...] = dy1 * dgelu
        zeros = jnp.zeros((TILE_M, WIDTH), BF16)
        dproj_ref[:, 0:WIDTH] = zeros
        dproj_ref[:, WIDTH:2 * WIDTH] = dza.astype(BF16)
        dproj_ref[:, 2 * WIDTH:3 * WIDTH] = zeros
        dproj_ref[:, 3 * WIDTH:4 * WIDTH] = dzb.astype(BF16)
        dproj_ref[:, 4 * WIDTH:4 * WIDTH + D_MODEL] = dga.astype(BF16)
        dproj_ref[:, 4 * WIDTH + D_MODEL:] = dgb.astype(BF16)

    tile = lambda w: pl.BlockSpec((TILE_M, w), lambda i: (i, 0))
    shapes = _big_shapes()
    big = ["w_out", "w_branch_a", "w_branch_b", "ssm_w_glu"]
    slab = lambda n: pl.BlockSpec((None,) + shapes[n][1:], lambda i: (layer, 0, 0, 0))
    args = [dx_next, proj, y0, pooled, wg_glu, b_glu, pool_w, pool_scale, wg_a, wg_b, wg_out]
    return _pcall(
        body, name=f"mix_bwd_l{layer}",
        out_shape=(SDS((SEQ, N_IN), BF16), SDS((SEQ, WIDTH), F32), SDS((SEQ, WIDTH), F32))
        + tuple(SDS(shapes[n], F32) for n in big)
        + (SDS((4, POOL_GROUP, POOL_GROUP), F32), SDS((1, WIDTH), F32), SDS((1, WIDTH), F32)),
        grid=(SEQ // TILE_M,),
        in_specs=[tile(D_MODEL), tile(N_IN), tile(WIDTH), tile(WIDTH)] + _mix_weight_specs(layer) + [ANY] * n_prev,
        out_specs=(tile(N_IN), tile(WIDTH), tile(WIDTH)) + tuple(slab(n) for n in big)
        + (_const((4, POOL_GROUP, POOL_GROUP)), _const((1, WIDTH)), _const((1, WIDTH))),
        args=args + list(prev or ()),
        aliases={len(args) + i: 3 + i for i in range(n_prev)},
        sem=("arbitrary",), limit=VMEM_LIMIT_BIG, carry=carry)


def _pool_bwd(layer, dpooled, dproj):
    def body(dp_ref, _, o_ref):
        for gi, win in enumerate(POOL_WINDOWS):
            cols = slice(gi * POOL_GROUP, (gi + 1) * POOL_GROUP)
            dp = dp_ref[:, cols]
            t, count = _pool_counts(win)
            e = dp / count
            acc = e
            k = 1
            while k < win:
                acc = acc + jnp.where(t < SEQ - k, pltpu.roll(acc, SEQ - k, 0), 0.0)
                k *= 2
            o_ref[:, cols] = (acc - dp).astype(BF16)

    return pl.pallas_call(
        body, name=f"pool_bwd_l{layer}",
        out_shape=SDS((SEQ, N_IN), BF16),
        grid=(1,),
        in_specs=[pl.BlockSpec((SEQ, WIDTH), lambda i: (0, 0)), ANY],
        out_specs=pl.BlockSpec((SEQ, WIDTH), lambda i: (0, 2)),
        input_output_aliases={1: 0},
        compiler_params=_cp(("arbitrary",)),
    )(dpooled, dproj)


def _s5_bwd(layer, dy0, proj, s_re, s_im, bbt_re, bbt_im, c_re, c_im, abar_re, abar_im, d_skip, dproj,
            carry=None):
    def body(dy_ref, u_ref, sre_ref, sim_ref, btre_ref, btim_ref, cre_ref, cim_ref, are_ref, aim_ref,
             d_ref, _, du_ref, gbre_ref, gbim_ref, gcre_ref, gcim_ref, gare_ref, gaim_ref, gd_ref,
             lre_ref, lim_ref, bdre, bdim, ctre, ctim, arow, airow):
        _s5_fill(btre_ref, btim_ref, cre_ref, cim_ref, are_ref, aim_ref, bdre, bdim, ctre, ctim, arow, airow)
        n_rb = SEQ // ROW_BLK
        gcre = jnp.zeros((CH_W, CH_S), F32)
        gcim = jnp.zeros((CH_W, CH_S), F32)
        for rb in range(n_rb):
            rows = pl.ds(rb * ROW_BLK, ROW_BLK)
            dyb = dy_ref[rows, :].astype(BF16)
            lre_ref[rows, :] = _dot(dyb, ctre[...])
            lim_ref[rows, :] = -_dot(dyb, ctim[...])
            gcre = gcre + _dot_tn(dyb, sre_ref[rows, :].astype(BF16))
            gcim = gcim - _dot_tn(dyb, sim_ref[rows, :].astype(BF16))
        ar = arow[...]
        ai = airow[...]
        row_id = _scan_tile_rows()

        def step(n, carry):
            lr, li = carry
            base = pl.multiple_of((SEQ // 8 - 1 - n) * 8, 8)
            tr = lre_ref[pl.ds(base, 8), :]
            ti = lim_ref[pl.ds(base, 8), :]
            outr, outi = tr, ti
            for r in range(7, -1, -1):
                nlr = ar * lr + ai * li + tr[r:r + 1, :]
                nli = ar * li - ai * lr + ti[r:r + 1, :]
                outr = jnp.where(row_id == r, nlr, outr)
                outi = jnp.where(row_id == r, nli, outi)
                lr, li = nlr, nli
            lre_ref[pl.ds(base, 8), :] = outr
            lim_ref[pl.ds(base, 8), :] = outi
            return lr, li

        zero = jnp.zeros((1, CH_S), F32)
        lax.fori_loop(0, SEQ // 8, step, (zero, zero))

        gare = jnp.zeros((1, CH_S), F32)
        gaim = jnp.zeros((1, CH_S), F32)
        gbre = jnp.zeros((CH_W, CH_S), F32)
        gbim = jnp.zeros((CH_W, CH_S), F32)
        gd = jnp.zeros((1, CH_W), F32)
        d = d_ref[layer:layer + 1, :]
        first = lax.broadcasted_iota(jnp.int32, (ROW_BLK, CH_S), 0) == 0
        for rb in range(n_rb):
            rows = pl.ds(rb * ROW_BLK, ROW_BLK)
            lr = lre_ref[rows, :]
            li = lim_ref[rows, :]
            if rb == 0:
                prev_r = jnp.zeros((1, CH_S), F32)
                prev_i = jnp.zeros((1, CH_S), F32)
            else:
                prev_r = sre_ref[pl.ds(rb * ROW_BLK - 1, 1), :]
                prev_i = sim_ref[pl.ds(rb * ROW_BLK - 1, 1), :]
            spr = jnp.where(first, prev_r, pltpu.roll(sre_ref[rows, :], 1, 0))
            spi = jnp.where(first, prev_i, pltpu.roll(sim_ref[rows, :], 1, 0))
            gare = gare + jnp.sum(lr * spr + li * spi, axis=0, keepdims=True)
            gaim = gaim + jnp.sum(li * spr - lr * spi, axis=0, keepdims=True)
            lrb = lr.astype(BF16)
            lib = li.astype(BF16)
            u = u_ref[rows, :]
            ub = u.astype(BF16)
            dy = dy_ref[rows, :]
            du = dy * d + _dot_nt(lrb, bdre[...]) + _dot_nt(lib, bdim[...])
            du_ref[rows, :] = du.astype(BF16)
            gbre = gbre + _dot_tn(ub, lrb)
            gbim = gbim + _dot_tn(ub, lib)
            gd = gd + jnp.sum(dy * u, axis=0, keepdims=True)
        gd_ref[...] = gd
        for g in range(CH_G):
            rows = slice(g * GROUP_W, (g + 1) * GROUP_W)
            cols = slice(g * STATE, (g + 1) * STATE)
            gcre_ref[g] = gcre[rows, cols]
            gcim_ref[g] = gcim[rows, cols]
            gbre_ref[:, g, :] = gbre[rows, cols]
            gbim_ref[:, g, :] = gbim[rows, cols]
            gare_ref[g:g + 1, :] = gare[:, cols]
            gaim_ref[g:g + 1, :] = gaim[:, cols]

    chunk_w = lambda: pl.BlockSpec((SEQ, CH_W), lambda j: (0, j))
    chunk_s = lambda: pl.BlockSpec((SEQ, CH_S), lambda j: (0, j))
    gbt = lambda: pl.BlockSpec((GROUP_W, CH_G, STATE), lambda j: (0, j, 0))
    gcc = lambda: pl.BlockSpec((CH_G, GROUP_W, STATE), lambda j: (j, 0, 0))
    gab = lambda: pl.BlockSpec((CH_G, STATE), lambda j: (j, 0))
    return _pcall(
        body, name=f"s5_bwd_l{layer}",
        out_shape=(SDS((SEQ, N_IN), BF16),
                   SDS((GROUP_W, N_GROUP, STATE), F32), SDS((GROUP_W, N_GROUP, STATE), F32),
                   SDS((N_GROUP, GROUP_W, STATE), F32), SDS((N_GROUP, GROUP_W, STATE), F32),
                   SDS((N_GROUP, STATE), F32), SDS((N_GROUP, STATE), F32), SDS((1, WIDTH), F32)),
        grid=(N_CHUNK,),
        in_specs=[chunk_w(), chunk_w(), chunk_s(), chunk_s()] + _s5_param_specs(layer) + [ANY],
        out_specs=(chunk_w(), gbt(), gbt(), gcc(), gcc(), gab(), gab(),
                   pl.BlockSpec((1, CH_W), lambda j: (0, j))),
        scratch_shapes=[pltpu.VMEM((SEQ, CH_S), F32), pltpu.VMEM((SEQ, CH_S), F32)] + _s5_param_scratch(),
        args=[dy0, proj, s_re, s_im, bbt_re, bbt_im, c_re, c_im, abar_re, abar_im, d_skip, dproj],
        aliases={11: 0}, sem=("arbitrary",), limit=VMEM_LIMIT_BIG, carry=carry)


def _proj_wgrad(layer, x, norm_g, dproj, prev):
    tm = 512
    n_prev = 0 if prev is None else 1

    def body(*refs):
        x_ref, g_ref, dp_ref = refs[:3]
        gw_ref, gb_ref = refs[3 + n_prev:]

        @pl.when(pl.program_id(1) == 0)
        def _():
            gw_ref[...] = jnp.zeros_like(gw_ref)
            gb_ref[...] = jnp.zeros_like(gb_ref)

        _, xn = _rms(x_ref[...])
        h = (xn * g_ref[layer:layer + 1, :]).astype(BF16)
        dp = dp_ref[...]
        gw_ref[...] += _dot_tn(h, dp)
        gb_ref[...] += jnp.sum(dp.astype(F32), axis=0, keepdims=True)

    return pl.pallas_call(
        body, name=f"proj_wgrad_l{layer}",
        out_shape=(SDS(_big_shapes()["w_in"], F32), SDS((1, N_IN), F32)),
        grid=(N_DEV, SEQ // tm),
        in_specs=[pl.BlockSpec((tm, D_MODEL), lambda n, t: (t, 0)),
                  _const((DEPTH, D_MODEL)),
                  pl.BlockSpec((tm, WIDTH), lambda n, t: (t, n))] + [ANY] * n_prev,
        out_specs=(pl.BlockSpec((None, None, D_MODEL, WIDTH), lambda n, t: (layer, _slot(n), 0, 0)),
                   pl.BlockSpec((1, WIDTH), lambda n, t: (0, n))),
        input_output_aliases={3: 0} if n_prev else {},
        compiler_params=_cp(("parallel", "arbitrary")),
    )(x, norm_g, dproj, *([prev] if n_prev else []))


def _proj_dgrad(layer, dx_next, x, norm_g, dproj, wg_in, carry=None):
    def body(dxn_ref, x_ref, g_ref, dp_ref, w_ref, dx_ref, gg_ref):
        @pl.when(pl.program_id(0) == 0)
        def _():
            gg_ref[...] = jnp.zeros_like(gg_ref)

        dh = jnp.zeros((TILE_M, D_MODEL), F32)
        for k in range(N_DEV):
            dh = dh + _dot_nt(dp_ref[:, k * WIDTH:(k + 1) * WIDTH], w_ref[k])
        rs, xn = _rms(x_ref[...])
        gg_ref[...] += jnp.sum(dh * xn, axis=0, keepdims=True)
        dxn = dh * g_ref[layer:layer + 1, :]
        dx_ref[...] = dxn_ref[...] + rs * (dxn - xn * jnp.mean(dxn * xn, axis=-1, keepdims=True))

    return _pcall(
        body, name=f"proj_dgrad_l{layer}",
        out_shape=(SDS((SEQ, D_MODEL), F32), SDS((1, D_MODEL), F32)),
        grid=(SEQ // TILE_M,),
        in_specs=[pl.BlockSpec((TILE_M, D_MODEL), lambda i: (i, 0)),
                  pl.BlockSpec((TILE_M, D_MODEL), lambda i: (i, 0)),
                  _const((DEPTH, D_MODEL)),
                  pl.BlockSpec((TILE_M, N_IN), lambda i: (i, 0)),
                  _const((N_DEV, D_MODEL, WIDTH))],
        out_specs=(pl.BlockSpec((TILE_M, D_MODEL), lambda i: (i, 0)), _const((1, D_MODEL))),
        args=[dx_next, x, norm_g, dproj, wg_in], sem=("arbitrary",), carry=carry)


def _my_place():
    return lax.axis_index("x"), lax.axis_index("y"), lax.axis_index("c")


def _gather_plan(shards, layer):
    n = len(shards)

    def parts(ins, outs, sems):
        send_sems, recv_sems, local_sems = sems
        x, y, c = _my_place()
        chips = [(1 - x, y), (x, 1 - y), (1 - x, 1 - y)]

        def rows(t, place):
            px, py, pc = place
            return outs[t].at[pl.ds(4 * px + 2 * py + pc, 1)]

        def copy(t, k, block, to, from_src=False):
            return pltpu.make_async_remote_copy(
                src_ref=ins[t].at[pl.ds(layer, 1)] if from_src else rows(t, block), dst_ref=rows(t, block),
                send_sem=send_sems.at[7 * t + k], recv_sem=recv_sems.at[7 * t + k], device_id=to,
                device_id_type=MESH)

        def mine(t):
            return pltpu.make_async_copy(ins[t].at[pl.ds(layer, 1)], rows(t, (x, y, c)), local_sems.at[t])

        return (x, y, c), chips, copy, mine

    def start(ins, outs, sems):
        me, chips, copy, mine = parts(ins, outs, sems)
        x, y, c = me
        for t in range(n):
            mine(t).start()
            copy(t, 0, me, (x, y, 1 - c), from_src=True).start()
            for j, chip in enumerate(chips):
                copy(t, 1 + j, me, (*chip, c), from_src=True).start()

    def finish(ins, outs, sems):
        me, chips, copy, mine = parts(ins, outs, sems)
        x, y, c = me
        sibling = (x, y, 1 - c)
        for t in range(n):
            for j, chip in enumerate(chips):
                copy(t, 1 + j, (*chip, c), me).wait_recv()
                copy(t, 4 + j, (*chip, c), sibling).start()
        for t in range(n):
            copy(t, 0, sibling, me).wait_recv()
            for j, chip in enumerate(chips):
                copy(t, 4 + j, (*chip, 1 - c), me).wait_recv()
            for k in range(7):
                copy(t, k, me, sibling, from_src=k < 4).wait_send()
            mine(t).wait()

    out_shape = [SDS((N_DEV,) + a.shape[1:], a.dtype) for a in shards]
    sems = [pltpu.SemaphoreType.DMA((7 * n,)), pltpu.SemaphoreType.DMA((7 * n,)), pltpu.SemaphoreType.DMA((n,))]
    return _Carried(shards, out_shape, sems, start, finish)


class _Carried:
    def __init__(self, ins, out_shape, sems, start, finish):
        self.ins, self.out_shape, self.sems = list(ins), list(out_shape), list(sems)
        self.start, self.finish = start, finish


def _pcall(body, *, name, grid, in_specs, out_specs, out_shape, args, scratch_shapes=(), aliases=None,
           sem=None, limit=VMEM_LIMIT, carry=None):
    out_shape, out_specs, scratch_shapes = list(out_shape), list(out_specs), list(scratch_shapes)
    n_in, n_out, n_scr = len(args), len(out_shape), len(scratch_shapes)
    if carry is None:
        kern, c_ins, c_out, c_sems = body, [], [], []
    else:
        c_ins, c_out, c_sems = carry.ins, carry.out_shape, carry.sems
        ci, co = len(c_ins), len(c_out)
        steps = tuple(grid)

        def kern(*refs):
            o0 = n_in + ci
            s0 = o0 + n_out + co
            mine = refs[:n_in] + refs[o0:o0 + n_out] + refs[s0:s0 + n_scr]
            theirs = (refs[n_in:o0], refs[o0 + n_out:s0], refs[s0 + n_scr:])
            first = pl.program_id(0) == 0
            last = pl.program_id(0) == steps[0] - 1
            for a in range(1, len(steps)):
                first = jnp.logical_and(first, pl.program_id(a) == 0)
                last = jnp.logical_and(last, pl.program_id(a) == steps[a] - 1)

            @pl.when(first)
            def _():
                carry.start(*theirs)

            body(*mine)

            @pl.when(last)
            def _():
                carry.finish(*theirs)

        sem = ("arbitrary",) * len(steps)
    res = pl.pallas_call(
        kern, name=name, grid=tuple(grid),
        in_specs=list(in_specs) + [ANY] * len(c_ins),
        out_specs=tuple(out_specs + [ANY] * len(c_out)),
        out_shape=tuple(out_shape + c_out),
        scratch_shapes=scratch_shapes + c_sems,
        input_output_aliases=aliases or {},
        compiler_params=_cp(sem, limit),
    )(*args, *c_ins)
    return res[:n_out], res[n_out:]


def _run_carried(name, carry):
    ci, co = len(carry.ins), len(carry.out_shape)

    def body(*refs):
        parts = (refs[:ci], refs[ci:ci + co], refs[ci + co:])
        carry.start(*parts)
        carry.finish(*parts)

    return pl.pallas_call(
        body, name=name, out_shape=tuple(carry.out_shape),
        in_specs=[ANY] * ci, out_specs=tuple([ANY] * co), scratch_shapes=carry.sems,
    )(*carry.ins)


def _sibling_plan(big, small):
    n = len(big)
    n_copies = 4 * n + len(small)

    def copies(ins, outs, sems):
        send_sems, recv_sems = sems
        x, y, c = _my_place()
        pairs = []
        for t, (_, layer) in enumerate(big):
            for s in range(4):
                pairs.append((ins[t].at[layer, pl.ds(4 * (1 - c) + s, 1)], outs[t].at[pl.ds(s, 1)]))
        pairs += list(zip(ins[n:], outs[n:]))
        return [pltpu.make_async_remote_copy(
            src_ref=src, dst_ref=dst, send_sem=send_sems.at[k], recv_sem=recv_sems.at[k],
            device_id=(x, y, 1 - c), device_id_type=MESH) for k, (src, dst) in enumerate(pairs)]

    def start(ins, outs, sems):
        for cp in copies(ins, outs, sems):
            cp.start()

    def finish(ins, outs, sems):
        for cp in copies(ins, outs, sems):
            cp.wait()

    out_shape = [SDS((4,) + a.shape[2:], a.dtype) for a, _ in big] + [SDS(a.shape, a.dtype) for a in small]
    sems = [pltpu.SemaphoreType.DMA((n_copies,)), pltpu.SemaphoreType.DMA((n_copies,))]
    return _Carried([a for a, _ in big] + list(small), out_shape, sems, start, finish)


def _chips_plan(big, small):
    n, n_small = len(big), len(small)
    max_rows = 512
    parts = [max(1, a.shape[1] // max_rows) for a in big]
    n_copies = 3 * (sum(parts) + n_small)

    def copies(ins, outs, sems, landing):
        send_sems, recv_sems, local_sems = sems
        x, y, c = _my_place()
        my_chip = 2 * x + y
        chips = [(1 - x, y), (x, 1 - y), (1 - x, 1 - y)]
        remote, local = [], []
        for chip in chips:
            to = 2 * chip[0] + chip[1]
            slot = to if landing else my_chip
            pairs = []
            for t in range(n):
                rows_per = big[t].shape[1] // parts[t]
                for p in range(parts[t]):
                    rows = pl.ds(p * rows_per, rows_per)
                    pairs.append((ins[t].at[to, rows], outs[t].at[slot, rows]))
            pairs += [(ins[t], outs[t].at[slot]) for t in range(n, n + n_small)]
            for src, dst in pairs:
                k = len(remote)
                remote.append(pltpu.make_async_remote_copy(
                    src_ref=src, dst_ref=dst, send_sem=send_sems.at[k], recv_sem=recv_sems.at[k],
                    device_id=(*chip, c), device_id_type=MESH))
        for t in range(n):
            local.append(pltpu.make_async_copy(ins[t].at[my_chip], outs[t].at[my_chip], local_sems.at[t]))
        for t in range(n, n + n_small):
            local.append(pltpu.make_async_copy(ins[t], outs[t].at[my_chip], local_sems.at[t]))
        return remote + local

    def start(ins, outs, sems):
        for cp in copies(ins, outs, sems, landing=False):
            cp.start()

    def finish(ins, outs, sems):
        for cp in copies(ins, outs, sems, landing=True):
            cp.wait()

    out_shape = [SDS(a.shape, a.dtype) for a in big] + [SDS((N_CHIP,) + a.shape, a.dtype) for a in small]
    sems = [pltpu.SemaphoreType.DMA((n_copies,)), pltpu.SemaphoreType.DMA((n_copies,)),
            pltpu.SemaphoreType.DMA((n + n_small,))]
    return _Carried(list(big) + list(small), out_shape, sems, start, finish)


def _row_block(rows):
    return rows if rows <= 256 else 256


def _add_own(tag, core, g, layer, got):
    _, r, c = got.shape
    rb = _row_block(r)

    def body(core_ref, a_ref, b_ref, o_ref):
        o_ref[...] = (a_ref[...] + b_ref[...]).astype(o_ref.dtype)

    return pl.pallas_call(
        body, name=f"add_{tag}", out_shape=SDS(got.shape, BF16),
        grid_spec=pltpu.PrefetchScalarGridSpec(
            num_scalar_prefetch=1, grid=(4, r // rb),
            in_specs=[pl.BlockSpec((None, None, rb, c), lambda s, j, core: (layer, 4 * core[0] + s, j, 0)),
                      pl.BlockSpec((None, rb, c), lambda s, j, core: (s, j, 0))],
            out_specs=pl.BlockSpec((None, rb, c), lambda s, j, core: (s, j, 0))),
        compiler_params=_cp(("parallel", "parallel")),
    )(core, g, got)


def _add_lists(tag, own, got, grid=None, specs=None):
    n = len(own)

    def body(*refs):
        for a, b, o in zip(refs[:n], refs[n:2 * n], refs[2 * n:]):
            o[...] = a[...] + b[...]

    kw = {}
    if grid is not None:
        kw = dict(grid=grid, in_specs=list(specs) * 2, out_specs=tuple(specs),
                  compiler_params=_cp(("parallel",) * len(grid)))
    return pl.pallas_call(
        body, name=f"add_{tag}", out_shape=tuple(SDS(a.shape, a.dtype) for a in own), **kw)(*own, *got)


def _adamw_math(w, g, m, v):
    m = ADAM_B1 * m + (1.0 - ADAM_B1) * g
    v = ADAM_B2 * v + (1.0 - ADAM_B2) * (g * g)
    m_hat = m / (1.0 - ADAM_B1 ** ADAM_STEP)
    v_hat = v / (1.0 - ADAM_B2 ** ADAM_STEP)
    delta = -ADAM_LR * (m_hat / (jnp.sqrt(v_hat) + ADAM_EPS) + ADAM_WD * w)
    return delta, m, v


def _sum_slots_adamw(tag, slots, w, m, v):
    _, r, c = slots[0].shape
    rb = _row_block(r)

    def body(s0_ref, s1_ref, w_ref, m_ref, v_ref, g_ref, d_ref, nm_ref, nv_ref):
        first = pl.program_id(1) == 0
        g = _sum4([jnp.where(first, s0_ref[k], s1_ref[k]).astype(F32) for k in range(N_CHIP)])
        delta, nm, nv = _adamw_math(w_ref[...], g, m_ref[...], v_ref[...])
        g_ref[...] = g
        d_ref[...] = delta
        nm_ref[...] = nm
        nv_ref[...] = nv

    spec = pl.BlockSpec((None, rb, c), lambda j, l: (l, j, 0))
    sspec = pl.BlockSpec((N_CHIP, rb, c), lambda j, l: (0, j, 0))
    s = SDS((DEPTH, r, c), F32)
    return pl.pallas_call(
        body, name=f"adamw_{tag}", out_shape=(s, s, s, s),
        grid=(r // rb, DEPTH), in_specs=[sspec, sspec, spec, spec, spec], out_specs=(spec, spec, spec, spec),
        compiler_params=_cp(("parallel", "arbitrary")),
    )(*slots, w, m, v)


def _adamw_small(tag, entries, grid=None):
    flat_in, in_specs, out_shape, out_specs, layout = [], [], [], [], []
    for slots, w, m, v, slot_spec, w_spec in entries:
        per_layer = isinstance(slots, (list, tuple))
        n_slot = len(slots) if per_layer else 1
        flat_in += (list(slots) if per_layer else [slots]) + [w, m, v]
        in_specs += [slot_spec] * n_slot + [w_spec] * 3
        out_shape += [SDS(w.shape, F32)] * 4
        out_specs += [w_spec] * 4
        layout.append((per_layer, n_slot))
    n_in = len(flat_in)

    def body(*refs):
        i, o = 0, n_in
        for per_layer, n_slot in layout:
            s_refs = refs[i:i + n_slot]
            w_ref, m_ref, v_ref = refs[i + n_slot:i + n_slot + 3]
            outs = refs[o:o + 4]
            if per_layer:
                for l, s_ref in enumerate(s_refs):
                    at = (slice(l, l + 1),) if len(w_ref.shape) == 2 else (l,)
                    g = _sum4([s_ref[k] for k in range(N_CHIP)])
                    res = (g,) + _adamw_math(w_ref[at], g, m_ref[at], v_ref[at])
                    for o_ref, val in zip(outs, res):
                        o_ref[at] = val
            else:
                g = _sum4([s_refs[0][k] for k in range(N_CHIP)])
                res = (g,) + _adamw_math(w_ref[...], g, m_ref[...], v_ref[...])
                for o_ref, val in zip(outs, res):
                    o_ref[...] = val
            i += n_slot + 3
            o += 4

    kw = {}
    if grid is not None:
        kw = dict(grid=grid, in_specs=in_specs, out_specs=tuple(out_specs),
                  compiler_params=_cp(("parallel",) * len(grid)))
    res = pl.pallas_call(body, name=f"adamw_{tag}", out_shape=tuple(out_shape), **kw)(*flat_in)
    return [tuple(res[4 * e:4 * e + 4]) for e in range(len(entries))]


def kernel(x, norm_g, w_in, b_in, ssm_log_dt, ssm_lam_re, ssm_lam_im, ssm_b_re, ssm_b_im, ssm_c_re, ssm_c_im, ssm_d, ssm_w_glu, ssm_b_glu, pool_w, pool_scale, w_branch_a, w_branch_b, w_out, final_norm_g, loss_target, m_norm_g, m_w_in, m_b_in, m_ssm_log_dt, m_ssm_lam_re, m_ssm_lam_im, m_ssm_b_re, m_ssm_b_im, m_ssm_c_re, m_ssm_c_im, m_ssm_d, m_ssm_w_glu, m_ssm_b_glu, m_pool_w, m_pool_scale, m_w_branch_a, m_w_branch_b, m_w_out, m_final_norm_g, v_norm_g, v_w_in, v_b_in, v_ssm_log_dt, v_ssm_lam_re, v_ssm_lam_im, v_ssm_b_re, v_ssm_b_im, v_ssm_c_re, v_ssm_c_im, v_ssm_d, v_ssm_w_glu, v_ssm_b_glu, v_pool_w, v_pool_scale, v_w_branch_a, v_w_branch_b, v_w_out, v_final_norm_g):
    weights = dict(norm_g=norm_g, w_in=w_in, b_in=b_in, ssm_log_dt=ssm_log_dt, ssm_lam_re=ssm_lam_re,
                   ssm_lam_im=ssm_lam_im, ssm_b_re=ssm_b_re, ssm_b_im=ssm_b_im, ssm_c_re=ssm_c_re,
                   ssm_c_im=ssm_c_im, ssm_d=ssm_d, ssm_w_glu=ssm_w_glu, ssm_b_glu=ssm_b_glu, pool_w=pool_w,
                   pool_scale=pool_scale, w_branch_a=w_branch_a, w_branch_b=w_branch_b, w_out=w_out,
                   final_norm_g=final_norm_g.reshape(1, D_MODEL))
    mom_m = dict(norm_g=m_norm_g, w_in=m_w_in, b_in=m_b_in, ssm_log_dt=m_ssm_log_dt, ssm_lam_re=m_ssm_lam_re,
                 ssm_lam_im=m_ssm_lam_im, ssm_b_re=m_ssm_b_re, ssm_b_im=m_ssm_b_im, ssm_c_re=m_ssm_c_re,
                 ssm_c_im=m_ssm_c_im, ssm_d=m_ssm_d, ssm_w_glu=m_ssm_w_glu, ssm_b_glu=m_ssm_b_glu,
                 pool_w=m_pool_w, pool_scale=m_pool_scale, w_branch_a=m_w_branch_a, w_branch_b=m_w_branch_b,
                 w_out=m_w_out, final_norm_g=m_final_norm_g.reshape(1, D_MODEL))
    mom_v = dict(norm_g=v_norm_g, w_in=v_w_in, b_in=v_b_in, ssm_log_dt=v_ssm_log_dt, ssm_lam_re=v_ssm_lam_re,
                 ssm_lam_im=v_ssm_lam_im, ssm_b_re=v_ssm_b_re, ssm_b_im=v_ssm_b_im, ssm_c_re=v_ssm_c_re,
                 ssm_c_im=v_ssm_c_im, ssm_d=v_ssm_d, ssm_w_glu=v_ssm_w_glu, ssm_b_glu=v_ssm_b_glu,
                 pool_w=v_pool_w, pool_scale=v_pool_scale, w_branch_a=v_w_branch_a, w_branch_b=v_w_branch_b,
                 w_out=v_w_out, final_norm_g=v_final_norm_g.reshape(1, D_MODEL))
    order = ["norm_g", "w_in", "b_in", "ssm_log_dt", "ssm_lam_re", "ssm_lam_im", "ssm_b_re", "ssm_b_im",
             "ssm_c_re", "ssm_c_im", "ssm_d", "ssm_w_glu", "ssm_b_glu", "pool_w", "pool_scale", "w_branch_a",
             "w_branch_b", "w_out", "final_norm_g"]
    big_names = ["w_in", "ssm_w_glu", "w_branch_a", "w_branch_b", "w_out"]

    log_dt3 = ssm_log_dt.reshape(DEPTH, N_GROUP, 1)
    bt_re = ssm_b_re.transpose(0, 3, 1, 2)
    bt_im = ssm_b_im.transpose(0, 3, 1, 2)
    abar_re, abar_im, bbt_re, bbt_im = _s5_params(log_dt3, ssm_lam_re, ssm_lam_im, bt_re, bt_im)
    s5_args = (bbt_re, bbt_im, ssm_c_re, ssm_c_im, abar_re, abar_im, ssm_d)

    w16 = {n: weights[n].astype(BF16) for n in big_names}
    rest = [w16[n] for n in big_names[1:]]
    wg_in = [None, None]
    wg_rest = [None, None]
    (wg_in[0],) = _run_carried("gather_w_in_l0", _gather_plan([w16["w_in"]], 0))
    xs = [x.reshape(SEQ, D_MODEL)]
    saved = []
    for l in range(DEPTH):
        proj, moved = _norm_proj(l, xs[l], norm_g, wg_in[l], b_in, carry=_gather_plan(rest, 0) if l == 0 else None)
        if l == 0:
            wg_rest[0] = moved
        (s_re, s_im, y0), moved = _s5_fwd(
            l, proj, *s5_args, carry=_gather_plan([w16["w_in"]], 1) if l == 0 else None)
        if l == 0:
            (wg_in[1],) = moved
        pooled = _pool_fwd(l, proj)
        wg_glu, wg_a, wg_b, wg_out = wg_rest[l]
        x_next, moved = _mix_fwd(l, xs[l], proj, y0, pooled, wg_glu, ssm_b_glu, pool_w, pool_scale, wg_a, wg_b,
                                 wg_out, carry=_gather_plan(rest, 1) if l == 0 else None)
        if l == 0:
            wg_rest[1] = moved
        xs.append(x_next)
        saved.append((proj, s_re, s_im, y0, pooled))

    dx, loss_part, g_final = _loss_head(xs[DEPTH], loss_target.reshape(SEQ, D_MODEL), weights["final_norm_g"])
    loss = lax.psum(loss_part[0, 0], ("x", "y", "c"))

    core = lax.axis_index("c").astype(jnp.int32).reshape(1)
    vec_names = ["norm_g", "b_in", "ssm_d", "ssm_b_glu", "pool_scale"]
    flat_names = ["final_norm_g", "ssm_log_dt", "ssm_lam_re", "ssm_lam_im"]
    mat_names = ["pool_w", "ssm_c_re", "ssm_c_im"]
    lane_sparse = {"ssm_c_re": ssm_c_re.shape[1:], "ssm_c_im": ssm_c_im.shape[1:],
                   "ssm_b_re": ssm_b_re.shape, "ssm_b_im": ssm_b_im.shape}
    gridded = set(mat_names) | {"ssm_b_re", "ssm_b_im"}

    def dense(key, a):
        return a.reshape(-1, 128) if key[0] in lane_sparse else a

    def undense(key, slots):
        return slots.reshape((N_CHIP,) + lane_sparse[key[0]]) if key[0] in lane_sparse else slots

    def add_small(tag, keys, own, got):
        out = [None] * len(keys)
        whole = [i for i, k in enumerate(keys) if k[0] not in gridded]
        tiled = [i for i, k in enumerate(keys) if k[0] in gridded]
        for i, r in zip(whole, _add_lists(f"{tag}_a", [own[i] for i in whole], [got[i] for i in whole])):
            out[i] = r
        specs = [pl.BlockSpec((1, POOL_GROUP, POOL_GROUP), lambda j: (j, 0, 0)) if keys[i][0] == "pool_w"
                 else pl.BlockSpec((own[i].shape[0] // N_CHUNK, 128), lambda j: (j, 0)) for i in tiled]
        for i, r in zip(tiled, _add_lists(f"{tag}_b", [own[i] for i in tiled], [got[i] for i in tiled],
                                          grid=(N_CHUNK,), specs=specs)):
            out[i] = r
        return out

    sm = {("final_norm_g", None): g_final}
    g_abar_re, g_abar_im, g_bbt_re, g_bbt_im = ([None] * DEPTH for _ in range(4))
    mix_big, gw_in = None, None
    keys1 = ([(n, 1) for n in vec_names[1:]] + [(n, 1) for n in mat_names] + [("final_norm_g", None)])
    chip1_big = chip1_small = slots1_big = slots1_small = None
    for l in reversed(range(DEPTH)):
        proj, s_re, s_im, y0, pooled = saved[l]
        wg_glu, wg_a, wg_b, wg_out = wg_rest[l]
        carry = None if l == 1 else _chips_plan(chip1_big[:1], [])
        res, moved = _mix_bwd(l, dx, proj, y0, pooled, wg_glu, ssm_b_glu, pool_w, pool_scale, wg_a, wg_b, wg_out,
                              mix_big, carry=carry)
        if l == 0:
            slots1_big = list(moved)
        dproj, dy0, dpooled = res[:3]
        mix_big = list(res[3:7])
        sm[("pool_w", l)], sm[("pool_scale", l)], sm[("ssm_b_glu", l)] = res[7:]
        dproj = _pool_bwd(l, dpooled, dproj)
        carry = None if l == 1 else _chips_plan(chip1_big[1:], chip1_small)
        res, moved = _s5_bwd(l, dy0, proj, s_re, s_im, *s5_args, dproj, carry=carry)
        if l == 0:
            slots1_big += list(moved[:4])
            slots1_small = moved[4:]
        (dproj, g_bbt_re[l], g_bbt_im[l], sm[("ssm_c_re", l)], sm[("ssm_c_im", l)], g_abar_re[l], g_abar_im[l],
         sm[("ssm_d", l)]) = res
        gw_in, sm[("b_in", l)] = _proj_wgrad(l, xs[l], norm_g, dproj, gw_in)
        gw_out, gw_a, gw_b, gw_glu = mix_big
        big_part = [gw_in, gw_glu, gw_a, gw_b, gw_out]
        carry = None
        if l == 1:
            own1 = [dense(k, sm[k]) for k in keys1]
            carry = _sibling_plan([(a, 1) for a in big_part], own1)
        (dx, sm[("norm_g", l)]), moved = _proj_dgrad(l, dx, xs[l], norm_g, dproj, wg_in[l], carry=carry)
        if l == 1:
            chip1_big = [_add_own(f"chip1_{n}", core, a, 1, b) for n, a, b in zip(big_names, big_part, moved[:5])]
            chip1_small = add_small("chip1_small", keys1, own1, moved[5:])
    grad_x = dx.reshape(1, SEQ, D_MODEL)

    g_ld, g_lr, g_li, g_btr, g_bti = _s5_params_bwd(
        log_dt3, ssm_lam_re, ssm_lam_im, bt_re, bt_im, g_abar_re, g_abar_im, g_bbt_re, g_bbt_im)
    sm[("ssm_log_dt", None)] = g_ld.reshape(DEPTH, N_GROUP)
    sm[("ssm_lam_re", None)] = g_lr
    sm[("ssm_lam_im", None)] = g_li
    sm[("ssm_b_re", None)] = g_btr.transpose(0, 2, 3, 1)
    sm[("ssm_b_im", None)] = g_bti.transpose(0, 2, 3, 1)

    keys0 = ([(n, 0) for n in vec_names] + [("norm_g", 1)] + [(n, 0) for n in mat_names]
             + [(n, None) for n in ("ssm_log_dt", "ssm_lam_re", "ssm_lam_im", "ssm_b_re", "ssm_b_im")])
    own0 = [dense(k, sm[k]) for k in keys0]
    moved = _run_carried("exchange_sibling_l0", _sibling_plan([(a, 0) for a in big_part], own0))
    chip0_big = [_add_own(f"chip0_{n}", core, a, 0, b) for n, a, b in zip(big_names, big_part, moved[:5])]
    chip0_small = add_small("chip0_small", keys0, own0, moved[5:])
    moved = _run_carried("exchange_chips_l0", _chips_plan(chip0_big, chip0_small))
    slots0_big = moved[:5]
    slots = {k: undense(k, s) for k, s in zip(keys0, moved[5:])}
    slots.update({k: undense(k, s) for k, s in zip(keys1, slots1_small)})

    res = {}
    for i, n in enumerate(big_names):
        res[n] = _sum_slots_adamw(n, [slots0_big[i], slots1_big[i]], weights[n], mom_m[n], mom_v[n])
    entries_a = []
    for n in vec_names:
        entries_a.append(([slots[(n, l)] for l in range(DEPTH)], weights[n], mom_m[n], mom_v[n], None, None))
    for n in flat_names:
        entries_a.append((slots[(n, None)], weights[n], mom_m[n], mom_v[n], None, None))
    out_a = _adamw_small("small_a", entries_a)
    for n, r in zip(vec_names + flat_names, out_a):
        res[n] = r
    res["final_norm_g"] = tuple(a.reshape(D_MODEL) for a in res["final_norm_g"])
    pw_s = pl.BlockSpec((N_CHIP, 1, POOL_GROUP, POOL_GROUP), lambda j: (0, j, 0, 0))
    pw_w = pl.BlockSpec((DEPTH, 1, POOL_GROUP, POOL_GROUP), lambda j: (0, j, 0, 0))
    c_s = pl.BlockSpec((N_CHIP, CH_G, GROUP_W, STATE), lambda j: (0, j, 0, 0))
    c_w = pl.BlockSpec((DEPTH, CH_G, GROUP_W, STATE), lambda j: (0, j, 0, 0))
    b_s = pl.BlockSpec((N_CHIP, DEPTH, CH_G, STATE, GROUP_W), lambda j: (0, 0, j, 0, 0))
    b_w = pl.BlockSpec((DEPTH, CH_G, STATE, GROUP_W), lambda j: (0, j, 0, 0))
    entries_b = []
    for n, s_spec, w_spec in (("pool_w", pw_s, pw_w), ("ssm_c_re", c_s, c_w), ("ssm_c_im", c_s, c_w)):
        entries_b.append(([slots[(n, l)] for l in range(DEPTH)], weights[n], mom_m[n], mom_v[n], s_spec, w_spec))
    for n in ("ssm_b_re", "ssm_b_im"):
        entries_b.append((slots[(n, None)], weights[n], mom_m[n], mom_v[n], b_s, b_w))
    out_b = _adamw_small("small_b", entries_b, grid=(N_CHUNK,))
    for n, r in zip(mat_names + ["ssm_b_re", "ssm_b_im"], out_b):
        res[n] = r

    outs = [loss, grad_x]
    for i in range(4):
        outs += [res[n][i] for n in order]
    return tuple(outs)
```

```python
import math

import jax
import jax.numpy as jnp
from jax import lax
from jax.experimental import pallas as pl
from jax.experimental.pallas import tpu as pltpu

F32 = jnp.float32
BF16 = jnp.bfloat16

SEQ = 2048
D_MODEL = 1024
N_IN = 4096
WIDTH = 512
N_GROUP = 32
GROUP_W = 16
STATE = 64
N_STATE = N_GROUP * STATE
N_CHUNK = 4
CH_G = N_GROUP // N_CHUNK
CH_W = WIDTH // N_CHUNK
CH_S = N_STATE // N_CHUNK
N_DEV = 8
N_CHIP = 4
POOL_WINDOWS = (2, 4, 8, 16)
POOL_GROUP = 128
EPS = 1e-6
DEPTH = 2

ADAM_LR = 0.001
ADAM_B1 = 0.9
ADAM_B2 = 0.999
ADAM_EPS = 1e-08
ADAM_WD = 0.01
ADAM_STEP = 10

TILE_M = 256
ROW_BLK = 512
VMEM_LIMIT = 48 * 1024 * 1024
VMEM_LIMIT_BIG = 60 * 1024 * 1024
MESH = pl.DeviceIdType.MESH
ANY = pl.BlockSpec(memory_space=pl.ANY)

GELU_C = math.sqrt(2.0 / math.pi)
GELU_A = 0.044715

SDS = jax.ShapeDtypeStruct


def _cp(sem=None, limit=VMEM_LIMIT):
    return pltpu.CompilerParams(dimension_semantics=sem, vmem_limit_bytes=limit)


def _dot(a, b):
    return jnp.dot(a, b, preferred_element_type=F32)


def _dot_nt(a, b):
    return lax.dot_general(a, b, (((1,), (1,)), ((), ())), preferred_element_type=F32)


def _dot_tn(a, b):
    return lax.dot_general(a, b, (((0,), (0,)), ((), ())), preferred_element_type=F32)


def _sig(x):
    return jax.nn.sigmoid(x)


def _rms(x):
    rs = lax.rsqrt(jnp.mean(x * x, axis=-1, keepdims=True) + EPS)
    return rs, x * rs


def _slot(n):
    return 4 * (n % 2) + n // 2


def _const(shape):
    n = len(shape)
    return pl.BlockSpec(shape, lambda *_: (0,) * n)


def _sum4(p):
    return (p[0] + p[1]) + (p[2] + p[3])


def _s5_param_fn(log_dt, lam_re, lam_im, bt_re, bt_im):
    dt = jnp.exp(log_dt)
    mag = jnp.exp(lam_re * dt)
    ang = lam_im * dt
    abar_re = mag * jnp.cos(ang)
    abar_im = mag * jnp.sin(ang)
    num_re = abar_re - 1.0
    num_im = abar_im
    den = lam_re * lam_re + lam_im * lam_im
    coef_re = (num_re * lam_re + num_im * lam_im) / den
    coef_im = (num_im * lam_re - num_re * lam_im) / den
    bbar_re = coef_re[:, :, None] * bt_re - coef_im[:, :, None] * bt_im
    bbar_im = coef_re[:, :, None] * bt_im + coef_im[:, :, None] * bt_re
    return abar_re, abar_im, bbar_re, bbar_im


def _s5_params(log_dt, lam_re, lam_im, bt_re, bt_im):
    def body(ld, lr, li, br, bi, o_ar, o_ai, o_br, o_bi):
        ar, ai, bbr, bbi = _s5_param_fn(ld[...], lr[...], li[...], br[...], bi[...])
        o_ar[...] = ar
        o_ai[...] = ai
        o_br[...] = bbr
        o_bi[...] = bbi

    return pl.pallas_call(
        body, name="s5_params",
        out_shape=(SDS(lam_re.shape, F32), SDS(lam_re.shape, F32), SDS(bt_re.shape, F32), SDS(bt_re.shape, F32)),
    )(log_dt, lam_re, lam_im, bt_re, bt_im)


def _s5_params_bwd(log_dt, lam_re, lam_im, bt_re, bt_im, g_ar, g_ai, g_br, g_bi):
    def body(ld, lr, li, br, bi, car0, car1, cai0, cai1, cbr0, cbr1, cbi0, cbi1, o_ld, o_lr, o_li, o_br, o_bi):
        _, vjp = jax.vjp(_s5_param_fn, ld[...], lr[...], li[...], br[...], bi[...])
        both = lambda a, b: jnp.stack([a[...], b[...]], axis=0)
        d_ld, d_lr, d_li, d_br, d_bi = vjp((both(car0, car1), both(cai0, cai1), both(cbr0, cbr1), both(cbi0, cbi1)))
        o_ld[...] = d_ld
        o_lr[...] = d_lr
        o_li[...] = d_li
        o_br[...] = d_br
        o_bi[...] = d_bi

    return pl.pallas_call(
        body, name="s5_params_bwd",
        out_shape=(SDS(log_dt.shape, F32), SDS(lam_re.shape, F32), SDS(lam_re.shape, F32),
                   SDS(bt_re.shape, F32), SDS(bt_re.shape, F32)),
    )(log_dt, lam_re, lam_im, bt_re, bt_im, *g_ar, *g_ai, *g_br, *g_bi)


def _norm_proj(layer, x, norm_g, wg_in, b_in, carry=None):
    def body(x_ref, g_ref, w_ref, b_ref, o_ref):
        _, xn = _rms(x_ref[...])
        h = (xn * g_ref[layer:layer + 1, :]).astype(BF16)
        for k in range(N_DEV):
            cols = slice(k * WIDTH, (k + 1) * WIDTH)
            o_ref[:, cols] = _dot(h, w_ref[k]) + b_ref[layer:layer + 1, cols]

    (proj,), moved = _pcall(
        body, name=f"norm_proj_l{layer}",
        out_shape=[SDS((SEQ, N_IN), F32)],
        grid=(SEQ // TILE_M,),
        in_specs=[pl.BlockSpec((TILE_M, D_MODEL), lambda i: (i, 0)),
                  _const((DEPTH, D_MODEL)),
                  _const((N_DEV, D_MODEL, WIDTH)),
                  _const((DEPTH, N_IN))],
        out_specs=[pl.BlockSpec((TILE_M, N_IN), lambda i: (i, 0))],
        args=[x, norm_g, wg_in, b_in], sem=("parallel",), carry=carry)
    return proj, moved


def _scan_tile_rows():
    return lax.broadcasted_iota(jnp.int32, (8, CH_S), 0)


def _s5_param_specs(layer):
    bt = lambda: pl.BlockSpec((None, CH_G, GROUP_W, STATE), lambda j: (layer, j, 0, 0))
    cc = lambda: pl.BlockSpec((None, CH_G, GROUP_W, STATE), lambda j: (layer, j, 0, 0))
    ab = lambda: pl.BlockSpec((None, CH_G, STATE), lambda j: (layer, j, 0))
    return [bt(), bt(), cc(), cc(), ab(), ab(), pl.BlockSpec((DEPTH, CH_W), lambda j: (0, j))]


def _s5_param_scratch():
    return [pltpu.VMEM((CH_W, CH_S), BF16)] * 4 + [pltpu.VMEM((1, CH_S), F32)] * 2


def _s5_fill(btre_ref, btim_ref, cre_ref, cim_ref, are_ref, aim_ref, bdre, bdim, ctre, ctim, arow, airow):
    for m in (bdre, bdim, ctre, ctim):
        m[...] = jnp.zeros_like(m)
    for g in range(CH_G):
        rows = slice(g * GROUP_W, (g + 1) * GROUP_W)
        cols = slice(g * STATE, (g + 1) * STATE)
        bdre[rows, cols] = btre_ref[g].astype(BF16)
        bdim[rows, cols] = btim_ref[g].astype(BF16)
        ctre[rows, cols] = cre_ref[g].astype(BF16)
        ctim[rows, cols] = cim_ref[g].astype(BF16)
        arow[:, cols] = are_ref[g:g + 1, :]
        airow[:, cols] = aim_ref[g:g + 1, :]


def _s5_fwd(layer, proj, bbt_re, bbt_im, c_re, c_im, abar_re, abar_im, d_skip, carry=None):
    def body(u_ref, btre_ref, btim_ref, cre_ref, cim_ref, are_ref, aim_ref, d_ref,
             sre_ref, sim_ref, y_ref, bdre, bdim, ctre, ctim, arow, airow):
        _s5_fill(btre_ref, btim_ref, cre_ref, cim_ref, are_ref, aim_ref, bdre, bdim, ctre, ctim, arow, airow)
        for rb in range(SEQ // ROW_BLK):
            rows = pl.ds(rb * ROW_BLK, ROW_BLK)
            ub = u_ref[rows, :].astype(BF16)
            sre_ref[rows, :] = _dot(ub, bdre[...])
            sim_ref[rows, :] = _dot(ub, bdim[...])
        ar = arow[...]
        ai = airow[...]
        row_id = _scan_tile_rows()

        def step(i, carry):
            sr, si = carry
            base = pl.multiple_of(i * 8, 8)
            tr = sre_ref[pl.ds(base, 8), :]
            ti = sim_ref[pl.ds(base, 8), :]
            outr, outi = tr, ti
            for r in range(8):
                nsr = ar * sr - ai * si + tr[r:r + 1, :]
                nsi = ar * si + ai * sr + ti[r:r + 1, :]
                outr = jnp.where(row_id == r, nsr, outr)
                outi = jnp.where(row_id == r, nsi, outi)
                sr, si = nsr, nsi
            sre_ref[pl.ds(base, 8), :] = outr
            sim_ref[pl.ds(base, 8), :] = outi
            return sr, si

        zero = jnp.zeros((1, CH_S), F32)
        lax.fori_loop(0, SEQ // 8, step, (zero, zero))
        d = d_ref[layer:layer + 1, :]
        for rb in range(SEQ // ROW_BLK):
            rows = pl.ds(rb * ROW_BLK, ROW_BLK)
            y = (_dot_nt(sre_ref[rows, :].astype(BF16), ctre[...])
                 - _dot_nt(sim_ref[rows, :].astype(BF16), ctim[...]))
            y_ref[rows, :] = y + d * u_ref[rows, :]

    return _pcall(
        body, name=f"s5_fwd_l{layer}",
        out_shape=(SDS((SEQ, N_STATE), F32), SDS((SEQ, N_STATE), F32), SDS((SEQ, WIDTH), F32)),
        grid=(N_CHUNK,),
        in_specs=[pl.BlockSpec((SEQ, CH_W), lambda j: (0, j))] + _s5_param_specs(layer),
        out_specs=(pl.BlockSpec((SEQ, CH_S), lambda j: (0, j)),
                   pl.BlockSpec((SEQ, CH_S), lambda j: (0, j)),
                   pl.BlockSpec((SEQ, CH_W), lambda j: (0, j))),
        scratch_shapes=_s5_param_scratch(),
        args=[proj, bbt_re, bbt_im, c_re, c_im, abar_re, abar_im, d_skip], sem=("parallel",), carry=carry)


def _pool_counts(win):
    t = lax.broadcasted_iota(jnp.int32, (SEQ, POOL_GROUP), 0)
    return t, jnp.minimum(t + 1, win).astype(F32)


def _pool_fwd(layer, proj):
    def body(u_ref, o_ref):
        for gi, win in enumerate(POOL_WINDOWS):
            cols = slice(gi * POOL_GROUP, (gi + 1) * POOL_GROUP)
            u = u_ref[:, cols]
            t, count = _pool_counts(win)
            acc = u
            k = 1
            while k < win:
                acc = acc + jnp.where(t >= k, pltpu.roll(acc, k, 0), 0.0)
                k *= 2
            o_ref[:, cols] = acc / count - u

    return pl.pallas_call(
        body, name=f"pool_fwd_l{layer}",
        out_shape=SDS((SEQ, WIDTH), F32),
        grid=(1,),
        in_specs=[pl.BlockSpec((SEQ, WIDTH), lambda i: (0, 2))],
        out_specs=pl.BlockSpec((SEQ, WIDTH), lambda i: (0, 0)),
        compiler_params=_cp(("arbitrary",)),
    )(proj)


def _gelu_parts(y0):
    t = jnp.tanh(GELU_C * (y0 + GELU_A * (y0 * y0 * y0)))
    return t, 0.5 * y0 * (1.0 + t)


def _mix_forward(layer, p_ref, y0_ref, pooled_ref, wglu_ref, bglu_ref, pw_ref, scale_ref, wa_ref, wb_ref):
    za = p_ref[:, WIDTH:2 * WIDTH]
    zb = p_ref[:, 3 * WIDTH:4 * WIDTH]
    ga = p_ref[:, 4 * WIDTH:4 * WIDTH + D_MODEL]
    gb = p_ref[:, 4 * WIDTH + D_MODEL:]
    y0 = y0_ref[...]
    t, y1 = _gelu_parts(y0)
    y1b = y1.astype(BF16)
    q = _dot(y1b, wglu_ref[...].reshape(WIDTH, WIDTH)) + bglu_ref[layer:layer + 1, :]
    sq = _sig(q)
    y2 = y1 * sq
    sza = _sig(za)
    silu_za = za * sza
    ya = y2 * silu_za
    pooled = pooled_ref[...]
    mixed = jnp.concatenate(
        [_dot(pooled[:, g * POOL_GROUP:(g + 1) * POOL_GROUP].astype(BF16), pw_ref[g].astype(BF16))
         for g in range(len(POOL_WINDOWS))], axis=1)
    szb = _sig(zb)
    silu_zb = zb * szb
    scale = scale_ref[layer:layer + 1, :]
    ms = mixed * scale
    yb = ms * silu_zb
    yab = ya.astype(BF16)
    ybb = yb.astype(BF16)
    ma = jnp.concatenate([_dot(yab, wa_ref[k]) for k in range(N_DEV)], axis=1)
    mb = jnp.concatenate([_dot(ybb, wb_ref[k]) for k in range(N_DEV)], axis=1)
    sga = _sig(ga)
    sgb = _sig(gb)
    merged = sga * ma + sgb * mb
    return dict(za=za, zb=zb, y0=y0, t=t, y1=y1, y1b=y1b, sq=sq, y2=y2, sza=sza, silu_za=silu_za,
                pooled=pooled, mixed=mixed, szb=szb, silu_zb=silu_zb, scale=scale, ms=ms, yab=yab, ybb=ybb,
                ma=ma, mb=mb, sga=sga, sgb=sgb, merged=merged)


def _mix_weight_specs(layer):
    return [_const((N_DEV, WIDTH // N_DEV, WIDTH)),
            _const((DEPTH, WIDTH)),
            pl.BlockSpec((None, 4, POOL_GROUP, POOL_GROUP), lambda i: (layer, 0, 0, 0)),
            _const((DEPTH, WIDTH)),
            _const((N_DEV, WIDTH, D_MODEL // N_DEV)),
            _const((N_DEV, WIDTH, D_MODEL // N_DEV)),
            _const((N_DEV, D_MODEL // N_DEV, D_MODEL))]


def _mix_fwd(layer, x, proj, y0, pooled, wg_glu, b_glu, pool_w, pool_scale, wg_a, wg_b, wg_out, carry=None):
    def body(x_ref, p_ref, y0_ref, pooled_ref, wglu_ref, bglu_ref, pw_ref, scale_ref, wa_ref, wb_ref,
             wout_ref, o_ref):
        f = _mix_forward(layer, p_ref, y0_ref, pooled_ref, wglu_ref, bglu_ref, pw_ref, scale_ref, wa_ref, wb_ref)
        wout = wout_ref[...].reshape(D_MODEL, D_MODEL)
        o_ref[...] = x_ref[...] + _dot(f["merged"].astype(BF16), wout)

    (x_next,), moved = _pcall(
        body, name=f"mix_fwd_l{layer}",
        out_shape=[SDS((SEQ, D_MODEL), F32)],
        grid=(SEQ // TILE_M,),
        in_specs=[pl.BlockSpec((TILE_M, D_MODEL), lambda i: (i, 0)),
                  pl.BlockSpec((TILE_M, N_IN), lambda i: (i, 0)),
                  pl.BlockSpec((TILE_M, WIDTH), lambda i: (i, 0)),
                  pl.BlockSpec((TILE_M, WIDTH), lambda i: (i, 0))] + _mix_weight_specs(layer),
        out_specs=[pl.BlockSpec((TILE_M, D_MODEL), lambda i: (i, 0))],
        args=[x, proj, y0, pooled, wg_glu, b_glu, pool_w, pool_scale, wg_a, wg_b, wg_out],
        sem=("parallel",), carry=carry)
    return x_next, moved


def _loss_head(x, target, final_g):
    def body(x_ref, t_ref, g_ref, dx_ref, loss_ref, gg_ref):
        @pl.when(pl.program_id(0) == 0)
        def _():
            loss_ref[...] = jnp.zeros_like(loss_ref)
            gg_ref[...] = jnp.zeros_like(gg_ref)

        g = g_ref[...]
        rs, xn = _rms(x_ref[...])
        err = xn * g - t_ref[...]
        loss_ref[...] += 0.5 * jnp.sum(jnp.mean(err * err, axis=-1, keepdims=True), axis=0, keepdims=True)
        dy = err * (1.0 / D_MODEL)
        gg_ref[...] += jnp.sum(dy * xn, axis=0, keepdims=True)
        dxn = dy * g
        dx_ref[...] = rs * (dxn - xn * jnp.mean(dxn * xn, axis=-1, keepdims=True))

    return pl.pallas_call(
        body, name="loss_head",
        out_shape=(SDS((SEQ, D_MODEL), F32), SDS((1, 1), F32), SDS((1, D_MODEL), F32)),
        grid=(SEQ // TILE_M,),
        in_specs=[pl.BlockSpec((TILE_M, D_MODEL), lambda i: (i, 0)),
                  pl.BlockSpec((TILE_M, D_MODEL), lambda i: (i, 0)),
                  _const((1, D_MODEL))],
        out_specs=(pl.BlockSpec((TILE_M, D_MODEL), lambda i: (i, 0)), _const((1, 1)), _const((1, D_MODEL))),
        compiler_params=_cp(("arbitrary",)),
    )(x, target, final_g)


def _big_shapes():
    return dict(w_out=(DEPTH, N_DEV, D_MODEL // N_DEV, D_MODEL), w_branch_a=(DEPTH, N_DEV, WIDTH, D_MODEL // N_DEV),
                w_branch_b=(DEPTH, N_DEV, WIDTH, D_MODEL // N_DEV), ssm_w_glu=(DEPTH, N_DEV, WIDTH // N_DEV, WIDTH),
                w_in=(DEPTH, N_DEV, D_MODEL, WIDTH))


def _mix_bwd(layer, dx_next, proj, y0, pooled, wg_glu, b_glu, pool_w, pool_scale, wg_a, wg_b, wg_out, prev,
             carry=None):
    n_k = N_DEV
    n_prev = 0 if prev is None else len(prev)

    def body(*refs):
        (dx_ref, p_ref, y0_ref, pooled_ref, wglu_ref, bglu_ref, pw_ref, scale_ref, wa_ref, wb_ref,
         wout_ref) = refs[:11]
        (dproj_ref, dy0_ref, dpooled_ref, gwout_ref, gwa_ref, gwb_ref, gwglu_ref, gpw_ref,
         gscale_ref, gbglu_ref) = refs[11 + n_prev:]

        @pl.when(pl.program_id(0) == 0)
        def _():
            for r in (gwout_ref, gwa_ref, gwb_ref, gwglu_ref, gpw_ref, gscale_ref, gbglu_ref):
                r[...] = jnp.zeros_like(r)

        f = _mix_forward(layer, p_ref, y0_ref, pooled_ref, wglu_ref, bglu_ref, pw_ref, scale_ref, wa_ref, wb_ref)
        wglu = wglu_ref[...].reshape(WIDTH, WIDTH)
        wout = wout_ref[...].reshape(D_MODEL, D_MODEL)
        blk = D_MODEL // n_k
        dxb = dx_ref[...].astype(BF16)
        dmerged = _dot_nt(dxb, wout)
        gwout = _dot_tn(f["merged"].astype(BF16), dxb)
        for k in range(n_k):
            gwout_ref[_slot(k)] += gwout[k * blk:(k + 1) * blk, :]
        dma = dmerged * f["sga"]
        dmb = dmerged * f["sgb"]
        dga = dmerged * f["ma"] * f["sga"] * (1.0 - f["sga"])
        dgb = dmerged * f["mb"] * f["sgb"] * (1.0 - f["sgb"])
        dmab = dma.astype(BF16)
        dmbb = dmb.astype(BF16)
        dya = jnp.zeros((TILE_M, WIDTH), F32)
        dyb = jnp.zeros((TILE_M, WIDTH), F32)
        for k in range(n_k):
            da_k = dmab[:, k * blk:(k + 1) * blk]
            db_k = dmbb[:, k * blk:(k + 1) * blk]
            dya = dya + _dot_nt(da_k, wa_ref[k])
            dyb = dyb + _dot_nt(db_k, wb_ref[k])
            gwa_ref[_slot(k)] += _dot_tn(f["yab"], da_k)
            gwb_ref[_slot(k)] += _dot_tn(f["ybb"], db_k)
        zb, szb = f["zb"], f["szb"]
        dzb = dyb * f["ms"] * (szb * (1.0 + zb * (1.0 - szb)))
        dms = dyb * f["silu_zb"]
        gscale_ref[...] += jnp.sum(dms * f["mixed"], axis=0, keepdims=True)
        dmixed = (dms * f["scale"]).astype(BF16)
        pooled = f["pooled"]
        for g in range(len(POOL_WINDOWS)):
            cols = slice(g * POOL_GROUP, (g + 1) * POOL_GROUP)
            dpooled_ref[:, cols] = _dot_nt(dmixed[:, cols], pw_ref[g].astype(BF16))
            gpw_ref[g] += _dot_tn(pooled[:, cols].astype(BF16), dmixed[:, cols])
        za, sza = f["za"], f["sza"]
        dza = dya * f["y2"] * (sza * (1.0 + za * (1.0 - sza)))
        dy2 = dya * f["silu_za"]
        sq = f["sq"]
        dq = dy2 * f["y1"] * sq * (1.0 - sq)
        dqb = dq.astype(BF16)
        dy1 = dy2 * sq + _dot_nt(dqb, wglu)
        gwglu = _dot_tn(f["y1b"], dqb)
        rblk = WIDTH // n_k
        for k in range(n_k):
            gwglu_ref[_slot(k)] += gwglu[k * rblk:(k + 1) * rblk, :]
        gbglu_ref[...] += jnp.sum(dq, axis=0, keepdims=True)
        y0, t = f["y0"], f["t"]
        dgelu = 0.5 * (1.0 + t) + 0.5 * y0 * (1.0 - t * t) * (GELU_C * (1.0 + 3.0 * GELU_A * y0 * y0))
        dy0_ref[...] = dy1 * dgelu
        zeros = jnp.zeros((TILE_M, WIDTH), BF16)
        dproj_ref[:, 0:WIDTH] = zeros
        dproj_ref[:, WIDTH:2 * WIDTH] = dza.astype(BF16)
        dproj_ref[:, 2 * WIDTH:3 * WIDTH] = zeros
        dproj_ref[:, 3 * WIDTH:4 * WIDTH] = dzb.astype(BF16)
        dproj_ref[:, 4 * WIDTH:4 * WIDTH + D_MODEL] = dga.astype(BF16)
        dproj_ref[:, 4 * WIDTH + D_MODEL:] = dgb.astype(BF16)

    tile = lambda w: pl.BlockSpec((TILE_M, w), lambda i: (i, 0))
    shapes = _big_shapes()
    big = ["w_out", "w_branch_a", "w_branch_b", "ssm_w_glu"]
    slab = lambda n: pl.BlockSpec((None,) + shapes[n][1:], lambda i: (layer, 0, 0, 0))
    args = [dx_next, proj, y0, pooled, wg_glu, b_glu, pool_w, pool_scale, wg_a, wg_b, wg_out]
    return _pcall(
        body, name=f"mix_bwd_l{layer}",
        out_shape=(SDS((SEQ, N_IN), BF16), SDS((SEQ, WIDTH), F32), SDS((SEQ, WIDTH), F32))
        + tuple(SDS(shapes[n], F32) for n in big)
        + (SDS((4, POOL_GROUP, POOL_GROUP), F32), SDS((1, WIDTH), F32), SDS((1, WIDTH), F32)),
        grid=(SEQ // TILE_M,),
        in_specs=[tile(D_MODEL), tile(N_IN), tile(WIDTH), tile(WIDTH)] + _mix_weight_specs(layer) + [ANY] * n_prev,
        out_specs=(tile(N_IN), tile(WIDTH), tile(WIDTH)) + tuple(slab(n) for n in big)
        + (_const((4, POOL_GROUP, POOL_GROUP)), _const((1, WIDTH)), _const((1, WIDTH))),
        args=args + list(prev or ()),
        aliases={len(args) + i: 3 + i for i in range(n_prev)},
        sem=("arbitrary",), limit=VMEM_LIMIT_BIG, carry=carry)


def _pool_bwd(layer, dpooled, dproj):
    def body(dp_ref, _, o_ref):
        for gi, win in enumerate(POOL_WINDOWS):
            cols = slice(gi * POOL_GROUP, (gi + 1) * POOL_GROUP)
            dp = dp_ref[:, cols]
            t, count = _pool_counts(win)
            e = dp / count
            acc = e
            k = 1
            while k < win:
                acc = acc + jnp.where(t < SEQ - k, pltpu.roll(acc, SEQ - k, 0), 0.0)
                k *= 2
            o_ref[:, cols] = (acc - dp).astype(BF16)

    return pl.pallas_call(
        body, name=f"pool_bwd_l{layer}",
        out_shape=SDS((SEQ, N_IN), BF16),
        grid=(1,),
        in_specs=[pl.BlockSpec((SEQ, WIDTH), lambda i: (0, 0)), ANY],
        out_specs=pl.BlockSpec((SEQ, WIDTH), lambda i: (0, 2)),
        input_output_aliases={1: 0},
        compiler_params=_cp(("arbitrary",)),
    )(dpooled, dproj)


def _s5_bwd(layer, dy0, proj, s_re, s_im, bbt_re, bbt_im, c_re, c_im, abar_re, abar_im, d_skip, dproj,
            carry=None):
    def body(dy_ref, u_ref, sre_ref, sim_ref, btre_ref, btim_ref, cre_ref, cim_ref, are_ref, aim_ref,
             d_ref, _, du_ref, gbre_ref, gbim_ref, gcre_ref, gcim_ref, gare_ref, gaim_ref, gd_ref,
             lre_ref, lim_ref, bdre, bdim, ctre, ctim, arow, airow):
        _s5_fill(btre_ref, btim_ref, cre_ref, cim_ref, are_ref, aim_ref, bdre, bdim, ctre, ctim, arow, airow)
        n_rb = SEQ // ROW_BLK
        gcre = jnp.zeros((CH_W, CH_S), F32)
        gcim = jnp.zeros((CH_W, CH_S), F32)
        for rb in range(n_rb):
            rows = pl.ds(rb * ROW_BLK, ROW_BLK)
            dyb = dy_ref[rows, :].astype(BF16)
            lre_ref[rows, :] = _dot(dyb, ctre[...])
            lim_ref[rows, :] = -_dot(dyb, ctim[...])
            gcre = gcre + _dot_tn(dyb, sre_ref[rows, :].astype(BF16))
            gcim = gcim - _dot_tn(dyb, sim_ref[rows, :].astype(BF16))
        ar = arow[...]
        ai = airow[...]
        row_id = _scan_tile_rows()

        def step(n, carry):
            lr, li = carry
            base = pl.multiple_of((SEQ // 8 - 1 - n) * 8, 8)
            tr = lre_ref[pl.ds(base, 8), :]
            ti = lim_ref[pl.ds(base, 8), :]
            outr, outi = tr, ti
            for r in range(7, -1, -1):
                nlr = ar * lr + ai * li + tr[r:r + 1, :]
                nli = ar * li - ai * lr + ti[r:r + 1, :]
                outr = jnp.where(row_id == r, nlr, outr)
                outi = jnp.where(row_id == r, nli, outi)
                lr, li = nlr, nli
            lre_ref[pl.ds(base, 8), :] = outr
            lim_ref[pl.ds(base, 8), :] = outi
            return lr, li

        zero = jnp.zeros((1, CH_S), F32)
        lax.fori_loop(0, SEQ // 8, step, (zero, zero))

        gare = jnp.zeros((1, CH_S), F32)
        gaim = jnp.zeros((1, CH_S), F32)
        gbre = jnp.zeros((CH_W, CH_S), F32)
        gbim = jnp.zeros((CH_W, CH_S), F32)
        gd = jnp.zeros((1, CH_W), F32)
        d = d_ref[layer:layer + 1, :]
        first = lax.broadcasted_iota(jnp.int32, (ROW_BLK, CH_S), 0) == 0
        for rb in range(n_rb):
            rows = pl.ds(rb * ROW_BLK, ROW_BLK)
            lr = lre_ref[rows, :]
            li = lim_ref[rows, :]
            if rb == 0:
                prev_r = jnp.zeros((1, CH_S), F32)
                prev_i = jnp.zeros((1, CH_S), F32)
            else:
                prev_r = sre_ref[pl.ds(rb * ROW_BLK - 1, 1), :]
                prev_i = sim_ref[pl.ds(rb * ROW_BLK - 1, 1), :]
            spr = jnp.where(first, prev_r, pltpu.roll(sre_ref[rows, :], 1, 0))
            spi = jnp.where(first, prev_i, pltpu.roll(sim_ref[rows, :], 1, 0))
            gare = gare + jnp.sum(lr * spr + li * spi, axis=0, keepdims=True)
            gaim = gaim + jnp.sum(li * spr - lr * spi, axis=0, keepdims=True)
            lrb = lr.astype(BF16)
            lib = li.astype(BF16)
            u = u_ref[rows, :]
            ub = u.astype(BF16)
            dy = dy_ref[rows, :]
            du = dy * d + _dot_nt(lrb, bdre[...]) + _dot_nt(lib, bdim[...])
            du_ref[rows, :] = du.astype(BF16)
            gbre = gbre + _dot_tn(ub, lrb)
            gbim = gbim + _dot_tn(ub, lib)
            gd = gd + jnp.sum(dy * u, axis=0, keepdims=True)
        gd_ref[...] = gd
        for g in range(CH_G):
            rows = slice(g * GROUP_W, (g + 1) * GROUP_W)
            cols = slice(g * STATE, (g + 1) * STATE)
            gcre_ref[g] = gcre[rows, cols]
            gcim_ref[g] = gcim[rows, cols]
            gbre_ref[g] = gbre[rows, cols]
            gbim_ref[g] = gbim[rows, cols]
            gare_ref[g:g + 1, :] = gare[:, cols]
            gaim_ref[g:g + 1, :] = gaim[:, cols]

    chunk_w = lambda: pl.BlockSpec((SEQ, CH_W), lambda j: (0, j))
    chunk_s = lambda: pl.BlockSpec((SEQ, CH_S), lambda j: (0, j))
    gbt = lambda: pl.BlockSpec((CH_G, GROUP_W, STATE), lambda j: (j, 0, 0))
    gcc = lambda: pl.BlockSpec((CH_G, GROUP_W, STATE), lambda j: (j, 0, 0))
    gab = lambda: pl.BlockSpec((CH_G, STATE), lambda j: (j, 0))
    return _pcall(
        body, name=f"s5_bwd_l{layer}",
        out_shape=(SDS((SEQ, N_IN), BF16),
                   SDS((N_GROUP, GROUP_W, STATE), F32), SDS((N_GROUP, GROUP_W, STATE), F32),
                   SDS((N_GROUP, GROUP_W, STATE), F32), SDS((N_GROUP, GROUP_W, STATE), F32),
                   SDS((N_GROUP, STATE), F32), SDS((N_GROUP, STATE), F32), SDS((1, WIDTH), F32)),
        grid=(N_CHUNK,),
        in_specs=[chunk_w(), chunk_w(), chunk_s(), chunk_s()] + _s5_param_specs(layer) + [ANY],
        out_specs=(chunk_w(), gbt(), gbt(), gcc(), gcc(), gab(), gab(),
                   pl.BlockSpec((1, CH_W), lambda j: (0, j))),
        scratch_shapes=[pltpu.VMEM((SEQ, CH_S), F32), pltpu.VMEM((SEQ, CH_S), F32)] + _s5_param_scratch(),
        args=[dy0, proj, s_re, s_im, bbt_re, bbt_im, c_re, c_im, abar_re, abar_im, d_skip, dproj],
        aliases={11: 0}, sem=("arbitrary",), limit=VMEM_LIMIT_BIG, carry=carry)


def _proj_wgrad(layer, x, norm_g, dproj, prev):
    tm = 512
    n_prev = 0 if prev is None else 1

    def body(*refs):
        x_ref, g_ref, dp_ref = refs[:3]
        gw_ref, gb_ref = refs[3 + n_prev:]

        @pl.when(pl.program_id(1) == 0)
        def _():
            gw_ref[...] = jnp.zeros_like(gw_ref)
            gb_ref[...] = jnp.zeros_like(gb_ref)

        _, xn = _rms(x_ref[...])
        h = (xn * g_ref[layer:layer + 1, :]).astype(BF16)
        dp = dp_ref[...]
        gw_ref[...] += _dot_tn(h, dp)
        gb_ref[...] += jnp.sum(dp.astype(F32), axis=0, keepdims=True)

    return pl.pallas_call(
        body, name=f"proj_wgrad_l{layer}",
        out_shape=(SDS(_big_shapes()["w_in"], F32), SDS((1, N_IN), F32)),
        grid=(N_DEV, SEQ // tm),
        in_specs=[pl.BlockSpec((tm, D_MODEL), lambda n, t: (t, 0)),
                  _const((DEPTH, D_MODEL)),
                  pl.BlockSpec((tm, WIDTH), lambda n, t: (t, n))] + [ANY] * n_prev,
        out_specs=(pl.BlockSpec((None, None, D_MODEL, WIDTH), lambda n, t: (layer, _slot(n), 0, 0)),
                   pl.BlockSpec((1, WIDTH), lambda n, t: (0, n))),
        input_output_aliases={3: 0} if n_prev else {},
        compiler_params=_cp(("parallel", "arbitrary")),
    )(x, norm_g, dproj, *([prev] if n_prev else []))


def _proj_dgrad(layer, dx_next, x, norm_g, dproj, wg_in, carry=None):
    def body(dxn_ref, x_ref, g_ref, dp_ref, w_ref, dx_ref, gg_ref):
        @pl.when(pl.program_id(0) == 0)
        def _():
            gg_ref[...] = jnp.zeros_like(gg_ref)

        dh = jnp.zeros((TILE_M, D_MODEL), F32)
        for k in range(N_DEV):
            dh = dh + _dot_nt(dp_ref[:, k * WIDTH:(k + 1) * WIDTH], w_ref[k])
        rs, xn = _rms(x_ref[...])
        gg_ref[...] += jnp.sum(dh * xn, axis=0, keepdims=True)
        dxn = dh * g_ref[layer:layer + 1, :]
        dx_ref[...] = dxn_ref[...] + rs * (dxn - xn * jnp.mean(dxn * xn, axis=-1, keepdims=True))

    return _pcall(
        body, name=f"proj_dgrad_l{layer}",
        out_shape=(SDS((SEQ, D_MODEL), F32), SDS((1, D_MODEL), F32)),
        grid=(SEQ // TILE_M,),
        in_specs=[pl.BlockSpec((TILE_M, D_MODEL), lambda i: (i, 0)),
                  pl.BlockSpec((TILE_M, D_MODEL), lambda i: (i, 0)),
                  _const((DEPTH, D_MODEL)),
                  pl.BlockSpec((TILE_M, N_IN), lambda i: (i, 0)),
                  _const((N_DEV, D_MODEL, WIDTH))],
        out_specs=(pl.BlockSpec((TILE_M, D_MODEL), lambda i: (i, 0)), _const((1, D_MODEL))),
        args=[dx_next, x, norm_g, dproj, wg_in], sem=("arbitrary",), carry=carry)


def _my_place():
    return lax.axis_index("x"), lax.axis_index("y"), lax.axis_index("c")


def _gather_plan(shards, layer):
    n = len(shards)

    def parts(ins, outs, sems):
        send_sems, recv_sems, local_sems = sems
        x, y, c = _my_place()
        chips = [(1 - x, y), (x, 1 - y), (1 - x, 1 - y)]

        def rows(t, place):
            px, py, pc = place
            return outs[t].at[pl.ds(4 * px + 2 * py + pc, 1)]

        def copy(t, k, block, to, from_src=False):
            return pltpu.make_async_remote_copy(
                src_ref=ins[t].at[pl.ds(layer, 1)] if from_src else rows(t, block), dst_ref=rows(t, block),
                send_sem=send_sems.at[7 * t + k], recv_sem=recv_sems.at[7 * t + k], device_id=to,
                device_id_type=MESH)

        def mine(t):
            return pltpu.make_async_copy(ins[t].at[pl.ds(layer, 1)], rows(t, (x, y, c)), local_sems.at[t])

        return (x, y, c), chips, copy, mine

    def start(ins, outs, sems):
        me, chips, copy, mine = parts(ins, outs, sems)
        x, y, c = me
        for t in range(n):
            mine(t).start()
            copy(t, 0, me, (x, y, 1 - c), from_src=True).start()
            for j, chip in enumerate(chips):
                copy(t, 1 + j, me, (*chip, c), from_src=True).start()

    def finish(ins, outs, sems):
        me, chips, copy, mine = parts(ins, outs, sems)
        x, y, c = me
        sibling = (x, y, 1 - c)
        for t in range(n):
            for j, chip in enumerate(chips):
                copy(t, 1 + j, (*chip, c), me).wait_recv()
                copy(t, 4 + j, (*chip, c), sibling).start()
        for t in range(n):
            copy(t, 0, sibling, me).wait_recv()
            for j, chip in enumerate(chips):
                copy(t, 4 + j, (*chip, 1 - c), me).wait_recv()
            for k in range(7):
                copy(t, k, me, sibling, from_src=k < 4).wait_send()
            mine(t).wait()

    out_shape = [SDS((N_DEV,) + a.shape[1:], a.dtype) for a in shards]
    sems = [pltpu.SemaphoreType.DMA((7 * n,)), pltpu.SemaphoreType.DMA((7 * n,)), pltpu.SemaphoreType.DMA((n,))]
    return _Carried(shards, out_shape, sems, start, finish)


class _Carried:
    def __init__(self, ins, out_shape, sems, start, finish):
        self.ins, self.out_shape, self.sems = list(ins), list(out_shape), list(sems)
        self.start, self.finish = start, finish


def _pcall(body, *, name, grid, in_specs, out_specs, out_shape, args, scratch_shapes=(), aliases=None,
           sem=None, limit=VMEM_LIMIT, carry=None):
    out_shape, out_specs, scratch_shapes = list(out_shape), list(out_specs), list(scratch_shapes)
    n_in, n_out, n_scr = len(args), len(out_shape), len(scratch_shapes)
    if carry is None:
        kern, c_ins, c_out, c_sems = body, [], [], []
    else:
        c_ins, c_out, c_sems = carry.ins, carry.out_shape, carry.sems
        ci, co = len(c_ins), len(c_out)
        steps = tuple(grid)

        def kern(*refs):
            o0 = n_in + ci
            s0 = o0 + n_out + co
            mine = refs[:n_in] + refs[o0:o0 + n_out] + refs[s0:s0 + n_scr]
            theirs = (refs[n_in:o0], refs[o0 + n_out:s0], refs[s0 + n_scr:])
            first = pl.program_id(0) == 0
            last = pl.program_id(0) == steps[0] - 1
            for a in range(1, len(steps)):
                first = jnp.logical_and(first, pl.program_id(a) == 0)
                last = jnp.logical_and(last, pl.program_id(a) == steps[a] - 1)

            @pl.when(first)
            def _():
                carry.start(*theirs)

            body(*mine)

            @pl.when(last)
            def _():
                carry.finish(*theirs)

        sem = ("arbitrary",) * len(steps)
    res = pl.pallas_call(
        kern, name=name, grid=tuple(grid),
        in_specs=list(in_specs) + [ANY] * len(c_ins),
        out_specs=tuple(out_specs + [ANY] * len(c_out)),
        out_shape=tuple(out_shape + c_out),
        scratch_shapes=scratch_shapes + c_sems,
        input_output_aliases=aliases or {},
        compiler_params=_cp(sem, limit),
    )(*args, *c_ins)
    return res[:n_out], res[n_out:]


def _run_carried(name, carry):
    ci, co = len(carry.ins), len(carry.out_shape)

    def body(*refs):
        parts = (refs[:ci], refs[ci:ci + co], refs[ci + co:])
        carry.start(*parts)
        carry.finish(*parts)

    return pl.pallas_call(
        body, name=name, out_shape=tuple(carry.out_shape),
        in_specs=[ANY] * ci, out_specs=tuple([ANY] * co), scratch_shapes=carry.sems,
    )(*carry.ins)


def _sibling_plan(big, small):
    n = len(big)
    n_copies = 4 * n + len(small)

    def copies(ins, outs, sems):
        send_sems, recv_sems = sems
        x, y, c = _my_place()
        pairs = []
        for t, (_, layer) in enumerate(big):
            for s in range(4):
                pairs.append((ins[t].at[layer, pl.ds(4 * (1 - c) + s, 1)], outs[t].at[pl.ds(s, 1)]))
        pairs += list(zip(ins[n:], outs[n:]))
        return [pltpu.make_async_remote_copy(
            src_ref=src, dst_ref=dst, send_sem=send_sems.at[k], recv_sem=recv_sems.at[k],
            device_id=(x, y, 1 - c), device_id_type=MESH) for k, (src, dst) in enumerate(pairs)]

    def start(ins, outs, sems):
        for cp in copies(ins, outs, sems):
            cp.start()

    def finish(ins, outs, sems):
        for cp in copies(ins, outs, sems):
            cp.wait()

    out_shape = [SDS((4,) + a.shape[2:], a.dtype) for a, _ in big] + [SDS(a.shape, a.dtype) for a in small]
    sems = [pltpu.SemaphoreType.DMA((n_copies,)), pltpu.SemaphoreType.DMA((n_copies,))]
    return _Carried([a for a, _ in big] + list(small), out_shape, sems, start, finish)


def _chips_plan(big, small):
    n, n_small = len(big), len(small)
    max_rows = 512
    parts = [max(1, a.shape[1] // max_rows) for a in big]
    n_copies = 3 * (sum(parts) + n_small)

    def copies(ins, outs, sems, landing):
        send_sems, recv_sems, local_sems = sems
        x, y, c = _my_place()
        my_chip = 2 * x + y
        chips = [(1 - x, y), (x, 1 - y), (1 - x, 1 - y)]
        remote, local = [], []
        for chip in chips:
            to = 2 * chip[0] + chip[1]
            slot = to if landing else my_chip
            pairs = []
            for t in range(n):
                rows_per = big[t].shape[1] // parts[t]
                for p in range(parts[t]):
                    rows = pl.ds(p * rows_per, rows_per)
                    pairs.append((ins[t].at[to, rows], outs[t].at[slot, rows]))
            pairs += [(ins[t], outs[t].at[slot]) for t in range(n, n + n_small)]
            for src, dst in pairs:
                k = len(remote)
                remote.append(pltpu.make_async_remote_copy(
                    src_ref=src, dst_ref=dst, send_sem=send_sems.at[k], recv_sem=recv_sems.at[k],
                    device_id=(*chip, c), device_id_type=MESH))
        for t in range(n):
            local.append(pltpu.make_async_copy(ins[t].at[my_chip], outs[t].at[my_chip], local_sems.at[t]))
        for t in range(n, n + n_small):
            local.append(pltpu.make_async_copy(ins[t], outs[t].at[my_chip], local_sems.at[t]))
        return remote + local

    def start(ins, outs, sems):
        for cp in copies(ins, outs, sems, landing=False):
            cp.start()

    def finish(ins, outs, sems):
        for cp in copies(ins, outs, sems, landing=True):
            cp.wait()

    out_shape = [SDS(a.shape, a.dtype) for a in big] + [SDS((N_CHIP,) + a.shape, a.dtype) for a in small]
    sems = [pltpu.SemaphoreType.DMA((n_copies,)), pltpu.SemaphoreType.DMA((n_copies,)),
            pltpu.SemaphoreType.DMA((n + n_small,))]
    return _Carried(list(big) + list(small), out_shape, sems, start, finish)


def _row_block(rows):
    return rows if rows <= 256 else 256


def _add_own(tag, core, g, layer, got):
    _, r, c = got.shape
    rb = _row_block(r)

    def body(core_ref, a_ref, b_ref, o_ref):
        o_ref[...] = (a_ref[...] + b_ref[...]).astype(o_ref.dtype)

    return pl.pallas_call(
        body, name=f"add_{tag}", out_shape=SDS(got.shape, BF16),
        grid_spec=pltpu.PrefetchScalarGridSpec(
            num_scalar_prefetch=1, grid=(4, r // rb),
            in_specs=[pl.BlockSpec((None, None, rb, c), lambda s, j, core: (layer, 4 * core[0] + s, j, 0)),
                      pl.BlockSpec((None, rb, c), lambda s, j, core: (s, j, 0))],
            out_specs=pl.BlockSpec((None, rb, c), lambda s, j, core: (s, j, 0))),
        compiler_params=_cp(("parallel", "parallel")),
    )(core, g, got)


def _add_lists(tag, own, got, grid=None, specs=None):
    n = len(own)

    def body(*refs):
        for a, b, o in zip(refs[:n], refs[n:2 * n], refs[2 * n:]):
            o[...] = a[...] + b[...]

    kw = {}
    if grid is not None:
        kw = dict(grid=grid, in_specs=list(specs) * 2, out_specs=tuple(specs),
                  compiler_params=_cp(("parallel",) * len(grid)))
    return pl.pallas_call(
        body, name=f"add_{tag}", out_shape=tuple(SDS(a.shape, a.dtype) for a in own), **kw)(*own, *got)


def _adamw_math(w, g, m, v):
    m = ADAM_B1 * m + (1.0 - ADAM_B1) * g
    v = ADAM_B2 * v + (1.0 - ADAM_B2) * (g * g)
    m_hat = m / (1.0 - ADAM_B1 ** ADAM_STEP)
    v_hat = v / (1.0 - ADAM_B2 ** ADAM_STEP)
    delta = -ADAM_LR * (m_hat / (jnp.sqrt(v_hat) + ADAM_EPS) + ADAM_WD * w)
    return delta, m, v


def _sum_slots_adamw(tag, slots, w, m, v):
    _, r, c = slots[0].shape
    rb = _row_block(r)

    def body(s0_ref, s1_ref, w_ref, m_ref, v_ref, g_ref, d_ref, nm_ref, nv_ref):
        first = pl.program_id(1) == 0
        g = _sum4([jnp.where(first, s0_ref[k], s1_ref[k]).astype(F32) for k in range(N_CHIP)])
        delta, nm, nv = _adamw_math(w_ref[...], g, m_ref[...], v_ref[...])
        g_ref[...] = g
        d_ref[...] = delta
        nm_ref[...] = nm
        nv_ref[...] = nv

    spec = pl.BlockSpec((None, rb, c), lambda j, l: (l, j, 0))
    sspec = pl.BlockSpec((N_CHIP, rb, c), lambda j, l: (0, j, 0))
    s = SDS((DEPTH, r, c), F32)
    return pl.pallas_call(
        body, name=f"adamw_{tag}", out_shape=(s, s, s, s),
        grid=(r // rb, DEPTH), in_specs=[sspec, sspec, spec, spec, spec], out_specs=(spec, spec, spec, spec),
        compiler_params=_cp(("parallel", "arbitrary")),
    )(*slots, w, m, v)


def _adamw_small(tag, entries, grid=None):
    flat_in, in_specs, out_shape, out_specs, layout = [], [], [], [], []
    for slots, w, m, v, slot_spec, w_spec in entries:
        per_layer = isinstance(slots, (list, tuple))
        n_slot = len(slots) if per_layer else 1
        flat_in += (list(slots) if per_layer else [slots]) + [w, m, v]
        in_specs += [slot_spec] * n_slot + [w_spec] * 3
        out_shape += [SDS(w.shape, F32)] * 4
        out_specs += [w_spec] * 4
        layout.append((per_layer, n_slot))
    n_in = len(flat_in)

    def body(*refs):
        i, o = 0, n_in
        for per_layer, n_slot in layout:
            s_refs = refs[i:i + n_slot]
            w_ref, m_ref, v_ref = refs[i + n_slot:i + n_slot + 3]
            outs = refs[o:o + 4]
            if per_layer:
                for l, s_ref in enumerate(s_refs):
                    at = (slice(l, l + 1),) if len(w_ref.shape) == 2 else (l,)
                    g = _sum4([s_ref[k] for k in range(N_CHIP)])
                    res = (g,) + _adamw_math(w_ref[at], g, m_ref[at], v_ref[at])
                    for o_ref, val in zip(outs, res):
                        o_ref[at] = val
            else:
                g = _sum4([s_refs[0][k] for k in range(N_CHIP)])
                res = (g,) + _adamw_math(w_ref[...], g, m_ref[...], v_ref[...])
                for o_ref, val in zip(outs, res):
                    o_ref[...] = val
            i += n_slot + 3
            o += 4

    kw = {}
    if grid is not None:
        kw = dict(grid=grid, in_specs=in_specs, out_specs=tuple(out_specs),
                  compiler_params=_cp(("parallel",) * len(grid)))
    res = pl.pallas_call(body, name=f"adamw_{tag}", out_shape=tuple(out_shape), **kw)(*flat_in)
    return [tuple(res[4 * e:4 * e + 4]) for e in range(len(entries))]


def kernel(x, norm_g, w_in, b_in, ssm_log_dt, ssm_lam_re, ssm_lam_im, ssm_b_re, ssm_b_im, ssm_c_re, ssm_c_im, ssm_d, ssm_w_glu, ssm_b_glu, pool_w, pool_scale, w_branch_a, w_branch_b, w_out, final_norm_g, loss_target, m_norm_g, m_w_in, m_b_in, m_ssm_log_dt, m_ssm_lam_re, m_ssm_lam_im, m_ssm_b_re, m_ssm_b_im, m_ssm_c_re, m_ssm_c_im, m_ssm_d, m_ssm_w_glu, m_ssm_b_glu, m_pool_w, m_pool_scale, m_w_branch_a, m_w_branch_b, m_w_out, m_final_norm_g, v_norm_g, v_w_in, v_b_in, v_ssm_log_dt, v_ssm_lam_re, v_ssm_lam_im, v_ssm_b_re, v_ssm_b_im, v_ssm_c_re, v_ssm_c_im, v_ssm_d, v_ssm_w_glu, v_ssm_b_glu, v_pool_w, v_pool_scale, v_w_branch_a, v_w_branch_b, v_w_out, v_final_norm_g):
    weights = dict(norm_g=norm_g, w_in=w_in, b_in=b_in, ssm_log_dt=ssm_log_dt, ssm_lam_re=ssm_lam_re,
                   ssm_lam_im=ssm_lam_im, ssm_b_re=ssm_b_re, ssm_b_im=ssm_b_im, ssm_c_re=ssm_c_re,
                   ssm_c_im=ssm_c_im, ssm_d=ssm_d, ssm_w_glu=ssm_w_glu, ssm_b_glu=ssm_b_glu, pool_w=pool_w,
                   pool_scale=pool_scale, w_branch_a=w_branch_a, w_branch_b=w_branch_b, w_out=w_out,
                   final_norm_g=final_norm_g.reshape(1, D_MODEL))
    mom_m = dict(norm_g=m_norm_g, w_in=m_w_in, b_in=m_b_in, ssm_log_dt=m_ssm_log_dt, ssm_lam_re=m_ssm_lam_re,
                 ssm_lam_im=m_ssm_lam_im, ssm_b_re=m_ssm_b_re, ssm_b_im=m_ssm_b_im, ssm_c_re=m_ssm_c_re,
                 ssm_c_im=m_ssm_c_im, ssm_d=m_ssm_d, ssm_w_glu=m_ssm_w_glu, ssm_b_glu=m_ssm_b_glu,
                 pool_w=m_pool_w, pool_scale=m_pool_scale, w_branch_a=m_w_branch_a, w_branch_b=m_w_branch_b,
                 w_out=m_w_out, final_norm_g=m_final_norm_g.reshape(1, D_MODEL))
    mom_v = dict(norm_g=v_norm_g, w_in=v_w_in, b_in=v_b_in, ssm_log_dt=v_ssm_log_dt, ssm_lam_re=v_ssm_lam_re,
                 ssm_lam_im=v_ssm_lam_im, ssm_b_re=v_ssm_b_re, ssm_b_im=v_ssm_b_im, ssm_c_re=v_ssm_c_re,
                 ssm_c_im=v_ssm_c_im, ssm_d=v_ssm_d, ssm_w_glu=v_ssm_w_glu, ssm_b_glu=v_ssm_b_glu,
                 pool_w=v_pool_w, pool_scale=v_pool_scale, w_branch_a=v_w_branch_a, w_branch_b=v_w_branch_b,
                 w_out=v_w_out, final_norm_g=v_final_norm_g.reshape(1, D_MODEL))
    order = ["norm_g", "w_in", "b_in", "ssm_log_dt", "ssm_lam_re", "ssm_lam_im", "ssm_b_re", "ssm_b_im",
             "ssm_c_re", "ssm_c_im", "ssm_d", "ssm_w_glu", "ssm_b_glu", "pool_w", "pool_scale", "w_branch_a",
             "w_branch_b", "w_out", "final_norm_g"]
    big_names = ["w_in", "ssm_w_glu", "w_branch_a", "w_branch_b", "w_out"]

    log_dt3 = ssm_log_dt.reshape(DEPTH, N_GROUP, 1)
    b_t = lambda a: a.transpose(0, 1, 3, 2)
    for d in (weights, mom_m, mom_v):
        d["ssm_b_re"], d["ssm_b_im"] = b_t(d["ssm_b_re"]), b_t(d["ssm_b_im"])
    bt_re, bt_im = weights["ssm_b_re"], weights["ssm_b_im"]
    abar_re, abar_im, bbt_re, bbt_im = _s5_params(log_dt3, ssm_lam_re, ssm_lam_im, bt_re, bt_im)
    s5_args = (bbt_re, bbt_im, ssm_c_re, ssm_c_im, abar_re, abar_im, ssm_d)

    w16 = {n: weights[n].astype(BF16) for n in big_names}
    rest = [w16[n] for n in big_names[1:]]
    wg_in = [None, None]
    wg_rest = [None, None]
    (wg_in[0],) = _run_carried("gather_w_in_l0", _gather_plan([w16["w_in"]], 0))
    xs = [x.reshape(SEQ, D_MODEL)]
    saved = []
    for l in range(DEPTH):
        proj, moved = _norm_proj(l, xs[l], norm_g, wg_in[l], b_in, carry=_gather_plan(rest, 0) if l == 0 else None)
        if l == 0:
            wg_rest[0] = moved
        (s_re, s_im, y0), moved = _s5_fwd(
            l, proj, *s5_args, carry=_gather_plan([w16["w_in"]], 1) if l == 0 else None)
        if l == 0:
            (wg_in[1],) = moved
        pooled = _pool_fwd(l, proj)
        wg_glu, wg_a, wg_b, wg_out = wg_rest[l]
        x_next, moved = _mix_fwd(l, xs[l], proj, y0, pooled, wg_glu, ssm_b_glu, pool_w, pool_scale, wg_a, wg_b,
                                 wg_out, carry=_gather_plan(rest, 1) if l == 0 else None)
        if l == 0:
            wg_rest[1] = moved
        xs.append(x_next)
        saved.append((proj, s_re, s_im, y0, pooled))

    dx, loss_part, g_final = _loss_head(xs[DEPTH], loss_target.reshape(SEQ, D_MODEL), weights["final_norm_g"])
    loss = lax.psum(loss_part[0, 0], ("x", "y", "c"))

    core = lax.axis_index("c").astype(jnp.int32).reshape(1)
    vec_names = ["norm_g", "b_in", "ssm_d", "ssm_b_glu", "pool_scale"]
    flat_names = ["final_norm_g", "ssm_log_dt", "ssm_lam_re", "ssm_lam_im"]
    mat_names = ["pool_w", "ssm_c_re", "ssm_c_im"]
    lane_sparse = {"ssm_c_re": ssm_c_re.shape[1:], "ssm_c_im": ssm_c_im.shape[1:],
                   "ssm_b_re": bt_re.shape, "ssm_b_im": bt_im.shape}
    gridded = set(mat_names) | {"ssm_b_re", "ssm_b_im"}

    def dense(key, a):
        return a.reshape(-1, 128) if key[0] in lane_sparse else a

    def undense(key, slots):
        return slots.reshape((N_CHIP,) + lane_sparse[key[0]]) if key[0] in lane_sparse else slots

    def add_small(tag, keys, own, got):
        out = [None] * len(keys)
        whole = [i for i, k in enumerate(keys) if k[0] not in gridded]
        tiled = [i for i, k in enumerate(keys) if k[0] in gridded]
        for i, r in zip(whole, _add_lists(f"{tag}_a", [own[i] for i in whole], [got[i] for i in whole])):
            out[i] = r
        specs = [pl.BlockSpec((1, POOL_GROUP, POOL_GROUP), lambda j: (j, 0, 0)) if keys[i][0] == "pool_w"
                 else pl.BlockSpec((own[i].shape[0] // N_CHUNK, 128), lambda j: (j, 0)) for i in tiled]
        for i, r in zip(tiled, _add_lists(f"{tag}_b", [own[i] for i in tiled], [got[i] for i in tiled],
                                          grid=(N_CHUNK,), specs=specs)):
            out[i] = r
        return out

    sm = {("final_norm_g", None): g_final}
    g_abar_re, g_abar_im, g_bbt_re, g_bbt_im = ([None] * DEPTH for _ in range(4))
    mix_big, gw_in = None, None
    keys1 = ([(n, 1) for n in vec_names[1:]] + [(n, 1) for n in mat_names] + [("final_norm_g", None)])
    chip1_big = chip1_small = slots1_big = slots1_small = None
    for l in reversed(range(DEPTH)):
        proj, s_re, s_im, y0, pooled = saved[l]
        wg_glu, wg_a, wg_b, wg_out = wg_rest[l]
        carry = None if l == 1 else _chips_plan(chip1_big[:1], [])
        res, moved = _mix_bwd(l, dx, proj, y0, pooled, wg_glu, ssm_b_glu, pool_w, pool_scale, wg_a, wg_b, wg_out,
                              mix_big, carry=carry)
        if l == 0:
            slots1_big = list(moved)
        dproj, dy0, dpooled = res[:3]
        mix_big = list(res[3:7])
        sm[("pool_w", l)], sm[("pool_scale", l)], sm[("ssm_b_glu", l)] = res[7:]
        dproj = _pool_bwd(l, dpooled, dproj)
        carry = None if l == 1 else _chips_plan(chip1_big[1:], chip1_small)
        res, moved = _s5_bwd(l, dy0, proj, s_re, s_im, *s5_args, dproj, carry=carry)
        if l == 0:
            slots1_big += list(moved[:4])
            slots1_small = moved[4:]
        (dproj, g_bbt_re[l], g_bbt_im[l], sm[("ssm_c_re", l)], sm[("ssm_c_im", l)], g_abar_re[l], g_abar_im[l],
         sm[("ssm_d", l)]) = res
        gw_in, sm[("b_in", l)] = _proj_wgrad(l, xs[l], norm_g, dproj, gw_in)
        gw_out, gw_a, gw_b, gw_glu = mix_big
        big_part = [gw_in, gw_glu, gw_a, gw_b, gw_out]
        carry = None
        if l == 1:
            own1 = [dense(k, sm[k]) for k in keys1]
            carry = _sibling_plan([(a, 1) for a in big_part], own1)
        (dx, sm[("norm_g", l)]), moved = _proj_dgrad(l, dx, xs[l], norm_g, dproj, wg_in[l], carry=carry)
        if l == 1:
            chip1_big = [_add_own(f"chip1_{n}", core, a, 1, b) for n, a, b in zip(big_names, big_part, moved[:5])]
            chip1_small = add_small("chip1_small", keys1, own1, moved[5:])
    grad_x = dx.reshape(1, SEQ, D_MODEL)

    g_ld, g_lr, g_li, g_btr, g_bti = _s5_params_bwd(
        log_dt3, ssm_lam_re, ssm_lam_im, bt_re, bt_im, g_abar_re, g_abar_im, g_bbt_re, g_bbt_im)
    sm[("ssm_log_dt", None)] = g_ld.reshape(DEPTH, N_GROUP)
    sm[("ssm_lam_re", None)] = g_lr
    sm[("ssm_lam_im", None)] = g_li
    sm[("ssm_b_re", None)] = g_btr
    sm[("ssm_b_im", None)] = g_bti

    keys0 = ([(n, 0) for n in vec_names] + [("norm_g", 1)] + [(n, 0) for n in mat_names]
             + [(n, None) for n in ("ssm_log_dt", "ssm_lam_re", "ssm_lam_im", "ssm_b_re", "ssm_b_im")])
    own0 = [dense(k, sm[k]) for k in keys0]
    moved = _run_carried("exchange_sibling_l0", _sibling_plan([(a, 0) for a in big_part], own0))
    chip0_big = [_add_own(f"chip0_{n}", core, a, 0, b) for n, a, b in zip(big_names, big_part, moved[:5])]
    chip0_small = add_small("chip0_small", keys0, own0, moved[5:])
    moved = _run_carried("exchange_chips_l0", _chips_plan(chip0_big, chip0_small))
    slots0_big = moved[:5]
    slots = {k: undense(k, s) for k, s in zip(keys0, moved[5:])}
    slots.update({k: undense(k, s) for k, s in zip(keys1, slots1_small)})

    res = {}
    for i, n in enumerate(big_names):
        res[n] = _sum_slots_adamw(n, [slots0_big[i], slots1_big[i]], weights[n], mom_m[n], mom_v[n])
    entries_a = []
    for n in vec_names:
        entries_a.append(([slots[(n, l)] for l in range(DEPTH)], weights[n], mom_m[n], mom_v[n], None, None))
    for n in flat_names:
        entries_a.append((slots[(n, None)], weights[n], mom_m[n], mom_v[n], None, None))
    out_a = _adamw_small("small_a", entries_a)
    for n, r in zip(vec_names + flat_names, out_a):
        res[n] = r
    res["final_norm_g"] = tuple(a.reshape(D_MODEL) for a in res["final_norm_g"])
    pw_s = pl.BlockSpec((N_CHIP, 1, POOL_GROUP, POOL_GROUP), lambda j: (0, j, 0, 0))
    pw_w = pl.BlockSpec((DEPTH, 1, POOL_GROUP, POOL_GROUP), lambda j: (0, j, 0, 0))
    c_s = pl.BlockSpec((N_CHIP, CH_G, GROUP_W, STATE), lambda j: (0, j, 0, 0))
    c_w = pl.BlockSpec((DEPTH, CH_G, GROUP_W, STATE), lambda j: (0, j, 0, 0))
    b_s = pl.BlockSpec((N_CHIP, DEPTH, CH_G, GROUP_W, STATE), lambda j: (0, 0, j, 0, 0))
    b_w = c_w
    entries_b = []
    for n, s_spec, w_spec in (("pool_w", pw_s, pw_w), ("ssm_c_re", c_s, c_w), ("ssm_c_im", c_s, c_w)):
        entries_b.append(([slots[(n, l)] for l in range(DEPTH)], weights[n], mom_m[n], mom_v[n], s_spec, w_spec))
    for n in ("ssm_b_re", "ssm_b_im"):
        entries_b.append((slots[(n, None)], weights[n], mom_m[n], mom_v[n], b_s, b_w))
    out_b = _adamw_small("small_b", entries_b, grid=(N_CHUNK,))
    for n, r in zip(mat_names + ["ssm_b_re", "ssm_b_im"], out_b):
        res[n] = tuple(b_t(a) for a in r) if n in ("ssm_b_re", "ssm_b_im") else r

    outs = [loss, grad_x]
    for i in range(4):
        outs += [res[n][i] for n in order]
    return tuple(outs)
```

```python
import math

import jax
import jax.numpy as jnp
from jax import lax
from jax.experimental import pallas as pl
from jax.experimental.pallas import tpu as pltpu

F32 = jnp.float32
BF16 = jnp.bfloat16

SEQ = 2048
D_MODEL = 1024
N_IN = 4096
WIDTH = 512
N_GROUP = 32
GROUP_W = 16
STATE = 64
N_STATE = N_GROUP * STATE
N_CHUNK = 4
CH_G = N_GROUP // N_CHUNK
CH_W = WIDTH // N_CHUNK
CH_S = N_STATE // N_CHUNK
N_DEV = 8
N_CHIP = 4
POOL_WINDOWS = (2, 4, 8, 16)
POOL_GROUP = 128
EPS = 1e-6
DEPTH = 2

ADAM_LR = 0.001
ADAM_B1 = 0.9
ADAM_B2 = 0.999
ADAM_EPS = 1e-08
ADAM_WD = 0.01
ADAM_STEP = 10

TILE_M = 256
ROW_BLK = 512
VMEM_LIMIT = 48 * 1024 * 1024
VMEM_LIMIT_BIG = 60 * 1024 * 1024
MESH = pl.DeviceIdType.MESH
ANY = pl.BlockSpec(memory_space=pl.ANY)

GELU_C = math.sqrt(2.0 / math.pi)
GELU_A = 0.044715

SDS = jax.ShapeDtypeStruct


def _cp(sem=None, limit=VMEM_LIMIT):
    return pltpu.CompilerParams(dimension_semantics=sem, vmem_limit_bytes=limit)


def _dot(a, b):
    return jnp.dot(a, b, preferred_element_type=F32)


def _dot_nt(a, b):
    return lax.dot_general(a, b, (((1,), (1,)), ((), ())), preferred_element_type=F32)


def _dot_tn(a, b):
    return lax.dot_general(a, b, (((0,), (0,)), ((), ())), preferred_element_type=F32)


def _sig(x):
    return jax.nn.sigmoid(x)


def _rms(x):
    rs = lax.rsqrt(jnp.mean(x * x, axis=-1, keepdims=True) + EPS)
    return rs, x * rs


def _slot(n):
    return 4 * (n % 2) + n // 2


def _const(shape):
    n = len(shape)
    return pl.BlockSpec(shape, lambda *_: (0,) * n)


def _sum4(p):
    return (p[0] + p[1]) + (p[2] + p[3])


def _s5_param_fn(log_dt, lam_re, lam_im, bt_re, bt_im):
    dt = jnp.exp(log_dt)
    mag = jnp.exp(lam_re * dt)
    ang = lam_im * dt
    abar_re = mag * jnp.cos(ang)
    abar_im = mag * jnp.sin(ang)
    num_re = abar_re - 1.0
    num_im = abar_im
    den = lam_re * lam_re + lam_im * lam_im
    coef_re = (num_re * lam_re + num_im * lam_im) / den
    coef_im = (num_im * lam_re - num_re * lam_im) / den
    bbar_re = coef_re[:, :, None] * bt_re - coef_im[:, :, None] * bt_im
    bbar_im = coef_re[:, :, None] * bt_im + coef_im[:, :, None] * bt_re
    return abar_re, abar_im, bbar_re, bbar_im


def _s5_params(log_dt, lam_re, lam_im, bt_re, bt_im):
    def body(ld, lr, li, br, bi, o_ar, o_ai, o_br, o_bi):
        ar, ai, bbr, bbi = _s5_param_fn(ld[...], lr[...], li[...], br[...], bi[...])
        o_ar[...] = ar
        o_ai[...] = ai
        o_br[...] = bbr
        o_bi[...] = bbi

    return pl.pallas_call(
        body, name="s5_params",
        out_shape=(SDS(lam_re.shape, F32), SDS(lam_re.shape, F32), SDS(bt_re.shape, F32), SDS(bt_re.shape, F32)),
    )(log_dt, lam_re, lam_im, bt_re, bt_im)


def _s5_params_bwd(log_dt, lam_re, lam_im, bt_re, bt_im, g_ar, g_ai, g_br, g_bi):
    def body(ld, lr, li, br, bi, car0, car1, cai0, cai1, cbr0, cbr1, cbi0, cbi1, o_ld, o_lr, o_li, o_br, o_bi):
        _, vjp = jax.vjp(_s5_param_fn, ld[...], lr[...], li[...], br[...], bi[...])
        both = lambda a, b: jnp.stack([a[...], b[...]], axis=0)
        d_ld, d_lr, d_li, d_br, d_bi = vjp((both(car0, car1), both(cai0, cai1), both(cbr0, cbr1), both(cbi0, cbi1)))
        o_ld[...] = d_ld
        o_lr[...] = d_lr
        o_li[...] = d_li
        o_br[...] = d_br
        o_bi[...] = d_bi

    return pl.pallas_call(
        body, name="s5_params_bwd",
        out_shape=(SDS(log_dt.shape, F32), SDS(lam_re.shape, F32), SDS(lam_re.shape, F32),
                   SDS(bt_re.shape, F32), SDS(bt_re.shape, F32)),
    )(log_dt, lam_re, lam_im, bt_re, bt_im, *g_ar, *g_ai, *g_br, *g_bi)


def _norm_proj(layer, x, norm_g, wg_in, b_in, carry=None):
    def body(x_ref, g_ref, w_ref, b_ref, o_ref):
        _, xn = _rms(x_ref[...])
        h = (xn * g_ref[layer:layer + 1, :]).astype(BF16)
        for k in range(N_DEV):
            cols = slice(k * WIDTH, (k + 1) * WIDTH)
            o_ref[:, cols] = _dot(h, w_ref[k]) + b_ref[layer:layer + 1, cols]

    (proj,), moved = _pcall(
        body, name=f"norm_proj_l{layer}",
        out_shape=[SDS((SEQ, N_IN), F32)],
        grid=(SEQ // TILE_M,),
        in_specs=[pl.BlockSpec((TILE_M, D_MODEL), lambda i: (i, 0)),
                  _const((DEPTH, D_MODEL)),
                  _const((N_DEV, D_MODEL, WIDTH)),
                  _const((DEPTH, N_IN))],
        out_specs=[pl.BlockSpec((TILE_M, N_IN), lambda i: (i, 0))],
        args=[x, norm_g, wg_in, b_in], sem=("parallel",), carry=carry)
    return proj, moved


TIME_BLK = 512
N_TBLK = SEQ // TIME_BLK
N_PANEL = CH_S // 128
STATE_SHAPE = (N_PANEL, SEQ * 8, 128)


def _s5_layer_specs(layer):
    mat = lambda: pl.BlockSpec((None, N_GROUP, GROUP_W, STATE), lambda i: (layer, 0, 0, 0))
    ab = lambda: pl.BlockSpec((None, N_GROUP, STATE), lambda i: (layer, 0, 0))
    return [mat(), mat(), mat(), mat(), ab(), ab(), _const((DEPTH, WIDTH))]


def _s5_layer_scratch():
    return [pltpu.VMEM((N_CHUNK, CH_W, CH_S), BF16)] * 4 + [pltpu.VMEM((8, CH_S), F32)] * 2


def _s5_layer_fill(btre_ref, btim_ref, cre_ref, cim_ref, are_ref, aim_ref, bdre, bdim, ctre, ctim, a1, a2):
    for m in (bdre, bdim, ctre, ctim):
        m[...] = jnp.zeros_like(m)
    for grp in range(N_GROUP):
        k, g = divmod(grp, CH_G)
        rows = slice(g * GROUP_W, (g + 1) * GROUP_W)
        cols = slice(g * STATE, (g + 1) * STATE)
        bdre[k, rows, cols] = btre_ref[grp].astype(BF16)
        bdim[k, rows, cols] = btim_ref[grp].astype(BF16)
        ctre[k, rows, cols] = cre_ref[grp].astype(BF16)
        ctim[k, rows, cols] = cim_ref[grp].astype(BF16)
        ar = are_ref[grp:grp + 1, :]
        ai = aim_ref[grp:grp + 1, :]
        a1[k:k + 1, cols] = ar
        a1[N_CHUNK + k:N_CHUNK + k + 1, cols] = ar
        a2[k:k + 1, cols] = -ai
        a2[N_CHUNK + k:N_CHUNK + k + 1, cols] = ai


def _tile_load(ref, base):
    return jnp.concatenate([ref[p, pl.ds(base, 8), :] for p in range(N_PANEL)], axis=1)


def _tile_store(ref, base, tile):
    for p in range(N_PANEL):
        ref[p, pl.ds(base, 8), :] = tile[:, p * 128:(p + 1) * 128]


def _rows_load(ref, row):
    return jnp.concatenate([ref[p, pl.ds(row, TIME_BLK, stride=8), :] for p in range(N_PANEL)], axis=1)


def _rows_store(ref, row, val):
    for p in range(N_PANEL):
        ref[p, pl.ds(row, TIME_BLK, stride=8), :] = val[:, p * 128:(p + 1) * 128]


def _s5_scan_fwd(layer, proj, bbt_re, bbt_im, c_re, c_im, abar_re, abar_im, d_skip, carry=None):
    def body(u_ref, btre_ref, btim_ref, cre_ref, cim_ref, are_ref, aim_ref, d_ref, s_ref, y_ref,
             bdre, bdim, ctre, ctim, a1, a2, state):
        @pl.when(pl.program_id(0) == 0)
        def _():
            _s5_layer_fill(btre_ref, btim_ref, cre_ref, cim_ref, are_ref, aim_ref, bdre, bdim, ctre, ctim, a1, a2)
            state[...] = jnp.zeros_like(state)

        for k in range(N_CHUNK):
            ub = u_ref[:, k * CH_W:(k + 1) * CH_W].astype(BF16)
            _rows_store(s_ref, k, _dot(ub, bdre[k]))
            _rows_store(s_ref, N_CHUNK + k, _dot(ub, bdim[k]))
        m1 = a1[...]
        m2 = a2[...]

        def step(t, tile):
            base = pl.multiple_of(t * 8, 8)
            tile = m1 * tile + m2 * pltpu.roll(tile, N_CHUNK, 0) + _tile_load(s_ref, base)
            _tile_store(s_ref, base, tile)
            return tile

        state[...] = lax.fori_loop(0, TIME_BLK, step, state[...], unroll=8)
        d = d_ref[layer:layer + 1, :]
        for k in range(N_CHUNK):
            cols = slice(k * CH_W, (k + 1) * CH_W)
            y = (_dot_nt(_rows_load(s_ref, k).astype(BF16), ctre[k])
                 - _dot_nt(_rows_load(s_ref, N_CHUNK + k).astype(BF16), ctim[k]))
            y_ref[:, cols] = y + d[:, cols] * u_ref[:, cols]

    return _pcall(
        body, name=f"s5_fwd_l{layer}",
        out_shape=(SDS(STATE_SHAPE, F32), SDS((SEQ, WIDTH), F32)),
        grid=(N_TBLK,),
        in_specs=[pl.BlockSpec((TIME_BLK, WIDTH), lambda i: (i, 0))] + _s5_layer_specs(layer),
        out_specs=(pl.BlockSpec((N_PANEL, TIME_BLK * 8, 128), lambda i: (0, i, 0)),
                   pl.BlockSpec((TIME_BLK, WIDTH), lambda i: (i, 0))),
        scratch_shapes=_s5_layer_scratch() + [pltpu.VMEM((8, CH_S), F32)],
        args=[proj, bbt_re, bbt_im, c_re, c_im, abar_re, abar_im, d_skip], sem=("arbitrary",), carry=carry)


def _s5_scan_bwd(layer, dy0, proj, states, bbt_re, bbt_im, c_re, c_im, abar_re, abar_im, d_skip, dproj,
                 carry=None):
    def body(dy_ref, u_ref, s_ref, sprev_ref, btre_ref, btim_ref, cre_ref, cim_ref, are_ref, aim_ref, d_ref, _,
             du_ref, gbre_ref, gbim_ref, gcre_ref, gcim_ref, gare_ref, gaim_ref, gd_ref,
             lam_ref, bdre, bdim, ctre, ctim, a1, a2, state, acc1, acc2, gbre, gbim, gcre, gcim, gd):
        step_id = pl.program_id(0)

        @pl.when(step_id == 0)
        def _():
            _s5_layer_fill(btre_ref, btim_ref, cre_ref, cim_ref, are_ref, aim_ref, bdre, bdim, ctre, ctim, a1, a2)
            for r in (state, acc1, acc2, gbre, gbim, gcre, gcim, gd):
                r[...] = jnp.zeros_like(r)

        for k in range(N_CHUNK):
            dyb = dy_ref[:, k * CH_W:(k + 1) * CH_W].astype(BF16)
            _rows_store(lam_ref, k, _dot(dyb, ctre[k]))
            _rows_store(lam_ref, N_CHUNK + k, -_dot(dyb, ctim[k]))
            gcre[k] += _dot_tn(dyb, _rows_load(s_ref, k).astype(BF16))
            gcim[k] -= _dot_tn(dyb, _rows_load(s_ref, N_CHUNK + k).astype(BF16))

        m1 = a1[...]
        m2 = -a2[...]

        def advance(base, tile):
            tile = m1 * tile + m2 * pltpu.roll(tile, N_CHUNK, 0) + _tile_load(lam_ref, base)
            _tile_store(lam_ref, base, tile)
            return tile

        def step(n, c):
            tile, p1, p2 = c
            base = pl.multiple_of((TIME_BLK - 1 - n) * 8, 8)
            tile = advance(base, tile)
            before = _tile_load(s_ref, pl.multiple_of(base - 8, 8))
            return tile, p1 + tile * before, p2 + pltpu.roll(tile, N_CHUNK, 0) * before

        tile, p1, p2 = lax.fori_loop(0, TIME_BLK - 1, step, (state[...], acc1[...], acc2[...]), unroll=8)
        tile = advance(0, tile)
        before = _tile_load(sprev_ref, 0) * (step_id < N_TBLK - 1).astype(F32)
        state[...] = tile
        acc1[...] = p1 + tile * before
        acc2[...] = p2 + pltpu.roll(tile, N_CHUNK, 0) * before

        d = d_ref[layer:layer + 1, :]
        for k in range(N_CHUNK):
            cols = slice(k * CH_W, (k + 1) * CH_W)
            lrb = _rows_load(lam_ref, k).astype(BF16)
            lib = _rows_load(lam_ref, N_CHUNK + k).astype(BF16)
            u = u_ref[:, cols]
            ub = u.astype(BF16)
            dy = dy_ref[:, cols]
            du = dy * d[:, cols] + _dot_nt(lrb, bdre[k]) + _dot_nt(lib, bdim[k])
            du_ref[:, cols] = du.astype(BF16)
            gbre[k] += _dot_tn(ub, lrb)
            gbim[k] += _dot_tn(ub, lib)
        gd[...] += jnp.sum(dy_ref[...] * u_ref[...], axis=0, keepdims=True)

        @pl.when(step_id == N_TBLK - 1)
        def _():
            gd_ref[...] = gd[...]
            ga_re = acc1[0:N_CHUNK, :] + acc1[N_CHUNK:, :]
            ga_im = acc2[0:N_CHUNK, :] - acc2[N_CHUNK:, :]
            for grp in range(N_GROUP):
                k, g = divmod(grp, CH_G)
                rows = slice(g * GROUP_W, (g + 1) * GROUP_W)
                cols = slice(g * STATE, (g + 1) * STATE)
                gcre_ref[grp] = gcre[k, rows, cols]
                gcim_ref[grp] = gcim[k, rows, cols]
                gbre_ref[grp] = gbre[k, rows, cols]
                gbim_ref[grp] = gbim[k, rows, cols]
                gare_ref[grp:grp + 1, :] = ga_re[k:k + 1, cols]
                gaim_ref[grp:grp + 1, :] = ga_im[k:k + 1, cols]

    back = lambda i: N_TBLK - 1 - i
    tok = lambda: pl.BlockSpec((TIME_BLK, WIDTH), lambda i: (back(i), 0))
    mat = lambda: _const((N_GROUP, GROUP_W, STATE))
    acc_mat = pltpu.VMEM((N_CHUNK, CH_W, CH_S), F32)
    return _pcall(
        body, name=f"s5_bwd_l{layer}",
        out_shape=(SDS((SEQ, N_IN), BF16), SDS((N_GROUP, GROUP_W, STATE), F32), SDS((N_GROUP, GROUP_W, STATE), F32),
                   SDS((N_GROUP, GROUP_W, STATE), F32), SDS((N_GROUP, GROUP_W, STATE), F32),
                   SDS((N_GROUP, STATE), F32), SDS((N_GROUP, STATE), F32), SDS((1, WIDTH), F32)),
        grid=(N_TBLK,),
        in_specs=[tok(), tok(),
                  pl.BlockSpec((N_PANEL, TIME_BLK * 8, 128), lambda i: (0, back(i), 0)),
                  pl.BlockSpec((N_PANEL, 8, 128), lambda i: (0, jnp.maximum(back(i) * TIME_BLK - 1, 0), 0))]
        + _s5_layer_specs(layer) + [ANY],
        out_specs=(tok(), mat(), mat(), mat(), mat(), _const((N_GROUP, STATE)), _const((N_GROUP, STATE)),
                   _const((1, WIDTH))),
        scratch_shapes=[pltpu.VMEM((N_PANEL, TIME_BLK * 8, 128), F32)] + _s5_layer_scratch()
        + [pltpu.VMEM((8, CH_S), F32)] * 3 + [acc_mat] * 4 + [pltpu.VMEM((1, WIDTH), F32)],
        args=[dy0, proj, states, states, bbt_re, bbt_im, c_re, c_im, abar_re, abar_im, d_skip, dproj],
        aliases={11: 0}, sem=("arbitrary",), limit=VMEM_LIMIT_BIG, carry=carry)


def _pool_counts(win):
    t = lax.broadcasted_iota(jnp.int32, (SEQ, POOL_GROUP), 0)
    return t, jnp.minimum(t + 1, win).astype(F32)


def _pool_fwd(layer, proj):
    def body(u_ref, o_ref):
        for gi, win in enumerate(POOL_WINDOWS):
            cols = slice(gi * POOL_GROUP, (gi + 1) * POOL_GROUP)
            u = u_ref[:, cols]
            t, count = _pool_counts(win)
            acc = u
            k = 1
            while k < win:
                acc = acc + jnp.where(t >= k, pltpu.roll(acc, k, 0), 0.0)
                k *= 2
            o_ref[:, cols] = acc / count - u

    return pl.pallas_call(
        body, name=f"pool_fwd_l{layer}",
        out_shape=SDS((SEQ, WIDTH), F32),
        grid=(1,),
        in_specs=[pl.BlockSpec((SEQ, WIDTH), lambda i: (0, 2))],
        out_specs=pl.BlockSpec((SEQ, WIDTH), lambda i: (0, 0)),
        compiler_params=_cp(("arbitrary",)),
    )(proj)


def _gelu_parts(y0):
    t = jnp.tanh(GELU_C * (y0 + GELU_A * (y0 * y0 * y0)))
    return t, 0.5 * y0 * (1.0 + t)


def _mix_forward(layer, p_ref, y0_ref, pooled_ref, wglu_ref, bglu_ref, pw_ref, scale_ref, wa_ref, wb_ref):
    za = p_ref[:, WIDTH:2 * WIDTH]
    zb = p_ref[:, 3 * WIDTH:4 * WIDTH]
    ga = p_ref[:, 4 * WIDTH:4 * WIDTH + D_MODEL]
    gb = p_ref[:, 4 * WIDTH + D_MODEL:]
    y0 = y0_ref[...]
    t, y1 = _gelu_parts(y0)
    y1b = y1.astype(BF16)
    q = _dot(y1b, wglu_ref[...].reshape(WIDTH, WIDTH)) + bglu_ref[layer:layer + 1, :]
    sq = _sig(q)
    y2 = y1 * sq
    sza = _sig(za)
    silu_za = za * sza
    ya = y2 * silu_za
    pooled = pooled_ref[...]
    mixed = jnp.concatenate(
        [_dot(pooled[:, g * POOL_GROUP:(g + 1) * POOL_GROUP].astype(BF16), pw_ref[g].astype(BF16))
         for g in range(len(POOL_WINDOWS))], axis=1)
    szb = _sig(zb)
    silu_zb = zb * szb
    scale = scale_ref[layer:layer + 1, :]
    ms = mixed * scale
    yb = ms * silu_zb
    yab = ya.astype(BF16)
    ybb = yb.astype(BF16)
    ma = jnp.concatenate([_dot(yab, wa_ref[k]) for k in range(N_DEV)], axis=1)
    mb = jnp.concatenate([_dot(ybb, wb_ref[k]) for k in range(N_DEV)], axis=1)
    sga = _sig(ga)
    sgb = _sig(gb)
    merged = sga * ma + sgb * mb
    return dict(za=za, zb=zb, y0=y0, t=t, y1=y1, y1b=y1b, sq=sq, y2=y2, sza=sza, silu_za=silu_za,
                pooled=pooled, mixed=mixed, szb=szb, silu_zb=silu_zb, scale=scale, ms=ms, yab=yab, ybb=ybb,
                ma=ma, mb=mb, sga=sga, sgb=sgb, merged=merged)


def _mix_weight_specs(layer):
    return [_const((N_DEV, WIDTH // N_DEV, WIDTH)),
            _const((DEPTH, WIDTH)),
            pl.BlockSpec((None, 4, POOL_GROUP, POOL_GROUP), lambda i: (layer, 0, 0, 0)),
            _const((DEPTH, WIDTH)),
            _const((N_DEV, WIDTH, D_MODEL // N_DEV)),
            _const((N_DEV, WIDTH, D_MODEL // N_DEV)),
            _const((N_DEV, D_MODEL // N_DEV, D_MODEL))]


def _mix_fwd(layer, x, proj, y0, pooled, wg_glu, b_glu, pool_w, pool_scale, wg_a, wg_b, wg_out, carry=None):
    def body(x_ref, p_ref, y0_ref, pooled_ref, wglu_ref, bglu_ref, pw_ref, scale_ref, wa_ref, wb_ref,
             wout_ref, o_ref):
        f = _mix_forward(layer, p_ref, y0_ref, pooled_ref, wglu_ref, bglu_ref, pw_ref, scale_ref, wa_ref, wb_ref)
        wout = wout_ref[...].reshape(D_MODEL, D_MODEL)
        o_ref[...] = x_ref[...] + _dot(f["merged"].astype(BF16), wout)

    (x_next,), moved = _pcall(
        body, name=f"mix_fwd_l{layer}",
        out_shape=[SDS((SEQ, D_MODEL), F32)],
        grid=(SEQ // TILE_M,),
        in_specs=[pl.BlockSpec((TILE_M, D_MODEL), lambda i: (i, 0)),
                  pl.BlockSpec((TILE_M, N_IN), lambda i: (i, 0)),
                  pl.BlockSpec((TILE_M, WIDTH), lambda i: (i, 0)),
                  pl.BlockSpec((TILE_M, WIDTH), lambda i: (i, 0))] + _mix_weight_specs(layer),
        out_specs=[pl.BlockSpec((TILE_M, D_MODEL), lambda i: (i, 0))],
        args=[x, proj, y0, pooled, wg_glu, b_glu, pool_w, pool_scale, wg_a, wg_b, wg_out],
        sem=("parallel",), carry=carry)
    return x_next, moved


def _loss_head(x, target, final_g):
    def body(x_ref, t_ref, g_ref, dx_ref, loss_ref, gg_ref):
        @pl.when(pl.program_id(0) == 0)
        def _():
            loss_ref[...] = jnp.zeros_like(loss_ref)
            gg_ref[...] = jnp.zeros_like(gg_ref)

        g = g_ref[...]
        rs, xn = _rms(x_ref[...])
        err = xn * g - t_ref[...]
        loss_ref[...] += 0.5 * jnp.sum(jnp.mean(err * err, axis=-1, keepdims=True), axis=0, keepdims=True)
        dy = err * (1.0 / D_MODEL)
        gg_ref[...] += jnp.sum(dy * xn, axis=0, keepdims=True)
        dxn = dy * g
        dx_ref[...] = rs * (dxn - xn * jnp.mean(dxn * xn, axis=-1, keepdims=True))

    return pl.pallas_call(
        body, name="loss_head",
        out_shape=(SDS((SEQ, D_MODEL), F32), SDS((1, 1), F32), SDS((1, D_MODEL), F32)),
        grid=(SEQ // TILE_M,),
        in_specs=[pl.BlockSpec((TILE_M, D_MODEL), lambda i: (i, 0)),
                  pl.BlockSpec((TILE_M, D_MODEL), lambda i: (i, 0)),
                  _const((1, D_MODEL))],
        out_specs=(pl.BlockSpec((TILE_M, D_MODEL), lambda i: (i, 0)), _const((1, 1)), _const((1, D_MODEL))),
        compiler_params=_cp(("arbitrary",)),
    )(x, target, final_g)


def _big_shapes():
    return dict(w_out=(DEPTH, N_DEV, D_MODEL // N_DEV, D_MODEL), w_branch_a=(DEPTH, N_DEV, WIDTH, D_MODEL // N_DEV),
                w_branch_b=(DEPTH, N_DEV, WIDTH, D_MODEL // N_DEV), ssm_w_glu=(DEPTH, N_DEV, WIDTH // N_DEV, WIDTH),
                w_in=(DEPTH, N_DEV, D_MODEL, WIDTH))


def _mix_bwd(layer, dx_next, proj, y0, pooled, wg_glu, b_glu, pool_w, pool_scale, wg_a, wg_b, wg_out, prev,
             carry=None):
    n_k = N_DEV
    n_prev = 0 if prev is None else len(prev)

    def body(*refs):
        (dx_ref, p_ref, y0_ref, pooled_ref, wglu_ref, bglu_ref, pw_ref, scale_ref, wa_ref, wb_ref,
         wout_ref) = refs[:11]
        (dproj_ref, dy0_ref, dpooled_ref, gwout_ref, gwa_ref, gwb_ref, gwglu_ref, gpw_ref,
         gscale_ref, gbglu_ref) = refs[11 + n_prev:]

        @pl.when(pl.program_id(0) == 0)
        def _():
            for r in (gwout_ref, gwa_ref, gwb_ref, gwglu_ref, gpw_ref, gscale_ref, gbglu_ref):
                r[...] = jnp.zeros_like(r)

        f = _mix_forward(layer, p_ref, y0_ref, pooled_ref, wglu_ref, bglu_ref, pw_ref, scale_ref, wa_ref, wb_ref)
        wglu = wglu_ref[...].reshape(WIDTH, WIDTH)
        wout = wout_ref[...].reshape(D_MODEL, D_MODEL)
        blk = D_MODEL // n_k
        dxb = dx_ref[...].astype(BF16)
        dmerged = _dot_nt(dxb, wout)
        gwout = _dot_tn(f["merged"].astype(BF16), dxb)
        for k in range(n_k):
            gwout_ref[_slot(k)] += gwout[k * blk:(k + 1) * blk, :]
        dma = dmerged * f["sga"]
        dmb = dmerged * f["sgb"]
        dga = dmerged * f["ma"] * f["sga"] * (1.0 - f["sga"])
        dgb = dmerged * f["mb"] * f["sgb"] * (1.0 - f["sgb"])
        dmab = dma.astype(BF16)
        dmbb = dmb.astype(BF16)
        dya = jnp.zeros((TILE_M, WIDTH), F32)
        dyb = jnp.zeros((TILE_M, WIDTH), F32)
        for k in range(n_k):
            da_k = dmab[:, k * blk:(k + 1) * blk]
            db_k = dmbb[:, k * blk:(k + 1) * blk]
            dya = dya + _dot_nt(da_k, wa_ref[k])
            dyb = dyb + _dot_nt(db_k, wb_ref[k])
            gwa_ref[_slot(k)] += _dot_tn(f["yab"], da_k)
            gwb_ref[_slot(k)] += _dot_tn(f["ybb"], db_k)
        zb, szb = f["zb"], f["szb"]
        dzb = dyb * f["ms"] * (szb * (1.0 + zb * (1.0 - szb)))
        dms = dyb * f["silu_zb"]
        gscale_ref[...] += jnp.sum(dms * f["mixed"], axis=0, keepdims=True)
        dmixed = (dms * f["scale"]).astype(BF16)
        pooled = f["pooled"]
        for g in range(len(POOL_WINDOWS)):
            cols = slice(g * POOL_GROUP, (g + 1) * POOL_GROUP)
            dpooled_ref[:, cols] = _dot_nt(dmixed[:, cols], pw_ref[g].astype(BF16))
            gpw_ref[g] += _dot_tn(pooled[:, cols].astype(BF16), dmixed[:, cols])
        za, sza = f["za"], f["sza"]
        dza = dya * f["y2"] * (sza * (1.0 + za * (1.0 - sza)))
        dy2 = dya * f["silu_za"]
        sq = f["sq"]
        dq = dy2 * f["y1"] * sq * (1.0 - sq)
        dqb = dq.astype(BF16)
        dy1 = dy2 * sq + _dot_nt(dqb, wglu)
        gwglu = _dot_tn(f["y1b"], dqb)
        rblk = WIDTH // n_k
        for k in range(n_k):
            gwglu_ref[_slot(k)] += gwglu[k * rblk:(k + 1) * rblk, :]
        gbglu_ref[...] += jnp.sum(dq, axis=0, keepdims=True)
        y0, t = f["y0"], f["t"]
        dgelu = 0.5 * (1.0 + t) + 0.5 * y0 * (1.0 - t * t) * (GELU_C * (1.0 + 3.0 * GELU_A * y0 * y0))
        dy0_ref[...] = dy1 * dgelu
        zeros = jnp.zeros((TILE_M, WIDTH), BF16)
        dproj_ref[:, 0:WIDTH] = zeros
        dproj_ref[:, WIDTH:2 * WIDTH] = dza.astype(BF16)
        dproj_ref[:, 2 * WIDTH:3 * WIDTH] = zeros
        dproj_ref[:, 3 * WIDTH:4 * WIDTH] = dzb.astype(BF16)
        dproj_ref[:, 4 * WIDTH:4 * WIDTH + D_MODEL] = dga.astype(BF16)
        dproj_ref[:, 4 * WIDTH + D_MODEL:] = dgb.astype(BF16)

    tile = lambda w: pl.BlockSpec((TILE_M, w), lambda i: (i, 0))
    shapes = _big_shapes()
    big = ["w_out", "w_branch_a", "w_branch_b", "ssm_w_glu"]
    slab = lambda n: pl.BlockSpec((None,) + shapes[n][1:], lambda i: (layer, 0, 0, 0))
    args = [dx_next, proj, y0, pooled, wg_glu, b_glu, pool_w, pool_scale, wg_a, wg_b, wg_out]
    return _pcall(
        body, name=f"mix_bwd_l{layer}",
        out_shape=(SDS((SEQ, N_IN), BF16), SDS((SEQ, WIDTH), F32), SDS((SEQ, WIDTH), F32))
        + tuple(SDS(shapes[n], F32) for n in big)
        + (SDS((4, POOL_GROUP, POOL_GROUP), F32), SDS((1, WIDTH), F32), SDS((1, WIDTH), F32)),
        grid=(SEQ // TILE_M,),
        in_specs=[tile(D_MODEL), tile(N_IN), tile(WIDTH), tile(WIDTH)] + _mix_weight_specs(layer) + [ANY] * n_prev,
        out_specs=(tile(N_IN), tile(WIDTH), tile(WIDTH)) + tuple(slab(n) for n in big)
        + (_const((4, POOL_GROUP, POOL_GROUP)), _const((1, WIDTH)), _const((1, WIDTH))),
        args=args + list(prev or ()),
        aliases={len(args) + i: 3 + i for i in range(n_prev)},
        sem=("arbitrary",), limit=VMEM_LIMIT_BIG, carry=carry)


def _pool_bwd(layer, dpooled, dproj):
    def body(dp_ref, _, o_ref):
        for gi, win in enumerate(POOL_WINDOWS):
            cols = slice(gi * POOL_GROUP, (gi + 1) * POOL_GROUP)
            dp = dp_ref[:, cols]
            t, count = _pool_counts(win)
            e = dp / count
            acc = e
            k = 1
            while k < win:
                acc = acc + jnp.where(t < SEQ - k, pltpu.roll(acc, SEQ - k, 0), 0.0)
                k *= 2
            o_ref[:, cols] = (acc - dp).astype(BF16)

    return pl.pallas_call(
        body, name=f"pool_bwd_l{layer}",
        out_shape=SDS((SEQ, N_IN), BF16),
        grid=(1,),
        in_specs=[pl.BlockSpec((SEQ, WIDTH), lambda i: (0, 0)), ANY],
        out_specs=pl.BlockSpec((SEQ, WIDTH), lambda i: (0, 2)),
        input_output_aliases={1: 0},
        compiler_params=_cp(("arbitrary",)),
    )(dpooled, dproj)


def _proj_wgrad(layer, x, norm_g, dproj, prev):
    tm = 512
    n_prev = 0 if prev is None else 1

    def body(*refs):
        x_ref, g_ref, dp_ref = refs[:3]
        gw_ref, gb_ref = refs[3 + n_prev:]

        @pl.when(pl.program_id(1) == 0)
        def _():
            gw_ref[...] = jnp.zeros_like(gw_ref)
            gb_ref[...] = jnp.zeros_like(gb_ref)

        _, xn = _rms(x_ref[...])
        h = (xn * g_ref[layer:layer + 1, :]).astype(BF16)
        dp = dp_ref[...]
        gw_ref[...] += _dot_tn(h, dp)
        gb_ref[...] += jnp.sum(dp.astype(F32), axis=0, keepdims=True)

    return pl.pallas_call(
        body, name=f"proj_wgrad_l{layer}",
        out_shape=(SDS(_big_shapes()["w_in"], F32), SDS((1, N_IN), F32)),
        grid=(N_DEV, SEQ // tm),
        in_specs=[pl.BlockSpec((tm, D_MODEL), lambda n, t: (t, 0)),
                  _const((DEPTH, D_MODEL)),
                  pl.BlockSpec((tm, WIDTH), lambda n, t: (t, n))] + [ANY] * n_prev,
        out_specs=(pl.BlockSpec((None, None, D_MODEL, WIDTH), lambda n, t: (layer, _slot(n), 0, 0)),
                   pl.BlockSpec((1, WIDTH), lambda n, t: (0, n))),
        input_output_aliases={3: 0} if n_prev else {},
        compiler_params=_cp(("parallel", "arbitrary")),
    )(x, norm_g, dproj, *([prev] if n_prev else []))


def _proj_dgrad(layer, dx_next, x, norm_g, dproj, wg_in, carry=None):
    def body(dxn_ref, x_ref, g_ref, dp_ref, w_ref, dx_ref, gg_ref):
        @pl.when(pl.program_id(0) == 0)
        def _():
            gg_ref[...] = jnp.zeros_like(gg_ref)

        dh = jnp.zeros((TILE_M, D_MODEL), F32)
        for k in range(N_DEV):
            dh = dh + _dot_nt(dp_ref[:, k * WIDTH:(k + 1) * WIDTH], w_ref[k])
        rs, xn = _rms(x_ref[...])
        gg_ref[...] += jnp.sum(dh * xn, axis=0, keepdims=True)
        dxn = dh * g_ref[layer:layer + 1, :]
        dx_ref[...] = dxn_ref[...] + rs * (dxn - xn * jnp.mean(dxn * xn, axis=-1, keepdims=True))

    return _pcall(
        body, name=f"proj_dgrad_l{layer}",
        out_shape=(SDS((SEQ, D_MODEL), F32), SDS((1, D_MODEL), F32)),
        grid=(SEQ // TILE_M,),
        in_specs=[pl.BlockSpec((TILE_M, D_MODEL), lambda i: (i, 0)),
                  pl.BlockSpec((TILE_M, D_MODEL), lambda i: (i, 0)),
                  _const((DEPTH, D_MODEL)),
                  pl.BlockSpec((TILE_M, N_IN), lambda i: (i, 0)),
                  _const((N_DEV, D_MODEL, WIDTH))],
        out_specs=(pl.BlockSpec((TILE_M, D_MODEL), lambda i: (i, 0)), _const((1, D_MODEL))),
        args=[dx_next, x, norm_g, dproj, wg_in], sem=("arbitrary",), carry=carry)


def _my_place():
    return lax.axis_index("x"), lax.axis_index("y"), lax.axis_index("c")


def _gather_plan(shards, layer):
    n = len(shards)

    def parts(ins, outs, sems):
        send_sems, recv_sems, local_sems = sems
        x, y, c = _my_place()
        chips = [(1 - x, y), (x, 1 - y), (1 - x, 1 - y)]

        def rows(t, place):
            px, py, pc = place
            return outs[t].at[pl.ds(4 * px + 2 * py + pc, 1)]

        def copy(t, k, block, to, from_src=False):
            return pltpu.make_async_remote_copy(
                src_ref=ins[t].at[pl.ds(layer, 1)] if from_src else rows(t, block), dst_ref=rows(t, block),
                send_sem=send_sems.at[7 * t + k], recv_sem=recv_sems.at[7 * t + k], device_id=to,
                device_id_type=MESH)

        def mine(t):
            return pltpu.make_async_copy(ins[t].at[pl.ds(layer, 1)], rows(t, (x, y, c)), local_sems.at[t])

        return (x, y, c), chips, copy, mine

    def start(ins, outs, sems):
        me, chips, copy, mine = parts(ins, outs, sems)
        x, y, c = me
        for t in range(n):
            mine(t).start()
            copy(t, 0, me, (x, y, 1 - c), from_src=True).start()
            for j, chip in enumerate(chips):
                copy(t, 1 + j, me, (*chip, c), from_src=True).start()

    def finish(ins, outs, sems):
        me, chips, copy, mine = parts(ins, outs, sems)
        x, y, c = me
        sibling = (x, y, 1 - c)
        for t in range(n):
            for j, chip in enumerate(chips):
                copy(t, 1 + j, (*chip, c), me).wait_recv()
                copy(t, 4 + j, (*chip, c), sibling).start()
        for t in range(n):
            copy(t, 0, sibling, me).wait_recv()
            for j, chip in enumerate(chips):
                copy(t, 4 + j, (*chip, 1 - c), me).wait_recv()
            for k in range(7):
                copy(t, k, me, sibling, from_src=k < 4).wait_send()
            mine(t).wait()

    out_shape = [SDS((N_DEV,) + a.shape[1:], a.dtype) for a in shards]
    sems = [pltpu.SemaphoreType.DMA((7 * n,)), pltpu.SemaphoreType.DMA((7 * n,)), pltpu.SemaphoreType.DMA((n,))]
    return _Carried(shards, out_shape, sems, start, finish)


class _Carried:
    def __init__(self, ins, out_shape, sems, start, finish):
        self.ins, self.out_shape, self.sems = list(ins), list(out_shape), list(sems)
        self.start, self.finish = start, finish


def _pcall(body, *, name, grid, in_specs, out_specs, out_shape, args, scratch_shapes=(), aliases=None,
           sem=None, limit=VMEM_LIMIT, carry=None):
    out_shape, out_specs, scratch_shapes = list(out_shape), list(out_specs), list(scratch_shapes)
    n_in, n_out, n_scr = len(args), len(out_shape), len(scratch_shapes)
    if carry is None:
        kern, c_ins, c_out, c_sems = body, [], [], []
    else:
        c_ins, c_out, c_sems = carry.ins, carry.out_shape, carry.sems
        ci, co = len(c_ins), len(c_out)
        steps = tuple(grid)

        def kern(*refs):
            o0 = n_in + ci
            s0 = o0 + n_out + co
            mine = refs[:n_in] + refs[o0:o0 + n_out] + refs[s0:s0 + n_scr]
            theirs = (refs[n_in:o0], refs[o0 + n_out:s0], refs[s0 + n_scr:])
            first = pl.program_id(0) == 0
            last = pl.program_id(0) == steps[0] - 1
            for a in range(1, len(steps)):
                first = jnp.logical_and(first, pl.program_id(a) == 0)
                last = jnp.logical_and(last, pl.program_id(a) == steps[a] - 1)

            @pl.when(first)
            def _():
                carry.start(*theirs)

            body(*mine)

            @pl.when(last)
            def _():
                carry.finish(*theirs)

        sem = ("arbitrary",) * len(steps)
    res = pl.pallas_call(
        kern, name=name, grid=tuple(grid),
        in_specs=list(in_specs) + [ANY] * len(c_ins),
        out_specs=tuple(out_specs + [ANY] * len(c_out)),
        out_shape=tuple(out_shape + c_out),
        scratch_shapes=scratch_shapes + c_sems,
        input_output_aliases=aliases or {},
        compiler_params=_cp(sem, limit),
    )(*args, *c_ins)
    return res[:n_out], res[n_out:]


def _run_carried(name, carry):
    ci, co = len(carry.ins), len(carry.out_shape)

    def body(*refs):
        parts = (refs[:ci], refs[ci:ci + co], refs[ci + co:])
        carry.start(*parts)
        carry.finish(*parts)

    return pl.pallas_call(
        body, name=name, out_shape=tuple(carry.out_shape),
        in_specs=[ANY] * ci, out_specs=tuple([ANY] * co), scratch_shapes=carry.sems,
    )(*carry.ins)


def _sibling_plan(big, small):
    n = len(big)
    n_copies = 4 * n + len(small)

    def copies(ins, outs, sems):
        send_sems, recv_sems = sems
        x, y, c = _my_place()
        pairs = []
        for t, (_, layer) in enumerate(big):
            for s in range(4):
                pairs.append((ins[t].at[layer, pl.ds(4 * (1 - c) + s, 1)], outs[t].at[pl.ds(s, 1)]))
        pairs += list(zip(ins[n:], outs[n:]))
        return [pltpu.make_async_remote_copy(
            src_ref=src, dst_ref=dst, send_sem=send_sems.at[k], recv_sem=recv_sems.at[k],
            device_id=(x, y, 1 - c), device_id_type=MESH) for k, (src, dst) in enumerate(pairs)]

    def start(ins, outs, sems):
        for cp in copies(ins, outs, sems):
            cp.start()

    def finish(ins, outs, sems):
        for cp in copies(ins, outs, sems):
            cp.wait()

    out_shape = [SDS((4,) + a.shape[2:], a.dtype) for a, _ in big] + [SDS(a.shape, a.dtype) for a in small]
    sems = [pltpu.SemaphoreType.DMA((n_copies,)), pltpu.SemaphoreType.DMA((n_copies,))]
    return _Carried([a for a, _ in big] + list(small), out_shape, sems, start, finish)


def _chips_plan(big, small):
    n, n_small = len(big), len(small)
    max_rows = 512
    parts = [max(1, a.shape[1] // max_rows) for a in big]
    n_copies = 3 * (sum(parts) + n_small)

    def copies(ins, outs, sems, landing):
        send_sems, recv_sems, local_sems = sems
        x, y, c = _my_place()
        my_chip = 2 * x + y
        chips = [(1 - x, y), (x, 1 - y), (1 - x, 1 - y)]
        remote, local = [], []
        for chip in chips:
            to = 2 * chip[0] + chip[1]
            slot = to if landing else my_chip
            pairs = []
            for t in range(n):
                rows_per = big[t].shape[1] // parts[t]
                for p in range(parts[t]):
                    rows = pl.ds(p * rows_per, rows_per)
                    pairs.append((ins[t].at[to, rows], outs[t].at[slot, rows]))
            pairs += [(ins[t], outs[t].at[slot]) for t in range(n, n + n_small)]
            for src, dst in pairs:
                k = len(remote)
                remote.append(pltpu.make_async_remote_copy(
                    src_ref=src, dst_ref=dst, send_sem=send_sems.at[k], recv_sem=recv_sems.at[k],
                    device_id=(*chip, c), device_id_type=MESH))
        for t in range(n):
            local.append(pltpu.make_async_copy(ins[t].at[my_chip], outs[t].at[my_chip], local_sems.at[t]))
        for t in range(n, n + n_small):
            local.append(pltpu.make_async_copy(ins[t], outs[t].at[my_chip], local_sems.at[t]))
        return remote + local

    def start(ins, outs, sems):
        for cp in copies(ins, outs, sems, landing=False):
            cp.start()

    def finish(ins, outs, sems):
        for cp in copies(ins, outs, sems, landing=True):
            cp.wait()

    out_shape = [SDS(a.shape, a.dtype) for a in big] + [SDS((N_CHIP,) + a.shape, a.dtype) for a in small]
    sems = [pltpu.SemaphoreType.DMA((n_copies,)), pltpu.SemaphoreType.DMA((n_copies,)),
            pltpu.SemaphoreType.DMA((n + n_small,))]
    return _Carried(list(big) + list(small), out_shape, sems, start, finish)


def _row_block(rows):
    return rows if rows <= 256 else 256


def _add_own(tag, core, g, layer, got):
    _, r, c = got.shape
    rb = _row_block(r)

    def body(core_ref, a_ref, b_ref, o_ref):
        o_ref[...] = (a_ref[...] + b_ref[...]).astype(o_ref.dtype)

    return pl.pallas_call(
        body, name=f"add_{tag}", out_shape=SDS(got.shape, BF16),
        grid_spec=pltpu.PrefetchScalarGridSpec(
            num_scalar_prefetch=1, grid=(4, r // rb),
            in_specs=[pl.BlockSpec((None, None, rb, c), lambda s, j, core: (layer, 4 * core[0] + s, j, 0)),
                      pl.BlockSpec((None, rb, c), lambda s, j, core: (s, j, 0))],
            out_specs=pl.BlockSpec((None, rb, c), lambda s, j, core: (s, j, 0))),
        compiler_params=_cp(("parallel", "parallel")),
    )(core, g, got)


def _add_lists(tag, own, got, grid=None, specs=None):
    n = len(own)

    def body(*refs):
        for a, b, o in zip(refs[:n], refs[n:2 * n], refs[2 * n:]):
            o[...] = a[...] + b[...]

    kw = {}
    if grid is not None:
        kw = dict(grid=grid, in_specs=list(specs) * 2, out_specs=tuple(specs),
                  compiler_params=_cp(("parallel",) * len(grid)))
    return pl.pallas_call(
        body, name=f"add_{tag}", out_shape=tuple(SDS(a.shape, a.dtype) for a in own), **kw)(*own, *got)


def _adamw_math(w, g, m, v):
    m = ADAM_B1 * m + (1.0 - ADAM_B1) * g
    v = ADAM_B2 * v + (1.0 - ADAM_B2) * (g * g)
    m_hat = m / (1.0 - ADAM_B1 ** ADAM_STEP)
    v_hat = v / (1.0 - ADAM_B2 ** ADAM_STEP)
    delta = -ADAM_LR * (m_hat / (jnp.sqrt(v_hat) + ADAM_EPS) + ADAM_WD * w)
    return delta, m, v


def _sum_slots_adamw(tag, slots, w, m, v):
    _, r, c = slots[0].shape
    rb = _row_block(r)

    def body(s0_ref, s1_ref, w_ref, m_ref, v_ref, g_ref, d_ref, nm_ref, nv_ref):
        first = pl.program_id(1) == 0
        g = _sum4([jnp.where(first, s0_ref[k], s1_ref[k]).astype(F32) for k in range(N_CHIP)])
        delta, nm, nv = _adamw_math(w_ref[...], g, m_ref[...], v_ref[...])
        g_ref[...] = g
        d_ref[...] = delta
        nm_ref[...] = nm
        nv_ref[...] = nv

    spec = pl.BlockSpec((None, rb, c), lambda j, l: (l, j, 0))
    sspec = pl.BlockSpec((N_CHIP, rb, c), lambda j, l: (0, j, 0))
    s = SDS((DEPTH, r, c), F32)
    return pl.pallas_call(
        body, name=f"adamw_{tag}", out_shape=(s, s, s, s),
        grid=(r // rb, DEPTH), in_specs=[sspec, sspec, spec, spec, spec], out_specs=(spec, spec, spec, spec),
        compiler_params=_cp(("parallel", "arbitrary")),
    )(*slots, w, m, v)


def _adamw_small(tag, entries, grid=None):
    flat_in, in_specs, out_shape, out_specs, layout = [], [], [], [], []
    for slots, w, m, v, slot_spec, w_spec in entries:
        per_layer = isinstance(slots, (list, tuple))
        n_slot = len(slots) if per_layer else 1
        flat_in += (list(slots) if per_layer else [slots]) + [w, m, v]
        in_specs += [slot_spec] * n_slot + [w_spec] * 3
        out_shape += [SDS(w.shape, F32)] * 4
        out_specs += [w_spec] * 4
        layout.append((per_layer, n_slot))
    n_in = len(flat_in)

    def body(*refs):
        i, o = 0, n_in
        for per_layer, n_slot in layout:
            s_refs = refs[i:i + n_slot]
            w_ref, m_ref, v_ref = refs[i + n_slot:i + n_slot + 3]
            outs = refs[o:o + 4]
            if per_layer:
                for l, s_ref in enumerate(s_refs):
                    at = (slice(l, l + 1),) if len(w_ref.shape) == 2 else (l,)
                    g = _sum4([s_ref[k] for k in range(N_CHIP)])
                    res = (g,) + _adamw_math(w_ref[at], g, m_ref[at], v_ref[at])
                    for o_ref, val in zip(outs, res):
                        o_ref[at] = val
            else:
                g = _sum4([s_refs[0][k] for k in range(N_CHIP)])
                res = (g,) + _adamw_math(w_ref[...], g, m_ref[...], v_ref[...])
                for o_ref, val in zip(outs, res):
                    o_ref[...] = val
            i += n_slot + 3
            o += 4

    kw = {}
    if grid is not None:
        kw = dict(grid=grid, in_specs=in_specs, out_specs=tuple(out_specs),
                  compiler_params=_cp(("parallel",) * len(grid)))
    res = pl.pallas_call(body, name=f"adamw_{tag}", out_shape=tuple(out_shape), **kw)(*flat_in)
    return [tuple(res[4 * e:4 * e + 4]) for e in range(len(entries))]


def kernel(x, norm_g, w_in, b_in, ssm_log_dt, ssm_lam_re, ssm_lam_im, ssm_b_re, ssm_b_im, ssm_c_re, ssm_c_im, ssm_d, ssm_w_glu, ssm_b_glu, pool_w, pool_scale, w_branch_a, w_branch_b, w_out, final_norm_g, loss_target, m_norm_g, m_w_in, m_b_in, m_ssm_log_dt, m_ssm_lam_re, m_ssm_lam_im, m_ssm_b_re, m_ssm_b_im, m_ssm_c_re, m_ssm_c_im, m_ssm_d, m_ssm_w_glu, m_ssm_b_glu, m_pool_w, m_pool_scale, m_w_branch_a, m_w_branch_b, m_w_out, m_final_norm_g, v_norm_g, v_w_in, v_b_in, v_ssm_log_dt, v_ssm_lam_re, v_ssm_lam_im, v_ssm_b_re, v_ssm_b_im, v_ssm_c_re, v_ssm_c_im, v_ssm_d, v_ssm_w_glu, v_ssm_b_glu, v_pool_w, v_pool_scale, v_w_branch_a, v_w_branch_b, v_w_out, v_final_norm_g):
    weights = dict(norm_g=norm_g, w_in=w_in, b_in=b_in, ssm_log_dt=ssm_log_dt, ssm_lam_re=ssm_lam_re,
                   ssm_lam_im=ssm_lam_im, ssm_b_re=ssm_b_re, ssm_b_im=ssm_b_im, ssm_c_re=ssm_c_re,
                   ssm_c_im=ssm_c_im, ssm_d=ssm_d, ssm_w_glu=ssm_w_glu, ssm_b_glu=ssm_b_glu, pool_w=pool_w,
                   pool_scale=pool_scale, w_branch_a=w_branch_a, w_branch_b=w_branch_b, w_out=w_out,
                   final_norm_g=final_norm_g.reshape(1, D_MODEL))
    mom_m = dict(norm_g=m_norm_g, w_in=m_w_in, b_in=m_b_in, ssm_log_dt=m_ssm_log_dt, ssm_lam_re=m_ssm_lam_re,
                 ssm_lam_im=m_ssm_lam_im, ssm_b_re=m_ssm_b_re, ssm_b_im=m_ssm_b_im, ssm_c_re=m_ssm_c_re,
                 ssm_c_im=m_ssm_c_im, ssm_d=m_ssm_d, ssm_w_glu=m_ssm_w_glu, ssm_b_glu=m_ssm_b_glu,
                 pool_w=m_pool_w, pool_scale=m_pool_scale, w_branch_a=m_w_branch_a, w_branch_b=m_w_branch_b,
                 w_out=m_w_out, final_norm_g=m_final_norm_g.reshape(1, D_MODEL))
    mom_v = dict(norm_g=v_norm_g, w_in=v_w_in, b_in=v_b_in, ssm_log_dt=v_ssm_log_dt, ssm_lam_re=v_ssm_lam_re,
                 ssm_lam_im=v_ssm_lam_im, ssm_b_re=v_ssm_b_re, ssm_b_im=v_ssm_b_im, ssm_c_re=v_ssm_c_re,
                 ssm_c_im=v_ssm_c_im, ssm_d=v_ssm_d, ssm_w_glu=v_ssm_w_glu, ssm_b_glu=v_ssm_b_glu,
                 pool_w=v_pool_w, pool_scale=v_pool_scale, w_branch_a=v_w_branch_a, w_branch_b=v_w_branch_b,
                 w_out=v_w_out, final_norm_g=v_final_norm_g.reshape(1, D_MODEL))
    order = ["norm_g", "w_in", "b_in", "ssm_log_dt", "ssm_lam_re", "ssm_lam_im", "ssm_b_re", "ssm_b_im",
             "ssm_c_re", "ssm_c_im", "ssm_d", "ssm_w_glu", "ssm_b_glu", "pool_w", "pool_scale", "w_branch_a",
             "w_branch_b", "w_out", "final_norm_g"]
    big_names = ["w_in", "ssm_w_glu", "w_branch_a", "w_branch_b", "w_out"]

    log_dt3 = ssm_log_dt.reshape(DEPTH, N_GROUP, 1)
    b_t = lambda a: a.transpose(0, 1, 3, 2)
    for d in (weights, mom_m, mom_v):
        d["ssm_b_re"], d["ssm_b_im"] = b_t(d["ssm_b_re"]), b_t(d["ssm_b_im"])
    bt_re, bt_im = weights["ssm_b_re"], weights["ssm_b_im"]
    abar_re, abar_im, bbt_re, bbt_im = _s5_params(log_dt3, ssm_lam_re, ssm_lam_im, bt_re, bt_im)
    s5_args = (bbt_re, bbt_im, ssm_c_re, ssm_c_im, abar_re, abar_im, ssm_d)

    w16 = {n: weights[n].astype(BF16) for n in big_names}
    rest = [w16[n] for n in big_names[1:]]
    wg_in = [None, None]
    wg_rest = [None, None]
    (wg_in[0],) = _run_carried("gather_w_in_l0", _gather_plan([w16["w_in"]], 0))
    xs = [x.reshape(SEQ, D_MODEL)]
    saved = []
    for l in range(DEPTH):
        proj, moved = _norm_proj(l, xs[l], norm_g, wg_in[l], b_in, carry=_gather_plan(rest, 0) if l == 0 else None)
        if l == 0:
            wg_rest[0] = moved
        (states, y0), moved = _s5_scan_fwd(
            l, proj, *s5_args, carry=_gather_plan([w16["w_in"]], 1) if l == 0 else None)
        if l == 0:
            (wg_in[1],) = moved
        pooled = _pool_fwd(l, proj)
        wg_glu, wg_a, wg_b, wg_out = wg_rest[l]
        x_next, moved = _mix_fwd(l, xs[l], proj, y0, pooled, wg_glu, ssm_b_glu, pool_w, pool_scale, wg_a, wg_b,
                                 wg_out, carry=_gather_plan(rest, 1) if l == 0 else None)
        if l == 0:
            wg_rest[1] = moved
        xs.append(x_next)
        saved.append((proj, states, y0, pooled))

    dx, loss_part, g_final = _loss_head(xs[DEPTH], loss_target.reshape(SEQ, D_MODEL), weights["final_norm_g"])
    loss = lax.psum(loss_part[0, 0], ("x", "y", "c"))

    core = lax.axis_index("c").astype(jnp.int32).reshape(1)
    vec_names = ["norm_g", "b_in", "ssm_d", "ssm_b_glu", "pool_scale"]
    flat_names = ["final_norm_g", "ssm_log_dt", "ssm_lam_re", "ssm_lam_im"]
    mat_names = ["pool_w", "ssm_c_re", "ssm_c_im"]
    lane_sparse = {"ssm_c_re": ssm_c_re.shape[1:], "ssm_c_im": ssm_c_im.shape[1:],
                   "ssm_b_re": bt_re.shape, "ssm_b_im": bt_im.shape}
    gridded = set(mat_names) | {"ssm_b_re", "ssm_b_im"}

    def dense(key, a):
        return a.reshape(-1, 128) if key[0] in lane_sparse else a

    def undense(key, slots):
        return slots.reshape((N_CHIP,) + lane_sparse[key[0]]) if key[0] in lane_sparse else slots

    def add_small(tag, keys, own, got):
        out = [None] * len(keys)
        whole = [i for i, k in enumerate(keys) if k[0] not in gridded]
        tiled = [i for i, k in enumerate(keys) if k[0] in gridded]
        for i, r in zip(whole, _add_lists(f"{tag}_a", [own[i] for i in whole], [got[i] for i in whole])):
            out[i] = r
        specs = [pl.BlockSpec((1, POOL_GROUP, POOL_GROUP), lambda j: (j, 0, 0)) if keys[i][0] == "pool_w"
                 else pl.BlockSpec((own[i].shape[0] // N_CHUNK, 128), lambda j: (j, 0)) for i in tiled]
        for i, r in zip(tiled, _add_lists(f"{tag}_b", [own[i] for i in tiled], [got[i] for i in tiled],
                                          grid=(N_CHUNK,), specs=specs)):
            out[i] = r
        return out

    sm = {("final_norm_g", None): g_final}
    g_abar_re, g_abar_im, g_bbt_re, g_bbt_im = ([None] * DEPTH for _ in range(4))
    mix_big, gw_in = None, None
    keys1 = ([(n, 1) for n in vec_names[1:]] + [(n, 1) for n in mat_names] + [("final_norm_g", None)])
    chip1_big = chip1_small = slots1_big = slots1_small = None
    for l in reversed(range(DEPTH)):
        proj, states, y0, pooled = saved[l]
        wg_glu, wg_a, wg_b, wg_out = wg_rest[l]
        carry = None if l == 1 else _chips_plan(chip1_big[:1], [])
        res, moved = _mix_bwd(l, dx, proj, y0, pooled, wg_glu, ssm_b_glu, pool_w, pool_scale, wg_a, wg_b, wg_out,
                              mix_big, carry=carry)
        if l == 0:
            slots1_big = list(moved)
        dproj, dy0, dpooled = res[:3]
        mix_big = list(res[3:7])
        sm[("pool_w", l)], sm[("pool_scale", l)], sm[("ssm_b_glu", l)] = res[7:]
        dproj = _pool_bwd(l, dpooled, dproj)
        carry = None if l == 1 else _chips_plan(chip1_big[1:], chip1_small)
        res, moved = _s5_scan_bwd(l, dy0, proj, states, *s5_args, dproj, carry=carry)
        if l == 0:
            slots1_big += list(moved[:4])
            slots1_small = moved[4:]
        (dproj, g_bbt_re[l], g_bbt_im[l], sm[("ssm_c_re", l)], sm[("ssm_c_im", l)], g_abar_re[l], g_abar_im[l],
         sm[("ssm_d", l)]) = res
        gw_in, sm[("b_in", l)] = _proj_wgrad(l, xs[l], norm_g, dproj, gw_in)
        gw_out, gw_a, gw_b, gw_glu = mix_big
        big_part = [gw_in, gw_glu, gw_a, gw_b, gw_out]
        carry = None
        if l == 1:
            own1 = [dense(k, sm[k]) for k in keys1]
            carry = _sibling_plan([(a, 1) for a in big_part], own1)
        (dx, sm[("norm_g", l)]), moved = _proj_dgrad(l, dx, xs[l], norm_g, dproj, wg_in[l], carry=carry)
        if l == 1:
            chip1_big = [_add_own(f"chip1_{n}", core, a, 1, b) for n, a, b in zip(big_names, big_part, moved[:5])]
            chip1_small = add_small("chip1_small", keys1, own1, moved[5:])
    grad_x = dx.reshape(1, SEQ, D_MODEL)

    g_ld, g_lr, g_li, g_btr, g_bti = _s5_params_bwd(
        log_dt3, ssm_lam_re, ssm_lam_im, bt_re, bt_im, g_abar_re, g_abar_im, g_bbt_re, g_bbt_im)
    sm[("ssm_log_dt", None)] = g_ld.reshape(DEPTH, N_GROUP)
    sm[("ssm_lam_re", None)] = g_lr
    sm[("ssm_lam_im", None)] = g_li
    sm[("ssm_b_re", None)] = g_btr
    sm[("ssm_b_im", None)] = g_bti

    keys0 = ([(n, 0) for n in vec_names] + [("norm_g", 1)] + [(n, 0) for n in mat_names]
             + [(n, None) for n in ("ssm_log_dt", "ssm_lam_re", "ssm_lam_im", "ssm_b_re", "ssm_b_im")])
    own0 = [dense(k, sm[k]) for k in keys0]
    moved = _run_carried("exchange_sibling_l0", _sibling_plan([(a, 0) for a in big_part], own0))
    chip0_big = [_add_own(f"chip0_{n}", core, a, 0, b) for n, a, b in zip(big_names, big_part, moved[:5])]
    chip0_small = add_small("chip0_small", keys0, own0, moved[5:])
    moved = _run_carried("exchange_chips_l0", _chips_plan(chip0_big, chip0_small))
    slots0_big = moved[:5]
    slots = {k: undense(k, s) for k, s in zip(keys0, moved[5:])}
    slots.update({k: undense(k, s) for k, s in zip(keys1, slots1_small)})

    res = {}
    for i, n in enumerate(big_names):
        res[n] = _sum_slots_adamw(n, [slots0_big[i], slots1_big[i]], weights[n], mom_m[n], mom_v[n])
    entries_a = []
    for n in vec_names:
        entries_a.append(([slots[(n, l)] for l in range(DEPTH)], weights[n], mom_m[n], mom_v[n], None, None))
    for n in flat_names:
        entries_a.append((slots[(n, None)], weights[n], mom_m[n], mom_v[n], None, None))
    out_a = _adamw_small("small_a", entries_a)
    for n, r in zip(vec_names + flat_names, out_a):
        res[n] = r
    res["final_norm_g"] = tuple(a.reshape(D_MODEL) for a in res["final_norm_g"])
    pw_s = pl.BlockSpec((N_CHIP, 1, POOL_GROUP, POOL_GROUP), lambda j: (0, j, 0, 0))
    pw_w = pl.BlockSpec((DEPTH, 1, POOL_GROUP, POOL_GROUP), lambda j: (0, j, 0, 0))
    c_s = pl.BlockSpec((N_CHIP, CH_G, GROUP_W, STATE), lambda j: (0, j, 0, 0))
    c_w = pl.BlockSpec((DEPTH, CH_G, GROUP_W, STATE), lambda j: (0, j, 0, 0))
    b_s = pl.BlockSpec((N_CHIP, DEPTH, CH_G, GROUP_W, STATE), lambda j: (0, 0, j, 0, 0))
    b_w = c_w
    entries_b = []
    for n, s_spec, w_spec in (("pool_w", pw_s, pw_w), ("ssm_c_re", c_s, c_w), ("ssm_c_im", c_s, c_w)):
        entries_b.append(([slots[(n, l)] for l in range(DEPTH)], weights[n], mom_m[n], mom_v[n], s_spec, w_spec))
    for n in ("ssm_b_re", "ssm_b_im"):
        entries_b.append((slots[(n, None)], weights[n], mom_m[n], mom_v[n], b_s, b_w))
    out_b = _adamw_small("small_b", entries_b, grid=(N_CHUNK,))
    for n, r in zip(mat_names + ["ssm_b_re", "ssm_b_im"], out_b):
        res[n] = tuple(b_t(a) for a in r) if n in ("ssm_b_re", "ssm_b_im") else r

    outs = [loss, grad_x]
    for i in range(4):
        outs += [res[n][i] for n in order]
    return tuple(outs)
```

```python
import math

import jax
import jax.numpy as jnp
from jax import lax
from jax.experimental import pallas as pl
from jax.experimental.pallas import tpu as pltpu

F32 = jnp.float32
BF16 = jnp.bfloat16

SEQ = 2048
D_MODEL = 1024
N_IN = 4096
WIDTH = 512
N_GROUP = 32
GROUP_W = 16
STATE = 64
N_STATE = N_GROUP * STATE
N_CHUNK = 4
CH_G = N_GROUP // N_CHUNK
CH_W = WIDTH // N_CHUNK
CH_S = N_STATE // N_CHUNK
N_DEV = 8
N_CHIP = 4
POOL_WINDOWS = (2, 4, 8, 16)
POOL_GROUP = 128
EPS = 1e-6
DEPTH = 2

ADAM_LR = 0.001
ADAM_B1 = 0.9
ADAM_B2 = 0.999
ADAM_EPS = 1e-08
ADAM_WD = 0.01
ADAM_STEP = 10

TILE_M = 256
ROW_BLK = 512
VMEM_LIMIT = 48 * 1024 * 1024
VMEM_LIMIT_BIG = 60 * 1024 * 1024
MESH = pl.DeviceIdType.MESH
ANY = pl.BlockSpec(memory_space=pl.ANY)

GELU_C = math.sqrt(2.0 / math.pi)
GELU_A = 0.044715

SDS = jax.ShapeDtypeStruct


def _cp(sem=None, limit=VMEM_LIMIT):
    return pltpu.CompilerParams(dimension_semantics=sem, vmem_limit_bytes=limit)


def _dot(a, b):
    return jnp.dot(a, b, preferred_element_type=F32)


def _dot_nt(a, b):
    return lax.dot_general(a, b, (((1,), (1,)), ((), ())), preferred_element_type=F32)


def _dot_tn(a, b):
    return lax.dot_general(a, b, (((0,), (0,)), ((), ())), preferred_element_type=F32)


def _sig(x):
    return jax.nn.sigmoid(x)


def _rms(x):
    rs = lax.rsqrt(jnp.mean(x * x, axis=-1, keepdims=True) + EPS)
    return rs, x * rs


def _slot(n):
    return 4 * (n % 2) + n // 2


def _const(shape):
    n = len(shape)
    return pl.BlockSpec(shape, lambda *_: (0,) * n)


def _sum4(p):
    return (p[0] + p[1]) + (p[2] + p[3])


def _s5_param_fn(log_dt, lam_re, lam_im, bt_re, bt_im):
    dt = jnp.exp(log_dt)
    mag = jnp.exp(lam_re * dt)
    ang = lam_im * dt
    abar_re = mag * jnp.cos(ang)
    abar_im = mag * jnp.sin(ang)
    num_re = abar_re - 1.0
    num_im = abar_im
    den = lam_re * lam_re + lam_im * lam_im
    coef_re = (num_re * lam_re + num_im * lam_im) / den
    coef_im = (num_im * lam_re - num_re * lam_im) / den
    bbar_re = coef_re[:, :, None] * bt_re - coef_im[:, :, None] * bt_im
    bbar_im = coef_re[:, :, None] * bt_im + coef_im[:, :, None] * bt_re
    return abar_re, abar_im, bbar_re, bbar_im


def _s5_params(log_dt, lam_re, lam_im, bt_re, bt_im):
    def body(ld, lr, li, br, bi, o_ar, o_ai, o_br, o_bi):
        ar, ai, bbr, bbi = _s5_param_fn(ld[...], lr[...], li[...], br[...], bi[...])
        o_ar[...] = ar
        o_ai[...] = ai
        o_br[...] = bbr
        o_bi[...] = bbi

    return pl.pallas_call(
        body, name="s5_params",
        out_shape=(SDS(lam_re.shape, F32), SDS(lam_re.shape, F32), SDS(bt_re.shape, F32), SDS(bt_re.shape, F32)),
    )(log_dt, lam_re, lam_im, bt_re, bt_im)


def _s5_params_bwd(log_dt, lam_re, lam_im, bt_re, bt_im, g_ar, g_ai, g_br, g_bi):
    def body(ld, lr, li, br, bi, car0, car1, cai0, cai1, cbr0, cbr1, cbi0, cbi1, o_ld, o_lr, o_li, o_br, o_bi):
        _, vjp = jax.vjp(_s5_param_fn, ld[...], lr[...], li[...], br[...], bi[...])
        both = lambda a, b: jnp.stack([a[...], b[...]], axis=0)
        d_ld, d_lr, d_li, d_br, d_bi = vjp((both(car0, car1), both(cai0, cai1), both(cbr0, cbr1), both(cbi0, cbi1)))
        o_ld[...] = d_ld
        o_lr[...] = d_lr
        o_li[...] = d_li
        o_br[...] = d_br
        o_bi[...] = d_bi

    return pl.pallas_call(
        body, name="s5_params_bwd",
        out_shape=(SDS(log_dt.shape, F32), SDS(lam_re.shape, F32), SDS(lam_re.shape, F32),
                   SDS(bt_re.shape, F32), SDS(bt_re.shape, F32)),
    )(log_dt, lam_re, lam_im, bt_re, bt_im, *g_ar, *g_ai, *g_br, *g_bi)


def _norm_proj(layer, x, norm_g, wg_in, b_in, carry=None):
    def body(x_ref, g_ref, w_ref, b_ref, o_ref):
        _, xn = _rms(x_ref[...])
        h = (xn * g_ref[layer:layer + 1, :]).astype(BF16)
        for k in range(N_DEV):
            cols = slice(k * WIDTH, (k + 1) * WIDTH)
            o_ref[:, cols] = _dot(h, w_ref[k]) + b_ref[layer:layer + 1, cols]

    (proj,), moved = _pcall(
        body, name=f"norm_proj_l{layer}",
        out_shape=[SDS((SEQ, N_IN), F32)],
        grid=(SEQ // TILE_M,),
        in_specs=[pl.BlockSpec((TILE_M, D_MODEL), lambda i: (i, 0)),
                  _const((DEPTH, D_MODEL)),
                  _const((N_DEV, D_MODEL, WIDTH)),
                  _const((DEPTH, N_IN))],
        out_specs=[pl.BlockSpec((TILE_M, N_IN), lambda i: (i, 0))],
        args=[x, norm_g, wg_in, b_in], sem=("parallel",), carry=carry)
    return proj, moved


TIME_BLK = 512
N_TBLK = SEQ // TIME_BLK
N_PANEL = CH_S // 128
STATE_SHAPE = (N_PANEL, SEQ * 8, 128)


def _s5_layer_specs(layer):
    mat = lambda: pl.BlockSpec((None, N_GROUP, GROUP_W, STATE), lambda i: (layer, 0, 0, 0))
    ab = lambda: pl.BlockSpec((None, N_GROUP, STATE), lambda i: (layer, 0, 0))
    return [mat(), mat(), mat(), mat(), ab(), ab(), _const((DEPTH, WIDTH))]


def _s5_layer_scratch():
    return [pltpu.VMEM((N_CHUNK, CH_W, CH_S), BF16)] * 4 + [pltpu.VMEM((8, CH_S), F32)] * 2


def _s5_layer_fill(btre_ref, btim_ref, cre_ref, cim_ref, are_ref, aim_ref, bdre, bdim, ctre, ctim, a1, a2):
    for m in (bdre, bdim, ctre, ctim):
        m[...] = jnp.zeros_like(m)
    for grp in range(N_GROUP):
        k, g = divmod(grp, CH_G)
        rows = slice(g * GROUP_W, (g + 1) * GROUP_W)
        cols = slice(g * STATE, (g + 1) * STATE)
        bdre[k, rows, cols] = btre_ref[grp].astype(BF16)
        bdim[k, rows, cols] = btim_ref[grp].astype(BF16)
        ctre[k, rows, cols] = cre_ref[grp].astype(BF16)
        ctim[k, rows, cols] = cim_ref[grp].astype(BF16)
        ar = are_ref[grp:grp + 1, :]
        ai = aim_ref[grp:grp + 1, :]
        a1[k:k + 1, cols] = ar
        a1[N_CHUNK + k:N_CHUNK + k + 1, cols] = ar
        a2[k:k + 1, cols] = -ai
        a2[N_CHUNK + k:N_CHUNK + k + 1, cols] = ai


SCAN_UNROLL = 8


def _panels(tile):
    return [tile[:, p * 128:(p + 1) * 128] for p in range(N_PANEL)]


def _rows_load(ref, row):
    return jnp.concatenate([ref[p, pl.ds(row, TIME_BLK, stride=8), :] for p in range(N_PANEL)], axis=1)


def _rows_store(ref, row, val):
    for p in range(N_PANEL):
        ref[p, pl.ds(row, TIME_BLK, stride=8), :] = val[:, p * 128:(p + 1) * 128]


def _s5_scan_fwd(layer, proj, bbt_re, bbt_im, c_re, c_im, abar_re, abar_im, d_skip, carry=None):
    def body(u_ref, btre_ref, btim_ref, cre_ref, cim_ref, are_ref, aim_ref, d_ref, s_ref, y_ref,
             bdre, bdim, ctre, ctim, a1, a2, state):
        @pl.when(pl.program_id(0) == 0)
        def _():
            _s5_layer_fill(btre_ref, btim_ref, cre_ref, cim_ref, are_ref, aim_ref, bdre, bdim, ctre, ctim, a1, a2)
            state[...] = jnp.zeros_like(state)

        for k in range(N_CHUNK):
            ub = u_ref[:, k * CH_W:(k + 1) * CH_W].astype(BF16)
            _rows_store(s_ref, k, _dot(ub, bdre[k]))
            _rows_store(s_ref, N_CHUNK + k, _dot(ub, bdim[k]))
        m1 = _panels(a1[...])
        m2 = _panels(a2[...])

        def steps(n, tile):
            for r in range(SCAN_UNROLL):
                rows = pl.ds(pl.multiple_of((n * SCAN_UNROLL + r) * 8, 8), 8)
                tile = [m1[p] * tile[p] + m2[p] * pltpu.roll(tile[p], N_CHUNK, 0) + s_ref[p, rows, :]
                        for p in range(N_PANEL)]
                for p in range(N_PANEL):
                    s_ref[p, rows, :] = tile[p]
            return tile

        tile = lax.fori_loop(0, TIME_BLK // SCAN_UNROLL, steps, _panels(state[...]))
        state[...] = jnp.concatenate(tile, axis=1)
        d = d_ref[layer:layer + 1, :]
        for k in range(N_CHUNK):
            cols = slice(k * CH_W, (k + 1) * CH_W)
            y = (_dot_nt(_rows_load(s_ref, k).astype(BF16), ctre[k])
                 - _dot_nt(_rows_load(s_ref, N_CHUNK + k).astype(BF16), ctim[k]))
            y_ref[:, cols] = y + d[:, cols] * u_ref[:, cols]

    return _pcall(
        body, name=f"s5_fwd_l{layer}",
        out_shape=(SDS(STATE_SHAPE, F32), SDS((SEQ, WIDTH), F32)),
        grid=(N_TBLK,),
        in_specs=[pl.BlockSpec((TIME_BLK, WIDTH), lambda i: (i, 0))] + _s5_layer_specs(layer),
        out_specs=(pl.BlockSpec((N_PANEL, TIME_BLK * 8, 128), lambda i: (0, i, 0)),
                   pl.BlockSpec((TIME_BLK, WIDTH), lambda i: (i, 0))),
        scratch_shapes=_s5_layer_scratch() + [pltpu.VMEM((8, CH_S), F32)],
        args=[proj, bbt_re, bbt_im, c_re, c_im, abar_re, abar_im, d_skip], sem=("arbitrary",), carry=carry)


def _s5_scan_bwd(layer, dy0, proj, states, bbt_re, bbt_im, c_re, c_im, abar_re, abar_im, d_skip, dproj,
                 carry=None):
    def body(dy_ref, u_ref, s_ref, sprev_ref, btre_ref, btim_ref, cre_ref, cim_ref, are_ref, aim_ref, d_ref, _,
             du_ref, gbre_ref, gbim_ref, gcre_ref, gcim_ref, gare_ref, gaim_ref, gd_ref,
             lam_ref, bdre, bdim, ctre, ctim, a1, a2, state, acc1, acc2, gbre, gbim, gcre, gcim, gd):
        step_id = pl.program_id(0)

        @pl.when(step_id == 0)
        def _():
            _s5_layer_fill(btre_ref, btim_ref, cre_ref, cim_ref, are_ref, aim_ref, bdre, bdim, ctre, ctim, a1, a2)
            for r in (state, acc1, acc2, gbre, gbim, gcre, gcim, gd):
                r[...] = jnp.zeros_like(r)

        for k in range(N_CHUNK):
            dyb = dy_ref[:, k * CH_W:(k + 1) * CH_W].astype(BF16)
            _rows_store(lam_ref, k, _dot(dyb, ctre[k]))
            _rows_store(lam_ref, N_CHUNK + k, -_dot(dyb, ctim[k]))
            gcre[k] += _dot_tn(dyb, _rows_load(s_ref, k).astype(BF16))
            gcim[k] -= _dot_tn(dyb, _rows_load(s_ref, N_CHUNK + k).astype(BF16))

        m1 = _panels(a1[...])
        m2 = _panels(-a2[...])
        has_before = (step_id < N_TBLK - 1).astype(F32)

        def one(t8, c, first_token):
            tile, swapped, p1, p2 = c
            rows = pl.ds(t8, 8)
            tile = [m1[p] * tile[p] + m2[p] * swapped[p] + lam_ref[p, rows, :] for p in range(N_PANEL)]
            swapped = [pltpu.roll(tile[p], N_CHUNK, 0) for p in range(N_PANEL)]
            for p in range(N_PANEL):
                lam_ref[p, rows, :] = tile[p]
            if first_token:
                before = [sprev_ref[p] * has_before for p in range(N_PANEL)]
            else:
                before = [s_ref[p, pl.ds(t8 - 8, 8), :] for p in range(N_PANEL)]
            p1 = [p1[p] + tile[p] * before[p] for p in range(N_PANEL)]
            p2 = [p2[p] + swapped[p] * before[p] for p in range(N_PANEL)]
            return tile, swapped, p1, p2

        def steps(n, c):
            for r in range(SCAN_UNROLL):
                t8 = pl.multiple_of((TIME_BLK - 1 - (n * SCAN_UNROLL + r)) * 8, 8)
                c = one(t8, c, False)
            return c

        tile0 = _panels(state[...])
        c = (tile0, [pltpu.roll(t, N_CHUNK, 0) for t in tile0], _panels(acc1[...]), _panels(acc2[...]))
        c = lax.fori_loop(0, TIME_BLK // SCAN_UNROLL - 1, steps, c)
        for r in range(SCAN_UNROLL - 1, -1, -1):
            c = one(r * 8, c, r == 0)
        state[...] = jnp.concatenate(c[0], axis=1)
        acc1[...] = jnp.concatenate(c[2], axis=1)
        acc2[...] = jnp.concatenate(c[3], axis=1)

        d = d_ref[layer:layer + 1, :]
        for k in range(N_CHUNK):
            cols = slice(k * CH_W, (k + 1) * CH_W)
            lrb = _rows_load(lam_ref, k).astype(BF16)
            lib = _rows_load(lam_ref, N_CHUNK + k).astype(BF16)
            u = u_ref[:, cols]
            ub = u.astype(BF16)
            dy = dy_ref[:, cols]
            du = dy * d[:, cols] + _dot_nt(lrb, bdre[k]) + _dot_nt(lib, bdim[k])
            du_ref[:, cols] = du.astype(BF16)
            gbre[k] += _dot_tn(ub, lrb)
            gbim[k] += _dot_tn(ub, lib)
        gd[...] += jnp.sum(dy_ref[...] * u_ref[...], axis=0, keepdims=True)

        @pl.when(step_id == N_TBLK - 1)
        def _():
            gd_ref[...] = gd[...]
            ga_re = acc1[0:N_CHUNK, :] + acc1[N_CHUNK:, :]
            ga_im = acc2[0:N_CHUNK, :] - acc2[N_CHUNK:, :]
            for grp in range(N_GROUP):
                k, g = divmod(grp, CH_G)
                rows = slice(g * GROUP_W, (g + 1) * GROUP_W)
                cols = slice(g * STATE, (g + 1) * STATE)
                gcre_ref[grp] = gcre[k, rows, cols]
                gcim_ref[grp] = gcim[k, rows, cols]
                gbre_ref[grp] = gbre[k, rows, cols]
                gbim_ref[grp] = gbim[k, rows, cols]
                gare_ref[grp:grp + 1, :] = ga_re[k:k + 1, cols]
                gaim_ref[grp:grp + 1, :] = ga_im[k:k + 1, cols]

    back = lambda i: N_TBLK - 1 - i
    tok = lambda: pl.BlockSpec((TIME_BLK, WIDTH), lambda i: (back(i), 0))
    mat = lambda: _const((N_GROUP, GROUP_W, STATE))
    acc_mat = pltpu.VMEM((N_CHUNK, CH_W, CH_S), F32)
    return _pcall(
        body, name=f"s5_bwd_l{layer}",
        out_shape=(SDS((SEQ, N_IN), BF16), SDS((N_GROUP, GROUP_W, STATE), F32), SDS((N_GROUP, GROUP_W, STATE), F32),
                   SDS((N_GROUP, GROUP_W, STATE), F32), SDS((N_GROUP, GROUP_W, STATE), F32),
                   SDS((N_GROUP, STATE), F32), SDS((N_GROUP, STATE), F32), SDS((1, WIDTH), F32)),
        grid=(N_TBLK,),
        in_specs=[tok(), tok(),
                  pl.BlockSpec((N_PANEL, TIME_BLK * 8, 128), lambda i: (0, back(i), 0)),
                  pl.BlockSpec((N_PANEL, 8, 128), lambda i: (0, jnp.maximum(back(i) * TIME_BLK - 1, 0), 0))]
        + _s5_layer_specs(layer) + [ANY],
        out_specs=(tok(), mat(), mat(), mat(), mat(), _const((N_GROUP, STATE)), _const((N_GROUP, STATE)),
                   _const((1, WIDTH))),
        scratch_shapes=[pltpu.VMEM((N_PANEL, TIME_BLK * 8, 128), F32)] + _s5_layer_scratch()
        + [pltpu.VMEM((8, CH_S), F32)] * 3 + [acc_mat] * 4 + [pltpu.VMEM((1, WIDTH), F32)],
        args=[dy0, proj, states, states, bbt_re, bbt_im, c_re, c_im, abar_re, abar_im, d_skip, dproj],
        aliases={11: 0}, sem=("arbitrary",), limit=VMEM_LIMIT_BIG, carry=carry)


def _pool_counts(win):
    t = lax.broadcasted_iota(jnp.int32, (SEQ, POOL_GROUP), 0)
    return t, jnp.minimum(t + 1, win).astype(F32)


def _pool_fwd(layer, proj):
    def body(u_ref, o_ref):
        for gi, win in enumerate(POOL_WINDOWS):
            cols = slice(gi * POOL_GROUP, (gi + 1) * POOL_GROUP)
            u = u_ref[:, cols]
            t, count = _pool_counts(win)
            acc = u
            k = 1
            while k < win:
                acc = acc + jnp.where(t >= k, pltpu.roll(acc, k, 0), 0.0)
                k *= 2
            o_ref[:, cols] = acc / count - u

    return pl.pallas_call(
        body, name=f"pool_fwd_l{layer}",
        out_shape=SDS((SEQ, WIDTH), F32),
        grid=(1,),
        in_specs=[pl.BlockSpec((SEQ, WIDTH), lambda i: (0, 2))],
        out_specs=pl.BlockSpec((SEQ, WIDTH), lambda i: (0, 0)),
        compiler_params=_cp(("arbitrary",)),
    )(proj)


def _gelu_parts(y0):
    t = jnp.tanh(GELU_C * (y0 + GELU_A * (y0 * y0 * y0)))
    return t, 0.5 * y0 * (1.0 + t)


def _mix_forward(layer, p_ref, y0_ref, pooled_ref, wglu_ref, bglu_ref, pw_ref, scale_ref, wa_ref, wb_ref):
    za = p_ref[:, WIDTH:2 * WIDTH]
    zb = p_ref[:, 3 * WIDTH:4 * WIDTH]
    ga = p_ref[:, 4 * WIDTH:4 * WIDTH + D_MODEL]
    gb = p_ref[:, 4 * WIDTH + D_MODEL:]
    y0 = y0_ref[...]
    t, y1 = _gelu_parts(y0)
    y1b = y1.astype(BF16)
    q = _dot(y1b, wglu_ref[...].reshape(WIDTH, WIDTH)) + bglu_ref[layer:layer + 1, :]
    sq = _sig(q)
    y2 = y1 * sq
    sza = _sig(za)
    silu_za = za * sza
    ya = y2 * silu_za
    pooled = pooled_ref[...]
    mixed = jnp.concatenate(
        [_dot(pooled[:, g * POOL_GROUP:(g + 1) * POOL_GROUP].astype(BF16), pw_ref[g].astype(BF16))
         for g in range(len(POOL_WINDOWS))], axis=1)
    szb = _sig(zb)
    silu_zb = zb * szb
    scale = scale_ref[layer:layer + 1, :]
    ms = mixed * scale
    yb = ms * silu_zb
    yab = ya.astype(BF16)
    ybb = yb.astype(BF16)
    ma = jnp.concatenate([_dot(yab, wa_ref[k]) for k in range(N_DEV)], axis=1)
    mb = jnp.concatenate([_dot(ybb, wb_ref[k]) for k in range(N_DEV)], axis=1)
    sga = _sig(ga)
    sgb = _sig(gb)
    merged = sga * ma + sgb * mb
    return dict(za=za, zb=zb, y0=y0, t=t, y1=y1, y1b=y1b, sq=sq, y2=y2, sza=sza, silu_za=silu_za,
                pooled=pooled, mixed=mixed, szb=szb, silu_zb=silu_zb, scale=scale, ms=ms, yab=yab, ybb=ybb,
                ma=ma, mb=mb, sga=sga, sgb=sgb, merged=merged)


def _mix_weight_specs(layer):
    return [_const((N_DEV, WIDTH // N_DEV, WIDTH)),
            _const((DEPTH, WIDTH)),
            pl.BlockSpec((None, 4, POOL_GROUP, POOL_GROUP), lambda i: (layer, 0, 0, 0)),
            _const((DEPTH, WIDTH)),
            _const((N_DEV, WIDTH, D_MODEL // N_DEV)),
            _const((N_DEV, WIDTH, D_MODEL // N_DEV)),
            _const((N_DEV, D_MODEL // N_DEV, D_MODEL))]


def _mix_fwd(layer, x, proj, y0, pooled, wg_glu, b_glu, pool_w, pool_scale, wg_a, wg_b, wg_out, carry=None):
    def body(x_ref, p_ref, y0_ref, pooled_ref, wglu_ref, bglu_ref, pw_ref, scale_ref, wa_ref, wb_ref,
             wout_ref, o_ref):
        f = _mix_forward(layer, p_ref, y0_ref, pooled_ref, wglu_ref, bglu_ref, pw_ref, scale_ref, wa_ref, wb_ref)
        wout = wout_ref[...].reshape(D_MODEL, D_MODEL)
        o_ref[...] = x_ref[...] + _dot(f["merged"].astype(BF16), wout)

    (x_next,), moved = _pcall(
        body, name=f"mix_fwd_l{layer}",
        out_shape=[SDS((SEQ, D_MODEL), F32)],
        grid=(SEQ // TILE_M,),
        in_specs=[pl.BlockSpec((TILE_M, D_MODEL), lambda i: (i, 0)),
                  pl.BlockSpec((TILE_M, N_IN), lambda i: (i, 0)),
                  pl.BlockSpec((TILE_M, WIDTH), lambda i: (i, 0)),
                  pl.BlockSpec((TILE_M, WIDTH), lambda i: (i, 0))] + _mix_weight_specs(layer),
        out_specs=[pl.BlockSpec((TILE_M, D_MODEL), lambda i: (i, 0))],
        args=[x, proj, y0, pooled, wg_glu, b_glu, pool_w, pool_scale, wg_a, wg_b, wg_out],
        sem=("parallel",), carry=carry)
    return x_next, moved


def _loss_head(x, target, final_g):
    def body(x_ref, t_ref, g_ref, dx_ref, loss_ref, gg_ref):
        @pl.when(pl.program_id(0) == 0)
        def _():
            loss_ref[...] = jnp.zeros_like(loss_ref)
            gg_ref[...] = jnp.zeros_like(gg_ref)

        g = g_ref[...]
        rs, xn = _rms(x_ref[...])
        err = xn * g - t_ref[...]
        loss_ref[...] += 0.5 * jnp.sum(jnp.mean(err * err, axis=-1, keepdims=True), axis=0, keepdims=True)
        dy = err * (1.0 / D_MODEL)
        gg_ref[...] += jnp.sum(dy * xn, axis=0, keepdims=True)
        dxn = dy * g
        dx_ref[...] = rs * (dxn - xn * jnp.mean(dxn * xn, axis=-1, keepdims=True))

    return pl.pallas_call(
        body, name="loss_head",
        out_shape=(SDS((SEQ, D_MODEL), F32), SDS((1, 1), F32), SDS((1, D_MODEL), F32)),
        grid=(SEQ // TILE_M,),
        in_specs=[pl.BlockSpec((TILE_M, D_MODEL), lambda i: (i, 0)),
                  pl.BlockSpec((TILE_M, D_MODEL), lambda i: (i, 0)),
                  _const((1, D_MODEL))],
        out_specs=(pl.BlockSpec((TILE_M, D_MODEL), lambda i: (i, 0)), _const((1, 1)), _const((1, D_MODEL))),
        compiler_params=_cp(("arbitrary",)),
    )(x, target, final_g)


def _big_shapes():
    return dict(w_out=(DEPTH, N_DEV, D_MODEL // N_DEV, D_MODEL), w_branch_a=(DEPTH, N_DEV, WIDTH, D_MODEL // N_DEV),
                w_branch_b=(DEPTH, N_DEV, WIDTH, D_MODEL // N_DEV), ssm_w_glu=(DEPTH, N_DEV, WIDTH // N_DEV, WIDTH),
                w_in=(DEPTH, N_DEV, D_MODEL, WIDTH))


def _mix_bwd(layer, dx_next, proj, y0, pooled, wg_glu, b_glu, pool_w, pool_scale, wg_a, wg_b, wg_out, prev,
             carry=None):
    n_k = N_DEV
    n_prev = 0 if prev is None else len(prev)

    def body(*refs):
        (dx_ref, p_ref, y0_ref, pooled_ref, wglu_ref, bglu_ref, pw_ref, scale_ref, wa_ref, wb_ref,
         wout_ref) = refs[:11]
        (dproj_ref, dy0_ref, dpooled_ref, gwout_ref, gwa_ref, gwb_ref, gwglu_ref, gpw_ref,
         gscale_ref, gbglu_ref) = refs[11 + n_prev:]

        @pl.when(pl.program_id(0) == 0)
        def _():
            for r in (gwout_ref, gwa_ref, gwb_ref, gwglu_ref, gpw_ref, gscale_ref, gbglu_ref):
                r[...] = jnp.zeros_like(r)

        f = _mix_forward(layer, p_ref, y0_ref, pooled_ref, wglu_ref, bglu_ref, pw_ref, scale_ref, wa_ref, wb_ref)
        wglu = wglu_ref[...].reshape(WIDTH, WIDTH)
        wout = wout_ref[...].reshape(D_MODEL, D_MODEL)
        blk = D_MODEL // n_k
        dxb = dx_ref[...].astype(BF16)
        dmerged = _dot_nt(dxb, wout)
        gwout = _dot_tn(f["merged"].astype(BF16), dxb)
        for k in range(n_k):
            gwout_ref[_slot(k)] += gwout[k * blk:(k + 1) * blk, :]
        dma = dmerged * f["sga"]
        dmb = dmerged * f["sgb"]
        dga = dmerged * f["ma"] * f["sga"] * (1.0 - f["sga"])
        dgb = dmerged * f["mb"] * f["sgb"] * (1.0 - f["sgb"])
        dmab = dma.astype(BF16)
        dmbb = dmb.astype(BF16)
        dya = jnp.zeros((TILE_M, WIDTH), F32)
        dyb = jnp.zeros((TILE_M, WIDTH), F32)
        for k in range(n_k):
            da_k = dmab[:, k * blk:(k + 1) * blk]
            db_k = dmbb[:, k * blk:(k + 1) * blk]
            dya = dya + _dot_nt(da_k, wa_ref[k])
            dyb = dyb + _dot_nt(db_k, wb_ref[k])
            gwa_ref[_slot(k)] += _dot_tn(f["yab"], da_k)
            gwb_ref[_slot(k)] += _dot_tn(f["ybb"], db_k)
        zb, szb = f["zb"], f["szb"]
        dzb = dyb * f["ms"] * (szb * (1.0 + zb * (1.0 - szb)))
        dms = dyb * f["silu_zb"]
        gscale_ref[...] += jnp.sum(dms * f["mixed"], axis=0, keepdims=True)
        dmixed = (dms * f["scale"]).astype(BF16)
        pooled = f["pooled"]
        for g in range(len(POOL_WINDOWS)):
            cols = slice(g * POOL_GROUP, (g + 1) * POOL_GROUP)
            dpooled_ref[:, cols] = _dot_nt(dmixed[:, cols], pw_ref[g].astype(BF16))
            gpw_ref[g] += _dot_tn(pooled[:, cols].astype(BF16), dmixed[:, cols])
        za, sza = f["za"], f["sza"]
        dza = dya * f["y2"] * (sza * (1.0 + za * (1.0 - sza)))
        dy2 = dya * f["silu_za"]
        sq = f["sq"]
        dq = dy2 * f["y1"] * sq * (1.0 - sq)
        dqb = dq.astype(BF16)
        dy1 = dy2 * sq + _dot_nt(dqb, wglu)
        gwglu = _dot_tn(f["y1b"], dqb)
        rblk = WIDTH // n_k
        for k in range(n_k):
            gwglu_ref[_slot(k)] += gwglu[k * rblk:(k + 1) * rblk, :]
        gbglu_ref[...] += jnp.sum(dq, axis=0, keepdims=True)
        y0, t = f["y0"], f["t"]
        dgelu = 0.5 * (1.0 + t) + 0.5 * y0 * (1.0 - t * t) * (GELU_C * (1.0 + 3.0 * GELU_A * y0 * y0))
        dy0_ref[...] = dy1 * dgelu
        zeros = jnp.zeros((TILE_M, WIDTH), BF16)
        dproj_ref[:, 0:WIDTH] = zeros
        dproj_ref[:, WIDTH:2 * WIDTH] = dza.astype(BF16)
        dproj_ref[:, 2 * WIDTH:3 * WIDTH] = zeros
        dproj_ref[:, 3 * WIDTH:4 * WIDTH] = dzb.astype(BF16)
        dproj_ref[:, 4 * WIDTH:4 * WIDTH + D_MODEL] = dga.astype(BF16)
        dproj_ref[:, 4 * WIDTH + D_MODEL:] = dgb.astype(BF16)

    tile = lambda w: pl.BlockSpec((TILE_M, w), lambda i: (i, 0))
    shapes = _big_shapes()
    big = ["w_out", "w_branch_a", "w_branch_b", "ssm_w_glu"]
    slab = lambda n: pl.BlockSpec((None,) + shapes[n][1:], lambda i: (layer, 0, 0, 0))
    args = [dx_next, proj, y0, pooled, wg_glu, b_glu, pool_w, pool_scale, wg_a, wg_b, wg_out]
    return _pcall(
        body, name=f"mix_bwd_l{layer}",
        out_shape=(SDS((SEQ, N_IN), BF16), SDS((SEQ, WIDTH), F32), SDS((SEQ, WIDTH), F32))
        + tuple(SDS(shapes[n], F32) for n in big)
        + (SDS((4, POOL_GROUP, POOL_GROUP), F32), SDS((1, WIDTH), F32), SDS((1, WIDTH), F32)),
        grid=(SEQ // TILE_M,),
        in_specs=[tile(D_MODEL), tile(N_IN), tile(WIDTH), tile(WIDTH)] + _mix_weight_specs(layer) + [ANY] * n_prev,
        out_specs=(tile(N_IN), tile(WIDTH), tile(WIDTH)) + tuple(slab(n) for n in big)
        + (_const((4, POOL_GROUP, POOL_GROUP)), _const((1, WIDTH)), _const((1, WIDTH))),
        args=args + list(prev or ()),
        aliases={len(args) + i: 3 + i for i in range(n_prev)},
        sem=("arbitrary",), limit=VMEM_LIMIT_BIG, carry=carry)


def _pool_bwd(layer, dpooled, dproj):
    def body(dp_ref, _, o_ref):
        for gi, win in enumerate(POOL_WINDOWS):
            cols = slice(gi * POOL_GROUP, (gi + 1) * POOL_GROUP)
            dp = dp_ref[:, cols]
            t, count = _pool_counts(win)
            e = dp / count
            acc = e
            k = 1
            while k < win:
                acc = acc + jnp.where(t < SEQ - k, pltpu.roll(acc, SEQ - k, 0), 0.0)
                k *= 2
            o_ref[:, cols] = (acc - dp).astype(BF16)

    return pl.pallas_call(
        body, name=f"pool_bwd_l{layer}",
        out_shape=SDS((SEQ, N_IN), BF16),
        grid=(1,),
        in_specs=[pl.BlockSpec((SEQ, WIDTH), lambda i: (0, 0)), ANY],
        out_specs=pl.BlockSpec((SEQ, WIDTH), lambda i: (0, 2)),
        input_output_aliases={1: 0},
        compiler_params=_cp(("arbitrary",)),
    )(dpooled, dproj)


def _proj_wgrad(layer, x, norm_g, dproj, prev):
    tm = 512
    n_prev = 0 if prev is None else 1

    def body(*refs):
        x_ref, g_ref, dp_ref = refs[:3]
        gw_ref, gb_ref = refs[3 + n_prev:]

        @pl.when(pl.program_id(1) == 0)
        def _():
            gw_ref[...] = jnp.zeros_like(gw_ref)
            gb_ref[...] = jnp.zeros_like(gb_ref)

        _, xn = _rms(x_ref[...])
        h = (xn * g_ref[layer:layer + 1, :]).astype(BF16)
        dp = dp_ref[...]
        gw_ref[...] += _dot_tn(h, dp)
        gb_ref[...] += jnp.sum(dp.astype(F32), axis=0, keepdims=True)

    return pl.pallas_call(
        body, name=f"proj_wgrad_l{layer}",
        out_shape=(SDS(_big_shapes()["w_in"], F32), SDS((1, N_IN), F32)),
        grid=(N_DEV, SEQ // tm),
        in_specs=[pl.BlockSpec((tm, D_MODEL), lambda n, t: (t, 0)),
                  _const((DEPTH, D_MODEL)),
                  pl.BlockSpec((tm, WIDTH), lambda n, t: (t, n))] + [ANY] * n_prev,
        out_specs=(pl.BlockSpec((None, None, D_MODEL, WIDTH), lambda n, t: (layer, _slot(n), 0, 0)),
                   pl.BlockSpec((1, WIDTH), lambda n, t: (0, n))),
        input_output_aliases={3: 0} if n_prev else {},
        compiler_params=_cp(("parallel", "arbitrary")),
    )(x, norm_g, dproj, *([prev] if n_prev else []))


def _proj_dgrad(layer, dx_next, x, norm_g, dproj, wg_in, carry=None):
    def body(dxn_ref, x_ref, g_ref, dp_ref, w_ref, dx_ref, gg_ref):
        @pl.when(pl.program_id(0) == 0)
        def _():
            gg_ref[...] = jnp.zeros_like(gg_ref)

        dh = jnp.zeros((TILE_M, D_MODEL), F32)
        for k in range(N_DEV):
            dh = dh + _dot_nt(dp_ref[:, k * WIDTH:(k + 1) * WIDTH], w_ref[k])
        rs, xn = _rms(x_ref[...])
        gg_ref[...] += jnp.sum(dh * xn, axis=0, keepdims=True)
        dxn = dh * g_ref[layer:layer + 1, :]
        dx_ref[...] = dxn_ref[...] + rs * (dxn - xn * jnp.mean(dxn * xn, axis=-1, keepdims=True))

    return _pcall(
        body, name=f"proj_dgrad_l{layer}",
        out_shape=(SDS((SEQ, D_MODEL), F32), SDS((1, D_MODEL), F32)),
        grid=(SEQ // TILE_M,),
        in_specs=[pl.BlockSpec((TILE_M, D_MODEL), lambda i: (i, 0)),
                  pl.BlockSpec((TILE_M, D_MODEL), lambda i: (i, 0)),
                  _const((DEPTH, D_MODEL)),
                  pl.BlockSpec((TILE_M, N_IN), lambda i: (i, 0)),
                  _const((N_DEV, D_MODEL, WIDTH))],
        out_specs=(pl.BlockSpec((TILE_M, D_MODEL), lambda i: (i, 0)), _const((1, D_MODEL))),
        args=[dx_next, x, norm_g, dproj, wg_in], sem=("arbitrary",), carry=carry)


def _my_place():
    return lax.axis_index("x"), lax.axis_index("y"), lax.axis_index("c")


def _gather_plan(shards, layer):
    n = len(shards)

    def parts(ins, outs, sems):
        send_sems, recv_sems, local_sems = sems
        x, y, c = _my_place()
        chips = [(1 - x, y), (x, 1 - y), (1 - x, 1 - y)]

        def rows(t, place):
            px, py, pc = place
            return outs[t].at[pl.ds(4 * px + 2 * py + pc, 1)]

        def copy(t, k, block, to, from_src=False):
            return pltpu.make_async_remote_copy(
                src_ref=ins[t].at[pl.ds(layer, 1)] if from_src else rows(t, block), dst_ref=rows(t, block),
                send_sem=send_sems.at[7 * t + k], recv_sem=recv_sems.at[7 * t + k], device_id=to,
                device_id_type=MESH)

        def mine(t):
            return pltpu.make_async_copy(ins[t].at[pl.ds(layer, 1)], rows(t, (x, y, c)), local_sems.at[t])

        return (x, y, c), chips, copy, mine

    def start(ins, outs, sems):
        me, chips, copy, mine = parts(ins, outs, sems)
        x, y, c = me
        for t in range(n):
            mine(t).start()
            copy(t, 0, me, (x, y, 1 - c), from_src=True).start()
            for j, chip in enumerate(chips):
                copy(t, 1 + j, me, (*chip, c), from_src=True).start()

    def finish(ins, outs, sems):
        me, chips, copy, mine = parts(ins, outs, sems)
        x, y, c = me
        sibling = (x, y, 1 - c)
        for t in range(n):
            for j, chip in enumerate(chips):
                copy(t, 1 + j, (*chip, c), me).wait_recv()
                copy(t, 4 + j, (*chip, c), sibling).start()
        for t in range(n):
            copy(t, 0, sibling, me).wait_recv()
            for j, chip in enumerate(chips):
                copy(t, 4 + j, (*chip, 1 - c), me).wait_recv()
            for k in range(7):
                copy(t, k, me, sibling, from_src=k < 4).wait_send()
            mine(t).wait()

    out_shape = [SDS((N_DEV,) + a.shape[1:], a.dtype) for a in shards]
    sems = [pltpu.SemaphoreType.DMA((7 * n,)), pltpu.SemaphoreType.DMA((7 * n,)), pltpu.SemaphoreType.DMA((n,))]
    return _Carried(shards, out_shape, sems, start, finish)


class _Carried:
    def __init__(self, ins, out_shape, sems, start, finish):
        self.ins, self.out_shape, self.sems = list(ins), list(out_shape), list(sems)
        self.start, self.finish = start, finish


def _pcall(body, *, name, grid, in_specs, out_specs, out_shape, args, scratch_shapes=(), aliases=None,
           sem=None, limit=VMEM_LIMIT, carry=None):
    out_shape, out_specs, scratch_shapes = list(out_shape), list(out_specs), list(scratch_shapes)
    n_in, n_out, n_scr = len(args), len(out_shape), len(scratch_shapes)
    if carry is None:
        kern, c_ins, c_out, c_sems = body, [], [], []
    else:
        c_ins, c_out, c_sems = carry.ins, carry.out_shape, carry.sems
        ci, co = len(c_ins), len(c_out)
        steps = tuple(grid)

        def kern(*refs):
            o0 = n_in + ci
            s0 = o0 + n_out + co
            mine = refs[:n_in] + refs[o0:o0 + n_out] + refs[s0:s0 + n_scr]
            theirs = (refs[n_in:o0], refs[o0 + n_out:s0], refs[s0 + n_scr:])
            first = pl.program_id(0) == 0
            last = pl.program_id(0) == steps[0] - 1
            for a in range(1, len(steps)):
                first = jnp.logical_and(first, pl.program_id(a) == 0)
                last = jnp.logical_and(last, pl.program_id(a) == steps[a] - 1)

            @pl.when(first)
            def _():
                carry.start(*theirs)

            body(*mine)

            @pl.when(last)
            def _():
                carry.finish(*theirs)

        sem = ("arbitrary",) * len(steps)
    res = pl.pallas_call(
        kern, name=name, grid=tuple(grid),
        in_specs=list(in_specs) + [ANY] * len(c_ins),
        out_specs=tuple(out_specs + [ANY] * len(c_out)),
        out_shape=tuple(out_shape + c_out),
        scratch_shapes=scratch_shapes + c_sems,
        input_output_aliases=aliases or {},
        compiler_params=_cp(sem, limit),
    )(*args, *c_ins)
    return res[:n_out], res[n_out:]


def _run_carried(name, carry):
    ci, co = len(carry.ins), len(carry.out_shape)

    def body(*refs):
        parts = (refs[:ci], refs[ci:ci + co], refs[ci + co:])
        carry.start(*parts)
        carry.finish(*parts)

    return pl.pallas_call(
        body, name=name, out_shape=tuple(carry.out_shape),
        in_specs=[ANY] * ci, out_specs=tuple([ANY] * co), scratch_shapes=carry.sems,
    )(*carry.ins)


def _sibling_plan(big, small):
    n = len(big)
    n_copies = 4 * n + len(small)

    def copies(ins, outs, sems):
        send_sems, recv_sems = sems
        x, y, c = _my_place()
        pairs = []
        for t, (_, layer) in enumerate(big):
            for s in range(4):
                pairs.append((ins[t].at[layer, pl.ds(4 * (1 - c) + s, 1)], outs[t].at[pl.ds(s, 1)]))
        pairs += list(zip(ins[n:], outs[n:]))
        return [pltpu.make_async_remote_copy(
            src_ref=src, dst_ref=dst, send_sem=send_sems.at[k], recv_sem=recv_sems.at[k],
            device_id=(x, y, 1 - c), device_id_type=MESH) for k, (src, dst) in enumerate(pairs)]

    def start(ins, outs, sems):
        for cp in copies(ins, outs, sems):
            cp.start()

    def finish(ins, outs, sems):
        for cp in copies(ins, outs, sems):
            cp.wait()

    out_shape = [SDS((4,) + a.shape[2:], a.dtype) for a, _ in big] + [SDS(a.shape, a.dtype) for a in small]
    sems = [pltpu.SemaphoreType.DMA((n_copies,)), pltpu.SemaphoreType.DMA((n_copies,))]
    return _Carried([a for a, _ in big] + list(small), out_shape, sems, start, finish)


def _chips_plan(big, small):
    n, n_small = len(big), len(small)
    max_rows = 512
    parts = [max(1, a.shape[1] // max_rows) for a in big]
    n_copies = 3 * (sum(parts) + n_small)

    def copies(ins, outs, sems, landing):
        send_sems, recv_sems, local_sems = sems
        x, y, c = _my_place()
        my_chip = 2 * x + y
        chips = [(1 - x, y), (x, 1 - y), (1 - x, 1 - y)]
        remote, local = [], []
        for chip in chips:
            to = 2 * chip[0] + chip[1]
            slot = to if landing else my_chip
            pairs = []
            for t in range(n):
                rows_per = big[t].shape[1] // parts[t]
                for p in range(parts[t]):
                    rows = pl.ds(p * rows_per, rows_per)
                    pairs.append((ins[t].at[to, rows], outs[t].at[slot, rows]))
            pairs += [(ins[t], outs[t].at[slot]) for t in range(n, n + n_small)]
            for src, dst in pairs:
                k = len(remote)
                remote.append(pltpu.make_async_remote_copy(
                    src_ref=src, dst_ref=dst, send_sem=send_sems.at[k], recv_sem=recv_sems.at[k],
                    device_id=(*chip, c), device_id_type=MESH))
        for t in range(n):
            local.append(pltpu.make_async_copy(ins[t].at[my_chip], outs[t].at[my_chip], local_sems.at[t]))
        for t in range(n, n + n_small):
            local.append(pltpu.make_async_copy(ins[t], outs[t].at[my_chip], local_sems.at[t]))
        return remote + local

    def start(ins, outs, sems):
        for cp in copies(ins, outs, sems, landing=False):
            cp.start()

    def finish(ins, outs, sems):
        for cp in copies(ins, outs, sems, landing=True):
            cp.wait()

    out_shape = [SDS(a.shape, a.dtype) for a in big] + [SDS((N_CHIP,) + a.shape, a.dtype) for a in small]
    sems = [pltpu.SemaphoreType.DMA((n_copies,)), pltpu.SemaphoreType.DMA((n_copies,)),
            pltpu.SemaphoreType.DMA((n + n_small,))]
    return _Carried(list(big) + list(small), out_shape, sems, start, finish)


def _row_block(rows):
    return rows if rows <= 256 else 256


def _add_own(tag, core, g, layer, got):
    _, r, c = got.shape
    rb = _row_block(r)

    def body(core_ref, a_ref, b_ref, o_ref):
        o_ref[...] = (a_ref[...] + b_ref[...]).astype(o_ref.dtype)

    return pl.pallas_call(
        body, name=f"add_{tag}", out_shape=SDS(got.shape, BF16),
        grid_spec=pltpu.PrefetchScalarGridSpec(
            num_scalar_prefetch=1, grid=(4, r // rb),
            in_specs=[pl.BlockSpec((None, None, rb, c), lambda s, j, core: (layer, 4 * core[0] + s, j, 0)),
                      pl.BlockSpec((None, rb, c), lambda s, j, core: (s, j, 0))],
            out_specs=pl.BlockSpec((None, rb, c), lambda s, j, core: (s, j, 0))),
        compiler_params=_cp(("parallel", "parallel")),
    )(core, g, got)


def _add_lists(tag, own, got, grid=None, specs=None):
    n = len(own)

    def body(*refs):
        for a, b, o in zip(refs[:n], refs[n:2 * n], refs[2 * n:]):
            o[...] = a[...] + b[...]

    kw = {}
    if grid is not None:
        kw = dict(grid=grid, in_specs=list(specs) * 2, out_specs=tuple(specs),
                  compiler_params=_cp(("parallel",) * len(grid)))
    return pl.pallas_call(
        body, name=f"add_{tag}", out_shape=tuple(SDS(a.shape, a.dtype) for a in own), **kw)(*own, *got)


def _adamw_math(w, g, m, v):
    m = ADAM_B1 * m + (1.0 - ADAM_B1) * g
    v = ADAM_B2 * v + (1.0 - ADAM_B2) * (g * g)
    m_hat = m / (1.0 - ADAM_B1 ** ADAM_STEP)
    v_hat = v / (1.0 - ADAM_B2 ** ADAM_STEP)
    delta = -ADAM_LR * (m_hat / (jnp.sqrt(v_hat) + ADAM_EPS) + ADAM_WD * w)
    return delta, m, v


def _sum_slots_adamw(tag, slots, w, m, v):
    _, r, c = slots[0].shape
    rb = _row_block(r)

    def body(s0_ref, s1_ref, w_ref, m_ref, v_ref, g_ref, d_ref, nm_ref, nv_ref):
        first = pl.program_id(1) == 0
        g = _sum4([jnp.where(first, s0_ref[k], s1_ref[k]).astype(F32) for k in range(N_CHIP)])
        delta, nm, nv = _adamw_math(w_ref[...], g, m_ref[...], v_ref[...])
        g_ref[...] = g
        d_ref[...] = delta
        nm_ref[...] = nm
        nv_ref[...] = nv

    spec = pl.BlockSpec((None, rb, c), lambda j, l: (l, j, 0))
    sspec = pl.BlockSpec((N_CHIP, rb, c), lambda j, l: (0, j, 0))
    s = SDS((DEPTH, r, c), F32)
    return pl.pallas_call(
        body, name=f"adamw_{tag}", out_shape=(s, s, s, s),
        grid=(r // rb, DEPTH), in_specs=[sspec, sspec, spec, spec, spec], out_specs=(spec, spec, spec, spec),
        compiler_params=_cp(("parallel", "arbitrary")),
    )(*slots, w, m, v)


def _adamw_small(tag, entries, grid=None):
    flat_in, in_specs, out_shape, out_specs, layout = [], [], [], [], []
    for slots, w, m, v, slot_spec, w_spec in entries:
        per_layer = isinstance(slots, (list, tuple))
        n_slot = len(slots) if per_layer else 1
        flat_in += (list(slots) if per_layer else [slots]) + [w, m, v]
        in_specs += [slot_spec] * n_slot + [w_spec] * 3
        out_shape += [SDS(w.shape, F32)] * 4
        out_specs += [w_spec] * 4
        layout.append((per_layer, n_slot))
    n_in = len(flat_in)

    def body(*refs):
        i, o = 0, n_in
        for per_layer, n_slot in layout:
            s_refs = refs[i:i + n_slot]
            w_ref, m_ref, v_ref = refs[i + n_slot:i + n_slot + 3]
            outs = refs[o:o + 4]
            if per_layer:
                for l, s_ref in enumerate(s_refs):
                    at = (slice(l, l + 1),) if len(w_ref.shape) == 2 else (l,)
                    g = _sum4([s_ref[k] for k in range(N_CHIP)])
                    res = (g,) + _adamw_math(w_ref[at], g, m_ref[at], v_ref[at])
                    for o_ref, val in zip(outs, res):
                        o_ref[at] = val
            else:
                g = _sum4([s_refs[0][k] for k in range(N_CHIP)])
                res = (g,) + _adamw_math(w_ref[...], g, m_ref[...], v_ref[...])
                for o_ref, val in zip(outs, res):
                    o_ref[...] = val
            i += n_slot + 3
            o += 4

    kw = {}
    if grid is not None:
        kw = dict(grid=grid, in_specs=in_specs, out_specs=tuple(out_specs),
                  compiler_params=_cp(("parallel",) * len(grid)))
    res = pl.pallas_call(body, name=f"adamw_{tag}", out_shape=tuple(out_shape), **kw)(*flat_in)
    return [tuple(res[4 * e:4 * e + 4]) for e in range(len(entries))]


def kernel(x, norm_g, w_in, b_in, ssm_log_dt, ssm_lam_re, ssm_lam_im, ssm_b_re, ssm_b_im, ssm_c_re, ssm_c_im, ssm_d, ssm_w_glu, ssm_b_glu, pool_w, pool_scale, w_branch_a, w_branch_b, w_out, final_norm_g, loss_target, m_norm_g, m_w_in, m_b_in, m_ssm_log_dt, m_ssm_lam_re, m_ssm_lam_im, m_ssm_b_re, m_ssm_b_im, m_ssm_c_re, m_ssm_c_im, m_ssm_d, m_ssm_w_glu, m_ssm_b_glu, m_pool_w, m_pool_scale, m_w_branch_a, m_w_branch_b, m_w_out, m_final_norm_g, v_norm_g, v_w_in, v_b_in, v_ssm_log_dt, v_ssm_lam_re, v_ssm_lam_im, v_ssm_b_re, v_ssm_b_im, v_ssm_c_re, v_ssm_c_im, v_ssm_d, v_ssm_w_glu, v_ssm_b_glu, v_pool_w, v_pool_scale, v_w_branch_a, v_w_branch_b, v_w_out, v_final_norm_g):
    weights = dict(norm_g=norm_g, w_in=w_in, b_in=b_in, ssm_log_dt=ssm_log_dt, ssm_lam_re=ssm_lam_re,
                   ssm_lam_im=ssm_lam_im, ssm_b_re=ssm_b_re, ssm_b_im=ssm_b_im, ssm_c_re=ssm_c_re,
                   ssm_c_im=ssm_c_im, ssm_d=ssm_d, ssm_w_glu=ssm_w_glu, ssm_b_glu=ssm_b_glu, pool_w=pool_w,
                   pool_scale=pool_scale, w_branch_a=w_branch_a, w_branch_b=w_branch_b, w_out=w_out,
                   final_norm_g=final_norm_g.reshape(1, D_MODEL))
    mom_m = dict(norm_g=m_norm_g, w_in=m_w_in, b_in=m_b_in, ssm_log_dt=m_ssm_log_dt, ssm_lam_re=m_ssm_lam_re,
                 ssm_lam_im=m_ssm_lam_im, ssm_b_re=m_ssm_b_re, ssm_b_im=m_ssm_b_im, ssm_c_re=m_ssm_c_re,
                 ssm_c_im=m_ssm_c_im, ssm_d=m_ssm_d, ssm_w_glu=m_ssm_w_glu, ssm_b_glu=m_ssm_b_glu,
                 pool_w=m_pool_w, pool_scale=m_pool_scale, w_branch_a=m_w_branch_a, w_branch_b=m_w_branch_b,
                 w_out=m_w_out, final_norm_g=m_final_norm_g.reshape(1, D_MODEL))
    mom_v = dict(norm_g=v_norm_g, w_in=v_w_in, b_in=v_b_in, ssm_log_dt=v_ssm_log_dt, ssm_lam_re=v_ssm_lam_re,
                 ssm_lam_im=v_ssm_lam_im, ssm_b_re=v_ssm_b_re, ssm_b_im=v_ssm_b_im, ssm_c_re=v_ssm_c_re,
                 ssm_c_im=v_ssm_c_im, ssm_d=v_ssm_d, ssm_w_glu=v_ssm_w_glu, ssm_b_glu=v_ssm_b_glu,
                 pool_w=v_pool_w, pool_scale=v_pool_scale, w_branch_a=v_w_branch_a, w_branch_b=v_w_branch_b,
                 w_out=v_w_out, final_norm_g=v_final_norm_g.reshape(1, D_MODEL))
    order = ["norm_g", "w_in", "b_in", "ssm_log_dt", "ssm_lam_re", "ssm_lam_im", "ssm_b_re", "ssm_b_im",
             "ssm_c_re", "ssm_c_im", "ssm_d", "ssm_w_glu", "ssm_b_glu", "pool_w", "pool_scale", "w_branch_a",
             "w_branch_b", "w_out", "final_norm_g"]
    big_names = ["w_in", "ssm_w_glu", "w_branch_a", "w_branch_b", "w_out"]

    log_dt3 = ssm_log_dt.reshape(DEPTH, N_GROUP, 1)
    b_t = lambda a: a.transpose(0, 1, 3, 2)
    for d in (weights, mom_m, mom_v):
        d["ssm_b_re"], d["ssm_b_im"] = b_t(d["ssm_b_re"]), b_t(d["ssm_b_im"])
    bt_re, bt_im = weights["ssm_b_re"], weights["ssm_b_im"]
    abar_re, abar_im, bbt_re, bbt_im = _s5_params(log_dt3, ssm_lam_re, ssm_lam_im, bt_re, bt_im)
    s5_args = (bbt_re, bbt_im, ssm_c_re, ssm_c_im, abar_re, abar_im, ssm_d)

    w16 = {n: weights[n].astype(BF16) for n in big_names}
    rest = [w16[n] for n in big_names[1:]]
    wg_in = [None, None]
    wg_rest = [None, None]
    (wg_in[0],) = _run_carried("gather_w_in_l0", _gather_plan([w16["w_in"]], 0))
    xs = [x.reshape(SEQ, D_MODEL)]
    saved = []
    for l in range(DEPTH):
        proj, moved = _norm_proj(l, xs[l], norm_g, wg_in[l], b_in, carry=_gather_plan(rest, 0) if l == 0 else None)
        if l == 0:
            wg_rest[0] = moved
        (states, y0), moved = _s5_scan_fwd(
            l, proj, *s5_args, carry=_gather_plan([w16["w_in"]], 1) if l == 0 else None)
        if l == 0:
            (wg_in[1],) = moved
        pooled = _pool_fwd(l, proj)
        wg_glu, wg_a, wg_b, wg_out = wg_rest[l]
        x_next, moved = _mix_fwd(l, xs[l], proj, y0, pooled, wg_glu, ssm_b_glu, pool_w, pool_scale, wg_a, wg_b,
                                 wg_out, carry=_gather_plan(rest, 1) if l == 0 else None)
        if l == 0:
            wg_rest[1] = moved
        xs.append(x_next)
        saved.append((proj, states, y0, pooled))

    dx, loss_part, g_final = _loss_head(xs[DEPTH], loss_target.reshape(SEQ, D_MODEL), weights["final_norm_g"])
    loss = lax.psum(loss_part[0, 0], ("x", "y", "c"))

    core = lax.axis_index("c").astype(jnp.int32).reshape(1)
    vec_names = ["norm_g", "b_in", "ssm_d", "ssm_b_glu", "pool_scale"]
    flat_names = ["final_norm_g", "ssm_log_dt", "ssm_lam_re", "ssm_lam_im"]
    mat_names = ["pool_w", "ssm_c_re", "ssm_c_im"]
    lane_sparse = {"ssm_c_re": ssm_c_re.shape[1:], "ssm_c_im": ssm_c_im.shape[1:],
                   "ssm_b_re": bt_re.shape, "ssm_b_im": bt_im.shape}
    gridded = set(mat_names) | {"ssm_b_re", "ssm_b_im"}

    def dense(key, a):
        return a.reshape(-1, 128) if key[0] in lane_sparse else a

    def undense(key, slots):
        return slots.reshape((N_CHIP,) + lane_sparse[key[0]]) if key[0] in lane_sparse else slots

    def add_small(tag, keys, own, got):
        out = [None] * len(keys)
        whole = [i for i, k in enumerate(keys) if k[0] not in gridded]
        tiled = [i for i, k in enumerate(keys) if k[0] in gridded]
        for i, r in zip(whole, _add_lists(f"{tag}_a", [own[i] for i in whole], [got[i] for i in whole])):
            out[i] = r
        specs = [pl.BlockSpec((1, POOL_GROUP, POOL_GROUP), lambda j: (j, 0, 0)) if keys[i][0] == "pool_w"
                 else pl.BlockSpec((own[i].shape[0] // N_CHUNK, 128), lambda j: (j, 0)) for i in tiled]
        for i, r in zip(tiled, _add_lists(f"{tag}_b", [own[i] for i in tiled], [got[i] for i in tiled],
                                          grid=(N_CHUNK,), specs=specs)):
            out[i] = r
        return out

    sm = {("final_norm_g", None): g_final}
    g_abar_re, g_abar_im, g_bbt_re, g_bbt_im = ([None] * DEPTH for _ in range(4))
    mix_big, gw_in = None, None
    keys1 = ([(n, 1) for n in vec_names[1:]] + [(n, 1) for n in mat_names] + [("final_norm_g", None)])
    chip1_big = chip1_small = slots1_big = slots1_small = None
    for l in reversed(range(DEPTH)):
        proj, states, y0, pooled = saved[l]
        wg_glu, wg_a, wg_b, wg_out = wg_rest[l]
        carry = None if l == 1 else _chips_plan(chip1_big[:1], [])
        res, moved = _mix_bwd(l, dx, proj, y0, pooled, wg_glu, ssm_b_glu, pool_w, pool_scale, wg_a, wg_b, wg_out,
                              mix_big, carry=carry)
        if l == 0:
            slots1_big = list(moved)
        dproj, dy0, dpooled = res[:3]
        mix_big = list(res[3:7])
        sm[("pool_w", l)], sm[("pool_scale", l)], sm[("ssm_b_glu", l)] = res[7:]
        dproj = _pool_bwd(l, dpooled, dproj)
        carry = None if l == 1 else _chips_plan(chip1_big[1:], chip1_small)
        res, moved = _s5_scan_bwd(l, dy0, proj, states, *s5_args, dproj, carry=carry)
        if l == 0:
            slots1_big += list(moved[:4])
            slots1_small = moved[4:]
        (dproj, g_bbt_re[l], g_bbt_im[l], sm[("ssm_c_re", l)], sm[("ssm_c_im", l)], g_abar_re[l], g_abar_im[l],
         sm[("ssm_d", l)]) = res
        gw_in, sm[("b_in", l)] = _proj_wgrad(l, xs[l], norm_g, dproj, gw_in)
        gw_out, gw_a, gw_b, gw_glu = mix_big
        big_part = [gw_in, gw_glu, gw_a, gw_b, gw_out]
        carry = None
        if l == 1:
            own1 = [dense(k, sm[k]) for k in keys1]
            carry = _sibling_plan([(a, 1) for a in big_part], own1)
        (dx, sm[("norm_g", l)]), moved = _proj_dgrad(l, dx, xs[l], norm_g, dproj, wg_in[l], carry=carry)
        if l == 1:
            chip1_big = [_add_own(f"chip1_{n}", core, a, 1, b) for n, a, b in zip(big_names, big_part, moved[:5])]
            chip1_small = add_small("chip1_small", keys1, own1, moved[5:])
    grad_x = dx.reshape(1, SEQ, D_MODEL)

    g_ld, g_lr, g_li, g_btr, g_bti = _s5_params_bwd(
        log_dt3, ssm_lam_re, ssm_lam_im, bt_re, bt_im, g_abar_re, g_abar_im, g_bbt_re, g_bbt_im)
    sm[("ssm_log_dt", None)] = g_ld.reshape(DEPTH, N_GROUP)
    sm[("ssm_lam_re", None)] = g_lr
    sm[("ssm_lam_im", None)] = g_li
    sm[("ssm_b_re", None)] = g_btr
    sm[("ssm_b_im", None)] = g_bti

    keys0 = ([(n, 0) for n in vec_names] + [("norm_g", 1)] + [(n, 0) for n in mat_names]
             + [(n, None) for n in ("ssm_log_dt", "ssm_lam_re", "ssm_lam_im", "ssm_b_re", "ssm_b_im")])
    own0 = [dense(k, sm[k]) for k in keys0]
    moved = _run_carried("exchange_sibling_l0", _sibling_plan([(a, 0) for a in big_part], own0))
    chip0_big = [_add_own(f"chip0_{n}", core, a, 0, b) for n, a, b in zip(big_names, big_part, moved[:5])]
    chip0_small = add_small("chip0_small", keys0, own0, moved[5:])
    moved = _run_carried("exchange_chips_l0", _chips_plan(chip0_big, chip0_small))
    slots0_big = moved[:5]
    slots = {k: undense(k, s) for k, s in zip(keys0, moved[5:])}
    slots.update({k: undense(k, s) for k, s in zip(keys1, slots1_small)})

    res = {}
    for i, n in enumerate(big_names):
        res[n] = _sum_slots_adamw(n, [slots0_big[i], slots1_big[i]], weights[n], mom_m[n], mom_v[n])
    entries_a = []
    for n in vec_names:
        entries_a.append(([slots[(n, l)] for l in range(DEPTH)], weights[n], mom_m[n], mom_v[n], None, None))
    for n in flat_names:
        entries_a.append((slots[(n, None)], weights[n], mom_m[n], mom_v[n], None, None))
    out_a = _adamw_small("small_a", entries_a)
    for n, r in zip(vec_names + flat_names, out_a):
        res[n] = r
    res["final_norm_g"] = tuple(a.reshape(D_MODEL) for a in res["final_norm_g"])
    pw_s = pl.BlockSpec((N_CHIP, 1, POOL_GROUP, POOL_GROUP), lambda j: (0, j, 0, 0))
    pw_w = pl.BlockSpec((DEPTH, 1, POOL_GROUP, POOL_GROUP), lambda j: (0, j, 0, 0))
    c_s = pl.BlockSpec((N_CHIP, CH_G, GROUP_W, STATE), lambda j: (0, j, 0, 0))
    c_w = pl.BlockSpec((DEPTH, CH_G, GROUP_W, STATE), lambda j: (0, j, 0, 0))
    b_s = pl.BlockSpec((N_CHIP, DEPTH, CH_G, GROUP_W, STATE), lambda j: (0, 0, j, 0, 0))
    b_w = c_w
    entries_b = []
    for n, s_spec, w_spec in (("pool_w", pw_s, pw_w), ("ssm_c_re", c_s, c_w), ("ssm_c_im", c_s, c_w)):
        entries_b.append(([slots[(n, l)] for l in range(DEPTH)], weights[n], mom_m[n], mom_v[n], s_spec, w_spec))
    for n in ("ssm_b_re", "ssm_b_im"):
        entries_b.append((slots[(n, None)], weights[n], mom_m[n], mom_v[n], b_s, b_w))
    out_b = _adamw_small("small_b", entries_b, grid=(N_CHUNK,))
    for n, r in zip(mat_names + ["ssm_b_re", "ssm_b_im"], out_b):
        res[n] = tuple(b_t(a) for a in r) if n in ("ssm_b_re", "ssm_b_im") else r

    outs = [loss, grad_x]
    for i in range(4):
        outs += [res[n][i] for n in order]
    return tuple(outs)
```

```python
import math

import jax
import jax.numpy as jnp
from jax import lax
from jax.experimental import pallas as pl
from jax.experimental.pallas import tpu as pltpu

F32 = jnp.float32
BF16 = jnp.bfloat16

SEQ = 2048
D_MODEL = 1024
N_IN = 4096
WIDTH = 512
N_GROUP = 32
GROUP_W = 16
STATE = 64
N_STATE = N_GROUP * STATE
N_CHUNK = 4
CH_G = N_GROUP // N_CHUNK
CH_W = WIDTH // N_CHUNK
CH_S = N_STATE // N_CHUNK
N_DEV = 8
N_CHIP = 4
POOL_WINDOWS = (2, 4, 8, 16)
POOL_GROUP = 128
EPS = 1e-6
DEPTH = 2

ADAM_LR = 0.001
ADAM_B1 = 0.9
ADAM_B2 = 0.999
ADAM_EPS = 1e-08
ADAM_WD = 0.01
ADAM_STEP = 10

TILE_M = 256
ROW_BLK = 512
VMEM_LIMIT = 48 * 1024 * 1024
VMEM_LIMIT_BIG = 60 * 1024 * 1024
MESH = pl.DeviceIdType.MESH
ANY = pl.BlockSpec(memory_space=pl.ANY)

GELU_C = math.sqrt(2.0 / math.pi)
GELU_A = 0.044715

SDS = jax.ShapeDtypeStruct


def _cp(sem=None, limit=VMEM_LIMIT):
    return pltpu.CompilerParams(dimension_semantics=sem, vmem_limit_bytes=limit)


def _dot(a, b):
    return jnp.dot(a, b, preferred_element_type=F32)


def _dot_nt(a, b):
    return lax.dot_general(a, b, (((1,), (1,)), ((), ())), preferred_element_type=F32)


def _dot_tn(a, b):
    return lax.dot_general(a, b, (((0,), (0,)), ((), ())), preferred_element_type=F32)


def _sig(x):
    return jax.nn.sigmoid(x)


def _rms(x):
    rs = lax.rsqrt(jnp.mean(x * x, axis=-1, keepdims=True) + EPS)
    return rs, x * rs


def _slot(n):
    return 4 * (n % 2) + n // 2


def _const(shape):
    n = len(shape)
    return pl.BlockSpec(shape, lambda *_: (0,) * n)


def _sum4(p):
    return (p[0] + p[1]) + (p[2] + p[3])


def _sum_slots(s_ref):
    vals = [s_ref[k] for k in range(s_ref.shape[0])]
    while len(vals) > 1:
        vals = [vals[i] + vals[i + 1] for i in range(0, len(vals), 2)]
    return vals[0]


def _s5_param_fn(log_dt, lam_re, lam_im, bt_re, bt_im):
    dt = jnp.exp(log_dt)
    mag = jnp.exp(lam_re * dt)
    ang = lam_im * dt
    abar_re = mag * jnp.cos(ang)
    abar_im = mag * jnp.sin(ang)
    num_re = abar_re - 1.0
    num_im = abar_im
    den = lam_re * lam_re + lam_im * lam_im
    coef_re = (num_re * lam_re + num_im * lam_im) / den
    coef_im = (num_im * lam_re - num_re * lam_im) / den
    bbar_re = coef_re[..., None, :] * bt_re - coef_im[..., None, :] * bt_im
    bbar_im = coef_re[..., None, :] * bt_im + coef_im[..., None, :] * bt_re
    return abar_re, abar_im, bbar_re, bbar_im


def _s5_params(log_dt, lam_re, lam_im, bt_re, bt_im):
    def body(ld, lr, li, br, bi, o_ar, o_ai, o_br, o_bi):
        ar, ai, bbr, bbi = _s5_param_fn(ld[...], lr[...], li[...], br[...], bi[...])
        o_ar[...] = ar
        o_ai[...] = ai
        o_br[...] = bbr
        o_bi[...] = bbi

    return pl.pallas_call(
        body, name="s5_params",
        out_shape=(SDS(lam_re.shape, F32), SDS(lam_re.shape, F32), SDS(bt_re.shape, F32), SDS(bt_re.shape, F32)),
    )(log_dt, lam_re, lam_im, bt_re, bt_im)


def _s5_params_bwd(layer, log_dt, lam_re, lam_im, bt_re, bt_im, g_ar, g_ai, g_br, g_bi):
    def body(ld, lr, li, br, bi, car, cai, cbr, cbi, o_ld, o_lr, o_li, o_br, o_bi):
        _, vjp = jax.vjp(_s5_param_fn, ld[...], lr[...], li[...], br[...], bi[...])
        d_ld, d_lr, d_li, d_br, d_bi = vjp((car[...], cai[...], cbr[...], cbi[...]))
        o_ld[...] = d_ld
        o_lr[...] = d_lr
        o_li[...] = d_li
        o_br[...] = d_br
        o_bi[...] = d_bi

    one = lambda shape: pl.BlockSpec((None,) + shape, lambda i: (layer,) + (0,) * len(shape))
    whole = lambda shape: _const(shape)
    vec, lam, mat = (N_GROUP, 1), (N_GROUP, STATE), (N_GROUP, GROUP_W, STATE)
    return pl.pallas_call(
        body, name=f"s5_params_bwd_l{layer}", grid=(1,),
        in_specs=[one(vec), one(lam), one(lam), one(mat), one(mat), whole(lam), whole(lam), whole(mat), whole(mat)],
        out_specs=(whole(vec), whole(lam), whole(lam), whole(mat), whole(mat)),
        out_shape=(SDS(vec, F32), SDS(lam, F32), SDS(lam, F32), SDS(mat, F32), SDS(mat, F32)),
    )(log_dt, lam_re, lam_im, bt_re, bt_im, g_ar, g_ai, g_br, g_bi)


def _norm_proj(layer, x, norm_g, wg_in, b_in, carry=None):
    def body(x_ref, g_ref, w_ref, b_ref, o_ref):
        _, xn = _rms(x_ref[...])
        h = (xn * g_ref[layer:layer + 1, :]).astype(BF16)
        for k in range(N_DEV):
            cols = slice(k * WIDTH, (k + 1) * WIDTH)
            o_ref[:, cols] = _dot(h, w_ref[k]) + b_ref[layer:layer + 1, cols]

    (proj,), moved = _pcall(
        body, name=f"norm_proj_l{layer}",
        out_shape=[SDS((SEQ, N_IN), F32)],
        grid=(SEQ // TILE_M,),
        in_specs=[pl.BlockSpec((TILE_M, D_MODEL), lambda i: (i, 0)),
                  _const((DEPTH, D_MODEL)),
                  _const((N_DEV, D_MODEL, WIDTH)),
                  _const((DEPTH, N_IN))],
        out_specs=[pl.BlockSpec((TILE_M, N_IN), lambda i: (i, 0))],
        args=[x, norm_g, wg_in, b_in], sem=("parallel",), carry=carry)
    return proj, moved


TIME_BLK = 512
N_TBLK = SEQ // TIME_BLK
N_PANEL = CH_S // 128
STATE_SHAPE = (N_PANEL, SEQ * 8, 128)


def _s5_layer_specs(layer):
    mat = lambda: pl.BlockSpec((None, N_GROUP, GROUP_W, STATE), lambda i: (layer, 0, 0, 0))
    ab = lambda: pl.BlockSpec((None, N_GROUP, STATE), lambda i: (layer, 0, 0))
    return [mat(), mat(), mat(), mat(), ab(), ab(), _const((DEPTH, WIDTH))]


def _s5_layer_scratch():
    return [pltpu.VMEM((N_CHUNK, CH_W, CH_S), BF16)] * 4 + [pltpu.VMEM((8, CH_S), F32)] * 2


def _s5_layer_fill(btre_ref, btim_ref, cre_ref, cim_ref, are_ref, aim_ref, bdre, bdim, ctre, ctim, a1, a2):
    for m in (bdre, bdim, ctre, ctim):
        m[...] = jnp.zeros_like(m)
    for grp in range(N_GROUP):
        k, g = divmod(grp, CH_G)
        rows = slice(g * GROUP_W, (g + 1) * GROUP_W)
        cols = slice(g * STATE, (g + 1) * STATE)
        bdre[k, rows, cols] = btre_ref[grp].astype(BF16)
        bdim[k, rows, cols] = btim_ref[grp].astype(BF16)
        ctre[k, rows, cols] = cre_ref[grp].astype(BF16)
        ctim[k, rows, cols] = cim_ref[grp].astype(BF16)
        ar = are_ref[grp:grp + 1, :]
        ai = aim_ref[grp:grp + 1, :]
        a1[k:k + 1, cols] = ar
        a1[N_CHUNK + k:N_CHUNK + k + 1, cols] = ar
        a2[k:k + 1, cols] = -ai
        a2[N_CHUNK + k:N_CHUNK + k + 1, cols] = ai


SCAN_UNROLL = 8


def _panels(tile):
    return [tile[:, p * 128:(p + 1) * 128] for p in range(N_PANEL)]


def _rows_load(ref, row):
    return jnp.concatenate([ref[p, pl.ds(row, TIME_BLK, stride=8), :] for p in range(N_PANEL)], axis=1)


def _rows_store(ref, row, val):
    for p in range(N_PANEL):
        ref[p, pl.ds(row, TIME_BLK, stride=8), :] = val[:, p * 128:(p + 1) * 128]


def _s5_scan_fwd(layer, proj, bbt_re, bbt_im, c_re, c_im, abar_re, abar_im, d_skip, carry=None):
    def body(u_ref, btre_ref, btim_ref, cre_ref, cim_ref, are_ref, aim_ref, d_ref, s_ref, y_ref,
             bdre, bdim, ctre, ctim, a1, a2, state):
        @pl.when(pl.program_id(0) == 0)
        def _():
            _s5_layer_fill(btre_ref, btim_ref, cre_ref, cim_ref, are_ref, aim_ref, bdre, bdim, ctre, ctim, a1, a2)
            state[...] = jnp.zeros_like(state)

        for k in range(N_CHUNK):
            ub = u_ref[:, k * CH_W:(k + 1) * CH_W].astype(BF16)
            _rows_store(s_ref, k, _dot(ub, bdre[k]))
            _rows_store(s_ref, N_CHUNK + k, _dot(ub, bdim[k]))
        m1 = _panels(a1[...])
        m2 = _panels(a2[...])

        def steps(n, tile):
            for r in range(SCAN_UNROLL):
                rows = pl.ds(pl.multiple_of((n * SCAN_UNROLL + r) * 8, 8), 8)
                tile = [m1[p] * tile[p] + m2[p] * pltpu.roll(tile[p], N_CHUNK, 0) + s_ref[p, rows, :]
                        for p in range(N_PANEL)]
                for p in range(N_PANEL):
                    s_ref[p, rows, :] = tile[p]
            return tile

        tile = lax.fori_loop(0, TIME_BLK // SCAN_UNROLL, steps, _panels(state[...]))
        state[...] = jnp.concatenate(tile, axis=1)
        d = d_ref[layer:layer + 1, :]
        for k in range(N_CHUNK):
            cols = slice(k * CH_W, (k + 1) * CH_W)
            y = (_dot_nt(_rows_load(s_ref, k).astype(BF16), ctre[k])
                 - _dot_nt(_rows_load(s_ref, N_CHUNK + k).astype(BF16), ctim[k]))
            y_ref[:, cols] = y + d[:, cols] * u_ref[:, cols]

    return _pcall(
        body, name=f"s5_fwd_l{layer}",
        out_shape=(SDS(STATE_SHAPE, F32), SDS((SEQ, WIDTH), F32)),
        grid=(N_TBLK,),
        in_specs=[pl.BlockSpec((TIME_BLK, WIDTH), lambda i: (i, 0))] + _s5_layer_specs(layer),
        out_specs=(pl.BlockSpec((N_PANEL, TIME_BLK * 8, 128), lambda i: (0, i, 0)),
                   pl.BlockSpec((TIME_BLK, WIDTH), lambda i: (i, 0))),
        scratch_shapes=_s5_layer_scratch() + [pltpu.VMEM((8, CH_S), F32)],
        args=[proj, bbt_re, bbt_im, c_re, c_im, abar_re, abar_im, d_skip], sem=("arbitrary",), carry=carry)


def _s5_scan_bwd(layer, dy0, proj, states, bbt_re, bbt_im, c_re, c_im, abar_re, abar_im, d_skip, dproj,
                 carry=None):
    def body(dy_ref, u_ref, s_ref, sprev_ref, btre_ref, btim_ref, cre_ref, cim_ref, are_ref, aim_ref, d_ref, _,
             du_ref, gbre_ref, gbim_ref, gcre_ref, gcim_ref, gare_ref, gaim_ref, gd_ref,
             lam_ref, bdre, bdim, ctre, ctim, a1, a2, state, acc1, acc2, gbre, gbim, gcre, gcim, gd):
        step_id = pl.program_id(0)

        @pl.when(step_id == 0)
        def _():
            _s5_layer_fill(btre_ref, btim_ref, cre_ref, cim_ref, are_ref, aim_ref, bdre, bdim, ctre, ctim, a1, a2)
            for r in (state, acc1, acc2, gbre, gbim, gcre, gcim, gd):
                r[...] = jnp.zeros_like(r)

        for k in range(N_CHUNK):
            dyb = dy_ref[:, k * CH_W:(k + 1) * CH_W].astype(BF16)
            _rows_store(lam_ref, k, _dot(dyb, ctre[k]))
            _rows_store(lam_ref, N_CHUNK + k, -_dot(dyb, ctim[k]))
            gcre[k] += _dot_tn(dyb, _rows_load(s_ref, k).astype(BF16))
            gcim[k] -= _dot_tn(dyb, _rows_load(s_ref, N_CHUNK + k).astype(BF16))

        m1 = _panels(a1[...])
        m2 = _panels(-a2[...])
        has_before = (step_id < N_TBLK - 1).astype(F32)

        def one(t8, c, first_token):
            tile, swapped, p1, p2 = c
            rows = pl.ds(t8, 8)
            tile = [m1[p] * tile[p] + m2[p] * swapped[p] + lam_ref[p, rows, :] for p in range(N_PANEL)]
            swapped = [pltpu.roll(tile[p], N_CHUNK, 0) for p in range(N_PANEL)]
            for p in range(N_PANEL):
                lam_ref[p, rows, :] = tile[p]
            if first_token:
                before = [sprev_ref[p] * has_before for p in range(N_PANEL)]
            else:
                before = [s_ref[p, pl.ds(t8 - 8, 8), :] for p in range(N_PANEL)]
            p1 = [p1[p] + tile[p] * before[p] for p in range(N_PANEL)]
            p2 = [p2[p] + swapped[p] * before[p] for p in range(N_PANEL)]
            return tile, swapped, p1, p2

        def steps(n, c):
            for r in range(SCAN_UNROLL):
                t8 = pl.multiple_of((TIME_BLK - 1 - (n * SCAN_UNROLL + r)) * 8, 8)
                c = one(t8, c, False)
            return c

        tile0 = _panels(state[...])
        c = (tile0, [pltpu.roll(t, N_CHUNK, 0) for t in tile0], _panels(acc1[...]), _panels(acc2[...]))
        c = lax.fori_loop(0, TIME_BLK // SCAN_UNROLL - 1, steps, c)
        for r in range(SCAN_UNROLL - 1, -1, -1):
            c = one(r * 8, c, r == 0)
        state[...] = jnp.concatenate(c[0], axis=1)
        acc1[...] = jnp.concatenate(c[2], axis=1)
        acc2[...] = jnp.concatenate(c[3], axis=1)

        d = d_ref[layer:layer + 1, :]
        for k in range(N_CHUNK):
            cols = slice(k * CH_W, (k + 1) * CH_W)
            lrb = _rows_load(lam_ref, k).astype(BF16)
            lib = _rows_load(lam_ref, N_CHUNK + k).astype(BF16)
            u = u_ref[:, cols]
            ub = u.astype(BF16)
            dy = dy_ref[:, cols]
            du = dy * d[:, cols] + _dot_nt(lrb, bdre[k]) + _dot_nt(lib, bdim[k])
            du_ref[:, cols] = du.astype(BF16)
            gbre[k] += _dot_tn(ub, lrb)
            gbim[k] += _dot_tn(ub, lib)
        gd[...] += jnp.sum(dy_ref[...] * u_ref[...], axis=0, keepdims=True)

        @pl.when(step_id == N_TBLK - 1)
        def _():
            gd_ref[...] = gd[...]
            ga_re = acc1[0:N_CHUNK, :] + acc1[N_CHUNK:, :]
            ga_im = acc2[0:N_CHUNK, :] - acc2[N_CHUNK:, :]
            for grp in range(N_GROUP):
                k, g = divmod(grp, CH_G)
                rows = slice(g * GROUP_W, (g + 1) * GROUP_W)
                cols = slice(g * STATE, (g + 1) * STATE)
                gcre_ref[grp] = gcre[k, rows, cols]
                gcim_ref[grp] = gcim[k, rows, cols]
                gbre_ref[grp] = gbre[k, rows, cols]
                gbim_ref[grp] = gbim[k, rows, cols]
                gare_ref[grp:grp + 1, :] = ga_re[k:k + 1, cols]
                gaim_ref[grp:grp + 1, :] = ga_im[k:k + 1, cols]

    back = lambda i: N_TBLK - 1 - i
    tok = lambda: pl.BlockSpec((TIME_BLK, WIDTH), lambda i: (back(i), 0))
    mat = lambda: _const((N_GROUP, GROUP_W, STATE))
    acc_mat = pltpu.VMEM((N_CHUNK, CH_W, CH_S), F32)
    return _pcall(
        body, name=f"s5_bwd_l{layer}",
        out_shape=(SDS((SEQ, N_IN), BF16), SDS((N_GROUP, GROUP_W, STATE), F32), SDS((N_GROUP, GROUP_W, STATE), F32),
                   SDS((N_GROUP, GROUP_W, STATE), F32), SDS((N_GROUP, GROUP_W, STATE), F32),
                   SDS((N_GROUP, STATE), F32), SDS((N_GROUP, STATE), F32), SDS((1, WIDTH), F32)),
        grid=(N_TBLK,),
        in_specs=[tok(), tok(),
                  pl.BlockSpec((N_PANEL, TIME_BLK * 8, 128), lambda i: (0, back(i), 0)),
                  pl.BlockSpec((N_PANEL, 8, 128), lambda i: (0, jnp.maximum(back(i) * TIME_BLK - 1, 0), 0))]
        + _s5_layer_specs(layer) + [ANY],
        out_specs=(tok(), mat(), mat(), mat(), mat(), _const((N_GROUP, STATE)), _const((N_GROUP, STATE)),
                   _const((1, WIDTH))),
        scratch_shapes=[pltpu.VMEM((N_PANEL, TIME_BLK * 8, 128), F32)] + _s5_layer_scratch()
        + [pltpu.VMEM((8, CH_S), F32)] * 3 + [acc_mat] * 4 + [pltpu.VMEM((1, WIDTH), F32)],
        args=[dy0, proj, states, states, bbt_re, bbt_im, c_re, c_im, abar_re, abar_im, d_skip, dproj],
        aliases={11: 0}, sem=("arbitrary",), limit=VMEM_LIMIT_BIG, carry=carry)


def _pool_counts(win):
    t = lax.broadcasted_iota(jnp.int32, (SEQ, POOL_GROUP), 0)
    return t, jnp.minimum(t + 1, win).astype(F32)


def _pool_fwd(layer, proj):
    def body(u_ref, o_ref):
        for gi, win in enumerate(POOL_WINDOWS):
            cols = slice(gi * POOL_GROUP, (gi + 1) * POOL_GROUP)
            u = u_ref[:, cols]
            t, count = _pool_counts(win)
            acc = u
            k = 1
            while k < win:
                acc = acc + jnp.where(t >= k, pltpu.roll(acc, k, 0), 0.0)
                k *= 2
            o_ref[:, cols] = acc / count - u

    return pl.pallas_call(
        body, name=f"pool_fwd_l{layer}",
        out_shape=SDS((SEQ, WIDTH), F32),
        grid=(1,),
        in_specs=[pl.BlockSpec((SEQ, WIDTH), lambda i: (0, 2))],
        out_specs=pl.BlockSpec((SEQ, WIDTH), lambda i: (0, 0)),
        compiler_params=_cp(("arbitrary",)),
    )(proj)


def _gelu_parts(y0):
    t = jnp.tanh(GELU_C * (y0 + GELU_A * (y0 * y0 * y0)))
    return t, 0.5 * y0 * (1.0 + t)


def _mix_forward(layer, p_ref, y0_ref, pooled_ref, wglu_ref, bglu_ref, pw_ref, scale_ref, wa_ref, wb_ref):
    za = p_ref[:, WIDTH:2 * WIDTH]
    zb = p_ref[:, 3 * WIDTH:4 * WIDTH]
    ga = p_ref[:, 4 * WIDTH:4 * WIDTH + D_MODEL]
    gb = p_ref[:, 4 * WIDTH + D_MODEL:]
    y0 = y0_ref[...]
    t, y1 = _gelu_parts(y0)
    y1b = y1.astype(BF16)
    q = _dot(y1b, wglu_ref[...].reshape(WIDTH, WIDTH)) + bglu_ref[layer:layer + 1, :]
    sq = _sig(q)
    y2 = y1 * sq
    sza = _sig(za)
    silu_za = za * sza
    ya = y2 * silu_za
    pooled = pooled_ref[...]
    mixed = jnp.concatenate(
        [_dot(pooled[:, g * POOL_GROUP:(g + 1) * POOL_GROUP].astype(BF16), pw_ref[g].astype(BF16))
         for g in range(len(POOL_WINDOWS))], axis=1)
    szb = _sig(zb)
    silu_zb = zb * szb
    scale = scale_ref[layer:layer + 1, :]
    ms = mixed * scale
    yb = ms * silu_zb
    yab = ya.astype(BF16)
    ybb = yb.astype(BF16)
    ma = jnp.concatenate([_dot(yab, wa_ref[k]) for k in range(N_DEV)], axis=1)
    mb = jnp.concatenate([_dot(ybb, wb_ref[k]) for k in range(N_DEV)], axis=1)
    sga = _sig(ga)
    sgb = _sig(gb)
    merged = sga * ma + sgb * mb
    return dict(za=za, zb=zb, y0=y0, t=t, y1=y1, y1b=y1b, sq=sq, y2=y2, sza=sza, silu_za=silu_za,
                pooled=pooled, mixed=mixed, szb=szb, silu_zb=silu_zb, scale=scale, ms=ms, yab=yab, ybb=ybb,
                ma=ma, mb=mb, sga=sga, sgb=sgb, merged=merged)


def _mix_weight_specs(layer):
    return [_const((N_DEV, WIDTH // N_DEV, WIDTH)),
            _const((DEPTH, WIDTH)),
            pl.BlockSpec((None, 4, POOL_GROUP, POOL_GROUP), lambda i: (layer, 0, 0, 0)),
            _const((DEPTH, WIDTH)),
            _const((N_DEV, WIDTH, D_MODEL // N_DEV)),
            _const((N_DEV, WIDTH, D_MODEL // N_DEV)),
            _const((N_DEV, D_MODEL // N_DEV, D_MODEL))]


def _mix_fwd(layer, x, proj, y0, pooled, wg_glu, b_glu, pool_w, pool_scale, wg_a, wg_b, wg_out, carry=None):
    def body(x_ref, p_ref, y0_ref, pooled_ref, wglu_ref, bglu_ref, pw_ref, scale_ref, wa_ref, wb_ref,
             wout_ref, o_ref):
        f = _mix_forward(layer, p_ref, y0_ref, pooled_ref, wglu_ref, bglu_ref, pw_ref, scale_ref, wa_ref, wb_ref)
        wout = wout_ref[...].reshape(D_MODEL, D_MODEL)
        o_ref[...] = x_ref[...] + _dot(f["merged"].astype(BF16), wout)

    (x_next,), moved = _pcall(
        body, name=f"mix_fwd_l{layer}",
        out_shape=[SDS((SEQ, D_MODEL), F32)],
        grid=(SEQ // TILE_M,),
        in_specs=[pl.BlockSpec((TILE_M, D_MODEL), lambda i: (i, 0)),
                  pl.BlockSpec((TILE_M, N_IN), lambda i: (i, 0)),
                  pl.BlockSpec((TILE_M, WIDTH), lambda i: (i, 0)),
                  pl.BlockSpec((TILE_M, WIDTH), lambda i: (i, 0))] + _mix_weight_specs(layer),
        out_specs=[pl.BlockSpec((TILE_M, D_MODEL), lambda i: (i, 0))],
        args=[x, proj, y0, pooled, wg_glu, b_glu, pool_w, pool_scale, wg_a, wg_b, wg_out],
        sem=("parallel",), carry=carry)
    return x_next, moved


def _loss_head(x, target, final_g):
    def body(x_ref, t_ref, g_ref, dx_ref, loss_ref, gg_ref):
        @pl.when(pl.program_id(0) == 0)
        def _():
            loss_ref[...] = jnp.zeros_like(loss_ref)
            gg_ref[...] = jnp.zeros_like(gg_ref)

        g = g_ref[...]
        rs, xn = _rms(x_ref[...])
        err = xn * g - t_ref[...]
        loss_ref[...] += 0.5 * jnp.sum(jnp.mean(err * err, axis=-1, keepdims=True), axis=0, keepdims=True)
        dy = err * (1.0 / D_MODEL)
        gg_ref[...] += jnp.sum(dy * xn, axis=0, keepdims=True)
        dxn = dy * g
        dx_ref[...] = rs * (dxn - xn * jnp.mean(dxn * xn, axis=-1, keepdims=True))

    return pl.pallas_call(
        body, name="loss_head",
        out_shape=(SDS((SEQ, D_MODEL), F32), SDS((1, 1), F32), SDS((1, D_MODEL), F32)),
        grid=(SEQ // TILE_M,),
        in_specs=[pl.BlockSpec((TILE_M, D_MODEL), lambda i: (i, 0)),
                  pl.BlockSpec((TILE_M, D_MODEL), lambda i: (i, 0)),
                  _const((1, D_MODEL))],
        out_specs=(pl.BlockSpec((TILE_M, D_MODEL), lambda i: (i, 0)), _const((1, 1)), _const((1, D_MODEL))),
        compiler_params=_cp(("arbitrary",)),
    )(x, target, final_g)


def _big_shapes():
    return dict(w_out=(DEPTH, N_DEV, D_MODEL // N_DEV, D_MODEL), w_branch_a=(DEPTH, N_DEV, WIDTH, D_MODEL // N_DEV),
                w_branch_b=(DEPTH, N_DEV, WIDTH, D_MODEL // N_DEV), ssm_w_glu=(DEPTH, N_DEV, WIDTH // N_DEV, WIDTH),
                w_in=(DEPTH, N_DEV, D_MODEL, WIDTH))


def _mix_bwd(layer, dx_next, proj, y0, pooled, wg_glu, b_glu, pool_w, pool_scale, wg_a, wg_b, wg_out, prev,
             carry=None):
    n_k = N_DEV
    n_prev = 0 if prev is None else len(prev)

    def body(*refs):
        (dx_ref, p_ref, y0_ref, pooled_ref, wglu_ref, bglu_ref, pw_ref, scale_ref, wa_ref, wb_ref,
         wout_ref) = refs[:11]
        (dproj_ref, dy0_ref, dpooled_ref, gwout_ref, gwa_ref, gwb_ref, gwglu_ref, gpw_ref,
         gscale_ref, gbglu_ref) = refs[11 + n_prev:]

        @pl.when(pl.program_id(0) == 0)
        def _():
            for r in (gwout_ref, gwa_ref, gwb_ref, gwglu_ref, gpw_ref, gscale_ref, gbglu_ref):
                r[...] = jnp.zeros_like(r)

        f = _mix_forward(layer, p_ref, y0_ref, pooled_ref, wglu_ref, bglu_ref, pw_ref, scale_ref, wa_ref, wb_ref)
        wglu = wglu_ref[...].reshape(WIDTH, WIDTH)
        wout = wout_ref[...].reshape(D_MODEL, D_MODEL)
        blk = D_MODEL // n_k
        dxb = dx_ref[...].astype(BF16)
        dmerged = _dot_nt(dxb, wout)
        gwout = _dot_tn(f["merged"].astype(BF16), dxb)
        for k in range(n_k):
            gwout_ref[_slot(k)] += gwout[k * blk:(k + 1) * blk, :]
        dma = dmerged * f["sga"]
        dmb = dmerged * f["sgb"]
        dga = dmerged * f["ma"] * f["sga"] * (1.0 - f["sga"])
        dgb = dmerged * f["mb"] * f["sgb"] * (1.0 - f["sgb"])
        dmab = dma.astype(BF16)
        dmbb = dmb.astype(BF16)
        dya = jnp.zeros((TILE_M, WIDTH), F32)
        dyb = jnp.zeros((TILE_M, WIDTH), F32)
        for k in range(n_k):
            da_k = dmab[:, k * blk:(k + 1) * blk]
            db_k = dmbb[:, k * blk:(k + 1) * blk]
            dya = dya + _dot_nt(da_k, wa_ref[k])
            dyb = dyb + _dot_nt(db_k, wb_ref[k])
            gwa_ref[_slot(k)] += _dot_tn(f["yab"], da_k)
            gwb_ref[_slot(k)] += _dot_tn(f["ybb"], db_k)
        zb, szb = f["zb"], f["szb"]
        dzb = dyb * f["ms"] * (szb * (1.0 + zb * (1.0 - szb)))
        dms = dyb * f["silu_zb"]
        gscale_ref[...] += jnp.sum(dms * f["mixed"], axis=0, keepdims=True)
        dmixed = (dms * f["scale"]).astype(BF16)
        pooled = f["pooled"]
        for g in range(len(POOL_WINDOWS)):
            cols = slice(g * POOL_GROUP, (g + 1) * POOL_GROUP)
            dpooled_ref[:, cols] = _dot_nt(dmixed[:, cols], pw_ref[g].astype(BF16))
            gpw_ref[g] += _dot_tn(pooled[:, cols].astype(BF16), dmixed[:, cols])
        za, sza = f["za"], f["sza"]
        dza = dya * f["y2"] * (sza * (1.0 + za * (1.0 - sza)))
        dy2 = dya * f["silu_za"]
        sq = f["sq"]
        dq = dy2 * f["y1"] * sq * (1.0 - sq)
        dqb = dq.astype(BF16)
        dy1 = dy2 * sq + _dot_nt(dqb, wglu)
        gwglu = _dot_tn(f["y1b"], dqb)
        rblk = WIDTH // n_k
        for k in range(n_k):
            gwglu_ref[_slot(k)] += gwglu[k * rblk:(k + 1) * rblk, :]
        gbglu_ref[...] += jnp.sum(dq, axis=0, keepdims=True)
        y0, t = f["y0"], f["t"]
        dgelu = 0.5 * (1.0 + t) + 0.5 * y0 * (1.0 - t * t) * (GELU_C * (1.0 + 3.0 * GELU_A * y0 * y0))
        dy0_ref[...] = dy1 * dgelu
        zeros = jnp.zeros((TILE_M, WIDTH), BF16)
        dproj_ref[:, 0:WIDTH] = zeros
        dproj_ref[:, WIDTH:2 * WIDTH] = dza.astype(BF16)
        dproj_ref[:, 2 * WIDTH:3 * WIDTH] = zeros
        dproj_ref[:, 3 * WIDTH:4 * WIDTH] = dzb.astype(BF16)
        dproj_ref[:, 4 * WIDTH:4 * WIDTH + D_MODEL] = dga.astype(BF16)
        dproj_ref[:, 4 * WIDTH + D_MODEL:] = dgb.astype(BF16)

    tile = lambda w: pl.BlockSpec((TILE_M, w), lambda i: (i, 0))
    shapes = _big_shapes()
    big = ["w_out", "w_branch_a", "w_branch_b", "ssm_w_glu"]
    slab = lambda n: pl.BlockSpec((None,) + shapes[n][1:], lambda i: (layer, 0, 0, 0))
    args = [dx_next, proj, y0, pooled, wg_glu, b_glu, pool_w, pool_scale, wg_a, wg_b, wg_out]
    return _pcall(
        body, name=f"mix_bwd_l{layer}",
        out_shape=(SDS((SEQ, N_IN), BF16), SDS((SEQ, WIDTH), F32), SDS((SEQ, WIDTH), F32))
        + tuple(SDS(shapes[n], F32) for n in big)
        + (SDS((4, POOL_GROUP, POOL_GROUP), F32), SDS((1, WIDTH), F32), SDS((1, WIDTH), F32)),
        grid=(SEQ // TILE_M,),
        in_specs=[tile(D_MODEL), tile(N_IN), tile(WIDTH), tile(WIDTH)] + _mix_weight_specs(layer) + [ANY] * n_prev,
        out_specs=(tile(N_IN), tile(WIDTH), tile(WIDTH)) + tuple(slab(n) for n in big)
        + (_const((4, POOL_GROUP, POOL_GROUP)), _const((1, WIDTH)), _const((1, WIDTH))),
        args=args + list(prev or ()),
        aliases={len(args) + i: 3 + i for i in range(n_prev)},
        sem=("arbitrary",), limit=VMEM_LIMIT_BIG, carry=carry)


def _pool_bwd(layer, dpooled, dproj):
    def body(dp_ref, _, o_ref):
        for gi, win in enumerate(POOL_WINDOWS):
            cols = slice(gi * POOL_GROUP, (gi + 1) * POOL_GROUP)
            dp = dp_ref[:, cols]
            t, count = _pool_counts(win)
            e = dp / count
            acc = e
            k = 1
            while k < win:
                acc = acc + jnp.where(t < SEQ - k, pltpu.roll(acc, SEQ - k, 0), 0.0)
                k *= 2
            o_ref[:, cols] = (acc - dp).astype(BF16)

    return pl.pallas_call(
        body, name=f"pool_bwd_l{layer}",
        out_shape=SDS((SEQ, N_IN), BF16),
        grid=(1,),
        in_specs=[pl.BlockSpec((SEQ, WIDTH), lambda i: (0, 0)), ANY],
        out_specs=pl.BlockSpec((SEQ, WIDTH), lambda i: (0, 2)),
        input_output_aliases={1: 0},
        compiler_params=_cp(("arbitrary",)),
    )(dpooled, dproj)


def _proj_wgrad(layer, x, norm_g, dproj, prev, carry=None):
    tm = 512
    n_prev = 0 if prev is None else 1

    def body(*refs):
        x_ref, g_ref, dp_ref = refs[:3]
        gw_ref, gb_ref = refs[3 + n_prev:]

        @pl.when(pl.program_id(1) == 0)
        def _():
            gw_ref[...] = jnp.zeros_like(gw_ref)
            gb_ref[...] = jnp.zeros_like(gb_ref)

        _, xn = _rms(x_ref[...])
        h = (xn * g_ref[layer:layer + 1, :]).astype(BF16)
        dp = dp_ref[...]
        gw_ref[...] += _dot_tn(h, dp)
        gb_ref[...] += jnp.sum(dp.astype(F32), axis=0, keepdims=True)

    return _pcall(
        body, name=f"proj_wgrad_l{layer}",
        out_shape=(SDS(_big_shapes()["w_in"], F32), SDS((1, N_IN), F32)),
        grid=(N_DEV, SEQ // tm),
        in_specs=[pl.BlockSpec((tm, D_MODEL), lambda n, t: (t, 0)),
                  _const((DEPTH, D_MODEL)),
                  pl.BlockSpec((tm, WIDTH), lambda n, t: (t, n))] + [ANY] * n_prev,
        out_specs=(pl.BlockSpec((None, None, D_MODEL, WIDTH), lambda n, t: (layer, _slot(n), 0, 0)),
                   pl.BlockSpec((1, WIDTH), lambda n, t: (0, n))),
        args=[x, norm_g, dproj] + ([prev] if n_prev else []),
        aliases={3: 0} if n_prev else {}, sem=("parallel", "arbitrary"), carry=carry)


def _proj_dgrad(layer, dx_next, x, norm_g, dproj, wg_in, carry=None):
    def body(dxn_ref, x_ref, g_ref, dp_ref, w_ref, dx_ref, gg_ref):
        @pl.when(pl.program_id(0) == 0)
        def _():
            gg_ref[...] = jnp.zeros_like(gg_ref)

        dh = jnp.zeros((TILE_M, D_MODEL), F32)
        for k in range(N_DEV):
            dh = dh + _dot_nt(dp_ref[:, k * WIDTH:(k + 1) * WIDTH], w_ref[k])
        rs, xn = _rms(x_ref[...])
        gg_ref[...] += jnp.sum(dh * xn, axis=0, keepdims=True)
        dxn = dh * g_ref[layer:layer + 1, :]
        dx_ref[...] = dxn_ref[...] + rs * (dxn - xn * jnp.mean(dxn * xn, axis=-1, keepdims=True))

    return _pcall(
        body, name=f"proj_dgrad_l{layer}",
        out_shape=(SDS((SEQ, D_MODEL), F32), SDS((1, D_MODEL), F32)),
        grid=(SEQ // TILE_M,),
        in_specs=[pl.BlockSpec((TILE_M, D_MODEL), lambda i: (i, 0)),
                  pl.BlockSpec((TILE_M, D_MODEL), lambda i: (i, 0)),
                  _const((DEPTH, D_MODEL)),
                  pl.BlockSpec((TILE_M, N_IN), lambda i: (i, 0)),
                  _const((N_DEV, D_MODEL, WIDTH))],
        out_specs=(pl.BlockSpec((TILE_M, D_MODEL), lambda i: (i, 0)), _const((1, D_MODEL))),
        args=[dx_next, x, norm_g, dproj, wg_in], sem=("arbitrary",), carry=carry)


def _my_place():
    return lax.axis_index("x"), lax.axis_index("y"), lax.axis_index("c")


def _gather_plan(shards, layer):
    n = len(shards)

    def parts(ins, outs, sems):
        send_sems, recv_sems, local_sems = sems
        x, y, c = _my_place()
        chips = [(1 - x, y), (x, 1 - y), (1 - x, 1 - y)]

        def rows(t, place):
            px, py, pc = place
            return outs[t].at[pl.ds(4 * px + 2 * py + pc, 1)]

        def copy(t, k, block, to, from_src=False):
            return pltpu.make_async_remote_copy(
                src_ref=ins[t].at[pl.ds(layer, 1)] if from_src else rows(t, block), dst_ref=rows(t, block),
                send_sem=send_sems.at[7 * t + k], recv_sem=recv_sems.at[7 * t + k], device_id=to,
                device_id_type=MESH)

        def mine(t):
            return pltpu.make_async_copy(ins[t].at[pl.ds(layer, 1)], rows(t, (x, y, c)), local_sems.at[t])

        return (x, y, c), chips, copy, mine

    def start(ins, outs, sems):
        me, chips, copy, mine = parts(ins, outs, sems)
        x, y, c = me
        for t in range(n):
            mine(t).start()
            copy(t, 0, me, (x, y, 1 - c), from_src=True).start()
            for j, chip in enumerate(chips):
                copy(t, 1 + j, me, (*chip, c), from_src=True).start()

    def finish(ins, outs, sems):
        me, chips, copy, mine = parts(ins, outs, sems)
        x, y, c = me
        sibling = (x, y, 1 - c)
        for t in range(n):
            for j, chip in enumerate(chips):
                copy(t, 1 + j, (*chip, c), me).wait_recv()
                copy(t, 4 + j, (*chip, c), sibling).start()
        for t in range(n):
            copy(t, 0, sibling, me).wait_recv()
            for j, chip in enumerate(chips):
                copy(t, 4 + j, (*chip, 1 - c), me).wait_recv()
            for k in range(7):
                copy(t, k, me, sibling, from_src=k < 4).wait_send()
            mine(t).wait()

    out_shape = [SDS((N_DEV,) + a.shape[1:], a.dtype) for a in shards]
    sems = [pltpu.SemaphoreType.DMA((7 * n,)), pltpu.SemaphoreType.DMA((7 * n,)), pltpu.SemaphoreType.DMA((n,))]
    return _Carried(shards, out_shape, sems, start, finish)


class _Carried:
    def __init__(self, ins, out_shape, sems, start, finish):
        self.ins, self.out_shape, self.sems = list(ins), list(out_shape), list(sems)
        self.start, self.finish = start, finish


def _pcall(body, *, name, grid, in_specs, out_specs, out_shape, args, scratch_shapes=(), aliases=None,
           sem=None, limit=VMEM_LIMIT, carry=None):
    out_shape, out_specs, scratch_shapes = list(out_shape), list(out_specs), list(scratch_shapes)
    n_in, n_out, n_scr = len(args), len(out_shape), len(scratch_shapes)
    if carry is None:
        kern, c_ins, c_out, c_sems = body, [], [], []
    else:
        c_ins, c_out, c_sems = carry.ins, carry.out_shape, carry.sems
        ci, co = len(c_ins), len(c_out)
        steps = tuple(grid)

        def kern(*refs):
            o0 = n_in + ci
            s0 = o0 + n_out + co
            mine = refs[:n_in] + refs[o0:o0 + n_out] + refs[s0:s0 + n_scr]
            theirs = (refs[n_in:o0], refs[o0 + n_out:s0], refs[s0 + n_scr:])
            first = pl.program_id(0) == 0
            last = pl.program_id(0) == steps[0] - 1
            for a in range(1, len(steps)):
                first = jnp.logical_and(first, pl.program_id(a) == 0)
                last = jnp.logical_and(last, pl.program_id(a) == steps[a] - 1)

            @pl.when(first)
            def _():
                carry.start(*theirs)

            body(*mine)

            @pl.when(last)
            def _():
                carry.finish(*theirs)

        sem = ("arbitrary",) * len(steps)
    res = pl.pallas_call(
        kern, name=name, grid=tuple(grid),
        in_specs=list(in_specs) + [ANY] * len(c_ins),
        out_specs=tuple(out_specs + [ANY] * len(c_out)),
        out_shape=tuple(out_shape + c_out),
        scratch_shapes=scratch_shapes + c_sems,
        input_output_aliases=aliases or {},
        compiler_params=_cp(sem, limit),
    )(*args, *c_ins)
    return res[:n_out], res[n_out:]


def _run_carried(name, carry):
    ci, co = len(carry.ins), len(carry.out_shape)

    def body(*refs):
        parts = (refs[:ci], refs[ci:ci + co], refs[ci + co:])
        carry.start(*parts)
        carry.finish(*parts)

    return pl.pallas_call(
        body, name=name, out_shape=tuple(carry.out_shape),
        in_specs=[ANY] * ci, out_specs=tuple([ANY] * co), scratch_shapes=carry.sems,
    )(*carry.ins)


def _sibling_plan(big, small):
    n = len(big)
    n_copies = 4 * n + len(small)

    def copies(ins, outs, sems):
        send_sems, recv_sems = sems
        x, y, c = _my_place()
        pairs = []
        for t, (_, layer) in enumerate(big):
            for s in range(4):
                pairs.append((ins[t].at[layer, pl.ds(4 * (1 - c) + s, 1)], outs[t].at[pl.ds(s, 1)]))
        pairs += list(zip(ins[n:], outs[n:]))
        return [pltpu.make_async_remote_copy(
            src_ref=src, dst_ref=dst, send_sem=send_sems.at[k], recv_sem=recv_sems.at[k],
            device_id=(x, y, 1 - c), device_id_type=MESH) for k, (src, dst) in enumerate(pairs)]

    def start(ins, outs, sems):
        for cp in copies(ins, outs, sems):
            cp.start()

    def finish(ins, outs, sems):
        for cp in copies(ins, outs, sems):
            cp.wait()

    out_shape = [SDS((4,) + a.shape[2:], a.dtype) for a, _ in big] + [SDS(a.shape, a.dtype) for a in small]
    sems = [pltpu.SemaphoreType.DMA((n_copies,)), pltpu.SemaphoreType.DMA((n_copies,))]
    return _Carried([a for a, _ in big] + list(small), out_shape, sems, start, finish)


def _chips_plan(big, small):
    n, n_small = len(big), len(small)
    max_rows = 512
    parts = [max(1, a.shape[1] // max_rows) for a in big]
    n_copies = 3 * (sum(parts) + n_small)

    def copies(ins, outs, sems, landing):
        send_sems, recv_sems, local_sems = sems
        x, y, c = _my_place()
        my_chip = 2 * x + y
        chips = [(1 - x, y), (x, 1 - y), (1 - x, 1 - y)]
        remote, local = [], []
        for chip in chips:
            to = 2 * chip[0] + chip[1]
            slot = to if landing else my_chip
            pairs = []
            for t in range(n):
                rows_per = big[t].shape[1] // parts[t]
                for p in range(parts[t]):
                    rows = pl.ds(p * rows_per, rows_per)
                    pairs.append((ins[t].at[to, rows], outs[t].at[slot, rows]))
            pairs += [(ins[t], outs[t].at[slot]) for t in range(n, n + n_small)]
            for src, dst in pairs:
                k = len(remote)
                remote.append(pltpu.make_async_remote_copy(
                    src_ref=src, dst_ref=dst, send_sem=send_sems.at[k], recv_sem=recv_sems.at[k],
                    device_id=(*chip, c), device_id_type=MESH))
        for t in range(n):
            local.append(pltpu.make_async_copy(ins[t].at[my_chip], outs[t].at[my_chip], local_sems.at[t]))
        for t in range(n, n + n_small):
            local.append(pltpu.make_async_copy(ins[t], outs[t].at[my_chip], local_sems.at[t]))
        return remote + local

    def start(ins, outs, sems):
        for cp in copies(ins, outs, sems, landing=False):
            cp.start()

    def finish(ins, outs, sems):
        for cp in copies(ins, outs, sems, landing=True):
            cp.wait()

    out_shape = [SDS(a.shape, a.dtype) for a in big] + [SDS((N_CHIP,) + a.shape, a.dtype) for a in small]
    sems = [pltpu.SemaphoreType.DMA((n_copies,)), pltpu.SemaphoreType.DMA((n_copies,)),
            pltpu.SemaphoreType.DMA((n + n_small,))]
    return _Carried(list(big) + list(small), out_shape, sems, start, finish)


def _all_plan(small):
    n = len(small)
    masks = [(m >> 2 & 1, m >> 1 & 1, m & 1) for m in range(1, N_DEV)]

    def copies(ins, outs, sems, landing):
        send_sems, recv_sems, local_sems = sems
        x, y, c = _my_place()
        me = 4 * x + 2 * y + c
        flip = lambda v, bit: 1 - v if bit else v
        remote = []
        for fx, fy, fc in masks:
            peer = (flip(x, fx), flip(y, fy), flip(c, fc))
            slot = 4 * peer[0] + 2 * peer[1] + peer[2] if landing else me
            for t in range(n):
                k = len(remote)
                remote.append(pltpu.make_async_remote_copy(
                    src_ref=ins[t], dst_ref=outs[t].at[slot], send_sem=send_sems.at[k], recv_sem=recv_sems.at[k],
                    device_id=peer, device_id_type=MESH))
        local = [pltpu.make_async_copy(ins[t], outs[t].at[me], local_sems.at[t]) for t in range(n)]
        return remote + local

    def start(ins, outs, sems):
        for cp in copies(ins, outs, sems, landing=False):
            cp.start()

    def finish(ins, outs, sems):
        for cp in copies(ins, outs, sems, landing=True):
            cp.wait()

    out_shape = [SDS((N_DEV,) + a.shape, a.dtype) for a in small]
    sems = [pltpu.SemaphoreType.DMA((7 * n,)), pltpu.SemaphoreType.DMA((7 * n,)), pltpu.SemaphoreType.DMA((n,))]
    return _Carried(list(small), out_shape, sems, start, finish)


def _join(*plans):
    plans = [p for p in plans if p is not None]
    if len(plans) <= 1:
        return plans[0] if plans else None

    def each(fn_name, ins, outs, sems):
        i = o = s = 0
        for p in plans:
            ni, no, ns = len(p.ins), len(p.out_shape), len(p.sems)
            getattr(p, fn_name)(ins[i:i + ni], outs[o:o + no], sems[s:s + ns])
            i, o, s = i + ni, o + no, s + ns

    return _Carried(sum((p.ins for p in plans), []), sum((p.out_shape for p in plans), []),
                    sum((p.sems for p in plans), []),
                    lambda i, o, s: each("start", i, o, s), lambda i, o, s: each("finish", i, o, s))


def _row_block(rows):
    return rows if rows <= 256 else 256


def _add_own(tag, core, g, layer, got):
    _, r, c = got.shape
    rb = _row_block(r)

    def body(core_ref, a_ref, b_ref, o_ref):
        o_ref[...] = (a_ref[...] + b_ref[...]).astype(o_ref.dtype)

    return pl.pallas_call(
        body, name=f"add_{tag}", out_shape=SDS(got.shape, BF16),
        grid_spec=pltpu.PrefetchScalarGridSpec(
            num_scalar_prefetch=1, grid=(4, r // rb),
            in_specs=[pl.BlockSpec((None, None, rb, c), lambda s, j, core: (layer, 4 * core[0] + s, j, 0)),
                      pl.BlockSpec((None, rb, c), lambda s, j, core: (s, j, 0))],
            out_specs=pl.BlockSpec((None, rb, c), lambda s, j, core: (s, j, 0))),
        compiler_params=_cp(("parallel", "parallel")),
    )(core, g, got)


def _add_lists(tag, own, got, grid=None, specs=None):
    n = len(own)

    def body(*refs):
        for a, b, o in zip(refs[:n], refs[n:2 * n], refs[2 * n:]):
            o[...] = a[...] + b[...]

    kw = {}
    if grid is not None:
        kw = dict(grid=grid, in_specs=list(specs) * 2, out_specs=tuple(specs),
                  compiler_params=_cp(("parallel",) * len(grid)))
    return pl.pallas_call(
        body, name=f"add_{tag}", out_shape=tuple(SDS(a.shape, a.dtype) for a in own), **kw)(*own, *got)


def _adamw_math(w, g, m, v):
    m = ADAM_B1 * m + (1.0 - ADAM_B1) * g
    v = ADAM_B2 * v + (1.0 - ADAM_B2) * (g * g)
    m_hat = m / (1.0 - ADAM_B1 ** ADAM_STEP)
    v_hat = v / (1.0 - ADAM_B2 ** ADAM_STEP)
    delta = -ADAM_LR * (m_hat / (jnp.sqrt(v_hat) + ADAM_EPS) + ADAM_WD * w)
    return delta, m, v


def _sum_slots_adamw(tag, slots, w, m, v):
    _, r, c = slots[0].shape
    rb = _row_block(r)

    def body(s0_ref, s1_ref, w_ref, m_ref, v_ref, g_ref, d_ref, nm_ref, nv_ref):
        first = pl.program_id(1) == 0
        g = _sum4([jnp.where(first, s0_ref[k], s1_ref[k]).astype(F32) for k in range(N_CHIP)])
        delta, nm, nv = _adamw_math(w_ref[...], g, m_ref[...], v_ref[...])
        g_ref[...] = g
        d_ref[...] = delta
        nm_ref[...] = nm
        nv_ref[...] = nv

    spec = pl.BlockSpec((None, rb, c), lambda j, l: (l, j, 0))
    sspec = pl.BlockSpec((N_CHIP, rb, c), lambda j, l: (0, j, 0))
    s = SDS((DEPTH, r, c), F32)
    return pl.pallas_call(
        body, name=f"adamw_{tag}", out_shape=(s, s, s, s),
        grid=(r // rb, DEPTH), in_specs=[sspec, sspec, spec, spec, spec], out_specs=(spec, spec, spec, spec),
        compiler_params=_cp(("parallel", "arbitrary")),
    )(*slots, w, m, v)


def _adamw_small(tag, entries, grid=None):
    flat_in, in_specs, out_shape, out_specs, layout = [], [], [], [], []
    for slots, w, m, v, slot_spec, w_spec in entries:
        per_layer = isinstance(slots, (list, tuple))
        n_slot = len(slots) if per_layer else 1
        flat_in += (list(slots) if per_layer else [slots]) + [w, m, v]
        in_specs += [slot_spec] * n_slot + [w_spec] * 3
        out_shape += [SDS(w.shape, F32)] * 4
        out_specs += [w_spec] * 4
        layout.append((per_layer, n_slot))
    n_in = len(flat_in)

    def body(*refs):
        i, o = 0, n_in
        for per_layer, n_slot in layout:
            s_refs = refs[i:i + n_slot]
            w_ref, m_ref, v_ref = refs[i + n_slot:i + n_slot + 3]
            outs = refs[o:o + 4]
            if per_layer:
                for l, s_ref in enumerate(s_refs):
                    at = (slice(l, l + 1),) if len(w_ref.shape) == 2 else (l,)
                    g = _sum_slots(s_ref)
                    res = (g,) + _adamw_math(w_ref[at], g, m_ref[at], v_ref[at])
                    for o_ref, val in zip(outs, res):
                        o_ref[at] = val
            else:
                g = _sum_slots(s_refs[0])
                res = (g,) + _adamw_math(w_ref[...], g, m_ref[...], v_ref[...])
                for o_ref, val in zip(outs, res):
                    o_ref[...] = val
            i += n_slot + 3
            o += 4

    kw = {}
    if grid is not None:
        kw = dict(grid=grid, in_specs=in_specs, out_specs=tuple(out_specs),
                  compiler_params=_cp(("parallel",) * len(grid)))
    res = pl.pallas_call(body, name=f"adamw_{tag}", out_shape=tuple(out_shape), **kw)(*flat_in)
    return [tuple(res[4 * e:4 * e + 4]) for e in range(len(entries))]


def kernel(x, norm_g, w_in, b_in, ssm_log_dt, ssm_lam_re, ssm_lam_im, ssm_b_re, ssm_b_im, ssm_c_re, ssm_c_im, ssm_d, ssm_w_glu, ssm_b_glu, pool_w, pool_scale, w_branch_a, w_branch_b, w_out, final_norm_g, loss_target, m_norm_g, m_w_in, m_b_in, m_ssm_log_dt, m_ssm_lam_re, m_ssm_lam_im, m_ssm_b_re, m_ssm_b_im, m_ssm_c_re, m_ssm_c_im, m_ssm_d, m_ssm_w_glu, m_ssm_b_glu, m_pool_w, m_pool_scale, m_w_branch_a, m_w_branch_b, m_w_out, m_final_norm_g, v_norm_g, v_w_in, v_b_in, v_ssm_log_dt, v_ssm_lam_re, v_ssm_lam_im, v_ssm_b_re, v_ssm_b_im, v_ssm_c_re, v_ssm_c_im, v_ssm_d, v_ssm_w_glu, v_ssm_b_glu, v_pool_w, v_pool_scale, v_w_branch_a, v_w_branch_b, v_w_out, v_final_norm_g):
    weights = dict(norm_g=norm_g, w_in=w_in, b_in=b_in, ssm_log_dt=ssm_log_dt, ssm_lam_re=ssm_lam_re,
                   ssm_lam_im=ssm_lam_im, ssm_b_re=ssm_b_re, ssm_b_im=ssm_b_im, ssm_c_re=ssm_c_re,
                   ssm_c_im=ssm_c_im, ssm_d=ssm_d, ssm_w_glu=ssm_w_glu, ssm_b_glu=ssm_b_glu, pool_w=pool_w,
                   pool_scale=pool_scale, w_branch_a=w_branch_a, w_branch_b=w_branch_b, w_out=w_out,
                   final_norm_g=final_norm_g.reshape(1, D_MODEL))
    mom_m = dict(norm_g=m_norm_g, w_in=m_w_in, b_in=m_b_in, ssm_log_dt=m_ssm_log_dt, ssm_lam_re=m_ssm_lam_re,
                 ssm_lam_im=m_ssm_lam_im, ssm_b_re=m_ssm_b_re, ssm_b_im=m_ssm_b_im, ssm_c_re=m_ssm_c_re,
                 ssm_c_im=m_ssm_c_im, ssm_d=m_ssm_d, ssm_w_glu=m_ssm_w_glu, ssm_b_glu=m_ssm_b_glu,
                 pool_w=m_pool_w, pool_scale=m_pool_scale, w_branch_a=m_w_branch_a, w_branch_b=m_w_branch_b,
                 w_out=m_w_out, final_norm_g=m_final_norm_g.reshape(1, D_MODEL))
    mom_v = dict(norm_g=v_norm_g, w_in=v_w_in, b_in=v_b_in, ssm_log_dt=v_ssm_log_dt, ssm_lam_re=v_ssm_lam_re,
                 ssm_lam_im=v_ssm_lam_im, ssm_b_re=v_ssm_b_re, ssm_b_im=v_ssm_b_im, ssm_c_re=v_ssm_c_re,
                 ssm_c_im=v_ssm_c_im, ssm_d=v_ssm_d, ssm_w_glu=v_ssm_w_glu, ssm_b_glu=v_ssm_b_glu,
                 pool_w=v_pool_w, pool_scale=v_pool_scale, w_branch_a=v_w_branch_a, w_branch_b=v_w_branch_b,
                 w_out=v_w_out, final_norm_g=v_final_norm_g.reshape(1, D_MODEL))
    order = ["norm_g", "w_in", "b_in", "ssm_log_dt", "ssm_lam_re", "ssm_lam_im", "ssm_b_re", "ssm_b_im",
             "ssm_c_re", "ssm_c_im", "ssm_d", "ssm_w_glu", "ssm_b_glu", "pool_w", "pool_scale", "w_branch_a",
             "w_branch_b", "w_out", "final_norm_g"]
    big_names = ["w_in", "ssm_w_glu", "w_branch_a", "w_branch_b", "w_out"]

    log_dt3 = ssm_log_dt.reshape(DEPTH, N_GROUP, 1)
    b_t = lambda a: a.transpose(0, 1, 3, 2)
    for d in (weights, mom_m, mom_v):
        d["ssm_b_re"], d["ssm_b_im"] = b_t(d["ssm_b_re"]), b_t(d["ssm_b_im"])
    bt_re, bt_im = weights["ssm_b_re"], weights["ssm_b_im"]
    abar_re, abar_im, bbt_re, bbt_im = _s5_params(log_dt3, ssm_lam_re, ssm_lam_im, bt_re, bt_im)
    s5_args = (bbt_re, bbt_im, ssm_c_re, ssm_c_im, abar_re, abar_im, ssm_d)

    w16 = {n: weights[n].astype(BF16) for n in big_names}
    rest = [w16[n] for n in big_names[1:]]
    wg_in = [None, None]
    wg_rest = [None, None]
    (wg_in[0],) = _run_carried("gather_w_in_l0", _gather_plan([w16["w_in"]], 0))
    xs = [x.reshape(SEQ, D_MODEL)]
    saved = []
    for l in range(DEPTH):
        proj, moved = _norm_proj(l, xs[l], norm_g, wg_in[l], b_in, carry=_gather_plan(rest, 0) if l == 0 else None)
        if l == 0:
            wg_rest[0] = moved
        (states, y0), moved = _s5_scan_fwd(
            l, proj, *s5_args, carry=_gather_plan([w16["w_in"]], 1) if l == 0 else None)
        if l == 0:
            (wg_in[1],) = moved
        pooled = _pool_fwd(l, proj)
        wg_glu, wg_a, wg_b, wg_out = wg_rest[l]
        x_next, moved = _mix_fwd(l, xs[l], proj, y0, pooled, wg_glu, ssm_b_glu, pool_w, pool_scale, wg_a, wg_b,
                                 wg_out, carry=_gather_plan(rest, 1) if l == 0 else None)
        if l == 0:
            wg_rest[1] = moved
        xs.append(x_next)
        saved.append((proj, states, y0, pooled))

    dx, loss_part, g_final = _loss_head(xs[DEPTH], loss_target.reshape(SEQ, D_MODEL), weights["final_norm_g"])
    loss = lax.psum(loss_part[0, 0], ("x", "y", "c"))

    core = lax.axis_index("c").astype(jnp.int32).reshape(1)
    vec_names = ["norm_g", "b_in", "ssm_d", "ssm_b_glu", "pool_scale", "ssm_log_dt"]
    s5_names = ["ssm_log_dt", "ssm_lam_re", "ssm_lam_im", "ssm_b_re", "ssm_b_im"]
    mat_names = ["pool_w", "ssm_c_re", "ssm_c_im", "ssm_b_re", "ssm_b_im"]
    lane_sparse = ("ssm_c_re", "ssm_c_im", "ssm_b_re", "ssm_b_im")

    def dense(key, a):
        return a.reshape(-1, 128) if key[0] in lane_sparse else a

    def undense(key, slots):
        return slots.reshape((N_CHIP, N_GROUP, GROUP_W, STATE)) if key[0] in lane_sparse else slots

    def add_small(tag, keys, own, got):
        out = [None] * len(keys)
        whole = [i for i, k in enumerate(keys) if k[0] not in mat_names]
        tiled = [i for i, k in enumerate(keys) if k[0] in mat_names]
        if whole:
            for i, r in zip(whole, _add_lists(f"{tag}_a", [own[i] for i in whole], [got[i] for i in whole])):
                out[i] = r
        if tiled:
            specs = [pl.BlockSpec((1, POOL_GROUP, POOL_GROUP), lambda j: (j, 0, 0)) if keys[i][0] == "pool_w"
                     else pl.BlockSpec((own[i].shape[0] // N_CHUNK, 128), lambda j: (j, 0)) for i in tiled]
            for i, r in zip(tiled, _add_lists(f"{tag}_b", [own[i] for i in tiled], [got[i] for i in tiled],
                                              grid=(N_CHUNK,), specs=specs)):
                out[i] = r
        return out

    sm = {("final_norm_g", None): g_final}
    slots = {}
    grads = dict.fromkeys(big_names)

    class Wave:
        def __init__(self, tag, layer, big, keys):
            self.tag, self.layer, self.big, self.keys = tag, layer, big, keys

        def to_sibling(self):
            self.own = [dense(k, sm[k]) for k in self.keys]
            return _sibling_plan([(grads[n], self.layer) for n in self.big], self.own)

        def add(self, moved):
            nb = len(self.big)
            self.chip_big = [_add_own(f"{self.tag}_{n}", core, grads[n], self.layer, b)
                             for n, b in zip(self.big, moved[:nb])]
            self.chip_small = add_small(self.tag, self.keys, self.own, moved[nb:])

        def to_chips(self, big=None, small=True):
            self.sent = list(self.big if big is None else big), small
            return _chips_plan([self.chip_big[self.big.index(n)] for n in self.sent[0]],
                               self.chip_small if small else [])

        def landed(self, moved):
            names, small = self.sent
            for n, s in zip(names, moved[:len(names)]):
                slots[(n, self.layer)] = s
            if small:
                for k, s in zip(self.keys, moved[len(names):]):
                    slots[k] = undense(k, s)
            return moved[len(names) + (len(self.keys) if small else 0):]

    def s5_param_grads(l, g_abar_re, g_abar_im, g_bbt_re, g_bbt_im):
        g = _s5_params_bwd(l, log_dt3, ssm_lam_re, ssm_lam_im, bt_re, bt_im, g_abar_re, g_abar_im, g_bbt_re, g_bbt_im)
        sm[("ssm_log_dt", l)] = g[0].reshape(1, N_GROUP)
        for n, a in zip(s5_names[1:], g[1:]):
            sm[(n, l)] = a

    small1 = ["b_in", "ssm_d", "ssm_b_glu", "pool_scale", "pool_w", "ssm_c_re", "ssm_c_im"] + s5_names
    w1 = Wave("chip1", 1, list(big_names), [(n, 1) for n in small1] + [("final_norm_g", None)])
    early = Wave("chip0e", 0, big_names[1:], [("pool_w", 0), ("pool_scale", 0), ("ssm_b_glu", 0)])
    mid = Wave("chip0m", 0, [], [(n, 0) for n in ["ssm_c_re", "ssm_c_im", "ssm_d"] + s5_names] + [("norm_g", 1)])
    late = Wave("chip0l", 0, ["w_in"], [("b_in", 0)])

    mix_prev, gw_in = None, None
    for l in reversed(range(DEPTH)):
        proj, states, y0, pooled = saved[l]
        wg_glu, wg_a, wg_b, wg_out = wg_rest[l]
        res, moved = _mix_bwd(l, dx, proj, y0, pooled, wg_glu, ssm_b_glu, pool_w, pool_scale, wg_a, wg_b, wg_out,
                              mix_prev, carry=None if l == 1 else w1.to_chips(big=["w_in"], small=False))
        if l == 0:
            w1.landed(moved)
        dproj, dy0, dpooled = res[:3]
        mix_prev = list(res[3:7])
        grads["w_out"], grads["w_branch_a"], grads["w_branch_b"], grads["ssm_w_glu"] = mix_prev
        sm[("pool_w", l)], sm[("pool_scale", l)], sm[("ssm_b_glu", l)] = res[7:]
        dproj = _pool_bwd(l, dpooled, dproj)
        carry = None if l == 1 else _join(w1.to_chips(big=big_names[1:]), early.to_sibling())
        res, moved = _s5_scan_bwd(l, dy0, proj, states, *s5_args, dproj, carry=carry)
        if l == 0:
            early.add(w1.landed(moved))
        dproj, g_bbt_re, g_bbt_im, sm[("ssm_c_re", l)], sm[("ssm_c_im", l)], g_abar_re, g_abar_im, sm[("ssm_d", l)] = res
        s5_param_grads(l, g_abar_re, g_abar_im, g_bbt_re, g_bbt_im)
        carry = None if l == 1 else _join(early.to_chips(), mid.to_sibling())
        (gw_in, sm[("b_in", l)]), moved = _proj_wgrad(l, xs[l], norm_g, dproj, gw_in, carry=carry)
        grads["w_in"] = gw_in
        if l == 0:
            mid.add(early.landed(moved))
        carry = w1.to_sibling() if l == 1 else _join(mid.to_chips(), late.to_sibling())
        (dx, sm[("norm_g", l)]), moved = _proj_dgrad(l, dx, xs[l], norm_g, dproj, wg_in[l], carry=carry)
        if l == 1:
            w1.add(moved)
        else:
            late.add(mid.landed(moved))
    grad_x = dx.reshape(1, SEQ, D_MODEL)
    moved = late.landed(_run_carried("exchange_last", _join(late.to_chips(), _all_plan([sm[("norm_g", 0)]]))))
    slots[("norm_g", 0)] = moved[0]

    res = {}
    for n in big_names:
        res[n] = _sum_slots_adamw(n, [slots[(n, l)] for l in range(DEPTH)], weights[n], mom_m[n], mom_v[n])
    per_layer = lambda n: [slots[(n, l)] for l in range(DEPTH)]
    names_a = vec_names + ["ssm_lam_re", "ssm_lam_im"]
    entries_a = [(per_layer(n), weights[n], mom_m[n], mom_v[n], None, None) for n in names_a]
    n = "final_norm_g"
    entries_a.append((slots[(n, None)], weights[n], mom_m[n], mom_v[n], None, None))
    out_a = _adamw_small("small_a", entries_a)
    for n, r in zip(names_a + ["final_norm_g"], out_a):
        res[n] = r
    res["final_norm_g"] = tuple(a.reshape(D_MODEL) for a in res["final_norm_g"])
    pw_s = pl.BlockSpec((N_CHIP, 1, POOL_GROUP, POOL_GROUP), lambda j: (0, j, 0, 0))
    pw_w = pl.BlockSpec((DEPTH, 1, POOL_GROUP, POOL_GROUP), lambda j: (0, j, 0, 0))
    c_s = pl.BlockSpec((N_CHIP, CH_G, GROUP_W, STATE), lambda j: (0, j, 0, 0))
    c_w = pl.BlockSpec((DEPTH, CH_G, GROUP_W, STATE), lambda j: (0, j, 0, 0))
    entries_b = [(per_layer(n), weights[n], mom_m[n], mom_v[n], pw_s if n == "pool_w" else c_s,
                  pw_w if n == "pool_w" else c_w) for n in mat_names]
    out_b = _adamw_small("small_b", entries_b, grid=(N_CHUNK,))
    for n, r in zip(mat_names, out_b):
        res[n] = tuple(b_t(a) for a in r) if n in ("ssm_b_re", "ssm_b_im") else r

    outs = [loss, grad_x]
    for i in range(4):
        outs += [res[n][i] for n in order]
    return tuple(outs)
```

```python
import math

import jax
import jax.numpy as jnp
from jax import lax
from jax.experimental import pallas as pl
from jax.experimental.pallas import tpu as pltpu

F32 = jnp.float32
BF16 = jnp.bfloat16

SEQ = 2048
D_MODEL = 1024
N_IN = 4096
WIDTH = 512
N_GROUP = 32
GROUP_W = 16
STATE = 64
N_STATE = N_GROUP * STATE
N_CHUNK = 4
CH_G = N_GROUP // N_CHUNK
CH_W = WIDTH // N_CHUNK
CH_S = N_STATE // N_CHUNK
N_DEV = 8
N_CHIP = 4
POOL_WINDOWS = (2, 4, 8, 16)
POOL_GROUP = 128
EPS = 1e-6
DEPTH = 2

ADAM_LR = 0.001
ADAM_B1 = 0.9
ADAM_B2 = 0.999
ADAM_EPS = 1e-08
ADAM_WD = 0.01
ADAM_STEP = 10

TILE_M = 256
ROW_BLK = 512
VMEM_LIMIT = 48 * 1024 * 1024
VMEM_LIMIT_BIG = 60 * 1024 * 1024
MESH = pl.DeviceIdType.MESH
ANY = pl.BlockSpec(memory_space=pl.ANY)

GELU_C = math.sqrt(2.0 / math.pi)
GELU_A = 0.044715

SDS = jax.ShapeDtypeStruct


def _cp(sem=None, limit=VMEM_LIMIT):
    return pltpu.CompilerParams(dimension_semantics=sem, vmem_limit_bytes=limit)


def _dot(a, b):
    return jnp.dot(a, b, preferred_element_type=F32)


def _dot_nt(a, b):
    return lax.dot_general(a, b, (((1,), (1,)), ((), ())), preferred_element_type=F32)


def _dot_tn(a, b):
    return lax.dot_general(a, b, (((0,), (0,)), ((), ())), preferred_element_type=F32)


def _sig(x):
    return jax.nn.sigmoid(x)


def _rms(x):
    rs = lax.rsqrt(jnp.mean(x * x, axis=-1, keepdims=True) + EPS)
    return rs, x * rs


def _slot(n):
    return 4 * (n % 2) + n // 2


def _const(shape):
    n = len(shape)
    return pl.BlockSpec(shape, lambda *_: (0,) * n)


def _sum4(p):
    return (p[0] + p[1]) + (p[2] + p[3])


def _sum_slots(s_ref):
    vals = [s_ref[k] for k in range(s_ref.shape[0])]
    while len(vals) > 1:
        vals = [vals[i] + vals[i + 1] for i in range(0, len(vals), 2)]
    return vals[0]


def _s5_param_fn(log_dt, lam_re, lam_im, bt_re, bt_im):
    dt = jnp.exp(log_dt)
    mag = jnp.exp(lam_re * dt)
    ang = lam_im * dt
    abar_re = mag * jnp.cos(ang)
    abar_im = mag * jnp.sin(ang)
    num_re = abar_re - 1.0
    num_im = abar_im
    den = lam_re * lam_re + lam_im * lam_im
    coef_re = (num_re * lam_re + num_im * lam_im) / den
    coef_im = (num_im * lam_re - num_re * lam_im) / den
    bbar_re = coef_re[..., None, :] * bt_re - coef_im[..., None, :] * bt_im
    bbar_im = coef_re[..., None, :] * bt_im + coef_im[..., None, :] * bt_re
    return abar_re, abar_im, bbar_re, bbar_im


def _s5_params(log_dt, lam_re, lam_im, bt_re, bt_im):
    def body(ld, lr, li, br, bi, o_ar, o_ai, o_br, o_bi):
        ar, ai, bbr, bbi = _s5_param_fn(ld[...], lr[...], li[...], br[...], bi[...])
        o_ar[...] = ar
        o_ai[...] = ai
        o_br[...] = bbr
        o_bi[...] = bbi

    return pl.pallas_call(
        body, name="s5_params",
        out_shape=(SDS(lam_re.shape, F32), SDS(lam_re.shape, F32), SDS(bt_re.shape, F32), SDS(bt_re.shape, F32)),
    )(log_dt, lam_re, lam_im, bt_re, bt_im)


def _s5_params_bwd(layer, log_dt, lam_re, lam_im, bt_re, bt_im, g_ar, g_ai, g_br, g_bi):
    def body(ld, lr, li, br, bi, car, cai, cbr, cbi, o_ld, o_lr, o_li, o_br, o_bi):
        _, vjp = jax.vjp(_s5_param_fn, ld[...], lr[...], li[...], br[...], bi[...])
        d_ld, d_lr, d_li, d_br, d_bi = vjp((car[...], cai[...], cbr[...], cbi[...]))
        o_ld[...] = d_ld
        o_lr[...] = d_lr
        o_li[...] = d_li
        o_br[...] = d_br
        o_bi[...] = d_bi

    one = lambda shape: pl.BlockSpec((None,) + shape, lambda i: (layer,) + (0,) * len(shape))
    whole = lambda shape: _const(shape)
    vec, lam, mat = (N_GROUP, 1), (N_GROUP, STATE), (N_GROUP, GROUP_W, STATE)
    return pl.pallas_call(
        body, name=f"s5_params_bwd_l{layer}", grid=(1,),
        in_specs=[one(vec), one(lam), one(lam), one(mat), one(mat), whole(lam), whole(lam), whole(mat), whole(mat)],
        out_specs=(whole(vec), whole(lam), whole(lam), whole(mat), whole(mat)),
        out_shape=(SDS(vec, F32), SDS(lam, F32), SDS(lam, F32), SDS(mat, F32), SDS(mat, F32)),
    )(log_dt, lam_re, lam_im, bt_re, bt_im, g_ar, g_ai, g_br, g_bi)


def _norm_proj(layer, x, norm_g, wg_in, b_in, carry=None):
    def body(x_ref, g_ref, w_ref, b_ref, o_ref):
        _, xn = _rms(x_ref[...])
        h = (xn * g_ref[layer:layer + 1, :]).astype(BF16)
        for k in range(N_DEV):
            cols = slice(k * WIDTH, (k + 1) * WIDTH)
            o_ref[:, cols] = _dot(h, w_ref[k]) + b_ref[layer:layer + 1, cols]

    (proj,), moved = _pcall(
        body, name=f"norm_proj_l{layer}",
        out_shape=[SDS((SEQ, N_IN), F32)],
        grid=(SEQ // TILE_M,),
        in_specs=[pl.BlockSpec((TILE_M, D_MODEL), lambda i: (i, 0)),
                  _const((DEPTH, D_MODEL)),
                  _const((N_DEV, D_MODEL, WIDTH)),
                  _const((DEPTH, N_IN))],
        out_specs=[pl.BlockSpec((TILE_M, N_IN), lambda i: (i, 0))],
        args=[x, norm_g, wg_in, b_in], sem=("parallel",), carry=carry)
    return proj, moved


TIME_BLK = 512
N_TBLK = SEQ // TIME_BLK
N_PANEL = CH_S // 128
STATE_SHAPE = (N_PANEL, SEQ * 8, 128)


def _s5_layer_specs(layer):
    mat = lambda: pl.BlockSpec((None, N_GROUP, GROUP_W, STATE), lambda i: (layer, 0, 0, 0))
    ab = lambda: pl.BlockSpec((None, N_GROUP, STATE), lambda i: (layer, 0, 0))
    return [mat(), mat(), mat(), mat(), ab(), ab(), _const((DEPTH, WIDTH))]


def _s5_layer_scratch():
    return [pltpu.VMEM((N_CHUNK, CH_W, CH_S), BF16)] * 4 + [pltpu.VMEM((8, CH_S), F32)] * 2


def _s5_layer_fill(btre_ref, btim_ref, cre_ref, cim_ref, are_ref, aim_ref, bdre, bdim, ctre, ctim, a1, a2):
    for m in (bdre, bdim, ctre, ctim):
        m[...] = jnp.zeros_like(m)
    for grp in range(N_GROUP):
        k, g = divmod(grp, CH_G)
        rows = slice(g * GROUP_W, (g + 1) * GROUP_W)
        cols = slice(g * STATE, (g + 1) * STATE)
        bdre[k, rows, cols] = btre_ref[grp].astype(BF16)
        bdim[k, rows, cols] = btim_ref[grp].astype(BF16)
        ctre[k, rows, cols] = cre_ref[grp].astype(BF16)
        ctim[k, rows, cols] = cim_ref[grp].astype(BF16)
        ar = are_ref[grp:grp + 1, :]
        ai = aim_ref[grp:grp + 1, :]
        a1[k:k + 1, cols] = ar
        a1[N_CHUNK + k:N_CHUNK + k + 1, cols] = ar
        a2[k:k + 1, cols] = -ai
        a2[N_CHUNK + k:N_CHUNK + k + 1, cols] = ai


SCAN_UNROLL = 8


def _panels(tile):
    return [tile[:, p * 128:(p + 1) * 128] for p in range(N_PANEL)]


def _rows_load(ref, row):
    return jnp.concatenate([ref[p, pl.ds(row, TIME_BLK, stride=8), :] for p in range(N_PANEL)], axis=1)


def _rows_store(ref, row, val):
    for p in range(N_PANEL):
        ref[p, pl.ds(row, TIME_BLK, stride=8), :] = val[:, p * 128:(p + 1) * 128]


def _s5_scan_fwd(layer, proj, bbt_re, bbt_im, c_re, c_im, abar_re, abar_im, d_skip, carry=None):
    def body(u_ref, btre_ref, btim_ref, cre_ref, cim_ref, are_ref, aim_ref, d_ref, s_ref, y_ref,
             bdre, bdim, ctre, ctim, a1, a2, state):
        @pl.when(pl.program_id(0) == 0)
        def _():
            _s5_layer_fill(btre_ref, btim_ref, cre_ref, cim_ref, are_ref, aim_ref, bdre, bdim, ctre, ctim, a1, a2)
            state[...] = jnp.zeros_like(state)

        for k in range(N_CHUNK):
            ub = u_ref[:, k * CH_W:(k + 1) * CH_W].astype(BF16)
            _rows_store(s_ref, k, _dot(ub, bdre[k]))
            _rows_store(s_ref, N_CHUNK + k, _dot(ub, bdim[k]))
        m1 = _panels(a1[...])
        m2 = _panels(a2[...])

        def steps(n, tile):
            for r in range(SCAN_UNROLL):
                rows = pl.ds(pl.multiple_of((n * SCAN_UNROLL + r) * 8, 8), 8)
                tile = [m1[p] * tile[p] + m2[p] * pltpu.roll(tile[p], N_CHUNK, 0) + s_ref[p, rows, :]
                        for p in range(N_PANEL)]
                for p in range(N_PANEL):
                    s_ref[p, rows, :] = tile[p]
            return tile

        tile = lax.fori_loop(0, TIME_BLK // SCAN_UNROLL, steps, _panels(state[...]))
        state[...] = jnp.concatenate(tile, axis=1)
        d = d_ref[layer:layer + 1, :]
        for k in range(N_CHUNK):
            cols = slice(k * CH_W, (k + 1) * CH_W)
            y = (_dot_nt(_rows_load(s_ref, k).astype(BF16), ctre[k])
                 - _dot_nt(_rows_load(s_ref, N_CHUNK + k).astype(BF16), ctim[k]))
            y_ref[:, cols] = y + d[:, cols] * u_ref[:, cols]

    return _pcall(
        body, name=f"s5_fwd_l{layer}",
        out_shape=(SDS(STATE_SHAPE, F32), SDS((SEQ, WIDTH), F32)),
        grid=(N_TBLK,),
        in_specs=[pl.BlockSpec((TIME_BLK, WIDTH), lambda i: (i, 0))] + _s5_layer_specs(layer),
        out_specs=(pl.BlockSpec((N_PANEL, TIME_BLK * 8, 128), lambda i: (0, i, 0)),
                   pl.BlockSpec((TIME_BLK, WIDTH), lambda i: (i, 0))),
        scratch_shapes=_s5_layer_scratch() + [pltpu.VMEM((8, CH_S), F32)],
        args=[proj, bbt_re, bbt_im, c_re, c_im, abar_re, abar_im, d_skip], sem=("arbitrary",), carry=carry)


def _s5_scan_bwd(layer, dy0, proj, states, bbt_re, bbt_im, c_re, c_im, abar_re, abar_im, d_skip, dproj,
                 carry=None):
    def body(dy_ref, u_ref, s_ref, sprev_ref, btre_ref, btim_ref, cre_ref, cim_ref, are_ref, aim_ref, d_ref, _,
             du_ref, gbre_ref, gbim_ref, gcre_ref, gcim_ref, gare_ref, gaim_ref, gd_ref,
             lam_ref, bdre, bdim, ctre, ctim, a1, a2, state, acc1, acc2, gbre, gbim, gcre, gcim, gd):
        step_id = pl.program_id(0)

        @pl.when(step_id == 0)
        def _():
            _s5_layer_fill(btre_ref, btim_ref, cre_ref, cim_ref, are_ref, aim_ref, bdre, bdim, ctre, ctim, a1, a2)
            for r in (state, acc1, acc2, gbre, gbim, gcre, gcim, gd):
                r[...] = jnp.zeros_like(r)

        for k in range(N_CHUNK):
            dyb = dy_ref[:, k * CH_W:(k + 1) * CH_W].astype(BF16)
            _rows_store(lam_ref, k, _dot(dyb, ctre[k]))
            _rows_store(lam_ref, N_CHUNK + k, -_dot(dyb, ctim[k]))
            gcre[k] += _dot_tn(dyb, _rows_load(s_ref, k).astype(BF16))
            gcim[k] -= _dot_tn(dyb, _rows_load(s_ref, N_CHUNK + k).astype(BF16))

        m1 = _panels(a1[...])
        m2 = _panels(-a2[...])
        has_before = (step_id < N_TBLK - 1).astype(F32)

        def one(t8, c, first_token):
            tile, swapped, p1, p2 = c
            rows = pl.ds(t8, 8)
            tile = [m1[p] * tile[p] + m2[p] * swapped[p] + lam_ref[p, rows, :] for p in range(N_PANEL)]
            swapped = [pltpu.roll(tile[p], N_CHUNK, 0) for p in range(N_PANEL)]
            for p in range(N_PANEL):
                lam_ref[p, rows, :] = tile[p]
            if first_token:
                before = [sprev_ref[p] * has_before for p in range(N_PANEL)]
            else:
                before = [s_ref[p, pl.ds(t8 - 8, 8), :] for p in range(N_PANEL)]
            p1 = [p1[p] + tile[p] * before[p] for p in range(N_PANEL)]
            p2 = [p2[p] + swapped[p] * before[p] for p in range(N_PANEL)]
            return tile, swapped, p1, p2

        def steps(n, c):
            for r in range(SCAN_UNROLL):
                t8 = pl.multiple_of((TIME_BLK - 1 - (n * SCAN_UNROLL + r)) * 8, 8)
                c = one(t8, c, False)
            return c

        tile0 = _panels(state[...])
        c = (tile0, [pltpu.roll(t, N_CHUNK, 0) for t in tile0], _panels(acc1[...]), _panels(acc2[...]))
        c = lax.fori_loop(0, TIME_BLK // SCAN_UNROLL - 1, steps, c)
        for r in range(SCAN_UNROLL - 1, -1, -1):
            c = one(r * 8, c, r == 0)
        state[...] = jnp.concatenate(c[0], axis=1)
        acc1[...] = jnp.concatenate(c[2], axis=1)
        acc2[...] = jnp.concatenate(c[3], axis=1)

        d = d_ref[layer:layer + 1, :]
        for k in range(N_CHUNK):
            cols = slice(k * CH_W, (k + 1) * CH_W)
            lrb = _rows_load(lam_ref, k).astype(BF16)
            lib = _rows_load(lam_ref, N_CHUNK + k).astype(BF16)
            u = u_ref[:, cols]
            ub = u.astype(BF16)
            dy = dy_ref[:, cols]
            du = dy * d[:, cols] + _dot_nt(lrb, bdre[k]) + _dot_nt(lib, bdim[k])
            du_ref[:, cols] = du.astype(BF16)
            gbre[k] += _dot_tn(ub, lrb)
            gbim[k] += _dot_tn(ub, lib)
        gd[...] += jnp.sum(dy_ref[...] * u_ref[...], axis=0, keepdims=True)

        @pl.when(step_id == N_TBLK - 1)
        def _():
            gd_ref[...] = gd[...]
            ga_re = acc1[0:N_CHUNK, :] + acc1[N_CHUNK:, :]
            ga_im = acc2[0:N_CHUNK, :] - acc2[N_CHUNK:, :]
            for grp in range(N_GROUP):
                k, g = divmod(grp, CH_G)
                rows = slice(g * GROUP_W, (g + 1) * GROUP_W)
                cols = slice(g * STATE, (g + 1) * STATE)
                gcre_ref[grp] = gcre[k, rows, cols]
                gcim_ref[grp] = gcim[k, rows, cols]
                gbre_ref[grp] = gbre[k, rows, cols]
                gbim_ref[grp] = gbim[k, rows, cols]
                gare_ref[grp:grp + 1, :] = ga_re[k:k + 1, cols]
                gaim_ref[grp:grp + 1, :] = ga_im[k:k + 1, cols]

    back = lambda i: N_TBLK - 1 - i
    tok = lambda: pl.BlockSpec((TIME_BLK, WIDTH), lambda i: (back(i), 0))
    mat = lambda: _const((N_GROUP, GROUP_W, STATE))
    acc_mat = pltpu.VMEM((N_CHUNK, CH_W, CH_S), F32)
    return _pcall(
        body, name=f"s5_bwd_l{layer}",
        out_shape=(SDS((SEQ, N_IN), BF16), SDS((N_GROUP, GROUP_W, STATE), F32), SDS((N_GROUP, GROUP_W, STATE), F32),
                   SDS((N_GROUP, GROUP_W, STATE), F32), SDS((N_GROUP, GROUP_W, STATE), F32),
                   SDS((N_GROUP, STATE), F32), SDS((N_GROUP, STATE), F32), SDS((1, WIDTH), F32)),
        grid=(N_TBLK,),
        in_specs=[tok(), tok(),
                  pl.BlockSpec((N_PANEL, TIME_BLK * 8, 128), lambda i: (0, back(i), 0)),
                  pl.BlockSpec((N_PANEL, 8, 128), lambda i: (0, jnp.maximum(back(i) * TIME_BLK - 1, 0), 0))]
        + _s5_layer_specs(layer) + [ANY],
        out_specs=(tok(), mat(), mat(), mat(), mat(), _const((N_GROUP, STATE)), _const((N_GROUP, STATE)),
                   _const((1, WIDTH))),
        scratch_shapes=[pltpu.VMEM((N_PANEL, TIME_BLK * 8, 128), F32)] + _s5_layer_scratch()
        + [pltpu.VMEM((8, CH_S), F32)] * 3 + [acc_mat] * 4 + [pltpu.VMEM((1, WIDTH), F32)],
        args=[dy0, proj, states, states, bbt_re, bbt_im, c_re, c_im, abar_re, abar_im, d_skip, dproj],
        aliases={11: 0}, sem=("arbitrary",), limit=VMEM_LIMIT_BIG, carry=carry)


def _pool_counts(win):
    t = lax.broadcasted_iota(jnp.int32, (SEQ, POOL_GROUP), 0)
    return t, jnp.minimum(t + 1, win).astype(F32)


def _pool_fwd(layer, proj):
    def body(u_ref, o_ref):
        for gi, win in enumerate(POOL_WINDOWS):
            cols = slice(gi * POOL_GROUP, (gi + 1) * POOL_GROUP)
            u = u_ref[:, cols]
            t, count = _pool_counts(win)
            acc = u
            k = 1
            while k < win:
                acc = acc + jnp.where(t >= k, pltpu.roll(acc, k, 0), 0.0)
                k *= 2
            o_ref[:, cols] = acc / count - u

    return pl.pallas_call(
        body, name=f"pool_fwd_l{layer}",
        out_shape=SDS((SEQ, WIDTH), F32),
        grid=(1,),
        in_specs=[pl.BlockSpec((SEQ, WIDTH), lambda i: (0, 2))],
        out_specs=pl.BlockSpec((SEQ, WIDTH), lambda i: (0, 0)),
        compiler_params=_cp(("arbitrary",)),
    )(proj)


def _gelu_parts(y0):
    t = jnp.tanh(GELU_C * (y0 + GELU_A * (y0 * y0 * y0)))
    return t, 0.5 * y0 * (1.0 + t)


def _mix_forward(layer, p_ref, y0_ref, pooled_ref, wglu_ref, bglu_ref, pw_ref, scale_ref, wa_ref, wb_ref):
    za = p_ref[:, WIDTH:2 * WIDTH]
    zb = p_ref[:, 3 * WIDTH:4 * WIDTH]
    ga = p_ref[:, 4 * WIDTH:4 * WIDTH + D_MODEL]
    gb = p_ref[:, 4 * WIDTH + D_MODEL:]
    y0 = y0_ref[...]
    t, y1 = _gelu_parts(y0)
    y1b = y1.astype(BF16)
    q = _dot(y1b, wglu_ref[...].reshape(WIDTH, WIDTH)) + bglu_ref[layer:layer + 1, :]
    sq = _sig(q)
    y2 = y1 * sq
    sza = _sig(za)
    silu_za = za * sza
    ya = y2 * silu_za
    pooled = pooled_ref[...]
    mixed = jnp.concatenate(
        [_dot(pooled[:, g * POOL_GROUP:(g + 1) * POOL_GROUP].astype(BF16), pw_ref[g].astype(BF16))
         for g in range(len(POOL_WINDOWS))], axis=1)
    szb = _sig(zb)
    silu_zb = zb * szb
    scale = scale_ref[layer:layer + 1, :]
    ms = mixed * scale
    yb = ms * silu_zb
    yab = ya.astype(BF16)
    ybb = yb.astype(BF16)
    ma = _dot(yab, wa_ref[...])
    mb = _dot(ybb, wb_ref[...])
    sga = _sig(ga)
    sgb = _sig(gb)
    merged = sga * ma + sgb * mb
    return dict(za=za, zb=zb, y0=y0, t=t, y1=y1, y1b=y1b, sq=sq, y2=y2, sza=sza, silu_za=silu_za,
                pooled=pooled, mixed=mixed, szb=szb, silu_zb=silu_zb, scale=scale, ms=ms, yab=yab, ybb=ybb,
                ma=ma, mb=mb, sga=sga, sgb=sgb, merged=merged)


def _mix_weight_specs(layer):
    return [_const((N_DEV, WIDTH // N_DEV, WIDTH)),
            _const((DEPTH, WIDTH)),
            pl.BlockSpec((None, 4, POOL_GROUP, POOL_GROUP), lambda i: (layer, 0, 0, 0)),
            _const((DEPTH, WIDTH)),
            _const((WIDTH, D_MODEL)),
            _const((WIDTH, D_MODEL)),
            _const((N_DEV, D_MODEL // N_DEV, D_MODEL))]


def _mix_fwd(layer, x, proj, y0, pooled, wg_glu, b_glu, pool_w, pool_scale, wg_a, wg_b, wg_out, carry=None):
    def body(x_ref, p_ref, y0_ref, pooled_ref, wglu_ref, bglu_ref, pw_ref, scale_ref, wa_ref, wb_ref,
             wout_ref, o_ref):
        f = _mix_forward(layer, p_ref, y0_ref, pooled_ref, wglu_ref, bglu_ref, pw_ref, scale_ref, wa_ref, wb_ref)
        wout = wout_ref[...].reshape(D_MODEL, D_MODEL)
        o_ref[...] = x_ref[...] + _dot(f["merged"].astype(BF16), wout)

    (x_next,), moved = _pcall(
        body, name=f"mix_fwd_l{layer}",
        out_shape=[SDS((SEQ, D_MODEL), F32)],
        grid=(SEQ // TILE_M,),
        in_specs=[pl.BlockSpec((TILE_M, D_MODEL), lambda i: (i, 0)),
                  pl.BlockSpec((TILE_M, N_IN), lambda i: (i, 0)),
                  pl.BlockSpec((TILE_M, WIDTH), lambda i: (i, 0)),
                  pl.BlockSpec((TILE_M, WIDTH), lambda i: (i, 0))] + _mix_weight_specs(layer),
        out_specs=[pl.BlockSpec((TILE_M, D_MODEL), lambda i: (i, 0))],
        args=[x, proj, y0, pooled, wg_glu, b_glu, pool_w, pool_scale, wg_a, wg_b, wg_out],
        sem=("parallel",), carry=carry)
    return x_next, moved


def _loss_head(x, target, final_g):
    def body(x_ref, t_ref, g_ref, dx_ref, loss_ref, gg_ref):
        @pl.when(pl.program_id(0) == 0)
        def _():
            loss_ref[...] = jnp.zeros_like(loss_ref)
            gg_ref[...] = jnp.zeros_like(gg_ref)

        g = g_ref[...]
        rs, xn = _rms(x_ref[...])
        err = xn * g - t_ref[...]
        loss_ref[...] += 0.5 * jnp.sum(jnp.mean(err * err, axis=-1, keepdims=True), axis=0, keepdims=True)
        dy = err * (1.0 / D_MODEL)
        gg_ref[...] += jnp.sum(dy * xn, axis=0, keepdims=True)
        dxn = dy * g
        dx_ref[...] = rs * (dxn - xn * jnp.mean(dxn * xn, axis=-1, keepdims=True))

    return pl.pallas_call(
        body, name="loss_head",
        out_shape=(SDS((SEQ, D_MODEL), F32), SDS((1, 1), F32), SDS((1, D_MODEL), F32)),
        grid=(SEQ // TILE_M,),
        in_specs=[pl.BlockSpec((TILE_M, D_MODEL), lambda i: (i, 0)),
                  pl.BlockSpec((TILE_M, D_MODEL), lambda i: (i, 0)),
                  _const((1, D_MODEL))],
        out_specs=(pl.BlockSpec((TILE_M, D_MODEL), lambda i: (i, 0)), _const((1, 1)), _const((1, D_MODEL))),
        compiler_params=_cp(("arbitrary",)),
    )(x, target, final_g)


def _big_shapes():
    return dict(w_out=(DEPTH, N_DEV, D_MODEL // N_DEV, D_MODEL), w_branch_a=(DEPTH, N_DEV, WIDTH, D_MODEL // N_DEV),
                w_branch_b=(DEPTH, N_DEV, WIDTH, D_MODEL // N_DEV), ssm_w_glu=(DEPTH, N_DEV, WIDTH // N_DEV, WIDTH),
                w_in=(DEPTH, N_DEV, D_MODEL, WIDTH))


def _mix_bwd(layer, dx_next, proj, y0, pooled, wg_glu, b_glu, pool_w, pool_scale, wg_a, wg_b, wg_out, prev,
             carry=None):
    n_k = N_DEV
    n_prev = 0 if prev is None else len(prev)

    def body(*refs):
        (dx_ref, p_ref, y0_ref, pooled_ref, wglu_ref, bglu_ref, pw_ref, scale_ref, wa_ref, wb_ref,
         wout_ref) = refs[:11]
        (dproj_ref, dy0_ref, dpooled_ref, gwout_ref, gwa_ref, gwb_ref, gwglu_ref, gpw_ref,
         gscale_ref, gbglu_ref) = refs[11 + n_prev:]

        @pl.when(pl.program_id(0) == 0)
        def _():
            for r in (gwout_ref, gwa_ref, gwb_ref, gwglu_ref, gpw_ref, gscale_ref, gbglu_ref):
                r[...] = jnp.zeros_like(r)

        f = _mix_forward(layer, p_ref, y0_ref, pooled_ref, wglu_ref, bglu_ref, pw_ref, scale_ref, wa_ref, wb_ref)
        wglu = wglu_ref[...].reshape(WIDTH, WIDTH)
        wout = wout_ref[...].reshape(D_MODEL, D_MODEL)
        blk = D_MODEL // n_k
        dxb = dx_ref[...].astype(BF16)
        dmerged = _dot_nt(dxb, wout)
        gwout = _dot_tn(f["merged"].astype(BF16), dxb)
        for k in range(n_k):
            gwout_ref[_slot(k)] += gwout[k * blk:(k + 1) * blk, :]
        dma = dmerged * f["sga"]
        dmb = dmerged * f["sgb"]
        dga = dmerged * f["ma"] * f["sga"] * (1.0 - f["sga"])
        dgb = dmerged * f["mb"] * f["sgb"] * (1.0 - f["sgb"])
        dmab = dma.astype(BF16)
        dmbb = dmb.astype(BF16)
        dya = _dot_nt(dmab, wa_ref[...])
        dyb = _dot_nt(dmbb, wb_ref[...])
        gwa = _dot_tn(f["yab"], dmab)
        gwb = _dot_tn(f["ybb"], dmbb)
        for k in range(n_k):
            gwa_ref[_slot(k)] += gwa[:, k * blk:(k + 1) * blk]
            gwb_ref[_slot(k)] += gwb[:, k * blk:(k + 1) * blk]
        zb, szb = f["zb"], f["szb"]
        dzb = dyb * f["ms"] * (szb * (1.0 + zb * (1.0 - szb)))
        dms = dyb * f["silu_zb"]
        gscale_ref[...] += jnp.sum(dms * f["mixed"], axis=0, keepdims=True)
        dmixed = (dms * f["scale"]).astype(BF16)
        pooled = f["pooled"]
        for g in range(len(POOL_WINDOWS)):
            cols = slice(g * POOL_GROUP, (g + 1) * POOL_GROUP)
            dpooled_ref[:, cols] = _dot_nt(dmixed[:, cols], pw_ref[g].astype(BF16))
            gpw_ref[g] += _dot_tn(pooled[:, cols].astype(BF16), dmixed[:, cols])
        za, sza = f["za"], f["sza"]
        dza = dya * f["y2"] * (sza * (1.0 + za * (1.0 - sza)))
        dy2 = dya * f["silu_za"]
        sq = f["sq"]
        dq = dy2 * f["y1"] * sq * (1.0 - sq)
        dqb = dq.astype(BF16)
        dy1 = dy2 * sq + _dot_nt(dqb, wglu)
        gwglu = _dot_tn(f["y1b"], dqb)
        rblk = WIDTH // n_k
        for k in range(n_k):
            gwglu_ref[_slot(k)] += gwglu[k * rblk:(k + 1) * rblk, :]
        gbglu_ref[...] += jnp.sum(dq, axis=0, keepdims=True)
        y0, t = f["y0"], f["t"]
        dgelu = 0.5 * (1.0 + t) + 0.5 * y0 * (1.0 - t * t) * (GELU_C * (1.0 + 3.0 * GELU_A * y0 * y0))
        dy0_ref[...] = dy1 * dgelu
        zeros = jnp.zeros((TILE_M, WIDTH), BF16)
        dproj_ref[:, 0:WIDTH] = zeros
        dproj_ref[:, WIDTH:2 * WIDTH] = dza.astype(BF16)
        dproj_ref[:, 2 * WIDTH:3 * WIDTH] = zeros
        dproj_ref[:, 3 * WIDTH:4 * WIDTH] = dzb.astype(BF16)
        dproj_ref[:, 4 * WIDTH:4 * WIDTH + D_MODEL] = dga.astype(BF16)
        dproj_ref[:, 4 * WIDTH + D_MODEL:] = dgb.astype(BF16)

    tile = lambda w: pl.BlockSpec((TILE_M, w), lambda i: (i, 0))
    shapes = _big_shapes()
    big = ["w_out", "w_branch_a", "w_branch_b", "ssm_w_glu"]
    slab = lambda n: pl.BlockSpec((None,) + shapes[n][1:], lambda i: (layer, 0, 0, 0))
    args = [dx_next, proj, y0, pooled, wg_glu, b_glu, pool_w, pool_scale, wg_a, wg_b, wg_out]
    return _pcall(
        body, name=f"mix_bwd_l{layer}",
        out_shape=(SDS((SEQ, N_IN), BF16), SDS((SEQ, WIDTH), F32), SDS((SEQ, WIDTH), F32))
        + tuple(SDS(shapes[n], F32) for n in big)
        + (SDS((4, POOL_GROUP, POOL_GROUP), F32), SDS((1, WIDTH), F32), SDS((1, WIDTH), F32)),
        grid=(SEQ // TILE_M,),
        in_specs=[tile(D_MODEL), tile(N_IN), tile(WIDTH), tile(WIDTH)] + _mix_weight_specs(layer) + [ANY] * n_prev,
        out_specs=(tile(N_IN), tile(WIDTH), tile(WIDTH)) + tuple(slab(n) for n in big)
        + (_const((4, POOL_GROUP, POOL_GROUP)), _const((1, WIDTH)), _const((1, WIDTH))),
        args=args + list(prev or ()),
        aliases={len(args) + i: 3 + i for i in range(n_prev)},
        sem=("arbitrary",), limit=VMEM_LIMIT_BIG, carry=carry)


def _pool_bwd(layer, dpooled, dproj):
    def body(dp_ref, _, o_ref):
        for gi, win in enumerate(POOL_WINDOWS):
            cols = slice(gi * POOL_GROUP, (gi + 1) * POOL_GROUP)
            dp = dp_ref[:, cols]
            t, count = _pool_counts(win)
            e = dp / count
            acc = e
            k = 1
            while k < win:
                acc = acc + jnp.where(t < SEQ - k, pltpu.roll(acc, SEQ - k, 0), 0.0)
                k *= 2
            o_ref[:, cols] = (acc - dp).astype(BF16)

    return pl.pallas_call(
        body, name=f"pool_bwd_l{layer}",
        out_shape=SDS((SEQ, N_IN), BF16),
        grid=(1,),
        in_specs=[pl.BlockSpec((SEQ, WIDTH), lambda i: (0, 0)), ANY],
        out_specs=pl.BlockSpec((SEQ, WIDTH), lambda i: (0, 2)),
        input_output_aliases={1: 0},
        compiler_params=_cp(("arbitrary",)),
    )(dpooled, dproj)


def _proj_wgrad(layer, x, norm_g, dproj, prev, carry=None):
    tm = 512
    n_prev = 0 if prev is None else 1

    def body(*refs):
        x_ref, g_ref, dp_ref = refs[:3]
        gw_ref, gb_ref, ht_ref = refs[3 + n_prev:]
        n, t = pl.program_id(0), pl.program_id(1)

        @pl.when(t == 0)
        def _():
            gw_ref[...] = jnp.zeros_like(gw_ref)
            gb_ref[...] = jnp.zeros_like(gb_ref)

        @pl.when(n == 0)
        def _():
            _, xn = _rms(x_ref[...])
            ht_ref[t] = (xn * g_ref[layer:layer + 1, :]).T.astype(BF16)

        dp = dp_ref[...]
        gw_ref[...] += _dot(ht_ref[t], dp)
        gb_ref[...] += jnp.sum(dp.astype(F32), axis=0, keepdims=True)

    return _pcall(
        body, name=f"proj_wgrad_l{layer}",
        out_shape=(SDS(_big_shapes()["w_in"], F32), SDS((1, N_IN), F32)),
        grid=(N_DEV, SEQ // tm),
        in_specs=[pl.BlockSpec((tm, D_MODEL), lambda n, t: (jnp.where(n == 0, t, 0), 0)),
                  _const((DEPTH, D_MODEL)),
                  pl.BlockSpec((tm, WIDTH), lambda n, t: (t, n))] + [ANY] * n_prev,
        out_specs=(pl.BlockSpec((None, None, D_MODEL, WIDTH), lambda n, t: (layer, _slot(n), 0, 0)),
                   pl.BlockSpec((1, WIDTH), lambda n, t: (0, n))),
        scratch_shapes=[pltpu.VMEM((SEQ // tm, D_MODEL, tm), BF16)],
        args=[x, norm_g, dproj] + ([prev] if n_prev else []),
        aliases={3: 0} if n_prev else {}, sem=("arbitrary", "arbitrary"), carry=carry)


def _proj_dgrad(layer, dx_next, x, norm_g, dproj, wg_in, carry=None):
    def body(dxn_ref, x_ref, g_ref, dp_ref, w_ref, dx_ref, gg_ref):
        @pl.when(pl.program_id(0) == 0)
        def _():
            gg_ref[...] = jnp.zeros_like(gg_ref)

        dh = jnp.zeros((TILE_M, D_MODEL), F32)
        for k in range(N_DEV):
            dh = dh + _dot_nt(dp_ref[:, k * WIDTH:(k + 1) * WIDTH], w_ref[k])
        rs, xn = _rms(x_ref[...])
        gg_ref[...] += jnp.sum(dh * xn, axis=0, keepdims=True)
        dxn = dh * g_ref[layer:layer + 1, :]
        dx_ref[...] = dxn_ref[...] + rs * (dxn - xn * jnp.mean(dxn * xn, axis=-1, keepdims=True))

    return _pcall(
        body, name=f"proj_dgrad_l{layer}",
        out_shape=(SDS((SEQ, D_MODEL), F32), SDS((1, D_MODEL), F32)),
        grid=(SEQ // TILE_M,),
        in_specs=[pl.BlockSpec((TILE_M, D_MODEL), lambda i: (i, 0)),
                  pl.BlockSpec((TILE_M, D_MODEL), lambda i: (i, 0)),
                  _const((DEPTH, D_MODEL)),
                  pl.BlockSpec((TILE_M, N_IN), lambda i: (i, 0)),
                  _const((N_DEV, D_MODEL, WIDTH))],
        out_specs=(pl.BlockSpec((TILE_M, D_MODEL), lambda i: (i, 0)), _const((1, D_MODEL))),
        args=[dx_next, x, norm_g, dproj, wg_in], sem=("arbitrary",), carry=carry)


def _my_place():
    return lax.axis_index("x"), lax.axis_index("y"), lax.axis_index("c")


def _gather_plan(shards, layer, by_columns=()):
    n = len(shards)

    def parts(ins, outs, sems):
        send_sems, recv_sems, local_sems = sems
        x, y, c = _my_place()
        chips = [(1 - x, y), (x, 1 - y), (1 - x, 1 - y)]

        def rows(t, place):
            px, py, pc = place
            index = 4 * px + 2 * py + pc
            if t in by_columns:
                width = shards[t].shape[2]
                return outs[t].at[:, pl.ds(pl.multiple_of(index * width, 128), width)]
            return outs[t].at[index]

        def copy(t, k, block, to, from_src=False):
            return pltpu.make_async_remote_copy(
                src_ref=ins[t].at[layer] if from_src else rows(t, block), dst_ref=rows(t, block),
                send_sem=send_sems.at[7 * t + k], recv_sem=recv_sems.at[7 * t + k], device_id=to,
                device_id_type=MESH)

        def mine(t):
            return pltpu.make_async_copy(ins[t].at[layer], rows(t, (x, y, c)), local_sems.at[t])

        return (x, y, c), chips, copy, mine

    def start(ins, outs, sems):
        me, chips, copy, mine = parts(ins, outs, sems)
        x, y, c = me
        for t in range(n):
            mine(t).start()
            copy(t, 0, me, (x, y, 1 - c), from_src=True).start()
            for j, chip in enumerate(chips):
                copy(t, 1 + j, me, (*chip, c), from_src=True).start()

    def finish(ins, outs, sems):
        me, chips, copy, mine = parts(ins, outs, sems)
        x, y, c = me
        sibling = (x, y, 1 - c)
        for t in range(n):
            for j, chip in enumerate(chips):
                copy(t, 1 + j, (*chip, c), me).wait_recv()
                copy(t, 4 + j, (*chip, c), sibling).start()
        for t in range(n):
            copy(t, 0, sibling, me).wait_recv()
            for j, chip in enumerate(chips):
                copy(t, 4 + j, (*chip, 1 - c), me).wait_recv()
            for k in range(7):
                copy(t, k, me, sibling, from_src=k < 4).wait_send()
            mine(t).wait()

    out_shape = [SDS((a.shape[1], N_DEV * a.shape[2]) if t in by_columns else (N_DEV,) + a.shape[1:], a.dtype)
                 for t, a in enumerate(shards)]
    sems = [pltpu.SemaphoreType.DMA((7 * n,)), pltpu.SemaphoreType.DMA((7 * n,)), pltpu.SemaphoreType.DMA((n,))]
    return _Carried(shards, out_shape, sems, start, finish)


class _Carried:
    def __init__(self, ins, out_shape, sems, start, finish):
        self.ins, self.out_shape, self.sems = list(ins), list(out_shape), list(sems)
        self.start, self.finish = start, finish


def _pcall(body, *, name, grid, in_specs, out_specs, out_shape, args, scratch_shapes=(), aliases=None,
           sem=None, limit=VMEM_LIMIT, carry=None):
    out_shape, out_specs, scratch_shapes = list(out_shape), list(out_specs), list(scratch_shapes)
    n_in, n_out, n_scr = len(args), len(out_shape), len(scratch_shapes)
    if carry is None:
        kern, c_ins, c_out, c_sems = body, [], [], []
    else:
        c_ins, c_out, c_sems = carry.ins, carry.out_shape, carry.sems
        ci, co = len(c_ins), len(c_out)
        steps = tuple(grid)

        def kern(*refs):
            o0 = n_in + ci
            s0 = o0 + n_out + co
            mine = refs[:n_in] + refs[o0:o0 + n_out] + refs[s0:s0 + n_scr]
            theirs = (refs[n_in:o0], refs[o0 + n_out:s0], refs[s0 + n_scr:])
            first = pl.program_id(0) == 0
            last = pl.program_id(0) == steps[0] - 1
            for a in range(1, len(steps)):
                first = jnp.logical_and(first, pl.program_id(a) == 0)
                last = jnp.logical_and(last, pl.program_id(a) == steps[a] - 1)

            @pl.when(first)
            def _():
                carry.start(*theirs)

            body(*mine)

            @pl.when(last)
            def _():
                carry.finish(*theirs)

        sem = ("arbitrary",) * len(steps)
    res = pl.pallas_call(
        kern, name=name, grid=tuple(grid),
        in_specs=list(in_specs) + [ANY] * len(c_ins),
        out_specs=tuple(out_specs + [ANY] * len(c_out)),
        out_shape=tuple(out_shape + c_out),
        scratch_shapes=scratch_shapes + c_sems,
        input_output_aliases=aliases or {},
        compiler_params=_cp(sem, limit),
    )(*args, *c_ins)
    return res[:n_out], res[n_out:]


def _run_carried(name, carry):
    ci, co = len(carry.ins), len(carry.out_shape)

    def body(*refs):
        parts = (refs[:ci], refs[ci:ci + co], refs[ci + co:])
        carry.start(*parts)
        carry.finish(*parts)

    return pl.pallas_call(
        body, name=name, out_shape=tuple(carry.out_shape),
        in_specs=[ANY] * ci, out_specs=tuple([ANY] * co), scratch_shapes=carry.sems,
    )(*carry.ins)


def _sibling_plan(big, small):
    n = len(big)
    n_copies = 4 * n + len(small)

    def copies(ins, outs, sems):
        send_sems, recv_sems = sems
        x, y, c = _my_place()
        pairs = []
        for t, (_, layer) in enumerate(big):
            for s in range(4):
                pairs.append((ins[t].at[layer, pl.ds(4 * (1 - c) + s, 1)], outs[t].at[pl.ds(s, 1)]))
        pairs += list(zip(ins[n:], outs[n:]))
        return [pltpu.make_async_remote_copy(
            src_ref=src, dst_ref=dst, send_sem=send_sems.at[k], recv_sem=recv_sems.at[k],
            device_id=(x, y, 1 - c), device_id_type=MESH) for k, (src, dst) in enumerate(pairs)]

    def start(ins, outs, sems):
        for cp in copies(ins, outs, sems):
            cp.start()

    def finish(ins, outs, sems):
        for cp in copies(ins, outs, sems):
            cp.wait()

    out_shape = [SDS((4,) + a.shape[2:], a.dtype) for a, _ in big] + [SDS(a.shape, a.dtype) for a in small]
    sems = [pltpu.SemaphoreType.DMA((n_copies,)), pltpu.SemaphoreType.DMA((n_copies,))]
    return _Carried([a for a, _ in big] + list(small), out_shape, sems, start, finish)


def _chips_plan(big, small):
    n, n_small = len(big), len(small)
    max_rows = 512
    parts = [max(1, a.shape[1] // max_rows) for a in big]
    n_copies = 3 * (sum(parts) + n_small)

    def copies(ins, outs, sems, landing):
        send_sems, recv_sems, local_sems = sems
        x, y, c = _my_place()
        my_chip = 2 * x + y
        chips = [(1 - x, y), (x, 1 - y), (1 - x, 1 - y)]
        remote, local = [], []
        for chip in chips:
            to = 2 * chip[0] + chip[1]
            slot = to if landing else my_chip
            pairs = []
            for t in range(n):
                rows_per = big[t].shape[1] // parts[t]
                for p in range(parts[t]):
                    rows = pl.ds(p * rows_per, rows_per)
                    pairs.append((ins[t].at[to, rows], outs[t].at[slot, rows]))
            pairs += [(ins[t], outs[t].at[slot]) for t in range(n, n + n_small)]
            for src, dst in pairs:
                k = len(remote)
                remote.append(pltpu.make_async_remote_copy(
                    src_ref=src, dst_ref=dst, send_sem=send_sems.at[k], recv_sem=recv_sems.at[k],
                    device_id=(*chip, c), device_id_type=MESH))
        for t in range(n):
            local.append(pltpu.make_async_copy(ins[t].at[my_chip], outs[t].at[my_chip], local_sems.at[t]))
        for t in range(n, n + n_small):
            local.append(pltpu.make_async_copy(ins[t], outs[t].at[my_chip], local_sems.at[t]))
        return remote + local

    def start(ins, outs, sems):
        for cp in copies(ins, outs, sems, landing=False):
            cp.start()

    def finish(ins, outs, sems):
        for cp in copies(ins, outs, sems, landing=True):
            cp.wait()

    out_shape = [SDS(a.shape, a.dtype) for a in big] + [SDS((N_CHIP,) + a.shape, a.dtype) for a in small]
    sems = [pltpu.SemaphoreType.DMA((n_copies,)), pltpu.SemaphoreType.DMA((n_copies,)),
            pltpu.SemaphoreType.DMA((n + n_small,))]
    return _Carried(list(big) + list(small), out_shape, sems, start, finish)


def _all_plan(small):
    n = len(small)
    masks = [(m >> 2 & 1, m >> 1 & 1, m & 1) for m in range(1, N_DEV)]

    def copies(ins, outs, sems, landing):
        send_sems, recv_sems, local_sems = sems
        x, y, c = _my_place()
        me = 4 * x + 2 * y + c
        flip = lambda v, bit: 1 - v if bit else v
        remote = []
        for fx, fy, fc in masks:
            peer = (flip(x, fx), flip(y, fy), flip(c, fc))
            slot = 4 * peer[0] + 2 * peer[1] + peer[2] if landing else me
            for t in range(n):
                k = len(remote)
                remote.append(pltpu.make_async_remote_copy(
                    src_ref=ins[t], dst_ref=outs[t].at[slot], send_sem=send_sems.at[k], recv_sem=recv_sems.at[k],
                    device_id=peer, device_id_type=MESH))
        local = [pltpu.make_async_copy(ins[t], outs[t].at[me], local_sems.at[t]) for t in range(n)]
        return remote + local

    def start(ins, outs, sems):
        for cp in copies(ins, outs, sems, landing=False):
            cp.start()

    def finish(ins, outs, sems):
        for cp in copies(ins, outs, sems, landing=True):
            cp.wait()

    out_shape = [SDS((N_DEV,) + a.shape, a.dtype) for a in small]
    sems = [pltpu.SemaphoreType.DMA((7 * n,)), pltpu.SemaphoreType.DMA((7 * n,)), pltpu.SemaphoreType.DMA((n,))]
    return _Carried(list(small), out_shape, sems, start, finish)


def _join(*plans):
    plans = [p for p in plans if p is not None]
    if len(plans) <= 1:
        return plans[0] if plans else None

    def each(fn_name, ins, outs, sems):
        i = o = s = 0
        for p in plans:
            ni, no, ns = len(p.ins), len(p.out_shape), len(p.sems)
            getattr(p, fn_name)(ins[i:i + ni], outs[o:o + no], sems[s:s + ns])
            i, o, s = i + ni, o + no, s + ns

    return _Carried(sum((p.ins for p in plans), []), sum((p.out_shape for p in plans), []),
                    sum((p.sems for p in plans), []),
                    lambda i, o, s: each("start", i, o, s), lambda i, o, s: each("finish", i, o, s))


def _row_block(rows):
    return rows if rows <= 256 else 256


def _add_own(tag, core, g, layer, got):
    _, r, c = got.shape
    rb = _row_block(r)

    def body(core_ref, a_ref, b_ref, o_ref):
        o_ref[...] = (a_ref[...] + b_ref[...]).astype(o_ref.dtype)

    return pl.pallas_call(
        body, name=f"add_{tag}", out_shape=SDS(got.shape, BF16),
        grid_spec=pltpu.PrefetchScalarGridSpec(
            num_scalar_prefetch=1, grid=(4, r // rb),
            in_specs=[pl.BlockSpec((None, None, rb, c), lambda s, j, core: (layer, 4 * core[0] + s, j, 0)),
                      pl.BlockSpec((None, rb, c), lambda s, j, core: (s, j, 0))],
            out_specs=pl.BlockSpec((None, rb, c), lambda s, j, core: (s, j, 0))),
        compiler_params=_cp(("parallel", "parallel")),
    )(core, g, got)


def _add_lists(tag, own, got, grid=None, specs=None):
    n = len(own)

    def body(*refs):
        for a, b, o in zip(refs[:n], refs[n:2 * n], refs[2 * n:]):
            o[...] = a[...] + b[...]

    kw = {}
    if grid is not None:
        kw = dict(grid=grid, in_specs=list(specs) * 2, out_specs=tuple(specs),
                  compiler_params=_cp(("parallel",) * len(grid)))
    return pl.pallas_call(
        body, name=f"add_{tag}", out_shape=tuple(SDS(a.shape, a.dtype) for a in own), **kw)(*own, *got)


def _adamw_math(w, g, m, v):
    m = ADAM_B1 * m + (1.0 - ADAM_B1) * g
    v = ADAM_B2 * v + (1.0 - ADAM_B2) * (g * g)
    m_hat = m / (1.0 - ADAM_B1 ** ADAM_STEP)
    v_hat = v / (1.0 - ADAM_B2 ** ADAM_STEP)
    delta = -ADAM_LR * (m_hat / (jnp.sqrt(v_hat) + ADAM_EPS) + ADAM_WD * w)
    return delta, m, v


def _sum_slots_adamw(tag, slots, w, m, v):
    _, r, c = slots[0].shape
    rb = _row_block(r)

    def body(s0_ref, s1_ref, w_ref, m_ref, v_ref, g_ref, d_ref, nm_ref, nv_ref):
        first = pl.program_id(1) == 0
        g = _sum4([jnp.where(first, s0_ref[k], s1_ref[k]).astype(F32) for k in range(N_CHIP)])
        delta, nm, nv = _adamw_math(w_ref[...], g, m_ref[...], v_ref[...])
        g_ref[...] = g
        d_ref[...] = delta
        nm_ref[...] = nm
        nv_ref[...] = nv

    spec = pl.BlockSpec((None, rb, c), lambda j, l: (l, j, 0))
    sspec = pl.BlockSpec((N_CHIP, rb, c), lambda j, l: (0, j, 0))
    s = SDS((DEPTH, r, c), F32)
    return pl.pallas_call(
        body, name=f"adamw_{tag}", out_shape=(s, s, s, s),
        grid=(r // rb, DEPTH), in_specs=[sspec, sspec, spec, spec, spec], out_specs=(spec, spec, spec, spec),
        compiler_params=_cp(("parallel", "arbitrary")),
    )(*slots, w, m, v)


def _adamw_small(tag, entries, grid=None):
    flat_in, in_specs, out_shape, out_specs, layout = [], [], [], [], []
    for slots, w, m, v, slot_spec, w_spec in entries:
        per_layer = isinstance(slots, (list, tuple))
        n_slot = len(slots) if per_layer else 1
        flat_in += (list(slots) if per_layer else [slots]) + [w, m, v]
        in_specs += [slot_spec] * n_slot + [w_spec] * 3
        out_shape += [SDS(w.shape, F32)] * 4
        out_specs += [w_spec] * 4
        layout.append((per_layer, n_slot))
    n_in = len(flat_in)

    def body(*refs):
        i, o = 0, n_in
        for per_layer, n_slot in layout:
            s_refs = refs[i:i + n_slot]
            w_ref, m_ref, v_ref = refs[i + n_slot:i + n_slot + 3]
            outs = refs[o:o + 4]
            if per_layer:
                for l, s_ref in enumerate(s_refs):
                    at = (slice(l, l + 1),) if len(w_ref.shape) == 2 else (l,)
                    g = _sum_slots(s_ref)
                    res = (g,) + _adamw_math(w_ref[at], g, m_ref[at], v_ref[at])
                    for o_ref, val in zip(outs, res):
                        o_ref[at] = val
            else:
                g = _sum_slots(s_refs[0])
                res = (g,) + _adamw_math(w_ref[...], g, m_ref[...], v_ref[...])
                for o_ref, val in zip(outs, res):
                    o_ref[...] = val
            i += n_slot + 3
            o += 4

    kw = {}
    if grid is not None:
        kw = dict(grid=grid, in_specs=in_specs, out_specs=tuple(out_specs),
                  compiler_params=_cp(("parallel",) * len(grid)))
    res = pl.pallas_call(body, name=f"adamw_{tag}", out_shape=tuple(out_shape), **kw)(*flat_in)
    return [tuple(res[4 * e:4 * e + 4]) for e in range(len(entries))]


def kernel(x, norm_g, w_in, b_in, ssm_log_dt, ssm_lam_re, ssm_lam_im, ssm_b_re, ssm_b_im, ssm_c_re, ssm_c_im, ssm_d, ssm_w_glu, ssm_b_glu, pool_w, pool_scale, w_branch_a, w_branch_b, w_out, final_norm_g, loss_target, m_norm_g, m_w_in, m_b_in, m_ssm_log_dt, m_ssm_lam_re, m_ssm_lam_im, m_ssm_b_re, m_ssm_b_im, m_ssm_c_re, m_ssm_c_im, m_ssm_d, m_ssm_w_glu, m_ssm_b_glu, m_pool_w, m_pool_scale, m_w_branch_a, m_w_branch_b, m_w_out, m_final_norm_g, v_norm_g, v_w_in, v_b_in, v_ssm_log_dt, v_ssm_lam_re, v_ssm_lam_im, v_ssm_b_re, v_ssm_b_im, v_ssm_c_re, v_ssm_c_im, v_ssm_d, v_ssm_w_glu, v_ssm_b_glu, v_pool_w, v_pool_scale, v_w_branch_a, v_w_branch_b, v_w_out, v_final_norm_g):
    weights = dict(norm_g=norm_g, w_in=w_in, b_in=b_in, ssm_log_dt=ssm_log_dt, ssm_lam_re=ssm_lam_re,
                   ssm_lam_im=ssm_lam_im, ssm_b_re=ssm_b_re, ssm_b_im=ssm_b_im, ssm_c_re=ssm_c_re,
                   ssm_c_im=ssm_c_im, ssm_d=ssm_d, ssm_w_glu=ssm_w_glu, ssm_b_glu=ssm_b_glu, pool_w=pool_w,
                   pool_scale=pool_scale, w_branch_a=w_branch_a, w_branch_b=w_branch_b, w_out=w_out,
                   final_norm_g=final_norm_g.reshape(1, D_MODEL))
    mom_m = dict(norm_g=m_norm_g, w_in=m_w_in, b_in=m_b_in, ssm_log_dt=m_ssm_log_dt, ssm_lam_re=m_ssm_lam_re,
                 ssm_lam_im=m_ssm_lam_im, ssm_b_re=m_ssm_b_re, ssm_b_im=m_ssm_b_im, ssm_c_re=m_ssm_c_re,
                 ssm_c_im=m_ssm_c_im, ssm_d=m_ssm_d, ssm_w_glu=m_ssm_w_glu, ssm_b_glu=m_ssm_b_glu,
                 pool_w=m_pool_w, pool_scale=m_pool_scale, w_branch_a=m_w_branch_a, w_branch_b=m_w_branch_b,
                 w_out=m_w_out, final_norm_g=m_final_norm_g.reshape(1, D_MODEL))
    mom_v = dict(norm_g=v_norm_g, w_in=v_w_in, b_in=v_b_in, ssm_log_dt=v_ssm_log_dt, ssm_lam_re=v_ssm_lam_re,
                 ssm_lam_im=v_ssm_lam_im, ssm_b_re=v_ssm_b_re, ssm_b_im=v_ssm_b_im, ssm_c_re=v_ssm_c_re,
                 ssm_c_im=v_ssm_c_im, ssm_d=v_ssm_d, ssm_w_glu=v_ssm_w_glu, ssm_b_glu=v_ssm_b_glu,
                 pool_w=v_pool_w, pool_scale=v_pool_scale, w_branch_a=v_w_branch_a, w_branch_b=v_w_branch_b,
                 w_out=v_w_out, final_norm_g=v_final_norm_g.reshape(1, D_MODEL))
    order = ["norm_g", "w_in", "b_in", "ssm_log_dt", "ssm_lam_re", "ssm_lam_im", "ssm_b_re", "ssm_b_im",
             "ssm_c_re", "ssm_c_im", "ssm_d", "ssm_w_glu", "ssm_b_glu", "pool_w", "pool_scale", "w_branch_a",
             "w_branch_b", "w_out", "final_norm_g"]
    big_names = ["w_in", "ssm_w_glu", "w_branch_a", "w_branch_b", "w_out"]

    log_dt3 = ssm_log_dt.reshape(DEPTH, N_GROUP, 1)
    b_t = lambda a: a.transpose(0, 1, 3, 2)
    for d in (weights, mom_m, mom_v):
        d["ssm_b_re"], d["ssm_b_im"] = b_t(d["ssm_b_re"]), b_t(d["ssm_b_im"])
    bt_re, bt_im = weights["ssm_b_re"], weights["ssm_b_im"]
    abar_re, abar_im, bbt_re, bbt_im = _s5_params(log_dt3, ssm_lam_re, ssm_lam_im, bt_re, bt_im)
    s5_args = (bbt_re, bbt_im, ssm_c_re, ssm_c_im, abar_re, abar_im, ssm_d)

    w16 = {n: weights[n].astype(BF16) for n in big_names}
    rest = [w16[n] for n in big_names[1:]]
    wg_in = [None, None]
    wg_rest = [None, None]
    (wg_in[0],) = _run_carried("gather_w_in_l0", _gather_plan([w16["w_in"]], 0))
    xs = [x.reshape(SEQ, D_MODEL)]
    saved = []
    for l in range(DEPTH):
        proj, moved = _norm_proj(l, xs[l], norm_g, wg_in[l], b_in,
                                 carry=_gather_plan(rest, 0, by_columns=(1, 2)) if l == 0 else None)
        if l == 0:
            wg_rest[0] = moved
        (states, y0), moved = _s5_scan_fwd(
            l, proj, *s5_args, carry=_gather_plan([w16["w_in"]], 1) if l == 0 else None)
        if l == 0:
            (wg_in[1],) = moved
        pooled = _pool_fwd(l, proj)
        wg_glu, wg_a, wg_b, wg_out = wg_rest[l]
        x_next, moved = _mix_fwd(l, xs[l], proj, y0, pooled, wg_glu, ssm_b_glu, pool_w, pool_scale, wg_a, wg_b,
                                 wg_out, carry=_gather_plan(rest, 1, by_columns=(1, 2)) if l == 0 else None)
        if l == 0:
            wg_rest[1] = moved
        xs.append(x_next)
        saved.append((proj, states, y0, pooled))

    dx, loss_part, g_final = _loss_head(xs[DEPTH], loss_target.reshape(SEQ, D_MODEL), weights["final_norm_g"])
    loss = lax.psum(loss_part[0, 0], ("x", "y", "c"))

    core = lax.axis_index("c").astype(jnp.int32).reshape(1)
    vec_names = ["norm_g", "b_in", "ssm_d", "ssm_b_glu", "pool_scale", "ssm_log_dt"]
    s5_names = ["ssm_log_dt", "ssm_lam_re", "ssm_lam_im", "ssm_b_re", "ssm_b_im"]
    mat_names = ["pool_w", "ssm_c_re", "ssm_c_im", "ssm_b_re", "ssm_b_im"]
    lane_sparse = ("ssm_c_re", "ssm_c_im", "ssm_b_re", "ssm_b_im")

    def dense(key, a):
        return a.reshape(-1, 128) if key[0] in lane_sparse else a

    def undense(key, slots):
        return slots.reshape((N_CHIP, N_GROUP, GROUP_W, STATE)) if key[0] in lane_sparse else slots

    def add_small(tag, keys, own, got):
        out = [None] * len(keys)
        whole = [i for i, k in enumerate(keys) if k[0] not in mat_names]
        tiled = [i for i, k in enumerate(keys) if k[0] in mat_names]
        if whole:
            for i, r in zip(whole, _add_lists(f"{tag}_a", [own[i] for i in whole], [got[i] for i in whole])):
                out[i] = r
        if tiled:
            specs = [pl.BlockSpec((1, POOL_GROUP, POOL_GROUP), lambda j: (j, 0, 0)) if keys[i][0] == "pool_w"
                     else pl.BlockSpec((own[i].shape[0] // N_CHUNK, 128), lambda j: (j, 0)) for i in tiled]
            for i, r in zip(tiled, _add_lists(f"{tag}_b", [own[i] for i in tiled], [got[i] for i in tiled],
                                              grid=(N_CHUNK,), specs=specs)):
                out[i] = r
        return out

    sm = {("final_norm_g", None): g_final}
    slots = {}
    grads = dict.fromkeys(big_names)

    class Wave:
        def __init__(self, tag, layer, big, keys):
            self.tag, self.layer, self.big, self.keys = tag, layer, big, keys

        def to_sibling(self):
            self.own = [dense(k, sm[k]) for k in self.keys]
            return _sibling_plan([(grads[n], self.layer) for n in self.big], self.own)

        def add(self, moved):
            nb = len(self.big)
            self.chip_big = [_add_own(f"{self.tag}_{n}", core, grads[n], self.layer, b)
                             for n, b in zip(self.big, moved[:nb])]
            self.chip_small = add_small(self.tag, self.keys, self.own, moved[nb:])

        def to_chips(self, big=None, small=True):
            self.sent = list(self.big if big is None else big), small
            return _chips_plan([self.chip_big[self.big.index(n)] for n in self.sent[0]],
                               self.chip_small if small else [])

        def landed(self, moved):
            names, small = self.sent
            for n, s in zip(names, moved[:len(names)]):
                slots[(n, self.layer)] = s
            if small:
                for k, s in zip(self.keys, moved[len(names):]):
                    slots[k] = undense(k, s)
            return moved[len(names) + (len(self.keys) if small else 0):]

    def s5_param_grads(l, g_abar_re, g_abar_im, g_bbt_re, g_bbt_im):
        g = _s5_params_bwd(l, log_dt3, ssm_lam_re, ssm_lam_im, bt_re, bt_im, g_abar_re, g_abar_im, g_bbt_re, g_bbt_im)
        sm[("ssm_log_dt", l)] = g[0].reshape(1, N_GROUP)
        for n, a in zip(s5_names[1:], g[1:]):
            sm[(n, l)] = a

    small1 = ["b_in", "ssm_d", "ssm_b_glu", "pool_scale", "pool_w", "ssm_c_re", "ssm_c_im"] + s5_names
    w1 = Wave("chip1", 1, list(big_names), [(n, 1) for n in small1] + [("final_norm_g", None)])
    early = Wave("chip0e", 0, big_names[1:], [("pool_w", 0), ("pool_scale", 0), ("ssm_b_glu", 0)])
    mid = Wave("chip0m", 0, [], [(n, 0) for n in ["ssm_c_re", "ssm_c_im", "ssm_d"] + s5_names] + [("norm_g", 1)])
    late = Wave("chip0l", 0, ["w_in"], [("b_in", 0)])

    mix_prev, gw_in = None, None
    for l in reversed(range(DEPTH)):
        proj, states, y0, pooled = saved[l]
        wg_glu, wg_a, wg_b, wg_out = wg_rest[l]
        res, moved = _mix_bwd(l, dx, proj, y0, pooled, wg_glu, ssm_b_glu, pool_w, pool_scale, wg_a, wg_b, wg_out,
                              mix_prev, carry=None if l == 1 else w1.to_chips(big=["w_in"], small=False))
        if l == 0:
            w1.landed(moved)
        dproj, dy0, dpooled = res[:3]
        mix_prev = list(res[3:7])
        grads["w_out"], grads["w_branch_a"], grads["w_branch_b"], grads["ssm_w_glu"] = mix_prev
        sm[("pool_w", l)], sm[("pool_scale", l)], sm[("ssm_b_glu", l)] = res[7:]
        dproj = _pool_bwd(l, dpooled, dproj)
        carry = None if l == 1 else _join(w1.to_chips(big=big_names[1:]), early.to_sibling())
        res, moved = _s5_scan_bwd(l, dy0, proj, states, *s5_args, dproj, carry=carry)
        if l == 0:
            early.add(w1.landed(moved))
        dproj, g_bbt_re, g_bbt_im, sm[("ssm_c_re", l)], sm[("ssm_c_im", l)], g_abar_re, g_abar_im, sm[("ssm_d", l)] = res
        s5_param_grads(l, g_abar_re, g_abar_im, g_bbt_re, g_bbt_im)
        carry = None if l == 1 else _join(early.to_chips(), mid.to_sibling())
        (gw_in, sm[("b_in", l)]), moved = _proj_wgrad(l, xs[l], norm_g, dproj, gw_in, carry=carry)
        grads["w_in"] = gw_in
        if l == 0:
            mid.add(early.landed(moved))
        carry = w1.to_sibling() if l == 1 else _join(mid.to_chips(), late.to_sibling())
        (dx, sm[("norm_g", l)]), moved = _proj_dgrad(l, dx, xs[l], norm_g, dproj, wg_in[l], carry=carry)
        if l == 1:
            w1.add(moved)
        else:
            late.add(mid.landed(moved))
    grad_x = dx.reshape(1, SEQ, D_MODEL)
    moved = late.landed(_run_carried("exchange_last", _join(late.to_chips(), _all_plan([sm[("norm_g", 0)]]))))
    slots[("norm_g", 0)] = moved[0]

    res = {}
    for n in big_names:
        res[n] = _sum_slots_adamw(n, [slots[(n, l)] for l in range(DEPTH)], weights[n], mom_m[n], mom_v[n])
    per_layer = lambda n: [slots[(n, l)] for l in range(DEPTH)]
    names_a = vec_names + ["ssm_lam_re", "ssm_lam_im"]
    entries_a = [(per_layer(n), weights[n], mom_m[n], mom_v[n], None, None) for n in names_a]
    n = "final_norm_g"
    entries_a.append((slots[(n, None)], weights[n], mom_m[n], mom_v[n], None, None))
    out_a = _adamw_small("small_a", entries_a)
    for n, r in zip(names_a + ["final_norm_g"], out_a):
        res[n] = r
    res["final_norm_g"] = tuple(a.reshape(D_MODEL) for a in res["final_norm_g"])
    pw_s = pl.BlockSpec((N_CHIP, 1, POOL_GROUP, POOL_GROUP), lambda j: (0, j, 0, 0))
    pw_w = pl.BlockSpec((DEPTH, 1, POOL_GROUP, POOL_GROUP), lambda j: (0, j, 0, 0))
    c_s = pl.BlockSpec((N_CHIP, CH_G, GROUP_W, STATE), lambda j: (0, j, 0, 0))
    c_w = pl.BlockSpec((DEPTH, CH_G, GROUP_W, STATE), lambda j: (0, j, 0, 0))
    entries_b = [(per_layer(n), weights[n], mom_m[n], mom_v[n], pw_s if n == "pool_w" else c_s,
                  pw_w if n == "pool_w" else c_w) for n in mat_names]
    out_b = _adamw_small("small_b", entries_b, grid=(N_CHUNK,))
    for n, r in zip(mat_names, out_b):
        res[n] = tuple(b_t(a) for a in r) if n in ("ssm_b_re", "ssm_b_im") else r

    outs = [loss, grad_x]
    for i in range(4):
        outs += [res[n][i] for n in order]
    return tuple(outs)
```

```python
import math

import jax
import jax.numpy as jnp
from jax import lax
from jax.experimental import pallas as pl
from jax.experimental.pallas import tpu as pltpu

F32 = jnp.float32
BF16 = jnp.bfloat16

SEQ = 2048
D_MODEL = 1024
N_IN = 4096
WIDTH = 512
N_GROUP = 32
GROUP_W = 16
STATE = 64
N_STATE = N_GROUP * STATE
N_CHUNK = 4
CH_G = N_GROUP // N_CHUNK
CH_W = WIDTH // N_CHUNK
CH_S = N_STATE // N_CHUNK
N_DEV = 8
N_CHIP = 4
POOL_WINDOWS = (2, 4, 8, 16)
POOL_GROUP = 128
EPS = 1e-6
DEPTH = 2

ADAM_LR = 0.001
ADAM_B1 = 0.9
ADAM_B2 = 0.999
ADAM_EPS = 1e-08
ADAM_WD = 0.01
ADAM_STEP = 10

TILE_M = 512
TILE_MB = 256
ROW_BLK = 512
VMEM_LIMIT = 48 * 1024 * 1024
VMEM_LIMIT_BIG = 60 * 1024 * 1024
MESH = pl.DeviceIdType.MESH
ANY = pl.BlockSpec(memory_space=pl.ANY)

GELU_C = math.sqrt(2.0 / math.pi)
GELU_A = 0.044715

SDS = jax.ShapeDtypeStruct


def _cp(sem=None, limit=VMEM_LIMIT):
    return pltpu.CompilerParams(dimension_semantics=sem, vmem_limit_bytes=limit)


def _dot(a, b):
    return jnp.dot(a, b, preferred_element_type=F32)


def _dot_nt(a, b):
    return lax.dot_general(a, b, (((1,), (1,)), ((), ())), preferred_element_type=F32)


def _dot_tn(a, b):
    return lax.dot_general(a, b, (((0,), (0,)), ((), ())), preferred_element_type=F32)


def _sig(x):
    return jax.nn.sigmoid(x)


def _rms(x):
    rs = lax.rsqrt(jnp.mean(x * x, axis=-1, keepdims=True) + EPS)
    return rs, x * rs


def _slot(n):
    return 4 * (n % 2) + n // 2


def _const(shape):
    n = len(shape)
    return pl.BlockSpec(shape, lambda *_: (0,) * n)


def _sum4(p):
    return (p[0] + p[1]) + (p[2] + p[3])


def _sum_slots(s_ref):
    vals = [s_ref[k] for k in range(s_ref.shape[0])]
    while len(vals) > 1:
        vals = [vals[i] + vals[i + 1] for i in range(0, len(vals), 2)]
    return vals[0]


def _s5_param_fn(log_dt, lam_re, lam_im, bt_re, bt_im):
    dt = jnp.exp(log_dt)
    mag = jnp.exp(lam_re * dt)
    ang = lam_im * dt
    abar_re = mag * jnp.cos(ang)
    abar_im = mag * jnp.sin(ang)
    num_re = abar_re - 1.0
    num_im = abar_im
    den = lam_re * lam_re + lam_im * lam_im
    coef_re = (num_re * lam_re + num_im * lam_im) / den
    coef_im = (num_im * lam_re - num_re * lam_im) / den
    bbar_re = coef_re[..., None, :] * bt_re - coef_im[..., None, :] * bt_im
    bbar_im = coef_re[..., None, :] * bt_im + coef_im[..., None, :] * bt_re
    return abar_re, abar_im, bbar_re, bbar_im


def _s5_params(log_dt, lam_re, lam_im, bt_re, bt_im):
    def body(ld, lr, li, br, bi, o_ar, o_ai, o_br, o_bi):
        ar, ai, bbr, bbi = _s5_param_fn(ld[...], lr[...], li[...], br[...], bi[...])
        o_ar[...] = ar
        o_ai[...] = ai
        o_br[...] = bbr
        o_bi[...] = bbi

    return pl.pallas_call(
        body, name="s5_params",
        out_shape=(SDS(lam_re.shape, F32), SDS(lam_re.shape, F32), SDS(bt_re.shape, F32), SDS(bt_re.shape, F32)),
    )(log_dt, lam_re, lam_im, bt_re, bt_im)


def _s5_params_bwd(layer, log_dt, lam_re, lam_im, bt_re, bt_im, g_ar, g_ai, g_br, g_bi):
    def body(ld, lr, li, br, bi, car, cai, cbr, cbi, o_ld, o_lr, o_li, o_br, o_bi):
        _, vjp = jax.vjp(_s5_param_fn, ld[...], lr[...], li[...], br[...], bi[...])
        d_ld, d_lr, d_li, d_br, d_bi = vjp((car[...], cai[...], cbr[...], cbi[...]))
        o_ld[...] = d_ld
        o_lr[...] = d_lr
        o_li[...] = d_li
        o_br[...] = d_br
        o_bi[...] = d_bi

    one = lambda shape: pl.BlockSpec((None,) + shape, lambda i: (layer,) + (0,) * len(shape))
    whole = lambda shape: _const(shape)
    vec, lam, mat = (N_GROUP, 1), (N_GROUP, STATE), (N_GROUP, GROUP_W, STATE)
    return pl.pallas_call(
        body, name=f"s5_params_bwd_l{layer}", grid=(1,),
        in_specs=[one(vec), one(lam), one(lam), one(mat), one(mat), whole(lam), whole(lam), whole(mat), whole(mat)],
        out_specs=(whole(vec), whole(lam), whole(lam), whole(mat), whole(mat)),
        out_shape=(SDS(vec, F32), SDS(lam, F32), SDS(lam, F32), SDS(mat, F32), SDS(mat, F32)),
    )(log_dt, lam_re, lam_im, bt_re, bt_im, g_ar, g_ai, g_br, g_bi)


def _norm_proj(layer, x, norm_g, wg_in, b_in, carry=None):
    def body(x_ref, g_ref, w_ref, b_ref, o_ref):
        _, xn = _rms(x_ref[...])
        h = (xn * g_ref[layer:layer + 1, :]).astype(BF16)
        for k in range(N_DEV):
            cols = slice(k * WIDTH, (k + 1) * WIDTH)
            o_ref[:, cols] = _dot(h, w_ref[k]) + b_ref[layer:layer + 1, cols]

    (proj,), moved = _pcall(
        body, name=f"norm_proj_l{layer}",
        out_shape=[SDS((SEQ, N_IN), F32)],
        grid=(SEQ // TILE_M,),
        in_specs=[pl.BlockSpec((TILE_M, D_MODEL), lambda i: (i, 0)),
                  _const((DEPTH, D_MODEL)),
                  _const((N_DEV, D_MODEL, WIDTH)),
                  _const((DEPTH, N_IN))],
        out_specs=[pl.BlockSpec((TILE_M, N_IN), lambda i: (i, 0))],
        args=[x, norm_g, wg_in, b_in], sem=("parallel",), limit=VMEM_LIMIT_BIG, carry=carry)
    return proj, moved


TIME_BLK = 512
N_TBLK = SEQ // TIME_BLK
N_PANEL = CH_S // 128
STATE_SHAPE = (N_PANEL, SEQ * 8, 128)


def _s5_layer_specs(layer):
    mat = lambda: pl.BlockSpec((None, N_GROUP, GROUP_W, STATE), lambda i: (layer, 0, 0, 0))
    ab = lambda: pl.BlockSpec((None, N_GROUP, STATE), lambda i: (layer, 0, 0))
    return [mat(), mat(), mat(), mat(), ab(), ab(), _const((DEPTH, WIDTH))]


def _s5_layer_scratch():
    return [pltpu.VMEM((N_CHUNK, CH_W, CH_S), BF16)] * 4 + [pltpu.VMEM((8, CH_S), F32)] * 2


def _s5_layer_fill(btre_ref, btim_ref, cre_ref, cim_ref, are_ref, aim_ref, bdre, bdim, ctre, ctim, a1, a2):
    for m in (bdre, bdim, ctre, ctim):
        m[...] = jnp.zeros_like(m)
    for grp in range(N_GROUP):
        k, g = divmod(grp, CH_G)
        rows = slice(g * GROUP_W, (g + 1) * GROUP_W)
        cols = slice(g * STATE, (g + 1) * STATE)
        bdre[k, rows, cols] = btre_ref[grp].astype(BF16)
        bdim[k, rows, cols] = btim_ref[grp].astype(BF16)
        ctre[k, rows, cols] = cre_ref[grp].astype(BF16)
        ctim[k, rows, cols] = cim_ref[grp].astype(BF16)
        ar = are_ref[grp:grp + 1, :]
        ai = aim_ref[grp:grp + 1, :]
        a1[k:k + 1, cols] = ar
        a1[N_CHUNK + k:N_CHUNK + k + 1, cols] = ar
        a2[k:k + 1, cols] = -ai
        a2[N_CHUNK + k:N_CHUNK + k + 1, cols] = ai


SCAN_UNROLL = 8


def _panels(tile):
    return [tile[:, p * 128:(p + 1) * 128] for p in range(N_PANEL)]


def _rows_load(ref, row):
    return jnp.concatenate([ref[p, pl.ds(row, TIME_BLK, stride=8), :] for p in range(N_PANEL)], axis=1)


def _rows_store(ref, row, val):
    for p in range(N_PANEL):
        ref[p, pl.ds(row, TIME_BLK, stride=8), :] = val[:, p * 128:(p + 1) * 128]


def _s5_scan_fwd(layer, proj, bbt_re, bbt_im, c_re, c_im, abar_re, abar_im, d_skip, carry=None):
    def body(u_ref, btre_ref, btim_ref, cre_ref, cim_ref, are_ref, aim_ref, d_ref, s_ref, y_ref,
             bdre, bdim, ctre, ctim, a1, a2, state):
        @pl.when(pl.program_id(0) == 0)
        def _():
            _s5_layer_fill(btre_ref, btim_ref, cre_ref, cim_ref, are_ref, aim_ref, bdre, bdim, ctre, ctim, a1, a2)
            state[...] = jnp.zeros_like(state)

        for k in range(N_CHUNK):
            ub = u_ref[:, k * CH_W:(k + 1) * CH_W].astype(BF16)
            _rows_store(s_ref, k, _dot(ub, bdre[k]))
            _rows_store(s_ref, N_CHUNK + k, _dot(ub, bdim[k]))
        m1 = _panels(a1[...])
        m2 = _panels(a2[...])

        def steps(n, tile):
            for r in range(SCAN_UNROLL):
                rows = pl.ds(pl.multiple_of((n * SCAN_UNROLL + r) * 8, 8), 8)
                tile = [m1[p] * tile[p] + m2[p] * pltpu.roll(tile[p], N_CHUNK, 0) + s_ref[p, rows, :]
                        for p in range(N_PANEL)]
                for p in range(N_PANEL):
                    s_ref[p, rows, :] = tile[p]
            return tile

        tile = lax.fori_loop(0, TIME_BLK // SCAN_UNROLL, steps, _panels(state[...]))
        state[...] = jnp.concatenate(tile, axis=1)
        d = d_ref[layer:layer + 1, :]
        for k in range(N_CHUNK):
            cols = slice(k * CH_W, (k + 1) * CH_W)
            y = (_dot_nt(_rows_load(s_ref, k).astype(BF16), ctre[k])
                 - _dot_nt(_rows_load(s_ref, N_CHUNK + k).astype(BF16), ctim[k]))
            y_ref[:, cols] = y + d[:, cols] * u_ref[:, cols]

    return _pcall(
        body, name=f"s5_fwd_l{layer}",
        out_shape=(SDS(STATE_SHAPE, F32), SDS((SEQ, WIDTH), F32)),
        grid=(N_TBLK,),
        in_specs=[pl.BlockSpec((TIME_BLK, WIDTH), lambda i: (i, 0))] + _s5_layer_specs(layer),
        out_specs=(pl.BlockSpec((N_PANEL, TIME_BLK * 8, 128), lambda i: (0, i, 0)),
                   pl.BlockSpec((TIME_BLK, WIDTH), lambda i: (i, 0))),
        scratch_shapes=_s5_layer_scratch() + [pltpu.VMEM((8, CH_S), F32)],
        args=[proj, bbt_re, bbt_im, c_re, c_im, abar_re, abar_im, d_skip], sem=("arbitrary",), carry=carry)


def _s5_scan_bwd(layer, dy0, proj, states, bbt_re, bbt_im, c_re, c_im, abar_re, abar_im, d_skip, dproj,
                 carry=None):
    def body(dy_ref, u_ref, s_ref, sprev_ref, btre_ref, btim_ref, cre_ref, cim_ref, are_ref, aim_ref, d_ref, _,
             du_ref, gbre_ref, gbim_ref, gcre_ref, gcim_ref, gare_ref, gaim_ref, gd_ref,
             lam_ref, bdre, bdim, ctre, ctim, a1, a2, state, acc1, acc2, gbre, gbim, gcre, gcim, gd):
        step_id = pl.program_id(0)

        @pl.when(step_id == 0)
        def _():
            _s5_layer_fill(btre_ref, btim_ref, cre_ref, cim_ref, are_ref, aim_ref, bdre, bdim, ctre, ctim, a1, a2)
            for r in (state, acc1, acc2, gbre, gbim, gcre, gcim, gd):
                r[...] = jnp.zeros_like(r)

        for k in range(N_CHUNK):
            dyb = dy_ref[:, k * CH_W:(k + 1) * CH_W].astype(BF16)
            _rows_store(lam_ref, k, _dot(dyb, ctre[k]))
            _rows_store(lam_ref, N_CHUNK + k, -_dot(dyb, ctim[k]))
            gcre[k] += _dot_tn(dyb, _rows_load(s_ref, k).astype(BF16))
            gcim[k] -= _dot_tn(dyb, _rows_load(s_ref, N_CHUNK + k).astype(BF16))

        m1 = _panels(a1[...])
        m2 = _panels(-a2[...])
        has_before = (step_id < N_TBLK - 1).astype(F32)

        def one(t8, c, first_token):
            tile, swapped, p1, p2 = c
            rows = pl.ds(t8, 8)
            tile = [m1[p] * tile[p] + m2[p] * swapped[p] + lam_ref[p, rows, :] for p in range(N_PANEL)]
            swapped = [pltpu.roll(tile[p], N_CHUNK, 0) for p in range(N_PANEL)]
            for p in range(N_PANEL):
                lam_ref[p, rows, :] = tile[p]
            if first_token:
                before = [sprev_ref[p] * has_before for p in range(N_PANEL)]
            else:
                before = [s_ref[p, pl.ds(t8 - 8, 8), :] for p in range(N_PANEL)]
            p1 = [p1[p] + tile[p] * before[p] for p in range(N_PANEL)]
            p2 = [p2[p] + swapped[p] * before[p] for p in range(N_PANEL)]
            return tile, swapped, p1, p2

        def steps(n, c):
            for r in range(SCAN_UNROLL):
                t8 = pl.multiple_of((TIME_BLK - 1 - (n * SCAN_UNROLL + r)) * 8, 8)
                c = one(t8, c, False)
            return c

        tile0 = _panels(state[...])
        c = (tile0, [pltpu.roll(t, N_CHUNK, 0) for t in tile0], _panels(acc1[...]), _panels(acc2[...]))
        c = lax.fori_loop(0, TIME_BLK // SCAN_UNROLL - 1, steps, c)
        for r in range(SCAN_UNROLL - 1, -1, -1):
            c = one(r * 8, c, r == 0)
        state[...] = jnp.concatenate(c[0], axis=1)
        acc1[...] = jnp.concatenate(c[2], axis=1)
        acc2[...] = jnp.concatenate(c[3], axis=1)

        d = d_ref[layer:layer + 1, :]
        for k in range(N_CHUNK):
            cols = slice(k * CH_W, (k + 1) * CH_W)
            lrb = _rows_load(lam_ref, k).astype(BF16)
            lib = _rows_load(lam_ref, N_CHUNK + k).astype(BF16)
            u = u_ref[:, cols]
            ub = u.astype(BF16)
            dy = dy_ref[:, cols]
            du = dy * d[:, cols] + _dot_nt(lrb, bdre[k]) + _dot_nt(lib, bdim[k])
            du_ref[:, cols] = du.astype(BF16)
            gbre[k] += _dot_tn(ub, lrb)
            gbim[k] += _dot_tn(ub, lib)
        gd[...] += jnp.sum(dy_ref[...] * u_ref[...], axis=0, keepdims=True)

        @pl.when(step_id == N_TBLK - 1)
        def _():
            gd_ref[...] = gd[...]
            ga_re = acc1[0:N_CHUNK, :] + acc1[N_CHUNK:, :]
            ga_im = acc2[0:N_CHUNK, :] - acc2[N_CHUNK:, :]
            for grp in range(N_GROUP):
                k, g = divmod(grp, CH_G)
                rows = slice(g * GROUP_W, (g + 1) * GROUP_W)
                cols = slice(g * STATE, (g + 1) * STATE)
                gcre_ref[grp] = gcre[k, rows, cols]
                gcim_ref[grp] = gcim[k, rows, cols]
                gbre_ref[grp] = gbre[k, rows, cols]
                gbim_ref[grp] = gbim[k, rows, cols]
                gare_ref[grp:grp + 1, :] = ga_re[k:k + 1, cols]
                gaim_ref[grp:grp + 1, :] = ga_im[k:k + 1, cols]

    back = lambda i: N_TBLK - 1 - i
    tok = lambda: pl.BlockSpec((TIME_BLK, WIDTH), lambda i: (back(i), 0))
    mat = lambda: _const((N_GROUP, GROUP_W, STATE))
    acc_mat = pltpu.VMEM((N_CHUNK, CH_W, CH_S), F32)
    return _pcall(
        body, name=f"s5_bwd_l{layer}",
        out_shape=(SDS((SEQ, N_IN), BF16), SDS((N_GROUP, GROUP_W, STATE), F32), SDS((N_GROUP, GROUP_W, STATE), F32),
                   SDS((N_GROUP, GROUP_W, STATE), F32), SDS((N_GROUP, GROUP_W, STATE), F32),
                   SDS((N_GROUP, STATE), F32), SDS((N_GROUP, STATE), F32), SDS((1, WIDTH), F32)),
        grid=(N_TBLK,),
        in_specs=[tok(), tok(),
                  pl.BlockSpec((N_PANEL, TIME_BLK * 8, 128), lambda i: (0, back(i), 0)),
                  pl.BlockSpec((N_PANEL, 8, 128), lambda i: (0, jnp.maximum(back(i) * TIME_BLK - 1, 0), 0))]
        + _s5_layer_specs(layer) + [ANY],
        out_specs=(tok(), mat(), mat(), mat(), mat(), _const((N_GROUP, STATE)), _const((N_GROUP, STATE)),
                   _const((1, WIDTH))),
        scratch_shapes=[pltpu.VMEM((N_PANEL, TIME_BLK * 8, 128), F32)] + _s5_layer_scratch()
        + [pltpu.VMEM((8, CH_S), F32)] * 3 + [acc_mat] * 4 + [pltpu.VMEM((1, WIDTH), F32)],
        args=[dy0, proj, states, states, bbt_re, bbt_im, c_re, c_im, abar_re, abar_im, d_skip, dproj],
        aliases={11: 0}, sem=("arbitrary",), limit=VMEM_LIMIT_BIG, carry=carry)


def _pool_counts(win):
    t = lax.broadcasted_iota(jnp.int32, (SEQ, POOL_GROUP), 0)
    return t, jnp.minimum(t + 1, win).astype(F32)


def _pool_fwd(layer, proj):
    def body(u_ref, o_ref):
        for gi, win in enumerate(POOL_WINDOWS):
            cols = slice(gi * POOL_GROUP, (gi + 1) * POOL_GROUP)
            u = u_ref[:, cols]
            t, count = _pool_counts(win)
            acc = u
            k = 1
            while k < win:
                acc = acc + jnp.where(t >= k, pltpu.roll(acc, k, 0), 0.0)
                k *= 2
            o_ref[:, cols] = acc / count - u

    return pl.pallas_call(
        body, name=f"pool_fwd_l{layer}",
        out_shape=SDS((SEQ, WIDTH), F32),
        grid=(1,),
        in_specs=[pl.BlockSpec((SEQ, WIDTH), lambda i: (0, 2))],
        out_specs=pl.BlockSpec((SEQ, WIDTH), lambda i: (0, 0)),
        compiler_params=_cp(("arbitrary",)),
    )(proj)


def _gelu_parts(y0):
    t = jnp.tanh(GELU_C * (y0 + GELU_A * (y0 * y0 * y0)))
    return t, 0.5 * y0 * (1.0 + t)


def _mix_forward(layer, p_ref, y0_ref, pooled_ref, wglu_ref, bglu_ref, pw_ref, scale_ref, wa_ref, wb_ref):
    za = p_ref[:, WIDTH:2 * WIDTH]
    zb = p_ref[:, 3 * WIDTH:4 * WIDTH]
    ga = p_ref[:, 4 * WIDTH:4 * WIDTH + D_MODEL]
    gb = p_ref[:, 4 * WIDTH + D_MODEL:]
    y0 = y0_ref[...]
    t, y1 = _gelu_parts(y0)
    y1b = y1.astype(BF16)
    q = _dot(y1b, wglu_ref[...].reshape(WIDTH, WIDTH)) + bglu_ref[layer:layer + 1, :]
    sq = _sig(q)
    y2 = y1 * sq
    sza = _sig(za)
    silu_za = za * sza
    ya = y2 * silu_za
    pooled = pooled_ref[...]
    mixed = jnp.concatenate(
        [_dot(pooled[:, g * POOL_GROUP:(g + 1) * POOL_GROUP].astype(BF16), pw_ref[g].astype(BF16))
         for g in range(len(POOL_WINDOWS))], axis=1)
    szb = _sig(zb)
    silu_zb = zb * szb
    scale = scale_ref[layer:layer + 1, :]
    ms = mixed * scale
    yb = ms * silu_zb
    yab = ya.astype(BF16)
    ybb = yb.astype(BF16)
    ma = _dot(yab, wa_ref[...])
    mb = _dot(ybb, wb_ref[...])
    sga = _sig(ga)
    sgb = _sig(gb)
    merged = sga * ma + sgb * mb
    return dict(za=za, zb=zb, y0=y0, t=t, y1=y1, y1b=y1b, sq=sq, y2=y2, sza=sza, silu_za=silu_za,
                pooled=pooled, mixed=mixed, szb=szb, silu_zb=silu_zb, scale=scale, ms=ms, yab=yab, ybb=ybb,
                ma=ma, mb=mb, sga=sga, sgb=sgb, merged=merged)


def _mix_weight_specs(layer):
    return [_const((N_DEV, WIDTH // N_DEV, WIDTH)),
            _const((DEPTH, WIDTH)),
            pl.BlockSpec((None, 4, POOL_GROUP, POOL_GROUP), lambda i: (layer, 0, 0, 0)),
            _const((DEPTH, WIDTH)),
            _const((WIDTH, D_MODEL)),
            _const((WIDTH, D_MODEL)),
            _const((N_DEV, D_MODEL // N_DEV, D_MODEL))]


def _mix_fwd(layer, x, proj, y0, pooled, wg_glu, b_glu, pool_w, pool_scale, wg_a, wg_b, wg_out, carry=None):
    def body(x_ref, p_ref, y0_ref, pooled_ref, wglu_ref, bglu_ref, pw_ref, scale_ref, wa_ref, wb_ref,
             wout_ref, o_ref):
        f = _mix_forward(layer, p_ref, y0_ref, pooled_ref, wglu_ref, bglu_ref, pw_ref, scale_ref, wa_ref, wb_ref)
        wout = wout_ref[...].reshape(D_MODEL, D_MODEL)
        o_ref[...] = x_ref[...] + _dot(f["merged"].astype(BF16), wout)

    (x_next,), moved = _pcall(
        body, name=f"mix_fwd_l{layer}",
        out_shape=[SDS((SEQ, D_MODEL), F32)],
        grid=(SEQ // TILE_M,),
        in_specs=[pl.BlockSpec((TILE_M, D_MODEL), lambda i: (i, 0)),
                  pl.BlockSpec((TILE_M, N_IN), lambda i: (i, 0)),
                  pl.BlockSpec((TILE_M, WIDTH), lambda i: (i, 0)),
                  pl.BlockSpec((TILE_M, WIDTH), lambda i: (i, 0))] + _mix_weight_specs(layer),
        out_specs=[pl.BlockSpec((TILE_M, D_MODEL), lambda i: (i, 0))],
        args=[x, proj, y0, pooled, wg_glu, b_glu, pool_w, pool_scale, wg_a, wg_b, wg_out],
        sem=("parallel",), limit=VMEM_LIMIT_BIG, carry=carry)
    return x_next, moved


def _loss_head(x, target, final_g):
    def body(x_ref, t_ref, g_ref, dx_ref, loss_ref, gg_ref):
        @pl.when(pl.program_id(0) == 0)
        def _():
            loss_ref[...] = jnp.zeros_like(loss_ref)
            gg_ref[...] = jnp.zeros_like(gg_ref)

        g = g_ref[...]
        rs, xn = _rms(x_ref[...])
        err = xn * g - t_ref[...]
        loss_ref[...] += 0.5 * jnp.sum(jnp.mean(err * err, axis=-1, keepdims=True), axis=0, keepdims=True)
        dy = err * (1.0 / D_MODEL)
        gg_ref[...] += jnp.sum(dy * xn, axis=0, keepdims=True)
        dxn = dy * g
        dx_ref[...] = rs * (dxn - xn * jnp.mean(dxn * xn, axis=-1, keepdims=True))

    return pl.pallas_call(
        body, name="loss_head",
        out_shape=(SDS((SEQ, D_MODEL), F32), SDS((1, 1), F32), SDS((1, D_MODEL), F32)),
        grid=(SEQ // TILE_M,),
        in_specs=[pl.BlockSpec((TILE_M, D_MODEL), lambda i: (i, 0)),
                  pl.BlockSpec((TILE_M, D_MODEL), lambda i: (i, 0)),
                  _const((1, D_MODEL))],
        out_specs=(pl.BlockSpec((TILE_M, D_MODEL), lambda i: (i, 0)), _const((1, 1)), _const((1, D_MODEL))),
        compiler_params=_cp(("arbitrary",)),
    )(x, target, final_g)


def _big_shapes():
    return dict(w_out=(DEPTH, N_DEV, D_MODEL // N_DEV, D_MODEL), w_branch_a=(DEPTH, N_DEV, WIDTH, D_MODEL // N_DEV),
                w_branch_b=(DEPTH, N_DEV, WIDTH, D_MODEL // N_DEV), ssm_w_glu=(DEPTH, N_DEV, WIDTH // N_DEV, WIDTH),
                w_in=(DEPTH, N_DEV, D_MODEL, WIDTH))


def _mix_bwd(layer, dx_next, proj, y0, pooled, wg_glu, b_glu, pool_w, pool_scale, wg_a, wg_b, wg_out, prev,
             carry=None):
    n_k = N_DEV
    n_prev = 0 if prev is None else len(prev)

    def body(*refs):
        (dx_ref, p_ref, y0_ref, pooled_ref, wglu_ref, bglu_ref, pw_ref, scale_ref, wa_ref, wb_ref,
         wout_ref) = refs[:11]
        (dproj_ref, dy0_ref, dpooled_ref, gwout_ref, gwa_ref, gwb_ref, gwglu_ref, gpw_ref,
         gscale_ref, gbglu_ref) = refs[11 + n_prev:]

        @pl.when(pl.program_id(0) == 0)
        def _():
            for r in (gwout_ref, gwa_ref, gwb_ref, gwglu_ref, gpw_ref, gscale_ref, gbglu_ref):
                r[...] = jnp.zeros_like(r)

        f = _mix_forward(layer, p_ref, y0_ref, pooled_ref, wglu_ref, bglu_ref, pw_ref, scale_ref, wa_ref, wb_ref)
        wglu = wglu_ref[...].reshape(WIDTH, WIDTH)
        wout = wout_ref[...].reshape(D_MODEL, D_MODEL)
        blk = D_MODEL // n_k
        dxb = dx_ref[...].astype(BF16)
        dmerged = _dot_nt(dxb, wout)
        gwout = _dot_tn(f["merged"].astype(BF16), dxb)
        for k in range(n_k):
            gwout_ref[_slot(k)] += gwout[k * blk:(k + 1) * blk, :]
        dma = dmerged * f["sga"]
        dmb = dmerged * f["sgb"]
        dga = dmerged * f["ma"] * f["sga"] * (1.0 - f["sga"])
        dgb = dmerged * f["mb"] * f["sgb"] * (1.0 - f["sgb"])
        dmab = dma.astype(BF16)
        dmbb = dmb.astype(BF16)
        dya = _dot_nt(dmab, wa_ref[...])
        dyb = _dot_nt(dmbb, wb_ref[...])
        gwa = _dot_tn(f["yab"], dmab)
        gwb = _dot_tn(f["ybb"], dmbb)
        for k in range(n_k):
            gwa_ref[_slot(k)] += gwa[:, k * blk:(k + 1) * blk]
            gwb_ref[_slot(k)] += gwb[:, k * blk:(k + 1) * blk]
        zb, szb = f["zb"], f["szb"]
        dzb = dyb * f["ms"] * (szb * (1.0 + zb * (1.0 - szb)))
        dms = dyb * f["silu_zb"]
        gscale_ref[...] += jnp.sum(dms * f["mixed"], axis=0, keepdims=True)
        dmixed = (dms * f["scale"]).astype(BF16)
        pooled = f["pooled"]
        for g in range(len(POOL_WINDOWS)):
            cols = slice(g * POOL_GROUP, (g + 1) * POOL_GROUP)
            dpooled_ref[:, cols] = _dot_nt(dmixed[:, cols], pw_ref[g].astype(BF16))
            gpw_ref[g] += _dot_tn(pooled[:, cols].astype(BF16), dmixed[:, cols])
        za, sza = f["za"], f["sza"]
        dza = dya * f["y2"] * (sza * (1.0 + za * (1.0 - sza)))
        dy2 = dya * f["silu_za"]
        sq = f["sq"]
        dq = dy2 * f["y1"] * sq * (1.0 - sq)
        dqb = dq.astype(BF16)
        dy1 = dy2 * sq + _dot_nt(dqb, wglu)
        gwglu = _dot_tn(f["y1b"], dqb)
        rblk = WIDTH // n_k
        for k in range(n_k):
            gwglu_ref[_slot(k)] += gwglu[k * rblk:(k + 1) * rblk, :]
        gbglu_ref[...] += jnp.sum(dq, axis=0, keepdims=True)
        y0, t = f["y0"], f["t"]
        dgelu = 0.5 * (1.0 + t) + 0.5 * y0 * (1.0 - t * t) * (GELU_C * (1.0 + 3.0 * GELU_A * y0 * y0))
        dy0_ref[...] = dy1 * dgelu
        zeros = jnp.zeros((TILE_MB, WIDTH), BF16)
        dproj_ref[:, 0:WIDTH] = zeros
        dproj_ref[:, WIDTH:2 * WIDTH] = dza.astype(BF16)
        dproj_ref[:, 2 * WIDTH:3 * WIDTH] = zeros
        dproj_ref[:, 3 * WIDTH:4 * WIDTH] = dzb.astype(BF16)
        dproj_ref[:, 4 * WIDTH:4 * WIDTH + D_MODEL] = dga.astype(BF16)
        dproj_ref[:, 4 * WIDTH + D_MODEL:] = dgb.astype(BF16)

    tile = lambda w: pl.BlockSpec((TILE_MB, w), lambda i: (i, 0))
    shapes = _big_shapes()
    big = ["w_out", "w_branch_a", "w_branch_b", "ssm_w_glu"]
    slab = lambda n: pl.BlockSpec((None,) + shapes[n][1:], lambda i: (layer, 0, 0, 0))
    args = [dx_next, proj, y0, pooled, wg_glu, b_glu, pool_w, pool_scale, wg_a, wg_b, wg_out]
    return _pcall(
        body, name=f"mix_bwd_l{layer}",
        out_shape=(SDS((SEQ, N_IN), BF16), SDS((SEQ, WIDTH), F32), SDS((SEQ, WIDTH), F32))
        + tuple(SDS(shapes[n], F32) for n in big)
        + (SDS((4, POOL_GROUP, POOL_GROUP), F32), SDS((1, WIDTH), F32), SDS((1, WIDTH), F32)),
        grid=(SEQ // TILE_MB,),
        in_specs=[tile(D_MODEL), tile(N_IN), tile(WIDTH), tile(WIDTH)] + _mix_weight_specs(layer) + [ANY] * n_prev,
        out_specs=(tile(N_IN), tile(WIDTH), tile(WIDTH)) + tuple(slab(n) for n in big)
        + (_const((4, POOL_GROUP, POOL_GROUP)), _const((1, WIDTH)), _const((1, WIDTH))),
        args=args + list(prev or ()),
        aliases={len(args) + i: 3 + i for i in range(n_prev)},
        sem=("arbitrary",), limit=VMEM_LIMIT_BIG, carry=carry)


def _pool_bwd(layer, dpooled, dproj):
    def body(dp_ref, _, o_ref):
        for gi, win in enumerate(POOL_WINDOWS):
            cols = slice(gi * POOL_GROUP, (gi + 1) * POOL_GROUP)
            dp = dp_ref[:, cols]
            t, count = _pool_counts(win)
            e = dp / count
            acc = e
            k = 1
            while k < win:
                acc = acc + jnp.where(t < SEQ - k, pltpu.roll(acc, SEQ - k, 0), 0.0)
                k *= 2
            o_ref[:, cols] = (acc - dp).astype(BF16)

    return pl.pallas_call(
        body, name=f"pool_bwd_l{layer}",
        out_shape=SDS((SEQ, N_IN), BF16),
        grid=(1,),
        in_specs=[pl.BlockSpec((SEQ, WIDTH), lambda i: (0, 0)), ANY],
        out_specs=pl.BlockSpec((SEQ, WIDTH), lambda i: (0, 2)),
        input_output_aliases={1: 0},
        compiler_params=_cp(("arbitrary",)),
    )(dpooled, dproj)


def _proj_wgrad(layer, x, norm_g, dproj, prev, carry=None):
    tm = 512
    n_prev = 0 if prev is None else 1

    def body(*refs):
        x_ref, g_ref, dp_ref = refs[:3]
        gw_ref, gb_ref, ht_ref = refs[3 + n_prev:]
        n, t = pl.program_id(0), pl.program_id(1)

        @pl.when(t == 0)
        def _():
            gw_ref[...] = jnp.zeros_like(gw_ref)
            gb_ref[...] = jnp.zeros_like(gb_ref)

        @pl.when(n == 0)
        def _():
            _, xn = _rms(x_ref[...])
            ht_ref[t] = (xn * g_ref[layer:layer + 1, :]).T.astype(BF16)

        dp = dp_ref[...]
        gw_ref[...] += _dot(ht_ref[t], dp)
        gb_ref[...] += jnp.sum(dp.astype(F32), axis=0, keepdims=True)

    return _pcall(
        body, name=f"proj_wgrad_l{layer}",
        out_shape=(SDS(_big_shapes()["w_in"], F32), SDS((1, N_IN), F32)),
        grid=(N_DEV, SEQ // tm),
        in_specs=[pl.BlockSpec((tm, D_MODEL), lambda n, t: (jnp.where(n == 0, t, 0), 0)),
                  _const((DEPTH, D_MODEL)),
                  pl.BlockSpec((tm, WIDTH), lambda n, t: (t, n))] + [ANY] * n_prev,
        out_specs=(pl.BlockSpec((None, None, D_MODEL, WIDTH), lambda n, t: (layer, _slot(n), 0, 0)),
                   pl.BlockSpec((1, WIDTH), lambda n, t: (0, n))),
        scratch_shapes=[pltpu.VMEM((SEQ // tm, D_MODEL, tm), BF16)],
        args=[x, norm_g, dproj] + ([prev] if n_prev else []),
        aliases={3: 0} if n_prev else {}, sem=("arbitrary", "arbitrary"), carry=carry)


def _proj_dgrad(layer, dx_next, x, norm_g, dproj, wg_in, carry=None):
    def body(dxn_ref, x_ref, g_ref, dp_ref, w_ref, dx_ref, gg_ref):
        @pl.when(pl.program_id(0) == 0)
        def _():
            gg_ref[...] = jnp.zeros_like(gg_ref)

        dh = jnp.zeros((TILE_M, D_MODEL), F32)
        for k in range(N_DEV):
            dh = dh + _dot_nt(dp_ref[:, k * WIDTH:(k + 1) * WIDTH], w_ref[k])
        rs, xn = _rms(x_ref[...])
        gg_ref[...] += jnp.sum(dh * xn, axis=0, keepdims=True)
        dxn = dh * g_ref[layer:layer + 1, :]
        dx_ref[...] = dxn_ref[...] + rs * (dxn - xn * jnp.mean(dxn * xn, axis=-1, keepdims=True))

    return _pcall(
        body, name=f"proj_dgrad_l{layer}",
        out_shape=(SDS((SEQ, D_MODEL), F32), SDS((1, D_MODEL), F32)),
        grid=(SEQ // TILE_M,),
        in_specs=[pl.BlockSpec((TILE_M, D_MODEL), lambda i: (i, 0)),
                  pl.BlockSpec((TILE_M, D_MODEL), lambda i: (i, 0)),
                  _const((DEPTH, D_MODEL)),
                  pl.BlockSpec((TILE_M, N_IN), lambda i: (i, 0)),
                  _const((N_DEV, D_MODEL, WIDTH))],
        out_specs=(pl.BlockSpec((TILE_M, D_MODEL), lambda i: (i, 0)), _const((1, D_MODEL))),
        args=[dx_next, x, norm_g, dproj, wg_in], sem=("arbitrary",), limit=VMEM_LIMIT_BIG, carry=carry)


def _my_place():
    return lax.axis_index("x"), lax.axis_index("y"), lax.axis_index("c")


def _gather_plan(shards, layer, by_columns=()):
    n = len(shards)

    def parts(ins, outs, sems):
        send_sems, recv_sems, local_sems = sems
        x, y, c = _my_place()
        chips = [(1 - x, y), (x, 1 - y), (1 - x, 1 - y)]

        def rows(t, place):
            px, py, pc = place
            index = 4 * px + 2 * py + pc
            if t in by_columns:
                width = shards[t].shape[2]
                return outs[t].at[:, pl.ds(pl.multiple_of(index * width, 128), width)]
            return outs[t].at[index]

        def copy(t, k, block, to, from_src=False):
            return pltpu.make_async_remote_copy(
                src_ref=ins[t].at[layer] if from_src else rows(t, block), dst_ref=rows(t, block),
                send_sem=send_sems.at[7 * t + k], recv_sem=recv_sems.at[7 * t + k], device_id=to,
                device_id_type=MESH)

        def mine(t):
            return pltpu.make_async_copy(ins[t].at[layer], rows(t, (x, y, c)), local_sems.at[t])

        return (x, y, c), chips, copy, mine

    def start(ins, outs, sems):
        me, chips, copy, mine = parts(ins, outs, sems)
        x, y, c = me
        for t in range(n):
            mine(t).start()
            copy(t, 0, me, (x, y, 1 - c), from_src=True).start()
            for j, chip in enumerate(chips):
                copy(t, 1 + j, me, (*chip, c), from_src=True).start()

    def finish(ins, outs, sems):
        me, chips, copy, mine = parts(ins, outs, sems)
        x, y, c = me
        sibling = (x, y, 1 - c)
        for t in range(n):
            for j, chip in enumerate(chips):
                copy(t, 1 + j, (*chip, c), me).wait_recv()
                copy(t, 4 + j, (*chip, c), sibling).start()
        for t in range(n):
            copy(t, 0, sibling, me).wait_recv()
            for j, chip in enumerate(chips):
                copy(t, 4 + j, (*chip, 1 - c), me).wait_recv()
            for k in range(7):
                copy(t, k, me, sibling, from_src=k < 4).wait_send()
            mine(t).wait()

    out_shape = [SDS((a.shape[1], N_DEV * a.shape[2]) if t in by_columns else (N_DEV,) + a.shape[1:], a.dtype)
                 for t, a in enumerate(shards)]
    sems = [pltpu.SemaphoreType.DMA((7 * n,)), pltpu.SemaphoreType.DMA((7 * n,)), pltpu.SemaphoreType.DMA((n,))]
    return _Carried(shards, out_shape, sems, start, finish)


class _Carried:
    def __init__(self, ins, out_shape, sems, start, finish):
        self.ins, self.out_shape, self.sems = list(ins), list(out_shape), list(sems)
        self.start, self.finish = start, finish


def _pcall(body, *, name, grid, in_specs, out_specs, out_shape, args, scratch_shapes=(), aliases=None,
           sem=None, limit=VMEM_LIMIT, carry=None):
    out_shape, out_specs, scratch_shapes = list(out_shape), list(out_specs), list(scratch_shapes)
    n_in, n_out, n_scr = len(args), len(out_shape), len(scratch_shapes)
    if carry is None:
        kern, c_ins, c_out, c_sems = body, [], [], []
    else:
        c_ins, c_out, c_sems = carry.ins, carry.out_shape, carry.sems
        ci, co = len(c_ins), len(c_out)
        steps = tuple(grid)

        def kern(*refs):
            o0 = n_in + ci
            s0 = o0 + n_out + co
            mine = refs[:n_in] + refs[o0:o0 + n_out] + refs[s0:s0 + n_scr]
            theirs = (refs[n_in:o0], refs[o0 + n_out:s0], refs[s0 + n_scr:])
            first = pl.program_id(0) == 0
            last = pl.program_id(0) == steps[0] - 1
            for a in range(1, len(steps)):
                first = jnp.logical_and(first, pl.program_id(a) == 0)
                last = jnp.logical_and(last, pl.program_id(a) == steps[a] - 1)

            @pl.when(first)
            def _():
                carry.start(*theirs)

            body(*mine)

            @pl.when(last)
            def _():
                carry.finish(*theirs)

        sem = ("arbitrary",) * len(steps)
    res = pl.pallas_call(
        kern, name=name, grid=tuple(grid),
        in_specs=list(in_specs) + [ANY] * len(c_ins),
        out_specs=tuple(out_specs + [ANY] * len(c_out)),
        out_shape=tuple(out_shape + c_out),
        scratch_shapes=scratch_shapes + c_sems,
        input_output_aliases=aliases or {},
        compiler_params=_cp(sem, limit),
    )(*args, *c_ins)
    return res[:n_out], res[n_out:]


def _run_carried(name, carry):
    ci, co = len(carry.ins), len(carry.out_shape)

    def body(*refs):
        parts = (refs[:ci], refs[ci:ci + co], refs[ci + co:])
        carry.start(*parts)
        carry.finish(*parts)

    return pl.pallas_call(
        body, name=name, out_shape=tuple(carry.out_shape),
        in_specs=[ANY] * ci, out_specs=tuple([ANY] * co), scratch_shapes=carry.sems,
    )(*carry.ins)


def _sibling_plan(big, small):
    n = len(big)
    n_copies = 4 * n + len(small)

    def copies(ins, outs, sems):
        send_sems, recv_sems = sems
        x, y, c = _my_place()
        pairs = []
        for t, (_, layer) in enumerate(big):
            for s in range(4):
                pairs.append((ins[t].at[layer, pl.ds(4 * (1 - c) + s, 1)], outs[t].at[pl.ds(s, 1)]))
        pairs += list(zip(ins[n:], outs[n:]))
        return [pltpu.make_async_remote_copy(
            src_ref=src, dst_ref=dst, send_sem=send_sems.at[k], recv_sem=recv_sems.at[k],
            device_id=(x, y, 1 - c), device_id_type=MESH) for k, (src, dst) in enumerate(pairs)]

    def start(ins, outs, sems):
        for cp in copies(ins, outs, sems):
            cp.start()

    def finish(ins, outs, sems):
        for cp in copies(ins, outs, sems):
            cp.wait()

    out_shape = [SDS((4,) + a.shape[2:], a.dtype) for a, _ in big] + [SDS(a.shape, a.dtype) for a in small]
    sems = [pltpu.SemaphoreType.DMA((n_copies,)), pltpu.SemaphoreType.DMA((n_copies,))]
    return _Carried([a for a, _ in big] + list(small), out_shape, sems, start, finish)


def _chips_plan(big, small):
    n, n_small = len(big), len(small)
    max_rows = 512
    parts = [max(1, a.shape[1] // max_rows) for a in big]
    n_copies = 3 * (sum(parts) + n_small)

    def copies(ins, outs, sems, landing):
        send_sems, recv_sems, local_sems = sems
        x, y, c = _my_place()
        my_chip = 2 * x + y
        chips = [(1 - x, y), (x, 1 - y), (1 - x, 1 - y)]
        remote, local = [], []
        for chip in chips:
            to = 2 * chip[0] + chip[1]
            slot = to if landing else my_chip
            pairs = []
            for t in range(n):
                rows_per = big[t].shape[1] // parts[t]
                for p in range(parts[t]):
                    rows = pl.ds(p * rows_per, rows_per)
                    pairs.append((ins[t].at[to, rows], outs[t].at[slot, rows]))
            pairs += [(ins[t], outs[t].at[slot]) for t in range(n, n + n_small)]
            for src, dst in pairs:
                k = len(remote)
                remote.append(pltpu.make_async_remote_copy(
                    src_ref=src, dst_ref=dst, send_sem=send_sems.at[k], recv_sem=recv_sems.at[k],
                    device_id=(*chip, c), device_id_type=MESH))
        for t in range(n):
            local.append(pltpu.make_async_copy(ins[t].at[my_chip], outs[t].at[my_chip], local_sems.at[t]))
        for t in range(n, n + n_small):
            local.append(pltpu.make_async_copy(ins[t], outs[t].at[my_chip], local_sems.at[t]))
        return remote + local

    def start(ins, outs, sems):
        for cp in copies(ins, outs, sems, landing=False):
            cp.start()

    def finish(ins, outs, sems):
        for cp in copies(ins, outs, sems, landing=True):
            cp.wait()

    out_shape = [SDS(a.shape, a.dtype) for a in big] + [SDS((N_CHIP,) + a.shape, a.dtype) for a in small]
    sems = [pltpu.SemaphoreType.DMA((n_copies,)), pltpu.SemaphoreType.DMA((n_copies,)),
            pltpu.SemaphoreType.DMA((n + n_small,))]
    return _Carried(list(big) + list(small), out_shape, sems, start, finish)


def _all_plan(small):
    n = len(small)
    masks = [(m >> 2 & 1, m >> 1 & 1, m & 1) for m in range(1, N_DEV)]

    def copies(ins, outs, sems, landing):
        send_sems, recv_sems, local_sems = sems
        x, y, c = _my_place()
        me = 4 * x + 2 * y + c
        flip = lambda v, bit: 1 - v if bit else v
        remote = []
        for fx, fy, fc in masks:
            peer = (flip(x, fx), flip(y, fy), flip(c, fc))
            slot = 4 * peer[0] + 2 * peer[1] + peer[2] if landing else me
            for t in range(n):
                k = len(remote)
                remote.append(pltpu.make_async_remote_copy(
                    src_ref=ins[t], dst_ref=outs[t].at[slot], send_sem=send_sems.at[k], recv_sem=recv_sems.at[k],
                    device_id=peer, device_id_type=MESH))
        local = [pltpu.make_async_copy(ins[t], outs[t].at[me], local_sems.at[t]) for t in range(n)]
        return remote + local

    def start(ins, outs, sems):
        for cp in copies(ins, outs, sems, landing=False):
            cp.start()

    def finish(ins, outs, sems):
        for cp in copies(ins, outs, sems, landing=True):
            cp.wait()

    out_shape = [SDS((N_DEV,) + a.shape, a.dtype) for a in small]
    sems = [pltpu.SemaphoreType.DMA((7 * n,)), pltpu.SemaphoreType.DMA((7 * n,)), pltpu.SemaphoreType.DMA((n,))]
    return _Carried(list(small), out_shape, sems, start, finish)


def _join(*plans):
    plans = [p for p in plans if p is not None]
    if len(plans) <= 1:
        return plans[0] if plans else None

    def each(fn_name, ins, outs, sems):
        i = o = s = 0
        for p in plans:
            ni, no, ns = len(p.ins), len(p.out_shape), len(p.sems)
            getattr(p, fn_name)(ins[i:i + ni], outs[o:o + no], sems[s:s + ns])
            i, o, s = i + ni, o + no, s + ns

    return _Carried(sum((p.ins for p in plans), []), sum((p.out_shape for p in plans), []),
                    sum((p.sems for p in plans), []),
                    lambda i, o, s: each("start", i, o, s), lambda i, o, s: each("finish", i, o, s))


def _row_block(rows):
    return rows if rows <= 256 else 256


def _add_own(tag, core, g, layer, got):
    _, r, c = got.shape
    rb = _row_block(r)

    def body(core_ref, a_ref, b_ref, o_ref):
        o_ref[...] = (a_ref[...] + b_ref[...]).astype(o_ref.dtype)

    return pl.pallas_call(
        body, name=f"add_{tag}", out_shape=SDS(got.shape, BF16),
        grid_spec=pltpu.PrefetchScalarGridSpec(
            num_scalar_prefetch=1, grid=(4, r // rb),
            in_specs=[pl.BlockSpec((None, None, rb, c), lambda s, j, core: (layer, 4 * core[0] + s, j, 0)),
                      pl.BlockSpec((None, rb, c), lambda s, j, core: (s, j, 0))],
            out_specs=pl.BlockSpec((None, rb, c), lambda s, j, core: (s, j, 0))),
        compiler_params=_cp(("parallel", "parallel")),
    )(core, g, got)


def _add_lists(tag, own, got, grid=None, specs=None):
    n = len(own)

    def body(*refs):
        for a, b, o in zip(refs[:n], refs[n:2 * n], refs[2 * n:]):
            o[...] = a[...] + b[...]

    kw = {}
    if grid is not None:
        kw = dict(grid=grid, in_specs=list(specs) * 2, out_specs=tuple(specs),
                  compiler_params=_cp(("parallel",) * len(grid)))
    return pl.pallas_call(
        body, name=f"add_{tag}", out_shape=tuple(SDS(a.shape, a.dtype) for a in own), **kw)(*own, *got)


def _adamw_math(w, g, m, v):
    m = ADAM_B1 * m + (1.0 - ADAM_B1) * g
    v = ADAM_B2 * v + (1.0 - ADAM_B2) * (g * g)
    m_hat = m / (1.0 - ADAM_B1 ** ADAM_STEP)
    v_hat = v / (1.0 - ADAM_B2 ** ADAM_STEP)
    delta = -ADAM_LR * (m_hat / (jnp.sqrt(v_hat) + ADAM_EPS) + ADAM_WD * w)
    return delta, m, v


def _sum_slots_adamw(tag, slots, w, m, v):
    _, r, c = slots[0].shape
    rb = _row_block(r)

    def body(s0_ref, s1_ref, w_ref, m_ref, v_ref, g_ref, d_ref, nm_ref, nv_ref):
        first = pl.program_id(1) == 0
        g = _sum4([jnp.where(first, s0_ref[k], s1_ref[k]).astype(F32) for k in range(N_CHIP)])
        delta, nm, nv = _adamw_math(w_ref[...], g, m_ref[...], v_ref[...])
        g_ref[...] = g
        d_ref[...] = delta
        nm_ref[...] = nm
        nv_ref[...] = nv

    spec = pl.BlockSpec((None, rb, c), lambda j, l: (l, j, 0))
    sspec = pl.BlockSpec((N_CHIP, rb, c), lambda j, l: (0, j, 0))
    s = SDS((DEPTH, r, c), F32)
    return pl.pallas_call(
        body, name=f"adamw_{tag}", out_shape=(s, s, s, s),
        grid=(r // rb, DEPTH), in_specs=[sspec, sspec, spec, spec, spec], out_specs=(spec, spec, spec, spec),
        compiler_params=_cp(("parallel", "arbitrary")),
    )(*slots, w, m, v)


def _adamw_small(tag, entries, grid=None):
    flat_in, in_specs, out_shape, out_specs, layout = [], [], [], [], []
    for slots, w, m, v, slot_spec, w_spec in entries:
        per_layer = isinstance(slots, (list, tuple))
        n_slot = len(slots) if per_layer else 1
        flat_in += (list(slots) if per_layer else [slots]) + [w, m, v]
        in_specs += [slot_spec] * n_slot + [w_spec] * 3
        out_shape += [SDS(w.shape, F32)] * 4
        out_specs += [w_spec] * 4
        layout.append((per_layer, n_slot))
    n_in = len(flat_in)

    def body(*refs):
        i, o = 0, n_in
        for per_layer, n_slot in layout:
            s_refs = refs[i:i + n_slot]
            w_ref, m_ref, v_ref = refs[i + n_slot:i + n_slot + 3]
            outs = refs[o:o + 4]
            if per_layer:
                for l, s_ref in enumerate(s_refs):
                    at = (slice(l, l + 1),) if len(w_ref.shape) == 2 else (l,)
                    g = _sum_slots(s_ref)
                    res = (g,) + _adamw_math(w_ref[at], g, m_ref[at], v_ref[at])
                    for o_ref, val in zip(outs, res):
                        o_ref[at] = val
            else:
                g = _sum_slots(s_refs[0])
                res = (g,) + _adamw_math(w_ref[...], g, m_ref[...], v_ref[...])
                for o_ref, val in zip(outs, res):
                    o_ref[...] = val
            i += n_slot + 3
            o += 4

    kw = {}
    if grid is not None:
        kw = dict(grid=grid, in_specs=in_specs, out_specs=tuple(out_specs),
                  compiler_params=_cp(("parallel",) * len(grid)))
    res = pl.pallas_call(body, name=f"adamw_{tag}", out_shape=tuple(out_shape), **kw)(*flat_in)
    return [tuple(res[4 * e:4 * e + 4]) for e in range(len(entries))]


def kernel(x, norm_g, w_in, b_in, ssm_log_dt, ssm_lam_re, ssm_lam_im, ssm_b_re, ssm_b_im, ssm_c_re, ssm_c_im, ssm_d, ssm_w_glu, ssm_b_glu, pool_w, pool_scale, w_branch_a, w_branch_b, w_out, final_norm_g, loss_target, m_norm_g, m_w_in, m_b_in, m_ssm_log_dt, m_ssm_lam_re, m_ssm_lam_im, m_ssm_b_re, m_ssm_b_im, m_ssm_c_re, m_ssm_c_im, m_ssm_d, m_ssm_w_glu, m_ssm_b_glu, m_pool_w, m_pool_scale, m_w_branch_a, m_w_branch_b, m_w_out, m_final_norm_g, v_norm_g, v_w_in, v_b_in, v_ssm_log_dt, v_ssm_lam_re, v_ssm_lam_im, v_ssm_b_re, v_ssm_b_im, v_ssm_c_re, v_ssm_c_im, v_ssm_d, v_ssm_w_glu, v_ssm_b_glu, v_pool_w, v_pool_scale, v_w_branch_a, v_w_branch_b, v_w_out, v_final_norm_g):
    weights = dict(norm_g=norm_g, w_in=w_in, b_in=b_in, ssm_log_dt=ssm_log_dt, ssm_lam_re=ssm_lam_re,
                   ssm_lam_im=ssm_lam_im, ssm_b_re=ssm_b_re, ssm_b_im=ssm_b_im, ssm_c_re=ssm_c_re,
                   ssm_c_im=ssm_c_im, ssm_d=ssm_d, ssm_w_glu=ssm_w_glu, ssm_b_glu=ssm_b_glu, pool_w=pool_w,
                   pool_scale=pool_scale, w_branch_a=w_branch_a, w_branch_b=w_branch_b, w_out=w_out,
                   final_norm_g=final_norm_g.reshape(1, D_MODEL))
    mom_m = dict(norm_g=m_norm_g, w_in=m_w_in, b_in=m_b_in, ssm_log_dt=m_ssm_log_dt, ssm_lam_re=m_ssm_lam_re,
                 ssm_lam_im=m_ssm_lam_im, ssm_b_re=m_ssm_b_re, ssm_b_im=m_ssm_b_im, ssm_c_re=m_ssm_c_re,
                 ssm_c_im=m_ssm_c_im, ssm_d=m_ssm_d, ssm_w_glu=m_ssm_w_glu, ssm_b_glu=m_ssm_b_glu,
                 pool_w=m_pool_w, pool_scale=m_pool_scale, w_branch_a=m_w_branch_a, w_branch_b=m_w_branch_b,
                 w_out=m_w_out, final_norm_g=m_final_norm_g.reshape(1, D_MODEL))
    mom_v = dict(norm_g=v_norm_g, w_in=v_w_in, b_in=v_b_in, ssm_log_dt=v_ssm_log_dt, ssm_lam_re=v_ssm_lam_re,
                 ssm_lam_im=v_ssm_lam_im, ssm_b_re=v_ssm_b_re, ssm_b_im=v_ssm_b_im, ssm_c_re=v_ssm_c_re,
                 ssm_c_im=v_ssm_c_im, ssm_d=v_ssm_d, ssm_w_glu=v_ssm_w_glu, ssm_b_glu=v_ssm_b_glu,
                 pool_w=v_pool_w, pool_scale=v_pool_scale, w_branch_a=v_w_branch_a, w_branch_b=v_w_branch_b,
                 w_out=v_w_out, final_norm_g=v_final_norm_g.reshape(1, D_MODEL))
    order = ["norm_g", "w_in", "b_in", "ssm_log_dt", "ssm_lam_re", "ssm_lam_im", "ssm_b_re", "ssm_b_im",
             "ssm_c_re", "ssm_c_im", "ssm_d", "ssm_w_glu", "ssm_b_glu", "pool_w", "pool_scale", "w_branch_a",
             "w_branch_b", "w_out", "final_norm_g"]
    big_names = ["w_in", "ssm_w_glu", "w_branch_a", "w_branch_b", "w_out"]

    log_dt3 = ssm_log_dt.reshape(DEPTH, N_GROUP, 1)
    b_t = lambda a: a.transpose(0, 1, 3, 2)
    for d in (weights, mom_m, mom_v):
        d["ssm_b_re"], d["ssm_b_im"] = b_t(d["ssm_b_re"]), b_t(d["ssm_b_im"])
    bt_re, bt_im = weights["ssm_b_re"], weights["ssm_b_im"]
    abar_re, abar_im, bbt_re, bbt_im = _s5_params(log_dt3, ssm_lam_re, ssm_lam_im, bt_re, bt_im)
    s5_args = (bbt_re, bbt_im, ssm_c_re, ssm_c_im, abar_re, abar_im, ssm_d)

    w16 = {n: weights[n].astype(BF16) for n in big_names}
    rest = [w16[n] for n in big_names[1:]]
    wg_in = [None, None]
    wg_rest = [None, None]
    (wg_in[0],) = _run_carried("gather_w_in_l0", _gather_plan([w16["w_in"]], 0))
    xs = [x.reshape(SEQ, D_MODEL)]
    saved = []
    for l in range(DEPTH):
        proj, moved = _norm_proj(l, xs[l], norm_g, wg_in[l], b_in,
                                 carry=_gather_plan(rest, 0, by_columns=(1, 2)) if l == 0 else None)
        if l == 0:
            wg_rest[0] = moved
        (states, y0), moved = _s5_scan_fwd(
            l, proj, *s5_args, carry=_gather_plan([w16["w_in"]], 1) if l == 0 else None)
        if l == 0:
            (wg_in[1],) = moved
        pooled = _pool_fwd(l, proj)
        wg_glu, wg_a, wg_b, wg_out = wg_rest[l]
        x_next, moved = _mix_fwd(l, xs[l], proj, y0, pooled, wg_glu, ssm_b_glu, pool_w, pool_scale, wg_a, wg_b,
                                 wg_out, carry=_gather_plan(rest, 1, by_columns=(1, 2)) if l == 0 else None)
        if l == 0:
            wg_rest[1] = moved
        xs.append(x_next)
        saved.append((proj, states, y0, pooled))

    dx, loss_part, g_final = _loss_head(xs[DEPTH], loss_target.reshape(SEQ, D_MODEL), weights["final_norm_g"])
    loss = lax.psum(loss_part[0, 0], ("x", "y", "c"))

    core = lax.axis_index("c").astype(jnp.int32).reshape(1)
    vec_names = ["norm_g", "b_in", "ssm_d", "ssm_b_glu", "pool_scale", "ssm_log_dt"]
    s5_names = ["ssm_log_dt", "ssm_lam_re", "ssm_lam_im", "ssm_b_re", "ssm_b_im"]
    mat_names = ["pool_w", "ssm_c_re", "ssm_c_im", "ssm_b_re", "ssm_b_im"]
    lane_sparse = ("ssm_c_re", "ssm_c_im", "ssm_b_re", "ssm_b_im")

    def dense(key, a):
        return a.reshape(-1, 128) if key[0] in lane_sparse else a

    def undense(key, slots):
        return slots.reshape((N_CHIP, N_GROUP, GROUP_W, STATE)) if key[0] in lane_sparse else slots

    def add_small(tag, keys, own, got):
        out = [None] * len(keys)
        whole = [i for i, k in enumerate(keys) if k[0] not in mat_names]
        tiled = [i for i, k in enumerate(keys) if k[0] in mat_names]
        if whole:
            for i, r in zip(whole, _add_lists(f"{tag}_a", [own[i] for i in whole], [got[i] for i in whole])):
                out[i] = r
        if tiled:
            specs = [pl.BlockSpec((1, POOL_GROUP, POOL_GROUP), lambda j: (j, 0, 0)) if keys[i][0] == "pool_w"
                     else pl.BlockSpec((own[i].shape[0] // N_CHUNK, 128), lambda j: (j, 0)) for i in tiled]
            for i, r in zip(tiled, _add_lists(f"{tag}_b", [own[i] for i in tiled], [got[i] for i in tiled],
                                              grid=(N_CHUNK,), specs=specs)):
                out[i] = r
        return out

    sm = {("final_norm_g", None): g_final}
    slots = {}
    grads = dict.fromkeys(big_names)

    class Wave:
        def __init__(self, tag, layer, big, keys):
            self.tag, self.layer, self.big, self.keys = tag, layer, big, keys

        def to_sibling(self):
            self.own = [dense(k, sm[k]) for k in self.keys]
            return _sibling_plan([(grads[n], self.layer) for n in self.big], self.own)

        def add(self, moved):
            nb = len(self.big)
            self.chip_big = [_add_own(f"{self.tag}_{n}", core, grads[n], self.layer, b)
                             for n, b in zip(self.big, moved[:nb])]
            self.chip_small = add_small(self.tag, self.keys, self.own, moved[nb:])

        def to_chips(self, big=None, small=True):
            self.sent = list(self.big if big is None else big), small
            return _chips_plan([self.chip_big[self.big.index(n)] for n in self.sent[0]],
                               self.chip_small if small else [])

        def landed(self, moved):
            names, small = self.sent
            for n, s in zip(names, moved[:len(names)]):
                slots[(n, self.layer)] = s
            if small:
                for k, s in zip(self.keys, moved[len(names):]):
                    slots[k] = undense(k, s)
            return moved[len(names) + (len(self.keys) if small else 0):]

    def s5_param_grads(l, g_abar_re, g_abar_im, g_bbt_re, g_bbt_im):
        g = _s5_params_bwd(l, log_dt3, ssm_lam_re, ssm_lam_im, bt_re, bt_im, g_abar_re, g_abar_im, g_bbt_re, g_bbt_im)
        sm[("ssm_log_dt", l)] = g[0].reshape(1, N_GROUP)
        for n, a in zip(s5_names[1:], g[1:]):
            sm[(n, l)] = a

    small1 = ["b_in", "ssm_d", "ssm_b_glu", "pool_scale", "pool_w", "ssm_c_re", "ssm_c_im"] + s5_names
    w1 = Wave("chip1", 1, list(big_names), [(n, 1) for n in small1] + [("final_norm_g", None)])
    early = Wave("chip0e", 0, big_names[1:], [("pool_w", 0), ("pool_scale", 0), ("ssm_b_glu", 0)])
    mid = Wave("chip0m", 0, [], [(n, 0) for n in ["ssm_c_re", "ssm_c_im", "ssm_d"] + s5_names] + [("norm_g", 1)])
    late = Wave("chip0l", 0, ["w_in"], [("b_in", 0)])

    mix_prev, gw_in = None, None
    for l in reversed(range(DEPTH)):
        proj, states, y0, pooled = saved[l]
        wg_glu, wg_a, wg_b, wg_out = wg_rest[l]
        res, moved = _mix_bwd(l, dx, proj, y0, pooled, wg_glu, ssm_b_glu, pool_w, pool_scale, wg_a, wg_b, wg_out,
                              mix_prev, carry=None if l == 1 else w1.to_chips(big=["w_in"], small=False))
        if l == 0:
            w1.landed(moved)
        dproj, dy0, dpooled = res[:3]
        mix_prev = list(res[3:7])
        grads["w_out"], grads["w_branch_a"], grads["w_branch_b"], grads["ssm_w_glu"] = mix_prev
        sm[("pool_w", l)], sm[("pool_scale", l)], sm[("ssm_b_glu", l)] = res[7:]
        dproj = _pool_bwd(l, dpooled, dproj)
        carry = None if l == 1 else _join(w1.to_chips(big=big_names[1:]), early.to_sibling())
        res, moved = _s5_scan_bwd(l, dy0, proj, states, *s5_args, dproj, carry=carry)
        if l == 0:
            early.add(w1.landed(moved))
        dproj, g_bbt_re, g_bbt_im, sm[("ssm_c_re", l)], sm[("ssm_c_im", l)], g_abar_re, g_abar_im, sm[("ssm_d", l)] = res
        s5_param_grads(l, g_abar_re, g_abar_im, g_bbt_re, g_bbt_im)
        carry = None if l == 1 else _join(early.to_chips(), mid.to_sibling())
        (gw_in, sm[("b_in", l)]), moved = _proj_wgrad(l, xs[l], norm_g, dproj, gw_in, carry=carry)
        grads["w_in"] = gw_in
        if l == 0:
            mid.add(early.landed(moved))
        carry = w1.to_sibling() if l == 1 else _join(mid.to_chips(), late.to_sibling())
        (dx, sm[("norm_g", l)]), moved = _proj_dgrad(l, dx, xs[l], norm_g, dproj, wg_in[l], carry=carry)
        if l == 1:
            w1.add(moved)
        else:
            late.add(mid.landed(moved))
    grad_x = dx.reshape(1, SEQ, D_MODEL)
    moved = late.landed(_run_carried("exchange_last", _join(late.to_chips(), _all_plan([sm[("norm_g", 0)]]))))
    slots[("norm_g", 0)] = moved[0]

    res = {}
    for n in big_names:
        res[n] = _sum_slots_adamw(n, [slots[(n, l)] for l in range(DEPTH)], weights[n], mom_m[n], mom_v[n])
    per_layer = lambda n: [slots[(n, l)] for l in range(DEPTH)]
    names_a = vec_names + ["ssm_lam_re", "ssm_lam_im"]
    entries_a = [(per_layer(n), weights[n], mom_m[n], mom_v[n], None, None) for n in names_a]
    n = "final_norm_g"
    entries_a.append((slots[(n, None)], weights[n], mom_m[n], mom_v[n], None, None))
    out_a = _adamw_small("small_a", entries_a)
    for n, r in zip(names_a + ["final_norm_g"], out_a):
        res[n] = r
    res["final_norm_g"] = tuple(a.reshape(D_MODEL) for a in res["final_norm_g"])
    pw_s = pl.BlockSpec((N_CHIP, 1, POOL_GROUP, POOL_GROUP), lambda j: (0, j, 0, 0))
    pw_w = pl.BlockSpec((DEPTH, 1, POOL_GROUP, POOL_GROUP), lambda j: (0, j, 0, 0))
    c_s = pl.BlockSpec((N_CHIP, CH_G, GROUP_W, STATE), lambda j: (0, j, 0, 0))
    c_w = pl.BlockSpec((DEPTH, CH_G, GROUP_W, STATE), lambda j: (0, j, 0, 0))
    entries_b = [(per_layer(n), weights[n], mom_m[n], mom_v[n], pw_s if n == "pool_w" else c_s,
                  pw_w if n == "pool_w" else c_w) for n in mat_names]
    out_b = _adamw_small("small_b", entries_b, grid=(N_CHUNK,))
    for n, r in zip(mat_names, out_b):
        res[n] = tuple(b_t(a) for a in r) if n in ("ssm_b_re", "ssm_b_im") else r

    outs = [loss, grad_x]
    for i in range(4):
        outs += [res[n][i] for n in order]
    return tuple(outs)
```

```python
import math

import jax
import jax.numpy as jnp
from jax import lax
from jax.experimental import pallas as pl
from jax.experimental.pallas import tpu as pltpu

F32 = jnp.float32
BF16 = jnp.bfloat16

SEQ = 2048
D_MODEL = 1024
N_IN = 4096
WIDTH = 512
N_GROUP = 32
GROUP_W = 16
STATE = 64
N_STATE = N_GROUP * STATE
N_CHUNK = 4
CH_G = N_GROUP // N_CHUNK
CH_W = WIDTH // N_CHUNK
CH_S = N_STATE // N_CHUNK
N_DEV = 8
N_CHIP = 4
POOL_WINDOWS = (2, 4, 8, 16)
POOL_GROUP = 128
EPS = 1e-6
DEPTH = 2

ADAM_LR = 0.001
ADAM_B1 = 0.9
ADAM_B2 = 0.999
ADAM_EPS = 1e-08
ADAM_WD = 0.01
ADAM_STEP = 10

TILE_M = 256
ROW_BLK = 512
VMEM_LIMIT = 48 * 1024 * 1024
VMEM_LIMIT_BIG = 60 * 1024 * 1024
MESH = pl.DeviceIdType.MESH
ANY = pl.BlockSpec(memory_space=pl.ANY)

GELU_C = math.sqrt(2.0 / math.pi)
GELU_A = 0.044715

SDS = jax.ShapeDtypeStruct


def _cp(sem=None, limit=VMEM_LIMIT):
    return pltpu.CompilerParams(dimension_semantics=sem, vmem_limit_bytes=limit)


def _dot(a, b):
    return jnp.dot(a, b, preferred_element_type=F32)


def _dot_nt(a, b):
    return lax.dot_general(a, b, (((1,), (1,)), ((), ())), preferred_element_type=F32)


def _dot_tn(a, b):
    return lax.dot_general(a, b, (((0,), (0,)), ((), ())), preferred_element_type=F32)


def _sig(x):
    return jax.nn.sigmoid(x)


def _rms(x):
    rs = lax.rsqrt(jnp.mean(x * x, axis=-1, keepdims=True) + EPS)
    return rs, x * rs


def _slot(n):
    return 4 * (n % 2) + n // 2


def _const(shape):
    n = len(shape)
    return pl.BlockSpec(shape, lambda *_: (0,) * n)


def _sum4(p):
    return (p[0] + p[1]) + (p[2] + p[3])


def _sum_slots(s_ref):
    vals = [s_ref[k].astype(F32) for k in range(s_ref.shape[0])]
    while len(vals) > 1:
        vals = [vals[i] + vals[i + 1] for i in range(0, len(vals), 2)]
    return vals[0]


def _s5_param_fn(log_dt, lam_re, lam_im, bt_re, bt_im):
    dt = jnp.exp(log_dt)
    mag = jnp.exp(lam_re * dt)
    ang = lam_im * dt
    abar_re = mag * jnp.cos(ang)
    abar_im = mag * jnp.sin(ang)
    num_re = abar_re - 1.0
    num_im = abar_im
    den = lam_re * lam_re + lam_im * lam_im
    coef_re = (num_re * lam_re + num_im * lam_im) / den
    coef_im = (num_im * lam_re - num_re * lam_im) / den
    bbar_re = coef_re[..., None, :] * bt_re - coef_im[..., None, :] * bt_im
    bbar_im = coef_re[..., None, :] * bt_im + coef_im[..., None, :] * bt_re
    return abar_re, abar_im, bbar_re, bbar_im


def _s5_params(log_dt, lam_re, lam_im, bt_re, bt_im):
    def body(ld, lr, li, br, bi, o_ar, o_ai, o_br, o_bi):
        ar, ai, bbr, bbi = _s5_param_fn(ld[...], lr[...], li[...], br[...], bi[...])
        o_ar[...] = ar
        o_ai[...] = ai
        o_br[...] = bbr
        o_bi[...] = bbi

    return pl.pallas_call(
        body, name="s5_params",
        out_shape=(SDS(lam_re.shape, F32), SDS(lam_re.shape, F32), SDS(bt_re.shape, F32), SDS(bt_re.shape, F32)),
    )(log_dt, lam_re, lam_im, bt_re, bt_im)


def _s5_params_bwd(layer, log_dt, lam_re, lam_im, bt_re, bt_im, g_ar, g_ai, g_br, g_bi):
    def body(ld, lr, li, br, bi, car, cai, cbr, cbi, o_ld, o_lr, o_li, o_br, o_bi):
        _, vjp = jax.vjp(_s5_param_fn, ld[...], lr[...], li[...], br[...], bi[...])
        d_ld, d_lr, d_li, d_br, d_bi = vjp((car[...], cai[...], cbr[...], cbi[...]))
        o_ld[...] = d_ld
        o_lr[...] = d_lr
        o_li[...] = d_li
        o_br[...] = d_br
        o_bi[...] = d_bi

    one = lambda shape: pl.BlockSpec((None,) + shape, lambda i: (layer,) + (0,) * len(shape))
    whole = lambda shape: _const(shape)
    vec, lam, mat = (N_GROUP, 1), (N_GROUP, STATE), (N_GROUP, GROUP_W, STATE)
    return pl.pallas_call(
        body, name=f"s5_params_bwd_l{layer}", grid=(1,),
        in_specs=[one(vec), one(lam), one(lam), one(mat), one(mat), whole(lam), whole(lam), whole(mat), whole(mat)],
        out_specs=(whole(vec), whole(lam), whole(lam), whole(mat), whole(mat)),
        out_shape=(SDS(vec, F32), SDS(lam, F32), SDS(lam, F32), SDS(mat, F32), SDS(mat, F32)),
    )(log_dt, lam_re, lam_im, bt_re, bt_im, g_ar, g_ai, g_br, g_bi)


def _norm_proj(layer, x, norm_g, wg_in, b_in, carry=None):
    n_w = len(wg_in)

    def body(x_ref, g_ref, b_ref, *refs):
        w_refs, o_ref = refs[:n_w], refs[n_w]
        _, xn = _rms(x_ref[...])
        h = (xn * g_ref[layer:layer + 1, :]).astype(BF16)
        for k in range(N_DEV):
            cols = slice(k * WIDTH, (k + 1) * WIDTH)
            acc = b_ref[layer:layer + 1, cols]
            row = 0
            for w_ref in w_refs:
                rows = w_ref.shape[1]
                acc = acc + _dot(h[:, row:row + rows], w_ref[k])
                row += rows
            o_ref[:, cols] = acc

    (proj,), moved = _pcall(
        body, name=f"norm_proj_l{layer}",
        out_shape=[SDS((SEQ, N_IN), F32)],
        grid=(SEQ // TILE_M,),
        in_specs=[pl.BlockSpec((TILE_M, D_MODEL), lambda i: (i, 0)),
                  _const((DEPTH, D_MODEL)),
                  _const((DEPTH, N_IN))] + [_const(w.shape) for w in wg_in],
        out_specs=[pl.BlockSpec((TILE_M, N_IN), lambda i: (i, 0))],
        args=[x, norm_g, b_in, *wg_in], sem=("parallel",), carry=carry)
    return proj, moved


TIME_BLK = 512
N_TBLK = SEQ // TIME_BLK
N_PANEL = CH_S // 128
STATE_SHAPE = (N_PANEL, SEQ * 8, 128)


def _s5_layer_specs(layer):
    mat = lambda: pl.BlockSpec((None, N_GROUP, GROUP_W, STATE), lambda i: (layer, 0, 0, 0))
    ab = lambda: pl.BlockSpec((None, N_GROUP, STATE), lambda i: (layer, 0, 0))
    return [mat(), mat(), mat(), mat(), ab(), ab(), _const((DEPTH, WIDTH))]


def _s5_layer_scratch():
    return [pltpu.VMEM((N_CHUNK, CH_W, CH_S), BF16)] * 4 + [pltpu.VMEM((8, CH_S), F32)] * 2


def _s5_layer_fill(btre_ref, btim_ref, cre_ref, cim_ref, are_ref, aim_ref, bdre, bdim, ctre, ctim, a1, a2):
    for m in (bdre, bdim, ctre, ctim):
        m[...] = jnp.zeros_like(m)
    for grp in range(N_GROUP):
        k, g = divmod(grp, CH_G)
        rows = slice(g * GROUP_W, (g + 1) * GROUP_W)
        cols = slice(g * STATE, (g + 1) * STATE)
        bdre[k, rows, cols] = btre_ref[grp].astype(BF16)
        bdim[k, rows, cols] = btim_ref[grp].astype(BF16)
        ctre[k, rows, cols] = cre_ref[grp].astype(BF16)
        ctim[k, rows, cols] = cim_ref[grp].astype(BF16)
        ar = are_ref[grp:grp + 1, :]
        ai = aim_ref[grp:grp + 1, :]
        a1[k:k + 1, cols] = ar
        a1[N_CHUNK + k:N_CHUNK + k + 1, cols] = ar
        a2[k:k + 1, cols] = -ai
        a2[N_CHUNK + k:N_CHUNK + k + 1, cols] = ai


SCAN_UNROLL = 8


def _panels(tile):
    return [tile[:, p * 128:(p + 1) * 128] for p in range(N_PANEL)]


def _rows_load(ref, row):
    return jnp.concatenate([ref[p, pl.ds(row, TIME_BLK, stride=8), :] for p in range(N_PANEL)], axis=1)


def _rows_store(ref, row, val):
    for p in range(N_PANEL):
        ref[p, pl.ds(row, TIME_BLK, stride=8), :] = val[:, p * 128:(p + 1) * 128]


def _s5_scan_fwd(layer, proj, bbt_re, bbt_im, c_re, c_im, abar_re, abar_im, d_skip, carry=None):
    def body(u_ref, btre_ref, btim_ref, cre_ref, cim_ref, are_ref, aim_ref, d_ref, s_ref, y_ref,
             bdre, bdim, ctre, ctim, a1, a2, state):
        @pl.when(pl.program_id(0) == 0)
        def _():
            _s5_layer_fill(btre_ref, btim_ref, cre_ref, cim_ref, are_ref, aim_ref, bdre, bdim, ctre, ctim, a1, a2)
            state[...] = jnp.zeros_like(state)

        for k in range(N_CHUNK):
            ub = u_ref[:, k * CH_W:(k + 1) * CH_W].astype(BF16)
            _rows_store(s_ref, k, _dot(ub, bdre[k]))
            _rows_store(s_ref, N_CHUNK + k, _dot(ub, bdim[k]))
        m1 = _panels(a1[...])
        m2 = _panels(a2[...])

        def steps(n, tile):
            for r in range(SCAN_UNROLL):
                rows = pl.ds(pl.multiple_of((n * SCAN_UNROLL + r) * 8, 8), 8)
                tile = [m1[p] * tile[p] + m2[p] * pltpu.roll(tile[p], N_CHUNK, 0) + s_ref[p, rows, :]
                        for p in range(N_PANEL)]
                for p in range(N_PANEL):
                    s_ref[p, rows, :] = tile[p]
            return tile

        tile = lax.fori_loop(0, TIME_BLK // SCAN_UNROLL, steps, _panels(state[...]))
        state[...] = jnp.concatenate(tile, axis=1)
        d = d_ref[layer:layer + 1, :]
        for k in range(N_CHUNK):
            cols = slice(k * CH_W, (k + 1) * CH_W)
            y = (_dot_nt(_rows_load(s_ref, k).astype(BF16), ctre[k])
                 - _dot_nt(_rows_load(s_ref, N_CHUNK + k).astype(BF16), ctim[k]))
            y_ref[:, cols] = y + d[:, cols] * u_ref[:, cols]

    return _pcall(
        body, name=f"s5_fwd_l{layer}",
        out_shape=(SDS(STATE_SHAPE, F32), SDS((SEQ, WIDTH), F32)),
        grid=(N_TBLK,),
        in_specs=[pl.BlockSpec((TIME_BLK, WIDTH), lambda i: (i, 0))] + _s5_layer_specs(layer),
        out_specs=(pl.BlockSpec((N_PANEL, TIME_BLK * 8, 128), lambda i: (0, i, 0)),
                   pl.BlockSpec((TIME_BLK, WIDTH), lambda i: (i, 0))),
        scratch_shapes=_s5_layer_scratch() + [pltpu.VMEM((8, CH_S), F32)],
        args=[proj, bbt_re, bbt_im, c_re, c_im, abar_re, abar_im, d_skip], sem=("arbitrary",), carry=carry)


def _s5_scan_bwd(layer, dy0, proj, states, bbt_re, bbt_im, c_re, c_im, abar_re, abar_im, d_skip, dproj,
                 carry=None):
    def body(dy_ref, u_ref, s_ref, sprev_ref, btre_ref, btim_ref, cre_ref, cim_ref, are_ref, aim_ref, d_ref, _,
             du_ref, gbre_ref, gbim_ref, gcre_ref, gcim_ref, gare_ref, gaim_ref, gd_ref,
             lam_ref, bdre, bdim, ctre, ctim, a1, a2, state, acc1, acc2, gbre, gbim, gcre, gcim, gd):
        step_id = pl.program_id(0)

        @pl.when(step_id == 0)
        def _():
            _s5_layer_fill(btre_ref, btim_ref, cre_ref, cim_ref, are_ref, aim_ref, bdre, bdim, ctre, ctim, a1, a2)
            for r in (state, acc1, acc2, gbre, gbim, gcre, gcim, gd):
                r[...] = jnp.zeros_like(r)

        for k in range(N_CHUNK):
            dyb = dy_ref[:, k * CH_W:(k + 1) * CH_W].astype(BF16)
            _rows_store(lam_ref, k, _dot(dyb, ctre[k]))
            _rows_store(lam_ref, N_CHUNK + k, -_dot(dyb, ctim[k]))
            gcre[k] += _dot_tn(dyb, _rows_load(s_ref, k).astype(BF16))
            gcim[k] -= _dot_tn(dyb, _rows_load(s_ref, N_CHUNK + k).astype(BF16))

        m1 = _panels(a1[...])
        m2 = _panels(-a2[...])
        has_before = (step_id < N_TBLK - 1).astype(F32)

        def one(t8, c, first_token):
            tile, swapped, p1, p2 = c
            rows = pl.ds(t8, 8)
            tile = [m1[p] * tile[p] + m2[p] * swapped[p] + lam_ref[p, rows, :] for p in range(N_PANEL)]
            swapped = [pltpu.roll(tile[p], N_CHUNK, 0) for p in range(N_PANEL)]
            for p in range(N_PANEL):
                lam_ref[p, rows, :] = tile[p]
            if first_token:
                before = [sprev_ref[p] * has_before for p in range(N_PANEL)]
            else:
                before = [s_ref[p, pl.ds(t8 - 8, 8), :] for p in range(N_PANEL)]
            p1 = [p1[p] + tile[p] * before[p] for p in range(N_PANEL)]
            p2 = [p2[p] + swapped[p] * before[p] for p in range(N_PANEL)]
            return tile, swapped, p1, p2

        def steps(n, c):
            for r in range(SCAN_UNROLL):
                t8 = pl.multiple_of((TIME_BLK - 1 - (n * SCAN_UNROLL + r)) * 8, 8)
                c = one(t8, c, False)
            return c

        tile0 = _panels(state[...])
        c = (tile0, [pltpu.roll(t, N_CHUNK, 0) for t in tile0], _panels(acc1[...]), _panels(acc2[...]))
        c = lax.fori_loop(0, TIME_BLK // SCAN_UNROLL - 1, steps, c)
        for r in range(SCAN_UNROLL - 1, -1, -1):
            c = one(r * 8, c, r == 0)
        state[...] = jnp.concatenate(c[0], axis=1)
        acc1[...] = jnp.concatenate(c[2], axis=1)
        acc2[...] = jnp.concatenate(c[3], axis=1)

        d = d_ref[layer:layer + 1, :]
        for k in range(N_CHUNK):
            cols = slice(k * CH_W, (k + 1) * CH_W)
            lrb = _rows_load(lam_ref, k).astype(BF16)
            lib = _rows_load(lam_ref, N_CHUNK + k).astype(BF16)
            u = u_ref[:, cols]
            ub = u.astype(BF16)
            dy = dy_ref[:, cols]
            du = dy * d[:, cols] + _dot_nt(lrb, bdre[k]) + _dot_nt(lib, bdim[k])
            du_ref[:, cols] = du.astype(BF16)
            gbre[k] += _dot_tn(ub, lrb)
            gbim[k] += _dot_tn(ub, lib)
        gd[...] += jnp.sum(dy_ref[...] * u_ref[...], axis=0, keepdims=True)

        @pl.when(step_id == N_TBLK - 1)
        def _():
            gd_ref[...] = gd[...]
            ga_re = acc1[0:N_CHUNK, :] + acc1[N_CHUNK:, :]
            ga_im = acc2[0:N_CHUNK, :] - acc2[N_CHUNK:, :]
            for grp in range(N_GROUP):
                k, g = divmod(grp, CH_G)
                rows = slice(g * GROUP_W, (g + 1) * GROUP_W)
                cols = slice(g * STATE, (g + 1) * STATE)
                gcre_ref[grp] = gcre[k, rows, cols]
                gcim_ref[grp] = gcim[k, rows, cols]
                gbre_ref[grp] = gbre[k, rows, cols]
                gbim_ref[grp] = gbim[k, rows, cols]
                gare_ref[grp:grp + 1, :] = ga_re[k:k + 1, cols]
                gaim_ref[grp:grp + 1, :] = ga_im[k:k + 1, cols]

    back = lambda i: N_TBLK - 1 - i
    tok = lambda: pl.BlockSpec((TIME_BLK, WIDTH), lambda i: (back(i), 0))
    mat = lambda: _const((N_GROUP, GROUP_W, STATE))
    acc_mat = pltpu.VMEM((N_CHUNK, CH_W, CH_S), F32)
    return _pcall(
        body, name=f"s5_bwd_l{layer}",
        out_shape=(SDS((SEQ, N_IN), BF16), SDS((N_GROUP, GROUP_W, STATE), F32), SDS((N_GROUP, GROUP_W, STATE), F32),
                   SDS((N_GROUP, GROUP_W, STATE), F32), SDS((N_GROUP, GROUP_W, STATE), F32),
                   SDS((N_GROUP, STATE), F32), SDS((N_GROUP, STATE), F32), SDS((1, WIDTH), F32)),
        grid=(N_TBLK,),
        in_specs=[tok(), tok(),
                  pl.BlockSpec((N_PANEL, TIME_BLK * 8, 128), lambda i: (0, back(i), 0)),
                  pl.BlockSpec((N_PANEL, 8, 128), lambda i: (0, jnp.maximum(back(i) * TIME_BLK - 1, 0), 0))]
        + _s5_layer_specs(layer) + [ANY],
        out_specs=(tok(), mat(), mat(), mat(), mat(), _const((N_GROUP, STATE)), _const((N_GROUP, STATE)),
                   _const((1, WIDTH))),
        scratch_shapes=[pltpu.VMEM((N_PANEL, TIME_BLK * 8, 128), F32)] + _s5_layer_scratch()
        + [pltpu.VMEM((8, CH_S), F32)] * 3 + [acc_mat] * 4 + [pltpu.VMEM((1, WIDTH), F32)],
        args=[dy0, proj, states, states, bbt_re, bbt_im, c_re, c_im, abar_re, abar_im, d_skip, dproj],
        aliases={11: 0}, sem=("arbitrary",), limit=VMEM_LIMIT_BIG, carry=carry)


def _pool_counts(win):
    t = lax.broadcasted_iota(jnp.int32, (SEQ, POOL_GROUP), 0)
    return t, jnp.minimum(t + 1, win).astype(F32)


def _pool_fwd(layer, proj):
    def body(u_ref, o_ref):
        for gi, win in enumerate(POOL_WINDOWS):
            cols = slice(gi * POOL_GROUP, (gi + 1) * POOL_GROUP)
            u = u_ref[:, cols]
            t, count = _pool_counts(win)
            acc = u
            k = 1
            while k < win:
                acc = acc + jnp.where(t >= k, pltpu.roll(acc, k, 0), 0.0)
                k *= 2
            o_ref[:, cols] = acc / count - u

    return pl.pallas_call(
        body, name=f"pool_fwd_l{layer}",
        out_shape=SDS((SEQ, WIDTH), F32),
        grid=(1,),
        in_specs=[pl.BlockSpec((SEQ, WIDTH), lambda i: (0, 2))],
        out_specs=pl.BlockSpec((SEQ, WIDTH), lambda i: (0, 0)),
        compiler_params=_cp(("arbitrary",)),
    )(proj)


def _gelu_parts(y0):
    t = jnp.tanh(GELU_C * (y0 + GELU_A * (y0 * y0 * y0)))
    return t, 0.5 * y0 * (1.0 + t)


def _mix_forward(layer, p_ref, y0_ref, pooled_ref, wglu_ref, bglu_ref, pw_ref, scale_ref, wa_ref, wb_ref):
    za = p_ref[:, WIDTH:2 * WIDTH]
    zb = p_ref[:, 3 * WIDTH:4 * WIDTH]
    ga = p_ref[:, 4 * WIDTH:4 * WIDTH + D_MODEL]
    gb = p_ref[:, 4 * WIDTH + D_MODEL:]
    y0 = y0_ref[...]
    t, y1 = _gelu_parts(y0)
    y1b = y1.astype(BF16)
    q = _dot(y1b, wglu_ref[...].reshape(WIDTH, WIDTH)) + bglu_ref[layer:layer + 1, :]
    sq = _sig(q)
    y2 = y1 * sq
    sza = _sig(za)
    silu_za = za * sza
    ya = y2 * silu_za
    pooled = pooled_ref[...]
    mixed = jnp.concatenate(
        [_dot(pooled[:, g * POOL_GROUP:(g + 1) * POOL_GROUP].astype(BF16), pw_ref[g].astype(BF16))
         for g in range(len(POOL_WINDOWS))], axis=1)
    szb = _sig(zb)
    silu_zb = zb * szb
    scale = scale_ref[layer:layer + 1, :]
    ms = mixed * scale
    yb = ms * silu_zb
    yab = ya.astype(BF16)
    ybb = yb.astype(BF16)
    ma = _dot(yab, wa_ref[...])
    mb = _dot(ybb, wb_ref[...])
    sga = _sig(ga)
    sgb = _sig(gb)
    merged = sga * ma + sgb * mb
    return dict(za=za, zb=zb, y0=y0, t=t, y1=y1, y1b=y1b, sq=sq, y2=y2, sza=sza, silu_za=silu_za,
                pooled=pooled, mixed=mixed, szb=szb, silu_zb=silu_zb, scale=scale, ms=ms, yab=yab, ybb=ybb,
                ma=ma, mb=mb, sga=sga, sgb=sgb, merged=merged)


def _mix_weight_specs(layer):
    return [_const((N_DEV, WIDTH // N_DEV, WIDTH)),
            _const((DEPTH, WIDTH)),
            pl.BlockSpec((None, 4, POOL_GROUP, POOL_GROUP), lambda i: (layer, 0, 0, 0)),
            _const((DEPTH, WIDTH)),
            _const((WIDTH, D_MODEL)),
            _const((WIDTH, D_MODEL)),
            _const((N_DEV, D_MODEL // N_DEV, D_MODEL))]


def _mix_fwd(layer, x, proj, y0, pooled, wg_glu, b_glu, pool_w, pool_scale, wg_a, wg_b, wg_out, carry=None):
    def body(x_ref, p_ref, y0_ref, pooled_ref, wglu_ref, bglu_ref, pw_ref, scale_ref, wa_ref, wb_ref,
             wout_ref, o_ref):
        f = _mix_forward(layer, p_ref, y0_ref, pooled_ref, wglu_ref, bglu_ref, pw_ref, scale_ref, wa_ref, wb_ref)
        wout = wout_ref[...].reshape(D_MODEL, D_MODEL)
        o_ref[...] = x_ref[...] + _dot(f["merged"].astype(BF16), wout)

    (x_next,), moved = _pcall(
        body, name=f"mix_fwd_l{layer}",
        out_shape=[SDS((SEQ, D_MODEL), F32)],
        grid=(SEQ // TILE_M,),
        in_specs=[pl.BlockSpec((TILE_M, D_MODEL), lambda i: (i, 0)),
                  pl.BlockSpec((TILE_M, N_IN), lambda i: (i, 0)),
                  pl.BlockSpec((TILE_M, WIDTH), lambda i: (i, 0)),
                  pl.BlockSpec((TILE_M, WIDTH), lambda i: (i, 0))] + _mix_weight_specs(layer),
        out_specs=[pl.BlockSpec((TILE_M, D_MODEL), lambda i: (i, 0))],
        args=[x, proj, y0, pooled, wg_glu, b_glu, pool_w, pool_scale, wg_a, wg_b, wg_out],
        sem=("parallel",), carry=carry)
    return x_next, moved


def _loss_head(x, target, final_g):
    def body(x_ref, t_ref, g_ref, dx_ref, loss_ref, gg_ref):
        @pl.when(pl.program_id(0) == 0)
        def _():
            loss_ref[...] = jnp.zeros_like(loss_ref)
            gg_ref[...] = jnp.zeros_like(gg_ref)

        g = g_ref[...]
        rs, xn = _rms(x_ref[...])
        err = xn * g - t_ref[...]
        loss_ref[...] += 0.5 * jnp.sum(jnp.mean(err * err, axis=-1, keepdims=True), axis=0, keepdims=True)
        dy = err * (1.0 / D_MODEL)
        gg_ref[...] += jnp.sum(dy * xn, axis=0, keepdims=True)
        dxn = dy * g
        dx_ref[...] = rs * (dxn - xn * jnp.mean(dxn * xn, axis=-1, keepdims=True))

    return pl.pallas_call(
        body, name="loss_head",
        out_shape=(SDS((SEQ, D_MODEL), F32), SDS((1, 1), F32), SDS((1, D_MODEL), F32)),
        grid=(SEQ // TILE_M,),
        in_specs=[pl.BlockSpec((TILE_M, D_MODEL), lambda i: (i, 0)),
                  pl.BlockSpec((TILE_M, D_MODEL), lambda i: (i, 0)),
                  _const((1, D_MODEL))],
        out_specs=(pl.BlockSpec((TILE_M, D_MODEL), lambda i: (i, 0)), _const((1, 1)), _const((1, D_MODEL))),
        compiler_params=_cp(("arbitrary",)),
    )(x, target, final_g)


def _big_shapes():
    return dict(w_out=(DEPTH, N_DEV, D_MODEL // N_DEV, D_MODEL), w_branch_a=(DEPTH, N_DEV, WIDTH, D_MODEL // N_DEV),
                w_branch_b=(DEPTH, N_DEV, WIDTH, D_MODEL // N_DEV), ssm_w_glu=(DEPTH, N_DEV, WIDTH // N_DEV, WIDTH),
                w_in=(DEPTH, N_DEV, D_MODEL, WIDTH))


def _mix_bwd(layer, dx_next, proj, y0, pooled, wg_glu, b_glu, pool_w, pool_scale, wg_a, wg_b, wg_out, prev,
             carry=None):
    n_k = N_DEV
    n_prev = 0 if prev is None else len(prev)

    def body(*refs):
        (dx_ref, p_ref, y0_ref, pooled_ref, wglu_ref, bglu_ref, pw_ref, scale_ref, wa_ref, wb_ref,
         wout_ref) = refs[:11]
        (dproj_ref, dy0_ref, dpooled_ref, gwout_ref, gwa_ref, gwb_ref, gwglu_ref, gpw_ref,
         gscale_ref, gbglu_ref) = refs[11 + n_prev:]

        @pl.when(pl.program_id(0) == 0)
        def _():
            for r in (gwout_ref, gwa_ref, gwb_ref, gwglu_ref, gpw_ref, gscale_ref, gbglu_ref):
                r[...] = jnp.zeros_like(r)

        f = _mix_forward(layer, p_ref, y0_ref, pooled_ref, wglu_ref, bglu_ref, pw_ref, scale_ref, wa_ref, wb_ref)
        wglu = wglu_ref[...].reshape(WIDTH, WIDTH)
        wout = wout_ref[...].reshape(D_MODEL, D_MODEL)
        blk = D_MODEL // n_k
        dxb = dx_ref[...].astype(BF16)
        dmerged = _dot_nt(dxb, wout)
        gwout = _dot_tn(f["merged"].astype(BF16), dxb)
        for k in range(n_k):
            gwout_ref[_slot(k)] += gwout[k * blk:(k + 1) * blk, :]
        dma = dmerged * f["sga"]
        dmb = dmerged * f["sgb"]
        dga = dmerged * f["ma"] * f["sga"] * (1.0 - f["sga"])
        dgb = dmerged * f["mb"] * f["sgb"] * (1.0 - f["sgb"])
        dmab = dma.astype(BF16)
        dmbb = dmb.astype(BF16)
        dya = _dot_nt(dmab, wa_ref[...])
        dyb = _dot_nt(dmbb, wb_ref[...])
        gwa = _dot_tn(f["yab"], dmab)
        gwb = _dot_tn(f["ybb"], dmbb)
        for k in range(n_k):
            gwa_ref[_slot(k)] += gwa[:, k * blk:(k + 1) * blk]
            gwb_ref[_slot(k)] += gwb[:, k * blk:(k + 1) * blk]
        zb, szb = f["zb"], f["szb"]
        dzb = dyb * f["ms"] * (szb * (1.0 + zb * (1.0 - szb)))
        dms = dyb * f["silu_zb"]
        gscale_ref[...] += jnp.sum(dms * f["mixed"], axis=0, keepdims=True)
        dmixed = (dms * f["scale"]).astype(BF16)
        pooled = f["pooled"]
        for g in range(len(POOL_WINDOWS)):
            cols = slice(g * POOL_GROUP, (g + 1) * POOL_GROUP)
            dpooled_ref[:, cols] = _dot_nt(dmixed[:, cols], pw_ref[g].astype(BF16))
            gpw_ref[g] += _dot_tn(pooled[:, cols].astype(BF16), dmixed[:, cols])
        za, sza = f["za"], f["sza"]
        dza = dya * f["y2"] * (sza * (1.0 + za * (1.0 - sza)))
        dy2 = dya * f["silu_za"]
        sq = f["sq"]
        dq = dy2 * f["y1"] * sq * (1.0 - sq)
        dqb = dq.astype(BF16)
        dy1 = dy2 * sq + _dot_nt(dqb, wglu)
        gwglu = _dot_tn(f["y1b"], dqb)
        rblk = WIDTH // n_k
        for k in range(n_k):
            gwglu_ref[_slot(k)] += gwglu[k * rblk:(k + 1) * rblk, :]
        gbglu_ref[...] += jnp.sum(dq, axis=0, keepdims=True)
        y0, t = f["y0"], f["t"]
        dgelu = 0.5 * (1.0 + t) + 0.5 * y0 * (1.0 - t * t) * (GELU_C * (1.0 + 3.0 * GELU_A * y0 * y0))
        dy0_ref[...] = dy1 * dgelu
        zeros = jnp.zeros((TILE_M, WIDTH), BF16)
        dproj_ref[:, 0:WIDTH] = zeros
        dproj_ref[:, WIDTH:2 * WIDTH] = dza.astype(BF16)
        dproj_ref[:, 2 * WIDTH:3 * WIDTH] = zeros
        dproj_ref[:, 3 * WIDTH:4 * WIDTH] = dzb.astype(BF16)
        dproj_ref[:, 4 * WIDTH:4 * WIDTH + D_MODEL] = dga.astype(BF16)
        dproj_ref[:, 4 * WIDTH + D_MODEL:] = dgb.astype(BF16)

    tile = lambda w: pl.BlockSpec((TILE_M, w), lambda i: (i, 0))
    shapes = _big_shapes()
    big = ["w_out", "w_branch_a", "w_branch_b", "ssm_w_glu"]
    slab = lambda n: pl.BlockSpec((None,) + shapes[n][1:], lambda i: (layer, 0, 0, 0))
    args = [dx_next, proj, y0, pooled, wg_glu, b_glu, pool_w, pool_scale, wg_a, wg_b, wg_out]
    return _pcall(
        body, name=f"mix_bwd_l{layer}",
        out_shape=(SDS((SEQ, N_IN), BF16), SDS((SEQ, WIDTH), F32), SDS((SEQ, WIDTH), F32))
        + tuple(SDS(shapes[n], F32) for n in big)
        + (SDS((4, POOL_GROUP, POOL_GROUP), F32), SDS((1, WIDTH), F32), SDS((1, WIDTH), F32)),
        grid=(SEQ // TILE_M,),
        in_specs=[tile(D_MODEL), tile(N_IN), tile(WIDTH), tile(WIDTH)] + _mix_weight_specs(layer) + [ANY] * n_prev,
        out_specs=(tile(N_IN), tile(WIDTH), tile(WIDTH)) + tuple(slab(n) for n in big)
        + (_const((4, POOL_GROUP, POOL_GROUP)), _const((1, WIDTH)), _const((1, WIDTH))),
        args=args + list(prev or ()),
        aliases={len(args) + i: 3 + i for i in range(n_prev)},
        sem=("arbitrary",), limit=VMEM_LIMIT_BIG, carry=carry)


def _pool_bwd(layer, dpooled, dproj):
    def body(dp_ref, _, o_ref):
        for gi, win in enumerate(POOL_WINDOWS):
            cols = slice(gi * POOL_GROUP, (gi + 1) * POOL_GROUP)
            dp = dp_ref[:, cols]
            t, count = _pool_counts(win)
            e = dp / count
            acc = e
            k = 1
            while k < win:
                acc = acc + jnp.where(t < SEQ - k, pltpu.roll(acc, SEQ - k, 0), 0.0)
                k *= 2
            o_ref[:, cols] = (acc - dp).astype(BF16)

    return pl.pallas_call(
        body, name=f"pool_bwd_l{layer}",
        out_shape=SDS((SEQ, N_IN), BF16),
        grid=(1,),
        in_specs=[pl.BlockSpec((SEQ, WIDTH), lambda i: (0, 0)), ANY],
        out_specs=pl.BlockSpec((SEQ, WIDTH), lambda i: (0, 2)),
        input_output_aliases={1: 0},
        compiler_params=_cp(("arbitrary",)),
    )(dpooled, dproj)


def _proj_wgrad(layer, x, norm_g, dproj, prev, carry=None):
    tm = 512
    n_prev = 0 if prev is None else 1

    def body(*refs):
        x_ref, g_ref, dp_ref = refs[:3]
        gw_ref, gb_ref, ht_ref = refs[3 + n_prev:]
        n, t = pl.program_id(0), pl.program_id(1)

        @pl.when(t == 0)
        def _():
            gw_ref[...] = jnp.zeros_like(gw_ref)
            gb_ref[...] = jnp.zeros_like(gb_ref)

        @pl.when(n == 0)
        def _():
            _, xn = _rms(x_ref[...])
            ht_ref[t] = (xn * g_ref[layer:layer + 1, :]).T.astype(BF16)

        dp = dp_ref[...]
        gw_ref[...] += _dot(ht_ref[t], dp)
        gb_ref[...] += jnp.sum(dp.astype(F32), axis=0, keepdims=True)

    return _pcall(
        body, name=f"proj_wgrad_l{layer}",
        out_shape=(SDS(_big_shapes()["w_in"], F32), SDS((1, N_IN), F32)),
        grid=(N_DEV, SEQ // tm),
        in_specs=[pl.BlockSpec((tm, D_MODEL), lambda n, t: (jnp.where(n == 0, t, 0), 0)),
                  _const((DEPTH, D_MODEL)),
                  pl.BlockSpec((tm, WIDTH), lambda n, t: (t, n))] + [ANY] * n_prev,
        out_specs=(pl.BlockSpec((None, None, D_MODEL, WIDTH), lambda n, t: (layer, _slot(n), 0, 0)),
                   pl.BlockSpec((1, WIDTH), lambda n, t: (0, n))),
        scratch_shapes=[pltpu.VMEM((SEQ // tm, D_MODEL, tm), BF16)],
        args=[x, norm_g, dproj] + ([prev] if n_prev else []),
        aliases={3: 0} if n_prev else {}, sem=("arbitrary", "arbitrary"), carry=carry)


def _proj_dgrad(layer, dx_next, x, norm_g, dproj, wg_in, carry=None):
    n_w = len(wg_in)

    def body(dxn_ref, x_ref, g_ref, dp_ref, *refs):
        w_refs, (dx_ref, gg_ref) = refs[:n_w], refs[n_w:]

        @pl.when(pl.program_id(0) == 0)
        def _():
            gg_ref[...] = jnp.zeros_like(gg_ref)

        parts = []
        for w_ref in w_refs:
            part = jnp.zeros((TILE_M, w_ref.shape[1]), F32)
            for k in range(N_DEV):
                part = part + _dot_nt(dp_ref[:, k * WIDTH:(k + 1) * WIDTH], w_ref[k])
            parts.append(part)
        dh = parts[0] if n_w == 1 else jnp.concatenate(parts, axis=1)
        rs, xn = _rms(x_ref[...])
        gg_ref[...] += jnp.sum(dh * xn, axis=0, keepdims=True)
        dxn = dh * g_ref[layer:layer + 1, :]
        dx_ref[...] = dxn_ref[...] + rs * (dxn - xn * jnp.mean(dxn * xn, axis=-1, keepdims=True))

    return _pcall(
        body, name=f"proj_dgrad_l{layer}",
        out_shape=(SDS((SEQ, D_MODEL), F32), SDS((1, D_MODEL), F32)),
        grid=(SEQ // TILE_M,),
        in_specs=[pl.BlockSpec((TILE_M, D_MODEL), lambda i: (i, 0)),
                  pl.BlockSpec((TILE_M, D_MODEL), lambda i: (i, 0)),
                  _const((DEPTH, D_MODEL)),
                  pl.BlockSpec((TILE_M, N_IN), lambda i: (i, 0))] + [_const(w.shape) for w in wg_in],
        out_specs=(pl.BlockSpec((TILE_M, D_MODEL), lambda i: (i, 0)), _const((1, D_MODEL))),
        args=[dx_next, x, norm_g, dproj, *wg_in], sem=("arbitrary",), carry=carry)


def _my_place():
    return lax.axis_index("x"), lax.axis_index("y"), lax.axis_index("c")


def _gather_plan(shards, layer, by_columns=(), rows_of=None):
    n = len(shards)

    def parts(ins, outs, sems):
        send_sems, recv_sems, local_sems = sems
        x, y, c = _my_place()
        chips = [(1 - x, y), (x, 1 - y), (1 - x, 1 - y)]

        def source(t):
            return ins[t].at[layer] if rows_of is None else ins[t].at[layer, pl.ds(*rows_of)]

        def rows(t, place):
            px, py, pc = place
            index = 4 * px + 2 * py + pc
            if t in by_columns:
                width = shards[t].shape[2]
                return outs[t].at[:, pl.ds(pl.multiple_of(index * width, 128), width)]
            return outs[t].at[index]

        def copy(t, k, block, to, from_src=False):
            return pltpu.make_async_remote_copy(
                src_ref=source(t) if from_src else rows(t, block), dst_ref=rows(t, block),
                send_sem=send_sems.at[7 * t + k], recv_sem=recv_sems.at[7 * t + k], device_id=to,
                device_id_type=MESH)

        def mine(t):
            return pltpu.make_async_copy(source(t), rows(t, (x, y, c)), local_sems.at[t])

        return (x, y, c), chips, copy, mine

    def start(ins, outs, sems):
        me, chips, copy, mine = parts(ins, outs, sems)
        x, y, c = me
        for t in range(n):
            mine(t).start()
            copy(t, 0, me, (x, y, 1 - c), from_src=True).start()
            for j, chip in enumerate(chips):
                copy(t, 1 + j, me, (*chip, c), from_src=True).start()

    def relay(ins, outs, sems):
        me, chips, copy, mine = parts(ins, outs, sems)
        x, y, c = me
        for t in range(n):
            for j, chip in enumerate(chips):
                copy(t, 1 + j, (*chip, c), me).wait_recv()
                copy(t, 4 + j, (*chip, c), (x, y, 1 - c)).start()

    def finish(ins, outs, sems):
        me, chips, copy, mine = parts(ins, outs, sems)
        x, y, c = me
        sibling = (x, y, 1 - c)
        for t in range(n):
            copy(t, 0, sibling, me).wait_recv()
            for j, chip in enumerate(chips):
                copy(t, 4 + j, (*chip, 1 - c), me).wait_recv()
            for k in range(7):
                copy(t, k, me, sibling, from_src=k < 4).wait_send()
            mine(t).wait()

    n_rows = lambda a: a.shape[1] if rows_of is None else rows_of[1]
    out_shape = [SDS((a.shape[1], N_DEV * a.shape[2]) if t in by_columns else (N_DEV, n_rows(a), a.shape[2]), a.dtype)
                 for t, a in enumerate(shards)]
    sems = [pltpu.SemaphoreType.DMA((7 * n,)), pltpu.SemaphoreType.DMA((7 * n,)), pltpu.SemaphoreType.DMA((n,))]
    return _Carried(shards, out_shape, sems, start, finish, relay)


class _Carried:
    def __init__(self, ins, out_shape, sems, start, finish, relay=None):
        self.ins, self.out_shape, self.sems = list(ins), list(out_shape), list(sems)
        self.start, self.finish = start, finish
        self.relay = relay or (lambda ins, outs, sems: None)


def _pcall(body, *, name, grid, in_specs, out_specs, out_shape, args, scratch_shapes=(), aliases=None,
           sem=None, limit=VMEM_LIMIT, carry=None):
    out_shape, out_specs, scratch_shapes = list(out_shape), list(out_specs), list(scratch_shapes)
    n_in, n_out, n_scr = len(args), len(out_shape), len(scratch_shapes)
    if carry is None:
        kern, c_ins, c_out, c_sems = body, [], [], []
    else:
        c_ins, c_out, c_sems = carry.ins, carry.out_shape, carry.sems
        ci, co = len(c_ins), len(c_out)
        steps = tuple(grid)

        def kern(*refs):
            o0 = n_in + ci
            s0 = o0 + n_out + co
            mine = refs[:n_in] + refs[o0:o0 + n_out] + refs[s0:s0 + n_scr]
            theirs = (refs[n_in:o0], refs[o0 + n_out:s0], refs[s0 + n_scr:])
            first = pl.program_id(0) == 0
            last = pl.program_id(0) == steps[0] - 1
            for a in range(1, len(steps)):
                first = jnp.logical_and(first, pl.program_id(a) == 0)
                last = jnp.logical_and(last, pl.program_id(a) == steps[a] - 1)

            @pl.when(first)
            def _():
                carry.start(*theirs)

            @pl.when(last)
            def _():
                carry.relay(*theirs)

            body(*mine)

            @pl.when(last)
            def _():
                carry.finish(*theirs)

        sem = ("arbitrary",) * len(steps)
    res = pl.pallas_call(
        kern, name=name, grid=tuple(grid),
        in_specs=list(in_specs) + [ANY] * len(c_ins),
        out_specs=tuple(out_specs + [ANY] * len(c_out)),
        out_shape=tuple(out_shape + c_out),
        scratch_shapes=scratch_shapes + c_sems,
        input_output_aliases=aliases or {},
        compiler_params=_cp(sem, limit),
    )(*args, *c_ins)
    return res[:n_out], res[n_out:]


def _run_carried(name, carry):
    ci, co = len(carry.ins), len(carry.out_shape)

    def body(*refs):
        parts = (refs[:ci], refs[ci:ci + co], refs[ci + co:])
        carry.start(*parts)
        carry.relay(*parts)
        carry.finish(*parts)

    return pl.pallas_call(
        body, name=name, out_shape=tuple(carry.out_shape),
        in_specs=[ANY] * ci, out_specs=tuple([ANY] * co), scratch_shapes=carry.sems,
    )(*carry.ins)


def _sibling_plan(big, small):
    n = len(big)
    n_copies = 4 * n + len(small)

    def copies(ins, outs, sems):
        send_sems, recv_sems = sems
        x, y, c = _my_place()
        pairs = []
        for t, (_, layer) in enumerate(big):
            for s in range(4):
                pairs.append((ins[t].at[layer, pl.ds(4 * (1 - c) + s, 1)], outs[t].at[pl.ds(s, 1)]))
        pairs += list(zip(ins[n:], outs[n:]))
        return [pltpu.make_async_remote_copy(
            src_ref=src, dst_ref=dst, send_sem=send_sems.at[k], recv_sem=recv_sems.at[k],
            device_id=(x, y, 1 - c), device_id_type=MESH) for k, (src, dst) in enumerate(pairs)]

    def start(ins, outs, sems):
        for cp in copies(ins, outs, sems):
            cp.start()

    def finish(ins, outs, sems):
        for cp in copies(ins, outs, sems):
            cp.wait()

    out_shape = [SDS((4,) + a.shape[2:], a.dtype) for a, _ in big] + [SDS(a.shape, a.dtype) for a in small]
    sems = [pltpu.SemaphoreType.DMA((n_copies,)), pltpu.SemaphoreType.DMA((n_copies,))]
    return _Carried([a for a, _ in big] + list(small), out_shape, sems, start, finish)


def _chips_plan(big, small):
    n, n_small = len(big), len(small)
    max_rows = 512
    parts = [max(1, a.shape[1] // max_rows) for a in big]
    n_copies = 3 * (sum(parts) + n_small)

    def copies(ins, outs, sems, landing):
        send_sems, recv_sems, local_sems = sems
        x, y, c = _my_place()
        my_chip = 2 * x + y
        chips = [(1 - x, y), (x, 1 - y), (1 - x, 1 - y)]
        remote, local = [], []
        for chip in chips:
            to = 2 * chip[0] + chip[1]
            slot = to if landing else my_chip
            pairs = []
            for t in range(n):
                rows_per = big[t].shape[1] // parts[t]
                for p in range(parts[t]):
                    rows = pl.ds(p * rows_per, rows_per)
                    pairs.append((ins[t].at[to, rows], outs[t].at[slot, rows]))
            pairs += [(ins[t], outs[t].at[slot]) for t in range(n, n + n_small)]
            for src, dst in pairs:
                k = len(remote)
                remote.append(pltpu.make_async_remote_copy(
                    src_ref=src, dst_ref=dst, send_sem=send_sems.at[k], recv_sem=recv_sems.at[k],
                    device_id=(*chip, c), device_id_type=MESH))
        for t in range(n):
            local.append(pltpu.make_async_copy(ins[t].at[my_chip], outs[t].at[my_chip], local_sems.at[t]))
        for t in range(n, n + n_small):
            local.append(pltpu.make_async_copy(ins[t], outs[t].at[my_chip], local_sems.at[t]))
        return remote + local

    def start(ins, outs, sems):
        for cp in copies(ins, outs, sems, landing=False):
            cp.start()

    def finish(ins, outs, sems):
        for cp in copies(ins, outs, sems, landing=True):
            cp.wait()

    out_shape = [SDS(a.shape, a.dtype) for a in big] + [SDS((N_CHIP,) + a.shape, a.dtype) for a in small]
    sems = [pltpu.SemaphoreType.DMA((n_copies,)), pltpu.SemaphoreType.DMA((n_copies,)),
            pltpu.SemaphoreType.DMA((n + n_small,))]
    return _Carried(list(big) + list(small), out_shape, sems, start, finish)


def _all_plan(small):
    n = len(small)
    masks = [(m >> 2 & 1, m >> 1 & 1, m & 1) for m in range(1, N_DEV)]

    def copies(ins, outs, sems, landing):
        send_sems, recv_sems, local_sems = sems
        x, y, c = _my_place()
        me = 4 * x + 2 * y + c
        flip = lambda v, bit: 1 - v if bit else v
        remote = []
        for fx, fy, fc in masks:
            peer = (flip(x, fx), flip(y, fy), flip(c, fc))
            slot = 4 * peer[0] + 2 * peer[1] + peer[2] if landing else me
            for t in range(n):
                k = len(remote)
                remote.append(pltpu.make_async_remote_copy(
                    src_ref=ins[t], dst_ref=outs[t].at[slot], send_sem=send_sems.at[k], recv_sem=recv_sems.at[k],
                    device_id=peer, device_id_type=MESH))
        local = [pltpu.make_async_copy(ins[t], outs[t].at[me], local_sems.at[t]) for t in range(n)]
        return remote + local

    def start(ins, outs, sems):
        for cp in copies(ins, outs, sems, landing=False):
            cp.start()

    def finish(ins, outs, sems):
        for cp in copies(ins, outs, sems, landing=True):
            cp.wait()

    out_shape = [SDS((N_DEV,) + a.shape, a.dtype) for a in small]
    sems = [pltpu.SemaphoreType.DMA((7 * n,)), pltpu.SemaphoreType.DMA((7 * n,)), pltpu.SemaphoreType.DMA((n,))]
    return _Carried(list(small), out_shape, sems, start, finish)


def _join(*plans):
    plans = [p for p in plans if p is not None]
    if len(plans) <= 1:
        return plans[0] if plans else None

    def each(fn_name, ins, outs, sems):
        i = o = s = 0
        for p in plans:
            ni, no, ns = len(p.ins), len(p.out_shape), len(p.sems)
            getattr(p, fn_name)(ins[i:i + ni], outs[o:o + no], sems[s:s + ns])
            i, o, s = i + ni, o + no, s + ns

    return _Carried(sum((p.ins for p in plans), []), sum((p.out_shape for p in plans), []),
                    sum((p.sems for p in plans), []),
                    lambda i, o, s: each("start", i, o, s), lambda i, o, s: each("finish", i, o, s),
                    lambda i, o, s: each("relay", i, o, s))


def _row_block(rows):
    return rows if rows <= 256 else 256


def _add_own(tag, core, g, layer, got):
    _, r, c = got.shape
    rb = _row_block(r)

    def body(core_ref, a_ref, b_ref, o_ref):
        o_ref[...] = (a_ref[...] + b_ref[...]).astype(o_ref.dtype)

    return pl.pallas_call(
        body, name=f"add_{tag}", out_shape=SDS(got.shape, BF16),
        grid_spec=pltpu.PrefetchScalarGridSpec(
            num_scalar_prefetch=1, grid=(4, r // rb),
            in_specs=[pl.BlockSpec((None, None, rb, c), lambda s, j, core: (layer, 4 * core[0] + s, j, 0)),
                      pl.BlockSpec((None, rb, c), lambda s, j, core: (s, j, 0))],
            out_specs=pl.BlockSpec((None, rb, c), lambda s, j, core: (s, j, 0))),
        compiler_params=_cp(("parallel", "parallel")),
    )(core, g, got)


def _add_lists(tag, own, got, grid=None, specs=None, dtype=F32):
    n = len(own)

    def body(*refs):
        for a, b, o in zip(refs[:n], refs[n:2 * n], refs[2 * n:]):
            o[...] = (a[...] + b[...]).astype(o.dtype)

    kw = {}
    if grid is not None:
        kw = dict(grid=grid, in_specs=list(specs) * 2, out_specs=tuple(specs),
                  compiler_params=_cp(("parallel",) * len(grid)))
    return pl.pallas_call(
        body, name=f"add_{tag}", out_shape=tuple(SDS(a.shape, dtype) for a in own), **kw)(*own, *got)


def _adamw_math(w, g, m, v):
    m = ADAM_B1 * m + (1.0 - ADAM_B1) * g
    v = ADAM_B2 * v + (1.0 - ADAM_B2) * (g * g)
    m_hat = m / (1.0 - ADAM_B1 ** ADAM_STEP)
    v_hat = v / (1.0 - ADAM_B2 ** ADAM_STEP)
    delta = -ADAM_LR * (m_hat / (jnp.sqrt(v_hat) + ADAM_EPS) + ADAM_WD * w)
    return delta, m, v


def _sum_slots_adamw(tag, slots, w, m, v):
    _, r, c = slots[0].shape
    rb = _row_block(r)

    def body(s0_ref, s1_ref, w_ref, m_ref, v_ref, g_ref, d_ref, nm_ref, nv_ref):
        first = pl.program_id(1) == 0
        g = _sum4([jnp.where(first, s0_ref[k], s1_ref[k]).astype(F32) for k in range(N_CHIP)])
        delta, nm, nv = _adamw_math(w_ref[...], g, m_ref[...], v_ref[...])
        g_ref[...] = g
        d_ref[...] = delta
        nm_ref[...] = nm
        nv_ref[...] = nv

    spec = pl.BlockSpec((None, rb, c), lambda j, l: (l, j, 0))
    sspec = pl.BlockSpec((N_CHIP, rb, c), lambda j, l: (0, j, 0))
    s = SDS((DEPTH, r, c), F32)
    return pl.pallas_call(
        body, name=f"adamw_{tag}", out_shape=(s, s, s, s),
        grid=(r // rb, DEPTH), in_specs=[sspec, sspec, spec, spec, spec], out_specs=(spec, spec, spec, spec),
        compiler_params=_cp(("parallel", "arbitrary")),
    )(*slots, w, m, v)


def _adamw_small(tag, entries, grid=None):
    flat_in, in_specs, out_shape, out_specs, layout = [], [], [], [], []
    for slots, w, m, v, slot_spec, w_spec in entries:
        per_layer = isinstance(slots, (list, tuple))
        n_slot = len(slots) if per_layer else 1
        flat_in += (list(slots) if per_layer else [slots]) + [w, m, v]
        in_specs += [slot_spec] * n_slot + [w_spec] * 3
        out_shape += [SDS(w.shape, F32)] * 4
        out_specs += [w_spec] * 4
        layout.append((per_layer, n_slot))
    n_in = len(flat_in)

    def body(*refs):
        i, o = 0, n_in
        for per_layer, n_slot in layout:
            s_refs = refs[i:i + n_slot]
            w_ref, m_ref, v_ref = refs[i + n_slot:i + n_slot + 3]
            outs = refs[o:o + 4]
            if per_layer:
                for l, s_ref in enumerate(s_refs):
                    at = (slice(l, l + 1),) if len(w_ref.shape) == 2 else (l,)
                    g = _sum_slots(s_ref)
                    res = (g,) + _adamw_math(w_ref[at], g, m_ref[at], v_ref[at])
                    for o_ref, val in zip(outs, res):
                        o_ref[at] = val
            else:
                g = _sum_slots(s_refs[0])
                res = (g,) + _adamw_math(w_ref[...], g, m_ref[...], v_ref[...])
                for o_ref, val in zip(outs, res):
                    o_ref[...] = val
            i += n_slot + 3
            o += 4

    kw = {}
    if grid is not None:
        kw = dict(grid=grid, in_specs=in_specs, out_specs=tuple(out_specs),
                  compiler_params=_cp(("parallel",) * len(grid)))
    res = pl.pallas_call(body, name=f"adamw_{tag}", out_shape=tuple(out_shape), **kw)(*flat_in)
    return [tuple(res[4 * e:4 * e + 4]) for e in range(len(entries))]


def kernel(x, norm_g, w_in, b_in, ssm_log_dt, ssm_lam_re, ssm_lam_im, ssm_b_re, ssm_b_im, ssm_c_re, ssm_c_im, ssm_d, ssm_w_glu, ssm_b_glu, pool_w, pool_scale, w_branch_a, w_branch_b, w_out, final_norm_g, loss_target, m_norm_g, m_w_in, m_b_in, m_ssm_log_dt, m_ssm_lam_re, m_ssm_lam_im, m_ssm_b_re, m_ssm_b_im, m_ssm_c_re, m_ssm_c_im, m_ssm_d, m_ssm_w_glu, m_ssm_b_glu, m_pool_w, m_pool_scale, m_w_branch_a, m_w_branch_b, m_w_out, m_final_norm_g, v_norm_g, v_w_in, v_b_in, v_ssm_log_dt, v_ssm_lam_re, v_ssm_lam_im, v_ssm_b_re, v_ssm_b_im, v_ssm_c_re, v_ssm_c_im, v_ssm_d, v_ssm_w_glu, v_ssm_b_glu, v_pool_w, v_pool_scale, v_w_branch_a, v_w_branch_b, v_w_out, v_final_norm_g):
    weights = dict(norm_g=norm_g, w_in=w_in, b_in=b_in, ssm_log_dt=ssm_log_dt, ssm_lam_re=ssm_lam_re,
                   ssm_lam_im=ssm_lam_im, ssm_b_re=ssm_b_re, ssm_b_im=ssm_b_im, ssm_c_re=ssm_c_re,
                   ssm_c_im=ssm_c_im, ssm_d=ssm_d, ssm_w_glu=ssm_w_glu, ssm_b_glu=ssm_b_glu, pool_w=pool_w,
                   pool_scale=pool_scale, w_branch_a=w_branch_a, w_branch_b=w_branch_b, w_out=w_out,
                   final_norm_g=final_norm_g.reshape(1, D_MODEL))
    mom_m = dict(norm_g=m_norm_g, w_in=m_w_in, b_in=m_b_in, ssm_log_dt=m_ssm_log_dt, ssm_lam_re=m_ssm_lam_re,
                 ssm_lam_im=m_ssm_lam_im, ssm_b_re=m_ssm_b_re, ssm_b_im=m_ssm_b_im, ssm_c_re=m_ssm_c_re,
                 ssm_c_im=m_ssm_c_im, ssm_d=m_ssm_d, ssm_w_glu=m_ssm_w_glu, ssm_b_glu=m_ssm_b_glu,
                 pool_w=m_pool_w, pool_scale=m_pool_scale, w_branch_a=m_w_branch_a, w_branch_b=m_w_branch_b,
                 w_out=m_w_out, final_norm_g=m_final_norm_g.reshape(1, D_MODEL))
    mom_v = dict(norm_g=v_norm_g, w_in=v_w_in, b_in=v_b_in, ssm_log_dt=v_ssm_log_dt, ssm_lam_re=v_ssm_lam_re,
                 ssm_lam_im=v_ssm_lam_im, ssm_b_re=v_ssm_b_re, ssm_b_im=v_ssm_b_im, ssm_c_re=v_ssm_c_re,
                 ssm_c_im=v_ssm_c_im, ssm_d=v_ssm_d, ssm_w_glu=v_ssm_w_glu, ssm_b_glu=v_ssm_b_glu,
                 pool_w=v_pool_w, pool_scale=v_pool_scale, w_branch_a=v_w_branch_a, w_branch_b=v_w_branch_b,
                 w_out=v_w_out, final_norm_g=v_final_norm_g.reshape(1, D_MODEL))
    order = ["norm_g", "w_in", "b_in", "ssm_log_dt", "ssm_lam_re", "ssm_lam_im", "ssm_b_re", "ssm_b_im",
             "ssm_c_re", "ssm_c_im", "ssm_d", "ssm_w_glu", "ssm_b_glu", "pool_w", "pool_scale", "w_branch_a",
             "w_branch_b", "w_out", "final_norm_g"]
    big_names = ["w_in", "ssm_w_glu", "w_branch_a", "w_branch_b", "w_out"]

    log_dt3 = ssm_log_dt.reshape(DEPTH, N_GROUP, 1)
    b_t = lambda a: a.transpose(0, 1, 3, 2)
    for d in (weights, mom_m, mom_v):
        d["ssm_b_re"], d["ssm_b_im"] = b_t(d["ssm_b_re"]), b_t(d["ssm_b_im"])
    bt_re, bt_im = weights["ssm_b_re"], weights["ssm_b_im"]
    abar_re, abar_im, bbt_re, bbt_im = _s5_params(log_dt3, ssm_lam_re, ssm_lam_im, bt_re, bt_im)
    s5_args = (bbt_re, bbt_im, ssm_c_re, ssm_c_im, abar_re, abar_im, ssm_d)

    w16 = {n: weights[n].astype(BF16) for n in big_names}
    rest = [w16[n] for n in big_names[1:]]
    half = D_MODEL // 2
    wg_in = [None, [None, None]]
    wg_rest = [None, None]
    wg_in[0] = list(_run_carried("gather_w_in_l0", _gather_plan([w16["w_in"]], 0)))
    xs = [x.reshape(SEQ, D_MODEL)]
    saved = []
    for l in range(DEPTH):
        proj, moved = _norm_proj(l, xs[l], norm_g, wg_in[l], b_in,
                                 carry=_gather_plan(rest, 0, by_columns=(1, 2)) if l == 0 else None)
        if l == 0:
            wg_rest[0] = moved
        carry = (_gather_plan([w16["w_in"]], 1, rows_of=(0, half)) if l == 0
                 else _gather_plan(rest, 1, by_columns=(1, 2)))
        (states, y0), moved = _s5_scan_fwd(l, proj, *s5_args, carry=carry)
        if l == 0:
            (wg_in[1][0],) = moved
        else:
            wg_rest[1] = moved
        pooled = _pool_fwd(l, proj)
        wg_glu, wg_a, wg_b, wg_out = wg_rest[l]
        x_next, moved = _mix_fwd(l, xs[l], proj, y0, pooled, wg_glu, ssm_b_glu, pool_w, pool_scale, wg_a, wg_b,
                                 wg_out, carry=_gather_plan([w16["w_in"]], 1, rows_of=(half, half)) if l == 0 else None)
        if l == 0:
            (wg_in[1][1],) = moved
        xs.append(x_next)
        saved.append((proj, states, y0, pooled))

    dx, loss_part, g_final = _loss_head(xs[DEPTH], loss_target.reshape(SEQ, D_MODEL), weights["final_norm_g"])
    loss = lax.psum(loss_part[0, 0], ("x", "y", "c"))

    core = lax.axis_index("c").astype(jnp.int32).reshape(1)
    vec_names = ["norm_g", "b_in", "ssm_d", "ssm_b_glu", "pool_scale", "ssm_log_dt"]
    s5_names = ["ssm_log_dt", "ssm_lam_re", "ssm_lam_im", "ssm_b_re", "ssm_b_im"]
    mat_names = ["pool_w", "ssm_c_re", "ssm_c_im", "ssm_b_re", "ssm_b_im"]
    lane_sparse = ("ssm_c_re", "ssm_c_im", "ssm_b_re", "ssm_b_im")

    def dense(key, a):
        return a.reshape(-1, 128) if key[0] in lane_sparse else a

    def undense(key, slots):
        return slots.reshape((N_CHIP, N_GROUP, GROUP_W, STATE)) if key[0] in lane_sparse else slots

    def add_small(tag, keys, own, got):
        out = [None] * len(keys)
        whole = [i for i, k in enumerate(keys) if k[0] not in mat_names]
        tiled = [i for i, k in enumerate(keys) if k[0] in mat_names]
        if whole:
            for i, r in zip(whole, _add_lists(f"{tag}_a", [own[i] for i in whole], [got[i] for i in whole])):
                out[i] = r
        if tiled:
            specs = [pl.BlockSpec((1, POOL_GROUP, POOL_GROUP), lambda j: (j, 0, 0)) if keys[i][0] == "pool_w"
                     else pl.BlockSpec((own[i].shape[0] // N_CHUNK, 128), lambda j: (j, 0)) for i in tiled]
            for i, r in zip(tiled, _add_lists(f"{tag}_b", [own[i] for i in tiled], [got[i] for i in tiled],
                                              grid=(N_CHUNK,), specs=specs, dtype=BF16)):
                out[i] = r
        return out

    sm = {("final_norm_g", None): g_final}
    slots = {}
    grads = dict.fromkeys(big_names)

    class Wave:
        def __init__(self, tag, layer, big, keys):
            self.tag, self.layer, self.big, self.keys = tag, layer, big, keys

        def to_sibling(self):
            self.own = [dense(k, sm[k]) for k in self.keys]
            return _sibling_plan([(grads[n], self.layer) for n in self.big], self.own)

        def add(self, moved):
            nb = len(self.big)
            self.chip_big = [_add_own(f"{self.tag}_{n}", core, grads[n], self.layer, b)
                             for n, b in zip(self.big, moved[:nb])]
            self.chip_small = add_small(self.tag, self.keys, self.own, moved[nb:])

        def to_chips(self, big=None, small=True):
            self.sent = list(self.big if big is None else big), small
            return _chips_plan([self.chip_big[self.big.index(n)] for n in self.sent[0]],
                               self.chip_small if small else [])

        def landed(self, moved):
            names, small = self.sent
            for n, s in zip(names, moved[:len(names)]):
                slots[(n, self.layer)] = s
            if small:
                for k, s in zip(self.keys, moved[len(names):]):
                    slots[k] = undense(k, s)
            return moved[len(names) + (len(self.keys) if small else 0):]

    def s5_param_grads(l, g_abar_re, g_abar_im, g_bbt_re, g_bbt_im):
        g = _s5_params_bwd(l, log_dt3, ssm_lam_re, ssm_lam_im, bt_re, bt_im, g_abar_re, g_abar_im, g_bbt_re, g_bbt_im)
        sm[("ssm_log_dt", l)] = g[0].reshape(1, N_GROUP)
        for n, a in zip(s5_names[1:], g[1:]):
            sm[(n, l)] = a

    small1 = ["b_in", "ssm_d", "ssm_b_glu", "pool_scale", "pool_w", "ssm_c_re", "ssm_c_im"] + s5_names
    w1 = Wave("chip1", 1, list(big_names), [(n, 1) for n in small1] + [("final_norm_g", None)])
    early = Wave("chip0e", 0, big_names[1:], [("pool_w", 0), ("pool_scale", 0), ("ssm_b_glu", 0)])
    mid = Wave("chip0m", 0, [], [(n, 0) for n in ["ssm_c_re", "ssm_c_im", "ssm_d"] + s5_names] + [("norm_g", 1)])
    late = Wave("chip0l", 0, ["w_in"], [("b_in", 0)])

    mix_prev, gw_in = None, None
    for l in reversed(range(DEPTH)):
        proj, states, y0, pooled = saved[l]
        wg_glu, wg_a, wg_b, wg_out = wg_rest[l]
        res, moved = _mix_bwd(l, dx, proj, y0, pooled, wg_glu, ssm_b_glu, pool_w, pool_scale, wg_a, wg_b, wg_out,
                              mix_prev, carry=None if l == 1 else w1.to_chips(big=["w_in"], small=False))
        if l == 0:
            w1.landed(moved)
        dproj, dy0, dpooled = res[:3]
        mix_prev = list(res[3:7])
        grads["w_out"], grads["w_branch_a"], grads["w_branch_b"], grads["ssm_w_glu"] = mix_prev
        sm[("pool_w", l)], sm[("pool_scale", l)], sm[("ssm_b_glu", l)] = res[7:]
        dproj = _pool_bwd(l, dpooled, dproj)
        carry = None if l == 1 else _join(w1.to_chips(big=big_names[1:]), early.to_sibling())
        res, moved = _s5_scan_bwd(l, dy0, proj, states, *s5_args, dproj, carry=carry)
        if l == 0:
            early.add(w1.landed(moved))
        dproj, g_bbt_re, g_bbt_im, sm[("ssm_c_re", l)], sm[("ssm_c_im", l)], g_abar_re, g_abar_im, sm[("ssm_d", l)] = res
        s5_param_grads(l, g_abar_re, g_abar_im, g_bbt_re, g_bbt_im)
        carry = None if l == 1 else _join(early.to_chips(), mid.to_sibling())
        (gw_in, sm[("b_in", l)]), moved = _proj_wgrad(l, xs[l], norm_g, dproj, gw_in, carry=carry)
        grads["w_in"] = gw_in
        if l == 0:
            mid.add(early.landed(moved))
        carry = w1.to_sibling() if l == 1 else _join(mid.to_chips(), late.to_sibling())
        (dx, sm[("norm_g", l)]), moved = _proj_dgrad(l, dx, xs[l], norm_g, dproj, wg_in[l], carry=carry)
        if l == 1:
            w1.add(moved)
        else:
            late.add(mid.landed(moved))
    grad_x = dx.reshape(1, SEQ, D_MODEL)
    moved = late.landed(_run_carried("exchange_last", _join(late.to_chips(), _all_plan([sm[("norm_g", 0)]]))))
    slots[("norm_g", 0)] = moved[0]

    res = {}
    for n in big_names:
        res[n] = _sum_slots_adamw(n, [slots[(n, l)] for l in range(DEPTH)], weights[n], mom_m[n], mom_v[n])
    per_layer = lambda n: [slots[(n, l)] for l in range(DEPTH)]
    names_a = vec_names + ["ssm_lam_re", "ssm_lam_im"]
    entries_a = [(per_layer(n), weights[n], mom_m[n], mom_v[n], None, None) for n in names_a]
    n = "final_norm_g"
    entries_a.append((slots[(n, None)], weights[n], mom_m[n], mom_v[n], None, None))
    out_a = _adamw_small("small_a", entries_a)
    for n, r in zip(names_a + ["final_norm_g"], out_a):
        res[n] = r
    res["final_norm_g"] = tuple(a.reshape(D_MODEL) for a in res["final_norm_g"])
    pw_s = pl.BlockSpec((N_CHIP, 1, POOL_GROUP, POOL_GROUP), lambda j: (0, j, 0, 0))
    pw_w = pl.BlockSpec((DEPTH, 1, POOL_GROUP, POOL_GROUP), lambda j: (0, j, 0, 0))
    c_s = pl.BlockSpec((N_CHIP, CH_G, GROUP_W, STATE), lambda j: (0, j, 0, 0))
    c_w = pl.BlockSpec((DEPTH, CH_G, GROUP_W, STATE), lambda j: (0, j, 0, 0))
    entries_b = [(per_layer(n), weights[n], mom_m[n], mom_v[n], pw_s if n == "pool_w" else c_s,
                  pw_w if n == "pool_w" else c_w) for n in mat_names]
    out_b = _adamw_small("small_b", entries_b, grid=(N_CHUNK,))
    for n, r in zip(mat_names, out_b):
        res[n] = tuple(b_t(a) for a in r) if n in ("ssm_b_re", "ssm_b_im") else r

    outs = [loss, grad_x]
    for i in range(4):
        outs += [res[n][i] for n in order]
    return tuple(outs)
```

```python
import math

import jax
import jax.numpy as jnp
from jax import lax
from jax.experimental import pallas as pl
from jax.experimental.pallas import tpu as pltpu

F32 = jnp.float32
BF16 = jnp.bfloat16

SEQ = 2048
D_MODEL = 1024
N_IN = 4096
WIDTH = 512
N_GROUP = 32
GROUP_W = 16
STATE = 64
N_STATE = N_GROUP * STATE
N_CHUNK = 4
CH_G = N_GROUP // N_CHUNK
CH_W = WIDTH // N_CHUNK
CH_S = N_STATE // N_CHUNK
N_DEV = 8
N_CHIP = 4
POOL_WINDOWS = (2, 4, 8, 16)
POOL_GROUP = 128
EPS = 1e-6
DEPTH = 2

ADAM_LR = 0.001
ADAM_B1 = 0.9
ADAM_B2 = 0.999
ADAM_EPS = 1e-08
ADAM_WD = 0.01
ADAM_STEP = 10

TILE_M = 256
ROW_BLK = 512
VMEM_LIMIT = 48 * 1024 * 1024
VMEM_LIMIT_BIG = 60 * 1024 * 1024
MESH = pl.DeviceIdType.MESH
ANY = pl.BlockSpec(memory_space=pl.ANY)

GELU_C = math.sqrt(2.0 / math.pi)
GELU_A = 0.044715

SDS = jax.ShapeDtypeStruct


def _cp(sem=None, limit=VMEM_LIMIT):
    return pltpu.CompilerParams(dimension_semantics=sem, vmem_limit_bytes=limit)


def _dot(a, b):
    return jnp.dot(a, b, preferred_element_type=F32)


def _dot_nt(a, b):
    return lax.dot_general(a, b, (((1,), (1,)), ((), ())), preferred_element_type=F32)


def _dot_tn(a, b):
    return lax.dot_general(a, b, (((0,), (0,)), ((), ())), preferred_element_type=F32)


def _sig(x):
    return jax.nn.sigmoid(x)


def _rms(x):
    rs = lax.rsqrt(jnp.mean(x * x, axis=-1, keepdims=True) + EPS)
    return rs, x * rs


def _slot(n):
    return 4 * (n % 2) + n // 2


def _const(shape):
    n = len(shape)
    return pl.BlockSpec(shape, lambda *_: (0,) * n)


def _sum4(p):
    return (p[0] + p[1]) + (p[2] + p[3])


def _sum_slots(s_ref):
    vals = [s_ref[k].astype(F32) for k in range(s_ref.shape[0])]
    while len(vals) > 1:
        vals = [vals[i] + vals[i + 1] for i in range(0, len(vals), 2)]
    return vals[0]


def _s5_param_fn(log_dt, lam_re, lam_im, bt_re, bt_im):
    dt = jnp.exp(log_dt)
    mag = jnp.exp(lam_re * dt)
    ang = lam_im * dt
    abar_re = mag * jnp.cos(ang)
    abar_im = mag * jnp.sin(ang)
    num_re = abar_re - 1.0
    num_im = abar_im
    den = lam_re * lam_re + lam_im * lam_im
    coef_re = (num_re * lam_re + num_im * lam_im) / den
    coef_im = (num_im * lam_re - num_re * lam_im) / den
    bbar_re = coef_re[..., None, :] * bt_re - coef_im[..., None, :] * bt_im
    bbar_im = coef_re[..., None, :] * bt_im + coef_im[..., None, :] * bt_re
    return abar_re, abar_im, bbar_re, bbar_im


def _s5_params(log_dt, lam_re, lam_im, bt_re, bt_im):
    def body(ld, lr, li, br, bi, o_ar, o_ai, o_br, o_bi):
        ar, ai, bbr, bbi = _s5_param_fn(ld[...], lr[...], li[...], br[...], bi[...])
        o_ar[...] = ar
        o_ai[...] = ai
        o_br[...] = bbr
        o_bi[...] = bbi

    return pl.pallas_call(
        body, name="s5_params",
        out_shape=(SDS(lam_re.shape, F32), SDS(lam_re.shape, F32), SDS(bt_re.shape, F32), SDS(bt_re.shape, F32)),
    )(log_dt, lam_re, lam_im, bt_re, bt_im)


def _s5_params_bwd(layer, log_dt, lam_re, lam_im, bt_re, bt_im, g_ar, g_ai, g_br, g_bi):
    def body(ld, lr, li, br, bi, car, cai, cbr, cbi, o_ld, o_lr, o_li, o_br, o_bi):
        _, vjp = jax.vjp(_s5_param_fn, ld[...], lr[...], li[...], br[...], bi[...])
        d_ld, d_lr, d_li, d_br, d_bi = vjp((car[...], cai[...], cbr[...], cbi[...]))
        o_ld[...] = d_ld
        o_lr[...] = d_lr
        o_li[...] = d_li
        o_br[...] = d_br
        o_bi[...] = d_bi

    one = lambda shape: pl.BlockSpec((None,) + shape, lambda i: (layer,) + (0,) * len(shape))
    whole = lambda shape: _const(shape)
    vec, lam, mat = (N_GROUP, 1), (N_GROUP, STATE), (N_GROUP, GROUP_W, STATE)
    return pl.pallas_call(
        body, name=f"s5_params_bwd_l{layer}", grid=(1,),
        in_specs=[one(vec), one(lam), one(lam), one(mat), one(mat), whole(lam), whole(lam), whole(mat), whole(mat)],
        out_specs=(whole(vec), whole(lam), whole(lam), whole(mat), whole(mat)),
        out_shape=(SDS(vec, F32), SDS(lam, F32), SDS(lam, F32), SDS(mat, F32), SDS(mat, F32)),
    )(log_dt, lam_re, lam_im, bt_re, bt_im, g_ar, g_ai, g_br, g_bi)


def _norm_proj(layer, x, norm_g, wg_in, b_in, carry=None):
    n_w = len(wg_in)

    def body(x_ref, g_ref, b_ref, *refs):
        w_refs, o_ref = refs[:n_w], refs[n_w]
        _, xn = _rms(x_ref[...])
        h = (xn * g_ref[layer:layer + 1, :]).astype(BF16)
        for k in range(N_DEV):
            cols = slice(k * WIDTH, (k + 1) * WIDTH)
            acc = b_ref[layer:layer + 1, cols]
            row = 0
            for w_ref in w_refs:
                rows = w_ref.shape[1]
                acc = acc + _dot(h[:, row:row + rows], w_ref[k])
                row += rows
            o_ref[:, cols] = acc

    (proj,), moved = _pcall(
        body, name=f"norm_proj_l{layer}",
        out_shape=[SDS((SEQ, N_IN), F32)],
        grid=(SEQ // TILE_M,),
        in_specs=[pl.BlockSpec((TILE_M, D_MODEL), lambda i: (i, 0)),
                  _const((DEPTH, D_MODEL)),
                  _const((DEPTH, N_IN))] + [_const(w.shape) for w in wg_in],
        out_specs=[pl.BlockSpec((TILE_M, N_IN), lambda i: (i, 0))],
        args=[x, norm_g, b_in, *wg_in], sem=("parallel",), carry=carry)
    return proj, moved


TIME_BLK = 512
N_TBLK = SEQ // TIME_BLK
N_PANEL = CH_S // 128
STATE_SHAPE = (N_PANEL, SEQ * 8, 128)


def _s5_layer_specs(layer):
    mat = lambda: pl.BlockSpec((None, N_GROUP, GROUP_W, STATE), lambda i: (layer, 0, 0, 0))
    ab = lambda: pl.BlockSpec((None, N_GROUP, STATE), lambda i: (layer, 0, 0))
    return [mat(), mat(), mat(), mat(), ab(), ab(), _const((DEPTH, WIDTH))]


def _s5_layer_scratch():
    return [pltpu.VMEM((N_CHUNK, CH_W, CH_S), BF16)] * 4 + [pltpu.VMEM((8, CH_S), F32)] * 2


def _s5_layer_fill(btre_ref, btim_ref, cre_ref, cim_ref, are_ref, aim_ref, bdre, bdim, ctre, ctim, a1, a2):
    for m in (bdre, bdim, ctre, ctim):
        m[...] = jnp.zeros_like(m)
    for grp in range(N_GROUP):
        k, g = divmod(grp, CH_G)
        rows = slice(g * GROUP_W, (g + 1) * GROUP_W)
        cols = slice(g * STATE, (g + 1) * STATE)
        bdre[k, rows, cols] = btre_ref[grp].astype(BF16)
        bdim[k, rows, cols] = btim_ref[grp].astype(BF16)
        ctre[k, rows, cols] = cre_ref[grp].astype(BF16)
        ctim[k, rows, cols] = cim_ref[grp].astype(BF16)
        ar = are_ref[grp:grp + 1, :]
        ai = aim_ref[grp:grp + 1, :]
        a1[k:k + 1, cols] = ar
        a1[N_CHUNK + k:N_CHUNK + k + 1, cols] = ar
        a2[k:k + 1, cols] = -ai
        a2[N_CHUNK + k:N_CHUNK + k + 1, cols] = ai


SCAN_UNROLL = 8


def _panels(tile):
    return [tile[:, p * 128:(p + 1) * 128] for p in range(N_PANEL)]


def _rows_load(ref, row):
    return jnp.concatenate([ref[p, pl.ds(row, TIME_BLK, stride=8), :] for p in range(N_PANEL)], axis=1)


def _rows_store(ref, row, val):
    for p in range(N_PANEL):
        ref[p, pl.ds(row, TIME_BLK, stride=8), :] = val[:, p * 128:(p + 1) * 128]


def _s5_scan_fwd(layer, proj, bbt_re, bbt_im, c_re, c_im, abar_re, abar_im, d_skip, carry=None):
    def body(u_ref, btre_ref, btim_ref, cre_ref, cim_ref, are_ref, aim_ref, d_ref, s_ref, y_ref,
             bdre, bdim, ctre, ctim, a1, a2, state):
        @pl.when(pl.program_id(0) == 0)
        def _():
            _s5_layer_fill(btre_ref, btim_ref, cre_ref, cim_ref, are_ref, aim_ref, bdre, bdim, ctre, ctim, a1, a2)
            state[...] = jnp.zeros_like(state)

        for k in range(N_CHUNK):
            ub = u_ref[:, k * CH_W:(k + 1) * CH_W].astype(BF16)
            _rows_store(s_ref, k, _dot(ub, bdre[k]))
            _rows_store(s_ref, N_CHUNK + k, _dot(ub, bdim[k]))
        m1 = _panels(a1[...])
        m2 = _panels(a2[...])

        def steps(n, tile):
            for r in range(SCAN_UNROLL):
                rows = pl.ds(pl.multiple_of((n * SCAN_UNROLL + r) * 8, 8), 8)
                tile = [m1[p] * tile[p] + m2[p] * pltpu.roll(tile[p], N_CHUNK, 0) + s_ref[p, rows, :]
                        for p in range(N_PANEL)]
                for p in range(N_PANEL):
                    s_ref[p, rows, :] = tile[p]
            return tile

        tile = lax.fori_loop(0, TIME_BLK // SCAN_UNROLL, steps, _panels(state[...]))
        state[...] = jnp.concatenate(tile, axis=1)
        d = d_ref[layer:layer + 1, :]
        for k in range(N_CHUNK):
            cols = slice(k * CH_W, (k + 1) * CH_W)
            y = (_dot_nt(_rows_load(s_ref, k).astype(BF16), ctre[k])
                 - _dot_nt(_rows_load(s_ref, N_CHUNK + k).astype(BF16), ctim[k]))
            y_ref[:, cols] = y + d[:, cols] * u_ref[:, cols]

    return _pcall(
        body, name=f"s5_fwd_l{layer}",
        out_shape=(SDS(STATE_SHAPE, F32), SDS((SEQ, WIDTH), F32)),
        grid=(N_TBLK,),
        in_specs=[pl.BlockSpec((TIME_BLK, WIDTH), lambda i: (i, 0))] + _s5_layer_specs(layer),
        out_specs=(pl.BlockSpec((N_PANEL, TIME_BLK * 8, 128), lambda i: (0, i, 0)),
                   pl.BlockSpec((TIME_BLK, WIDTH), lambda i: (i, 0))),
        scratch_shapes=_s5_layer_scratch() + [pltpu.VMEM((8, CH_S), F32)],
        args=[proj, bbt_re, bbt_im, c_re, c_im, abar_re, abar_im, d_skip], sem=("arbitrary",), carry=carry)


def _s5_scan_bwd(layer, dy0, proj, states, bbt_re, bbt_im, c_re, c_im, abar_re, abar_im, d_skip, dproj,
                 carry=None):
    def body(dy_ref, u_ref, s_ref, sprev_ref, btre_ref, btim_ref, cre_ref, cim_ref, are_ref, aim_ref, d_ref, _,
             du_ref, gbre_ref, gbim_ref, gcre_ref, gcim_ref, gare_ref, gaim_ref, gd_ref,
             lam_ref, bdre, bdim, ctre, ctim, a1, a2, state, acc1, acc2, gbre, gbim, gcre, gcim, gd):
        step_id = pl.program_id(0)

        @pl.when(step_id == 0)
        def _():
            _s5_layer_fill(btre_ref, btim_ref, cre_ref, cim_ref, are_ref, aim_ref, bdre, bdim, ctre, ctim, a1, a2)
            for r in (state, acc1, acc2, gbre, gbim, gcre, gcim, gd):
                r[...] = jnp.zeros_like(r)

        for k in range(N_CHUNK):
            dyb = dy_ref[:, k * CH_W:(k + 1) * CH_W].astype(BF16)
            _rows_store(lam_ref, k, _dot(dyb, ctre[k]))
            _rows_store(lam_ref, N_CHUNK + k, -_dot(dyb, ctim[k]))
            gcre[k] += _dot_tn(dyb, _rows_load(s_ref, k).astype(BF16))
            gcim[k] -= _dot_tn(dyb, _rows_load(s_ref, N_CHUNK + k).astype(BF16))

        m1 = _panels(a1[...])
        m2 = _panels(-a2[...])
        has_before = (step_id < N_TBLK - 1).astype(F32)

        def one(t8, c, first_token):
            tile, swapped, p1, p2 = c
            rows = pl.ds(t8, 8)
            tile = [m1[p] * tile[p] + m2[p] * swapped[p] + lam_ref[p, rows, :] for p in range(N_PANEL)]
            swapped = [pltpu.roll(tile[p], N_CHUNK, 0) for p in range(N_PANEL)]
            for p in range(N_PANEL):
                lam_ref[p, rows, :] = tile[p]
            if first_token:
                before = [sprev_ref[p] * has_before for p in range(N_PANEL)]
            else:
                before = [s_ref[p, pl.ds(t8 - 8, 8), :] for p in range(N_PANEL)]
            p1 = [p1[p] + tile[p] * before[p] for p in range(N_PANEL)]
            p2 = [p2[p] + swapped[p] * before[p] for p in range(N_PANEL)]
            return tile, swapped, p1, p2

        def steps(n, c):
            for r in range(SCAN_UNROLL):
                t8 = pl.multiple_of((TIME_BLK - 1 - (n * SCAN_UNROLL + r)) * 8, 8)
                c = one(t8, c, False)
            return c

        tile0 = _panels(state[...])
        c = (tile0, [pltpu.roll(t, N_CHUNK, 0) for t in tile0], _panels(acc1[...]), _panels(acc2[...]))
        c = lax.fori_loop(0, TIME_BLK // SCAN_UNROLL - 1, steps, c)
        for r in range(SCAN_UNROLL - 1, -1, -1):
            c = one(r * 8, c, r == 0)
        state[...] = jnp.concatenate(c[0], axis=1)
        acc1[...] = jnp.concatenate(c[2], axis=1)
        acc2[...] = jnp.concatenate(c[3], axis=1)

        d = d_ref[layer:layer + 1, :]
        for k in range(N_CHUNK):
            cols = slice(k * CH_W, (k + 1) * CH_W)
            lrb = _rows_load(lam_ref, k).astype(BF16)
            lib = _rows_load(lam_ref, N_CHUNK + k).astype(BF16)
            u = u_ref[:, cols]
            ub = u.astype(BF16)
            dy = dy_ref[:, cols]
            du = dy * d[:, cols] + _dot_nt(lrb, bdre[k]) + _dot_nt(lib, bdim[k])
            du_ref[:, cols] = du.astype(BF16)
            gbre[k] += _dot_tn(ub, lrb)
            gbim[k] += _dot_tn(ub, lib)
        gd[...] += jnp.sum(dy_ref[...] * u_ref[...], axis=0, keepdims=True)

        @pl.when(step_id == N_TBLK - 1)
        def _():
            gd_ref[...] = gd[...]
            ga_re = acc1[0:N_CHUNK, :] + acc1[N_CHUNK:, :]
            ga_im = acc2[0:N_CHUNK, :] - acc2[N_CHUNK:, :]
            for grp in range(N_GROUP):
                k, g = divmod(grp, CH_G)
                rows = slice(g * GROUP_W, (g + 1) * GROUP_W)
                cols = slice(g * STATE, (g + 1) * STATE)
                gcre_ref[grp] = gcre[k, rows, cols]
                gcim_ref[grp] = gcim[k, rows, cols]
                gbre_ref[grp] = gbre[k, rows, cols]
                gbim_ref[grp] = gbim[k, rows, cols]
                gare_ref[grp:grp + 1, :] = ga_re[k:k + 1, cols]
                gaim_ref[grp:grp + 1, :] = ga_im[k:k + 1, cols]

    back = lambda i: N_TBLK - 1 - i
    tok = lambda: pl.BlockSpec((TIME_BLK, WIDTH), lambda i: (back(i), 0))
    mat = lambda: _const((N_GROUP, GROUP_W, STATE))
    acc_mat = pltpu.VMEM((N_CHUNK, CH_W, CH_S), F32)
    return _pcall(
        body, name=f"s5_bwd_l{layer}",
        out_shape=(SDS((SEQ, N_IN), BF16), SDS((N_GROUP, GROUP_W, STATE), F32), SDS((N_GROUP, GROUP_W, STATE), F32),
                   SDS((N_GROUP, GROUP_W, STATE), F32), SDS((N_GROUP, GROUP_W, STATE), F32),
                   SDS((N_GROUP, STATE), F32), SDS((N_GROUP, STATE), F32), SDS((1, WIDTH), F32)),
        grid=(N_TBLK,),
        in_specs=[tok(), tok(),
                  pl.BlockSpec((N_PANEL, TIME_BLK * 8, 128), lambda i: (0, back(i), 0)),
                  pl.BlockSpec((N_PANEL, 8, 128), lambda i: (0, jnp.maximum(back(i) * TIME_BLK - 1, 0), 0))]
        + _s5_layer_specs(layer) + [ANY],
        out_specs=(tok(), mat(), mat(), mat(), mat(), _const((N_GROUP, STATE)), _const((N_GROUP, STATE)),
                   _const((1, WIDTH))),
        scratch_shapes=[pltpu.VMEM((N_PANEL, TIME_BLK * 8, 128), F32)] + _s5_layer_scratch()
        + [pltpu.VMEM((8, CH_S), F32)] * 3 + [acc_mat] * 4 + [pltpu.VMEM((1, WIDTH), F32)],
        args=[dy0, proj, states, states, bbt_re, bbt_im, c_re, c_im, abar_re, abar_im, d_skip, dproj],
        aliases={11: 0}, sem=("arbitrary",), limit=VMEM_LIMIT_BIG, carry=carry)


def _pool_counts(win):
    t = lax.broadcasted_iota(jnp.int32, (SEQ, POOL_GROUP), 0)
    return t, jnp.minimum(t + 1, win).astype(F32)


def _pool_fwd(layer, proj):
    def body(u_ref, o_ref):
        for gi, win in enumerate(POOL_WINDOWS):
            cols = slice(gi * POOL_GROUP, (gi + 1) * POOL_GROUP)
            u = u_ref[:, cols]
            t, count = _pool_counts(win)
            acc = u
            k = 1
            while k < win:
                acc = acc + jnp.where(t >= k, pltpu.roll(acc, k, 0), 0.0)
                k *= 2
            o_ref[:, cols] = acc / count - u

    return pl.pallas_call(
        body, name=f"pool_fwd_l{layer}",
        out_shape=SDS((SEQ, WIDTH), F32),
        grid=(1,),
        in_specs=[pl.BlockSpec((SEQ, WIDTH), lambda i: (0, 2))],
        out_specs=pl.BlockSpec((SEQ, WIDTH), lambda i: (0, 0)),
        compiler_params=_cp(("arbitrary",)),
    )(proj)


def _gelu_parts(y0):
    t = jnp.tanh(GELU_C * (y0 + GELU_A * (y0 * y0 * y0)))
    return t, 0.5 * y0 * (1.0 + t)


def _mix_forward(layer, p_ref, y0_ref, pooled_ref, wglu_ref, bglu_ref, pw_ref, scale_ref, wa_ref, wb_ref):
    za = p_ref[:, WIDTH:2 * WIDTH]
    zb = p_ref[:, 3 * WIDTH:4 * WIDTH]
    ga = p_ref[:, 4 * WIDTH:4 * WIDTH + D_MODEL]
    gb = p_ref[:, 4 * WIDTH + D_MODEL:]
    y0 = y0_ref[...]
    t, y1 = _gelu_parts(y0)
    y1b = y1.astype(BF16)
    q = _dot(y1b, wglu_ref[...].reshape(WIDTH, WIDTH)) + bglu_ref[layer:layer + 1, :]
    sq = _sig(q)
    y2 = y1 * sq
    sza = _sig(za)
    silu_za = za * sza
    ya = y2 * silu_za
    pooled = pooled_ref[...]
    mixed = jnp.concatenate(
        [_dot(pooled[:, g * POOL_GROUP:(g + 1) * POOL_GROUP].astype(BF16), pw_ref[g].astype(BF16))
         for g in range(len(POOL_WINDOWS))], axis=1)
    szb = _sig(zb)
    silu_zb = zb * szb
    scale = scale_ref[layer:layer + 1, :]
    ms = mixed * scale
    yb = ms * silu_zb
    yab = ya.astype(BF16)
    ybb = yb.astype(BF16)
    ma = _dot(yab, wa_ref[...])
    mb = _dot(ybb, wb_ref[...])
    sga = _sig(ga)
    sgb = _sig(gb)
    merged = sga * ma + sgb * mb
    return dict(za=za, zb=zb, y0=y0, t=t, y1=y1, y1b=y1b, sq=sq, y2=y2, sza=sza, silu_za=silu_za,
                pooled=pooled, mixed=mixed, szb=szb, silu_zb=silu_zb, scale=scale, ms=ms, yab=yab, ybb=ybb,
                ma=ma, mb=mb, sga=sga, sgb=sgb, merged=merged)


def _mix_weight_specs(layer):
    return [_const((N_DEV, WIDTH // N_DEV, WIDTH)),
            _const((DEPTH, WIDTH)),
            pl.BlockSpec((None, 4, POOL_GROUP, POOL_GROUP), lambda i: (layer, 0, 0, 0)),
            _const((DEPTH, WIDTH)),
            _const((WIDTH, D_MODEL)),
            _const((WIDTH, D_MODEL)),
            _const((N_DEV, D_MODEL // N_DEV, D_MODEL))]


def _mix_fwd(layer, x, proj, y0, pooled, wg_glu, b_glu, pool_w, pool_scale, wg_a, wg_b, wg_out, carry=None):
    def body(x_ref, p_ref, y0_ref, pooled_ref, wglu_ref, bglu_ref, pw_ref, scale_ref, wa_ref, wb_ref,
             wout_ref, o_ref):
        f = _mix_forward(layer, p_ref, y0_ref, pooled_ref, wglu_ref, bglu_ref, pw_ref, scale_ref, wa_ref, wb_ref)
        wout = wout_ref[...].reshape(D_MODEL, D_MODEL)
        o_ref[...] = x_ref[...] + _dot(f["merged"].astype(BF16), wout)

    (x_next,), moved = _pcall(
        body, name=f"mix_fwd_l{layer}",
        out_shape=[SDS((SEQ, D_MODEL), F32)],
        grid=(SEQ // TILE_M,),
        in_specs=[pl.BlockSpec((TILE_M, D_MODEL), lambda i: (i, 0)),
                  pl.BlockSpec((TILE_M, N_IN), lambda i: (i, 0)),
                  pl.BlockSpec((TILE_M, WIDTH), lambda i: (i, 0)),
                  pl.BlockSpec((TILE_M, WIDTH), lambda i: (i, 0))] + _mix_weight_specs(layer),
        out_specs=[pl.BlockSpec((TILE_M, D_MODEL), lambda i: (i, 0))],
        args=[x, proj, y0, pooled, wg_glu, b_glu, pool_w, pool_scale, wg_a, wg_b, wg_out],
        sem=("parallel",), carry=carry)
    return x_next, moved


def _loss_head(x, target, final_g):
    def body(x_ref, t_ref, g_ref, dx_ref, loss_ref, gg_ref):
        @pl.when(pl.program_id(0) == 0)
        def _():
            loss_ref[...] = jnp.zeros_like(loss_ref)
            gg_ref[...] = jnp.zeros_like(gg_ref)

        g = g_ref[...]
        rs, xn = _rms(x_ref[...])
        err = xn * g - t_ref[...]
        loss_ref[...] += 0.5 * jnp.sum(jnp.mean(err * err, axis=-1, keepdims=True), axis=0, keepdims=True)
        dy = err * (1.0 / D_MODEL)
        gg_ref[...] += jnp.sum(dy * xn, axis=0, keepdims=True)
        dxn = dy * g
        dx_ref[...] = rs * (dxn - xn * jnp.mean(dxn * xn, axis=-1, keepdims=True))

    return pl.pallas_call(
        body, name="loss_head",
        out_shape=(SDS((SEQ, D_MODEL), F32), SDS((1, 1), F32), SDS((1, D_MODEL), F32)),
        grid=(SEQ // TILE_M,),
        in_specs=[pl.BlockSpec((TILE_M, D_MODEL), lambda i: (i, 0)),
                  pl.BlockSpec((TILE_M, D_MODEL), lambda i: (i, 0)),
                  _const((1, D_MODEL))],
        out_specs=(pl.BlockSpec((TILE_M, D_MODEL), lambda i: (i, 0)), _const((1, 1)), _const((1, D_MODEL))),
        compiler_params=_cp(("arbitrary",)),
    )(x, target, final_g)


def _big_shapes():
    return dict(w_out=(DEPTH, N_DEV, D_MODEL // N_DEV, D_MODEL), w_branch_a=(DEPTH, N_DEV, WIDTH, D_MODEL // N_DEV),
                w_branch_b=(DEPTH, N_DEV, WIDTH, D_MODEL // N_DEV), ssm_w_glu=(DEPTH, N_DEV, WIDTH // N_DEV, WIDTH),
                w_in=(DEPTH, N_DEV, D_MODEL, WIDTH))


def _mix_bwd(layer, dx_next, proj, y0, pooled, wg_glu, b_glu, pool_w, pool_scale, wg_a, wg_b, wg_out, prev,
             carry=None):
    n_k = N_DEV
    n_prev = 0 if prev is None else len(prev)

    def body(*refs):
        (dx_ref, p_ref, y0_ref, pooled_ref, wglu_ref, bglu_ref, pw_ref, scale_ref, wa_ref, wb_ref,
         wout_ref) = refs[:11]
        (dproj_ref, dy0_ref, dpooled_ref, gwout_ref, gwa_ref, gwb_ref, gwglu_ref, gpw_ref,
         gscale_ref, gbglu_ref) = refs[11 + n_prev:]

        @pl.when(pl.program_id(0) == 0)
        def _():
            for r in (gwout_ref, gwa_ref, gwb_ref, gwglu_ref, gpw_ref, gscale_ref, gbglu_ref):
                r[...] = jnp.zeros_like(r)

        f = _mix_forward(layer, p_ref, y0_ref, pooled_ref, wglu_ref, bglu_ref, pw_ref, scale_ref, wa_ref, wb_ref)
        wglu = wglu_ref[...].reshape(WIDTH, WIDTH)
        wout = wout_ref[...].reshape(D_MODEL, D_MODEL)
        blk = D_MODEL // n_k
        dxb = dx_ref[...].astype(BF16)
        dmerged = _dot_nt(dxb, wout)
        gwout = _dot_tn(f["merged"].astype(BF16), dxb)
        for k in range(n_k):
            gwout_ref[_slot(k)] += gwout[k * blk:(k + 1) * blk, :]
        dma = dmerged * f["sga"]
        dmb = dmerged * f["sgb"]
        dga = dmerged * f["ma"] * f["sga"] * (1.0 - f["sga"])
        dgb = dmerged * f["mb"] * f["sgb"] * (1.0 - f["sgb"])
        dmab = dma.astype(BF16)
        dmbb = dmb.astype(BF16)
        dya = _dot_nt(dmab, wa_ref[...])
        dyb = _dot_nt(dmbb, wb_ref[...])
        gwa = _dot_tn(f["yab"], dmab)
        gwb = _dot_tn(f["ybb"], dmbb)
        for k in range(n_k):
            gwa_ref[_slot(k)] += gwa[:, k * blk:(k + 1) * blk]
            gwb_ref[_slot(k)] += gwb[:, k * blk:(k + 1) * blk]
        zb, szb = f["zb"], f["szb"]
        dzb = dyb * f["ms"] * (szb * (1.0 + zb * (1.0 - szb)))
        dms = dyb * f["silu_zb"]
        gscale_ref[...] += jnp.sum(dms * f["mixed"], axis=0, keepdims=True)
        dmixed = (dms * f["scale"]).astype(BF16)
        pooled = f["pooled"]
        for g in range(len(POOL_WINDOWS)):
            cols = slice(g * POOL_GROUP, (g + 1) * POOL_GROUP)
            dpooled_ref[:, cols] = _dot_nt(dmixed[:, cols], pw_ref[g].astype(BF16))
            gpw_ref[g] += _dot_tn(pooled[:, cols].astype(BF16), dmixed[:, cols])
        za, sza = f["za"], f["sza"]
        dza = dya * f["y2"] * (sza * (1.0 + za * (1.0 - sza)))
        dy2 = dya * f["silu_za"]
        sq = f["sq"]
        dq = dy2 * f["y1"] * sq * (1.0 - sq)
        dqb = dq.astype(BF16)
        dy1 = dy2 * sq + _dot_nt(dqb, wglu)
        gwglu = _dot_tn(f["y1b"], dqb)
        rblk = WIDTH // n_k
        for k in range(n_k):
            gwglu_ref[_slot(k)] += gwglu[k * rblk:(k + 1) * rblk, :]
        gbglu_ref[...] += jnp.sum(dq, axis=0, keepdims=True)
        y0, t = f["y0"], f["t"]
        dgelu = 0.5 * (1.0 + t) + 0.5 * y0 * (1.0 - t * t) * (GELU_C * (1.0 + 3.0 * GELU_A * y0 * y0))
        dy0_ref[...] = dy1 * dgelu
        zeros = jnp.zeros((TILE_M, WIDTH), BF16)
        dproj_ref[:, 0:WIDTH] = zeros
        dproj_ref[:, WIDTH:2 * WIDTH] = dza.astype(BF16)
        dproj_ref[:, 2 * WIDTH:3 * WIDTH] = zeros
        dproj_ref[:, 3 * WIDTH:4 * WIDTH] = dzb.astype(BF16)
        dproj_ref[:, 4 * WIDTH:4 * WIDTH + D_MODEL] = dga.astype(BF16)
        dproj_ref[:, 4 * WIDTH + D_MODEL:] = dgb.astype(BF16)

    tile = lambda w: pl.BlockSpec((TILE_M, w), lambda i: (i, 0))
    shapes = _big_shapes()
    big = ["w_out", "w_branch_a", "w_branch_b", "ssm_w_glu"]
    slab = lambda n: pl.BlockSpec((None,) + shapes[n][1:], lambda i: (layer, 0, 0, 0))
    args = [dx_next, proj, y0, pooled, wg_glu, b_glu, pool_w, pool_scale, wg_a, wg_b, wg_out]
    return _pcall(
        body, name=f"mix_bwd_l{layer}",
        out_shape=(SDS((SEQ, N_IN), BF16), SDS((SEQ, WIDTH), F32), SDS((SEQ, WIDTH), F32))
        + tuple(SDS(shapes[n], F32) for n in big)
        + (SDS((4, POOL_GROUP, POOL_GROUP), F32), SDS((1, WIDTH), F32), SDS((1, WIDTH), F32)),
        grid=(SEQ // TILE_M,),
        in_specs=[tile(D_MODEL), tile(N_IN), tile(WIDTH), tile(WIDTH)] + _mix_weight_specs(layer) + [ANY] * n_prev,
        out_specs=(tile(N_IN), tile(WIDTH), tile(WIDTH)) + tuple(slab(n) for n in big)
        + (_const((4, POOL_GROUP, POOL_GROUP)), _const((1, WIDTH)), _const((1, WIDTH))),
        args=args + list(prev or ()),
        aliases={len(args) + i: 3 + i for i in range(n_prev)},
        sem=("arbitrary",), limit=VMEM_LIMIT_BIG, carry=carry)


def _pool_bwd(layer, dpooled, dproj):
    def body(dp_ref, _, o_ref):
        for gi, win in enumerate(POOL_WINDOWS):
            cols = slice(gi * POOL_GROUP, (gi + 1) * POOL_GROUP)
            dp = dp_ref[:, cols]
            t, count = _pool_counts(win)
            e = dp / count
            acc = e
            k = 1
            while k < win:
                acc = acc + jnp.where(t < SEQ - k, pltpu.roll(acc, SEQ - k, 0), 0.0)
                k *= 2
            o_ref[:, cols] = (acc - dp).astype(BF16)

    return pl.pallas_call(
        body, name=f"pool_bwd_l{layer}",
        out_shape=SDS((SEQ, N_IN), BF16),
        grid=(1,),
        in_specs=[pl.BlockSpec((SEQ, WIDTH), lambda i: (0, 0)), ANY],
        out_specs=pl.BlockSpec((SEQ, WIDTH), lambda i: (0, 2)),
        input_output_aliases={1: 0},
        compiler_params=_cp(("arbitrary",)),
    )(dpooled, dproj)


def _proj_wgrad(layer, x, norm_g, dproj, prev, carry=None):
    tm = 512
    n_prev = 0 if prev is None else 1

    def body(*refs):
        x_ref, g_ref, dp_ref = refs[:3]
        gw_ref, gb_ref, ht_ref = refs[3 + n_prev:]
        n, t = pl.program_id(0), pl.program_id(1)

        @pl.when(t == 0)
        def _():
            gw_ref[...] = jnp.zeros_like(gw_ref)
            gb_ref[...] = jnp.zeros_like(gb_ref)

        @pl.when(n == 0)
        def _():
            _, xn = _rms(x_ref[...])
            ht_ref[t] = (xn * g_ref[layer:layer + 1, :]).T.astype(BF16)

        dp = dp_ref[...]
        gw_ref[...] += _dot(ht_ref[t], dp)
        gb_ref[...] += jnp.sum(dp.astype(F32), axis=0, keepdims=True)

    return _pcall(
        body, name=f"proj_wgrad_l{layer}",
        out_shape=(SDS(_big_shapes()["w_in"], F32), SDS((1, N_IN), F32)),
        grid=(N_DEV, SEQ // tm),
        in_specs=[pl.BlockSpec((tm, D_MODEL), lambda n, t: (jnp.where(n == 0, t, 0), 0)),
                  _const((DEPTH, D_MODEL)),
                  pl.BlockSpec((tm, WIDTH), lambda n, t: (t, n))] + [ANY] * n_prev,
        out_specs=(pl.BlockSpec((None, None, D_MODEL, WIDTH), lambda n, t: (layer, _slot(n), 0, 0)),
                   pl.BlockSpec((1, WIDTH), lambda n, t: (0, n))),
        scratch_shapes=[pltpu.VMEM((SEQ // tm, D_MODEL, tm), BF16)],
        args=[x, norm_g, dproj] + ([prev] if n_prev else []),
        aliases={3: 0} if n_prev else {}, sem=("arbitrary", "arbitrary"), carry=carry)


def _proj_dgrad(layer, dx_next, x, norm_g, dproj, wg_in, carry=None):
    n_w = len(wg_in)

    def body(dxn_ref, x_ref, g_ref, dp_ref, *refs):
        w_refs, (dx_ref, gg_ref) = refs[:n_w], refs[n_w:]

        @pl.when(pl.program_id(0) == 0)
        def _():
            gg_ref[...] = jnp.zeros_like(gg_ref)

        parts = []
        for w_ref in w_refs:
            part = jnp.zeros((TILE_M, w_ref.shape[1]), F32)
            for k in range(N_DEV):
                part = part + _dot_nt(dp_ref[:, k * WIDTH:(k + 1) * WIDTH], w_ref[k])
            parts.append(part)
        dh = parts[0] if n_w == 1 else jnp.concatenate(parts, axis=1)
        rs, xn = _rms(x_ref[...])
        gg_ref[...] += jnp.sum(dh * xn, axis=0, keepdims=True)
        dxn = dh * g_ref[layer:layer + 1, :]
        dx_ref[...] = dxn_ref[...] + rs * (dxn - xn * jnp.mean(dxn * xn, axis=-1, keepdims=True))

    return _pcall(
        body, name=f"proj_dgrad_l{layer}",
        out_shape=(SDS((SEQ, D_MODEL), F32), SDS((1, D_MODEL), F32)),
        grid=(SEQ // TILE_M,),
        in_specs=[pl.BlockSpec((TILE_M, D_MODEL), lambda i: (i, 0)),
                  pl.BlockSpec((TILE_M, D_MODEL), lambda i: (i, 0)),
                  _const((DEPTH, D_MODEL)),
                  pl.BlockSpec((TILE_M, N_IN), lambda i: (i, 0))] + [_const(w.shape) for w in wg_in],
        out_specs=(pl.BlockSpec((TILE_M, D_MODEL), lambda i: (i, 0)), _const((1, D_MODEL))),
        args=[dx_next, x, norm_g, dproj, *wg_in], sem=("arbitrary",), carry=carry)


def _my_place():
    return lax.axis_index("x"), lax.axis_index("y"), lax.axis_index("c")


def _gather_plan(shards, layer, by_columns=(), rows_of=None):
    n = len(shards)

    def parts(ins, outs, sems):
        send_sems, recv_sems, local_sems = sems
        x, y, c = _my_place()
        chips = [(1 - x, y), (x, 1 - y), (1 - x, 1 - y)]

        def source(t):
            return ins[t].at[layer] if rows_of is None else ins[t].at[layer, pl.ds(*rows_of)]

        def rows(t, place):
            px, py, pc = place
            index = 4 * px + 2 * py + pc
            if t in by_columns:
                width = shards[t].shape[2]
                return outs[t].at[:, pl.ds(pl.multiple_of(index * width, 128), width)]
            return outs[t].at[index]

        def copy(t, k, block, to, from_src=False):
            return pltpu.make_async_remote_copy(
                src_ref=source(t) if from_src else rows(t, block), dst_ref=rows(t, block),
                send_sem=send_sems.at[7 * t + k], recv_sem=recv_sems.at[7 * t + k], device_id=to,
                device_id_type=MESH)

        def mine(t):
            return pltpu.make_async_copy(source(t), rows(t, (x, y, c)), local_sems.at[t])

        return (x, y, c), chips, copy, mine

    def start(ins, outs, sems):
        me, chips, copy, mine = parts(ins, outs, sems)
        x, y, c = me
        for t in range(n):
            mine(t).start()
            copy(t, 0, me, (x, y, 1 - c), from_src=True).start()
            for j, chip in enumerate(chips):
                copy(t, 1 + j, me, (*chip, c), from_src=True).start()

    def relay(ins, outs, sems):
        me, chips, copy, mine = parts(ins, outs, sems)
        x, y, c = me
        for t in range(n):
            for j, chip in enumerate(chips):
                copy(t, 1 + j, (*chip, c), me).wait_recv()
                copy(t, 4 + j, (*chip, c), (x, y, 1 - c)).start()

    def finish(ins, outs, sems):
        me, chips, copy, mine = parts(ins, outs, sems)
        x, y, c = me
        sibling = (x, y, 1 - c)
        for t in range(n):
            copy(t, 0, sibling, me).wait_recv()
            for j, chip in enumerate(chips):
                copy(t, 4 + j, (*chip, 1 - c), me).wait_recv()
            for k in range(7):
                copy(t, k, me, sibling, from_src=k < 4).wait_send()
            mine(t).wait()

    n_rows = lambda a: a.shape[1] if rows_of is None else rows_of[1]
    out_shape = [SDS((a.shape[1], N_DEV * a.shape[2]) if t in by_columns else (N_DEV, n_rows(a), a.shape[2]), a.dtype)
                 for t, a in enumerate(shards)]
    sems = [pltpu.SemaphoreType.DMA((7 * n,)), pltpu.SemaphoreType.DMA((7 * n,)), pltpu.SemaphoreType.DMA((n,))]
    return _Carried(shards, out_shape, sems, start, finish, relay)


class _Carried:
    def __init__(self, ins, out_shape, sems, start, finish, relay=None):
        self.ins, self.out_shape, self.sems = list(ins), list(out_shape), list(sems)
        self.start, self.finish = start, finish
        self.relay = relay or (lambda ins, outs, sems: None)


def _pcall(body, *, name, grid, in_specs, out_specs, out_shape, args, scratch_shapes=(), aliases=None,
           sem=None, limit=VMEM_LIMIT, carry=None):
    out_shape, out_specs, scratch_shapes = list(out_shape), list(out_specs), list(scratch_shapes)
    n_in, n_out, n_scr = len(args), len(out_shape), len(scratch_shapes)
    if carry is None:
        kern, c_ins, c_out, c_sems = body, [], [], []
    else:
        c_ins, c_out, c_sems = carry.ins, carry.out_shape, carry.sems
        ci, co = len(c_ins), len(c_out)
        steps = tuple(grid)

        def kern(*refs):
            o0 = n_in + ci
            s0 = o0 + n_out + co
            mine = refs[:n_in] + refs[o0:o0 + n_out] + refs[s0:s0 + n_scr]
            theirs = (refs[n_in:o0], refs[o0 + n_out:s0], refs[s0 + n_scr:])
            first = pl.program_id(0) == 0
            last = pl.program_id(0) == steps[0] - 1
            for a in range(1, len(steps)):
                first = jnp.logical_and(first, pl.program_id(a) == 0)
                last = jnp.logical_and(last, pl.program_id(a) == steps[a] - 1)

            @pl.when(first)
            def _():
                carry.start(*theirs)

            @pl.when(last)
            def _():
                carry.relay(*theirs)

            body(*mine)

            @pl.when(last)
            def _():
                carry.finish(*theirs)

        sem = ("arbitrary",) * len(steps)
    res = pl.pallas_call(
        kern, name=name, grid=tuple(grid),
        in_specs=list(in_specs) + [ANY] * len(c_ins),
        out_specs=tuple(out_specs + [ANY] * len(c_out)),
        out_shape=tuple(out_shape + c_out),
        scratch_shapes=scratch_shapes + c_sems,
        input_output_aliases=aliases or {},
        compiler_params=_cp(sem, limit),
    )(*args, *c_ins)
    return res[:n_out], res[n_out:]


def _run_carried(name, carry):
    ci, co = len(carry.ins), len(carry.out_shape)

    def body(*refs):
        parts = (refs[:ci], refs[ci:ci + co], refs[ci + co:])
        carry.start(*parts)
        carry.relay(*parts)
        carry.finish(*parts)

    return pl.pallas_call(
        body, name=name, out_shape=tuple(carry.out_shape),
        in_specs=[ANY] * ci, out_specs=tuple([ANY] * co), scratch_shapes=carry.sems,
    )(*carry.ins)


def _sibling_plan(big, small):
    n = len(big)
    n_copies = 4 * n + len(small)

    def copies(ins, outs, sems):
        send_sems, recv_sems = sems
        x, y, c = _my_place()
        pairs = []
        for t, (_, layer) in enumerate(big):
            for s in range(4):
                pairs.append((ins[t].at[layer, pl.ds(4 * (1 - c) + s, 1)], outs[t].at[pl.ds(s, 1)]))
        pairs += list(zip(ins[n:], outs[n:]))
        return [pltpu.make_async_remote_copy(
            src_ref=src, dst_ref=dst, send_sem=send_sems.at[k], recv_sem=recv_sems.at[k],
            device_id=(x, y, 1 - c), device_id_type=MESH) for k, (src, dst) in enumerate(pairs)]

    def start(ins, outs, sems):
        for cp in copies(ins, outs, sems):
            cp.start()

    def finish(ins, outs, sems):
        for cp in copies(ins, outs, sems):
            cp.wait()

    out_shape = [SDS((4,) + a.shape[2:], a.dtype) for a, _ in big] + [SDS(a.shape, a.dtype) for a in small]
    sems = [pltpu.SemaphoreType.DMA((n_copies,)), pltpu.SemaphoreType.DMA((n_copies,))]
    return _Carried([a for a, _ in big] + list(small), out_shape, sems, start, finish)


def _chips_plan(big, small):
    n, n_small = len(big), len(small)
    max_rows = 512
    parts = [max(1, a.shape[1] // max_rows) for a in big]
    n_copies = 3 * (sum(parts) + n_small)

    def copies(ins, outs, sems, landing):
        send_sems, recv_sems, local_sems = sems
        x, y, c = _my_place()
        my_chip = 2 * x + y
        chips = [(1 - x, y), (x, 1 - y), (1 - x, 1 - y)]
        remote, local = [], []
        for chip in chips:
            to = 2 * chip[0] + chip[1]
            slot = to if landing else my_chip
            pairs = []
            for t in range(n):
                rows_per = big[t].shape[1] // parts[t]
                for p in range(parts[t]):
                    rows = pl.ds(p * rows_per, rows_per)
                    pairs.append((ins[t].at[to, rows], outs[t].at[slot, rows]))
            pairs += [(ins[t], outs[t].at[slot]) for t in range(n, n + n_small)]
            for src, dst in pairs:
                k = len(remote)
                remote.append(pltpu.make_async_remote_copy(
                    src_ref=src, dst_ref=dst, send_sem=send_sems.at[k], recv_sem=recv_sems.at[k],
                    device_id=(*chip, c), device_id_type=MESH))
        for t in range(n):
            local.append(pltpu.make_async_copy(ins[t].at[my_chip], outs[t].at[my_chip], local_sems.at[t]))
        for t in range(n, n + n_small):
            local.append(pltpu.make_async_copy(ins[t], outs[t].at[my_chip], local_sems.at[t]))
        return remote + local

    def start(ins, outs, sems):
        for cp in copies(ins, outs, sems, landing=False):
            cp.start()

    def finish(ins, outs, sems):
        for cp in copies(ins, outs, sems, landing=True):
            cp.wait()

    out_shape = [SDS(a.shape, a.dtype) for a in big] + [SDS((N_CHIP,) + a.shape, a.dtype) for a in small]
    sems = [pltpu.SemaphoreType.DMA((n_copies,)), pltpu.SemaphoreType.DMA((n_copies,)),
            pltpu.SemaphoreType.DMA((n + n_small,))]
    return _Carried(list(big) + list(small), out_shape, sems, start, finish)


def _all_plan(small):
    n = len(small)
    masks = [(m >> 2 & 1, m >> 1 & 1, m & 1) for m in range(1, N_DEV)]

    def copies(ins, outs, sems, landing):
        send_sems, recv_sems, local_sems = sems
        x, y, c = _my_place()
        me = 4 * x + 2 * y + c
        flip = lambda v, bit: 1 - v if bit else v
        remote = []
        for fx, fy, fc in masks:
            peer = (flip(x, fx), flip(y, fy), flip(c, fc))
            slot = 4 * peer[0] + 2 * peer[1] + peer[2] if landing else me
            for t in range(n):
                k = len(remote)
                remote.append(pltpu.make_async_remote_copy(
                    src_ref=ins[t], dst_ref=outs[t].at[slot], send_sem=send_sems.at[k], recv_sem=recv_sems.at[k],
                    device_id=peer, device_id_type=MESH))
        local = [pltpu.make_async_copy(ins[t], outs[t].at[me], local_sems.at[t]) for t in range(n)]
        return remote + local

    def start(ins, outs, sems):
        for cp in copies(ins, outs, sems, landing=False):
            cp.start()

    def finish(ins, outs, sems):
        for cp in copies(ins, outs, sems, landing=True):
            cp.wait()

    out_shape = [SDS((N_DEV,) + a.shape, a.dtype) for a in small]
    sems = [pltpu.SemaphoreType.DMA((7 * n,)), pltpu.SemaphoreType.DMA((7 * n,)), pltpu.SemaphoreType.DMA((n,))]
    return _Carried(list(small), out_shape, sems, start, finish)


def _join(*plans):
    plans = [p for p in plans if p is not None]
    if len(plans) <= 1:
        return plans[0] if plans else None

    def each(fn_name, ins, outs, sems):
        i = o = s = 0
        for p in plans:
            ni, no, ns = len(p.ins), len(p.out_shape), len(p.sems)
            getattr(p, fn_name)(ins[i:i + ni], outs[o:o + no], sems[s:s + ns])
            i, o, s = i + ni, o + no, s + ns

    return _Carried(sum((p.ins for p in plans), []), sum((p.out_shape for p in plans), []),
                    sum((p.sems for p in plans), []),
                    lambda i, o, s: each("start", i, o, s), lambda i, o, s: each("finish", i, o, s),
                    lambda i, o, s: each("relay", i, o, s))


def _row_block(rows, most=256):
    return min(rows, most)


def _add_own(tag, core, g, layer, got):
    _, r, c = got.shape
    rb = _row_block(r, most=1024)

    def body(core_ref, a_ref, b_ref, o_ref):
        o_ref[...] = (a_ref[...] + b_ref[...]).astype(o_ref.dtype)

    return pl.pallas_call(
        body, name=f"add_{tag}", out_shape=SDS(got.shape, BF16),
        grid_spec=pltpu.PrefetchScalarGridSpec(
            num_scalar_prefetch=1, grid=(4, r // rb),
            in_specs=[pl.BlockSpec((None, None, rb, c), lambda s, j, core: (layer, 4 * core[0] + s, j, 0)),
                      pl.BlockSpec((None, rb, c), lambda s, j, core: (s, j, 0))],
            out_specs=pl.BlockSpec((None, rb, c), lambda s, j, core: (s, j, 0))),
        compiler_params=_cp(("parallel", "parallel")),
    )(core, g, got)


def _add_lists(tag, own, got, grid=None, specs=None, dtype=F32):
    n = len(own)

    def body(*refs):
        for a, b, o in zip(refs[:n], refs[n:2 * n], refs[2 * n:]):
            o[...] = (a[...] + b[...]).astype(o.dtype)

    kw = {}
    if grid is not None:
        kw = dict(grid=grid, in_specs=list(specs) * 2, out_specs=tuple(specs),
                  compiler_params=_cp(("parallel",) * len(grid)))
    return pl.pallas_call(
        body, name=f"add_{tag}", out_shape=tuple(SDS(a.shape, dtype) for a in own), **kw)(*own, *got)


def _adamw_math(w, g, m, v):
    m = ADAM_B1 * m + (1.0 - ADAM_B1) * g
    v = ADAM_B2 * v + (1.0 - ADAM_B2) * (g * g)
    m_hat = m / (1.0 - ADAM_B1 ** ADAM_STEP)
    v_hat = v / (1.0 - ADAM_B2 ** ADAM_STEP)
    delta = -ADAM_LR * (m_hat / (jnp.sqrt(v_hat) + ADAM_EPS) + ADAM_WD * w)
    return delta, m, v


def _sum_slots_adamw(tag, slots, w, m, v):
    _, r, c = slots[0].shape
    rb = _row_block(r)

    def body(s0_ref, s1_ref, w_ref, m_ref, v_ref, g_ref, d_ref, nm_ref, nv_ref):
        first = pl.program_id(1) == 0
        g = _sum4([jnp.where(first, s0_ref[k], s1_ref[k]).astype(F32) for k in range(N_CHIP)])
        delta, nm, nv = _adamw_math(w_ref[...], g, m_ref[...], v_ref[...])
        g_ref[...] = g
        d_ref[...] = delta
        nm_ref[...] = nm
        nv_ref[...] = nv

    spec = pl.BlockSpec((None, rb, c), lambda j, l: (l, j, 0))
    sspec = pl.BlockSpec((N_CHIP, rb, c), lambda j, l: (0, j, 0))
    s = SDS((DEPTH, r, c), F32)
    return pl.pallas_call(
        body, name=f"adamw_{tag}", out_shape=(s, s, s, s),
        grid=(r // rb, DEPTH), in_specs=[sspec, sspec, spec, spec, spec], out_specs=(spec, spec, spec, spec),
        compiler_params=_cp(("parallel", "arbitrary")),
    )(*slots, w, m, v)


def _adamw_small(tag, entries, grid=None):
    flat_in, in_specs, out_shape, out_specs, layout = [], [], [], [], []
    for slots, w, m, v, slot_spec, w_spec in entries:
        per_layer = isinstance(slots, (list, tuple))
        n_slot = len(slots) if per_layer else 1
        flat_in += (list(slots) if per_layer else [slots]) + [w, m, v]
        in_specs += [slot_spec] * n_slot + [w_spec] * 3
        out_shape += [SDS(w.shape, F32)] * 4
        out_specs += [w_spec] * 4
        layout.append((per_layer, n_slot))
    n_in = len(flat_in)

    def body(*refs):
        i, o = 0, n_in
        for per_layer, n_slot in layout:
            s_refs = refs[i:i + n_slot]
            w_ref, m_ref, v_ref = refs[i + n_slot:i + n_slot + 3]
            outs = refs[o:o + 4]
            if per_layer:
                for l, s_ref in enumerate(s_refs):
                    at = (slice(l, l + 1),) if len(w_ref.shape) == 2 else (l,)
                    g = _sum_slots(s_ref)
                    res = (g,) + _adamw_math(w_ref[at], g, m_ref[at], v_ref[at])
                    for o_ref, val in zip(outs, res):
                        o_ref[at] = val
            else:
                g = _sum_slots(s_refs[0])
                res = (g,) + _adamw_math(w_ref[...], g, m_ref[...], v_ref[...])
                for o_ref, val in zip(outs, res):
                    o_ref[...] = val
            i += n_slot + 3
            o += 4

    kw = {}
    if grid is not None:
        kw = dict(grid=grid, in_specs=in_specs, out_specs=tuple(out_specs),
                  compiler_params=_cp(("parallel",) * len(grid)))
    res = pl.pallas_call(body, name=f"adamw_{tag}", out_shape=tuple(out_shape), **kw)(*flat_in)
    return [tuple(res[4 * e:4 * e + 4]) for e in range(len(entries))]


def kernel(x, norm_g, w_in, b_in, ssm_log_dt, ssm_lam_re, ssm_lam_im, ssm_b_re, ssm_b_im, ssm_c_re, ssm_c_im, ssm_d, ssm_w_glu, ssm_b_glu, pool_w, pool_scale, w_branch_a, w_branch_b, w_out, final_norm_g, loss_target, m_norm_g, m_w_in, m_b_in, m_ssm_log_dt, m_ssm_lam_re, m_ssm_lam_im, m_ssm_b_re, m_ssm_b_im, m_ssm_c_re, m_ssm_c_im, m_ssm_d, m_ssm_w_glu, m_ssm_b_glu, m_pool_w, m_pool_scale, m_w_branch_a, m_w_branch_b, m_w_out, m_final_norm_g, v_norm_g, v_w_in, v_b_in, v_ssm_log_dt, v_ssm_lam_re, v_ssm_lam_im, v_ssm_b_re, v_ssm_b_im, v_ssm_c_re, v_ssm_c_im, v_ssm_d, v_ssm_w_glu, v_ssm_b_glu, v_pool_w, v_pool_scale, v_w_branch_a, v_w_branch_b, v_w_out, v_final_norm_g):
    weights = dict(norm_g=norm_g, w_in=w_in, b_in=b_in, ssm_log_dt=ssm_log_dt, ssm_lam_re=ssm_lam_re,
                   ssm_lam_im=ssm_lam_im, ssm_b_re=ssm_b_re, ssm_b_im=ssm_b_im, ssm_c_re=ssm_c_re,
                   ssm_c_im=ssm_c_im, ssm_d=ssm_d, ssm_w_glu=ssm_w_glu, ssm_b_glu=ssm_b_glu, pool_w=pool_w,
                   pool_scale=pool_scale, w_branch_a=w_branch_a, w_branch_b=w_branch_b, w_out=w_out,
                   final_norm_g=final_norm_g.reshape(1, D_MODEL))
    mom_m = dict(norm_g=m_norm_g, w_in=m_w_in, b_in=m_b_in, ssm_log_dt=m_ssm_log_dt, ssm_lam_re=m_ssm_lam_re,
                 ssm_lam_im=m_ssm_lam_im, ssm_b_re=m_ssm_b_re, ssm_b_im=m_ssm_b_im, ssm_c_re=m_ssm_c_re,
                 ssm_c_im=m_ssm_c_im, ssm_d=m_ssm_d, ssm_w_glu=m_ssm_w_glu, ssm_b_glu=m_ssm_b_glu,
                 pool_w=m_pool_w, pool_scale=m_pool_scale, w_branch_a=m_w_branch_a, w_branch_b=m_w_branch_b,
                 w_out=m_w_out, final_norm_g=m_final_norm_g.reshape(1, D_MODEL))
    mom_v = dict(norm_g=v_norm_g, w_in=v_w_in, b_in=v_b_in, ssm_log_dt=v_ssm_log_dt, ssm_lam_re=v_ssm_lam_re,
                 ssm_lam_im=v_ssm_lam_im, ssm_b_re=v_ssm_b_re, ssm_b_im=v_ssm_b_im, ssm_c_re=v_ssm_c_re,
                 ssm_c_im=v_ssm_c_im, ssm_d=v_ssm_d, ssm_w_glu=v_ssm_w_glu, ssm_b_glu=v_ssm_b_glu,
                 pool_w=v_pool_w, pool_scale=v_pool_scale, w_branch_a=v_w_branch_a, w_branch_b=v_w_branch_b,
                 w_out=v_w_out, final_norm_g=v_final_norm_g.reshape(1, D_MODEL))
    order = ["norm_g", "w_in", "b_in", "ssm_log_dt", "ssm_lam_re", "ssm_lam_im", "ssm_b_re", "ssm_b_im",
             "ssm_c_re", "ssm_c_im", "ssm_d", "ssm_w_glu", "ssm_b_glu", "pool_w", "pool_scale", "w_branch_a",
             "w_branch_b", "w_out", "final_norm_g"]
    big_names = ["w_in", "ssm_w_glu", "w_branch_a", "w_branch_b", "w_out"]

    log_dt3 = ssm_log_dt.reshape(DEPTH, N_GROUP, 1)
    b_t = lambda a: a.transpose(0, 1, 3, 2)
    for d in (weights, mom_m, mom_v):
        d["ssm_b_re"], d["ssm_b_im"] = b_t(d["ssm_b_re"]), b_t(d["ssm_b_im"])
    bt_re, bt_im = weights["ssm_b_re"], weights["ssm_b_im"]
    abar_re, abar_im, bbt_re, bbt_im = _s5_params(log_dt3, ssm_lam_re, ssm_lam_im, bt_re, bt_im)
    s5_args = (bbt_re, bbt_im, ssm_c_re, ssm_c_im, abar_re, abar_im, ssm_d)

    w16 = {n: weights[n].astype(BF16) for n in big_names}
    rest = [w16[n] for n in big_names[1:]]
    half = D_MODEL // 2
    wg_in = [None, [None, None]]
    wg_rest = [None, None]
    wg_in[0] = list(_run_carried("gather_w_in_l0", _gather_plan([w16["w_in"]], 0)))
    xs = [x.reshape(SEQ, D_MODEL)]
    saved = []
    for l in range(DEPTH):
        proj, moved = _norm_proj(l, xs[l], norm_g, wg_in[l], b_in,
                                 carry=_gather_plan([w16["w_in"]], 1, rows_of=(0, half)) if l == 0 else None)
        if l == 0:
            (wg_in[1][0],) = moved
        (states, y0), wg_rest[l] = _s5_scan_fwd(l, proj, *s5_args, carry=_gather_plan(rest, l, by_columns=(1, 2)))
        pooled = _pool_fwd(l, proj)
        wg_glu, wg_a, wg_b, wg_out = wg_rest[l]
        x_next, moved = _mix_fwd(l, xs[l], proj, y0, pooled, wg_glu, ssm_b_glu, pool_w, pool_scale, wg_a, wg_b,
                                 wg_out, carry=_gather_plan([w16["w_in"]], 1, rows_of=(half, half)) if l == 0 else None)
        if l == 0:
            (wg_in[1][1],) = moved
        xs.append(x_next)
        saved.append((proj, states, y0, pooled))

    dx, loss_part, g_final = _loss_head(xs[DEPTH], loss_target.reshape(SEQ, D_MODEL), weights["final_norm_g"])
    loss = lax.psum(loss_part[0, 0], ("x", "y", "c"))

    core = lax.axis_index("c").astype(jnp.int32).reshape(1)
    vec_names = ["norm_g", "b_in", "ssm_d", "ssm_b_glu", "pool_scale", "ssm_log_dt"]
    s5_names = ["ssm_log_dt", "ssm_lam_re", "ssm_lam_im", "ssm_b_re", "ssm_b_im"]
    mat_names = ["pool_w", "ssm_c_re", "ssm_c_im", "ssm_b_re", "ssm_b_im"]
    lane_sparse = ("ssm_c_re", "ssm_c_im", "ssm_b_re", "ssm_b_im")

    def dense(key, a):
        return a.reshape(-1, 128) if key[0] in lane_sparse else a

    def undense(key, slots):
        return slots.reshape((N_CHIP, N_GROUP, GROUP_W, STATE)) if key[0] in lane_sparse else slots

    def add_small(tag, keys, own, got):
        out = [None] * len(keys)
        whole = [i for i, k in enumerate(keys) if k[0] not in mat_names]
        tiled = [i for i, k in enumerate(keys) if k[0] in mat_names]
        if whole:
            for i, r in zip(whole, _add_lists(f"{tag}_a", [own[i] for i in whole], [got[i] for i in whole])):
                out[i] = r
        if tiled:
            specs = [pl.BlockSpec((1, POOL_GROUP, POOL_GROUP), lambda j: (j, 0, 0)) if keys[i][0] == "pool_w"
                     else pl.BlockSpec((own[i].shape[0] // N_CHUNK, 128), lambda j: (j, 0)) for i in tiled]
            for i, r in zip(tiled, _add_lists(f"{tag}_b", [own[i] for i in tiled], [got[i] for i in tiled],
                                              grid=(N_CHUNK,), specs=specs, dtype=BF16)):
                out[i] = r
        return out

    sm = {("final_norm_g", None): g_final}
    slots = {}
    grads = dict.fromkeys(big_names)

    class Wave:
        def __init__(self, tag, layer, big, keys):
            self.tag, self.layer, self.big, self.keys = tag, layer, big, keys

        def to_sibling(self):
            self.own = [dense(k, sm[k]) for k in self.keys]
            return _sibling_plan([(grads[n], self.layer) for n in self.big], self.own)

        def add(self, moved):
            nb = len(self.big)
            self.chip_big = [_add_own(f"{self.tag}_{n}", core, grads[n], self.layer, b)
                             for n, b in zip(self.big, moved[:nb])]
            self.chip_small = add_small(self.tag, self.keys, self.own, moved[nb:])

        def to_chips(self, big=None, small=True):
            self.sent = list(self.big if big is None else big), small
            return _chips_plan([self.chip_big[self.big.index(n)] for n in self.sent[0]],
                               self.chip_small if small else [])

        def landed(self, moved):
            names, small = self.sent
            for n, s in zip(names, moved[:len(names)]):
                slots[(n, self.layer)] = s
            if small:
                for k, s in zip(self.keys, moved[len(names):]):
                    slots[k] = undense(k, s)
            return moved[len(names) + (len(self.keys) if small else 0):]

    def s5_param_grads(l, g_abar_re, g_abar_im, g_bbt_re, g_bbt_im):
        g = _s5_params_bwd(l, log_dt3, ssm_lam_re, ssm_lam_im, bt_re, bt_im, g_abar_re, g_abar_im, g_bbt_re, g_bbt_im)
        sm[("ssm_log_dt", l)] = g[0].reshape(1, N_GROUP)
        for n, a in zip(s5_names[1:], g[1:]):
            sm[(n, l)] = a

    small1 = ["b_in", "ssm_d", "ssm_b_glu", "pool_scale", "pool_w", "ssm_c_re", "ssm_c_im"] + s5_names
    w1 = Wave("chip1", 1, list(big_names), [(n, 1) for n in small1] + [("final_norm_g", None)])
    early = Wave("chip0e", 0, big_names[1:], [("pool_w", 0), ("pool_scale", 0), ("ssm_b_glu", 0)])
    mid = Wave("chip0m", 0, [], [(n, 0) for n in ["ssm_c_re", "ssm_c_im", "ssm_d"] + s5_names] + [("norm_g", 1)])
    late = Wave("chip0l", 0, ["w_in"], [("b_in", 0)])

    mix_prev, gw_in = None, None
    for l in reversed(range(DEPTH)):
        proj, states, y0, pooled = saved[l]
        wg_glu, wg_a, wg_b, wg_out = wg_rest[l]
        res, moved = _mix_bwd(l, dx, proj, y0, pooled, wg_glu, ssm_b_glu, pool_w, pool_scale, wg_a, wg_b, wg_out,
                              mix_prev, carry=None if l == 1 else w1.to_chips(big=["w_in"], small=False))
        if l == 0:
            w1.landed(moved)
        dproj, dy0, dpooled = res[:3]
        mix_prev = list(res[3:7])
        grads["w_out"], grads["w_branch_a"], grads["w_branch_b"], grads["ssm_w_glu"] = mix_prev
        sm[("pool_w", l)], sm[("pool_scale", l)], sm[("ssm_b_glu", l)] = res[7:]
        dproj = _pool_bwd(l, dpooled, dproj)
        carry = None if l == 1 else _join(w1.to_chips(big=big_names[1:]), early.to_sibling())
        res, moved = _s5_scan_bwd(l, dy0, proj, states, *s5_args, dproj, carry=carry)
        if l == 0:
            early.add(w1.landed(moved))
        dproj, g_bbt_re, g_bbt_im, sm[("ssm_c_re", l)], sm[("ssm_c_im", l)], g_abar_re, g_abar_im, sm[("ssm_d", l)] = res
        s5_param_grads(l, g_abar_re, g_abar_im, g_bbt_re, g_bbt_im)
        carry = None if l == 1 else _join(early.to_chips(), mid.to_sibling())
        (gw_in, sm[("b_in", l)]), moved = _proj_wgrad(l, xs[l], norm_g, dproj, gw_in, carry=carry)
        grads["w_in"] = gw_in
        if l == 0:
            mid.add(early.landed(moved))
        carry = w1.to_sibling() if l == 1 else _join(mid.to_chips(), late.to_sibling())
        (dx, sm[("norm_g", l)]), moved = _proj_dgrad(l, dx, xs[l], norm_g, dproj, wg_in[l], carry=carry)
        if l == 1:
            w1.add(moved)
        else:
            late.add(mid.landed(moved))
    grad_x = dx.reshape(1, SEQ, D_MODEL)
    moved = late.landed(_run_carried("exchange_last", _join(late.to_chips(), _all_plan([sm[("norm_g", 0)]]))))
    slots[("norm_g", 0)] = moved[0]

    res = {}
    for n in big_names:
        res[n] = _sum_slots_adamw(n, [slots[(n, l)] for l in range(DEPTH)], weights[n], mom_m[n], mom_v[n])
    per_layer = lambda n: [slots[(n, l)] for l in range(DEPTH)]
    names_a = vec_names + ["ssm_lam_re", "ssm_lam_im"]
    entries_a = [(per_layer(n), weights[n], mom_m[n], mom_v[n], None, None) for n in names_a]
    n = "final_norm_g"
    entries_a.append((slots[(n, None)], weights[n], mom_m[n], mom_v[n], None, None))
    out_a = _adamw_small("small_a", entries_a)
    for n, r in zip(names_a + ["final_norm_g"], out_a):
        res[n] = r
    res["final_norm_g"] = tuple(a.reshape(D_MODEL) for a in res["final_norm_g"])
    pw_s = pl.BlockSpec((N_CHIP, 1, POOL_GROUP, POOL_GROUP), lambda j: (0, j, 0, 0))
    pw_w = pl.BlockSpec((DEPTH, 1, POOL_GROUP, POOL_GROUP), lambda j: (0, j, 0, 0))
    c_s = pl.BlockSpec((N_CHIP, CH_G, GROUP_W, STATE), lambda j: (0, j, 0, 0))
    c_w = pl.BlockSpec((DEPTH, CH_G, GROUP_W, STATE), lambda j: (0, j, 0, 0))
    entries_b = [(per_layer(n), weights[n], mom_m[n], mom_v[n], pw_s if n == "pool_w" else c_s,
                  pw_w if n == "pool_w" else c_w) for n in mat_names]
    out_b = _adamw_small("small_b", entries_b, grid=(N_CHUNK,))
    for n, r in zip(mat_names, out_b):
        res[n] = tuple(b_t(a) for a in r) if n in ("ssm_b_re", "ssm_b_im") else r

    outs = [loss, grad_x]
    for i in range(4):
        outs += [res[n][i] for n in order]
    return tuple(outs)
```

```python
import math

import jax
import jax.numpy as jnp
from jax import lax
from jax.experimental import pallas as pl
from jax.experimental.pallas import tpu as pltpu

F32 = jnp.float32
BF16 = jnp.bfloat16

SEQ = 2048
D_MODEL = 1024
N_IN = 4096
WIDTH = 512
N_GROUP = 32
GROUP_W = 16
STATE = 64
N_STATE = N_GROUP * STATE
N_CHUNK = 4
CH_G = N_GROUP // N_CHUNK
CH_W = WIDTH // N_CHUNK
CH_S = N_STATE // N_CHUNK
N_DEV = 8
N_CHIP = 4
POOL_WINDOWS = (2, 4, 8, 16)
POOL_GROUP = 128
EPS = 1e-6
DEPTH = 2

ADAM_LR = 0.001
ADAM_B1 = 0.9
ADAM_B2 = 0.999
ADAM_EPS = 1e-08
ADAM_WD = 0.01
ADAM_STEP = 10

TILE_M = 256
ROW_BLK = 512
VMEM_LIMIT = 48 * 1024 * 1024
VMEM_LIMIT_BIG = 60 * 1024 * 1024
MESH = pl.DeviceIdType.MESH
ANY = pl.BlockSpec(memory_space=pl.ANY)

GELU_C = math.sqrt(2.0 / math.pi)
GELU_A = 0.044715

SDS = jax.ShapeDtypeStruct


def _cp(sem=None, limit=VMEM_LIMIT):
    return pltpu.CompilerParams(dimension_semantics=sem, vmem_limit_bytes=limit)


def _dot(a, b):
    return jnp.dot(a, b, preferred_element_type=F32)


def _dot_nt(a, b):
    return lax.dot_general(a, b, (((1,), (1,)), ((), ())), preferred_element_type=F32)


def _dot_tn(a, b):
    return lax.dot_general(a, b, (((0,), (0,)), ((), ())), preferred_element_type=F32)


def _sig(x):
    return jax.nn.sigmoid(x)


def _rms(x):
    rs = lax.rsqrt(jnp.mean(x * x, axis=-1, keepdims=True) + EPS)
    return rs, x * rs


def _slot(n):
    return 4 * (n % 2) + n // 2


def _const(shape):
    n = len(shape)
    return pl.BlockSpec(shape, lambda *_: (0,) * n)


def _sum4(p):
    return (p[0] + p[1]) + (p[2] + p[3])


def _sum_slots(s_ref):
    vals = [s_ref[k].astype(F32) for k in range(s_ref.shape[0])]
    while len(vals) > 1:
        vals = [vals[i] + vals[i + 1] for i in range(0, len(vals), 2)]
    return vals[0]


def _s5_param_fn(log_dt, lam_re, lam_im, bt_re, bt_im):
    dt = jnp.exp(log_dt)
    mag = jnp.exp(lam_re * dt)
    ang = lam_im * dt
    abar_re = mag * jnp.cos(ang)
    abar_im = mag * jnp.sin(ang)
    num_re = abar_re - 1.0
    num_im = abar_im
    den = lam_re * lam_re + lam_im * lam_im
    coef_re = (num_re * lam_re + num_im * lam_im) / den
    coef_im = (num_im * lam_re - num_re * lam_im) / den
    bbar_re = coef_re[..., None, :] * bt_re - coef_im[..., None, :] * bt_im
    bbar_im = coef_re[..., None, :] * bt_im + coef_im[..., None, :] * bt_re
    return abar_re, abar_im, bbar_re, bbar_im


def _s5_params(log_dt, lam_re, lam_im, bt_re, bt_im):
    def body(ld, lr, li, br, bi, o_ar, o_ai, o_br, o_bi):
        ar, ai, bbr, bbi = _s5_param_fn(ld[...], lr[...], li[...], br[...], bi[...])
        o_ar[...] = ar
        o_ai[...] = ai
        o_br[...] = bbr
        o_bi[...] = bbi

    return pl.pallas_call(
        body, name="s5_params",
        out_shape=(SDS(lam_re.shape, F32), SDS(lam_re.shape, F32), SDS(bt_re.shape, F32), SDS(bt_re.shape, F32)),
    )(log_dt, lam_re, lam_im, bt_re, bt_im)


def _s5_params_bwd(layer, log_dt, lam_re, lam_im, bt_re, bt_im, g_ar, g_ai, g_br, g_bi):
    def body(ld, lr, li, br, bi, car, cai, cbr, cbi, o_ld, o_lr, o_li, o_br, o_bi):
        _, vjp = jax.vjp(_s5_param_fn, ld[...], lr[...], li[...], br[...], bi[...])
        d_ld, d_lr, d_li, d_br, d_bi = vjp((car[...], cai[...], cbr[...], cbi[...]))
        o_ld[...] = d_ld
        o_lr[...] = d_lr
        o_li[...] = d_li
        o_br[...] = d_br
        o_bi[...] = d_bi

    one = lambda shape: pl.BlockSpec((None,) + shape, lambda i: (layer,) + (0,) * len(shape))
    whole = lambda shape: _const(shape)
    vec, lam, mat = (N_GROUP, 1), (N_GROUP, STATE), (N_GROUP, GROUP_W, STATE)
    return pl.pallas_call(
        body, name=f"s5_params_bwd_l{layer}", grid=(1,),
        in_specs=[one(vec), one(lam), one(lam), one(mat), one(mat), whole(lam), whole(lam), whole(mat), whole(mat)],
        out_specs=(whole(vec), whole(lam), whole(lam), whole(mat), whole(mat)),
        out_shape=(SDS(vec, F32), SDS(lam, F32), SDS(lam, F32), SDS(mat, F32), SDS(mat, F32)),
    )(log_dt, lam_re, lam_im, bt_re, bt_im, g_ar, g_ai, g_br, g_bi)


def _norm_proj(layer, x, norm_g, wg_in, b_in, carry=None):
    n_w = len(wg_in)

    def body(x_ref, g_ref, b_ref, *refs):
        w_refs, o_ref = refs[:n_w], refs[n_w]
        _, xn = _rms(x_ref[...])
        h = (xn * g_ref[layer:layer + 1, :]).astype(BF16)
        for k in range(N_DEV):
            cols = slice(k * WIDTH, (k + 1) * WIDTH)
            acc = b_ref[layer:layer + 1, cols]
            row = 0
            for w_ref in w_refs:
                rows = w_ref.shape[1]
                acc = acc + _dot(h[:, row:row + rows], w_ref[k])
                row += rows
            o_ref[:, cols] = acc

    (proj,), moved = _pcall(
        body, name=f"norm_proj_l{layer}",
        out_shape=[SDS((SEQ, N_IN), F32)],
        grid=(SEQ // TILE_M,),
        in_specs=[pl.BlockSpec((TILE_M, D_MODEL), lambda i: (i, 0)),
                  _const((DEPTH, D_MODEL)),
                  _const((DEPTH, N_IN))] + [_const(w.shape) for w in wg_in],
        out_specs=[pl.BlockSpec((TILE_M, N_IN), lambda i: (i, 0))],
        args=[x, norm_g, b_in, *wg_in], sem=("parallel",), carry=carry)
    return proj, moved


TIME_BLK = 512
N_TBLK = SEQ // TIME_BLK
N_PANEL = CH_S // 128
STATE_SHAPE = (N_PANEL, SEQ * 8, 128)


def _s5_layer_specs(layer):
    mat = lambda: pl.BlockSpec((None, N_GROUP, GROUP_W, STATE), lambda i: (layer, 0, 0, 0))
    ab = lambda: pl.BlockSpec((None, N_GROUP, STATE), lambda i: (layer, 0, 0))
    return [mat(), mat(), mat(), mat(), ab(), ab(), _const((DEPTH, WIDTH))]


def _s5_layer_scratch():
    return [pltpu.VMEM((N_CHUNK, CH_W, CH_S), BF16)] * 4 + [pltpu.VMEM((8, CH_S), F32)] * 2


def _s5_layer_fill(btre_ref, btim_ref, cre_ref, cim_ref, are_ref, aim_ref, bdre, bdim, ctre, ctim, a1, a2):
    for m in (bdre, bdim, ctre, ctim):
        m[...] = jnp.zeros_like(m)
    for grp in range(N_GROUP):
        k, g = divmod(grp, CH_G)
        rows = slice(g * GROUP_W, (g + 1) * GROUP_W)
        cols = slice(g * STATE, (g + 1) * STATE)
        bdre[k, rows, cols] = btre_ref[grp].astype(BF16)
        bdim[k, rows, cols] = btim_ref[grp].astype(BF16)
        ctre[k, rows, cols] = cre_ref[grp].astype(BF16)
        ctim[k, rows, cols] = cim_ref[grp].astype(BF16)
        ar = are_ref[grp:grp + 1, :]
        ai = aim_ref[grp:grp + 1, :]
        a1[k:k + 1, cols] = ar
        a1[N_CHUNK + k:N_CHUNK + k + 1, cols] = ar
        a2[k:k + 1, cols] = -ai
        a2[N_CHUNK + k:N_CHUNK + k + 1, cols] = ai


SCAN_UNROLL = 8


def _panels(tile):
    return [tile[:, p * 128:(p + 1) * 128] for p in range(N_PANEL)]


def _rows_load(ref, row):
    return jnp.concatenate([ref[p, pl.ds(row, TIME_BLK, stride=8), :] for p in range(N_PANEL)], axis=1)


def _rows_store(ref, row, val):
    for p in range(N_PANEL):
        ref[p, pl.ds(row, TIME_BLK, stride=8), :] = val[:, p * 128:(p + 1) * 128]


def _s5_scan_fwd(layer, proj, bbt_re, bbt_im, c_re, c_im, abar_re, abar_im, d_skip, carry=None):
    def body(u_ref, btre_ref, btim_ref, cre_ref, cim_ref, are_ref, aim_ref, d_ref, s_ref, y_ref,
             bdre, bdim, ctre, ctim, a1, a2, state):
        @pl.when(pl.program_id(0) == 0)
        def _():
            _s5_layer_fill(btre_ref, btim_ref, cre_ref, cim_ref, are_ref, aim_ref, bdre, bdim, ctre, ctim, a1, a2)
            state[...] = jnp.zeros_like(state)

        for k in range(N_CHUNK):
            ub = u_ref[:, k * CH_W:(k + 1) * CH_W].astype(BF16)
            _rows_store(s_ref, k, _dot(ub, bdre[k]))
            _rows_store(s_ref, N_CHUNK + k, _dot(ub, bdim[k]))
        m1 = _panels(a1[...])
        m2 = _panels(a2[...])

        def steps(n, tile):
            for r in range(SCAN_UNROLL):
                rows = pl.ds(pl.multiple_of((n * SCAN_UNROLL + r) * 8, 8), 8)
                tile = [m1[p] * tile[p] + m2[p] * pltpu.roll(tile[p], N_CHUNK, 0) + s_ref[p, rows, :]
                        for p in range(N_PANEL)]
                for p in range(N_PANEL):
                    s_ref[p, rows, :] = tile[p]
            return tile

        tile = lax.fori_loop(0, TIME_BLK // SCAN_UNROLL, steps, _panels(state[...]))
        state[...] = jnp.concatenate(tile, axis=1)
        d = d_ref[layer:layer + 1, :]
        for k in range(N_CHUNK):
            cols = slice(k * CH_W, (k + 1) * CH_W)
            y = (_dot_nt(_rows_load(s_ref, k).astype(BF16), ctre[k])
                 - _dot_nt(_rows_load(s_ref, N_CHUNK + k).astype(BF16), ctim[k]))
            y_ref[:, cols] = y + d[:, cols] * u_ref[:, cols]

    return _pcall(
        body, name=f"s5_fwd_l{layer}",
        out_shape=(SDS(STATE_SHAPE, F32), SDS((SEQ, WIDTH), F32)),
        grid=(N_TBLK,),
        in_specs=[pl.BlockSpec((TIME_BLK, WIDTH), lambda i: (i, 0))] + _s5_layer_specs(layer),
        out_specs=(pl.BlockSpec((N_PANEL, TIME_BLK * 8, 128), lambda i: (0, i, 0)),
                   pl.BlockSpec((TIME_BLK, WIDTH), lambda i: (i, 0))),
        scratch_shapes=_s5_layer_scratch() + [pltpu.VMEM((8, CH_S), F32)],
        args=[proj, bbt_re, bbt_im, c_re, c_im, abar_re, abar_im, d_skip], sem=("arbitrary",), carry=carry)


def _s5_scan_bwd(layer, dy0, proj, states, bbt_re, bbt_im, c_re, c_im, abar_re, abar_im, d_skip, dproj,
                 carry=None):
    def body(dy_ref, u_ref, s_ref, sprev_ref, btre_ref, btim_ref, cre_ref, cim_ref, are_ref, aim_ref, d_ref, _,
             du_ref, gbre_ref, gbim_ref, gcre_ref, gcim_ref, gare_ref, gaim_ref, gd_ref,
             lam_ref, bdre, bdim, ctre, ctim, a1, a2, state, acc1, acc2, gbre, gbim, gcre, gcim, gd):
        step_id = pl.program_id(0)

        @pl.when(step_id == 0)
        def _():
            _s5_layer_fill(btre_ref, btim_ref, cre_ref, cim_ref, are_ref, aim_ref, bdre, bdim, ctre, ctim, a1, a2)
            for r in (state, acc1, acc2, gbre, gbim, gcre, gcim, gd):
                r[...] = jnp.zeros_like(r)

        for k in range(N_CHUNK):
            dyb = dy_ref[:, k * CH_W:(k + 1) * CH_W].astype(BF16)
            _rows_store(lam_ref, k, _dot(dyb, ctre[k]))
            _rows_store(lam_ref, N_CHUNK + k, -_dot(dyb, ctim[k]))
            gcre[k] += _dot_tn(dyb, _rows_load(s_ref, k).astype(BF16))
            gcim[k] -= _dot_tn(dyb, _rows_load(s_ref, N_CHUNK + k).astype(BF16))

        m1 = _panels(a1[...])
        m2 = _panels(-a2[...])
        has_before = (step_id < N_TBLK - 1).astype(F32)

        def one(t8, c, first_token):
            tile, swapped, p1, p2 = c
            rows = pl.ds(t8, 8)
            tile = [m1[p] * tile[p] + m2[p] * swapped[p] + lam_ref[p, rows, :] for p in range(N_PANEL)]
            swapped = [pltpu.roll(tile[p], N_CHUNK, 0) for p in range(N_PANEL)]
            for p in range(N_PANEL):
                lam_ref[p, rows, :] = tile[p]
            if first_token:
                before = [sprev_ref[p] * has_before for p in range(N_PANEL)]
            else:
                before = [s_ref[p, pl.ds(t8 - 8, 8), :] for p in range(N_PANEL)]
            p1 = [p1[p] + tile[p] * before[p] for p in range(N_PANEL)]
            p2 = [p2[p] + swapped[p] * before[p] for p in range(N_PANEL)]
            return tile, swapped, p1, p2

        def steps(n, c):
            for r in range(SCAN_UNROLL):
                t8 = pl.multiple_of((TIME_BLK - 1 - (n * SCAN_UNROLL + r)) * 8, 8)
                c = one(t8, c, False)
            return c

        tile0 = _panels(state[...])
        c = (tile0, [pltpu.roll(t, N_CHUNK, 0) for t in tile0], _panels(acc1[...]), _panels(acc2[...]))
        c = lax.fori_loop(0, TIME_BLK // SCAN_UNROLL - 1, steps, c)
        for r in range(SCAN_UNROLL - 1, -1, -1):
            c = one(r * 8, c, r == 0)
        state[...] = jnp.concatenate(c[0], axis=1)
        acc1[...] = jnp.concatenate(c[2], axis=1)
        acc2[...] = jnp.concatenate(c[3], axis=1)

        d = d_ref[layer:layer + 1, :]
        for k in range(N_CHUNK):
            cols = slice(k * CH_W, (k + 1) * CH_W)
            lrb = _rows_load(lam_ref, k).astype(BF16)
            lib = _rows_load(lam_ref, N_CHUNK + k).astype(BF16)
            u = u_ref[:, cols]
            ub = u.astype(BF16)
            dy = dy_ref[:, cols]
            du = dy * d[:, cols] + _dot_nt(lrb, bdre[k]) + _dot_nt(lib, bdim[k])
            du_ref[:, cols] = du.astype(BF16)
            gbre[k] += _dot_tn(ub, lrb)
            gbim[k] += _dot_tn(ub, lib)
        gd[...] += jnp.sum(dy_ref[...] * u_ref[...], axis=0, keepdims=True)

        @pl.when(step_id == N_TBLK - 1)
        def _():
            gd_ref[...] = gd[...]
            ga_re = acc1[0:N_CHUNK, :] + acc1[N_CHUNK:, :]
            ga_im = acc2[0:N_CHUNK, :] - acc2[N_CHUNK:, :]
            for grp in range(N_GROUP):
                k, g = divmod(grp, CH_G)
                rows = slice(g * GROUP_W, (g + 1) * GROUP_W)
                cols = slice(g * STATE, (g + 1) * STATE)
                gcre_ref[grp] = gcre[k, rows, cols]
                gcim_ref[grp] = gcim[k, rows, cols]
                gbre_ref[grp] = gbre[k, rows, cols]
                gbim_ref[grp] = gbim[k, rows, cols]
                gare_ref[grp:grp + 1, :] = ga_re[k:k + 1, cols]
                gaim_ref[grp:grp + 1, :] = ga_im[k:k + 1, cols]

    back = lambda i: N_TBLK - 1 - i
    tok = lambda: pl.BlockSpec((TIME_BLK, WIDTH), lambda i: (back(i), 0))
    mat = lambda: _const((N_GROUP, GROUP_W, STATE))
    acc_mat = pltpu.VMEM((N_CHUNK, CH_W, CH_S), F32)
    return _pcall(
        body, name=f"s5_bwd_l{layer}",
        out_shape=(SDS((SEQ, N_IN), BF16), SDS((N_GROUP, GROUP_W, STATE), F32), SDS((N_GROUP, GROUP_W, STATE), F32),
                   SDS((N_GROUP, GROUP_W, STATE), F32), SDS((N_GROUP, GROUP_W, STATE), F32),
                   SDS((N_GROUP, STATE), F32), SDS((N_GROUP, STATE), F32), SDS((1, WIDTH), F32)),
        grid=(N_TBLK,),
        in_specs=[tok(), tok(),
                  pl.BlockSpec((N_PANEL, TIME_BLK * 8, 128), lambda i: (0, back(i), 0)),
                  pl.BlockSpec((N_PANEL, 8, 128), lambda i: (0, jnp.maximum(back(i) * TIME_BLK - 1, 0), 0))]
        + _s5_layer_specs(layer) + [ANY],
        out_specs=(tok(), mat(), mat(), mat(), mat(), _const((N_GROUP, STATE)), _const((N_GROUP, STATE)),
                   _const((1, WIDTH))),
        scratch_shapes=[pltpu.VMEM((N_PANEL, TIME_BLK * 8, 128), F32)] + _s5_layer_scratch()
        + [pltpu.VMEM((8, CH_S), F32)] * 3 + [acc_mat] * 4 + [pltpu.VMEM((1, WIDTH), F32)],
        args=[dy0, proj, states, states, bbt_re, bbt_im, c_re, c_im, abar_re, abar_im, d_skip, dproj],
        aliases={11: 0}, sem=("arbitrary",), limit=VMEM_LIMIT_BIG, carry=carry)


def _pool_counts(win):
    t = lax.broadcasted_iota(jnp.int32, (SEQ, POOL_GROUP), 0)
    return t, jnp.minimum(t + 1, win).astype(F32)


def _pool_fwd(layer, proj):
    def body(u_ref, o_ref):
        for gi, win in enumerate(POOL_WINDOWS):
            cols = slice(gi * POOL_GROUP, (gi + 1) * POOL_GROUP)
            u = u_ref[:, cols]
            t, count = _pool_counts(win)
            acc = u
            k = 1
            while k < win:
                acc = acc + jnp.where(t >= k, pltpu.roll(acc, k, 0), 0.0)
                k *= 2
            o_ref[:, cols] = acc / count - u

    return pl.pallas_call(
        body, name=f"pool_fwd_l{layer}",
        out_shape=SDS((SEQ, WIDTH), F32),
        grid=(1,),
        in_specs=[pl.BlockSpec((SEQ, WIDTH), lambda i: (0, 2))],
        out_specs=pl.BlockSpec((SEQ, WIDTH), lambda i: (0, 0)),
        compiler_params=_cp(("arbitrary",)),
    )(proj)


def _gelu_parts(y0):
    t = jnp.tanh(GELU_C * (y0 + GELU_A * (y0 * y0 * y0)))
    return t, 0.5 * y0 * (1.0 + t)


def _mix_forward(layer, p_ref, y0_ref, pooled_ref, wglu_ref, bglu_ref, pw_ref, scale_ref, wa_ref, wb_ref):
    za = p_ref[:, WIDTH:2 * WIDTH]
    zb = p_ref[:, 3 * WIDTH:4 * WIDTH]
    ga = p_ref[:, 4 * WIDTH:4 * WIDTH + D_MODEL]
    gb = p_ref[:, 4 * WIDTH + D_MODEL:]
    y0 = y0_ref[...]
    t, y1 = _gelu_parts(y0)
    y1b = y1.astype(BF16)
    q = _dot(y1b, wglu_ref[...].reshape(WIDTH, WIDTH)) + bglu_ref[layer:layer + 1, :]
    sq = _sig(q)
    y2 = y1 * sq
    sza = _sig(za)
    silu_za = za * sza
    ya = y2 * silu_za
    pooled = pooled_ref[...]
    mixed = jnp.concatenate(
        [_dot(pooled[:, g * POOL_GROUP:(g + 1) * POOL_GROUP].astype(BF16), pw_ref[g].astype(BF16))
         for g in range(len(POOL_WINDOWS))], axis=1)
    szb = _sig(zb)
    silu_zb = zb * szb
    scale = scale_ref[layer:layer + 1, :]
    ms = mixed * scale
    yb = ms * silu_zb
    yab = ya.astype(BF16)
    ybb = yb.astype(BF16)
    ma = _dot(yab, wa_ref[...])
    mb = _dot(ybb, wb_ref[...])
    sga = _sig(ga)
    sgb = _sig(gb)
    merged = sga * ma + sgb * mb
    return dict(za=za, zb=zb, y0=y0, t=t, y1=y1, y1b=y1b, sq=sq, y2=y2, sza=sza, silu_za=silu_za,
                pooled=pooled, mixed=mixed, szb=szb, silu_zb=silu_zb, scale=scale, ms=ms, yab=yab, ybb=ybb,
                ma=ma, mb=mb, sga=sga, sgb=sgb, merged=merged)


def _mix_weight_specs(layer):
    return [_const((N_DEV, WIDTH // N_DEV, WIDTH)),
            _const((DEPTH, WIDTH)),
            pl.BlockSpec((None, 4, POOL_GROUP, POOL_GROUP), lambda i: (layer, 0, 0, 0)),
            _const((DEPTH, WIDTH)),
            _const((WIDTH, D_MODEL)),
            _const((WIDTH, D_MODEL)),
            _const((N_DEV, D_MODEL // N_DEV, D_MODEL))]


def _mix_fwd(layer, x, proj, y0, pooled, wg_glu, b_glu, pool_w, pool_scale, wg_a, wg_b, wg_out, carry=None):
    def body(x_ref, p_ref, y0_ref, pooled_ref, wglu_ref, bglu_ref, pw_ref, scale_ref, wa_ref, wb_ref,
             wout_ref, o_ref):
        f = _mix_forward(layer, p_ref, y0_ref, pooled_ref, wglu_ref, bglu_ref, pw_ref, scale_ref, wa_ref, wb_ref)
        wout = wout_ref[...].reshape(D_MODEL, D_MODEL)
        o_ref[...] = x_ref[...] + _dot(f["merged"].astype(BF16), wout)

    (x_next,), moved = _pcall(
        body, name=f"mix_fwd_l{layer}",
        out_shape=[SDS((SEQ, D_MODEL), F32)],
        grid=(SEQ // TILE_M,),
        in_specs=[pl.BlockSpec((TILE_M, D_MODEL), lambda i: (i, 0)),
                  pl.BlockSpec((TILE_M, N_IN), lambda i: (i, 0)),
                  pl.BlockSpec((TILE_M, WIDTH), lambda i: (i, 0)),
                  pl.BlockSpec((TILE_M, WIDTH), lambda i: (i, 0))] + _mix_weight_specs(layer),
        out_specs=[pl.BlockSpec((TILE_M, D_MODEL), lambda i: (i, 0))],
        args=[x, proj, y0, pooled, wg_glu, b_glu, pool_w, pool_scale, wg_a, wg_b, wg_out],
        sem=("parallel",), carry=carry)
    return x_next, moved


def _loss_head(x, target, final_g):
    def body(x_ref, t_ref, g_ref, dx_ref, loss_ref, gg_ref):
        @pl.when(pl.program_id(0) == 0)
        def _():
            loss_ref[...] = jnp.zeros_like(loss_ref)
            gg_ref[...] = jnp.zeros_like(gg_ref)

        g = g_ref[...]
        rs, xn = _rms(x_ref[...])
        err = xn * g - t_ref[...]
        loss_ref[...] += 0.5 * jnp.sum(jnp.mean(err * err, axis=-1, keepdims=True), axis=0, keepdims=True)
        dy = err * (1.0 / D_MODEL)
        gg_ref[...] += jnp.sum(dy * xn, axis=0, keepdims=True)
        dxn = dy * g
        dx_ref[...] = rs * (dxn - xn * jnp.mean(dxn * xn, axis=-1, keepdims=True))

    return pl.pallas_call(
        body, name="loss_head",
        out_shape=(SDS((SEQ, D_MODEL), F32), SDS((1, 1), F32), SDS((1, D_MODEL), F32)),
        grid=(SEQ // TILE_M,),
        in_specs=[pl.BlockSpec((TILE_M, D_MODEL), lambda i: (i, 0)),
                  pl.BlockSpec((TILE_M, D_MODEL), lambda i: (i, 0)),
                  _const((1, D_MODEL))],
        out_specs=(pl.BlockSpec((TILE_M, D_MODEL), lambda i: (i, 0)), _const((1, 1)), _const((1, D_MODEL))),
        compiler_params=_cp(("arbitrary",)),
    )(x, target, final_g)


def _big_shapes():
    return dict(w_out=(DEPTH, N_DEV, D_MODEL // N_DEV, D_MODEL), w_branch_a=(DEPTH, N_DEV, WIDTH, D_MODEL // N_DEV),
                w_branch_b=(DEPTH, N_DEV, WIDTH, D_MODEL // N_DEV), ssm_w_glu=(DEPTH, N_DEV, WIDTH // N_DEV, WIDTH),
                w_in=(DEPTH, N_DEV, D_MODEL, WIDTH))


def _mix_bwd(layer, dx_next, proj, y0, pooled, wg_glu, b_glu, pool_w, pool_scale, wg_a, wg_b, wg_out, prev,
             carry=None):
    n_k = N_DEV
    n_prev = 0 if prev is None else len(prev)

    def body(*refs):
        (dx_ref, p_ref, y0_ref, pooled_ref, wglu_ref, bglu_ref, pw_ref, scale_ref, wa_ref, wb_ref,
         wout_ref) = refs[:11]
        (dproj_ref, dy0_ref, dpooled_ref, gwout_ref, gwa_ref, gwb_ref, gwglu_ref, gpw_ref,
         gscale_ref, gbglu_ref) = refs[11 + n_prev:]

        @pl.when(pl.program_id(0) == 0)
        def _():
            for r in (gwout_ref, gwa_ref, gwb_ref, gwglu_ref, gpw_ref, gscale_ref, gbglu_ref):
                r[...] = jnp.zeros_like(r)

        f = _mix_forward(layer, p_ref, y0_ref, pooled_ref, wglu_ref, bglu_ref, pw_ref, scale_ref, wa_ref, wb_ref)
        wglu = wglu_ref[...].reshape(WIDTH, WIDTH)
        wout = wout_ref[...].reshape(D_MODEL, D_MODEL)
        blk = D_MODEL // n_k
        dxb = dx_ref[...].astype(BF16)
        dmerged = _dot_nt(dxb, wout)
        gwout = _dot_tn(f["merged"].astype(BF16), dxb)
        for k in range(n_k):
            gwout_ref[_slot(k)] += gwout[k * blk:(k + 1) * blk, :]
        dma = dmerged * f["sga"]
        dmb = dmerged * f["sgb"]
        dga = dmerged * f["ma"] * f["sga"] * (1.0 - f["sga"])
        dgb = dmerged * f["mb"] * f["sgb"] * (1.0 - f["sgb"])
        dmab = dma.astype(BF16)
        dmbb = dmb.astype(BF16)
        dya = _dot_nt(dmab, wa_ref[...])
        dyb = _dot_nt(dmbb, wb_ref[...])
        gwa = _dot_tn(f["yab"], dmab)
        gwb = _dot_tn(f["ybb"], dmbb)
        for k in range(n_k):
            gwa_ref[_slot(k)] += gwa[:, k * blk:(k + 1) * blk]
            gwb_ref[_slot(k)] += gwb[:, k * blk:(k + 1) * blk]
        zb, szb = f["zb"], f["szb"]
        dzb = dyb * f["ms"] * (szb * (1.0 + zb * (1.0 - szb)))
        dms = dyb * f["silu_zb"]
        gscale_ref[...] += jnp.sum(dms * f["mixed"], axis=0, keepdims=True)
        dmixed = (dms * f["scale"]).astype(BF16)
        pooled = f["pooled"]
        for g in range(len(POOL_WINDOWS)):
            cols = slice(g * POOL_GROUP, (g + 1) * POOL_GROUP)
            dpooled_ref[:, cols] = _dot_nt(dmixed[:, cols], pw_ref[g].astype(BF16))
            gpw_ref[g] += _dot_tn(pooled[:, cols].astype(BF16), dmixed[:, cols])
        za, sza = f["za"], f["sza"]
        dza = dya * f["y2"] * (sza * (1.0 + za * (1.0 - sza)))
        dy2 = dya * f["silu_za"]
        sq = f["sq"]
        dq = dy2 * f["y1"] * sq * (1.0 - sq)
        dqb = dq.astype(BF16)
        dy1 = dy2 * sq + _dot_nt(dqb, wglu)
        gwglu = _dot_tn(f["y1b"], dqb)
        rblk = WIDTH // n_k
        for k in range(n_k):
            gwglu_ref[_slot(k)] += gwglu[k * rblk:(k + 1) * rblk, :]
        gbglu_ref[...] += jnp.sum(dq, axis=0, keepdims=True)
        y0, t = f["y0"], f["t"]
        dgelu = 0.5 * (1.0 + t) + 0.5 * y0 * (1.0 - t * t) * (GELU_C * (1.0 + 3.0 * GELU_A * y0 * y0))
        dy0_ref[...] = dy1 * dgelu
        zeros = jnp.zeros((TILE_M, WIDTH), BF16)
        dproj_ref[:, 0:WIDTH] = zeros
        dproj_ref[:, WIDTH:2 * WIDTH] = dza.astype(BF16)
        dproj_ref[:, 2 * WIDTH:3 * WIDTH] = zeros
        dproj_ref[:, 3 * WIDTH:4 * WIDTH] = dzb.astype(BF16)
        dproj_ref[:, 4 * WIDTH:4 * WIDTH + D_MODEL] = dga.astype(BF16)
        dproj_ref[:, 4 * WIDTH + D_MODEL:] = dgb.astype(BF16)

    tile = lambda w: pl.BlockSpec((TILE_M, w), lambda i: (i, 0))
    shapes = _big_shapes()
    big = ["w_out", "w_branch_a", "w_branch_b", "ssm_w_glu"]
    slab = lambda n: pl.BlockSpec((None,) + shapes[n][1:], lambda i: (layer, 0, 0, 0))
    args = [dx_next, proj, y0, pooled, wg_glu, b_glu, pool_w, pool_scale, wg_a, wg_b, wg_out]
    return _pcall(
        body, name=f"mix_bwd_l{layer}",
        out_shape=(SDS((SEQ, N_IN), BF16), SDS((SEQ, WIDTH), F32), SDS((SEQ, WIDTH), F32))
        + tuple(SDS(shapes[n], F32) for n in big)
        + (SDS((4, POOL_GROUP, POOL_GROUP), F32), SDS((1, WIDTH), F32), SDS((1, WIDTH), F32)),
        grid=(SEQ // TILE_M,),
        in_specs=[tile(D_MODEL), tile(N_IN), tile(WIDTH), tile(WIDTH)] + _mix_weight_specs(layer) + [ANY] * n_prev,
        out_specs=(tile(N_IN), tile(WIDTH), tile(WIDTH)) + tuple(slab(n) for n in big)
        + (_const((4, POOL_GROUP, POOL_GROUP)), _const((1, WIDTH)), _const((1, WIDTH))),
        args=args + list(prev or ()),
        aliases={len(args) + i: 3 + i for i in range(n_prev)},
        sem=("arbitrary",), limit=VMEM_LIMIT_BIG, carry=carry)


def _pool_bwd(layer, dpooled, dproj):
    def body(dp_ref, _, o_ref):
        for gi, win in enumerate(POOL_WINDOWS):
            cols = slice(gi * POOL_GROUP, (gi + 1) * POOL_GROUP)
            dp = dp_ref[:, cols]
            t, count = _pool_counts(win)
            e = dp / count
            acc = e
            k = 1
            while k < win:
                acc = acc + jnp.where(t < SEQ - k, pltpu.roll(acc, SEQ - k, 0), 0.0)
                k *= 2
            o_ref[:, cols] = (acc - dp).astype(BF16)

    return pl.pallas_call(
        body, name=f"pool_bwd_l{layer}",
        out_shape=SDS((SEQ, N_IN), BF16),
        grid=(1,),
        in_specs=[pl.BlockSpec((SEQ, WIDTH), lambda i: (0, 0)), ANY],
        out_specs=pl.BlockSpec((SEQ, WIDTH), lambda i: (0, 2)),
        input_output_aliases={1: 0},
        compiler_params=_cp(("arbitrary",)),
    )(dpooled, dproj)


def _proj_wgrad(layer, x, norm_g, dproj, prev, carry=None):
    tm = 512
    n_prev = 0 if prev is None else 1

    def body(*refs):
        x_ref, g_ref, dp_ref = refs[:3]
        gw_ref, gb_ref, ht_ref = refs[3 + n_prev:]
        n, t = pl.program_id(0), pl.program_id(1)

        @pl.when(t == 0)
        def _():
            gw_ref[...] = jnp.zeros_like(gw_ref)
            gb_ref[...] = jnp.zeros_like(gb_ref)

        @pl.when(n == 0)
        def _():
            _, xn = _rms(x_ref[...])
            ht_ref[t] = (xn * g_ref[layer:layer + 1, :]).T.astype(BF16)

        dp = dp_ref[...]
        gw_ref[...] += _dot(ht_ref[t], dp)
        gb_ref[...] += jnp.sum(dp.astype(F32), axis=0, keepdims=True)

    return _pcall(
        body, name=f"proj_wgrad_l{layer}",
        out_shape=(SDS(_big_shapes()["w_in"], F32), SDS((1, N_IN), F32)),
        grid=(N_DEV, SEQ // tm),
        in_specs=[pl.BlockSpec((tm, D_MODEL), lambda n, t: (jnp.where(n == 0, t, 0), 0)),
                  _const((DEPTH, D_MODEL)),
                  pl.BlockSpec((tm, WIDTH), lambda n, t: (t, n))] + [ANY] * n_prev,
        out_specs=(pl.BlockSpec((None, None, D_MODEL, WIDTH), lambda n, t: (layer, _slot(n), 0, 0)),
                   pl.BlockSpec((1, WIDTH), lambda n, t: (0, n))),
        scratch_shapes=[pltpu.VMEM((SEQ // tm, D_MODEL, tm), BF16)],
        args=[x, norm_g, dproj] + ([prev] if n_prev else []),
        aliases={3: 0} if n_prev else {}, sem=("arbitrary", "arbitrary"), carry=carry)


def _proj_dgrad(layer, dx_next, x, norm_g, dproj, wg_in, carry=None):
    n_w = len(wg_in)

    def body(dxn_ref, x_ref, g_ref, dp_ref, *refs):
        w_refs, (dx_ref, gg_ref) = refs[:n_w], refs[n_w:]

        @pl.when(pl.program_id(0) == 0)
        def _():
            gg_ref[...] = jnp.zeros_like(gg_ref)

        parts = []
        for w_ref in w_refs:
            part = jnp.zeros((TILE_M, w_ref.shape[1]), F32)
            for k in range(N_DEV):
                part = part + _dot_nt(dp_ref[:, k * WIDTH:(k + 1) * WIDTH], w_ref[k])
            parts.append(part)
        dh = parts[0] if n_w == 1 else jnp.concatenate(parts, axis=1)
        rs, xn = _rms(x_ref[...])
        gg_ref[...] += jnp.sum(dh * xn, axis=0, keepdims=True)
        dxn = dh * g_ref[layer:layer + 1, :]
        dx_ref[...] = dxn_ref[...] + rs * (dxn - xn * jnp.mean(dxn * xn, axis=-1, keepdims=True))

    return _pcall(
        body, name=f"proj_dgrad_l{layer}",
        out_shape=(SDS((SEQ, D_MODEL), F32), SDS((1, D_MODEL), F32)),
        grid=(SEQ // TILE_M,),
        in_specs=[pl.BlockSpec((TILE_M, D_MODEL), lambda i: (i, 0)),
                  pl.BlockSpec((TILE_M, D_MODEL), lambda i: (i, 0)),
                  _const((DEPTH, D_MODEL)),
                  pl.BlockSpec((TILE_M, N_IN), lambda i: (i, 0))] + [_const(w.shape) for w in wg_in],
        out_specs=(pl.BlockSpec((TILE_M, D_MODEL), lambda i: (i, 0)), _const((1, D_MODEL))),
        args=[dx_next, x, norm_g, dproj, *wg_in], sem=("arbitrary",), carry=carry)


def _my_place():
    return lax.axis_index("x"), lax.axis_index("y"), lax.axis_index("c")


def _gather_plan(shards, layer, by_columns=(), rows_of=None):
    n = len(shards)

    def parts(ins, outs, sems):
        send_sems, recv_sems, local_sems = sems
        x, y, c = _my_place()
        chips = [(1 - x, y), (x, 1 - y), (1 - x, 1 - y)]

        def source(t):
            return ins[t].at[layer] if rows_of is None else ins[t].at[layer, pl.ds(*rows_of)]

        def rows(t, place):
            px, py, pc = place
            index = 4 * px + 2 * py + pc
            if t in by_columns:
                width = shards[t].shape[2]
                return outs[t].at[:, pl.ds(pl.multiple_of(index * width, 128), width)]
            return outs[t].at[index]

        def copy(t, k, block, to, from_src=False):
            return pltpu.make_async_remote_copy(
                src_ref=source(t) if from_src else rows(t, block), dst_ref=rows(t, block),
                send_sem=send_sems.at[7 * t + k], recv_sem=recv_sems.at[7 * t + k], device_id=to,
                device_id_type=MESH)

        def mine(t):
            return pltpu.make_async_copy(source(t), rows(t, (x, y, c)), local_sems.at[t])

        return (x, y, c), chips, copy, mine

    def start(ins, outs, sems):
        me, chips, copy, mine = parts(ins, outs, sems)
        x, y, c = me
        for t in range(n):
            mine(t).start()
            copy(t, 0, me, (x, y, 1 - c), from_src=True).start()
            for j, chip in enumerate(chips):
                copy(t, 1 + j, me, (*chip, c), from_src=True).start()

    def relay(ins, outs, sems):
        me, chips, copy, mine = parts(ins, outs, sems)
        x, y, c = me
        for t in range(n):
            for j, chip in enumerate(chips):
                copy(t, 1 + j, (*chip, c), me).wait_recv()
                copy(t, 4 + j, (*chip, c), (x, y, 1 - c)).start()

    def finish(ins, outs, sems):
        me, chips, copy, mine = parts(ins, outs, sems)
        x, y, c = me
        sibling = (x, y, 1 - c)
        for t in range(n):
            copy(t, 0, sibling, me).wait_recv()
            for j, chip in enumerate(chips):
                copy(t, 4 + j, (*chip, 1 - c), me).wait_recv()
            for k in range(7):
                copy(t, k, me, sibling, from_src=k < 4).wait_send()
            mine(t).wait()

    n_rows = lambda a: a.shape[1] if rows_of is None else rows_of[1]
    out_shape = [SDS((a.shape[1], N_DEV * a.shape[2]) if t in by_columns else (N_DEV, n_rows(a), a.shape[2]), a.dtype)
                 for t, a in enumerate(shards)]
    sems = [pltpu.SemaphoreType.DMA((7 * n,)), pltpu.SemaphoreType.DMA((7 * n,)), pltpu.SemaphoreType.DMA((n,))]
    return _Carried(shards, out_shape, sems, start, finish, relay)


class _Carried:
    def __init__(self, ins, out_shape, sems, start, finish, relay=None):
        self.ins, self.out_shape, self.sems = list(ins), list(out_shape), list(sems)
        self.start, self.finish = start, finish
        self.relay = relay or (lambda ins, outs, sems: None)


def _pcall(body, *, name, grid, in_specs, out_specs, out_shape, args, scratch_shapes=(), aliases=None,
           sem=None, limit=VMEM_LIMIT, carry=None):
    out_shape, out_specs, scratch_shapes = list(out_shape), list(out_specs), list(scratch_shapes)
    n_in, n_out, n_scr = len(args), len(out_shape), len(scratch_shapes)
    if carry is None:
        kern, c_ins, c_out, c_sems = body, [], [], []
    else:
        c_ins, c_out, c_sems = carry.ins, carry.out_shape, carry.sems
        ci, co = len(c_ins), len(c_out)
        steps = tuple(grid)

        def kern(*refs):
            o0 = n_in + ci
            s0 = o0 + n_out + co
            mine = refs[:n_in] + refs[o0:o0 + n_out] + refs[s0:s0 + n_scr]
            theirs = (refs[n_in:o0], refs[o0 + n_out:s0], refs[s0 + n_scr:])
            first = pl.program_id(0) == 0
            last = pl.program_id(0) == steps[0] - 1
            for a in range(1, len(steps)):
                first = jnp.logical_and(first, pl.program_id(a) == 0)
                last = jnp.logical_and(last, pl.program_id(a) == steps[a] - 1)

            @pl.when(first)
            def _():
                carry.start(*theirs)

            @pl.when(last)
            def _():
                carry.relay(*theirs)

            body(*mine)

            @pl.when(last)
            def _():
                carry.finish(*theirs)

        sem = ("arbitrary",) * len(steps)
    res = pl.pallas_call(
        kern, name=name, grid=tuple(grid),
        in_specs=list(in_specs) + [ANY] * len(c_ins),
        out_specs=tuple(out_specs + [ANY] * len(c_out)),
        out_shape=tuple(out_shape + c_out),
        scratch_shapes=scratch_shapes + c_sems,
        input_output_aliases=aliases or {},
        compiler_params=_cp(sem, limit),
    )(*args, *c_ins)
    return res[:n_out], res[n_out:]


def _run_carried(name, carry):
    ci, co = len(carry.ins), len(carry.out_shape)

    def body(*refs):
        parts = (refs[:ci], refs[ci:ci + co], refs[ci + co:])
        carry.start(*parts)
        carry.relay(*parts)
        carry.finish(*parts)

    return pl.pallas_call(
        body, name=name, out_shape=tuple(carry.out_shape),
        in_specs=[ANY] * ci, out_specs=tuple([ANY] * co), scratch_shapes=carry.sems,
    )(*carry.ins)


def _sibling_plan(big, small):
    n = len(big)
    n_copies = 4 * n + len(small)

    def copies(ins, outs, sems):
        send_sems, recv_sems = sems
        x, y, c = _my_place()
        pairs = []
        for t, (_, layer) in enumerate(big):
            for s in range(4):
                pairs.append((ins[t].at[layer, pl.ds(4 * (1 - c) + s, 1)], outs[t].at[pl.ds(s, 1)]))
        pairs += list(zip(ins[n:], outs[n:]))
        return [pltpu.make_async_remote_copy(
            src_ref=src, dst_ref=dst, send_sem=send_sems.at[k], recv_sem=recv_sems.at[k],
            device_id=(x, y, 1 - c), device_id_type=MESH) for k, (src, dst) in enumerate(pairs)]

    def start(ins, outs, sems):
        for cp in copies(ins, outs, sems):
            cp.start()

    def finish(ins, outs, sems):
        for cp in copies(ins, outs, sems):
            cp.wait()

    out_shape = [SDS((4,) + a.shape[2:], a.dtype) for a, _ in big] + [SDS(a.shape, a.dtype) for a in small]
    sems = [pltpu.SemaphoreType.DMA((n_copies,)), pltpu.SemaphoreType.DMA((n_copies,))]
    return _Carried([a for a, _ in big] + list(small), out_shape, sems, start, finish)


def _chips_plan(big, small):
    n, n_small = len(big), len(small)
    max_rows = 512
    parts = [max(1, a.shape[1] // max_rows) for a in big]
    n_copies = 3 * (sum(parts) + n_small)

    def copies(ins, outs, sems, landing):
        send_sems, recv_sems, local_sems = sems
        x, y, c = _my_place()
        my_chip = 2 * x + y
        chips = [(1 - x, y), (x, 1 - y), (1 - x, 1 - y)]
        remote, local = [], []
        for chip in chips:
            to = 2 * chip[0] + chip[1]
            slot = to if landing else my_chip
            pairs = []
            for t in range(n):
                rows_per = big[t].shape[1] // parts[t]
                for p in range(parts[t]):
                    rows = pl.ds(p * rows_per, rows_per)
                    pairs.append((ins[t].at[to, rows], outs[t].at[slot, rows]))
            pairs += [(ins[t], outs[t].at[slot]) for t in range(n, n + n_small)]
            for src, dst in pairs:
                k = len(remote)
                remote.append(pltpu.make_async_remote_copy(
                    src_ref=src, dst_ref=dst, send_sem=send_sems.at[k], recv_sem=recv_sems.at[k],
                    device_id=(*chip, c), device_id_type=MESH))
        for t in range(n):
            local.append(pltpu.make_async_copy(ins[t].at[my_chip], outs[t].at[my_chip], local_sems.at[t]))
        for t in range(n, n + n_small):
            local.append(pltpu.make_async_copy(ins[t], outs[t].at[my_chip], local_sems.at[t]))
        return remote + local

    def start(ins, outs, sems):
        for cp in copies(ins, outs, sems, landing=False):
            cp.start()

    def finish(ins, outs, sems):
        for cp in copies(ins, outs, sems, landing=True):
            cp.wait()

    out_shape = [SDS(a.shape, a.dtype) for a in big] + [SDS((N_CHIP,) + a.shape, a.dtype) for a in small]
    sems = [pltpu.SemaphoreType.DMA((n_copies,)), pltpu.SemaphoreType.DMA((n_copies,)),
            pltpu.SemaphoreType.DMA((n + n_small,))]
    return _Carried(list(big) + list(small), out_shape, sems, start, finish)


def _all_plan(small):
    n = len(small)
    masks = [(m >> 2 & 1, m >> 1 & 1, m & 1) for m in range(1, N_DEV)]

    def copies(ins, outs, sems, landing):
        send_sems, recv_sems, local_sems = sems
        x, y, c = _my_place()
        me = 4 * x + 2 * y + c
        flip = lambda v, bit: 1 - v if bit else v
        remote = []
        for fx, fy, fc in masks:
            peer = (flip(x, fx), flip(y, fy), flip(c, fc))
            slot = 4 * peer[0] + 2 * peer[1] + peer[2] if landing else me
            for t in range(n):
                k = len(remote)
                remote.append(pltpu.make_async_remote_copy(
                    src_ref=ins[t], dst_ref=outs[t].at[slot], send_sem=send_sems.at[k], recv_sem=recv_sems.at[k],
                    device_id=peer, device_id_type=MESH))
        local = [pltpu.make_async_copy(ins[t], outs[t].at[me], local_sems.at[t]) for t in range(n)]
        return remote + local

    def start(ins, outs, sems):
        for cp in copies(ins, outs, sems, landing=False):
            cp.start()

    def finish(ins, outs, sems):
        for cp in copies(ins, outs, sems, landing=True):
            cp.wait()

    out_shape = [SDS((N_DEV,) + a.shape, a.dtype) for a in small]
    sems = [pltpu.SemaphoreType.DMA((7 * n,)), pltpu.SemaphoreType.DMA((7 * n,)), pltpu.SemaphoreType.DMA((n,))]
    return _Carried(list(small), out_shape, sems, start, finish)


def _join(*plans):
    plans = [p for p in plans if p is not None]
    if len(plans) <= 1:
        return plans[0] if plans else None

    def each(fn_name, ins, outs, sems):
        i = o = s = 0
        for p in plans:
            ni, no, ns = len(p.ins), len(p.out_shape), len(p.sems)
            getattr(p, fn_name)(ins[i:i + ni], outs[o:o + no], sems[s:s + ns])
            i, o, s = i + ni, o + no, s + ns

    return _Carried(sum((p.ins for p in plans), []), sum((p.out_shape for p in plans), []),
                    sum((p.sems for p in plans), []),
                    lambda i, o, s: each("start", i, o, s), lambda i, o, s: each("finish", i, o, s),
                    lambda i, o, s: each("relay", i, o, s))


def _row_block(rows, most=256):
    return min(rows, most)


def _add_own(tag, core, g, layer, got):
    _, r, c = got.shape
    rb = _row_block(r, most=1024)

    def body(core_ref, a_ref, b_ref, o_ref):
        o_ref[...] = (a_ref[...] + b_ref[...]).astype(o_ref.dtype)

    return pl.pallas_call(
        body, name=f"add_{tag}", out_shape=SDS(got.shape, BF16),
        grid_spec=pltpu.PrefetchScalarGridSpec(
            num_scalar_prefetch=1, grid=(4, r // rb),
            in_specs=[pl.BlockSpec((None, None, rb, c), lambda s, j, core: (layer, 4 * core[0] + s, j, 0)),
                      pl.BlockSpec((None, rb, c), lambda s, j, core: (s, j, 0))],
            out_specs=pl.BlockSpec((None, rb, c), lambda s, j, core: (s, j, 0))),
        compiler_params=_cp(("parallel", "parallel")),
    )(core, g, got)


def _add_lists(tag, own, got, grid=None, specs=None, dtype=F32):
    n = len(own)

    def body(*refs):
        for a, b, o in zip(refs[:n], refs[n:2 * n], refs[2 * n:]):
            o[...] = (a[...] + b[...]).astype(o.dtype)

    kw = {}
    if grid is not None:
        kw = dict(grid=grid, in_specs=list(specs) * 2, out_specs=tuple(specs),
                  compiler_params=_cp(("parallel",) * len(grid)))
    return pl.pallas_call(
        body, name=f"add_{tag}", out_shape=tuple(SDS(a.shape, dtype) for a in own), **kw)(*own, *got)


def _adamw_math(w, g, m, v):
    m = ADAM_B1 * m + (1.0 - ADAM_B1) * g
    v = ADAM_B2 * v + (1.0 - ADAM_B2) * (g * g)
    m_hat = m / (1.0 - ADAM_B1 ** ADAM_STEP)
    v_hat = v / (1.0 - ADAM_B2 ** ADAM_STEP)
    delta = -ADAM_LR * (m_hat / (jnp.sqrt(v_hat) + ADAM_EPS) + ADAM_WD * w)
    return delta, m, v


def _sum_slots_adamw(tag, slots, w, m, v):
    _, r, c = slots[0].shape
    rb = _row_block(r)

    def body(s0_ref, s1_ref, w_ref, m_ref, v_ref, g_ref, d_ref, nm_ref, nv_ref):
        first = pl.program_id(1) == 0
        g = _sum4([jnp.where(first, s0_ref[k], s1_ref[k]).astype(F32) for k in range(N_CHIP)])
        delta, nm, nv = _adamw_math(w_ref[...], g, m_ref[...], v_ref[...])
        g_ref[...] = g
        d_ref[...] = delta
        nm_ref[...] = nm
        nv_ref[...] = nv

    spec = pl.BlockSpec((None, rb, c), lambda j, l: (l, j, 0))
    sspec = pl.BlockSpec((N_CHIP, rb, c), lambda j, l: (0, j, 0))
    s = SDS((DEPTH, r, c), F32)
    return pl.pallas_call(
        body, name=f"adamw_{tag}", out_shape=(s, s, s, s),
        grid=(r // rb, DEPTH), in_specs=[sspec, sspec, spec, spec, spec], out_specs=(spec, spec, spec, spec),
        compiler_params=_cp(("parallel", "arbitrary")),
    )(*slots, w, m, v)


def _adamw_small(tag, entries, grid=None, sums=()):
    flat_in, in_specs, out_shape, out_specs, layout = [], [], [], [], []
    for slots, w, m, v, slot_spec, w_spec in entries:
        per_layer = isinstance(slots, (list, tuple))
        n_slot = len(slots) if per_layer else 1
        flat_in += (list(slots) if per_layer else [slots]) + [w, m, v]
        in_specs += [slot_spec] * n_slot + [w_spec] * 3
        out_shape += [SDS(w.shape, F32)] * 4
        out_specs += [w_spec] * 4
        layout.append((per_layer, n_slot))
    n_entry_in = len(flat_in)
    flat_in += list(sums)
    out_shape += [SDS(s.shape[1:], F32) for s in sums]
    n_in = len(flat_in)

    def body(*refs):
        for s_ref, o_ref in zip(refs[n_entry_in:n_in], refs[len(refs) - len(sums):]):
            o_ref[...] = _sum_slots(s_ref)
        i, o = 0, n_in
        for per_layer, n_slot in layout:
            s_refs = refs[i:i + n_slot]
            w_ref, m_ref, v_ref = refs[i + n_slot:i + n_slot + 3]
            outs = refs[o:o + 4]
            if per_layer:
                for l, s_ref in enumerate(s_refs):
                    at = (slice(l, l + 1),) if len(w_ref.shape) == 2 else (l,)
                    g = _sum_slots(s_ref)
                    res = (g,) + _adamw_math(w_ref[at], g, m_ref[at], v_ref[at])
                    for o_ref, val in zip(outs, res):
                        o_ref[at] = val
            else:
                g = _sum_slots(s_refs[0])
                res = (g,) + _adamw_math(w_ref[...], g, m_ref[...], v_ref[...])
                for o_ref, val in zip(outs, res):
                    o_ref[...] = val
            i += n_slot + 3
            o += 4

    kw = {}
    if grid is not None:
        kw = dict(grid=grid, in_specs=in_specs, out_specs=tuple(out_specs),
                  compiler_params=_cp(("parallel",) * len(grid)))
    res = pl.pallas_call(body, name=f"adamw_{tag}", out_shape=tuple(out_shape), **kw)(*flat_in)
    return [tuple(res[4 * e:4 * e + 4]) for e in range(len(entries))], res[4 * len(entries):]


def kernel(x, norm_g, w_in, b_in, ssm_log_dt, ssm_lam_re, ssm_lam_im, ssm_b_re, ssm_b_im, ssm_c_re, ssm_c_im, ssm_d, ssm_w_glu, ssm_b_glu, pool_w, pool_scale, w_branch_a, w_branch_b, w_out, final_norm_g, loss_target, m_norm_g, m_w_in, m_b_in, m_ssm_log_dt, m_ssm_lam_re, m_ssm_lam_im, m_ssm_b_re, m_ssm_b_im, m_ssm_c_re, m_ssm_c_im, m_ssm_d, m_ssm_w_glu, m_ssm_b_glu, m_pool_w, m_pool_scale, m_w_branch_a, m_w_branch_b, m_w_out, m_final_norm_g, v_norm_g, v_w_in, v_b_in, v_ssm_log_dt, v_ssm_lam_re, v_ssm_lam_im, v_ssm_b_re, v_ssm_b_im, v_ssm_c_re, v_ssm_c_im, v_ssm_d, v_ssm_w_glu, v_ssm_b_glu, v_pool_w, v_pool_scale, v_w_branch_a, v_w_branch_b, v_w_out, v_final_norm_g):
    weights = dict(norm_g=norm_g, w_in=w_in, b_in=b_in, ssm_log_dt=ssm_log_dt, ssm_lam_re=ssm_lam_re,
                   ssm_lam_im=ssm_lam_im, ssm_b_re=ssm_b_re, ssm_b_im=ssm_b_im, ssm_c_re=ssm_c_re,
                   ssm_c_im=ssm_c_im, ssm_d=ssm_d, ssm_w_glu=ssm_w_glu, ssm_b_glu=ssm_b_glu, pool_w=pool_w,
                   pool_scale=pool_scale, w_branch_a=w_branch_a, w_branch_b=w_branch_b, w_out=w_out,
                   final_norm_g=final_norm_g.reshape(1, D_MODEL))
    mom_m = dict(norm_g=m_norm_g, w_in=m_w_in, b_in=m_b_in, ssm_log_dt=m_ssm_log_dt, ssm_lam_re=m_ssm_lam_re,
                 ssm_lam_im=m_ssm_lam_im, ssm_b_re=m_ssm_b_re, ssm_b_im=m_ssm_b_im, ssm_c_re=m_ssm_c_re,
                 ssm_c_im=m_ssm_c_im, ssm_d=m_ssm_d, ssm_w_glu=m_ssm_w_glu, ssm_b_glu=m_ssm_b_glu,
                 pool_w=m_pool_w, pool_scale=m_pool_scale, w_branch_a=m_w_branch_a, w_branch_b=m_w_branch_b,
                 w_out=m_w_out, final_norm_g=m_final_norm_g.reshape(1, D_MODEL))
    mom_v = dict(norm_g=v_norm_g, w_in=v_w_in, b_in=v_b_in, ssm_log_dt=v_ssm_log_dt, ssm_lam_re=v_ssm_lam_re,
                 ssm_lam_im=v_ssm_lam_im, ssm_b_re=v_ssm_b_re, ssm_b_im=v_ssm_b_im, ssm_c_re=v_ssm_c_re,
                 ssm_c_im=v_ssm_c_im, ssm_d=v_ssm_d, ssm_w_glu=v_ssm_w_glu, ssm_b_glu=v_ssm_b_glu,
                 pool_w=v_pool_w, pool_scale=v_pool_scale, w_branch_a=v_w_branch_a, w_branch_b=v_w_branch_b,
                 w_out=v_w_out, final_norm_g=v_final_norm_g.reshape(1, D_MODEL))
    order = ["norm_g", "w_in", "b_in", "ssm_log_dt", "ssm_lam_re", "ssm_lam_im", "ssm_b_re", "ssm_b_im",
             "ssm_c_re", "ssm_c_im", "ssm_d", "ssm_w_glu", "ssm_b_glu", "pool_w", "pool_scale", "w_branch_a",
             "w_branch_b", "w_out", "final_norm_g"]
    big_names = ["w_in", "ssm_w_glu", "w_branch_a", "w_branch_b", "w_out"]

    log_dt3 = ssm_log_dt.reshape(DEPTH, N_GROUP, 1)
    b_t = lambda a: a.transpose(0, 1, 3, 2)
    for d in (weights, mom_m, mom_v):
        d["ssm_b_re"], d["ssm_b_im"] = b_t(d["ssm_b_re"]), b_t(d["ssm_b_im"])
    bt_re, bt_im = weights["ssm_b_re"], weights["ssm_b_im"]
    abar_re, abar_im, bbt_re, bbt_im = _s5_params(log_dt3, ssm_lam_re, ssm_lam_im, bt_re, bt_im)
    s5_args = (bbt_re, bbt_im, ssm_c_re, ssm_c_im, abar_re, abar_im, ssm_d)

    w16 = {n: weights[n].astype(BF16) for n in big_names}
    rest = [w16[n] for n in big_names[1:]]
    half = D_MODEL // 2
    wg_in = [None, [None, None]]
    wg_rest = [None, None]
    wg_in[0] = list(_run_carried("gather_w_in_l0", _gather_plan([w16["w_in"]], 0)))
    xs = [x.reshape(SEQ, D_MODEL)]
    saved = []
    for l in range(DEPTH):
        proj, moved = _norm_proj(l, xs[l], norm_g, wg_in[l], b_in,
                                 carry=_gather_plan([w16["w_in"]], 1, rows_of=(0, half)) if l == 0 else None)
        if l == 0:
            (wg_in[1][0],) = moved
        (states, y0), wg_rest[l] = _s5_scan_fwd(l, proj, *s5_args, carry=_gather_plan(rest, l, by_columns=(1, 2)))
        pooled = _pool_fwd(l, proj)
        wg_glu, wg_a, wg_b, wg_out = wg_rest[l]
        x_next, moved = _mix_fwd(l, xs[l], proj, y0, pooled, wg_glu, ssm_b_glu, pool_w, pool_scale, wg_a, wg_b,
                                 wg_out, carry=_gather_plan([w16["w_in"]], 1, rows_of=(half, half)) if l == 0 else None)
        if l == 0:
            (wg_in[1][1],) = moved
        xs.append(x_next)
        saved.append((proj, states, y0, pooled))

    dx, loss_part, g_final = _loss_head(xs[DEPTH], loss_target.reshape(SEQ, D_MODEL), weights["final_norm_g"])

    core = lax.axis_index("c").astype(jnp.int32).reshape(1)
    vec_names = ["norm_g", "b_in", "ssm_d", "ssm_b_glu", "pool_scale", "ssm_log_dt"]
    s5_names = ["ssm_log_dt", "ssm_lam_re", "ssm_lam_im", "ssm_b_re", "ssm_b_im"]
    mat_names = ["pool_w", "ssm_c_re", "ssm_c_im", "ssm_b_re", "ssm_b_im"]
    lane_sparse = ("ssm_c_re", "ssm_c_im", "ssm_b_re", "ssm_b_im")

    def dense(key, a):
        return a.reshape(-1, 128) if key[0] in lane_sparse else a

    def undense(key, slots):
        return slots.reshape((N_CHIP, N_GROUP, GROUP_W, STATE)) if key[0] in lane_sparse else slots

    def add_small(tag, keys, own, got):
        out = [None] * len(keys)
        whole = [i for i, k in enumerate(keys) if k[0] not in mat_names]
        tiled = [i for i, k in enumerate(keys) if k[0] in mat_names]
        if whole:
            for i, r in zip(whole, _add_lists(f"{tag}_a", [own[i] for i in whole], [got[i] for i in whole])):
                out[i] = r
        if tiled:
            specs = [pl.BlockSpec((1, POOL_GROUP, POOL_GROUP), lambda j: (j, 0, 0)) if keys[i][0] == "pool_w"
                     else pl.BlockSpec((own[i].shape[0] // N_CHUNK, 128), lambda j: (j, 0)) for i in tiled]
            for i, r in zip(tiled, _add_lists(f"{tag}_b", [own[i] for i in tiled], [got[i] for i in tiled],
                                              grid=(N_CHUNK,), specs=specs, dtype=BF16)):
                out[i] = r
        return out

    sm = {("final_norm_g", None): g_final, ("loss", None): loss_part}
    slots = {}
    grads = dict.fromkeys(big_names)

    class Wave:
        def __init__(self, tag, layer, big, keys):
            self.tag, self.layer, self.big, self.keys = tag, layer, big, keys

        def to_sibling(self):
            self.own = [dense(k, sm[k]) for k in self.keys]
            return _sibling_plan([(grads[n], self.layer) for n in self.big], self.own)

        def add(self, moved):
            nb = len(self.big)
            self.chip_big = [_add_own(f"{self.tag}_{n}", core, grads[n], self.layer, b)
                             for n, b in zip(self.big, moved[:nb])]
            self.chip_small = add_small(self.tag, self.keys, self.own, moved[nb:])

        def to_chips(self, big=None, small=True):
            self.sent = list(self.big if big is None else big), small
            return _chips_plan([self.chip_big[self.big.index(n)] for n in self.sent[0]],
                               self.chip_small if small else [])

        def landed(self, moved):
            names, small = self.sent
            for n, s in zip(names, moved[:len(names)]):
                slots[(n, self.layer)] = s
            if small:
                for k, s in zip(self.keys, moved[len(names):]):
                    slots[k] = undense(k, s)
            return moved[len(names) + (len(self.keys) if small else 0):]

    def s5_param_grads(l, g_abar_re, g_abar_im, g_bbt_re, g_bbt_im):
        g = _s5_params_bwd(l, log_dt3, ssm_lam_re, ssm_lam_im, bt_re, bt_im, g_abar_re, g_abar_im, g_bbt_re, g_bbt_im)
        sm[("ssm_log_dt", l)] = g[0].reshape(1, N_GROUP)
        for n, a in zip(s5_names[1:], g[1:]):
            sm[(n, l)] = a

    small1 = ["b_in", "ssm_d", "ssm_b_glu", "pool_scale", "pool_w", "ssm_c_re", "ssm_c_im"] + s5_names
    w1 = Wave("chip1", 1, list(big_names), [(n, 1) for n in small1] + [("final_norm_g", None), ("loss", None)])
    early = Wave("chip0e", 0, big_names[1:], [("pool_w", 0), ("pool_scale", 0), ("ssm_b_glu", 0)])
    mid = Wave("chip0m", 0, [], [(n, 0) for n in ["ssm_c_re", "ssm_c_im", "ssm_d"] + s5_names] + [("norm_g", 1)])
    late = Wave("chip0l", 0, ["w_in"], [("b_in", 0)])

    mix_prev, gw_in = None, None
    for l in reversed(range(DEPTH)):
        proj, states, y0, pooled = saved[l]
        wg_glu, wg_a, wg_b, wg_out = wg_rest[l]
        res, moved = _mix_bwd(l, dx, proj, y0, pooled, wg_glu, ssm_b_glu, pool_w, pool_scale, wg_a, wg_b, wg_out,
                              mix_prev, carry=None if l == 1 else w1.to_chips(big=["w_in"], small=False))
        if l == 0:
            w1.landed(moved)
        dproj, dy0, dpooled = res[:3]
        mix_prev = list(res[3:7])
        grads["w_out"], grads["w_branch_a"], grads["w_branch_b"], grads["ssm_w_glu"] = mix_prev
        sm[("pool_w", l)], sm[("pool_scale", l)], sm[("ssm_b_glu", l)] = res[7:]
        dproj = _pool_bwd(l, dpooled, dproj)
        carry = None if l == 1 else _join(w1.to_chips(big=big_names[1:]), early.to_sibling())
        res, moved = _s5_scan_bwd(l, dy0, proj, states, *s5_args, dproj, carry=carry)
        if l == 0:
            early.add(w1.landed(moved))
        dproj, g_bbt_re, g_bbt_im, sm[("ssm_c_re", l)], sm[("ssm_c_im", l)], g_abar_re, g_abar_im, sm[("ssm_d", l)] = res
        s5_param_grads(l, g_abar_re, g_abar_im, g_bbt_re, g_bbt_im)
        carry = None if l == 1 else _join(early.to_chips(), mid.to_sibling())
        (gw_in, sm[("b_in", l)]), moved = _proj_wgrad(l, xs[l], norm_g, dproj, gw_in, carry=carry)
        grads["w_in"] = gw_in
        if l == 0:
            mid.add(early.landed(moved))
        carry = w1.to_sibling() if l == 1 else _join(mid.to_chips(), late.to_sibling())
        (dx, sm[("norm_g", l)]), moved = _proj_dgrad(l, dx, xs[l], norm_g, dproj, wg_in[l], carry=carry)
        if l == 1:
            w1.add(moved)
        else:
            late.add(mid.landed(moved))
    grad_x = dx.reshape(1, SEQ, D_MODEL)
    moved = late.landed(_run_carried("exchange_last", _join(late.to_chips(), _all_plan([sm[("norm_g", 0)]]))))
    slots[("norm_g", 0)] = moved[0]

    res = {}
    for n in big_names:
        res[n] = _sum_slots_adamw(n, [slots[(n, l)] for l in range(DEPTH)], weights[n], mom_m[n], mom_v[n])
    per_layer = lambda n: [slots[(n, l)] for l in range(DEPTH)]
    names_a = vec_names + ["ssm_lam_re", "ssm_lam_im"]
    entries_a = [(per_layer(n), weights[n], mom_m[n], mom_v[n], None, None) for n in names_a]
    n = "final_norm_g"
    entries_a.append((slots[(n, None)], weights[n], mom_m[n], mom_v[n], None, None))
    out_a, (loss,) = _adamw_small("small_a", entries_a, sums=[slots[("loss", None)]])
    loss = loss.reshape(())
    for n, r in zip(names_a + ["final_norm_g"], out_a):
        res[n] = r
    res["final_norm_g"] = tuple(a.reshape(D_MODEL) for a in res["final_norm_g"])
    pw_s = pl.BlockSpec((N_CHIP, 1, POOL_GROUP, POOL_GROUP), lambda j: (0, j, 0, 0))
    pw_w = pl.BlockSpec((DEPTH, 1, POOL_GROUP, POOL_GROUP), lambda j: (0, j, 0, 0))
    c_s = pl.BlockSpec((N_CHIP, CH_G, GROUP_W, STATE), lambda j: (0, j, 0, 0))
    c_w = pl.BlockSpec((DEPTH, CH_G, GROUP_W, STATE), lambda j: (0, j, 0, 0))
    entries_b = [(per_layer(n), weights[n], mom_m[n], mom_v[n], pw_s if n == "pool_w" else c_s,
                  pw_w if n == "pool_w" else c_w) for n in mat_names]
    out_b, _ = _adamw_small("small_b", entries_b, grid=(N_CHUNK,))
    for n, r in zip(mat_names, out_b):
        res[n] = tuple(b_t(a) for a in r) if n in ("ssm_b_re", "ssm_b_im") else r

    outs = [loss, grad_x]
    for i in range(4):
        outs += [res[n][i] for n in order]
    return tuple(outs)
```

```python
import math

import jax
import jax.numpy as jnp
from jax import lax
from jax.experimental import pallas as pl
from jax.experimental.pallas import tpu as pltpu

F32 = jnp.float32
BF16 = jnp.bfloat16

SEQ = 2048
D_MODEL = 1024
N_IN = 4096
WIDTH = 512
N_GROUP = 32
GROUP_W = 16
STATE = 64
N_STATE = N_GROUP * STATE
N_CHUNK = 4
CH_G = N_GROUP // N_CHUNK
CH_W = WIDTH // N_CHUNK
CH_S = N_STATE // N_CHUNK
N_DEV = 8
N_CHIP = 4
POOL_WINDOWS = (2, 4, 8, 16)
POOL_GROUP = 128
EPS = 1e-6
DEPTH = 2

ADAM_LR = 0.001
ADAM_B1 = 0.9
ADAM_B2 = 0.999
ADAM_EPS = 1e-08
ADAM_WD = 0.01
ADAM_STEP = 10

LANES = 128
SUBLANES = 8
TILE_M = 256
VMEM_LIMIT = 48 * 1024 * 1024
VMEM_LIMIT_BIG = 60 * 1024 * 1024
MESH = pl.DeviceIdType.MESH
ANY = pl.BlockSpec(memory_space=pl.ANY)

GELU_C = math.sqrt(2.0 / math.pi)
GELU_A = 0.044715

SDS = jax.ShapeDtypeStruct


def _cp(sem=None, limit=VMEM_LIMIT):
    return pltpu.CompilerParams(dimension_semantics=sem, vmem_limit_bytes=limit)


def _dot(a, b):
    return jnp.dot(a, b, preferred_element_type=F32)


def _dot_nt(a, b):
    return lax.dot_general(a, b, (((1,), (1,)), ((), ())), preferred_element_type=F32)


def _dot_tn(a, b):
    return lax.dot_general(a, b, (((0,), (0,)), ((), ())), preferred_element_type=F32)


def _sig(x):
    return jax.nn.sigmoid(x)


def _rms(x):
    rs = lax.rsqrt(jnp.mean(x * x, axis=-1, keepdims=True) + EPS)
    return rs, x * rs


def _slot(n):
    return 4 * (n % 2) + n // 2


def _const(shape):
    n = len(shape)
    return pl.BlockSpec(shape, lambda *_: (0,) * n)


def _pair_sum(vals):
    while len(vals) > 1:
        vals = [vals[i] + vals[i + 1] for i in range(0, len(vals), 2)]
    return vals[0]


def _sum_slots(s_ref):
    return _pair_sum([s_ref[k].astype(F32) for k in range(s_ref.shape[0])])


def _s5_param_fn(log_dt, lam_re, lam_im, bt_re, bt_im):
    dt = jnp.exp(log_dt)
    mag = jnp.exp(lam_re * dt)
    ang = lam_im * dt
    abar_re = mag * jnp.cos(ang)
    abar_im = mag * jnp.sin(ang)
    num_re = abar_re - 1.0
    num_im = abar_im
    den = lam_re * lam_re + lam_im * lam_im
    coef_re = (num_re * lam_re + num_im * lam_im) / den
    coef_im = (num_im * lam_re - num_re * lam_im) / den
    bbar_re = coef_re[..., None, :] * bt_re - coef_im[..., None, :] * bt_im
    bbar_im = coef_re[..., None, :] * bt_im + coef_im[..., None, :] * bt_re
    return abar_re, abar_im, bbar_re, bbar_im


def _s5_params(log_dt, lam_re, lam_im, bt_re, bt_im):
    def body(ld, lr, li, br, bi, o_ar, o_ai, o_br, o_bi):
        ar, ai, bbr, bbi = _s5_param_fn(ld[...], lr[...], li[...], br[...], bi[...])
        o_ar[...] = ar
        o_ai[...] = ai
        o_br[...] = bbr
        o_bi[...] = bbi

    return pl.pallas_call(
        body, name="s5_params",
        out_shape=(SDS(lam_re.shape, F32), SDS(lam_re.shape, F32), SDS(bt_re.shape, F32), SDS(bt_re.shape, F32)),
    )(log_dt, lam_re, lam_im, bt_re, bt_im)


def _s5_params_bwd(layer, log_dt, lam_re, lam_im, bt_re, bt_im, g_ar, g_ai, g_br, g_bi):
    def body(ld, lr, li, br, bi, car, cai, cbr, cbi, o_ld, o_lr, o_li, o_br, o_bi):
        _, vjp = jax.vjp(_s5_param_fn, ld[...], lr[...], li[...], br[...], bi[...])
        d_ld, d_lr, d_li, d_br, d_bi = vjp((car[...], cai[...], cbr[...], cbi[...]))
        o_ld[...] = d_ld
        o_lr[...] = d_lr
        o_li[...] = d_li
        o_br[...] = d_br
        o_bi[...] = d_bi

    one = lambda shape: pl.BlockSpec((None,) + shape, lambda i: (layer,) + (0,) * len(shape))
    whole = lambda shape: _const(shape)
    vec, lam, mat = (N_GROUP, 1), (N_GROUP, STATE), (N_GROUP, GROUP_W, STATE)
    return pl.pallas_call(
        body, name=f"s5_params_bwd_l{layer}", grid=(1,),
        in_specs=[one(vec), one(lam), one(lam), one(mat), one(mat), whole(lam), whole(lam), whole(mat), whole(mat)],
        out_specs=(whole(vec), whole(lam), whole(lam), whole(mat), whole(mat)),
        out_shape=(SDS(vec, F32), SDS(lam, F32), SDS(lam, F32), SDS(mat, F32), SDS(mat, F32)),
    )(log_dt, lam_re, lam_im, bt_re, bt_im, g_ar, g_ai, g_br, g_bi)


def _norm_proj(layer, x, norm_g, wg_in, b_in, carry=None):
    n_w = len(wg_in)

    def body(x_ref, g_ref, b_ref, *refs):
        w_refs, o_ref = refs[:n_w], refs[n_w]
        _, xn = _rms(x_ref[...])
        h = (xn * g_ref[layer:layer + 1, :]).astype(BF16)
        for k in range(N_DEV):
            cols = slice(k * WIDTH, (k + 1) * WIDTH)
            acc = b_ref[layer:layer + 1, cols]
            row = 0
            for w_ref in w_refs:
                rows = w_ref.shape[1]
                acc = acc + _dot(h[:, row:row + rows], w_ref[k])
                row += rows
            o_ref[:, cols] = acc

    (proj,), moved = _pcall(
        body, name=f"norm_proj_l{layer}",
        out_shape=[SDS((SEQ, N_IN), F32)],
        grid=(SEQ // TILE_M,),
        in_specs=[pl.BlockSpec((TILE_M, D_MODEL), lambda i: (i, 0)),
                  _const((DEPTH, D_MODEL)),
                  _const((DEPTH, N_IN))] + [_const(w.shape) for w in wg_in],
        out_specs=[pl.BlockSpec((TILE_M, N_IN), lambda i: (i, 0))],
        args=[x, norm_g, b_in, *wg_in], sem=("parallel",), carry=carry)
    return proj, moved


TIME_BLK = 512
N_TBLK = SEQ // TIME_BLK
N_PANEL = CH_S // LANES
STATE_SHAPE = (N_PANEL, SEQ * SUBLANES, LANES)


def _s5_layer_specs(layer):
    mat = lambda: pl.BlockSpec((None, N_GROUP, GROUP_W, STATE), lambda i: (layer, 0, 0, 0))
    ab = lambda: pl.BlockSpec((None, N_GROUP, STATE), lambda i: (layer, 0, 0))
    return [mat(), mat(), mat(), mat(), ab(), ab(), _const((DEPTH, WIDTH))]


def _s5_layer_scratch():
    return [pltpu.VMEM((N_CHUNK, CH_W, CH_S), BF16)] * 4 + [pltpu.VMEM((8, CH_S), F32)] * 2


def _s5_layer_fill(btre_ref, btim_ref, cre_ref, cim_ref, are_ref, aim_ref, bdre, bdim, ctre, ctim, a1, a2):
    for m in (bdre, bdim, ctre, ctim):
        m[...] = jnp.zeros_like(m)
    for grp in range(N_GROUP):
        k, g = divmod(grp, CH_G)
        rows = slice(g * GROUP_W, (g + 1) * GROUP_W)
        cols = slice(g * STATE, (g + 1) * STATE)
        bdre[k, rows, cols] = btre_ref[grp].astype(BF16)
        bdim[k, rows, cols] = btim_ref[grp].astype(BF16)
        ctre[k, rows, cols] = cre_ref[grp].astype(BF16)
        ctim[k, rows, cols] = cim_ref[grp].astype(BF16)
        ar = are_ref[grp:grp + 1, :]
        ai = aim_ref[grp:grp + 1, :]
        a1[k:k + 1, cols] = ar
        a1[N_CHUNK + k:N_CHUNK + k + 1, cols] = ar
        a2[k:k + 1, cols] = -ai
        a2[N_CHUNK + k:N_CHUNK + k + 1, cols] = ai


SCAN_UNROLL = 8


def _panels(tile):
    return [tile[:, p * LANES:(p + 1) * LANES] for p in range(N_PANEL)]


def _rows_load(ref, row):
    return jnp.concatenate([ref[p, pl.ds(row, TIME_BLK, stride=SUBLANES), :] for p in range(N_PANEL)], axis=1)


def _rows_store(ref, row, val):
    for p in range(N_PANEL):
        ref[p, pl.ds(row, TIME_BLK, stride=SUBLANES), :] = val[:, p * LANES:(p + 1) * LANES]


def _s5_scan_fwd(layer, proj, bbt_re, bbt_im, c_re, c_im, abar_re, abar_im, d_skip, carry=None):
    def body(u_ref, btre_ref, btim_ref, cre_ref, cim_ref, are_ref, aim_ref, d_ref, s_ref, y_ref,
             bdre, bdim, ctre, ctim, a1, a2, state):
        @pl.when(pl.program_id(0) == 0)
        def _():
            _s5_layer_fill(btre_ref, btim_ref, cre_ref, cim_ref, are_ref, aim_ref, bdre, bdim, ctre, ctim, a1, a2)
            state[...] = jnp.zeros_like(state)

        for k in range(N_CHUNK):
            ub = u_ref[:, k * CH_W:(k + 1) * CH_W].astype(BF16)
            _rows_store(s_ref, k, _dot(ub, bdre[k]))
            _rows_store(s_ref, N_CHUNK + k, _dot(ub, bdim[k]))
        m1 = _panels(a1[...])
        m2 = _panels(a2[...])

        def steps(n, tile):
            for r in range(SCAN_UNROLL):
                rows = pl.ds(pl.multiple_of((n * SCAN_UNROLL + r) * 8, 8), 8)
                tile = [m1[p] * tile[p] + m2[p] * pltpu.roll(tile[p], N_CHUNK, 0) + s_ref[p, rows, :]
                        for p in range(N_PANEL)]
                for p in range(N_PANEL):
                    s_ref[p, rows, :] = tile[p]
            return tile

        tile = lax.fori_loop(0, TIME_BLK // SCAN_UNROLL, steps, _panels(state[...]))
        state[...] = jnp.concatenate(tile, axis=1)
        d = d_ref[layer:layer + 1, :]
        for k in range(N_CHUNK):
            cols = slice(k * CH_W, (k + 1) * CH_W)
            y = (_dot_nt(_rows_load(s_ref, k).astype(BF16), ctre[k])
                 - _dot_nt(_rows_load(s_ref, N_CHUNK + k).astype(BF16), ctim[k]))
            y_ref[:, cols] = y + d[:, cols] * u_ref[:, cols]

    return _pcall(
        body, name=f"s5_fwd_l{layer}",
        out_shape=(SDS(STATE_SHAPE, F32), SDS((SEQ, WIDTH), F32)),
        grid=(N_TBLK,),
        in_specs=[pl.BlockSpec((TIME_BLK, WIDTH), lambda i: (i, 0))] + _s5_layer_specs(layer),
        out_specs=(pl.BlockSpec((N_PANEL, TIME_BLK * SUBLANES, LANES), lambda i: (0, i, 0)),
                   pl.BlockSpec((TIME_BLK, WIDTH), lambda i: (i, 0))),
        scratch_shapes=_s5_layer_scratch() + [pltpu.VMEM((8, CH_S), F32)],
        args=[proj, bbt_re, bbt_im, c_re, c_im, abar_re, abar_im, d_skip], sem=("arbitrary",), carry=carry)


def _s5_scan_bwd(layer, dy0, proj, states, bbt_re, bbt_im, c_re, c_im, abar_re, abar_im, d_skip, dproj,
                 carry=None):
    def body(dy_ref, u_ref, s_ref, sprev_ref, btre_ref, btim_ref, cre_ref, cim_ref, are_ref, aim_ref, d_ref, _,
             du_ref, gbre_ref, gbim_ref, gcre_ref, gcim_ref, gare_ref, gaim_ref, gd_ref,
             lam_ref, bdre, bdim, ctre, ctim, a1, a2, state, acc1, acc2, gbre, gbim, gcre, gcim, gd):
        step_id = pl.program_id(0)

        @pl.when(step_id == 0)
        def _():
            _s5_layer_fill(btre_ref, btim_ref, cre_ref, cim_ref, are_ref, aim_ref, bdre, bdim, ctre, ctim, a1, a2)
            for r in (state, acc1, acc2, gbre, gbim, gcre, gcim, gd):
                r[...] = jnp.zeros_like(r)

        for k in range(N_CHUNK):
            dyb = dy_ref[:, k * CH_W:(k + 1) * CH_W].astype(BF16)
            _rows_store(lam_ref, k, _dot(dyb, ctre[k]))
            _rows_store(lam_ref, N_CHUNK + k, -_dot(dyb, ctim[k]))
            gcre[k] += _dot_tn(dyb, _rows_load(s_ref, k).astype(BF16))
            gcim[k] -= _dot_tn(dyb, _rows_load(s_ref, N_CHUNK + k).astype(BF16))

        m1 = _panels(a1[...])
        m2 = _panels(-a2[...])
        has_before = (step_id < N_TBLK - 1).astype(F32)

        def one(t8, c, first_token):
            tile, swapped, p1, p2 = c
            rows = pl.ds(t8, 8)
            tile = [m1[p] * tile[p] + m2[p] * swapped[p] + lam_ref[p, rows, :] for p in range(N_PANEL)]
            swapped = [pltpu.roll(tile[p], N_CHUNK, 0) for p in range(N_PANEL)]
            for p in range(N_PANEL):
                lam_ref[p, rows, :] = tile[p]
            if first_token:
                before = [sprev_ref[p] * has_before for p in range(N_PANEL)]
            else:
                before = [s_ref[p, pl.ds(t8 - 8, 8), :] for p in range(N_PANEL)]
            p1 = [p1[p] + tile[p] * before[p] for p in range(N_PANEL)]
            p2 = [p2[p] + swapped[p] * before[p] for p in range(N_PANEL)]
            return tile, swapped, p1, p2

        def steps(n, c):
            for r in range(SCAN_UNROLL):
                t8 = pl.multiple_of((TIME_BLK - 1 - (n * SCAN_UNROLL + r)) * 8, 8)
                c = one(t8, c, False)
            return c

        tile0 = _panels(state[...])
        c = (tile0, [pltpu.roll(t, N_CHUNK, 0) for t in tile0], _panels(acc1[...]), _panels(acc2[...]))
        c = lax.fori_loop(0, TIME_BLK // SCAN_UNROLL - 1, steps, c)
        for r in range(SCAN_UNROLL - 1, -1, -1):
            c = one(r * 8, c, r == 0)
        state[...] = jnp.concatenate(c[0], axis=1)
        acc1[...] = jnp.concatenate(c[2], axis=1)
        acc2[...] = jnp.concatenate(c[3], axis=1)

        d = d_ref[layer:layer + 1, :]
        for k in range(N_CHUNK):
            cols = slice(k * CH_W, (k + 1) * CH_W)
            lrb = _rows_load(lam_ref, k).astype(BF16)
            lib = _rows_load(lam_ref, N_CHUNK + k).astype(BF16)
            u = u_ref[:, cols]
            ub = u.astype(BF16)
            dy = dy_ref[:, cols]
            du = dy * d[:, cols] + _dot_nt(lrb, bdre[k]) + _dot_nt(lib, bdim[k])
            du_ref[:, cols] = du.astype(BF16)
            gbre[k] += _dot_tn(ub, lrb)
            gbim[k] += _dot_tn(ub, lib)
        gd[...] += jnp.sum(dy_ref[...] * u_ref[...], axis=0, keepdims=True)

        @pl.when(step_id == N_TBLK - 1)
        def _():
            gd_ref[...] = gd[...]
            ga_re = acc1[0:N_CHUNK, :] + acc1[N_CHUNK:, :]
            ga_im = acc2[0:N_CHUNK, :] - acc2[N_CHUNK:, :]
            for grp in range(N_GROUP):
                k, g = divmod(grp, CH_G)
                rows = slice(g * GROUP_W, (g + 1) * GROUP_W)
                cols = slice(g * STATE, (g + 1) * STATE)
                gcre_ref[grp] = gcre[k, rows, cols]
                gcim_ref[grp] = gcim[k, rows, cols]
                gbre_ref[grp] = gbre[k, rows, cols]
                gbim_ref[grp] = gbim[k, rows, cols]
                gare_ref[grp:grp + 1, :] = ga_re[k:k + 1, cols]
                gaim_ref[grp:grp + 1, :] = ga_im[k:k + 1, cols]

    back = lambda i: N_TBLK - 1 - i
    tok = lambda: pl.BlockSpec((TIME_BLK, WIDTH), lambda i: (back(i), 0))
    mat = lambda: _const((N_GROUP, GROUP_W, STATE))
    acc_mat = pltpu.VMEM((N_CHUNK, CH_W, CH_S), F32)
    return _pcall(
        body, name=f"s5_bwd_l{layer}",
        out_shape=(SDS((SEQ, N_IN), BF16), SDS((N_GROUP, GROUP_W, STATE), F32), SDS((N_GROUP, GROUP_W, STATE), F32),
                   SDS((N_GROUP, GROUP_W, STATE), F32), SDS((N_GROUP, GROUP_W, STATE), F32),
                   SDS((N_GROUP, STATE), F32), SDS((N_GROUP, STATE), F32), SDS((1, WIDTH), F32)),
        grid=(N_TBLK,),
        in_specs=[tok(), tok(),
                  pl.BlockSpec((N_PANEL, TIME_BLK * SUBLANES, LANES), lambda i: (0, back(i), 0)),
                  pl.BlockSpec((N_PANEL, SUBLANES, LANES), lambda i: (0, jnp.maximum(back(i) * TIME_BLK - 1, 0), 0))]
        + _s5_layer_specs(layer) + [ANY],
        out_specs=(tok(), mat(), mat(), mat(), mat(), _const((N_GROUP, STATE)), _const((N_GROUP, STATE)),
                   _const((1, WIDTH))),
        scratch_shapes=[pltpu.VMEM((N_PANEL, TIME_BLK * SUBLANES, LANES), F32)] + _s5_layer_scratch()
        + [pltpu.VMEM((8, CH_S), F32)] * 3 + [acc_mat] * 4 + [pltpu.VMEM((1, WIDTH), F32)],
        args=[dy0, proj, states, states, bbt_re, bbt_im, c_re, c_im, abar_re, abar_im, d_skip, dproj],
        aliases={11: 0}, sem=("arbitrary",), limit=VMEM_LIMIT_BIG, carry=carry)


def _pool_counts(win):
    t = lax.broadcasted_iota(jnp.int32, (SEQ, POOL_GROUP), 0)
    return t, jnp.minimum(t + 1, win).astype(F32)


def _pool_fwd(layer, proj):
    def body(u_ref, o_ref):
        for gi, win in enumerate(POOL_WINDOWS):
            cols = slice(gi * POOL_GROUP, (gi + 1) * POOL_GROUP)
            u = u_ref[:, cols]
            t, count = _pool_counts(win)
            acc = u
            k = 1
            while k < win:
                acc = acc + jnp.where(t >= k, pltpu.roll(acc, k, 0), 0.0)
                k *= 2
            o_ref[:, cols] = acc / count - u

    return pl.pallas_call(
        body, name=f"pool_fwd_l{layer}",
        out_shape=SDS((SEQ, WIDTH), F32),
        grid=(1,),
        in_specs=[pl.BlockSpec((SEQ, WIDTH), lambda i: (0, 2))],
        out_specs=pl.BlockSpec((SEQ, WIDTH), lambda i: (0, 0)),
        compiler_params=_cp(("arbitrary",)),
    )(proj)


def _gelu_parts(y0):
    t = jnp.tanh(GELU_C * (y0 + GELU_A * (y0 * y0 * y0)))
    return t, 0.5 * y0 * (1.0 + t)


def _mix_forward(layer, p_ref, y0_ref, pooled_ref, wglu_ref, bglu_ref, pw_ref, scale_ref, wa_ref, wb_ref):
    za = p_ref[:, WIDTH:2 * WIDTH]
    zb = p_ref[:, 3 * WIDTH:4 * WIDTH]
    ga = p_ref[:, 4 * WIDTH:4 * WIDTH + D_MODEL]
    gb = p_ref[:, 4 * WIDTH + D_MODEL:]
    y0 = y0_ref[...]
    t, y1 = _gelu_parts(y0)
    y1b = y1.astype(BF16)
    q = _dot(y1b, wglu_ref[...].reshape(WIDTH, WIDTH)) + bglu_ref[layer:layer + 1, :]
    sq = _sig(q)
    y2 = y1 * sq
    sza = _sig(za)
    silu_za = za * sza
    ya = y2 * silu_za
    pooled = pooled_ref[...]
    mixed = jnp.concatenate(
        [_dot(pooled[:, g * POOL_GROUP:(g + 1) * POOL_GROUP].astype(BF16), pw_ref[g].astype(BF16))
         for g in range(len(POOL_WINDOWS))], axis=1)
    szb = _sig(zb)
    silu_zb = zb * szb
    scale = scale_ref[layer:layer + 1, :]
    ms = mixed * scale
    yb = ms * silu_zb
    yab = ya.astype(BF16)
    ybb = yb.astype(BF16)
    ma = _dot(yab, wa_ref[...])
    mb = _dot(ybb, wb_ref[...])
    sga = _sig(ga)
    sgb = _sig(gb)
    merged = sga * ma + sgb * mb
    return dict(za=za, zb=zb, y0=y0, t=t, y1=y1, y1b=y1b, sq=sq, y2=y2, sza=sza, silu_za=silu_za,
                pooled=pooled, mixed=mixed, szb=szb, silu_zb=silu_zb, scale=scale, ms=ms, yab=yab, ybb=ybb,
                ma=ma, mb=mb, sga=sga, sgb=sgb, merged=merged)


def _mix_weight_specs(layer):
    return [_const((N_DEV, WIDTH // N_DEV, WIDTH)),
            _const((DEPTH, WIDTH)),
            pl.BlockSpec((None, 4, POOL_GROUP, POOL_GROUP), lambda i: (layer, 0, 0, 0)),
            _const((DEPTH, WIDTH)),
            _const((WIDTH, D_MODEL)),
            _const((WIDTH, D_MODEL)),
            _const((N_DEV, D_MODEL // N_DEV, D_MODEL))]


def _mix_fwd(layer, x, proj, y0, pooled, wg_glu, b_glu, pool_w, pool_scale, wg_a, wg_b, wg_out, carry=None):
    def body(x_ref, p_ref, y0_ref, pooled_ref, wglu_ref, bglu_ref, pw_ref, scale_ref, wa_ref, wb_ref,
             wout_ref, o_ref):
        f = _mix_forward(layer, p_ref, y0_ref, pooled_ref, wglu_ref, bglu_ref, pw_ref, scale_ref, wa_ref, wb_ref)
        wout = wout_ref[...].reshape(D_MODEL, D_MODEL)
        o_ref[...] = x_ref[...] + _dot(f["merged"].astype(BF16), wout)

    (x_next,), moved = _pcall(
        body, name=f"mix_fwd_l{layer}",
        out_shape=[SDS((SEQ, D_MODEL), F32)],
        grid=(SEQ // TILE_M,),
        in_specs=[pl.BlockSpec((TILE_M, D_MODEL), lambda i: (i, 0)),
                  pl.BlockSpec((TILE_M, N_IN), lambda i: (i, 0)),
                  pl.BlockSpec((TILE_M, WIDTH), lambda i: (i, 0)),
                  pl.BlockSpec((TILE_M, WIDTH), lambda i: (i, 0))] + _mix_weight_specs(layer),
        out_specs=[pl.BlockSpec((TILE_M, D_MODEL), lambda i: (i, 0))],
        args=[x, proj, y0, pooled, wg_glu, b_glu, pool_w, pool_scale, wg_a, wg_b, wg_out],
        sem=("parallel",), carry=carry)
    return x_next, moved


def _loss_head(x, target, final_g):
    def body(x_ref, t_ref, g_ref, dx_ref, loss_ref, gg_ref):
        @pl.when(pl.program_id(0) == 0)
        def _():
            loss_ref[...] = jnp.zeros_like(loss_ref)
            gg_ref[...] = jnp.zeros_like(gg_ref)

        g = g_ref[...]
        rs, xn = _rms(x_ref[...])
        err = xn * g - t_ref[...]
        loss_ref[...] += 0.5 * jnp.sum(jnp.mean(err * err, axis=-1, keepdims=True), axis=0, keepdims=True)
        dy = err * (1.0 / D_MODEL)
        gg_ref[...] += jnp.sum(dy * xn, axis=0, keepdims=True)
        dxn = dy * g
        dx_ref[...] = rs * (dxn - xn * jnp.mean(dxn * xn, axis=-1, keepdims=True))

    return pl.pallas_call(
        body, name="loss_head",
        out_shape=(SDS((SEQ, D_MODEL), F32), SDS((1, 1), F32), SDS((1, D_MODEL), F32)),
        grid=(SEQ // TILE_M,),
        in_specs=[pl.BlockSpec((TILE_M, D_MODEL), lambda i: (i, 0)),
                  pl.BlockSpec((TILE_M, D_MODEL), lambda i: (i, 0)),
                  _const((1, D_MODEL))],
        out_specs=(pl.BlockSpec((TILE_M, D_MODEL), lambda i: (i, 0)), _const((1, 1)), _const((1, D_MODEL))),
        compiler_params=_cp(("arbitrary",)),
    )(x, target, final_g)


def _big_shapes():
    return dict(w_out=(DEPTH, N_DEV, D_MODEL // N_DEV, D_MODEL), w_branch_a=(DEPTH, N_DEV, WIDTH, D_MODEL // N_DEV),
                w_branch_b=(DEPTH, N_DEV, WIDTH, D_MODEL // N_DEV), ssm_w_glu=(DEPTH, N_DEV, WIDTH // N_DEV, WIDTH),
                w_in=(DEPTH, N_DEV, D_MODEL, WIDTH))


def _mix_bwd(layer, dx_next, proj, y0, pooled, wg_glu, b_glu, pool_w, pool_scale, wg_a, wg_b, wg_out, prev,
             carry=None):
    n_k = N_DEV
    n_prev = 0 if prev is None else len(prev)

    def body(*refs):
        (dx_ref, p_ref, y0_ref, pooled_ref, wglu_ref, bglu_ref, pw_ref, scale_ref, wa_ref, wb_ref,
         wout_ref) = refs[:11]
        (dproj_ref, dy0_ref, dpooled_ref, gwout_ref, gwa_ref, gwb_ref, gwglu_ref, gpw_ref,
         gscale_ref, gbglu_ref) = refs[11 + n_prev:]

        @pl.when(pl.program_id(0) == 0)
        def _():
            for r in (gwout_ref, gwa_ref, gwb_ref, gwglu_ref, gpw_ref, gscale_ref, gbglu_ref):
                r[...] = jnp.zeros_like(r)

        f = _mix_forward(layer, p_ref, y0_ref, pooled_ref, wglu_ref, bglu_ref, pw_ref, scale_ref, wa_ref, wb_ref)
        wglu = wglu_ref[...].reshape(WIDTH, WIDTH)
        wout = wout_ref[...].reshape(D_MODEL, D_MODEL)
        blk = D_MODEL // n_k
        dxb = dx_ref[...].astype(BF16)
        dmerged = _dot_nt(dxb, wout)
        gwout = _dot_tn(f["merged"].astype(BF16), dxb)
        for k in range(n_k):
            gwout_ref[_slot(k)] += gwout[k * blk:(k + 1) * blk, :]
        dma = dmerged * f["sga"]
        dmb = dmerged * f["sgb"]
        dga = dmerged * f["ma"] * f["sga"] * (1.0 - f["sga"])
        dgb = dmerged * f["mb"] * f["sgb"] * (1.0 - f["sgb"])
        dmab = dma.astype(BF16)
        dmbb = dmb.astype(BF16)
        dya = _dot_nt(dmab, wa_ref[...])
        dyb = _dot_nt(dmbb, wb_ref[...])
        gwa = _dot_tn(f["yab"], dmab)
        gwb = _dot_tn(f["ybb"], dmbb)
        for k in range(n_k):
            gwa_ref[_slot(k)] += gwa[:, k * blk:(k + 1) * blk]
            gwb_ref[_slot(k)] += gwb[:, k * blk:(k + 1) * blk]
        zb, szb = f["zb"], f["szb"]
        dzb = dyb * f["ms"] * (szb * (1.0 + zb * (1.0 - szb)))
        dms = dyb * f["silu_zb"]
        gscale_ref[...] += jnp.sum(dms * f["mixed"], axis=0, keepdims=True)
        dmixed = (dms * f["scale"]).astype(BF16)
        pooled = f["pooled"]
        for g in range(len(POOL_WINDOWS)):
            cols = slice(g * POOL_GROUP, (g + 1) * POOL_GROUP)
            dpooled_ref[:, cols] = _dot_nt(dmixed[:, cols], pw_ref[g].astype(BF16))
            gpw_ref[g] += _dot_tn(pooled[:, cols].astype(BF16), dmixed[:, cols])
        za, sza = f["za"], f["sza"]
        dza = dya * f["y2"] * (sza * (1.0 + za * (1.0 - sza)))
        dy2 = dya * f["silu_za"]
        sq = f["sq"]
        dq = dy2 * f["y1"] * sq * (1.0 - sq)
        dqb = dq.astype(BF16)
        dy1 = dy2 * sq + _dot_nt(dqb, wglu)
        gwglu = _dot_tn(f["y1b"], dqb)
        rblk = WIDTH // n_k
        for k in range(n_k):
            gwglu_ref[_slot(k)] += gwglu[k * rblk:(k + 1) * rblk, :]
        gbglu_ref[...] += jnp.sum(dq, axis=0, keepdims=True)
        y0, t = f["y0"], f["t"]
        dgelu = 0.5 * (1.0 + t) + 0.5 * y0 * (1.0 - t * t) * (GELU_C * (1.0 + 3.0 * GELU_A * y0 * y0))
        dy0_ref[...] = dy1 * dgelu
        zeros = jnp.zeros((TILE_M, WIDTH), BF16)
        dproj_ref[:, 0:WIDTH] = zeros
        dproj_ref[:, WIDTH:2 * WIDTH] = dza.astype(BF16)
        dproj_ref[:, 2 * WIDTH:3 * WIDTH] = zeros
        dproj_ref[:, 3 * WIDTH:4 * WIDTH] = dzb.astype(BF16)
        dproj_ref[:, 4 * WIDTH:4 * WIDTH + D_MODEL] = dga.astype(BF16)
        dproj_ref[:, 4 * WIDTH + D_MODEL:] = dgb.astype(BF16)

    tile = lambda w: pl.BlockSpec((TILE_M, w), lambda i: (i, 0))
    shapes = _big_shapes()
    big = ["w_out", "w_branch_a", "w_branch_b", "ssm_w_glu"]
    slab = lambda n: pl.BlockSpec((None,) + shapes[n][1:], lambda i: (layer, 0, 0, 0))
    args = [dx_next, proj, y0, pooled, wg_glu, b_glu, pool_w, pool_scale, wg_a, wg_b, wg_out]
    return _pcall(
        body, name=f"mix_bwd_l{layer}",
        out_shape=(SDS((SEQ, N_IN), BF16), SDS((SEQ, WIDTH), F32), SDS((SEQ, WIDTH), F32))
        + tuple(SDS(shapes[n], F32) for n in big)
        + (SDS((4, POOL_GROUP, POOL_GROUP), F32), SDS((1, WIDTH), F32), SDS((1, WIDTH), F32)),
        grid=(SEQ // TILE_M,),
        in_specs=[tile(D_MODEL), tile(N_IN), tile(WIDTH), tile(WIDTH)] + _mix_weight_specs(layer) + [ANY] * n_prev,
        out_specs=(tile(N_IN), tile(WIDTH), tile(WIDTH)) + tuple(slab(n) for n in big)
        + (_const((4, POOL_GROUP, POOL_GROUP)), _const((1, WIDTH)), _const((1, WIDTH))),
        args=args + list(prev or ()),
        aliases={len(args) + i: 3 + i for i in range(n_prev)},
        sem=("arbitrary",), limit=VMEM_LIMIT_BIG, carry=carry)


def _pool_bwd(layer, dpooled, dproj):
    def body(dp_ref, _, o_ref):
        for gi, win in enumerate(POOL_WINDOWS):
            cols = slice(gi * POOL_GROUP, (gi + 1) * POOL_GROUP)
            dp = dp_ref[:, cols]
            t, count = _pool_counts(win)
            e = dp / count
            acc = e
            k = 1
            while k < win:
                acc = acc + jnp.where(t < SEQ - k, pltpu.roll(acc, SEQ - k, 0), 0.0)
                k *= 2
            o_ref[:, cols] = (acc - dp).astype(BF16)

    return pl.pallas_call(
        body, name=f"pool_bwd_l{layer}",
        out_shape=SDS((SEQ, N_IN), BF16),
        grid=(1,),
        in_specs=[pl.BlockSpec((SEQ, WIDTH), lambda i: (0, 0)), ANY],
        out_specs=pl.BlockSpec((SEQ, WIDTH), lambda i: (0, 2)),
        input_output_aliases={1: 0},
        compiler_params=_cp(("arbitrary",)),
    )(dpooled, dproj)


def _proj_wgrad(layer, x, norm_g, dproj, rows_of=(0, D_MODEL), carry=None):
    tm = 512
    row0, n_rows = rows_of

    def body(*refs):
        x_ref, g_ref, dp_ref, gw_ref, gb_ref, ht_ref = refs
        n, t = pl.program_id(0), pl.program_id(1)

        @pl.when(t == 0)
        def _():
            gw_ref[...] = jnp.zeros_like(gw_ref)
            gb_ref[...] = jnp.zeros_like(gb_ref)

        @pl.when(n == 0)
        def _():
            _, xn = _rms(x_ref[...])
            h = xn * g_ref[layer:layer + 1, :]
            ht_ref[t] = h[:, row0:row0 + n_rows].T.astype(BF16)

        dp = dp_ref[...]
        gw_ref[...] += _dot(ht_ref[t], dp)
        gb_ref[...] += jnp.sum(dp.astype(F32), axis=0, keepdims=True)

    return _pcall(
        body, name=f"proj_wgrad_l{layer}_r{row0}",
        out_shape=(SDS((1, N_DEV, n_rows, WIDTH), F32), SDS((1, N_IN), F32)),
        grid=(N_DEV, SEQ // tm),
        in_specs=[pl.BlockSpec((tm, D_MODEL), lambda n, t: (jnp.where(n == 0, t, 0), 0)),
                  _const((DEPTH, D_MODEL)),
                  pl.BlockSpec((tm, WIDTH), lambda n, t: (t, n))],
        out_specs=(pl.BlockSpec((None, None, n_rows, WIDTH), lambda n, t: (0, _slot(n), 0, 0)),
                   pl.BlockSpec((1, WIDTH), lambda n, t: (0, n))),
        scratch_shapes=[pltpu.VMEM((SEQ // tm, n_rows, tm), BF16)],
        args=[x, norm_g, dproj], sem=("arbitrary", "arbitrary"), carry=carry)


def _proj_dgrad(layer, dx_next, x, norm_g, dproj, wg_in, carry=None):
    n_w = len(wg_in)

    def body(dxn_ref, x_ref, g_ref, dp_ref, *refs):
        w_refs, (dx_ref, gg_ref) = refs[:n_w], refs[n_w:]

        @pl.when(pl.program_id(0) == 0)
        def _():
            gg_ref[...] = jnp.zeros_like(gg_ref)

        parts = []
        for w_ref in w_refs:
            part = jnp.zeros((TILE_M, w_ref.shape[1]), F32)
            for k in range(N_DEV):
                part = part + _dot_nt(dp_ref[:, k * WIDTH:(k + 1) * WIDTH], w_ref[k])
            parts.append(part)
        dh = parts[0] if n_w == 1 else jnp.concatenate(parts, axis=1)
        rs, xn = _rms(x_ref[...])
        gg_ref[...] += jnp.sum(dh * xn, axis=0, keepdims=True)
        dxn = dh * g_ref[layer:layer + 1, :]
        dx_ref[...] = dxn_ref[...] + rs * (dxn - xn * jnp.mean(dxn * xn, axis=-1, keepdims=True))

    return _pcall(
        body, name=f"proj_dgrad_l{layer}",
        out_shape=(SDS((SEQ, D_MODEL), F32), SDS((1, D_MODEL), F32)),
        grid=(SEQ // TILE_M,),
        in_specs=[pl.BlockSpec((TILE_M, D_MODEL), lambda i: (i, 0)),
                  pl.BlockSpec((TILE_M, D_MODEL), lambda i: (i, 0)),
                  _const((DEPTH, D_MODEL)),
                  pl.BlockSpec((TILE_M, N_IN), lambda i: (i, 0))] + [_const(w.shape) for w in wg_in],
        out_specs=(pl.BlockSpec((TILE_M, D_MODEL), lambda i: (i, 0)), _const((1, D_MODEL))),
        args=[dx_next, x, norm_g, dproj, *wg_in], sem=("arbitrary",), carry=carry)


def _my_place():
    return lax.axis_index("x"), lax.axis_index("y"), lax.axis_index("c")


def _gather_plan(shards, layer, by_columns=(), rows_of=None):
    n = len(shards)

    def parts(ins, outs, sems):
        send_sems, recv_sems, local_sems = sems
        x, y, c = _my_place()
        chips = [(1 - x, y), (x, 1 - y), (1 - x, 1 - y)]

        def source(t):
            return ins[t].at[layer] if rows_of is None else ins[t].at[layer, pl.ds(*rows_of)]

        def rows(t, place):
            px, py, pc = place
            index = 4 * px + 2 * py + pc
            if t in by_columns:
                width = shards[t].shape[2]
                return outs[t].at[:, pl.ds(pl.multiple_of(index * width, LANES), width)]
            return outs[t].at[index]

        def copy(t, k, block, to, from_src=False):
            return pltpu.make_async_remote_copy(
                src_ref=source(t) if from_src else rows(t, block), dst_ref=rows(t, block),
                send_sem=send_sems.at[7 * t + k], recv_sem=recv_sems.at[7 * t + k], device_id=to,
                device_id_type=MESH)

        def mine(t):
            return pltpu.make_async_copy(source(t), rows(t, (x, y, c)), local_sems.at[t])

        return (x, y, c), chips, copy, mine

    def start(ins, outs, sems):
        me, chips, copy, mine = parts(ins, outs, sems)
        x, y, c = me
        for t in range(n):
            mine(t).start()
            copy(t, 0, me, (x, y, 1 - c), from_src=True).start()
            for j, chip in enumerate(chips):
                copy(t, 1 + j, me, (*chip, c), from_src=True).start()

    def relay(ins, outs, sems):
        me, chips, copy, mine = parts(ins, outs, sems)
        x, y, c = me
        for t in range(n):
            for j, chip in enumerate(chips):
                copy(t, 1 + j, (*chip, c), me).wait_recv()
                copy(t, 4 + j, (*chip, c), (x, y, 1 - c)).start()

    def finish(ins, outs, sems):
        me, chips, copy, mine = parts(ins, outs, sems)
        x, y, c = me
        sibling = (x, y, 1 - c)
        for t in range(n):
            copy(t, 0, sibling, me).wait_recv()
            for j, chip in enumerate(chips):
                copy(t, 4 + j, (*chip, 1 - c), me).wait_recv()
            for k in range(7):
                copy(t, k, me, sibling, from_src=k < 4).wait_send()
            mine(t).wait()

    n_rows = lambda a: a.shape[1] if rows_of is None else rows_of[1]
    out_shape = [SDS((a.shape[1], N_DEV * a.shape[2]) if t in by_columns else (N_DEV, n_rows(a), a.shape[2]), a.dtype)
                 for t, a in enumerate(shards)]
    sems = [pltpu.SemaphoreType.DMA((7 * n,)), pltpu.SemaphoreType.DMA((7 * n,)), pltpu.SemaphoreType.DMA((n,))]
    return _Carried(shards, out_shape, sems, start, finish, relay)


class _Carried:
    def __init__(self, ins, out_shape, sems, start, finish, relay=None):
        self.ins, self.out_shape, self.sems = list(ins), list(out_shape), list(sems)
        self.start, self.finish = start, finish
        self.relay = relay or (lambda ins, outs, sems: None)


def _pcall(body, *, name, grid, in_specs, out_specs, out_shape, args, scratch_shapes=(), aliases=None,
           sem=None, limit=VMEM_LIMIT, carry=None):
    out_shape, out_specs, scratch_shapes = list(out_shape), list(out_specs), list(scratch_shapes)
    n_in, n_out, n_scr = len(args), len(out_shape), len(scratch_shapes)
    if carry is None:
        kern, c_ins, c_out, c_sems = body, [], [], []
    else:
        c_ins, c_out, c_sems = carry.ins, carry.out_shape, carry.sems
        ci, co = len(c_ins), len(c_out)
        steps = tuple(grid)

        def kern(*refs):
            o0 = n_in + ci
            s0 = o0 + n_out + co
            mine = refs[:n_in] + refs[o0:o0 + n_out] + refs[s0:s0 + n_scr]
            theirs = (refs[n_in:o0], refs[o0 + n_out:s0], refs[s0 + n_scr:])
            first = pl.program_id(0) == 0
            last = pl.program_id(0) == steps[0] - 1
            for a in range(1, len(steps)):
                first = jnp.logical_and(first, pl.program_id(a) == 0)
                last = jnp.logical_and(last, pl.program_id(a) == steps[a] - 1)

            @pl.when(first)
            def _():
                carry.start(*theirs)

            @pl.when(last)
            def _():
                carry.relay(*theirs)

            body(*mine)

            @pl.when(last)
            def _():
                carry.finish(*theirs)

        sem = ("arbitrary",) * len(steps)
    res = pl.pallas_call(
        kern, name=name, grid=tuple(grid),
        in_specs=list(in_specs) + [ANY] * len(c_ins),
        out_specs=tuple(out_specs + [ANY] * len(c_out)),
        out_shape=tuple(out_shape + c_out),
        scratch_shapes=scratch_shapes + c_sems,
        input_output_aliases=aliases or {},
        compiler_params=_cp(sem, limit),
    )(*args, *c_ins)
    return res[:n_out], res[n_out:]


def _run_carried(name, carry):
    ci, co = len(carry.ins), len(carry.out_shape)

    def body(*refs):
        parts = (refs[:ci], refs[ci:ci + co], refs[ci + co:])
        carry.start(*parts)
        carry.relay(*parts)
        carry.finish(*parts)

    return pl.pallas_call(
        body, name=name, out_shape=tuple(carry.out_shape),
        in_specs=[ANY] * ci, out_specs=tuple([ANY] * co), scratch_shapes=carry.sems,
    )(*carry.ins)


def _sibling_plan(big, small):
    n = len(big)
    n_copies = 4 * n + len(small)

    def copies(ins, outs, sems):
        send_sems, recv_sems = sems
        x, y, c = _my_place()
        pairs = []
        for t, (_, layer, rows_of) in enumerate(big):
            for s in range(4):
                pairs.append((ins[t].at[layer, 4 * (1 - c) + s, pl.ds(*rows_of)], outs[t].at[s]))
        pairs += list(zip(ins[n:], outs[n:]))
        return [pltpu.make_async_remote_copy(
            src_ref=src, dst_ref=dst, send_sem=send_sems.at[k], recv_sem=recv_sems.at[k],
            device_id=(x, y, 1 - c), device_id_type=MESH) for k, (src, dst) in enumerate(pairs)]

    def start(ins, outs, sems):
        for cp in copies(ins, outs, sems):
            cp.start()

    def finish(ins, outs, sems):
        for cp in copies(ins, outs, sems):
            cp.wait()

    out_shape = ([SDS((4, rows_of[1], a.shape[3]), a.dtype) for a, _, rows_of in big]
                 + [SDS(a.shape, a.dtype) for a in small])
    sems = [pltpu.SemaphoreType.DMA((n_copies,)), pltpu.SemaphoreType.DMA((n_copies,))]
    return _Carried([a for a, _, _ in big] + list(small), out_shape, sems, start, finish)


def _chips_plan(big, small):
    n, n_small = len(big), len(small)
    max_rows = 512
    parts = [max(1, a.shape[1] // max_rows) for a in big]
    n_copies = 3 * (sum(parts) + n_small)

    def copies(ins, outs, sems, landing):
        send_sems, recv_sems, local_sems = sems
        x, y, c = _my_place()
        my_chip = 2 * x + y
        chips = [(1 - x, y), (x, 1 - y), (1 - x, 1 - y)]
        remote, local = [], []
        for chip in chips:
            to = 2 * chip[0] + chip[1]
            slot = to if landing else my_chip
            pairs = []
            for t in range(n):
                rows_per = big[t].shape[1] // parts[t]
                for p in range(parts[t]):
                    rows = pl.ds(p * rows_per, rows_per)
                    pairs.append((ins[t].at[to, rows], outs[t].at[slot, rows]))
            pairs += [(ins[t], outs[t].at[slot]) for t in range(n, n + n_small)]
            for src, dst in pairs:
                k = len(remote)
                remote.append(pltpu.make_async_remote_copy(
                    src_ref=src, dst_ref=dst, send_sem=send_sems.at[k], recv_sem=recv_sems.at[k],
                    device_id=(*chip, c), device_id_type=MESH))
        for t in range(n):
            local.append(pltpu.make_async_copy(ins[t].at[my_chip], outs[t].at[my_chip], local_sems.at[t]))
        for t in range(n, n + n_small):
            local.append(pltpu.make_async_copy(ins[t], outs[t].at[my_chip], local_sems.at[t]))
        return remote + local

    def start(ins, outs, sems):
        for cp in copies(ins, outs, sems, landing=False):
            cp.start()

    def finish(ins, outs, sems):
        for cp in copies(ins, outs, sems, landing=True):
            cp.wait()

    out_shape = [SDS(a.shape, a.dtype) for a in big] + [SDS((N_CHIP,) + a.shape, a.dtype) for a in small]
    sems = [pltpu.SemaphoreType.DMA((n_copies,)), pltpu.SemaphoreType.DMA((n_copies,)),
            pltpu.SemaphoreType.DMA((n + n_small,))]
    return _Carried(list(big) + list(small), out_shape, sems, start, finish)


def _all_plan(small):
    n = len(small)
    masks = [(m >> 2 & 1, m >> 1 & 1, m & 1) for m in range(1, N_DEV)]

    def copies(ins, outs, sems, landing):
        send_sems, recv_sems, local_sems = sems
        x, y, c = _my_place()
        me = 4 * x + 2 * y + c
        flip = lambda v, bit: 1 - v if bit else v
        remote = []
        for fx, fy, fc in masks:
            peer = (flip(x, fx), flip(y, fy), flip(c, fc))
            slot = 4 * peer[0] + 2 * peer[1] + peer[2] if landing else me
            for t in range(n):
                k = len(remote)
                remote.append(pltpu.make_async_remote_copy(
                    src_ref=ins[t], dst_ref=outs[t].at[slot], send_sem=send_sems.at[k], recv_sem=recv_sems.at[k],
                    device_id=peer, device_id_type=MESH))
        local = [pltpu.make_async_copy(ins[t], outs[t].at[me], local_sems.at[t]) for t in range(n)]
        return remote + local

    def start(ins, outs, sems):
        for cp in copies(ins, outs, sems, landing=False):
            cp.start()

    def finish(ins, outs, sems):
        for cp in copies(ins, outs, sems, landing=True):
            cp.wait()

    out_shape = [SDS((N_DEV,) + a.shape, a.dtype) for a in small]
    sems = [pltpu.SemaphoreType.DMA((7 * n,)), pltpu.SemaphoreType.DMA((7 * n,)), pltpu.SemaphoreType.DMA((n,))]
    return _Carried(list(small), out_shape, sems, start, finish)


def _join(*plans):
    plans = [p for p in plans if p is not None]
    if len(plans) <= 1:
        return plans[0] if plans else None

    def each(fn_name, ins, outs, sems):
        i = o = s = 0
        for p in plans:
            ni, no, ns = len(p.ins), len(p.out_shape), len(p.sems)
            getattr(p, fn_name)(ins[i:i + ni], outs[o:o + no], sems[s:s + ns])
            i, o, s = i + ni, o + no, s + ns

    return _Carried(sum((p.ins for p in plans), []), sum((p.out_shape for p in plans), []),
                    sum((p.sems for p in plans), []),
                    lambda i, o, s: each("start", i, o, s), lambda i, o, s: each("finish", i, o, s),
                    lambda i, o, s: each("relay", i, o, s))


def _row_block(rows, most=256):
    return min(rows, most)


def _add_own(tag, core, g, layer, got, row0=0):
    _, r, c = got.shape
    rb = _row_block(r, most=1024)
    first = row0 // rb

    def body(core_ref, a_ref, b_ref, o_ref):
        o_ref[...] = (a_ref[...] + b_ref[...]).astype(o_ref.dtype)

    return pl.pallas_call(
        body, name=f"add_{tag}", out_shape=SDS(got.shape, BF16),
        grid_spec=pltpu.PrefetchScalarGridSpec(
            num_scalar_prefetch=1, grid=(4, r // rb),
            in_specs=[pl.BlockSpec((None, None, rb, c), lambda s, j, core: (layer, 4 * core[0] + s, first + j, 0)),
                      pl.BlockSpec((None, rb, c), lambda s, j, core: (s, j, 0))],
            out_specs=pl.BlockSpec((None, rb, c), lambda s, j, core: (s, j, 0))),
        compiler_params=_cp(("parallel", "parallel")),
    )(core, g, got)


def _add_lists(tag, own, got, grid=None, specs=None, dtype=F32):
    n = len(own)

    def body(*refs):
        for a, b, o in zip(refs[:n], refs[n:2 * n], refs[2 * n:]):
            o[...] = (a[...] + b[...]).astype(o.dtype)

    kw = {}
    if grid is not None:
        kw = dict(grid=grid, in_specs=list(specs) * 2, out_specs=tuple(specs),
                  compiler_params=_cp(("parallel",) * len(grid)))
    return pl.pallas_call(
        body, name=f"add_{tag}", out_shape=tuple(SDS(a.shape, dtype) for a in own), **kw)(*own, *got)


def _adamw_math(w, g, m, v):
    m = ADAM_B1 * m + (1.0 - ADAM_B1) * g
    v = ADAM_B2 * v + (1.0 - ADAM_B2) * (g * g)
    m_hat = m / (1.0 - ADAM_B1 ** ADAM_STEP)
    v_hat = v / (1.0 - ADAM_B2 ** ADAM_STEP)
    delta = -ADAM_LR * (m_hat / (jnp.sqrt(v_hat) + ADAM_EPS) + ADAM_WD * w)
    return delta, m, v


def _sum_slots_adamw(tag, slots, w, m, v):
    _, r, c = slots[0].shape
    rb = _row_block(r)

    def body(s0_ref, s1_ref, w_ref, m_ref, v_ref, g_ref, d_ref, nm_ref, nv_ref):
        first = pl.program_id(1) == 0
        g = _pair_sum([jnp.where(first, s0_ref[k], s1_ref[k]).astype(F32) for k in range(N_CHIP)])
        delta, nm, nv = _adamw_math(w_ref[...], g, m_ref[...], v_ref[...])
        g_ref[...] = g
        d_ref[...] = delta
        nm_ref[...] = nm
        nv_ref[...] = nv

    spec = pl.BlockSpec((None, rb, c), lambda j, l: (l, j, 0))
    sspec = pl.BlockSpec((N_CHIP, rb, c), lambda j, l: (0, j, 0))
    s = SDS((DEPTH, r, c), F32)
    return pl.pallas_call(
        body, name=f"adamw_{tag}", out_shape=(s, s, s, s),
        grid=(r // rb, DEPTH), in_specs=[sspec, sspec, spec, spec, spec], out_specs=(spec, spec, spec, spec),
        compiler_params=_cp(("parallel", "arbitrary")),
    )(*slots, w, m, v)


def _adamw_small(tag, entries, grid=None, sums=()):
    flat_in, in_specs, out_shape, out_specs, layout = [], [], [], [], []
    for slots, w, m, v, slot_spec, w_spec in entries:
        per_layer = isinstance(slots, (list, tuple))
        n_slot = len(slots) if per_layer else 1
        flat_in += (list(slots) if per_layer else [slots]) + [w, m, v]
        in_specs += [slot_spec] * n_slot + [w_spec] * 3
        out_shape += [SDS(w.shape, F32)] * 4
        out_specs += [w_spec] * 4
        layout.append((per_layer, n_slot))
    n_entry_in = len(flat_in)
    flat_in += list(sums)
    out_shape += [SDS(s.shape[1:], F32) for s in sums]
    n_in = len(flat_in)

    def body(*refs):
        for s_ref, o_ref in zip(refs[n_entry_in:n_in], refs[len(refs) - len(sums):]):
            o_ref[...] = _sum_slots(s_ref)
        i, o = 0, n_in
        for per_layer, n_slot in layout:
            s_refs = refs[i:i + n_slot]
            w_ref, m_ref, v_ref = refs[i + n_slot:i + n_slot + 3]
            outs = refs[o:o + 4]
            if per_layer:
                for l, s_ref in enumerate(s_refs):
                    at = (slice(l, l + 1),) if len(w_ref.shape) == 2 else (l,)
                    g = _sum_slots(s_ref)
                    res = (g,) + _adamw_math(w_ref[at], g, m_ref[at], v_ref[at])
                    for o_ref, val in zip(outs, res):
                        o_ref[at] = val
            else:
                g = _sum_slots(s_refs[0])
                res = (g,) + _adamw_math(w_ref[...], g, m_ref[...], v_ref[...])
                for o_ref, val in zip(outs, res):
                    o_ref[...] = val
            i += n_slot + 3
            o += 4

    kw = {}
    if grid is not None:
        kw = dict(grid=grid, in_specs=in_specs, out_specs=tuple(out_specs),
                  compiler_params=_cp(("parallel",) * len(grid)))
    res = pl.pallas_call(body, name=f"adamw_{tag}", out_shape=tuple(out_shape), **kw)(*flat_in)
    return [tuple(res[4 * e:4 * e + 4]) for e in range(len(entries))], res[4 * len(entries):]


def kernel(x, norm_g, w_in, b_in, ssm_log_dt, ssm_lam_re, ssm_lam_im, ssm_b_re, ssm_b_im, ssm_c_re, ssm_c_im, ssm_d, ssm_w_glu, ssm_b_glu, pool_w, pool_scale, w_branch_a, w_branch_b, w_out, final_norm_g, loss_target, m_norm_g, m_w_in, m_b_in, m_ssm_log_dt, m_ssm_lam_re, m_ssm_lam_im, m_ssm_b_re, m_ssm_b_im, m_ssm_c_re, m_ssm_c_im, m_ssm_d, m_ssm_w_glu, m_ssm_b_glu, m_pool_w, m_pool_scale, m_w_branch_a, m_w_branch_b, m_w_out, m_final_norm_g, v_norm_g, v_w_in, v_b_in, v_ssm_log_dt, v_ssm_lam_re, v_ssm_lam_im, v_ssm_b_re, v_ssm_b_im, v_ssm_c_re, v_ssm_c_im, v_ssm_d, v_ssm_w_glu, v_ssm_b_glu, v_pool_w, v_pool_scale, v_w_branch_a, v_w_branch_b, v_w_out, v_final_norm_g):
    weights = dict(norm_g=norm_g, w_in=w_in, b_in=b_in, ssm_log_dt=ssm_log_dt, ssm_lam_re=ssm_lam_re,
                   ssm_lam_im=ssm_lam_im, ssm_b_re=ssm_b_re, ssm_b_im=ssm_b_im, ssm_c_re=ssm_c_re,
                   ssm_c_im=ssm_c_im, ssm_d=ssm_d, ssm_w_glu=ssm_w_glu, ssm_b_glu=ssm_b_glu, pool_w=pool_w,
                   pool_scale=pool_scale, w_branch_a=w_branch_a, w_branch_b=w_branch_b, w_out=w_out,
                   final_norm_g=final_norm_g.reshape(1, D_MODEL))
    mom_m = dict(norm_g=m_norm_g, w_in=m_w_in, b_in=m_b_in, ssm_log_dt=m_ssm_log_dt, ssm_lam_re=m_ssm_lam_re,
                 ssm_lam_im=m_ssm_lam_im, ssm_b_re=m_ssm_b_re, ssm_b_im=m_ssm_b_im, ssm_c_re=m_ssm_c_re,
                 ssm_c_im=m_ssm_c_im, ssm_d=m_ssm_d, ssm_w_glu=m_ssm_w_glu, ssm_b_glu=m_ssm_b_glu,
                 pool_w=m_pool_w, pool_scale=m_pool_scale, w_branch_a=m_w_branch_a, w_branch_b=m_w_branch_b,
                 w_out=m_w_out, final_norm_g=m_final_norm_g.reshape(1, D_MODEL))
    mom_v = dict(norm_g=v_norm_g, w_in=v_w_in, b_in=v_b_in, ssm_log_dt=v_ssm_log_dt, ssm_lam_re=v_ssm_lam_re,
                 ssm_lam_im=v_ssm_lam_im, ssm_b_re=v_ssm_b_re, ssm_b_im=v_ssm_b_im, ssm_c_re=v_ssm_c_re,
                 ssm_c_im=v_ssm_c_im, ssm_d=v_ssm_d, ssm_w_glu=v_ssm_w_glu, ssm_b_glu=v_ssm_b_glu,
                 pool_w=v_pool_w, pool_scale=v_pool_scale, w_branch_a=v_w_branch_a, w_branch_b=v_w_branch_b,
                 w_out=v_w_out, final_norm_g=v_final_norm_g.reshape(1, D_MODEL))
    order = ["norm_g", "w_in", "b_in", "ssm_log_dt", "ssm_lam_re", "ssm_lam_im", "ssm_b_re", "ssm_b_im",
             "ssm_c_re", "ssm_c_im", "ssm_d", "ssm_w_glu", "ssm_b_glu", "pool_w", "pool_scale", "w_branch_a",
             "w_branch_b", "w_out", "final_norm_g"]
    big_names = ["w_in", "ssm_w_glu", "w_branch_a", "w_branch_b", "w_out"]

    log_dt3 = ssm_log_dt.reshape(DEPTH, N_GROUP, 1)
    b_t = lambda a: a.transpose(0, 1, 3, 2)
    for d in (weights, mom_m, mom_v):
        d["ssm_b_re"], d["ssm_b_im"] = b_t(d["ssm_b_re"]), b_t(d["ssm_b_im"])
    bt_re, bt_im = weights["ssm_b_re"], weights["ssm_b_im"]
    abar_re, abar_im, bbt_re, bbt_im = _s5_params(log_dt3, ssm_lam_re, ssm_lam_im, bt_re, bt_im)
    s5_args = (bbt_re, bbt_im, ssm_c_re, ssm_c_im, abar_re, abar_im, ssm_d)

    w16 = {n: weights[n].astype(BF16) for n in big_names}
    rest = [w16[n] for n in big_names[1:]]
    half = D_MODEL // 2
    wg_in = [None, [None, None]]
    wg_rest = [None, None]
    wg_in[0] = list(_run_carried("gather_w_in_l0", _gather_plan([w16["w_in"]], 0)))
    xs = [x.reshape(SEQ, D_MODEL)]
    saved = []
    for l in range(DEPTH):
        proj, moved = _norm_proj(l, xs[l], norm_g, wg_in[l], b_in,
                                 carry=_gather_plan([w16["w_in"]], 1, rows_of=(0, half)) if l == 0 else None)
        if l == 0:
            (wg_in[1][0],) = moved
        (states, y0), wg_rest[l] = _s5_scan_fwd(l, proj, *s5_args, carry=_gather_plan(rest, l, by_columns=(1, 2)))
        pooled = _pool_fwd(l, proj)
        wg_glu, wg_a, wg_b, wg_out = wg_rest[l]
        x_next, moved = _mix_fwd(l, xs[l], proj, y0, pooled, wg_glu, ssm_b_glu, pool_w, pool_scale, wg_a, wg_b,
                                 wg_out, carry=_gather_plan([w16["w_in"]], 1, rows_of=(half, half)) if l == 0 else None)
        if l == 0:
            (wg_in[1][1],) = moved
        xs.append(x_next)
        saved.append((proj, states, y0, pooled))

    dx, loss_part, g_final = _loss_head(xs[DEPTH], loss_target.reshape(SEQ, D_MODEL), weights["final_norm_g"])

    core = lax.axis_index("c").astype(jnp.int32).reshape(1)
    vec_names = ["norm_g", "b_in", "ssm_d", "ssm_b_glu", "pool_scale", "ssm_log_dt"]
    s5_names = ["ssm_log_dt", "ssm_lam_re", "ssm_lam_im", "ssm_b_re", "ssm_b_im"]
    mat_names = ["pool_w", "ssm_c_re", "ssm_c_im", "ssm_b_re", "ssm_b_im"]
    lane_sparse = ("ssm_c_re", "ssm_c_im", "ssm_b_re", "ssm_b_im")

    def dense(key, a):
        return a.reshape(-1, LANES) if key[0] in lane_sparse else a

    def undense(key, slots):
        return slots.reshape((N_CHIP, N_GROUP, GROUP_W, STATE)) if key[0] in lane_sparse else slots

    def add_small(tag, keys, own, got):
        out = [None] * len(keys)
        whole = [i for i, k in enumerate(keys) if k[0] not in mat_names]
        tiled = [i for i, k in enumerate(keys) if k[0] in mat_names]
        if whole:
            for i, r in zip(whole, _add_lists(f"{tag}_a", [own[i] for i in whole], [got[i] for i in whole])):
                out[i] = r
        if tiled:
            specs = [pl.BlockSpec((1, POOL_GROUP, POOL_GROUP), lambda j: (j, 0, 0)) if keys[i][0] == "pool_w"
                     else pl.BlockSpec((own[i].shape[0] // N_CHUNK, LANES), lambda j: (j, 0)) for i in tiled]
            for i, r in zip(tiled, _add_lists(f"{tag}_b", [own[i] for i in tiled], [got[i] for i in tiled],
                                              grid=(N_CHUNK,), specs=specs, dtype=BF16)):
                out[i] = r
        return out

    sm = {("final_norm_g", None): g_final, ("loss", None): loss_part}
    slots = {}
    grads = {}

    class Wave:
        def __init__(self, tag, layer, big, keys):
            self.tag, self.layer, self.big, self.keys = tag, layer, big, keys

        def to_sibling(self):
            self.own = [dense(k, sm[k]) for k in self.keys]
            return _sibling_plan([(grads[n], self.slab(n), (0, grads[n].shape[2])) for n in self.big], self.own)

        def slab(self, n):
            return self.layer if grads[n].shape[0] == DEPTH else 0

        def add(self, moved):
            nb = len(self.big)
            self.chip_big = [_add_own(f"{self.tag}_{n}", core, grads[n], self.slab(n), b)
                             for n, b in zip(self.big, moved[:nb])]
            self.chip_small = add_small(self.tag, self.keys, self.own, moved[nb:])

        def to_chips(self, big=None, small=True):
            self.sent = list(self.big if big is None else big), small
            return _chips_plan([self.chip_big[self.big.index(n)] for n in self.sent[0]],
                               self.chip_small if small else [])

        def landed(self, moved):
            names, small = self.sent
            for n, s in zip(names, moved[:len(names)]):
                slots[(n, self.layer)] = s
            if small:
                for k, s in zip(self.keys, moved[len(names):]):
                    slots[k] = undense(k, s)
            return moved[len(names) + (len(self.keys) if small else 0):]

    def s5_param_grads(l, g_abar_re, g_abar_im, g_bbt_re, g_bbt_im):
        g = _s5_params_bwd(l, log_dt3, ssm_lam_re, ssm_lam_im, bt_re, bt_im, g_abar_re, g_abar_im, g_bbt_re, g_bbt_im)
        sm[("ssm_log_dt", l)] = g[0].reshape(1, N_GROUP)
        for n, a in zip(s5_names[1:], g[1:]):
            sm[(n, l)] = a

    small1 = ["b_in", "ssm_d", "ssm_b_glu", "pool_scale", "pool_w", "ssm_c_re", "ssm_c_im"] + s5_names
    w1 = Wave("chip1", 1, list(big_names), [(n, 1) for n in small1] + [("final_norm_g", None), ("loss", None)])
    early = Wave("chip0e", 0, big_names[1:], [("pool_w", 0), ("pool_scale", 0), ("ssm_b_glu", 0)])
    mid = Wave("chip0m", 0, [], [(n, 0) for n in ["ssm_c_re", "ssm_c_im", "ssm_d"] + s5_names] + [("norm_g", 1)])
    half_rows = D_MODEL // 2
    late_a = Wave("chip0la", 0, ["w_in_a"], [("b_in", 0)])
    late_b = Wave("chip0lb", 0, ["w_in_b"], [])

    mix_prev = None
    for l in reversed(range(DEPTH)):
        proj, states, y0, pooled = saved[l]
        wg_glu, wg_a, wg_b, wg_out = wg_rest[l]
        res, moved = _mix_bwd(l, dx, proj, y0, pooled, wg_glu, ssm_b_glu, pool_w, pool_scale, wg_a, wg_b, wg_out,
                              mix_prev, carry=None if l == 1 else w1.to_chips(big=["w_in"], small=False))
        if l == 0:
            w1.landed(moved)
        dproj, dy0, dpooled = res[:3]
        mix_prev = list(res[3:7])
        grads["w_out"], grads["w_branch_a"], grads["w_branch_b"], grads["ssm_w_glu"] = mix_prev
        sm[("pool_w", l)], sm[("pool_scale", l)], sm[("ssm_b_glu", l)] = res[7:]
        dproj = _pool_bwd(l, dpooled, dproj)
        carry = None if l == 1 else _join(w1.to_chips(big=big_names[1:]), early.to_sibling())
        res, moved = _s5_scan_bwd(l, dy0, proj, states, *s5_args, dproj, carry=carry)
        if l == 0:
            early.add(w1.landed(moved))
        dproj, g_bbt_re, g_bbt_im, sm[("ssm_c_re", l)], sm[("ssm_c_im", l)], g_abar_re, g_abar_im, sm[("ssm_d", l)] = res
        s5_param_grads(l, g_abar_re, g_abar_im, g_bbt_re, g_bbt_im)
        if l == 1:
            (grads["w_in"], sm[("b_in", l)]), _ = _proj_wgrad(l, xs[l], norm_g, dproj)
            carry = w1.to_sibling()
        else:
            (grads["w_in_a"], sm[("b_in", l)]), moved = _proj_wgrad(
                l, xs[l], norm_g, dproj, (0, half_rows), carry=_join(early.to_chips(), mid.to_sibling()))
            mid.add(early.landed(moved))
            (grads["w_in_b"], _), moved = _proj_wgrad(
                l, xs[l], norm_g, dproj, (half_rows, half_rows), carry=_join(mid.to_chips(), late_a.to_sibling()))
            late_a.add(mid.landed(moved))
            carry = _join(late_a.to_chips(), late_b.to_sibling())
        (dx, sm[("norm_g", l)]), moved = _proj_dgrad(l, dx, xs[l], norm_g, dproj, wg_in[l], carry=carry)
        if l == 1:
            w1.add(moved)
        else:
            late_b.add(late_a.landed(moved))
    grad_x = dx.reshape(1, SEQ, D_MODEL)
    moved = late_b.landed(_run_carried("exchange_last", _join(late_b.to_chips(), _all_plan([sm[("norm_g", 0)]]))))
    slots[("norm_g", 0)] = moved[0]
    slots[("w_in", 0)] = jnp.concatenate([slots[("w_in_a", 0)], slots[("w_in_b", 0)]], axis=1)

    res = {}
    for n in big_names:
        res[n] = _sum_slots_adamw(n, [slots[(n, l)] for l in range(DEPTH)], weights[n], mom_m[n], mom_v[n])
    per_layer = lambda n: [slots[(n, l)] for l in range(DEPTH)]
    names_a = vec_names + ["ssm_lam_re", "ssm_lam_im"]
    entries_a = [(per_layer(n), weights[n], mom_m[n], mom_v[n], None, None) for n in names_a]
    n = "final_norm_g"
    entries_a.append((slots[(n, None)], weights[n], mom_m[n], mom_v[n], None, None))
    out_a, (loss,) = _adamw_small("small_a", entries_a, sums=[slots[("loss", None)]])
    loss = loss.reshape(())
    for n, r in zip(names_a + ["final_norm_g"], out_a):
        res[n] = r
    res["final_norm_g"] = tuple(a.reshape(D_MODEL) for a in res["final_norm_g"])
    pw_s = pl.BlockSpec((N_CHIP, 1, POOL_GROUP, POOL_GROUP), lambda j: (0, j, 0, 0))
    pw_w = pl.BlockSpec((DEPTH, 1, POOL_GROUP, POOL_GROUP), lambda j: (0, j, 0, 0))
    c_s = pl.BlockSpec((N_CHIP, CH_G, GROUP_W, STATE), lambda j: (0, j, 0, 0))
    c_w = pl.BlockSpec((DEPTH, CH_G, GROUP_W, STATE), lambda j: (0, j, 0, 0))
    entries_b = [(per_layer(n), weights[n], mom_m[n], mom_v[n], pw_s if n == "pool_w" else c_s,
                  pw_w if n == "pool_w" else c_w) for n in mat_names]
    out_b, _ = _adamw_small("small_b", entries_b, grid=(N_CHUNK,))
    for n, r in zip(mat_names, out_b):
        res[n] = tuple(b_t(a) for a in r) if n in ("ssm_b_re", "ssm_b_im") else r

    outs = [loss, grad_x]
    for i in range(4):
        outs += [res[n][i] for n in order]
    return tuple(outs)
```

```python
import math

import jax
import jax.numpy as jnp
from jax import lax
from jax.experimental import pallas as pl
from jax.experimental.pallas import tpu as pltpu

F32 = jnp.float32
BF16 = jnp.bfloat16

SEQ = 2048
D_MODEL = 1024
N_IN = 4096
WIDTH = 512
N_GROUP = 32
GROUP_W = 16
STATE = 64
N_STATE = N_GROUP * STATE
N_CHUNK = 4
CH_G = N_GROUP // N_CHUNK
CH_W = WIDTH // N_CHUNK
CH_S = N_STATE // N_CHUNK
N_DEV = 8
N_CHIP = 4
POOL_WINDOWS = (2, 4, 8, 16)
POOL_GROUP = 128
EPS = 1e-6
DEPTH = 2

ADAM_LR = 0.001
ADAM_B1 = 0.9
ADAM_B2 = 0.999
ADAM_EPS = 1e-08
ADAM_WD = 0.01
ADAM_STEP = 10

LANES = 128
SUBLANES = 8
TILE_M = 256
VMEM_LIMIT = 48 * 1024 * 1024
VMEM_LIMIT_BIG = 60 * 1024 * 1024
MESH = pl.DeviceIdType.MESH
ANY = pl.BlockSpec(memory_space=pl.ANY)

GELU_C = math.sqrt(2.0 / math.pi)
GELU_A = 0.044715

SDS = jax.ShapeDtypeStruct


def _cp(sem=None, limit=VMEM_LIMIT):
    return pltpu.CompilerParams(dimension_semantics=sem, vmem_limit_bytes=limit)


def _dot(a, b):
    return jnp.dot(a, b, preferred_element_type=F32)


def _dot_nt(a, b):
    return lax.dot_general(a, b, (((1,), (1,)), ((), ())), preferred_element_type=F32)


def _dot_tn(a, b):
    return lax.dot_general(a, b, (((0,), (0,)), ((), ())), preferred_element_type=F32)


def _sig(x):
    return jax.nn.sigmoid(x)


def _rms(x):
    rs = lax.rsqrt(jnp.mean(x * x, axis=-1, keepdims=True) + EPS)
    return rs, x * rs


def _slot(n):
    return 4 * (n % 2) + n // 2


def _const(shape):
    n = len(shape)
    return pl.BlockSpec(shape, lambda *_: (0,) * n)


def _pair_sum(vals):
    while len(vals) > 1:
        vals = [vals[i] + vals[i + 1] for i in range(0, len(vals), 2)]
    return vals[0]


def _sum_slots(s_ref):
    return _pair_sum([s_ref[k].astype(F32) for k in range(s_ref.shape[0])])


def _s5_param_fn(log_dt, lam_re, lam_im, bt_re, bt_im):
    dt = jnp.exp(log_dt)
    mag = jnp.exp(lam_re * dt)
    ang = lam_im * dt
    abar_re = mag * jnp.cos(ang)
    abar_im = mag * jnp.sin(ang)
    num_re = abar_re - 1.0
    num_im = abar_im
    den = lam_re * lam_re + lam_im * lam_im
    coef_re = (num_re * lam_re + num_im * lam_im) / den
    coef_im = (num_im * lam_re - num_re * lam_im) / den
    bbar_re = coef_re[..., None, :] * bt_re - coef_im[..., None, :] * bt_im
    bbar_im = coef_re[..., None, :] * bt_im + coef_im[..., None, :] * bt_re
    return abar_re, abar_im, bbar_re, bbar_im


def _s5_params(log_dt, lam_re, lam_im, bt_re, bt_im):
    def body(ld, lr, li, br, bi, o_ar, o_ai, o_br, o_bi):
        ar, ai, bbr, bbi = _s5_param_fn(ld[...], lr[...], li[...], br[...], bi[...])
        o_ar[...] = ar
        o_ai[...] = ai
        o_br[...] = bbr
        o_bi[...] = bbi

    return pl.pallas_call(
        body, name="s5_params",
        out_shape=(SDS(lam_re.shape, F32), SDS(lam_re.shape, F32), SDS(bt_re.shape, F32), SDS(bt_re.shape, F32)),
    )(log_dt, lam_re, lam_im, bt_re, bt_im)


def _s5_params_bwd(layer, log_dt, lam_re, lam_im, bt_re, bt_im, g_ar, g_ai, g_br, g_bi):
    def body(ld, lr, li, br, bi, car, cai, cbr, cbi, o_ld, o_lr, o_li, o_br, o_bi):
        _, vjp = jax.vjp(_s5_param_fn, ld[...], lr[...], li[...], br[...], bi[...])
        d_ld, d_lr, d_li, d_br, d_bi = vjp((car[...], cai[...], cbr[...], cbi[...]))
        o_ld[...] = d_ld
        o_lr[...] = d_lr
        o_li[...] = d_li
        o_br[...] = d_br
        o_bi[...] = d_bi

    one = lambda shape: pl.BlockSpec((None,) + shape, lambda i: (layer,) + (0,) * len(shape))
    whole = lambda shape: _const(shape)
    vec, lam, mat = (N_GROUP, 1), (N_GROUP, STATE), (N_GROUP, GROUP_W, STATE)
    return pl.pallas_call(
        body, name=f"s5_params_bwd_l{layer}", grid=(1,),
        in_specs=[one(vec), one(lam), one(lam), one(mat), one(mat), whole(lam), whole(lam), whole(mat), whole(mat)],
        out_specs=(whole(vec), whole(lam), whole(lam), whole(mat), whole(mat)),
        out_shape=(SDS(vec, F32), SDS(lam, F32), SDS(lam, F32), SDS(mat, F32), SDS(mat, F32)),
    )(log_dt, lam_re, lam_im, bt_re, bt_im, g_ar, g_ai, g_br, g_bi)


def _norm_proj(layer, x, norm_g, wg_in, b_in, carry=None):
    n_w = len(wg_in)

    def body(x_ref, g_ref, b_ref, *refs):
        w_refs, o_ref = refs[:n_w], refs[n_w]
        _, xn = _rms(x_ref[...])
        h = (xn * g_ref[layer:layer + 1, :]).astype(BF16)
        for k in range(N_DEV):
            cols = slice(k * WIDTH, (k + 1) * WIDTH)
            acc = b_ref[layer:layer + 1, cols]
            row = 0
            for w_ref in w_refs:
                rows = w_ref.shape[1]
                acc = acc + _dot(h[:, row:row + rows], w_ref[k])
                row += rows
            o_ref[:, cols] = acc

    (proj,), moved = _pcall(
        body, name=f"norm_proj_l{layer}",
        out_shape=[SDS((SEQ, N_IN), F32)],
        grid=(SEQ // TILE_M,),
        in_specs=[pl.BlockSpec((TILE_M, D_MODEL), lambda i: (i, 0)),
                  _const((DEPTH, D_MODEL)),
                  _const((DEPTH, N_IN))] + [_const(w.shape) for w in wg_in],
        out_specs=[pl.BlockSpec((TILE_M, N_IN), lambda i: (i, 0))],
        args=[x, norm_g, b_in, *wg_in], sem=("parallel",), carry=carry)
    return proj, moved


TIME_BLK = 512
N_TBLK = SEQ // TIME_BLK
N_PANEL = CH_S // LANES
STATE_SHAPE = (N_PANEL, SEQ * SUBLANES, LANES)


def _s5_layer_specs(layer):
    mat = lambda: pl.BlockSpec((None, N_GROUP, GROUP_W, STATE), lambda i: (layer, 0, 0, 0))
    ab = lambda: pl.BlockSpec((None, N_GROUP, STATE), lambda i: (layer, 0, 0))
    return [mat(), mat(), mat(), mat(), ab(), ab(), _const((DEPTH, WIDTH))]


def _s5_layer_scratch():
    return [pltpu.VMEM((N_CHUNK, CH_W, CH_S), BF16)] * 4 + [pltpu.VMEM((8, CH_S), F32)] * 2


def _s5_layer_fill(btre_ref, btim_ref, cre_ref, cim_ref, are_ref, aim_ref, bdre, bdim, ctre, ctim, a1, a2):
    for m in (bdre, bdim, ctre, ctim):
        m[...] = jnp.zeros_like(m)
    for grp in range(N_GROUP):
        k, g = divmod(grp, CH_G)
        rows = slice(g * GROUP_W, (g + 1) * GROUP_W)
        cols = slice(g * STATE, (g + 1) * STATE)
        bdre[k, rows, cols] = btre_ref[grp].astype(BF16)
        bdim[k, rows, cols] = btim_ref[grp].astype(BF16)
        ctre[k, rows, cols] = cre_ref[grp].astype(BF16)
        ctim[k, rows, cols] = cim_ref[grp].astype(BF16)
        ar = are_ref[grp:grp + 1, :]
        ai = aim_ref[grp:grp + 1, :]
        a1[k:k + 1, cols] = ar
        a1[N_CHUNK + k:N_CHUNK + k + 1, cols] = ar
        a2[k:k + 1, cols] = -ai
        a2[N_CHUNK + k:N_CHUNK + k + 1, cols] = ai


SCAN_UNROLL = 8


def _panels(tile):
    return [tile[:, p * LANES:(p + 1) * LANES] for p in range(N_PANEL)]


def _rows_load(ref, row):
    return jnp.concatenate([ref[p, pl.ds(row, TIME_BLK, stride=SUBLANES), :] for p in range(N_PANEL)], axis=1)


def _rows_store(ref, row, val):
    for p in range(N_PANEL):
        ref[p, pl.ds(row, TIME_BLK, stride=SUBLANES), :] = val[:, p * LANES:(p + 1) * LANES]


def _s5_scan_fwd(layer, proj, bbt_re, bbt_im, c_re, c_im, abar_re, abar_im, d_skip, carry=None):
    def body(u_ref, btre_ref, btim_ref, cre_ref, cim_ref, are_ref, aim_ref, d_ref, s_ref, y_ref,
             bdre, bdim, ctre, ctim, a1, a2, state):
        @pl.when(pl.program_id(0) == 0)
        def _():
            _s5_layer_fill(btre_ref, btim_ref, cre_ref, cim_ref, are_ref, aim_ref, bdre, bdim, ctre, ctim, a1, a2)
            state[...] = jnp.zeros_like(state)

        for k in range(N_CHUNK):
            ub = u_ref[:, k * CH_W:(k + 1) * CH_W].astype(BF16)
            _rows_store(s_ref, k, _dot(ub, bdre[k]))
            _rows_store(s_ref, N_CHUNK + k, _dot(ub, bdim[k]))
        m1 = _panels(a1[...])
        m2 = _panels(a2[...])

        def steps(n, tile):
            for r in range(SCAN_UNROLL):
                rows = pl.ds(pl.multiple_of((n * SCAN_UNROLL + r) * 8, 8), 8)
                tile = [m1[p] * tile[p] + m2[p] * pltpu.roll(tile[p], N_CHUNK, 0) + s_ref[p, rows, :]
                        for p in range(N_PANEL)]
                for p in range(N_PANEL):
                    s_ref[p, rows, :] = tile[p]
            return tile

        tile = lax.fori_loop(0, TIME_BLK // SCAN_UNROLL, steps, _panels(state[...]))
        state[...] = jnp.concatenate(tile, axis=1)
        d = d_ref[layer:layer + 1, :]
        for k in range(N_CHUNK):
            cols = slice(k * CH_W, (k + 1) * CH_W)
            y = (_dot_nt(_rows_load(s_ref, k).astype(BF16), ctre[k])
                 - _dot_nt(_rows_load(s_ref, N_CHUNK + k).astype(BF16), ctim[k]))
            y_ref[:, cols] = y + d[:, cols] * u_ref[:, cols]

    return _pcall(
        body, name=f"s5_fwd_l{layer}",
        out_shape=(SDS(STATE_SHAPE, F32), SDS((SEQ, WIDTH), F32)),
        grid=(N_TBLK,),
        in_specs=[pl.BlockSpec((TIME_BLK, WIDTH), lambda i: (i, 0))] + _s5_layer_specs(layer),
        out_specs=(pl.BlockSpec((N_PANEL, TIME_BLK * SUBLANES, LANES), lambda i: (0, i, 0)),
                   pl.BlockSpec((TIME_BLK, WIDTH), lambda i: (i, 0))),
        scratch_shapes=_s5_layer_scratch() + [pltpu.VMEM((8, CH_S), F32)],
        args=[proj, bbt_re, bbt_im, c_re, c_im, abar_re, abar_im, d_skip], sem=("arbitrary",), carry=carry)


def _s5_scan_bwd(layer, dy0, proj, states, bbt_re, bbt_im, c_re, c_im, abar_re, abar_im, d_skip, dproj,
                 carry=None):
    def body(dy_ref, u_ref, s_ref, sprev_ref, btre_ref, btim_ref, cre_ref, cim_ref, are_ref, aim_ref, d_ref, _,
             du_ref, gbre_ref, gbim_ref, gcre_ref, gcim_ref, gare_ref, gaim_ref, gd_ref,
             lam_ref, bdre, bdim, ctre, ctim, a1, a2, state, acc1, acc2, gbre, gbim, gcre, gcim, gd):
        step_id = pl.program_id(0)

        @pl.when(step_id == 0)
        def _():
            _s5_layer_fill(btre_ref, btim_ref, cre_ref, cim_ref, are_ref, aim_ref, bdre, bdim, ctre, ctim, a1, a2)
            for r in (state, acc1, acc2, gbre, gbim, gcre, gcim, gd):
                r[...] = jnp.zeros_like(r)

        for k in range(N_CHUNK):
            dyb = dy_ref[:, k * CH_W:(k + 1) * CH_W].astype(BF16)
            _rows_store(lam_ref, k, _dot(dyb, ctre[k]))
            _rows_store(lam_ref, N_CHUNK + k, -_dot(dyb, ctim[k]))
            gcre[k] += _dot_tn(dyb, _rows_load(s_ref, k).astype(BF16))
            gcim[k] -= _dot_tn(dyb, _rows_load(s_ref, N_CHUNK + k).astype(BF16))

        m1 = _panels(a1[...])
        m2 = _panels(-a2[...])
        has_before = (step_id < N_TBLK - 1).astype(F32)

        def one(t8, c, first_token):
            tile, swapped, p1, p2 = c
            rows = pl.ds(t8, 8)
            tile = [m1[p] * tile[p] + m2[p] * swapped[p] + lam_ref[p, rows, :] for p in range(N_PANEL)]
            swapped = [pltpu.roll(tile[p], N_CHUNK, 0) for p in range(N_PANEL)]
            for p in range(N_PANEL):
                lam_ref[p, rows, :] = tile[p]
            if first_token:
                before = [sprev_ref[p] * has_before for p in range(N_PANEL)]
            else:
                before = [s_ref[p, pl.ds(t8 - 8, 8), :] for p in range(N_PANEL)]
            p1 = [p1[p] + tile[p] * before[p] for p in range(N_PANEL)]
            p2 = [p2[p] + swapped[p] * before[p] for p in range(N_PANEL)]
            return tile, swapped, p1, p2

        def steps(n, c):
            for r in range(SCAN_UNROLL):
                t8 = pl.multiple_of((TIME_BLK - 1 - (n * SCAN_UNROLL + r)) * 8, 8)
                c = one(t8, c, False)
            return c

        tile0 = _panels(state[...])
        c = (tile0, [pltpu.roll(t, N_CHUNK, 0) for t in tile0], _panels(acc1[...]), _panels(acc2[...]))
        c = lax.fori_loop(0, TIME_BLK // SCAN_UNROLL - 1, steps, c)
        for r in range(SCAN_UNROLL - 1, -1, -1):
            c = one(r * 8, c, r == 0)
        state[...] = jnp.concatenate(c[0], axis=1)
        acc1[...] = jnp.concatenate(c[2], axis=1)
        acc2[...] = jnp.concatenate(c[3], axis=1)

        d = d_ref[layer:layer + 1, :]
        for k in range(N_CHUNK):
            cols = slice(k * CH_W, (k + 1) * CH_W)
            lrb = _rows_load(lam_ref, k).astype(BF16)
            lib = _rows_load(lam_ref, N_CHUNK + k).astype(BF16)
            u = u_ref[:, cols]
            ub = u.astype(BF16)
            dy = dy_ref[:, cols]
            du = dy * d[:, cols] + _dot_nt(lrb, bdre[k]) + _dot_nt(lib, bdim[k])
            du_ref[:, cols] = du.astype(BF16)
            gbre[k] += _dot_tn(ub, lrb)
            gbim[k] += _dot_tn(ub, lib)
        gd[...] += jnp.sum(dy_ref[...] * u_ref[...], axis=0, keepdims=True)

        @pl.when(step_id == N_TBLK - 1)
        def _():
            gd_ref[...] = gd[...]
            ga_re = acc1[0:N_CHUNK, :] + acc1[N_CHUNK:, :]
            ga_im = acc2[0:N_CHUNK, :] - acc2[N_CHUNK:, :]
            for grp in range(N_GROUP):
                k, g = divmod(grp, CH_G)
                rows = slice(g * GROUP_W, (g + 1) * GROUP_W)
                cols = slice(g * STATE, (g + 1) * STATE)
                gcre_ref[grp] = gcre[k, rows, cols]
                gcim_ref[grp] = gcim[k, rows, cols]
                gbre_ref[grp] = gbre[k, rows, cols]
                gbim_ref[grp] = gbim[k, rows, cols]
                gare_ref[grp:grp + 1, :] = ga_re[k:k + 1, cols]
                gaim_ref[grp:grp + 1, :] = ga_im[k:k + 1, cols]

    back = lambda i: N_TBLK - 1 - i
    tok = lambda: pl.BlockSpec((TIME_BLK, WIDTH), lambda i: (back(i), 0))
    mat = lambda: _const((N_GROUP, GROUP_W, STATE))
    acc_mat = pltpu.VMEM((N_CHUNK, CH_W, CH_S), F32)
    return _pcall(
        body, name=f"s5_bwd_l{layer}",
        out_shape=(SDS((SEQ, N_IN), BF16), SDS((N_GROUP, GROUP_W, STATE), F32), SDS((N_GROUP, GROUP_W, STATE), F32),
                   SDS((N_GROUP, GROUP_W, STATE), F32), SDS((N_GROUP, GROUP_W, STATE), F32),
                   SDS((N_GROUP, STATE), F32), SDS((N_GROUP, STATE), F32), SDS((1, WIDTH), F32)),
        grid=(N_TBLK,),
        in_specs=[tok(), tok(),
                  pl.BlockSpec((N_PANEL, TIME_BLK * SUBLANES, LANES), lambda i: (0, back(i), 0)),
                  pl.BlockSpec((N_PANEL, SUBLANES, LANES), lambda i: (0, jnp.maximum(back(i) * TIME_BLK - 1, 0), 0))]
        + _s5_layer_specs(layer) + [ANY],
        out_specs=(tok(), mat(), mat(), mat(), mat(), _const((N_GROUP, STATE)), _const((N_GROUP, STATE)),
                   _const((1, WIDTH))),
        scratch_shapes=[pltpu.VMEM((N_PANEL, TIME_BLK * SUBLANES, LANES), F32)] + _s5_layer_scratch()
        + [pltpu.VMEM((8, CH_S), F32)] * 3 + [acc_mat] * 4 + [pltpu.VMEM((1, WIDTH), F32)],
        args=[dy0, proj, states, states, bbt_re, bbt_im, c_re, c_im, abar_re, abar_im, d_skip, dproj],
        aliases={11: 0}, sem=("arbitrary",), limit=VMEM_LIMIT_BIG, carry=carry)


def _pool_counts(win):
    t = lax.broadcasted_iota(jnp.int32, (SEQ, POOL_GROUP), 0)
    return t, jnp.minimum(t + 1, win).astype(F32)


def _pool_fwd(layer, proj):
    def body(u_ref, o_ref):
        for gi, win in enumerate(POOL_WINDOWS):
            cols = slice(gi * POOL_GROUP, (gi + 1) * POOL_GROUP)
            u = u_ref[:, cols]
            t, count = _pool_counts(win)
            acc = u
            k = 1
            while k < win:
                acc = acc + jnp.where(t >= k, pltpu.roll(acc, k, 0), 0.0)
                k *= 2
            o_ref[:, cols] = acc / count - u

    return pl.pallas_call(
        body, name=f"pool_fwd_l{layer}",
        out_shape=SDS((SEQ, WIDTH), F32),
        grid=(1,),
        in_specs=[pl.BlockSpec((SEQ, WIDTH), lambda i: (0, 2))],
        out_specs=pl.BlockSpec((SEQ, WIDTH), lambda i: (0, 0)),
        compiler_params=_cp(("arbitrary",)),
    )(proj)


def _gelu_parts(y0):
    t = jnp.tanh(GELU_C * (y0 + GELU_A * (y0 * y0 * y0)))
    return t, 0.5 * y0 * (1.0 + t)


def _mix_forward(layer, p_ref, y0_ref, pooled_ref, wglu_ref, bglu_ref, pw_ref, scale_ref, wa_ref, wb_ref):
    za = p_ref[:, WIDTH:2 * WIDTH]
    zb = p_ref[:, 3 * WIDTH:4 * WIDTH]
    ga = p_ref[:, 4 * WIDTH:4 * WIDTH + D_MODEL]
    gb = p_ref[:, 4 * WIDTH + D_MODEL:]
    y0 = y0_ref[...]
    t, y1 = _gelu_parts(y0)
    y1b = y1.astype(BF16)
    q = _dot(y1b, wglu_ref[...].reshape(WIDTH, WIDTH)) + bglu_ref[layer:layer + 1, :]
    sq = _sig(q)
    y2 = y1 * sq
    sza = _sig(za)
    silu_za = za * sza
    ya = y2 * silu_za
    pooled = pooled_ref[...]
    mixed = jnp.concatenate(
        [_dot(pooled[:, g * POOL_GROUP:(g + 1) * POOL_GROUP].astype(BF16), pw_ref[g].astype(BF16))
         for g in range(len(POOL_WINDOWS))], axis=1)
    szb = _sig(zb)
    silu_zb = zb * szb
    scale = scale_ref[layer:layer + 1, :]
    ms = mixed * scale
    yb = ms * silu_zb
    yab = ya.astype(BF16)
    ybb = yb.astype(BF16)
    ma = _dot(yab, wa_ref[...])
    mb = _dot(ybb, wb_ref[...])
    sga = _sig(ga)
    sgb = _sig(gb)
    merged = sga * ma + sgb * mb
    return dict(za=za, zb=zb, y0=y0, t=t, y1=y1, y1b=y1b, sq=sq, y2=y2, sza=sza, silu_za=silu_za,
                pooled=pooled, mixed=mixed, szb=szb, silu_zb=silu_zb, scale=scale, ms=ms, yab=yab, ybb=ybb,
                ma=ma, mb=mb, sga=sga, sgb=sgb, merged=merged)


def _mix_weight_specs(layer):
    return [_const((N_DEV, WIDTH // N_DEV, WIDTH)),
            _const((DEPTH, WIDTH)),
            pl.BlockSpec((None, 4, POOL_GROUP, POOL_GROUP), lambda i: (layer, 0, 0, 0)),
            _const((DEPTH, WIDTH)),
            _const((WIDTH, D_MODEL)),
            _const((WIDTH, D_MODEL)),
            _const((N_DEV, D_MODEL // N_DEV, D_MODEL))]


def _mix_fwd(layer, x, proj, y0, pooled, wg_glu, b_glu, pool_w, pool_scale, wg_a, wg_b, wg_out, carry=None):
    def body(x_ref, p_ref, y0_ref, pooled_ref, wglu_ref, bglu_ref, pw_ref, scale_ref, wa_ref, wb_ref,
             wout_ref, o_ref):
        f = _mix_forward(layer, p_ref, y0_ref, pooled_ref, wglu_ref, bglu_ref, pw_ref, scale_ref, wa_ref, wb_ref)
        wout = wout_ref[...].reshape(D_MODEL, D_MODEL)
        o_ref[...] = x_ref[...] + _dot(f["merged"].astype(BF16), wout)

    (x_next,), moved = _pcall(
        body, name=f"mix_fwd_l{layer}",
        out_shape=[SDS((SEQ, D_MODEL), F32)],
        grid=(SEQ // TILE_M,),
        in_specs=[pl.BlockSpec((TILE_M, D_MODEL), lambda i: (i, 0)),
                  pl.BlockSpec((TILE_M, N_IN), lambda i: (i, 0)),
                  pl.BlockSpec((TILE_M, WIDTH), lambda i: (i, 0)),
                  pl.BlockSpec((TILE_M, WIDTH), lambda i: (i, 0))] + _mix_weight_specs(layer),
        out_specs=[pl.BlockSpec((TILE_M, D_MODEL), lambda i: (i, 0))],
        args=[x, proj, y0, pooled, wg_glu, b_glu, pool_w, pool_scale, wg_a, wg_b, wg_out],
        sem=("parallel",), carry=carry)
    return x_next, moved


def _loss_head(x, target, final_g):
    def body(x_ref, t_ref, g_ref, dx_ref, loss_ref, gg_ref):
        @pl.when(pl.program_id(0) == 0)
        def _():
            loss_ref[...] = jnp.zeros_like(loss_ref)
            gg_ref[...] = jnp.zeros_like(gg_ref)

        g = g_ref[...]
        rs, xn = _rms(x_ref[...])
        err = xn * g - t_ref[...]
        loss_ref[...] += 0.5 * jnp.sum(jnp.mean(err * err, axis=-1, keepdims=True), axis=0, keepdims=True)
        dy = err * (1.0 / D_MODEL)
        gg_ref[...] += jnp.sum(dy * xn, axis=0, keepdims=True)
        dxn = dy * g
        dx_ref[...] = rs * (dxn - xn * jnp.mean(dxn * xn, axis=-1, keepdims=True))

    return pl.pallas_call(
        body, name="loss_head",
        out_shape=(SDS((SEQ, D_MODEL), F32), SDS((1, 1), F32), SDS((1, D_MODEL), F32)),
        grid=(SEQ // TILE_M,),
        in_specs=[pl.BlockSpec((TILE_M, D_MODEL), lambda i: (i, 0)),
                  pl.BlockSpec((TILE_M, D_MODEL), lambda i: (i, 0)),
                  _const((1, D_MODEL))],
        out_specs=(pl.BlockSpec((TILE_M, D_MODEL), lambda i: (i, 0)), _const((1, 1)), _const((1, D_MODEL))),
        compiler_params=_cp(("arbitrary",)),
    )(x, target, final_g)


def _big_shapes():
    return dict(w_out=(DEPTH, N_DEV, D_MODEL // N_DEV, D_MODEL), w_branch_a=(DEPTH, N_DEV, WIDTH, D_MODEL // N_DEV),
                w_branch_b=(DEPTH, N_DEV, WIDTH, D_MODEL // N_DEV), ssm_w_glu=(DEPTH, N_DEV, WIDTH // N_DEV, WIDTH),
                w_in=(DEPTH, N_DEV, D_MODEL, WIDTH))


def _mix_bwd(layer, dx_next, proj, y0, pooled, wg_glu, b_glu, pool_w, pool_scale, wg_a, wg_b, wg_out, prev,
             carry=None):
    n_k = N_DEV
    n_prev = 0 if prev is None else len(prev)

    def body(*refs):
        (dx_ref, p_ref, y0_ref, pooled_ref, wglu_ref, bglu_ref, pw_ref, scale_ref, wa_ref, wb_ref,
         wout_ref) = refs[:11]
        (dproj_ref, dy0_ref, dpooled_ref, gwout_ref, gwa_ref, gwb_ref, gwglu_ref, gpw_ref,
         gscale_ref, gbglu_ref) = refs[11 + n_prev:]

        @pl.when(pl.program_id(0) == 0)
        def _():
            for r in (gwout_ref, gwa_ref, gwb_ref, gwglu_ref, gpw_ref, gscale_ref, gbglu_ref):
                r[...] = jnp.zeros_like(r)

        f = _mix_forward(layer, p_ref, y0_ref, pooled_ref, wglu_ref, bglu_ref, pw_ref, scale_ref, wa_ref, wb_ref)
        wglu = wglu_ref[...].reshape(WIDTH, WIDTH)
        wout = wout_ref[...].reshape(D_MODEL, D_MODEL)
        blk = D_MODEL // n_k
        dxb = dx_ref[...].astype(BF16)
        dmerged = _dot_nt(dxb, wout)
        gwout = _dot_tn(f["merged"].astype(BF16), dxb)
        for k in range(n_k):
            gwout_ref[_slot(k)] += gwout[k * blk:(k + 1) * blk, :]
        dma = dmerged * f["sga"]
        dmb = dmerged * f["sgb"]
        dga = dmerged * f["ma"] * f["sga"] * (1.0 - f["sga"])
        dgb = dmerged * f["mb"] * f["sgb"] * (1.0 - f["sgb"])
        dmab = dma.astype(BF16)
        dmbb = dmb.astype(BF16)
        dya = _dot_nt(dmab, wa_ref[...])
        dyb = _dot_nt(dmbb, wb_ref[...])
        gwa = _dot_tn(f["yab"], dmab)
        gwb = _dot_tn(f["ybb"], dmbb)
        for k in range(n_k):
            gwa_ref[_slot(k)] += gwa[:, k * blk:(k + 1) * blk]
            gwb_ref[_slot(k)] += gwb[:, k * blk:(k + 1) * blk]
        zb, szb = f["zb"], f["szb"]
        dzb = dyb * f["ms"] * (szb * (1.0 + zb * (1.0 - szb)))
        dms = dyb * f["silu_zb"]
        gscale_ref[...] += jnp.sum(dms * f["mixed"], axis=0, keepdims=True)
        dmixed = (dms * f["scale"]).astype(BF16)
        pooled = f["pooled"]
        for g in range(len(POOL_WINDOWS)):
            cols = slice(g * POOL_GROUP, (g + 1) * POOL_GROUP)
            dpooled_ref[:, cols] = _dot_nt(dmixed[:, cols], pw_ref[g].astype(BF16))
            gpw_ref[g] += _dot_tn(pooled[:, cols].astype(BF16), dmixed[:, cols])
        za, sza = f["za"], f["sza"]
        dza = dya * f["y2"] * (sza * (1.0 + za * (1.0 - sza)))
        dy2 = dya * f["silu_za"]
        sq = f["sq"]
        dq = dy2 * f["y1"] * sq * (1.0 - sq)
        dqb = dq.astype(BF16)
        dy1 = dy2 * sq + _dot_nt(dqb, wglu)
        gwglu = _dot_tn(f["y1b"], dqb)
        rblk = WIDTH // n_k
        for k in range(n_k):
            gwglu_ref[_slot(k)] += gwglu[k * rblk:(k + 1) * rblk, :]
        gbglu_ref[...] += jnp.sum(dq, axis=0, keepdims=True)
        y0, t = f["y0"], f["t"]
        dgelu = 0.5 * (1.0 + t) + 0.5 * y0 * (1.0 - t * t) * (GELU_C * (1.0 + 3.0 * GELU_A * y0 * y0))
        dy0_ref[...] = dy1 * dgelu
        zeros = jnp.zeros((TILE_M, WIDTH), BF16)
        dproj_ref[:, 0:WIDTH] = zeros
        dproj_ref[:, WIDTH:2 * WIDTH] = dza.astype(BF16)
        dproj_ref[:, 2 * WIDTH:3 * WIDTH] = zeros
        dproj_ref[:, 3 * WIDTH:4 * WIDTH] = dzb.astype(BF16)
        dproj_ref[:, 4 * WIDTH:4 * WIDTH + D_MODEL] = dga.astype(BF16)
        dproj_ref[:, 4 * WIDTH + D_MODEL:] = dgb.astype(BF16)

    tile = lambda w: pl.BlockSpec((TILE_M, w), lambda i: (i, 0))
    shapes = _big_shapes()
    big = ["w_out", "w_branch_a", "w_branch_b", "ssm_w_glu"]
    slab = lambda n: pl.BlockSpec((None,) + shapes[n][1:], lambda i: (layer, 0, 0, 0))
    args = [dx_next, proj, y0, pooled, wg_glu, b_glu, pool_w, pool_scale, wg_a, wg_b, wg_out]
    return _pcall(
        body, name=f"mix_bwd_l{layer}",
        out_shape=(SDS((SEQ, N_IN), BF16), SDS((SEQ, WIDTH), F32), SDS((SEQ, WIDTH), F32))
        + tuple(SDS(shapes[n], F32) for n in big)
        + (SDS((4, POOL_GROUP, POOL_GROUP), F32), SDS((1, WIDTH), F32), SDS((1, WIDTH), F32)),
        grid=(SEQ // TILE_M,),
        in_specs=[tile(D_MODEL), tile(N_IN), tile(WIDTH), tile(WIDTH)] + _mix_weight_specs(layer) + [ANY] * n_prev,
        out_specs=(tile(N_IN), tile(WIDTH), tile(WIDTH)) + tuple(slab(n) for n in big)
        + (_const((4, POOL_GROUP, POOL_GROUP)), _const((1, WIDTH)), _const((1, WIDTH))),
        args=args + list(prev or ()),
        aliases={len(args) + i: 3 + i for i in range(n_prev)},
        sem=("arbitrary",), limit=VMEM_LIMIT_BIG, carry=carry)


def _pool_bwd(layer, dpooled, dproj):
    def body(dp_ref, _, o_ref):
        for gi, win in enumerate(POOL_WINDOWS):
            cols = slice(gi * POOL_GROUP, (gi + 1) * POOL_GROUP)
            dp = dp_ref[:, cols]
            t, count = _pool_counts(win)
            e = dp / count
            acc = e
            k = 1
            while k < win:
                acc = acc + jnp.where(t < SEQ - k, pltpu.roll(acc, SEQ - k, 0), 0.0)
                k *= 2
            o_ref[:, cols] = (acc - dp).astype(BF16)

    return pl.pallas_call(
        body, name=f"pool_bwd_l{layer}",
        out_shape=SDS((SEQ, N_IN), BF16),
        grid=(1,),
        in_specs=[pl.BlockSpec((SEQ, WIDTH), lambda i: (0, 0)), ANY],
        out_specs=pl.BlockSpec((SEQ, WIDTH), lambda i: (0, 2)),
        input_output_aliases={1: 0},
        compiler_params=_cp(("arbitrary",)),
    )(dpooled, dproj)


def _proj_wgrad(layer, x, norm_g, dproj, rows_of=(0, D_MODEL), carry=None):
    row0, n_rows = rows_of
    tm = 512 * D_MODEL // n_rows

    def body(*refs):
        x_ref, g_ref, dp_ref, gw_ref, gb_ref, ht_ref = refs
        n, t = pl.program_id(0), pl.program_id(1)

        @pl.when(t == 0)
        def _():
            gw_ref[...] = jnp.zeros_like(gw_ref)
            gb_ref[...] = jnp.zeros_like(gb_ref)

        @pl.when(n == 0)
        def _():
            _, xn = _rms(x_ref[...])
            h = xn * g_ref[layer:layer + 1, :]
            ht_ref[t] = h[:, row0:row0 + n_rows].T.astype(BF16)

        dp = dp_ref[...]
        gw_ref[...] += _dot(ht_ref[t], dp)
        gb_ref[...] += jnp.sum(dp.astype(F32), axis=0, keepdims=True)

    return _pcall(
        body, name=f"proj_wgrad_l{layer}_r{row0}",
        out_shape=(SDS((1, N_DEV, n_rows, WIDTH), F32), SDS((1, N_IN), F32)),
        grid=(N_DEV, SEQ // tm),
        in_specs=[pl.BlockSpec((tm, D_MODEL), lambda n, t: (jnp.where(n == 0, t, 0), 0)),
                  _const((DEPTH, D_MODEL)),
                  pl.BlockSpec((tm, WIDTH), lambda n, t: (t, n))],
        out_specs=(pl.BlockSpec((None, None, n_rows, WIDTH), lambda n, t: (0, _slot(n), 0, 0)),
                   pl.BlockSpec((1, WIDTH), lambda n, t: (0, n))),
        scratch_shapes=[pltpu.VMEM((SEQ // tm, n_rows, tm), BF16)],
        args=[x, norm_g, dproj], sem=("arbitrary", "arbitrary"), carry=carry)


def _proj_dgrad(layer, dx_next, x, norm_g, dproj, wg_in, carry=None):
    n_w = len(wg_in)

    def body(dxn_ref, x_ref, g_ref, dp_ref, *refs):
        w_refs, (dx_ref, gg_ref) = refs[:n_w], refs[n_w:]

        @pl.when(pl.program_id(0) == 0)
        def _():
            gg_ref[...] = jnp.zeros_like(gg_ref)

        parts = []
        for w_ref in w_refs:
            part = jnp.zeros((TILE_M, w_ref.shape[1]), F32)
            for k in range(N_DEV):
                part = part + _dot_nt(dp_ref[:, k * WIDTH:(k + 1) * WIDTH], w_ref[k])
            parts.append(part)
        dh = parts[0] if n_w == 1 else jnp.concatenate(parts, axis=1)
        rs, xn = _rms(x_ref[...])
        gg_ref[...] += jnp.sum(dh * xn, axis=0, keepdims=True)
        dxn = dh * g_ref[layer:layer + 1, :]
        dx_ref[...] = dxn_ref[...] + rs * (dxn - xn * jnp.mean(dxn * xn, axis=-1, keepdims=True))

    return _pcall(
        body, name=f"proj_dgrad_l{layer}",
        out_shape=(SDS((SEQ, D_MODEL), F32), SDS((1, D_MODEL), F32)),
        grid=(SEQ // TILE_M,),
        in_specs=[pl.BlockSpec((TILE_M, D_MODEL), lambda i: (i, 0)),
                  pl.BlockSpec((TILE_M, D_MODEL), lambda i: (i, 0)),
                  _const((DEPTH, D_MODEL)),
                  pl.BlockSpec((TILE_M, N_IN), lambda i: (i, 0))] + [_const(w.shape) for w in wg_in],
        out_specs=(pl.BlockSpec((TILE_M, D_MODEL), lambda i: (i, 0)), _const((1, D_MODEL))),
        args=[dx_next, x, norm_g, dproj, *wg_in], sem=("arbitrary",), carry=carry)


def _my_place():
    return lax.axis_index("x"), lax.axis_index("y"), lax.axis_index("c")


def _gather_plan(shards, layer, by_columns=(), rows_of=None):
    n = len(shards)

    def parts(ins, outs, sems):
        send_sems, recv_sems, local_sems = sems
        x, y, c = _my_place()
        chips = [(1 - x, y), (x, 1 - y), (1 - x, 1 - y)]

        def source(t):
            return ins[t].at[layer] if rows_of is None else ins[t].at[layer, pl.ds(*rows_of)]

        def rows(t, place):
            px, py, pc = place
            index = 4 * px + 2 * py + pc
            if t in by_columns:
                width = shards[t].shape[2]
                return outs[t].at[:, pl.ds(pl.multiple_of(index * width, LANES), width)]
            return outs[t].at[index]

        def copy(t, k, block, to, from_src=False):
            return pltpu.make_async_remote_copy(
                src_ref=source(t) if from_src else rows(t, block), dst_ref=rows(t, block),
                send_sem=send_sems.at[7 * t + k], recv_sem=recv_sems.at[7 * t + k], device_id=to,
                device_id_type=MESH)

        def mine(t):
            return pltpu.make_async_copy(source(t), rows(t, (x, y, c)), local_sems.at[t])

        return (x, y, c), chips, copy, mine

    def start(ins, outs, sems):
        me, chips, copy, mine = parts(ins, outs, sems)
        x, y, c = me
        for t in range(n):
            mine(t).start()
            copy(t, 0, me, (x, y, 1 - c), from_src=True).start()
            for j, chip in enumerate(chips):
                copy(t, 1 + j, me, (*chip, c), from_src=True).start()

    def relay(ins, outs, sems):
        me, chips, copy, mine = parts(ins, outs, sems)
        x, y, c = me
        for t in range(n):
            for j, chip in enumerate(chips):
                copy(t, 1 + j, (*chip, c), me).wait_recv()
                copy(t, 4 + j, (*chip, c), (x, y, 1 - c)).start()

    def finish(ins, outs, sems):
        me, chips, copy, mine = parts(ins, outs, sems)
        x, y, c = me
        sibling = (x, y, 1 - c)
        for t in range(n):
            copy(t, 0, sibling, me).wait_recv()
            for j, chip in enumerate(chips):
                copy(t, 4 + j, (*chip, 1 - c), me).wait_recv()
            for k in range(7):
                copy(t, k, me, sibling, from_src=k < 4).wait_send()
            mine(t).wait()

    n_rows = lambda a: a.shape[1] if rows_of is None else rows_of[1]
    out_shape = [SDS((a.shape[1], N_DEV * a.shape[2]) if t in by_columns else (N_DEV, n_rows(a), a.shape[2]), a.dtype)
                 for t, a in enumerate(shards)]
    sems = [pltpu.SemaphoreType.DMA((7 * n,)), pltpu.SemaphoreType.DMA((7 * n,)), pltpu.SemaphoreType.DMA((n,))]
    return _Carried(shards, out_shape, sems, start, finish, relay)


class _Carried:
    def __init__(self, ins, out_shape, sems, start, finish, relay=None):
        self.ins, self.out_shape, self.sems = list(ins), list(out_shape), list(sems)
        self.start, self.finish = start, finish
        self.relay = relay or (lambda ins, outs, sems: None)


def _pcall(body, *, name, grid, in_specs, out_specs, out_shape, args, scratch_shapes=(), aliases=None,
           sem=None, limit=VMEM_LIMIT, carry=None):
    out_shape, out_specs, scratch_shapes = list(out_shape), list(out_specs), list(scratch_shapes)
    n_in, n_out, n_scr = len(args), len(out_shape), len(scratch_shapes)
    if carry is None:
        kern, c_ins, c_out, c_sems = body, [], [], []
    else:
        c_ins, c_out, c_sems = carry.ins, carry.out_shape, carry.sems
        ci, co = len(c_ins), len(c_out)
        steps = tuple(grid)

        def kern(*refs):
            o0 = n_in + ci
            s0 = o0 + n_out + co
            mine = refs[:n_in] + refs[o0:o0 + n_out] + refs[s0:s0 + n_scr]
            theirs = (refs[n_in:o0], refs[o0 + n_out:s0], refs[s0 + n_scr:])
            first = pl.program_id(0) == 0
            last = pl.program_id(0) == steps[0] - 1
            for a in range(1, len(steps)):
                first = jnp.logical_and(first, pl.program_id(a) == 0)
                last = jnp.logical_and(last, pl.program_id(a) == steps[a] - 1)

            @pl.when(first)
            def _():
                carry.start(*theirs)

            @pl.when(last)
            def _():
                carry.relay(*theirs)

            body(*mine)

            @pl.when(last)
            def _():
                carry.finish(*theirs)

        sem = ("arbitrary",) * len(steps)
    res = pl.pallas_call(
        kern, name=name, grid=tuple(grid),
        in_specs=list(in_specs) + [ANY] * len(c_ins),
        out_specs=tuple(out_specs + [ANY] * len(c_out)),
        out_shape=tuple(out_shape + c_out),
        scratch_shapes=scratch_shapes + c_sems,
        input_output_aliases=aliases or {},
        compiler_params=_cp(sem, limit),
    )(*args, *c_ins)
    return res[:n_out], res[n_out:]


def _run_carried(name, carry):
    ci, co = len(carry.ins), len(carry.out_shape)

    def body(*refs):
        parts = (refs[:ci], refs[ci:ci + co], refs[ci + co:])
        carry.start(*parts)
        carry.relay(*parts)
        carry.finish(*parts)

    return pl.pallas_call(
        body, name=name, out_shape=tuple(carry.out_shape),
        in_specs=[ANY] * ci, out_specs=tuple([ANY] * co), scratch_shapes=carry.sems,
    )(*carry.ins)


def _sibling_plan(big, small):
    n = len(big)
    n_copies = 4 * n + len(small)

    def copies(ins, outs, sems):
        send_sems, recv_sems = sems
        x, y, c = _my_place()
        pairs = []
        for t, (_, layer, rows_of) in enumerate(big):
            for s in range(4):
                pairs.append((ins[t].at[layer, 4 * (1 - c) + s, pl.ds(*rows_of)], outs[t].at[s]))
        pairs += list(zip(ins[n:], outs[n:]))
        return [pltpu.make_async_remote_copy(
            src_ref=src, dst_ref=dst, send_sem=send_sems.at[k], recv_sem=recv_sems.at[k],
            device_id=(x, y, 1 - c), device_id_type=MESH) for k, (src, dst) in enumerate(pairs)]

    def start(ins, outs, sems):
        for cp in copies(ins, outs, sems):
            cp.start()

    def finish(ins, outs, sems):
        for cp in copies(ins, outs, sems):
            cp.wait()

    out_shape = ([SDS((4, rows_of[1], a.shape[3]), a.dtype) for a, _, rows_of in big]
                 + [SDS(a.shape, a.dtype) for a in small])
    sems = [pltpu.SemaphoreType.DMA((n_copies,)), pltpu.SemaphoreType.DMA((n_copies,))]
    return _Carried([a for a, _, _ in big] + list(small), out_shape, sems, start, finish)


def _chips_plan(big, small):
    n, n_small = len(big), len(small)
    max_rows = 512
    parts = [max(1, a.shape[1] // max_rows) for a in big]
    n_copies = 3 * (sum(parts) + n_small)

    def copies(ins, outs, sems, landing):
        send_sems, recv_sems, local_sems = sems
        x, y, c = _my_place()
        my_chip = 2 * x + y
        chips = [(1 - x, y), (x, 1 - y), (1 - x, 1 - y)]
        remote, local = [], []
        for chip in chips:
            to = 2 * chip[0] + chip[1]
            slot = to if landing else my_chip
            pairs = []
            for t in range(n):
                rows_per = big[t].shape[1] // parts[t]
                for p in range(parts[t]):
                    rows = pl.ds(p * rows_per, rows_per)
                    pairs.append((ins[t].at[to, rows], outs[t].at[slot, rows]))
            pairs += [(ins[t], outs[t].at[slot]) for t in range(n, n + n_small)]
            for src, dst in pairs:
                k = len(remote)
                remote.append(pltpu.make_async_remote_copy(
                    src_ref=src, dst_ref=dst, send_sem=send_sems.at[k], recv_sem=recv_sems.at[k],
                    device_id=(*chip, c), device_id_type=MESH))
        for t in range(n):
            local.append(pltpu.make_async_copy(ins[t].at[my_chip], outs[t].at[my_chip], local_sems.at[t]))
        for t in range(n, n + n_small):
            local.append(pltpu.make_async_copy(ins[t], outs[t].at[my_chip], local_sems.at[t]))
        return remote + local

    def start(ins, outs, sems):
        for cp in copies(ins, outs, sems, landing=False):
            cp.start()

    def finish(ins, outs, sems):
        for cp in copies(ins, outs, sems, landing=True):
            cp.wait()

    out_shape = [SDS(a.shape, a.dtype) for a in big] + [SDS((N_CHIP,) + a.shape, a.dtype) for a in small]
    sems = [pltpu.SemaphoreType.DMA((n_copies,)), pltpu.SemaphoreType.DMA((n_copies,)),
            pltpu.SemaphoreType.DMA((n + n_small,))]
    return _Carried(list(big) + list(small), out_shape, sems, start, finish)


def _all_plan(small):
    n = len(small)
    masks = [(m >> 2 & 1, m >> 1 & 1, m & 1) for m in range(1, N_DEV)]

    def copies(ins, outs, sems, landing):
        send_sems, recv_sems, local_sems = sems
        x, y, c = _my_place()
        me = 4 * x + 2 * y + c
        flip = lambda v, bit: 1 - v if bit else v
        remote = []
        for fx, fy, fc in masks:
            peer = (flip(x, fx), flip(y, fy), flip(c, fc))
            slot = 4 * peer[0] + 2 * peer[1] + peer[2] if landing else me
            for t in range(n):
                k = len(remote)
                remote.append(pltpu.make_async_remote_copy(
                    src_ref=ins[t], dst_ref=outs[t].at[slot], send_sem=send_sems.at[k], recv_sem=recv_sems.at[k],
                    device_id=peer, device_id_type=MESH))
        local = [pltpu.make_async_copy(ins[t], outs[t].at[me], local_sems.at[t]) for t in range(n)]
        return remote + local

    def start(ins, outs, sems):
        for cp in copies(ins, outs, sems, landing=False):
            cp.start()

    def finish(ins, outs, sems):
        for cp in copies(ins, outs, sems, landing=True):
            cp.wait()

    out_shape = [SDS((N_DEV,) + a.shape, a.dtype) for a in small]
    sems = [pltpu.SemaphoreType.DMA((7 * n,)), pltpu.SemaphoreType.DMA((7 * n,)), pltpu.SemaphoreType.DMA((n,))]
    return _Carried(list(small), out_shape, sems, start, finish)


def _join(*plans):
    plans = [p for p in plans if p is not None]
    if len(plans) <= 1:
        return plans[0] if plans else None

    def each(fn_name, ins, outs, sems):
        i = o = s = 0
        for p in plans:
            ni, no, ns = len(p.ins), len(p.out_shape), len(p.sems)
            getattr(p, fn_name)(ins[i:i + ni], outs[o:o + no], sems[s:s + ns])
            i, o, s = i + ni, o + no, s + ns

    return _Carried(sum((p.ins for p in plans), []), sum((p.out_shape for p in plans), []),
                    sum((p.sems for p in plans), []),
                    lambda i, o, s: each("start", i, o, s), lambda i, o, s: each("finish", i, o, s),
                    lambda i, o, s: each("relay", i, o, s))


def _row_block(rows, most=256):
    return min(rows, most)


def _add_own(tag, core, g, layer, got, row0=0):
    _, r, c = got.shape
    rb = _row_block(r, most=1024)
    first = row0 // rb

    def body(core_ref, a_ref, b_ref, o_ref):
        o_ref[...] = (a_ref[...] + b_ref[...]).astype(o_ref.dtype)

    return pl.pallas_call(
        body, name=f"add_{tag}", out_shape=SDS(got.shape, BF16),
        grid_spec=pltpu.PrefetchScalarGridSpec(
            num_scalar_prefetch=1, grid=(4, r // rb),
            in_specs=[pl.BlockSpec((None, None, rb, c), lambda s, j, core: (layer, 4 * core[0] + s, first + j, 0)),
                      pl.BlockSpec((None, rb, c), lambda s, j, core: (s, j, 0))],
            out_specs=pl.BlockSpec((None, rb, c), lambda s, j, core: (s, j, 0))),
        compiler_params=_cp(("parallel", "parallel")),
    )(core, g, got)


def _add_lists(tag, own, got, grid=None, specs=None, dtype=F32):
    n = len(own)

    def body(*refs):
        for a, b, o in zip(refs[:n], refs[n:2 * n], refs[2 * n:]):
            o[...] = (a[...] + b[...]).astype(o.dtype)

    kw = {}
    if grid is not None:
        kw = dict(grid=grid, in_specs=list(specs) * 2, out_specs=tuple(specs),
                  compiler_params=_cp(("parallel",) * len(grid)))
    return pl.pallas_call(
        body, name=f"add_{tag}", out_shape=tuple(SDS(a.shape, dtype) for a in own), **kw)(*own, *got)


def _adamw_math(w, g, m, v):
    m = ADAM_B1 * m + (1.0 - ADAM_B1) * g
    v = ADAM_B2 * v + (1.0 - ADAM_B2) * (g * g)
    m_hat = m / (1.0 - ADAM_B1 ** ADAM_STEP)
    v_hat = v / (1.0 - ADAM_B2 ** ADAM_STEP)
    delta = -ADAM_LR * (m_hat / (jnp.sqrt(v_hat) + ADAM_EPS) + ADAM_WD * w)
    return delta, m, v


def _sum_slots_adamw(tag, slots, w, m, v):
    _, r, c = slots[0].shape
    rb = _row_block(r)

    def body(s0_ref, s1_ref, w_ref, m_ref, v_ref, g_ref, d_ref, nm_ref, nv_ref):
        first = pl.program_id(1) == 0
        g = _pair_sum([jnp.where(first, s0_ref[k], s1_ref[k]).astype(F32) for k in range(N_CHIP)])
        delta, nm, nv = _adamw_math(w_ref[...], g, m_ref[...], v_ref[...])
        g_ref[...] = g
        d_ref[...] = delta
        nm_ref[...] = nm
        nv_ref[...] = nv

    spec = pl.BlockSpec((None, rb, c), lambda j, l: (l, j, 0))
    sspec = pl.BlockSpec((N_CHIP, rb, c), lambda j, l: (0, j, 0))
    s = SDS((DEPTH, r, c), F32)
    return pl.pallas_call(
        body, name=f"adamw_{tag}", out_shape=(s, s, s, s),
        grid=(r // rb, DEPTH), in_specs=[sspec, sspec, spec, spec, spec], out_specs=(spec, spec, spec, spec),
        compiler_params=_cp(("parallel", "arbitrary")),
    )(*slots, w, m, v)


def _adamw_small(tag, entries, grid=None, sums=()):
    flat_in, in_specs, out_shape, out_specs, layout = [], [], [], [], []
    for slots, w, m, v, slot_spec, w_spec in entries:
        per_layer = isinstance(slots, (list, tuple))
        n_slot = len(slots) if per_layer else 1
        flat_in += (list(slots) if per_layer else [slots]) + [w, m, v]
        in_specs += [slot_spec] * n_slot + [w_spec] * 3
        out_shape += [SDS(w.shape, F32)] * 4
        out_specs += [w_spec] * 4
        layout.append((per_layer, n_slot))
    n_entry_in = len(flat_in)
    flat_in += list(sums)
    out_shape += [SDS(s.shape[1:], F32) for s in sums]
    n_in = len(flat_in)

    def body(*refs):
        for s_ref, o_ref in zip(refs[n_entry_in:n_in], refs[len(refs) - len(sums):]):
            o_ref[...] = _sum_slots(s_ref)
        i, o = 0, n_in
        for per_layer, n_slot in layout:
            s_refs = refs[i:i + n_slot]
            w_ref, m_ref, v_ref = refs[i + n_slot:i + n_slot + 3]
            outs = refs[o:o + 4]
            if per_layer:
                for l, s_ref in enumerate(s_refs):
                    at = (slice(l, l + 1),) if len(w_ref.shape) == 2 else (l,)
                    g = _sum_slots(s_ref)
                    res = (g,) + _adamw_math(w_ref[at], g, m_ref[at], v_ref[at])
                    for o_ref, val in zip(outs, res):
                        o_ref[at] = val
            else:
                g = _sum_slots(s_refs[0])
                res = (g,) + _adamw_math(w_ref[...], g, m_ref[...], v_ref[...])
                for o_ref, val in zip(outs, res):
                    o_ref[...] = val
            i += n_slot + 3
            o += 4

    kw = {}
    if grid is not None:
        kw = dict(grid=grid, in_specs=in_specs, out_specs=tuple(out_specs),
                  compiler_params=_cp(("parallel",) * len(grid)))
    res = pl.pallas_call(body, name=f"adamw_{tag}", out_shape=tuple(out_shape), **kw)(*flat_in)
    return [tuple(res[4 * e:4 * e + 4]) for e in range(len(entries))], res[4 * len(entries):]


def kernel(x, norm_g, w_in, b_in, ssm_log_dt, ssm_lam_re, ssm_lam_im, ssm_b_re, ssm_b_im, ssm_c_re, ssm_c_im, ssm_d, ssm_w_glu, ssm_b_glu, pool_w, pool_scale, w_branch_a, w_branch_b, w_out, final_norm_g, loss_target, m_norm_g, m_w_in, m_b_in, m_ssm_log_dt, m_ssm_lam_re, m_ssm_lam_im, m_ssm_b_re, m_ssm_b_im, m_ssm_c_re, m_ssm_c_im, m_ssm_d, m_ssm_w_glu, m_ssm_b_glu, m_pool_w, m_pool_scale, m_w_branch_a, m_w_branch_b, m_w_out, m_final_norm_g, v_norm_g, v_w_in, v_b_in, v_ssm_log_dt, v_ssm_lam_re, v_ssm_lam_im, v_ssm_b_re, v_ssm_b_im, v_ssm_c_re, v_ssm_c_im, v_ssm_d, v_ssm_w_glu, v_ssm_b_glu, v_pool_w, v_pool_scale, v_w_branch_a, v_w_branch_b, v_w_out, v_final_norm_g):
    weights = dict(norm_g=norm_g, w_in=w_in, b_in=b_in, ssm_log_dt=ssm_log_dt, ssm_lam_re=ssm_lam_re,
                   ssm_lam_im=ssm_lam_im, ssm_b_re=ssm_b_re, ssm_b_im=ssm_b_im, ssm_c_re=ssm_c_re,
                   ssm_c_im=ssm_c_im, ssm_d=ssm_d, ssm_w_glu=ssm_w_glu, ssm_b_glu=ssm_b_glu, pool_w=pool_w,
                   pool_scale=pool_scale, w_branch_a=w_branch_a, w_branch_b=w_branch_b, w_out=w_out,
                   final_norm_g=final_norm_g.reshape(1, D_MODEL))
    mom_m = dict(norm_g=m_norm_g, w_in=m_w_in, b_in=m_b_in, ssm_log_dt=m_ssm_log_dt, ssm_lam_re=m_ssm_lam_re,
                 ssm_lam_im=m_ssm_lam_im, ssm_b_re=m_ssm_b_re, ssm_b_im=m_ssm_b_im, ssm_c_re=m_ssm_c_re,
                 ssm_c_im=m_ssm_c_im, ssm_d=m_ssm_d, ssm_w_glu=m_ssm_w_glu, ssm_b_glu=m_ssm_b_glu,
                 pool_w=m_pool_w, pool_scale=m_pool_scale, w_branch_a=m_w_branch_a, w_branch_b=m_w_branch_b,
                 w_out=m_w_out, final_norm_g=m_final_norm_g.reshape(1, D_MODEL))
    mom_v = dict(norm_g=v_norm_g, w_in=v_w_in, b_in=v_b_in, ssm_log_dt=v_ssm_log_dt, ssm_lam_re=v_ssm_lam_re,
                 ssm_lam_im=v_ssm_lam_im, ssm_b_re=v_ssm_b_re, ssm_b_im=v_ssm_b_im, ssm_c_re=v_ssm_c_re,
                 ssm_c_im=v_ssm_c_im, ssm_d=v_ssm_d, ssm_w_glu=v_ssm_w_glu, ssm_b_glu=v_ssm_b_glu,
                 pool_w=v_pool_w, pool_scale=v_pool_scale, w_branch_a=v_w_branch_a, w_branch_b=v_w_branch_b,
                 w_out=v_w_out, final_norm_g=v_final_norm_g.reshape(1, D_MODEL))
    order = ["norm_g", "w_in", "b_in", "ssm_log_dt", "ssm_lam_re", "ssm_lam_im", "ssm_b_re", "ssm_b_im",
             "ssm_c_re", "ssm_c_im", "ssm_d", "ssm_w_glu", "ssm_b_glu", "pool_w", "pool_scale", "w_branch_a",
             "w_branch_b", "w_out", "final_norm_g"]
    big_names = ["w_in", "ssm_w_glu", "w_branch_a", "w_branch_b", "w_out"]

    log_dt3 = ssm_log_dt.reshape(DEPTH, N_GROUP, 1)
    b_t = lambda a: a.transpose(0, 1, 3, 2)
    for d in (weights, mom_m, mom_v):
        d["ssm_b_re"], d["ssm_b_im"] = b_t(d["ssm_b_re"]), b_t(d["ssm_b_im"])
    bt_re, bt_im = weights["ssm_b_re"], weights["ssm_b_im"]
    abar_re, abar_im, bbt_re, bbt_im = _s5_params(log_dt3, ssm_lam_re, ssm_lam_im, bt_re, bt_im)
    s5_args = (bbt_re, bbt_im, ssm_c_re, ssm_c_im, abar_re, abar_im, ssm_d)

    w16 = {n: weights[n].astype(BF16) for n in big_names}
    rest = [w16[n] for n in big_names[1:]]
    half = D_MODEL // 2
    wg_in = [None, [None, None]]
    wg_rest = [None, None]
    wg_in[0] = list(_run_carried("gather_w_in_l0", _gather_plan([w16["w_in"]], 0)))
    xs = [x.reshape(SEQ, D_MODEL)]
    saved = []
    for l in range(DEPTH):
        proj, moved = _norm_proj(l, xs[l], norm_g, wg_in[l], b_in,
                                 carry=_gather_plan([w16["w_in"]], 1, rows_of=(0, half)) if l == 0 else None)
        if l == 0:
            (wg_in[1][0],) = moved
        (states, y0), wg_rest[l] = _s5_scan_fwd(l, proj, *s5_args, carry=_gather_plan(rest, l, by_columns=(1, 2)))
        pooled = _pool_fwd(l, proj)
        wg_glu, wg_a, wg_b, wg_out = wg_rest[l]
        x_next, moved = _mix_fwd(l, xs[l], proj, y0, pooled, wg_glu, ssm_b_glu, pool_w, pool_scale, wg_a, wg_b,
                                 wg_out, carry=_gather_plan([w16["w_in"]], 1, rows_of=(half, half)) if l == 0 else None)
        if l == 0:
            (wg_in[1][1],) = moved
        xs.append(x_next)
        saved.append((proj, states, y0, pooled))

    dx, loss_part, g_final = _loss_head(xs[DEPTH], loss_target.reshape(SEQ, D_MODEL), weights["final_norm_g"])

    core = lax.axis_index("c").astype(jnp.int32).reshape(1)
    vec_names = ["norm_g", "b_in", "ssm_d", "ssm_b_glu", "pool_scale", "ssm_log_dt"]
    s5_names = ["ssm_log_dt", "ssm_lam_re", "ssm_lam_im", "ssm_b_re", "ssm_b_im"]
    mat_names = ["pool_w", "ssm_c_re", "ssm_c_im", "ssm_b_re", "ssm_b_im"]
    lane_sparse = ("ssm_c_re", "ssm_c_im", "ssm_b_re", "ssm_b_im")

    def dense(key, a):
        return a.reshape(-1, LANES) if key[0] in lane_sparse else a

    def undense(key, slots):
        return slots.reshape((N_CHIP, N_GROUP, GROUP_W, STATE)) if key[0] in lane_sparse else slots

    def add_small(tag, keys, own, got):
        out = [None] * len(keys)
        whole = [i for i, k in enumerate(keys) if k[0] not in mat_names]
        tiled = [i for i, k in enumerate(keys) if k[0] in mat_names]
        if whole:
            for i, r in zip(whole, _add_lists(f"{tag}_a", [own[i] for i in whole], [got[i] for i in whole])):
                out[i] = r
        if tiled:
            specs = [pl.BlockSpec((1, POOL_GROUP, POOL_GROUP), lambda j: (j, 0, 0)) if keys[i][0] == "pool_w"
                     else pl.BlockSpec((own[i].shape[0] // N_CHUNK, LANES), lambda j: (j, 0)) for i in tiled]
            for i, r in zip(tiled, _add_lists(f"{tag}_b", [own[i] for i in tiled], [got[i] for i in tiled],
                                              grid=(N_CHUNK,), specs=specs, dtype=BF16)):
                out[i] = r
        return out

    sm = {("final_norm_g", None): g_final, ("loss", None): loss_part}
    slots = {}
    grads = {}

    class Wave:
        def __init__(self, tag, layer, big, keys):
            self.tag, self.layer, self.big, self.keys = tag, layer, big, keys

        def to_sibling(self):
            self.own = [dense(k, sm[k]) for k in self.keys]
            return _sibling_plan([(grads[n], self.slab(n), (0, grads[n].shape[2])) for n in self.big], self.own)

        def slab(self, n):
            return self.layer if grads[n].shape[0] == DEPTH else 0

        def add(self, moved):
            nb = len(self.big)
            self.chip_big = [_add_own(f"{self.tag}_{n}", core, grads[n], self.slab(n), b)
                             for n, b in zip(self.big, moved[:nb])]
            self.chip_small = add_small(self.tag, self.keys, self.own, moved[nb:])

        def to_chips(self, big=None, small=True):
            self.sent = list(self.big if big is None else big), small
            return _chips_plan([self.chip_big[self.big.index(n)] for n in self.sent[0]],
                               self.chip_small if small else [])

        def landed(self, moved):
            names, small = self.sent
            for n, s in zip(names, moved[:len(names)]):
                slots[(n, self.layer)] = s
            if small:
                for k, s in zip(self.keys, moved[len(names):]):
                    slots[k] = undense(k, s)
            return moved[len(names) + (len(self.keys) if small else 0):]

    def s5_param_grads(l, g_abar_re, g_abar_im, g_bbt_re, g_bbt_im):
        g = _s5_params_bwd(l, log_dt3, ssm_lam_re, ssm_lam_im, bt_re, bt_im, g_abar_re, g_abar_im, g_bbt_re, g_bbt_im)
        sm[("ssm_log_dt", l)] = g[0].reshape(1, N_GROUP)
        for n, a in zip(s5_names[1:], g[1:]):
            sm[(n, l)] = a

    small1 = ["b_in", "ssm_d", "ssm_b_glu", "pool_scale", "pool_w", "ssm_c_re", "ssm_c_im"] + s5_names
    w1 = Wave("chip1", 1, list(big_names), [(n, 1) for n in small1] + [("final_norm_g", None), ("loss", None)])
    early = Wave("chip0e", 0, big_names[1:], [("pool_w", 0), ("pool_scale", 0), ("ssm_b_glu", 0)])
    mid = Wave("chip0m", 0, [], [(n, 0) for n in ["ssm_c_re", "ssm_c_im", "ssm_d"] + s5_names] + [("norm_g", 1)])
    half_rows = D_MODEL // 2
    late_a = Wave("chip0la", 0, ["w_in_a"], [("b_in", 0)])
    late_b = Wave("chip0lb", 0, ["w_in_b"], [])

    mix_prev = None
    for l in reversed(range(DEPTH)):
        proj, states, y0, pooled = saved[l]
        wg_glu, wg_a, wg_b, wg_out = wg_rest[l]
        res, moved = _mix_bwd(l, dx, proj, y0, pooled, wg_glu, ssm_b_glu, pool_w, pool_scale, wg_a, wg_b, wg_out,
                              mix_prev, carry=None if l == 1 else w1.to_chips(big=["w_in"], small=False))
        if l == 0:
            w1.landed(moved)
        dproj, dy0, dpooled = res[:3]
        mix_prev = list(res[3:7])
        grads["w_out"], grads["w_branch_a"], grads["w_branch_b"], grads["ssm_w_glu"] = mix_prev
        sm[("pool_w", l)], sm[("pool_scale", l)], sm[("ssm_b_glu", l)] = res[7:]
        dproj = _pool_bwd(l, dpooled, dproj)
        carry = None if l == 1 else _join(w1.to_chips(big=big_names[1:]), early.to_sibling())
        res, moved = _s5_scan_bwd(l, dy0, proj, states, *s5_args, dproj, carry=carry)
        if l == 0:
            early.add(w1.landed(moved))
        dproj, g_bbt_re, g_bbt_im, sm[("ssm_c_re", l)], sm[("ssm_c_im", l)], g_abar_re, g_abar_im, sm[("ssm_d", l)] = res
        s5_param_grads(l, g_abar_re, g_abar_im, g_bbt_re, g_bbt_im)
        if l == 1:
            (grads["w_in"], sm[("b_in", l)]), _ = _proj_wgrad(l, xs[l], norm_g, dproj)
            carry = w1.to_sibling()
        else:
            (grads["w_in_a"], sm[("b_in", l)]), moved = _proj_wgrad(
                l, xs[l], norm_g, dproj, (0, half_rows), carry=_join(early.to_chips(), mid.to_sibling()))
            mid.add(early.landed(moved))
            (grads["w_in_b"], _), moved = _proj_wgrad(
                l, xs[l], norm_g, dproj, (half_rows, half_rows), carry=_join(mid.to_chips(), late_a.to_sibling()))
            late_a.add(mid.landed(moved))
            carry = _join(late_a.to_chips(), late_b.to_sibling())
        (dx, sm[("norm_g", l)]), moved = _proj_dgrad(l, dx, xs[l], norm_g, dproj, wg_in[l], carry=carry)
        if l == 1:
            w1.add(moved)
        else:
            late_b.add(late_a.landed(moved))
    grad_x = dx.reshape(1, SEQ, D_MODEL)
    moved = late_b.landed(_run_carried("exchange_last", _join(late_b.to_chips(), _all_plan([sm[("norm_g", 0)]]))))
    slots[("norm_g", 0)] = moved[0]
    slots[("w_in", 0)] = jnp.concatenate([slots[("w_in_a", 0)], slots[("w_in_b", 0)]], axis=1)

    res = {}
    for n in big_names:
        res[n] = _sum_slots_adamw(n, [slots[(n, l)] for l in range(DEPTH)], weights[n], mom_m[n], mom_v[n])
    per_layer = lambda n: [slots[(n, l)] for l in range(DEPTH)]
    names_a = vec_names + ["ssm_lam_re", "ssm_lam_im"]
    entries_a = [(per_layer(n), weights[n], mom_m[n], mom_v[n], None, None) for n in names_a]
    n = "final_norm_g"
    entries_a.append((slots[(n, None)], weights[n], mom_m[n], mom_v[n], None, None))
    out_a, (loss,) = _adamw_small("small_a", entries_a, sums=[slots[("loss", None)]])
    loss = loss.reshape(())
    for n, r in zip(names_a + ["final_norm_g"], out_a):
        res[n] = r
    res["final_norm_g"] = tuple(a.reshape(D_MODEL) for a in res["final_norm_g"])
    pw_s = pl.BlockSpec((N_CHIP, 1, POOL_GROUP, POOL_GROUP), lambda j: (0, j, 0, 0))
    pw_w = pl.BlockSpec((DEPTH, 1, POOL_GROUP, POOL_GROUP), lambda j: (0, j, 0, 0))
    c_s = pl.BlockSpec((N_CHIP, CH_G, GROUP_W, STATE), lambda j: (0, j, 0, 0))
    c_w = pl.BlockSpec((DEPTH, CH_G, GROUP_W, STATE), lambda j: (0, j, 0, 0))
    entries_b = [(per_layer(n), weights[n], mom_m[n], mom_v[n], pw_s if n == "pool_w" else c_s,
                  pw_w if n == "pool_w" else c_w) for n in mat_names]
    out_b, _ = _adamw_small("small_b", entries_b, grid=(N_CHUNK,))
    for n, r in zip(mat_names, out_b):
        res[n] = tuple(b_t(a) for a in r) if n in ("ssm_b_re", "ssm_b_im") else r

    outs = [loss, grad_x]
    for i in range(4):
        outs += [res[n][i] for n in order]
    return tuple(outs)
```

```python
import math

import jax
import jax.numpy as jnp
from jax import lax
from jax.experimental import pallas as pl
from jax.experimental.pallas import tpu as pltpu

F32 = jnp.float32
BF16 = jnp.bfloat16

SEQ = 2048
D_MODEL = 1024
N_IN = 4096
WIDTH = 512
N_GROUP = 32
GROUP_W = 16
STATE = 64
N_STATE = N_GROUP * STATE
N_CHUNK = 4
CH_G = N_GROUP // N_CHUNK
CH_W = WIDTH // N_CHUNK
CH_S = N_STATE // N_CHUNK
N_DEV = 8
N_CHIP = 4
POOL_WINDOWS = (2, 4, 8, 16)
POOL_GROUP = 128
EPS = 1e-6
DEPTH = 2

ADAM_LR = 0.001
ADAM_B1 = 0.9
ADAM_B2 = 0.999
ADAM_EPS = 1e-08
ADAM_WD = 0.01
ADAM_STEP = 10

LANES = 128
SUBLANES = 8
TILE_M = 256
VMEM_LIMIT = 48 * 1024 * 1024
VMEM_LIMIT_BIG = 60 * 1024 * 1024
MESH = pl.DeviceIdType.MESH
ANY = pl.BlockSpec(memory_space=pl.ANY)

GELU_C = math.sqrt(2.0 / math.pi)
GELU_A = 0.044715

SDS = jax.ShapeDtypeStruct


def _cp(sem=None, limit=VMEM_LIMIT):
    return pltpu.CompilerParams(dimension_semantics=sem, vmem_limit_bytes=limit)


def _dot(a, b):
    return jnp.dot(a, b, preferred_element_type=F32)


def _dot_nt(a, b):
    return lax.dot_general(a, b, (((1,), (1,)), ((), ())), preferred_element_type=F32)


def _dot_tn(a, b):
    return lax.dot_general(a, b, (((0,), (0,)), ((), ())), preferred_element_type=F32)


def _sig(x):
    return jax.nn.sigmoid(x)


def _rms(x):
    rs = lax.rsqrt(jnp.mean(x * x, axis=-1, keepdims=True) + EPS)
    return rs, x * rs


def _slot(n):
    return 4 * (n % 2) + n // 2


def _const(shape):
    n = len(shape)
    return pl.BlockSpec(shape, lambda *_: (0,) * n)


def _pair_sum(vals):
    while len(vals) > 1:
        vals = [vals[i] + vals[i + 1] for i in range(0, len(vals), 2)]
    return vals[0]


def _sum_slots(s_ref):
    return _pair_sum([s_ref[k].astype(F32) for k in range(s_ref.shape[0])])


def _s5_param_fn(log_dt, lam_re, lam_im, bt_re, bt_im):
    dt = jnp.exp(log_dt)
    mag = jnp.exp(lam_re * dt)
    ang = lam_im * dt
    abar_re = mag * jnp.cos(ang)
    abar_im = mag * jnp.sin(ang)
    num_re = abar_re - 1.0
    num_im = abar_im
    den = lam_re * lam_re + lam_im * lam_im
    coef_re = (num_re * lam_re + num_im * lam_im) / den
    coef_im = (num_im * lam_re - num_re * lam_im) / den
    bbar_re = coef_re[..., None, :] * bt_re - coef_im[..., None, :] * bt_im
    bbar_im = coef_re[..., None, :] * bt_im + coef_im[..., None, :] * bt_re
    return abar_re, abar_im, bbar_re, bbar_im


def _s5_params(log_dt, lam_re, lam_im, bt_re, bt_im):
    def body(ld, lr, li, br, bi, o_ar, o_ai, o_br, o_bi):
        ar, ai, bbr, bbi = _s5_param_fn(ld[...], lr[...], li[...], br[...], bi[...])
        o_ar[...] = ar
        o_ai[...] = ai
        o_br[...] = bbr
        o_bi[...] = bbi

    return pl.pallas_call(
        body, name="s5_params",
        out_shape=(SDS(lam_re.shape, F32), SDS(lam_re.shape, F32), SDS(bt_re.shape, F32), SDS(bt_re.shape, F32)),
    )(log_dt, lam_re, lam_im, bt_re, bt_im)


def _s5_params_bwd(layer, log_dt, lam_re, lam_im, bt_re, bt_im, g_ar, g_ai, g_br, g_bi):
    def body(ld, lr, li, br, bi, car, cai, cbr, cbi, o_ld, o_lr, o_li, o_br, o_bi):
        _, vjp = jax.vjp(_s5_param_fn, ld[...], lr[...], li[...], br[...], bi[...])
        d_ld, d_lr, d_li, d_br, d_bi = vjp((car[...], cai[...], cbr[...], cbi[...]))
        o_ld[...] = d_ld
        o_lr[...] = d_lr
        o_li[...] = d_li
        o_br[...] = d_br
        o_bi[...] = d_bi

    one = lambda shape: pl.BlockSpec((None,) + shape, lambda i: (layer,) + (0,) * len(shape))
    whole = lambda shape: _const(shape)
    vec, lam, mat = (N_GROUP, 1), (N_GROUP, STATE), (N_GROUP, GROUP_W, STATE)
    return pl.pallas_call(
        body, name=f"s5_params_bwd_l{layer}", grid=(1,),
        in_specs=[one(vec), one(lam), one(lam), one(mat), one(mat), whole(lam), whole(lam), whole(mat), whole(mat)],
        out_specs=(whole(vec), whole(lam), whole(lam), whole(mat), whole(mat)),
        out_shape=(SDS(vec, F32), SDS(lam, F32), SDS(lam, F32), SDS(mat, F32), SDS(mat, F32)),
    )(log_dt, lam_re, lam_im, bt_re, bt_im, g_ar, g_ai, g_br, g_bi)


def _norm_proj(layer, x, norm_g, wg_in, b_in, carry=None):
    n_w = len(wg_in)

    def body(x_ref, g_ref, b_ref, *refs):
        w_refs, o_ref = refs[:n_w], refs[n_w]
        _, xn = _rms(x_ref[...])
        h = (xn * g_ref[layer:layer + 1, :]).astype(BF16)
        for k in range(N_DEV):
            cols = slice(k * WIDTH, (k + 1) * WIDTH)
            acc = b_ref[layer:layer + 1, cols]
            row = 0
            for w_ref in w_refs:
                rows = w_ref.shape[1]
                acc = acc + _dot(h[:, row:row + rows], w_ref[k])
                row += rows
            o_ref[:, cols] = acc

    (proj,), moved = _pcall(
        body, name=f"norm_proj_l{layer}",
        out_shape=[SDS((SEQ, N_IN), F32)],
        grid=(SEQ // TILE_M,),
        in_specs=[pl.BlockSpec((TILE_M, D_MODEL), lambda i: (i, 0)),
                  _const((DEPTH, D_MODEL)),
                  _const((DEPTH, N_IN))] + [_const(w.shape) for w in wg_in],
        out_specs=[pl.BlockSpec((TILE_M, N_IN), lambda i: (i, 0))],
        args=[x, norm_g, b_in, *wg_in], sem=("parallel",), carry=carry)
    return proj, moved


TIME_BLK = 512
N_TBLK = SEQ // TIME_BLK
N_PANEL = CH_S // LANES
STATE_SHAPE = (N_PANEL, SEQ * SUBLANES, LANES)


def _s5_layer_specs(layer):
    mat = lambda: pl.BlockSpec((None, N_GROUP, GROUP_W, STATE), lambda i: (layer, 0, 0, 0))
    ab = lambda: pl.BlockSpec((None, N_GROUP, STATE), lambda i: (layer, 0, 0))
    return [mat(), mat(), mat(), mat(), ab(), ab(), _const((DEPTH, WIDTH))]


def _s5_layer_scratch():
    return [pltpu.VMEM((N_CHUNK, CH_W, CH_S), BF16)] * 4 + [pltpu.VMEM((8, CH_S), F32)] * 2


def _s5_layer_fill(btre_ref, btim_ref, cre_ref, cim_ref, are_ref, aim_ref, bdre, bdim, ctre, ctim, a1, a2):
    for m in (bdre, bdim, ctre, ctim):
        m[...] = jnp.zeros_like(m)
    for grp in range(N_GROUP):
        k, g = divmod(grp, CH_G)
        rows = slice(g * GROUP_W, (g + 1) * GROUP_W)
        cols = slice(g * STATE, (g + 1) * STATE)
        bdre[k, rows, cols] = btre_ref[grp].astype(BF16)
        bdim[k, rows, cols] = btim_ref[grp].astype(BF16)
        ctre[k, rows, cols] = cre_ref[grp].astype(BF16)
        ctim[k, rows, cols] = cim_ref[grp].astype(BF16)
        ar = are_ref[grp:grp + 1, :]
        ai = aim_ref[grp:grp + 1, :]
        a1[k:k + 1, cols] = ar
        a1[N_CHUNK + k:N_CHUNK + k + 1, cols] = ar
        a2[k:k + 1, cols] = -ai
        a2[N_CHUNK + k:N_CHUNK + k + 1, cols] = ai


SCAN_UNROLL = 16


def _panels(tile):
    return [tile[:, p * LANES:(p + 1) * LANES] for p in range(N_PANEL)]


def _rows_load(ref, row):
    return jnp.concatenate([ref[p, pl.ds(row, TIME_BLK, stride=SUBLANES), :] for p in range(N_PANEL)], axis=1)


def _rows_store(ref, row, val):
    for p in range(N_PANEL):
        ref[p, pl.ds(row, TIME_BLK, stride=SUBLANES), :] = val[:, p * LANES:(p + 1) * LANES]


def _s5_scan_fwd(layer, proj, bbt_re, bbt_im, c_re, c_im, abar_re, abar_im, d_skip, carry=None):
    def body(u_ref, btre_ref, btim_ref, cre_ref, cim_ref, are_ref, aim_ref, d_ref, s_ref, y_ref,
             bdre, bdim, ctre, ctim, a1, a2, state):
        @pl.when(pl.program_id(0) == 0)
        def _():
            _s5_layer_fill(btre_ref, btim_ref, cre_ref, cim_ref, are_ref, aim_ref, bdre, bdim, ctre, ctim, a1, a2)
            state[...] = jnp.zeros_like(state)

        for k in range(N_CHUNK):
            ub = u_ref[:, k * CH_W:(k + 1) * CH_W].astype(BF16)
            _rows_store(s_ref, k, _dot(ub, bdre[k]))
            _rows_store(s_ref, N_CHUNK + k, _dot(ub, bdim[k]))
        m1 = _panels(a1[...])
        m2 = _panels(a2[...])

        def steps(n, tile):
            for r in range(SCAN_UNROLL):
                rows = pl.ds(pl.multiple_of((n * SCAN_UNROLL + r) * 8, 8), 8)
                tile = [m1[p] * tile[p] + m2[p] * pltpu.roll(tile[p], N_CHUNK, 0) + s_ref[p, rows, :]
                        for p in range(N_PANEL)]
                for p in range(N_PANEL):
                    s_ref[p, rows, :] = tile[p]
            return tile

        tile = lax.fori_loop(0, TIME_BLK // SCAN_UNROLL, steps, _panels(state[...]))
        state[...] = jnp.concatenate(tile, axis=1)
        d = d_ref[layer:layer + 1, :]
        for k in range(N_CHUNK):
            cols = slice(k * CH_W, (k + 1) * CH_W)
            y = (_dot_nt(_rows_load(s_ref, k).astype(BF16), ctre[k])
                 - _dot_nt(_rows_load(s_ref, N_CHUNK + k).astype(BF16), ctim[k]))
            y_ref[:, cols] = y + d[:, cols] * u_ref[:, cols]

    return _pcall(
        body, name=f"s5_fwd_l{layer}",
        out_shape=(SDS(STATE_SHAPE, F32), SDS((SEQ, WIDTH), F32)),
        grid=(N_TBLK,),
        in_specs=[pl.BlockSpec((TIME_BLK, WIDTH), lambda i: (i, 0))] + _s5_layer_specs(layer),
        out_specs=(pl.BlockSpec((N_PANEL, TIME_BLK * SUBLANES, LANES), lambda i: (0, i, 0)),
                   pl.BlockSpec((TIME_BLK, WIDTH), lambda i: (i, 0))),
        scratch_shapes=_s5_layer_scratch() + [pltpu.VMEM((8, CH_S), F32)],
        args=[proj, bbt_re, bbt_im, c_re, c_im, abar_re, abar_im, d_skip], sem=("arbitrary",), carry=carry)


def _s5_scan_bwd(layer, dy0, proj, states, bbt_re, bbt_im, c_re, c_im, abar_re, abar_im, d_skip, dproj,
                 carry=None):
    def body(dy_ref, u_ref, s_ref, sprev_ref, btre_ref, btim_ref, cre_ref, cim_ref, are_ref, aim_ref, d_ref, _,
             du_ref, gbre_ref, gbim_ref, gcre_ref, gcim_ref, gare_ref, gaim_ref, gd_ref,
             lam_ref, bdre, bdim, ctre, ctim, a1, a2, state, acc1, acc2, gbre, gbim, gcre, gcim, gd):
        step_id = pl.program_id(0)

        @pl.when(step_id == 0)
        def _():
            _s5_layer_fill(btre_ref, btim_ref, cre_ref, cim_ref, are_ref, aim_ref, bdre, bdim, ctre, ctim, a1, a2)
            for r in (state, acc1, acc2, gbre, gbim, gcre, gcim, gd):
                r[...] = jnp.zeros_like(r)

        for k in range(N_CHUNK):
            dyb = dy_ref[:, k * CH_W:(k + 1) * CH_W].astype(BF16)
            _rows_store(lam_ref, k, _dot(dyb, ctre[k]))
            _rows_store(lam_ref, N_CHUNK + k, -_dot(dyb, ctim[k]))
            gcre[k] += _dot_tn(dyb, _rows_load(s_ref, k).astype(BF16))
            gcim[k] -= _dot_tn(dyb, _rows_load(s_ref, N_CHUNK + k).astype(BF16))

        m1 = _panels(a1[...])
        m2 = _panels(-a2[...])
        has_before = (step_id < N_TBLK - 1).astype(F32)

        def one(t8, c, first_token):
            tile, swapped, p1, p2 = c
            rows = pl.ds(t8, 8)
            tile = [m1[p] * tile[p] + m2[p] * swapped[p] + lam_ref[p, rows, :] for p in range(N_PANEL)]
            swapped = [pltpu.roll(tile[p], N_CHUNK, 0) for p in range(N_PANEL)]
            for p in range(N_PANEL):
                lam_ref[p, rows, :] = tile[p]
            if first_token:
                before = [sprev_ref[p] * has_before for p in range(N_PANEL)]
            else:
                before = [s_ref[p, pl.ds(t8 - 8, 8), :] for p in range(N_PANEL)]
            p1 = [p1[p] + tile[p] * before[p] for p in range(N_PANEL)]
            p2 = [p2[p] + swapped[p] * before[p] for p in range(N_PANEL)]
            return tile, swapped, p1, p2

        def steps(n, c):
            for r in range(SCAN_UNROLL):
                t8 = pl.multiple_of((TIME_BLK - 1 - (n * SCAN_UNROLL + r)) * 8, 8)
                c = one(t8, c, False)
            return c

        tile0 = _panels(state[...])
        c = (tile0, [pltpu.roll(t, N_CHUNK, 0) for t in tile0], _panels(acc1[...]), _panels(acc2[...]))
        c = lax.fori_loop(0, TIME_BLK // SCAN_UNROLL - 1, steps, c)
        for r in range(SCAN_UNROLL - 1, -1, -1):
            c = one(r * 8, c, r == 0)
        state[...] = jnp.concatenate(c[0], axis=1)
        acc1[...] = jnp.concatenate(c[2], axis=1)
        acc2[...] = jnp.concatenate(c[3], axis=1)

        d = d_ref[layer:layer + 1, :]
        for k in range(N_CHUNK):
            cols = slice(k * CH_W, (k + 1) * CH_W)
            lrb = _rows_load(lam_ref, k).astype(BF16)
            lib = _rows_load(lam_ref, N_CHUNK + k).astype(BF16)
            u = u_ref[:, cols]
            ub = u.astype(BF16)
            dy = dy_ref[:, cols]
            du = dy * d[:, cols] + _dot_nt(lrb, bdre[k]) + _dot_nt(lib, bdim[k])
            du_ref[:, cols] = du.astype(BF16)
            gbre[k] += _dot_tn(ub, lrb)
            gbim[k] += _dot_tn(ub, lib)
        gd[...] += jnp.sum(dy_ref[...] * u_ref[...], axis=0, keepdims=True)

        @pl.when(step_id == N_TBLK - 1)
        def _():
            gd_ref[...] = gd[...]
            ga_re = acc1[0:N_CHUNK, :] + acc1[N_CHUNK:, :]
            ga_im = acc2[0:N_CHUNK, :] - acc2[N_CHUNK:, :]
            for grp in range(N_GROUP):
                k, g = divmod(grp, CH_G)
                rows = slice(g * GROUP_W, (g + 1) * GROUP_W)
                cols = slice(g * STATE, (g + 1) * STATE)
                gcre_ref[grp] = gcre[k, rows, cols]
                gcim_ref[grp] = gcim[k, rows, cols]
                gbre_ref[grp] = gbre[k, rows, cols]
                gbim_ref[grp] = gbim[k, rows, cols]
                gare_ref[grp:grp + 1, :] = ga_re[k:k + 1, cols]
                gaim_ref[grp:grp + 1, :] = ga_im[k:k + 1, cols]

    back = lambda i: N_TBLK - 1 - i
    tok = lambda: pl.BlockSpec((TIME_BLK, WIDTH), lambda i: (back(i), 0))
    mat = lambda: _const((N_GROUP, GROUP_W, STATE))
    acc_mat = pltpu.VMEM((N_CHUNK, CH_W, CH_S), F32)
    return _pcall(
        body, name=f"s5_bwd_l{layer}",
        out_shape=(SDS((SEQ, N_IN), BF16), SDS((N_GROUP, GROUP_W, STATE), F32), SDS((N_GROUP, GROUP_W, STATE), F32),
                   SDS((N_GROUP, GROUP_W, STATE), F32), SDS((N_GROUP, GROUP_W, STATE), F32),
                   SDS((N_GROUP, STATE), F32), SDS((N_GROUP, STATE), F32), SDS((1, WIDTH), F32)),
        grid=(N_TBLK,),
        in_specs=[tok(), tok(),
                  pl.BlockSpec((N_PANEL, TIME_BLK * SUBLANES, LANES), lambda i: (0, back(i), 0)),
                  pl.BlockSpec((N_PANEL, SUBLANES, LANES), lambda i: (0, jnp.maximum(back(i) * TIME_BLK - 1, 0), 0))]
        + _s5_layer_specs(layer) + [ANY],
        out_specs=(tok(), mat(), mat(), mat(), mat(), _const((N_GROUP, STATE)), _const((N_GROUP, STATE)),
                   _const((1, WIDTH))),
        scratch_shapes=[pltpu.VMEM((N_PANEL, TIME_BLK * SUBLANES, LANES), F32)] + _s5_layer_scratch()
        + [pltpu.VMEM((8, CH_S), F32)] * 3 + [acc_mat] * 4 + [pltpu.VMEM((1, WIDTH), F32)],
        args=[dy0, proj, states, states, bbt_re, bbt_im, c_re, c_im, abar_re, abar_im, d_skip, dproj],
        aliases={11: 0}, sem=("arbitrary",), limit=VMEM_LIMIT_BIG, carry=carry)


def _pool_counts(win):
    t = lax.broadcasted_iota(jnp.int32, (SEQ, POOL_GROUP), 0)
    return t, jnp.minimum(t + 1, win).astype(F32)


def _pool_fwd(layer, proj):
    def body(u_ref, o_ref):
        for gi, win in enumerate(POOL_WINDOWS):
            cols = slice(gi * POOL_GROUP, (gi + 1) * POOL_GROUP)
            u = u_ref[:, cols]
            t, count = _pool_counts(win)
            acc = u
            k = 1
            while k < win:
                acc = acc + jnp.where(t >= k, pltpu.roll(acc, k, 0), 0.0)
                k *= 2
            o_ref[:, cols] = acc / count - u

    return pl.pallas_call(
        body, name=f"pool_fwd_l{layer}",
        out_shape=SDS((SEQ, WIDTH), F32),
        grid=(1,),
        in_specs=[pl.BlockSpec((SEQ, WIDTH), lambda i: (0, 2))],
        out_specs=pl.BlockSpec((SEQ, WIDTH), lambda i: (0, 0)),
        compiler_params=_cp(("arbitrary",)),
    )(proj)


def _gelu_parts(y0):
    t = jnp.tanh(GELU_C * (y0 + GELU_A * (y0 * y0 * y0)))
    return t, 0.5 * y0 * (1.0 + t)


def _mix_forward(layer, p_ref, y0_ref, pooled_ref, wglu_ref, bglu_ref, pw_ref, scale_ref, wa_ref, wb_ref):
    za = p_ref[:, WIDTH:2 * WIDTH]
    zb = p_ref[:, 3 * WIDTH:4 * WIDTH]
    ga = p_ref[:, 4 * WIDTH:4 * WIDTH + D_MODEL]
    gb = p_ref[:, 4 * WIDTH + D_MODEL:]
    y0 = y0_ref[...]
    t, y1 = _gelu_parts(y0)
    y1b = y1.astype(BF16)
    q = _dot(y1b, wglu_ref[...].reshape(WIDTH, WIDTH)) + bglu_ref[layer:layer + 1, :]
    sq = _sig(q)
    y2 = y1 * sq
    sza = _sig(za)
    silu_za = za * sza
    ya = y2 * silu_za
    pooled = pooled_ref[...]
    mixed = jnp.concatenate(
        [_dot(pooled[:, g * POOL_GROUP:(g + 1) * POOL_GROUP].astype(BF16), pw_ref[g].astype(BF16))
         for g in range(len(POOL_WINDOWS))], axis=1)
    szb = _sig(zb)
    silu_zb = zb * szb
    scale = scale_ref[layer:layer + 1, :]
    ms = mixed * scale
    yb = ms * silu_zb
    yab = ya.astype(BF16)
    ybb = yb.astype(BF16)
    ma = _dot(yab, wa_ref[...])
    mb = _dot(ybb, wb_ref[...])
    sga = _sig(ga)
    sgb = _sig(gb)
    merged = sga * ma + sgb * mb
    return dict(za=za, zb=zb, y0=y0, t=t, y1=y1, y1b=y1b, sq=sq, y2=y2, sza=sza, silu_za=silu_za,
                pooled=pooled, mixed=mixed, szb=szb, silu_zb=silu_zb, scale=scale, ms=ms, yab=yab, ybb=ybb,
                ma=ma, mb=mb, sga=sga, sgb=sgb, merged=merged)


def _mix_weight_specs(layer):
    return [_const((N_DEV, WIDTH // N_DEV, WIDTH)),
            _const((DEPTH, WIDTH)),
            pl.BlockSpec((None, 4, POOL_GROUP, POOL_GROUP), lambda i: (layer, 0, 0, 0)),
            _const((DEPTH, WIDTH)),
            _const((WIDTH, D_MODEL)),
            _const((WIDTH, D_MODEL)),
            _const((N_DEV, D_MODEL // N_DEV, D_MODEL))]


def _mix_fwd(layer, x, proj, y0, pooled, wg_glu, b_glu, pool_w, pool_scale, wg_a, wg_b, wg_out, carry=None):
    def body(x_ref, p_ref, y0_ref, pooled_ref, wglu_ref, bglu_ref, pw_ref, scale_ref, wa_ref, wb_ref,
             wout_ref, o_ref):
        f = _mix_forward(layer, p_ref, y0_ref, pooled_ref, wglu_ref, bglu_ref, pw_ref, scale_ref, wa_ref, wb_ref)
        wout = wout_ref[...].reshape(D_MODEL, D_MODEL)
        o_ref[...] = x_ref[...] + _dot(f["merged"].astype(BF16), wout)

    (x_next,), moved = _pcall(
        body, name=f"mix_fwd_l{layer}",
        out_shape=[SDS((SEQ, D_MODEL), F32)],
        grid=(SEQ // TILE_M,),
        in_specs=[pl.BlockSpec((TILE_M, D_MODEL), lambda i: (i, 0)),
                  pl.BlockSpec((TILE_M, N_IN), lambda i: (i, 0)),
                  pl.BlockSpec((TILE_M, WIDTH), lambda i: (i, 0)),
                  pl.BlockSpec((TILE_M, WIDTH), lambda i: (i, 0))] + _mix_weight_specs(layer),
        out_specs=[pl.BlockSpec((TILE_M, D_MODEL), lambda i: (i, 0))],
        args=[x, proj, y0, pooled, wg_glu, b_glu, pool_w, pool_scale, wg_a, wg_b, wg_out],
        sem=("parallel",), carry=carry)
    return x_next, moved


def _loss_head(x, target, final_g):
    def body(x_ref, t_ref, g_ref, dx_ref, loss_ref, gg_ref):
        @pl.when(pl.program_id(0) == 0)
        def _():
            loss_ref[...] = jnp.zeros_like(loss_ref)
            gg_ref[...] = jnp.zeros_like(gg_ref)

        g = g_ref[...]
        rs, xn = _rms(x_ref[...])
        err = xn * g - t_ref[...]
        loss_ref[...] += 0.5 * jnp.sum(jnp.mean(err * err, axis=-1, keepdims=True), axis=0, keepdims=True)
        dy = err * (1.0 / D_MODEL)
        gg_ref[...] += jnp.sum(dy * xn, axis=0, keepdims=True)
        dxn = dy * g
        dx_ref[...] = rs * (dxn - xn * jnp.mean(dxn * xn, axis=-1, keepdims=True))

    return pl.pallas_call(
        body, name="loss_head",
        out_shape=(SDS((SEQ, D_MODEL), F32), SDS((1, 1), F32), SDS((1, D_MODEL), F32)),
        grid=(SEQ // TILE_M,),
        in_specs=[pl.BlockSpec((TILE_M, D_MODEL), lambda i: (i, 0)),
                  pl.BlockSpec((TILE_M, D_MODEL), lambda i: (i, 0)),
                  _const((1, D_MODEL))],
        out_specs=(pl.BlockSpec((TILE_M, D_MODEL), lambda i: (i, 0)), _const((1, 1)), _const((1, D_MODEL))),
        compiler_params=_cp(("arbitrary",)),
    )(x, target, final_g)


def _big_shapes():
    return dict(w_out=(DEPTH, N_DEV, D_MODEL // N_DEV, D_MODEL), w_branch_a=(DEPTH, N_DEV, WIDTH, D_MODEL // N_DEV),
                w_branch_b=(DEPTH, N_DEV, WIDTH, D_MODEL // N_DEV), ssm_w_glu=(DEPTH, N_DEV, WIDTH // N_DEV, WIDTH),
                w_in=(DEPTH, N_DEV, D_MODEL, WIDTH))


def _mix_bwd(layer, dx_next, proj, y0, pooled, wg_glu, b_glu, pool_w, pool_scale, wg_a, wg_b, wg_out, prev,
             carry=None):
    n_k = N_DEV
    n_prev = 0 if prev is None else len(prev)

    def body(*refs):
        (dx_ref, p_ref, y0_ref, pooled_ref, wglu_ref, bglu_ref, pw_ref, scale_ref, wa_ref, wb_ref,
         wout_ref) = refs[:11]
        (dproj_ref, dy0_ref, dpooled_ref, gwout_ref, gwa_ref, gwb_ref, gwglu_ref, gpw_ref,
         gscale_ref, gbglu_ref) = refs[11 + n_prev:]

        @pl.when(pl.program_id(0) == 0)
        def _():
            for r in (gwout_ref, gwa_ref, gwb_ref, gwglu_ref, gpw_ref, gscale_ref, gbglu_ref):
                r[...] = jnp.zeros_like(r)

        f = _mix_forward(layer, p_ref, y0_ref, pooled_ref, wglu_ref, bglu_ref, pw_ref, scale_ref, wa_ref, wb_ref)
        wglu = wglu_ref[...].reshape(WIDTH, WIDTH)
        wout = wout_ref[...].reshape(D_MODEL, D_MODEL)
        blk = D_MODEL // n_k
        dxb = dx_ref[...].astype(BF16)
        dmerged = _dot_nt(dxb, wout)
        gwout = _dot_tn(f["merged"].astype(BF16), dxb)
        for k in range(n_k):
            gwout_ref[_slot(k)] += gwout[k * blk:(k + 1) * blk, :]
        dma = dmerged * f["sga"]
        dmb = dmerged * f["sgb"]
        dga = dmerged * f["ma"] * f["sga"] * (1.0 - f["sga"])
        dgb = dmerged * f["mb"] * f["sgb"] * (1.0 - f["sgb"])
        dmab = dma.astype(BF16)
        dmbb = dmb.astype(BF16)
        dya = _dot_nt(dmab, wa_ref[...])
        dyb = _dot_nt(dmbb, wb_ref[...])
        gwa = _dot_tn(f["yab"], dmab)
        gwb = _dot_tn(f["ybb"], dmbb)
        for k in range(n_k):
            gwa_ref[_slot(k)] += gwa[:, k * blk:(k + 1) * blk]
            gwb_ref[_slot(k)] += gwb[:, k * blk:(k + 1) * blk]
        zb, szb = f["zb"], f["szb"]
        dzb = dyb * f["ms"] * (szb * (1.0 + zb * (1.0 - szb)))
        dms = dyb * f["silu_zb"]
        gscale_ref[...] += jnp.sum(dms * f["mixed"], axis=0, keepdims=True)
        dmixed = (dms * f["scale"]).astype(BF16)
        pooled = f["pooled"]
        for g in range(len(POOL_WINDOWS)):
            cols = slice(g * POOL_GROUP, (g + 1) * POOL_GROUP)
            dpooled_ref[:, cols] = _dot_nt(dmixed[:, cols], pw_ref[g].astype(BF16))
            gpw_ref[g] += _dot_tn(pooled[:, cols].astype(BF16), dmixed[:, cols])
        za, sza = f["za"], f["sza"]
        dza = dya * f["y2"] * (sza * (1.0 + za * (1.0 - sza)))
        dy2 = dya * f["silu_za"]
        sq = f["sq"]
        dq = dy2 * f["y1"] * sq * (1.0 - sq)
        dqb = dq.astype(BF16)
        dy1 = dy2 * sq + _dot_nt(dqb, wglu)
        gwglu = _dot_tn(f["y1b"], dqb)
        rblk = WIDTH // n_k
        for k in range(n_k):
            gwglu_ref[_slot(k)] += gwglu[k * rblk:(k + 1) * rblk, :]
        gbglu_ref[...] += jnp.sum(dq, axis=0, keepdims=True)
        y0, t = f["y0"], f["t"]
        dgelu = 0.5 * (1.0 + t) + 0.5 * y0 * (1.0 - t * t) * (GELU_C * (1.0 + 3.0 * GELU_A * y0 * y0))
        dy0_ref[...] = dy1 * dgelu
        zeros = jnp.zeros((TILE_M, WIDTH), BF16)
        dproj_ref[:, 0:WIDTH] = zeros
        dproj_ref[:, WIDTH:2 * WIDTH] = dza.astype(BF16)
        dproj_ref[:, 2 * WIDTH:3 * WIDTH] = zeros
        dproj_ref[:, 3 * WIDTH:4 * WIDTH] = dzb.astype(BF16)
        dproj_ref[:, 4 * WIDTH:4 * WIDTH + D_MODEL] = dga.astype(BF16)
        dproj_ref[:, 4 * WIDTH + D_MODEL:] = dgb.astype(BF16)

    tile = lambda w: pl.BlockSpec((TILE_M, w), lambda i: (i, 0))
    shapes = _big_shapes()
    big = ["w_out", "w_branch_a", "w_branch_b", "ssm_w_glu"]
    slab = lambda n: pl.BlockSpec((None,) + shapes[n][1:], lambda i: (layer, 0, 0, 0))
    args = [dx_next, proj, y0, pooled, wg_glu, b_glu, pool_w, pool_scale, wg_a, wg_b, wg_out]
    return _pcall(
        body, name=f"mix_bwd_l{layer}",
        out_shape=(SDS((SEQ, N_IN), BF16), SDS((SEQ, WIDTH), F32), SDS((SEQ, WIDTH), F32))
        + tuple(SDS(shapes[n], F32) for n in big)
        + (SDS((4, POOL_GROUP, POOL_GROUP), F32), SDS((1, WIDTH), F32), SDS((1, WIDTH), F32)),
        grid=(SEQ // TILE_M,),
        in_specs=[tile(D_MODEL), tile(N_IN), tile(WIDTH), tile(WIDTH)] + _mix_weight_specs(layer) + [ANY] * n_prev,
        out_specs=(tile(N_IN), tile(WIDTH), tile(WIDTH)) + tuple(slab(n) for n in big)
        + (_const((4, POOL_GROUP, POOL_GROUP)), _const((1, WIDTH)), _const((1, WIDTH))),
        args=args + list(prev or ()),
        aliases={len(args) + i: 3 + i for i in range(n_prev)},
        sem=("arbitrary",), limit=VMEM_LIMIT_BIG, carry=carry)


def _pool_bwd(layer, dpooled, dproj):
    def body(dp_ref, _, o_ref):
        for gi, win in enumerate(POOL_WINDOWS):
            cols = slice(gi * POOL_GROUP, (gi + 1) * POOL_GROUP)
            dp = dp_ref[:, cols]
            t, count = _pool_counts(win)
            e = dp / count
            acc = e
            k = 1
            while k < win:
                acc = acc + jnp.where(t < SEQ - k, pltpu.roll(acc, SEQ - k, 0), 0.0)
                k *= 2
            o_ref[:, cols] = (acc - dp).astype(BF16)

    return pl.pallas_call(
        body, name=f"pool_bwd_l{layer}",
        out_shape=SDS((SEQ, N_IN), BF16),
        grid=(1,),
        in_specs=[pl.BlockSpec((SEQ, WIDTH), lambda i: (0, 0)), ANY],
        out_specs=pl.BlockSpec((SEQ, WIDTH), lambda i: (0, 2)),
        input_output_aliases={1: 0},
        compiler_params=_cp(("arbitrary",)),
    )(dpooled, dproj)


def _proj_wgrad(layer, x, norm_g, dproj, prev, carry=None):
    tm = 512
    n_prev = 0 if prev is None else 1

    def body(*refs):
        x_ref, g_ref, dp_ref = refs[:3]
        gw_ref, gb_ref, ht_ref = refs[3 + n_prev:]
        n, t = pl.program_id(0), pl.program_id(1)

        @pl.when(t == 0)
        def _():
            gw_ref[...] = jnp.zeros_like(gw_ref)
            gb_ref[...] = jnp.zeros_like(gb_ref)

        @pl.when(n == 0)
        def _():
            _, xn = _rms(x_ref[...])
            ht_ref[t] = (xn * g_ref[layer:layer + 1, :]).T.astype(BF16)

        dp = dp_ref[...]
        gw_ref[...] += _dot(ht_ref[t], dp)
        gb_ref[...] += jnp.sum(dp.astype(F32), axis=0, keepdims=True)

    return _pcall(
        body, name=f"proj_wgrad_l{layer}",
        out_shape=(SDS(_big_shapes()["w_in"], F32), SDS((1, N_IN), F32)),
        grid=(N_DEV, SEQ // tm),
        in_specs=[pl.BlockSpec((tm, D_MODEL), lambda n, t: (jnp.where(n == 0, t, 0), 0)),
                  _const((DEPTH, D_MODEL)),
                  pl.BlockSpec((tm, WIDTH), lambda n, t: (t, n))] + [ANY] * n_prev,
        out_specs=(pl.BlockSpec((None, None, D_MODEL, WIDTH), lambda n, t: (layer, _slot(n), 0, 0)),
                   pl.BlockSpec((1, WIDTH), lambda n, t: (0, n))),
        scratch_shapes=[pltpu.VMEM((SEQ // tm, D_MODEL, tm), BF16)],
        args=[x, norm_g, dproj] + ([prev] if n_prev else []),
        aliases={3: 0} if n_prev else {}, sem=("arbitrary", "arbitrary"), carry=carry)


def _proj_dgrad(layer, dx_next, x, norm_g, dproj, wg_in, carry=None):
    n_w = len(wg_in)

    def body(dxn_ref, x_ref, g_ref, dp_ref, *refs):
        w_refs, (dx_ref, gg_ref) = refs[:n_w], refs[n_w:]

        @pl.when(pl.program_id(0) == 0)
        def _():
            gg_ref[...] = jnp.zeros_like(gg_ref)

        parts = []
        for w_ref in w_refs:
            part = jnp.zeros((TILE_M, w_ref.shape[1]), F32)
            for k in range(N_DEV):
                part = part + _dot_nt(dp_ref[:, k * WIDTH:(k + 1) * WIDTH], w_ref[k])
            parts.append(part)
        dh = parts[0] if n_w == 1 else jnp.concatenate(parts, axis=1)
        rs, xn = _rms(x_ref[...])
        gg_ref[...] += jnp.sum(dh * xn, axis=0, keepdims=True)
        dxn = dh * g_ref[layer:layer + 1, :]
        dx_ref[...] = dxn_ref[...] + rs * (dxn - xn * jnp.mean(dxn * xn, axis=-1, keepdims=True))

    return _pcall(
        body, name=f"proj_dgrad_l{layer}",
        out_shape=(SDS((SEQ, D_MODEL), F32), SDS((1, D_MODEL), F32)),
        grid=(SEQ // TILE_M,),
        in_specs=[pl.BlockSpec((TILE_M, D_MODEL), lambda i: (i, 0)),
                  pl.BlockSpec((TILE_M, D_MODEL), lambda i: (i, 0)),
                  _const((DEPTH, D_MODEL)),
                  pl.BlockSpec((TILE_M, N_IN), lambda i: (i, 0))] + [_const(w.shape) for w in wg_in],
        out_specs=(pl.BlockSpec((TILE_M, D_MODEL), lambda i: (i, 0)), _const((1, D_MODEL))),
        args=[dx_next, x, norm_g, dproj, *wg_in], sem=("arbitrary",), carry=carry)


def _my_place():
    return lax.axis_index("x"), lax.axis_index("y"), lax.axis_index("c")


def _gather_plan(shards, layer, by_columns=(), rows_of=None):
    n = len(shards)

    def parts(ins, outs, sems):
        send_sems, recv_sems, local_sems = sems
        x, y, c = _my_place()
        chips = [(1 - x, y), (x, 1 - y), (1 - x, 1 - y)]

        def source(t):
            return ins[t].at[layer] if rows_of is None else ins[t].at[layer, pl.ds(*rows_of)]

        def rows(t, place):
            px, py, pc = place
            index = 4 * px + 2 * py + pc
            if t in by_columns:
                width = shards[t].shape[2]
                return outs[t].at[:, pl.ds(pl.multiple_of(index * width, LANES), width)]
            return outs[t].at[index]

        def copy(t, k, block, to, from_src=False):
            return pltpu.make_async_remote_copy(
                src_ref=source(t) if from_src else rows(t, block), dst_ref=rows(t, block),
                send_sem=send_sems.at[7 * t + k], recv_sem=recv_sems.at[7 * t + k], device_id=to,
                device_id_type=MESH)

        def mine(t):
            return pltpu.make_async_copy(source(t), rows(t, (x, y, c)), local_sems.at[t])

        return (x, y, c), chips, copy, mine

    def start(ins, outs, sems):
        me, chips, copy, mine = parts(ins, outs, sems)
        x, y, c = me
        for t in range(n):
            mine(t).start()
            copy(t, 0, me, (x, y, 1 - c), from_src=True).start()
            for j, chip in enumerate(chips):
                copy(t, 1 + j, me, (*chip, c), from_src=True).start()

    def relay(ins, outs, sems):
        me, chips, copy, mine = parts(ins, outs, sems)
        x, y, c = me
        for t in range(n):
            for j, chip in enumerate(chips):
                copy(t, 1 + j, (*chip, c), me).wait_recv()
                copy(t, 4 + j, (*chip, c), (x, y, 1 - c)).start()

    def finish(ins, outs, sems):
        me, chips, copy, mine = parts(ins, outs, sems)
        x, y, c = me
        sibling = (x, y, 1 - c)
        for t in range(n):
            copy(t, 0, sibling, me).wait_recv()
            for j, chip in enumerate(chips):
                copy(t, 4 + j, (*chip, 1 - c), me).wait_recv()
            for k in range(7):
                copy(t, k, me, sibling, from_src=k < 4).wait_send()
            mine(t).wait()

    n_rows = lambda a: a.shape[1] if rows_of is None else rows_of[1]
    out_shape = [SDS((a.shape[1], N_DEV * a.shape[2]) if t in by_columns else (N_DEV, n_rows(a), a.shape[2]), a.dtype)
                 for t, a in enumerate(shards)]
    sems = [pltpu.SemaphoreType.DMA((7 * n,)), pltpu.SemaphoreType.DMA((7 * n,)), pltpu.SemaphoreType.DMA((n,))]
    return _Carried(shards, out_shape, sems, start, finish, relay)


class _Carried:
    def __init__(self, ins, out_shape, sems, start, finish, relay=None):
        self.ins, self.out_shape, self.sems = list(ins), list(out_shape), list(sems)
        self.start, self.finish = start, finish
        self.relay = relay or (lambda ins, outs, sems: None)


def _pcall(body, *, name, grid, in_specs, out_specs, out_shape, args, scratch_shapes=(), aliases=None,
           sem=None, limit=VMEM_LIMIT, carry=None):
    out_shape, out_specs, scratch_shapes = list(out_shape), list(out_specs), list(scratch_shapes)
    n_in, n_out, n_scr = len(args), len(out_shape), len(scratch_shapes)
    if carry is None:
        kern, c_ins, c_out, c_sems = body, [], [], []
    else:
        c_ins, c_out, c_sems = carry.ins, carry.out_shape, carry.sems
        ci, co = len(c_ins), len(c_out)
        steps = tuple(grid)

        def kern(*refs):
            o0 = n_in + ci
            s0 = o0 + n_out + co
            mine = refs[:n_in] + refs[o0:o0 + n_out] + refs[s0:s0 + n_scr]
            theirs = (refs[n_in:o0], refs[o0 + n_out:s0], refs[s0 + n_scr:])
            first = pl.program_id(0) == 0
            last = pl.program_id(0) == steps[0] - 1
            for a in range(1, len(steps)):
                first = jnp.logical_and(first, pl.program_id(a) == 0)
                last = jnp.logical_and(last, pl.program_id(a) == steps[a] - 1)

            @pl.when(first)
            def _():
                carry.start(*theirs)

            @pl.when(last)
            def _():
                carry.relay(*theirs)

            body(*mine)

            @pl.when(last)
            def _():
                carry.finish(*theirs)

        sem = ("arbitrary",) * len(steps)
    res = pl.pallas_call(
        kern, name=name, grid=tuple(grid),
        in_specs=list(in_specs) + [ANY] * len(c_ins),
        out_specs=tuple(out_specs + [ANY] * len(c_out)),
        out_shape=tuple(out_shape + c_out),
        scratch_shapes=scratch_shapes + c_sems,
        input_output_aliases=aliases or {},
        compiler_params=_cp(sem, limit),
    )(*args, *c_ins)
    return res[:n_out], res[n_out:]


def _run_carried(name, carry):
    ci, co = len(carry.ins), len(carry.out_shape)

    def body(*refs):
        parts = (refs[:ci], refs[ci:ci + co], refs[ci + co:])
        carry.start(*parts)
        carry.relay(*parts)
        carry.finish(*parts)

    return pl.pallas_call(
        body, name=name, out_shape=tuple(carry.out_shape),
        in_specs=[ANY] * ci, out_specs=tuple([ANY] * co), scratch_shapes=carry.sems,
    )(*carry.ins)


def _sibling_plan(big, small):
    n = len(big)
    n_copies = 4 * n + len(small)

    def copies(ins, outs, sems):
        send_sems, recv_sems = sems
        x, y, c = _my_place()
        pairs = []
        for t, (_, layer) in enumerate(big):
            for s in range(4):
                pairs.append((ins[t].at[layer, pl.ds(4 * (1 - c) + s, 1)], outs[t].at[pl.ds(s, 1)]))
        pairs += list(zip(ins[n:], outs[n:]))
        return [pltpu.make_async_remote_copy(
            src_ref=src, dst_ref=dst, send_sem=send_sems.at[k], recv_sem=recv_sems.at[k],
            device_id=(x, y, 1 - c), device_id_type=MESH) for k, (src, dst) in enumerate(pairs)]

    def start(ins, outs, sems):
        for cp in copies(ins, outs, sems):
            cp.start()

    def finish(ins, outs, sems):
        for cp in copies(ins, outs, sems):
            cp.wait()

    out_shape = [SDS((4,) + a.shape[2:], a.dtype) for a, _ in big] + [SDS(a.shape, a.dtype) for a in small]
    sems = [pltpu.SemaphoreType.DMA((n_copies,)), pltpu.SemaphoreType.DMA((n_copies,))]
    return _Carried([a for a, _ in big] + list(small), out_shape, sems, start, finish)


def _chips_plan(big, small):
    n, n_small = len(big), len(small)
    max_rows = 512
    parts = [max(1, a.shape[1] // max_rows) for a in big]
    n_copies = 3 * (sum(parts) + n_small)

    def copies(ins, outs, sems, landing):
        send_sems, recv_sems, local_sems = sems
        x, y, c = _my_place()
        my_chip = 2 * x + y
        chips = [(1 - x, y), (x, 1 - y), (1 - x, 1 - y)]
        remote, local = [], []
        for chip in chips:
            to = 2 * chip[0] + chip[1]
            slot = to if landing else my_chip
            pairs = []
            for t in range(n):
                rows_per = big[t].shape[1] // parts[t]
                for p in range(parts[t]):
                    rows = pl.ds(p * rows_per, rows_per)
                    pairs.append((ins[t].at[to, rows], outs[t].at[slot, rows]))
            pairs += [(ins[t], outs[t].at[slot]) for t in range(n, n + n_small)]
            for src, dst in pairs:
                k = len(remote)
                remote.append(pltpu.make_async_remote_copy(
                    src_ref=src, dst_ref=dst, send_sem=send_sems.at[k], recv_sem=recv_sems.at[k],
                    device_id=(*chip, c), device_id_type=MESH))
        for t in range(n):
            local.append(pltpu.make_async_copy(ins[t].at[my_chip], outs[t].at[my_chip], local_sems.at[t]))
        for t in range(n, n + n_small):
            local.append(pltpu.make_async_copy(ins[t], outs[t].at[my_chip], local_sems.at[t]))
        return remote + local

    def start(ins, outs, sems):
        for cp in copies(ins, outs, sems, landing=False):
            cp.start()

    def finish(ins, outs, sems):
        for cp in copies(ins, outs, sems, landing=True):
            cp.wait()

    out_shape = [SDS(a.shape, a.dtype) for a in big] + [SDS((N_CHIP,) + a.shape, a.dtype) for a in small]
    sems = [pltpu.SemaphoreType.DMA((n_copies,)), pltpu.SemaphoreType.DMA((n_copies,)),
            pltpu.SemaphoreType.DMA((n + n_small,))]
    return _Carried(list(big) + list(small), out_shape, sems, start, finish)


def _all_plan(small):
    n = len(small)
    masks = [(m >> 2 & 1, m >> 1 & 1, m & 1) for m in range(1, N_DEV)]

    def copies(ins, outs, sems, landing):
        send_sems, recv_sems, local_sems = sems
        x, y, c = _my_place()
        me = 4 * x + 2 * y + c
        flip = lambda v, bit: 1 - v if bit else v
        remote = []
        for fx, fy, fc in masks:
            peer = (flip(x, fx), flip(y, fy), flip(c, fc))
            slot = 4 * peer[0] + 2 * peer[1] + peer[2] if landing else me
            for t in range(n):
                k = len(remote)
                remote.append(pltpu.make_async_remote_copy(
                    src_ref=ins[t], dst_ref=outs[t].at[slot], send_sem=send_sems.at[k], recv_sem=recv_sems.at[k],
                    device_id=peer, device_id_type=MESH))
        local = [pltpu.make_async_copy(ins[t], outs[t].at[me], local_sems.at[t]) for t in range(n)]
        return remote + local

    def start(ins, outs, sems):
        for cp in copies(ins, outs, sems, landing=False):
            cp.start()

    def finish(ins, outs, sems):
        for cp in copies(ins, outs, sems, landing=True):
            cp.wait()

    out_shape = [SDS((N_DEV,) + a.shape, a.dtype) for a in small]
    sems = [pltpu.SemaphoreType.DMA((7 * n,)), pltpu.SemaphoreType.DMA((7 * n,)), pltpu.SemaphoreType.DMA((n,))]
    return _Carried(list(small), out_shape, sems, start, finish)


def _join(*plans):
    plans = [p for p in plans if p is not None]
    if len(plans) <= 1:
        return plans[0] if plans else None

    def each(fn_name, ins, outs, sems):
        i = o = s = 0
        for p in plans:
            ni, no, ns = len(p.ins), len(p.out_shape), len(p.sems)
            getattr(p, fn_name)(ins[i:i + ni], outs[o:o + no], sems[s:s + ns])
            i, o, s = i + ni, o + no, s + ns

    return _Carried(sum((p.ins for p in plans), []), sum((p.out_shape for p in plans), []),
                    sum((p.sems for p in plans), []),
                    lambda i, o, s: each("start", i, o, s), lambda i, o, s: each("finish", i, o, s),
                    lambda i, o, s: each("relay", i, o, s))


def _row_block(rows, most=256):
    return min(rows, most)


def _add_own(tag, core, g, layer, got):
    _, r, c = got.shape
    rb = _row_block(r, most=1024)

    def body(core_ref, a_ref, b_ref, o_ref):
        o_ref[...] = (a_ref[...] + b_ref[...]).astype(o_ref.dtype)

    return pl.pallas_call(
        body, name=f"add_{tag}", out_shape=SDS(got.shape, BF16),
        grid_spec=pltpu.PrefetchScalarGridSpec(
            num_scalar_prefetch=1, grid=(4, r // rb),
            in_specs=[pl.BlockSpec((None, None, rb, c), lambda s, j, core: (layer, 4 * core[0] + s, j, 0)),
                      pl.BlockSpec((None, rb, c), lambda s, j, core: (s, j, 0))],
            out_specs=pl.BlockSpec((None, rb, c), lambda s, j, core: (s, j, 0))),
        compiler_params=_cp(("parallel", "parallel")),
    )(core, g, got)


def _add_lists(tag, own, got, grid=None, specs=None, dtype=F32):
    n = len(own)

    def body(*refs):
        for a, b, o in zip(refs[:n], refs[n:2 * n], refs[2 * n:]):
            o[...] = (a[...] + b[...]).astype(o.dtype)

    kw = {}
    if grid is not None:
        kw = dict(grid=grid, in_specs=list(specs) * 2, out_specs=tuple(specs),
                  compiler_params=_cp(("parallel",) * len(grid)))
    return pl.pallas_call(
        body, name=f"add_{tag}", out_shape=tuple(SDS(a.shape, dtype) for a in own), **kw)(*own, *got)


def _adamw_math(w, g, m, v):
    m = ADAM_B1 * m + (1.0 - ADAM_B1) * g
    v = ADAM_B2 * v + (1.0 - ADAM_B2) * (g * g)
    m_hat = m / (1.0 - ADAM_B1 ** ADAM_STEP)
    v_hat = v / (1.0 - ADAM_B2 ** ADAM_STEP)
    delta = -ADAM_LR * (m_hat / (jnp.sqrt(v_hat) + ADAM_EPS) + ADAM_WD * w)
    return delta, m, v


def _sum_slots_adamw(tag, slots, w, m, v):
    _, r, c = slots[0].shape
    rb = _row_block(r, most=512)

    def body(s0_ref, s1_ref, w_ref, m_ref, v_ref, g_ref, d_ref, nm_ref, nv_ref):
        first = pl.program_id(1) == 0
        g = _pair_sum([jnp.where(first, s0_ref[k], s1_ref[k]).astype(F32) for k in range(N_CHIP)])
        delta, nm, nv = _adamw_math(w_ref[...], g, m_ref[...], v_ref[...])
        g_ref[...] = g
        d_ref[...] = delta
        nm_ref[...] = nm
        nv_ref[...] = nv

    spec = pl.BlockSpec((None, rb, c), lambda j, l: (l, j, 0))
    sspec = pl.BlockSpec((N_CHIP, rb, c), lambda j, l: (0, j, 0))
    s = SDS((DEPTH, r, c), F32)
    return pl.pallas_call(
        body, name=f"adamw_{tag}", out_shape=(s, s, s, s),
        grid=(r // rb, DEPTH), in_specs=[sspec, sspec, spec, spec, spec], out_specs=(spec, spec, spec, spec),
        compiler_params=_cp(("parallel", "arbitrary")),
    )(*slots, w, m, v)


def _adamw_small(tag, entries, grid=None, sums=()):
    flat_in, in_specs, out_shape, out_specs, layout = [], [], [], [], []
    for slots, w, m, v, slot_spec, w_spec in entries:
        per_layer = isinstance(slots, (list, tuple))
        n_slot = len(slots) if per_layer else 1
        flat_in += (list(slots) if per_layer else [slots]) + [w, m, v]
        in_specs += [slot_spec] * n_slot + [w_spec] * 3
        out_shape += [SDS(w.shape, F32)] * 4
        out_specs += [w_spec] * 4
        layout.append((per_layer, n_slot))
    n_entry_in = len(flat_in)
    flat_in += list(sums)
    out_shape += [SDS(s.shape[1:], F32) for s in sums]
    n_in = len(flat_in)

    def body(*refs):
        for s_ref, o_ref in zip(refs[n_entry_in:n_in], refs[len(refs) - len(sums):]):
            o_ref[...] = _sum_slots(s_ref)
        i, o = 0, n_in
        for per_layer, n_slot in layout:
            s_refs = refs[i:i + n_slot]
            w_ref, m_ref, v_ref = refs[i + n_slot:i + n_slot + 3]
            outs = refs[o:o + 4]
            if per_layer:
                for l, s_ref in enumerate(s_refs):
                    at = (slice(l, l + 1),) if len(w_ref.shape) == 2 else (l,)
                    g = _sum_slots(s_ref)
                    res = (g,) + _adamw_math(w_ref[at], g, m_ref[at], v_ref[at])
                    for o_ref, val in zip(outs, res):
                        o_ref[at] = val
            else:
                g = _sum_slots(s_refs[0])
                res = (g,) + _adamw_math(w_ref[...], g, m_ref[...], v_ref[...])
                for o_ref, val in zip(outs, res):
                    o_ref[...] = val
            i += n_slot + 3
            o += 4

    kw = {}
    if grid is not None:
        kw = dict(grid=grid, in_specs=in_specs, out_specs=tuple(out_specs),
                  compiler_params=_cp(("parallel",) * len(grid)))
    res = pl.pallas_call(body, name=f"adamw_{tag}", out_shape=tuple(out_shape), **kw)(*flat_in)
    return [tuple(res[4 * e:4 * e + 4]) for e in range(len(entries))], res[4 * len(entries):]


def kernel(x, norm_g, w_in, b_in, ssm_log_dt, ssm_lam_re, ssm_lam_im, ssm_b_re, ssm_b_im, ssm_c_re, ssm_c_im, ssm_d, ssm_w_glu, ssm_b_glu, pool_w, pool_scale, w_branch_a, w_branch_b, w_out, final_norm_g, loss_target, m_norm_g, m_w_in, m_b_in, m_ssm_log_dt, m_ssm_lam_re, m_ssm_lam_im, m_ssm_b_re, m_ssm_b_im, m_ssm_c_re, m_ssm_c_im, m_ssm_d, m_ssm_w_glu, m_ssm_b_glu, m_pool_w, m_pool_scale, m_w_branch_a, m_w_branch_b, m_w_out, m_final_norm_g, v_norm_g, v_w_in, v_b_in, v_ssm_log_dt, v_ssm_lam_re, v_ssm_lam_im, v_ssm_b_re, v_ssm_b_im, v_ssm_c_re, v_ssm_c_im, v_ssm_d, v_ssm_w_glu, v_ssm_b_glu, v_pool_w, v_pool_scale, v_w_branch_a, v_w_branch_b, v_w_out, v_final_norm_g):
    weights = dict(norm_g=norm_g, w_in=w_in, b_in=b_in, ssm_log_dt=ssm_log_dt, ssm_lam_re=ssm_lam_re,
                   ssm_lam_im=ssm_lam_im, ssm_b_re=ssm_b_re, ssm_b_im=ssm_b_im, ssm_c_re=ssm_c_re,
                   ssm_c_im=ssm_c_im, ssm_d=ssm_d, ssm_w_glu=ssm_w_glu, ssm_b_glu=ssm_b_glu, pool_w=pool_w,
                   pool_scale=pool_scale, w_branch_a=w_branch_a, w_branch_b=w_branch_b, w_out=w_out,
                   final_norm_g=final_norm_g.reshape(1, D_MODEL))
    mom_m = dict(norm_g=m_norm_g, w_in=m_w_in, b_in=m_b_in, ssm_log_dt=m_ssm_log_dt, ssm_lam_re=m_ssm_lam_re,
                 ssm_lam_im=m_ssm_lam_im, ssm_b_re=m_ssm_b_re, ssm_b_im=m_ssm_b_im, ssm_c_re=m_ssm_c_re,
                 ssm_c_im=m_ssm_c_im, ssm_d=m_ssm_d, ssm_w_glu=m_ssm_w_glu, ssm_b_glu=m_ssm_b_glu,
                 pool_w=m_pool_w, pool_scale=m_pool_scale, w_branch_a=m_w_branch_a, w_branch_b=m_w_branch_b,
                 w_out=m_w_out, final_norm_g=m_final_norm_g.reshape(1, D_MODEL))
    mom_v = dict(norm_g=v_norm_g, w_in=v_w_in, b_in=v_b_in, ssm_log_dt=v_ssm_log_dt, ssm_lam_re=v_ssm_lam_re,
                 ssm_lam_im=v_ssm_lam_im, ssm_b_re=v_ssm_b_re, ssm_b_im=v_ssm_b_im, ssm_c_re=v_ssm_c_re,
                 ssm_c_im=v_ssm_c_im, ssm_d=v_ssm_d, ssm_w_glu=v_ssm_w_glu, ssm_b_glu=v_ssm_b_glu,
                 pool_w=v_pool_w, pool_scale=v_pool_scale, w_branch_a=v_w_branch_a, w_branch_b=v_w_branch_b,
                 w_out=v_w_out, final_norm_g=v_final_norm_g.reshape(1, D_MODEL))
    order = ["norm_g", "w_in", "b_in", "ssm_log_dt", "ssm_lam_re", "ssm_lam_im", "ssm_b_re", "ssm_b_im",
             "ssm_c_re", "ssm_c_im", "ssm_d", "ssm_w_glu", "ssm_b_glu", "pool_w", "pool_scale", "w_branch_a",
             "w_branch_b", "w_out", "final_norm_g"]
    big_names = ["w_in", "ssm_w_glu", "w_branch_a", "w_branch_b", "w_out"]

    log_dt3 = ssm_log_dt.reshape(DEPTH, N_GROUP, 1)
    b_t = lambda a: a.transpose(0, 1, 3, 2)
    for d in (weights, mom_m, mom_v):
        d["ssm_b_re"], d["ssm_b_im"] = b_t(d["ssm_b_re"]), b_t(d["ssm_b_im"])
    bt_re, bt_im = weights["ssm_b_re"], weights["ssm_b_im"]
    abar_re, abar_im, bbt_re, bbt_im = _s5_params(log_dt3, ssm_lam_re, ssm_lam_im, bt_re, bt_im)
    s5_args = (bbt_re, bbt_im, ssm_c_re, ssm_c_im, abar_re, abar_im, ssm_d)

    w16 = {n: weights[n].astype(BF16) for n in big_names}
    rest = [w16[n] for n in big_names[1:]]
    half = D_MODEL // 2
    wg_in = [None, [None, None]]
    wg_rest = [None, None]
    wg_in[0] = list(_run_carried("gather_w_in_l0", _gather_plan([w16["w_in"]], 0)))
    xs = [x.reshape(SEQ, D_MODEL)]
    saved = []
    for l in range(DEPTH):
        proj, moved = _norm_proj(l, xs[l], norm_g, wg_in[l], b_in,
                                 carry=_gather_plan([w16["w_in"]], 1, rows_of=(0, half)) if l == 0 else None)
        if l == 0:
            (wg_in[1][0],) = moved
        (states, y0), wg_rest[l] = _s5_scan_fwd(l, proj, *s5_args, carry=_gather_plan(rest, l, by_columns=(1, 2)))
        pooled = _pool_fwd(l, proj)
        wg_glu, wg_a, wg_b, wg_out = wg_rest[l]
        x_next, moved = _mix_fwd(l, xs[l], proj, y0, pooled, wg_glu, ssm_b_glu, pool_w, pool_scale, wg_a, wg_b,
                                 wg_out, carry=_gather_plan([w16["w_in"]], 1, rows_of=(half, half)) if l == 0 else None)
        if l == 0:
            (wg_in[1][1],) = moved
        xs.append(x_next)
        saved.append((proj, states, y0, pooled))

    dx, loss_part, g_final = _loss_head(xs[DEPTH], loss_target.reshape(SEQ, D_MODEL), weights["final_norm_g"])

    core = lax.axis_index("c").astype(jnp.int32).reshape(1)
    vec_names = ["norm_g", "b_in", "ssm_d", "ssm_b_glu", "pool_scale", "ssm_log_dt"]
    s5_names = ["ssm_log_dt", "ssm_lam_re", "ssm_lam_im", "ssm_b_re", "ssm_b_im"]
    mat_names = ["pool_w", "ssm_c_re", "ssm_c_im", "ssm_b_re", "ssm_b_im"]
    lane_sparse = ("ssm_c_re", "ssm_c_im", "ssm_b_re", "ssm_b_im")

    def dense(key, a):
        return a.reshape(-1, LANES) if key[0] in lane_sparse else a

    def undense(key, slots):
        return slots.reshape((N_CHIP, N_GROUP, GROUP_W, STATE)) if key[0] in lane_sparse else slots

    def add_small(tag, keys, own, got):
        out = [None] * len(keys)
        whole = [i for i, k in enumerate(keys) if k[0] not in mat_names]
        tiled = [i for i, k in enumerate(keys) if k[0] in mat_names]
        if whole:
            for i, r in zip(whole, _add_lists(f"{tag}_a", [own[i] for i in whole], [got[i] for i in whole])):
                out[i] = r
        if tiled:
            specs = [pl.BlockSpec((1, POOL_GROUP, POOL_GROUP), lambda j: (j, 0, 0)) if keys[i][0] == "pool_w"
                     else pl.BlockSpec((own[i].shape[0] // N_CHUNK, LANES), lambda j: (j, 0)) for i in tiled]
            for i, r in zip(tiled, _add_lists(f"{tag}_b", [own[i] for i in tiled], [got[i] for i in tiled],
                                              grid=(N_CHUNK,), specs=specs, dtype=BF16)):
                out[i] = r
        return out

    sm = {("final_norm_g", None): g_final, ("loss", None): loss_part}
    slots = {}
    grads = dict.fromkeys(big_names)

    class Wave:
        def __init__(self, tag, layer, big, keys):
            self.tag, self.layer, self.big, self.keys = tag, layer, big, keys

        def to_sibling(self):
            self.own = [dense(k, sm[k]) for k in self.keys]
            return _sibling_plan([(grads[n], self.layer) for n in self.big], self.own)

        def add(self, moved):
            nb = len(self.big)
            self.chip_big = [_add_own(f"{self.tag}_{n}", core, grads[n], self.layer, b)
                             for n, b in zip(self.big, moved[:nb])]
            self.chip_small = add_small(self.tag, self.keys, self.own, moved[nb:])

        def to_chips(self, big=None, small=True):
            self.sent = list(self.big if big is None else big), small
            return _chips_plan([self.chip_big[self.big.index(n)] for n in self.sent[0]],
                               self.chip_small if small else [])

        def landed(self, moved):
            names, small = self.sent
            for n, s in zip(names, moved[:len(names)]):
                slots[(n, self.layer)] = s
            if small:
                for k, s in zip(self.keys, moved[len(names):]):
                    slots[k] = undense(k, s)
            return moved[len(names) + (len(self.keys) if small else 0):]

    def s5_param_grads(l, g_abar_re, g_abar_im, g_bbt_re, g_bbt_im):
        g = _s5_params_bwd(l, log_dt3, ssm_lam_re, ssm_lam_im, bt_re, bt_im, g_abar_re, g_abar_im, g_bbt_re, g_bbt_im)
        sm[("ssm_log_dt", l)] = g[0].reshape(1, N_GROUP)
        for n, a in zip(s5_names[1:], g[1:]):
            sm[(n, l)] = a

    small1 = ["b_in", "ssm_d", "ssm_b_glu", "pool_scale", "pool_w", "ssm_c_re", "ssm_c_im"] + s5_names
    w1 = Wave("chip1", 1, list(big_names), [(n, 1) for n in small1] + [("final_norm_g", None), ("loss", None)])
    early = Wave("chip0e", 0, big_names[1:], [("pool_w", 0), ("pool_scale", 0), ("ssm_b_glu", 0)])
    mid = Wave("chip0m", 0, [], [(n, 0) for n in ["ssm_c_re", "ssm_c_im", "ssm_d"] + s5_names] + [("norm_g", 1)])
    late = Wave("chip0l", 0, ["w_in"], [("b_in", 0)])

    mix_prev, gw_in = None, None
    for l in reversed(range(DEPTH)):
        proj, states, y0, pooled = saved[l]
        wg_glu, wg_a, wg_b, wg_out = wg_rest[l]
        res, moved = _mix_bwd(l, dx, proj, y0, pooled, wg_glu, ssm_b_glu, pool_w, pool_scale, wg_a, wg_b, wg_out,
                              mix_prev, carry=None if l == 1 else w1.to_chips(big=["w_in"], small=False))
        if l == 0:
            w1.landed(moved)
        dproj, dy0, dpooled = res[:3]
        mix_prev = list(res[3:7])
        grads["w_out"], grads["w_branch_a"], grads["w_branch_b"], grads["ssm_w_glu"] = mix_prev
        sm[("pool_w", l)], sm[("pool_scale", l)], sm[("ssm_b_glu", l)] = res[7:]
        dproj = _pool_bwd(l, dpooled, dproj)
        carry = None if l == 1 else _join(w1.to_chips(big=big_names[1:]), early.to_sibling())
        res, moved = _s5_scan_bwd(l, dy0, proj, states, *s5_args, dproj, carry=carry)
        if l == 0:
            early.add(w1.landed(moved))
        dproj, g_bbt_re, g_bbt_im, sm[("ssm_c_re", l)], sm[("ssm_c_im", l)], g_abar_re, g_abar_im, sm[("ssm_d", l)] = res
        s5_param_grads(l, g_abar_re, g_abar_im, g_bbt_re, g_bbt_im)
        carry = None if l == 1 else _join(early.to_chips(), mid.to_sibling())
        (gw_in, sm[("b_in", l)]), moved = _proj_wgrad(l, xs[l], norm_g, dproj, gw_in, carry=carry)
        grads["w_in"] = gw_in
        if l == 0:
            mid.add(early.landed(moved))
        carry = w1.to_sibling() if l == 1 else _join(mid.to_chips(), late.to_sibling())
        (dx, sm[("norm_g", l)]), moved = _proj_dgrad(l, dx, xs[l], norm_g, dproj, wg_in[l], carry=carry)
        if l == 1:
            w1.add(moved)
        else:
            late.add(mid.landed(moved))
    grad_x = dx.reshape(1, SEQ, D_MODEL)
    moved = late.landed(_run_carried("exchange_last", _join(late.to_chips(), _all_plan([sm[("norm_g", 0)]]))))
    slots[("norm_g", 0)] = moved[0]

    res = {}
    for n in big_names:
        res[n] = _sum_slots_adamw(n, [slots[(n, l)] for l in range(DEPTH)], weights[n], mom_m[n], mom_v[n])
    per_layer = lambda n: [slots[(n, l)] for l in range(DEPTH)]
    names_a = vec_names + ["ssm_lam_re", "ssm_lam_im"]
    entries_a = [(per_layer(n), weights[n], mom_m[n], mom_v[n], None, None) for n in names_a]
    n = "final_norm_g"
    entries_a.append((slots[(n, None)], weights[n], mom_m[n], mom_v[n], None, None))
    out_a, (loss,) = _adamw_small("small_a", entries_a, sums=[slots[("loss", None)]])
    loss = loss.reshape(())
    for n, r in zip(names_a + ["final_norm_g"], out_a):
        res[n] = r
    res["final_norm_g"] = tuple(a.reshape(D_MODEL) for a in res["final_norm_g"])
    pw_s = pl.BlockSpec((N_CHIP, 1, POOL_GROUP, POOL_GROUP), lambda j: (0, j, 0, 0))
    pw_w = pl.BlockSpec((DEPTH, 1, POOL_GROUP, POOL_GROUP), lambda j: (0, j, 0, 0))
    c_s = pl.BlockSpec((N_CHIP, CH_G, GROUP_W, STATE), lambda j: (0, j, 0, 0))
    c_w = pl.BlockSpec((DEPTH, CH_G, GROUP_W, STATE), lambda j: (0, j, 0, 0))
    entries_b = [(per_layer(n), weights[n], mom_m[n], mom_v[n], pw_s if n == "pool_w" else c_s,
                  pw_w if n == "pool_w" else c_w) for n in mat_names]
    out_b, _ = _adamw_small("small_b", entries_b, grid=(N_CHUNK,))
    for n, r in zip(mat_names, out_b):
        res[n] = tuple(b_t(a) for a in r) if n in ("ssm_b_re", "ssm_b_im") else r

    outs = [loss, grad_x]
    for i in range(4):
        outs += [res[n][i] for n in order]
    return tuple(outs)
```

```python
import math

import jax
import jax.numpy as jnp
from jax import lax
from jax.experimental import pallas as pl
from jax.experimental.pallas import tpu as pltpu

F32 = jnp.float32
BF16 = jnp.bfloat16

SEQ = 2048
D_MODEL = 1024
N_IN = 4096
WIDTH = 512
N_GROUP = 32
GROUP_W = 16
STATE = 64
N_STATE = N_GROUP * STATE
N_CHUNK = 4
CH_G = N_GROUP // N_CHUNK
CH_W = WIDTH // N_CHUNK
CH_S = N_STATE // N_CHUNK
N_DEV = 8
N_CHIP = 4
POOL_WINDOWS = (2, 4, 8, 16)
POOL_GROUP = 128
EPS = 1e-6
DEPTH = 2

ADAM_LR = 0.001
ADAM_B1 = 0.9
ADAM_B2 = 0.999
ADAM_EPS = 1e-08
ADAM_WD = 0.01
ADAM_STEP = 10

LANES = 128
SUBLANES = 8
TILE_M = 256
VMEM_LIMIT = 48 * 1024 * 1024
VMEM_LIMIT_BIG = 60 * 1024 * 1024
MESH = pl.DeviceIdType.MESH
ANY = pl.BlockSpec(memory_space=pl.ANY)

GELU_C = math.sqrt(2.0 / math.pi)
GELU_A = 0.044715

SDS = jax.ShapeDtypeStruct


def _cp(sem=None, limit=VMEM_LIMIT):
    return pltpu.CompilerParams(dimension_semantics=sem, vmem_limit_bytes=limit)


def _dot(a, b):
    return jnp.dot(a, b, preferred_element_type=F32)


def _dot_nt(a, b):
    return lax.dot_general(a, b, (((1,), (1,)), ((), ())), preferred_element_type=F32)


def _dot_tn(a, b):
    return lax.dot_general(a, b, (((0,), (0,)), ((), ())), preferred_element_type=F32)


def _sig(x):
    return jax.nn.sigmoid(x)


def _rms(x):
    rs = lax.rsqrt(jnp.mean(x * x, axis=-1, keepdims=True) + EPS)
    return rs, x * rs


def _slot(n):
    return 4 * (n % 2) + n // 2


def _const(shape):
    n = len(shape)
    return pl.BlockSpec(shape, lambda *_: (0,) * n)


def _pair_sum(vals):
    while len(vals) > 1:
        vals = [vals[i] + vals[i + 1] for i in range(0, len(vals), 2)]
    return vals[0]


def _sum_slots(s_ref):
    return _pair_sum([s_ref[k].astype(F32) for k in range(s_ref.shape[0])])


def _s5_param_fn(log_dt, lam_re, lam_im, bt_re, bt_im):
    dt = jnp.exp(log_dt)
    mag = jnp.exp(lam_re * dt)
    ang = lam_im * dt
    abar_re = mag * jnp.cos(ang)
    abar_im = mag * jnp.sin(ang)
    num_re = abar_re - 1.0
    num_im = abar_im
    den = lam_re * lam_re + lam_im * lam_im
    coef_re = (num_re * lam_re + num_im * lam_im) / den
    coef_im = (num_im * lam_re - num_re * lam_im) / den
    bbar_re = coef_re[..., None, :] * bt_re - coef_im[..., None, :] * bt_im
    bbar_im = coef_re[..., None, :] * bt_im + coef_im[..., None, :] * bt_re
    return abar_re, abar_im, bbar_re, bbar_im


def _s5_params(log_dt, lam_re, lam_im, bt_re, bt_im):
    def body(ld, lr, li, br, bi, o_ar, o_ai, o_br, o_bi):
        ar, ai, bbr, bbi = _s5_param_fn(ld[...], lr[...], li[...], br[...], bi[...])
        o_ar[...] = ar
        o_ai[...] = ai
        o_br[...] = bbr
        o_bi[...] = bbi

    return pl.pallas_call(
        body, name="s5_params",
        out_shape=(SDS(lam_re.shape, F32), SDS(lam_re.shape, F32), SDS(bt_re.shape, F32), SDS(bt_re.shape, F32)),
    )(log_dt, lam_re, lam_im, bt_re, bt_im)


def _s5_params_bwd(layer, log_dt, lam_re, lam_im, bt_re, bt_im, g_ar, g_ai, g_br, g_bi):
    def body(ld, lr, li, br, bi, car, cai, cbr, cbi, o_ld, o_lr, o_li, o_br, o_bi):
        _, vjp = jax.vjp(_s5_param_fn, ld[...], lr[...], li[...], br[...], bi[...])
        d_ld, d_lr, d_li, d_br, d_bi = vjp((car[...], cai[...], cbr[...], cbi[...]))
        o_ld[...] = d_ld
        o_lr[...] = d_lr
        o_li[...] = d_li
        o_br[...] = d_br
        o_bi[...] = d_bi

    one = lambda shape: pl.BlockSpec((None,) + shape, lambda i: (layer,) + (0,) * len(shape))
    whole = lambda shape: _const(shape)
    vec, lam, mat = (N_GROUP, 1), (N_GROUP, STATE), (N_GROUP, GROUP_W, STATE)
    return pl.pallas_call(
        body, name=f"s5_params_bwd_l{layer}", grid=(1,),
        in_specs=[one(vec), one(lam), one(lam), one(mat), one(mat), whole(lam), whole(lam), whole(mat), whole(mat)],
        out_specs=(whole(vec), whole(lam), whole(lam), whole(mat), whole(mat)),
        out_shape=(SDS(vec, F32), SDS(lam, F32), SDS(lam, F32), SDS(mat, F32), SDS(mat, F32)),
    )(log_dt, lam_re, lam_im, bt_re, bt_im, g_ar, g_ai, g_br, g_bi)


def _norm_proj(layer, x, norm_g, wg_in, b_in, carry=None):
    n_w = len(wg_in)

    def body(x_ref, g_ref, b_ref, *refs):
        w_refs, o_ref = refs[:n_w], refs[n_w]
        _, xn = _rms(x_ref[...])
        h = (xn * g_ref[layer:layer + 1, :]).astype(BF16)
        for k in range(N_DEV):
            cols = slice(k * WIDTH, (k + 1) * WIDTH)
            acc = b_ref[layer:layer + 1, cols]
            row = 0
            for w_ref in w_refs:
                rows = w_ref.shape[1]
                acc = acc + _dot(h[:, row:row + rows], w_ref[k])
                row += rows
            o_ref[:, cols] = acc

    (proj,), moved = _pcall(
        body, name=f"norm_proj_l{layer}",
        out_shape=[SDS((SEQ, N_IN), F32)],
        grid=(SEQ // TILE_M,),
        in_specs=[pl.BlockSpec((TILE_M, D_MODEL), lambda i: (i, 0)),
                  _const((DEPTH, D_MODEL)),
                  _const((DEPTH, N_IN))] + [_const(w.shape) for w in wg_in],
        out_specs=[pl.BlockSpec((TILE_M, N_IN), lambda i: (i, 0))],
        args=[x, norm_g, b_in, *wg_in], sem=("parallel",), carry=carry)
    return proj, moved


TIME_BLK = 512
N_TBLK = SEQ // TIME_BLK
N_PANEL = CH_S // LANES
STATE_SHAPE = (N_PANEL, SEQ * SUBLANES, LANES)


def _s5_layer_specs(layer):
    mat = lambda: pl.BlockSpec((None, N_GROUP, GROUP_W, STATE), lambda i: (layer, 0, 0, 0))
    ab = lambda: pl.BlockSpec((None, N_GROUP, STATE), lambda i: (layer, 0, 0))
    return [mat(), mat(), mat(), mat(), ab(), ab(), _const((DEPTH, WIDTH))]


def _s5_layer_scratch():
    return [pltpu.VMEM((N_CHUNK, CH_W, CH_S), BF16)] * 4 + [pltpu.VMEM((8, CH_S), F32)] * 2


def _s5_layer_fill(btre_ref, btim_ref, cre_ref, cim_ref, are_ref, aim_ref, bdre, bdim, ctre, ctim, a1, a2):
    for m in (bdre, bdim, ctre, ctim):
        m[...] = jnp.zeros_like(m)
    for grp in range(N_GROUP):
        k, g = divmod(grp, CH_G)
        rows = slice(g * GROUP_W, (g + 1) * GROUP_W)
        cols = slice(g * STATE, (g + 1) * STATE)
        bdre[k, rows, cols] = btre_ref[grp].astype(BF16)
        bdim[k, rows, cols] = btim_ref[grp].astype(BF16)
        ctre[k, rows, cols] = cre_ref[grp].astype(BF16)
        ctim[k, rows, cols] = cim_ref[grp].astype(BF16)
        ar = are_ref[grp:grp + 1, :]
        ai = aim_ref[grp:grp + 1, :]
        a1[k:k + 1, cols] = ar
        a1[N_CHUNK + k:N_CHUNK + k + 1, cols] = ar
        a2[k:k + 1, cols] = -ai
        a2[N_CHUNK + k:N_CHUNK + k + 1, cols] = ai


SCAN_UNROLL = 16


def _panels(tile):
    return [tile[:, p * LANES:(p + 1) * LANES] for p in range(N_PANEL)]


def _rows_load(ref, row):
    return jnp.concatenate([ref[p, pl.ds(row, TIME_BLK, stride=SUBLANES), :] for p in range(N_PANEL)], axis=1)


def _rows_store(ref, row, val):
    for p in range(N_PANEL):
        ref[p, pl.ds(row, TIME_BLK, stride=SUBLANES), :] = val[:, p * LANES:(p + 1) * LANES]


def _s5_scan_fwd(layer, proj, bbt_re, bbt_im, c_re, c_im, abar_re, abar_im, d_skip, carry=None):
    def body(u_ref, btre_ref, btim_ref, cre_ref, cim_ref, are_ref, aim_ref, d_ref, s_ref, y_ref,
             bdre, bdim, ctre, ctim, a1, a2, state):
        @pl.when(pl.program_id(0) == 0)
        def _():
            _s5_layer_fill(btre_ref, btim_ref, cre_ref, cim_ref, are_ref, aim_ref, bdre, bdim, ctre, ctim, a1, a2)
            state[...] = jnp.zeros_like(state)

        for k in range(N_CHUNK):
            ub = u_ref[:, k * CH_W:(k + 1) * CH_W].astype(BF16)
            _rows_store(s_ref, k, _dot(ub, bdre[k]))
            _rows_store(s_ref, N_CHUNK + k, _dot(ub, bdim[k]))
        m1 = _panels(a1[...])
        m2 = _panels(a2[...])

        def steps(n, tile):
            for r in range(SCAN_UNROLL):
                rows = pl.ds(pl.multiple_of((n * SCAN_UNROLL + r) * 8, 8), 8)
                tile = [m1[p] * tile[p] + m2[p] * pltpu.roll(tile[p], N_CHUNK, 0) + s_ref[p, rows, :]
                        for p in range(N_PANEL)]
                for p in range(N_PANEL):
                    s_ref[p, rows, :] = tile[p]
            return tile

        tile = lax.fori_loop(0, TIME_BLK // SCAN_UNROLL, steps, _panels(state[...]))
        state[...] = jnp.concatenate(tile, axis=1)
        d = d_ref[layer:layer + 1, :]
        for k in range(N_CHUNK):
            cols = slice(k * CH_W, (k + 1) * CH_W)
            y = (_dot_nt(_rows_load(s_ref, k).astype(BF16), ctre[k])
                 - _dot_nt(_rows_load(s_ref, N_CHUNK + k).astype(BF16), ctim[k]))
            y_ref[:, cols] = y + d[:, cols] * u_ref[:, cols]

    return _pcall(
        body, name=f"s5_fwd_l{layer}",
        out_shape=(SDS(STATE_SHAPE, F32), SDS((SEQ, WIDTH), F32)),
        grid=(N_TBLK,),
        in_specs=[pl.BlockSpec((TIME_BLK, WIDTH), lambda i: (i, 0))] + _s5_layer_specs(layer),
        out_specs=(pl.BlockSpec((N_PANEL, TIME_BLK * SUBLANES, LANES), lambda i: (0, i, 0)),
                   pl.BlockSpec((TIME_BLK, WIDTH), lambda i: (i, 0))),
        scratch_shapes=_s5_layer_scratch() + [pltpu.VMEM((8, CH_S), F32)],
        args=[proj, bbt_re, bbt_im, c_re, c_im, abar_re, abar_im, d_skip], sem=("arbitrary",), carry=carry)


def _s5_scan_bwd(layer, dy0, proj, states, bbt_re, bbt_im, c_re, c_im, abar_re, abar_im, d_skip, dproj,
                 carry=None):
    def body(dy_ref, u_ref, s_ref, sprev_ref, btre_ref, btim_ref, cre_ref, cim_ref, are_ref, aim_ref, d_ref, _,
             du_ref, gbre_ref, gbim_ref, gcre_ref, gcim_ref, gare_ref, gaim_ref, gd_ref,
             lam_ref, bdre, bdim, ctre, ctim, a1, a2, state, acc1, acc2, gbre, gbim, gcre, gcim, gd):
        step_id = pl.program_id(0)

        @pl.when(step_id == 0)
        def _():
            _s5_layer_fill(btre_ref, btim_ref, cre_ref, cim_ref, are_ref, aim_ref, bdre, bdim, ctre, ctim, a1, a2)
            for r in (state, acc1, acc2, gbre, gbim, gcre, gcim, gd):
                r[...] = jnp.zeros_like(r)

        for k in range(N_CHUNK):
            dyb = dy_ref[:, k * CH_W:(k + 1) * CH_W].astype(BF16)
            _rows_store(lam_ref, k, _dot(dyb, ctre[k]))
            _rows_store(lam_ref, N_CHUNK + k, -_dot(dyb, ctim[k]))
            gcre[k] += _dot_tn(dyb, _rows_load(s_ref, k).astype(BF16))
            gcim[k] -= _dot_tn(dyb, _rows_load(s_ref, N_CHUNK + k).astype(BF16))

        m1 = _panels(a1[...])
        m2 = _panels(-a2[...])
        has_before = (step_id < N_TBLK - 1).astype(F32)

        def one(t8, c, first_token):
            tile, swapped, p1, p2 = c
            rows = pl.ds(t8, 8)
            tile = [m1[p] * tile[p] + m2[p] * swapped[p] + lam_ref[p, rows, :] for p in range(N_PANEL)]
            swapped = [pltpu.roll(tile[p], N_CHUNK, 0) for p in range(N_PANEL)]
            for p in range(N_PANEL):
                lam_ref[p, rows, :] = tile[p]
            if first_token:
                before = [sprev_ref[p] * has_before for p in range(N_PANEL)]
            else:
                before = [s_ref[p, pl.ds(t8 - 8, 8), :] for p in range(N_PANEL)]
            p1 = [p1[p] + tile[p] * before[p] for p in range(N_PANEL)]
            p2 = [p2[p] + swapped[p] * before[p] for p in range(N_PANEL)]
            return tile, swapped, p1, p2

        def steps(n, c):
            for r in range(SCAN_UNROLL):
                t8 = pl.multiple_of((TIME_BLK - 1 - (n * SCAN_UNROLL + r)) * 8, 8)
                c = one(t8, c, False)
            return c

        tile0 = _panels(state[...])
        c = (tile0, [pltpu.roll(t, N_CHUNK, 0) for t in tile0], _panels(acc1[...]), _panels(acc2[...]))
        c = lax.fori_loop(0, TIME_BLK // SCAN_UNROLL - 1, steps, c)
        for r in range(SCAN_UNROLL - 1, -1, -1):
            c = one(r * 8, c, r == 0)
        state[...] = jnp.concatenate(c[0], axis=1)
        acc1[...] = jnp.concatenate(c[2], axis=1)
        acc2[...] = jnp.concatenate(c[3], axis=1)

        d = d_ref[layer:layer + 1, :]
        for k in range(N_CHUNK):
            cols = slice(k * CH_W, (k + 1) * CH_W)
            lrb = _rows_load(lam_ref, k).astype(BF16)
            lib = _rows_load(lam_ref, N_CHUNK + k).astype(BF16)
            u = u_ref[:, cols]
            ub = u.astype(BF16)
            dy = dy_ref[:, cols]
            du = dy * d[:, cols] + _dot_nt(lrb, bdre[k]) + _dot_nt(lib, bdim[k])
            du_ref[:, cols] = du.astype(BF16)
            gbre[k] += _dot_tn(ub, lrb)
            gbim[k] += _dot_tn(ub, lib)
        gd[...] += jnp.sum(dy_ref[...] * u_ref[...], axis=0, keepdims=True)

        @pl.when(step_id == N_TBLK - 1)
        def _():
            gd_ref[...] = gd[...]
            ga_re = acc1[0:N_CHUNK, :] + acc1[N_CHUNK:, :]
            ga_im = acc2[0:N_CHUNK, :] - acc2[N_CHUNK:, :]
            for grp in range(N_GROUP):
                k, g = divmod(grp, CH_G)
                rows = slice(g * GROUP_W, (g + 1) * GROUP_W)
                cols = slice(g * STATE, (g + 1) * STATE)
                gcre_ref[grp] = gcre[k, rows, cols]
                gcim_ref[grp] = gcim[k, rows, cols]
                gbre_ref[grp] = gbre[k, rows, cols]
                gbim_ref[grp] = gbim[k, rows, cols]
                gare_ref[grp:grp + 1, :] = ga_re[k:k + 1, cols]
                gaim_ref[grp:grp + 1, :] = ga_im[k:k + 1, cols]

    back = lambda i: N_TBLK - 1 - i
    tok = lambda: pl.BlockSpec((TIME_BLK, WIDTH), lambda i: (back(i), 0))
    mat = lambda: _const((N_GROUP, GROUP_W, STATE))
    acc_mat = pltpu.VMEM((N_CHUNK, CH_W, CH_S), F32)
    return _pcall(
        body, name=f"s5_bwd_l{layer}",
        out_shape=(SDS((SEQ, N_IN), BF16), SDS((N_GROUP, GROUP_W, STATE), F32), SDS((N_GROUP, GROUP_W, STATE), F32),
                   SDS((N_GROUP, GROUP_W, STATE), F32), SDS((N_GROUP, GROUP_W, STATE), F32),
                   SDS((N_GROUP, STATE), F32), SDS((N_GROUP, STATE), F32), SDS((1, WIDTH), F32)),
        grid=(N_TBLK,),
        in_specs=[tok(), tok(),
                  pl.BlockSpec((N_PANEL, TIME_BLK * SUBLANES, LANES), lambda i: (0, back(i), 0)),
                  pl.BlockSpec((N_PANEL, SUBLANES, LANES), lambda i: (0, jnp.maximum(back(i) * TIME_BLK - 1, 0), 0))]
        + _s5_layer_specs(layer) + [ANY],
        out_specs=(tok(), mat(), mat(), mat(), mat(), _const((N_GROUP, STATE)), _const((N_GROUP, STATE)),
                   _const((1, WIDTH))),
        scratch_shapes=[pltpu.VMEM((N_PANEL, TIME_BLK * SUBLANES, LANES), F32)] + _s5_layer_scratch()
        + [pltpu.VMEM((8, CH_S), F32)] * 3 + [acc_mat] * 4 + [pltpu.VMEM((1, WIDTH), F32)],
        args=[dy0, proj, states, states, bbt_re, bbt_im, c_re, c_im, abar_re, abar_im, d_skip, dproj],
        aliases={11: 0}, sem=("arbitrary",), limit=VMEM_LIMIT_BIG, carry=carry)


def _pool_counts(win):
    t = lax.broadcasted_iota(jnp.int32, (SEQ, POOL_GROUP), 0)
    return t, jnp.minimum(t + 1, win).astype(F32)


def _pool_fwd(layer, proj):
    def body(u_ref, o_ref):
        for gi, win in enumerate(POOL_WINDOWS):
            cols = slice(gi * POOL_GROUP, (gi + 1) * POOL_GROUP)
            u = u_ref[:, cols]
            t, count = _pool_counts(win)
            acc = u
            k = 1
            while k < win:
                acc = acc + jnp.where(t >= k, pltpu.roll(acc, k, 0), 0.0)
                k *= 2
            o_ref[:, cols] = acc / count - u

    return pl.pallas_call(
        body, name=f"pool_fwd_l{layer}",
        out_shape=SDS((SEQ, WIDTH), F32),
        grid=(1,),
        in_specs=[pl.BlockSpec((SEQ, WIDTH), lambda i: (0, 2))],
        out_specs=pl.BlockSpec((SEQ, WIDTH), lambda i: (0, 0)),
        compiler_params=_cp(("arbitrary",)),
    )(proj)


def _gelu_parts(y0):
    t = jnp.tanh(GELU_C * (y0 + GELU_A * (y0 * y0 * y0)))
    return t, 0.5 * y0 * (1.0 + t)


def _mix_forward(layer, p_ref, y0_ref, pooled_ref, wglu_ref, bglu_ref, pw_ref, scale_ref, wa_ref, wb_ref):
    za = p_ref[:, WIDTH:2 * WIDTH]
    zb = p_ref[:, 3 * WIDTH:4 * WIDTH]
    ga = p_ref[:, 4 * WIDTH:4 * WIDTH + D_MODEL]
    gb = p_ref[:, 4 * WIDTH + D_MODEL:]
    y0 = y0_ref[...]
    t, y1 = _gelu_parts(y0)
    y1b = y1.astype(BF16)
    q = _dot(y1b, wglu_ref[...].reshape(WIDTH, WIDTH)) + bglu_ref[layer:layer + 1, :]
    sq = _sig(q)
    y2 = y1 * sq
    sza = _sig(za)
    silu_za = za * sza
    ya = y2 * silu_za
    pooled = pooled_ref[...]
    mixed = jnp.concatenate(
        [_dot(pooled[:, g * POOL_GROUP:(g + 1) * POOL_GROUP].astype(BF16), pw_ref[g].astype(BF16))
         for g in range(len(POOL_WINDOWS))], axis=1)
    szb = _sig(zb)
    silu_zb = zb * szb
    scale = scale_ref[layer:layer + 1, :]
    ms = mixed * scale
    yb = ms * silu_zb
    yab = ya.astype(BF16)
    ybb = yb.astype(BF16)
    ma = _dot(yab, wa_ref[...])
    mb = _dot(ybb, wb_ref[...])
    sga = _sig(ga)
    sgb = _sig(gb)
    merged = sga * ma + sgb * mb
    return dict(za=za, zb=zb, y0=y0, t=t, y1=y1, y1b=y1b, sq=sq, y2=y2, sza=sza, silu_za=silu_za,
                pooled=pooled, mixed=mixed, szb=szb, silu_zb=silu_zb, scale=scale, ms=ms, yab=yab, ybb=ybb,
                ma=ma, mb=mb, sga=sga, sgb=sgb, merged=merged)


def _mix_weight_specs(layer):
    return [_const((N_DEV, WIDTH // N_DEV, WIDTH)),
            _const((DEPTH, WIDTH)),
            pl.BlockSpec((None, 4, POOL_GROUP, POOL_GROUP), lambda i: (layer, 0, 0, 0)),
            _const((DEPTH, WIDTH)),
            _const((WIDTH, D_MODEL)),
            _const((WIDTH, D_MODEL)),
            _const((N_DEV, D_MODEL // N_DEV, D_MODEL))]


def _mix_fwd(layer, x, proj, y0, pooled, wg_glu, b_glu, pool_w, pool_scale, wg_a, wg_b, wg_out, carry=None):
    def body(x_ref, p_ref, y0_ref, pooled_ref, wglu_ref, bglu_ref, pw_ref, scale_ref, wa_ref, wb_ref,
             wout_ref, o_ref):
        f = _mix_forward(layer, p_ref, y0_ref, pooled_ref, wglu_ref, bglu_ref, pw_ref, scale_ref, wa_ref, wb_ref)
        wout = wout_ref[...].reshape(D_MODEL, D_MODEL)
        o_ref[...] = x_ref[...] + _dot(f["merged"].astype(BF16), wout)

    (x_next,), moved = _pcall(
        body, name=f"mix_fwd_l{layer}",
        out_shape=[SDS((SEQ, D_MODEL), F32)],
        grid=(SEQ // TILE_M,),
        in_specs=[pl.BlockSpec((TILE_M, D_MODEL), lambda i: (i, 0)),
                  pl.BlockSpec((TILE_M, N_IN), lambda i: (i, 0)),
                  pl.BlockSpec((TILE_M, WIDTH), lambda i: (i, 0)),
                  pl.BlockSpec((TILE_M, WIDTH), lambda i: (i, 0))] + _mix_weight_specs(layer),
        out_specs=[pl.BlockSpec((TILE_M, D_MODEL), lambda i: (i, 0))],
        args=[x, proj, y0, pooled, wg_glu, b_glu, pool_w, pool_scale, wg_a, wg_b, wg_out],
        sem=("parallel",), carry=carry)
    return x_next, moved


def _loss_head(x, target, final_g):
    def body(x_ref, t_ref, g_ref, dx_ref, loss_ref, gg_ref):
        @pl.when(pl.program_id(0) == 0)
        def _():
            loss_ref[...] = jnp.zeros_like(loss_ref)
            gg_ref[...] = jnp.zeros_like(gg_ref)

        g = g_ref[...]
        rs, xn = _rms(x_ref[...])
        err = xn * g - t_ref[...]
        loss_ref[...] += 0.5 * jnp.sum(jnp.mean(err * err, axis=-1, keepdims=True), axis=0, keepdims=True)
        dy = err * (1.0 / D_MODEL)
        gg_ref[...] += jnp.sum(dy * xn, axis=0, keepdims=True)
        dxn = dy * g
        dx_ref[...] = rs * (dxn - xn * jnp.mean(dxn * xn, axis=-1, keepdims=True))

    return pl.pallas_call(
        body, name="loss_head",
        out_shape=(SDS((SEQ, D_MODEL), F32), SDS((1, 1), F32), SDS((1, D_MODEL), F32)),
        grid=(SEQ // TILE_M,),
        in_specs=[pl.BlockSpec((TILE_M, D_MODEL), lambda i: (i, 0)),
                  pl.BlockSpec((TILE_M, D_MODEL), lambda i: (i, 0)),
                  _const((1, D_MODEL))],
        out_specs=(pl.BlockSpec((TILE_M, D_MODEL), lambda i: (i, 0)), _const((1, 1)), _const((1, D_MODEL))),
        compiler_params=_cp(("arbitrary",)),
    )(x, target, final_g)


def _big_shapes():
    return dict(w_out=(DEPTH, N_DEV, D_MODEL // N_DEV, D_MODEL), w_branch_a=(DEPTH, N_DEV, WIDTH, D_MODEL // N_DEV),
                w_branch_b=(DEPTH, N_DEV, WIDTH, D_MODEL // N_DEV), ssm_w_glu=(DEPTH, N_DEV, WIDTH // N_DEV, WIDTH),
                w_in=(DEPTH, N_DEV, D_MODEL, WIDTH))


def _mix_bwd(layer, dx_next, proj, y0, pooled, wg_glu, b_glu, pool_w, pool_scale, wg_a, wg_b, wg_out, prev,
             carry=None):
    n_k = N_DEV
    n_prev = 0 if prev is None else len(prev)

    def body(*refs):
        (dx_ref, p_ref, y0_ref, pooled_ref, wglu_ref, bglu_ref, pw_ref, scale_ref, wa_ref, wb_ref,
         wout_ref) = refs[:11]
        (dproj_ref, dy0_ref, dpooled_ref, gwout_ref, gwa_ref, gwb_ref, gwglu_ref, gpw_ref,
         gscale_ref, gbglu_ref) = refs[11 + n_prev:]

        @pl.when(pl.program_id(0) == 0)
        def _():
            for r in (gwout_ref, gwa_ref, gwb_ref, gwglu_ref, gpw_ref, gscale_ref, gbglu_ref):
                r[...] = jnp.zeros_like(r)

        f = _mix_forward(layer, p_ref, y0_ref, pooled_ref, wglu_ref, bglu_ref, pw_ref, scale_ref, wa_ref, wb_ref)
        wglu = wglu_ref[...].reshape(WIDTH, WIDTH)
        wout = wout_ref[...].reshape(D_MODEL, D_MODEL)
        blk = D_MODEL // n_k
        dxb = dx_ref[...].astype(BF16)
        dmerged = _dot_nt(dxb, wout)
        gwout = _dot_tn(f["merged"].astype(BF16), dxb)
        for k in range(n_k):
            gwout_ref[_slot(k)] += gwout[k * blk:(k + 1) * blk, :]
        dma = dmerged * f["sga"]
        dmb = dmerged * f["sgb"]
        dga = dmerged * f["ma"] * f["sga"] * (1.0 - f["sga"])
        dgb = dmerged * f["mb"] * f["sgb"] * (1.0 - f["sgb"])
        dmab = dma.astype(BF16)
        dmbb = dmb.astype(BF16)
        dya = _dot_nt(dmab, wa_ref[...])
        dyb = _dot_nt(dmbb, wb_ref[...])
        gwa = _dot_tn(f["yab"], dmab)
        gwb = _dot_tn(f["ybb"], dmbb)
        for k in range(n_k):
            gwa_ref[_slot(k)] += gwa[:, k * blk:(k + 1) * blk]
            gwb_ref[_slot(k)] += gwb[:, k * blk:(k + 1) * blk]
        zb, szb = f["zb"], f["szb"]
        dzb = dyb * f["ms"] * (szb * (1.0 + zb * (1.0 - szb)))
        dms = dyb * f["silu_zb"]
        gscale_ref[...] += jnp.sum(dms * f["mixed"], axis=0, keepdims=True)
        dmixed = (dms * f["scale"]).astype(BF16)
        pooled = f["pooled"]
        for g in range(len(POOL_WINDOWS)):
            cols = slice(g * POOL_GROUP, (g + 1) * POOL_GROUP)
            dpooled_ref[:, cols] = _dot_nt(dmixed[:, cols], pw_ref[g].astype(BF16))
            gpw_ref[g] += _dot_tn(pooled[:, cols].astype(BF16), dmixed[:, cols])
        za, sza = f["za"], f["sza"]
        dza = dya * f["y2"] * (sza * (1.0 + za * (1.0 - sza)))
        dy2 = dya * f["silu_za"]
        sq = f["sq"]
        dq = dy2 * f["y1"] * sq * (1.0 - sq)
        dqb = dq.astype(BF16)
        dy1 = dy2 * sq + _dot_nt(dqb, wglu)
        gwglu = _dot_tn(f["y1b"], dqb)
        rblk = WIDTH // n_k
        for k in range(n_k):
            gwglu_ref[_slot(k)] += gwglu[k * rblk:(k + 1) * rblk, :]
        gbglu_ref[...] += jnp.sum(dq, axis=0, keepdims=True)
        y0, t = f["y0"], f["t"]
        dgelu = 0.5 * (1.0 + t) + 0.5 * y0 * (1.0 - t * t) * (GELU_C * (1.0 + 3.0 * GELU_A * y0 * y0))
        dy0_ref[...] = dy1 * dgelu
        zeros = jnp.zeros((TILE_M, WIDTH), BF16)
        dproj_ref[:, 0:WIDTH] = zeros
        dproj_ref[:, WIDTH:2 * WIDTH] = dza.astype(BF16)
        dproj_ref[:, 2 * WIDTH:3 * WIDTH] = zeros
        dproj_ref[:, 3 * WIDTH:4 * WIDTH] = dzb.astype(BF16)
        dproj_ref[:, 4 * WIDTH:4 * WIDTH + D_MODEL] = dga.astype(BF16)
        dproj_ref[:, 4 * WIDTH + D_MODEL:] = dgb.astype(BF16)

    tile = lambda w: pl.BlockSpec((TILE_M, w), lambda i: (i, 0))
    shapes = _big_shapes()
    big = ["w_out", "w_branch_a", "w_branch_b", "ssm_w_glu"]
    slab = lambda n: pl.BlockSpec((None,) + shapes[n][1:], lambda i: (layer, 0, 0, 0))
    args = [dx_next, proj, y0, pooled, wg_glu, b_glu, pool_w, pool_scale, wg_a, wg_b, wg_out]
    return _pcall(
        body, name=f"mix_bwd_l{layer}",
        out_shape=(SDS((SEQ, N_IN), BF16), SDS((SEQ, WIDTH), F32), SDS((SEQ, WIDTH), F32))
        + tuple(SDS(shapes[n], F32) for n in big)
        + (SDS((4, POOL_GROUP, POOL_GROUP), F32), SDS((1, WIDTH), F32), SDS((1, WIDTH), F32)),
        grid=(SEQ // TILE_M,),
        in_specs=[tile(D_MODEL), tile(N_IN), tile(WIDTH), tile(WIDTH)] + _mix_weight_specs(layer) + [ANY] * n_prev,
        out_specs=(tile(N_IN), tile(WIDTH), tile(WIDTH)) + tuple(slab(n) for n in big)
        + (_const((4, POOL_GROUP, POOL_GROUP)), _const((1, WIDTH)), _const((1, WIDTH))),
        args=args + list(prev or ()),
        aliases={len(args) + i: 3 + i for i in range(n_prev)},
        sem=("arbitrary",), limit=VMEM_LIMIT_BIG, carry=carry)


def _pool_bwd(layer, dpooled, dproj):
    def body(dp_ref, _, o_ref):
        for gi, win in enumerate(POOL_WINDOWS):
            cols = slice(gi * POOL_GROUP, (gi + 1) * POOL_GROUP)
            dp = dp_ref[:, cols]
            t, count = _pool_counts(win)
            e = dp / count
            acc = e
            k = 1
            while k < win:
                acc = acc + jnp.where(t < SEQ - k, pltpu.roll(acc, SEQ - k, 0), 0.0)
                k *= 2
            o_ref[:, cols] = (acc - dp).astype(BF16)

    return pl.pallas_call(
        body, name=f"pool_bwd_l{layer}",
        out_shape=SDS((SEQ, N_IN), BF16),
        grid=(1,),
        in_specs=[pl.BlockSpec((SEQ, WIDTH), lambda i: (0, 0)), ANY],
        out_specs=pl.BlockSpec((SEQ, WIDTH), lambda i: (0, 2)),
        input_output_aliases={1: 0},
        compiler_params=_cp(("arbitrary",)),
    )(dpooled, dproj)


def _proj_wgrad(layer, x, norm_g, dproj, prev, carry=None):
    tm = 512
    n_prev = 0 if prev is None else 1

    def body(*refs):
        x_ref, g_ref, dp_ref = refs[:3]
        gw_ref, gb_ref, ht_ref = refs[3 + n_prev:]
        n, t = pl.program_id(0), pl.program_id(1)

        @pl.when(t == 0)
        def _():
            gw_ref[...] = jnp.zeros_like(gw_ref)
            gb_ref[...] = jnp.zeros_like(gb_ref)

        @pl.when(n == 0)
        def _():
            _, xn = _rms(x_ref[...])
            ht_ref[t] = (xn * g_ref[layer:layer + 1, :]).T.astype(BF16)

        dp = dp_ref[...]
        gw_ref[...] += _dot(ht_ref[t], dp)
        gb_ref[...] += jnp.sum(dp.astype(F32), axis=0, keepdims=True)

    return _pcall(
        body, name=f"proj_wgrad_l{layer}",
        out_shape=(SDS(_big_shapes()["w_in"], F32), SDS((1, N_IN), F32)),
        grid=(N_DEV, SEQ // tm),
        in_specs=[pl.BlockSpec((tm, D_MODEL), lambda n, t: (jnp.where(n == 0, t, 0), 0)),
                  _const((DEPTH, D_MODEL)),
                  pl.BlockSpec((tm, WIDTH), lambda n, t: (t, n))] + [ANY] * n_prev,
        out_specs=(pl.BlockSpec((None, None, D_MODEL, WIDTH), lambda n, t: (layer, _slot(n), 0, 0)),
                   pl.BlockSpec((1, WIDTH), lambda n, t: (0, n))),
        scratch_shapes=[pltpu.VMEM((SEQ // tm, D_MODEL, tm), BF16)],
        args=[x, norm_g, dproj] + ([prev] if n_prev else []),
        aliases={3: 0} if n_prev else {}, sem=("arbitrary", "arbitrary"), carry=carry)


def _proj_dgrad(layer, dx_next, x, norm_g, dproj, wg_in, carry=None):
    n_w = len(wg_in)

    def body(dxn_ref, x_ref, g_ref, dp_ref, *refs):
        w_refs, (dx_ref, gg_ref) = refs[:n_w], refs[n_w:]

        @pl.when(pl.program_id(0) == 0)
        def _():
            gg_ref[...] = jnp.zeros_like(gg_ref)

        parts = []
        for w_ref in w_refs:
            part = jnp.zeros((TILE_M, w_ref.shape[1]), F32)
            for k in range(N_DEV):
                part = part + _dot_nt(dp_ref[:, k * WIDTH:(k + 1) * WIDTH], w_ref[k])
            parts.append(part)
        dh = parts[0] if n_w == 1 else jnp.concatenate(parts, axis=1)
        rs, xn = _rms(x_ref[...])
        gg_ref[...] += jnp.sum(dh * xn, axis=0, keepdims=True)
        dxn = dh * g_ref[layer:layer + 1, :]
        dx_ref[...] = dxn_ref[...] + rs * (dxn - xn * jnp.mean(dxn * xn, axis=-1, keepdims=True))

    return _pcall(
        body, name=f"proj_dgrad_l{layer}",
        out_shape=(SDS((SEQ, D_MODEL), F32), SDS((1, D_MODEL), F32)),
        grid=(SEQ // TILE_M,),
        in_specs=[pl.BlockSpec((TILE_M, D_MODEL), lambda i: (i, 0)),
                  pl.BlockSpec((TILE_M, D_MODEL), lambda i: (i, 0)),
                  _const((DEPTH, D_MODEL)),
                  pl.BlockSpec((TILE_M, N_IN), lambda i: (i, 0))] + [_const(w.shape) for w in wg_in],
        out_specs=(pl.BlockSpec((TILE_M, D_MODEL), lambda i: (i, 0)), _const((1, D_MODEL))),
        args=[dx_next, x, norm_g, dproj, *wg_in], sem=("arbitrary",), carry=carry)


def _my_place():
    return lax.axis_index("x"), lax.axis_index("y"), lax.axis_index("c")


def _gather_plan(shards, layer, by_columns=(), rows_of=None):
    n = len(shards)

    def parts(ins, outs, sems):
        send_sems, recv_sems, local_sems = sems
        x, y, c = _my_place()
        chips = [(1 - x, y), (x, 1 - y), (1 - x, 1 - y)]

        def source(t):
            return ins[t].at[layer] if rows_of is None else ins[t].at[layer, pl.ds(*rows_of)]

        def rows(t, place):
            px, py, pc = place
            index = 4 * px + 2 * py + pc
            if t in by_columns:
                width = shards[t].shape[2]
                return outs[t].at[:, pl.ds(pl.multiple_of(index * width, LANES), width)]
            return outs[t].at[index]

        def copy(t, k, block, to, from_src=False):
            return pltpu.make_async_remote_copy(
                src_ref=source(t) if from_src else rows(t, block), dst_ref=rows(t, block),
                send_sem=send_sems.at[7 * t + k], recv_sem=recv_sems.at[7 * t + k], device_id=to,
                device_id_type=MESH)

        def mine(t):
            return pltpu.make_async_copy(source(t), rows(t, (x, y, c)), local_sems.at[t])

        return (x, y, c), chips, copy, mine

    def start(ins, outs, sems):
        me, chips, copy, mine = parts(ins, outs, sems)
        x, y, c = me
        for t in range(n):
            mine(t).start()
            copy(t, 0, me, (x, y, 1 - c), from_src=True).start()
            for j, chip in enumerate(chips):
                copy(t, 1 + j, me, (*chip, c), from_src=True).start()

    def relay(ins, outs, sems):
        me, chips, copy, mine = parts(ins, outs, sems)
        x, y, c = me
        for t in range(n):
            for j, chip in enumerate(chips):
                copy(t, 1 + j, (*chip, c), me).wait_recv()
                copy(t, 4 + j, (*chip, c), (x, y, 1 - c)).start()

    def finish(ins, outs, sems):
        me, chips, copy, mine = parts(ins, outs, sems)
        x, y, c = me
        sibling = (x, y, 1 - c)
        for t in range(n):
            copy(t, 0, sibling, me).wait_recv()
            for j, chip in enumerate(chips):
                copy(t, 4 + j, (*chip, 1 - c), me).wait_recv()
            for k in range(7):
                copy(t, k, me, sibling, from_src=k < 4).wait_send()
            mine(t).wait()

    n_rows = lambda a: a.shape[1] if rows_of is None else rows_of[1]
    out_shape = [SDS((a.shape[1], N_DEV * a.shape[2]) if t in by_columns else (N_DEV, n_rows(a), a.shape[2]), a.dtype)
                 for t, a in enumerate(shards)]
    sems = [pltpu.SemaphoreType.DMA((7 * n,)), pltpu.SemaphoreType.DMA((7 * n,)), pltpu.SemaphoreType.DMA((n,))]
    return _Carried(shards, out_shape, sems, start, finish, relay)


class _Carried:
    def __init__(self, ins, out_shape, sems, start, finish, relay=None):
        self.ins, self.out_shape, self.sems = list(ins), list(out_shape), list(sems)
        self.start, self.finish = start, finish
        self.relay = relay or (lambda ins, outs, sems: None)


def _pcall(body, *, name, grid, in_specs, out_specs, out_shape, args, scratch_shapes=(), aliases=None,
           sem=None, limit=VMEM_LIMIT, carry=None):
    out_shape, out_specs, scratch_shapes = list(out_shape), list(out_specs), list(scratch_shapes)
    n_in, n_out, n_scr = len(args), len(out_shape), len(scratch_shapes)
    if carry is None:
        kern, c_ins, c_out, c_sems = body, [], [], []
    else:
        c_ins, c_out, c_sems = carry.ins, carry.out_shape, carry.sems
        ci, co = len(c_ins), len(c_out)
        steps = tuple(grid)

        def kern(*refs):
            o0 = n_in + ci
            s0 = o0 + n_out + co
            mine = refs[:n_in] + refs[o0:o0 + n_out] + refs[s0:s0 + n_scr]
            theirs = (refs[n_in:o0], refs[o0 + n_out:s0], refs[s0 + n_scr:])
            first = pl.program_id(0) == 0
            last = pl.program_id(0) == steps[0] - 1
            for a in range(1, len(steps)):
                first = jnp.logical_and(first, pl.program_id(a) == 0)
                last = jnp.logical_and(last, pl.program_id(a) == steps[a] - 1)

            @pl.when(first)
            def _():
                carry.start(*theirs)

            @pl.when(last)
            def _():
                carry.relay(*theirs)

            body(*mine)

            @pl.when(last)
            def _():
                carry.finish(*theirs)

        sem = ("arbitrary",) * len(steps)
    res = pl.pallas_call(
        kern, name=name, grid=tuple(grid),
        in_specs=list(in_specs) + [ANY] * len(c_ins),
        out_specs=tuple(out_specs + [ANY] * len(c_out)),
        out_shape=tuple(out_shape + c_out),
        scratch_shapes=scratch_shapes + c_sems,
        input_output_aliases=aliases or {},
        compiler_params=_cp(sem, limit),
    )(*args, *c_ins)
    return res[:n_out], res[n_out:]


def _run_carried(name, carry):
    ci, co = len(carry.ins), len(carry.out_shape)

    def body(*refs):
        parts = (refs[:ci], refs[ci:ci + co], refs[ci + co:])
        carry.start(*parts)
        carry.relay(*parts)
        carry.finish(*parts)

    return pl.pallas_call(
        body, name=name, out_shape=tuple(carry.out_shape),
        in_specs=[ANY] * ci, out_specs=tuple([ANY] * co), scratch_shapes=carry.sems,
    )(*carry.ins)


def _sibling_plan(big, small):
    n = len(big)
    n_copies = 4 * n + len(small)

    def copies(ins, outs, sems):
        send_sems, recv_sems = sems
        x, y, c = _my_place()
        pairs = []
        for t, (_, layer) in enumerate(big):
            for s in range(4):
                pairs.append((ins[t].at[layer, pl.ds(4 * (1 - c) + s, 1)], outs[t].at[pl.ds(s, 1)]))
        pairs += list(zip(ins[n:], outs[n:]))
        return [pltpu.make_async_remote_copy(
            src_ref=src, dst_ref=dst, send_sem=send_sems.at[k], recv_sem=recv_sems.at[k],
            device_id=(x, y, 1 - c), device_id_type=MESH) for k, (src, dst) in enumerate(pairs)]

    def start(ins, outs, sems):
        for cp in copies(ins, outs, sems):
            cp.start()

    def finish(ins, outs, sems):
        for cp in copies(ins, outs, sems):
            cp.wait()

    out_shape = [SDS((4,) + a.shape[2:], a.dtype) for a, _ in big] + [SDS(a.shape, a.dtype) for a in small]
    sems = [pltpu.SemaphoreType.DMA((n_copies,)), pltpu.SemaphoreType.DMA((n_copies,))]
    return _Carried([a for a, _ in big] + list(small), out_shape, sems, start, finish)


def _chips_plan(big, small):
    n, n_small = len(big), len(small)
    max_rows = 512
    parts = [max(1, a.shape[1] // max_rows) for a in big]
    n_copies = 3 * (sum(parts) + n_small)

    def copies(ins, outs, sems, landing):
        send_sems, recv_sems, local_sems = sems
        x, y, c = _my_place()
        my_chip = 2 * x + y
        chips = [(1 - x, y), (x, 1 - y), (1 - x, 1 - y)]
        remote, local = [], []
        for chip in chips:
            to = 2 * chip[0] + chip[1]
            slot = to if landing else my_chip
            pairs = []
            for t in range(n):
                rows_per = big[t].shape[1] // parts[t]
                for p in range(parts[t]):
                    rows = pl.ds(p * rows_per, rows_per)
                    pairs.append((ins[t].at[to, rows], outs[t].at[slot, rows]))
            pairs += [(ins[t], outs[t].at[slot]) for t in range(n, n + n_small)]
            for src, dst in pairs:
                k = len(remote)
                remote.append(pltpu.make_async_remote_copy(
                    src_ref=src, dst_ref=dst, send_sem=send_sems.at[k], recv_sem=recv_sems.at[k],
                    device_id=(*chip, c), device_id_type=MESH))
        for t in range(n):
            local.append(pltpu.make_async_copy(ins[t].at[my_chip], outs[t].at[my_chip], local_sems.at[t]))
        for t in range(n, n + n_small):
            local.append(pltpu.make_async_copy(ins[t], outs[t].at[my_chip], local_sems.at[t]))
        return remote + local

    def start(ins, outs, sems):
        for cp in copies(ins, outs, sems, landing=False):
            cp.start()

    def finish(ins, outs, sems):
        for cp in copies(ins, outs, sems, landing=True):
            cp.wait()

    out_shape = [SDS(a.shape, a.dtype) for a in big] + [SDS((N_CHIP,) + a.shape, a.dtype) for a in small]
    sems = [pltpu.SemaphoreType.DMA((n_copies,)), pltpu.SemaphoreType.DMA((n_copies,)),
            pltpu.SemaphoreType.DMA((n + n_small,))]
    return _Carried(list(big) + list(small), out_shape, sems, start, finish)


def _all_plan(small):
    n = len(small)
    masks = [(m >> 2 & 1, m >> 1 & 1, m & 1) for m in range(1, N_DEV)]

    def copies(ins, outs, sems, landing):
        send_sems, recv_sems, local_sems = sems
        x, y, c = _my_place()
        me = 4 * x + 2 * y + c
        flip = lambda v, bit: 1 - v if bit else v
        remote = []
        for fx, fy, fc in masks:
            peer = (flip(x, fx), flip(y, fy), flip(c, fc))
            slot = 4 * peer[0] + 2 * peer[1] + peer[2] if landing else me
            for t in range(n):
                k = len(remote)
                remote.append(pltpu.make_async_remote_copy(
                    src_ref=ins[t], dst_ref=outs[t].at[slot], send_sem=send_sems.at[k], recv_sem=recv_sems.at[k],
                    device_id=peer, device_id_type=MESH))
        local = [pltpu.make_async_copy(ins[t], outs[t].at[me], local_sems.at[t]) for t in range(n)]
        return remote + local

    def start(ins, outs, sems):
        for cp in copies(ins, outs, sems, landing=False):
            cp.start()

    def finish(ins, outs, sems):
        for cp in copies(ins, outs, sems, landing=True):
            cp.wait()

    out_shape = [SDS((N_DEV,) + a.shape, a.dtype) for a in small]
    sems = [pltpu.SemaphoreType.DMA((7 * n,)), pltpu.SemaphoreType.DMA((7 * n,)), pltpu.SemaphoreType.DMA((n,))]
    return _Carried(list(small), out_shape, sems, start, finish)


def _join(*plans):
    plans = [p for p in plans if p is not None]
    if len(plans) <= 1:
        return plans[0] if plans else None

    def each(fn_name, ins, outs, sems):
        i = o = s = 0
        for p in plans:
            ni, no, ns = len(p.ins), len(p.out_shape), len(p.sems)
            getattr(p, fn_name)(ins[i:i + ni], outs[o:o + no], sems[s:s + ns])
            i, o, s = i + ni, o + no, s + ns

    return _Carried(sum((p.ins for p in plans), []), sum((p.out_shape for p in plans), []),
                    sum((p.sems for p in plans), []),
                    lambda i, o, s: each("start", i, o, s), lambda i, o, s: each("finish", i, o, s),
                    lambda i, o, s: each("relay", i, o, s))


def _row_block(rows, most=256):
    return min(rows, most)


def _add_own(tag, core, gs, layer, gots):
    n = len(gs)

    def body(core_ref, *refs):
        for a_ref, b_ref, o_ref in zip(refs[:n], refs[n:2 * n], refs[2 * n:]):
            o_ref[...] = (a_ref[...] + b_ref[...]).astype(o_ref.dtype)

    mine = lambda a: pl.BlockSpec((None, None) + a.shape[1:], lambda s, core: (layer, 4 * core[0] + s, 0, 0))
    theirs = lambda a: pl.BlockSpec((None,) + a.shape[1:], lambda s, core: (s, 0, 0))
    return pl.pallas_call(
        body, name=f"add_{tag}", out_shape=tuple(SDS(a.shape, BF16) for a in gots),
        grid_spec=pltpu.PrefetchScalarGridSpec(
            num_scalar_prefetch=1, grid=(4,),
            in_specs=[mine(a) for a in gots] + [theirs(a) for a in gots],
            out_specs=tuple(theirs(a) for a in gots)),
        compiler_params=_cp(("parallel",)),
    )(core, *gs, *gots)


def _add_lists(tag, own, got, grid=None, specs=None, dtype=F32):
    n = len(own)

    def body(*refs):
        for a, b, o in zip(refs[:n], refs[n:2 * n], refs[2 * n:]):
            o[...] = (a[...] + b[...]).astype(o.dtype)

    kw = {}
    if grid is not None:
        kw = dict(grid=grid, in_specs=list(specs) * 2, out_specs=tuple(specs),
                  compiler_params=_cp(("parallel",) * len(grid)))
    return pl.pallas_call(
        body, name=f"add_{tag}", out_shape=tuple(SDS(a.shape, dtype) for a in own), **kw)(*own, *got)


def _adamw_math(w, g, m, v):
    m = ADAM_B1 * m + (1.0 - ADAM_B1) * g
    v = ADAM_B2 * v + (1.0 - ADAM_B2) * (g * g)
    m_hat = m / (1.0 - ADAM_B1 ** ADAM_STEP)
    v_hat = v / (1.0 - ADAM_B2 ** ADAM_STEP)
    delta = -ADAM_LR * (m_hat / (jnp.sqrt(v_hat) + ADAM_EPS) + ADAM_WD * w)
    return delta, m, v


def _sum_slots_adamw(tag, slots, w, m, v):
    _, r, c = slots[0].shape
    rb = _row_block(r, most=512)

    def body(s0_ref, s1_ref, w_ref, m_ref, v_ref, g_ref, d_ref, nm_ref, nv_ref):
        first = pl.program_id(1) == 0
        g = _pair_sum([jnp.where(first, s0_ref[k], s1_ref[k]).astype(F32) for k in range(N_CHIP)])
        delta, nm, nv = _adamw_math(w_ref[...], g, m_ref[...], v_ref[...])
        g_ref[...] = g
        d_ref[...] = delta
        nm_ref[...] = nm
        nv_ref[...] = nv

    spec = pl.BlockSpec((None, rb, c), lambda j, l: (l, j, 0))
    sspec = pl.BlockSpec((N_CHIP, rb, c), lambda j, l: (0, j, 0))
    s = SDS((DEPTH, r, c), F32)
    return pl.pallas_call(
        body, name=f"adamw_{tag}", out_shape=(s, s, s, s),
        grid=(r // rb, DEPTH), in_specs=[sspec, sspec, spec, spec, spec], out_specs=(spec, spec, spec, spec),
        compiler_params=_cp(("parallel", "arbitrary")),
    )(*slots, w, m, v)


def _adamw_small(tag, entries, grid=None, sums=()):
    flat_in, in_specs, out_shape, out_specs, layout = [], [], [], [], []
    for slots, w, m, v, slot_spec, w_spec in entries:
        per_layer = isinstance(slots, (list, tuple))
        n_slot = len(slots) if per_layer else 1
        flat_in += (list(slots) if per_layer else [slots]) + [w, m, v]
        in_specs += [slot_spec] * n_slot + [w_spec] * 3
        out_shape += [SDS(w.shape, F32)] * 4
        out_specs += [w_spec] * 4
        layout.append((per_layer, n_slot))
    n_entry_in = len(flat_in)
    flat_in += list(sums)
    out_shape += [SDS(s.shape[1:], F32) for s in sums]
    n_in = len(flat_in)

    def body(*refs):
        for s_ref, o_ref in zip(refs[n_entry_in:n_in], refs[len(refs) - len(sums):]):
            o_ref[...] = _sum_slots(s_ref)
        i, o = 0, n_in
        for per_layer, n_slot in layout:
            s_refs = refs[i:i + n_slot]
            w_ref, m_ref, v_ref = refs[i + n_slot:i + n_slot + 3]
            outs = refs[o:o + 4]
            if per_layer:
                for l, s_ref in enumerate(s_refs):
                    at = (slice(l, l + 1),) if len(w_ref.shape) == 2 else (l,)
                    g = _sum_slots(s_ref)
                    res = (g,) + _adamw_math(w_ref[at], g, m_ref[at], v_ref[at])
                    for o_ref, val in zip(outs, res):
                        o_ref[at] = val
            else:
                g = _sum_slots(s_refs[0])
                res = (g,) + _adamw_math(w_ref[...], g, m_ref[...], v_ref[...])
                for o_ref, val in zip(outs, res):
                    o_ref[...] = val
            i += n_slot + 3
            o += 4

    kw = {}
    if grid is not None:
        kw = dict(grid=grid, in_specs=in_specs, out_specs=tuple(out_specs),
                  compiler_params=_cp(("parallel",) * len(grid)))
    res = pl.pallas_call(body, name=f"adamw_{tag}", out_shape=tuple(out_shape), **kw)(*flat_in)
    return [tuple(res[4 * e:4 * e + 4]) for e in range(len(entries))], res[4 * len(entries):]


def kernel(x, norm_g, w_in, b_in, ssm_log_dt, ssm_lam_re, ssm_lam_im, ssm_b_re, ssm_b_im, ssm_c_re, ssm_c_im, ssm_d, ssm_w_glu, ssm_b_glu, pool_w, pool_scale, w_branch_a, w_branch_b, w_out, final_norm_g, loss_target, m_norm_g, m_w_in, m_b_in, m_ssm_log_dt, m_ssm_lam_re, m_ssm_lam_im, m_ssm_b_re, m_ssm_b_im, m_ssm_c_re, m_ssm_c_im, m_ssm_d, m_ssm_w_glu, m_ssm_b_glu, m_pool_w, m_pool_scale, m_w_branch_a, m_w_branch_b, m_w_out, m_final_norm_g, v_norm_g, v_w_in, v_b_in, v_ssm_log_dt, v_ssm_lam_re, v_ssm_lam_im, v_ssm_b_re, v_ssm_b_im, v_ssm_c_re, v_ssm_c_im, v_ssm_d, v_ssm_w_glu, v_ssm_b_glu, v_pool_w, v_pool_scale, v_w_branch_a, v_w_branch_b, v_w_out, v_final_norm_g):
    weights = dict(norm_g=norm_g, w_in=w_in, b_in=b_in, ssm_log_dt=ssm_log_dt, ssm_lam_re=ssm_lam_re,
                   ssm_lam_im=ssm_lam_im, ssm_b_re=ssm_b_re, ssm_b_im=ssm_b_im, ssm_c_re=ssm_c_re,
                   ssm_c_im=ssm_c_im, ssm_d=ssm_d, ssm_w_glu=ssm_w_glu, ssm_b_glu=ssm_b_glu, pool_w=pool_w,
                   pool_scale=pool_scale, w_branch_a=w_branch_a, w_branch_b=w_branch_b, w_out=w_out,
                   final_norm_g=final_norm_g.reshape(1, D_MODEL))
    mom_m = dict(norm_g=m_norm_g, w_in=m_w_in, b_in=m_b_in, ssm_log_dt=m_ssm_log_dt, ssm_lam_re=m_ssm_lam_re,
                 ssm_lam_im=m_ssm_lam_im, ssm_b_re=m_ssm_b_re, ssm_b_im=m_ssm_b_im, ssm_c_re=m_ssm_c_re,
                 ssm_c_im=m_ssm_c_im, ssm_d=m_ssm_d, ssm_w_glu=m_ssm_w_glu, ssm_b_glu=m_ssm_b_glu,
                 pool_w=m_pool_w, pool_scale=m_pool_scale, w_branch_a=m_w_branch_a, w_branch_b=m_w_branch_b,
                 w_out=m_w_out, final_norm_g=m_final_norm_g.reshape(1, D_MODEL))
    mom_v = dict(norm_g=v_norm_g, w_in=v_w_in, b_in=v_b_in, ssm_log_dt=v_ssm_log_dt, ssm_lam_re=v_ssm_lam_re,
                 ssm_lam_im=v_ssm_lam_im, ssm_b_re=v_ssm_b_re, ssm_b_im=v_ssm_b_im, ssm_c_re=v_ssm_c_re,
                 ssm_c_im=v_ssm_c_im, ssm_d=v_ssm_d, ssm_w_glu=v_ssm_w_glu, ssm_b_glu=v_ssm_b_glu,
                 pool_w=v_pool_w, pool_scale=v_pool_scale, w_branch_a=v_w_branch_a, w_branch_b=v_w_branch_b,
                 w_out=v_w_out, final_norm_g=v_final_norm_g.reshape(1, D_MODEL))
    order = ["norm_g", "w_in", "b_in", "ssm_log_dt", "ssm_lam_re", "ssm_lam_im", "ssm_b_re", "ssm_b_im",
             "ssm_c_re", "ssm_c_im", "ssm_d", "ssm_w_glu", "ssm_b_glu", "pool_w", "pool_scale", "w_branch_a",
             "w_branch_b", "w_out", "final_norm_g"]
    big_names = ["w_in", "ssm_w_glu", "w_branch_a", "w_branch_b", "w_out"]

    log_dt3 = ssm_log_dt.reshape(DEPTH, N_GROUP, 1)
    b_t = lambda a: a.transpose(0, 1, 3, 2)
    for d in (weights, mom_m, mom_v):
        d["ssm_b_re"], d["ssm_b_im"] = b_t(d["ssm_b_re"]), b_t(d["ssm_b_im"])
    bt_re, bt_im = weights["ssm_b_re"], weights["ssm_b_im"]
    abar_re, abar_im, bbt_re, bbt_im = _s5_params(log_dt3, ssm_lam_re, ssm_lam_im, bt_re, bt_im)
    s5_args = (bbt_re, bbt_im, ssm_c_re, ssm_c_im, abar_re, abar_im, ssm_d)

    w16 = {n: weights[n].astype(BF16) for n in big_names}
    rest = [w16[n] for n in big_names[1:]]
    half = D_MODEL // 2
    wg_in = [None, [None, None]]
    wg_rest = [None, None]
    wg_in[0] = list(_run_carried("gather_w_in_l0", _gather_plan([w16["w_in"]], 0)))
    xs = [x.reshape(SEQ, D_MODEL)]
    saved = []
    for l in range(DEPTH):
        proj, moved = _norm_proj(l, xs[l], norm_g, wg_in[l], b_in,
                                 carry=_gather_plan([w16["w_in"]], 1, rows_of=(0, half)) if l == 0 else None)
        if l == 0:
            (wg_in[1][0],) = moved
        (states, y0), wg_rest[l] = _s5_scan_fwd(l, proj, *s5_args, carry=_gather_plan(rest, l, by_columns=(1, 2)))
        pooled = _pool_fwd(l, proj)
        wg_glu, wg_a, wg_b, wg_out = wg_rest[l]
        x_next, moved = _mix_fwd(l, xs[l], proj, y0, pooled, wg_glu, ssm_b_glu, pool_w, pool_scale, wg_a, wg_b,
                                 wg_out, carry=_gather_plan([w16["w_in"]], 1, rows_of=(half, half)) if l == 0 else None)
        if l == 0:
            (wg_in[1][1],) = moved
        xs.append(x_next)
        saved.append((proj, states, y0, pooled))

    dx, loss_part, g_final = _loss_head(xs[DEPTH], loss_target.reshape(SEQ, D_MODEL), weights["final_norm_g"])

    core = lax.axis_index("c").astype(jnp.int32).reshape(1)
    vec_names = ["norm_g", "b_in", "ssm_d", "ssm_b_glu", "pool_scale", "ssm_log_dt"]
    s5_names = ["ssm_log_dt", "ssm_lam_re", "ssm_lam_im", "ssm_b_re", "ssm_b_im"]
    mat_names = ["pool_w", "ssm_c_re", "ssm_c_im", "ssm_b_re", "ssm_b_im"]
    lane_sparse = ("ssm_c_re", "ssm_c_im", "ssm_b_re", "ssm_b_im")

    def dense(key, a):
        return a.reshape(-1, LANES) if key[0] in lane_sparse else a

    def undense(key, slots):
        return slots.reshape((N_CHIP, N_GROUP, GROUP_W, STATE)) if key[0] in lane_sparse else slots

    def add_small(tag, keys, own, got):
        out = [None] * len(keys)
        whole = [i for i, k in enumerate(keys) if k[0] not in mat_names]
        tiled = [i for i, k in enumerate(keys) if k[0] in mat_names]
        if whole:
            for i, r in zip(whole, _add_lists(f"{tag}_a", [own[i] for i in whole], [got[i] for i in whole])):
                out[i] = r
        if tiled:
            specs = [pl.BlockSpec((1, POOL_GROUP, POOL_GROUP), lambda j: (j, 0, 0)) if keys[i][0] == "pool_w"
                     else pl.BlockSpec((own[i].shape[0] // N_CHUNK, LANES), lambda j: (j, 0)) for i in tiled]
            for i, r in zip(tiled, _add_lists(f"{tag}_b", [own[i] for i in tiled], [got[i] for i in tiled],
                                              grid=(N_CHUNK,), specs=specs, dtype=BF16)):
                out[i] = r
        return out

    sm = {("final_norm_g", None): g_final, ("loss", None): loss_part}
    slots = {}
    grads = dict.fromkeys(big_names)

    class Wave:
        def __init__(self, tag, layer, big, keys):
            self.tag, self.layer, self.big, self.keys = tag, layer, big, keys

        def to_sibling(self):
            self.own = [dense(k, sm[k]) for k in self.keys]
            return _sibling_plan([(grads[n], self.layer) for n in self.big], self.own)

        def add(self, moved):
            nb = len(self.big)
            self.chip_big = list(_add_own(self.tag, core, [grads[n] for n in self.big], self.layer, moved[:nb])
                                 ) if nb else []
            self.chip_small = add_small(self.tag, self.keys, self.own, moved[nb:])

        def to_chips(self, big=None, small=True):
            self.sent = list(self.big if big is None else big), small
            return _chips_plan([self.chip_big[self.big.index(n)] for n in self.sent[0]],
                               self.chip_small if small else [])

        def landed(self, moved):
            names, small = self.sent
            for n, s in zip(names, moved[:len(names)]):
                slots[(n, self.layer)] = s
            if small:
                for k, s in zip(self.keys, moved[len(names):]):
                    slots[k] = undense(k, s)
            return moved[len(names) + (len(self.keys) if small else 0):]

    def s5_param_grads(l, g_abar_re, g_abar_im, g_bbt_re, g_bbt_im):
        g = _s5_params_bwd(l, log_dt3, ssm_lam_re, ssm_lam_im, bt_re, bt_im, g_abar_re, g_abar_im, g_bbt_re, g_bbt_im)
        sm[("ssm_log_dt", l)] = g[0].reshape(1, N_GROUP)
        for n, a in zip(s5_names[1:], g[1:]):
            sm[(n, l)] = a

    small1 = ["b_in", "ssm_d", "ssm_b_glu", "pool_scale", "pool_w", "ssm_c_re", "ssm_c_im"] + s5_names
    w1 = Wave("chip1", 1, list(big_names), [(n, 1) for n in small1] + [("final_norm_g", None), ("loss", None)])
    early = Wave("chip0e", 0, big_names[1:], [("pool_w", 0), ("pool_scale", 0), ("ssm_b_glu", 0)])
    mid = Wave("chip0m", 0, [], [(n, 0) for n in ["ssm_c_re", "ssm_c_im", "ssm_d"] + s5_names] + [("norm_g", 1)])
    late = Wave("chip0l", 0, ["w_in"], [("b_in", 0)])

    mix_prev, gw_in = None, None
    for l in reversed(range(DEPTH)):
        proj, states, y0, pooled = saved[l]
        wg_glu, wg_a, wg_b, wg_out = wg_rest[l]
        res, moved = _mix_bwd(l, dx, proj, y0, pooled, wg_glu, ssm_b_glu, pool_w, pool_scale, wg_a, wg_b, wg_out,
                              mix_prev, carry=None if l == 1 else w1.to_chips(big=["w_in"], small=False))
        if l == 0:
            w1.landed(moved)
        dproj, dy0, dpooled = res[:3]
        mix_prev = list(res[3:7])
        grads["w_out"], grads["w_branch_a"], grads["w_branch_b"], grads["ssm_w_glu"] = mix_prev
        sm[("pool_w", l)], sm[("pool_scale", l)], sm[("ssm_b_glu", l)] = res[7:]
        dproj = _pool_bwd(l, dpooled, dproj)
        carry = None if l == 1 else _join(w1.to_chips(big=big_names[1:]), early.to_sibling())
        res, moved = _s5_scan_bwd(l, dy0, proj, states, *s5_args, dproj, carry=carry)
        if l == 0:
            early.add(w1.landed(moved))
        dproj, g_bbt_re, g_bbt_im, sm[("ssm_c_re", l)], sm[("ssm_c_im", l)], g_abar_re, g_abar_im, sm[("ssm_d", l)] = res
        s5_param_grads(l, g_abar_re, g_abar_im, g_bbt_re, g_bbt_im)
        carry = None if l == 1 else _join(early.to_chips(), mid.to_sibling())
        (gw_in, sm[("b_in", l)]), moved = _proj_wgrad(l, xs[l], norm_g, dproj, gw_in, carry=carry)
        grads["w_in"] = gw_in
        if l == 0:
            mid.add(early.landed(moved))
        carry = w1.to_sibling() if l == 1 else _join(mid.to_chips(), late.to_sibling())
        (dx, sm[("norm_g", l)]), moved = _proj_dgrad(l, dx, xs[l], norm_g, dproj, wg_in[l], carry=carry)
        if l == 1:
            w1.add(moved)
        else:
            late.add(mid.landed(moved))
    grad_x = dx.reshape(1, SEQ, D_MODEL)
    moved = late.landed(_run_carried("exchange_last", _join(late.to_chips(), _all_plan([sm[("norm_g", 0)]]))))
    slots[("norm_g", 0)] = moved[0]

    res = {}
    for n in big_names:
        res[n] = _sum_slots_adamw(n, [slots[(n, l)] for l in range(DEPTH)], weights[n], mom_m[n], mom_v[n])
    per_layer = lambda n: [slots[(n, l)] for l in range(DEPTH)]
    names_a = vec_names + ["ssm_lam_re", "ssm_lam_im"]
    entries_a = [(per_layer(n), weights[n], mom_m[n], mom_v[n], None, None) for n in names_a]
    n = "final_norm_g"
    entries_a.append((slots[(n, None)], weights[n], mom_m[n], mom_v[n], None, None))
    out_a, (loss,) = _adamw_small("small_a", entries_a, sums=[slots[("loss", None)]])
    loss = loss.reshape(())
    for n, r in zip(names_a + ["final_norm_g"], out_a):
        res[n] = r
    res["final_norm_g"] = tuple(a.reshape(D_MODEL) for a in res["final_norm_g"])
    pw_s = pl.BlockSpec((N_CHIP, 1, POOL_GROUP, POOL_GROUP), lambda j: (0, j, 0, 0))
    pw_w = pl.BlockSpec((DEPTH, 1, POOL_GROUP, POOL_GROUP), lambda j: (0, j, 0, 0))
    c_s = pl.BlockSpec((N_CHIP, CH_G, GROUP_W, STATE), lambda j: (0, j, 0, 0))
    c_w = pl.BlockSpec((DEPTH, CH_G, GROUP_W, STATE), lambda j: (0, j, 0, 0))
    entries_b = [(per_layer(n), weights[n], mom_m[n], mom_v[n], pw_s if n == "pool_w" else c_s,
                  pw_w if n == "pool_w" else c_w) for n in mat_names]
    out_b, _ = _adamw_small("small_b", entries_b, grid=(N_CHUNK,))
    for n, r in zip(mat_names, out_b):
        res[n] = tuple(b_t(a) for a in r) if n in ("ssm_b_re", "ssm_b_im") else r

    outs = [loss, grad_x]
    for i in range(4):
        outs += [res[n][i] for n in order]
    return tuple(outs)
```

```python
import math

import jax
import jax.numpy as jnp
from jax import lax
from jax.experimental import pallas as pl
from jax.experimental.pallas import tpu as pltpu

F32 = jnp.float32
BF16 = jnp.bfloat16

SEQ = 2048
D_MODEL = 1024
N_IN = 4096
WIDTH = 512
N_GROUP = 32
GROUP_W = 16
STATE = 64
N_STATE = N_GROUP * STATE
N_CHUNK = 4
CH_G = N_GROUP // N_CHUNK
CH_W = WIDTH // N_CHUNK
CH_S = N_STATE // N_CHUNK
N_DEV = 8
N_CHIP = 4
POOL_WINDOWS = (2, 4, 8, 16)
POOL_GROUP = 128
EPS = 1e-6
DEPTH = 2

ADAM_LR = 0.001
ADAM_B1 = 0.9
ADAM_B2 = 0.999
ADAM_EPS = 1e-08
ADAM_WD = 0.01
ADAM_STEP = 10

LANES = 128
SUBLANES = 8
TILE_M = 256
VMEM_LIMIT = 48 * 1024 * 1024
VMEM_LIMIT_BIG = 60 * 1024 * 1024
MESH = pl.DeviceIdType.MESH
ANY = pl.BlockSpec(memory_space=pl.ANY)

GELU_C = math.sqrt(2.0 / math.pi)
GELU_A = 0.044715

SDS = jax.ShapeDtypeStruct


def _cp(sem=None, limit=VMEM_LIMIT):
    return pltpu.CompilerParams(dimension_semantics=sem, vmem_limit_bytes=limit)


def _dot(a, b):
    return jnp.dot(a, b, preferred_element_type=F32)


def _dot_nt(a, b):
    return lax.dot_general(a, b, (((1,), (1,)), ((), ())), preferred_element_type=F32)


def _dot_tn(a, b):
    return lax.dot_general(a, b, (((0,), (0,)), ((), ())), preferred_element_type=F32)


def _sig(x):
    return jax.nn.sigmoid(x)


def _rms(x):
    rs = lax.rsqrt(jnp.mean(x * x, axis=-1, keepdims=True) + EPS)
    return rs, x * rs


def _slot(n):
    return 4 * (n % 2) + n // 2


def _const(shape):
    n = len(shape)
    return pl.BlockSpec(shape, lambda *_: (0,) * n)


def _pair_sum(vals):
    while len(vals) > 1:
        vals = [vals[i] + vals[i + 1] for i in range(0, len(vals), 2)]
    return vals[0]


def _sum_slots(s_ref):
    return _pair_sum([s_ref[k].astype(F32) for k in range(s_ref.shape[0])])


def _s5_param_fn(log_dt, lam_re, lam_im, bt_re, bt_im):
    dt = jnp.exp(log_dt)
    mag = jnp.exp(lam_re * dt)
    ang = lam_im * dt
    abar_re = mag * jnp.cos(ang)
    abar_im = mag * jnp.sin(ang)
    num_re = abar_re - 1.0
    num_im = abar_im
    den = lam_re * lam_re + lam_im * lam_im
    coef_re = (num_re * lam_re + num_im * lam_im) / den
    coef_im = (num_im * lam_re - num_re * lam_im) / den
    bbar_re = coef_re[..., None, :] * bt_re - coef_im[..., None, :] * bt_im
    bbar_im = coef_re[..., None, :] * bt_im + coef_im[..., None, :] * bt_re
    return abar_re, abar_im, bbar_re, bbar_im


def _s5_params(log_dt, lam_re, lam_im, bt_re, bt_im):
    def body(ld, lr, li, br, bi, o_ar, o_ai, o_br, o_bi):
        ar, ai, bbr, bbi = _s5_param_fn(ld[...], lr[...], li[...], br[...], bi[...])
        o_ar[...] = ar
        o_ai[...] = ai
        o_br[...] = bbr
        o_bi[...] = bbi

    return pl.pallas_call(
        body, name="s5_params",
        out_shape=(SDS(lam_re.shape, F32), SDS(lam_re.shape, F32), SDS(bt_re.shape, F32), SDS(bt_re.shape, F32)),
    )(log_dt, lam_re, lam_im, bt_re, bt_im)


def _s5_params_bwd(layer, log_dt, lam_re, lam_im, bt_re, bt_im, g_ar, g_ai, g_br, g_bi):
    def body(ld, lr, li, br, bi, car, cai, cbr, cbi, o_ld, o_lr, o_li, o_br, o_bi):
        _, vjp = jax.vjp(_s5_param_fn, ld[...], lr[...], li[...], br[...], bi[...])
        d_ld, d_lr, d_li, d_br, d_bi = vjp((car[...], cai[...], cbr[...], cbi[...]))
        o_ld[...] = d_ld
        o_lr[...] = d_lr
        o_li[...] = d_li
        o_br[...] = d_br
        o_bi[...] = d_bi

    one = lambda shape: pl.BlockSpec((None,) + shape, lambda i: (layer,) + (0,) * len(shape))
    whole = lambda shape: _const(shape)
    vec, lam, mat = (N_GROUP, 1), (N_GROUP, STATE), (N_GROUP, GROUP_W, STATE)
    return pl.pallas_call(
        body, name=f"s5_params_bwd_l{layer}", grid=(1,),
        in_specs=[one(vec), one(lam), one(lam), one(mat), one(mat), whole(lam), whole(lam), whole(mat), whole(mat)],
        out_specs=(whole(vec), whole(lam), whole(lam), whole(mat), whole(mat)),
        out_shape=(SDS(vec, F32), SDS(lam, F32), SDS(lam, F32), SDS(mat, F32), SDS(mat, F32)),
    )(log_dt, lam_re, lam_im, bt_re, bt_im, g_ar, g_ai, g_br, g_bi)


def _norm_proj(layer, x, norm_g, wg_in, b_in, carry=None):
    n_w = len(wg_in)

    def body(x_ref, g_ref, b_ref, *refs):
        w_refs, o_ref = refs[:n_w], refs[n_w]
        _, xn = _rms(x_ref[...])
        h = (xn * g_ref[layer:layer + 1, :]).astype(BF16)
        for k in range(N_DEV):
            cols = slice(k * WIDTH, (k + 1) * WIDTH)
            acc = b_ref[layer:layer + 1, cols]
            row = 0
            for w_ref in w_refs:
                rows = w_ref.shape[1]
                acc = acc + _dot(h[:, row:row + rows], w_ref[k])
                row += rows
            o_ref[:, cols] = acc

    (proj,), moved = _pcall(
        body, name=f"norm_proj_l{layer}",
        out_shape=[SDS((SEQ, N_IN), F32)],
        grid=(SEQ // TILE_M,),
        in_specs=[pl.BlockSpec((TILE_M, D_MODEL), lambda i: (i, 0)),
                  _const((DEPTH, D_MODEL)),
                  _const((DEPTH, N_IN))] + [_const(w.shape) for w in wg_in],
        out_specs=[pl.BlockSpec((TILE_M, N_IN), lambda i: (i, 0))],
        args=[x, norm_g, b_in, *wg_in], sem=("parallel",), carry=carry)
    return proj, moved


TIME_BLK = 512
N_TBLK = SEQ // TIME_BLK
N_PANEL = CH_S // LANES
STATE_SHAPE = (N_PANEL, SEQ * SUBLANES, LANES)


def _s5_layer_specs(layer):
    mat = lambda: pl.BlockSpec((None, N_GROUP, GROUP_W, STATE), lambda i: (layer, 0, 0, 0))
    ab = lambda: pl.BlockSpec((None, N_GROUP, STATE), lambda i: (layer, 0, 0))
    return [mat(), mat(), mat(), mat(), ab(), ab(), _const((DEPTH, WIDTH))]


def _s5_layer_scratch():
    return [pltpu.VMEM((N_CHUNK, CH_W, CH_S), BF16)] * 4 + [pltpu.VMEM((8, CH_S), F32)] * 2


def _s5_layer_fill(btre_ref, btim_ref, cre_ref, cim_ref, are_ref, aim_ref, bdre, bdim, ctre, ctim, a1, a2):
    for m in (bdre, bdim, ctre, ctim):
        m[...] = jnp.zeros_like(m)
    for grp in range(N_GROUP):
        k, g = divmod(grp, CH_G)
        rows = slice(g * GROUP_W, (g + 1) * GROUP_W)
        cols = slice(g * STATE, (g + 1) * STATE)
        bdre[k, rows, cols] = btre_ref[grp].astype(BF16)
        bdim[k, rows, cols] = btim_ref[grp].astype(BF16)
        ctre[k, rows, cols] = cre_ref[grp].astype(BF16)
        ctim[k, rows, cols] = cim_ref[grp].astype(BF16)
        ar = are_ref[grp:grp + 1, :]
        ai = aim_ref[grp:grp + 1, :]
        a1[k:k + 1, cols] = ar
        a1[N_CHUNK + k:N_CHUNK + k + 1, cols] = ar
        a2[k:k + 1, cols] = -ai
        a2[N_CHUNK + k:N_CHUNK + k + 1, cols] = ai


SCAN_UNROLL = 16


def _panels(tile):
    return [tile[:, p * LANES:(p + 1) * LANES] for p in range(N_PANEL)]


def _rows_load(ref, row):
    return jnp.concatenate([ref[p, pl.ds(row, TIME_BLK, stride=SUBLANES), :] for p in range(N_PANEL)], axis=1)


def _rows_store(ref, row, val):
    for p in range(N_PANEL):
        ref[p, pl.ds(row, TIME_BLK, stride=SUBLANES), :] = val[:, p * LANES:(p + 1) * LANES]


def _s5_scan_fwd(layer, proj, bbt_re, bbt_im, c_re, c_im, abar_re, abar_im, d_skip, carry=None):
    def body(u_ref, btre_ref, btim_ref, cre_ref, cim_ref, are_ref, aim_ref, d_ref, s_ref, y_ref,
             bdre, bdim, ctre, ctim, a1, a2, state):
        @pl.when(pl.program_id(0) == 0)
        def _():
            _s5_layer_fill(btre_ref, btim_ref, cre_ref, cim_ref, are_ref, aim_ref, bdre, bdim, ctre, ctim, a1, a2)
            state[...] = jnp.zeros_like(state)

        for k in range(N_CHUNK):
            ub = u_ref[:, k * CH_W:(k + 1) * CH_W].astype(BF16)
            _rows_store(s_ref, k, _dot(ub, bdre[k]))
            _rows_store(s_ref, N_CHUNK + k, _dot(ub, bdim[k]))
        m1 = _panels(a1[...])
        m2 = _panels(a2[...])

        def steps(n, tile):
            for r in range(SCAN_UNROLL):
                rows = pl.ds(pl.multiple_of((n * SCAN_UNROLL + r) * 8, 8), 8)
                tile = [m1[p] * tile[p] + m2[p] * pltpu.roll(tile[p], N_CHUNK, 0) + s_ref[p, rows, :]
                        for p in range(N_PANEL)]
                for p in range(N_PANEL):
                    s_ref[p, rows, :] = tile[p]
            return tile

        tile = lax.fori_loop(0, TIME_BLK // SCAN_UNROLL, steps, _panels(state[...]))
        state[...] = jnp.concatenate(tile, axis=1)
        d = d_ref[layer:layer + 1, :]
        for k in range(N_CHUNK):
            cols = slice(k * CH_W, (k + 1) * CH_W)
            y = (_dot_nt(_rows_load(s_ref, k).astype(BF16), ctre[k])
                 - _dot_nt(_rows_load(s_ref, N_CHUNK + k).astype(BF16), ctim[k]))
            y_ref[:, cols] = y + d[:, cols] * u_ref[:, cols]

    return _pcall(
        body, name=f"s5_fwd_l{layer}",
        out_shape=(SDS(STATE_SHAPE, F32), SDS((SEQ, WIDTH), F32)),
        grid=(N_TBLK,),
        in_specs=[pl.BlockSpec((TIME_BLK, WIDTH), lambda i: (i, 0))] + _s5_layer_specs(layer),
        out_specs=(pl.BlockSpec((N_PANEL, TIME_BLK * SUBLANES, LANES), lambda i: (0, i, 0)),
                   pl.BlockSpec((TIME_BLK, WIDTH), lambda i: (i, 0))),
        scratch_shapes=_s5_layer_scratch() + [pltpu.VMEM((8, CH_S), F32)],
        args=[proj, bbt_re, bbt_im, c_re, c_im, abar_re, abar_im, d_skip], sem=("arbitrary",), carry=carry)


def _s5_scan_bwd(layer, dy0, proj, states, bbt_re, bbt_im, c_re, c_im, abar_re, abar_im, d_skip, dproj,
                 carry=None):
    def body(dy_ref, u_ref, s_ref, sprev_ref, btre_ref, btim_ref, cre_ref, cim_ref, are_ref, aim_ref, d_ref, _,
             du_ref, gbre_ref, gbim_ref, gcre_ref, gcim_ref, gare_ref, gaim_ref, gd_ref,
             lam_ref, bdre, bdim, ctre, ctim, a1, a2, state, acc1, acc2, gbre, gbim, gcre, gcim, gd):
        step_id = pl.program_id(0)

        @pl.when(step_id == 0)
        def _():
            _s5_layer_fill(btre_ref, btim_ref, cre_ref, cim_ref, are_ref, aim_ref, bdre, bdim, ctre, ctim, a1, a2)
            for r in (state, acc1, acc2, gbre, gbim, gcre, gcim, gd):
                r[...] = jnp.zeros_like(r)

        for k in range(N_CHUNK):
            dyb = dy_ref[:, k * CH_W:(k + 1) * CH_W].astype(BF16)
            _rows_store(lam_ref, k, _dot(dyb, ctre[k]))
            _rows_store(lam_ref, N_CHUNK + k, -_dot(dyb, ctim[k]))
            gcre[k] += _dot_tn(dyb, _rows_load(s_ref, k).astype(BF16))
            gcim[k] -= _dot_tn(dyb, _rows_load(s_ref, N_CHUNK + k).astype(BF16))

        m1 = _panels(a1[...])
        m2 = _panels(-a2[...])
        has_before = (step_id < N_TBLK - 1).astype(F32)

        def one(t8, c, first_token):
            tile, swapped, p1, p2 = c
            rows = pl.ds(t8, 8)
            tile = [m1[p] * tile[p] + m2[p] * swapped[p] + lam_ref[p, rows, :] for p in range(N_PANEL)]
            swapped = [pltpu.roll(tile[p], N_CHUNK, 0) for p in range(N_PANEL)]
            for p in range(N_PANEL):
                lam_ref[p, rows, :] = tile[p]
            if first_token:
                before = [sprev_ref[p] * has_before for p in range(N_PANEL)]
            else:
                before = [s_ref[p, pl.ds(t8 - 8, 8), :] for p in range(N_PANEL)]
            p1 = [p1[p] + tile[p] * before[p] for p in range(N_PANEL)]
            p2 = [p2[p] + swapped[p] * before[p] for p in range(N_PANEL)]
            return tile, swapped, p1, p2

        def steps(n, c):
            for r in range(SCAN_UNROLL):
                t8 = pl.multiple_of((TIME_BLK - 1 - (n * SCAN_UNROLL + r)) * 8, 8)
                c = one(t8, c, False)
            return c

        tile0 = _panels(state[...])
        c = (tile0, [pltpu.roll(t, N_CHUNK, 0) for t in tile0], _panels(acc1[...]), _panels(acc2[...]))
        c = lax.fori_loop(0, TIME_BLK // SCAN_UNROLL - 1, steps, c)
        for r in range(SCAN_UNROLL - 1, -1, -1):
            c = one(r * 8, c, r == 0)
        state[...] = jnp.concatenate(c[0], axis=1)
        acc1[...] = jnp.concatenate(c[2], axis=1)
        acc2[...] = jnp.concatenate(c[3], axis=1)

        d = d_ref[layer:layer + 1, :]
        for k in range(N_CHUNK):
            cols = slice(k * CH_W, (k + 1) * CH_W)
            lrb = _rows_load(lam_ref, k).astype(BF16)
            lib = _rows_load(lam_ref, N_CHUNK + k).astype(BF16)
            u = u_ref[:, cols]
            ub = u.astype(BF16)
            dy = dy_ref[:, cols]
            du = dy * d[:, cols] + _dot_nt(lrb, bdre[k]) + _dot_nt(lib, bdim[k])
            du_ref[:, cols] = du.astype(BF16)
            gbre[k] += _dot_tn(ub, lrb)
            gbim[k] += _dot_tn(ub, lib)
        gd[...] += jnp.sum(dy_ref[...] * u_ref[...], axis=0, keepdims=True)

        @pl.when(step_id == N_TBLK - 1)
        def _():
            gd_ref[...] = gd[...]
            ga_re = acc1[0:N_CHUNK, :] + acc1[N_CHUNK:, :]
            ga_im = acc2[0:N_CHUNK, :] - acc2[N_CHUNK:, :]
            for grp in range(N_GROUP):
                k, g = divmod(grp, CH_G)
                rows = slice(g * GROUP_W, (g + 1) * GROUP_W)
                cols = slice(g * STATE, (g + 1) * STATE)
                gcre_ref[grp] = gcre[k, rows, cols]
                gcim_ref[grp] = gcim[k, rows, cols]
                gbre_ref[grp] = gbre[k, rows, cols]
                gbim_ref[grp] = gbim[k, rows, cols]
                gare_ref[grp:grp + 1, :] = ga_re[k:k + 1, cols]
                gaim_ref[grp:grp + 1, :] = ga_im[k:k + 1, cols]

    back = lambda i: N_TBLK - 1 - i
    tok = lambda: pl.BlockSpec((TIME_BLK, WIDTH), lambda i: (back(i), 0))
    mat = lambda: _const((N_GROUP, GROUP_W, STATE))
    acc_mat = pltpu.VMEM((N_CHUNK, CH_W, CH_S), F32)
    return _pcall(
        body, name=f"s5_bwd_l{layer}",
        out_shape=(SDS((SEQ, N_IN), BF16), SDS((N_GROUP, GROUP_W, STATE), F32), SDS((N_GROUP, GROUP_W, STATE), F32),
                   SDS((N_GROUP, GROUP_W, STATE), F32), SDS((N_GROUP, GROUP_W, STATE), F32),
                   SDS((N_GROUP, STATE), F32), SDS((N_GROUP, STATE), F32), SDS((1, WIDTH), F32)),
        grid=(N_TBLK,),
        in_specs=[tok(), tok(),
                  pl.BlockSpec((N_PANEL, TIME_BLK * SUBLANES, LANES), lambda i: (0, back(i), 0)),
                  pl.BlockSpec((N_PANEL, SUBLANES, LANES), lambda i: (0, jnp.maximum(back(i) * TIME_BLK - 1, 0), 0))]
        + _s5_layer_specs(layer) + [ANY],
        out_specs=(tok(), mat(), mat(), mat(), mat(), _const((N_GROUP, STATE)), _const((N_GROUP, STATE)),
                   _const((1, WIDTH))),
        scratch_shapes=[pltpu.VMEM((N_PANEL, TIME_BLK * SUBLANES, LANES), F32)] + _s5_layer_scratch()
        + [pltpu.VMEM((8, CH_S), F32)] * 3 + [acc_mat] * 4 + [pltpu.VMEM((1, WIDTH), F32)],
        args=[dy0, proj, states, states, bbt_re, bbt_im, c_re, c_im, abar_re, abar_im, d_skip, dproj],
        aliases={11: 0}, sem=("arbitrary",), limit=VMEM_LIMIT_BIG, carry=carry)


def _pool_counts(win):
    t = lax.broadcasted_iota(jnp.int32, (SEQ, POOL_GROUP), 0)
    return t, jnp.minimum(t + 1, win).astype(F32)


def _pool_fwd(layer, proj):
    def body(u_ref, o_ref):
        for gi, win in enumerate(POOL_WINDOWS):
            cols = slice(gi * POOL_GROUP, (gi + 1) * POOL_GROUP)
            u = u_ref[:, cols]
            t, count = _pool_counts(win)
            acc = u
            k = 1
            while k < win:
                acc = acc + jnp.where(t >= k, pltpu.roll(acc, k, 0), 0.0)
                k *= 2
            o_ref[:, cols] = acc / count - u

    return pl.pallas_call(
        body, name=f"pool_fwd_l{layer}",
        out_shape=SDS((SEQ, WIDTH), F32),
        grid=(1,),
        in_specs=[pl.BlockSpec((SEQ, WIDTH), lambda i: (0, 2))],
        out_specs=pl.BlockSpec((SEQ, WIDTH), lambda i: (0, 0)),
        compiler_params=_cp(("arbitrary",)),
    )(proj)


def _gelu_parts(y0):
    t = jnp.tanh(GELU_C * (y0 + GELU_A * (y0 * y0 * y0)))
    return t, 0.5 * y0 * (1.0 + t)


def _mix_forward(layer, p_ref, y0_ref, pooled_ref, wglu_ref, bglu_ref, pw_ref, scale_ref, wa_ref, wb_ref):
    za = p_ref[:, WIDTH:2 * WIDTH]
    zb = p_ref[:, 3 * WIDTH:4 * WIDTH]
    ga = p_ref[:, 4 * WIDTH:4 * WIDTH + D_MODEL]
    gb = p_ref[:, 4 * WIDTH + D_MODEL:]
    y0 = y0_ref[...]
    t, y1 = _gelu_parts(y0)
    y1b = y1.astype(BF16)
    q = _dot(y1b, wglu_ref[...].reshape(WIDTH, WIDTH)) + bglu_ref[layer:layer + 1, :]
    sq = _sig(q)
    y2 = y1 * sq
    sza = _sig(za)
    silu_za = za * sza
    ya = y2 * silu_za
    pooled = pooled_ref[...]
    mixed = jnp.concatenate(
        [_dot(pooled[:, g * POOL_GROUP:(g + 1) * POOL_GROUP].astype(BF16), pw_ref[g].astype(BF16))
         for g in range(len(POOL_WINDOWS))], axis=1)
    szb = _sig(zb)
    silu_zb = zb * szb
    scale = scale_ref[layer:layer + 1, :]
    ms = mixed * scale
    yb = ms * silu_zb
    yab = ya.astype(BF16)
    ybb = yb.astype(BF16)
    ma = _dot(yab, wa_ref[...])
    mb = _dot(ybb, wb_ref[...])
    sga = _sig(ga)
    sgb = _sig(gb)
    merged = sga * ma + sgb * mb
    return dict(za=za, zb=zb, y0=y0, t=t, y1=y1, y1b=y1b, sq=sq, y2=y2, sza=sza, silu_za=silu_za,
                pooled=pooled, mixed=mixed, szb=szb, silu_zb=silu_zb, scale=scale, ms=ms, yab=yab, ybb=ybb,
                ma=ma, mb=mb, sga=sga, sgb=sgb, merged=merged)


def _mix_weight_specs(layer):
    return [_const((N_DEV, WIDTH // N_DEV, WIDTH)),
            _const((DEPTH, WIDTH)),
            pl.BlockSpec((None, 4, POOL_GROUP, POOL_GROUP), lambda i: (layer, 0, 0, 0)),
            _const((DEPTH, WIDTH)),
            _const((WIDTH, D_MODEL)),
            _const((WIDTH, D_MODEL)),
            _const((N_DEV, D_MODEL // N_DEV, D_MODEL))]


def _mix_fwd(layer, x, proj, y0, pooled, wg_glu, b_glu, pool_w, pool_scale, wg_a, wg_b, wg_out, carry=None):
    def body(x_ref, p_ref, y0_ref, pooled_ref, wglu_ref, bglu_ref, pw_ref, scale_ref, wa_ref, wb_ref,
             wout_ref, o_ref):
        f = _mix_forward(layer, p_ref, y0_ref, pooled_ref, wglu_ref, bglu_ref, pw_ref, scale_ref, wa_ref, wb_ref)
        wout = wout_ref[...].reshape(D_MODEL, D_MODEL)
        o_ref[...] = x_ref[...] + _dot(f["merged"].astype(BF16), wout)

    (x_next,), moved = _pcall(
        body, name=f"mix_fwd_l{layer}",
        out_shape=[SDS((SEQ, D_MODEL), F32)],
        grid=(SEQ // TILE_M,),
        in_specs=[pl.BlockSpec((TILE_M, D_MODEL), lambda i: (i, 0)),
                  pl.BlockSpec((TILE_M, N_IN), lambda i: (i, 0)),
                  pl.BlockSpec((TILE_M, WIDTH), lambda i: (i, 0)),
                  pl.BlockSpec((TILE_M, WIDTH), lambda i: (i, 0))] + _mix_weight_specs(layer),
        out_specs=[pl.BlockSpec((TILE_M, D_MODEL), lambda i: (i, 0))],
        args=[x, proj, y0, pooled, wg_glu, b_glu, pool_w, pool_scale, wg_a, wg_b, wg_out],
        sem=("parallel",), carry=carry)
    return x_next, moved


def _loss_head(x, target, final_g):
    def body(x_ref, t_ref, g_ref, dx_ref, loss_ref, gg_ref):
        @pl.when(pl.program_id(0) == 0)
        def _():
            loss_ref[...] = jnp.zeros_like(loss_ref)
            gg_ref[...] = jnp.zeros_like(gg_ref)

        g = g_ref[...]
        rs, xn = _rms(x_ref[...])
        err = xn * g - t_ref[...]
        loss_ref[...] += 0.5 * jnp.sum(jnp.mean(err * err, axis=-1, keepdims=True), axis=0, keepdims=True)
        dy = err * (1.0 / D_MODEL)
        gg_ref[...] += jnp.sum(dy * xn, axis=0, keepdims=True)
        dxn = dy * g
        dx_ref[...] = rs * (dxn - xn * jnp.mean(dxn * xn, axis=-1, keepdims=True))

    return pl.pallas_call(
        body, name="loss_head",
        out_shape=(SDS((SEQ, D_MODEL), F32), SDS((1, 1), F32), SDS((1, D_MODEL), F32)),
        grid=(SEQ // TILE_M,),
        in_specs=[pl.BlockSpec((TILE_M, D_MODEL), lambda i: (i, 0)),
                  pl.BlockSpec((TILE_M, D_MODEL), lambda i: (i, 0)),
                  _const((1, D_MODEL))],
        out_specs=(pl.BlockSpec((TILE_M, D_MODEL), lambda i: (i, 0)), _const((1, 1)), _const((1, D_MODEL))),
        compiler_params=_cp(("arbitrary",)),
    )(x, target, final_g)


def _big_shapes():
    return dict(w_out=(DEPTH, N_DEV, D_MODEL // N_DEV, D_MODEL), w_branch_a=(DEPTH, N_DEV, WIDTH, D_MODEL // N_DEV),
                w_branch_b=(DEPTH, N_DEV, WIDTH, D_MODEL // N_DEV), ssm_w_glu=(DEPTH, N_DEV, WIDTH // N_DEV, WIDTH),
                w_in=(DEPTH, N_DEV, D_MODEL, WIDTH))


def _mix_bwd(layer, dx_next, proj, y0, pooled, wg_glu, b_glu, pool_w, pool_scale, wg_a, wg_b, wg_out, prev,
             carry=None):
    n_k = N_DEV
    n_prev = 0 if prev is None else len(prev)

    def body(*refs):
        (dx_ref, p_ref, y0_ref, pooled_ref, wglu_ref, bglu_ref, pw_ref, scale_ref, wa_ref, wb_ref,
         wout_ref) = refs[:11]
        (dproj_ref, dy0_ref, dpooled_ref, gwout_ref, gwa_ref, gwb_ref, gwglu_ref, gpw_ref,
         gscale_ref, gbglu_ref) = refs[11 + n_prev:]

        @pl.when(pl.program_id(0) == 0)
        def _():
            for r in (gwout_ref, gwa_ref, gwb_ref, gwglu_ref, gpw_ref, gscale_ref, gbglu_ref):
                r[...] = jnp.zeros_like(r)

        f = _mix_forward(layer, p_ref, y0_ref, pooled_ref, wglu_ref, bglu_ref, pw_ref, scale_ref, wa_ref, wb_ref)
        wglu = wglu_ref[...].reshape(WIDTH, WIDTH)
        wout = wout_ref[...].reshape(D_MODEL, D_MODEL)
        blk = D_MODEL // n_k
        dxb = dx_ref[...].astype(BF16)
        dmerged = _dot_nt(dxb, wout)
        gwout = _dot_tn(f["merged"].astype(BF16), dxb)
        for k in range(n_k):
            gwout_ref[_slot(k)] += gwout[k * blk:(k + 1) * blk, :]
        dma = dmerged * f["sga"]
        dmb = dmerged * f["sgb"]
        dga = dmerged * f["ma"] * f["sga"] * (1.0 - f["sga"])
        dgb = dmerged * f["mb"] * f["sgb"] * (1.0 - f["sgb"])
        dmab = dma.astype(BF16)
        dmbb = dmb.astype(BF16)
        dya = _dot_nt(dmab, wa_ref[...])
        dyb = _dot_nt(dmbb, wb_ref[...])
        gwa = _dot_tn(f["yab"], dmab)
        gwb = _dot_tn(f["ybb"], dmbb)
        for k in range(n_k):
            gwa_ref[_slot(k)] += gwa[:, k * blk:(k + 1) * blk]
            gwb_ref[_slot(k)] += gwb[:, k * blk:(k + 1) * blk]
        zb, szb = f["zb"], f["szb"]
        dzb = dyb * f["ms"] * (szb * (1.0 + zb * (1.0 - szb)))
        dms = dyb * f["silu_zb"]
        gscale_ref[...] += jnp.sum(dms * f["mixed"], axis=0, keepdims=True)
        dmixed = (dms * f["scale"]).astype(BF16)
        pooled = f["pooled"]
        for g in range(len(POOL_WINDOWS)):
            cols = slice(g * POOL_GROUP, (g + 1) * POOL_GROUP)
            dpooled_ref[:, cols] = _dot_nt(dmixed[:, cols], pw_ref[g].astype(BF16))
            gpw_ref[g] += _dot_tn(pooled[:, cols].astype(BF16), dmixed[:, cols])
        za, sza = f["za"], f["sza"]
        dza = dya * f["y2"] * (sza * (1.0 + za * (1.0 - sza)))
        dy2 = dya * f["silu_za"]
        sq = f["sq"]
        dq = dy2 * f["y1"] * sq * (1.0 - sq)
        dqb = dq.astype(BF16)
        dy1 = dy2 * sq + _dot_nt(dqb, wglu)
        gwglu = _dot_tn(f["y1b"], dqb)
        rblk = WIDTH // n_k
        for k in range(n_k):
            gwglu_ref[_slot(k)] += gwglu[k * rblk:(k + 1) * rblk, :]
        gbglu_ref[...] += jnp.sum(dq, axis=0, keepdims=True)
        y0, t = f["y0"], f["t"]
        dgelu = 0.5 * (1.0 + t) + 0.5 * y0 * (1.0 - t * t) * (GELU_C * (1.0 + 3.0 * GELU_A * y0 * y0))
        dy0_ref[...] = dy1 * dgelu
        zeros = jnp.zeros((TILE_M, WIDTH), BF16)
        dproj_ref[:, 0:WIDTH] = zeros
        dproj_ref[:, WIDTH:2 * WIDTH] = dza.astype(BF16)
        dproj_ref[:, 2 * WIDTH:3 * WIDTH] = zeros
        dproj_ref[:, 3 * WIDTH:4 * WIDTH] = dzb.astype(BF16)
        dproj_ref[:, 4 * WIDTH:4 * WIDTH + D_MODEL] = dga.astype(BF16)
        dproj_ref[:, 4 * WIDTH + D_MODEL:] = dgb.astype(BF16)

    tile = lambda w: pl.BlockSpec((TILE_M, w), lambda i: (i, 0))
    shapes = _big_shapes()
    big = ["w_out", "w_branch_a", "w_branch_b", "ssm_w_glu"]
    slab = lambda n: pl.BlockSpec((None,) + shapes[n][1:], lambda i: (layer, 0, 0, 0))
    args = [dx_next, proj, y0, pooled, wg_glu, b_glu, pool_w, pool_scale, wg_a, wg_b, wg_out]
    return _pcall(
        body, name=f"mix_bwd_l{layer}",
        out_shape=(SDS((SEQ, N_IN), BF16), SDS((SEQ, WIDTH), F32), SDS((SEQ, WIDTH), F32))
        + tuple(SDS(shapes[n], F32) for n in big)
        + (SDS((4, POOL_GROUP, POOL_GROUP), F32), SDS((1, WIDTH), F32), SDS((1, WIDTH), F32)),
        grid=(SEQ // TILE_M,),
        in_specs=[tile(D_MODEL), tile(N_IN), tile(WIDTH), tile(WIDTH)] + _mix_weight_specs(layer) + [ANY] * n_prev,
        out_specs=(tile(N_IN), tile(WIDTH), tile(WIDTH)) + tuple(slab(n) for n in big)
        + (_const((4, POOL_GROUP, POOL_GROUP)), _const((1, WIDTH)), _const((1, WIDTH))),
        args=args + list(prev or ()),
        aliases={len(args) + i: 3 + i for i in range(n_prev)},
        sem=("arbitrary",), limit=VMEM_LIMIT_BIG, carry=carry)


def _pool_bwd(layer, dpooled, dproj):
    def body(dp_ref, _, o_ref):
        for gi, win in enumerate(POOL_WINDOWS):
            cols = slice(gi * POOL_GROUP, (gi + 1) * POOL_GROUP)
            dp = dp_ref[:, cols]
            t, count = _pool_counts(win)
            e = dp / count
            acc = e
            k = 1
            while k < win:
                acc = acc + jnp.where(t < SEQ - k, pltpu.roll(acc, SEQ - k, 0), 0.0)
                k *= 2
            o_ref[:, cols] = (acc - dp).astype(BF16)

    return pl.pallas_call(
        body, name=f"pool_bwd_l{layer}",
        out_shape=SDS((SEQ, N_IN), BF16),
        grid=(1,),
        in_specs=[pl.BlockSpec((SEQ, WIDTH), lambda i: (0, 0)), ANY],
        out_specs=pl.BlockSpec((SEQ, WIDTH), lambda i: (0, 2)),
        input_output_aliases={1: 0},
        compiler_params=_cp(("arbitrary",)),
    )(dpooled, dproj)


def _proj_wgrad(layer, x, norm_g, dproj, prev, carry=None):
    tm = 512
    n_prev = 0 if prev is None else 1

    def body(*refs):
        x_ref, g_ref, dp_ref = refs[:3]
        gw_ref, gb_ref, ht_ref = refs[3 + n_prev:]
        n, t = pl.program_id(0), pl.program_id(1)

        @pl.when(t == 0)
        def _():
            gw_ref[...] = jnp.zeros_like(gw_ref)
            gb_ref[...] = jnp.zeros_like(gb_ref)

        @pl.when(n == 0)
        def _():
            _, xn = _rms(x_ref[...])
            ht_ref[t] = (xn * g_ref[layer:layer + 1, :]).T.astype(BF16)

        dp = dp_ref[...]
        gw_ref[...] += _dot(ht_ref[t], dp)
        gb_ref[...] += jnp.sum(dp.astype(F32), axis=0, keepdims=True)

    return _pcall(
        body, name=f"proj_wgrad_l{layer}",
        out_shape=(SDS(_big_shapes()["w_in"], F32), SDS((1, N_IN), F32)),
        grid=(N_DEV, SEQ // tm),
        in_specs=[pl.BlockSpec((tm, D_MODEL), lambda n, t: (jnp.where(n == 0, t, 0), 0)),
                  _const((DEPTH, D_MODEL)),
                  pl.BlockSpec((tm, WIDTH), lambda n, t: (t, n))] + [ANY] * n_prev,
        out_specs=(pl.BlockSpec((None, None, D_MODEL, WIDTH), lambda n, t: (layer, _slot(n), 0, 0)),
                   pl.BlockSpec((1, WIDTH), lambda n, t: (0, n))),
        scratch_shapes=[pltpu.VMEM((SEQ // tm, D_MODEL, tm), BF16)],
        args=[x, norm_g, dproj] + ([prev] if n_prev else []),
        aliases={3: 0} if n_prev else {}, sem=("arbitrary", "arbitrary"), carry=carry)


def _proj_dgrad(layer, dx_next, x, norm_g, dproj, wg_in, carry=None):
    n_w = len(wg_in)

    def body(dxn_ref, x_ref, g_ref, dp_ref, *refs):
        w_refs, (dx_ref, gg_ref) = refs[:n_w], refs[n_w:]

        @pl.when(pl.program_id(0) == 0)
        def _():
            gg_ref[...] = jnp.zeros_like(gg_ref)

        parts = []
        for w_ref in w_refs:
            part = jnp.zeros((TILE_M, w_ref.shape[1]), F32)
            for k in range(N_DEV):
                part = part + _dot_nt(dp_ref[:, k * WIDTH:(k + 1) * WIDTH], w_ref[k])
            parts.append(part)
        dh = parts[0] if n_w == 1 else jnp.concatenate(parts, axis=1)
        rs, xn = _rms(x_ref[...])
        gg_ref[...] += jnp.sum(dh * xn, axis=0, keepdims=True)
        dxn = dh * g_ref[layer:layer + 1, :]
        dx_ref[...] = dxn_ref[...] + rs * (dxn - xn * jnp.mean(dxn * xn, axis=-1, keepdims=True))

    return _pcall(
        body, name=f"proj_dgrad_l{layer}",
        out_shape=(SDS((SEQ, D_MODEL), F32), SDS((1, D_MODEL), F32)),
        grid=(SEQ // TILE_M,),
        in_specs=[pl.BlockSpec((TILE_M, D_MODEL), lambda i: (i, 0)),
                  pl.BlockSpec((TILE_M, D_MODEL), lambda i: (i, 0)),
                  _const((DEPTH, D_MODEL)),
                  pl.BlockSpec((TILE_M, N_IN), lambda i: (i, 0))] + [_const(w.shape) for w in wg_in],
        out_specs=(pl.BlockSpec((TILE_M, D_MODEL), lambda i: (i, 0)), _const((1, D_MODEL))),
        args=[dx_next, x, norm_g, dproj, *wg_in], sem=("arbitrary",), carry=carry)


def _my_place():
    return lax.axis_index("x"), lax.axis_index("y"), lax.axis_index("c")


def _gather_plan(shards, layer, by_columns=(), rows_of=None):
    n = len(shards)

    def parts(ins, outs, sems):
        send_sems, recv_sems, local_sems = sems
        x, y, c = _my_place()
        chips = [(1 - x, y), (x, 1 - y), (1 - x, 1 - y)]

        def source(t):
            return ins[t].at[layer] if rows_of is None else ins[t].at[layer, pl.ds(*rows_of)]

        def rows(t, place):
            px, py, pc = place
            index = 4 * px + 2 * py + pc
            if t in by_columns:
                width = shards[t].shape[2]
                return outs[t].at[:, pl.ds(pl.multiple_of(index * width, LANES), width)]
            return outs[t].at[index]

        def copy(t, k, block, to, from_src=False):
            return pltpu.make_async_remote_copy(
                src_ref=source(t) if from_src else rows(t, block), dst_ref=rows(t, block),
                send_sem=send_sems.at[7 * t + k], recv_sem=recv_sems.at[7 * t + k], device_id=to,
                device_id_type=MESH)

        def mine(t):
            return pltpu.make_async_copy(source(t), rows(t, (x, y, c)), local_sems.at[t])

        return (x, y, c), chips, copy, mine

    def start(ins, outs, sems):
        me, chips, copy, mine = parts(ins, outs, sems)
        x, y, c = me
        for t in range(n):
            mine(t).start()
            copy(t, 0, me, (x, y, 1 - c), from_src=True).start()
            for j, chip in enumerate(chips):
                copy(t, 1 + j, me, (*chip, c), from_src=True).start()

    def relay(ins, outs, sems):
        me, chips, copy, mine = parts(ins, outs, sems)
        x, y, c = me
        for t in range(n):
            for j, chip in enumerate(chips):
                copy(t, 1 + j, (*chip, c), me).wait_recv()
                copy(t, 4 + j, (*chip, c), (x, y, 1 - c)).start()

    def finish(ins, outs, sems):
        me, chips, copy, mine = parts(ins, outs, sems)
        x, y, c = me
        sibling = (x, y, 1 - c)
        for t in range(n):
            copy(t, 0, sibling, me).wait_recv()
            for j, chip in enumerate(chips):
                copy(t, 4 + j, (*chip, 1 - c), me).wait_recv()
            for k in range(7):
                copy(t, k, me, sibling, from_src=k < 4).wait_send()
            mine(t).wait()

    n_rows = lambda a: a.shape[1] if rows_of is None else rows_of[1]
    out_shape = [SDS((a.shape[1], N_DEV * a.shape[2]) if t in by_columns else (N_DEV, n_rows(a), a.shape[2]), a.dtype)
                 for t, a in enumerate(shards)]
    sems = [pltpu.SemaphoreType.DMA((7 * n,)), pltpu.SemaphoreType.DMA((7 * n,)), pltpu.SemaphoreType.DMA((n,))]
    return _Carried(shards, out_shape, sems, start, finish, relay)


class _Carried:
    def __init__(self, ins, out_shape, sems, start, finish, relay=None):
        self.ins, self.out_shape, self.sems = list(ins), list(out_shape), list(sems)
        self.start, self.finish = start, finish
        self.relay = relay or (lambda ins, outs, sems: None)


def _pcall(body, *, name, grid, in_specs, out_specs, out_shape, args, scratch_shapes=(), aliases=None,
           sem=None, limit=VMEM_LIMIT, carry=None):
    out_shape, out_specs, scratch_shapes = list(out_shape), list(out_specs), list(scratch_shapes)
    n_in, n_out, n_scr = len(args), len(out_shape), len(scratch_shapes)
    if carry is None:
        kern, c_ins, c_out, c_sems = body, [], [], []
    else:
        c_ins, c_out, c_sems = carry.ins, carry.out_shape, carry.sems
        ci, co = len(c_ins), len(c_out)
        steps = tuple(grid)

        def kern(*refs):
            o0 = n_in + ci
            s0 = o0 + n_out + co
            mine = refs[:n_in] + refs[o0:o0 + n_out] + refs[s0:s0 + n_scr]
            theirs = (refs[n_in:o0], refs[o0 + n_out:s0], refs[s0 + n_scr:])
            first = pl.program_id(0) == 0
            last = pl.program_id(0) == steps[0] - 1
            for a in range(1, len(steps)):
                first = jnp.logical_and(first, pl.program_id(a) == 0)
                last = jnp.logical_and(last, pl.program_id(a) == steps[a] - 1)

            @pl.when(first)
            def _():
                carry.start(*theirs)

            @pl.when(last)
            def _():
                carry.relay(*theirs)

            body(*mine)

            @pl.when(last)
            def _():
                carry.finish(*theirs)

        sem = ("arbitrary",) * len(steps)
    res = pl.pallas_call(
        kern, name=name, grid=tuple(grid),
        in_specs=list(in_specs) + [ANY] * len(c_ins),
        out_specs=tuple(out_specs + [ANY] * len(c_out)),
        out_shape=tuple(out_shape + c_out),
        scratch_shapes=scratch_shapes + c_sems,
        input_output_aliases=aliases or {},
        compiler_params=_cp(sem, limit),
    )(*args, *c_ins)
    return res[:n_out], res[n_out:]


def _run_carried(name, carry):
    ci, co = len(carry.ins), len(carry.out_shape)

    def body(*refs):
        parts = (refs[:ci], refs[ci:ci + co], refs[ci + co:])
        carry.start(*parts)
        carry.relay(*parts)
        carry.finish(*parts)

    return pl.pallas_call(
        body, name=name, out_shape=tuple(carry.out_shape),
        in_specs=[ANY] * ci, out_specs=tuple([ANY] * co), scratch_shapes=carry.sems,
    )(*carry.ins)


def _sibling_plan(big, small):
    n = len(big)
    n_copies = 4 * n + len(small)

    def copies(ins, outs, sems):
        send_sems, recv_sems = sems
        x, y, c = _my_place()
        pairs = []
        for t, (_, layer) in enumerate(big):
            for s in range(4):
                pairs.append((ins[t].at[layer, pl.ds(4 * (1 - c) + s, 1)], outs[t].at[pl.ds(s, 1)]))
        pairs += list(zip(ins[n:], outs[n:]))
        return [pltpu.make_async_remote_copy(
            src_ref=src, dst_ref=dst, send_sem=send_sems.at[k], recv_sem=recv_sems.at[k],
            device_id=(x, y, 1 - c), device_id_type=MESH) for k, (src, dst) in enumerate(pairs)]

    def start(ins, outs, sems):
        for cp in copies(ins, outs, sems):
            cp.start()

    def finish(ins, outs, sems):
        for cp in copies(ins, outs, sems):
            cp.wait()

    out_shape = [SDS((4,) + a.shape[2:], a.dtype) for a, _ in big] + [SDS(a.shape, a.dtype) for a in small]
    sems = [pltpu.SemaphoreType.DMA((n_copies,)), pltpu.SemaphoreType.DMA((n_copies,))]
    return _Carried([a for a, _ in big] + list(small), out_shape, sems, start, finish)


def _chips_plan(big, small):
    n, n_small = len(big), len(small)
    max_rows = 512
    parts = [max(1, a.shape[1] // max_rows) for a in big]
    n_copies = 3 * (sum(parts) + n_small)

    def copies(ins, outs, sems, landing):
        send_sems, recv_sems, local_sems = sems
        x, y, c = _my_place()
        my_chip = 2 * x + y
        chips = [(1 - x, y), (x, 1 - y), (1 - x, 1 - y)]
        remote, local = [], []
        for chip in chips:
            to = 2 * chip[0] + chip[1]
            slot = to if landing else my_chip
            pairs = []
            for t in range(n):
                rows_per = big[t].shape[1] // parts[t]
                for p in range(parts[t]):
                    rows = pl.ds(p * rows_per, rows_per)
                    pairs.append((ins[t].at[to, rows], outs[t].at[slot, rows]))
            pairs += [(ins[t], outs[t].at[slot]) for t in range(n, n + n_small)]
            for src, dst in pairs:
                k = len(remote)
                remote.append(pltpu.make_async_remote_copy(
                    src_ref=src, dst_ref=dst, send_sem=send_sems.at[k], recv_sem=recv_sems.at[k],
                    device_id=(*chip, c), device_id_type=MESH))
        for t in range(n):
            local.append(pltpu.make_async_copy(ins[t].at[my_chip], outs[t].at[my_chip], local_sems.at[t]))
        for t in range(n, n + n_small):
            local.append(pltpu.make_async_copy(ins[t], outs[t].at[my_chip], local_sems.at[t]))
        return remote + local

    def start(ins, outs, sems):
        for cp in copies(ins, outs, sems, landing=False):
            cp.start()

    def finish(ins, outs, sems):
        for cp in copies(ins, outs, sems, landing=True):
            cp.wait()

    out_shape = [SDS(a.shape, a.dtype) for a in big] + [SDS((N_CHIP,) + a.shape, a.dtype) for a in small]
    sems = [pltpu.SemaphoreType.DMA((n_copies,)), pltpu.SemaphoreType.DMA((n_copies,)),
            pltpu.SemaphoreType.DMA((n + n_small,))]
    return _Carried(list(big) + list(small), out_shape, sems, start, finish)


def _all_plan(small):
    n = len(small)
    masks = [(m >> 2 & 1, m >> 1 & 1, m & 1) for m in range(1, N_DEV)]

    def copies(ins, outs, sems, landing):
        send_sems, recv_sems, local_sems = sems
        x, y, c = _my_place()
        me = 4 * x + 2 * y + c
        flip = lambda v, bit: 1 - v if bit else v
        remote = []
        for fx, fy, fc in masks:
            peer = (flip(x, fx), flip(y, fy), flip(c, fc))
            slot = 4 * peer[0] + 2 * peer[1] + peer[2] if landing else me
            for t in range(n):
                k = len(remote)
                remote.append(pltpu.make_async_remote_copy(
                    src_ref=ins[t], dst_ref=outs[t].at[slot], send_sem=send_sems.at[k], recv_sem=recv_sems.at[k],
                    device_id=peer, device_id_type=MESH))
        local = [pltpu.make_async_copy(ins[t], outs[t].at[me], local_sems.at[t]) for t in range(n)]
        return remote + local

    def start(ins, outs, sems):
        for cp in copies(ins, outs, sems, landing=False):
            cp.start()

    def finish(ins, outs, sems):
        for cp in copies(ins, outs, sems, landing=True):
            cp.wait()

    out_shape = [SDS((N_DEV,) + a.shape, a.dtype) for a in small]
    sems = [pltpu.SemaphoreType.DMA((7 * n,)), pltpu.SemaphoreType.DMA((7 * n,)), pltpu.SemaphoreType.DMA((n,))]
    return _Carried(list(small), out_shape, sems, start, finish)


def _join(*plans):
    plans = [p for p in plans if p is not None]
    if len(plans) <= 1:
        return plans[0] if plans else None

    def each(fn_name, ins, outs, sems):
        i = o = s = 0
        for p in plans:
            ni, no, ns = len(p.ins), len(p.out_shape), len(p.sems)
            getattr(p, fn_name)(ins[i:i + ni], outs[o:o + no], sems[s:s + ns])
            i, o, s = i + ni, o + no, s + ns

    return _Carried(sum((p.ins for p in plans), []), sum((p.out_shape for p in plans), []),
                    sum((p.sems for p in plans), []),
                    lambda i, o, s: each("start", i, o, s), lambda i, o, s: each("finish", i, o, s),
                    lambda i, o, s: each("relay", i, o, s))


def _row_block(rows, most=256):
    return min(rows, most)


def _add_own(tag, core, gs, layer, gots):
    n = len(gs)

    def body(core_ref, *refs):
        for a_ref, b_ref, o_ref in zip(refs[:n], refs[n:2 * n], refs[2 * n:]):
            o_ref[...] = (a_ref[...] + b_ref[...]).astype(o_ref.dtype)

    mine = lambda a: pl.BlockSpec((None, None) + a.shape[1:], lambda s, core: (layer, 4 * core[0] + s, 0, 0))
    theirs = lambda a: pl.BlockSpec((None,) + a.shape[1:], lambda s, core: (s, 0, 0))
    return pl.pallas_call(
        body, name=f"add_{tag}", out_shape=tuple(SDS(a.shape, BF16) for a in gots),
        grid_spec=pltpu.PrefetchScalarGridSpec(
            num_scalar_prefetch=1, grid=(4,),
            in_specs=[mine(a) for a in gots] + [theirs(a) for a in gots],
            out_specs=tuple(theirs(a) for a in gots)),
        compiler_params=_cp(("parallel",)),
    )(core, *gs, *gots)


def _add_lists(tag, own, got, grid=None, specs=None, dtype=F32):
    n = len(own)

    def body(*refs):
        for a, b, o in zip(refs[:n], refs[n:2 * n], refs[2 * n:]):
            o[...] = (a[...] + b[...]).astype(o.dtype)

    kw = {}
    if grid is not None:
        kw = dict(grid=grid, in_specs=list(specs) * 2, out_specs=tuple(specs),
                  compiler_params=_cp(("parallel",) * len(grid)))
    return pl.pallas_call(
        body, name=f"add_{tag}", out_shape=tuple(SDS(a.shape, dtype) for a in own), **kw)(*own, *got)


def _adamw_math(w, g, m, v):
    m = ADAM_B1 * m + (1.0 - ADAM_B1) * g
    v = ADAM_B2 * v + (1.0 - ADAM_B2) * (g * g)
    m_hat = m / (1.0 - ADAM_B1 ** ADAM_STEP)
    v_hat = v / (1.0 - ADAM_B2 ** ADAM_STEP)
    delta = -ADAM_LR * (m_hat / (jnp.sqrt(v_hat) + ADAM_EPS) + ADAM_WD * w)
    return delta, m, v


def _sum_slots_adamw(tag, slots, w, m, v):
    _, r, c = slots[0].shape
    rb = _row_block(r, most=512)

    def body(s0_ref, s1_ref, w_ref, m_ref, v_ref, g_ref, d_ref, nm_ref, nv_ref):
        first = pl.program_id(1) == 0
        g = _pair_sum([jnp.where(first, s0_ref[k], s1_ref[k]).astype(F32) for k in range(N_CHIP)])
        delta, nm, nv = _adamw_math(w_ref[...], g, m_ref[...], v_ref[...])
        g_ref[...] = g
        d_ref[...] = delta
        nm_ref[...] = nm
        nv_ref[...] = nv

    spec = pl.BlockSpec((None, rb, c), lambda j, l: (l, j, 0))
    sspec = pl.BlockSpec((N_CHIP, rb, c), lambda j, l: (0, j, 0))
    s = SDS((DEPTH, r, c), F32)
    return pl.pallas_call(
        body, name=f"adamw_{tag}", out_shape=(s, s, s, s),
        grid=(r // rb, DEPTH), in_specs=[sspec, sspec, spec, spec, spec], out_specs=(spec, spec, spec, spec),
        compiler_params=_cp(("parallel", "arbitrary")),
    )(*slots, w, m, v)


def _adamw_small(tag, entries, grid=None, sums=(), carry=None):
    flat_in, in_specs, out_shape, out_specs, layout = [], [], [], [], []
    for slots, w, m, v, slot_spec, w_spec in entries:
        per_layer = isinstance(slots, (list, tuple))
        n_slot = len(slots) if per_layer else 1
        flat_in += (list(slots) if per_layer else [slots]) + [w, m, v]
        in_specs += [slot_spec] * n_slot + [w_spec] * 3
        out_shape += [SDS(w.shape, F32)] * 4
        out_specs += [w_spec] * 4
        layout.append((per_layer, n_slot))
    n_entry_in = len(flat_in)
    flat_in += list(sums)
    out_shape += [SDS(s.shape[1:], F32) for s in sums]
    n_in = len(flat_in)

    def body(*refs):
        for s_ref, o_ref in zip(refs[n_entry_in:n_in], refs[len(refs) - len(sums):]):
            o_ref[...] = _sum_slots(s_ref)
        i, o = 0, n_in
        for per_layer, n_slot in layout:
            s_refs = refs[i:i + n_slot]
            w_ref, m_ref, v_ref = refs[i + n_slot:i + n_slot + 3]
            outs = refs[o:o + 4]
            if per_layer:
                for l, s_ref in enumerate(s_refs):
                    at = (slice(l, l + 1),) if len(w_ref.shape) == 2 else (l,)
                    g = _sum_slots(s_ref)
                    res = (g,) + _adamw_math(w_ref[at], g, m_ref[at], v_ref[at])
                    for o_ref, val in zip(outs, res):
                        o_ref[at] = val
            else:
                g = _sum_slots(s_refs[0])
                res = (g,) + _adamw_math(w_ref[...], g, m_ref[...], v_ref[...])
                for o_ref, val in zip(outs, res):
                    o_ref[...] = val
            i += n_slot + 3
            o += 4

    moved = ()
    if grid is not None:
        res, moved = _pcall(body, name=f"adamw_{tag}", grid=grid, in_specs=in_specs, out_specs=out_specs,
                            out_shape=out_shape, args=flat_in, sem=("parallel",) * len(grid), carry=carry)
    else:
        res = pl.pallas_call(body, name=f"adamw_{tag}", out_shape=tuple(out_shape))(*flat_in)
    return [tuple(res[4 * e:4 * e + 4]) for e in range(len(entries))], res[4 * len(entries):], moved


def kernel(x, norm_g, w_in, b_in, ssm_log_dt, ssm_lam_re, ssm_lam_im, ssm_b_re, ssm_b_im, ssm_c_re, ssm_c_im, ssm_d, ssm_w_glu, ssm_b_glu, pool_w, pool_scale, w_branch_a, w_branch_b, w_out, final_norm_g, loss_target, m_norm_g, m_w_in, m_b_in, m_ssm_log_dt, m_ssm_lam_re, m_ssm_lam_im, m_ssm_b_re, m_ssm_b_im, m_ssm_c_re, m_ssm_c_im, m_ssm_d, m_ssm_w_glu, m_ssm_b_glu, m_pool_w, m_pool_scale, m_w_branch_a, m_w_branch_b, m_w_out, m_final_norm_g, v_norm_g, v_w_in, v_b_in, v_ssm_log_dt, v_ssm_lam_re, v_ssm_lam_im, v_ssm_b_re, v_ssm_b_im, v_ssm_c_re, v_ssm_c_im, v_ssm_d, v_ssm_w_glu, v_ssm_b_glu, v_pool_w, v_pool_scale, v_w_branch_a, v_w_branch_b, v_w_out, v_final_norm_g):
    weights = dict(norm_g=norm_g, w_in=w_in, b_in=b_in, ssm_log_dt=ssm_log_dt, ssm_lam_re=ssm_lam_re,
                   ssm_lam_im=ssm_lam_im, ssm_b_re=ssm_b_re, ssm_b_im=ssm_b_im, ssm_c_re=ssm_c_re,
                   ssm_c_im=ssm_c_im, ssm_d=ssm_d, ssm_w_glu=ssm_w_glu, ssm_b_glu=ssm_b_glu, pool_w=pool_w,
                   pool_scale=pool_scale, w_branch_a=w_branch_a, w_branch_b=w_branch_b, w_out=w_out,
                   final_norm_g=final_norm_g.reshape(1, D_MODEL))
    mom_m = dict(norm_g=m_norm_g, w_in=m_w_in, b_in=m_b_in, ssm_log_dt=m_ssm_log_dt, ssm_lam_re=m_ssm_lam_re,
                 ssm_lam_im=m_ssm_lam_im, ssm_b_re=m_ssm_b_re, ssm_b_im=m_ssm_b_im, ssm_c_re=m_ssm_c_re,
                 ssm_c_im=m_ssm_c_im, ssm_d=m_ssm_d, ssm_w_glu=m_ssm_w_glu, ssm_b_glu=m_ssm_b_glu,
                 pool_w=m_pool_w, pool_scale=m_pool_scale, w_branch_a=m_w_branch_a, w_branch_b=m_w_branch_b,
                 w_out=m_w_out, final_norm_g=m_final_norm_g.reshape(1, D_MODEL))
    mom_v = dict(norm_g=v_norm_g, w_in=v_w_in, b_in=v_b_in, ssm_log_dt=v_ssm_log_dt, ssm_lam_re=v_ssm_lam_re,
                 ssm_lam_im=v_ssm_lam_im, ssm_b_re=v_ssm_b_re, ssm_b_im=v_ssm_b_im, ssm_c_re=v_ssm_c_re,
                 ssm_c_im=v_ssm_c_im, ssm_d=v_ssm_d, ssm_w_glu=v_ssm_w_glu, ssm_b_glu=v_ssm_b_glu,
                 pool_w=v_pool_w, pool_scale=v_pool_scale, w_branch_a=v_w_branch_a, w_branch_b=v_w_branch_b,
                 w_out=v_w_out, final_norm_g=v_final_norm_g.reshape(1, D_MODEL))
    order = ["norm_g", "w_in", "b_in", "ssm_log_dt", "ssm_lam_re", "ssm_lam_im", "ssm_b_re", "ssm_b_im",
             "ssm_c_re", "ssm_c_im", "ssm_d", "ssm_w_glu", "ssm_b_glu", "pool_w", "pool_scale", "w_branch_a",
             "w_branch_b", "w_out", "final_norm_g"]
    big_names = ["w_in", "ssm_w_glu", "w_branch_a", "w_branch_b", "w_out"]

    log_dt3 = ssm_log_dt.reshape(DEPTH, N_GROUP, 1)
    b_t = lambda a: a.transpose(0, 1, 3, 2)
    for d in (weights, mom_m, mom_v):
        d["ssm_b_re"], d["ssm_b_im"] = b_t(d["ssm_b_re"]), b_t(d["ssm_b_im"])
    bt_re, bt_im = weights["ssm_b_re"], weights["ssm_b_im"]
    abar_re, abar_im, bbt_re, bbt_im = _s5_params(log_dt3, ssm_lam_re, ssm_lam_im, bt_re, bt_im)
    s5_args = (bbt_re, bbt_im, ssm_c_re, ssm_c_im, abar_re, abar_im, ssm_d)

    w16 = {n: weights[n].astype(BF16) for n in big_names}
    rest = [w16[n] for n in big_names[1:]]
    half = D_MODEL // 2
    wg_in = [None, [None, None]]
    wg_rest = [None, None]
    wg_in[0] = list(_run_carried("gather_w_in_l0", _gather_plan([w16["w_in"]], 0)))
    xs = [x.reshape(SEQ, D_MODEL)]
    saved = []
    for l in range(DEPTH):
        proj, moved = _norm_proj(l, xs[l], norm_g, wg_in[l], b_in,
                                 carry=_gather_plan([w16["w_in"]], 1, rows_of=(0, half)) if l == 0 else None)
        if l == 0:
            (wg_in[1][0],) = moved
        (states, y0), wg_rest[l] = _s5_scan_fwd(l, proj, *s5_args, carry=_gather_plan(rest, l, by_columns=(1, 2)))
        pooled = _pool_fwd(l, proj)
        wg_glu, wg_a, wg_b, wg_out = wg_rest[l]
        x_next, moved = _mix_fwd(l, xs[l], proj, y0, pooled, wg_glu, ssm_b_glu, pool_w, pool_scale, wg_a, wg_b,
                                 wg_out, carry=_gather_plan([w16["w_in"]], 1, rows_of=(half, half)) if l == 0 else None)
        if l == 0:
            (wg_in[1][1],) = moved
        xs.append(x_next)
        saved.append((proj, states, y0, pooled))

    dx, loss_part, g_final = _loss_head(xs[DEPTH], loss_target.reshape(SEQ, D_MODEL), weights["final_norm_g"])

    core = lax.axis_index("c").astype(jnp.int32).reshape(1)
    vec_names = ["norm_g", "b_in", "ssm_d", "ssm_b_glu", "pool_scale", "ssm_log_dt"]
    s5_names = ["ssm_log_dt", "ssm_lam_re", "ssm_lam_im", "ssm_b_re", "ssm_b_im"]
    mat_names = ["pool_w", "ssm_c_re", "ssm_c_im", "ssm_b_re", "ssm_b_im"]
    lane_sparse = ("ssm_c_re", "ssm_c_im", "ssm_b_re", "ssm_b_im")

    def dense(key, a):
        return a.reshape(-1, LANES) if key[0] in lane_sparse else a

    def undense(key, slots):
        return slots.reshape((N_CHIP, N_GROUP, GROUP_W, STATE)) if key[0] in lane_sparse else slots

    def add_small(tag, keys, own, got):
        out = [None] * len(keys)
        whole = [i for i, k in enumerate(keys) if k[0] not in mat_names]
        tiled = [i for i, k in enumerate(keys) if k[0] in mat_names]
        if whole:
            for i, r in zip(whole, _add_lists(f"{tag}_a", [own[i] for i in whole], [got[i] for i in whole])):
                out[i] = r
        if tiled:
            specs = [pl.BlockSpec((1, POOL_GROUP, POOL_GROUP), lambda j: (j, 0, 0)) if keys[i][0] == "pool_w"
                     else pl.BlockSpec((own[i].shape[0] // N_CHUNK, LANES), lambda j: (j, 0)) for i in tiled]
            for i, r in zip(tiled, _add_lists(f"{tag}_b", [own[i] for i in tiled], [got[i] for i in tiled],
                                              grid=(N_CHUNK,), specs=specs, dtype=BF16)):
                out[i] = r
        return out

    sm = {("final_norm_g", None): g_final, ("loss", None): loss_part}
    slots = {}
    grads = dict.fromkeys(big_names)

    class Wave:
        def __init__(self, tag, layer, big, keys):
            self.tag, self.layer, self.big, self.keys = tag, layer, big, keys

        def to_sibling(self):
            self.own = [dense(k, sm[k]) for k in self.keys]
            return _sibling_plan([(grads[n], self.layer) for n in self.big], self.own)

        def add(self, moved):
            nb = len(self.big)
            self.chip_big = list(_add_own(self.tag, core, [grads[n] for n in self.big], self.layer, moved[:nb])
                                 ) if nb else []
            self.chip_small = add_small(self.tag, self.keys, self.own, moved[nb:])

        def to_chips(self, big=None, small=True):
            self.sent = list(self.big if big is None else big), small
            return _chips_plan([self.chip_big[self.big.index(n)] for n in self.sent[0]],
                               self.chip_small if small else [])

        def landed(self, moved):
            names, small = self.sent
            for n, s in zip(names, moved[:len(names)]):
                slots[(n, self.layer)] = s
            if small:
                for k, s in zip(self.keys, moved[len(names):]):
                    slots[k] = undense(k, s)
            return moved[len(names) + (len(self.keys) if small else 0):]

    def s5_param_grads(l, g_abar_re, g_abar_im, g_bbt_re, g_bbt_im):
        g = _s5_params_bwd(l, log_dt3, ssm_lam_re, ssm_lam_im, bt_re, bt_im, g_abar_re, g_abar_im, g_bbt_re, g_bbt_im)
        sm[("ssm_log_dt", l)] = g[0].reshape(1, N_GROUP)
        for n, a in zip(s5_names[1:], g[1:]):
            sm[(n, l)] = a

    small1 = ["b_in", "ssm_d", "ssm_b_glu", "pool_scale", "pool_w", "ssm_c_re", "ssm_c_im"] + s5_names
    w1 = Wave("chip1", 1, list(big_names), [(n, 1) for n in small1] + [("final_norm_g", None), ("loss", None)])
    early = Wave("chip0e", 0, big_names[1:], [("pool_w", 0), ("pool_scale", 0), ("ssm_b_glu", 0)])
    mid = Wave("chip0m", 0, [], [(n, 0) for n in ["ssm_c_re", "ssm_c_im", "ssm_d"] + s5_names] + [("norm_g", 1)])
    late = Wave("chip0l", 0, ["w_in"], [("b_in", 0)])

    mix_prev, gw_in = None, None
    for l in reversed(range(DEPTH)):
        proj, states, y0, pooled = saved[l]
        wg_glu, wg_a, wg_b, wg_out = wg_rest[l]
        res, moved = _mix_bwd(l, dx, proj, y0, pooled, wg_glu, ssm_b_glu, pool_w, pool_scale, wg_a, wg_b, wg_out,
                              mix_prev, carry=None if l == 1 else w1.to_chips(big=["w_in"], small=False))
        if l == 0:
            w1.landed(moved)
        dproj, dy0, dpooled = res[:3]
        mix_prev = list(res[3:7])
        grads["w_out"], grads["w_branch_a"], grads["w_branch_b"], grads["ssm_w_glu"] = mix_prev
        sm[("pool_w", l)], sm[("pool_scale", l)], sm[("ssm_b_glu", l)] = res[7:]
        dproj = _pool_bwd(l, dpooled, dproj)
        carry = None if l == 1 else _join(w1.to_chips(big=big_names[1:]), early.to_sibling())
        res, moved = _s5_scan_bwd(l, dy0, proj, states, *s5_args, dproj, carry=carry)
        if l == 0:
            early.add(w1.landed(moved))
        dproj, g_bbt_re, g_bbt_im, sm[("ssm_c_re", l)], sm[("ssm_c_im", l)], g_abar_re, g_abar_im, sm[("ssm_d", l)] = res
        s5_param_grads(l, g_abar_re, g_abar_im, g_bbt_re, g_bbt_im)
        carry = None if l == 1 else _join(early.to_chips(), mid.to_sibling())
        (gw_in, sm[("b_in", l)]), moved = _proj_wgrad(l, xs[l], norm_g, dproj, gw_in, carry=carry)
        grads["w_in"] = gw_in
        if l == 0:
            mid.add(early.landed(moved))
        carry = w1.to_sibling() if l == 1 else _join(mid.to_chips(), late.to_sibling())
        (dx, sm[("norm_g", l)]), moved = _proj_dgrad(l, dx, xs[l], norm_g, dproj, wg_in[l], carry=carry)
        if l == 1:
            w1.add(moved)
        else:
            late.add(mid.landed(moved))
    grad_x = dx.reshape(1, SEQ, D_MODEL)

    res = {}
    per_layer = lambda n: [slots[(n, l)] for l in range(DEPTH)]
    pw_s = pl.BlockSpec((N_CHIP, 1, POOL_GROUP, POOL_GROUP), lambda j: (0, j, 0, 0))
    pw_w = pl.BlockSpec((DEPTH, 1, POOL_GROUP, POOL_GROUP), lambda j: (0, j, 0, 0))
    c_s = pl.BlockSpec((N_CHIP, CH_G, GROUP_W, STATE), lambda j: (0, j, 0, 0))
    c_w = pl.BlockSpec((DEPTH, CH_G, GROUP_W, STATE), lambda j: (0, j, 0, 0))
    entries_b = [(per_layer(n), weights[n], mom_m[n], mom_v[n], pw_s if n == "pool_w" else c_s,
                  pw_w if n == "pool_w" else c_w) for n in mat_names]
    out_b, _, moved = _adamw_small("small_b", entries_b, grid=(N_CHUNK,),
                                   carry=_join(late.to_chips(), _all_plan([sm[("norm_g", 0)]])))
    for n, r in zip(mat_names, out_b):
        res[n] = tuple(b_t(a) for a in r) if n in ("ssm_b_re", "ssm_b_im") else r
    (slots[("norm_g", 0)],) = late.landed(moved)

    for n in big_names:
        res[n] = _sum_slots_adamw(n, [slots[(n, l)] for l in range(DEPTH)], weights[n], mom_m[n], mom_v[n])
    names_a = vec_names + ["ssm_lam_re", "ssm_lam_im"]
    entries_a = [(per_layer(n), weights[n], mom_m[n], mom_v[n], None, None) for n in names_a]
    n = "final_norm_g"
    entries_a.append((slots[(n, None)], weights[n], mom_m[n], mom_v[n], None, None))
    out_a, (loss,), _ = _adamw_small("small_a", entries_a, sums=[slots[("loss", None)]])
    loss = loss.reshape(())
    for n, r in zip(names_a + ["final_norm_g"], out_a):
        res[n] = r
    res["final_norm_g"] = tuple(a.reshape(D_MODEL) for a in res["final_norm_g"])

    outs = [loss, grad_x]
    for i in range(4):
        outs += [res[n][i] for n in order]
    return tuple(outs)
```

```python
import math

import jax
import jax.numpy as jnp
from jax import lax
from jax.experimental import pallas as pl
from jax.experimental.pallas import tpu as pltpu

F32 = jnp.float32
BF16 = jnp.bfloat16

SEQ = 2048
D_MODEL = 1024
N_IN = 4096
WIDTH = 512
N_GROUP = 32
GROUP_W = 16
STATE = 64
N_STATE = N_GROUP * STATE
N_CHUNK = 4
CH_G = N_GROUP // N_CHUNK
CH_W = WIDTH // N_CHUNK
CH_S = N_STATE // N_CHUNK
N_DEV = 8
N_CHIP = 4
POOL_WINDOWS = (2, 4, 8, 16)
POOL_GROUP = 128
EPS = 1e-6
DEPTH = 2

ADAM_LR = 0.001
ADAM_B1 = 0.9
ADAM_B2 = 0.999
ADAM_EPS = 1e-08
ADAM_WD = 0.01
ADAM_STEP = 10

LANES = 128
SUBLANES = 8
TILE_M = 256
VMEM_LIMIT = 48 * 1024 * 1024
VMEM_LIMIT_BIG = 60 * 1024 * 1024
MESH = pl.DeviceIdType.MESH
ANY = pl.BlockSpec(memory_space=pl.ANY)

GELU_C = math.sqrt(2.0 / math.pi)
GELU_A = 0.044715

SDS = jax.ShapeDtypeStruct


def _cp(sem=None, limit=VMEM_LIMIT):
    return pltpu.CompilerParams(dimension_semantics=sem, vmem_limit_bytes=limit)


def _dot(a, b):
    return jnp.dot(a, b, preferred_element_type=F32)


def _dot_nt(a, b):
    return lax.dot_general(a, b, (((1,), (1,)), ((), ())), preferred_element_type=F32)


def _dot_tn(a, b):
    return lax.dot_general(a, b, (((0,), (0,)), ((), ())), preferred_element_type=F32)


def _sig(x):
    return jax.nn.sigmoid(x)


def _rms(x):
    rs = lax.rsqrt(jnp.mean(x * x, axis=-1, keepdims=True) + EPS)
    return rs, x * rs


def _slot(n):
    return 4 * (n % 2) + n // 2


def _const(shape):
    n = len(shape)
    return pl.BlockSpec(shape, lambda *_: (0,) * n)


def _pair_sum(vals):
    while len(vals) > 1:
        vals = [vals[i] + vals[i + 1] for i in range(0, len(vals), 2)]
    return vals[0]


def _sum_slots(s_ref):
    return _pair_sum([s_ref[k].astype(F32) for k in range(s_ref.shape[0])])


def _s5_param_fn(log_dt, lam_re, lam_im, bt_re, bt_im):
    dt = jnp.exp(log_dt)
    mag = jnp.exp(lam_re * dt)
    ang = lam_im * dt
    abar_re = mag * jnp.cos(ang)
    abar_im = mag * jnp.sin(ang)
    num_re = abar_re - 1.0
    num_im = abar_im
    den = lam_re * lam_re + lam_im * lam_im
    coef_re = (num_re * lam_re + num_im * lam_im) / den
    coef_im = (num_im * lam_re - num_re * lam_im) / den
    bbar_re = coef_re[..., None, :] * bt_re - coef_im[..., None, :] * bt_im
    bbar_im = coef_re[..., None, :] * bt_im + coef_im[..., None, :] * bt_re
    return abar_re, abar_im, bbar_re, bbar_im


def _s5_params(log_dt, lam_re, lam_im, bt_re, bt_im):
    def body(ld, lr, li, br, bi, o_ar, o_ai, o_br, o_bi):
        ar, ai, bbr, bbi = _s5_param_fn(ld[...], lr[...], li[...], br[...], bi[...])
        o_ar[...] = ar
        o_ai[...] = ai
        o_br[...] = bbr
        o_bi[...] = bbi

    return pl.pallas_call(
        body, name="s5_params",
        out_shape=(SDS(lam_re.shape, F32), SDS(lam_re.shape, F32), SDS(bt_re.shape, F32), SDS(bt_re.shape, F32)),
    )(log_dt, lam_re, lam_im, bt_re, bt_im)


def _s5_params_bwd(layer, log_dt, lam_re, lam_im, bt_re, bt_im, g_ar, g_ai, g_br, g_bi):
    def body(ld, lr, li, br, bi, car, cai, cbr, cbi, o_ld, o_lr, o_li, o_br, o_bi):
        _, vjp = jax.vjp(_s5_param_fn, ld[...], lr[...], li[...], br[...], bi[...])
        d_ld, d_lr, d_li, d_br, d_bi = vjp((car[...], cai[...], cbr[...], cbi[...]))
        o_ld[...] = d_ld
        o_lr[...] = d_lr
        o_li[...] = d_li
        o_br[...] = d_br
        o_bi[...] = d_bi

    one = lambda shape: pl.BlockSpec((None,) + shape, lambda i: (layer,) + (0,) * len(shape))
    whole = lambda shape: _const(shape)
    vec, lam, mat = (N_GROUP, 1), (N_GROUP, STATE), (N_GROUP, GROUP_W, STATE)
    return pl.pallas_call(
        body, name=f"s5_params_bwd_l{layer}", grid=(1,),
        in_specs=[one(vec), one(lam), one(lam), one(mat), one(mat), whole(lam), whole(lam), whole(mat), whole(mat)],
        out_specs=(whole(vec), whole(lam), whole(lam), whole(mat), whole(mat)),
        out_shape=(SDS(vec, F32), SDS(lam, F32), SDS(lam, F32), SDS(mat, F32), SDS(mat, F32)),
    )(log_dt, lam_re, lam_im, bt_re, bt_im, g_ar, g_ai, g_br, g_bi)


def _norm_proj(layer, x, norm_g, wg_in, b_in, carry=None):
    n_w = len(wg_in)

    def body(x_ref, g_ref, b_ref, *refs):
        w_refs, o_ref = refs[:n_w], refs[n_w]
        _, xn = _rms(x_ref[...])
        h = (xn * g_ref[layer:layer + 1, :]).astype(BF16)
        for k in range(N_DEV):
            cols = slice(k * WIDTH, (k + 1) * WIDTH)
            acc = b_ref[layer:layer + 1, cols]
            row = 0
            for w_ref in w_refs:
                rows = w_ref.shape[1]
                acc = acc + _dot(h[:, row:row + rows], w_ref[k])
                row += rows
            o_ref[:, cols] = acc

    (proj,), moved = _pcall(
        body, name=f"norm_proj_l{layer}",
        out_shape=[SDS((SEQ, N_IN), F32)],
        grid=(SEQ // TILE_M,),
        in_specs=[pl.BlockSpec((TILE_M, D_MODEL), lambda i: (i, 0)),
                  _const((DEPTH, D_MODEL)),
                  _const((DEPTH, N_IN))] + [_const(w.shape) for w in wg_in],
        out_specs=[pl.BlockSpec((TILE_M, N_IN), lambda i: (i, 0))],
        args=[x, norm_g, b_in, *wg_in], sem=("parallel",), carry=carry)
    return proj, moved


TIME_BLK = 512
N_TBLK = SEQ // TIME_BLK
N_PANEL = CH_S // LANES
STATE_SHAPE = (N_PANEL, SEQ * SUBLANES, LANES)


def _s5_layer_specs(layer):
    mat = lambda: pl.BlockSpec((None, N_GROUP, GROUP_W, STATE), lambda i: (layer, 0, 0, 0))
    ab = lambda: pl.BlockSpec((None, N_GROUP, STATE), lambda i: (layer, 0, 0))
    return [mat(), mat(), mat(), mat(), ab(), ab(), _const((DEPTH, WIDTH))]


def _s5_layer_scratch():
    return [pltpu.VMEM((N_CHUNK, CH_W, CH_S), BF16)] * 4 + [pltpu.VMEM((8, CH_S), F32)] * 2


def _s5_layer_fill(btre_ref, btim_ref, cre_ref, cim_ref, are_ref, aim_ref, bdre, bdim, ctre, ctim, a1, a2):
    for m in (bdre, bdim, ctre, ctim):
        m[...] = jnp.zeros_like(m)
    for grp in range(N_GROUP):
        k, g = divmod(grp, CH_G)
        rows = slice(g * GROUP_W, (g + 1) * GROUP_W)
        cols = slice(g * STATE, (g + 1) * STATE)
        bdre[k, rows, cols] = btre_ref[grp].astype(BF16)
        bdim[k, rows, cols] = btim_ref[grp].astype(BF16)
        ctre[k, rows, cols] = cre_ref[grp].astype(BF16)
        ctim[k, rows, cols] = cim_ref[grp].astype(BF16)
        ar = are_ref[grp:grp + 1, :]
        ai = aim_ref[grp:grp + 1, :]
        a1[k:k + 1, cols] = ar
        a1[N_CHUNK + k:N_CHUNK + k + 1, cols] = ar
        a2[k:k + 1, cols] = -ai
        a2[N_CHUNK + k:N_CHUNK + k + 1, cols] = ai


SCAN_UNROLL = 16


def _panels(tile):
    return [tile[:, p * LANES:(p + 1) * LANES] for p in range(N_PANEL)]


def _rows_load(ref, row):
    return jnp.concatenate([ref[p, pl.ds(row, TIME_BLK, stride=SUBLANES), :] for p in range(N_PANEL)], axis=1)


def _rows_store(ref, row, val):
    for p in range(N_PANEL):
        ref[p, pl.ds(row, TIME_BLK, stride=SUBLANES), :] = val[:, p * LANES:(p + 1) * LANES]


def _s5_scan_fwd(layer, proj, bbt_re, bbt_im, c_re, c_im, abar_re, abar_im, d_skip, carry=None):
    def body(u_ref, btre_ref, btim_ref, cre_ref, cim_ref, are_ref, aim_ref, d_ref, s_ref, y_ref,
             bdre, bdim, ctre, ctim, a1, a2, state):
        @pl.when(pl.program_id(0) == 0)
        def _():
            _s5_layer_fill(btre_ref, btim_ref, cre_ref, cim_ref, are_ref, aim_ref, bdre, bdim, ctre, ctim, a1, a2)
            state[...] = jnp.zeros_like(state)

        for k in range(N_CHUNK):
            ub = u_ref[:, k * CH_W:(k + 1) * CH_W].astype(BF16)
            _rows_store(s_ref, k, _dot(ub, bdre[k]))
            _rows_store(s_ref, N_CHUNK + k, _dot(ub, bdim[k]))
        m1 = _panels(a1[...])
        m2 = _panels(a2[...])

        def steps(n, tile):
            for r in range(SCAN_UNROLL):
                rows = pl.ds(pl.multiple_of((n * SCAN_UNROLL + r) * 8, 8), 8)
                tile = [m1[p] * tile[p] + m2[p] * pltpu.roll(tile[p], N_CHUNK, 0) + s_ref[p, rows, :]
                        for p in range(N_PANEL)]
                for p in range(N_PANEL):
                    s_ref[p, rows, :] = tile[p]
            return tile

        tile = lax.fori_loop(0, TIME_BLK // SCAN_UNROLL, steps, _panels(state[...]))
        state[...] = jnp.concatenate(tile, axis=1)
        d = d_ref[layer:layer + 1, :]
        for k in range(N_CHUNK):
            cols = slice(k * CH_W, (k + 1) * CH_W)
            y = (_dot_nt(_rows_load(s_ref, k).astype(BF16), ctre[k])
                 - _dot_nt(_rows_load(s_ref, N_CHUNK + k).astype(BF16), ctim[k]))
            y_ref[:, cols] = y + d[:, cols] * u_ref[:, cols]

    return _pcall(
        body, name=f"s5_fwd_l{layer}",
        out_shape=(SDS(STATE_SHAPE, F32), SDS((SEQ, WIDTH), F32)),
        grid=(N_TBLK,),
        in_specs=[pl.BlockSpec((TIME_BLK, WIDTH), lambda i: (i, 0))] + _s5_layer_specs(layer),
        out_specs=(pl.BlockSpec((N_PANEL, TIME_BLK * SUBLANES, LANES), lambda i: (0, i, 0)),
                   pl.BlockSpec((TIME_BLK, WIDTH), lambda i: (i, 0))),
        scratch_shapes=_s5_layer_scratch() + [pltpu.VMEM((8, CH_S), F32)],
        args=[proj, bbt_re, bbt_im, c_re, c_im, abar_re, abar_im, d_skip], sem=("arbitrary",), carry=carry)


def _s5_scan_bwd(layer, dy0, proj, states, bbt_re, bbt_im, c_re, c_im, abar_re, abar_im, d_skip, dproj,
                 carry=None):
    def body(dy_ref, u_ref, s_ref, sprev_ref, btre_ref, btim_ref, cre_ref, cim_ref, are_ref, aim_ref, d_ref, _,
             du_ref, gbre_ref, gbim_ref, gcre_ref, gcim_ref, gare_ref, gaim_ref, gd_ref,
             lam_ref, bdre, bdim, ctre, ctim, a1, a2, state, acc1, acc2, gbre, gbim, gcre, gcim, gd):
        step_id = pl.program_id(0)

        @pl.when(step_id == 0)
        def _():
            _s5_layer_fill(btre_ref, btim_ref, cre_ref, cim_ref, are_ref, aim_ref, bdre, bdim, ctre, ctim, a1, a2)
            for r in (state, acc1, acc2, gbre, gbim, gcre, gcim, gd):
                r[...] = jnp.zeros_like(r)

        for k in range(N_CHUNK):
            dyb = dy_ref[:, k * CH_W:(k + 1) * CH_W].astype(BF16)
            _rows_store(lam_ref, k, _dot(dyb, ctre[k]))
            _rows_store(lam_ref, N_CHUNK + k, -_dot(dyb, ctim[k]))
            gcre[k] += _dot_tn(dyb, _rows_load(s_ref, k).astype(BF16))
            gcim[k] -= _dot_tn(dyb, _rows_load(s_ref, N_CHUNK + k).astype(BF16))

        m1 = _panels(a1[...])
        m2 = _panels(-a2[...])
        has_before = (step_id < N_TBLK - 1).astype(F32)

        def one(t8, c, first_token):
            tile, swapped, p1, p2 = c
            rows = pl.ds(t8, 8)
            tile = [m1[p] * tile[p] + m2[p] * swapped[p] + lam_ref[p, rows, :] for p in range(N_PANEL)]
            swapped = [pltpu.roll(tile[p], N_CHUNK, 0) for p in range(N_PANEL)]
            for p in range(N_PANEL):
                lam_ref[p, rows, :] = tile[p]
            if first_token:
                before = [sprev_ref[p] * has_before for p in range(N_PANEL)]
            else:
                before = [s_ref[p, pl.ds(t8 - 8, 8), :] for p in range(N_PANEL)]
            p1 = [p1[p] + tile[p] * before[p] for p in range(N_PANEL)]
            p2 = [p2[p] + swapped[p] * before[p] for p in range(N_PANEL)]
            return tile, swapped, p1, p2

        def steps(n, c):
            for r in range(SCAN_UNROLL):
                t8 = pl.multiple_of((TIME_BLK - 1 - (n * SCAN_UNROLL + r)) * 8, 8)
                c = one(t8, c, False)
            return c

        tile0 = _panels(state[...])
        c = (tile0, [pltpu.roll(t, N_CHUNK, 0) for t in tile0], _panels(acc1[...]), _panels(acc2[...]))
        c = lax.fori_loop(0, TIME_BLK // SCAN_UNROLL - 1, steps, c)
        for r in range(SCAN_UNROLL - 1, -1, -1):
            c = one(r * 8, c, r == 0)
        state[...] = jnp.concatenate(c[0], axis=1)
        acc1[...] = jnp.concatenate(c[2], axis=1)
        acc2[...] = jnp.concatenate(c[3], axis=1)

        d = d_ref[layer:layer + 1, :]
        for k in range(N_CHUNK):
            cols = slice(k * CH_W, (k + 1) * CH_W)
            lrb = _rows_load(lam_ref, k).astype(BF16)
            lib = _rows_load(lam_ref, N_CHUNK + k).astype(BF16)
            u = u_ref[:, cols]
            ub = u.astype(BF16)
            dy = dy_ref[:, cols]
            du = dy * d[:, cols] + _dot_nt(lrb, bdre[k]) + _dot_nt(lib, bdim[k])
            du_ref[:, cols] = du.astype(BF16)
            gbre[k] += _dot_tn(ub, lrb)
            gbim[k] += _dot_tn(ub, lib)
        gd[...] += jnp.sum(dy_ref[...] * u_ref[...], axis=0, keepdims=True)

        @pl.when(step_id == N_TBLK - 1)
        def _():
            gd_ref[...] = gd[...]
            ga_re = acc1[0:N_CHUNK, :] + acc1[N_CHUNK:, :]
            ga_im = acc2[0:N_CHUNK, :] - acc2[N_CHUNK:, :]
            for grp in range(N_GROUP):
                k, g = divmod(grp, CH_G)
                rows = slice(g * GROUP_W, (g + 1) * GROUP_W)
                cols = slice(g * STATE, (g + 1) * STATE)
                gcre_ref[grp] = gcre[k, rows, cols]
                gcim_ref[grp] = gcim[k, rows, cols]
                gbre_ref[grp] = gbre[k, rows, cols]
                gbim_ref[grp] = gbim[k, rows, cols]
                gare_ref[grp:grp + 1, :] = ga_re[k:k + 1, cols]
                gaim_ref[grp:grp + 1, :] = ga_im[k:k + 1, cols]

    back = lambda i: N_TBLK - 1 - i
    tok = lambda: pl.BlockSpec((TIME_BLK, WIDTH), lambda i: (back(i), 0))
    mat = lambda: _const((N_GROUP, GROUP_W, STATE))
    acc_mat = pltpu.VMEM((N_CHUNK, CH_W, CH_S), F32)
    return _pcall(
        body, name=f"s5_bwd_l{layer}",
        out_shape=(SDS((SEQ, N_IN), BF16), SDS((N_GROUP, GROUP_W, STATE), F32), SDS((N_GROUP, GROUP_W, STATE), F32),
                   SDS((N_GROUP, GROUP_W, STATE), F32), SDS((N_GROUP, GROUP_W, STATE), F32),
                   SDS((N_GROUP, STATE), F32), SDS((N_GROUP, STATE), F32), SDS((1, WIDTH), F32)),
        grid=(N_TBLK,),
        in_specs=[tok(), tok(),
                  pl.BlockSpec((N_PANEL, TIME_BLK * SUBLANES, LANES), lambda i: (0, back(i), 0)),
                  pl.BlockSpec((N_PANEL, SUBLANES, LANES), lambda i: (0, jnp.maximum(back(i) * TIME_BLK - 1, 0), 0))]
        + _s5_layer_specs(layer) + [ANY],
        out_specs=(tok(), mat(), mat(), mat(), mat(), _const((N_GROUP, STATE)), _const((N_GROUP, STATE)),
                   _const((1, WIDTH))),
        scratch_shapes=[pltpu.VMEM((N_PANEL, TIME_BLK * SUBLANES, LANES), F32)] + _s5_layer_scratch()
        + [pltpu.VMEM((8, CH_S), F32)] * 3 + [acc_mat] * 4 + [pltpu.VMEM((1, WIDTH), F32)],
        args=[dy0, proj, states, states, bbt_re, bbt_im, c_re, c_im, abar_re, abar_im, d_skip, dproj],
        aliases={11: 0}, sem=("arbitrary",), limit=VMEM_LIMIT_BIG, carry=carry)


def _pool_counts(win):
    t = lax.broadcasted_iota(jnp.int32, (SEQ, POOL_GROUP), 0)
    return t, jnp.minimum(t + 1, win).astype(F32)


def _pool_fwd(layer, proj):
    def body(u_ref, o_ref):
        for gi, win in enumerate(POOL_WINDOWS):
            cols = slice(gi * POOL_GROUP, (gi + 1) * POOL_GROUP)
            u = u_ref[:, cols]
            t, count = _pool_counts(win)
            acc = u
            k = 1
            while k < win:
                acc = acc + jnp.where(t >= k, pltpu.roll(acc, k, 0), 0.0)
                k *= 2
            o_ref[:, cols] = acc / count - u

    return pl.pallas_call(
        body, name=f"pool_fwd_l{layer}",
        out_shape=SDS((SEQ, WIDTH), F32),
        grid=(1,),
        in_specs=[pl.BlockSpec((SEQ, WIDTH), lambda i: (0, 2))],
        out_specs=pl.BlockSpec((SEQ, WIDTH), lambda i: (0, 0)),
        compiler_params=_cp(("arbitrary",)),
    )(proj)


def _gelu_parts(y0):
    t = jnp.tanh(GELU_C * (y0 + GELU_A * (y0 * y0 * y0)))
    return t, 0.5 * y0 * (1.0 + t)


def _mix_forward(layer, p_ref, y0_ref, pooled_ref, wglu_ref, bglu_ref, pw_ref, scale_ref, wa_ref, wb_ref,
                 saved=None):
    za = p_ref[:, WIDTH:2 * WIDTH]
    zb = p_ref[:, 3 * WIDTH:4 * WIDTH]
    ga = p_ref[:, 4 * WIDTH:4 * WIDTH + D_MODEL]
    gb = p_ref[:, 4 * WIDTH + D_MODEL:]
    y0 = y0_ref[...]
    t, y1 = _gelu_parts(y0)
    y1b = y1.astype(BF16)
    q = _dot(y1b, wglu_ref[...].reshape(WIDTH, WIDTH)) + bglu_ref[layer:layer + 1, :]
    sq = _sig(q)
    y2 = y1 * sq
    sza = _sig(za)
    silu_za = za * sza
    ya = y2 * silu_za
    pooled = pooled_ref[...]
    mixed = jnp.concatenate(
        [_dot(pooled[:, g * POOL_GROUP:(g + 1) * POOL_GROUP].astype(BF16), pw_ref[g].astype(BF16))
         for g in range(len(POOL_WINDOWS))], axis=1)
    szb = _sig(zb)
    silu_zb = zb * szb
    scale = scale_ref[layer:layer + 1, :]
    ms = mixed * scale
    yb = ms * silu_zb
    yab = ya.astype(BF16)
    ybb = yb.astype(BF16)
    if saved is None:
        ma = _dot(yab, wa_ref[...])
        mb = _dot(ybb, wb_ref[...])
    else:
        ma = saved[0][...].astype(F32)
        mb = saved[1][...].astype(F32)
    sga = _sig(ga)
    sgb = _sig(gb)
    merged = sga * ma + sgb * mb
    return dict(za=za, zb=zb, y0=y0, t=t, y1=y1, y1b=y1b, sq=sq, y2=y2, sza=sza, silu_za=silu_za,
                pooled=pooled, mixed=mixed, szb=szb, silu_zb=silu_zb, scale=scale, ms=ms, yab=yab, ybb=ybb,
                ma=ma, mb=mb, sga=sga, sgb=sgb, merged=merged)


def _mix_weight_specs(layer):
    return [_const((N_DEV, WIDTH // N_DEV, WIDTH)),
            _const((DEPTH, WIDTH)),
            pl.BlockSpec((None, 4, POOL_GROUP, POOL_GROUP), lambda i: (layer, 0, 0, 0)),
            _const((DEPTH, WIDTH)),
            _const((WIDTH, D_MODEL)),
            _const((WIDTH, D_MODEL)),
            _const((N_DEV, D_MODEL // N_DEV, D_MODEL))]


def _mix_fwd(layer, x, proj, y0, pooled, wg_glu, b_glu, pool_w, pool_scale, wg_a, wg_b, wg_out, carry=None):
    def body(x_ref, p_ref, y0_ref, pooled_ref, wglu_ref, bglu_ref, pw_ref, scale_ref, wa_ref, wb_ref,
             wout_ref, o_ref, ma_ref, mb_ref):
        f = _mix_forward(layer, p_ref, y0_ref, pooled_ref, wglu_ref, bglu_ref, pw_ref, scale_ref, wa_ref, wb_ref)
        wout = wout_ref[...].reshape(D_MODEL, D_MODEL)
        o_ref[...] = x_ref[...] + _dot(f["merged"].astype(BF16), wout)
        ma_ref[...] = f["ma"].astype(BF16)
        mb_ref[...] = f["mb"].astype(BF16)

    tile = lambda: pl.BlockSpec((TILE_M, D_MODEL), lambda i: (i, 0))
    (x_next, ma, mb), moved = _pcall(
        body, name=f"mix_fwd_l{layer}",
        out_shape=[SDS((SEQ, D_MODEL), F32), SDS((SEQ, D_MODEL), BF16), SDS((SEQ, D_MODEL), BF16)],
        grid=(SEQ // TILE_M,),
        in_specs=[pl.BlockSpec((TILE_M, D_MODEL), lambda i: (i, 0)),
                  pl.BlockSpec((TILE_M, N_IN), lambda i: (i, 0)),
                  pl.BlockSpec((TILE_M, WIDTH), lambda i: (i, 0)),
                  pl.BlockSpec((TILE_M, WIDTH), lambda i: (i, 0))] + _mix_weight_specs(layer),
        out_specs=[tile(), tile(), tile()],
        args=[x, proj, y0, pooled, wg_glu, b_glu, pool_w, pool_scale, wg_a, wg_b, wg_out],
        sem=("parallel",), carry=carry)
    return (x_next, ma, mb), moved


def _loss_head(x, target, final_g):
    def body(x_ref, t_ref, g_ref, dx_ref, loss_ref, gg_ref):
        @pl.when(pl.program_id(0) == 0)
        def _():
            loss_ref[...] = jnp.zeros_like(loss_ref)
            gg_ref[...] = jnp.zeros_like(gg_ref)

        g = g_ref[...]
        rs, xn = _rms(x_ref[...])
        err = xn * g - t_ref[...]
        loss_ref[...] += 0.5 * jnp.sum(jnp.mean(err * err, axis=-1, keepdims=True), axis=0, keepdims=True)
        dy = err * (1.0 / D_MODEL)
        gg_ref[...] += jnp.sum(dy * xn, axis=0, keepdims=True)
        dxn = dy * g
        dx_ref[...] = rs * (dxn - xn * jnp.mean(dxn * xn, axis=-1, keepdims=True))

    return pl.pallas_call(
        body, name="loss_head",
        out_shape=(SDS((SEQ, D_MODEL), F32), SDS((1, 1), F32), SDS((1, D_MODEL), F32)),
        grid=(SEQ // TILE_M,),
        in_specs=[pl.BlockSpec((TILE_M, D_MODEL), lambda i: (i, 0)),
                  pl.BlockSpec((TILE_M, D_MODEL), lambda i: (i, 0)),
                  _const((1, D_MODEL))],
        out_specs=(pl.BlockSpec((TILE_M, D_MODEL), lambda i: (i, 0)), _const((1, 1)), _const((1, D_MODEL))),
        compiler_params=_cp(("arbitrary",)),
    )(x, target, final_g)


def _big_shapes():
    return dict(w_out=(DEPTH, N_DEV, D_MODEL // N_DEV, D_MODEL), w_branch_a=(DEPTH, N_DEV, WIDTH, D_MODEL // N_DEV),
                w_branch_b=(DEPTH, N_DEV, WIDTH, D_MODEL // N_DEV), ssm_w_glu=(DEPTH, N_DEV, WIDTH // N_DEV, WIDTH),
                w_in=(DEPTH, N_DEV, D_MODEL, WIDTH))


def _mix_bwd(layer, dx_next, proj, y0, pooled, ma, mb, wg_glu, b_glu, pool_w, pool_scale, wg_a, wg_b, wg_out, prev,
             carry=None):
    n_k = N_DEV
    n_prev = 0 if prev is None else len(prev)

    def body(*refs):
        (dx_ref, p_ref, y0_ref, pooled_ref, ma_ref, mb_ref, wglu_ref, bglu_ref, pw_ref, scale_ref, wa_ref, wb_ref,
         wout_ref) = refs[:13]
        (dproj_ref, dy0_ref, dpooled_ref, gwout_ref, gwa_ref, gwb_ref, gwglu_ref, gpw_ref,
         gscale_ref, gbglu_ref) = refs[13 + n_prev:]

        @pl.when(pl.program_id(0) == 0)
        def _():
            for r in (gwout_ref, gwa_ref, gwb_ref, gwglu_ref, gpw_ref, gscale_ref, gbglu_ref):
                r[...] = jnp.zeros_like(r)

        f = _mix_forward(layer, p_ref, y0_ref, pooled_ref, wglu_ref, bglu_ref, pw_ref, scale_ref, wa_ref, wb_ref,
                         saved=(ma_ref, mb_ref))
        wglu = wglu_ref[...].reshape(WIDTH, WIDTH)
        wout = wout_ref[...].reshape(D_MODEL, D_MODEL)
        blk = D_MODEL // n_k
        dxb = dx_ref[...].astype(BF16)
        dmerged = _dot_nt(dxb, wout)
        gwout = _dot_tn(f["merged"].astype(BF16), dxb)
        for k in range(n_k):
            gwout_ref[_slot(k)] += gwout[k * blk:(k + 1) * blk, :]
        dma = dmerged * f["sga"]
        dmb = dmerged * f["sgb"]
        dga = dmerged * f["ma"] * f["sga"] * (1.0 - f["sga"])
        dgb = dmerged * f["mb"] * f["sgb"] * (1.0 - f["sgb"])
        dmab = dma.astype(BF16)
        dmbb = dmb.astype(BF16)
        dya = _dot_nt(dmab, wa_ref[...])
        dyb = _dot_nt(dmbb, wb_ref[...])
        gwa = _dot_tn(f["yab"], dmab)
        gwb = _dot_tn(f["ybb"], dmbb)
        for k in range(n_k):
            gwa_ref[_slot(k)] += gwa[:, k * blk:(k + 1) * blk]
            gwb_ref[_slot(k)] += gwb[:, k * blk:(k + 1) * blk]
        zb, szb = f["zb"], f["szb"]
        dzb = dyb * f["ms"] * (szb * (1.0 + zb * (1.0 - szb)))
        dms = dyb * f["silu_zb"]
        gscale_ref[...] += jnp.sum(dms * f["mixed"], axis=0, keepdims=True)
        dmixed = (dms * f["scale"]).astype(BF16)
        pooled = f["pooled"]
        for g in range(len(POOL_WINDOWS)):
            cols = slice(g * POOL_GROUP, (g + 1) * POOL_GROUP)
            dpooled_ref[:, cols] = _dot_nt(dmixed[:, cols], pw_ref[g].astype(BF16))
            gpw_ref[g] += _dot_tn(pooled[:, cols].astype(BF16), dmixed[:, cols])
        za, sza = f["za"], f["sza"]
        dza = dya * f["y2"] * (sza * (1.0 + za * (1.0 - sza)))
        dy2 = dya * f["silu_za"]
        sq = f["sq"]
        dq = dy2 * f["y1"] * sq * (1.0 - sq)
        dqb = dq.astype(BF16)
        dy1 = dy2 * sq + _dot_nt(dqb, wglu)
        gwglu = _dot_tn(f["y1b"], dqb)
        rblk = WIDTH // n_k
        for k in range(n_k):
            gwglu_ref[_slot(k)] += gwglu[k * rblk:(k + 1) * rblk, :]
        gbglu_ref[...] += jnp.sum(dq, axis=0, keepdims=True)
        y0, t = f["y0"], f["t"]
        dgelu = 0.5 * (1.0 + t) + 0.5 * y0 * (1.0 - t * t) * (GELU_C * (1.0 + 3.0 * GELU_A * y0 * y0))
        dy0_ref[...] = dy1 * dgelu
        zeros = jnp.zeros((TILE_M, WIDTH), BF16)
        dproj_ref[:, 0:WIDTH] = zeros
        dproj_ref[:, WIDTH:2 * WIDTH] = dza.astype(BF16)
        dproj_ref[:, 2 * WIDTH:3 * WIDTH] = zeros
        dproj_ref[:, 3 * WIDTH:4 * WIDTH] = dzb.astype(BF16)
        dproj_ref[:, 4 * WIDTH:4 * WIDTH + D_MODEL] = dga.astype(BF16)
        dproj_ref[:, 4 * WIDTH + D_MODEL:] = dgb.astype(BF16)

    tile = lambda w: pl.BlockSpec((TILE_M, w), lambda i: (i, 0))
    shapes = _big_shapes()
    big = ["w_out", "w_branch_a", "w_branch_b", "ssm_w_glu"]
    slab = lambda n: pl.BlockSpec((None,) + shapes[n][1:], lambda i: (layer, 0, 0, 0))
    args = [dx_next, proj, y0, pooled, ma, mb, wg_glu, b_glu, pool_w, pool_scale, wg_a, wg_b, wg_out]
    return _pcall(
        body, name=f"mix_bwd_l{layer}",
        out_shape=(SDS((SEQ, N_IN), BF16), SDS((SEQ, WIDTH), F32), SDS((SEQ, WIDTH), F32))
        + tuple(SDS(shapes[n], F32) for n in big)
        + (SDS((4, POOL_GROUP, POOL_GROUP), F32), SDS((1, WIDTH), F32), SDS((1, WIDTH), F32)),
        grid=(SEQ // TILE_M,),
        in_specs=[tile(D_MODEL), tile(N_IN), tile(WIDTH), tile(WIDTH), tile(D_MODEL), tile(D_MODEL)]
        + _mix_weight_specs(layer) + [ANY] * n_prev,
        out_specs=(tile(N_IN), tile(WIDTH), tile(WIDTH)) + tuple(slab(n) for n in big)
        + (_const((4, POOL_GROUP, POOL_GROUP)), _const((1, WIDTH)), _const((1, WIDTH))),
        args=args + list(prev or ()),
        aliases={len(args) + i: 3 + i for i in range(n_prev)},
        sem=("arbitrary",), limit=VMEM_LIMIT_BIG, carry=carry)


def _pool_bwd(layer, dpooled, dproj):
    def body(dp_ref, _, o_ref):
        for gi, win in enumerate(POOL_WINDOWS):
            cols = slice(gi * POOL_GROUP, (gi + 1) * POOL_GROUP)
            dp = dp_ref[:, cols]
            t, count = _pool_counts(win)
            e = dp / count
            acc = e
            k = 1
            while k < win:
                acc = acc + jnp.where(t < SEQ - k, pltpu.roll(acc, SEQ - k, 0), 0.0)
                k *= 2
            o_ref[:, cols] = (acc - dp).astype(BF16)

    return pl.pallas_call(
        body, name=f"pool_bwd_l{layer}",
        out_shape=SDS((SEQ, N_IN), BF16),
        grid=(1,),
        in_specs=[pl.BlockSpec((SEQ, WIDTH), lambda i: (0, 0)), ANY],
        out_specs=pl.BlockSpec((SEQ, WIDTH), lambda i: (0, 2)),
        input_output_aliases={1: 0},
        compiler_params=_cp(("arbitrary",)),
    )(dpooled, dproj)


def _proj_wgrad(layer, x, norm_g, dproj, prev, carry=None):
    tm = 512
    n_prev = 0 if prev is None else 1

    def body(*refs):
        x_ref, g_ref, dp_ref = refs[:3]
        gw_ref, gb_ref, ht_ref = refs[3 + n_prev:]
        n, t = pl.program_id(0), pl.program_id(1)

        @pl.when(t == 0)
        def _():
            gw_ref[...] = jnp.zeros_like(gw_ref)
            gb_ref[...] = jnp.zeros_like(gb_ref)

        @pl.when(n == 0)
        def _():
            _, xn = _rms(x_ref[...])
            ht_ref[t] = (xn * g_ref[layer:layer + 1, :]).T.astype(BF16)

        dp = dp_ref[...]
        gw_ref[...] += _dot(ht_ref[t], dp)
        gb_ref[...] += jnp.sum(dp.astype(F32), axis=0, keepdims=True)

    return _pcall(
        body, name=f"proj_wgrad_l{layer}",
        out_shape=(SDS(_big_shapes()["w_in"], F32), SDS((1, N_IN), F32)),
        grid=(N_DEV, SEQ // tm),
        in_specs=[pl.BlockSpec((tm, D_MODEL), lambda n, t: (jnp.where(n == 0, t, 0), 0)),
                  _const((DEPTH, D_MODEL)),
                  pl.BlockSpec((tm, WIDTH), lambda n, t: (t, n))] + [ANY] * n_prev,
        out_specs=(pl.BlockSpec((None, None, D_MODEL, WIDTH), lambda n, t: (layer, _slot(n), 0, 0)),
                   pl.BlockSpec((1, WIDTH), lambda n, t: (0, n))),
        scratch_shapes=[pltpu.VMEM((SEQ // tm, D_MODEL, tm), BF16)],
        args=[x, norm_g, dproj] + ([prev] if n_prev else []),
        aliases={3: 0} if n_prev else {}, sem=("arbitrary", "arbitrary"), carry=carry)


def _proj_dgrad(layer, dx_next, x, norm_g, dproj, wg_in, carry=None):
    n_w = len(wg_in)

    def body(dxn_ref, x_ref, g_ref, dp_ref, *refs):
        w_refs, (dx_ref, gg_ref) = refs[:n_w], refs[n_w:]

        @pl.when(pl.program_id(0) == 0)
        def _():
            gg_ref[...] = jnp.zeros_like(gg_ref)

        parts = []
        for w_ref in w_refs:
            part = jnp.zeros((TILE_M, w_ref.shape[1]), F32)
            for k in range(N_DEV):
                part = part + _dot_nt(dp_ref[:, k * WIDTH:(k + 1) * WIDTH], w_ref[k])
            parts.append(part)
        dh = parts[0] if n_w == 1 else jnp.concatenate(parts, axis=1)
        rs, xn = _rms(x_ref[...])
        gg_ref[...] += jnp.sum(dh * xn, axis=0, keepdims=True)
        dxn = dh * g_ref[layer:layer + 1, :]
        dx_ref[...] = dxn_ref[...] + rs * (dxn - xn * jnp.mean(dxn * xn, axis=-1, keepdims=True))

    return _pcall(
        body, name=f"proj_dgrad_l{layer}",
        out_shape=(SDS((SEQ, D_MODEL), F32), SDS((1, D_MODEL), F32)),
        grid=(SEQ // TILE_M,),
        in_specs=[pl.BlockSpec((TILE_M, D_MODEL), lambda i: (i, 0)),
                  pl.BlockSpec((TILE_M, D_MODEL), lambda i: (i, 0)),
                  _const((DEPTH, D_MODEL)),
                  pl.BlockSpec((TILE_M, N_IN), lambda i: (i, 0))] + [_const(w.shape) for w in wg_in],
        out_specs=(pl.BlockSpec((TILE_M, D_MODEL), lambda i: (i, 0)), _const((1, D_MODEL))),
        args=[dx_next, x, norm_g, dproj, *wg_in], sem=("arbitrary",), carry=carry)


def _my_place():
    return lax.axis_index("x"), lax.axis_index("y"), lax.axis_index("c")


def _gather_plan(shards, layer, by_columns=(), rows_of=None):
    n = len(shards)

    def parts(ins, outs, sems):
        send_sems, recv_sems, local_sems = sems
        x, y, c = _my_place()
        chips = [(1 - x, y), (x, 1 - y), (1 - x, 1 - y)]

        def source(t):
            return ins[t].at[layer] if rows_of is None else ins[t].at[layer, pl.ds(*rows_of)]

        def rows(t, place):
            px, py, pc = place
            index = 4 * px + 2 * py + pc
            if t in by_columns:
                width = shards[t].shape[2]
                return outs[t].at[:, pl.ds(pl.multiple_of(index * width, LANES), width)]
            return outs[t].at[index]

        def copy(t, k, block, to, from_src=False):
            return pltpu.make_async_remote_copy(
                src_ref=source(t) if from_src else rows(t, block), dst_ref=rows(t, block),
                send_sem=send_sems.at[7 * t + k], recv_sem=recv_sems.at[7 * t + k], device_id=to,
                device_id_type=MESH)

        def mine(t):
            return pltpu.make_async_copy(source(t), rows(t, (x, y, c)), local_sems.at[t])

        return (x, y, c), chips, copy, mine

    def start(ins, outs, sems):
        me, chips, copy, mine = parts(ins, outs, sems)
        x, y, c = me
        for t in range(n):
            mine(t).start()
            copy(t, 0, me, (x, y, 1 - c), from_src=True).start()
            for j, chip in enumerate(chips):
                copy(t, 1 + j, me, (*chip, c), from_src=True).start()

    def relay(ins, outs, sems):
        me, chips, copy, mine = parts(ins, outs, sems)
        x, y, c = me
        for t in range(n):
            for j, chip in enumerate(chips):
                copy(t, 1 + j, (*chip, c), me).wait_recv()
                copy(t, 4 + j, (*chip, c), (x, y, 1 - c)).start()

    def finish(ins, outs, sems):
        me, chips, copy, mine = parts(ins, outs, sems)
        x, y, c = me
        sibling = (x, y, 1 - c)
        for t in range(n):
            copy(t, 0, sibling, me).wait_recv()
            for j, chip in enumerate(chips):
                copy(t, 4 + j, (*chip, 1 - c), me).wait_recv()
            for k in range(7):
                copy(t, k, me, sibling, from_src=k < 4).wait_send()
            mine(t).wait()

    n_rows = lambda a: a.shape[1] if rows_of is None else rows_of[1]
    out_shape = [SDS((a.shape[1], N_DEV * a.shape[2]) if t in by_columns else (N_DEV, n_rows(a), a.shape[2]), a.dtype)
                 for t, a in enumerate(shards)]
    sems = [pltpu.SemaphoreType.DMA((7 * n,)), pltpu.SemaphoreType.DMA((7 * n,)), pltpu.SemaphoreType.DMA((n,))]
    return _Carried(shards, out_shape, sems, start, finish, relay)


class _Carried:
    def __init__(self, ins, out_shape, sems, start, finish, relay=None):
        self.ins, self.out_shape, self.sems = list(ins), list(out_shape), list(sems)
        self.start, self.finish = start, finish
        self.relay = relay or (lambda ins, outs, sems: None)


def _pcall(body, *, name, grid, in_specs, out_specs, out_shape, args, scratch_shapes=(), aliases=None,
           sem=None, limit=VMEM_LIMIT, carry=None):
    out_shape, out_specs, scratch_shapes = list(out_shape), list(out_specs), list(scratch_shapes)
    n_in, n_out, n_scr = len(args), len(out_shape), len(scratch_shapes)
    if carry is None:
        kern, c_ins, c_out, c_sems = body, [], [], []
    else:
        c_ins, c_out, c_sems = carry.ins, carry.out_shape, carry.sems
        ci, co = len(c_ins), len(c_out)
        steps = tuple(grid)

        def kern(*refs):
            o0 = n_in + ci
            s0 = o0 + n_out + co
            mine = refs[:n_in] + refs[o0:o0 + n_out] + refs[s0:s0 + n_scr]
            theirs = (refs[n_in:o0], refs[o0 + n_out:s0], refs[s0 + n_scr:])
            first = pl.program_id(0) == 0
            last = pl.program_id(0) == steps[0] - 1
            for a in range(1, len(steps)):
                first = jnp.logical_and(first, pl.program_id(a) == 0)
                last = jnp.logical_and(last, pl.program_id(a) == steps[a] - 1)

            @pl.when(first)
            def _():
                carry.start(*theirs)

            @pl.when(last)
            def _():
                carry.relay(*theirs)

            body(*mine)

            @pl.when(last)
            def _():
                carry.finish(*theirs)

        sem = ("arbitrary",) * len(steps)
    res = pl.pallas_call(
        kern, name=name, grid=tuple(grid),
        in_specs=list(in_specs) + [ANY] * len(c_ins),
        out_specs=tuple(out_specs + [ANY] * len(c_out)),
        out_shape=tuple(out_shape + c_out),
        scratch_shapes=scratch_shapes + c_sems,
        input_output_aliases=aliases or {},
        compiler_params=_cp(sem, limit),
    )(*args, *c_ins)
    return res[:n_out], res[n_out:]


def _run_carried(name, carry):
    ci, co = len(carry.ins), len(carry.out_shape)

    def body(*refs):
        parts = (refs[:ci], refs[ci:ci + co], refs[ci + co:])
        carry.start(*parts)
        carry.relay(*parts)
        carry.finish(*parts)

    return pl.pallas_call(
        body, name=name, out_shape=tuple(carry.out_shape),
        in_specs=[ANY] * ci, out_specs=tuple([ANY] * co), scratch_shapes=carry.sems,
    )(*carry.ins)


def _sibling_plan(big, small):
    n = len(big)
    n_copies = 4 * n + len(small)

    def copies(ins, outs, sems):
        send_sems, recv_sems = sems
        x, y, c = _my_place()
        pairs = []
        for t, (_, layer) in enumerate(big):
            for s in range(4):
                pairs.append((ins[t].at[layer, pl.ds(4 * (1 - c) + s, 1)], outs[t].at[pl.ds(s, 1)]))
        pairs += list(zip(ins[n:], outs[n:]))
        return [pltpu.make_async_remote_copy(
            src_ref=src, dst_ref=dst, send_sem=send_sems.at[k], recv_sem=recv_sems.at[k],
            device_id=(x, y, 1 - c), device_id_type=MESH) for k, (src, dst) in enumerate(pairs)]

    def start(ins, outs, sems):
        for cp in copies(ins, outs, sems):
            cp.start()

    def finish(ins, outs, sems):
        for cp in copies(ins, outs, sems):
            cp.wait()

    out_shape = [SDS((4,) + a.shape[2:], a.dtype) for a, _ in big] + [SDS(a.shape, a.dtype) for a in small]
    sems = [pltpu.SemaphoreType.DMA((n_copies,)), pltpu.SemaphoreType.DMA((n_copies,))]
    return _Carried([a for a, _ in big] + list(small), out_shape, sems, start, finish)


def _chips_plan(big, small):
    n, n_small = len(big), len(small)
    max_rows = 512
    parts = [max(1, a.shape[1] // max_rows) for a in big]
    n_copies = 3 * (sum(parts) + n_small)

    def copies(ins, outs, sems, landing):
        send_sems, recv_sems, local_sems = sems
        x, y, c = _my_place()
        my_chip = 2 * x + y
        chips = [(1 - x, y), (x, 1 - y), (1 - x, 1 - y)]
        remote, local = [], []
        for chip in chips:
            to = 2 * chip[0] + chip[1]
            slot = to if landing else my_chip
            pairs = []
            for t in range(n):
                rows_per = big[t].shape[1] // parts[t]
                for p in range(parts[t]):
                    rows = pl.ds(p * rows_per, rows_per)
                    pairs.append((ins[t].at[to, rows], outs[t].at[slot, rows]))
            pairs += [(ins[t], outs[t].at[slot]) for t in range(n, n + n_small)]
            for src, dst in pairs:
                k = len(remote)
                remote.append(pltpu.make_async_remote_copy(
                    src_ref=src, dst_ref=dst, send_sem=send_sems.at[k], recv_sem=recv_sems.at[k],
                    device_id=(*chip, c), device_id_type=MESH))
        for t in range(n):
            local.append(pltpu.make_async_copy(ins[t].at[my_chip], outs[t].at[my_chip], local_sems.at[t]))
        for t in range(n, n + n_small):
            local.append(pltpu.make_async_copy(ins[t], outs[t].at[my_chip], local_sems.at[t]))
        return remote + local

    def start(ins, outs, sems):
        for cp in copies(ins, outs, sems, landing=False):
            cp.start()

    def finish(ins, outs, sems):
        for cp in copies(ins, outs, sems, landing=True):
            cp.wait()

    out_shape = [SDS(a.shape, a.dtype) for a in big] + [SDS((N_CHIP,) + a.shape, a.dtype) for a in small]
    sems = [pltpu.SemaphoreType.DMA((n_copies,)), pltpu.SemaphoreType.DMA((n_copies,)),
            pltpu.SemaphoreType.DMA((n + n_small,))]
    return _Carried(list(big) + list(small), out_shape, sems, start, finish)


def _all_plan(small):
    n = len(small)
    masks = [(m >> 2 & 1, m >> 1 & 1, m & 1) for m in range(1, N_DEV)]

    def copies(ins, outs, sems, landing):
        send_sems, recv_sems, local_sems = sems
        x, y, c = _my_place()
        me = 4 * x + 2 * y + c
        flip = lambda v, bit: 1 - v if bit else v
        remote = []
        for fx, fy, fc in masks:
            peer = (flip(x, fx), flip(y, fy), flip(c, fc))
            slot = 4 * peer[0] + 2 * peer[1] + peer[2] if landing else me
            for t in range(n):
                k = len(remote)
                remote.append(pltpu.make_async_remote_copy(
                    src_ref=ins[t], dst_ref=outs[t].at[slot], send_sem=send_sems.at[k], recv_sem=recv_sems.at[k],
                    device_id=peer, device_id_type=MESH))
        local = [pltpu.make_async_copy(ins[t], outs[t].at[me], local_sems.at[t]) for t in range(n)]
        return remote + local

    def start(ins, outs, sems):
        for cp in copies(ins, outs, sems, landing=False):
            cp.start()

    def finish(ins, outs, sems):
        for cp in copies(ins, outs, sems, landing=True):
            cp.wait()

    out_shape = [SDS((N_DEV,) + a.shape, a.dtype) for a in small]
    sems = [pltpu.SemaphoreType.DMA((7 * n,)), pltpu.SemaphoreType.DMA((7 * n,)), pltpu.SemaphoreType.DMA((n,))]
    return _Carried(list(small), out_shape, sems, start, finish)


def _join(*plans):
    plans = [p for p in plans if p is not None]
    if len(plans) <= 1:
        return plans[0] if plans else None

    def each(fn_name, ins, outs, sems):
        i = o = s = 0
        for p in plans:
            ni, no, ns = len(p.ins), len(p.out_shape), len(p.sems)
            getattr(p, fn_name)(ins[i:i + ni], outs[o:o + no], sems[s:s + ns])
            i, o, s = i + ni, o + no, s + ns

    return _Carried(sum((p.ins for p in plans), []), sum((p.out_shape for p in plans), []),
                    sum((p.sems for p in plans), []),
                    lambda i, o, s: each("start", i, o, s), lambda i, o, s: each("finish", i, o, s),
                    lambda i, o, s: each("relay", i, o, s))


def _row_block(rows, most=256):
    return min(rows, most)


def _add_own(tag, core, gs, layer, gots):
    n = len(gs)

    def body(core_ref, *refs):
        for a_ref, b_ref, o_ref in zip(refs[:n], refs[n:2 * n], refs[2 * n:]):
            o_ref[...] = (a_ref[...] + b_ref[...]).astype(o_ref.dtype)

    mine = lambda a: pl.BlockSpec((None, None) + a.shape[1:], lambda s, core: (layer, 4 * core[0] + s, 0, 0))
    theirs = lambda a: pl.BlockSpec((None,) + a.shape[1:], lambda s, core: (s, 0, 0))
    return pl.pallas_call(
        body, name=f"add_{tag}", out_shape=tuple(SDS(a.shape, BF16) for a in gots),
        grid_spec=pltpu.PrefetchScalarGridSpec(
            num_scalar_prefetch=1, grid=(4,),
            in_specs=[mine(a) for a in gots] + [theirs(a) for a in gots],
            out_specs=tuple(theirs(a) for a in gots)),
        compiler_params=_cp(("parallel",)),
    )(core, *gs, *gots)


def _add_lists(tag, own, got, grid=None, specs=None, dtype=F32):
    n = len(own)

    def body(*refs):
        for a, b, o in zip(refs[:n], refs[n:2 * n], refs[2 * n:]):
            o[...] = (a[...] + b[...]).astype(o.dtype)

    kw = {}
    if grid is not None:
        kw = dict(grid=grid, in_specs=list(specs) * 2, out_specs=tuple(specs),
                  compiler_params=_cp(("parallel",) * len(grid)))
    return pl.pallas_call(
        body, name=f"add_{tag}", out_shape=tuple(SDS(a.shape, dtype) for a in own), **kw)(*own, *got)


def _adamw_math(w, g, m, v):
    m = ADAM_B1 * m + (1.0 - ADAM_B1) * g
    v = ADAM_B2 * v + (1.0 - ADAM_B2) * (g * g)
    m_hat = m / (1.0 - ADAM_B1 ** ADAM_STEP)
    v_hat = v / (1.0 - ADAM_B2 ** ADAM_STEP)
    delta = -ADAM_LR * (m_hat / (jnp.sqrt(v_hat) + ADAM_EPS) + ADAM_WD * w)
    return delta, m, v


def _sum_slots_adamw(tag, slots, w, m, v):
    _, r, c = slots[0].shape
    rb = _row_block(r, most=512)

    def body(s0_ref, s1_ref, w_ref, m_ref, v_ref, g_ref, d_ref, nm_ref, nv_ref):
        first = pl.program_id(1) == 0
        g = _pair_sum([jnp.where(first, s0_ref[k], s1_ref[k]).astype(F32) for k in range(N_CHIP)])
        delta, nm, nv = _adamw_math(w_ref[...], g, m_ref[...], v_ref[...])
        g_ref[...] = g
        d_ref[...] = delta
        nm_ref[...] = nm
        nv_ref[...] = nv

    spec = pl.BlockSpec((None, rb, c), lambda j, l: (l, j, 0))
    sspec = pl.BlockSpec((N_CHIP, rb, c), lambda j, l: (0, j, 0))
    s = SDS((DEPTH, r, c), F32)
    return pl.pallas_call(
        body, name=f"adamw_{tag}", out_shape=(s, s, s, s),
        grid=(r // rb, DEPTH), in_specs=[sspec, sspec, spec, spec, spec], out_specs=(spec, spec, spec, spec),
        compiler_params=_cp(("parallel", "arbitrary")),
    )(*slots, w, m, v)


def _adamw_small(tag, entries, grid=None, sums=()):
    flat_in, in_specs, out_shape, out_specs, layout = [], [], [], [], []
    for slots, w, m, v, slot_spec, w_spec in entries:
        per_layer = isinstance(slots, (list, tuple))
        n_slot = len(slots) if per_layer else 1
        flat_in += (list(slots) if per_layer else [slots]) + [w, m, v]
        in_specs += [slot_spec] * n_slot + [w_spec] * 3
        out_shape += [SDS(w.shape, F32)] * 4
        out_specs += [w_spec] * 4
        layout.append((per_layer, n_slot))
    n_entry_in = len(flat_in)
    flat_in += list(sums)
    out_shape += [SDS(s.shape[1:], F32) for s in sums]
    n_in = len(flat_in)

    def body(*refs):
        for s_ref, o_ref in zip(refs[n_entry_in:n_in], refs[len(refs) - len(sums):]):
            o_ref[...] = _sum_slots(s_ref)
        i, o = 0, n_in
        for per_layer, n_slot in layout:
            s_refs = refs[i:i + n_slot]
            w_ref, m_ref, v_ref = refs[i + n_slot:i + n_slot + 3]
            outs = refs[o:o + 4]
            if per_layer:
                for l, s_ref in enumerate(s_refs):
                    at = (slice(l, l + 1),) if len(w_ref.shape) == 2 else (l,)
                    g = _sum_slots(s_ref)
                    res = (g,) + _adamw_math(w_ref[at], g, m_ref[at], v_ref[at])
                    for o_ref, val in zip(outs, res):
                        o_ref[at] = val
            else:
                g = _sum_slots(s_refs[0])
                res = (g,) + _adamw_math(w_ref[...], g, m_ref[...], v_ref[...])
                for o_ref, val in zip(outs, res):
                    o_ref[...] = val
            i += n_slot + 3
            o += 4

    kw = {}
    if grid is not None:
        kw = dict(grid=grid, in_specs=in_specs, out_specs=tuple(out_specs),
                  compiler_params=_cp(("parallel",) * len(grid)))
    res = pl.pallas_call(body, name=f"adamw_{tag}", out_shape=tuple(out_shape), **kw)(*flat_in)
    return [tuple(res[4 * e:4 * e + 4]) for e in range(len(entries))], res[4 * len(entries):]


def kernel(x, norm_g, w_in, b_in, ssm_log_dt, ssm_lam_re, ssm_lam_im, ssm_b_re, ssm_b_im, ssm_c_re, ssm_c_im, ssm_d, ssm_w_glu, ssm_b_glu, pool_w, pool_scale, w_branch_a, w_branch_b, w_out, final_norm_g, loss_target, m_norm_g, m_w_in, m_b_in, m_ssm_log_dt, m_ssm_lam_re, m_ssm_lam_im, m_ssm_b_re, m_ssm_b_im, m_ssm_c_re, m_ssm_c_im, m_ssm_d, m_ssm_w_glu, m_ssm_b_glu, m_pool_w, m_pool_scale, m_w_branch_a, m_w_branch_b, m_w_out, m_final_norm_g, v_norm_g, v_w_in, v_b_in, v_ssm_log_dt, v_ssm_lam_re, v_ssm_lam_im, v_ssm_b_re, v_ssm_b_im, v_ssm_c_re, v_ssm_c_im, v_ssm_d, v_ssm_w_glu, v_ssm_b_glu, v_pool_w, v_pool_scale, v_w_branch_a, v_w_branch_b, v_w_out, v_final_norm_g):
    weights = dict(norm_g=norm_g, w_in=w_in, b_in=b_in, ssm_log_dt=ssm_log_dt, ssm_lam_re=ssm_lam_re,
                   ssm_lam_im=ssm_lam_im, ssm_b_re=ssm_b_re, ssm_b_im=ssm_b_im, ssm_c_re=ssm_c_re,
                   ssm_c_im=ssm_c_im, ssm_d=ssm_d, ssm_w_glu=ssm_w_glu, ssm_b_glu=ssm_b_glu, pool_w=pool_w,
                   pool_scale=pool_scale, w_branch_a=w_branch_a, w_branch_b=w_branch_b, w_out=w_out,
                   final_norm_g=final_norm_g.reshape(1, D_MODEL))
    mom_m = dict(norm_g=m_norm_g, w_in=m_w_in, b_in=m_b_in, ssm_log_dt=m_ssm_log_dt, ssm_lam_re=m_ssm_lam_re,
                 ssm_lam_im=m_ssm_lam_im, ssm_b_re=m_ssm_b_re, ssm_b_im=m_ssm_b_im, ssm_c_re=m_ssm_c_re,
                 ssm_c_im=m_ssm_c_im, ssm_d=m_ssm_d, ssm_w_glu=m_ssm_w_glu, ssm_b_glu=m_ssm_b_glu,
                 pool_w=m_pool_w, pool_scale=m_pool_scale, w_branch_a=m_w_branch_a, w_branch_b=m_w_branch_b,
                 w_out=m_w_out, final_norm_g=m_final_norm_g.reshape(1, D_MODEL))
    mom_v = dict(norm_g=v_norm_g, w_in=v_w_in, b_in=v_b_in, ssm_log_dt=v_ssm_log_dt, ssm_lam_re=v_ssm_lam_re,
                 ssm_lam_im=v_ssm_lam_im, ssm_b_re=v_ssm_b_re, ssm_b_im=v_ssm_b_im, ssm_c_re=v_ssm_c_re,
                 ssm_c_im=v_ssm_c_im, ssm_d=v_ssm_d, ssm_w_glu=v_ssm_w_glu, ssm_b_glu=v_ssm_b_glu,
                 pool_w=v_pool_w, pool_scale=v_pool_scale, w_branch_a=v_w_branch_a, w_branch_b=v_w_branch_b,
                 w_out=v_w_out, final_norm_g=v_final_norm_g.reshape(1, D_MODEL))
    order = ["norm_g", "w_in", "b_in", "ssm_log_dt", "ssm_lam_re", "ssm_lam_im", "ssm_b_re", "ssm_b_im",
             "ssm_c_re", "ssm_c_im", "ssm_d", "ssm_w_glu", "ssm_b_glu", "pool_w", "pool_scale", "w_branch_a",
             "w_branch_b", "w_out", "final_norm_g"]
    big_names = ["w_in", "ssm_w_glu", "w_branch_a", "w_branch_b", "w_out"]

    log_dt3 = ssm_log_dt.reshape(DEPTH, N_GROUP, 1)
    b_t = lambda a: a.transpose(0, 1, 3, 2)
    for d in (weights, mom_m, mom_v):
        d["ssm_b_re"], d["ssm_b_im"] = b_t(d["ssm_b_re"]), b_t(d["ssm_b_im"])
    bt_re, bt_im = weights["ssm_b_re"], weights["ssm_b_im"]
    abar_re, abar_im, bbt_re, bbt_im = _s5_params(log_dt3, ssm_lam_re, ssm_lam_im, bt_re, bt_im)
    s5_args = (bbt_re, bbt_im, ssm_c_re, ssm_c_im, abar_re, abar_im, ssm_d)

    w16 = {n: weights[n].astype(BF16) for n in big_names}
    rest = [w16[n] for n in big_names[1:]]
    half = D_MODEL // 2
    wg_in = [None, [None, None]]
    wg_rest = [None, None]
    wg_in[0] = list(_run_carried("gather_w_in_l0", _gather_plan([w16["w_in"]], 0)))
    xs = [x.reshape(SEQ, D_MODEL)]
    saved = []
    for l in range(DEPTH):
        proj, moved = _norm_proj(l, xs[l], norm_g, wg_in[l], b_in,
                                 carry=_gather_plan([w16["w_in"]], 1, rows_of=(0, half)) if l == 0 else None)
        if l == 0:
            (wg_in[1][0],) = moved
        (states, y0), wg_rest[l] = _s5_scan_fwd(l, proj, *s5_args, carry=_gather_plan(rest, l, by_columns=(1, 2)))
        pooled = _pool_fwd(l, proj)
        wg_glu, wg_a, wg_b, wg_out = wg_rest[l]
        (x_next, ma, mb), moved = _mix_fwd(l, xs[l], proj, y0, pooled, wg_glu, ssm_b_glu, pool_w, pool_scale, wg_a, wg_b,
                                 wg_out, carry=_gather_plan([w16["w_in"]], 1, rows_of=(half, half)) if l == 0 else None)
        if l == 0:
            (wg_in[1][1],) = moved
        xs.append(x_next)
        saved.append((proj, states, y0, pooled, ma, mb))

    dx, loss_part, g_final = _loss_head(xs[DEPTH], loss_target.reshape(SEQ, D_MODEL), weights["final_norm_g"])

    core = lax.axis_index("c").astype(jnp.int32).reshape(1)
    vec_names = ["norm_g", "b_in", "ssm_d", "ssm_b_glu", "pool_scale", "ssm_log_dt"]
    s5_names = ["ssm_log_dt", "ssm_lam_re", "ssm_lam_im", "ssm_b_re", "ssm_b_im"]
    mat_names = ["pool_w", "ssm_c_re", "ssm_c_im", "ssm_b_re", "ssm_b_im"]
    lane_sparse = ("ssm_c_re", "ssm_c_im", "ssm_b_re", "ssm_b_im")

    def dense(key, a):
        return a.reshape(-1, LANES) if key[0] in lane_sparse else a

    def undense(key, slots):
        return slots.reshape((N_CHIP, N_GROUP, GROUP_W, STATE)) if key[0] in lane_sparse else slots

    def add_small(tag, keys, own, got):
        out = [None] * len(keys)
        whole = [i for i, k in enumerate(keys) if k[0] not in mat_names]
        tiled = [i for i, k in enumerate(keys) if k[0] in mat_names]
        if whole:
            for i, r in zip(whole, _add_lists(f"{tag}_a", [own[i] for i in whole], [got[i] for i in whole])):
                out[i] = r
        if tiled:
            specs = [pl.BlockSpec((1, POOL_GROUP, POOL_GROUP), lambda j: (j, 0, 0)) if keys[i][0] == "pool_w"
                     else pl.BlockSpec((own[i].shape[0] // N_CHUNK, LANES), lambda j: (j, 0)) for i in tiled]
            for i, r in zip(tiled, _add_lists(f"{tag}_b", [own[i] for i in tiled], [got[i] for i in tiled],
                                              grid=(N_CHUNK,), specs=specs, dtype=BF16)):
                out[i] = r
        return out

    sm = {("final_norm_g", None): g_final, ("loss", None): loss_part}
    slots = {}
    grads = dict.fromkeys(big_names)

    class Wave:
        def __init__(self, tag, layer, big, keys):
            self.tag, self.layer, self.big, self.keys = tag, layer, big, keys

        def to_sibling(self):
            self.own = [dense(k, sm[k]) for k in self.keys]
            return _sibling_plan([(grads[n], self.layer) for n in self.big], self.own)

        def add(self, moved):
            nb = len(self.big)
            self.chip_big = list(_add_own(self.tag, core, [grads[n] for n in self.big], self.layer, moved[:nb])
                                 ) if nb else []
            self.chip_small = add_small(self.tag, self.keys, self.own, moved[nb:])

        def to_chips(self, big=None, small=True):
            self.sent = list(self.big if big is None else big), small
            return _chips_plan([self.chip_big[self.big.index(n)] for n in self.sent[0]],
                               self.chip_small if small else [])

        def landed(self, moved):
            names, small = self.sent
            for n, s in zip(names, moved[:len(names)]):
                slots[(n, self.layer)] = s
            if small:
                for k, s in zip(self.keys, moved[len(names):]):
                    slots[k] = undense(k, s)
            return moved[len(names) + (len(self.keys) if small else 0):]

    def s5_param_grads(l, g_abar_re, g_abar_im, g_bbt_re, g_bbt_im):
        g = _s5_params_bwd(l, log_dt3, ssm_lam_re, ssm_lam_im, bt_re, bt_im, g_abar_re, g_abar_im, g_bbt_re, g_bbt_im)
        sm[("ssm_log_dt", l)] = g[0].reshape(1, N_GROUP)
        for n, a in zip(s5_names[1:], g[1:]):
            sm[(n, l)] = a

    small1 = ["b_in", "ssm_d", "ssm_b_glu", "pool_scale", "pool_w", "ssm_c_re", "ssm_c_im"] + s5_names
    w1 = Wave("chip1", 1, list(big_names), [(n, 1) for n in small1] + [("final_norm_g", None), ("loss", None)])
    early = Wave("chip0e", 0, big_names[1:], [("pool_w", 0), ("pool_scale", 0), ("ssm_b_glu", 0)])
    mid = Wave("chip0m", 0, [], [(n, 0) for n in ["ssm_c_re", "ssm_c_im", "ssm_d"] + s5_names] + [("norm_g", 1)])
    late = Wave("chip0l", 0, ["w_in"], [("b_in", 0)])

    mix_prev, gw_in = None, None
    for l in reversed(range(DEPTH)):
        proj, states, y0, pooled, ma, mb = saved[l]
        wg_glu, wg_a, wg_b, wg_out = wg_rest[l]
        res, moved = _mix_bwd(l, dx, proj, y0, pooled, ma, mb, wg_glu, ssm_b_glu, pool_w, pool_scale, wg_a, wg_b, wg_out,
                              mix_prev, carry=None if l == 1 else w1.to_chips(big=["w_in"], small=False))
        if l == 0:
            w1.landed(moved)
        dproj, dy0, dpooled = res[:3]
        mix_prev = list(res[3:7])
        grads["w_out"], grads["w_branch_a"], grads["w_branch_b"], grads["ssm_w_glu"] = mix_prev
        sm[("pool_w", l)], sm[("pool_scale", l)], sm[("ssm_b_glu", l)] = res[7:]
        dproj = _pool_bwd(l, dpooled, dproj)
        carry = None if l == 1 else _join(w1.to_chips(big=big_names[1:]), early.to_sibling())
        res, moved = _s5_scan_bwd(l, dy0, proj, states, *s5_args, dproj, carry=carry)
        if l == 0:
            early.add(w1.landed(moved))
        dproj, g_bbt_re, g_bbt_im, sm[("ssm_c_re", l)], sm[("ssm_c_im", l)], g_abar_re, g_abar_im, sm[("ssm_d", l)] = res
        s5_param_grads(l, g_abar_re, g_abar_im, g_bbt_re, g_bbt_im)
        carry = None if l == 1 else _join(early.to_chips(), mid.to_sibling())
        (gw_in, sm[("b_in", l)]), moved = _proj_wgrad(l, xs[l], norm_g, dproj, gw_in, carry=carry)
        grads["w_in"] = gw_in
        if l == 0:
            mid.add(early.landed(moved))
        carry = w1.to_sibling() if l == 1 else _join(mid.to_chips(), late.to_sibling())
        (dx, sm[("norm_g", l)]), moved = _proj_dgrad(l, dx, xs[l], norm_g, dproj, wg_in[l], carry=carry)
        if l == 1:
            w1.add(moved)
        else:
            late.add(mid.landed(moved))
    grad_x = dx.reshape(1, SEQ, D_MODEL)
    moved = late.landed(_run_carried("exchange_last", _join(late.to_chips(), _all_plan([sm[("norm_g", 0)]]))))
    slots[("norm_g", 0)] = moved[0]

    res = {}
    for n in big_names:
        res[n] = _sum_slots_adamw(n, [slots[(n, l)] for l in range(DEPTH)], weights[n], mom_m[n], mom_v[n])
    per_layer = lambda n: [slots[(n, l)] for l in range(DEPTH)]
    names_a = vec_names + ["ssm_lam_re", "ssm_lam_im"]
    entries_a = [(per_layer(n), weights[n], mom_m[n], mom_v[n], None, None) for n in names_a]
    n = "final_norm_g"
    entries_a.append((slots[(n, None)], weights[n], mom_m[n], mom_v[n], None, None))
    out_a, (loss,) = _adamw_small("small_a", entries_a, sums=[slots[("loss", None)]])
    loss = loss.reshape(())
    for n, r in zip(names_a + ["final_norm_g"], out_a):
        res[n] = r
    res["final_norm_g"] = tuple(a.reshape(D_MODEL) for a in res["final_norm_g"])
    pw_s = pl.BlockSpec((N_CHIP, 1, POOL_GROUP, POOL_GROUP), lambda j: (0, j, 0, 0))
    pw_w = pl.BlockSpec((DEPTH, 1, POOL_GROUP, POOL_GROUP), lambda j: (0, j, 0, 0))
    c_s = pl.BlockSpec((N_CHIP, CH_G, GROUP_W, STATE), lambda j: (0, j, 0, 0))
    c_w = pl.BlockSpec((DEPTH, CH_G, GROUP_W, STATE), lambda j: (0, j, 0, 0))
    entries_b = [(per_layer(n), weights[n], mom_m[n], mom_v[n], pw_s if n == "pool_w" else c_s,
                  pw_w if n == "pool_w" else c_w) for n in mat_names]
    out_b, _ = _adamw_small("small_b", entries_b, grid=(N_CHUNK,))
    for n, r in zip(mat_names, out_b):
        res[n] = tuple(b_t(a) for a in r) if n in ("ssm_b_re", "ssm_b_im") else r

    outs = [loss, grad_x]
    for i in range(4):
        outs += [res[n][i] for n in order]
    return tuple(outs)
```

```python
import math

import jax
import jax.numpy as jnp
from jax import lax
from jax.experimental import pallas as pl
from jax.experimental.pallas import tpu as pltpu

F32 = jnp.float32
BF16 = jnp.bfloat16

SEQ = 2048
D_MODEL = 1024
N_IN = 4096
WIDTH = 512
N_GROUP = 32
GROUP_W = 16
STATE = 64
N_STATE = N_GROUP * STATE
N_CHUNK = 4
CH_G = N_GROUP // N_CHUNK
CH_W = WIDTH // N_CHUNK
CH_S = N_STATE // N_CHUNK
N_DEV = 8
N_CHIP = 4
POOL_WINDOWS = (2, 4, 8, 16)
POOL_GROUP = 128
EPS = 1e-6
DEPTH = 2

ADAM_LR = 0.001
ADAM_B1 = 0.9
ADAM_B2 = 0.999
ADAM_EPS = 1e-08
ADAM_WD = 0.01
ADAM_STEP = 10

LANES = 128
SUBLANES = 8
TILE_M = 256
VMEM_LIMIT = 48 * 1024 * 1024
VMEM_LIMIT_BIG = 60 * 1024 * 1024
MESH = pl.DeviceIdType.MESH
ANY = pl.BlockSpec(memory_space=pl.ANY)

GELU_C = math.sqrt(2.0 / math.pi)
GELU_A = 0.044715

SDS = jax.ShapeDtypeStruct


def _cp(sem=None, limit=VMEM_LIMIT):
    return pltpu.CompilerParams(dimension_semantics=sem, vmem_limit_bytes=limit)


def _dot(a, b):
    return jnp.dot(a, b, preferred_element_type=F32)


def _dot_nt(a, b):
    return lax.dot_general(a, b, (((1,), (1,)), ((), ())), preferred_element_type=F32)


def _dot_tn(a, b):
    return lax.dot_general(a, b, (((0,), (0,)), ((), ())), preferred_element_type=F32)


def _sig(x):
    return jax.nn.sigmoid(x)


def _rms(x):
    rs = lax.rsqrt(jnp.mean(x * x, axis=-1, keepdims=True) + EPS)
    return rs, x * rs


def _slot(n):
    return 4 * (n % 2) + n // 2


def _const(shape):
    n = len(shape)
    return pl.BlockSpec(shape, lambda *_: (0,) * n)


def _pair_sum(vals):
    while len(vals) > 1:
        vals = [vals[i] + vals[i + 1] for i in range(0, len(vals), 2)]
    return vals[0]


def _sum_slots(s_ref):
    return _pair_sum([s_ref[k].astype(F32) for k in range(s_ref.shape[0])])


def _s5_param_fn(log_dt, lam_re, lam_im, bt_re, bt_im):
    dt = jnp.exp(log_dt)
    mag = jnp.exp(lam_re * dt)
    ang = lam_im * dt
    abar_re = mag * jnp.cos(ang)
    abar_im = mag * jnp.sin(ang)
    num_re = abar_re - 1.0
    num_im = abar_im
    den = lam_re * lam_re + lam_im * lam_im
    coef_re = (num_re * lam_re + num_im * lam_im) / den
    coef_im = (num_im * lam_re - num_re * lam_im) / den
    bbar_re = coef_re[..., None, :] * bt_re - coef_im[..., None, :] * bt_im
    bbar_im = coef_re[..., None, :] * bt_im + coef_im[..., None, :] * bt_re
    return abar_re, abar_im, bbar_re, bbar_im


def _s5_params(log_dt, lam_re, lam_im, bt_re, bt_im):
    def body(ld, lr, li, br, bi, o_ar, o_ai, o_br, o_bi):
        ar, ai, bbr, bbi = _s5_param_fn(ld[...], lr[...], li[...], br[...], bi[...])
        o_ar[...] = ar
        o_ai[...] = ai
        o_br[...] = bbr
        o_bi[...] = bbi

    return pl.pallas_call(
        body, name="s5_params",
        out_shape=(SDS(lam_re.shape, F32), SDS(lam_re.shape, F32), SDS(bt_re.shape, F32), SDS(bt_re.shape, F32)),
    )(log_dt, lam_re, lam_im, bt_re, bt_im)


def _s5_params_bwd(layer, log_dt, lam_re, lam_im, bt_re, bt_im, g_ar, g_ai, g_br, g_bi):
    def body(ld, lr, li, br, bi, car, cai, cbr, cbi, o_ld, o_lr, o_li, o_br, o_bi):
        _, vjp = jax.vjp(_s5_param_fn, ld[...], lr[...], li[...], br[...], bi[...])
        d_ld, d_lr, d_li, d_br, d_bi = vjp((car[...], cai[...], cbr[...], cbi[...]))
        o_ld[...] = d_ld
        o_lr[...] = d_lr
        o_li[...] = d_li
        o_br[...] = d_br
        o_bi[...] = d_bi

    one = lambda shape: pl.BlockSpec((None,) + shape, lambda i: (layer,) + (0,) * len(shape))
    whole = lambda shape: _const(shape)
    vec, lam, mat = (N_GROUP, 1), (N_GROUP, STATE), (N_GROUP, GROUP_W, STATE)
    return pl.pallas_call(
        body, name=f"s5_params_bwd_l{layer}", grid=(1,),
        in_specs=[one(vec), one(lam), one(lam), one(mat), one(mat), whole(lam), whole(lam), whole(mat), whole(mat)],
        out_specs=(whole(vec), whole(lam), whole(lam), whole(mat), whole(mat)),
        out_shape=(SDS(vec, F32), SDS(lam, F32), SDS(lam, F32), SDS(mat, F32), SDS(mat, F32)),
    )(log_dt, lam_re, lam_im, bt_re, bt_im, g_ar, g_ai, g_br, g_bi)


def _norm_proj(layer, x, norm_g, wg_in, b_in, carry=None):
    n_w = len(wg_in)

    def body(x_ref, g_ref, b_ref, *refs):
        w_refs, o_ref = refs[:n_w], refs[n_w]
        _, xn = _rms(x_ref[...])
        h = (xn * g_ref[layer:layer + 1, :]).astype(BF16)
        for k in range(N_DEV):
            cols = slice(k * WIDTH, (k + 1) * WIDTH)
            acc = b_ref[layer:layer + 1, cols]
            row = 0
            for w_ref in w_refs:
                rows = w_ref.shape[1]
                acc = acc + _dot(h[:, row:row + rows], w_ref[k])
                row += rows
            o_ref[:, cols] = acc

    (proj,), moved = _pcall(
        body, name=f"norm_proj_l{layer}",
        out_shape=[SDS((SEQ, N_IN), F32)],
        grid=(SEQ // TILE_M,),
        in_specs=[pl.BlockSpec((TILE_M, D_MODEL), lambda i: (i, 0)),
                  _const((DEPTH, D_MODEL)),
                  _const((DEPTH, N_IN))] + [_const(w.shape) for w in wg_in],
        out_specs=[pl.BlockSpec((TILE_M, N_IN), lambda i: (i, 0))],
        args=[x, norm_g, b_in, *wg_in], sem=("parallel",), carry=carry)
    return proj, moved


TIME_BLK = 512
N_TBLK = SEQ // TIME_BLK
N_PANEL = CH_S // LANES
STATE_SHAPE = (N_PANEL, SEQ * SUBLANES, LANES)


def _s5_layer_specs(layer):
    mat = lambda: pl.BlockSpec((None, N_GROUP, GROUP_W, STATE), lambda i: (layer, 0, 0, 0))
    ab = lambda: pl.BlockSpec((None, N_GROUP, STATE), lambda i: (layer, 0, 0))
    return [mat(), mat(), mat(), mat(), ab(), ab(), _const((DEPTH, WIDTH))]


def _s5_layer_scratch():
    return [pltpu.VMEM((N_CHUNK, CH_W, CH_S), BF16)] * 4 + [pltpu.VMEM((8, CH_S), F32)] * 2


def _s5_layer_fill(btre_ref, btim_ref, cre_ref, cim_ref, are_ref, aim_ref, bdre, bdim, ctre, ctim, a1, a2):
    for m in (bdre, bdim, ctre, ctim):
        m[...] = jnp.zeros_like(m)
    for grp in range(N_GROUP):
        k, g = divmod(grp, CH_G)
        rows = slice(g * GROUP_W, (g + 1) * GROUP_W)
        cols = slice(g * STATE, (g + 1) * STATE)
        bdre[k, rows, cols] = btre_ref[grp].astype(BF16)
        bdim[k, rows, cols] = btim_ref[grp].astype(BF16)
        ctre[k, rows, cols] = cre_ref[grp].astype(BF16)
        ctim[k, rows, cols] = cim_ref[grp].astype(BF16)
        ar = are_ref[grp:grp + 1, :]
        ai = aim_ref[grp:grp + 1, :]
        a1[k:k + 1, cols] = ar
        a1[N_CHUNK + k:N_CHUNK + k + 1, cols] = ar
        a2[k:k + 1, cols] = -ai
        a2[N_CHUNK + k:N_CHUNK + k + 1, cols] = ai


SCAN_UNROLL = 16


def _panels(tile):
    return [tile[:, p * LANES:(p + 1) * LANES] for p in range(N_PANEL)]


def _rows_load(ref, row):
    return jnp.concatenate([ref[p, pl.ds(row, TIME_BLK, stride=SUBLANES), :] for p in range(N_PANEL)], axis=1)


def _rows_store(ref, row, val):
    for p in range(N_PANEL):
        ref[p, pl.ds(row, TIME_BLK, stride=SUBLANES), :] = val[:, p * LANES:(p + 1) * LANES]


def _s5_scan_fwd(layer, proj, bbt_re, bbt_im, c_re, c_im, abar_re, abar_im, d_skip, carry=None):
    def body(u_ref, btre_ref, btim_ref, cre_ref, cim_ref, are_ref, aim_ref, d_ref, s_ref, y_ref,
             bdre, bdim, ctre, ctim, a1, a2, state):
        @pl.when(pl.program_id(0) == 0)
        def _():
            _s5_layer_fill(btre_ref, btim_ref, cre_ref, cim_ref, are_ref, aim_ref, bdre, bdim, ctre, ctim, a1, a2)
            state[...] = jnp.zeros_like(state)

        for k in range(N_CHUNK):
            ub = u_ref[:, k * CH_W:(k + 1) * CH_W].astype(BF16)
            _rows_store(s_ref, k, _dot(ub, bdre[k]))
            _rows_store(s_ref, N_CHUNK + k, _dot(ub, bdim[k]))
        m1 = _panels(a1[...])
        m2 = _panels(a2[...])

        def steps(n, tile):
            for r in range(SCAN_UNROLL):
                rows = pl.ds(pl.multiple_of((n * SCAN_UNROLL + r) * 8, 8), 8)
                tile = [m1[p] * tile[p] + m2[p] * pltpu.roll(tile[p], N_CHUNK, 0) + s_ref[p, rows, :]
                        for p in range(N_PANEL)]
                for p in range(N_PANEL):
                    s_ref[p, rows, :] = tile[p]
            return tile

        tile = lax.fori_loop(0, TIME_BLK // SCAN_UNROLL, steps, _panels(state[...]))
        state[...] = jnp.concatenate(tile, axis=1)
        d = d_ref[layer:layer + 1, :]
        for k in range(N_CHUNK):
            cols = slice(k * CH_W, (k + 1) * CH_W)
            y = (_dot_nt(_rows_load(s_ref, k).astype(BF16), ctre[k])
                 - _dot_nt(_rows_load(s_ref, N_CHUNK + k).astype(BF16), ctim[k]))
            y_ref[:, cols] = y + d[:, cols] * u_ref[:, cols]

    return _pcall(
        body, name=f"s5_fwd_l{layer}",
        out_shape=(SDS(STATE_SHAPE, F32), SDS((SEQ, WIDTH), F32)),
        grid=(N_TBLK,),
        in_specs=[pl.BlockSpec((TIME_BLK, WIDTH), lambda i: (i, 0))] + _s5_layer_specs(layer),
        out_specs=(pl.BlockSpec((N_PANEL, TIME_BLK * SUBLANES, LANES), lambda i: (0, i, 0)),
                   pl.BlockSpec((TIME_BLK, WIDTH), lambda i: (i, 0))),
        scratch_shapes=_s5_layer_scratch() + [pltpu.VMEM((8, CH_S), F32)],
        args=[proj, bbt_re, bbt_im, c_re, c_im, abar_re, abar_im, d_skip], sem=("arbitrary",), carry=carry)


def _s5_scan_bwd(layer, dy0, proj, states, bbt_re, bbt_im, c_re, c_im, abar_re, abar_im, d_skip, dproj,
                 carry=None):
    def body(dy_ref, u_ref, s_ref, sprev_ref, btre_ref, btim_ref, cre_ref, cim_ref, are_ref, aim_ref, d_ref, _,
             du_ref, gbre_ref, gbim_ref, gcre_ref, gcim_ref, gare_ref, gaim_ref, gd_ref,
             lam_ref, bdre, bdim, ctre, ctim, a1, a2, state, acc1, acc2, gbre, gbim, gcre, gcim, gd):
        step_id = pl.program_id(0)

        @pl.when(step_id == 0)
        def _():
            _s5_layer_fill(btre_ref, btim_ref, cre_ref, cim_ref, are_ref, aim_ref, bdre, bdim, ctre, ctim, a1, a2)
            for r in (state, acc1, acc2, gbre, gbim, gcre, gcim, gd):
                r[...] = jnp.zeros_like(r)

        for k in range(N_CHUNK):
            dyb = dy_ref[:, k * CH_W:(k + 1) * CH_W].astype(BF16)
            _rows_store(lam_ref, k, _dot(dyb, ctre[k]))
            _rows_store(lam_ref, N_CHUNK + k, -_dot(dyb, ctim[k]))
            gcre[k] += _dot_tn(dyb, _rows_load(s_ref, k).astype(BF16))
            gcim[k] -= _dot_tn(dyb, _rows_load(s_ref, N_CHUNK + k).astype(BF16))

        m1 = _panels(a1[...])
        m2 = _panels(-a2[...])
        has_before = (step_id < N_TBLK - 1).astype(F32)

        def one(t8, c, first_token):
            tile, swapped, p1, p2 = c
            rows = pl.ds(t8, 8)
            tile = [m1[p] * tile[p] + m2[p] * swapped[p] + lam_ref[p, rows, :] for p in range(N_PANEL)]
            swapped = [pltpu.roll(tile[p], N_CHUNK, 0) for p in range(N_PANEL)]
            for p in range(N_PANEL):
                lam_ref[p, rows, :] = tile[p]
            if first_token:
                before = [sprev_ref[p] * has_before for p in range(N_PANEL)]
            else:
                before = [s_ref[p, pl.ds(t8 - 8, 8), :] for p in range(N_PANEL)]
            p1 = [p1[p] + tile[p] * before[p] for p in range(N_PANEL)]
            p2 = [p2[p] + swapped[p] * before[p] for p in range(N_PANEL)]
            return tile, swapped, p1, p2

        def steps(n, c):
            for r in range(SCAN_UNROLL):
                t8 = pl.multiple_of((TIME_BLK - 1 - (n * SCAN_UNROLL + r)) * 8, 8)
                c = one(t8, c, False)
            return c

        tile0 = _panels(state[...])
        c = (tile0, [pltpu.roll(t, N_CHUNK, 0) for t in tile0], _panels(acc1[...]), _panels(acc2[...]))
        c = lax.fori_loop(0, TIME_BLK // SCAN_UNROLL - 1, steps, c)
        for r in range(SCAN_UNROLL - 1, -1, -1):
            c = one(r * 8, c, r == 0)
        state[...] = jnp.concatenate(c[0], axis=1)
        acc1[...] = jnp.concatenate(c[2], axis=1)
        acc2[...] = jnp.concatenate(c[3], axis=1)

        d = d_ref[layer:layer + 1, :]
        for k in range(N_CHUNK):
            cols = slice(k * CH_W, (k + 1) * CH_W)
            lrb = _rows_load(lam_ref, k).astype(BF16)
            lib = _rows_load(lam_ref, N_CHUNK + k).astype(BF16)
            u = u_ref[:, cols]
            ub = u.astype(BF16)
            dy = dy_ref[:, cols]
            du = dy * d[:, cols] + _dot_nt(lrb, bdre[k]) + _dot_nt(lib, bdim[k])
            du_ref[:, cols] = du.astype(BF16)
            gbre[k] += _dot_tn(ub, lrb)
            gbim[k] += _dot_tn(ub, lib)
        gd[...] += jnp.sum(dy_ref[...] * u_ref[...], axis=0, keepdims=True)

        @pl.when(step_id == N_TBLK - 1)
        def _():
            gd_ref[...] = gd[...]
            ga_re = acc1[0:N_CHUNK, :] + acc1[N_CHUNK:, :]
            ga_im = acc2[0:N_CHUNK, :] - acc2[N_CHUNK:, :]
            for grp in range(N_GROUP):
                k, g = divmod(grp, CH_G)
                rows = slice(g * GROUP_W, (g + 1) * GROUP_W)
                cols = slice(g * STATE, (g + 1) * STATE)
                gcre_ref[grp] = gcre[k, rows, cols]
                gcim_ref[grp] = gcim[k, rows, cols]
                gbre_ref[grp] = gbre[k, rows, cols]
                gbim_ref[grp] = gbim[k, rows, cols]
                gare_ref[grp:grp + 1, :] = ga_re[k:k + 1, cols]
                gaim_ref[grp:grp + 1, :] = ga_im[k:k + 1, cols]

    back = lambda i: N_TBLK - 1 - i
    tok = lambda: pl.BlockSpec((TIME_BLK, WIDTH), lambda i: (back(i), 0))
    mat = lambda: _const((N_GROUP, GROUP_W, STATE))
    acc_mat = pltpu.VMEM((N_CHUNK, CH_W, CH_S), F32)
    return _pcall(
        body, name=f"s5_bwd_l{layer}",
        out_shape=(SDS((SEQ, N_IN), BF16), SDS((N_GROUP, GROUP_W, STATE), F32), SDS((N_GROUP, GROUP_W, STATE), F32),
                   SDS((N_GROUP, GROUP_W, STATE), F32), SDS((N_GROUP, GROUP_W, STATE), F32),
                   SDS((N_GROUP, STATE), F32), SDS((N_GROUP, STATE), F32), SDS((1, WIDTH), F32)),
        grid=(N_TBLK,),
        in_specs=[tok(), tok(),
                  pl.BlockSpec((N_PANEL, TIME_BLK * SUBLANES, LANES), lambda i: (0, back(i), 0)),
                  pl.BlockSpec((N_PANEL, SUBLANES, LANES), lambda i: (0, jnp.maximum(back(i) * TIME_BLK - 1, 0), 0))]
        + _s5_layer_specs(layer) + [ANY],
        out_specs=(tok(), mat(), mat(), mat(), mat(), _const((N_GROUP, STATE)), _const((N_GROUP, STATE)),
                   _const((1, WIDTH))),
        scratch_shapes=[pltpu.VMEM((N_PANEL, TIME_BLK * SUBLANES, LANES), F32)] + _s5_layer_scratch()
        + [pltpu.VMEM((8, CH_S), F32)] * 3 + [acc_mat] * 4 + [pltpu.VMEM((1, WIDTH), F32)],
        args=[dy0, proj, states, states, bbt_re, bbt_im, c_re, c_im, abar_re, abar_im, d_skip, dproj],
        aliases={11: 0}, sem=("arbitrary",), limit=VMEM_LIMIT_BIG, carry=carry)


def _pool_counts(win):
    t = lax.broadcasted_iota(jnp.int32, (SEQ, POOL_GROUP), 0)
    return t, jnp.minimum(t + 1, win).astype(F32)


def _pool_fwd(layer, proj):
    def body(u_ref, o_ref):
        for gi, win in enumerate(POOL_WINDOWS):
            cols = slice(gi * POOL_GROUP, (gi + 1) * POOL_GROUP)
            u = u_ref[:, cols]
            t, count = _pool_counts(win)
            acc = u
            k = 1
            while k < win:
                acc = acc + jnp.where(t >= k, pltpu.roll(acc, k, 0), 0.0)
                k *= 2
            o_ref[:, cols] = acc / count - u

    return pl.pallas_call(
        body, name=f"pool_fwd_l{layer}",
        out_shape=SDS((SEQ, WIDTH), F32),
        grid=(1,),
        in_specs=[pl.BlockSpec((SEQ, WIDTH), lambda i: (0, 2))],
        out_specs=pl.BlockSpec((SEQ, WIDTH), lambda i: (0, 0)),
        compiler_params=_cp(("arbitrary",)),
    )(proj)


def _gelu_parts(y0):
    t = jnp.tanh(GELU_C * (y0 + GELU_A * (y0 * y0 * y0)))
    return t, 0.5 * y0 * (1.0 + t)


def _mix_forward(layer, p_ref, y0_ref, pooled_ref, wglu_ref, bglu_ref, pw_ref, scale_ref, wa_ref, wb_ref):
    za = p_ref[:, WIDTH:2 * WIDTH]
    zb = p_ref[:, 3 * WIDTH:4 * WIDTH]
    ga = p_ref[:, 4 * WIDTH:4 * WIDTH + D_MODEL]
    gb = p_ref[:, 4 * WIDTH + D_MODEL:]
    y0 = y0_ref[...]
    t, y1 = _gelu_parts(y0)
    y1b = y1.astype(BF16)
    q = _dot(y1b, wglu_ref[...].reshape(WIDTH, WIDTH)) + bglu_ref[layer:layer + 1, :]
    sq = _sig(q)
    y2 = y1 * sq
    sza = _sig(za)
    silu_za = za * sza
    ya = y2 * silu_za
    pooled = pooled_ref[...]
    mixed = jnp.concatenate(
        [_dot(pooled[:, g * POOL_GROUP:(g + 1) * POOL_GROUP].astype(BF16), pw_ref[g].astype(BF16))
         for g in range(len(POOL_WINDOWS))], axis=1)
    szb = _sig(zb)
    silu_zb = zb * szb
    scale = scale_ref[layer:layer + 1, :]
    ms = mixed * scale
    yb = ms * silu_zb
    yab = ya.astype(BF16)
    ybb = yb.astype(BF16)
    ma = _dot(yab, wa_ref[...])
    mb = _dot(ybb, wb_ref[...])
    sga = _sig(ga)
    sgb = _sig(gb)
    merged = sga * ma + sgb * mb
    return dict(za=za, zb=zb, y0=y0, t=t, y1=y1, y1b=y1b, sq=sq, y2=y2, sza=sza, silu_za=silu_za,
                pooled=pooled, mixed=mixed, szb=szb, silu_zb=silu_zb, scale=scale, ms=ms, yab=yab, ybb=ybb,
                ma=ma, mb=mb, sga=sga, sgb=sgb, merged=merged)


def _mix_weight_specs(layer):
    return [_const((N_DEV, WIDTH // N_DEV, WIDTH)),
            _const((DEPTH, WIDTH)),
            pl.BlockSpec((None, 4, POOL_GROUP, POOL_GROUP), lambda i: (layer, 0, 0, 0)),
            _const((DEPTH, WIDTH)),
            _const((WIDTH, D_MODEL)),
            _const((WIDTH, D_MODEL)),
            _const((N_DEV, D_MODEL // N_DEV, D_MODEL))]


def _loss_head(x, t_ref, g_ref, dx_ref, loss_ref, gg_ref):
    @pl.when(pl.program_id(0) == 0)
    def _():
        loss_ref[...] = jnp.zeros_like(loss_ref)
        gg_ref[...] = jnp.zeros_like(gg_ref)

    g = g_ref[...]
    rs, xn = _rms(x)
    err = xn * g - t_ref[...]
    loss_ref[...] += 0.5 * jnp.sum(jnp.mean(err * err, axis=-1, keepdims=True), axis=0, keepdims=True)
    dy = err * (1.0 / D_MODEL)
    gg_ref[...] += jnp.sum(dy * xn, axis=0, keepdims=True)
    dxn = dy * g
    dx_ref[...] = rs * (dxn - xn * jnp.mean(dxn * xn, axis=-1, keepdims=True))


def _mix_fwd(layer, x, proj, y0, pooled, wg_glu, b_glu, pool_w, pool_scale, wg_a, wg_b, wg_out, carry=None,
             head=None):
    def body(x_ref, p_ref, y0_ref, pooled_ref, wglu_ref, bglu_ref, pw_ref, scale_ref, wa_ref, wb_ref,
             wout_ref, *rest):
        f = _mix_forward(layer, p_ref, y0_ref, pooled_ref, wglu_ref, bglu_ref, pw_ref, scale_ref, wa_ref, wb_ref)
        wout = wout_ref[...].reshape(D_MODEL, D_MODEL)
        x_next = x_ref[...] + _dot(f["merged"].astype(BF16), wout)
        if head is None:
            rest[0][...] = x_next
        else:
            _loss_head(x_next, *rest)

    tile = lambda: pl.BlockSpec((TILE_M, D_MODEL), lambda i: (i, 0))
    if head is None:
        extra, out_shape, out_specs = [], [SDS((SEQ, D_MODEL), F32)], [tile()]
    else:
        extra = list(head)
        out_shape = [SDS((SEQ, D_MODEL), F32), SDS((1, 1), F32), SDS((1, D_MODEL), F32)]
        out_specs = [tile(), _const((1, 1)), _const((1, D_MODEL))]
    return _pcall(
        body, name=f"mix_fwd_l{layer}",
        out_shape=out_shape,
        grid=(SEQ // TILE_M,),
        in_specs=[tile(),
                  pl.BlockSpec((TILE_M, N_IN), lambda i: (i, 0)),
                  pl.BlockSpec((TILE_M, WIDTH), lambda i: (i, 0)),
                  pl.BlockSpec((TILE_M, WIDTH), lambda i: (i, 0))] + _mix_weight_specs(layer)
        + ([tile(), _const((1, D_MODEL))] if head else []),
        out_specs=out_specs,
        args=[x, proj, y0, pooled, wg_glu, b_glu, pool_w, pool_scale, wg_a, wg_b, wg_out] + extra,
        sem=("parallel",) if head is None else ("arbitrary",), carry=carry)


def _big_shapes():
    return dict(w_out=(DEPTH, N_DEV, D_MODEL // N_DEV, D_MODEL), w_branch_a=(DEPTH, N_DEV, WIDTH, D_MODEL // N_DEV),
                w_branch_b=(DEPTH, N_DEV, WIDTH, D_MODEL // N_DEV), ssm_w_glu=(DEPTH, N_DEV, WIDTH // N_DEV, WIDTH),
                w_in=(DEPTH, N_DEV, D_MODEL, WIDTH))


def _mix_bwd(layer, dx_next, proj, y0, pooled, wg_glu, b_glu, pool_w, pool_scale, wg_a, wg_b, wg_out, prev,
             carry=None):
    n_k = N_DEV
    n_prev = 0 if prev is None else len(prev)

    def body(*refs):
        (dx_ref, p_ref, y0_ref, pooled_ref, wglu_ref, bglu_ref, pw_ref, scale_ref, wa_ref, wb_ref,
         wout_ref) = refs[:11]
        (dproj_ref, dy0_ref, dpooled_ref, gwout_ref, gwa_ref, gwb_ref, gwglu_ref, gpw_ref,
         gscale_ref, gbglu_ref) = refs[11 + n_prev:]

        @pl.when(pl.program_id(0) == 0)
        def _():
            for r in (gwout_ref, gwa_ref, gwb_ref, gwglu_ref, gpw_ref, gscale_ref, gbglu_ref):
                r[...] = jnp.zeros_like(r)

        f = _mix_forward(layer, p_ref, y0_ref, pooled_ref, wglu_ref, bglu_ref, pw_ref, scale_ref, wa_ref, wb_ref)
        wglu = wglu_ref[...].reshape(WIDTH, WIDTH)
        wout = wout_ref[...].reshape(D_MODEL, D_MODEL)
        blk = D_MODEL // n_k
        dxb = dx_ref[...].astype(BF16)
        dmerged = _dot_nt(dxb, wout)
        gwout = _dot_tn(f["merged"].astype(BF16), dxb)
        for k in range(n_k):
            gwout_ref[_slot(k)] += gwout[k * blk:(k + 1) * blk, :]
        dma = dmerged * f["sga"]
        dmb = dmerged * f["sgb"]
        dga = dmerged * f["ma"] * f["sga"] * (1.0 - f["sga"])
        dgb = dmerged * f["mb"] * f["sgb"] * (1.0 - f["sgb"])
        dmab = dma.astype(BF16)
        dmbb = dmb.astype(BF16)
        dya = _dot_nt(dmab, wa_ref[...])
        dyb = _dot_nt(dmbb, wb_ref[...])
        gwa = _dot_tn(f["yab"], dmab)
        gwb = _dot_tn(f["ybb"], dmbb)
        for k in range(n_k):
            gwa_ref[_slot(k)] += gwa[:, k * blk:(k + 1) * blk]
            gwb_ref[_slot(k)] += gwb[:, k * blk:(k + 1) * blk]
        zb, szb = f["zb"], f["szb"]
        dzb = dyb * f["ms"] * (szb * (1.0 + zb * (1.0 - szb)))
        dms = dyb * f["silu_zb"]
        gscale_ref[...] += jnp.sum(dms * f["mixed"], axis=0, keepdims=True)
        dmixed = (dms * f["scale"]).astype(BF16)
        pooled = f["pooled"]
        for g in range(len(POOL_WINDOWS)):
            cols = slice(g * POOL_GROUP, (g + 1) * POOL_GROUP)
            dpooled_ref[:, cols] = _dot_nt(dmixed[:, cols], pw_ref[g].astype(BF16))
            gpw_ref[g] += _dot_tn(pooled[:, cols].astype(BF16), dmixed[:, cols])
        za, sza = f["za"], f["sza"]
        dza = dya * f["y2"] * (sza * (1.0 + za * (1.0 - sza)))
        dy2 = dya * f["silu_za"]
        sq = f["sq"]
        dq = dy2 * f["y1"] * sq * (1.0 - sq)
        dqb = dq.astype(BF16)
        dy1 = dy2 * sq + _dot_nt(dqb, wglu)
        gwglu = _dot_tn(f["y1b"], dqb)
        rblk = WIDTH // n_k
        for k in range(n_k):
            gwglu_ref[_slot(k)] += gwglu[k * rblk:(k + 1) * rblk, :]
        gbglu_ref[...] += jnp.sum(dq, axis=0, keepdims=True)
        y0, t = f["y0"], f["t"]
        dgelu = 0.5 * (1.0 + t) + 0.5 * y0 * (1.0 - t * t) * (GELU_C * (1.0 + 3.0 * GELU_A * y0 * y0))
        dy0_ref[...] = dy1 * dgelu
        zeros = jnp.zeros((TILE_M, WIDTH), BF16)
        dproj_ref[:, 0:WIDTH] = zeros
        dproj_ref[:, WIDTH:2 * WIDTH] = dza.astype(BF16)
        dproj_ref[:, 2 * WIDTH:3 * WIDTH] = zeros
        dproj_ref[:, 3 * WIDTH:4 * WIDTH] = dzb.astype(BF16)
        dproj_ref[:, 4 * WIDTH:4 * WIDTH + D_MODEL] = dga.astype(BF16)
        dproj_ref[:, 4 * WIDTH + D_MODEL:] = dgb.astype(BF16)

    tile = lambda w: pl.BlockSpec((TILE_M, w), lambda i: (i, 0))
    shapes = _big_shapes()
    big = ["w_out", "w_branch_a", "w_branch_b", "ssm_w_glu"]
    slab = lambda n: pl.BlockSpec((None,) + shapes[n][1:], lambda i: (layer, 0, 0, 0))
    args = [dx_next, proj, y0, pooled, wg_glu, b_glu, pool_w, pool_scale, wg_a, wg_b, wg_out]
    return _pcall(
        body, name=f"mix_bwd_l{layer}",
        out_shape=(SDS((SEQ, N_IN), BF16), SDS((SEQ, WIDTH), F32), SDS((SEQ, WIDTH), F32))
        + tuple(SDS(shapes[n], F32) for n in big)
        + (SDS((4, POOL_GROUP, POOL_GROUP), F32), SDS((1, WIDTH), F32), SDS((1, WIDTH), F32)),
        grid=(SEQ // TILE_M,),
        in_specs=[tile(D_MODEL), tile(N_IN), tile(WIDTH), tile(WIDTH)] + _mix_weight_specs(layer) + [ANY] * n_prev,
        out_specs=(tile(N_IN), tile(WIDTH), tile(WIDTH)) + tuple(slab(n) for n in big)
        + (_const((4, POOL_GROUP, POOL_GROUP)), _const((1, WIDTH)), _const((1, WIDTH))),
        args=args + list(prev or ()),
        aliases={len(args) + i: 3 + i for i in range(n_prev)},
        sem=("arbitrary",), limit=VMEM_LIMIT_BIG, carry=carry)


def _pool_bwd(layer, dpooled, dproj):
    def body(dp_ref, _, o_ref):
        for gi, win in enumerate(POOL_WINDOWS):
            cols = slice(gi * POOL_GROUP, (gi + 1) * POOL_GROUP)
            dp = dp_ref[:, cols]
            t, count = _pool_counts(win)
            e = dp / count
            acc = e
            k = 1
            while k < win:
                acc = acc + jnp.where(t < SEQ - k, pltpu.roll(acc, SEQ - k, 0), 0.0)
                k *= 2
            o_ref[:, cols] = (acc - dp).astype(BF16)

    return pl.pallas_call(
        body, name=f"pool_bwd_l{layer}",
        out_shape=SDS((SEQ, N_IN), BF16),
        grid=(1,),
        in_specs=[pl.BlockSpec((SEQ, WIDTH), lambda i: (0, 0)), ANY],
        out_specs=pl.BlockSpec((SEQ, WIDTH), lambda i: (0, 2)),
        input_output_aliases={1: 0},
        compiler_params=_cp(("arbitrary",)),
    )(dpooled, dproj)


def _proj_wgrad(layer, x, norm_g, dproj, prev, carry=None):
    tm = 512
    n_prev = 0 if prev is None else 1

    def body(*refs):
        x_ref, g_ref, dp_ref = refs[:3]
        gw_ref, gb_ref, ht_ref = refs[3 + n_prev:]
        n, t = pl.program_id(0), pl.program_id(1)

        @pl.when(t == 0)
        def _():
            gw_ref[...] = jnp.zeros_like(gw_ref)
            gb_ref[...] = jnp.zeros_like(gb_ref)

        @pl.when(n == 0)
        def _():
            _, xn = _rms(x_ref[...])
            ht_ref[t] = (xn * g_ref[layer:layer + 1, :]).T.astype(BF16)

        dp = dp_ref[...]
        gw_ref[...] += _dot(ht_ref[t], dp)
        gb_ref[...] += jnp.sum(dp.astype(F32), axis=0, keepdims=True)

    return _pcall(
        body, name=f"proj_wgrad_l{layer}",
        out_shape=(SDS(_big_shapes()["w_in"], F32), SDS((1, N_IN), F32)),
        grid=(N_DEV, SEQ // tm),
        in_specs=[pl.BlockSpec((tm, D_MODEL), lambda n, t: (jnp.where(n == 0, t, 0), 0)),
                  _const((DEPTH, D_MODEL)),
                  pl.BlockSpec((tm, WIDTH), lambda n, t: (t, n))] + [ANY] * n_prev,
        out_specs=(pl.BlockSpec((None, None, D_MODEL, WIDTH), lambda n, t: (layer, _slot(n), 0, 0)),
                   pl.BlockSpec((1, WIDTH), lambda n, t: (0, n))),
        scratch_shapes=[pltpu.VMEM((SEQ // tm, D_MODEL, tm), BF16)],
        args=[x, norm_g, dproj] + ([prev] if n_prev else []),
        aliases={3: 0} if n_prev else {}, sem=("arbitrary", "arbitrary"), carry=carry)


def _proj_dgrad(layer, dx_next, x, norm_g, dproj, wg_in, carry=None):
    n_w = len(wg_in)

    def body(dxn_ref, x_ref, g_ref, dp_ref, *refs):
        w_refs, (dx_ref, gg_ref) = refs[:n_w], refs[n_w:]

        @pl.when(pl.program_id(0) == 0)
        def _():
            gg_ref[...] = jnp.zeros_like(gg_ref)

        parts = []
        for w_ref in w_refs:
            part = jnp.zeros((TILE_M, w_ref.shape[1]), F32)
            for k in range(N_DEV):
                part = part + _dot_nt(dp_ref[:, k * WIDTH:(k + 1) * WIDTH], w_ref[k])
            parts.append(part)
        dh = parts[0] if n_w == 1 else jnp.concatenate(parts, axis=1)
        rs, xn = _rms(x_ref[...])
        gg_ref[...] += jnp.sum(dh * xn, axis=0, keepdims=True)
        dxn = dh * g_ref[layer:layer + 1, :]
        dx_ref[...] = dxn_ref[...] + rs * (dxn - xn * jnp.mean(dxn * xn, axis=-1, keepdims=True))

    return _pcall(
        body, name=f"proj_dgrad_l{layer}",
        out_shape=(SDS((SEQ, D_MODEL), F32), SDS((1, D_MODEL), F32)),
        grid=(SEQ // TILE_M,),
        in_specs=[pl.BlockSpec((TILE_M, D_MODEL), lambda i: (i, 0)),
                  pl.BlockSpec((TILE_M, D_MODEL), lambda i: (i, 0)),
                  _const((DEPTH, D_MODEL)),
                  pl.BlockSpec((TILE_M, N_IN), lambda i: (i, 0))] + [_const(w.shape) for w in wg_in],
        out_specs=(pl.BlockSpec((TILE_M, D_MODEL), lambda i: (i, 0)), _const((1, D_MODEL))),
        args=[dx_next, x, norm_g, dproj, *wg_in], sem=("arbitrary",), carry=carry)


def _my_place():
    return lax.axis_index("x"), lax.axis_index("y"), lax.axis_index("c")


def _gather_plan(shards, layer, by_columns=(), rows_of=None):
    n = len(shards)

    def parts(ins, outs, sems):
        send_sems, recv_sems, local_sems = sems
        x, y, c = _my_place()
        chips = [(1 - x, y), (x, 1 - y), (1 - x, 1 - y)]

        def source(t):
            return ins[t].at[layer] if rows_of is None else ins[t].at[layer, pl.ds(*rows_of)]

        def rows(t, place):
            px, py, pc = place
            index = 4 * px + 2 * py + pc
            if t in by_columns:
                width = shards[t].shape[2]
                return outs[t].at[:, pl.ds(pl.multiple_of(index * width, LANES), width)]
            return outs[t].at[index]

        def copy(t, k, block, to, from_src=False):
            return pltpu.make_async_remote_copy(
                src_ref=source(t) if from_src else rows(t, block), dst_ref=rows(t, block),
                send_sem=send_sems.at[7 * t + k], recv_sem=recv_sems.at[7 * t + k], device_id=to,
                device_id_type=MESH)

        def mine(t):
            return pltpu.make_async_copy(source(t), rows(t, (x, y, c)), local_sems.at[t])

        return (x, y, c), chips, copy, mine

    def start(ins, outs, sems):
        me, chips, copy, mine = parts(ins, outs, sems)
        x, y, c = me
        for t in range(n):
            mine(t).start()
            copy(t, 0, me, (x, y, 1 - c), from_src=True).start()
            for j, chip in enumerate(chips):
                copy(t, 1 + j, me, (*chip, c), from_src=True).start()

    def relay(ins, outs, sems):
        me, chips, copy, mine = parts(ins, outs, sems)
        x, y, c = me
        for t in range(n):
            for j, chip in enumerate(chips):
                copy(t, 1 + j, (*chip, c), me).wait_recv()
                copy(t, 4 + j, (*chip, c), (x, y, 1 - c)).start()

    def finish(ins, outs, sems):
        me, chips, copy, mine = parts(ins, outs, sems)
        x, y, c = me
        sibling = (x, y, 1 - c)
        for t in range(n):
            copy(t, 0, sibling, me).wait_recv()
            for j, chip in enumerate(chips):
                copy(t, 4 + j, (*chip, 1 - c), me).wait_recv()
            for k in range(7):
                copy(t, k, me, sibling, from_src=k < 4).wait_send()
            mine(t).wait()

    n_rows = lambda a: a.shape[1] if rows_of is None else rows_of[1]
    out_shape = [SDS((a.shape[1], N_DEV * a.shape[2]) if t in by_columns else (N_DEV, n_rows(a), a.shape[2]), a.dtype)
                 for t, a in enumerate(shards)]
    sems = [pltpu.SemaphoreType.DMA((7 * n,)), pltpu.SemaphoreType.DMA((7 * n,)), pltpu.SemaphoreType.DMA((n,))]
    return _Carried(shards, out_shape, sems, start, finish, relay)


class _Carried:
    def __init__(self, ins, out_shape, sems, start, finish, relay=None):
        self.ins, self.out_shape, self.sems = list(ins), list(out_shape), list(sems)
        self.start, self.finish = start, finish
        self.relay = relay or (lambda ins, outs, sems: None)


def _pcall(body, *, name, grid, in_specs, out_specs, out_shape, args, scratch_shapes=(), aliases=None,
           sem=None, limit=VMEM_LIMIT, carry=None):
    out_shape, out_specs, scratch_shapes = list(out_shape), list(out_specs), list(scratch_shapes)
    n_in, n_out, n_scr = len(args), len(out_shape), len(scratch_shapes)
    if carry is None:
        kern, c_ins, c_out, c_sems = body, [], [], []
    else:
        c_ins, c_out, c_sems = carry.ins, carry.out_shape, carry.sems
        ci, co = len(c_ins), len(c_out)
        steps = tuple(grid)

        def kern(*refs):
            o0 = n_in + ci
            s0 = o0 + n_out + co
            mine = refs[:n_in] + refs[o0:o0 + n_out] + refs[s0:s0 + n_scr]
            theirs = (refs[n_in:o0], refs[o0 + n_out:s0], refs[s0 + n_scr:])
            first = pl.program_id(0) == 0
            last = pl.program_id(0) == steps[0] - 1
            for a in range(1, len(steps)):
                first = jnp.logical_and(first, pl.program_id(a) == 0)
                last = jnp.logical_and(last, pl.program_id(a) == steps[a] - 1)

            @pl.when(first)
            def _():
                carry.start(*theirs)

            @pl.when(last)
            def _():
                carry.relay(*theirs)

            body(*mine)

            @pl.when(last)
            def _():
                carry.finish(*theirs)

        sem = ("arbitrary",) * len(steps)
    res = pl.pallas_call(
        kern, name=name, grid=tuple(grid),
        in_specs=list(in_specs) + [ANY] * len(c_ins),
        out_specs=tuple(out_specs + [ANY] * len(c_out)),
        out_shape=tuple(out_shape + c_out),
        scratch_shapes=scratch_shapes + c_sems,
        input_output_aliases=aliases or {},
        compiler_params=_cp(sem, limit),
    )(*args, *c_ins)
    return res[:n_out], res[n_out:]


def _run_carried(name, carry):
    ci, co = len(carry.ins), len(carry.out_shape)

    def body(*refs):
        parts = (refs[:ci], refs[ci:ci + co], refs[ci + co:])
        carry.start(*parts)
        carry.relay(*parts)
        carry.finish(*parts)

    return pl.pallas_call(
        body, name=name, out_shape=tuple(carry.out_shape),
        in_specs=[ANY] * ci, out_specs=tuple([ANY] * co), scratch_shapes=carry.sems,
    )(*carry.ins)


def _sibling_plan(big, small):
    n = len(big)
    n_copies = 4 * n + len(small)

    def copies(ins, outs, sems):
        send_sems, recv_sems = sems
        x, y, c = _my_place()
        pairs = []
        for t, (_, layer) in enumerate(big):
            for s in range(4):
                pairs.append((ins[t].at[layer, pl.ds(4 * (1 - c) + s, 1)], outs[t].at[pl.ds(s, 1)]))
        pairs += list(zip(ins[n:], outs[n:]))
        return [pltpu.make_async_remote_copy(
            src_ref=src, dst_ref=dst, send_sem=send_sems.at[k], recv_sem=recv_sems.at[k],
            device_id=(x, y, 1 - c), device_id_type=MESH) for k, (src, dst) in enumerate(pairs)]

    def start(ins, outs, sems):
        for cp in copies(ins, outs, sems):
            cp.start()

    def finish(ins, outs, sems):
        for cp in copies(ins, outs, sems):
            cp.wait()

    out_shape = [SDS((4,) + a.shape[2:], a.dtype) for a, _ in big] + [SDS(a.shape, a.dtype) for a in small]
    sems = [pltpu.SemaphoreType.DMA((n_copies,)), pltpu.SemaphoreType.DMA((n_copies,))]
    return _Carried([a for a, _ in big] + list(small), out_shape, sems, start, finish)


def _chips_plan(big, small):
    n, n_small = len(big), len(small)
    max_rows = 512
    parts = [max(1, a.shape[1] // max_rows) for a in big]
    n_copies = 3 * (sum(parts) + n_small)

    def copies(ins, outs, sems, landing):
        send_sems, recv_sems, local_sems = sems
        x, y, c = _my_place()
        my_chip = 2 * x + y
        chips = [(1 - x, y), (x, 1 - y), (1 - x, 1 - y)]
        remote, local = [], []
        for chip in chips:
            to = 2 * chip[0] + chip[1]
            slot = to if landing else my_chip
            pairs = []
            for t in range(n):
                rows_per = big[t].shape[1] // parts[t]
                for p in range(parts[t]):
                    rows = pl.ds(p * rows_per, rows_per)
                    pairs.append((ins[t].at[to, rows], outs[t].at[slot, rows]))
            pairs += [(ins[t], outs[t].at[slot]) for t in range(n, n + n_small)]
            for src, dst in pairs:
                k = len(remote)
                remote.append(pltpu.make_async_remote_copy(
                    src_ref=src, dst_ref=dst, send_sem=send_sems.at[k], recv_sem=recv_sems.at[k],
                    device_id=(*chip, c), device_id_type=MESH))
        for t in range(n):
            local.append(pltpu.make_async_copy(ins[t].at[my_chip], outs[t].at[my_chip], local_sems.at[t]))
        for t in range(n, n + n_small):
            local.append(pltpu.make_async_copy(ins[t], outs[t].at[my_chip], local_sems.at[t]))
        return remote + local

    def start(ins, outs, sems):
        for cp in copies(ins, outs, sems, landing=False):
            cp.start()

    def finish(ins, outs, sems):
        for cp in copies(ins, outs, sems, landing=True):
            cp.wait()

    out_shape = [SDS(a.shape, a.dtype) for a in big] + [SDS((N_CHIP,) + a.shape, a.dtype) for a in small]
    sems = [pltpu.SemaphoreType.DMA((n_copies,)), pltpu.SemaphoreType.DMA((n_copies,)),
            pltpu.SemaphoreType.DMA((n + n_small,))]
    return _Carried(list(big) + list(small), out_shape, sems, start, finish)


def _all_plan(small):
    n = len(small)
    masks = [(m >> 2 & 1, m >> 1 & 1, m & 1) for m in range(1, N_DEV)]

    def copies(ins, outs, sems, landing):
        send_sems, recv_sems, local_sems = sems
        x, y, c = _my_place()
        me = 4 * x + 2 * y + c
        flip = lambda v, bit: 1 - v if bit else v
        remote = []
        for fx, fy, fc in masks:
            peer = (flip(x, fx), flip(y, fy), flip(c, fc))
            slot = 4 * peer[0] + 2 * peer[1] + peer[2] if landing else me
            for t in range(n):
                k = len(remote)
                remote.append(pltpu.make_async_remote_copy(
                    src_ref=ins[t], dst_ref=outs[t].at[slot], send_sem=send_sems.at[k], recv_sem=recv_sems.at[k],
                    device_id=peer, device_id_type=MESH))
        local = [pltpu.make_async_copy(ins[t], outs[t].at[me], local_sems.at[t]) for t in range(n)]
        return remote + local

    def start(ins, outs, sems):
        for cp in copies(ins, outs, sems, landing=False):
            cp.start()

    def finish(ins, outs, sems):
        for cp in copies(ins, outs, sems, landing=True):
            cp.wait()

    out_shape = [SDS((N_DEV,) + a.shape, a.dtype) for a in small]
    sems = [pltpu.SemaphoreType.DMA((7 * n,)), pltpu.SemaphoreType.DMA((7 * n,)), pltpu.SemaphoreType.DMA((n,))]
    return _Carried(list(small), out_shape, sems, start, finish)


def _join(*plans):
    plans = [p for p in plans if p is not None]
    if len(plans) <= 1:
        return plans[0] if plans else None

    def each(fn_name, ins, outs, sems):
        i = o = s = 0
        for p in plans:
            ni, no, ns = len(p.ins), len(p.out_shape), len(p.sems)
            getattr(p, fn_name)(ins[i:i + ni], outs[o:o + no], sems[s:s + ns])
            i, o, s = i + ni, o + no, s + ns

    return _Carried(sum((p.ins for p in plans), []), sum((p.out_shape for p in plans), []),
                    sum((p.sems for p in plans), []),
                    lambda i, o, s: each("start", i, o, s), lambda i, o, s: each("finish", i, o, s),
                    lambda i, o, s: each("relay", i, o, s))


def _row_block(rows, most=256):
    return min(rows, most)


def _add_own(tag, core, gs, layer, gots):
    n = len(gs)

    def body(core_ref, *refs):
        for a_ref, b_ref, o_ref in zip(refs[:n], refs[n:2 * n], refs[2 * n:]):
            o_ref[...] = (a_ref[...] + b_ref[...]).astype(o_ref.dtype)

    mine = lambda a: pl.BlockSpec((None, None) + a.shape[1:], lambda s, core: (layer, 4 * core[0] + s, 0, 0))
    theirs = lambda a: pl.BlockSpec((None,) + a.shape[1:], lambda s, core: (s, 0, 0))
    return pl.pallas_call(
        body, name=f"add_{tag}", out_shape=tuple(SDS(a.shape, BF16) for a in gots),
        grid_spec=pltpu.PrefetchScalarGridSpec(
            num_scalar_prefetch=1, grid=(4,),
            in_specs=[mine(a) for a in gots] + [theirs(a) for a in gots],
            out_specs=tuple(theirs(a) for a in gots)),
        compiler_params=_cp(("parallel",)),
    )(core, *gs, *gots)


def _add_lists(tag, own, got, grid=None, specs=None, dtype=F32):
    n = len(own)

    def body(*refs):
        for a, b, o in zip(refs[:n], refs[n:2 * n], refs[2 * n:]):
            o[...] = (a[...] + b[...]).astype(o.dtype)

    kw = {}
    if grid is not None:
        kw = dict(grid=grid, in_specs=list(specs) * 2, out_specs=tuple(specs),
                  compiler_params=_cp(("parallel",) * len(grid)))
    return pl.pallas_call(
        body, name=f"add_{tag}", out_shape=tuple(SDS(a.shape, dtype) for a in own), **kw)(*own, *got)


def _adamw_math(w, g, m, v):
    m = ADAM_B1 * m + (1.0 - ADAM_B1) * g
    v = ADAM_B2 * v + (1.0 - ADAM_B2) * (g * g)
    m_hat = m / (1.0 - ADAM_B1 ** ADAM_STEP)
    v_hat = v / (1.0 - ADAM_B2 ** ADAM_STEP)
    delta = -ADAM_LR * (m_hat / (jnp.sqrt(v_hat) + ADAM_EPS) + ADAM_WD * w)
    return delta, m, v


def _sum_slots_adamw(tag, slots, w, m, v):
    _, r, c = slots[0].shape
    rb = _row_block(r, most=512)

    def body(s0_ref, s1_ref, w_ref, m_ref, v_ref, g_ref, d_ref, nm_ref, nv_ref):
        first = pl.program_id(1) == 0
        g = _pair_sum([jnp.where(first, s0_ref[k], s1_ref[k]).astype(F32) for k in range(N_CHIP)])
        delta, nm, nv = _adamw_math(w_ref[...], g, m_ref[...], v_ref[...])
        g_ref[...] = g
        d_ref[...] = delta
        nm_ref[...] = nm
        nv_ref[...] = nv

    spec = pl.BlockSpec((None, rb, c), lambda j, l: (l, j, 0))
    sspec = pl.BlockSpec((N_CHIP, rb, c), lambda j, l: (0, j, 0))
    s = SDS((DEPTH, r, c), F32)
    return pl.pallas_call(
        body, name=f"adamw_{tag}", out_shape=(s, s, s, s),
        grid=(r // rb, DEPTH), in_specs=[sspec, sspec, spec, spec, spec], out_specs=(spec, spec, spec, spec),
        compiler_params=_cp(("parallel", "arbitrary")),
    )(*slots, w, m, v)


def _adamw_small(tag, entries, grid=None, sums=()):
    flat_in, in_specs, out_shape, out_specs, layout = [], [], [], [], []
    for slots, w, m, v, slot_spec, w_spec in entries:
        per_layer = isinstance(slots, (list, tuple))
        n_slot = len(slots) if per_layer else 1
        flat_in += (list(slots) if per_layer else [slots]) + [w, m, v]
        in_specs += [slot_spec] * n_slot + [w_spec] * 3
        out_shape += [SDS(w.shape, F32)] * 4
        out_specs += [w_spec] * 4
        layout.append((per_layer, n_slot))
    n_entry_in = len(flat_in)
    flat_in += list(sums)
    out_shape += [SDS(s.shape[1:], F32) for s in sums]
    n_in = len(flat_in)

    def body(*refs):
        for s_ref, o_ref in zip(refs[n_entry_in:n_in], refs[len(refs) - len(sums):]):
            o_ref[...] = _sum_slots(s_ref)
        i, o = 0, n_in
        for per_layer, n_slot in layout:
            s_refs = refs[i:i + n_slot]
            w_ref, m_ref, v_ref = refs[i + n_slot:i + n_slot + 3]
            outs = refs[o:o + 4]
            if per_layer:
                for l, s_ref in enumerate(s_refs):
                    at = (slice(l, l + 1),) if len(w_ref.shape) == 2 else (l,)
                    g = _sum_slots(s_ref)
                    res = (g,) + _adamw_math(w_ref[at], g, m_ref[at], v_ref[at])
                    for o_ref, val in zip(outs, res):
                        o_ref[at] = val
            else:
                g = _sum_slots(s_refs[0])
                res = (g,) + _adamw_math(w_ref[...], g, m_ref[...], v_ref[...])
                for o_ref, val in zip(outs, res):
                    o_ref[...] = val
            i += n_slot + 3
            o += 4

    kw = {}
    if grid is not None:
        kw = dict(grid=grid, in_specs=in_specs, out_specs=tuple(out_specs),
                  compiler_params=_cp(("parallel",) * len(grid)))
    res = pl.pallas_call(body, name=f"adamw_{tag}", out_shape=tuple(out_shape), **kw)(*flat_in)
    return [tuple(res[4 * e:4 * e + 4]) for e in range(len(entries))], res[4 * len(entries):]


def kernel(x, norm_g, w_in, b_in, ssm_log_dt, ssm_lam_re, ssm_lam_im, ssm_b_re, ssm_b_im, ssm_c_re, ssm_c_im, ssm_d, ssm_w_glu, ssm_b_glu, pool_w, pool_scale, w_branch_a, w_branch_b, w_out, final_norm_g, loss_target, m_norm_g, m_w_in, m_b_in, m_ssm_log_dt, m_ssm_lam_re, m_ssm_lam_im, m_ssm_b_re, m_ssm_b_im, m_ssm_c_re, m_ssm_c_im, m_ssm_d, m_ssm_w_glu, m_ssm_b_glu, m_pool_w, m_pool_scale, m_w_branch_a, m_w_branch_b, m_w_out, m_final_norm_g, v_norm_g, v_w_in, v_b_in, v_ssm_log_dt, v_ssm_lam_re, v_ssm_lam_im, v_ssm_b_re, v_ssm_b_im, v_ssm_c_re, v_ssm_c_im, v_ssm_d, v_ssm_w_glu, v_ssm_b_glu, v_pool_w, v_pool_scale, v_w_branch_a, v_w_branch_b, v_w_out, v_final_norm_g):
    weights = dict(norm_g=norm_g, w_in=w_in, b_in=b_in, ssm_log_dt=ssm_log_dt, ssm_lam_re=ssm_lam_re,
                   ssm_lam_im=ssm_lam_im, ssm_b_re=ssm_b_re, ssm_b_im=ssm_b_im, ssm_c_re=ssm_c_re,
                   ssm_c_im=ssm_c_im, ssm_d=ssm_d, ssm_w_glu=ssm_w_glu, ssm_b_glu=ssm_b_glu, pool_w=pool_w,
                   pool_scale=pool_scale, w_branch_a=w_branch_a, w_branch_b=w_branch_b, w_out=w_out,
                   final_norm_g=final_norm_g.reshape(1, D_MODEL))
    mom_m = dict(norm_g=m_norm_g, w_in=m_w_in, b_in=m_b_in, ssm_log_dt=m_ssm_log_dt, ssm_lam_re=m_ssm_lam_re,
                 ssm_lam_im=m_ssm_lam_im, ssm_b_re=m_ssm_b_re, ssm_b_im=m_ssm_b_im, ssm_c_re=m_ssm_c_re,
                 ssm_c_im=m_ssm_c_im, ssm_d=m_ssm_d, ssm_w_glu=m_ssm_w_glu, ssm_b_glu=m_ssm_b_glu,
                 pool_w=m_pool_w, pool_scale=m_pool_scale, w_branch_a=m_w_branch_a, w_branch_b=m_w_branch_b,
                 w_out=m_w_out, final_norm_g=m_final_norm_g.reshape(1, D_MODEL))
    mom_v = dict(norm_g=v_norm_g, w_in=v_w_in, b_in=v_b_in, ssm_log_dt=v_ssm_log_dt, ssm_lam_re=v_ssm_lam_re,
                 ssm_lam_im=v_ssm_lam_im, ssm_b_re=v_ssm_b_re, ssm_b_im=v_ssm_b_im, ssm_c_re=v_ssm_c_re,
                 ssm_c_im=v_ssm_c_im, ssm_d=v_ssm_d, ssm_w_glu=v_ssm_w_glu, ssm_b_glu=v_ssm_b_glu,
                 pool_w=v_pool_w, pool_scale=v_pool_scale, w_branch_a=v_w_branch_a, w_branch_b=v_w_branch_b,
                 w_out=v_w_out, final_norm_g=v_final_norm_g.reshape(1, D_MODEL))
    order = ["norm_g", "w_in", "b_in", "ssm_log_dt", "ssm_lam_re", "ssm_lam_im", "ssm_b_re", "ssm_b_im",
             "ssm_c_re", "ssm_c_im", "ssm_d", "ssm_w_glu", "ssm_b_glu", "pool_w", "pool_scale", "w_branch_a",
             "w_branch_b", "w_out", "final_norm_g"]
    big_names = ["w_in", "ssm_w_glu", "w_branch_a", "w_branch_b", "w_out"]

    log_dt3 = ssm_log_dt.reshape(DEPTH, N_GROUP, 1)
    b_t = lambda a: a.transpose(0, 1, 3, 2)
    for d in (weights, mom_m, mom_v):
        d["ssm_b_re"], d["ssm_b_im"] = b_t(d["ssm_b_re"]), b_t(d["ssm_b_im"])
    bt_re, bt_im = weights["ssm_b_re"], weights["ssm_b_im"]
    abar_re, abar_im, bbt_re, bbt_im = _s5_params(log_dt3, ssm_lam_re, ssm_lam_im, bt_re, bt_im)
    s5_args = (bbt_re, bbt_im, ssm_c_re, ssm_c_im, abar_re, abar_im, ssm_d)

    w16 = {n: weights[n].astype(BF16) for n in big_names}
    rest = [w16[n] for n in big_names[1:]]
    half = D_MODEL // 2
    wg_in = [None, [None, None]]
    wg_rest = [None, None]
    wg_in[0] = list(_run_carried("gather_w_in_l0", _gather_plan([w16["w_in"]], 0)))
    xs = [x.reshape(SEQ, D_MODEL)]
    saved = []
    for l in range(DEPTH):
        proj, moved = _norm_proj(l, xs[l], norm_g, wg_in[l], b_in,
                                 carry=_gather_plan([w16["w_in"]], 1, rows_of=(0, half)) if l == 0 else None)
        if l == 0:
            (wg_in[1][0],) = moved
        (states, y0), wg_rest[l] = _s5_scan_fwd(l, proj, *s5_args, carry=_gather_plan(rest, l, by_columns=(1, 2)))
        pooled = _pool_fwd(l, proj)
        wg_glu, wg_a, wg_b, wg_out = wg_rest[l]
        last = l == DEPTH - 1
        res, moved = _mix_fwd(
            l, xs[l], proj, y0, pooled, wg_glu, ssm_b_glu, pool_w, pool_scale, wg_a, wg_b, wg_out,
            carry=_gather_plan([w16["w_in"]], 1, rows_of=(half, half)) if l == 0 else None,
            head=(loss_target.reshape(SEQ, D_MODEL), weights["final_norm_g"]) if last else None)
        if l == 0:
            (wg_in[1][1],) = moved
        if last:
            dx, loss_part, g_final = res
        else:
            xs.append(res[0])
        saved.append((proj, states, y0, pooled))

    core = lax.axis_index("c").astype(jnp.int32).reshape(1)
    vec_names = ["norm_g", "b_in", "ssm_d", "ssm_b_glu", "pool_scale", "ssm_log_dt"]
    s5_names = ["ssm_log_dt", "ssm_lam_re", "ssm_lam_im", "ssm_b_re", "ssm_b_im"]
    mat_names = ["pool_w", "ssm_c_re", "ssm_c_im", "ssm_b_re", "ssm_b_im"]
    lane_sparse = ("ssm_c_re", "ssm_c_im", "ssm_b_re", "ssm_b_im")

    def dense(key, a):
        return a.reshape(-1, LANES) if key[0] in lane_sparse else a

    def undense(key, slots):
        return slots.reshape((N_CHIP, N_GROUP, GROUP_W, STATE)) if key[0] in lane_sparse else slots

    def add_small(tag, keys, own, got):
        out = [None] * len(keys)
        whole = [i for i, k in enumerate(keys) if k[0] not in mat_names]
        tiled = [i for i, k in enumerate(keys) if k[0] in mat_names]
        if whole:
            for i, r in zip(whole, _add_lists(f"{tag}_a", [own[i] for i in whole], [got[i] for i in whole])):
                out[i] = r
        if tiled:
            specs = [pl.BlockSpec((1, POOL_GROUP, POOL_GROUP), lambda j: (j, 0, 0)) if keys[i][0] == "pool_w"
                     else pl.BlockSpec((own[i].shape[0] // N_CHUNK, LANES), lambda j: (j, 0)) for i in tiled]
            for i, r in zip(tiled, _add_lists(f"{tag}_b", [own[i] for i in tiled], [got[i] for i in tiled],
                                              grid=(N_CHUNK,), specs=specs, dtype=BF16)):
                out[i] = r
        return out

    sm = {("final_norm_g", None): g_final, ("loss", None): loss_part}
    slots = {}
    grads = dict.fromkeys(big_names)

    class Wave:
        def __init__(self, tag, layer, big, keys):
            self.tag, self.layer, self.big, self.keys = tag, layer, big, keys

        def to_sibling(self):
            self.own = [dense(k, sm[k]) for k in self.keys]
            return _sibling_plan([(grads[n], self.layer) for n in self.big], self.own)

        def add(self, moved):
            nb = len(self.big)
            self.chip_big = list(_add_own(self.tag, core, [grads[n] for n in self.big], self.layer, moved[:nb])
                                 ) if nb else []
            self.chip_small = add_small(self.tag, self.keys, self.own, moved[nb:])

        def to_chips(self, big=None, small=True):
            self.sent = list(self.big if big is None else big), small
            return _chips_plan([self.chip_big[self.big.index(n)] for n in self.sent[0]],
                               self.chip_small if small else [])

        def landed(self, moved):
            names, small = self.sent
            for n, s in zip(names, moved[:len(names)]):
                slots[(n, self.layer)] = s
            if small:
                for k, s in zip(self.keys, moved[len(names):]):
                    slots[k] = undense(k, s)
            return moved[len(names) + (len(self.keys) if small else 0):]

    def s5_param_grads(l, g_abar_re, g_abar_im, g_bbt_re, g_bbt_im):
        g = _s5_params_bwd(l, log_dt3, ssm_lam_re, ssm_lam_im, bt_re, bt_im, g_abar_re, g_abar_im, g_bbt_re, g_bbt_im)
        sm[("ssm_log_dt", l)] = g[0].reshape(1, N_GROUP)
        for n, a in zip(s5_names[1:], g[1:]):
            sm[(n, l)] = a

    small1 = ["b_in", "ssm_d", "ssm_b_glu", "pool_scale", "pool_w", "ssm_c_re", "ssm_c_im"] + s5_names
    w1 = Wave("chip1", 1, list(big_names), [(n, 1) for n in small1] + [("final_norm_g", None), ("loss", None)])
    early = Wave("chip0e", 0, big_names[1:], [("pool_w", 0), ("pool_scale", 0), ("ssm_b_glu", 0)])
    mid = Wave("chip0m", 0, [], [(n, 0) for n in ["ssm_c_re", "ssm_c_im", "ssm_d"] + s5_names] + [("norm_g", 1)])
    late = Wave("chip0l", 0, ["w_in"], [("b_in", 0)])

    mix_prev, gw_in = None, None
    for l in reversed(range(DEPTH)):
        proj, states, y0, pooled = saved[l]
        wg_glu, wg_a, wg_b, wg_out = wg_rest[l]
        res, moved = _mix_bwd(l, dx, proj, y0, pooled, wg_glu, ssm_b_glu, pool_w, pool_scale, wg_a, wg_b, wg_out,
                              mix_prev, carry=None if l == 1 else w1.to_chips(big=["w_in"], small=False))
        if l == 0:
            w1.landed(moved)
        dproj, dy0, dpooled = res[:3]
        mix_prev = list(res[3:7])
        grads["w_out"], grads["w_branch_a"], grads["w_branch_b"], grads["ssm_w_glu"] = mix_prev
        sm[("pool_w", l)], sm[("pool_scale", l)], sm[("ssm_b_glu", l)] = res[7:]
        dproj = _pool_bwd(l, dpooled, dproj)
        carry = None if l == 1 else _join(w1.to_chips(big=big_names[1:]), early.to_sibling())
        res, moved = _s5_scan_bwd(l, dy0, proj, states, *s5_args, dproj, carry=carry)
        if l == 0:
            early.add(w1.landed(moved))
        dproj, g_bbt_re, g_bbt_im, sm[("ssm_c_re", l)], sm[("ssm_c_im", l)], g_abar_re, g_abar_im, sm[("ssm_d", l)] = res
        s5_param_grads(l, g_abar_re, g_abar_im, g_bbt_re, g_bbt_im)
        carry = None if l == 1 else _join(early.to_chips(), mid.to_sibling())
        (gw_in, sm[("b_in", l)]), moved = _proj_wgrad(l, xs[l], norm_g, dproj, gw_in, carry=carry)
        grads["w_in"] = gw_in
        if l == 0:
            mid.add(early.landed(moved))
        carry = w1.to_sibling() if l == 1 else _join(mid.to_chips(), late.to_sibling())
        (dx, sm[("norm_g", l)]), moved = _proj_dgrad(l, dx, xs[l], norm_g, dproj, wg_in[l], carry=carry)
        if l == 1:
            w1.add(moved)
        else:
            late.add(mid.landed(moved))
    grad_x = dx.reshape(1, SEQ, D_MODEL)
    moved = late.landed(_run_carried("exchange_last", _join(late.to_chips(), _all_plan([sm[("norm_g", 0)]]))))
    slots[("norm_g", 0)] = moved[0]

    res = {}
    for n in big_names:
        res[n] = _sum_slots_adamw(n, [slots[(n, l)] for l in range(DEPTH)], weights[n], mom_m[n], mom_v[n])
    per_layer = lambda n: [slots[(n, l)] for l in range(DEPTH)]
    names_a = vec_names + ["ssm_lam_re", "ssm_lam_im"]
    entries_a = [(per_layer(n), weights[n], mom_m[n], mom_v[n], None, None) for n in names_a]
    n = "final_norm_g"
    entries_a.append((slots[(n, None)], weights[n], mom_m[n], mom_v[n], None, None))
    out_a, (loss,) = _adamw_small("small_a", entries_a, sums=[slots[("loss", None)]])
    loss = loss.reshape(())
    for n, r in zip(names_a + ["final_norm_g"], out_a):
        res[n] = r
    res["final_norm_g"] = tuple(a.reshape(D_MODEL) for a in res["final_norm_g"])
    pw_s = pl.BlockSpec((N_CHIP, 1, POOL_GROUP, POOL_GROUP), lambda j: (0, j, 0, 0))
    pw_w = pl.BlockSpec((DEPTH, 1, POOL_GROUP, POOL_GROUP), lambda j: (0, j, 0, 0))
    c_s = pl.BlockSpec((N_CHIP, CH_G, GROUP_W, STATE), lambda j: (0, j, 0, 0))
    c_w = pl.BlockSpec((DEPTH, CH_G, GROUP_W, STATE), lambda j: (0, j, 0, 0))
    entries_b = [(per_layer(n), weights[n], mom_m[n], mom_v[n], pw_s if n == "pool_w" else c_s,
                  pw_w if n == "pool_w" else c_w) for n in mat_names]
    out_b, _ = _adamw_small("small_b", entries_b, grid=(N_CHUNK,))
    for n, r in zip(mat_names, out_b):
        res[n] = tuple(b_t(a) for a in r) if n in ("ssm_b_re", "ssm_b_im") else r

    outs = [loss, grad_x]
    for i in range(4):
        outs += [res[n][i] for n in order]
    return tuple(outs)
```

```python
import math

import jax
import jax.numpy as jnp
from jax import lax
from jax.experimental import pallas as pl
from jax.experimental.pallas import tpu as pltpu

F32 = jnp.float32
BF16 = jnp.bfloat16

SEQ = 2048
D_MODEL = 1024
N_IN = 4096
WIDTH = 512
N_GROUP = 32
GROUP_W = 16
STATE = 64
N_STATE = N_GROUP * STATE
N_CHUNK = 4
CH_G = N_GROUP // N_CHUNK
CH_W = WIDTH // N_CHUNK
CH_S = N_STATE // N_CHUNK
N_DEV = 8
N_CHIP = 4
POOL_WINDOWS = (2, 4, 8, 16)
POOL_GROUP = 128
EPS = 1e-6
DEPTH = 2

ADAM_LR = 0.001
ADAM_B1 = 0.9
ADAM_B2 = 0.999
ADAM_EPS = 1e-08
ADAM_WD = 0.01
ADAM_STEP = 10

LANES = 128
SUBLANES = 8
TILE_M = 256
VMEM_LIMIT = 48 * 1024 * 1024
VMEM_LIMIT_BIG = 60 * 1024 * 1024
MESH = pl.DeviceIdType.MESH
ANY = pl.BlockSpec(memory_space=pl.ANY)

GELU_C = math.sqrt(2.0 / math.pi)
GELU_A = 0.044715

SDS = jax.ShapeDtypeStruct


def _cp(sem=None, limit=VMEM_LIMIT):
    return pltpu.CompilerParams(dimension_semantics=sem, vmem_limit_bytes=limit)


def _dot(a, b):
    return jnp.dot(a, b, preferred_element_type=F32)


def _dot_nt(a, b):
    return lax.dot_general(a, b, (((1,), (1,)), ((), ())), preferred_element_type=F32)


def _dot_tn(a, b):
    return lax.dot_general(a, b, (((0,), (0,)), ((), ())), preferred_element_type=F32)


def _sig(x):
    return jax.nn.sigmoid(x)


def _rms(x):
    rs = lax.rsqrt(jnp.mean(x * x, axis=-1, keepdims=True) + EPS)
    return rs, x * rs


def _slot(n):
    return 4 * (n % 2) + n // 2


def _const(shape):
    n = len(shape)
    return pl.BlockSpec(shape, lambda *_: (0,) * n)


def _pair_sum(vals):
    while len(vals) > 1:
        vals = [vals[i] + vals[i + 1] for i in range(0, len(vals), 2)]
    return vals[0]


def _sum_slots(s_ref):
    return _pair_sum([s_ref[k].astype(F32) for k in range(s_ref.shape[0])])


def _s5_param_fn(log_dt, lam_re, lam_im, bt_re, bt_im):
    dt = jnp.exp(log_dt)
    mag = jnp.exp(lam_re * dt)
    ang = lam_im * dt
    abar_re = mag * jnp.cos(ang)
    abar_im = mag * jnp.sin(ang)
    num_re = abar_re - 1.0
    num_im = abar_im
    den = lam_re * lam_re + lam_im * lam_im
    coef_re = (num_re * lam_re + num_im * lam_im) / den
    coef_im = (num_im * lam_re - num_re * lam_im) / den
    bbar_re = coef_re[..., None, :] * bt_re - coef_im[..., None, :] * bt_im
    bbar_im = coef_re[..., None, :] * bt_im + coef_im[..., None, :] * bt_re
    return abar_re, abar_im, bbar_re, bbar_im


def _s5_params(log_dt, lam_re, lam_im, bt_re, bt_im):
    def body(ld, lr, li, br, bi, o_ar, o_ai, o_br, o_bi):
        ar, ai, bbr, bbi = _s5_param_fn(ld[...], lr[...], li[...], br[...], bi[...])
        o_ar[...] = ar
        o_ai[...] = ai
        o_br[...] = bbr
        o_bi[...] = bbi

    return pl.pallas_call(
        body, name="s5_params",
        out_shape=(SDS(lam_re.shape, F32), SDS(lam_re.shape, F32), SDS(bt_re.shape, F32), SDS(bt_re.shape, F32)),
    )(log_dt, lam_re, lam_im, bt_re, bt_im)


def _s5_params_bwd(layer, log_dt, lam_re, lam_im, bt_re, bt_im, g_ar, g_ai, g_br, g_bi):
    def body(ld, lr, li, br, bi, car, cai, cbr, cbi, o_ld, o_lr, o_li, o_br, o_bi):
        _, vjp = jax.vjp(_s5_param_fn, ld[...], lr[...], li[...], br[...], bi[...])
        d_ld, d_lr, d_li, d_br, d_bi = vjp((car[...], cai[...], cbr[...], cbi[...]))
        o_ld[...] = d_ld
        o_lr[...] = d_lr
        o_li[...] = d_li
        o_br[...] = d_br
        o_bi[...] = d_bi

    one = lambda shape: pl.BlockSpec((None,) + shape, lambda i: (layer,) + (0,) * len(shape))
    whole = lambda shape: _const(shape)
    vec, lam, mat = (N_GROUP, 1), (N_GROUP, STATE), (N_GROUP, GROUP_W, STATE)
    return pl.pallas_call(
        body, name=f"s5_params_bwd_l{layer}", grid=(1,),
        in_specs=[one(vec), one(lam), one(lam), one(mat), one(mat), whole(lam), whole(lam), whole(mat), whole(mat)],
        out_specs=(whole(vec), whole(lam), whole(lam), whole(mat), whole(mat)),
        out_shape=(SDS(vec, F32), SDS(lam, F32), SDS(lam, F32), SDS(mat, F32), SDS(mat, F32)),
    )(log_dt, lam_re, lam_im, bt_re, bt_im, g_ar, g_ai, g_br, g_bi)


def _norm_proj(layer, x, norm_g, wg_in, b_in, carry=None):
    n_w = len(wg_in)

    def body(x_ref, g_ref, b_ref, *refs):
        w_refs, o_ref = refs[:n_w], refs[n_w]
        _, xn = _rms(x_ref[...])
        h = (xn * g_ref[layer:layer + 1, :]).astype(BF16)
        for k in range(N_DEV):
            cols = slice(k * WIDTH, (k + 1) * WIDTH)
            acc = b_ref[layer:layer + 1, cols]
            row = 0
            for w_ref in w_refs:
                rows = w_ref.shape[1]
                acc = acc + _dot(h[:, row:row + rows], w_ref[k])
                row += rows
            o_ref[:, cols] = acc

    (proj,), moved = _pcall(
        body, name=f"norm_proj_l{layer}",
        out_shape=[SDS((SEQ, N_IN), F32)],
        grid=(SEQ // TILE_M,),
        in_specs=[pl.BlockSpec((TILE_M, D_MODEL), lambda i: (i, 0)),
                  _const((DEPTH, D_MODEL)),
                  _const((DEPTH, N_IN))] + [_const(w.shape) for w in wg_in],
        out_specs=[pl.BlockSpec((TILE_M, N_IN), lambda i: (i, 0))],
        args=[x, norm_g, b_in, *wg_in], sem=("parallel",), carry=carry)
    return proj, moved


TIME_BLK = 512
N_TBLK = SEQ // TIME_BLK
N_PANEL = CH_S // LANES
STATE_SHAPE = (N_PANEL, SEQ * SUBLANES, LANES)


def _s5_layer_specs(layer):
    mat = lambda: pl.BlockSpec((None, N_GROUP, GROUP_W, STATE), lambda i: (layer, 0, 0, 0))
    ab = lambda: pl.BlockSpec((None, N_GROUP, STATE), lambda i: (layer, 0, 0))
    return [mat(), mat(), mat(), mat(), ab(), ab(), _const((DEPTH, WIDTH))]


def _s5_layer_scratch():
    return [pltpu.VMEM((N_CHUNK, CH_W, CH_S), BF16)] * 4 + [pltpu.VMEM((8, CH_S), F32)] * 2


def _s5_layer_fill(btre_ref, btim_ref, cre_ref, cim_ref, are_ref, aim_ref, bdre, bdim, ctre, ctim, a1, a2):
    for m in (bdre, bdim, ctre, ctim):
        m[...] = jnp.zeros_like(m)
    for grp in range(N_GROUP):
        k, g = divmod(grp, CH_G)
        rows = slice(g * GROUP_W, (g + 1) * GROUP_W)
        cols = slice(g * STATE, (g + 1) * STATE)
        bdre[k, rows, cols] = btre_ref[grp].astype(BF16)
        bdim[k, rows, cols] = btim_ref[grp].astype(BF16)
        ctre[k, rows, cols] = cre_ref[grp].astype(BF16)
        ctim[k, rows, cols] = cim_ref[grp].astype(BF16)
        ar = are_ref[grp:grp + 1, :]
        ai = aim_ref[grp:grp + 1, :]
        a1[k:k + 1, cols] = ar
        a1[N_CHUNK + k:N_CHUNK + k + 1, cols] = ar
        a2[k:k + 1, cols] = -ai
        a2[N_CHUNK + k:N_CHUNK + k + 1, cols] = ai


SCAN_UNROLL = 16


def _panels(tile):
    return [tile[:, p * LANES:(p + 1) * LANES] for p in range(N_PANEL)]


def _rows_load(ref, row):
    return jnp.concatenate([ref[p, pl.ds(row, TIME_BLK, stride=SUBLANES), :] for p in range(N_PANEL)], axis=1)


def _rows_store(ref, row, val):
    for p in range(N_PANEL):
        ref[p, pl.ds(row, TIME_BLK, stride=SUBLANES), :] = val[:, p * LANES:(p + 1) * LANES]


def _s5_scan_fwd(layer, proj, bbt_re, bbt_im, c_re, c_im, abar_re, abar_im, d_skip, carry=None):
    def body(u_ref, btre_ref, btim_ref, cre_ref, cim_ref, are_ref, aim_ref, d_ref, s_ref, y_ref,
             bdre, bdim, ctre, ctim, a1, a2, state):
        @pl.when(pl.program_id(0) == 0)
        def _():
            _s5_layer_fill(btre_ref, btim_ref, cre_ref, cim_ref, are_ref, aim_ref, bdre, bdim, ctre, ctim, a1, a2)
            state[...] = jnp.zeros_like(state)

        for k in range(N_CHUNK):
            ub = u_ref[:, k * CH_W:(k + 1) * CH_W].astype(BF16)
            _rows_store(s_ref, k, _dot(ub, bdre[k]))
            _rows_store(s_ref, N_CHUNK + k, _dot(ub, bdim[k]))
        m1 = _panels(a1[...])
        m2 = _panels(a2[...])

        def steps(n, tile):
            for r in range(SCAN_UNROLL):
                rows = pl.ds(pl.multiple_of((n * SCAN_UNROLL + r) * 8, 8), 8)
                tile = [m1[p] * tile[p] + m2[p] * pltpu.roll(tile[p], N_CHUNK, 0) + s_ref[p, rows, :]
                        for p in range(N_PANEL)]
                for p in range(N_PANEL):
                    s_ref[p, rows, :] = tile[p]
            return tile

        tile = lax.fori_loop(0, TIME_BLK // SCAN_UNROLL, steps, _panels(state[...]))
        state[...] = jnp.concatenate(tile, axis=1)
        d = d_ref[layer:layer + 1, :]
        for k in range(N_CHUNK):
            cols = slice(k * CH_W, (k + 1) * CH_W)
            y = (_dot_nt(_rows_load(s_ref, k).astype(BF16), ctre[k])
                 - _dot_nt(_rows_load(s_ref, N_CHUNK + k).astype(BF16), ctim[k]))
            y_ref[:, cols] = y + d[:, cols] * u_ref[:, cols]

    return _pcall(
        body, name=f"s5_fwd_l{layer}",
        out_shape=(SDS(STATE_SHAPE, F32), SDS((SEQ, WIDTH), F32)),
        grid=(N_TBLK,),
        in_specs=[pl.BlockSpec((TIME_BLK, WIDTH), lambda i: (i, 0))] + _s5_layer_specs(layer),
        out_specs=(pl.BlockSpec((N_PANEL, TIME_BLK * SUBLANES, LANES), lambda i: (0, i, 0)),
                   pl.BlockSpec((TIME_BLK, WIDTH), lambda i: (i, 0))),
        scratch_shapes=_s5_layer_scratch() + [pltpu.VMEM((8, CH_S), F32)],
        args=[proj, bbt_re, bbt_im, c_re, c_im, abar_re, abar_im, d_skip], sem=("arbitrary",), carry=carry)


def _s5_scan_bwd(layer, dy0, proj, states, bbt_re, bbt_im, c_re, c_im, abar_re, abar_im, d_skip, dproj,
                 carry=None):
    def body(dy_ref, u_ref, s_ref, sprev_ref, btre_ref, btim_ref, cre_ref, cim_ref, are_ref, aim_ref, d_ref, _,
             du_ref, gbre_ref, gbim_ref, gcre_ref, gcim_ref, gare_ref, gaim_ref, gd_ref,
             lam_ref, bdre, bdim, ctre, ctim, a1, a2, state, acc1, acc2, gbre, gbim, gcre, gcim, gd):
        step_id = pl.program_id(0)

        @pl.when(step_id == 0)
        def _():
            _s5_layer_fill(btre_ref, btim_ref, cre_ref, cim_ref, are_ref, aim_ref, bdre, bdim, ctre, ctim, a1, a2)
            for r in (state, acc1, acc2, gbre, gbim, gcre, gcim, gd):
                r[...] = jnp.zeros_like(r)

        for k in range(N_CHUNK):
            dyb = dy_ref[:, k * CH_W:(k + 1) * CH_W].astype(BF16)
            _rows_store(lam_ref, k, _dot(dyb, ctre[k]))
            _rows_store(lam_ref, N_CHUNK + k, -_dot(dyb, ctim[k]))
            gcre[k] += _dot_tn(dyb, _rows_load(s_ref, k).astype(BF16))
            gcim[k] -= _dot_tn(dyb, _rows_load(s_ref, N_CHUNK + k).astype(BF16))

        m1 = _panels(a1[...])
        m2 = _panels(-a2[...])
        has_before = (step_id < N_TBLK - 1).astype(F32)

        def one(t8, c, first_token):
            tile, swapped, p1, p2 = c
            rows = pl.ds(t8, 8)
            tile = [m1[p] * tile[p] + m2[p] * swapped[p] + lam_ref[p, rows, :] for p in range(N_PANEL)]
            swapped = [pltpu.roll(tile[p], N_CHUNK, 0) for p in range(N_PANEL)]
            for p in range(N_PANEL):
                lam_ref[p, rows, :] = tile[p]
            if first_token:
                before = [sprev_ref[p] * has_before for p in range(N_PANEL)]
            else:
                before = [s_ref[p, pl.ds(t8 - 8, 8), :] for p in range(N_PANEL)]
            p1 = [p1[p] + tile[p] * before[p] for p in range(N_PANEL)]
            p2 = [p2[p] + swapped[p] * before[p] for p in range(N_PANEL)]
            return tile, swapped, p1, p2

        def steps(n, c):
            for r in range(SCAN_UNROLL):
                t8 = pl.multiple_of((TIME_BLK - 1 - (n * SCAN_UNROLL + r)) * 8, 8)
                c = one(t8, c, False)
            return c

        tile0 = _panels(state[...])
        c = (tile0, [pltpu.roll(t, N_CHUNK, 0) for t in tile0], _panels(acc1[...]), _panels(acc2[...]))
        c = lax.fori_loop(0, TIME_BLK // SCAN_UNROLL - 1, steps, c)
        for r in range(SCAN_UNROLL - 1, -1, -1):
            c = one(r * 8, c, r == 0)
        state[...] = jnp.concatenate(c[0], axis=1)
        acc1[...] = jnp.concatenate(c[2], axis=1)
        acc2[...] = jnp.concatenate(c[3], axis=1)

        d = d_ref[layer:layer + 1, :]
        for k in range(N_CHUNK):
            cols = slice(k * CH_W, (k + 1) * CH_W)
            lrb = _rows_load(lam_ref, k).astype(BF16)
            lib = _rows_load(lam_ref, N_CHUNK + k).astype(BF16)
            u = u_ref[:, cols]
            ub = u.astype(BF16)
            dy = dy_ref[:, cols]
            du = dy * d[:, cols] + _dot_nt(lrb, bdre[k]) + _dot_nt(lib, bdim[k])
            du_ref[:, cols] = du.astype(BF16)
            gbre[k] += _dot_tn(ub, lrb)
            gbim[k] += _dot_tn(ub, lib)
        gd[...] += jnp.sum(dy_ref[...] * u_ref[...], axis=0, keepdims=True)

        @pl.when(step_id == N_TBLK - 1)
        def _():
            gd_ref[...] = gd[...]
            ga_re = acc1[0:N_CHUNK, :] + acc1[N_CHUNK:, :]
            ga_im = acc2[0:N_CHUNK, :] - acc2[N_CHUNK:, :]
            for grp in range(N_GROUP):
                k, g = divmod(grp, CH_G)
                rows = slice(g * GROUP_W, (g + 1) * GROUP_W)
                cols = slice(g * STATE, (g + 1) * STATE)
                gcre_ref[grp] = gcre[k, rows, cols]
                gcim_ref[grp] = gcim[k, rows, cols]
                gbre_ref[grp] = gbre[k, rows, cols]
                gbim_ref[grp] = gbim[k, rows, cols]
                gare_ref[grp:grp + 1, :] = ga_re[k:k + 1, cols]
                gaim_ref[grp:grp + 1, :] = ga_im[k:k + 1, cols]

    back = lambda i: N_TBLK - 1 - i
    tok = lambda: pl.BlockSpec((TIME_BLK, WIDTH), lambda i: (back(i), 0))
    mat = lambda: _const((N_GROUP, GROUP_W, STATE))
    acc_mat = pltpu.VMEM((N_CHUNK, CH_W, CH_S), F32)
    return _pcall(
        body, name=f"s5_bwd_l{layer}",
        out_shape=(SDS((SEQ, N_IN), BF16), SDS((N_GROUP, GROUP_W, STATE), F32), SDS((N_GROUP, GROUP_W, STATE), F32),
                   SDS((N_GROUP, GROUP_W, STATE), F32), SDS((N_GROUP, GROUP_W, STATE), F32),
                   SDS((N_GROUP, STATE), F32), SDS((N_GROUP, STATE), F32), SDS((1, WIDTH), F32)),
        grid=(N_TBLK,),
        in_specs=[tok(), tok(),
                  pl.BlockSpec((N_PANEL, TIME_BLK * SUBLANES, LANES), lambda i: (0, back(i), 0)),
                  pl.BlockSpec((N_PANEL, SUBLANES, LANES), lambda i: (0, jnp.maximum(back(i) * TIME_BLK - 1, 0), 0))]
        + _s5_layer_specs(layer) + [ANY],
        out_specs=(tok(), mat(), mat(), mat(), mat(), _const((N_GROUP, STATE)), _const((N_GROUP, STATE)),
                   _const((1, WIDTH))),
        scratch_shapes=[pltpu.VMEM((N_PANEL, TIME_BLK * SUBLANES, LANES), F32)] + _s5_layer_scratch()
        + [pltpu.VMEM((8, CH_S), F32)] * 3 + [acc_mat] * 4 + [pltpu.VMEM((1, WIDTH), F32)],
        args=[dy0, proj, states, states, bbt_re, bbt_im, c_re, c_im, abar_re, abar_im, d_skip, dproj],
        aliases={11: 0}, sem=("arbitrary",), limit=VMEM_LIMIT_BIG, carry=carry)


def _pool_counts(win):
    t = lax.broadcasted_iota(jnp.int32, (SEQ, POOL_GROUP), 0)
    return t, jnp.minimum(t + 1, win).astype(F32)


def _pool_fwd(layer, proj):
    def body(u_ref, o_ref):
        for gi, win in enumerate(POOL_WINDOWS):
            cols = slice(gi * POOL_GROUP, (gi + 1) * POOL_GROUP)
            u = u_ref[:, cols]
            t, count = _pool_counts(win)
            acc = u
            k = 1
            while k < win:
                acc = acc + jnp.where(t >= k, pltpu.roll(acc, k, 0), 0.0)
                k *= 2
            o_ref[:, cols] = acc / count - u

    return pl.pallas_call(
        body, name=f"pool_fwd_l{layer}",
        out_shape=SDS((SEQ, WIDTH), F32),
        grid=(1,),
        in_specs=[pl.BlockSpec((SEQ, WIDTH), lambda i: (0, 2))],
        out_specs=pl.BlockSpec((SEQ, WIDTH), lambda i: (0, 0)),
        compiler_params=_cp(("arbitrary",)),
    )(proj)


def _gelu_parts(y0):
    t = jnp.tanh(GELU_C * (y0 + GELU_A * (y0 * y0 * y0)))
    return t, 0.5 * y0 * (1.0 + t)


def _mix_forward(layer, p_ref, y0_ref, pooled_ref, wglu_ref, bglu_ref, pw_ref, scale_ref, wa_ref, wb_ref):
    za = p_ref[:, WIDTH:2 * WIDTH]
    zb = p_ref[:, 3 * WIDTH:4 * WIDTH]
    ga = p_ref[:, 4 * WIDTH:4 * WIDTH + D_MODEL]
    gb = p_ref[:, 4 * WIDTH + D_MODEL:]
    y0 = y0_ref[...]
    t, y1 = _gelu_parts(y0)
    y1b = y1.astype(BF16)
    q = _dot(y1b, wglu_ref[...].reshape(WIDTH, WIDTH)) + bglu_ref[layer:layer + 1, :]
    sq = _sig(q)
    y2 = y1 * sq
    sza = _sig(za)
    silu_za = za * sza
    ya = y2 * silu_za
    pooled = pooled_ref[...]
    mixed = jnp.concatenate(
        [_dot(pooled[:, g * POOL_GROUP:(g + 1) * POOL_GROUP].astype(BF16), pw_ref[g].astype(BF16))
         for g in range(len(POOL_WINDOWS))], axis=1)
    szb = _sig(zb)
    silu_zb = zb * szb
    scale = scale_ref[layer:layer + 1, :]
    ms = mixed * scale
    yb = ms * silu_zb
    yab = ya.astype(BF16)
    ybb = yb.astype(BF16)
    ma = _dot(yab, wa_ref[...])
    mb = _dot(ybb, wb_ref[...])
    sga = _sig(ga)
    sgb = _sig(gb)
    merged = sga * ma + sgb * mb
    return dict(za=za, zb=zb, y0=y0, t=t, y1=y1, y1b=y1b, sq=sq, y2=y2, sza=sza, silu_za=silu_za,
                pooled=pooled, mixed=mixed, szb=szb, silu_zb=silu_zb, scale=scale, ms=ms, yab=yab, ybb=ybb,
                ma=ma, mb=mb, sga=sga, sgb=sgb, merged=merged)


def _mix_weight_specs(layer):
    return [_const((N_DEV, WIDTH // N_DEV, WIDTH)),
            _const((DEPTH, WIDTH)),
            pl.BlockSpec((None, 4, POOL_GROUP, POOL_GROUP), lambda i: (layer, 0, 0, 0)),
            _const((DEPTH, WIDTH)),
            _const((WIDTH, D_MODEL)),
            _const((WIDTH, D_MODEL)),
            _const((N_DEV, D_MODEL // N_DEV, D_MODEL))]


def _loss_head(x, t_ref, g_ref, dx_ref, loss_ref, gg_ref):
    @pl.when(pl.program_id(0) == 0)
    def _():
        loss_ref[...] = jnp.zeros_like(loss_ref)
        gg_ref[...] = jnp.zeros_like(gg_ref)

    g = g_ref[...]
    rs, xn = _rms(x)
    err = xn * g - t_ref[...]
    loss_ref[...] += 0.5 * jnp.sum(jnp.mean(err * err, axis=-1, keepdims=True), axis=0, keepdims=True)
    dy = err * (1.0 / D_MODEL)
    gg_ref[...] += jnp.sum(dy * xn, axis=0, keepdims=True)
    dxn = dy * g
    dx_ref[...] = rs * (dxn - xn * jnp.mean(dxn * xn, axis=-1, keepdims=True))


def _mix_fwd(layer, x, proj, y0, pooled, wg_glu, b_glu, pool_w, pool_scale, wg_a, wg_b, wg_out, carry=None,
             head=None):
    def body(x_ref, p_ref, y0_ref, pooled_ref, wglu_ref, bglu_ref, pw_ref, scale_ref, wa_ref, wb_ref,
             wout_ref, *rest):
        f = _mix_forward(layer, p_ref, y0_ref, pooled_ref, wglu_ref, bglu_ref, pw_ref, scale_ref, wa_ref, wb_ref)
        wout = wout_ref[...].reshape(D_MODEL, D_MODEL)
        x_next = x_ref[...] + _dot(f["merged"].astype(BF16), wout)
        if head is None:
            rest[0][...] = x_next
        else:
            _loss_head(x_next, *rest)

    tile = lambda: pl.BlockSpec((TILE_M, D_MODEL), lambda i: (i, 0))
    if head is None:
        extra, out_shape, out_specs = [], [SDS((SEQ, D_MODEL), F32)], [tile()]
    else:
        extra = list(head)
        out_shape = [SDS((SEQ, D_MODEL), F32), SDS((1, 1), F32), SDS((1, D_MODEL), F32)]
        out_specs = [tile(), _const((1, 1)), _const((1, D_MODEL))]
    return _pcall(
        body, name=f"mix_fwd_l{layer}",
        out_shape=out_shape,
        grid=(SEQ // TILE_M,),
        in_specs=[tile(),
                  pl.BlockSpec((TILE_M, N_IN), lambda i: (i, 0)),
                  pl.BlockSpec((TILE_M, WIDTH), lambda i: (i, 0)),
                  pl.BlockSpec((TILE_M, WIDTH), lambda i: (i, 0))] + _mix_weight_specs(layer)
        + ([tile(), _const((1, D_MODEL))] if head else []),
        out_specs=out_specs,
        args=[x, proj, y0, pooled, wg_glu, b_glu, pool_w, pool_scale, wg_a, wg_b, wg_out] + extra,
        sem=("parallel",) if head is None else ("arbitrary",), carry=carry)


def _big_shapes():
    return dict(w_out=(DEPTH, N_DEV, D_MODEL // N_DEV, D_MODEL), w_branch_a=(DEPTH, N_DEV, WIDTH, D_MODEL // N_DEV),
                w_branch_b=(DEPTH, N_DEV, WIDTH, D_MODEL // N_DEV), ssm_w_glu=(DEPTH, N_DEV, WIDTH // N_DEV, WIDTH),
                w_in=(DEPTH, N_DEV, D_MODEL, WIDTH))


def _mix_bwd(layer, dx_next, proj, y0, pooled, wg_glu, b_glu, pool_w, pool_scale, wg_a, wg_b, wg_out, prev,
             carry=None):
    n_k = N_DEV
    n_prev = 0 if prev is None else len(prev)

    def body(*refs):
        (dx_ref, p_ref, y0_ref, pooled_ref, wglu_ref, bglu_ref, pw_ref, scale_ref, wa_ref, wb_ref,
         wout_ref) = refs[:11]
        (dproj_ref, dy0_ref, dpooled_ref, gwout_ref, gwa_ref, gwb_ref, gwglu_ref, gpw_ref,
         gscale_ref, gbglu_ref) = refs[11 + n_prev:]

        @pl.when(pl.program_id(0) == 0)
        def _():
            for r in (gwout_ref, gwa_ref, gwb_ref, gwglu_ref, gpw_ref, gscale_ref, gbglu_ref):
                r[...] = jnp.zeros_like(r)

        f = _mix_forward(layer, p_ref, y0_ref, pooled_ref, wglu_ref, bglu_ref, pw_ref, scale_ref, wa_ref, wb_ref)
        wglu = wglu_ref[...].reshape(WIDTH, WIDTH)
        wout = wout_ref[...].reshape(D_MODEL, D_MODEL)
        blk = D_MODEL // n_k
        dxb = dx_ref[...].astype(BF16)
        dmerged = _dot_nt(dxb, wout)
        gwout = _dot_tn(f["merged"].astype(BF16), dxb)
        for k in range(n_k):
            gwout_ref[_slot(k)] += gwout[k * blk:(k + 1) * blk, :]
        dma = dmerged * f["sga"]
        dmb = dmerged * f["sgb"]
        dga = dmerged * f["ma"] * f["sga"] * (1.0 - f["sga"])
        dgb = dmerged * f["mb"] * f["sgb"] * (1.0 - f["sgb"])
        dmab = dma.astype(BF16)
        dmbb = dmb.astype(BF16)
        dya = _dot_nt(dmab, wa_ref[...])
        dyb = _dot_nt(dmbb, wb_ref[...])
        gwa = _dot_tn(f["yab"], dmab)
        gwb = _dot_tn(f["ybb"], dmbb)
        for k in range(n_k):
            gwa_ref[_slot(k)] += gwa[:, k * blk:(k + 1) * blk]
            gwb_ref[_slot(k)] += gwb[:, k * blk:(k + 1) * blk]
        zb, szb = f["zb"], f["szb"]
        dzb = dyb * f["ms"] * (szb * (1.0 + zb * (1.0 - szb)))
        dms = dyb * f["silu_zb"]
        gscale_ref[...] += jnp.sum(dms * f["mixed"], axis=0, keepdims=True)
        dmixed = (dms * f["scale"]).astype(BF16)
        pooled = f["pooled"]
        for g in range(len(POOL_WINDOWS)):
            cols = slice(g * POOL_GROUP, (g + 1) * POOL_GROUP)
            dpooled_ref[:, cols] = _dot_nt(dmixed[:, cols], pw_ref[g].astype(BF16))
            gpw_ref[g] += _dot_tn(pooled[:, cols].astype(BF16), dmixed[:, cols])
        za, sza = f["za"], f["sza"]
        dza = dya * f["y2"] * (sza * (1.0 + za * (1.0 - sza)))
        dy2 = dya * f["silu_za"]
        sq = f["sq"]
        dq = dy2 * f["y1"] * sq * (1.0 - sq)
        dqb = dq.astype(BF16)
        dy1 = dy2 * sq + _dot_nt(dqb, wglu)
        gwglu = _dot_tn(f["y1b"], dqb)
        rblk = WIDTH // n_k
        for k in range(n_k):
            gwglu_ref[_slot(k)] += gwglu[k * rblk:(k + 1) * rblk, :]
        gbglu_ref[...] += jnp.sum(dq, axis=0, keepdims=True)
        y0, t = f["y0"], f["t"]
        dgelu = 0.5 * (1.0 + t) + 0.5 * y0 * (1.0 - t * t) * (GELU_C * (1.0 + 3.0 * GELU_A * y0 * y0))
        dy0_ref[...] = dy1 * dgelu
        zeros = jnp.zeros((TILE_M, WIDTH), BF16)
        dproj_ref[:, 0:WIDTH] = zeros
        dproj_ref[:, WIDTH:2 * WIDTH] = dza.astype(BF16)
        dproj_ref[:, 2 * WIDTH:3 * WIDTH] = zeros
        dproj_ref[:, 3 * WIDTH:4 * WIDTH] = dzb.astype(BF16)
        dproj_ref[:, 4 * WIDTH:4 * WIDTH + D_MODEL] = dga.astype(BF16)
        dproj_ref[:, 4 * WIDTH + D_MODEL:] = dgb.astype(BF16)

    tile = lambda w: pl.BlockSpec((TILE_M, w), lambda i: (i, 0))
    shapes = _big_shapes()
    big = ["w_out", "w_branch_a", "w_branch_b", "ssm_w_glu"]
    slab = lambda n: pl.BlockSpec((None,) + shapes[n][1:], lambda i: (layer, 0, 0, 0))
    args = [dx_next, proj, y0, pooled, wg_glu, b_glu, pool_w, pool_scale, wg_a, wg_b, wg_out]
    return _pcall(
        body, name=f"mix_bwd_l{layer}",
        out_shape=(SDS((SEQ, N_IN), BF16), SDS((SEQ, WIDTH), F32), SDS((SEQ, WIDTH), F32))
        + tuple(SDS(shapes[n], F32) for n in big)
        + (SDS((4, POOL_GROUP, POOL_GROUP), F32), SDS((1, WIDTH), F32), SDS((1, WIDTH), F32)),
        grid=(SEQ // TILE_M,),
        in_specs=[tile(D_MODEL), tile(N_IN), tile(WIDTH), tile(WIDTH)] + _mix_weight_specs(layer) + [ANY] * n_prev,
        out_specs=(tile(N_IN), tile(WIDTH), tile(WIDTH)) + tuple(slab(n) for n in big)
        + (_const((4, POOL_GROUP, POOL_GROUP)), _const((1, WIDTH)), _const((1, WIDTH))),
        args=args + list(prev or ()),
        aliases={len(args) + i: 3 + i for i in range(n_prev)},
        sem=("arbitrary",), limit=VMEM_LIMIT_BIG, carry=carry)


def _pool_bwd(layer, dpooled, dproj):
    def body(dp_ref, _, o_ref):
        for gi, win in enumerate(POOL_WINDOWS):
            cols = slice(gi * POOL_GROUP, (gi + 1) * POOL_GROUP)
            dp = dp_ref[:, cols]
            t, count = _pool_counts(win)
            e = dp / count
            acc = e
            k = 1
            while k < win:
                acc = acc + jnp.where(t < SEQ - k, pltpu.roll(acc, SEQ - k, 0), 0.0)
                k *= 2
            o_ref[:, cols] = (acc - dp).astype(BF16)

    return pl.pallas_call(
        body, name=f"pool_bwd_l{layer}",
        out_shape=SDS((SEQ, N_IN), BF16),
        grid=(1,),
        in_specs=[pl.BlockSpec((SEQ, WIDTH), lambda i: (0, 0)), ANY],
        out_specs=pl.BlockSpec((SEQ, WIDTH), lambda i: (0, 2)),
        input_output_aliases={1: 0},
        compiler_params=_cp(("arbitrary",)),
    )(dpooled, dproj)


def _proj_wgrad(layer, x, norm_g, dproj, prev, carry=None):
    tm = 512
    n_prev = 0 if prev is None else 1

    def body(*refs):
        x_ref, g_ref, dp_ref = refs[:3]
        gw_ref, gb_ref, ht_ref = refs[3 + n_prev:]
        n, t = pl.program_id(0), pl.program_id(1)

        @pl.when(t == 0)
        def _():
            gw_ref[...] = jnp.zeros_like(gw_ref)
            gb_ref[...] = jnp.zeros_like(gb_ref)

        @pl.when(n == 0)
        def _():
            _, xn = _rms(x_ref[...])
            ht_ref[t] = (xn * g_ref[layer:layer + 1, :]).T.astype(BF16)

        dp = dp_ref[...]
        gw_ref[...] += _dot(ht_ref[t], dp)
        gb_ref[...] += jnp.sum(dp.astype(F32), axis=0, keepdims=True)

    return _pcall(
        body, name=f"proj_wgrad_l{layer}",
        out_shape=(SDS(_big_shapes()["w_in"], F32), SDS((1, N_IN), F32)),
        grid=(N_DEV, SEQ // tm),
        in_specs=[pl.BlockSpec((tm, D_MODEL), lambda n, t: (jnp.where(n == 0, t, 0), 0)),
                  _const((DEPTH, D_MODEL)),
                  pl.BlockSpec((tm, WIDTH), lambda n, t: (t, n))] + [ANY] * n_prev,
        out_specs=(pl.BlockSpec((None, None, D_MODEL, WIDTH), lambda n, t: (layer, _slot(n), 0, 0)),
                   pl.BlockSpec((1, WIDTH), lambda n, t: (0, n))),
        scratch_shapes=[pltpu.VMEM((SEQ // tm, D_MODEL, tm), BF16)],
        args=[x, norm_g, dproj] + ([prev] if n_prev else []),
        aliases={3: 0} if n_prev else {}, sem=("arbitrary", "arbitrary"), carry=carry)


def _proj_wgrad_part(layer, x, norm_g, dproj, core, mine, carry=None):
    tm = 512

    def column_block(i, core_ref):
        return 2 * i + (core_ref[0] if mine else 1 - core_ref[0])

    def body(core_ref, x_ref, g_ref, dp_ref, gw_ref, ht_ref):
        i, t = pl.program_id(0), pl.program_id(1)

        @pl.when(t == 0)
        def _():
            gw_ref[...] = jnp.zeros_like(gw_ref)

        @pl.when(i == 0)
        def _():
            _, xn = _rms(x_ref[...])
            ht_ref[t] = (xn * g_ref[layer:layer + 1, :]).T.astype(BF16)

        gw_ref[...] += _dot(ht_ref[t], dp_ref[...])

    (gw,), moved = _pcall(
        body, name=f"proj_wgrad_l{layer}_{'mine' if mine else 'sibling'}",
        out_shape=[SDS((N_CHIP, D_MODEL, WIDTH), F32)],
        grid=(N_CHIP, SEQ // tm),
        in_specs=[pl.BlockSpec((tm, D_MODEL), lambda i, t, core: (jnp.where(i == 0, t, 0), 0)),
                  pl.BlockSpec((DEPTH, D_MODEL), lambda i, t, core: (0, 0)),
                  pl.BlockSpec((tm, WIDTH), lambda i, t, core: (t, column_block(i, core)))],
        out_specs=[pl.BlockSpec((None, D_MODEL, WIDTH), lambda i, t, core: (i, 0, 0))],
        scratch_shapes=[pltpu.VMEM((SEQ // tm, D_MODEL, tm), BF16)],
        args=[x, norm_g, dproj], sem=("arbitrary", "arbitrary"), carry=carry, prefetch=core)
    return gw, moved


def _proj_dgrad(layer, dx_next, x, norm_g, dproj, wg_in, carry=None, bias_grad=False):
    n_w = len(wg_in)

    def body(dxn_ref, x_ref, g_ref, dp_ref, *refs):
        w_refs, (dx_ref, gg_ref, *gb_ref) = refs[:n_w], refs[n_w:]

        @pl.when(pl.program_id(0) == 0)
        def _():
            gg_ref[...] = jnp.zeros_like(gg_ref)
            for r in gb_ref:
                r[...] = jnp.zeros_like(r)

        for r in gb_ref:
            r[...] += jnp.sum(dp_ref[...].astype(F32), axis=0, keepdims=True)

        parts = []
        for w_ref in w_refs:
            part = jnp.zeros((TILE_M, w_ref.shape[1]), F32)
            for k in range(N_DEV):
                part = part + _dot_nt(dp_ref[:, k * WIDTH:(k + 1) * WIDTH], w_ref[k])
            parts.append(part)
        dh = parts[0] if n_w == 1 else jnp.concatenate(parts, axis=1)
        rs, xn = _rms(x_ref[...])
        gg_ref[...] += jnp.sum(dh * xn, axis=0, keepdims=True)
        dxn = dh * g_ref[layer:layer + 1, :]
        dx_ref[...] = dxn_ref[...] + rs * (dxn - xn * jnp.mean(dxn * xn, axis=-1, keepdims=True))

    return _pcall(
        body, name=f"proj_dgrad_l{layer}",
        out_shape=(SDS((SEQ, D_MODEL), F32), SDS((1, D_MODEL), F32)) + ((SDS((1, N_IN), F32),) if bias_grad else ()),
        grid=(SEQ // TILE_M,),
        in_specs=[pl.BlockSpec((TILE_M, D_MODEL), lambda i: (i, 0)),
                  pl.BlockSpec((TILE_M, D_MODEL), lambda i: (i, 0)),
                  _const((DEPTH, D_MODEL)),
                  pl.BlockSpec((TILE_M, N_IN), lambda i: (i, 0))] + [_const(w.shape) for w in wg_in],
        out_specs=(pl.BlockSpec((TILE_M, D_MODEL), lambda i: (i, 0)), _const((1, D_MODEL)))
        + ((_const((1, N_IN)),) if bias_grad else ()),
        args=[dx_next, x, norm_g, dproj, *wg_in], sem=("arbitrary",), carry=carry)


def _my_place():
    return lax.axis_index("x"), lax.axis_index("y"), lax.axis_index("c")


def _gather_plan(shards, layer, by_columns=(), rows_of=None):
    n = len(shards)

    def parts(ins, outs, sems):
        send_sems, recv_sems, local_sems = sems
        x, y, c = _my_place()
        chips = [(1 - x, y), (x, 1 - y), (1 - x, 1 - y)]

        def source(t):
            return ins[t].at[layer] if rows_of is None else ins[t].at[layer, pl.ds(*rows_of)]

        def rows(t, place):
            px, py, pc = place
            index = 4 * px + 2 * py + pc
            if t in by_columns:
                width = shards[t].shape[2]
                return outs[t].at[:, pl.ds(pl.multiple_of(index * width, LANES), width)]
            return outs[t].at[index]

        def copy(t, k, block, to, from_src=False):
            return pltpu.make_async_remote_copy(
                src_ref=source(t) if from_src else rows(t, block), dst_ref=rows(t, block),
                send_sem=send_sems.at[7 * t + k], recv_sem=recv_sems.at[7 * t + k], device_id=to,
                device_id_type=MESH)

        def mine(t):
            return pltpu.make_async_copy(source(t), rows(t, (x, y, c)), local_sems.at[t])

        return (x, y, c), chips, copy, mine

    def start(ins, outs, sems):
        me, chips, copy, mine = parts(ins, outs, sems)
        x, y, c = me
        for t in range(n):
            mine(t).start()
            copy(t, 0, me, (x, y, 1 - c), from_src=True).start()
            for j, chip in enumerate(chips):
                copy(t, 1 + j, me, (*chip, c), from_src=True).start()

    def relay(ins, outs, sems):
        me, chips, copy, mine = parts(ins, outs, sems)
        x, y, c = me
        for t in range(n):
            for j, chip in enumerate(chips):
                copy(t, 1 + j, (*chip, c), me).wait_recv()
                copy(t, 4 + j, (*chip, c), (x, y, 1 - c)).start()

    def finish(ins, outs, sems):
        me, chips, copy, mine = parts(ins, outs, sems)
        x, y, c = me
        sibling = (x, y, 1 - c)
        for t in range(n):
            copy(t, 0, sibling, me).wait_recv()
            for j, chip in enumerate(chips):
                copy(t, 4 + j, (*chip, 1 - c), me).wait_recv()
            for k in range(7):
                copy(t, k, me, sibling, from_src=k < 4).wait_send()
            mine(t).wait()

    n_rows = lambda a: a.shape[1] if rows_of is None else rows_of[1]
    out_shape = [SDS((a.shape[1], N_DEV * a.shape[2]) if t in by_columns else (N_DEV, n_rows(a), a.shape[2]), a.dtype)
                 for t, a in enumerate(shards)]
    sems = [pltpu.SemaphoreType.DMA((7 * n,)), pltpu.SemaphoreType.DMA((7 * n,)), pltpu.SemaphoreType.DMA((n,))]
    return _Carried(shards, out_shape, sems, start, finish, relay)


class _Carried:
    def __init__(self, ins, out_shape, sems, start, finish, relay=None):
        self.ins, self.out_shape, self.sems = list(ins), list(out_shape), list(sems)
        self.start, self.finish = start, finish
        self.relay = relay or (lambda ins, outs, sems: None)


def _pcall(body, *, name, grid, in_specs, out_specs, out_shape, args, scratch_shapes=(), aliases=None,
           sem=None, limit=VMEM_LIMIT, carry=None, prefetch=None):
    out_shape, out_specs, scratch_shapes = list(out_shape), list(out_specs), list(scratch_shapes)
    n_in, n_out, n_scr = len(args), len(out_shape), len(scratch_shapes)
    lead = [] if prefetch is None else [prefetch]
    if carry is None:
        kern, c_ins, c_out, c_sems = body, [], [], []
    else:
        c_ins, c_out, c_sems = carry.ins, carry.out_shape, carry.sems
        ci, co = len(c_ins), len(c_out)
        steps = tuple(grid)

        def kern(*refs):
            pre, refs = refs[:len(lead)], refs[len(lead):]
            o0 = n_in + ci
            s0 = o0 + n_out + co
            mine = refs[:n_in] + refs[o0:o0 + n_out] + refs[s0:s0 + n_scr]
            theirs = (refs[n_in:o0], refs[o0 + n_out:s0], refs[s0 + n_scr:])
            first = pl.program_id(0) == 0
            last = pl.program_id(0) == steps[0] - 1
            for a in range(1, len(steps)):
                first = jnp.logical_and(first, pl.program_id(a) == 0)
                last = jnp.logical_and(last, pl.program_id(a) == steps[a] - 1)

            @pl.when(first)
            def _():
                carry.start(*theirs)

            @pl.when(last)
            def _():
                carry.relay(*theirs)

            body(*pre, *mine)

            @pl.when(last)
            def _():
                carry.finish(*theirs)

        sem = ("arbitrary",) * len(steps)
    layout = dict(grid=tuple(grid), in_specs=list(in_specs) + [ANY] * len(c_ins),
                  out_specs=tuple(out_specs + [ANY] * len(c_out)), scratch_shapes=scratch_shapes + c_sems)
    if prefetch is not None:
        layout = dict(grid_spec=pltpu.PrefetchScalarGridSpec(num_scalar_prefetch=1, **layout))
    res = pl.pallas_call(
        kern, name=name, out_shape=tuple(out_shape + c_out), input_output_aliases=aliases or {},
        compiler_params=_cp(sem, limit), **layout,
    )(*lead, *args, *c_ins)
    return res[:n_out], res[n_out:]


def _run_carried(name, carry):
    ci, co = len(carry.ins), len(carry.out_shape)

    def body(*refs):
        parts = (refs[:ci], refs[ci:ci + co], refs[ci + co:])
        carry.start(*parts)
        carry.relay(*parts)
        carry.finish(*parts)

    return pl.pallas_call(
        body, name=name, out_shape=tuple(carry.out_shape),
        in_specs=[ANY] * ci, out_specs=tuple([ANY] * co), scratch_shapes=carry.sems,
    )(*carry.ins)


def _sibling_plan(big, small, by_block=()):
    n = len(big)
    n_copies = 4 * n + sum(a.shape[0] if t in by_block else 1 for t, a in enumerate(small))

    def copies(ins, outs, sems):
        send_sems, recv_sems = sems
        x, y, c = _my_place()
        pairs = []
        for t, (_, layer) in enumerate(big):
            for s in range(4):
                pairs.append((ins[t].at[layer, pl.ds(4 * (1 - c) + s, 1)], outs[t].at[pl.ds(s, 1)]))
        for t in range(len(small)):
            src, dst = ins[n + t], outs[n + t]
            pairs += [(src.at[b], dst.at[b]) for b in range(small[t].shape[0])] if t in by_block else [(src, dst)]
        return [pltpu.make_async_remote_copy(
            src_ref=src, dst_ref=dst, send_sem=send_sems.at[k], recv_sem=recv_sems.at[k],
            device_id=(x, y, 1 - c), device_id_type=MESH) for k, (src, dst) in enumerate(pairs)]

    def start(ins, outs, sems):
        for cp in copies(ins, outs, sems):
            cp.start()

    def finish(ins, outs, sems):
        for cp in copies(ins, outs, sems):
            cp.wait()

    out_shape = [SDS((4,) + a.shape[2:], a.dtype) for a, _ in big] + [SDS(a.shape, a.dtype) for a in small]
    sems = [pltpu.SemaphoreType.DMA((n_copies,)), pltpu.SemaphoreType.DMA((n_copies,))]
    return _Carried([a for a, _ in big] + list(small), out_shape, sems, start, finish)


def _chips_plan(big, small):
    n, n_small = len(big), len(small)
    max_rows = 512
    parts = [max(1, a.shape[1] // max_rows) for a in big]
    n_copies = 3 * (sum(parts) + n_small)

    def copies(ins, outs, sems, landing):
        send_sems, recv_sems, local_sems = sems
        x, y, c = _my_place()
        my_chip = 2 * x + y
        chips = [(1 - x, y), (x, 1 - y), (1 - x, 1 - y)]
        remote, local = [], []
        for chip in chips:
            to = 2 * chip[0] + chip[1]
            slot = to if landing else my_chip
            pairs = []
            for t in range(n):
                rows_per = big[t].shape[1] // parts[t]
                for p in range(parts[t]):
                    rows = pl.ds(p * rows_per, rows_per)
                    pairs.append((ins[t].at[to, rows], outs[t].at[slot, rows]))
            pairs += [(ins[t], outs[t].at[slot]) for t in range(n, n + n_small)]
            for src, dst in pairs:
                k = len(remote)
                remote.append(pltpu.make_async_remote_copy(
                    src_ref=src, dst_ref=dst, send_sem=send_sems.at[k], recv_sem=recv_sems.at[k],
                    device_id=(*chip, c), device_id_type=MESH))
        for t in range(n):
            local.append(pltpu.make_async_copy(ins[t].at[my_chip], outs[t].at[my_chip], local_sems.at[t]))
        for t in range(n, n + n_small):
            local.append(pltpu.make_async_copy(ins[t], outs[t].at[my_chip], local_sems.at[t]))
        return remote + local

    def start(ins, outs, sems):
        for cp in copies(ins, outs, sems, landing=False):
            cp.start()

    def finish(ins, outs, sems):
        for cp in copies(ins, outs, sems, landing=True):
            cp.wait()

    out_shape = [SDS(a.shape, a.dtype) for a in big] + [SDS((N_CHIP,) + a.shape, a.dtype) for a in small]
    sems = [pltpu.SemaphoreType.DMA((n_copies,)), pltpu.SemaphoreType.DMA((n_copies,)),
            pltpu.SemaphoreType.DMA((n + n_small,))]
    return _Carried(list(big) + list(small), out_shape, sems, start, finish)


def _all_plan(small):
    n = len(small)
    masks = [(m >> 2 & 1, m >> 1 & 1, m & 1) for m in range(1, N_DEV)]

    def copies(ins, outs, sems, landing):
        send_sems, recv_sems, local_sems = sems
        x, y, c = _my_place()
        me = 4 * x + 2 * y + c
        flip = lambda v, bit: 1 - v if bit else v
        remote = []
        for fx, fy, fc in masks:
            peer = (flip(x, fx), flip(y, fy), flip(c, fc))
            slot = 4 * peer[0] + 2 * peer[1] + peer[2] if landing else me
            for t in range(n):
                k = len(remote)
                remote.append(pltpu.make_async_remote_copy(
                    src_ref=ins[t], dst_ref=outs[t].at[slot], send_sem=send_sems.at[k], recv_sem=recv_sems.at[k],
                    device_id=peer, device_id_type=MESH))
        local = [pltpu.make_async_copy(ins[t], outs[t].at[me], local_sems.at[t]) for t in range(n)]
        return remote + local

    def start(ins, outs, sems):
        for cp in copies(ins, outs, sems, landing=False):
            cp.start()

    def finish(ins, outs, sems):
        for cp in copies(ins, outs, sems, landing=True):
            cp.wait()

    out_shape = [SDS((N_DEV,) + a.shape, a.dtype) for a in small]
    sems = [pltpu.SemaphoreType.DMA((7 * n,)), pltpu.SemaphoreType.DMA((7 * n,)), pltpu.SemaphoreType.DMA((n,))]
    return _Carried(list(small), out_shape, sems, start, finish)


def _join(*plans):
    plans = [p for p in plans if p is not None]
    if len(plans) <= 1:
        return plans[0] if plans else None

    def each(fn_name, ins, outs, sems):
        i = o = s = 0
        for p in plans:
            ni, no, ns = len(p.ins), len(p.out_shape), len(p.sems)
            getattr(p, fn_name)(ins[i:i + ni], outs[o:o + no], sems[s:s + ns])
            i, o, s = i + ni, o + no, s + ns

    return _Carried(sum((p.ins for p in plans), []), sum((p.out_shape for p in plans), []),
                    sum((p.sems for p in plans), []),
                    lambda i, o, s: each("start", i, o, s), lambda i, o, s: each("finish", i, o, s),
                    lambda i, o, s: each("relay", i, o, s))


def _row_block(rows, most=256):
    return min(rows, most)


def _add_own(tag, core, gs, layer, gots):
    n = len(gs)

    def body(core_ref, *refs):
        for a_ref, b_ref, o_ref in zip(refs[:n], refs[n:2 * n], refs[2 * n:]):
            o_ref[...] = (a_ref[...] + b_ref[...]).astype(o_ref.dtype)

    mine = lambda a: pl.BlockSpec((None, None) + a.shape[1:], lambda s, core: (layer, 4 * core[0] + s, 0, 0))
    theirs = lambda a: pl.BlockSpec((None,) + a.shape[1:], lambda s, core: (s, 0, 0))
    return pl.pallas_call(
        body, name=f"add_{tag}", out_shape=tuple(SDS(a.shape, BF16) for a in gots),
        grid_spec=pltpu.PrefetchScalarGridSpec(
            num_scalar_prefetch=1, grid=(4,),
            in_specs=[mine(a) for a in gots] + [theirs(a) for a in gots],
            out_specs=tuple(theirs(a) for a in gots)),
        compiler_params=_cp(("parallel",)),
    )(core, *gs, *gots)


def _add_lists(tag, own, got, grid=None, specs=None, dtype=F32):
    n = len(own)

    def body(*refs):
        for a, b, o in zip(refs[:n], refs[n:2 * n], refs[2 * n:]):
            o[...] = (a[...] + b[...]).astype(o.dtype)

    kw = {}
    if grid is not None:
        kw = dict(grid=grid, in_specs=list(specs) * 2, out_specs=tuple(specs),
                  compiler_params=_cp(("parallel",) * len(grid)))
    return pl.pallas_call(
        body, name=f"add_{tag}", out_shape=tuple(SDS(a.shape, dtype) for a in own), **kw)(*own, *got)


def _adamw_math(w, g, m, v):
    m = ADAM_B1 * m + (1.0 - ADAM_B1) * g
    v = ADAM_B2 * v + (1.0 - ADAM_B2) * (g * g)
    m_hat = m / (1.0 - ADAM_B1 ** ADAM_STEP)
    v_hat = v / (1.0 - ADAM_B2 ** ADAM_STEP)
    delta = -ADAM_LR * (m_hat / (jnp.sqrt(v_hat) + ADAM_EPS) + ADAM_WD * w)
    return delta, m, v


def _sum_slots_adamw(tag, slots, w, m, v):
    _, r, c = slots[0].shape
    rb = _row_block(r, most=512)

    def body(s0_ref, s1_ref, w_ref, m_ref, v_ref, g_ref, d_ref, nm_ref, nv_ref):
        first = pl.program_id(1) == 0
        g = _pair_sum([jnp.where(first, s0_ref[k], s1_ref[k]).astype(F32) for k in range(N_CHIP)])
        delta, nm, nv = _adamw_math(w_ref[...], g, m_ref[...], v_ref[...])
        g_ref[...] = g
        d_ref[...] = delta
        nm_ref[...] = nm
        nv_ref[...] = nv

    spec = pl.BlockSpec((None, rb, c), lambda j, l: (l, j, 0))
    sspec = pl.BlockSpec((N_CHIP, rb, c), lambda j, l: (0, j, 0))
    s = SDS((DEPTH, r, c), F32)
    return pl.pallas_call(
        body, name=f"adamw_{tag}", out_shape=(s, s, s, s),
        grid=(r // rb, DEPTH), in_specs=[sspec, sspec, spec, spec, spec], out_specs=(spec, spec, spec, spec),
        compiler_params=_cp(("parallel", "arbitrary")),
    )(*slots, w, m, v)


def _adamw_small(tag, entries, grid=None, sums=()):
    flat_in, in_specs, out_shape, out_specs, layout = [], [], [], [], []
    for slots, w, m, v, slot_spec, w_spec in entries:
        per_layer = isinstance(slots, (list, tuple))
        n_slot = len(slots) if per_layer else 1
        flat_in += (list(slots) if per_layer else [slots]) + [w, m, v]
        in_specs += [slot_spec] * n_slot + [w_spec] * 3
        out_shape += [SDS(w.shape, F32)] * 4
        out_specs += [w_spec] * 4
        layout.append((per_layer, n_slot))
    n_entry_in = len(flat_in)
    flat_in += list(sums)
    out_shape += [SDS(s.shape[1:], F32) for s in sums]
    n_in = len(flat_in)

    def body(*refs):
        for s_ref, o_ref in zip(refs[n_entry_in:n_in], refs[len(refs) - len(sums):]):
            o_ref[...] = _sum_slots(s_ref)
        i, o = 0, n_in
        for per_layer, n_slot in layout:
            s_refs = refs[i:i + n_slot]
            w_ref, m_ref, v_ref = refs[i + n_slot:i + n_slot + 3]
            outs = refs[o:o + 4]
            if per_layer:
                for l, s_ref in enumerate(s_refs):
                    at = (slice(l, l + 1),) if len(w_ref.shape) == 2 else (l,)
                    g = _sum_slots(s_ref)
                    res = (g,) + _adamw_math(w_ref[at], g, m_ref[at], v_ref[at])
                    for o_ref, val in zip(outs, res):
                        o_ref[at] = val
            else:
                g = _sum_slots(s_refs[0])
                res = (g,) + _adamw_math(w_ref[...], g, m_ref[...], v_ref[...])
                for o_ref, val in zip(outs, res):
                    o_ref[...] = val
            i += n_slot + 3
            o += 4

    kw = {}
    if grid is not None:
        kw = dict(grid=grid, in_specs=in_specs, out_specs=tuple(out_specs),
                  compiler_params=_cp(("parallel",) * len(grid)))
    res = pl.pallas_call(body, name=f"adamw_{tag}", out_shape=tuple(out_shape), **kw)(*flat_in)
    return [tuple(res[4 * e:4 * e + 4]) for e in range(len(entries))], res[4 * len(entries):]


def kernel(x, norm_g, w_in, b_in, ssm_log_dt, ssm_lam_re, ssm_lam_im, ssm_b_re, ssm_b_im, ssm_c_re, ssm_c_im, ssm_d, ssm_w_glu, ssm_b_glu, pool_w, pool_scale, w_branch_a, w_branch_b, w_out, final_norm_g, loss_target, m_norm_g, m_w_in, m_b_in, m_ssm_log_dt, m_ssm_lam_re, m_ssm_lam_im, m_ssm_b_re, m_ssm_b_im, m_ssm_c_re, m_ssm_c_im, m_ssm_d, m_ssm_w_glu, m_ssm_b_glu, m_pool_w, m_pool_scale, m_w_branch_a, m_w_branch_b, m_w_out, m_final_norm_g, v_norm_g, v_w_in, v_b_in, v_ssm_log_dt, v_ssm_lam_re, v_ssm_lam_im, v_ssm_b_re, v_ssm_b_im, v_ssm_c_re, v_ssm_c_im, v_ssm_d, v_ssm_w_glu, v_ssm_b_glu, v_pool_w, v_pool_scale, v_w_branch_a, v_w_branch_b, v_w_out, v_final_norm_g):
    weights = dict(norm_g=norm_g, w_in=w_in, b_in=b_in, ssm_log_dt=ssm_log_dt, ssm_lam_re=ssm_lam_re,
                   ssm_lam_im=ssm_lam_im, ssm_b_re=ssm_b_re, ssm_b_im=ssm_b_im, ssm_c_re=ssm_c_re,
                   ssm_c_im=ssm_c_im, ssm_d=ssm_d, ssm_w_glu=ssm_w_glu, ssm_b_glu=ssm_b_glu, pool_w=pool_w,
                   pool_scale=pool_scale, w_branch_a=w_branch_a, w_branch_b=w_branch_b, w_out=w_out,
                   final_norm_g=final_norm_g.reshape(1, D_MODEL))
    mom_m = dict(norm_g=m_norm_g, w_in=m_w_in, b_in=m_b_in, ssm_log_dt=m_ssm_log_dt, ssm_lam_re=m_ssm_lam_re,
                 ssm_lam_im=m_ssm_lam_im, ssm_b_re=m_ssm_b_re, ssm_b_im=m_ssm_b_im, ssm_c_re=m_ssm_c_re,
                 ssm_c_im=m_ssm_c_im, ssm_d=m_ssm_d, ssm_w_glu=m_ssm_w_glu, ssm_b_glu=m_ssm_b_glu,
                 pool_w=m_pool_w, pool_scale=m_pool_scale, w_branch_a=m_w_branch_a, w_branch_b=m_w_branch_b,
                 w_out=m_w_out, final_norm_g=m_final_norm_g.reshape(1, D_MODEL))
    mom_v = dict(norm_g=v_norm_g, w_in=v_w_in, b_in=v_b_in, ssm_log_dt=v_ssm_log_dt, ssm_lam_re=v_ssm_lam_re,
                 ssm_lam_im=v_ssm_lam_im, ssm_b_re=v_ssm_b_re, ssm_b_im=v_ssm_b_im, ssm_c_re=v_ssm_c_re,
                 ssm_c_im=v_ssm_c_im, ssm_d=v_ssm_d, ssm_w_glu=v_ssm_w_glu, ssm_b_glu=v_ssm_b_glu,
                 pool_w=v_pool_w, pool_scale=v_pool_scale, w_branch_a=v_w_branch_a, w_branch_b=v_w_branch_b,
                 w_out=v_w_out, final_norm_g=v_final_norm_g.reshape(1, D_MODEL))
    order = ["norm_g", "w_in", "b_in", "ssm_log_dt", "ssm_lam_re", "ssm_lam_im", "ssm_b_re", "ssm_b_im",
             "ssm_c_re", "ssm_c_im", "ssm_d", "ssm_w_glu", "ssm_b_glu", "pool_w", "pool_scale", "w_branch_a",
             "w_branch_b", "w_out", "final_norm_g"]
    big_names = ["w_in", "ssm_w_glu", "w_branch_a", "w_branch_b", "w_out"]

    log_dt3 = ssm_log_dt.reshape(DEPTH, N_GROUP, 1)
    b_t = lambda a: a.transpose(0, 1, 3, 2)
    for d in (weights, mom_m, mom_v):
        d["ssm_b_re"], d["ssm_b_im"] = b_t(d["ssm_b_re"]), b_t(d["ssm_b_im"])
    bt_re, bt_im = weights["ssm_b_re"], weights["ssm_b_im"]
    abar_re, abar_im, bbt_re, bbt_im = _s5_params(log_dt3, ssm_lam_re, ssm_lam_im, bt_re, bt_im)
    s5_args = (bbt_re, bbt_im, ssm_c_re, ssm_c_im, abar_re, abar_im, ssm_d)

    w16 = {n: weights[n].astype(BF16) for n in big_names}
    rest = [w16[n] for n in big_names[1:]]
    half = D_MODEL // 2
    wg_in = [None, [None, None]]
    wg_rest = [None, None]
    wg_in[0] = list(_run_carried("gather_w_in_l0", _gather_plan([w16["w_in"]], 0)))
    xs = [x.reshape(SEQ, D_MODEL)]
    saved = []
    for l in range(DEPTH):
        proj, moved = _norm_proj(l, xs[l], norm_g, wg_in[l], b_in,
                                 carry=_gather_plan([w16["w_in"]], 1, rows_of=(0, half)) if l == 0 else None)
        if l == 0:
            (wg_in[1][0],) = moved
        (states, y0), wg_rest[l] = _s5_scan_fwd(l, proj, *s5_args, carry=_gather_plan(rest, l, by_columns=(1, 2)))
        pooled = _pool_fwd(l, proj)
        wg_glu, wg_a, wg_b, wg_out = wg_rest[l]
        last = l == DEPTH - 1
        res, moved = _mix_fwd(
            l, xs[l], proj, y0, pooled, wg_glu, ssm_b_glu, pool_w, pool_scale, wg_a, wg_b, wg_out,
            carry=_gather_plan([w16["w_in"]], 1, rows_of=(half, half)) if l == 0 else None,
            head=(loss_target.reshape(SEQ, D_MODEL), weights["final_norm_g"]) if last else None)
        if l == 0:
            (wg_in[1][1],) = moved
        if last:
            dx, loss_part, g_final = res
        else:
            xs.append(res[0])
        saved.append((proj, states, y0, pooled))

    core = lax.axis_index("c").astype(jnp.int32).reshape(1)
    vec_names = ["norm_g", "b_in", "ssm_d", "ssm_b_glu", "pool_scale", "ssm_log_dt"]
    s5_names = ["ssm_log_dt", "ssm_lam_re", "ssm_lam_im", "ssm_b_re", "ssm_b_im"]
    mat_names = ["pool_w", "ssm_c_re", "ssm_c_im", "ssm_b_re", "ssm_b_im"]
    lane_sparse = ("ssm_c_re", "ssm_c_im", "ssm_b_re", "ssm_b_im")

    def dense(key, a):
        return a.reshape(-1, LANES) if key[0] in lane_sparse else a

    def undense(key, slots):
        return slots.reshape((N_CHIP, N_GROUP, GROUP_W, STATE)) if key[0] in lane_sparse else slots

    def add_small(tag, keys, own, got):
        out = [None] * len(keys)
        whole = [i for i, k in enumerate(keys) if k[0] not in mat_names]
        tiled = [i for i, k in enumerate(keys) if k[0] in mat_names]
        if whole:
            for i, r in zip(whole, _add_lists(f"{tag}_a", [own[i] for i in whole], [got[i] for i in whole])):
                out[i] = r
        if tiled:
            specs = [pl.BlockSpec((1, POOL_GROUP, POOL_GROUP), lambda j: (j, 0, 0)) if keys[i][0] == "pool_w"
                     else pl.BlockSpec((own[i].shape[0] // N_CHUNK, LANES), lambda j: (j, 0)) for i in tiled]
            for i, r in zip(tiled, _add_lists(f"{tag}_b", [own[i] for i in tiled], [got[i] for i in tiled],
                                              grid=(N_CHUNK,), specs=specs, dtype=BF16)):
                out[i] = r
        return out

    sm = {("final_norm_g", None): g_final, ("loss", None): loss_part}
    slots = {}
    grads = dict.fromkeys(big_names)

    class Wave:
        def __init__(self, tag, layer, big, keys):
            self.tag, self.layer, self.big, self.keys = tag, layer, big, keys

        def to_sibling(self):
            self.own = [dense(k, sm[k]) for k in self.keys]
            return _sibling_plan([(grads[n], self.layer) for n in self.big], self.own)

        def add(self, moved):
            nb = len(self.big)
            self.chip_big = list(_add_own(self.tag, core, [grads[n] for n in self.big], self.layer, moved[:nb])
                                 ) if nb else []
            self.chip_small = add_small(self.tag, self.keys, self.own, moved[nb:])

        def to_chips(self, big=None, small=True):
            self.sent = list(self.big if big is None else big), small
            return _chips_plan([self.chip_big[self.big.index(n)] for n in self.sent[0]],
                               self.chip_small if small else [])

        def landed(self, moved):
            names, small = self.sent
            for n, s in zip(names, moved[:len(names)]):
                slots[(n, self.layer)] = s
            if small:
                for k, s in zip(self.keys, moved[len(names):]):
                    slots[k] = undense(k, s)
            return moved[len(names) + (len(self.keys) if small else 0):]

    def s5_param_grads(l, g_abar_re, g_abar_im, g_bbt_re, g_bbt_im):
        g = _s5_params_bwd(l, log_dt3, ssm_lam_re, ssm_lam_im, bt_re, bt_im, g_abar_re, g_abar_im, g_bbt_re, g_bbt_im)
        sm[("ssm_log_dt", l)] = g[0].reshape(1, N_GROUP)
        for n, a in zip(s5_names[1:], g[1:]):
            sm[(n, l)] = a

    small1 = ["b_in", "ssm_d", "ssm_b_glu", "pool_scale", "pool_w", "ssm_c_re", "ssm_c_im"] + s5_names
    w1 = Wave("chip1", 1, list(big_names), [(n, 1) for n in small1] + [("final_norm_g", None), ("loss", None)])
    early = Wave("chip0e", 0, big_names[1:], [("pool_w", 0), ("pool_scale", 0), ("ssm_b_glu", 0)])
    mid = Wave("chip0m", 0, [], [(n, 0) for n in ["ssm_c_re", "ssm_c_im", "ssm_d"] + s5_names] + [("norm_g", 1)])

    mix_prev = None
    for l in reversed(range(DEPTH)):
        proj, states, y0, pooled = saved[l]
        wg_glu, wg_a, wg_b, wg_out = wg_rest[l]
        res, moved = _mix_bwd(l, dx, proj, y0, pooled, wg_glu, ssm_b_glu, pool_w, pool_scale, wg_a, wg_b, wg_out,
                              mix_prev, carry=None if l == 1 else w1.to_chips(big=["w_in"], small=False))
        if l == 0:
            w1.landed(moved)
        dproj, dy0, dpooled = res[:3]
        mix_prev = list(res[3:7])
        grads["w_out"], grads["w_branch_a"], grads["w_branch_b"], grads["ssm_w_glu"] = mix_prev
        sm[("pool_w", l)], sm[("pool_scale", l)], sm[("ssm_b_glu", l)] = res[7:]
        dproj = _pool_bwd(l, dpooled, dproj)
        carry = None if l == 1 else _join(w1.to_chips(big=big_names[1:]), early.to_sibling())
        res, moved = _s5_scan_bwd(l, dy0, proj, states, *s5_args, dproj, carry=carry)
        if l == 0:
            early.add(w1.landed(moved))
        dproj, g_bbt_re, g_bbt_im, sm[("ssm_c_re", l)], sm[("ssm_c_im", l)], g_abar_re, g_abar_im, sm[("ssm_d", l)] = res
        s5_param_grads(l, g_abar_re, g_abar_im, g_bbt_re, g_bbt_im)
        if l == 1:
            (grads["w_in"], sm[("b_in", l)]), _ = _proj_wgrad(l, xs[l], norm_g, dproj, None)
            (dx, sm[("norm_g", l)]), moved = _proj_dgrad(l, dx, xs[l], norm_g, dproj, wg_in[l], carry=w1.to_sibling())
            w1.add(moved)
        else:
            theirs, moved = _proj_wgrad_part(l, xs[l], norm_g, dproj, core, False,
                                             carry=_join(early.to_chips(), mid.to_sibling()))
            mid.add(early.landed(moved))
            mine, moved = _proj_wgrad_part(l, xs[l], norm_g, dproj, core, True,
                                           carry=_join(mid.to_chips(), _sibling_plan([], [theirs], by_block=(0,))))
            (got,) = mid.landed(moved)
            block = pl.BlockSpec((1, D_MODEL, WIDTH), lambda j: (j, 0, 0))
            (chip_w_in,) = _add_lists("chip0l", [mine], [got], grid=(N_CHIP,), specs=[block], dtype=BF16)
            (dx, sm[("norm_g", l)], sm[("b_in", l)]), moved = _proj_dgrad(
                l, dx, xs[l], norm_g, dproj, wg_in[l], carry=_chips_plan([chip_w_in], []), bias_grad=True)
            (slots[("w_in", l)],) = moved
    grad_x = dx.reshape(1, SEQ, D_MODEL)
    slots[("norm_g", 0)], slots[("b_in", 0)] = _run_carried(
        "exchange_last", _all_plan([sm[("norm_g", 0)], sm[("b_in", 0)]]))

    res = {}
    for n in big_names:
        res[n] = _sum_slots_adamw(n, [slots[(n, l)] for l in range(DEPTH)], weights[n], mom_m[n], mom_v[n])
    per_layer = lambda n: [slots[(n, l)] for l in range(DEPTH)]
    names_a = vec_names + ["ssm_lam_re", "ssm_lam_im"]
    entries_a = [(per_layer(n), weights[n], mom_m[n], mom_v[n], None, None) for n in names_a]
    n = "final_norm_g"
    entries_a.append((slots[(n, None)], weights[n], mom_m[n], mom_v[n], None, None))
    out_a, (loss,) = _adamw_small("small_a", entries_a, sums=[slots[("loss", None)]])
    loss = loss.reshape(())
    for n, r in zip(names_a + ["final_norm_g"], out_a):
        res[n] = r
    res["final_norm_g"] = tuple(a.reshape(D_MODEL) for a in res["final_norm_g"])
    pw_s = pl.BlockSpec((N_CHIP, 1, POOL_GROUP, POOL_GROUP), lambda j: (0, j, 0, 0))
    pw_w = pl.BlockSpec((DEPTH, 1, POOL_GROUP, POOL_GROUP), lambda j: (0, j, 0, 0))
    c_s = pl.BlockSpec((N_CHIP, CH_G, GROUP_W, STATE), lambda j: (0, j, 0, 0))
    c_w = pl.BlockSpec((DEPTH, CH_G, GROUP_W, STATE), lambda j: (0, j, 0, 0))
    entries_b = [(per_layer(n), weights[n], mom_m[n], mom_v[n], pw_s if n == "pool_w" else c_s,
                  pw_w if n == "pool_w" else c_w) for n in mat_names]
    out_b, _ = _adamw_small("small_b", entries_b, grid=(N_CHUNK,))
    for n, r in zip(mat_names, out_b):
        res[n] = tuple(b_t(a) for a in r) if n in ("ssm_b_re", "ssm_b_im") else r

    outs = [loss, grad_x]
    for i in range(4):
        outs += [res[n][i] for n in order]
    return tuple(outs)
```

```python
import math

import jax
import jax.numpy as jnp
from jax import lax
from jax.experimental import pallas as pl
from jax.experimental.pallas import tpu as pltpu

F32 = jnp.float32
BF16 = jnp.bfloat16

SEQ = 2048
D_MODEL = 1024
N_IN = 4096
WIDTH = 512
N_GROUP = 32
GROUP_W = 16
STATE = 64
N_STATE = N_GROUP * STATE
N_CHUNK = 4
CH_G = N_GROUP // N_CHUNK
CH_W = WIDTH // N_CHUNK
CH_S = N_STATE // N_CHUNK
N_DEV = 8
N_CHIP = 4
POOL_WINDOWS = (2, 4, 8, 16)
POOL_GROUP = 128
EPS = 1e-6
DEPTH = 2

ADAM_LR = 0.001
ADAM_B1 = 0.9
ADAM_B2 = 0.999
ADAM_EPS = 1e-08
ADAM_WD = 0.01
ADAM_STEP = 10

LANES = 128
SUBLANES = 8
TILE_M = 256
VMEM_LIMIT = 48 * 1024 * 1024
VMEM_LIMIT_BIG = 60 * 1024 * 1024
MESH = pl.DeviceIdType.MESH
ANY = pl.BlockSpec(memory_space=pl.ANY)

GELU_C = math.sqrt(2.0 / math.pi)
GELU_A = 0.044715

SDS = jax.ShapeDtypeStruct


def _cp(sem=None, limit=VMEM_LIMIT):
    return pltpu.CompilerParams(dimension_semantics=sem, vmem_limit_bytes=limit)


def _dot(a, b):
    return jnp.dot(a, b, preferred_element_type=F32)


def _dot_nt(a, b):
    return lax.dot_general(a, b, (((1,), (1,)), ((), ())), preferred_element_type=F32)


def _dot_tn(a, b):
    return lax.dot_general(a, b, (((0,), (0,)), ((), ())), preferred_element_type=F32)


def _sig(x):
    return jax.nn.sigmoid(x)


def _rms(x):
    rs = lax.rsqrt(jnp.mean(x * x, axis=-1, keepdims=True) + EPS)
    return rs, x * rs


def _slot(n):
    return 4 * (n % 2) + n // 2


def _const(shape):
    n = len(shape)
    return pl.BlockSpec(shape, lambda *_: (0,) * n)


def _pair_sum(vals):
    while len(vals) > 1:
        vals = [vals[i] + vals[i + 1] for i in range(0, len(vals), 2)]
    return vals[0]


def _sum_slots(s_ref):
    return _pair_sum([s_ref[k].astype(F32) for k in range(s_ref.shape[0])])


def _s5_param_fn(log_dt, lam_re, lam_im, bt_re, bt_im):
    dt = jnp.exp(log_dt)
    mag = jnp.exp(lam_re * dt)
    ang = lam_im * dt
    abar_re = mag * jnp.cos(ang)
    abar_im = mag * jnp.sin(ang)
    num_re = abar_re - 1.0
    num_im = abar_im
    den = lam_re * lam_re + lam_im * lam_im
    coef_re = (num_re * lam_re + num_im * lam_im) / den
    coef_im = (num_im * lam_re - num_re * lam_im) / den
    bbar_re = coef_re[..., None, :] * bt_re - coef_im[..., None, :] * bt_im
    bbar_im = coef_re[..., None, :] * bt_im + coef_im[..., None, :] * bt_re
    return abar_re, abar_im, bbar_re, bbar_im


def _s5_params(log_dt, lam_re, lam_im, bt_re, bt_im):
    def body(ld, lr, li, br, bi, o_ar, o_ai, o_br, o_bi):
        ar, ai, bbr, bbi = _s5_param_fn(ld[...], lr[...], li[...], br[...], bi[...])
        o_ar[...] = ar
        o_ai[...] = ai
        o_br[...] = bbr
        o_bi[...] = bbi

    return pl.pallas_call(
        body, name="s5_params",
        out_shape=(SDS(lam_re.shape, F32), SDS(lam_re.shape, F32), SDS(bt_re.shape, F32), SDS(bt_re.shape, F32)),
    )(log_dt, lam_re, lam_im, bt_re, bt_im)


def _s5_params_bwd(layer, log_dt, lam_re, lam_im, bt_re, bt_im, g_ar, g_ai, g_br, g_bi):
    def body(ld, lr, li, br, bi, car, cai, cbr, cbi, o_ld, o_lr, o_li, o_br, o_bi):
        _, vjp = jax.vjp(_s5_param_fn, ld[...], lr[...], li[...], br[...], bi[...])
        d_ld, d_lr, d_li, d_br, d_bi = vjp((car[...], cai[...], cbr[...], cbi[...]))
        o_ld[...] = d_ld
        o_lr[...] = d_lr
        o_li[...] = d_li
        o_br[...] = d_br
        o_bi[...] = d_bi

    one = lambda shape: pl.BlockSpec((None,) + shape, lambda i: (layer,) + (0,) * len(shape))
    whole = lambda shape: _const(shape)
    vec, lam, mat = (N_GROUP, 1), (N_GROUP, STATE), (N_GROUP, GROUP_W, STATE)
    return pl.pallas_call(
        body, name=f"s5_params_bwd_l{layer}", grid=(1,),
        in_specs=[one(vec), one(lam), one(lam), one(mat), one(mat), whole(lam), whole(lam), whole(mat), whole(mat)],
        out_specs=(whole(vec), whole(lam), whole(lam), whole(mat), whole(mat)),
        out_shape=(SDS(vec, F32), SDS(lam, F32), SDS(lam, F32), SDS(mat, F32), SDS(mat, F32)),
    )(log_dt, lam_re, lam_im, bt_re, bt_im, g_ar, g_ai, g_br, g_bi)


def _norm_proj(layer, x, norm_g, wg_in, b_in, carry=None):
    n_w = len(wg_in)

    def body(x_ref, g_ref, b_ref, *refs):
        w_refs, o_ref = refs[:n_w], refs[n_w]
        _, xn = _rms(x_ref[...])
        h = (xn * g_ref[layer:layer + 1, :]).astype(BF16)
        for k in range(N_DEV):
            cols = slice(k * WIDTH, (k + 1) * WIDTH)
            acc = b_ref[layer:layer + 1, cols]
            row = 0
            for w_ref in w_refs:
                rows = w_ref.shape[1]
                acc = acc + _dot(h[:, row:row + rows], w_ref[k])
                row += rows
            o_ref[:, cols] = acc

    (proj,), moved = _pcall(
        body, name=f"norm_proj_l{layer}",
        out_shape=[SDS((SEQ, N_IN), F32)],
        grid=(SEQ // TILE_M,),
        in_specs=[pl.BlockSpec((TILE_M, D_MODEL), lambda i: (i, 0)),
                  _const((DEPTH, D_MODEL)),
                  _const((DEPTH, N_IN))] + [_const(w.shape) for w in wg_in],
        out_specs=[pl.BlockSpec((TILE_M, N_IN), lambda i: (i, 0))],
        args=[x, norm_g, b_in, *wg_in], sem=("parallel",), carry=carry)
    return proj, moved


TIME_BLK = 512
N_TBLK = SEQ // TIME_BLK
N_PANEL = CH_S // LANES
STATE_SHAPE = (N_PANEL, SEQ * SUBLANES, LANES)


def _s5_layer_specs(layer):
    mat = lambda: pl.BlockSpec((None, N_GROUP, GROUP_W, STATE), lambda i: (layer, 0, 0, 0))
    ab = lambda: pl.BlockSpec((None, N_GROUP, STATE), lambda i: (layer, 0, 0))
    return [mat(), mat(), mat(), mat(), ab(), ab(), _const((DEPTH, WIDTH))]


def _s5_layer_scratch():
    return [pltpu.VMEM((N_CHUNK, CH_W, CH_S), BF16)] * 4 + [pltpu.VMEM((8, CH_S), F32)] * 2


def _s5_layer_fill(btre_ref, btim_ref, cre_ref, cim_ref, are_ref, aim_ref, bdre, bdim, ctre, ctim, a1, a2):
    for m in (bdre, bdim, ctre, ctim):
        m[...] = jnp.zeros_like(m)
    for grp in range(N_GROUP):
        k, g = divmod(grp, CH_G)
        rows = slice(g * GROUP_W, (g + 1) * GROUP_W)
        cols = slice(g * STATE, (g + 1) * STATE)
        bdre[k, rows, cols] = btre_ref[grp].astype(BF16)
        bdim[k, rows, cols] = btim_ref[grp].astype(BF16)
        ctre[k, rows, cols] = cre_ref[grp].astype(BF16)
        ctim[k, rows, cols] = cim_ref[grp].astype(BF16)
        ar = are_ref[grp:grp + 1, :]
        ai = aim_ref[grp:grp + 1, :]
        a1[k:k + 1, cols] = ar
        a1[N_CHUNK + k:N_CHUNK + k + 1, cols] = ar
        a2[k:k + 1, cols] = -ai
        a2[N_CHUNK + k:N_CHUNK + k + 1, cols] = ai


SCAN_UNROLL = 16


def _panels(tile):
    return [tile[:, p * LANES:(p + 1) * LANES] for p in range(N_PANEL)]


def _rows_load(ref, row):
    return jnp.concatenate([ref[p, pl.ds(row, TIME_BLK, stride=SUBLANES), :] for p in range(N_PANEL)], axis=1)


def _rows_store(ref, row, val):
    for p in range(N_PANEL):
        ref[p, pl.ds(row, TIME_BLK, stride=SUBLANES), :] = val[:, p * LANES:(p + 1) * LANES]


def _s5_scan_fwd(layer, proj, bbt_re, bbt_im, c_re, c_im, abar_re, abar_im, d_skip, carry=None):
    def body(u_ref, btre_ref, btim_ref, cre_ref, cim_ref, are_ref, aim_ref, d_ref, s_ref, y_ref,
             bdre, bdim, ctre, ctim, a1, a2, state):
        @pl.when(pl.program_id(0) == 0)
        def _():
            _s5_layer_fill(btre_ref, btim_ref, cre_ref, cim_ref, are_ref, aim_ref, bdre, bdim, ctre, ctim, a1, a2)
            state[...] = jnp.zeros_like(state)

        for k in range(N_CHUNK):
            ub = u_ref[:, k * CH_W:(k + 1) * CH_W].astype(BF16)
            _rows_store(s_ref, k, _dot(ub, bdre[k]))
            _rows_store(s_ref, N_CHUNK + k, _dot(ub, bdim[k]))
        m1 = _panels(a1[...])
        m2 = _panels(a2[...])

        def steps(n, tile):
            for r in range(SCAN_UNROLL):
                rows = pl.ds(pl.multiple_of((n * SCAN_UNROLL + r) * 8, 8), 8)
                tile = [m1[p] * tile[p] + m2[p] * pltpu.roll(tile[p], N_CHUNK, 0) + s_ref[p, rows, :]
                        for p in range(N_PANEL)]
                for p in range(N_PANEL):
                    s_ref[p, rows, :] = tile[p]
            return tile

        tile = lax.fori_loop(0, TIME_BLK // SCAN_UNROLL, steps, _panels(state[...]))
        state[...] = jnp.concatenate(tile, axis=1)
        d = d_ref[layer:layer + 1, :]
        for k in range(N_CHUNK):
            cols = slice(k * CH_W, (k + 1) * CH_W)
            y = (_dot_nt(_rows_load(s_ref, k).astype(BF16), ctre[k])
                 - _dot_nt(_rows_load(s_ref, N_CHUNK + k).astype(BF16), ctim[k]))
            y_ref[:, cols] = y + d[:, cols] * u_ref[:, cols]

    return _pcall(
        body, name=f"s5_fwd_l{layer}",
        out_shape=(SDS(STATE_SHAPE, F32), SDS((SEQ, WIDTH), F32)),
        grid=(N_TBLK,),
        in_specs=[pl.BlockSpec((TIME_BLK, WIDTH), lambda i: (i, 0))] + _s5_layer_specs(layer),
        out_specs=(pl.BlockSpec((N_PANEL, TIME_BLK * SUBLANES, LANES), lambda i: (0, i, 0)),
                   pl.BlockSpec((TIME_BLK, WIDTH), lambda i: (i, 0))),
        scratch_shapes=_s5_layer_scratch() + [pltpu.VMEM((8, CH_S), F32)],
        args=[proj, bbt_re, bbt_im, c_re, c_im, abar_re, abar_im, d_skip], sem=("arbitrary",), carry=carry)


def _s5_scan_bwd(layer, dy0, proj, states, bbt_re, bbt_im, c_re, c_im, abar_re, abar_im, d_skip, dproj,
                 carry=None):
    def body(dy_ref, u_ref, s_ref, sprev_ref, btre_ref, btim_ref, cre_ref, cim_ref, are_ref, aim_ref, d_ref, _,
             du_ref, gbre_ref, gbim_ref, gcre_ref, gcim_ref, gare_ref, gaim_ref, gd_ref,
             lam_ref, bdre, bdim, ctre, ctim, a1, a2, state, acc1, acc2, gbre, gbim, gcre, gcim, gd):
        step_id = pl.program_id(0)

        @pl.when(step_id == 0)
        def _():
            _s5_layer_fill(btre_ref, btim_ref, cre_ref, cim_ref, are_ref, aim_ref, bdre, bdim, ctre, ctim, a1, a2)
            for r in (state, acc1, acc2, gbre, gbim, gcre, gcim, gd):
                r[...] = jnp.zeros_like(r)

        for k in range(N_CHUNK):
            dyb = dy_ref[:, k * CH_W:(k + 1) * CH_W].astype(BF16)
            _rows_store(lam_ref, k, _dot(dyb, ctre[k]))
            _rows_store(lam_ref, N_CHUNK + k, -_dot(dyb, ctim[k]))
            gcre[k] += _dot_tn(dyb, _rows_load(s_ref, k).astype(BF16))
            gcim[k] -= _dot_tn(dyb, _rows_load(s_ref, N_CHUNK + k).astype(BF16))

        m1 = _panels(a1[...])
        m2 = _panels(-a2[...])
        has_before = (step_id < N_TBLK - 1).astype(F32)

        def one(t8, c, first_token):
            tile, swapped, p1, p2 = c
            rows = pl.ds(t8, 8)
            tile = [m1[p] * tile[p] + m2[p] * swapped[p] + lam_ref[p, rows, :] for p in range(N_PANEL)]
            swapped = [pltpu.roll(tile[p], N_CHUNK, 0) for p in range(N_PANEL)]
            for p in range(N_PANEL):
                lam_ref[p, rows, :] = tile[p]
            if first_token:
                before = [sprev_ref[p] * has_before for p in range(N_PANEL)]
            else:
                before = [s_ref[p, pl.ds(t8 - 8, 8), :] for p in range(N_PANEL)]
            p1 = [p1[p] + tile[p] * before[p] for p in range(N_PANEL)]
            p2 = [p2[p] + swapped[p] * before[p] for p in range(N_PANEL)]
            return tile, swapped, p1, p2

        def steps(n, c):
            for r in range(SCAN_UNROLL):
                t8 = pl.multiple_of((TIME_BLK - 1 - (n * SCAN_UNROLL + r)) * 8, 8)
                c = one(t8, c, False)
            return c

        tile0 = _panels(state[...])
        c = (tile0, [pltpu.roll(t, N_CHUNK, 0) for t in tile0], _panels(acc1[...]), _panels(acc2[...]))
        c = lax.fori_loop(0, TIME_BLK // SCAN_UNROLL - 1, steps, c)
        for r in range(SCAN_UNROLL - 1, -1, -1):
            c = one(r * 8, c, r == 0)
        state[...] = jnp.concatenate(c[0], axis=1)
        acc1[...] = jnp.concatenate(c[2], axis=1)
        acc2[...] = jnp.concatenate(c[3], axis=1)

        d = d_ref[layer:layer + 1, :]
        for k in range(N_CHUNK):
            cols = slice(k * CH_W, (k + 1) * CH_W)
            lrb = _rows_load(lam_ref, k).astype(BF16)
            lib = _rows_load(lam_ref, N_CHUNK + k).astype(BF16)
            u = u_ref[:, cols]
            ub = u.astype(BF16)
            dy = dy_ref[:, cols]
            du = dy * d[:, cols] + _dot_nt(lrb, bdre[k]) + _dot_nt(lib, bdim[k])
            du_ref[:, cols] = du.astype(BF16)
            gbre[k] += _dot_tn(ub, lrb)
            gbim[k] += _dot_tn(ub, lib)
        gd[...] += jnp.sum(dy_ref[...] * u_ref[...], axis=0, keepdims=True)

        @pl.when(step_id == N_TBLK - 1)
        def _():
            gd_ref[...] = gd[...]
            ga_re = acc1[0:N_CHUNK, :] + acc1[N_CHUNK:, :]
            ga_im = acc2[0:N_CHUNK, :] - acc2[N_CHUNK:, :]
            for grp in range(N_GROUP):
                k, g = divmod(grp, CH_G)
                rows = slice(g * GROUP_W, (g + 1) * GROUP_W)
                cols = slice(g * STATE, (g + 1) * STATE)
                gcre_ref[grp] = gcre[k, rows, cols]
                gcim_ref[grp] = gcim[k, rows, cols]
                gbre_ref[grp] = gbre[k, rows, cols]
                gbim_ref[grp] = gbim[k, rows, cols]
                gare_ref[grp:grp + 1, :] = ga_re[k:k + 1, cols]
                gaim_ref[grp:grp + 1, :] = ga_im[k:k + 1, cols]

    back = lambda i: N_TBLK - 1 - i
    tok = lambda: pl.BlockSpec((TIME_BLK, WIDTH), lambda i: (back(i), 0))
    mat = lambda: _const((N_GROUP, GROUP_W, STATE))
    acc_mat = pltpu.VMEM((N_CHUNK, CH_W, CH_S), F32)
    return _pcall(
        body, name=f"s5_bwd_l{layer}",
        out_shape=(SDS((SEQ, N_IN), BF16), SDS((N_GROUP, GROUP_W, STATE), F32), SDS((N_GROUP, GROUP_W, STATE), F32),
                   SDS((N_GROUP, GROUP_W, STATE), F32), SDS((N_GROUP, GROUP_W, STATE), F32),
                   SDS((N_GROUP, STATE), F32), SDS((N_GROUP, STATE), F32), SDS((1, WIDTH), F32)),
        grid=(N_TBLK,),
        in_specs=[tok(), tok(),
                  pl.BlockSpec((N_PANEL, TIME_BLK * SUBLANES, LANES), lambda i: (0, back(i), 0)),
                  pl.BlockSpec((N_PANEL, SUBLANES, LANES), lambda i: (0, jnp.maximum(back(i) * TIME_BLK - 1, 0), 0))]
        + _s5_layer_specs(layer) + [ANY],
        out_specs=(tok(), mat(), mat(), mat(), mat(), _const((N_GROUP, STATE)), _const((N_GROUP, STATE)),
                   _const((1, WIDTH))),
        scratch_shapes=[pltpu.VMEM((N_PANEL, TIME_BLK * SUBLANES, LANES), F32)] + _s5_layer_scratch()
        + [pltpu.VMEM((8, CH_S), F32)] * 3 + [acc_mat] * 4 + [pltpu.VMEM((1, WIDTH), F32)],
        args=[dy0, proj, states, states, bbt_re, bbt_im, c_re, c_im, abar_re, abar_im, d_skip, dproj],
        aliases={11: 0}, sem=("arbitrary",), limit=VMEM_LIMIT_BIG, carry=carry)


def _pool_counts(win):
    t = lax.broadcasted_iota(jnp.int32, (SEQ, POOL_GROUP), 0)
    return t, jnp.minimum(t + 1, win).astype(F32)


def _pool_fwd(layer, proj):
    def body(u_ref, o_ref):
        for gi, win in enumerate(POOL_WINDOWS):
            cols = slice(gi * POOL_GROUP, (gi + 1) * POOL_GROUP)
            u = u_ref[:, cols]
            t, count = _pool_counts(win)
            acc = u
            k = 1
            while k < win:
                acc = acc + jnp.where(t >= k, pltpu.roll(acc, k, 0), 0.0)
                k *= 2
            o_ref[:, cols] = acc / count - u

    return pl.pallas_call(
        body, name=f"pool_fwd_l{layer}",
        out_shape=SDS((SEQ, WIDTH), F32),
        grid=(1,),
        in_specs=[pl.BlockSpec((SEQ, WIDTH), lambda i: (0, 2))],
        out_specs=pl.BlockSpec((SEQ, WIDTH), lambda i: (0, 0)),
        compiler_params=_cp(("arbitrary",)),
    )(proj)


def _gelu_parts(y0):
    t = jnp.tanh(GELU_C * (y0 + GELU_A * (y0 * y0 * y0)))
    return t, 0.5 * y0 * (1.0 + t)


def _mix_forward(layer, p_ref, y0_ref, pooled_ref, wglu_ref, bglu_ref, pw_ref, scale_ref, wa_ref, wb_ref):
    za = p_ref[:, WIDTH:2 * WIDTH]
    zb = p_ref[:, 3 * WIDTH:4 * WIDTH]
    ga = p_ref[:, 4 * WIDTH:4 * WIDTH + D_MODEL]
    gb = p_ref[:, 4 * WIDTH + D_MODEL:]
    y0 = y0_ref[...]
    t, y1 = _gelu_parts(y0)
    y1b = y1.astype(BF16)
    q = _dot(y1b, wglu_ref[...].reshape(WIDTH, WIDTH)) + bglu_ref[layer:layer + 1, :]
    sq = _sig(q)
    y2 = y1 * sq
    sza = _sig(za)
    silu_za = za * sza
    ya = y2 * silu_za
    pooled = pooled_ref[...]
    mixed = jnp.concatenate(
        [_dot(pooled[:, g * POOL_GROUP:(g + 1) * POOL_GROUP].astype(BF16), pw_ref[g].astype(BF16))
         for g in range(len(POOL_WINDOWS))], axis=1)
    szb = _sig(zb)
    silu_zb = zb * szb
    scale = scale_ref[layer:layer + 1, :]
    ms = mixed * scale
    yb = ms * silu_zb
    yab = ya.astype(BF16)
    ybb = yb.astype(BF16)
    ma = _dot(yab, wa_ref[...])
    mb = _dot(ybb, wb_ref[...])
    sga = _sig(ga)
    sgb = _sig(gb)
    merged = sga * ma + sgb * mb
    return dict(za=za, zb=zb, y0=y0, t=t, y1=y1, y1b=y1b, sq=sq, y2=y2, sza=sza, silu_za=silu_za,
                pooled=pooled, mixed=mixed, szb=szb, silu_zb=silu_zb, scale=scale, ms=ms, yab=yab, ybb=ybb,
                ma=ma, mb=mb, sga=sga, sgb=sgb, merged=merged)


def _mix_weight_specs(layer):
    return [_const((N_DEV, WIDTH // N_DEV, WIDTH)),
            _const((DEPTH, WIDTH)),
            pl.BlockSpec((None, 4, POOL_GROUP, POOL_GROUP), lambda i: (layer, 0, 0, 0)),
            _const((DEPTH, WIDTH)),
            _const((WIDTH, D_MODEL)),
            _const((WIDTH, D_MODEL)),
            _const((N_DEV, D_MODEL // N_DEV, D_MODEL))]


def _loss_head(x, t_ref, g_ref, dx_ref, loss_ref, gg_ref):
    @pl.when(pl.program_id(0) == 0)
    def _():
        loss_ref[...] = jnp.zeros_like(loss_ref)
        gg_ref[...] = jnp.zeros_like(gg_ref)

    g = g_ref[...]
    rs, xn = _rms(x)
    err = xn * g - t_ref[...]
    loss_ref[...] += 0.5 * jnp.sum(jnp.mean(err * err, axis=-1, keepdims=True), axis=0, keepdims=True)
    dy = err * (1.0 / D_MODEL)
    gg_ref[...] += jnp.sum(dy * xn, axis=0, keepdims=True)
    dxn = dy * g
    dx_ref[...] = rs * (dxn - xn * jnp.mean(dxn * xn, axis=-1, keepdims=True))


def _mix_fwd(layer, x, proj, y0, pooled, wg_glu, b_glu, pool_w, pool_scale, wg_a, wg_b, wg_out, carry=None,
             head=None):
    def body(x_ref, p_ref, y0_ref, pooled_ref, wglu_ref, bglu_ref, pw_ref, scale_ref, wa_ref, wb_ref,
             wout_ref, *rest):
        f = _mix_forward(layer, p_ref, y0_ref, pooled_ref, wglu_ref, bglu_ref, pw_ref, scale_ref, wa_ref, wb_ref)
        wout = wout_ref[...].reshape(D_MODEL, D_MODEL)
        x_next = x_ref[...] + _dot(f["merged"].astype(BF16), wout)
        if head is None:
            rest[0][...] = x_next
        else:
            _loss_head(x_next, *rest)

    tile = lambda: pl.BlockSpec((TILE_M, D_MODEL), lambda i: (i, 0))
    if head is None:
        extra, out_shape, out_specs = [], [SDS((SEQ, D_MODEL), F32)], [tile()]
    else:
        extra = list(head)
        out_shape = [SDS((SEQ, D_MODEL), F32), SDS((1, 1), F32), SDS((1, D_MODEL), F32)]
        out_specs = [tile(), _const((1, 1)), _const((1, D_MODEL))]
    return _pcall(
        body, name=f"mix_fwd_l{layer}",
        out_shape=out_shape,
        grid=(SEQ // TILE_M,),
        in_specs=[tile(),
                  pl.BlockSpec((TILE_M, N_IN), lambda i: (i, 0)),
                  pl.BlockSpec((TILE_M, WIDTH), lambda i: (i, 0)),
                  pl.BlockSpec((TILE_M, WIDTH), lambda i: (i, 0))] + _mix_weight_specs(layer)
        + ([tile(), _const((1, D_MODEL))] if head else []),
        out_specs=out_specs,
        args=[x, proj, y0, pooled, wg_glu, b_glu, pool_w, pool_scale, wg_a, wg_b, wg_out] + extra,
        sem=("parallel",) if head is None else ("arbitrary",), carry=carry)


def _big_shapes():
    return dict(w_out=(DEPTH, N_DEV, D_MODEL // N_DEV, D_MODEL), w_branch_a=(DEPTH, N_DEV, WIDTH, D_MODEL // N_DEV),
                w_branch_b=(DEPTH, N_DEV, WIDTH, D_MODEL // N_DEV), ssm_w_glu=(DEPTH, N_DEV, WIDTH // N_DEV, WIDTH),
                w_in=(DEPTH, N_DEV, D_MODEL, WIDTH))


def _mix_bwd(layer, dx_next, proj, y0, pooled, wg_glu, b_glu, pool_w, pool_scale, wg_a, wg_b, wg_out, prev,
             carry=None):
    n_k = N_DEV
    n_prev = 0 if prev is None else len(prev)

    def body(*refs):
        (dx_ref, p_ref, y0_ref, pooled_ref, wglu_ref, bglu_ref, pw_ref, scale_ref, wa_ref, wb_ref,
         wout_ref) = refs[:11]
        (dproj_ref, dy0_ref, dpooled_ref, gwout_ref, gwa_ref, gwb_ref, gwglu_ref, gpw_ref,
         gscale_ref, gbglu_ref) = refs[11 + n_prev:]

        @pl.when(pl.program_id(0) == 0)
        def _():
            for r in (gwout_ref, gwa_ref, gwb_ref, gwglu_ref, gpw_ref, gscale_ref, gbglu_ref):
                r[...] = jnp.zeros_like(r)

        f = _mix_forward(layer, p_ref, y0_ref, pooled_ref, wglu_ref, bglu_ref, pw_ref, scale_ref, wa_ref, wb_ref)
        wglu = wglu_ref[...].reshape(WIDTH, WIDTH)
        wout = wout_ref[...].reshape(D_MODEL, D_MODEL)
        blk = D_MODEL // n_k
        dxb = dx_ref[...].astype(BF16)
        dmerged = _dot_nt(dxb, wout)
        gwout = _dot_tn(f["merged"].astype(BF16), dxb)
        for k in range(n_k):
            gwout_ref[_slot(k)] += gwout[k * blk:(k + 1) * blk, :]
        dma = dmerged * f["sga"]
        dmb = dmerged * f["sgb"]
        dga = dmerged * f["ma"] * f["sga"] * (1.0 - f["sga"])
        dgb = dmerged * f["mb"] * f["sgb"] * (1.0 - f["sgb"])
        dmab = dma.astype(BF16)
        dmbb = dmb.astype(BF16)
        dya = _dot_nt(dmab, wa_ref[...])
        dyb = _dot_nt(dmbb, wb_ref[...])
        gwa = _dot_tn(f["yab"], dmab)
        gwb = _dot_tn(f["ybb"], dmbb)
        for k in range(n_k):
            gwa_ref[_slot(k)] += gwa[:, k * blk:(k + 1) * blk]
            gwb_ref[_slot(k)] += gwb[:, k * blk:(k + 1) * blk]
        zb, szb = f["zb"], f["szb"]
        dzb = dyb * f["ms"] * (szb * (1.0 + zb * (1.0 - szb)))
        dms = dyb * f["silu_zb"]
        gscale_ref[...] += jnp.sum(dms * f["mixed"], axis=0, keepdims=True)
        dmixed = (dms * f["scale"]).astype(BF16)
        pooled = f["pooled"]
        for g in range(len(POOL_WINDOWS)):
            cols = slice(g * POOL_GROUP, (g + 1) * POOL_GROUP)
            dpooled_ref[:, cols] = _dot_nt(dmixed[:, cols], pw_ref[g].astype(BF16))
            gpw_ref[g] += _dot_tn(pooled[:, cols].astype(BF16), dmixed[:, cols])
        za, sza = f["za"], f["sza"]
        dza = dya * f["y2"] * (sza * (1.0 + za * (1.0 - sza)))
        dy2 = dya * f["silu_za"]
        sq = f["sq"]
        dq = dy2 * f["y1"] * sq * (1.0 - sq)
        dqb = dq.astype(BF16)
        dy1 = dy2 * sq + _dot_nt(dqb, wglu)
        gwglu = _dot_tn(f["y1b"], dqb)
        rblk = WIDTH // n_k
        for k in range(n_k):
            gwglu_ref[_slot(k)] += gwglu[k * rblk:(k + 1) * rblk, :]
        gbglu_ref[...] += jnp.sum(dq, axis=0, keepdims=True)
        y0, t = f["y0"], f["t"]
        dgelu = 0.5 * (1.0 + t) + 0.5 * y0 * (1.0 - t * t) * (GELU_C * (1.0 + 3.0 * GELU_A * y0 * y0))
        dy0_ref[...] = dy1 * dgelu
        zeros = jnp.zeros((TILE_M, WIDTH), BF16)
        dproj_ref[:, 0:WIDTH] = zeros
        dproj_ref[:, WIDTH:2 * WIDTH] = dza.astype(BF16)
        dproj_ref[:, 2 * WIDTH:3 * WIDTH] = zeros
        dproj_ref[:, 3 * WIDTH:4 * WIDTH] = dzb.astype(BF16)
        dproj_ref[:, 4 * WIDTH:4 * WIDTH + D_MODEL] = dga.astype(BF16)
        dproj_ref[:, 4 * WIDTH + D_MODEL:] = dgb.astype(BF16)

    tile = lambda w: pl.BlockSpec((TILE_M, w), lambda i: (i, 0))
    shapes = _big_shapes()
    big = ["w_out", "w_branch_a", "w_branch_b", "ssm_w_glu"]
    slab = lambda n: pl.BlockSpec((None,) + shapes[n][1:], lambda i: (layer, 0, 0, 0))
    args = [dx_next, proj, y0, pooled, wg_glu, b_glu, pool_w, pool_scale, wg_a, wg_b, wg_out]
    return _pcall(
        body, name=f"mix_bwd_l{layer}",
        out_shape=(SDS((SEQ, N_IN), BF16), SDS((SEQ, WIDTH), F32), SDS((SEQ, WIDTH), F32))
        + tuple(SDS(shapes[n], F32) for n in big)
        + (SDS((4, POOL_GROUP, POOL_GROUP), F32), SDS((1, WIDTH), F32), SDS((1, WIDTH), F32)),
        grid=(SEQ // TILE_M,),
        in_specs=[tile(D_MODEL), tile(N_IN), tile(WIDTH), tile(WIDTH)] + _mix_weight_specs(layer) + [ANY] * n_prev,
        out_specs=(tile(N_IN), tile(WIDTH), tile(WIDTH)) + tuple(slab(n) for n in big)
        + (_const((4, POOL_GROUP, POOL_GROUP)), _const((1, WIDTH)), _const((1, WIDTH))),
        args=args + list(prev or ()),
        aliases={len(args) + i: 3 + i for i in range(n_prev)},
        sem=("arbitrary",), limit=VMEM_LIMIT_BIG, carry=carry)


def _pool_bwd(layer, dpooled, dproj):
    def body(dp_ref, _, o_ref):
        for gi, win in enumerate(POOL_WINDOWS):
            cols = slice(gi * POOL_GROUP, (gi + 1) * POOL_GROUP)
            dp = dp_ref[:, cols]
            t, count = _pool_counts(win)
            e = dp / count
            acc = e
            k = 1
            while k < win:
                acc = acc + jnp.where(t < SEQ - k, pltpu.roll(acc, SEQ - k, 0), 0.0)
                k *= 2
            o_ref[:, cols] = (acc - dp).astype(BF16)

    return pl.pallas_call(
        body, name=f"pool_bwd_l{layer}",
        out_shape=SDS((SEQ, N_IN), BF16),
        grid=(1,),
        in_specs=[pl.BlockSpec((SEQ, WIDTH), lambda i: (0, 0)), ANY],
        out_specs=pl.BlockSpec((SEQ, WIDTH), lambda i: (0, 2)),
        input_output_aliases={1: 0},
        compiler_params=_cp(("arbitrary",)),
    )(dpooled, dproj)


def _proj_wgrad(layer, x, norm_g, dproj, prev, carry=None):
    tm = 512
    n_prev = 0 if prev is None else 1

    def body(*refs):
        x_ref, g_ref, dp_ref = refs[:3]
        gw_ref, gb_ref, ht_ref = refs[3 + n_prev:]
        n, t = pl.program_id(0), pl.program_id(1)

        @pl.when(t == 0)
        def _():
            gw_ref[...] = jnp.zeros_like(gw_ref)
            gb_ref[...] = jnp.zeros_like(gb_ref)

        @pl.when(n == 0)
        def _():
            _, xn = _rms(x_ref[...])
            ht_ref[t] = (xn * g_ref[layer:layer + 1, :]).T.astype(BF16)

        dp = dp_ref[...]
        gw_ref[...] += _dot(ht_ref[t], dp)
        gb_ref[...] += jnp.sum(dp.astype(F32), axis=0, keepdims=True)

    return _pcall(
        body, name=f"proj_wgrad_l{layer}",
        out_shape=(SDS(_big_shapes()["w_in"], F32), SDS((1, N_IN), F32)),
        grid=(N_DEV, SEQ // tm),
        in_specs=[pl.BlockSpec((tm, D_MODEL), lambda n, t: (jnp.where(n == 0, t, 0), 0)),
                  _const((DEPTH, D_MODEL)),
                  pl.BlockSpec((tm, WIDTH), lambda n, t: (t, n))] + [ANY] * n_prev,
        out_specs=(pl.BlockSpec((None, None, D_MODEL, WIDTH), lambda n, t: (layer, _slot(n), 0, 0)),
                   pl.BlockSpec((1, WIDTH), lambda n, t: (0, n))),
        scratch_shapes=[pltpu.VMEM((SEQ // tm, D_MODEL, tm), BF16)],
        args=[x, norm_g, dproj] + ([prev] if n_prev else []),
        aliases={3: 0} if n_prev else {}, sem=("arbitrary", "arbitrary"), carry=carry)


def _proj_dgrad(layer, dx_next, x, norm_g, dproj, wg_in, carry=None):
    n_w = len(wg_in)

    def body(dxn_ref, x_ref, g_ref, dp_ref, *refs):
        w_refs, (dx_ref, gg_ref) = refs[:n_w], refs[n_w:]

        @pl.when(pl.program_id(0) == 0)
        def _():
            gg_ref[...] = jnp.zeros_like(gg_ref)

        parts = []
        for w_ref in w_refs:
            part = jnp.zeros((TILE_M, w_ref.shape[1]), F32)
            for k in range(N_DEV):
                part = part + _dot_nt(dp_ref[:, k * WIDTH:(k + 1) * WIDTH], w_ref[k])
            parts.append(part)
        dh = parts[0] if n_w == 1 else jnp.concatenate(parts, axis=1)
        rs, xn = _rms(x_ref[...])
        gg_ref[...] += jnp.sum(dh * xn, axis=0, keepdims=True)
        dxn = dh * g_ref[layer:layer + 1, :]
        dx_ref[...] = dxn_ref[...] + rs * (dxn - xn * jnp.mean(dxn * xn, axis=-1, keepdims=True))

    return _pcall(
        body, name=f"proj_dgrad_l{layer}",
        out_shape=(SDS((SEQ, D_MODEL), F32), SDS((1, D_MODEL), F32)),
        grid=(SEQ // TILE_M,),
        in_specs=[pl.BlockSpec((TILE_M, D_MODEL), lambda i: (i, 0)),
                  pl.BlockSpec((TILE_M, D_MODEL), lambda i: (i, 0)),
                  _const((DEPTH, D_MODEL)),
                  pl.BlockSpec((TILE_M, N_IN), lambda i: (i, 0))] + [_const(w.shape) for w in wg_in],
        out_specs=(pl.BlockSpec((TILE_M, D_MODEL), lambda i: (i, 0)), _const((1, D_MODEL))),
        args=[dx_next, x, norm_g, dproj, *wg_in], sem=("arbitrary",), carry=carry)


def _my_place():
    return lax.axis_index("x"), lax.axis_index("y"), lax.axis_index("c")


def _gather_plan(shards, layer, by_columns=(), rows_of=None):
    n = len(shards)

    def parts(ins, outs, sems):
        send_sems, recv_sems, local_sems = sems
        x, y, c = _my_place()
        chips = [(1 - x, y), (x, 1 - y), (1 - x, 1 - y)]

        def source(t):
            return ins[t].at[layer] if rows_of is None else ins[t].at[layer, pl.ds(*rows_of)]

        def rows(t, place):
            px, py, pc = place
            index = 4 * px + 2 * py + pc
            if t in by_columns:
                width = shards[t].shape[2]
                return outs[t].at[:, pl.ds(pl.multiple_of(index * width, LANES), width)]
            return outs[t].at[index]

        def copy(t, k, block, to, from_src=False):
            return pltpu.make_async_remote_copy(
                src_ref=source(t) if from_src else rows(t, block), dst_ref=rows(t, block),
                send_sem=send_sems.at[7 * t + k], recv_sem=recv_sems.at[7 * t + k], device_id=to,
                device_id_type=MESH)

        def mine(t):
            return pltpu.make_async_copy(source(t), rows(t, (x, y, c)), local_sems.at[t])

        return (x, y, c), chips, copy, mine

    def start(ins, outs, sems):
        me, chips, copy, mine = parts(ins, outs, sems)
        x, y, c = me
        for t in range(n):
            mine(t).start()
            copy(t, 0, me, (x, y, 1 - c), from_src=True).start()
            for j, chip in enumerate(chips):
                copy(t, 1 + j, me, (*chip, c), from_src=True).start()

    def relay(ins, outs, sems):
        me, chips, copy, mine = parts(ins, outs, sems)
        x, y, c = me
        for t in range(n):
            for j, chip in enumerate(chips):
                copy(t, 1 + j, (*chip, c), me).wait_recv()
                copy(t, 4 + j, (*chip, c), (x, y, 1 - c)).start()

    def finish(ins, outs, sems):
        me, chips, copy, mine = parts(ins, outs, sems)
        x, y, c = me
        sibling = (x, y, 1 - c)
        for t in range(n):
            copy(t, 0, sibling, me).wait_recv()
            for j, chip in enumerate(chips):
                copy(t, 4 + j, (*chip, 1 - c), me).wait_recv()
            for k in range(7):
                copy(t, k, me, sibling, from_src=k < 4).wait_send()
            mine(t).wait()

    n_rows = lambda a: a.shape[1] if rows_of is None else rows_of[1]
    out_shape = [SDS((a.shape[1], N_DEV * a.shape[2]) if t in by_columns else (N_DEV, n_rows(a), a.shape[2]), a.dtype)
                 for t, a in enumerate(shards)]
    sems = [pltpu.SemaphoreType.DMA((7 * n,)), pltpu.SemaphoreType.DMA((7 * n,)), pltpu.SemaphoreType.DMA((n,))]
    return _Carried(shards, out_shape, sems, start, finish, relay)


class _Carried:
    def __init__(self, ins, out_shape, sems, start, finish, relay=None):
        self.ins, self.out_shape, self.sems = list(ins), list(out_shape), list(sems)
        self.start, self.finish = start, finish
        self.relay = relay or (lambda ins, outs, sems: None)


def _pcall(body, *, name, grid, in_specs, out_specs, out_shape, args, scratch_shapes=(), aliases=None,
           sem=None, limit=VMEM_LIMIT, carry=None):
    out_shape, out_specs, scratch_shapes = list(out_shape), list(out_specs), list(scratch_shapes)
    n_in, n_out, n_scr = len(args), len(out_shape), len(scratch_shapes)
    if carry is None:
        kern, c_ins, c_out, c_sems = body, [], [], []
    else:
        c_ins, c_out, c_sems = carry.ins, carry.out_shape, carry.sems
        ci, co = len(c_ins), len(c_out)
        steps = tuple(grid)

        def kern(*refs):
            o0 = n_in + ci
            s0 = o0 + n_out + co
            mine = refs[:n_in] + refs[o0:o0 + n_out] + refs[s0:s0 + n_scr]
            theirs = (refs[n_in:o0], refs[o0 + n_out:s0], refs[s0 + n_scr:])
            first = pl.program_id(0) == 0
            last = pl.program_id(0) == steps[0] - 1
            for a in range(1, len(steps)):
                first = jnp.logical_and(first, pl.program_id(a) == 0)
                last = jnp.logical_and(last, pl.program_id(a) == steps[a] - 1)

            @pl.when(first)
            def _():
                carry.start(*theirs)

            @pl.when(last)
            def _():
                carry.relay(*theirs)

            body(*mine)

            @pl.when(last)
            def _():
                carry.finish(*theirs)

        sem = ("arbitrary",) * len(steps)
    res = pl.pallas_call(
        kern, name=name, grid=tuple(grid),
        in_specs=list(in_specs) + [ANY] * len(c_ins),
        out_specs=tuple(out_specs + [ANY] * len(c_out)),
        out_shape=tuple(out_shape + c_out),
        scratch_shapes=scratch_shapes + c_sems,
        input_output_aliases=aliases or {},
        compiler_params=_cp(sem, limit),
    )(*args, *c_ins)
    return res[:n_out], res[n_out:]


def _run_carried(name, carry):
    ci, co = len(carry.ins), len(carry.out_shape)

    def body(*refs):
        parts = (refs[:ci], refs[ci:ci + co], refs[ci + co:])
        carry.start(*parts)
        carry.relay(*parts)
        carry.finish(*parts)

    return pl.pallas_call(
        body, name=name, out_shape=tuple(carry.out_shape),
        in_specs=[ANY] * ci, out_specs=tuple([ANY] * co), scratch_shapes=carry.sems,
    )(*carry.ins)


def _sibling_plan(big, small):
    n = len(big)
    n_copies = 4 * n + len(small)

    def copies(ins, outs, sems):
        send_sems, recv_sems = sems
        x, y, c = _my_place()
        pairs = []
        for t, (_, layer) in enumerate(big):
            for s in range(4):
                pairs.append((ins[t].at[layer, pl.ds(4 * (1 - c) + s, 1)], outs[t].at[pl.ds(s, 1)]))
        pairs += list(zip(ins[n:], outs[n:]))
        return [pltpu.make_async_remote_copy(
            src_ref=src, dst_ref=dst, send_sem=send_sems.at[k], recv_sem=recv_sems.at[k],
            device_id=(x, y, 1 - c), device_id_type=MESH) for k, (src, dst) in enumerate(pairs)]

    def start(ins, outs, sems):
        for cp in copies(ins, outs, sems):
            cp.start()

    def finish(ins, outs, sems):
        for cp in copies(ins, outs, sems):
            cp.wait()

    out_shape = [SDS((4,) + a.shape[2:], a.dtype) for a, _ in big] + [SDS(a.shape, a.dtype) for a in small]
    sems = [pltpu.SemaphoreType.DMA((n_copies,)), pltpu.SemaphoreType.DMA((n_copies,))]
    return _Carried([a for a, _ in big] + list(small), out_shape, sems, start, finish)


def _chips_plan(big, small):
    n, n_small = len(big), len(small)
    max_rows = 256
    parts = [max(1, a.shape[1] // max_rows) for a in big]
    n_copies = 3 * (sum(parts) + n_small)

    def copies(ins, outs, sems, landing):
        send_sems, recv_sems, local_sems = sems
        x, y, c = _my_place()
        my_chip = 2 * x + y
        chips = [(1 - x, y), (x, 1 - y), (1 - x, 1 - y)]
        remote, local = [], []
        for chip in chips:
            to = 2 * chip[0] + chip[1]
            slot = to if landing else my_chip
            pairs = []
            for t in range(n):
                rows_per = big[t].shape[1] // parts[t]
                for p in range(parts[t]):
                    rows = pl.ds(p * rows_per, rows_per)
                    pairs.append((ins[t].at[to, rows], outs[t].at[slot, rows]))
            pairs += [(ins[t], outs[t].at[slot]) for t in range(n, n + n_small)]
            for src, dst in pairs:
                k = len(remote)
                remote.append(pltpu.make_async_remote_copy(
                    src_ref=src, dst_ref=dst, send_sem=send_sems.at[k], recv_sem=recv_sems.at[k],
                    device_id=(*chip, c), device_id_type=MESH))
        for t in range(n):
            local.append(pltpu.make_async_copy(ins[t].at[my_chip], outs[t].at[my_chip], local_sems.at[t]))
        for t in range(n, n + n_small):
            local.append(pltpu.make_async_copy(ins[t], outs[t].at[my_chip], local_sems.at[t]))
        return remote + local

    def start(ins, outs, sems):
        for cp in copies(ins, outs, sems, landing=False):
            cp.start()

    def finish(ins, outs, sems):
        for cp in copies(ins, outs, sems, landing=True):
            cp.wait()

    out_shape = [SDS(a.shape, a.dtype) for a in big] + [SDS((N_CHIP,) + a.shape, a.dtype) for a in small]
    sems = [pltpu.SemaphoreType.DMA((n_copies,)), pltpu.SemaphoreType.DMA((n_copies,)),
            pltpu.SemaphoreType.DMA((n + n_small,))]
    return _Carried(list(big) + list(small), out_shape, sems, start, finish)


def _all_plan(small):
    n = len(small)
    masks = [(m >> 2 & 1, m >> 1 & 1, m & 1) for m in range(1, N_DEV)]

    def copies(ins, outs, sems, landing):
        send_sems, recv_sems, local_sems = sems
        x, y, c = _my_place()
        me = 4 * x + 2 * y + c
        flip = lambda v, bit: 1 - v if bit else v
        remote = []
        for fx, fy, fc in masks:
            peer = (flip(x, fx), flip(y, fy), flip(c, fc))
            slot = 4 * peer[0] + 2 * peer[1] + peer[2] if landing else me
            for t in range(n):
                k = len(remote)
                remote.append(pltpu.make_async_remote_copy(
                    src_ref=ins[t], dst_ref=outs[t].at[slot], send_sem=send_sems.at[k], recv_sem=recv_sems.at[k],
                    device_id=peer, device_id_type=MESH))
        local = [pltpu.make_async_copy(ins[t], outs[t].at[me], local_sems.at[t]) for t in range(n)]
        return remote + local

    def start(ins, outs, sems):
        for cp in copies(ins, outs, sems, landing=False):
            cp.start()

    def finish(ins, outs, sems):
        for cp in copies(ins, outs, sems, landing=True):
            cp.wait()

    out_shape = [SDS((N_DEV,) + a.shape, a.dtype) for a in small]
    sems = [pltpu.SemaphoreType.DMA((7 * n,)), pltpu.SemaphoreType.DMA((7 * n,)), pltpu.SemaphoreType.DMA((n,))]
    return _Carried(list(small), out_shape, sems, start, finish)


def _join(*plans):
    plans = [p for p in plans if p is not None]
    if len(plans) <= 1:
        return plans[0] if plans else None

    def each(fn_name, ins, outs, sems):
        i = o = s = 0
        for p in plans:
            ni, no, ns = len(p.ins), len(p.out_shape), len(p.sems)
            getattr(p, fn_name)(ins[i:i + ni], outs[o:o + no], sems[s:s + ns])
            i, o, s = i + ni, o + no, s + ns

    return _Carried(sum((p.ins for p in plans), []), sum((p.out_shape for p in plans), []),
                    sum((p.sems for p in plans), []),
                    lambda i, o, s: each("start", i, o, s), lambda i, o, s: each("finish", i, o, s),
                    lambda i, o, s: each("relay", i, o, s))


def _row_block(rows, most=256):
    return min(rows, most)


def _add_own(tag, core, gs, layer, gots):
    n = len(gs)

    def body(core_ref, *refs):
        for a_ref, b_ref, o_ref in zip(refs[:n], refs[n:2 * n], refs[2 * n:]):
            o_ref[...] = (a_ref[...] + b_ref[...]).astype(o_ref.dtype)

    mine = lambda a: pl.BlockSpec((None, None) + a.shape[1:], lambda s, core: (layer, 4 * core[0] + s, 0, 0))
    theirs = lambda a: pl.BlockSpec((None,) + a.shape[1:], lambda s, core: (s, 0, 0))
    return pl.pallas_call(
        body, name=f"add_{tag}", out_shape=tuple(SDS(a.shape, BF16) for a in gots),
        grid_spec=pltpu.PrefetchScalarGridSpec(
            num_scalar_prefetch=1, grid=(4,),
            in_specs=[mine(a) for a in gots] + [theirs(a) for a in gots],
            out_specs=tuple(theirs(a) for a in gots)),
        compiler_params=_cp(("parallel",)),
    )(core, *gs, *gots)


def _add_lists(tag, own, got, grid=None, specs=None, dtype=F32):
    n = len(own)

    def body(*refs):
        for a, b, o in zip(refs[:n], refs[n:2 * n], refs[2 * n:]):
            o[...] = (a[...] + b[...]).astype(o.dtype)

    kw = {}
    if grid is not None:
        kw = dict(grid=grid, in_specs=list(specs) * 2, out_specs=tuple(specs),
                  compiler_params=_cp(("parallel",) * len(grid)))
    return pl.pallas_call(
        body, name=f"add_{tag}", out_shape=tuple(SDS(a.shape, dtype) for a in own), **kw)(*own, *got)


def _adamw_math(w, g, m, v):
    m = ADAM_B1 * m + (1.0 - ADAM_B1) * g
    v = ADAM_B2 * v + (1.0 - ADAM_B2) * (g * g)
    m_hat = m / (1.0 - ADAM_B1 ** ADAM_STEP)
    v_hat = v / (1.0 - ADAM_B2 ** ADAM_STEP)
    delta = -ADAM_LR * (m_hat / (jnp.sqrt(v_hat) + ADAM_EPS) + ADAM_WD * w)
    return delta, m, v


def _sum_slots_adamw(tag, slots, w, m, v):
    _, r, c = slots[0].shape
    rb = _row_block(r, most=512)

    def body(s0_ref, s1_ref, w_ref, m_ref, v_ref, g_ref, d_ref, nm_ref, nv_ref):
        first = pl.program_id(1) == 0
        g = _pair_sum([jnp.where(first, s0_ref[k], s1_ref[k]).astype(F32) for k in range(N_CHIP)])
        delta, nm, nv = _adamw_math(w_ref[...], g, m_ref[...], v_ref[...])
        g_ref[...] = g
        d_ref[...] = delta
        nm_ref[...] = nm
        nv_ref[...] = nv

    spec = pl.BlockSpec((None, rb, c), lambda j, l: (l, j, 0))
    sspec = pl.BlockSpec((N_CHIP, rb, c), lambda j, l: (0, j, 0))
    s = SDS((DEPTH, r, c), F32)
    return pl.pallas_call(
        body, name=f"adamw_{tag}", out_shape=(s, s, s, s),
        grid=(r // rb, DEPTH), in_specs=[sspec, sspec, spec, spec, spec], out_specs=(spec, spec, spec, spec),
        compiler_params=_cp(("parallel", "arbitrary")),
    )(*slots, w, m, v)


def _adamw_small(tag, entries, grid=None, sums=()):
    flat_in, in_specs, out_shape, out_specs, layout = [], [], [], [], []
    for slots, w, m, v, slot_spec, w_spec in entries:
        per_layer = isinstance(slots, (list, tuple))
        n_slot = len(slots) if per_layer else 1
        flat_in += (list(slots) if per_layer else [slots]) + [w, m, v]
        in_specs += [slot_spec] * n_slot + [w_spec] * 3
        out_shape += [SDS(w.shape, F32)] * 4
        out_specs += [w_spec] * 4
        layout.append((per_layer, n_slot))
    n_entry_in = len(flat_in)
    flat_in += list(sums)
    out_shape += [SDS(s.shape[1:], F32) for s in sums]
    n_in = len(flat_in)

    def body(*refs):
        for s_ref, o_ref in zip(refs[n_entry_in:n_in], refs[len(refs) - len(sums):]):
            o_ref[...] = _sum_slots(s_ref)
        i, o = 0, n_in
        for per_layer, n_slot in layout:
            s_refs = refs[i:i + n_slot]
            w_ref, m_ref, v_ref = refs[i + n_slot:i + n_slot + 3]
            outs = refs[o:o + 4]
            if per_layer:
                for l, s_ref in enumerate(s_refs):
                    at = (slice(l, l + 1),) if len(w_ref.shape) == 2 else (l,)
                    g = _sum_slots(s_ref)
                    res = (g,) + _adamw_math(w_ref[at], g, m_ref[at], v_ref[at])
                    for o_ref, val in zip(outs, res):
                        o_ref[at] = val
            else:
                g = _sum_slots(s_refs[0])
                res = (g,) + _adamw_math(w_ref[...], g, m_ref[...], v_ref[...])
                for o_ref, val in zip(outs, res):
                    o_ref[...] = val
            i += n_slot + 3
            o += 4

    kw = {}
    if grid is not None:
        kw = dict(grid=grid, in_specs=in_specs, out_specs=tuple(out_specs),
                  compiler_params=_cp(("parallel",) * len(grid)))
    res = pl.pallas_call(body, name=f"adamw_{tag}", out_shape=tuple(out_shape), **kw)(*flat_in)
    return [tuple(res[4 * e:4 * e + 4]) for e in range(len(entries))], res[4 * len(entries):]


def kernel(x, norm_g, w_in, b_in, ssm_log_dt, ssm_lam_re, ssm_lam_im, ssm_b_re, ssm_b_im, ssm_c_re, ssm_c_im, ssm_d, ssm_w_glu, ssm_b_glu, pool_w, pool_scale, w_branch_a, w_branch_b, w_out, final_norm_g, loss_target, m_norm_g, m_w_in, m_b_in, m_ssm_log_dt, m_ssm_lam_re, m_ssm_lam_im, m_ssm_b_re, m_ssm_b_im, m_ssm_c_re, m_ssm_c_im, m_ssm_d, m_ssm_w_glu, m_ssm_b_glu, m_pool_w, m_pool_scale, m_w_branch_a, m_w_branch_b, m_w_out, m_final_norm_g, v_norm_g, v_w_in, v_b_in, v_ssm_log_dt, v_ssm_lam_re, v_ssm_lam_im, v_ssm_b_re, v_ssm_b_im, v_ssm_c_re, v_ssm_c_im, v_ssm_d, v_ssm_w_glu, v_ssm_b_glu, v_pool_w, v_pool_scale, v_w_branch_a, v_w_branch_b, v_w_out, v_final_norm_g):
    weights = dict(norm_g=norm_g, w_in=w_in, b_in=b_in, ssm_log_dt=ssm_log_dt, ssm_lam_re=ssm_lam_re,
                   ssm_lam_im=ssm_lam_im, ssm_b_re=ssm_b_re, ssm_b_im=ssm_b_im, ssm_c_re=ssm_c_re,
                   ssm_c_im=ssm_c_im, ssm_d=ssm_d, ssm_w_glu=ssm_w_glu, ssm_b_glu=ssm_b_glu, pool_w=pool_w,
                   pool_scale=pool_scale, w_branch_a=w_branch_a, w_branch_b=w_branch_b, w_out=w_out,
                   final_norm_g=final_norm_g.reshape(1, D_MODEL))
    mom_m = dict(norm_g=m_norm_g, w_in=m_w_in, b_in=m_b_in, ssm_log_dt=m_ssm_log_dt, ssm_lam_re=m_ssm_lam_re,
                 ssm_lam_im=m_ssm_lam_im, ssm_b_re=m_ssm_b_re, ssm_b_im=m_ssm_b_im, ssm_c_re=m_ssm_c_re,
                 ssm_c_im=m_ssm_c_im, ssm_d=m_ssm_d, ssm_w_glu=m_ssm_w_glu, ssm_b_glu=m_ssm_b_glu,
                 pool_w=m_pool_w, pool_scale=m_pool_scale, w_branch_a=m_w_branch_a, w_branch_b=m_w_branch_b,
                 w_out=m_w_out, final_norm_g=m_final_norm_g.reshape(1, D_MODEL))
    mom_v = dict(norm_g=v_norm_g, w_in=v_w_in, b_in=v_b_in, ssm_log_dt=v_ssm_log_dt, ssm_lam_re=v_ssm_lam_re,
                 ssm_lam_im=v_ssm_lam_im, ssm_b_re=v_ssm_b_re, ssm_b_im=v_ssm_b_im, ssm_c_re=v_ssm_c_re,
                 ssm_c_im=v_ssm_c_im, ssm_d=v_ssm_d, ssm_w_glu=v_ssm_w_glu, ssm_b_glu=v_ssm_b_glu,
                 pool_w=v_pool_w, pool_scale=v_pool_scale, w_branch_a=v_w_branch_a, w_branch_b=v_w_branch_b,
                 w_out=v_w_out, final_norm_g=v_final_norm_g.reshape(1, D_MODEL))
    order = ["norm_g", "w_in", "b_in", "ssm_log_dt", "ssm_lam_re", "ssm_lam_im", "ssm_b_re", "ssm_b_im",
             "ssm_c_re", "ssm_c_im", "ssm_d", "ssm_w_glu", "ssm_b_glu", "pool_w", "pool_scale", "w_branch_a",
             "w_branch_b", "w_out", "final_norm_g"]
    big_names = ["w_in", "ssm_w_glu", "w_branch_a", "w_branch_b", "w_out"]

    log_dt3 = ssm_log_dt.reshape(DEPTH, N_GROUP, 1)
    b_t = lambda a: a.transpose(0, 1, 3, 2)
    for d in (weights, mom_m, mom_v):
        d["ssm_b_re"], d["ssm_b_im"] = b_t(d["ssm_b_re"]), b_t(d["ssm_b_im"])
    bt_re, bt_im = weights["ssm_b_re"], weights["ssm_b_im"]
    abar_re, abar_im, bbt_re, bbt_im = _s5_params(log_dt3, ssm_lam_re, ssm_lam_im, bt_re, bt_im)
    s5_args = (bbt_re, bbt_im, ssm_c_re, ssm_c_im, abar_re, abar_im, ssm_d)

    w16 = {n: weights[n].astype(BF16) for n in big_names}
    rest = [w16[n] for n in big_names[1:]]
    half = D_MODEL // 2
    wg_in = [None, [None, None]]
    wg_rest = [None, None]
    wg_in[0] = list(_run_carried("gather_w_in_l0", _gather_plan([w16["w_in"]], 0)))
    xs = [x.reshape(SEQ, D_MODEL)]
    saved = []
    for l in range(DEPTH):
        proj, moved = _norm_proj(l, xs[l], norm_g, wg_in[l], b_in,
                                 carry=_gather_plan([w16["w_in"]], 1, rows_of=(0, half)) if l == 0 else None)
        if l == 0:
            (wg_in[1][0],) = moved
        (states, y0), wg_rest[l] = _s5_scan_fwd(l, proj, *s5_args, carry=_gather_plan(rest, l, by_columns=(1, 2)))
        pooled = _pool_fwd(l, proj)
        wg_glu, wg_a, wg_b, wg_out = wg_rest[l]
        last = l == DEPTH - 1
        res, moved = _mix_fwd(
            l, xs[l], proj, y0, pooled, wg_glu, ssm_b_glu, pool_w, pool_scale, wg_a, wg_b, wg_out,
            carry=_gather_plan([w16["w_in"]], 1, rows_of=(half, half)) if l == 0 else None,
            head=(loss_target.reshape(SEQ, D_MODEL), weights["final_norm_g"]) if last else None)
        if l == 0:
            (wg_in[1][1],) = moved
        if last:
            dx, loss_part, g_final = res
        else:
            xs.append(res[0])
        saved.append((proj, states, y0, pooled))

    core = lax.axis_index("c").astype(jnp.int32).reshape(1)
    vec_names = ["norm_g", "b_in", "ssm_d", "ssm_b_glu", "pool_scale", "ssm_log_dt"]
    s5_names = ["ssm_log_dt", "ssm_lam_re", "ssm_lam_im", "ssm_b_re", "ssm_b_im"]
    mat_names = ["pool_w", "ssm_c_re", "ssm_c_im", "ssm_b_re", "ssm_b_im"]
    lane_sparse = ("ssm_c_re", "ssm_c_im", "ssm_b_re", "ssm_b_im")

    def dense(key, a):
        return a.reshape(-1, LANES) if key[0] in lane_sparse else a

    def undense(key, slots):
        return slots.reshape((N_CHIP, N_GROUP, GROUP_W, STATE)) if key[0] in lane_sparse else slots

    def add_small(tag, keys, own, got):
        out = [None] * len(keys)
        whole = [i for i, k in enumerate(keys) if k[0] not in mat_names]
        tiled = [i for i, k in enumerate(keys) if k[0] in mat_names]
        if whole:
            for i, r in zip(whole, _add_lists(f"{tag}_a", [own[i] for i in whole], [got[i] for i in whole])):
                out[i] = r
        if tiled:
            specs = [pl.BlockSpec((1, POOL_GROUP, POOL_GROUP), lambda j: (j, 0, 0)) if keys[i][0] == "pool_w"
                     else pl.BlockSpec((own[i].shape[0] // N_CHUNK, LANES), lambda j: (j, 0)) for i in tiled]
            for i, r in zip(tiled, _add_lists(f"{tag}_b", [own[i] for i in tiled], [got[i] for i in tiled],
                                              grid=(N_CHUNK,), specs=specs, dtype=BF16)):
                out[i] = r
        return out

    sm = {("final_norm_g", None): g_final, ("loss", None): loss_part}
    slots = {}
    grads = dict.fromkeys(big_names)

    class Wave:
        def __init__(self, tag, layer, big, keys):
            self.tag, self.layer, self.big, self.keys = tag, layer, big, keys

        def to_sibling(self):
            self.own = [dense(k, sm[k]) for k in self.keys]
            return _sibling_plan([(grads[n], self.layer) for n in self.big], self.own)

        def add(self, moved):
            nb = len(self.big)
            self.chip_big = list(_add_own(self.tag, core, [grads[n] for n in self.big], self.layer, moved[:nb])
                                 ) if nb else []
            self.chip_small = add_small(self.tag, self.keys, self.own, moved[nb:])

        def to_chips(self, big=None, small=True):
            self.sent = list(self.big if big is None else big), small
            return _chips_plan([self.chip_big[self.big.index(n)] for n in self.sent[0]],
                               self.chip_small if small else [])

        def landed(self, moved):
            names, small = self.sent
            for n, s in zip(names, moved[:len(names)]):
                slots[(n, self.layer)] = s
            if small:
                for k, s in zip(self.keys, moved[len(names):]):
                    slots[k] = undense(k, s)
            return moved[len(names) + (len(self.keys) if small else 0):]

    def s5_param_grads(l, g_abar_re, g_abar_im, g_bbt_re, g_bbt_im):
        g = _s5_params_bwd(l, log_dt3, ssm_lam_re, ssm_lam_im, bt_re, bt_im, g_abar_re, g_abar_im, g_bbt_re, g_bbt_im)
        sm[("ssm_log_dt", l)] = g[0].reshape(1, N_GROUP)
        for n, a in zip(s5_names[1:], g[1:]):
            sm[(n, l)] = a

    small1 = ["b_in", "ssm_d", "ssm_b_glu", "pool_scale", "pool_w", "ssm_c_re", "ssm_c_im"] + s5_names
    w1 = Wave("chip1", 1, list(big_names), [(n, 1) for n in small1] + [("final_norm_g", None), ("loss", None)])
    early = Wave("chip0e", 0, big_names[1:], [("pool_w", 0), ("pool_scale", 0), ("ssm_b_glu", 0)])
    mid = Wave("chip0m", 0, [], [(n, 0) for n in ["ssm_c_re", "ssm_c_im", "ssm_d"] + s5_names] + [("norm_g", 1)])
    late = Wave("chip0l", 0, ["w_in"], [("b_in", 0)])

    mix_prev, gw_in = None, None
    for l in reversed(range(DEPTH)):
        proj, states, y0, pooled = saved[l]
        wg_glu, wg_a, wg_b, wg_out = wg_rest[l]
        res, moved = _mix_bwd(l, dx, proj, y0, pooled, wg_glu, ssm_b_glu, pool_w, pool_scale, wg_a, wg_b, wg_out,
                              mix_prev, carry=None if l == 1 else w1.to_chips(big=["w_in"], small=False))
        if l == 0:
            w1.landed(moved)
        dproj, dy0, dpooled = res[:3]
        mix_prev = list(res[3:7])
        grads["w_out"], grads["w_branch_a"], grads["w_branch_b"], grads["ssm_w_glu"] = mix_prev
        sm[("pool_w", l)], sm[("pool_scale", l)], sm[("ssm_b_glu", l)] = res[7:]
        dproj = _pool_bwd(l, dpooled, dproj)
        carry = None if l == 1 else _join(w1.to_chips(big=big_names[1:]), early.to_sibling())
        res, moved = _s5_scan_bwd(l, dy0, proj, states, *s5_args, dproj, carry=carry)
        if l == 0:
            early.add(w1.landed(moved))
        dproj, g_bbt_re, g_bbt_im, sm[("ssm_c_re", l)], sm[("ssm_c_im", l)], g_abar_re, g_abar_im, sm[("ssm_d", l)] = res
        s5_param_grads(l, g_abar_re, g_abar_im, g_bbt_re, g_bbt_im)
        carry = None if l == 1 else _join(early.to_chips(), mid.to_sibling())
        (gw_in, sm[("b_in", l)]), moved = _proj_wgrad(l, xs[l], norm_g, dproj, gw_in, carry=carry)
        grads["w_in"] = gw_in
        if l == 0:
            mid.add(early.landed(moved))
        carry = w1.to_sibling() if l == 1 else _join(mid.to_chips(), late.to_sibling())
        (dx, sm[("norm_g", l)]), moved = _proj_dgrad(l, dx, xs[l], norm_g, dproj, wg_in[l], carry=carry)
        if l == 1:
            w1.add(moved)
        else:
            late.add(mid.landed(moved))
    grad_x = dx.reshape(1, SEQ, D_MODEL)
    moved = late.landed(_run_carried("exchange_last", _join(late.to_chips(), _all_plan([sm[("norm_g", 0)]]))))
    slots[("norm_g", 0)] = moved[0]

    res = {}
    for n in big_names:
        res[n] = _sum_slots_adamw(n, [slots[(n, l)] for l in range(DEPTH)], weights[n], mom_m[n], mom_v[n])
    per_layer = lambda n: [slots[(n, l)] for l in range(DEPTH)]
    names_a = vec_names + ["ssm_lam_re", "ssm_lam_im"]
    entries_a = [(per_layer(n), weights[n], mom_m[n], mom_v[n], None, None) for n in names_a]
    n = "final_norm_g"
    entries_a.append((slots[(n, None)], weights[n], mom_m[n], mom_v[n], None, None))
    out_a, (loss,) = _adamw_small("small_a", entries_a, sums=[slots[("loss", None)]])
    loss = loss.reshape(())
    for n, r in zip(names_a + ["final_norm_g"], out_a):
        res[n] = r
    res["final_norm_g"] = tuple(a.reshape(D_MODEL) for a in res["final_norm_g"])
    pw_s = pl.BlockSpec((N_CHIP, 1, POOL_GROUP, POOL_GROUP), lambda j: (0, j, 0, 0))
    pw_w = pl.BlockSpec((DEPTH, 1, POOL_GROUP, POOL_GROUP), lambda j: (0, j, 0, 0))
    c_s = pl.BlockSpec((N_CHIP, CH_G, GROUP_W, STATE), lambda j: (0, j, 0, 0))
    c_w = pl.BlockSpec((DEPTH, CH_G, GROUP_W, STATE), lambda j: (0, j, 0, 0))
    entries_b = [(per_layer(n), weights[n], mom_m[n], mom_v[n], pw_s if n == "pool_w" else c_s,
                  pw_w if n == "pool_w" else c_w) for n in mat_names]
    out_b, _ = _adamw_small("small_b", entries_b, grid=(N_CHUNK,))
    for n, r in zip(mat_names, out_b):
        res[n] = tuple(b_t(a) for a in r) if n in ("ssm_b_re", "ssm_b_im") else r

    outs = [loss, grad_x]
    for i in range(4):
        outs += [res[n][i] for n in order]
    return tuple(outs)
```

```python
import math

import jax
import jax.numpy as jnp
from jax import lax
from jax.experimental import pallas as pl
from jax.experimental.pallas import tpu as pltpu

F32 = jnp.float32
BF16 = jnp.bfloat16

SEQ = 2048
D_MODEL = 1024
N_IN = 4096
WIDTH = 512
N_GROUP = 32
GROUP_W = 16
STATE = 64
N_STATE = N_GROUP * STATE
N_CHUNK = 4
CH_G = N_GROUP // N_CHUNK
CH_W = WIDTH // N_CHUNK
CH_S = N_STATE // N_CHUNK
N_DEV = 8
N_CHIP = 4
POOL_WINDOWS = (2, 4, 8, 16)
POOL_GROUP = 128
EPS = 1e-6
DEPTH = 2

ADAM_LR = 0.001
ADAM_B1 = 0.9
ADAM_B2 = 0.999
ADAM_EPS = 1e-08
ADAM_WD = 0.01
ADAM_STEP = 10

LANES = 128
SUBLANES = 8
TILE_M = 256
VMEM_LIMIT = 48 * 1024 * 1024
VMEM_LIMIT_BIG = 60 * 1024 * 1024
MESH = pl.DeviceIdType.MESH
ANY = pl.BlockSpec(memory_space=pl.ANY)

GELU_C = math.sqrt(2.0 / math.pi)
GELU_A = 0.044715

SDS = jax.ShapeDtypeStruct


def _cp(sem=None, limit=VMEM_LIMIT):
    return pltpu.CompilerParams(dimension_semantics=sem, vmem_limit_bytes=limit)


def _dot(a, b):
    return jnp.dot(a, b, preferred_element_type=F32)


def _dot_nt(a, b):
    return lax.dot_general(a, b, (((1,), (1,)), ((), ())), preferred_element_type=F32)


def _dot_tn(a, b):
    return lax.dot_general(a, b, (((0,), (0,)), ((), ())), preferred_element_type=F32)


def _sig(x):
    return jax.nn.sigmoid(x)


def _rms(x):
    rs = lax.rsqrt(jnp.mean(x * x, axis=-1, keepdims=True) + EPS)
    return rs, x * rs


def _slot(n):
    return 4 * (n % 2) + n // 2


def _const(shape):
    n = len(shape)
    return pl.BlockSpec(shape, lambda *_: (0,) * n)


def _pair_sum(vals):
    while len(vals) > 1:
        vals = [vals[i] + vals[i + 1] for i in range(0, len(vals), 2)]
    return vals[0]


def _sum_slots(s_ref):
    return _pair_sum([s_ref[k].astype(F32) for k in range(s_ref.shape[0])])


def _s5_param_fn(log_dt, lam_re, lam_im, bt_re, bt_im):
    dt = jnp.exp(log_dt)
    mag = jnp.exp(lam_re * dt)
    ang = lam_im * dt
    abar_re = mag * jnp.cos(ang)
    abar_im = mag * jnp.sin(ang)
    num_re = abar_re - 1.0
    num_im = abar_im
    den = lam_re * lam_re + lam_im * lam_im
    coef_re = (num_re * lam_re + num_im * lam_im) / den
    coef_im = (num_im * lam_re - num_re * lam_im) / den
    bbar_re = coef_re[..., None, :] * bt_re - coef_im[..., None, :] * bt_im
    bbar_im = coef_re[..., None, :] * bt_im + coef_im[..., None, :] * bt_re
    return abar_re, abar_im, bbar_re, bbar_im


def _s5_params(log_dt, lam_re, lam_im, bt_re, bt_im):
    def body(ld, lr, li, br, bi, o_ar, o_ai, o_br, o_bi):
        ar, ai, bbr, bbi = _s5_param_fn(ld[...], lr[...], li[...], br[...], bi[...])
        o_ar[...] = ar
        o_ai[...] = ai
        o_br[...] = bbr
        o_bi[...] = bbi

    return pl.pallas_call(
        body, name="s5_params",
        out_shape=(SDS(lam_re.shape, F32), SDS(lam_re.shape, F32), SDS(bt_re.shape, F32), SDS(bt_re.shape, F32)),
    )(log_dt, lam_re, lam_im, bt_re, bt_im)


def _s5_params_bwd(layer, log_dt, lam_re, lam_im, bt_re, bt_im, g_ar, g_ai, g_br, g_bi):
    def body(ld, lr, li, br, bi, car, cai, cbr, cbi, o_ld, o_lr, o_li, o_br, o_bi):
        _, vjp = jax.vjp(_s5_param_fn, ld[...], lr[...], li[...], br[...], bi[...])
        d_ld, d_lr, d_li, d_br, d_bi = vjp((car[...], cai[...], cbr[...], cbi[...]))
        o_ld[...] = d_ld
        o_lr[...] = d_lr
        o_li[...] = d_li
        o_br[...] = d_br
        o_bi[...] = d_bi

    one = lambda shape: pl.BlockSpec((None,) + shape, lambda i: (layer,) + (0,) * len(shape))
    whole = lambda shape: _const(shape)
    vec, lam, mat = (N_GROUP, 1), (N_GROUP, STATE), (N_GROUP, GROUP_W, STATE)
    return pl.pallas_call(
        body, name=f"s5_params_bwd_l{layer}", grid=(1,),
        in_specs=[one(vec), one(lam), one(lam), one(mat), one(mat), whole(lam), whole(lam), whole(mat), whole(mat)],
        out_specs=(whole(vec), whole(lam), whole(lam), whole(mat), whole(mat)),
        out_shape=(SDS(vec, F32), SDS(lam, F32), SDS(lam, F32), SDS(mat, F32), SDS(mat, F32)),
    )(log_dt, lam_re, lam_im, bt_re, bt_im, g_ar, g_ai, g_br, g_bi)


def _norm_proj(layer, x, norm_g, wg_in, b_in, carry=None):
    n_w = len(wg_in)

    def body(x_ref, g_ref, b_ref, *refs):
        w_refs, o_ref = refs[:n_w], refs[n_w]
        _, xn = _rms(x_ref[...])
        h = (xn * g_ref[layer:layer + 1, :]).astype(BF16)
        for k in range(N_DEV):
            cols = slice(k * WIDTH, (k + 1) * WIDTH)
            acc = b_ref[layer:layer + 1, cols]
            row = 0
            for w_ref in w_refs:
                rows = w_ref.shape[1]
                acc = acc + _dot(h[:, row:row + rows], w_ref[k])
                row += rows
            o_ref[:, cols] = acc

    (proj,), moved = _pcall(
        body, name=f"norm_proj_l{layer}",
        out_shape=[SDS((SEQ, N_IN), F32)],
        grid=(SEQ // TILE_M,),
        in_specs=[pl.BlockSpec((TILE_M, D_MODEL), lambda i: (i, 0)),
                  _const((DEPTH, D_MODEL)),
                  _const((DEPTH, N_IN))] + [_const(w.shape) for w in wg_in],
        out_specs=[pl.BlockSpec((TILE_M, N_IN), lambda i: (i, 0))],
        args=[x, norm_g, b_in, *wg_in], sem=("parallel",), carry=carry)
    return proj, moved


TIME_BLK = 512
N_TBLK = SEQ // TIME_BLK
N_PANEL = CH_S // LANES
STATE_SHAPE = (N_PANEL, SEQ * SUBLANES, LANES)


def _s5_layer_specs(layer):
    mat = lambda: pl.BlockSpec((None, N_GROUP, GROUP_W, STATE), lambda i: (layer, 0, 0, 0))
    ab = lambda: pl.BlockSpec((None, N_GROUP, STATE), lambda i: (layer, 0, 0))
    return [mat(), mat(), mat(), mat(), ab(), ab(), _const((DEPTH, WIDTH))]


def _s5_layer_scratch():
    return [pltpu.VMEM((N_CHUNK, CH_W, CH_S), BF16)] * 4 + [pltpu.VMEM((8, CH_S), F32)] * 2


def _s5_layer_fill(btre_ref, btim_ref, cre_ref, cim_ref, are_ref, aim_ref, bdre, bdim, ctre, ctim, a1, a2):
    for m in (bdre, bdim, ctre, ctim):
        m[...] = jnp.zeros_like(m)
    for grp in range(N_GROUP):
        k, g = divmod(grp, CH_G)
        rows = slice(g * GROUP_W, (g + 1) * GROUP_W)
        cols = slice(g * STATE, (g + 1) * STATE)
        bdre[k, rows, cols] = btre_ref[grp].astype(BF16)
        bdim[k, rows, cols] = btim_ref[grp].astype(BF16)
        ctre[k, rows, cols] = cre_ref[grp].astype(BF16)
        ctim[k, rows, cols] = cim_ref[grp].astype(BF16)
        ar = are_ref[grp:grp + 1, :]
        ai = aim_ref[grp:grp + 1, :]
        a1[k:k + 1, cols] = ar
        a1[N_CHUNK + k:N_CHUNK + k + 1, cols] = ar
        a2[k:k + 1, cols] = -ai
        a2[N_CHUNK + k:N_CHUNK + k + 1, cols] = ai


SCAN_UNROLL = 16


def _panels(tile):
    return [tile[:, p * LANES:(p + 1) * LANES] for p in range(N_PANEL)]


def _rows_load(ref, row):
    return jnp.concatenate([ref[p, pl.ds(row, TIME_BLK, stride=SUBLANES), :] for p in range(N_PANEL)], axis=1)


def _rows_store(ref, row, val):
    for p in range(N_PANEL):
        ref[p, pl.ds(row, TIME_BLK, stride=SUBLANES), :] = val[:, p * LANES:(p + 1) * LANES]


def _s5_scan_fwd(layer, proj, bbt_re, bbt_im, c_re, c_im, abar_re, abar_im, d_skip, carry=None):
    def body(u_ref, btre_ref, btim_ref, cre_ref, cim_ref, are_ref, aim_ref, d_ref, s_ref, y_ref,
             bdre, bdim, ctre, ctim, a1, a2, state):
        @pl.when(pl.program_id(0) == 0)
        def _():
            _s5_layer_fill(btre_ref, btim_ref, cre_ref, cim_ref, are_ref, aim_ref, bdre, bdim, ctre, ctim, a1, a2)
            state[...] = jnp.zeros_like(state)

        for k in range(N_CHUNK):
            ub = u_ref[:, k * CH_W:(k + 1) * CH_W].astype(BF16)
            _rows_store(s_ref, k, _dot(ub, bdre[k]))
            _rows_store(s_ref, N_CHUNK + k, _dot(ub, bdim[k]))
        m1 = _panels(a1[...])
        m2 = _panels(a2[...])

        def steps(n, tile):
            for r in range(SCAN_UNROLL):
                rows = pl.ds(pl.multiple_of((n * SCAN_UNROLL + r) * 8, 8), 8)
                tile = [m1[p] * tile[p] + m2[p] * pltpu.roll(tile[p], N_CHUNK, 0) + s_ref[p, rows, :]
                        for p in range(N_PANEL)]
                for p in range(N_PANEL):
                    s_ref[p, rows, :] = tile[p]
            return tile

        tile = lax.fori_loop(0, TIME_BLK // SCAN_UNROLL, steps, _panels(state[...]))
        state[...] = jnp.concatenate(tile, axis=1)
        d = d_ref[layer:layer + 1, :]
        for k in range(N_CHUNK):
            cols = slice(k * CH_W, (k + 1) * CH_W)
            y = (_dot_nt(_rows_load(s_ref, k).astype(BF16), ctre[k])
                 - _dot_nt(_rows_load(s_ref, N_CHUNK + k).astype(BF16), ctim[k]))
            y_ref[:, cols] = y + d[:, cols] * u_ref[:, cols]

    return _pcall(
        body, name=f"s5_fwd_l{layer}",
        out_shape=(SDS(STATE_SHAPE, F32), SDS((SEQ, WIDTH), F32)),
        grid=(N_TBLK,),
        in_specs=[pl.BlockSpec((TIME_BLK, WIDTH), lambda i: (i, 0))] + _s5_layer_specs(layer),
        out_specs=(pl.BlockSpec((N_PANEL, TIME_BLK * SUBLANES, LANES), lambda i: (0, i, 0)),
                   pl.BlockSpec((TIME_BLK, WIDTH), lambda i: (i, 0))),
        scratch_shapes=_s5_layer_scratch() + [pltpu.VMEM((8, CH_S), F32)],
        args=[proj, bbt_re, bbt_im, c_re, c_im, abar_re, abar_im, d_skip], sem=("arbitrary",), carry=carry)


def _s5_scan_bwd(layer, dy0, proj, states, bbt_re, bbt_im, c_re, c_im, abar_re, abar_im, d_skip, dproj,
                 carry=None):
    def body(dy_ref, u_ref, s_ref, sprev_ref, btre_ref, btim_ref, cre_ref, cim_ref, are_ref, aim_ref, d_ref, _,
             du_ref, gbre_ref, gbim_ref, gcre_ref, gcim_ref, gare_ref, gaim_ref, gd_ref,
             lam_ref, bdre, bdim, ctre, ctim, a1, a2, state, acc1, acc2, gbre, gbim, gcre, gcim, gd):
        step_id = pl.program_id(0)

        @pl.when(step_id == 0)
        def _():
            _s5_layer_fill(btre_ref, btim_ref, cre_ref, cim_ref, are_ref, aim_ref, bdre, bdim, ctre, ctim, a1, a2)
            for r in (state, acc1, acc2, gbre, gbim, gcre, gcim, gd):
                r[...] = jnp.zeros_like(r)

        for k in range(N_CHUNK):
            dyb = dy_ref[:, k * CH_W:(k + 1) * CH_W].astype(BF16)
            _rows_store(lam_ref, k, _dot(dyb, ctre[k]))
            _rows_store(lam_ref, N_CHUNK + k, -_dot(dyb, ctim[k]))
            gcre[k] += _dot_tn(dyb, _rows_load(s_ref, k).astype(BF16))
            gcim[k] -= _dot_tn(dyb, _rows_load(s_ref, N_CHUNK + k).astype(BF16))

        m1 = _panels(a1[...])
        m2 = _panels(-a2[...])
        has_before = (step_id < N_TBLK - 1).astype(F32)

        def one(t8, c, first_token):
            tile, swapped, p1, p2 = c
            rows = pl.ds(t8, 8)
            tile = [m1[p] * tile[p] + m2[p] * swapped[p] + lam_ref[p, rows, :] for p in range(N_PANEL)]
            swapped = [pltpu.roll(tile[p], N_CHUNK, 0) for p in range(N_PANEL)]
            for p in range(N_PANEL):
                lam_ref[p, rows, :] = tile[p]
            if first_token:
                before = [sprev_ref[p] * has_before for p in range(N_PANEL)]
            else:
                before = [s_ref[p, pl.ds(t8 - 8, 8), :] for p in range(N_PANEL)]
            p1 = [p1[p] + tile[p] * before[p] for p in range(N_PANEL)]
            p2 = [p2[p] + swapped[p] * before[p] for p in range(N_PANEL)]
            return tile, swapped, p1, p2

        def steps(n, c):
            for r in range(SCAN_UNROLL):
                t8 = pl.multiple_of((TIME_BLK - 1 - (n * SCAN_UNROLL + r)) * 8, 8)
                c = one(t8, c, False)
            return c

        tile0 = _panels(state[...])
        c = (tile0, [pltpu.roll(t, N_CHUNK, 0) for t in tile0], _panels(acc1[...]), _panels(acc2[...]))
        c = lax.fori_loop(0, TIME_BLK // SCAN_UNROLL - 1, steps, c)
        for r in range(SCAN_UNROLL - 1, -1, -1):
            c = one(r * 8, c, r == 0)
        state[...] = jnp.concatenate(c[0], axis=1)
        acc1[...] = jnp.concatenate(c[2], axis=1)
        acc2[...] = jnp.concatenate(c[3], axis=1)

        d = d_ref[layer:layer + 1, :]
        for k in range(N_CHUNK):
            cols = slice(k * CH_W, (k + 1) * CH_W)
            lrb = _rows_load(lam_ref, k).astype(BF16)
            lib = _rows_load(lam_ref, N_CHUNK + k).astype(BF16)
            u = u_ref[:, cols]
            ub = u.astype(BF16)
            dy = dy_ref[:, cols]
            du = dy * d[:, cols] + _dot_nt(lrb, bdre[k]) + _dot_nt(lib, bdim[k])
            du_ref[:, cols] = du.astype(BF16)
            gbre[k] += _dot_tn(ub, lrb)
            gbim[k] += _dot_tn(ub, lib)
        gd[...] += jnp.sum(dy_ref[...] * u_ref[...], axis=0, keepdims=True)

        @pl.when(step_id == N_TBLK - 1)
        def _():
            gd_ref[...] = gd[...]
            ga_re = acc1[0:N_CHUNK, :] + acc1[N_CHUNK:, :]
            ga_im = acc2[0:N_CHUNK, :] - acc2[N_CHUNK:, :]
            for grp in range(N_GROUP):
                k, g = divmod(grp, CH_G)
                rows = slice(g * GROUP_W, (g + 1) * GROUP_W)
                cols = slice(g * STATE, (g + 1) * STATE)
                gcre_ref[grp] = gcre[k, rows, cols]
                gcim_ref[grp] = gcim[k, rows, cols]
                gbre_ref[grp] = gbre[k, rows, cols]
                gbim_ref[grp] = gbim[k, rows, cols]
                gare_ref[grp:grp + 1, :] = ga_re[k:k + 1, cols]
                gaim_ref[grp:grp + 1, :] = ga_im[k:k + 1, cols]

    back = lambda i: N_TBLK - 1 - i
    tok = lambda: pl.BlockSpec((TIME_BLK, WIDTH), lambda i: (back(i), 0))
    mat = lambda: _const((N_GROUP, GROUP_W, STATE))
    acc_mat = pltpu.VMEM((N_CHUNK, CH_W, CH_S), F32)
    return _pcall(
        body, name=f"s5_bwd_l{layer}",
        out_shape=(SDS((SEQ, N_IN), BF16), SDS((N_GROUP, GROUP_W, STATE), F32), SDS((N_GROUP, GROUP_W, STATE), F32),
                   SDS((N_GROUP, GROUP_W, STATE), F32), SDS((N_GROUP, GROUP_W, STATE), F32),
                   SDS((N_GROUP, STATE), F32), SDS((N_GROUP, STATE), F32), SDS((1, WIDTH), F32)),
        grid=(N_TBLK,),
        in_specs=[tok(), tok(),
                  pl.BlockSpec((N_PANEL, TIME_BLK * SUBLANES, LANES), lambda i: (0, back(i), 0)),
                  pl.BlockSpec((N_PANEL, SUBLANES, LANES), lambda i: (0, jnp.maximum(back(i) * TIME_BLK - 1, 0), 0))]
        + _s5_layer_specs(layer) + [ANY],
        out_specs=(tok(), mat(), mat(), mat(), mat(), _const((N_GROUP, STATE)), _const((N_GROUP, STATE)),
                   _const((1, WIDTH))),
        scratch_shapes=[pltpu.VMEM((N_PANEL, TIME_BLK * SUBLANES, LANES), F32)] + _s5_layer_scratch()
        + [pltpu.VMEM((8, CH_S), F32)] * 3 + [acc_mat] * 4 + [pltpu.VMEM((1, WIDTH), F32)],
        args=[dy0, proj, states, states, bbt_re, bbt_im, c_re, c_im, abar_re, abar_im, d_skip, dproj],
        aliases={11: 0}, sem=("arbitrary",), limit=VMEM_LIMIT_BIG, carry=carry)


def _pool_counts(win):
    t = lax.broadcasted_iota(jnp.int32, (SEQ, POOL_GROUP), 0)
    return t, jnp.minimum(t + 1, win).astype(F32)


def _pool_fwd(layer, proj):
    def body(u_ref, o_ref):
        for gi, win in enumerate(POOL_WINDOWS):
            cols = slice(gi * POOL_GROUP, (gi + 1) * POOL_GROUP)
            u = u_ref[:, cols]
            t, count = _pool_counts(win)
            acc = u
            k = 1
            while k < win:
                acc = acc + jnp.where(t >= k, pltpu.roll(acc, k, 0), 0.0)
                k *= 2
            o_ref[:, cols] = acc / count - u

    return pl.pallas_call(
        body, name=f"pool_fwd_l{layer}",
        out_shape=SDS((SEQ, WIDTH), F32),
        grid=(1,),
        in_specs=[pl.BlockSpec((SEQ, WIDTH), lambda i: (0, 2))],
        out_specs=pl.BlockSpec((SEQ, WIDTH), lambda i: (0, 0)),
        compiler_params=_cp(("arbitrary",)),
    )(proj)


def _gelu_parts(y0):
    t = jnp.tanh(GELU_C * (y0 + GELU_A * (y0 * y0 * y0)))
    return t, 0.5 * y0 * (1.0 + t)


def _mix_forward(layer, p_ref, y0_ref, pooled_ref, wglu_ref, bglu_ref, pw_ref, scale_ref, wa_ref, wb_ref):
    za = p_ref[:, WIDTH:2 * WIDTH]
    zb = p_ref[:, 3 * WIDTH:4 * WIDTH]
    ga = p_ref[:, 4 * WIDTH:4 * WIDTH + D_MODEL]
    gb = p_ref[:, 4 * WIDTH + D_MODEL:]
    y0 = y0_ref[...]
    t, y1 = _gelu_parts(y0)
    y1b = y1.astype(BF16)
    q = _dot(y1b, wglu_ref[...].reshape(WIDTH, WIDTH)) + bglu_ref[layer:layer + 1, :]
    sq = _sig(q)
    y2 = y1 * sq
    sza = _sig(za)
    silu_za = za * sza
    ya = y2 * silu_za
    pooled = pooled_ref[...]
    mixed = jnp.concatenate(
        [_dot(pooled[:, g * POOL_GROUP:(g + 1) * POOL_GROUP].astype(BF16), pw_ref[g].astype(BF16))
         for g in range(len(POOL_WINDOWS))], axis=1)
    szb = _sig(zb)
    silu_zb = zb * szb
    scale = scale_ref[layer:layer + 1, :]
    ms = mixed * scale
    yb = ms * silu_zb
    yab = ya.astype(BF16)
    ybb = yb.astype(BF16)
    ma = _dot(yab, wa_ref[...])
    mb = _dot(ybb, wb_ref[...])
    sga = _sig(ga)
    sgb = _sig(gb)
    merged = sga * ma + sgb * mb
    return dict(za=za, zb=zb, y0=y0, t=t, y1=y1, y1b=y1b, sq=sq, y2=y2, sza=sza, silu_za=silu_za,
                pooled=pooled, mixed=mixed, szb=szb, silu_zb=silu_zb, scale=scale, ms=ms, yab=yab, ybb=ybb,
                ma=ma, mb=mb, sga=sga, sgb=sgb, merged=merged)


def _mix_weight_specs(layer):
    return [_const((N_DEV, WIDTH // N_DEV, WIDTH)),
            _const((DEPTH, WIDTH)),
            pl.BlockSpec((None, 4, POOL_GROUP, POOL_GROUP), lambda i: (layer, 0, 0, 0)),
            _const((DEPTH, WIDTH)),
            _const((WIDTH, D_MODEL)),
            _const((WIDTH, D_MODEL)),
            _const((N_DEV, D_MODEL // N_DEV, D_MODEL))]


def _loss_head(x, t_ref, g_ref, dx_ref, loss_ref, gg_ref):
    @pl.when(pl.program_id(0) == 0)
    def _():
        loss_ref[...] = jnp.zeros_like(loss_ref)
        gg_ref[...] = jnp.zeros_like(gg_ref)

    g = g_ref[...]
    rs, xn = _rms(x)
    err = xn * g - t_ref[...]
    loss_ref[...] += 0.5 * jnp.sum(jnp.mean(err * err, axis=-1, keepdims=True), axis=0, keepdims=True)
    dy = err * (1.0 / D_MODEL)
    gg_ref[...] += jnp.sum(dy * xn, axis=0, keepdims=True)
    dxn = dy * g
    dx_ref[...] = rs * (dxn - xn * jnp.mean(dxn * xn, axis=-1, keepdims=True))


def _mix_fwd(layer, x, proj, y0, pooled, wg_glu, b_glu, pool_w, pool_scale, wg_a, wg_b, wg_out, carry=None,
             head=None):
    def body(x_ref, p_ref, y0_ref, pooled_ref, wglu_ref, bglu_ref, pw_ref, scale_ref, wa_ref, wb_ref,
             wout_ref, *rest):
        f = _mix_forward(layer, p_ref, y0_ref, pooled_ref, wglu_ref, bglu_ref, pw_ref, scale_ref, wa_ref, wb_ref)
        wout = wout_ref[...].reshape(D_MODEL, D_MODEL)
        x_next = x_ref[...] + _dot(f["merged"].astype(BF16), wout)
        if head is None:
            rest[0][...] = x_next
        else:
            _loss_head(x_next, *rest)

    tile = lambda: pl.BlockSpec((TILE_M, D_MODEL), lambda i: (i, 0))
    if head is None:
        extra, out_shape, out_specs = [], [SDS((SEQ, D_MODEL), F32)], [tile()]
    else:
        extra = list(head)
        out_shape = [SDS((SEQ, D_MODEL), F32), SDS((1, 1), F32), SDS((1, D_MODEL), F32)]
        out_specs = [tile(), _const((1, 1)), _const((1, D_MODEL))]
    return _pcall(
        body, name=f"mix_fwd_l{layer}",
        out_shape=out_shape,
        grid=(SEQ // TILE_M,),
        in_specs=[tile(),
                  pl.BlockSpec((TILE_M, N_IN), lambda i: (i, 0)),
                  pl.BlockSpec((TILE_M, WIDTH), lambda i: (i, 0)),
                  pl.BlockSpec((TILE_M, WIDTH), lambda i: (i, 0))] + _mix_weight_specs(layer)
        + ([tile(), _const((1, D_MODEL))] if head else []),
        out_specs=out_specs,
        args=[x, proj, y0, pooled, wg_glu, b_glu, pool_w, pool_scale, wg_a, wg_b, wg_out] + extra,
        sem=("parallel",) if head is None else ("arbitrary",), carry=carry)


def _big_shapes():
    return dict(w_out=(DEPTH, N_DEV, D_MODEL // N_DEV, D_MODEL), w_branch_a=(DEPTH, N_DEV, WIDTH, D_MODEL // N_DEV),
                w_branch_b=(DEPTH, N_DEV, WIDTH, D_MODEL // N_DEV), ssm_w_glu=(DEPTH, N_DEV, WIDTH // N_DEV, WIDTH),
                w_in=(DEPTH, N_DEV, D_MODEL, WIDTH))


def _mix_bwd(layer, dx_next, proj, y0, pooled, wg_glu, b_glu, pool_w, pool_scale, wg_a, wg_b, wg_out, prev,
             carry=None):
    n_k = N_DEV
    n_prev = 0 if prev is None else len(prev)

    def body(*refs):
        (dx_ref, p_ref, y0_ref, pooled_ref, wglu_ref, bglu_ref, pw_ref, scale_ref, wa_ref, wb_ref,
         wout_ref) = refs[:11]
        (dproj_ref, dy0_ref, dpooled_ref, gwout_ref, gwa_ref, gwb_ref, gwglu_ref, gpw_ref,
         gscale_ref, gbglu_ref) = refs[11 + n_prev:]

        @pl.when(pl.program_id(0) == 0)
        def _():
            for r in (gwout_ref, gwa_ref, gwb_ref, gwglu_ref, gpw_ref, gscale_ref, gbglu_ref):
                r[...] = jnp.zeros_like(r)

        f = _mix_forward(layer, p_ref, y0_ref, pooled_ref, wglu_ref, bglu_ref, pw_ref, scale_ref, wa_ref, wb_ref)
        wglu = wglu_ref[...].reshape(WIDTH, WIDTH)
        wout = wout_ref[...].reshape(D_MODEL, D_MODEL)
        blk = D_MODEL // n_k
        dxb = dx_ref[...].astype(BF16)
        dmerged = _dot_nt(dxb, wout)
        gwout = _dot_tn(f["merged"].astype(BF16), dxb)
        for k in range(n_k):
            gwout_ref[_slot(k)] += gwout[k * blk:(k + 1) * blk, :]
        dma = dmerged * f["sga"]
        dmb = dmerged * f["sgb"]
        dga = dmerged * f["ma"] * f["sga"] * (1.0 - f["sga"])
        dgb = dmerged * f["mb"] * f["sgb"] * (1.0 - f["sgb"])
        dmab = dma.astype(BF16)
        dmbb = dmb.astype(BF16)
        dya = _dot_nt(dmab, wa_ref[...])
        dyb = _dot_nt(dmbb, wb_ref[...])
        gwa = _dot_tn(f["yab"], dmab)
        gwb = _dot_tn(f["ybb"], dmbb)
        for k in range(n_k):
            gwa_ref[_slot(k)] += gwa[:, k * blk:(k + 1) * blk]
            gwb_ref[_slot(k)] += gwb[:, k * blk:(k + 1) * blk]
        zb, szb = f["zb"], f["szb"]
        dzb = dyb * f["ms"] * (szb * (1.0 + zb * (1.0 - szb)))
        dms = dyb * f["silu_zb"]
        gscale_ref[...] += jnp.sum(dms * f["mixed"], axis=0, keepdims=True)
        dmixed = (dms * f["scale"]).astype(BF16)
        pooled = f["pooled"]
        for g in range(len(POOL_WINDOWS)):
            cols = slice(g * POOL_GROUP, (g + 1) * POOL_GROUP)
            dpooled_ref[:, cols] = _dot_nt(dmixed[:, cols], pw_ref[g].astype(BF16))
            gpw_ref[g] += _dot_tn(pooled[:, cols].astype(BF16), dmixed[:, cols])
        za, sza = f["za"], f["sza"]
        dza = dya * f["y2"] * (sza * (1.0 + za * (1.0 - sza)))
        dy2 = dya * f["silu_za"]
        sq = f["sq"]
        dq = dy2 * f["y1"] * sq * (1.0 - sq)
        dqb = dq.astype(BF16)
        dy1 = dy2 * sq + _dot_nt(dqb, wglu)
        gwglu = _dot_tn(f["y1b"], dqb)
        rblk = WIDTH // n_k
        for k in range(n_k):
            gwglu_ref[_slot(k)] += gwglu[k * rblk:(k + 1) * rblk, :]
        gbglu_ref[...] += jnp.sum(dq, axis=0, keepdims=True)
        y0, t = f["y0"], f["t"]
        dgelu = 0.5 * (1.0 + t) + 0.5 * y0 * (1.0 - t * t) * (GELU_C * (1.0 + 3.0 * GELU_A * y0 * y0))
        dy0_ref[...] = dy1 * dgelu
        zeros = jnp.zeros((TILE_M, WIDTH), BF16)
        dproj_ref[:, 0:WIDTH] = zeros
        dproj_ref[:, WIDTH:2 * WIDTH] = dza.astype(BF16)
        dproj_ref[:, 2 * WIDTH:3 * WIDTH] = zeros
        dproj_ref[:, 3 * WIDTH:4 * WIDTH] = dzb.astype(BF16)
        dproj_ref[:, 4 * WIDTH:4 * WIDTH + D_MODEL] = dga.astype(BF16)
        dproj_ref[:, 4 * WIDTH + D_MODEL:] = dgb.astype(BF16)

    tile = lambda w: pl.BlockSpec((TILE_M, w), lambda i: (i, 0))
    shapes = _big_shapes()
    big = ["w_out", "w_branch_a", "w_branch_b", "ssm_w_glu"]
    slab = lambda n: pl.BlockSpec((None,) + shapes[n][1:], lambda i: (layer, 0, 0, 0))
    args = [dx_next, proj, y0, pooled, wg_glu, b_glu, pool_w, pool_scale, wg_a, wg_b, wg_out]
    return _pcall(
        body, name=f"mix_bwd_l{layer}",
        out_shape=(SDS((SEQ, N_IN), BF16), SDS((SEQ, WIDTH), F32), SDS((SEQ, WIDTH), F32))
        + tuple(SDS(shapes[n], F32) for n in big)
        + (SDS((4, POOL_GROUP, POOL_GROUP), F32), SDS((1, WIDTH), F32), SDS((1, WIDTH), F32)),
        grid=(SEQ // TILE_M,),
        in_specs=[tile(D_MODEL), tile(N_IN), tile(WIDTH), tile(WIDTH)] + _mix_weight_specs(layer) + [ANY] * n_prev,
        out_specs=(tile(N_IN), tile(WIDTH), tile(WIDTH)) + tuple(slab(n) for n in big)
        + (_const((4, POOL_GROUP, POOL_GROUP)), _const((1, WIDTH)), _const((1, WIDTH))),
        args=args + list(prev or ()),
        aliases={len(args) + i: 3 + i for i in range(n_prev)},
        sem=("arbitrary",), limit=VMEM_LIMIT_BIG, carry=carry)


def _pool_bwd(layer, dpooled, dproj):
    def body(dp_ref, _, o_ref):
        for gi, win in enumerate(POOL_WINDOWS):
            cols = slice(gi * POOL_GROUP, (gi + 1) * POOL_GROUP)
            dp = dp_ref[:, cols]
            t, count = _pool_counts(win)
            e = dp / count
            acc = e
            k = 1
            while k < win:
                acc = acc + jnp.where(t < SEQ - k, pltpu.roll(acc, SEQ - k, 0), 0.0)
                k *= 2
            o_ref[:, cols] = (acc - dp).astype(BF16)

    return pl.pallas_call(
        body, name=f"pool_bwd_l{layer}",
        out_shape=SDS((SEQ, N_IN), BF16),
        grid=(1,),
        in_specs=[pl.BlockSpec((SEQ, WIDTH), lambda i: (0, 0)), ANY],
        out_specs=pl.BlockSpec((SEQ, WIDTH), lambda i: (0, 2)),
        input_output_aliases={1: 0},
        compiler_params=_cp(("arbitrary",)),
    )(dpooled, dproj)


def _proj_wgrad(layer, x, norm_g, dproj, prev, carry=None):
    tm = 1024
    n_prev = 0 if prev is None else 1

    def body(*refs):
        x_ref, g_ref, dp_ref = refs[:3]
        gw_ref, gb_ref, ht_ref = refs[3 + n_prev:]
        n, t = pl.program_id(0), pl.program_id(1)

        @pl.when(t == 0)
        def _():
            gw_ref[...] = jnp.zeros_like(gw_ref)
            gb_ref[...] = jnp.zeros_like(gb_ref)

        @pl.when(n == 0)
        def _():
            _, xn = _rms(x_ref[...])
            ht_ref[t] = (xn * g_ref[layer:layer + 1, :]).T.astype(BF16)

        dp = dp_ref[...]
        gw_ref[...] += _dot(ht_ref[t], dp)
        gb_ref[...] += jnp.sum(dp.astype(F32), axis=0, keepdims=True)

    return _pcall(
        body, name=f"proj_wgrad_l{layer}",
        out_shape=(SDS(_big_shapes()["w_in"], F32), SDS((1, N_IN), F32)),
        grid=(N_DEV, SEQ // tm),
        in_specs=[pl.BlockSpec((tm, D_MODEL), lambda n, t: (jnp.where(n == 0, t, 0), 0)),
                  _const((DEPTH, D_MODEL)),
                  pl.BlockSpec((tm, WIDTH), lambda n, t: (t, n))] + [ANY] * n_prev,
        out_specs=(pl.BlockSpec((None, None, D_MODEL, WIDTH), lambda n, t: (layer, _slot(n), 0, 0)),
                   pl.BlockSpec((1, WIDTH), lambda n, t: (0, n))),
        scratch_shapes=[pltpu.VMEM((SEQ // tm, D_MODEL, tm), BF16)],
        args=[x, norm_g, dproj] + ([prev] if n_prev else []),
        aliases={3: 0} if n_prev else {}, sem=("arbitrary", "arbitrary"), carry=carry)


def _proj_dgrad(layer, dx_next, x, norm_g, dproj, wg_in, carry=None):
    n_w = len(wg_in)

    def body(dxn_ref, x_ref, g_ref, dp_ref, *refs):
        w_refs, (dx_ref, gg_ref) = refs[:n_w], refs[n_w:]

        @pl.when(pl.program_id(0) == 0)
        def _():
            gg_ref[...] = jnp.zeros_like(gg_ref)

        parts = []
        for w_ref in w_refs:
            part = jnp.zeros((TILE_M, w_ref.shape[1]), F32)
            for k in range(N_DEV):
                part = part + _dot_nt(dp_ref[:, k * WIDTH:(k + 1) * WIDTH], w_ref[k])
            parts.append(part)
        dh = parts[0] if n_w == 1 else jnp.concatenate(parts, axis=1)
        rs, xn = _rms(x_ref[...])
        gg_ref[...] += jnp.sum(dh * xn, axis=0, keepdims=True)
        dxn = dh * g_ref[layer:layer + 1, :]
        dx_ref[...] = dxn_ref[...] + rs * (dxn - xn * jnp.mean(dxn * xn, axis=-1, keepdims=True))

    return _pcall(
        body, name=f"proj_dgrad_l{layer}",
        out_shape=(SDS((SEQ, D_MODEL), F32), SDS((1, D_MODEL), F32)),
        grid=(SEQ // TILE_M,),
        in_specs=[pl.BlockSpec((TILE_M, D_MODEL), lambda i: (i, 0)),
                  pl.BlockSpec((TILE_M, D_MODEL), lambda i: (i, 0)),
                  _const((DEPTH, D_MODEL)),
                  pl.BlockSpec((TILE_M, N_IN), lambda i: (i, 0))] + [_const(w.shape) for w in wg_in],
        out_specs=(pl.BlockSpec((TILE_M, D_MODEL), lambda i: (i, 0)), _const((1, D_MODEL))),
        args=[dx_next, x, norm_g, dproj, *wg_in], sem=("arbitrary",), carry=carry)


def _my_place():
    return lax.axis_index("x"), lax.axis_index("y"), lax.axis_index("c")


def _gather_plan(shards, layer, by_columns=(), rows_of=None):
    n = len(shards)

    def parts(ins, outs, sems):
        send_sems, recv_sems, local_sems = sems
        x, y, c = _my_place()
        chips = [(1 - x, y), (x, 1 - y), (1 - x, 1 - y)]

        def source(t):
            return ins[t].at[layer] if rows_of is None else ins[t].at[layer, pl.ds(*rows_of)]

        def rows(t, place):
            px, py, pc = place
            index = 4 * px + 2 * py + pc
            if t in by_columns:
                width = shards[t].shape[2]
                return outs[t].at[:, pl.ds(pl.multiple_of(index * width, LANES), width)]
            return outs[t].at[index]

        def copy(t, k, block, to, from_src=False):
            return pltpu.make_async_remote_copy(
                src_ref=source(t) if from_src else rows(t, block), dst_ref=rows(t, block),
                send_sem=send_sems.at[7 * t + k], recv_sem=recv_sems.at[7 * t + k], device_id=to,
                device_id_type=MESH)

        def mine(t):
            return pltpu.make_async_copy(source(t), rows(t, (x, y, c)), local_sems.at[t])

        return (x, y, c), chips, copy, mine

    def start(ins, outs, sems):
        me, chips, copy, mine = parts(ins, outs, sems)
        x, y, c = me
        for t in range(n):
            mine(t).start()
            copy(t, 0, me, (x, y, 1 - c), from_src=True).start()
            for j, chip in enumerate(chips):
                copy(t, 1 + j, me, (*chip, c), from_src=True).start()

    def relay(ins, outs, sems):
        me, chips, copy, mine = parts(ins, outs, sems)
        x, y, c = me
        for t in range(n):
            for j, chip in enumerate(chips):
                copy(t, 1 + j, (*chip, c), me).wait_recv()
                copy(t, 4 + j, (*chip, c), (x, y, 1 - c)).start()

    def finish(ins, outs, sems):
        me, chips, copy, mine = parts(ins, outs, sems)
        x, y, c = me
        sibling = (x, y, 1 - c)
        for t in range(n):
            copy(t, 0, sibling, me).wait_recv()
            for j, chip in enumerate(chips):
                copy(t, 4 + j, (*chip, 1 - c), me).wait_recv()
            for k in range(7):
                copy(t, k, me, sibling, from_src=k < 4).wait_send()
            mine(t).wait()

    n_rows = lambda a: a.shape[1] if rows_of is None else rows_of[1]
    out_shape = [SDS((a.shape[1], N_DEV * a.shape[2]) if t in by_columns else (N_DEV, n_rows(a), a.shape[2]), a.dtype)
                 for t, a in enumerate(shards)]
    sems = [pltpu.SemaphoreType.DMA((7 * n,)), pltpu.SemaphoreType.DMA((7 * n,)), pltpu.SemaphoreType.DMA((n,))]
    return _Carried(shards, out_shape, sems, start, finish, relay)


class _Carried:
    def __init__(self, ins, out_shape, sems, start, finish, relay=None):
        self.ins, self.out_shape, self.sems = list(ins), list(out_shape), list(sems)
        self.start, self.finish = start, finish
        self.relay = relay or (lambda ins, outs, sems: None)


def _pcall(body, *, name, grid, in_specs, out_specs, out_shape, args, scratch_shapes=(), aliases=None,
           sem=None, limit=VMEM_LIMIT, carry=None):
    out_shape, out_specs, scratch_shapes = list(out_shape), list(out_specs), list(scratch_shapes)
    n_in, n_out, n_scr = len(args), len(out_shape), len(scratch_shapes)
    if carry is None:
        kern, c_ins, c_out, c_sems = body, [], [], []
    else:
        c_ins, c_out, c_sems = carry.ins, carry.out_shape, carry.sems
        ci, co = len(c_ins), len(c_out)
        steps = tuple(grid)

        def kern(*refs):
            o0 = n_in + ci
            s0 = o0 + n_out + co
            mine = refs[:n_in] + refs[o0:o0 + n_out] + refs[s0:s0 + n_scr]
            theirs = (refs[n_in:o0], refs[o0 + n_out:s0], refs[s0 + n_scr:])
            first = pl.program_id(0) == 0
            last = pl.program_id(0) == steps[0] - 1
            for a in range(1, len(steps)):
                first = jnp.logical_and(first, pl.program_id(a) == 0)
                last = jnp.logical_and(last, pl.program_id(a) == steps[a] - 1)

            @pl.when(first)
            def _():
                carry.start(*theirs)

            @pl.when(last)
            def _():
                carry.relay(*theirs)

            body(*mine)

            @pl.when(last)
            def _():
                carry.finish(*theirs)

        sem = ("arbitrary",) * len(steps)
    res = pl.pallas_call(
        kern, name=name, grid=tuple(grid),
        in_specs=list(in_specs) + [ANY] * len(c_ins),
        out_specs=tuple(out_specs + [ANY] * len(c_out)),
        out_shape=tuple(out_shape + c_out),
        scratch_shapes=scratch_shapes + c_sems,
        input_output_aliases=aliases or {},
        compiler_params=_cp(sem, limit),
    )(*args, *c_ins)
    return res[:n_out], res[n_out:]


def _run_carried(name, carry):
    ci, co = len(carry.ins), len(carry.out_shape)

    def body(*refs):
        parts = (refs[:ci], refs[ci:ci + co], refs[ci + co:])
        carry.start(*parts)
        carry.relay(*parts)
        carry.finish(*parts)

    return pl.pallas_call(
        body, name=name, out_shape=tuple(carry.out_shape),
        in_specs=[ANY] * ci, out_specs=tuple([ANY] * co), scratch_shapes=carry.sems,
    )(*carry.ins)


def _sibling_plan(big, small):
    n = len(big)
    n_copies = 4 * n + len(small)

    def copies(ins, outs, sems):
        send_sems, recv_sems = sems
        x, y, c = _my_place()
        pairs = []
        for t, (_, layer) in enumerate(big):
            for s in range(4):
                pairs.append((ins[t].at[layer, pl.ds(4 * (1 - c) + s, 1)], outs[t].at[pl.ds(s, 1)]))
        pairs += list(zip(ins[n:], outs[n:]))
        return [pltpu.make_async_remote_copy(
            src_ref=src, dst_ref=dst, send_sem=send_sems.at[k], recv_sem=recv_sems.at[k],
            device_id=(x, y, 1 - c), device_id_type=MESH) for k, (src, dst) in enumerate(pairs)]

    def start(ins, outs, sems):
        for cp in copies(ins, outs, sems):
            cp.start()

    def finish(ins, outs, sems):
        for cp in copies(ins, outs, sems):
            cp.wait()

    out_shape = [SDS((4,) + a.shape[2:], a.dtype) for a, _ in big] + [SDS(a.shape, a.dtype) for a in small]
    sems = [pltpu.SemaphoreType.DMA((n_copies,)), pltpu.SemaphoreType.DMA((n_copies,))]
    return _Carried([a for a, _ in big] + list(small), out_shape, sems, start, finish)


def _chips_plan(big, small):
    n, n_small = len(big), len(small)
    max_rows = 512
    parts = [max(1, a.shape[1] // max_rows) for a in big]
    n_copies = 3 * (sum(parts) + n_small)

    def copies(ins, outs, sems, landing):
        send_sems, recv_sems, local_sems = sems
        x, y, c = _my_place()
        my_chip = 2 * x + y
        chips = [(1 - x, y), (x, 1 - y), (1 - x, 1 - y)]
        remote, local = [], []
        for chip in chips:
            to = 2 * chip[0] + chip[1]
            slot = to if landing else my_chip
            pairs = []
            for t in range(n):
                rows_per = big[t].shape[1] // parts[t]
                for p in range(parts[t]):
                    rows = pl.ds(p * rows_per, rows_per)
                    pairs.append((ins[t].at[to, rows], outs[t].at[slot, rows]))
            pairs += [(ins[t], outs[t].at[slot]) for t in range(n, n + n_small)]
            for src, dst in pairs:
                k = len(remote)
                remote.append(pltpu.make_async_remote_copy(
                    src_ref=src, dst_ref=dst, send_sem=send_sems.at[k], recv_sem=recv_sems.at[k],
                    device_id=(*chip, c), device_id_type=MESH))
        for t in range(n):
            local.append(pltpu.make_async_copy(ins[t].at[my_chip], outs[t].at[my_chip], local_sems.at[t]))
        for t in range(n, n + n_small):
            local.append(pltpu.make_async_copy(ins[t], outs[t].at[my_chip], local_sems.at[t]))
        return remote + local

    def start(ins, outs, sems):
        for cp in copies(ins, outs, sems, landing=False):
            cp.start()

    def finish(ins, outs, sems):
        for cp in copies(ins, outs, sems, landing=True):
            cp.wait()

    out_shape = [SDS(a.shape, a.dtype) for a in big] + [SDS((N_CHIP,) + a.shape, a.dtype) for a in small]
    sems = [pltpu.SemaphoreType.DMA((n_copies,)), pltpu.SemaphoreType.DMA((n_copies,)),
            pltpu.SemaphoreType.DMA((n + n_small,))]
    return _Carried(list(big) + list(small), out_shape, sems, start, finish)


def _all_plan(small):
    n = len(small)
    masks = [(m >> 2 & 1, m >> 1 & 1, m & 1) for m in range(1, N_DEV)]

    def copies(ins, outs, sems, landing):
        send_sems, recv_sems, local_sems = sems
        x, y, c = _my_place()
        me = 4 * x + 2 * y + c
        flip = lambda v, bit: 1 - v if bit else v
        remote = []
        for fx, fy, fc in masks:
            peer = (flip(x, fx), flip(y, fy), flip(c, fc))
            slot = 4 * peer[0] + 2 * peer[1] + peer[2] if landing else me
            for t in range(n):
                k = len(remote)
                remote.append(pltpu.make_async_remote_copy(
                    src_ref=ins[t], dst_ref=outs[t].at[slot], send_sem=send_sems.at[k], recv_sem=recv_sems.at[k],
                    device_id=peer, device_id_type=MESH))
        local = [pltpu.make_async_copy(ins[t], outs[t].at[me], local_sems.at[t]) for t in range(n)]
        return remote + local

    def start(ins, outs, sems):
        for cp in copies(ins, outs, sems, landing=False):
            cp.start()

    def finish(ins, outs, sems):
        for cp in copies(ins, outs, sems, landing=True):
            cp.wait()

    out_shape = [SDS((N_DEV,) + a.shape, a.dtype) for a in small]
    sems = [pltpu.SemaphoreType.DMA((7 * n,)), pltpu.SemaphoreType.DMA((7 * n,)), pltpu.SemaphoreType.DMA((n,))]
    return _Carried(list(small), out_shape, sems, start, finish)


def _join(*plans):
    plans = [p for p in plans if p is not None]
    if len(plans) <= 1:
        return plans[0] if plans else None

    def each(fn_name, ins, outs, sems):
        i = o = s = 0
        for p in plans:
            ni, no, ns = len(p.ins), len(p.out_shape), len(p.sems)
            getattr(p, fn_name)(ins[i:i + ni], outs[o:o + no], sems[s:s + ns])
            i, o, s = i + ni, o + no, s + ns

    return _Carried(sum((p.ins for p in plans), []), sum((p.out_shape for p in plans), []),
                    sum((p.sems for p in plans), []),
                    lambda i, o, s: each("start", i, o, s), lambda i, o, s: each("finish", i, o, s),
                    lambda i, o, s: each("relay", i, o, s))


def _row_block(rows, most=256):
    return min(rows, most)


def _add_own(tag, core, gs, layer, gots):
    n = len(gs)

    def body(core_ref, *refs):
        for a_ref, b_ref, o_ref in zip(refs[:n], refs[n:2 * n], refs[2 * n:]):
            o_ref[...] = (a_ref[...] + b_ref[...]).astype(o_ref.dtype)

    mine = lambda a: pl.BlockSpec((None, None) + a.shape[1:], lambda s, core: (layer, 4 * core[0] + s, 0, 0))
    theirs = lambda a: pl.BlockSpec((None,) + a.shape[1:], lambda s, core: (s, 0, 0))
    return pl.pallas_call(
        body, name=f"add_{tag}", out_shape=tuple(SDS(a.shape, BF16) for a in gots),
        grid_spec=pltpu.PrefetchScalarGridSpec(
            num_scalar_prefetch=1, grid=(4,),
            in_specs=[mine(a) for a in gots] + [theirs(a) for a in gots],
            out_specs=tuple(theirs(a) for a in gots)),
        compiler_params=_cp(("parallel",)),
    )(core, *gs, *gots)


def _add_lists(tag, own, got, grid=None, specs=None, dtype=F32):
    n = len(own)

    def body(*refs):
        for a, b, o in zip(refs[:n], refs[n:2 * n], refs[2 * n:]):
            o[...] = (a[...] + b[...]).astype(o.dtype)

    kw = {}
    if grid is not None:
        kw = dict(grid=grid, in_specs=list(specs) * 2, out_specs=tuple(specs),
                  compiler_params=_cp(("parallel",) * len(grid)))
    return pl.pallas_call(
        body, name=f"add_{tag}", out_shape=tuple(SDS(a.shape, dtype) for a in own), **kw)(*own, *got)


def _adamw_math(w, g, m, v):
    m = ADAM_B1 * m + (1.0 - ADAM_B1) * g
    v = ADAM_B2 * v + (1.0 - ADAM_B2) * (g * g)
    m_hat = m / (1.0 - ADAM_B1 ** ADAM_STEP)
    v_hat = v / (1.0 - ADAM_B2 ** ADAM_STEP)
    delta = -ADAM_LR * (m_hat / (jnp.sqrt(v_hat) + ADAM_EPS) + ADAM_WD * w)
    return delta, m, v


def _sum_slots_adamw(tag, slots, w, m, v):
    _, r, c = slots[0].shape
    rb = _row_block(r, most=512)

    def body(s0_ref, s1_ref, w_ref, m_ref, v_ref, g_ref, d_ref, nm_ref, nv_ref):
        first = pl.program_id(1) == 0
        g = _pair_sum([jnp.where(first, s0_ref[k], s1_ref[k]).astype(F32) for k in range(N_CHIP)])
        delta, nm, nv = _adamw_math(w_ref[...], g, m_ref[...], v_ref[...])
        g_ref[...] = g
        d_ref[...] = delta
        nm_ref[...] = nm
        nv_ref[...] = nv

    spec = pl.BlockSpec((None, rb, c), lambda j, l: (l, j, 0))
    sspec = pl.BlockSpec((N_CHIP, rb, c), lambda j, l: (0, j, 0))
    s = SDS((DEPTH, r, c), F32)
    return pl.pallas_call(
        body, name=f"adamw_{tag}", out_shape=(s, s, s, s),
        grid=(r // rb, DEPTH), in_specs=[sspec, sspec, spec, spec, spec], out_specs=(spec, spec, spec, spec),
        compiler_params=_cp(("parallel", "arbitrary")),
    )(*slots, w, m, v)


def _adamw_small(tag, entries, grid=None, sums=()):
    flat_in, in_specs, out_shape, out_specs, layout = [], [], [], [], []
    for slots, w, m, v, slot_spec, w_spec in entries:
        per_layer = isinstance(slots, (list, tuple))
        n_slot = len(slots) if per_layer else 1
        flat_in += (list(slots) if per_layer else [slots]) + [w, m, v]
        in_specs += [slot_spec] * n_slot + [w_spec] * 3
        out_shape += [SDS(w.shape, F32)] * 4
        out_specs += [w_spec] * 4
        layout.append((per_layer, n_slot))
    n_entry_in = len(flat_in)
    flat_in += list(sums)
    out_shape += [SDS(s.shape[1:], F32) for s in sums]
    n_in = len(flat_in)

    def body(*refs):
        for s_ref, o_ref in zip(refs[n_entry_in:n_in], refs[len(refs) - len(sums):]):
            o_ref[...] = _sum_slots(s_ref)
        i, o = 0, n_in
        for per_layer, n_slot in layout:
            s_refs = refs[i:i + n_slot]
            w_ref, m_ref, v_ref = refs[i + n_slot:i + n_slot + 3]
            outs = refs[o:o + 4]
            if per_layer:
                for l, s_ref in enumerate(s_refs):
                    at = (slice(l, l + 1),) if len(w_ref.shape) == 2 else (l,)
                    g = _sum_slots(s_ref)
                    res = (g,) + _adamw_math(w_ref[at], g, m_ref[at], v_ref[at])
                    for o_ref, val in zip(outs, res):
                        o_ref[at] = val
            else:
                g = _sum_slots(s_refs[0])
                res = (g,) + _adamw_math(w_ref[...], g, m_ref[...], v_ref[...])
                for o_ref, val in zip(outs, res):
                    o_ref[...] = val
            i += n_slot + 3
            o += 4

    kw = {}
    if grid is not None:
        kw = dict(grid=grid, in_specs=in_specs, out_specs=tuple(out_specs),
                  compiler_params=_cp(("parallel",) * len(grid)))
    res = pl.pallas_call(body, name=f"adamw_{tag}", out_shape=tuple(out_shape), **kw)(*flat_in)
    return [tuple(res[4 * e:4 * e + 4]) for e in range(len(entries))], res[4 * len(entries):]


def kernel(x, norm_g, w_in, b_in, ssm_log_dt, ssm_lam_re, ssm_lam_im, ssm_b_re, ssm_b_im, ssm_c_re, ssm_c_im, ssm_d, ssm_w_glu, ssm_b_glu, pool_w, pool_scale, w_branch_a, w_branch_b, w_out, final_norm_g, loss_target, m_norm_g, m_w_in, m_b_in, m_ssm_log_dt, m_ssm_lam_re, m_ssm_lam_im, m_ssm_b_re, m_ssm_b_im, m_ssm_c_re, m_ssm_c_im, m_ssm_d, m_ssm_w_glu, m_ssm_b_glu, m_pool_w, m_pool_scale, m_w_branch_a, m_w_branch_b, m_w_out, m_final_norm_g, v_norm_g, v_w_in, v_b_in, v_ssm_log_dt, v_ssm_lam_re, v_ssm_lam_im, v_ssm_b_re, v_ssm_b_im, v_ssm_c_re, v_ssm_c_im, v_ssm_d, v_ssm_w_glu, v_ssm_b_glu, v_pool_w, v_pool_scale, v_w_branch_a, v_w_branch_b, v_w_out, v_final_norm_g):
    weights = dict(norm_g=norm_g, w_in=w_in, b_in=b_in, ssm_log_dt=ssm_log_dt, ssm_lam_re=ssm_lam_re,
                   ssm_lam_im=ssm_lam_im, ssm_b_re=ssm_b_re, ssm_b_im=ssm_b_im, ssm_c_re=ssm_c_re,
                   ssm_c_im=ssm_c_im, ssm_d=ssm_d, ssm_w_glu=ssm_w_glu, ssm_b_glu=ssm_b_glu, pool_w=pool_w,
                   pool_scale=pool_scale, w_branch_a=w_branch_a, w_branch_b=w_branch_b, w_out=w_out,
                   final_norm_g=final_norm_g.reshape(1, D_MODEL))
    mom_m = dict(norm_g=m_norm_g, w_in=m_w_in, b_in=m_b_in, ssm_log_dt=m_ssm_log_dt, ssm_lam_re=m_ssm_lam_re,
                 ssm_lam_im=m_ssm_lam_im, ssm_b_re=m_ssm_b_re, ssm_b_im=m_ssm_b_im, ssm_c_re=m_ssm_c_re,
                 ssm_c_im=m_ssm_c_im, ssm_d=m_ssm_d, ssm_w_glu=m_ssm_w_glu, ssm_b_glu=m_ssm_b_glu,
                 pool_w=m_pool_w, pool_scale=m_pool_scale, w_branch_a=m_w_branch_a, w_branch_b=m_w_branch_b,
                 w_out=m_w_out, final_norm_g=m_final_norm_g.reshape(1, D_MODEL))
    mom_v = dict(norm_g=v_norm_g, w_in=v_w_in, b_in=v_b_in, ssm_log_dt=v_ssm_log_dt, ssm_lam_re=v_ssm_lam_re,
                 ssm_lam_im=v_ssm_lam_im, ssm_b_re=v_ssm_b_re, ssm_b_im=v_ssm_b_im, ssm_c_re=v_ssm_c_re,
                 ssm_c_im=v_ssm_c_im, ssm_d=v_ssm_d, ssm_w_glu=v_ssm_w_glu, ssm_b_glu=v_ssm_b_glu,
                 pool_w=v_pool_w, pool_scale=v_pool_scale, w_branch_a=v_w_branch_a, w_branch_b=v_w_branch_b,
                 w_out=v_w_out, final_norm_g=v_final_norm_g.reshape(1, D_MODEL))
    order = ["norm_g", "w_in", "b_in", "ssm_log_dt", "ssm_lam_re", "ssm_lam_im", "ssm_b_re", "ssm_b_im",
             "ssm_c_re", "ssm_c_im", "ssm_d", "ssm_w_glu", "ssm_b_glu", "pool_w", "pool_scale", "w_branch_a",
             "w_branch_b", "w_out", "final_norm_g"]
    big_names = ["w_in", "ssm_w_glu", "w_branch_a", "w_branch_b", "w_out"]

    log_dt3 = ssm_log_dt.reshape(DEPTH, N_GROUP, 1)
    b_t = lambda a: a.transpose(0, 1, 3, 2)
    for d in (weights, mom_m, mom_v):
        d["ssm_b_re"], d["ssm_b_im"] = b_t(d["ssm_b_re"]), b_t(d["ssm_b_im"])
    bt_re, bt_im = weights["ssm_b_re"], weights["ssm_b_im"]
    abar_re, abar_im, bbt_re, bbt_im = _s5_params(log_dt3, ssm_lam_re, ssm_lam_im, bt_re, bt_im)
    s5_args = (bbt_re, bbt_im, ssm_c_re, ssm_c_im, abar_re, abar_im, ssm_d)

    w16 = {n: weights[n].astype(BF16) for n in big_names}
    rest = [w16[n] for n in big_names[1:]]
    half = D_MODEL // 2
    wg_in = [None, [None, None]]
    wg_rest = [None, None]
    wg_in[0] = list(_run_carried("gather_w_in_l0", _gather_plan([w16["w_in"]], 0)))
    xs = [x.reshape(SEQ, D_MODEL)]
    saved = []
    for l in range(DEPTH):
        proj, moved = _norm_proj(l, xs[l], norm_g, wg_in[l], b_in,
                                 carry=_gather_plan([w16["w_in"]], 1, rows_of=(0, half)) if l == 0 else None)
        if l == 0:
            (wg_in[1][0],) = moved
        (states, y0), wg_rest[l] = _s5_scan_fwd(l, proj, *s5_args, carry=_gather_plan(rest, l, by_columns=(1, 2)))
        pooled = _pool_fwd(l, proj)
        wg_glu, wg_a, wg_b, wg_out = wg_rest[l]
        last = l == DEPTH - 1
        res, moved = _mix_fwd(
            l, xs[l], proj, y0, pooled, wg_glu, ssm_b_glu, pool_w, pool_scale, wg_a, wg_b, wg_out,
            carry=_gather_plan([w16["w_in"]], 1, rows_of=(half, half)) if l == 0 else None,
            head=(loss_target.reshape(SEQ, D_MODEL), weights["final_norm_g"]) if last else None)
        if l == 0:
            (wg_in[1][1],) = moved
        if last:
            dx, loss_part, g_final = res
        else:
            xs.append(res[0])
        saved.append((proj, states, y0, pooled))

    core = lax.axis_index("c").astype(jnp.int32).reshape(1)
    vec_names = ["norm_g", "b_in", "ssm_d", "ssm_b_glu", "pool_scale", "ssm_log_dt"]
    s5_names = ["ssm_log_dt", "ssm_lam_re", "ssm_lam_im", "ssm_b_re", "ssm_b_im"]
    mat_names = ["pool_w", "ssm_c_re", "ssm_c_im", "ssm_b_re", "ssm_b_im"]
    lane_sparse = ("ssm_c_re", "ssm_c_im", "ssm_b_re", "ssm_b_im")

    def dense(key, a):
        return a.reshape(-1, LANES) if key[0] in lane_sparse else a

    def undense(key, slots):
        return slots.reshape((N_CHIP, N_GROUP, GROUP_W, STATE)) if key[0] in lane_sparse else slots

    def add_small(tag, keys, own, got):
        out = [None] * len(keys)
        whole = [i for i, k in enumerate(keys) if k[0] not in mat_names]
        tiled = [i for i, k in enumerate(keys) if k[0] in mat_names]
        if whole:
            for i, r in zip(whole, _add_lists(f"{tag}_a", [own[i] for i in whole], [got[i] for i in whole])):
                out[i] = r
        if tiled:
            specs = [pl.BlockSpec((1, POOL_GROUP, POOL_GROUP), lambda j: (j, 0, 0)) if keys[i][0] == "pool_w"
                     else pl.BlockSpec((own[i].shape[0] // N_CHUNK, LANES), lambda j: (j, 0)) for i in tiled]
            for i, r in zip(tiled, _add_lists(f"{tag}_b", [own[i] for i in tiled], [got[i] for i in tiled],
                                              grid=(N_CHUNK,), specs=specs, dtype=BF16)):
                out[i] = r
        return out

    sm = {("final_norm_g", None): g_final, ("loss", None): loss_part}
    slots = {}
    grads = dict.fromkeys(big_names)

    class Wave:
        def __init__(self, tag, layer, big, keys):
            self.tag, self.layer, self.big, self.keys = tag, layer, big, keys

        def to_sibling(self):
            self.own = [dense(k, sm[k]) for k in self.keys]
            return _sibling_plan([(grads[n], self.layer) for n in self.big], self.own)

        def add(self, moved):
            nb = len(self.big)
            self.chip_big = list(_add_own(self.tag, core, [grads[n] for n in self.big], self.layer, moved[:nb])
                                 ) if nb else []
            self.chip_small = add_small(self.tag, self.keys, self.own, moved[nb:])

        def to_chips(self, big=None, small=True):
            self.sent = list(self.big if big is None else big), small
            return _chips_plan([self.chip_big[self.big.index(n)] for n in self.sent[0]],
                               self.chip_small if small else [])

        def landed(self, moved):
            names, small = self.sent
            for n, s in zip(names, moved[:len(names)]):
                slots[(n, self.layer)] = s
            if small:
                for k, s in zip(self.keys, moved[len(names):]):
                    slots[k] = undense(k, s)
            return moved[len(names) + (len(self.keys) if small else 0):]

    def s5_param_grads(l, g_abar_re, g_abar_im, g_bbt_re, g_bbt_im):
        g = _s5_params_bwd(l, log_dt3, ssm_lam_re, ssm_lam_im, bt_re, bt_im, g_abar_re, g_abar_im, g_bbt_re, g_bbt_im)
        sm[("ssm_log_dt", l)] = g[0].reshape(1, N_GROUP)
        for n, a in zip(s5_names[1:], g[1:]):
            sm[(n, l)] = a

    small1 = ["b_in", "ssm_d", "ssm_b_glu", "pool_scale", "pool_w", "ssm_c_re", "ssm_c_im"] + s5_names
    w1 = Wave("chip1", 1, list(big_names), [(n, 1) for n in small1] + [("final_norm_g", None), ("loss", None)])
    early = Wave("chip0e", 0, big_names[1:], [("pool_w", 0), ("pool_scale", 0), ("ssm_b_glu", 0)])
    mid = Wave("chip0m", 0, [], [(n, 0) for n in ["ssm_c_re", "ssm_c_im", "ssm_d"] + s5_names] + [("norm_g", 1)])
    late = Wave("chip0l", 0, ["w_in"], [("b_in", 0)])

    mix_prev, gw_in = None, None
    for l in reversed(range(DEPTH)):
        proj, states, y0, pooled = saved[l]
        wg_glu, wg_a, wg_b, wg_out = wg_rest[l]
        res, moved = _mix_bwd(l, dx, proj, y0, pooled, wg_glu, ssm_b_glu, pool_w, pool_scale, wg_a, wg_b, wg_out,
                              mix_prev, carry=None if l == 1 else w1.to_chips(big=["w_in"], small=False))
        if l == 0:
            w1.landed(moved)
        dproj, dy0, dpooled = res[:3]
        mix_prev = list(res[3:7])
        grads["w_out"], grads["w_branch_a"], grads["w_branch_b"], grads["ssm_w_glu"] = mix_prev
        sm[("pool_w", l)], sm[("pool_scale", l)], sm[("ssm_b_glu", l)] = res[7:]
        dproj = _pool_bwd(l, dpooled, dproj)
        carry = None if l == 1 else _join(w1.to_chips(big=big_names[1:]), early.to_sibling())
        res, moved = _s5_scan_bwd(l, dy0, proj, states, *s5_args, dproj, carry=carry)
        if l == 0:
            early.add(w1.landed(moved))
        dproj, g_bbt_re, g_bbt_im, sm[("ssm_c_re", l)], sm[("ssm_c_im", l)], g_abar_re, g_abar_im, sm[("ssm_d", l)] = res
        s5_param_grads(l, g_abar_re, g_abar_im, g_bbt_re, g_bbt_im)
        carry = None if l == 1 else _join(early.to_chips(), mid.to_sibling())
        (gw_in, sm[("b_in", l)]), moved = _proj_wgrad(l, xs[l], norm_g, dproj, gw_in, carry=carry)
        grads["w_in"] = gw_in
        if l == 0:
            mid.add(early.landed(moved))
        carry = w1.to_sibling() if l == 1 else _join(mid.to_chips(), late.to_sibling())
        (dx, sm[("norm_g", l)]), moved = _proj_dgrad(l, dx, xs[l], norm_g, dproj, wg_in[l], carry=carry)
        if l == 1:
            w1.add(moved)
        else:
            late.add(mid.landed(moved))
    grad_x = dx.reshape(1, SEQ, D_MODEL)
    moved = late.landed(_run_carried("exchange_last", _join(late.to_chips(), _all_plan([sm[("norm_g", 0)]]))))
    slots[("norm_g", 0)] = moved[0]

    res = {}
    for n in big_names:
        res[n] = _sum_slots_adamw(n, [slots[(n, l)] for l in range(DEPTH)], weights[n], mom_m[n], mom_v[n])
    per_layer = lambda n: [slots[(n, l)] for l in range(DEPTH)]
    names_a = vec_names + ["ssm_lam_re", "ssm_lam_im"]
    entries_a = [(per_layer(n), weights[n], mom_m[n], mom_v[n], None, None) for n in names_a]
    n = "final_norm_g"
    entries_a.append((slots[(n, None)], weights[n], mom_m[n], mom_v[n], None, None))
    out_a, (loss,) = _adamw_small("small_a", entries_a, sums=[slots[("loss", None)]])
    loss = loss.reshape(())
    for n, r in zip(names_a + ["final_norm_g"], out_a):
        res[n] = r
    res["final_norm_g"] = tuple(a.reshape(D_MODEL) for a in res["final_norm_g"])
    pw_s = pl.BlockSpec((N_CHIP, 1, POOL_GROUP, POOL_GROUP), lambda j: (0, j, 0, 0))
    pw_w = pl.BlockSpec((DEPTH, 1, POOL_GROUP, POOL_GROUP), lambda j: (0, j, 0, 0))
    c_s = pl.BlockSpec((N_CHIP, CH_G, GROUP_W, STATE), lambda j: (0, j, 0, 0))
    c_w = pl.BlockSpec((DEPTH, CH_G, GROUP_W, STATE), lambda j: (0, j, 0, 0))
    entries_b = [(per_layer(n), weights[n], mom_m[n], mom_v[n], pw_s if n == "pool_w" else c_s,
                  pw_w if n == "pool_w" else c_w) for n in mat_names]
    out_b, _ = _adamw_small("small_b", entries_b, grid=(N_CHUNK,))
    for n, r in zip(mat_names, out_b):
        res[n] = tuple(b_t(a) for a in r) if n in ("ssm_b_re", "ssm_b_im") else r

    outs = [loss, grad_x]
    for i in range(4):
        outs += [res[n][i] for n in order]
    return tuple(outs)
```

```python
import math

import jax
import jax.numpy as jnp
from jax import lax
from jax.experimental import pallas as pl
from jax.experimental.pallas import tpu as pltpu

F32 = jnp.float32
BF16 = jnp.bfloat16

SEQ = 2048
D_MODEL = 1024
N_IN = 4096
WIDTH = 512
N_GROUP = 32
GROUP_W = 16
STATE = 64
N_STATE = N_GROUP * STATE
N_CHUNK = 4
CH_G = N_GROUP // N_CHUNK
CH_W = WIDTH // N_CHUNK
CH_S = N_STATE // N_CHUNK
N_DEV = 8
N_CHIP = 4
POOL_WINDOWS = (2, 4, 8, 16)
POOL_GROUP = 128
EPS = 1e-6
DEPTH = 2

ADAM_LR = 0.001
ADAM_B1 = 0.9
ADAM_B2 = 0.999
ADAM_EPS = 1e-08
ADAM_WD = 0.01
ADAM_STEP = 10

LANES = 128
SUBLANES = 8
TILE_M = 256
VMEM_LIMIT = 48 * 1024 * 1024
VMEM_LIMIT_BIG = 60 * 1024 * 1024
MESH = pl.DeviceIdType.MESH
ANY = pl.BlockSpec(memory_space=pl.ANY)

GELU_C = math.sqrt(2.0 / math.pi)
GELU_A = 0.044715

SDS = jax.ShapeDtypeStruct


def _cp(sem=None, limit=VMEM_LIMIT):
    return pltpu.CompilerParams(dimension_semantics=sem, vmem_limit_bytes=limit)


def _dot(a, b):
    return jnp.dot(a, b, preferred_element_type=F32)


def _dot_nt(a, b):
    return lax.dot_general(a, b, (((1,), (1,)), ((), ())), preferred_element_type=F32)


def _dot_tn(a, b):
    return lax.dot_general(a, b, (((0,), (0,)), ((), ())), preferred_element_type=F32)


def _sig(x):
    return jax.nn.sigmoid(x)


def _rms(x):
    rs = lax.rsqrt(jnp.mean(x * x, axis=-1, keepdims=True) + EPS)
    return rs, x * rs


def _slot(n):
    return 4 * (n % 2) + n // 2


def _const(shape):
    n = len(shape)
    return pl.BlockSpec(shape, lambda *_: (0,) * n)


def _pair_sum(vals):
    while len(vals) > 1:
        vals = [vals[i] + vals[i + 1] for i in range(0, len(vals), 2)]
    return vals[0]


def _sum_slots(s_ref):
    return _pair_sum([s_ref[k].astype(F32) for k in range(s_ref.shape[0])])


def _s5_param_fn(log_dt, lam_re, lam_im, bt_re, bt_im):
    dt = jnp.exp(log_dt)
    mag = jnp.exp(lam_re * dt)
    ang = lam_im * dt
    abar_re = mag * jnp.cos(ang)
    abar_im = mag * jnp.sin(ang)
    num_re = abar_re - 1.0
    num_im = abar_im
    den = lam_re * lam_re + lam_im * lam_im
    coef_re = (num_re * lam_re + num_im * lam_im) / den
    coef_im = (num_im * lam_re - num_re * lam_im) / den
    bbar_re = coef_re[..., None, :] * bt_re - coef_im[..., None, :] * bt_im
    bbar_im = coef_re[..., None, :] * bt_im + coef_im[..., None, :] * bt_re
    return abar_re, abar_im, bbar_re, bbar_im


def _s5_params(log_dt, lam_re, lam_im, bt_re, bt_im):
    def body(ld, lr, li, br, bi, o_ar, o_ai, o_br, o_bi):
        ar, ai, bbr, bbi = _s5_param_fn(ld[...], lr[...], li[...], br[...], bi[...])
        o_ar[...] = ar
        o_ai[...] = ai
        o_br[...] = bbr
        o_bi[...] = bbi

    return pl.pallas_call(
        body, name="s5_params",
        out_shape=(SDS(lam_re.shape, F32), SDS(lam_re.shape, F32), SDS(bt_re.shape, F32), SDS(bt_re.shape, F32)),
    )(log_dt, lam_re, lam_im, bt_re, bt_im)


def _s5_params_bwd(layer, log_dt, lam_re, lam_im, bt_re, bt_im, g_ar, g_ai, g_br, g_bi):
    def body(ld, lr, li, br, bi, car, cai, cbr, cbi, o_ld, o_lr, o_li, o_br, o_bi):
        _, vjp = jax.vjp(_s5_param_fn, ld[...], lr[...], li[...], br[...], bi[...])
        d_ld, d_lr, d_li, d_br, d_bi = vjp((car[...], cai[...], cbr[...], cbi[...]))
        o_ld[...] = d_ld
        o_lr[...] = d_lr
        o_li[...] = d_li
        o_br[...] = d_br
        o_bi[...] = d_bi

    one = lambda shape: pl.BlockSpec((None,) + shape, lambda i: (layer,) + (0,) * len(shape))
    whole = lambda shape: _const(shape)
    vec, lam, mat = (N_GROUP, 1), (N_GROUP, STATE), (N_GROUP, GROUP_W, STATE)
    return pl.pallas_call(
        body, name=f"s5_params_bwd_l{layer}", grid=(1,),
        in_specs=[one(vec), one(lam), one(lam), one(mat), one(mat), whole(lam), whole(lam), whole(mat), whole(mat)],
        out_specs=(whole(vec), whole(lam), whole(lam), whole(mat), whole(mat)),
        out_shape=(SDS(vec, F32), SDS(lam, F32), SDS(lam, F32), SDS(mat, F32), SDS(mat, F32)),
    )(log_dt, lam_re, lam_im, bt_re, bt_im, g_ar, g_ai, g_br, g_bi)


def _norm_proj(layer, x, norm_g, wg_in, b_in, carry=None):
    n_w = len(wg_in)

    def body(x_ref, g_ref, b_ref, *refs):
        w_refs, o_ref = refs[:n_w], refs[n_w]
        _, xn = _rms(x_ref[...])
        h = (xn * g_ref[layer:layer + 1, :]).astype(BF16)
        for k in range(N_DEV):
            cols = slice(k * WIDTH, (k + 1) * WIDTH)
            acc = b_ref[layer:layer + 1, cols]
            row = 0
            for w_ref in w_refs:
                rows = w_ref.shape[1]
                acc = acc + _dot(h[:, row:row + rows], w_ref[k])
                row += rows
            o_ref[:, cols] = acc

    (proj,), moved = _pcall(
        body, name=f"norm_proj_l{layer}",
        out_shape=[SDS((SEQ, N_IN), F32)],
        grid=(SEQ // TILE_M,),
        in_specs=[pl.BlockSpec((TILE_M, D_MODEL), lambda i: (i, 0)),
                  _const((DEPTH, D_MODEL)),
                  _const((DEPTH, N_IN))] + [_const(w.shape) for w in wg_in],
        out_specs=[pl.BlockSpec((TILE_M, N_IN), lambda i: (i, 0))],
        args=[x, norm_g, b_in, *wg_in], sem=("parallel",), carry=carry)
    return proj, moved


TIME_BLK = 512
N_TBLK = SEQ // TIME_BLK
N_PANEL = CH_S // LANES
STATE_SHAPE = (N_PANEL, SEQ * SUBLANES, LANES)


def _s5_layer_specs(layer):
    mat = lambda: pl.BlockSpec((None, N_GROUP, GROUP_W, STATE), lambda i: (layer, 0, 0, 0))
    ab = lambda: pl.BlockSpec((None, N_GROUP, STATE), lambda i: (layer, 0, 0))
    return [mat(), mat(), mat(), mat(), ab(), ab(), _const((DEPTH, WIDTH))]


def _s5_layer_scratch():
    return [pltpu.VMEM((N_CHUNK, CH_W, CH_S), BF16)] * 4 + [pltpu.VMEM((8, CH_S), F32)] * 2


def _s5_layer_fill(btre_ref, btim_ref, cre_ref, cim_ref, are_ref, aim_ref, bdre, bdim, ctre, ctim, a1, a2):
    for m in (bdre, bdim, ctre, ctim):
        m[...] = jnp.zeros_like(m)
    for grp in range(N_GROUP):
        k, g = divmod(grp, CH_G)
        rows = slice(g * GROUP_W, (g + 1) * GROUP_W)
        cols = slice(g * STATE, (g + 1) * STATE)
        bdre[k, rows, cols] = btre_ref[grp].astype(BF16)
        bdim[k, rows, cols] = btim_ref[grp].astype(BF16)
        ctre[k, rows, cols] = cre_ref[grp].astype(BF16)
        ctim[k, rows, cols] = cim_ref[grp].astype(BF16)
        ar = are_ref[grp:grp + 1, :]
        ai = aim_ref[grp:grp + 1, :]
        a1[k:k + 1, cols] = ar
        a1[N_CHUNK + k:N_CHUNK + k + 1, cols] = ar
        a2[k:k + 1, cols] = -ai
        a2[N_CHUNK + k:N_CHUNK + k + 1, cols] = ai


SCAN_UNROLL = 16


def _panels(tile):
    return [tile[:, p * LANES:(p + 1) * LANES] for p in range(N_PANEL)]


def _rows_load(ref, row):
    return jnp.concatenate([ref[p, pl.ds(row, TIME_BLK, stride=SUBLANES), :] for p in range(N_PANEL)], axis=1)


def _rows_store(ref, row, val):
    for p in range(N_PANEL):
        ref[p, pl.ds(row, TIME_BLK, stride=SUBLANES), :] = val[:, p * LANES:(p + 1) * LANES]


def _s5_scan_fwd(layer, proj, bbt_re, bbt_im, c_re, c_im, abar_re, abar_im, d_skip, carry=None):
    def body(u_ref, btre_ref, btim_ref, cre_ref, cim_ref, are_ref, aim_ref, d_ref, s_ref, y_ref,
             bdre, bdim, ctre, ctim, a1, a2, state):
        @pl.when(pl.program_id(0) == 0)
        def _():
            _s5_layer_fill(btre_ref, btim_ref, cre_ref, cim_ref, are_ref, aim_ref, bdre, bdim, ctre, ctim, a1, a2)
            state[...] = jnp.zeros_like(state)

        for k in range(N_CHUNK):
            ub = u_ref[:, k * CH_W:(k + 1) * CH_W].astype(BF16)
            _rows_store(s_ref, k, _dot(ub, bdre[k]))
            _rows_store(s_ref, N_CHUNK + k, _dot(ub, bdim[k]))
        m1 = _panels(a1[...])
        m2 = _panels(a2[...])

        def steps(n, tile):
            for r in range(SCAN_UNROLL):
                rows = pl.ds(pl.multiple_of((n * SCAN_UNROLL + r) * 8, 8), 8)
                tile = [m1[p] * tile[p] + m2[p] * pltpu.roll(tile[p], N_CHUNK, 0) + s_ref[p, rows, :]
                        for p in range(N_PANEL)]
                for p in range(N_PANEL):
                    s_ref[p, rows, :] = tile[p]
            return tile

        tile = lax.fori_loop(0, TIME_BLK // SCAN_UNROLL, steps, _panels(state[...]))
        state[...] = jnp.concatenate(tile, axis=1)
        d = d_ref[layer:layer + 1, :]
        for k in range(N_CHUNK):
            cols = slice(k * CH_W, (k + 1) * CH_W)
            y = (_dot_nt(_rows_load(s_ref, k).astype(BF16), ctre[k])
                 - _dot_nt(_rows_load(s_ref, N_CHUNK + k).astype(BF16), ctim[k]))
            y_ref[:, cols] = y + d[:, cols] * u_ref[:, cols]

    return _pcall(
        body, name=f"s5_fwd_l{layer}",
        out_shape=(SDS(STATE_SHAPE, F32), SDS((SEQ, WIDTH), F32)),
        grid=(N_TBLK,),
        in_specs=[pl.BlockSpec((TIME_BLK, WIDTH), lambda i: (i, 0))] + _s5_layer_specs(layer),
        out_specs=(pl.BlockSpec((N_PANEL, TIME_BLK * SUBLANES, LANES), lambda i: (0, i, 0)),
                   pl.BlockSpec((TIME_BLK, WIDTH), lambda i: (i, 0))),
        scratch_shapes=_s5_layer_scratch() + [pltpu.VMEM((8, CH_S), F32)],
        args=[proj, bbt_re, bbt_im, c_re, c_im, abar_re, abar_im, d_skip], sem=("arbitrary",), carry=carry)


def _s5_scan_bwd(layer, dy0, proj, states, bbt_re, bbt_im, c_re, c_im, abar_re, abar_im, d_skip, dproj,
                 carry=None):
    def body(dy_ref, u_ref, s_ref, sprev_ref, btre_ref, btim_ref, cre_ref, cim_ref, are_ref, aim_ref, d_ref, _,
             du_ref, gbre_ref, gbim_ref, gcre_ref, gcim_ref, gare_ref, gaim_ref, gd_ref,
             lam_ref, bdre, bdim, ctre, ctim, a1, a2, state, acc1, acc2, gbre, gbim, gcre, gcim, gd):
        step_id = pl.program_id(0)

        @pl.when(step_id == 0)
        def _():
            _s5_layer_fill(btre_ref, btim_ref, cre_ref, cim_ref, are_ref, aim_ref, bdre, bdim, ctre, ctim, a1, a2)
            for r in (state, acc1, acc2, gbre, gbim, gcre, gcim, gd):
                r[...] = jnp.zeros_like(r)

        for k in range(N_CHUNK):
            dyb = dy_ref[:, k * CH_W:(k + 1) * CH_W].astype(BF16)
            _rows_store(lam_ref, k, _dot(dyb, ctre[k]))
            _rows_store(lam_ref, N_CHUNK + k, -_dot(dyb, ctim[k]))
            gcre[k] += _dot_tn(dyb, _rows_load(s_ref, k).astype(BF16))
            gcim[k] -= _dot_tn(dyb, _rows_load(s_ref, N_CHUNK + k).astype(BF16))

        m1 = _panels(a1[...])
        m2 = _panels(-a2[...])
        has_before = (step_id < N_TBLK - 1).astype(F32)

        def one(t8, c, first_token):
            tile, swapped, p1, p2 = c
            rows = pl.ds(t8, 8)
            tile = [m1[p] * tile[p] + m2[p] * swapped[p] + lam_ref[p, rows, :] for p in range(N_PANEL)]
            swapped = [pltpu.roll(tile[p], N_CHUNK, 0) for p in range(N_PANEL)]
            for p in range(N_PANEL):
                lam_ref[p, rows, :] = tile[p]
            if first_token:
                before = [sprev_ref[p] * has_before for p in range(N_PANEL)]
            else:
                before = [s_ref[p, pl.ds(t8 - 8, 8), :] for p in range(N_PANEL)]
            p1 = [p1[p] + tile[p] * before[p] for p in range(N_PANEL)]
            p2 = [p2[p] + swapped[p] * before[p] for p in range(N_PANEL)]
            return tile, swapped, p1, p2

        def steps(n, c):
            for r in range(SCAN_UNROLL):
                t8 = pl.multiple_of((TIME_BLK - 1 - (n * SCAN_UNROLL + r)) * 8, 8)
                c = one(t8, c, False)
            return c

        tile0 = _panels(state[...])
        c = (tile0, [pltpu.roll(t, N_CHUNK, 0) for t in tile0], _panels(acc1[...]), _panels(acc2[...]))
        c = lax.fori_loop(0, TIME_BLK // SCAN_UNROLL - 1, steps, c)
        for r in range(SCAN_UNROLL - 1, -1, -1):
            c = one(r * 8, c, r == 0)
        state[...] = jnp.concatenate(c[0], axis=1)
        acc1[...] = jnp.concatenate(c[2], axis=1)
        acc2[...] = jnp.concatenate(c[3], axis=1)

        d = d_ref[layer:layer + 1, :]
        for k in range(N_CHUNK):
            cols = slice(k * CH_W, (k + 1) * CH_W)
            lrb = _rows_load(lam_ref, k).astype(BF16)
            lib = _rows_load(lam_ref, N_CHUNK + k).astype(BF16)
            u = u_ref[:, cols]
            ub = u.astype(BF16)
            dy = dy_ref[:, cols]
            du = dy * d[:, cols] + _dot_nt(lrb, bdre[k]) + _dot_nt(lib, bdim[k])
            du_ref[:, cols] = du.astype(BF16)
            gbre[k] += _dot_tn(ub, lrb)
            gbim[k] += _dot_tn(ub, lib)
        gd[...] += jnp.sum(dy_ref[...] * u_ref[...], axis=0, keepdims=True)

        @pl.when(step_id == N_TBLK - 1)
        def _():
            gd_ref[...] = gd[...]
            ga_re = acc1[0:N_CHUNK, :] + acc1[N_CHUNK:, :]
            ga_im = acc2[0:N_CHUNK, :] - acc2[N_CHUNK:, :]
            for grp in range(N_GROUP):
                k, g = divmod(grp, CH_G)
                rows = slice(g * GROUP_W, (g + 1) * GROUP_W)
                cols = slice(g * STATE, (g + 1) * STATE)
                gcre_ref[grp] = gcre[k, rows, cols]
                gcim_ref[grp] = gcim[k, rows, cols]
                gbre_ref[grp] = gbre[k, rows, cols]
                gbim_ref[grp] = gbim[k, rows, cols]
                gare_ref[grp:grp + 1, :] = ga_re[k:k + 1, cols]
                gaim_ref[grp:grp + 1, :] = ga_im[k:k + 1, cols]

    back = lambda i: N_TBLK - 1 - i
    tok = lambda: pl.BlockSpec((TIME_BLK, WIDTH), lambda i: (back(i), 0))
    mat = lambda: _const((N_GROUP, GROUP_W, STATE))
    acc_mat = pltpu.VMEM((N_CHUNK, CH_W, CH_S), F32)
    return _pcall(
        body, name=f"s5_bwd_l{layer}",
        out_shape=(SDS((SEQ, N_IN), BF16), SDS((N_GROUP, GROUP_W, STATE), F32), SDS((N_GROUP, GROUP_W, STATE), F32),
                   SDS((N_GROUP, GROUP_W, STATE), F32), SDS((N_GROUP, GROUP_W, STATE), F32),
                   SDS((N_GROUP, STATE), F32), SDS((N_GROUP, STATE), F32), SDS((1, WIDTH), F32)),
        grid=(N_TBLK,),
        in_specs=[tok(), tok(),
                  pl.BlockSpec((N_PANEL, TIME_BLK * SUBLANES, LANES), lambda i: (0, back(i), 0)),
                  pl.BlockSpec((N_PANEL, SUBLANES, LANES), lambda i: (0, jnp.maximum(back(i) * TIME_BLK - 1, 0), 0))]
        + _s5_layer_specs(layer) + [ANY],
        out_specs=(tok(), mat(), mat(), mat(), mat(), _const((N_GROUP, STATE)), _const((N_GROUP, STATE)),
                   _const((1, WIDTH))),
        scratch_shapes=[pltpu.VMEM((N_PANEL, TIME_BLK * SUBLANES, LANES), F32)] + _s5_layer_scratch()
        + [pltpu.VMEM((8, CH_S), F32)] * 3 + [acc_mat] * 4 + [pltpu.VMEM((1, WIDTH), F32)],
        args=[dy0, proj, states, states, bbt_re, bbt_im, c_re, c_im, abar_re, abar_im, d_skip, dproj],
        aliases={11: 0}, sem=("arbitrary",), limit=VMEM_LIMIT_BIG, carry=carry)


def _pool_counts(win):
    t = lax.broadcasted_iota(jnp.int32, (SEQ, POOL_GROUP), 0)
    return t, jnp.minimum(t + 1, win).astype(F32)


def _pool_fwd(layer, proj):
    def body(u_ref, o_ref):
        for gi, win in enumerate(POOL_WINDOWS):
            cols = slice(gi * POOL_GROUP, (gi + 1) * POOL_GROUP)
            u = u_ref[:, cols]
            t, count = _pool_counts(win)
            acc = u
            k = 1
            while k < win:
                acc = acc + jnp.where(t >= k, pltpu.roll(acc, k, 0), 0.0)
                k *= 2
            o_ref[:, cols] = acc / count - u

    return pl.pallas_call(
        body, name=f"pool_fwd_l{layer}",
        out_shape=SDS((SEQ, WIDTH), F32),
        grid=(1,),
        in_specs=[pl.BlockSpec((SEQ, WIDTH), lambda i: (0, 2))],
        out_specs=pl.BlockSpec((SEQ, WIDTH), lambda i: (0, 0)),
        compiler_params=_cp(("arbitrary",)),
    )(proj)


def _gelu_parts(y0):
    t = jnp.tanh(GELU_C * (y0 + GELU_A * (y0 * y0 * y0)))
    return t, 0.5 * y0 * (1.0 + t)


def _mix_forward(layer, p_ref, y0_ref, pooled_ref, wglu_ref, bglu_ref, pw_ref, scale_ref, wa_ref, wb_ref):
    za = p_ref[:, WIDTH:2 * WIDTH]
    zb = p_ref[:, 3 * WIDTH:4 * WIDTH]
    ga = p_ref[:, 4 * WIDTH:4 * WIDTH + D_MODEL]
    gb = p_ref[:, 4 * WIDTH + D_MODEL:]
    y0 = y0_ref[...]
    t, y1 = _gelu_parts(y0)
    y1b = y1.astype(BF16)
    q = _dot(y1b, wglu_ref[...].reshape(WIDTH, WIDTH)) + bglu_ref[layer:layer + 1, :]
    sq = _sig(q)
    y2 = y1 * sq
    sza = _sig(za)
    silu_za = za * sza
    ya = y2 * silu_za
    pooled = pooled_ref[...]
    mixed = jnp.concatenate(
        [_dot(pooled[:, g * POOL_GROUP:(g + 1) * POOL_GROUP].astype(BF16), pw_ref[g].astype(BF16))
         for g in range(len(POOL_WINDOWS))], axis=1)
    szb = _sig(zb)
    silu_zb = zb * szb
    scale = scale_ref[layer:layer + 1, :]
    ms = mixed * scale
    yb = ms * silu_zb
    yab = ya.astype(BF16)
    ybb = yb.astype(BF16)
    ma = _dot(yab, wa_ref[...])
    mb = _dot(ybb, wb_ref[...])
    sga = _sig(ga)
    sgb = _sig(gb)
    merged = sga * ma + sgb * mb
    return dict(za=za, zb=zb, y0=y0, t=t, y1=y1, y1b=y1b, sq=sq, y2=y2, sza=sza, silu_za=silu_za,
                pooled=pooled, mixed=mixed, szb=szb, silu_zb=silu_zb, scale=scale, ms=ms, yab=yab, ybb=ybb,
                ma=ma, mb=mb, sga=sga, sgb=sgb, merged=merged)


def _mix_weight_specs(layer):
    return [_const((N_DEV, WIDTH // N_DEV, WIDTH)),
            _const((DEPTH, WIDTH)),
            pl.BlockSpec((None, 4, POOL_GROUP, POOL_GROUP), lambda i: (layer, 0, 0, 0)),
            _const((DEPTH, WIDTH)),
            _const((WIDTH, D_MODEL)),
            _const((WIDTH, D_MODEL)),
            _const((N_DEV, D_MODEL // N_DEV, D_MODEL))]


def _loss_head(x, t_ref, g_ref, dx_ref, loss_ref, gg_ref):
    @pl.when(pl.program_id(0) == 0)
    def _():
        loss_ref[...] = jnp.zeros_like(loss_ref)
        gg_ref[...] = jnp.zeros_like(gg_ref)

    g = g_ref[...]
    rs, xn = _rms(x)
    err = xn * g - t_ref[...]
    loss_ref[...] += 0.5 * jnp.sum(jnp.mean(err * err, axis=-1, keepdims=True), axis=0, keepdims=True)
    dy = err * (1.0 / D_MODEL)
    gg_ref[...] += jnp.sum(dy * xn, axis=0, keepdims=True)
    dxn = dy * g
    dx_ref[...] = rs * (dxn - xn * jnp.mean(dxn * xn, axis=-1, keepdims=True))


def _mix_fwd(layer, x, proj, y0, pooled, wg_glu, b_glu, pool_w, pool_scale, wg_a, wg_b, wg_out, carry=None,
             head=None):
    def body(x_ref, p_ref, y0_ref, pooled_ref, wglu_ref, bglu_ref, pw_ref, scale_ref, wa_ref, wb_ref,
             wout_ref, *rest):
        f = _mix_forward(layer, p_ref, y0_ref, pooled_ref, wglu_ref, bglu_ref, pw_ref, scale_ref, wa_ref, wb_ref)
        wout = wout_ref[...].reshape(D_MODEL, D_MODEL)
        x_next = x_ref[...] + _dot(f["merged"].astype(BF16), wout)
        if head is None:
            rest[0][...] = x_next
        else:
            _loss_head(x_next, *rest)

    tile = lambda: pl.BlockSpec((TILE_M, D_MODEL), lambda i: (i, 0))
    if head is None:
        extra, out_shape, out_specs = [], [SDS((SEQ, D_MODEL), F32)], [tile()]
    else:
        extra = list(head)
        out_shape = [SDS((SEQ, D_MODEL), F32), SDS((1, 1), F32), SDS((1, D_MODEL), F32)]
        out_specs = [tile(), _const((1, 1)), _const((1, D_MODEL))]
    return _pcall(
        body, name=f"mix_fwd_l{layer}",
        out_shape=out_shape,
        grid=(SEQ // TILE_M,),
        in_specs=[tile(),
                  pl.BlockSpec((TILE_M, N_IN), lambda i: (i, 0)),
                  pl.BlockSpec((TILE_M, WIDTH), lambda i: (i, 0)),
                  pl.BlockSpec((TILE_M, WIDTH), lambda i: (i, 0))] + _mix_weight_specs(layer)
        + ([tile(), _const((1, D_MODEL))] if head else []),
        out_specs=out_specs,
        args=[x, proj, y0, pooled, wg_glu, b_glu, pool_w, pool_scale, wg_a, wg_b, wg_out] + extra,
        sem=("parallel",) if head is None else ("arbitrary",), carry=carry)


def _big_shapes():
    return dict(w_out=(DEPTH, N_DEV, D_MODEL // N_DEV, D_MODEL), w_branch_a=(DEPTH, N_DEV, WIDTH, D_MODEL // N_DEV),
                w_branch_b=(DEPTH, N_DEV, WIDTH, D_MODEL // N_DEV), ssm_w_glu=(DEPTH, N_DEV, WIDTH // N_DEV, WIDTH),
                w_in=(DEPTH, N_DEV, D_MODEL, WIDTH))


def _mix_bwd(layer, dx_next, proj, y0, pooled, wg_glu, b_glu, pool_w, pool_scale, wg_a, wg_b, wg_out, prev,
             carry=None):
    n_k = N_DEV
    n_prev = 0 if prev is None else len(prev)

    def body(*refs):
        (dx_ref, p_ref, y0_ref, pooled_ref, wglu_ref, bglu_ref, pw_ref, scale_ref, wa_ref, wb_ref,
         wout_ref) = refs[:11]
        (dproj_ref, dy0_ref, dpooled_ref, gwout_ref, gwa_ref, gwb_ref, gwglu_ref, gpw_ref,
         gscale_ref, gbglu_ref) = refs[11 + n_prev:]

        @pl.when(pl.program_id(0) == 0)
        def _():
            for r in (gwout_ref, gwa_ref, gwb_ref, gwglu_ref, gpw_ref, gscale_ref, gbglu_ref):
                r[...] = jnp.zeros_like(r)

        f = _mix_forward(layer, p_ref, y0_ref, pooled_ref, wglu_ref, bglu_ref, pw_ref, scale_ref, wa_ref, wb_ref)
        wglu = wglu_ref[...].reshape(WIDTH, WIDTH)
        wout = wout_ref[...].reshape(D_MODEL, D_MODEL)
        blk = D_MODEL // n_k
        dxb = dx_ref[...].astype(BF16)
        dmerged = _dot_nt(dxb, wout)
        gwout = _dot_tn(f["merged"].astype(BF16), dxb)
        for k in range(n_k):
            gwout_ref[_slot(k)] += gwout[k * blk:(k + 1) * blk, :]
        dma = dmerged * f["sga"]
        dmb = dmerged * f["sgb"]
        dga = dmerged * f["ma"] * f["sga"] * (1.0 - f["sga"])
        dgb = dmerged * f["mb"] * f["sgb"] * (1.0 - f["sgb"])
        dmab = dma.astype(BF16)
        dmbb = dmb.astype(BF16)
        dya = _dot_nt(dmab, wa_ref[...])
        dyb = _dot_nt(dmbb, wb_ref[...])
        gwa = _dot_tn(f["yab"], dmab)
        gwb = _dot_tn(f["ybb"], dmbb)
        for k in range(n_k):
            gwa_ref[_slot(k)] += gwa[:, k * blk:(k + 1) * blk]
            gwb_ref[_slot(k)] += gwb[:, k * blk:(k + 1) * blk]
        zb, szb = f["zb"], f["szb"]
        dzb = dyb * f["ms"] * (szb * (1.0 + zb * (1.0 - szb)))
        dms = dyb * f["silu_zb"]
        gscale_ref[...] += jnp.sum(dms * f["mixed"], axis=0, keepdims=True)
        dmixed = (dms * f["scale"]).astype(BF16)
        pooled = f["pooled"]
        for g in range(len(POOL_WINDOWS)):
            cols = slice(g * POOL_GROUP, (g + 1) * POOL_GROUP)
            dpooled_ref[:, cols] = _dot_nt(dmixed[:, cols], pw_ref[g].astype(BF16))
            gpw_ref[g] += _dot_tn(pooled[:, cols].astype(BF16), dmixed[:, cols])
        za, sza = f["za"], f["sza"]
        dza = dya * f["y2"] * (sza * (1.0 + za * (1.0 - sza)))
        dy2 = dya * f["silu_za"]
        sq = f["sq"]
        dq = dy2 * f["y1"] * sq * (1.0 - sq)
        dqb = dq.astype(BF16)
        dy1 = dy2 * sq + _dot_nt(dqb, wglu)
        gwglu = _dot_tn(f["y1b"], dqb)
        rblk = WIDTH // n_k
        for k in range(n_k):
            gwglu_ref[_slot(k)] += gwglu[k * rblk:(k + 1) * rblk, :]
        gbglu_ref[...] += jnp.sum(dq, axis=0, keepdims=True)
        y0, t = f["y0"], f["t"]
        dgelu = 0.5 * (1.0 + t) + 0.5 * y0 * (1.0 - t * t) * (GELU_C * (1.0 + 3.0 * GELU_A * y0 * y0))
        dy0_ref[...] = dy1 * dgelu
        zeros = jnp.zeros((TILE_M, WIDTH), BF16)
        dproj_ref[:, 0:WIDTH] = zeros
        dproj_ref[:, WIDTH:2 * WIDTH] = dza.astype(BF16)
        dproj_ref[:, 2 * WIDTH:3 * WIDTH] = zeros
        dproj_ref[:, 3 * WIDTH:4 * WIDTH] = dzb.astype(BF16)
        dproj_ref[:, 4 * WIDTH:4 * WIDTH + D_MODEL] = dga.astype(BF16)
        dproj_ref[:, 4 * WIDTH + D_MODEL:] = dgb.astype(BF16)

    tile = lambda w: pl.BlockSpec((TILE_M, w), lambda i: (i, 0))
    shapes = _big_shapes()
    big = ["w_out", "w_branch_a", "w_branch_b", "ssm_w_glu"]
    slab = lambda n: pl.BlockSpec((None,) + shapes[n][1:], lambda i: (layer, 0, 0, 0))
    args = [dx_next, proj, y0, pooled, wg_glu, b_glu, pool_w, pool_scale, wg_a, wg_b, wg_out]
    return _pcall(
        body, name=f"mix_bwd_l{layer}",
        out_shape=(SDS((SEQ, N_IN), BF16), SDS((SEQ, WIDTH), F32), SDS((SEQ, WIDTH), F32))
        + tuple(SDS(shapes[n], F32) for n in big)
        + (SDS((4, POOL_GROUP, POOL_GROUP), F32), SDS((1, WIDTH), F32), SDS((1, WIDTH), F32)),
        grid=(SEQ // TILE_M,),
        in_specs=[tile(D_MODEL), tile(N_IN), tile(WIDTH), tile(WIDTH)] + _mix_weight_specs(layer) + [ANY] * n_prev,
        out_specs=(tile(N_IN), tile(WIDTH), tile(WIDTH)) + tuple(slab(n) for n in big)
        + (_const((4, POOL_GROUP, POOL_GROUP)), _const((1, WIDTH)), _const((1, WIDTH))),
        args=args + list(prev or ()),
        aliases={len(args) + i: 3 + i for i in range(n_prev)},
        sem=("arbitrary",), limit=VMEM_LIMIT_BIG, carry=carry)


def _pool_bwd(layer, dpooled, dproj):
    def body(dp_ref, _, o_ref):
        for gi, win in enumerate(POOL_WINDOWS):
            cols = slice(gi * POOL_GROUP, (gi + 1) * POOL_GROUP)
            dp = dp_ref[:, cols]
            t, count = _pool_counts(win)
            e = dp / count
            acc = e
            k = 1
            while k < win:
                acc = acc + jnp.where(t < SEQ - k, pltpu.roll(acc, SEQ - k, 0), 0.0)
                k *= 2
            o_ref[:, cols] = (acc - dp).astype(BF16)

    return pl.pallas_call(
        body, name=f"pool_bwd_l{layer}",
        out_shape=SDS((SEQ, N_IN), BF16),
        grid=(1,),
        in_specs=[pl.BlockSpec((SEQ, WIDTH), lambda i: (0, 0)), ANY],
        out_specs=pl.BlockSpec((SEQ, WIDTH), lambda i: (0, 2)),
        input_output_aliases={1: 0},
        compiler_params=_cp(("arbitrary",)),
    )(dpooled, dproj)


def _proj_wgrad(layer, x, norm_g, dproj, prev, carry=None):
    tm = 1024
    n_prev = 0 if prev is None else 1

    def body(*refs):
        x_ref, g_ref, dp_ref = refs[:3]
        gw_ref, gb_ref, ht_ref = refs[3 + n_prev:]
        n, t = pl.program_id(0), pl.program_id(1)

        @pl.when(t == 0)
        def _():
            gw_ref[...] = jnp.zeros_like(gw_ref)
            gb_ref[...] = jnp.zeros_like(gb_ref)

        @pl.when(n == 0)
        def _():
            _, xn = _rms(x_ref[...])
            ht_ref[t] = (xn * g_ref[layer:layer + 1, :]).T.astype(BF16)

        dp = dp_ref[...]
        gw_ref[...] += _dot(ht_ref[t], dp)
        gb_ref[...] += jnp.sum(dp.astype(F32), axis=0, keepdims=True)

    return _pcall(
        body, name=f"proj_wgrad_l{layer}",
        out_shape=(SDS(_big_shapes()["w_in"], F32), SDS((1, N_IN), F32)),
        grid=(N_DEV, SEQ // tm),
        in_specs=[pl.BlockSpec((tm, D_MODEL), lambda n, t: (jnp.where(n == 0, t, 0), 0)),
                  _const((DEPTH, D_MODEL)),
                  pl.BlockSpec((tm, WIDTH), lambda n, t: (t, n))] + [ANY] * n_prev,
        out_specs=(pl.BlockSpec((None, None, D_MODEL, WIDTH), lambda n, t: (layer, _slot(n), 0, 0)),
                   pl.BlockSpec((1, WIDTH), lambda n, t: (0, n))),
        scratch_shapes=[pltpu.VMEM((SEQ // tm, D_MODEL, tm), BF16)],
        args=[x, norm_g, dproj] + ([prev] if n_prev else []),
        aliases={3: 0} if n_prev else {}, sem=("arbitrary", "arbitrary"), carry=carry)


def _proj_dgrad(layer, dx_next, x, norm_g, dproj, wg_in, carry=None):
    n_w = len(wg_in)

    def body(dxn_ref, x_ref, g_ref, dp_ref, *refs):
        w_refs, (dx_ref, gg_ref) = refs[:n_w], refs[n_w:]

        @pl.when(pl.program_id(0) == 0)
        def _():
            gg_ref[...] = jnp.zeros_like(gg_ref)

        parts = []
        for w_ref in w_refs:
            part = jnp.zeros((TILE_M, w_ref.shape[1]), F32)
            for k in range(N_DEV):
                part = part + _dot_nt(dp_ref[:, k * WIDTH:(k + 1) * WIDTH], w_ref[k])
            parts.append(part)
        dh = parts[0] if n_w == 1 else jnp.concatenate(parts, axis=1)
        rs, xn = _rms(x_ref[...])
        gg_ref[...] += jnp.sum(dh * xn, axis=0, keepdims=True)
        dxn = dh * g_ref[layer:layer + 1, :]
        dx_ref[...] = dxn_ref[...] + rs * (dxn - xn * jnp.mean(dxn * xn, axis=-1, keepdims=True))

    return _pcall(
        body, name=f"proj_dgrad_l{layer}",
        out_shape=(SDS((SEQ, D_MODEL), F32), SDS((1, D_MODEL), F32)),
        grid=(SEQ // TILE_M,),
        in_specs=[pl.BlockSpec((TILE_M, D_MODEL), lambda i: (i, 0)),
                  pl.BlockSpec((TILE_M, D_MODEL), lambda i: (i, 0)),
                  _const((DEPTH, D_MODEL)),
                  pl.BlockSpec((TILE_M, N_IN), lambda i: (i, 0))] + [_const(w.shape) for w in wg_in],
        out_specs=(pl.BlockSpec((TILE_M, D_MODEL), lambda i: (i, 0)), _const((1, D_MODEL))),
        args=[dx_next, x, norm_g, dproj, *wg_in], sem=("arbitrary",), carry=carry)


def _my_place():
    return lax.axis_index("x"), lax.axis_index("y"), lax.axis_index("c")


def _gather_plan(shards, layer, by_columns=(), rows_of=None):
    n = len(shards)

    def parts(ins, outs, sems):
        send_sems, recv_sems, local_sems = sems
        x, y, c = _my_place()
        chips = [(1 - x, y), (x, 1 - y), (1 - x, 1 - y)]

        def source(t):
            return ins[t].at[layer] if rows_of is None else ins[t].at[layer, pl.ds(*rows_of)]

        def rows(t, place):
            px, py, pc = place
            index = 4 * px + 2 * py + pc
            if t in by_columns:
                width = shards[t].shape[2]
                return outs[t].at[:, pl.ds(pl.multiple_of(index * width, LANES), width)]
            return outs[t].at[index]

        def copy(t, k, block, to, from_src=False):
            return pltpu.make_async_remote_copy(
                src_ref=source(t) if from_src else rows(t, block), dst_ref=rows(t, block),
                send_sem=send_sems.at[7 * t + k], recv_sem=recv_sems.at[7 * t + k], device_id=to,
                device_id_type=MESH)

        def mine(t):
            return pltpu.make_async_copy(source(t), rows(t, (x, y, c)), local_sems.at[t])

        return (x, y, c), chips, copy, mine

    def start(ins, outs, sems):
        me, chips, copy, mine = parts(ins, outs, sems)
        x, y, c = me
        for t in range(n):
            mine(t).start()
            copy(t, 0, me, (x, y, 1 - c), from_src=True).start()
            for j, chip in enumerate(chips):
                copy(t, 1 + j, me, (*chip, c), from_src=True).start()

    def relay(ins, outs, sems):
        me, chips, copy, mine = parts(ins, outs, sems)
        x, y, c = me
        for t in range(n):
            for j, chip in enumerate(chips):
                copy(t, 1 + j, (*chip, c), me).wait_recv()
                copy(t, 4 + j, (*chip, c), (x, y, 1 - c)).start()

    def finish(ins, outs, sems):
        me, chips, copy, mine = parts(ins, outs, sems)
        x, y, c = me
        sibling = (x, y, 1 - c)
        for t in range(n):
            copy(t, 0, sibling, me).wait_recv()
            for j, chip in enumerate(chips):
                copy(t, 4 + j, (*chip, 1 - c), me).wait_recv()
            for k in range(7):
                copy(t, k, me, sibling, from_src=k < 4).wait_send()
            mine(t).wait()

    n_rows = lambda a: a.shape[1] if rows_of is None else rows_of[1]
    out_shape = [SDS((a.shape[1], N_DEV * a.shape[2]) if t in by_columns else (N_DEV, n_rows(a), a.shape[2]), a.dtype)
                 for t, a in enumerate(shards)]
    sems = [pltpu.SemaphoreType.DMA((7 * n,)), pltpu.SemaphoreType.DMA((7 * n,)), pltpu.SemaphoreType.DMA((n,))]
    return _Carried(shards, out_shape, sems, start, finish, relay)


class _Carried:
    def __init__(self, ins, out_shape, sems, start, finish, relay=None):
        self.ins, self.out_shape, self.sems = list(ins), list(out_shape), list(sems)
        self.start, self.finish = start, finish
        self.relay = relay or (lambda ins, outs, sems: None)


def _pcall(body, *, name, grid, in_specs, out_specs, out_shape, args, scratch_shapes=(), aliases=None,
           sem=None, limit=VMEM_LIMIT, carry=None):
    out_shape, out_specs, scratch_shapes = list(out_shape), list(out_specs), list(scratch_shapes)
    n_in, n_out, n_scr = len(args), len(out_shape), len(scratch_shapes)
    if carry is None:
        kern, c_ins, c_out, c_sems = body, [], [], []
    else:
        c_ins, c_out, c_sems = carry.ins, carry.out_shape, carry.sems
        ci, co = len(c_ins), len(c_out)
        steps = tuple(grid)

        def kern(*refs):
            o0 = n_in + ci
            s0 = o0 + n_out + co
            mine = refs[:n_in] + refs[o0:o0 + n_out] + refs[s0:s0 + n_scr]
            theirs = (refs[n_in:o0], refs[o0 + n_out:s0], refs[s0 + n_scr:])
            first = pl.program_id(0) == 0
            last = pl.program_id(0) == steps[0] - 1
            for a in range(1, len(steps)):
                first = jnp.logical_and(first, pl.program_id(a) == 0)
                last = jnp.logical_and(last, pl.program_id(a) == steps[a] - 1)

            @pl.when(first)
            def _():
                carry.start(*theirs)

            @pl.when(last)
            def _():
                carry.relay(*theirs)

            body(*mine)

            @pl.when(last)
            def _():
                carry.finish(*theirs)

        sem = ("arbitrary",) * len(steps)
    res = pl.pallas_call(
        kern, name=name, grid=tuple(grid),
        in_specs=list(in_specs) + [ANY] * len(c_ins),
        out_specs=tuple(out_specs + [ANY] * len(c_out)),
        out_shape=tuple(out_shape + c_out),
        scratch_shapes=scratch_shapes + c_sems,
        input_output_aliases=aliases or {},
        compiler_params=_cp(sem, limit),
    )(*args, *c_ins)
    return res[:n_out], res[n_out:]


def _run_carried(name, carry):
    ci, co = len(carry.ins), len(carry.out_shape)

    def body(*refs):
        parts = (refs[:ci], refs[ci:ci + co], refs[ci + co:])
        carry.start(*parts)
        carry.relay(*parts)
        carry.finish(*parts)

    return pl.pallas_call(
        body, name=name, out_shape=tuple(carry.out_shape),
        in_specs=[ANY] * ci, out_specs=tuple([ANY] * co), scratch_shapes=carry.sems,
    )(*carry.ins)


def _sibling_plan(big, small):
    n = len(big)
    n_copies = 4 * n + len(small)

    def copies(ins, outs, sems):
        send_sems, recv_sems = sems
        x, y, c = _my_place()
        pairs = []
        for t, (_, layer) in enumerate(big):
            for s in range(4):
                pairs.append((ins[t].at[layer, pl.ds(4 * (1 - c) + s, 1)], outs[t].at[pl.ds(s, 1)]))
        pairs += list(zip(ins[n:], outs[n:]))
        return [pltpu.make_async_remote_copy(
            src_ref=src, dst_ref=dst, send_sem=send_sems.at[k], recv_sem=recv_sems.at[k],
            device_id=(x, y, 1 - c), device_id_type=MESH) for k, (src, dst) in enumerate(pairs)]

    def start(ins, outs, sems):
        for cp in copies(ins, outs, sems):
            cp.start()

    def finish(ins, outs, sems):
        for cp in copies(ins, outs, sems):
            cp.wait()

    out_shape = [SDS((4,) + a.shape[2:], a.dtype) for a, _ in big] + [SDS(a.shape, a.dtype) for a in small]
    sems = [pltpu.SemaphoreType.DMA((n_copies,)), pltpu.SemaphoreType.DMA((n_copies,))]
    return _Carried([a for a, _ in big] + list(small), out_shape, sems, start, finish)


def _chips_plan(big, small):
    n, n_small = len(big), len(small)
    max_rows = 512
    parts = [max(1, a.shape[1] // max_rows) for a in big]
    n_copies = 3 * (sum(parts) + n_small)

    def copies(ins, outs, sems, landing):
        send_sems, recv_sems, local_sems = sems
        x, y, c = _my_place()
        my_chip = 2 * x + y
        chips = [(1 - x, y), (x, 1 - y), (1 - x, 1 - y)]
        remote, local = [], []
        for chip in chips:
            to = 2 * chip[0] + chip[1]
            slot = to if landing else my_chip
            pairs = []
            for t in range(n):
                rows_per = big[t].shape[1] // parts[t]
                for p in range(parts[t]):
                    rows = pl.ds(p * rows_per, rows_per)
                    pairs.append((ins[t].at[to, rows], outs[t].at[slot, rows]))
            pairs += [(ins[t], outs[t].at[slot]) for t in range(n, n + n_small)]
            for src, dst in pairs:
                k = len(remote)
                remote.append(pltpu.make_async_remote_copy(
                    src_ref=src, dst_ref=dst, send_sem=send_sems.at[k], recv_sem=recv_sems.at[k],
                    device_id=(*chip, c), device_id_type=MESH))
        for t in range(n):
            rows_per = big[t].shape[1] // parts[t]
            for p in range(parts[t]):
                rows = pl.ds(p * rows_per, rows_per)
                local.append(pltpu.make_async_copy(ins[t].at[my_chip, rows], outs[t].at[my_chip, rows],
                                                   local_sems.at[len(local)]))
        for t in range(n, n + n_small):
            local.append(pltpu.make_async_copy(ins[t], outs[t].at[my_chip], local_sems.at[len(local)]))
        return remote + local

    def start(ins, outs, sems):
        for cp in copies(ins, outs, sems, landing=False):
            cp.start()

    def finish(ins, outs, sems):
        for cp in copies(ins, outs, sems, landing=True):
            cp.wait()

    out_shape = [SDS(a.shape, a.dtype) for a in big] + [SDS((N_CHIP,) + a.shape, a.dtype) for a in small]
    sems = [pltpu.SemaphoreType.DMA((n_copies,)), pltpu.SemaphoreType.DMA((n_copies,)),
            pltpu.SemaphoreType.DMA((sum(parts) + n_small,))]
    return _Carried(list(big) + list(small), out_shape, sems, start, finish)


def _all_plan(small):
    n = len(small)
    masks = [(m >> 2 & 1, m >> 1 & 1, m & 1) for m in range(1, N_DEV)]

    def copies(ins, outs, sems, landing):
        send_sems, recv_sems, local_sems = sems
        x, y, c = _my_place()
        me = 4 * x + 2 * y + c
        flip = lambda v, bit: 1 - v if bit else v
        remote = []
        for fx, fy, fc in masks:
            peer = (flip(x, fx), flip(y, fy), flip(c, fc))
            slot = 4 * peer[0] + 2 * peer[1] + peer[2] if landing else me
            for t in range(n):
                k = len(remote)
                remote.append(pltpu.make_async_remote_copy(
                    src_ref=ins[t], dst_ref=outs[t].at[slot], send_sem=send_sems.at[k], recv_sem=recv_sems.at[k],
                    device_id=peer, device_id_type=MESH))
        local = [pltpu.make_async_copy(ins[t], outs[t].at[me], local_sems.at[t]) for t in range(n)]
        return remote + local

    def start(ins, outs, sems):
        for cp in copies(ins, outs, sems, landing=False):
            cp.start()

    def finish(ins, outs, sems):
        for cp in copies(ins, outs, sems, landing=True):
            cp.wait()

    out_shape = [SDS((N_DEV,) + a.shape, a.dtype) for a in small]
    sems = [pltpu.SemaphoreType.DMA((7 * n,)), pltpu.SemaphoreType.DMA((7 * n,)), pltpu.SemaphoreType.DMA((n,))]
    return _Carried(list(small), out_shape, sems, start, finish)


def _join(*plans):
    plans = [p for p in plans if p is not None]
    if len(plans) <= 1:
        return plans[0] if plans else None

    def each(fn_name, ins, outs, sems):
        i = o = s = 0
        for p in plans:
            ni, no, ns = len(p.ins), len(p.out_shape), len(p.sems)
            getattr(p, fn_name)(ins[i:i + ni], outs[o:o + no], sems[s:s + ns])
            i, o, s = i + ni, o + no, s + ns

    return _Carried(sum((p.ins for p in plans), []), sum((p.out_shape for p in plans), []),
                    sum((p.sems for p in plans), []),
                    lambda i, o, s: each("start", i, o, s), lambda i, o, s: each("finish", i, o, s),
                    lambda i, o, s: each("relay", i, o, s))


def _row_block(rows, most=256):
    return min(rows, most)


def _add_own(tag, core, gs, layer, gots):
    n = len(gs)

    def body(core_ref, *refs):
        for a_ref, b_ref, o_ref in zip(refs[:n], refs[n:2 * n], refs[2 * n:]):
            o_ref[...] = (a_ref[...] + b_ref[...]).astype(o_ref.dtype)

    mine = lambda a: pl.BlockSpec((None, None) + a.shape[1:], lambda s, core: (layer, 4 * core[0] + s, 0, 0))
    theirs = lambda a: pl.BlockSpec((None,) + a.shape[1:], lambda s, core: (s, 0, 0))
    return pl.pallas_call(
        body, name=f"add_{tag}", out_shape=tuple(SDS(a.shape, BF16) for a in gots),
        grid_spec=pltpu.PrefetchScalarGridSpec(
            num_scalar_prefetch=1, grid=(4,),
            in_specs=[mine(a) for a in gots] + [theirs(a) for a in gots],
            out_specs=tuple(theirs(a) for a in gots)),
        compiler_params=_cp(("parallel",)),
    )(core, *gs, *gots)


def _add_lists(tag, own, got, grid=None, specs=None, dtype=F32):
    n = len(own)

    def body(*refs):
        for a, b, o in zip(refs[:n], refs[n:2 * n], refs[2 * n:]):
            o[...] = (a[...] + b[...]).astype(o.dtype)

    kw = {}
    if grid is not None:
        kw = dict(grid=grid, in_specs=list(specs) * 2, out_specs=tuple(specs),
                  compiler_params=_cp(("parallel",) * len(grid)))
    return pl.pallas_call(
        body, name=f"add_{tag}", out_shape=tuple(SDS(a.shape, dtype) for a in own), **kw)(*own, *got)


def _adamw_math(w, g, m, v):
    m = ADAM_B1 * m + (1.0 - ADAM_B1) * g
    v = ADAM_B2 * v + (1.0 - ADAM_B2) * (g * g)
    m_hat = m / (1.0 - ADAM_B1 ** ADAM_STEP)
    v_hat = v / (1.0 - ADAM_B2 ** ADAM_STEP)
    delta = -ADAM_LR * (m_hat / (jnp.sqrt(v_hat) + ADAM_EPS) + ADAM_WD * w)
    return delta, m, v


def _sum_slots_adamw(tag, slots, w, m, v):
    _, r, c = slots[0].shape
    rb = _row_block(r, most=512)

    def body(s0_ref, s1_ref, w_ref, m_ref, v_ref, g_ref, d_ref, nm_ref, nv_ref):
        first = pl.program_id(1) == 0
        g = _pair_sum([jnp.where(first, s0_ref[k], s1_ref[k]).astype(F32) for k in range(N_CHIP)])
        delta, nm, nv = _adamw_math(w_ref[...], g, m_ref[...], v_ref[...])
        g_ref[...] = g
        d_ref[...] = delta
        nm_ref[...] = nm
        nv_ref[...] = nv

    spec = pl.BlockSpec((None, rb, c), lambda j, l: (l, j, 0))
    sspec = pl.BlockSpec((N_CHIP, rb, c), lambda j, l: (0, j, 0))
    s = SDS((DEPTH, r, c), F32)
    return pl.pallas_call(
        body, name=f"adamw_{tag}", out_shape=(s, s, s, s),
        grid=(r // rb, DEPTH), in_specs=[sspec, sspec, spec, spec, spec], out_specs=(spec, spec, spec, spec),
        compiler_params=_cp(("parallel", "arbitrary")),
    )(*slots, w, m, v)


def _adamw_small(tag, entries, grid=None, sums=()):
    flat_in, in_specs, out_shape, out_specs, layout = [], [], [], [], []
    for slots, w, m, v, slot_spec, w_spec in entries:
        per_layer = isinstance(slots, (list, tuple))
        n_slot = len(slots) if per_layer else 1
        flat_in += (list(slots) if per_layer else [slots]) + [w, m, v]
        in_specs += [slot_spec] * n_slot + [w_spec] * 3
        out_shape += [SDS(w.shape, F32)] * 4
        out_specs += [w_spec] * 4
        layout.append((per_layer, n_slot))
    n_entry_in = len(flat_in)
    flat_in += list(sums)
    out_shape += [SDS(s.shape[1:], F32) for s in sums]
    n_in = len(flat_in)

    def body(*refs):
        for s_ref, o_ref in zip(refs[n_entry_in:n_in], refs[len(refs) - len(sums):]):
            o_ref[...] = _sum_slots(s_ref)
        i, o = 0, n_in
        for per_layer, n_slot in layout:
            s_refs = refs[i:i + n_slot]
            w_ref, m_ref, v_ref = refs[i + n_slot:i + n_slot + 3]
            outs = refs[o:o + 4]
            if per_layer:
                for l, s_ref in enumerate(s_refs):
                    at = (slice(l, l + 1),) if len(w_ref.shape) == 2 else (l,)
                    g = _sum_slots(s_ref)
                    res = (g,) + _adamw_math(w_ref[at], g, m_ref[at], v_ref[at])
                    for o_ref, val in zip(outs, res):
                        o_ref[at] = val
            else:
                g = _sum_slots(s_refs[0])
                res = (g,) + _adamw_math(w_ref[...], g, m_ref[...], v_ref[...])
                for o_ref, val in zip(outs, res):
                    o_ref[...] = val
            i += n_slot + 3
            o += 4

    kw = {}
    if grid is not None:
        kw = dict(grid=grid, in_specs=in_specs, out_specs=tuple(out_specs),
                  compiler_params=_cp(("parallel",) * len(grid)))
    res = pl.pallas_call(body, name=f"adamw_{tag}", out_shape=tuple(out_shape), **kw)(*flat_in)
    return [tuple(res[4 * e:4 * e + 4]) for e in range(len(entries))], res[4 * len(entries):]


def kernel(x, norm_g, w_in, b_in, ssm_log_dt, ssm_lam_re, ssm_lam_im, ssm_b_re, ssm_b_im, ssm_c_re, ssm_c_im, ssm_d, ssm_w_glu, ssm_b_glu, pool_w, pool_scale, w_branch_a, w_branch_b, w_out, final_norm_g, loss_target, m_norm_g, m_w_in, m_b_in, m_ssm_log_dt, m_ssm_lam_re, m_ssm_lam_im, m_ssm_b_re, m_ssm_b_im, m_ssm_c_re, m_ssm_c_im, m_ssm_d, m_ssm_w_glu, m_ssm_b_glu, m_pool_w, m_pool_scale, m_w_branch_a, m_w_branch_b, m_w_out, m_final_norm_g, v_norm_g, v_w_in, v_b_in, v_ssm_log_dt, v_ssm_lam_re, v_ssm_lam_im, v_ssm_b_re, v_ssm_b_im, v_ssm_c_re, v_ssm_c_im, v_ssm_d, v_ssm_w_glu, v_ssm_b_glu, v_pool_w, v_pool_scale, v_w_branch_a, v_w_branch_b, v_w_out, v_final_norm_g):
    weights = dict(norm_g=norm_g, w_in=w_in, b_in=b_in, ssm_log_dt=ssm_log_dt, ssm_lam_re=ssm_lam_re,
                   ssm_lam_im=ssm_lam_im, ssm_b_re=ssm_b_re, ssm_b_im=ssm_b_im, ssm_c_re=ssm_c_re,
                   ssm_c_im=ssm_c_im, ssm_d=ssm_d, ssm_w_glu=ssm_w_glu, ssm_b_glu=ssm_b_glu, pool_w=pool_w,
                   pool_scale=pool_scale, w_branch_a=w_branch_a, w_branch_b=w_branch_b, w_out=w_out,
                   final_norm_g=final_norm_g.reshape(1, D_MODEL))
    mom_m = dict(norm_g=m_norm_g, w_in=m_w_in, b_in=m_b_in, ssm_log_dt=m_ssm_log_dt, ssm_lam_re=m_ssm_lam_re,
                 ssm_lam_im=m_ssm_lam_im, ssm_b_re=m_ssm_b_re, ssm_b_im=m_ssm_b_im, ssm_c_re=m_ssm_c_re,
                 ssm_c_im=m_ssm_c_im, ssm_d=m_ssm_d, ssm_w_glu=m_ssm_w_glu, ssm_b_glu=m_ssm_b_glu,
                 pool_w=m_pool_w, pool_scale=m_pool_scale, w_branch_a=m_w_branch_a, w_branch_b=m_w_branch_b,
                 w_out=m_w_out, final_norm_g=m_final_norm_g.reshape(1, D_MODEL))
    mom_v = dict(norm_g=v_norm_g, w_in=v_w_in, b_in=v_b_in, ssm_log_dt=v_ssm_log_dt, ssm_lam_re=v_ssm_lam_re,
                 ssm_lam_im=v_ssm_lam_im, ssm_b_re=v_ssm_b_re, ssm_b_im=v_ssm_b_im, ssm_c_re=v_ssm_c_re,
                 ssm_c_im=v_ssm_c_im, ssm_d=v_ssm_d, ssm_w_glu=v_ssm_w_glu, ssm_b_glu=v_ssm_b_glu,
                 pool_w=v_pool_w, pool_scale=v_pool_scale, w_branch_a=v_w_branch_a, w_branch_b=v_w_branch_b,
                 w_out=v_w_out, final_norm_g=v_final_norm_g.reshape(1, D_MODEL))
    order = ["norm_g", "w_in", "b_in", "ssm_log_dt", "ssm_lam_re", "ssm_lam_im", "ssm_b_re", "ssm_b_im",
             "ssm_c_re", "ssm_c_im", "ssm_d", "ssm_w_glu", "ssm_b_glu", "pool_w", "pool_scale", "w_branch_a",
             "w_branch_b", "w_out", "final_norm_g"]
    big_names = ["w_in", "ssm_w_glu", "w_branch_a", "w_branch_b", "w_out"]

    log_dt3 = ssm_log_dt.reshape(DEPTH, N_GROUP, 1)
    b_t = lambda a: a.transpose(0, 1, 3, 2)
    for d in (weights, mom_m, mom_v):
        d["ssm_b_re"], d["ssm_b_im"] = b_t(d["ssm_b_re"]), b_t(d["ssm_b_im"])
    bt_re, bt_im = weights["ssm_b_re"], weights["ssm_b_im"]
    abar_re, abar_im, bbt_re, bbt_im = _s5_params(log_dt3, ssm_lam_re, ssm_lam_im, bt_re, bt_im)
    s5_args = (bbt_re, bbt_im, ssm_c_re, ssm_c_im, abar_re, abar_im, ssm_d)

    w16 = {n: weights[n].astype(BF16) for n in big_names}
    rest = [w16[n] for n in big_names[1:]]
    half = D_MODEL // 2
    wg_in = [None, [None, None]]
    wg_rest = [None, None]
    wg_in[0] = list(_run_carried("gather_w_in_l0", _gather_plan([w16["w_in"]], 0)))
    xs = [x.reshape(SEQ, D_MODEL)]
    saved = []
    for l in range(DEPTH):
        proj, moved = _norm_proj(l, xs[l], norm_g, wg_in[l], b_in,
                                 carry=_gather_plan([w16["w_in"]], 1, rows_of=(0, half)) if l == 0 else None)
        if l == 0:
            (wg_in[1][0],) = moved
        (states, y0), wg_rest[l] = _s5_scan_fwd(l, proj, *s5_args, carry=_gather_plan(rest, l, by_columns=(1, 2)))
        pooled = _pool_fwd(l, proj)
        wg_glu, wg_a, wg_b, wg_out = wg_rest[l]
        last = l == DEPTH - 1
        res, moved = _mix_fwd(
            l, xs[l], proj, y0, pooled, wg_glu, ssm_b_glu, pool_w, pool_scale, wg_a, wg_b, wg_out,
            carry=_gather_plan([w16["w_in"]], 1, rows_of=(half, half)) if l == 0 else None,
            head=(loss_target.reshape(SEQ, D_MODEL), weights["final_norm_g"]) if last else None)
        if l == 0:
            (wg_in[1][1],) = moved
        if last:
            dx, loss_part, g_final = res
        else:
            xs.append(res[0])
        saved.append((proj, states, y0, pooled))

    core = lax.axis_index("c").astype(jnp.int32).reshape(1)
    vec_names = ["norm_g", "b_in", "ssm_d", "ssm_b_glu", "pool_scale", "ssm_log_dt"]
    s5_names = ["ssm_log_dt", "ssm_lam_re", "ssm_lam_im", "ssm_b_re", "ssm_b_im"]
    mat_names = ["pool_w", "ssm_c_re", "ssm_c_im", "ssm_b_re", "ssm_b_im"]
    lane_sparse = ("ssm_c_re", "ssm_c_im", "ssm_b_re", "ssm_b_im")

    def dense(key, a):
        return a.reshape(-1, LANES) if key[0] in lane_sparse else a

    def undense(key, slots):
        return slots.reshape((N_CHIP, N_GROUP, GROUP_W, STATE)) if key[0] in lane_sparse else slots

    def add_small(tag, keys, own, got):
        out = [None] * len(keys)
        whole = [i for i, k in enumerate(keys) if k[0] not in mat_names]
        tiled = [i for i, k in enumerate(keys) if k[0] in mat_names]
        if whole:
            for i, r in zip(whole, _add_lists(f"{tag}_a", [own[i] for i in whole], [got[i] for i in whole])):
                out[i] = r
        if tiled:
            specs = [pl.BlockSpec((1, POOL_GROUP, POOL_GROUP), lambda j: (j, 0, 0)) if keys[i][0] == "pool_w"
                     else pl.BlockSpec((own[i].shape[0] // N_CHUNK, LANES), lambda j: (j, 0)) for i in tiled]
            for i, r in zip(tiled, _add_lists(f"{tag}_b", [own[i] for i in tiled], [got[i] for i in tiled],
                                              grid=(N_CHUNK,), specs=specs, dtype=BF16)):
                out[i] = r
        return out

    sm = {("final_norm_g", None): g_final, ("loss", None): loss_part}
    slots = {}
    grads = dict.fromkeys(big_names)

    class Wave:
        def __init__(self, tag, layer, big, keys):
            self.tag, self.layer, self.big, self.keys = tag, layer, big, keys

        def to_sibling(self):
            self.own = [dense(k, sm[k]) for k in self.keys]
            return _sibling_plan([(grads[n], self.layer) for n in self.big], self.own)

        def add(self, moved):
            nb = len(self.big)
            self.chip_big = list(_add_own(self.tag, core, [grads[n] for n in self.big], self.layer, moved[:nb])
                                 ) if nb else []
            self.chip_small = add_small(self.tag, self.keys, self.own, moved[nb:])

        def to_chips(self, big=None, small=True):
            self.sent = list(self.big if big is None else big), small
            return _chips_plan([self.chip_big[self.big.index(n)] for n in self.sent[0]],
                               self.chip_small if small else [])

        def landed(self, moved):
            names, small = self.sent
            for n, s in zip(names, moved[:len(names)]):
                slots[(n, self.layer)] = s
            if small:
                for k, s in zip(self.keys, moved[len(names):]):
                    slots[k] = undense(k, s)
            return moved[len(names) + (len(self.keys) if small else 0):]

    def s5_param_grads(l, g_abar_re, g_abar_im, g_bbt_re, g_bbt_im):
        g = _s5_params_bwd(l, log_dt3, ssm_lam_re, ssm_lam_im, bt_re, bt_im, g_abar_re, g_abar_im, g_bbt_re, g_bbt_im)
        sm[("ssm_log_dt", l)] = g[0].reshape(1, N_GROUP)
        for n, a in zip(s5_names[1:], g[1:]):
            sm[(n, l)] = a

    small1 = ["b_in", "ssm_d", "ssm_b_glu", "pool_scale", "pool_w", "ssm_c_re", "ssm_c_im"] + s5_names
    w1 = Wave("chip1", 1, list(big_names), [(n, 1) for n in small1] + [("final_norm_g", None), ("loss", None)])
    early = Wave("chip0e", 0, big_names[1:], [("pool_w", 0), ("pool_scale", 0), ("ssm_b_glu", 0)])
    mid = Wave("chip0m", 0, [], [(n, 0) for n in ["ssm_c_re", "ssm_c_im", "ssm_d"] + s5_names] + [("norm_g", 1)])
    late = Wave("chip0l", 0, ["w_in"], [("b_in", 0)])

    mix_prev, gw_in = None, None
    for l in reversed(range(DEPTH)):
        proj, states, y0, pooled = saved[l]
        wg_glu, wg_a, wg_b, wg_out = wg_rest[l]
        res, moved = _mix_bwd(l, dx, proj, y0, pooled, wg_glu, ssm_b_glu, pool_w, pool_scale, wg_a, wg_b, wg_out,
                              mix_prev, carry=None if l == 1 else w1.to_chips(big=["w_in"], small=False))
        if l == 0:
            w1.landed(moved)
        dproj, dy0, dpooled = res[:3]
        mix_prev = list(res[3:7])
        grads["w_out"], grads["w_branch_a"], grads["w_branch_b"], grads["ssm_w_glu"] = mix_prev
        sm[("pool_w", l)], sm[("pool_scale", l)], sm[("ssm_b_glu", l)] = res[7:]
        dproj = _pool_bwd(l, dpooled, dproj)
        carry = None if l == 1 else _join(w1.to_chips(big=big_names[1:]), early.to_sibling())
        res, moved = _s5_scan_bwd(l, dy0, proj, states, *s5_args, dproj, carry=carry)
        if l == 0:
            early.add(w1.landed(moved))
        dproj, g_bbt_re, g_bbt_im, sm[("ssm_c_re", l)], sm[("ssm_c_im", l)], g_abar_re, g_abar_im, sm[("ssm_d", l)] = res
        s5_param_grads(l, g_abar_re, g_abar_im, g_bbt_re, g_bbt_im)
        carry = None if l == 1 else _join(early.to_chips(), mid.to_sibling())
        (gw_in, sm[("b_in", l)]), moved = _proj_wgrad(l, xs[l], norm_g, dproj, gw_in, carry=carry)
        grads["w_in"] = gw_in
        if l == 0:
            mid.add(early.landed(moved))
        carry = w1.to_sibling() if l == 1 else _join(mid.to_chips(), late.to_sibling())
        (dx, sm[("norm_g", l)]), moved = _proj_dgrad(l, dx, xs[l], norm_g, dproj, wg_in[l], carry=carry)
        if l == 1:
            w1.add(moved)
        else:
            late.add(mid.landed(moved))
    grad_x = dx.reshape(1, SEQ, D_MODEL)
    moved = late.landed(_run_carried("exchange_last", _join(late.to_chips(), _all_plan([sm[("norm_g", 0)]]))))
    slots[("norm_g", 0)] = moved[0]

    res = {}
    for n in big_names:
        res[n] = _sum_slots_adamw(n, [slots[(n, l)] for l in range(DEPTH)], weights[n], mom_m[n], mom_v[n])
    per_layer = lambda n: [slots[(n, l)] for l in range(DEPTH)]
    names_a = vec_names + ["ssm_lam_re", "ssm_lam_im"]
    entries_a = [(per_layer(n), weights[n], mom_m[n], mom_v[n], None, None) for n in names_a]
    n = "final_norm_g"
    entries_a.append((slots[(n, None)], weights[n], mom_m[n], mom_v[n], None, None))
    out_a, (loss,) = _adamw_small("small_a", entries_a, sums=[slots[("loss", None)]])
    loss = loss.reshape(())
    for n, r in zip(names_a + ["final_norm_g"], out_a):
        res[n] = r
    res["final_norm_g"] = tuple(a.reshape(D_MODEL) for a in res["final_norm_g"])
    pw_s = pl.BlockSpec((N_CHIP, 1, POOL_GROUP, POOL_GROUP), lambda j: (0, j, 0, 0))
    pw_w = pl.BlockSpec((DEPTH, 1, POOL_GROUP, POOL_GROUP), lambda j: (0, j, 0, 0))
    c_s = pl.BlockSpec((N_CHIP, CH_G, GROUP_W, STATE), lambda j: (0, j, 0, 0))
    c_w = pl.BlockSpec((DEPTH, CH_G, GROUP_W, STATE), lambda j: (0, j, 0, 0))
    entries_b = [(per_layer(n), weights[n], mom_m[n], mom_v[n], pw_s if n == "pool_w" else c_s,
                  pw_w if n == "pool_w" else c_w) for n in mat_names]
    out_b, _ = _adamw_small("small_b", entries_b, grid=(N_CHUNK,))
    for n, r in zip(mat_names, out_b):
        res[n] = tuple(b_t(a) for a in r) if n in ("ssm_b_re", "ssm_b_im") else r

    outs = [loss, grad_x]
    for i in range(4):
        outs += [res[n][i] for n in order]
    return tuple(outs)
```

```python
import math

import jax
import jax.numpy as jnp
from jax import lax
from jax.experimental import pallas as pl
from jax.experimental.pallas import tpu as pltpu

F32 = jnp.float32
BF16 = jnp.bfloat16

SEQ = 2048
D_MODEL = 1024
N_IN = 4096
WIDTH = 512
N_GROUP = 32
GROUP_W = 16
STATE = 64
N_STATE = N_GROUP * STATE
N_CHUNK = 4
CH_G = N_GROUP // N_CHUNK
CH_W = WIDTH // N_CHUNK
CH_S = N_STATE // N_CHUNK
N_DEV = 8
N_CHIP = 4
POOL_WINDOWS = (2, 4, 8, 16)
POOL_GROUP = 128
EPS = 1e-6
DEPTH = 2

ADAM_LR = 0.001
ADAM_B1 = 0.9
ADAM_B2 = 0.999
ADAM_EPS = 1e-08
ADAM_WD = 0.01
ADAM_STEP = 10

LANES = 128
SUBLANES = 8
TILE_M = 256
VMEM_LIMIT = 48 * 1024 * 1024
VMEM_LIMIT_BIG = 60 * 1024 * 1024
MESH = pl.DeviceIdType.MESH
ANY = pl.BlockSpec(memory_space=pl.ANY)

GELU_C = math.sqrt(2.0 / math.pi)
GELU_A = 0.044715

SDS = jax.ShapeDtypeStruct


def _cp(sem=None, limit=VMEM_LIMIT):
    return pltpu.CompilerParams(dimension_semantics=sem, vmem_limit_bytes=limit)


def _dot(a, b):
    return jnp.dot(a, b, preferred_element_type=F32)


def _dot_nt(a, b):
    return lax.dot_general(a, b, (((1,), (1,)), ((), ())), preferred_element_type=F32)


def _dot_tn(a, b):
    return lax.dot_general(a, b, (((0,), (0,)), ((), ())), preferred_element_type=F32)


def _sig(x):
    return jax.nn.sigmoid(x)


def _rms(x):
    rs = lax.rsqrt(jnp.mean(x * x, axis=-1, keepdims=True) + EPS)
    return rs, x * rs


def _slot(n):
    return 4 * (n % 2) + n // 2


def _const(shape):
    n = len(shape)
    return pl.BlockSpec(shape, lambda *_: (0,) * n)


def _pair_sum(vals):
    while len(vals) > 1:
        vals = [vals[i] + vals[i + 1] for i in range(0, len(vals), 2)]
    return vals[0]


def _sum_slots(s_ref):
    return _pair_sum([s_ref[k].astype(F32) for k in range(s_ref.shape[0])])


def _s5_param_fn(log_dt, lam_re, lam_im, bt_re, bt_im):
    dt = jnp.exp(log_dt)
    mag = jnp.exp(lam_re * dt)
    ang = lam_im * dt
    abar_re = mag * jnp.cos(ang)
    abar_im = mag * jnp.sin(ang)
    num_re = abar_re - 1.0
    num_im = abar_im
    den = lam_re * lam_re + lam_im * lam_im
    coef_re = (num_re * lam_re + num_im * lam_im) / den
    coef_im = (num_im * lam_re - num_re * lam_im) / den
    bbar_re = coef_re[..., None, :] * bt_re - coef_im[..., None, :] * bt_im
    bbar_im = coef_re[..., None, :] * bt_im + coef_im[..., None, :] * bt_re
    return abar_re, abar_im, bbar_re, bbar_im


def _s5_params(log_dt, lam_re, lam_im, bt_re, bt_im):
    def body(ld, lr, li, br, bi, o_ar, o_ai, o_br, o_bi):
        ar, ai, bbr, bbi = _s5_param_fn(ld[...], lr[...], li[...], br[...], bi[...])
        o_ar[...] = ar
        o_ai[...] = ai
        o_br[...] = bbr
        o_bi[...] = bbi

    return pl.pallas_call(
        body, name="s5_params",
        out_shape=(SDS(lam_re.shape, F32), SDS(lam_re.shape, F32), SDS(bt_re.shape, F32), SDS(bt_re.shape, F32)),
    )(log_dt, lam_re, lam_im, bt_re, bt_im)


def _s5_params_bwd(layer, log_dt, lam_re, lam_im, bt_re, bt_im, g_ar, g_ai, g_br, g_bi):
    def body(ld, lr, li, br, bi, car, cai, cbr, cbi, o_ld, o_lr, o_li, o_br, o_bi):
        _, vjp = jax.vjp(_s5_param_fn, ld[...], lr[...], li[...], br[...], bi[...])
        d_ld, d_lr, d_li, d_br, d_bi = vjp((car[...], cai[...], cbr[...], cbi[...]))
        o_ld[...] = d_ld
        o_lr[...] = d_lr
        o_li[...] = d_li
        o_br[...] = d_br
        o_bi[...] = d_bi

    one = lambda shape: pl.BlockSpec((None,) + shape, lambda i: (layer,) + (0,) * len(shape))
    whole = lambda shape: _const(shape)
    vec, lam, mat = (N_GROUP, 1), (N_GROUP, STATE), (N_GROUP, GROUP_W, STATE)
    return pl.pallas_call(
        body, name=f"s5_params_bwd_l{layer}", grid=(1,),
        in_specs=[one(vec), one(lam), one(lam), one(mat), one(mat), whole(lam), whole(lam), whole(mat), whole(mat)],
        out_specs=(whole(vec), whole(lam), whole(lam), whole(mat), whole(mat)),
        out_shape=(SDS(vec, F32), SDS(lam, F32), SDS(lam, F32), SDS(mat, F32), SDS(mat, F32)),
    )(log_dt, lam_re, lam_im, bt_re, bt_im, g_ar, g_ai, g_br, g_bi)


def _norm_proj(layer, x, norm_g, wg_in, b_in, carry=None):
    n_w = len(wg_in)

    def body(x_ref, g_ref, b_ref, *refs):
        w_refs, o_ref = refs[:n_w], refs[n_w]
        _, xn = _rms(x_ref[...])
        h = (xn * g_ref[layer:layer + 1, :]).astype(BF16)
        for k in range(N_DEV):
            cols = slice(k * WIDTH, (k + 1) * WIDTH)
            acc = b_ref[layer:layer + 1, cols]
            row = 0
            for w_ref in w_refs:
                rows = w_ref.shape[1]
                acc = acc + _dot(h[:, row:row + rows], w_ref[k])
                row += rows
            o_ref[:, cols] = acc

    (proj,), moved = _pcall(
        body, name=f"norm_proj_l{layer}",
        out_shape=[SDS((SEQ, N_IN), F32)],
        grid=(SEQ // TILE_M,),
        in_specs=[pl.BlockSpec((TILE_M, D_MODEL), lambda i: (i, 0)),
                  _const((DEPTH, D_MODEL)),
                  _const((DEPTH, N_IN))] + [_const(w.shape) for w in wg_in],
        out_specs=[pl.BlockSpec((TILE_M, N_IN), lambda i: (i, 0))],
        args=[x, norm_g, b_in, *wg_in], sem=("parallel",), carry=carry)
    return proj, moved


TIME_BLK = 512
N_TBLK = SEQ // TIME_BLK
N_PANEL = CH_S // LANES
STATE_SHAPE = (N_PANEL, SEQ * SUBLANES, LANES)


def _s5_layer_specs(layer):
    mat = lambda: pl.BlockSpec((None, N_GROUP, GROUP_W, STATE), lambda i: (layer, 0, 0, 0))
    ab = lambda: pl.BlockSpec((None, N_GROUP, STATE), lambda i: (layer, 0, 0))
    return [mat(), mat(), mat(), mat(), ab(), ab(), _const((DEPTH, WIDTH))]


def _s5_layer_scratch():
    return [pltpu.VMEM((N_CHUNK, CH_W, CH_S), BF16)] * 4 + [pltpu.VMEM((8, CH_S), F32)] * 2


def _s5_layer_fill(btre_ref, btim_ref, cre_ref, cim_ref, are_ref, aim_ref, bdre, bdim, ctre, ctim, a1, a2):
    for m in (bdre, bdim, ctre, ctim):
        m[...] = jnp.zeros_like(m)
    for grp in range(N_GROUP):
        k, g = divmod(grp, CH_G)
        rows = slice(g * GROUP_W, (g + 1) * GROUP_W)
        cols = slice(g * STATE, (g + 1) * STATE)
        bdre[k, rows, cols] = btre_ref[grp].astype(BF16)
        bdim[k, rows, cols] = btim_ref[grp].astype(BF16)
        ctre[k, rows, cols] = cre_ref[grp].astype(BF16)
        ctim[k, rows, cols] = cim_ref[grp].astype(BF16)
        ar = are_ref[grp:grp + 1, :]
        ai = aim_ref[grp:grp + 1, :]
        a1[k:k + 1, cols] = ar
        a1[N_CHUNK + k:N_CHUNK + k + 1, cols] = ar
        a2[k:k + 1, cols] = -ai
        a2[N_CHUNK + k:N_CHUNK + k + 1, cols] = ai


SCAN_UNROLL = 16


def _panels(tile):
    return [tile[:, p * LANES:(p + 1) * LANES] for p in range(N_PANEL)]


def _rows_load(ref, row):
    return jnp.concatenate([ref[p, pl.ds(row, TIME_BLK, stride=SUBLANES), :] for p in range(N_PANEL)], axis=1)


def _rows_store(ref, row, val):
    for p in range(N_PANEL):
        ref[p, pl.ds(row, TIME_BLK, stride=SUBLANES), :] = val[:, p * LANES:(p + 1) * LANES]


def _s5_scan_fwd(layer, proj, bbt_re, bbt_im, c_re, c_im, abar_re, abar_im, d_skip, carry=None):
    def body(u_ref, btre_ref, btim_ref, cre_ref, cim_ref, are_ref, aim_ref, d_ref, s_ref, y_ref,
             bdre, bdim, ctre, ctim, a1, a2, state):
        @pl.when(pl.program_id(0) == 0)
        def _():
            _s5_layer_fill(btre_ref, btim_ref, cre_ref, cim_ref, are_ref, aim_ref, bdre, bdim, ctre, ctim, a1, a2)
            state[...] = jnp.zeros_like(state)

        for k in range(N_CHUNK):
            ub = u_ref[:, k * CH_W:(k + 1) * CH_W].astype(BF16)
            _rows_store(s_ref, k, _dot(ub, bdre[k]))
            _rows_store(s_ref, N_CHUNK + k, _dot(ub, bdim[k]))
        m1 = _panels(a1[...])
        m2 = _panels(a2[...])

        def steps(n, tile):
            for r in range(SCAN_UNROLL):
                rows = pl.ds(pl.multiple_of((n * SCAN_UNROLL + r) * 8, 8), 8)
                tile = [m1[p] * tile[p] + m2[p] * pltpu.roll(tile[p], N_CHUNK, 0) + s_ref[p, rows, :]
                        for p in range(N_PANEL)]
                for p in range(N_PANEL):
                    s_ref[p, rows, :] = tile[p]
            return tile

        tile = lax.fori_loop(0, TIME_BLK // SCAN_UNROLL, steps, _panels(state[...]))
        state[...] = jnp.concatenate(tile, axis=1)
        d = d_ref[layer:layer + 1, :]
        for k in range(N_CHUNK):
            cols = slice(k * CH_W, (k + 1) * CH_W)
            y = (_dot_nt(_rows_load(s_ref, k).astype(BF16), ctre[k])
                 - _dot_nt(_rows_load(s_ref, N_CHUNK + k).astype(BF16), ctim[k]))
            y_ref[:, cols] = y + d[:, cols] * u_ref[:, cols]

    return _pcall(
        body, name=f"s5_fwd_l{layer}",
        out_shape=(SDS(STATE_SHAPE, F32), SDS((SEQ, WIDTH), F32)),
        grid=(N_TBLK,),
        in_specs=[pl.BlockSpec((TIME_BLK, WIDTH), lambda i: (i, 0))] + _s5_layer_specs(layer),
        out_specs=(pl.BlockSpec((N_PANEL, TIME_BLK * SUBLANES, LANES), lambda i: (0, i, 0)),
                   pl.BlockSpec((TIME_BLK, WIDTH), lambda i: (i, 0))),
        scratch_shapes=_s5_layer_scratch() + [pltpu.VMEM((8, CH_S), F32)],
        args=[proj, bbt_re, bbt_im, c_re, c_im, abar_re, abar_im, d_skip], sem=("arbitrary",), carry=carry)


def _s5_scan_bwd(layer, dy0, proj, states, bbt_re, bbt_im, c_re, c_im, abar_re, abar_im, d_skip, dproj,
                 carry=None):
    def body(dy_ref, u_ref, s_ref, sprev_ref, btre_ref, btim_ref, cre_ref, cim_ref, are_ref, aim_ref, d_ref, _,
             du_ref, gbre_ref, gbim_ref, gcre_ref, gcim_ref, gare_ref, gaim_ref, gd_ref,
             lam_ref, bdre, bdim, ctre, ctim, a1, a2, state, acc1, acc2, gbre, gbim, gcre, gcim, gd):
        step_id = pl.program_id(0)

        @pl.when(step_id == 0)
        def _():
            _s5_layer_fill(btre_ref, btim_ref, cre_ref, cim_ref, are_ref, aim_ref, bdre, bdim, ctre, ctim, a1, a2)
            for r in (state, acc1, acc2, gbre, gbim, gcre, gcim, gd):
                r[...] = jnp.zeros_like(r)

        for k in range(N_CHUNK):
            dyb = dy_ref[:, k * CH_W:(k + 1) * CH_W].astype(BF16)
            _rows_store(lam_ref, k, _dot(dyb, ctre[k]))
            _rows_store(lam_ref, N_CHUNK + k, -_dot(dyb, ctim[k]))
            gcre[k] += _dot_tn(dyb, _rows_load(s_ref, k).astype(BF16))
            gcim[k] -= _dot_tn(dyb, _rows_load(s_ref, N_CHUNK + k).astype(BF16))

        m1 = _panels(a1[...])
        m2 = _panels(-a2[...])
        has_before = (step_id < N_TBLK - 1).astype(F32)

        def one(t8, c, first_token):
            tile, swapped, p1, p2 = c
            rows = pl.ds(t8, 8)
            tile = [m1[p] * tile[p] + m2[p] * swapped[p] + lam_ref[p, rows, :] for p in range(N_PANEL)]
            swapped = [pltpu.roll(tile[p], N_CHUNK, 0) for p in range(N_PANEL)]
            for p in range(N_PANEL):
                lam_ref[p, rows, :] = tile[p]
            if first_token:
                before = [sprev_ref[p] * has_before for p in range(N_PANEL)]
            else:
                before = [s_ref[p, pl.ds(t8 - 8, 8), :] for p in range(N_PANEL)]
            p1 = [p1[p] + tile[p] * before[p] for p in range(N_PANEL)]
            p2 = [p2[p] + swapped[p] * before[p] for p in range(N_PANEL)]
            return tile, swapped, p1, p2

        def steps(n, c):
            for r in range(SCAN_UNROLL):
                t8 = pl.multiple_of((TIME_BLK - 1 - (n * SCAN_UNROLL + r)) * 8, 8)
                c = one(t8, c, False)
            return c

        tile0 = _panels(state[...])
        c = (tile0, [pltpu.roll(t, N_CHUNK, 0) for t in tile0], _panels(acc1[...]), _panels(acc2[...]))
        c = lax.fori_loop(0, TIME_BLK // SCAN_UNROLL - 1, steps, c)
        for r in range(SCAN_UNROLL - 1, -1, -1):
            c = one(r * 8, c, r == 0)
        state[...] = jnp.concatenate(c[0], axis=1)
        acc1[...] = jnp.concatenate(c[2], axis=1)
        acc2[...] = jnp.concatenate(c[3], axis=1)

        d = d_ref[layer:layer + 1, :]
        for k in range(N_CHUNK):
            cols = slice(k * CH_W, (k + 1) * CH_W)
            lrb = _rows_load(lam_ref, k).astype(BF16)
            lib = _rows_load(lam_ref, N_CHUNK + k).astype(BF16)
            u = u_ref[:, cols]
            ub = u.astype(BF16)
            dy = dy_ref[:, cols]
            du = dy * d[:, cols] + _dot_nt(lrb, bdre[k]) + _dot_nt(lib, bdim[k])
            du_ref[:, cols] = du.astype(BF16)
            gbre[k] += _dot_tn(ub, lrb)
            gbim[k] += _dot_tn(ub, lib)
        gd[...] += jnp.sum(dy_ref[...] * u_ref[...], axis=0, keepdims=True)

        @pl.when(step_id == N_TBLK - 1)
        def _():
            gd_ref[...] = gd[...]
            ga_re = acc1[0:N_CHUNK, :] + acc1[N_CHUNK:, :]
            ga_im = acc2[0:N_CHUNK, :] - acc2[N_CHUNK:, :]
            for grp in range(N_GROUP):
                k, g = divmod(grp, CH_G)
                rows = slice(g * GROUP_W, (g + 1) * GROUP_W)
                cols = slice(g * STATE, (g + 1) * STATE)
                gcre_ref[grp] = gcre[k, rows, cols]
                gcim_ref[grp] = gcim[k, rows, cols]
                gbre_ref[grp] = gbre[k, rows, cols]
                gbim_ref[grp] = gbim[k, rows, cols]
                gare_ref[grp:grp + 1, :] = ga_re[k:k + 1, cols]
                gaim_ref[grp:grp + 1, :] = ga_im[k:k + 1, cols]

    back = lambda i: N_TBLK - 1 - i
    tok = lambda: pl.BlockSpec((TIME_BLK, WIDTH), lambda i: (back(i), 0))
    mat = lambda: _const((N_GROUP, GROUP_W, STATE))
    acc_mat = pltpu.VMEM((N_CHUNK, CH_W, CH_S), F32)
    return _pcall(
        body, name=f"s5_bwd_l{layer}",
        out_shape=(SDS((SEQ, N_IN), BF16), SDS((N_GROUP, GROUP_W, STATE), F32), SDS((N_GROUP, GROUP_W, STATE), F32),
                   SDS((N_GROUP, GROUP_W, STATE), F32), SDS((N_GROUP, GROUP_W, STATE), F32),
                   SDS((N_GROUP, STATE), F32), SDS((N_GROUP, STATE), F32), SDS((1, WIDTH), F32)),
        grid=(N_TBLK,),
        in_specs=[tok(), tok(),
                  pl.BlockSpec((N_PANEL, TIME_BLK * SUBLANES, LANES), lambda i: (0, back(i), 0)),
                  pl.BlockSpec((N_PANEL, SUBLANES, LANES), lambda i: (0, jnp.maximum(back(i) * TIME_BLK - 1, 0), 0))]
        + _s5_layer_specs(layer) + [ANY],
        out_specs=(tok(), mat(), mat(), mat(), mat(), _const((N_GROUP, STATE)), _const((N_GROUP, STATE)),
                   _const((1, WIDTH))),
        scratch_shapes=[pltpu.VMEM((N_PANEL, TIME_BLK * SUBLANES, LANES), F32)] + _s5_layer_scratch()
        + [pltpu.VMEM((8, CH_S), F32)] * 3 + [acc_mat] * 4 + [pltpu.VMEM((1, WIDTH), F32)],
        args=[dy0, proj, states, states, bbt_re, bbt_im, c_re, c_im, abar_re, abar_im, d_skip, dproj],
        aliases={11: 0}, sem=("arbitrary",), limit=VMEM_LIMIT_BIG, carry=carry)


def _pool_counts(win):
    t = lax.broadcasted_iota(jnp.int32, (SEQ, POOL_GROUP), 0)
    return t, jnp.minimum(t + 1, win).astype(F32)


def _pool_fwd(layer, proj):
    def body(u_ref, o_ref):
        for gi, win in enumerate(POOL_WINDOWS):
            cols = slice(gi * POOL_GROUP, (gi + 1) * POOL_GROUP)
            u = u_ref[:, cols]
            t, count = _pool_counts(win)
            acc = u
            k = 1
            while k < win:
                acc = acc + jnp.where(t >= k, pltpu.roll(acc, k, 0), 0.0)
                k *= 2
            o_ref[:, cols] = acc / count - u

    return pl.pallas_call(
        body, name=f"pool_fwd_l{layer}",
        out_shape=SDS((SEQ, WIDTH), F32),
        grid=(1,),
        in_specs=[pl.BlockSpec((SEQ, WIDTH), lambda i: (0, 2))],
        out_specs=pl.BlockSpec((SEQ, WIDTH), lambda i: (0, 0)),
        compiler_params=_cp(("arbitrary",)),
    )(proj)


def _gelu_parts(y0):
    t = jnp.tanh(GELU_C * (y0 + GELU_A * (y0 * y0 * y0)))
    return t, 0.5 * y0 * (1.0 + t)


def _mix_forward(layer, p_ref, y0_ref, pooled_ref, wglu_ref, bglu_ref, pw_ref, scale_ref, wa_ref, wb_ref):
    za = p_ref[:, WIDTH:2 * WIDTH]
    zb = p_ref[:, 3 * WIDTH:4 * WIDTH]
    ga = p_ref[:, 4 * WIDTH:4 * WIDTH + D_MODEL]
    gb = p_ref[:, 4 * WIDTH + D_MODEL:]
    y0 = y0_ref[...]
    t, y1 = _gelu_parts(y0)
    y1b = y1.astype(BF16)
    q = _dot(y1b, wglu_ref[...].reshape(WIDTH, WIDTH)) + bglu_ref[layer:layer + 1, :]
    sq = _sig(q)
    y2 = y1 * sq
    sza = _sig(za)
    silu_za = za * sza
    ya = y2 * silu_za
    pooled = pooled_ref[...]
    mixed = jnp.concatenate(
        [_dot(pooled[:, g * POOL_GROUP:(g + 1) * POOL_GROUP].astype(BF16), pw_ref[g].astype(BF16))
         for g in range(len(POOL_WINDOWS))], axis=1)
    szb = _sig(zb)
    silu_zb = zb * szb
    scale = scale_ref[layer:layer + 1, :]
    ms = mixed * scale
    yb = ms * silu_zb
    yab = ya.astype(BF16)
    ybb = yb.astype(BF16)
    ma = _dot(yab, wa_ref[...])
    mb = _dot(ybb, wb_ref[...])
    sga = _sig(ga)
    sgb = _sig(gb)
    merged = sga * ma + sgb * mb
    return dict(za=za, zb=zb, y0=y0, t=t, y1=y1, y1b=y1b, sq=sq, y2=y2, sza=sza, silu_za=silu_za,
                pooled=pooled, mixed=mixed, szb=szb, silu_zb=silu_zb, scale=scale, ms=ms, yab=yab, ybb=ybb,
                ma=ma, mb=mb, sga=sga, sgb=sgb, merged=merged)


def _mix_weight_specs(layer):
    return [_const((N_DEV, WIDTH // N_DEV, WIDTH)),
            _const((DEPTH, WIDTH)),
            pl.BlockSpec((None, 4, POOL_GROUP, POOL_GROUP), lambda i: (layer, 0, 0, 0)),
            _const((DEPTH, WIDTH)),
            _const((WIDTH, D_MODEL)),
            _const((WIDTH, D_MODEL)),
            _const((N_DEV, D_MODEL // N_DEV, D_MODEL))]


def _loss_head(x, t_ref, g_ref, dx_ref, loss_ref, gg_ref):
    @pl.when(pl.program_id(0) == 0)
    def _():
        loss_ref[...] = jnp.zeros_like(loss_ref)
        gg_ref[...] = jnp.zeros_like(gg_ref)

    g = g_ref[...]
    rs, xn = _rms(x)
    err = xn * g - t_ref[...]
    loss_ref[...] += 0.5 * jnp.sum(jnp.mean(err * err, axis=-1, keepdims=True), axis=0, keepdims=True)
    dy = err * (1.0 / D_MODEL)
    gg_ref[...] += jnp.sum(dy * xn, axis=0, keepdims=True)
    dxn = dy * g
    dx_ref[...] = rs * (dxn - xn * jnp.mean(dxn * xn, axis=-1, keepdims=True))


def _mix_fwd(layer, x, proj, y0, pooled, wg_glu, b_glu, pool_w, pool_scale, wg_a, wg_b, wg_out, carry=None,
             head=None):
    def body(x_ref, p_ref, y0_ref, pooled_ref, wglu_ref, bglu_ref, pw_ref, scale_ref, wa_ref, wb_ref,
             wout_ref, *rest):
        f = _mix_forward(layer, p_ref, y0_ref, pooled_ref, wglu_ref, bglu_ref, pw_ref, scale_ref, wa_ref, wb_ref)
        wout = wout_ref[...].reshape(D_MODEL, D_MODEL)
        x_next = x_ref[...] + _dot(f["merged"].astype(BF16), wout)
        if head is None:
            rest[0][...] = x_next
        else:
            _loss_head(x_next, *rest)

    tile = lambda: pl.BlockSpec((TILE_M, D_MODEL), lambda i: (i, 0))
    if head is None:
        extra, out_shape, out_specs = [], [SDS((SEQ, D_MODEL), F32)], [tile()]
    else:
        extra = list(head)
        out_shape = [SDS((SEQ, D_MODEL), F32), SDS((1, 1), F32), SDS((1, D_MODEL), F32)]
        out_specs = [tile(), _const((1, 1)), _const((1, D_MODEL))]
    return _pcall(
        body, name=f"mix_fwd_l{layer}",
        out_shape=out_shape,
        grid=(SEQ // TILE_M,),
        in_specs=[tile(),
                  pl.BlockSpec((TILE_M, N_IN), lambda i: (i, 0)),
                  pl.BlockSpec((TILE_M, WIDTH), lambda i: (i, 0)),
                  pl.BlockSpec((TILE_M, WIDTH), lambda i: (i, 0))] + _mix_weight_specs(layer)
        + ([tile(), _const((1, D_MODEL))] if head else []),
        out_specs=out_specs,
        args=[x, proj, y0, pooled, wg_glu, b_glu, pool_w, pool_scale, wg_a, wg_b, wg_out] + extra,
        sem=("parallel",) if head is None else ("arbitrary",), carry=carry)


def _big_shapes():
    return dict(w_out=(DEPTH, N_DEV, D_MODEL // N_DEV, D_MODEL), w_branch_a=(DEPTH, N_DEV, WIDTH, D_MODEL // N_DEV),
                w_branch_b=(DEPTH, N_DEV, WIDTH, D_MODEL // N_DEV), ssm_w_glu=(DEPTH, N_DEV, WIDTH // N_DEV, WIDTH),
                w_in=(DEPTH, N_DEV, D_MODEL, WIDTH))


def _mix_bwd(layer, dx_next, proj, y0, pooled, wg_glu, b_glu, pool_w, pool_scale, wg_a, wg_b, wg_out, prev,
             carry=None):
    n_k = N_DEV
    n_prev = 0 if prev is None else len(prev)

    def body(*refs):
        (dx_ref, p_ref, y0_ref, pooled_ref, wglu_ref, bglu_ref, pw_ref, scale_ref, wa_ref, wb_ref,
         wout_ref) = refs[:11]
        (dproj_ref, dy0_ref, dpooled_ref, gwout_ref, gwa_ref, gwb_ref, gwglu_ref, gpw_ref,
         gscale_ref, gbglu_ref) = refs[11 + n_prev:]

        @pl.when(pl.program_id(0) == 0)
        def _():
            for r in (gwout_ref, gwa_ref, gwb_ref, gwglu_ref, gpw_ref, gscale_ref, gbglu_ref):
                r[...] = jnp.zeros_like(r)

        f = _mix_forward(layer, p_ref, y0_ref, pooled_ref, wglu_ref, bglu_ref, pw_ref, scale_ref, wa_ref, wb_ref)
        wglu = wglu_ref[...].reshape(WIDTH, WIDTH)
        wout = wout_ref[...].reshape(D_MODEL, D_MODEL)
        blk = D_MODEL // n_k
        dxb = dx_ref[...].astype(BF16)
        dmerged = _dot_nt(dxb, wout)
        gwout = _dot_tn(f["merged"].astype(BF16), dxb)
        for k in range(n_k):
            gwout_ref[_slot(k)] += gwout[k * blk:(k + 1) * blk, :]
        dma = dmerged * f["sga"]
        dmb = dmerged * f["sgb"]
        dga = dmerged * f["ma"] * f["sga"] * (1.0 - f["sga"])
        dgb = dmerged * f["mb"] * f["sgb"] * (1.0 - f["sgb"])
        dmab = dma.astype(BF16)
        dmbb = dmb.astype(BF16)
        dya = _dot_nt(dmab, wa_ref[...])
        dyb = _dot_nt(dmbb, wb_ref[...])
        gwa = _dot_tn(f["yab"], dmab)
        gwb = _dot_tn(f["ybb"], dmbb)
        for k in range(n_k):
            gwa_ref[_slot(k)] += gwa[:, k * blk:(k + 1) * blk]
            gwb_ref[_slot(k)] += gwb[:, k * blk:(k + 1) * blk]
        zb, szb = f["zb"], f["szb"]
        dzb = dyb * f["ms"] * (szb * (1.0 + zb * (1.0 - szb)))
        dms = dyb * f["silu_zb"]
        gscale_ref[...] += jnp.sum(dms * f["mixed"], axis=0, keepdims=True)
        dmixed = (dms * f["scale"]).astype(BF16)
        pooled = f["pooled"]
        for g in range(len(POOL_WINDOWS)):
            cols = slice(g * POOL_GROUP, (g + 1) * POOL_GROUP)
            dpooled_ref[:, cols] = _dot_nt(dmixed[:, cols], pw_ref[g].astype(BF16))
            gpw_ref[g] += _dot_tn(pooled[:, cols].astype(BF16), dmixed[:, cols])
        za, sza = f["za"], f["sza"]
        dza = dya * f["y2"] * (sza * (1.0 + za * (1.0 - sza)))
        dy2 = dya * f["silu_za"]
        sq = f["sq"]
        dq = dy2 * f["y1"] * sq * (1.0 - sq)
        dqb = dq.astype(BF16)
        dy1 = dy2 * sq + _dot_nt(dqb, wglu)
        gwglu = _dot_tn(f["y1b"], dqb)
        rblk = WIDTH // n_k
        for k in range(n_k):
            gwglu_ref[_slot(k)] += gwglu[k * rblk:(k + 1) * rblk, :]
        gbglu_ref[...] += jnp.sum(dq, axis=0, keepdims=True)
        y0, t = f["y0"], f["t"]
        dgelu = 0.5 * (1.0 + t) + 0.5 * y0 * (1.0 - t * t) * (GELU_C * (1.0 + 3.0 * GELU_A * y0 * y0))
        dy0_ref[...] = dy1 * dgelu
        zeros = jnp.zeros((TILE_M, WIDTH), BF16)
        dproj_ref[:, 0:WIDTH] = zeros
        dproj_ref[:, WIDTH:2 * WIDTH] = dza.astype(BF16)
        dproj_ref[:, 2 * WIDTH:3 * WIDTH] = zeros
        dproj_ref[:, 3 * WIDTH:4 * WIDTH] = dzb.astype(BF16)
        dproj_ref[:, 4 * WIDTH:4 * WIDTH + D_MODEL] = dga.astype(BF16)
        dproj_ref[:, 4 * WIDTH + D_MODEL:] = dgb.astype(BF16)

    tile = lambda w: pl.BlockSpec((TILE_M, w), lambda i: (i, 0))
    shapes = _big_shapes()
    big = ["w_out", "w_branch_a", "w_branch_b", "ssm_w_glu"]
    slab = lambda n: pl.BlockSpec((None,) + shapes[n][1:], lambda i: (layer, 0, 0, 0))
    args = [dx_next, proj, y0, pooled, wg_glu, b_glu, pool_w, pool_scale, wg_a, wg_b, wg_out]
    return _pcall(
        body, name=f"mix_bwd_l{layer}",
        out_shape=(SDS((SEQ, N_IN), BF16), SDS((SEQ, WIDTH), F32), SDS((SEQ, WIDTH), F32))
        + tuple(SDS(shapes[n], F32) for n in big)
        + (SDS((4, POOL_GROUP, POOL_GROUP), F32), SDS((1, WIDTH), F32), SDS((1, WIDTH), F32)),
        grid=(SEQ // TILE_M,),
        in_specs=[tile(D_MODEL), tile(N_IN), tile(WIDTH), tile(WIDTH)] + _mix_weight_specs(layer) + [ANY] * n_prev,
        out_specs=(tile(N_IN), tile(WIDTH), tile(WIDTH)) + tuple(slab(n) for n in big)
        + (_const((4, POOL_GROUP, POOL_GROUP)), _const((1, WIDTH)), _const((1, WIDTH))),
        args=args + list(prev or ()),
        aliases={len(args) + i: 3 + i for i in range(n_prev)},
        sem=("arbitrary",), limit=VMEM_LIMIT_BIG, carry=carry)


def _pool_bwd(layer, dpooled, dproj):
    def body(dp_ref, _, o_ref):
        for gi, win in enumerate(POOL_WINDOWS):
            cols = slice(gi * POOL_GROUP, (gi + 1) * POOL_GROUP)
            dp = dp_ref[:, cols]
            t, count = _pool_counts(win)
            e = dp / count
            acc = e
            k = 1
            while k < win:
                acc = acc + jnp.where(t < SEQ - k, pltpu.roll(acc, SEQ - k, 0), 0.0)
                k *= 2
            o_ref[:, cols] = (acc - dp).astype(BF16)

    return pl.pallas_call(
        body, name=f"pool_bwd_l{layer}",
        out_shape=SDS((SEQ, N_IN), BF16),
        grid=(1,),
        in_specs=[pl.BlockSpec((SEQ, WIDTH), lambda i: (0, 0)), ANY],
        out_specs=pl.BlockSpec((SEQ, WIDTH), lambda i: (0, 2)),
        input_output_aliases={1: 0},
        compiler_params=_cp(("arbitrary",)),
    )(dpooled, dproj)


def _proj_wgrad(layer, x, norm_g, dproj, prev, carry=None):
    tm = 1024
    n_prev = 0 if prev is None else 1

    def body(*refs):
        x_ref, g_ref, dp_ref = refs[:3]
        gw_ref, gb_ref, ht_ref = refs[3 + n_prev:]
        n, t = pl.program_id(0), pl.program_id(1)

        @pl.when(t == 0)
        def _():
            gw_ref[...] = jnp.zeros_like(gw_ref)
            gb_ref[...] = jnp.zeros_like(gb_ref)

        @pl.when(n == 0)
        def _():
            _, xn = _rms(x_ref[...])
            ht_ref[t] = (xn * g_ref[layer:layer + 1, :]).T.astype(BF16)

        dp = dp_ref[...]
        gw_ref[...] += _dot(ht_ref[t], dp)
        gb_ref[...] += jnp.sum(dp.astype(F32), axis=0, keepdims=True)

    return _pcall(
        body, name=f"proj_wgrad_l{layer}",
        out_shape=(SDS(_big_shapes()["w_in"], F32), SDS((1, N_IN), F32)),
        grid=(N_DEV, SEQ // tm),
        in_specs=[pl.BlockSpec((tm, D_MODEL), lambda n, t: (jnp.where(n == 0, t, 0), 0)),
                  _const((DEPTH, D_MODEL)),
                  pl.BlockSpec((tm, WIDTH), lambda n, t: (t, n))] + [ANY] * n_prev,
        out_specs=(pl.BlockSpec((None, None, D_MODEL, WIDTH), lambda n, t: (layer, _slot(n), 0, 0)),
                   pl.BlockSpec((1, WIDTH), lambda n, t: (0, n))),
        scratch_shapes=[pltpu.VMEM((SEQ // tm, D_MODEL, tm), BF16)],
        args=[x, norm_g, dproj] + ([prev] if n_prev else []),
        aliases={3: 0} if n_prev else {}, sem=("arbitrary", "arbitrary"), carry=carry)


def _proj_dgrad(layer, dx_next, x, norm_g, dproj, wg_in, carry=None):
    n_w = len(wg_in)

    def body(dxn_ref, x_ref, g_ref, dp_ref, *refs):
        w_refs, (dx_ref, gg_ref) = refs[:n_w], refs[n_w:]

        @pl.when(pl.program_id(0) == 0)
        def _():
            gg_ref[...] = jnp.zeros_like(gg_ref)

        parts = []
        for w_ref in w_refs:
            part = jnp.zeros((TILE_M, w_ref.shape[1]), F32)
            for k in range(N_DEV):
                part = part + _dot_nt(dp_ref[:, k * WIDTH:(k + 1) * WIDTH], w_ref[k])
            parts.append(part)
        dh = parts[0] if n_w == 1 else jnp.concatenate(parts, axis=1)
        rs, xn = _rms(x_ref[...])
        gg_ref[...] += jnp.sum(dh * xn, axis=0, keepdims=True)
        dxn = dh * g_ref[layer:layer + 1, :]
        dx_ref[...] = dxn_ref[...] + rs * (dxn - xn * jnp.mean(dxn * xn, axis=-1, keepdims=True))

    return _pcall(
        body, name=f"proj_dgrad_l{layer}",
        out_shape=(SDS((SEQ, D_MODEL), F32), SDS((1, D_MODEL), F32)),
        grid=(SEQ // TILE_M,),
        in_specs=[pl.BlockSpec((TILE_M, D_MODEL), lambda i: (i, 0)),
                  pl.BlockSpec((TILE_M, D_MODEL), lambda i: (i, 0)),
                  _const((DEPTH, D_MODEL)),
                  pl.BlockSpec((TILE_M, N_IN), lambda i: (i, 0))] + [_const(w.shape) for w in wg_in],
        out_specs=(pl.BlockSpec((TILE_M, D_MODEL), lambda i: (i, 0)), _const((1, D_MODEL))),
        args=[dx_next, x, norm_g, dproj, *wg_in], sem=("arbitrary",), carry=carry)


def _my_place():
    return lax.axis_index("x"), lax.axis_index("y"), lax.axis_index("c")


def _gather_plan(shards, layer, by_columns=(), rows_of=None):
    n = len(shards)

    def parts(ins, outs, sems):
        send_sems, recv_sems, local_sems = sems
        x, y, c = _my_place()
        chips = [(1 - x, y), (x, 1 - y), (1 - x, 1 - y)]

        def source(t):
            return ins[t].at[layer] if rows_of is None else ins[t].at[layer, pl.ds(*rows_of)]

        def rows(t, place):
            px, py, pc = place
            index = 4 * px + 2 * py + pc
            if t in by_columns:
                width = shards[t].shape[2]
                return outs[t].at[:, pl.ds(pl.multiple_of(index * width, LANES), width)]
            return outs[t].at[index]

        def copy(t, k, block, to, from_src=False):
            return pltpu.make_async_remote_copy(
                src_ref=source(t) if from_src else rows(t, block), dst_ref=rows(t, block),
                send_sem=send_sems.at[7 * t + k], recv_sem=recv_sems.at[7 * t + k], device_id=to,
                device_id_type=MESH)

        def mine(t):
            return pltpu.make_async_copy(source(t), rows(t, (x, y, c)), local_sems.at[t])

        return (x, y, c), chips, copy, mine

    def start(ins, outs, sems):
        me, chips, copy, mine = parts(ins, outs, sems)
        x, y, c = me
        for t in range(n):
            mine(t).start()
            copy(t, 0, me, (x, y, 1 - c), from_src=True).start()
            for j, chip in enumerate(chips):
                copy(t, 1 + j, me, (*chip, c), from_src=True).start()

    def relay(ins, outs, sems):
        me, chips, copy, mine = parts(ins, outs, sems)
        x, y, c = me
        for t in range(n):
            for j, chip in enumerate(chips):
                copy(t, 1 + j, (*chip, c), me).wait_recv()
                copy(t, 4 + j, (*chip, c), (x, y, 1 - c)).start()

    def finish(ins, outs, sems):
        me, chips, copy, mine = parts(ins, outs, sems)
        x, y, c = me
        sibling = (x, y, 1 - c)
        for t in range(n):
            copy(t, 0, sibling, me).wait_recv()
            for j, chip in enumerate(chips):
                copy(t, 4 + j, (*chip, 1 - c), me).wait_recv()
            for k in range(7):
                copy(t, k, me, sibling, from_src=k < 4).wait_send()
            mine(t).wait()

    n_rows = lambda a: a.shape[1] if rows_of is None else rows_of[1]
    out_shape = [SDS((a.shape[1], N_DEV * a.shape[2]) if t in by_columns else (N_DEV, n_rows(a), a.shape[2]), a.dtype)
                 for t, a in enumerate(shards)]
    sems = [pltpu.SemaphoreType.DMA((7 * n,)), pltpu.SemaphoreType.DMA((7 * n,)), pltpu.SemaphoreType.DMA((n,))]
    return _Carried(shards, out_shape, sems, start, finish, relay)


class _Carried:
    def __init__(self, ins, out_shape, sems, start, finish, relay=None):
        self.ins, self.out_shape, self.sems = list(ins), list(out_shape), list(sems)
        self.start, self.finish = start, finish
        self.relay = relay or (lambda ins, outs, sems: None)


def _pcall(body, *, name, grid, in_specs, out_specs, out_shape, args, scratch_shapes=(), aliases=None,
           sem=None, limit=VMEM_LIMIT, carry=None):
    out_shape, out_specs, scratch_shapes = list(out_shape), list(out_specs), list(scratch_shapes)
    n_in, n_out, n_scr = len(args), len(out_shape), len(scratch_shapes)
    if carry is None:
        kern, c_ins, c_out, c_sems = body, [], [], []
    else:
        c_ins, c_out, c_sems = carry.ins, carry.out_shape, carry.sems
        ci, co = len(c_ins), len(c_out)
        steps = tuple(grid)

        def kern(*refs):
            o0 = n_in + ci
            s0 = o0 + n_out + co
            mine = refs[:n_in] + refs[o0:o0 + n_out] + refs[s0:s0 + n_scr]
            theirs = (refs[n_in:o0], refs[o0 + n_out:s0], refs[s0 + n_scr:])
            first = pl.program_id(0) == 0
            last = pl.program_id(0) == steps[0] - 1
            for a in range(1, len(steps)):
                first = jnp.logical_and(first, pl.program_id(a) == 0)
                last = jnp.logical_and(last, pl.program_id(a) == steps[a] - 1)

            @pl.when(first)
            def _():
                carry.start(*theirs)

            body(*mine)

            @pl.when(last)
            def _():
                carry.relay(*theirs)
                carry.finish(*theirs)

        sem = ("arbitrary",) * len(steps)
    res = pl.pallas_call(
        kern, name=name, grid=tuple(grid),
        in_specs=list(in_specs) + [ANY] * len(c_ins),
        out_specs=tuple(out_specs + [ANY] * len(c_out)),
        out_shape=tuple(out_shape + c_out),
        scratch_shapes=scratch_shapes + c_sems,
        input_output_aliases=aliases or {},
        compiler_params=_cp(sem, limit),
    )(*args, *c_ins)
    return res[:n_out], res[n_out:]


def _run_carried(name, carry):
    ci, co = len(carry.ins), len(carry.out_shape)

    def body(*refs):
        parts = (refs[:ci], refs[ci:ci + co], refs[ci + co:])
        carry.start(*parts)
        carry.relay(*parts)
        carry.finish(*parts)

    return pl.pallas_call(
        body, name=name, out_shape=tuple(carry.out_shape),
        in_specs=[ANY] * ci, out_specs=tuple([ANY] * co), scratch_shapes=carry.sems,
    )(*carry.ins)


def _sibling_plan(big, small):
    n = len(big)
    n_copies = 4 * n + len(small)

    def copies(ins, outs, sems):
        send_sems, recv_sems = sems
        x, y, c = _my_place()
        pairs = []
        for t, (_, layer) in enumerate(big):
            for s in range(4):
                pairs.append((ins[t].at[layer, pl.ds(4 * (1 - c) + s, 1)], outs[t].at[pl.ds(s, 1)]))
        pairs += list(zip(ins[n:], outs[n:]))
        return [pltpu.make_async_remote_copy(
            src_ref=src, dst_ref=dst, send_sem=send_sems.at[k], recv_sem=recv_sems.at[k],
            device_id=(x, y, 1 - c), device_id_type=MESH) for k, (src, dst) in enumerate(pairs)]

    def start(ins, outs, sems):
        for cp in copies(ins, outs, sems):
            cp.start()

    def finish(ins, outs, sems):
        for cp in copies(ins, outs, sems):
            cp.wait()

    out_shape = [SDS((4,) + a.shape[2:], a.dtype) for a, _ in big] + [SDS(a.shape, a.dtype) for a in small]
    sems = [pltpu.SemaphoreType.DMA((n_copies,)), pltpu.SemaphoreType.DMA((n_copies,))]
    return _Carried([a for a, _ in big] + list(small), out_shape, sems, start, finish)


def _chips_plan(big, small):
    n, n_small = len(big), len(small)
    max_rows = 512
    parts = [max(1, a.shape[1] // max_rows) for a in big]
    n_copies = 3 * (sum(parts) + n_small)

    def copies(ins, outs, sems, landing):
        send_sems, recv_sems, local_sems = sems
        x, y, c = _my_place()
        my_chip = 2 * x + y
        chips = [(1 - x, y), (x, 1 - y), (1 - x, 1 - y)]
        remote, local = [], []
        for chip in chips:
            to = 2 * chip[0] + chip[1]
            slot = to if landing else my_chip
            pairs = []
            for t in range(n):
                rows_per = big[t].shape[1] // parts[t]
                for p in range(parts[t]):
                    rows = pl.ds(p * rows_per, rows_per)
                    pairs.append((ins[t].at[to, rows], outs[t].at[slot, rows]))
            pairs += [(ins[t], outs[t].at[slot]) for t in range(n, n + n_small)]
            for src, dst in pairs:
                k = len(remote)
                remote.append(pltpu.make_async_remote_copy(
                    src_ref=src, dst_ref=dst, send_sem=send_sems.at[k], recv_sem=recv_sems.at[k],
                    device_id=(*chip, c), device_id_type=MESH))
        for t in range(n):
            local.append(pltpu.make_async_copy(ins[t].at[my_chip], outs[t].at[my_chip], local_sems.at[t]))
        for t in range(n, n + n_small):
            local.append(pltpu.make_async_copy(ins[t], outs[t].at[my_chip], local_sems.at[t]))
        return remote + local

    def start(ins, outs, sems):
        for cp in copies(ins, outs, sems, landing=False):
            cp.start()

    def finish(ins, outs, sems):
        for cp in copies(ins, outs, sems, landing=True):
            cp.wait()

    out_shape = [SDS(a.shape, a.dtype) for a in big] + [SDS((N_CHIP,) + a.shape, a.dtype) for a in small]
    sems = [pltpu.SemaphoreType.DMA((n_copies,)), pltpu.SemaphoreType.DMA((n_copies,)),
            pltpu.SemaphoreType.DMA((n + n_small,))]
    return _Carried(list(big) + list(small), out_shape, sems, start, finish)


def _all_plan(small):
    n = len(small)
    masks = [(m >> 2 & 1, m >> 1 & 1, m & 1) for m in range(1, N_DEV)]

    def copies(ins, outs, sems, landing):
        send_sems, recv_sems, local_sems = sems
        x, y, c = _my_place()
        me = 4 * x + 2 * y + c
        flip = lambda v, bit: 1 - v if bit else v
        remote = []
        for fx, fy, fc in masks:
            peer = (flip(x, fx), flip(y, fy), flip(c, fc))
            slot = 4 * peer[0] + 2 * peer[1] + peer[2] if landing else me
            for t in range(n):
                k = len(remote)
                remote.append(pltpu.make_async_remote_copy(
                    src_ref=ins[t], dst_ref=outs[t].at[slot], send_sem=send_sems.at[k], recv_sem=recv_sems.at[k],
                    device_id=peer, device_id_type=MESH))
        local = [pltpu.make_async_copy(ins[t], outs[t].at[me], local_sems.at[t]) for t in range(n)]
        return remote + local

    def start(ins, outs, sems):
        for cp in copies(ins, outs, sems, landing=False):
            cp.start()

    def finish(ins, outs, sems):
        for cp in copies(ins, outs, sems, landing=True):
            cp.wait()

    out_shape = [SDS((N_DEV,) + a.shape, a.dtype) for a in small]
    sems = [pltpu.SemaphoreType.DMA((7 * n,)), pltpu.SemaphoreType.DMA((7 * n,)), pltpu.SemaphoreType.DMA((n,))]
    return _Carried(list(small), out_shape, sems, start, finish)


def _join(*plans):
    plans = [p for p in plans if p is not None]
    if len(plans) <= 1:
        return plans[0] if plans else None

    def each(fn_name, ins, outs, sems):
        i = o = s = 0
        for p in plans:
            ni, no, ns = len(p.ins), len(p.out_shape), len(p.sems)
            getattr(p, fn_name)(ins[i:i + ni], outs[o:o + no], sems[s:s + ns])
            i, o, s = i + ni, o + no, s + ns

    return _Carried(sum((p.ins for p in plans), []), sum((p.out_shape for p in plans), []),
                    sum((p.sems for p in plans), []),
                    lambda i, o, s: each("start", i, o, s), lambda i, o, s: each("finish", i, o, s),
                    lambda i, o, s: each("relay", i, o, s))


def _row_block(rows, most=256):
    return min(rows, most)


def _add_own(tag, core, gs, layer, gots):
    n = len(gs)

    def body(core_ref, *refs):
        for a_ref, b_ref, o_ref in zip(refs[:n], refs[n:2 * n], refs[2 * n:]):
            o_ref[...] = (a_ref[...] + b_ref[...]).astype(o_ref.dtype)

    mine = lambda a: pl.BlockSpec((None, None) + a.shape[1:], lambda s, core: (layer, 4 * core[0] + s, 0, 0))
    theirs = lambda a: pl.BlockSpec((None,) + a.shape[1:], lambda s, core: (s, 0, 0))
    return pl.pallas_call(
        body, name=f"add_{tag}", out_shape=tuple(SDS(a.shape, BF16) for a in gots),
        grid_spec=pltpu.PrefetchScalarGridSpec(
            num_scalar_prefetch=1, grid=(4,),
            in_specs=[mine(a) for a in gots] + [theirs(a) for a in gots],
            out_specs=tuple(theirs(a) for a in gots)),
        compiler_params=_cp(("parallel",)),
    )(core, *gs, *gots)


def _add_lists(tag, own, got, grid=None, specs=None, dtype=F32):
    n = len(own)

    def body(*refs):
        for a, b, o in zip(refs[:n], refs[n:2 * n], refs[2 * n:]):
            o[...] = (a[...] + b[...]).astype(o.dtype)

    kw = {}
    if grid is not None:
        kw = dict(grid=grid, in_specs=list(specs) * 2, out_specs=tuple(specs),
                  compiler_params=_cp(("parallel",) * len(grid)))
    return pl.pallas_call(
        body, name=f"add_{tag}", out_shape=tuple(SDS(a.shape, dtype) for a in own), **kw)(*own, *got)


def _adamw_math(w, g, m, v):
    m = ADAM_B1 * m + (1.0 - ADAM_B1) * g
    v = ADAM_B2 * v + (1.0 - ADAM_B2) * (g * g)
    m_hat = m / (1.0 - ADAM_B1 ** ADAM_STEP)
    v_hat = v / (1.0 - ADAM_B2 ** ADAM_STEP)
    delta = -ADAM_LR * (m_hat / (jnp.sqrt(v_hat) + ADAM_EPS) + ADAM_WD * w)
    return delta, m, v


def _sum_slots_adamw(tag, slots, w, m, v):
    _, r, c = slots[0].shape
    rb = _row_block(r, most=512)

    def body(s0_ref, s1_ref, w_ref, m_ref, v_ref, g_ref, d_ref, nm_ref, nv_ref):
        first = pl.program_id(1) == 0
        g = _pair_sum([jnp.where(first, s0_ref[k], s1_ref[k]).astype(F32) for k in range(N_CHIP)])
        delta, nm, nv = _adamw_math(w_ref[...], g, m_ref[...], v_ref[...])
        g_ref[...] = g
        d_ref[...] = delta
        nm_ref[...] = nm
        nv_ref[...] = nv

    spec = pl.BlockSpec((None, rb, c), lambda j, l: (l, j, 0))
    sspec = pl.BlockSpec((N_CHIP, rb, c), lambda j, l: (0, j, 0))
    s = SDS((DEPTH, r, c), F32)
    return pl.pallas_call(
        body, name=f"adamw_{tag}", out_shape=(s, s, s, s),
        grid=(r // rb, DEPTH), in_specs=[sspec, sspec, spec, spec, spec], out_specs=(spec, spec, spec, spec),
        compiler_params=_cp(("parallel", "arbitrary")),
    )(*slots, w, m, v)


def _adamw_small(tag, entries, grid=None, sums=()):
    flat_in, in_specs, out_shape, out_specs, layout = [], [], [], [], []
    for slots, w, m, v, slot_spec, w_spec in entries:
        per_layer = isinstance(slots, (list, tuple))
        n_slot = len(slots) if per_layer else 1
        flat_in += (list(slots) if per_layer else [slots]) + [w, m, v]
        in_specs += [slot_spec] * n_slot + [w_spec] * 3
        out_shape += [SDS(w.shape, F32)] * 4
        out_specs += [w_spec] * 4
        layout.append((per_layer, n_slot))
    n_entry_in = len(flat_in)
    flat_in += list(sums)
    out_shape += [SDS(s.shape[1:], F32) for s in sums]
    n_in = len(flat_in)

    def body(*refs):
        for s_ref, o_ref in zip(refs[n_entry_in:n_in], refs[len(refs) - len(sums):]):
            o_ref[...] = _sum_slots(s_ref)
        i, o = 0, n_in
        for per_layer, n_slot in layout:
            s_refs = refs[i:i + n_slot]
            w_ref, m_ref, v_ref = refs[i + n_slot:i + n_slot + 3]
            outs = refs[o:o + 4]
            if per_layer:
                for l, s_ref in enumerate(s_refs):
                    at = (slice(l, l + 1),) if len(w_ref.shape) == 2 else (l,)
                    g = _sum_slots(s_ref)
                    res = (g,) + _adamw_math(w_ref[at], g, m_ref[at], v_ref[at])
                    for o_ref, val in zip(outs, res):
                        o_ref[at] = val
            else:
                g = _sum_slots(s_refs[0])
                res = (g,) + _adamw_math(w_ref[...], g, m_ref[...], v_ref[...])
                for o_ref, val in zip(outs, res):
                    o_ref[...] = val
            i += n_slot + 3
            o += 4

    kw = {}
    if grid is not None:
        kw = dict(grid=grid, in_specs=in_specs, out_specs=tuple(out_specs),
                  compiler_params=_cp(("parallel",) * len(grid)))
    res = pl.pallas_call(body, name=f"adamw_{tag}", out_shape=tuple(out_shape), **kw)(*flat_in)
    return [tuple(res[4 * e:4 * e + 4]) for e in range(len(entries))], res[4 * len(entries):]


def kernel(x, norm_g, w_in, b_in, ssm_log_dt, ssm_lam_re, ssm_lam_im, ssm_b_re, ssm_b_im, ssm_c_re, ssm_c_im, ssm_d, ssm_w_glu, ssm_b_glu, pool_w, pool_scale, w_branch_a, w_branch_b, w_out, final_norm_g, loss_target, m_norm_g, m_w_in, m_b_in, m_ssm_log_dt, m_ssm_lam_re, m_ssm_lam_im, m_ssm_b_re, m_ssm_b_im, m_ssm_c_re, m_ssm_c_im, m_ssm_d, m_ssm_w_glu, m_ssm_b_glu, m_pool_w, m_pool_scale, m_w_branch_a, m_w_branch_b, m_w_out, m_final_norm_g, v_norm_g, v_w_in, v_b_in, v_ssm_log_dt, v_ssm_lam_re, v_ssm_lam_im, v_ssm_b_re, v_ssm_b_im, v_ssm_c_re, v_ssm_c_im, v_ssm_d, v_ssm_w_glu, v_ssm_b_glu, v_pool_w, v_pool_scale, v_w_branch_a, v_w_branch_b, v_w_out, v_final_norm_g):
    weights = dict(norm_g=norm_g, w_in=w_in, b_in=b_in, ssm_log_dt=ssm_log_dt, ssm_lam_re=ssm_lam_re,
                   ssm_lam_im=ssm_lam_im, ssm_b_re=ssm_b_re, ssm_b_im=ssm_b_im, ssm_c_re=ssm_c_re,
                   ssm_c_im=ssm_c_im, ssm_d=ssm_d, ssm_w_glu=ssm_w_glu, ssm_b_glu=ssm_b_glu, pool_w=pool_w,
                   pool_scale=pool_scale, w_branch_a=w_branch_a, w_branch_b=w_branch_b, w_out=w_out,
                   final_norm_g=final_norm_g.reshape(1, D_MODEL))
    mom_m = dict(norm_g=m_norm_g, w_in=m_w_in, b_in=m_b_in, ssm_log_dt=m_ssm_log_dt, ssm_lam_re=m_ssm_lam_re,
                 ssm_lam_im=m_ssm_lam_im, ssm_b_re=m_ssm_b_re, ssm_b_im=m_ssm_b_im, ssm_c_re=m_ssm_c_re,
                 ssm_c_im=m_ssm_c_im, ssm_d=m_ssm_d, ssm_w_glu=m_ssm_w_glu, ssm_b_glu=m_ssm_b_glu,
                 pool_w=m_pool_w, pool_scale=m_pool_scale, w_branch_a=m_w_branch_a, w_branch_b=m_w_branch_b,
                 w_out=m_w_out, final_norm_g=m_final_norm_g.reshape(1, D_MODEL))
    mom_v = dict(norm_g=v_norm_g, w_in=v_w_in, b_in=v_b_in, ssm_log_dt=v_ssm_log_dt, ssm_lam_re=v_ssm_lam_re,
                 ssm_lam_im=v_ssm_lam_im, ssm_b_re=v_ssm_b_re, ssm_b_im=v_ssm_b_im, ssm_c_re=v_ssm_c_re,
                 ssm_c_im=v_ssm_c_im, ssm_d=v_ssm_d, ssm_w_glu=v_ssm_w_glu, ssm_b_glu=v_ssm_b_glu,
                 pool_w=v_pool_w, pool_scale=v_pool_scale, w_branch_a=v_w_branch_a, w_branch_b=v_w_branch_b,
                 w_out=v_w_out, final_norm_g=v_final_norm_g.reshape(1, D_MODEL))
    order = ["norm_g", "w_in", "b_in", "ssm_log_dt", "ssm_lam_re", "ssm_lam_im", "ssm_b_re", "ssm_b_im",
             "ssm_c_re", "ssm_c_im", "ssm_d", "ssm_w_glu", "ssm_b_glu", "pool_w", "pool_scale", "w_branch_a",
             "w_branch_b", "w_out", "final_norm_g"]
    big_names = ["w_in", "ssm_w_glu", "w_branch_a", "w_branch_b", "w_out"]

    log_dt3 = ssm_log_dt.reshape(DEPTH, N_GROUP, 1)
    b_t = lambda a: a.transpose(0, 1, 3, 2)
    for d in (weights, mom_m, mom_v):
        d["ssm_b_re"], d["ssm_b_im"] = b_t(d["ssm_b_re"]), b_t(d["ssm_b_im"])
    bt_re, bt_im = weights["ssm_b_re"], weights["ssm_b_im"]
    abar_re, abar_im, bbt_re, bbt_im = _s5_params(log_dt3, ssm_lam_re, ssm_lam_im, bt_re, bt_im)
    s5_args = (bbt_re, bbt_im, ssm_c_re, ssm_c_im, abar_re, abar_im, ssm_d)

    w16 = {n: weights[n].astype(BF16) for n in big_names}
    rest = [w16[n] for n in big_names[1:]]
    half = D_MODEL // 2
    wg_in = [None, [None, None]]
    wg_rest = [None, None]
    wg_in[0] = list(_run_carried("gather_w_in_l0", _gather_plan([w16["w_in"]], 0)))
    xs = [x.reshape(SEQ, D_MODEL)]
    saved = []
    for l in range(DEPTH):
        proj, moved = _norm_proj(l, xs[l], norm_g, wg_in[l], b_in,
                                 carry=_gather_plan([w16["w_in"]], 1, rows_of=(0, half)) if l == 0 else None)
        if l == 0:
            (wg_in[1][0],) = moved
        (states, y0), wg_rest[l] = _s5_scan_fwd(l, proj, *s5_args, carry=_gather_plan(rest, l, by_columns=(1, 2)))
        pooled = _pool_fwd(l, proj)
        wg_glu, wg_a, wg_b, wg_out = wg_rest[l]
        last = l == DEPTH - 1
        res, moved = _mix_fwd(
            l, xs[l], proj, y0, pooled, wg_glu, ssm_b_glu, pool_w, pool_scale, wg_a, wg_b, wg_out,
            carry=_gather_plan([w16["w_in"]], 1, rows_of=(half, half)) if l == 0 else None,
            head=(loss_target.reshape(SEQ, D_MODEL), weights["final_norm_g"]) if last else None)
        if l == 0:
            (wg_in[1][1],) = moved
        if last:
            dx, loss_part, g_final = res
        else:
            xs.append(res[0])
        saved.append((proj, states, y0, pooled))

    core = lax.axis_index("c").astype(jnp.int32).reshape(1)
    vec_names = ["norm_g", "b_in", "ssm_d", "ssm_b_glu", "pool_scale", "ssm_log_dt"]
    s5_names = ["ssm_log_dt", "ssm_lam_re", "ssm_lam_im", "ssm_b_re", "ssm_b_im"]
    mat_names = ["pool_w", "ssm_c_re", "ssm_c_im", "ssm_b_re", "ssm_b_im"]
    lane_sparse = ("ssm_c_re", "ssm_c_im", "ssm_b_re", "ssm_b_im")

    def dense(key, a):
        return a.reshape(-1, LANES) if key[0] in lane_sparse else a

    def undense(key, slots):
        return slots.reshape((N_CHIP, N_GROUP, GROUP_W, STATE)) if key[0] in lane_sparse else slots

    def add_small(tag, keys, own, got):
        out = [None] * len(keys)
        whole = [i for i, k in enumerate(keys) if k[0] not in mat_names]
        tiled = [i for i, k in enumerate(keys) if k[0] in mat_names]
        if whole:
            for i, r in zip(whole, _add_lists(f"{tag}_a", [own[i] for i in whole], [got[i] for i in whole])):
                out[i] = r
        if tiled:
            specs = [pl.BlockSpec((1, POOL_GROUP, POOL_GROUP), lambda j: (j, 0, 0)) if keys[i][0] == "pool_w"
                     else pl.BlockSpec((own[i].shape[0] // N_CHUNK, LANES), lambda j: (j, 0)) for i in tiled]
            for i, r in zip(tiled, _add_lists(f"{tag}_b", [own[i] for i in tiled], [got[i] for i in tiled],
                                              grid=(N_CHUNK,), specs=specs, dtype=BF16)):
                out[i] = r
        return out

    sm = {("final_norm_g", None): g_final, ("loss", None): loss_part}
    slots = {}
    grads = dict.fromkeys(big_names)

    class Wave:
        def __init__(self, tag, layer, big, keys):
            self.tag, self.layer, self.big, self.keys = tag, layer, big, keys

        def to_sibling(self):
            self.own = [dense(k, sm[k]) for k in self.keys]
            return _sibling_plan([(grads[n], self.layer) for n in self.big], self.own)

        def add(self, moved):
            nb = len(self.big)
            self.chip_big = list(_add_own(self.tag, core, [grads[n] for n in self.big], self.layer, moved[:nb])
                                 ) if nb else []
            self.chip_small = add_small(self.tag, self.keys, self.own, moved[nb:])

        def to_chips(self, big=None, small=True):
            self.sent = list(self.big if big is None else big), small
            return _chips_plan([self.chip_big[self.big.index(n)] for n in self.sent[0]],
                               self.chip_small if small else [])

        def landed(self, moved):
            names, small = self.sent
            for n, s in zip(names, moved[:len(names)]):
                slots[(n, self.layer)] = s
            if small:
                for k, s in zip(self.keys, moved[len(names):]):
                    slots[k] = undense(k, s)
            return moved[len(names) + (len(self.keys) if small else 0):]

    def s5_param_grads(l, g_abar_re, g_abar_im, g_bbt_re, g_bbt_im):
        g = _s5_params_bwd(l, log_dt3, ssm_lam_re, ssm_lam_im, bt_re, bt_im, g_abar_re, g_abar_im, g_bbt_re, g_bbt_im)
        sm[("ssm_log_dt", l)] = g[0].reshape(1, N_GROUP)
        for n, a in zip(s5_names[1:], g[1:]):
            sm[(n, l)] = a

    small1 = ["b_in", "ssm_d", "ssm_b_glu", "pool_scale", "pool_w", "ssm_c_re", "ssm_c_im"] + s5_names
    w1 = Wave("chip1", 1, list(big_names), [(n, 1) for n in small1] + [("final_norm_g", None), ("loss", None)])
    early = Wave("chip0e", 0, big_names[1:], [("pool_w", 0), ("pool_scale", 0), ("ssm_b_glu", 0)])
    mid = Wave("chip0m", 0, [], [(n, 0) for n in ["ssm_c_re", "ssm_c_im", "ssm_d"] + s5_names] + [("norm_g", 1)])
    late = Wave("chip0l", 0, ["w_in"], [("b_in", 0)])

    mix_prev, gw_in = None, None
    for l in reversed(range(DEPTH)):
        proj, states, y0, pooled = saved[l]
        wg_glu, wg_a, wg_b, wg_out = wg_rest[l]
        res, moved = _mix_bwd(l, dx, proj, y0, pooled, wg_glu, ssm_b_glu, pool_w, pool_scale, wg_a, wg_b, wg_out,
                              mix_prev, carry=None if l == 1 else w1.to_chips(big=["w_in"], small=False))
        if l == 0:
            w1.landed(moved)
        dproj, dy0, dpooled = res[:3]
        mix_prev = list(res[3:7])
        grads["w_out"], grads["w_branch_a"], grads["w_branch_b"], grads["ssm_w_glu"] = mix_prev
        sm[("pool_w", l)], sm[("pool_scale", l)], sm[("ssm_b_glu", l)] = res[7:]
        dproj = _pool_bwd(l, dpooled, dproj)
        carry = None if l == 1 else _join(w1.to_chips(big=big_names[1:]), early.to_sibling())
        res, moved = _s5_scan_bwd(l, dy0, proj, states, *s5_args, dproj, carry=carry)
        if l == 0:
            early.add(w1.landed(moved))
        dproj, g_bbt_re, g_bbt_im, sm[("ssm_c_re", l)], sm[("ssm_c_im", l)], g_abar_re, g_abar_im, sm[("ssm_d", l)] = res
        s5_param_grads(l, g_abar_re, g_abar_im, g_bbt_re, g_bbt_im)
        carry = None if l == 1 else _join(early.to_chips(), mid.to_sibling())
        (gw_in, sm[("b_in", l)]), moved = _proj_wgrad(l, xs[l], norm_g, dproj, gw_in, carry=carry)
        grads["w_in"] = gw_in
        if l == 0:
            mid.add(early.landed(moved))
        carry = w1.to_sibling() if l == 1 else _join(mid.to_chips(), late.to_sibling())
        (dx, sm[("norm_g", l)]), moved = _proj_dgrad(l, dx, xs[l], norm_g, dproj, wg_in[l], carry=carry)
        if l == 1:
            w1.add(moved)
        else:
            late.add(mid.landed(moved))
    grad_x = dx.reshape(1, SEQ, D_MODEL)
    moved = late.landed(_run_carried("exchange_last", _join(late.to_chips(), _all_plan([sm[("norm_g", 0)]]))))
    slots[("norm_g", 0)] = moved[0]

    res = {}
    for n in big_names:
        res[n] = _sum_slots_adamw(n, [slots[(n, l)] for l in range(DEPTH)], weights[n], mom_m[n], mom_v[n])
    per_layer = lambda n: [slots[(n, l)] for l in range(DEPTH)]
    names_a = vec_names + ["ssm_lam_re", "ssm_lam_im"]
    entries_a = [(per_layer(n), weights[n], mom_m[n], mom_v[n], None, None) for n in names_a]
    n = "final_norm_g"
    entries_a.append((slots[(n, None)], weights[n], mom_m[n], mom_v[n], None, None))
    out_a, (loss,) = _adamw_small("small_a", entries_a, sums=[slots[("loss", None)]])
    loss = loss.reshape(())
    for n, r in zip(names_a + ["final_norm_g"], out_a):
        res[n] = r
    res["final_norm_g"] = tuple(a.reshape(D_MODEL) for a in res["final_norm_g"])
    pw_s = pl.BlockSpec((N_CHIP, 1, POOL_GROUP, POOL_GROUP), lambda j: (0, j, 0, 0))
    pw_w = pl.BlockSpec((DEPTH, 1, POOL_GROUP, POOL_GROUP), lambda j: (0, j, 0, 0))
    c_s = pl.BlockSpec((N_CHIP, CH_G, GROUP_W, STATE), lambda j: (0, j, 0, 0))
    c_w = pl.BlockSpec((DEPTH, CH_G, GROUP_W, STATE), lambda j: (0, j, 0, 0))
    entries_b = [(per_layer(n), weights[n], mom_m[n], mom_v[n], pw_s if n == "pool_w" else c_s,
                  pw_w if n == "pool_w" else c_w) for n in mat_names]
    out_b, _ = _adamw_small("small_b", entries_b, grid=(N_CHUNK,))
    for n, r in zip(mat_names, out_b):
        res[n] = tuple(b_t(a) for a in r) if n in ("ssm_b_re", "ssm_b_im") else r

    outs = [loss, grad_x]
    for i in range(4):
        outs += [res[n][i] for n in order]
    return tuple(outs)
```

```python
import math

import jax
import jax.numpy as jnp
from jax import lax
from jax.experimental import pallas as pl
from jax.experimental.pallas import tpu as pltpu

F32 = jnp.float32
BF16 = jnp.bfloat16

SEQ = 2048
D_MODEL = 1024
N_IN = 4096
WIDTH = 512
N_GROUP = 32
GROUP_W = 16
STATE = 64
N_STATE = N_GROUP * STATE
N_CHUNK = 4
CH_G = N_GROUP // N_CHUNK
CH_W = WIDTH // N_CHUNK
CH_S = N_STATE // N_CHUNK
N_DEV = 8
N_CHIP = 4
POOL_WINDOWS = (2, 4, 8, 16)
POOL_GROUP = 128
EPS = 1e-6
DEPTH = 2

ADAM_LR = 0.001
ADAM_B1 = 0.9
ADAM_B2 = 0.999
ADAM_EPS = 1e-08
ADAM_WD = 0.01
ADAM_STEP = 10

LANES = 128
SUBLANES = 8
TILE_M = 256
VMEM_LIMIT = 48 * 1024 * 1024
VMEM_LIMIT_BIG = 60 * 1024 * 1024
MESH = pl.DeviceIdType.MESH
ANY = pl.BlockSpec(memory_space=pl.ANY)

GELU_C = math.sqrt(2.0 / math.pi)
GELU_A = 0.044715

SDS = jax.ShapeDtypeStruct


def _cp(sem=None, limit=VMEM_LIMIT):
    return pltpu.CompilerParams(dimension_semantics=sem, vmem_limit_bytes=limit)


def _dot(a, b):
    return jnp.dot(a, b, preferred_element_type=F32)


def _dot_nt(a, b):
    return lax.dot_general(a, b, (((1,), (1,)), ((), ())), preferred_element_type=F32)


def _dot_tn(a, b):
    return lax.dot_general(a, b, (((0,), (0,)), ((), ())), preferred_element_type=F32)


def _sig(x):
    return jax.nn.sigmoid(x)


def _rms(x):
    rs = lax.rsqrt(jnp.mean(x * x, axis=-1, keepdims=True) + EPS)
    return rs, x * rs


def _slot(n):
    return 4 * (n % 2) + n // 2


def _const(shape):
    n = len(shape)
    return pl.BlockSpec(shape, lambda *_: (0,) * n)


def _pair_sum(vals):
    while len(vals) > 1:
        vals = [vals[i] + vals[i + 1] for i in range(0, len(vals), 2)]
    return vals[0]


def _sum_slots(s_ref):
    return _pair_sum([s_ref[k].astype(F32) for k in range(s_ref.shape[0])])


def _s5_param_fn(log_dt, lam_re, lam_im, bt_re, bt_im):
    dt = jnp.exp(log_dt)
    mag = jnp.exp(lam_re * dt)
    ang = lam_im * dt
    abar_re = mag * jnp.cos(ang)
    abar_im = mag * jnp.sin(ang)
    num_re = abar_re - 1.0
    num_im = abar_im
    den = lam_re * lam_re + lam_im * lam_im
    coef_re = (num_re * lam_re + num_im * lam_im) / den
    coef_im = (num_im * lam_re - num_re * lam_im) / den
    bbar_re = coef_re[..., None, :] * bt_re - coef_im[..., None, :] * bt_im
    bbar_im = coef_re[..., None, :] * bt_im + coef_im[..., None, :] * bt_re
    return abar_re, abar_im, bbar_re, bbar_im


def _s5_params(log_dt, lam_re, lam_im, bt_re, bt_im):
    def body(ld, lr, li, br, bi, o_ar, o_ai, o_br, o_bi):
        ar, ai, bbr, bbi = _s5_param_fn(ld[...], lr[...], li[...], br[...], bi[...])
        o_ar[...] = ar
        o_ai[...] = ai
        o_br[...] = bbr
        o_bi[...] = bbi

    return pl.pallas_call(
        body, name="s5_params",
        out_shape=(SDS(lam_re.shape, F32), SDS(lam_re.shape, F32), SDS(bt_re.shape, F32), SDS(bt_re.shape, F32)),
    )(log_dt, lam_re, lam_im, bt_re, bt_im)


def _s5_params_bwd(layer, log_dt, lam_re, lam_im, bt_re, bt_im, g_ar, g_ai, g_br, g_bi):
    def body(ld, lr, li, br, bi, car, cai, cbr, cbi, o_ld, o_lr, o_li, o_br, o_bi):
        _, vjp = jax.vjp(_s5_param_fn, ld[...], lr[...], li[...], br[...], bi[...])
        d_ld, d_lr, d_li, d_br, d_bi = vjp((car[...], cai[...], cbr[...], cbi[...]))
        o_ld[...] = d_ld
        o_lr[...] = d_lr
        o_li[...] = d_li
        o_br[...] = d_br
        o_bi[...] = d_bi

    one = lambda shape: pl.BlockSpec((None,) + shape, lambda i: (layer,) + (0,) * len(shape))
    whole = lambda shape: _const(shape)
    vec, lam, mat = (N_GROUP, 1), (N_GROUP, STATE), (N_GROUP, GROUP_W, STATE)
    return pl.pallas_call(
        body, name=f"s5_params_bwd_l{layer}", grid=(1,),
        in_specs=[one(vec), one(lam), one(lam), one(mat), one(mat), whole(lam), whole(lam), whole(mat), whole(mat)],
        out_specs=(whole(vec), whole(lam), whole(lam), whole(mat), whole(mat)),
        out_shape=(SDS(vec, F32), SDS(lam, F32), SDS(lam, F32), SDS(mat, F32), SDS(mat, F32)),
    )(log_dt, lam_re, lam_im, bt_re, bt_im, g_ar, g_ai, g_br, g_bi)


def _norm_proj(layer, x, norm_g, wg_in, b_in, carry=None):
    n_w = len(wg_in)

    def body(x_ref, g_ref, b_ref, *refs):
        w_refs, o_ref = refs[:n_w], refs[n_w]
        _, xn = _rms(x_ref[...])
        h = (xn * g_ref[layer:layer + 1, :]).astype(BF16)
        for k in range(N_DEV):
            cols = slice(k * WIDTH, (k + 1) * WIDTH)
            acc = b_ref[layer:layer + 1, cols]
            row = 0
            for w_ref in w_refs:
                rows = w_ref.shape[1]
                acc = acc + _dot(h[:, row:row + rows], w_ref[k])
                row += rows
            o_ref[:, cols] = acc

    (proj,), moved = _pcall(
        body, name=f"norm_proj_l{layer}",
        out_shape=[SDS((SEQ, N_IN), F32)],
        grid=(SEQ // TILE_M,),
        in_specs=[pl.BlockSpec((TILE_M, D_MODEL), lambda i: (i, 0)),
                  _const((DEPTH, D_MODEL)),
                  _const((DEPTH, N_IN))] + [_const(w.shape) for w in wg_in],
        out_specs=[pl.BlockSpec((TILE_M, N_IN), lambda i: (i, 0))],
        args=[x, norm_g, b_in, *wg_in], sem=("parallel",), carry=carry)
    return proj, moved


TIME_BLK = 512
N_TBLK = SEQ // TIME_BLK
N_PANEL = CH_S // LANES
STATE_SHAPE = (N_PANEL, SEQ * SUBLANES, LANES)


def _s5_layer_specs(layer):
    mat = lambda: pl.BlockSpec((None, N_GROUP, GROUP_W, STATE), lambda i: (layer, 0, 0, 0))
    ab = lambda: pl.BlockSpec((None, N_GROUP, STATE), lambda i: (layer, 0, 0))
    return [mat(), mat(), mat(), mat(), ab(), ab(), _const((DEPTH, WIDTH))]


def _s5_layer_scratch():
    return [pltpu.VMEM((N_CHUNK, CH_W, CH_S), BF16)] * 4 + [pltpu.VMEM((8, CH_S), F32)] * 2


def _s5_layer_fill(btre_ref, btim_ref, cre_ref, cim_ref, are_ref, aim_ref, bdre, bdim, ctre, ctim, a1, a2):
    for m in (bdre, bdim, ctre, ctim):
        m[...] = jnp.zeros_like(m)
    for grp in range(N_GROUP):
        k, g = divmod(grp, CH_G)
        rows = slice(g * GROUP_W, (g + 1) * GROUP_W)
        cols = slice(g * STATE, (g + 1) * STATE)
        bdre[k, rows, cols] = btre_ref[grp].astype(BF16)
        bdim[k, rows, cols] = btim_ref[grp].astype(BF16)
        ctre[k, rows, cols] = cre_ref[grp].astype(BF16)
        ctim[k, rows, cols] = cim_ref[grp].astype(BF16)
        ar = are_ref[grp:grp + 1, :]
        ai = aim_ref[grp:grp + 1, :]
        a1[k:k + 1, cols] = ar
        a1[N_CHUNK + k:N_CHUNK + k + 1, cols] = ar
        a2[k:k + 1, cols] = -ai
        a2[N_CHUNK + k:N_CHUNK + k + 1, cols] = ai


SCAN_UNROLL = 16


def _panels(tile):
    return [tile[:, p * LANES:(p + 1) * LANES] for p in range(N_PANEL)]


def _rows_load(ref, row):
    return jnp.concatenate([ref[p, pl.ds(row, TIME_BLK, stride=SUBLANES), :] for p in range(N_PANEL)], axis=1)


def _rows_store(ref, row, val):
    for p in range(N_PANEL):
        ref[p, pl.ds(row, TIME_BLK, stride=SUBLANES), :] = val[:, p * LANES:(p + 1) * LANES]


def _s5_scan_fwd(layer, proj, bbt_re, bbt_im, c_re, c_im, abar_re, abar_im, d_skip, carry=None):
    def body(u_ref, btre_ref, btim_ref, cre_ref, cim_ref, are_ref, aim_ref, d_ref, s_ref, y_ref,
             bdre, bdim, ctre, ctim, a1, a2, state):
        @pl.when(pl.program_id(0) == 0)
        def _():
            _s5_layer_fill(btre_ref, btim_ref, cre_ref, cim_ref, are_ref, aim_ref, bdre, bdim, ctre, ctim, a1, a2)
            state[...] = jnp.zeros_like(state)

        for k in range(N_CHUNK):
            ub = u_ref[:, k * CH_W:(k + 1) * CH_W].astype(BF16)
            _rows_store(s_ref, k, _dot(ub, bdre[k]))
            _rows_store(s_ref, N_CHUNK + k, _dot(ub, bdim[k]))
        m1 = _panels(a1[...])
        m2 = _panels(a2[...])

        def steps(n, tile):
            for r in range(SCAN_UNROLL):
                rows = pl.ds(pl.multiple_of((n * SCAN_UNROLL + r) * 8, 8), 8)
                tile = [m1[p] * tile[p] + m2[p] * pltpu.roll(tile[p], N_CHUNK, 0) + s_ref[p, rows, :]
                        for p in range(N_PANEL)]
                for p in range(N_PANEL):
                    s_ref[p, rows, :] = tile[p]
            return tile

        tile = lax.fori_loop(0, TIME_BLK // SCAN_UNROLL, steps, _panels(state[...]))
        state[...] = jnp.concatenate(tile, axis=1)
        d = d_ref[layer:layer + 1, :]
        for k in range(N_CHUNK):
            cols = slice(k * CH_W, (k + 1) * CH_W)
            y = (_dot_nt(_rows_load(s_ref, k).astype(BF16), ctre[k])
                 - _dot_nt(_rows_load(s_ref, N_CHUNK + k).astype(BF16), ctim[k]))
            y_ref[:, cols] = y + d[:, cols] * u_ref[:, cols]

    return _pcall(
        body, name=f"s5_fwd_l{layer}",
        out_shape=(SDS(STATE_SHAPE, F32), SDS((SEQ, WIDTH), F32)),
        grid=(N_TBLK,),
        in_specs=[pl.BlockSpec((TIME_BLK, WIDTH), lambda i: (i, 0))] + _s5_layer_specs(layer),
        out_specs=(pl.BlockSpec((N_PANEL, TIME_BLK * SUBLANES, LANES), lambda i: (0, i, 0)),
                   pl.BlockSpec((TIME_BLK, WIDTH), lambda i: (i, 0))),
        scratch_shapes=_s5_layer_scratch() + [pltpu.VMEM((8, CH_S), F32)],
        args=[proj, bbt_re, bbt_im, c_re, c_im, abar_re, abar_im, d_skip], sem=("arbitrary",), carry=carry)


def _s5_scan_bwd(layer, dy0, proj, states, bbt_re, bbt_im, c_re, c_im, abar_re, abar_im, d_skip, dproj,
                 carry=None):
    def body(dy_ref, u_ref, s_ref, sprev_ref, btre_ref, btim_ref, cre_ref, cim_ref, are_ref, aim_ref, d_ref, _,
             du_ref, gbre_ref, gbim_ref, gcre_ref, gcim_ref, gare_ref, gaim_ref, gd_ref,
             lam_ref, bdre, bdim, ctre, ctim, a1, a2, state, acc1, acc2, gbre, gbim, gcre, gcim, gd):
        step_id = pl.program_id(0)

        @pl.when(step_id == 0)
        def _():
            _s5_layer_fill(btre_ref, btim_ref, cre_ref, cim_ref, are_ref, aim_ref, bdre, bdim, ctre, ctim, a1, a2)
            for r in (state, acc1, acc2, gbre, gbim, gcre, gcim, gd):
                r[...] = jnp.zeros_like(r)

        for k in range(N_CHUNK):
            dyb = dy_ref[:, k * CH_W:(k + 1) * CH_W].astype(BF16)
            _rows_store(lam_ref, k, _dot(dyb, ctre[k]))
            _rows_store(lam_ref, N_CHUNK + k, -_dot(dyb, ctim[k]))
            gcre[k] += _dot_tn(dyb, _rows_load(s_ref, k).astype(BF16))
            gcim[k] -= _dot_tn(dyb, _rows_load(s_ref, N_CHUNK + k).astype(BF16))

        m1 = _panels(a1[...])
        m2 = _panels(-a2[...])
        has_before = (step_id < N_TBLK - 1).astype(F32)

        def one(t8, c, first_token):
            tile, swapped, p1, p2 = c
            rows = pl.ds(t8, 8)
            tile = [m1[p] * tile[p] + m2[p] * swapped[p] + lam_ref[p, rows, :] for p in range(N_PANEL)]
            swapped = [pltpu.roll(tile[p], N_CHUNK, 0) for p in range(N_PANEL)]
            for p in range(N_PANEL):
                lam_ref[p, rows, :] = tile[p]
            if first_token:
                before = [sprev_ref[p] * has_before for p in range(N_PANEL)]
            else:
                before = [s_ref[p, pl.ds(t8 - 8, 8), :] for p in range(N_PANEL)]
            p1 = [p1[p] + tile[p] * before[p] for p in range(N_PANEL)]
            p2 = [p2[p] + swapped[p] * before[p] for p in range(N_PANEL)]
            return tile, swapped, p1, p2

        def steps(n, c):
            for r in range(SCAN_UNROLL):
                t8 = pl.multiple_of((TIME_BLK - 1 - (n * SCAN_UNROLL + r)) * 8, 8)
                c = one(t8, c, False)
            return c

        tile0 = _panels(state[...])
        c = (tile0, [pltpu.roll(t, N_CHUNK, 0) for t in tile0], _panels(acc1[...]), _panels(acc2[...]))
        c = lax.fori_loop(0, TIME_BLK // SCAN_UNROLL - 1, steps, c)
        for r in range(SCAN_UNROLL - 1, -1, -1):
            c = one(r * 8, c, r == 0)
        state[...] = jnp.concatenate(c[0], axis=1)
        acc1[...] = jnp.concatenate(c[2], axis=1)
        acc2[...] = jnp.concatenate(c[3], axis=1)

        d = d_ref[layer:layer + 1, :]
        for k in range(N_CHUNK):
            cols = slice(k * CH_W, (k + 1) * CH_W)
            lrb = _rows_load(lam_ref, k).astype(BF16)
            lib = _rows_load(lam_ref, N_CHUNK + k).astype(BF16)
            u = u_ref[:, cols]
            ub = u.astype(BF16)
            dy = dy_ref[:, cols]
            du = dy * d[:, cols] + _dot_nt(lrb, bdre[k]) + _dot_nt(lib, bdim[k])
            du_ref[:, cols] = du.astype(BF16)
            gbre[k] += _dot_tn(ub, lrb)
            gbim[k] += _dot_tn(ub, lib)
        gd[...] += jnp.sum(dy_ref[...] * u_ref[...], axis=0, keepdims=True)

        @pl.when(step_id == N_TBLK - 1)
        def _():
            gd_ref[...] = gd[...]
            ga_re = acc1[0:N_CHUNK, :] + acc1[N_CHUNK:, :]
            ga_im = acc2[0:N_CHUNK, :] - acc2[N_CHUNK:, :]
            for grp in range(N_GROUP):
                k, g = divmod(grp, CH_G)
                rows = slice(g * GROUP_W, (g + 1) * GROUP_W)
                cols = slice(g * STATE, (g + 1) * STATE)
                gcre_ref[grp] = gcre[k, rows, cols]
                gcim_ref[grp] = gcim[k, rows, cols]
                gbre_ref[grp] = gbre[k, rows, cols]
                gbim_ref[grp] = gbim[k, rows, cols]
                gare_ref[grp:grp + 1, :] = ga_re[k:k + 1, cols]
                gaim_ref[grp:grp + 1, :] = ga_im[k:k + 1, cols]

    back = lambda i: N_TBLK - 1 - i
    tok = lambda: pl.BlockSpec((TIME_BLK, WIDTH), lambda i: (back(i), 0))
    mat = lambda: _const((N_GROUP, GROUP_W, STATE))
    acc_mat = pltpu.VMEM((N_CHUNK, CH_W, CH_S), F32)
    return _pcall(
        body, name=f"s5_bwd_l{layer}",
        out_shape=(SDS((SEQ, N_IN), BF16), SDS((N_GROUP, GROUP_W, STATE), F32), SDS((N_GROUP, GROUP_W, STATE), F32),
                   SDS((N_GROUP, GROUP_W, STATE), F32), SDS((N_GROUP, GROUP_W, STATE), F32),
                   SDS((N_GROUP, STATE), F32), SDS((N_GROUP, STATE), F32), SDS((1, WIDTH), F32)),
        grid=(N_TBLK,),
        in_specs=[tok(), tok(),
                  pl.BlockSpec((N_PANEL, TIME_BLK * SUBLANES, LANES), lambda i: (0, back(i), 0)),
                  pl.BlockSpec((N_PANEL, SUBLANES, LANES), lambda i: (0, jnp.maximum(back(i) * TIME_BLK - 1, 0), 0))]
        + _s5_layer_specs(layer) + [ANY],
        out_specs=(tok(), mat(), mat(), mat(), mat(), _const((N_GROUP, STATE)), _const((N_GROUP, STATE)),
                   _const((1, WIDTH))),
        scratch_shapes=[pltpu.VMEM((N_PANEL, TIME_BLK * SUBLANES, LANES), F32)] + _s5_layer_scratch()
        + [pltpu.VMEM((8, CH_S), F32)] * 3 + [acc_mat] * 4 + [pltpu.VMEM((1, WIDTH), F32)],
        args=[dy0, proj, states, states, bbt_re, bbt_im, c_re, c_im, abar_re, abar_im, d_skip, dproj],
        aliases={11: 0}, sem=("arbitrary",), limit=VMEM_LIMIT_BIG, carry=carry)


def _pool_counts(win):
    t = lax.broadcasted_iota(jnp.int32, (SEQ, POOL_GROUP), 0)
    return t, jnp.minimum(t + 1, win).astype(F32)


def _pool_fwd(layer, proj):
    def body(u_ref, o_ref):
        for gi, win in enumerate(POOL_WINDOWS):
            cols = slice(gi * POOL_GROUP, (gi + 1) * POOL_GROUP)
            u = u_ref[:, cols]
            t, count = _pool_counts(win)
            acc = u
            k = 1
            while k < win:
                acc = acc + jnp.where(t >= k, pltpu.roll(acc, k, 0), 0.0)
                k *= 2
            o_ref[:, cols] = acc / count - u

    return pl.pallas_call(
        body, name=f"pool_fwd_l{layer}",
        out_shape=SDS((SEQ, WIDTH), F32),
        grid=(1,),
        in_specs=[pl.BlockSpec((SEQ, WIDTH), lambda i: (0, 2))],
        out_specs=pl.BlockSpec((SEQ, WIDTH), lambda i: (0, 0)),
        compiler_params=_cp(("arbitrary",)),
    )(proj)


def _gelu_parts(y0):
    t = jnp.tanh(GELU_C * (y0 + GELU_A * (y0 * y0 * y0)))
    return t, 0.5 * y0 * (1.0 + t)


def _mix_forward(layer, p_ref, y0_ref, pooled_ref, wglu_ref, bglu_ref, pw_ref, scale_ref, wa_ref, wb_ref):
    za = p_ref[:, WIDTH:2 * WIDTH]
    zb = p_ref[:, 3 * WIDTH:4 * WIDTH]
    ga = p_ref[:, 4 * WIDTH:4 * WIDTH + D_MODEL]
    gb = p_ref[:, 4 * WIDTH + D_MODEL:]
    y0 = y0_ref[...]
    t, y1 = _gelu_parts(y0)
    y1b = y1.astype(BF16)
    q = _dot(y1b, wglu_ref[...].reshape(WIDTH, WIDTH)) + bglu_ref[layer:layer + 1, :]
    sq = _sig(q)
    y2 = y1 * sq
    sza = _sig(za)
    silu_za = za * sza
    ya = y2 * silu_za
    pooled = pooled_ref[...]
    mixed = jnp.concatenate(
        [_dot(pooled[:, g * POOL_GROUP:(g + 1) * POOL_GROUP].astype(BF16), pw_ref[g].astype(BF16))
         for g in range(len(POOL_WINDOWS))], axis=1)
    szb = _sig(zb)
    silu_zb = zb * szb
    scale = scale_ref[layer:layer + 1, :]
    ms = mixed * scale
    yb = ms * silu_zb
    yab = ya.astype(BF16)
    ybb = yb.astype(BF16)
    ma = _dot(yab, wa_ref[...])
    mb = _dot(ybb, wb_ref[...])
    sga = _sig(ga)
    sgb = _sig(gb)
    merged = sga * ma + sgb * mb
    return dict(za=za, zb=zb, y0=y0, t=t, y1=y1, y1b=y1b, sq=sq, y2=y2, sza=sza, silu_za=silu_za,
                pooled=pooled, mixed=mixed, szb=szb, silu_zb=silu_zb, scale=scale, ms=ms, yab=yab, ybb=ybb,
                ma=ma, mb=mb, sga=sga, sgb=sgb, merged=merged)


def _mix_weight_specs(layer):
    return [_const((N_DEV, WIDTH // N_DEV, WIDTH)),
            _const((DEPTH, WIDTH)),
            pl.BlockSpec((None, 4, POOL_GROUP, POOL_GROUP), lambda i: (layer, 0, 0, 0)),
            _const((DEPTH, WIDTH)),
            _const((WIDTH, D_MODEL)),
            _const((WIDTH, D_MODEL)),
            _const((N_DEV, D_MODEL // N_DEV, D_MODEL))]


def _loss_head(x, t_ref, g_ref, dx_ref, loss_ref, gg_ref):
    @pl.when(pl.program_id(0) == 0)
    def _():
        loss_ref[...] = jnp.zeros_like(loss_ref)
        gg_ref[...] = jnp.zeros_like(gg_ref)

    g = g_ref[...]
    rs, xn = _rms(x)
    err = xn * g - t_ref[...]
    loss_ref[...] += 0.5 * jnp.sum(jnp.mean(err * err, axis=-1, keepdims=True), axis=0, keepdims=True)
    dy = err * (1.0 / D_MODEL)
    gg_ref[...] += jnp.sum(dy * xn, axis=0, keepdims=True)
    dxn = dy * g
    dx_ref[...] = rs * (dxn - xn * jnp.mean(dxn * xn, axis=-1, keepdims=True))


def _mix_fwd(layer, x, proj, y0, pooled, wg_glu, b_glu, pool_w, pool_scale, wg_a, wg_b, wg_out, carry=None,
             head=None):
    def body(x_ref, p_ref, y0_ref, pooled_ref, wglu_ref, bglu_ref, pw_ref, scale_ref, wa_ref, wb_ref,
             wout_ref, *rest):
        f = _mix_forward(layer, p_ref, y0_ref, pooled_ref, wglu_ref, bglu_ref, pw_ref, scale_ref, wa_ref, wb_ref)
        wout = wout_ref[...].reshape(D_MODEL, D_MODEL)
        x_next = x_ref[...] + _dot(f["merged"].astype(BF16), wout)
        if head is None:
            rest[0][...] = x_next
        else:
            _loss_head(x_next, *rest)

    tile = lambda: pl.BlockSpec((TILE_M, D_MODEL), lambda i: (i, 0))
    if head is None:
        extra, out_shape, out_specs = [], [SDS((SEQ, D_MODEL), F32)], [tile()]
    else:
        extra = list(head)
        out_shape = [SDS((SEQ, D_MODEL), F32), SDS((1, 1), F32), SDS((1, D_MODEL), F32)]
        out_specs = [tile(), _const((1, 1)), _const((1, D_MODEL))]
    return _pcall(
        body, name=f"mix_fwd_l{layer}",
        out_shape=out_shape,
        grid=(SEQ // TILE_M,),
        in_specs=[tile(),
                  pl.BlockSpec((TILE_M, N_IN), lambda i: (i, 0)),
                  pl.BlockSpec((TILE_M, WIDTH), lambda i: (i, 0)),
                  pl.BlockSpec((TILE_M, WIDTH), lambda i: (i, 0))] + _mix_weight_specs(layer)
        + ([tile(), _const((1, D_MODEL))] if head else []),
        out_specs=out_specs,
        args=[x, proj, y0, pooled, wg_glu, b_glu, pool_w, pool_scale, wg_a, wg_b, wg_out] + extra,
        sem=("parallel",) if head is None else ("arbitrary",), carry=carry)


def _big_shapes():
    return dict(w_out=(DEPTH, N_DEV, D_MODEL // N_DEV, D_MODEL), w_branch_a=(DEPTH, N_DEV, WIDTH, D_MODEL // N_DEV),
                w_branch_b=(DEPTH, N_DEV, WIDTH, D_MODEL // N_DEV), ssm_w_glu=(DEPTH, N_DEV, WIDTH // N_DEV, WIDTH),
                w_in=(DEPTH, N_DEV, D_MODEL, WIDTH))


def _mix_bwd(layer, dx_next, proj, y0, pooled, wg_glu, b_glu, pool_w, pool_scale, wg_a, wg_b, wg_out, prev,
             carry=None):
    n_k = N_DEV
    n_prev = 0 if prev is None else len(prev)

    def body(*refs):
        (dx_ref, p_ref, y0_ref, pooled_ref, wglu_ref, bglu_ref, pw_ref, scale_ref, wa_ref, wb_ref,
         wout_ref) = refs[:11]
        (dproj_ref, dy0_ref, dpooled_ref, gwout_ref, gwa_ref, gwb_ref, gwglu_ref, gpw_ref,
         gscale_ref, gbglu_ref) = refs[11 + n_prev:]

        @pl.when(pl.program_id(0) == 0)
        def _():
            for r in (gwout_ref, gwa_ref, gwb_ref, gwglu_ref, gpw_ref, gscale_ref, gbglu_ref):
                r[...] = jnp.zeros_like(r)

        f = _mix_forward(layer, p_ref, y0_ref, pooled_ref, wglu_ref, bglu_ref, pw_ref, scale_ref, wa_ref, wb_ref)
        wglu = wglu_ref[...].reshape(WIDTH, WIDTH)
        wout = wout_ref[...].reshape(D_MODEL, D_MODEL)
        blk = D_MODEL // n_k
        dxb = dx_ref[...].astype(BF16)
        dmerged = _dot_nt(dxb, wout)
        gwout = _dot_tn(f["merged"].astype(BF16), dxb)
        for k in range(n_k):
            gwout_ref[_slot(k)] += gwout[k * blk:(k + 1) * blk, :]
        dma = dmerged * f["sga"]
        dmb = dmerged * f["sgb"]
        dga = dmerged * f["ma"] * f["sga"] * (1.0 - f["sga"])
        dgb = dmerged * f["mb"] * f["sgb"] * (1.0 - f["sgb"])
        dmab = dma.astype(BF16)
        dmbb = dmb.astype(BF16)
        dya = _dot_nt(dmab, wa_ref[...])
        dyb = _dot_nt(dmbb, wb_ref[...])
        gwa = _dot_tn(f["yab"], dmab)
        gwb = _dot_tn(f["ybb"], dmbb)
        for k in range(n_k):
            gwa_ref[_slot(k)] += gwa[:, k * blk:(k + 1) * blk]
            gwb_ref[_slot(k)] += gwb[:, k * blk:(k + 1) * blk]
        zb, szb = f["zb"], f["szb"]
        dzb = dyb * f["ms"] * (szb * (1.0 + zb * (1.0 - szb)))
        dms = dyb * f["silu_zb"]
        gscale_ref[...] += jnp.sum(dms * f["mixed"], axis=0, keepdims=True)
        dmixed = (dms * f["scale"]).astype(BF16)
        pooled = f["pooled"]
        for g in range(len(POOL_WINDOWS)):
            cols = slice(g * POOL_GROUP, (g + 1) * POOL_GROUP)
            dpooled_ref[:, cols] = _dot_nt(dmixed[:, cols], pw_ref[g].astype(BF16))
            gpw_ref[g] += _dot_tn(pooled[:, cols].astype(BF16), dmixed[:, cols])
        za, sza = f["za"], f["sza"]
        dza = dya * f["y2"] * (sza * (1.0 + za * (1.0 - sza)))
        dy2 = dya * f["silu_za"]
        sq = f["sq"]
        dq = dy2 * f["y1"] * sq * (1.0 - sq)
        dqb = dq.astype(BF16)
        dy1 = dy2 * sq + _dot_nt(dqb, wglu)
        gwglu = _dot_tn(f["y1b"], dqb)
        rblk = WIDTH // n_k
        for k in range(n_k):
            gwglu_ref[_slot(k)] += gwglu[k * rblk:(k + 1) * rblk, :]
        gbglu_ref[...] += jnp.sum(dq, axis=0, keepdims=True)
        y0, t = f["y0"], f["t"]
        dgelu = 0.5 * (1.0 + t) + 0.5 * y0 * (1.0 - t * t) * (GELU_C * (1.0 + 3.0 * GELU_A * y0 * y0))
        dy0_ref[...] = dy1 * dgelu
        zeros = jnp.zeros((TILE_M, WIDTH), BF16)
        dproj_ref[:, 0:WIDTH] = zeros
        dproj_ref[:, WIDTH:2 * WIDTH] = dza.astype(BF16)
        dproj_ref[:, 2 * WIDTH:3 * WIDTH] = zeros
        dproj_ref[:, 3 * WIDTH:4 * WIDTH] = dzb.astype(BF16)
        dproj_ref[:, 4 * WIDTH:4 * WIDTH + D_MODEL] = dga.astype(BF16)
        dproj_ref[:, 4 * WIDTH + D_MODEL:] = dgb.astype(BF16)

    tile = lambda w: pl.BlockSpec((TILE_M, w), lambda i: (i, 0))
    shapes = _big_shapes()
    big = ["w_out", "w_branch_a", "w_branch_b", "ssm_w_glu"]
    slab = lambda n: pl.BlockSpec((None,) + shapes[n][1:], lambda i: (layer, 0, 0, 0))
    args = [dx_next, proj, y0, pooled, wg_glu, b_glu, pool_w, pool_scale, wg_a, wg_b, wg_out]
    return _pcall(
        body, name=f"mix_bwd_l{layer}",
        out_shape=(SDS((SEQ, N_IN), BF16), SDS((SEQ, WIDTH), F32), SDS((SEQ, WIDTH), F32))
        + tuple(SDS(shapes[n], F32) for n in big)
        + (SDS((4, POOL_GROUP, POOL_GROUP), F32), SDS((1, WIDTH), F32), SDS((1, WIDTH), F32)),
        grid=(SEQ // TILE_M,),
        in_specs=[tile(D_MODEL), tile(N_IN), tile(WIDTH), tile(WIDTH)] + _mix_weight_specs(layer) + [ANY] * n_prev,
        out_specs=(tile(N_IN), tile(WIDTH), tile(WIDTH)) + tuple(slab(n) for n in big)
        + (_const((4, POOL_GROUP, POOL_GROUP)), _const((1, WIDTH)), _const((1, WIDTH))),
        args=args + list(prev or ()),
        aliases={len(args) + i: 3 + i for i in range(n_prev)},
        sem=("arbitrary",), limit=VMEM_LIMIT_BIG, carry=carry)


def _pool_bwd(layer, dpooled, dproj):
    def body(dp_ref, _, o_ref):
        for gi, win in enumerate(POOL_WINDOWS):
            cols = slice(gi * POOL_GROUP, (gi + 1) * POOL_GROUP)
            dp = dp_ref[:, cols]
            t, count = _pool_counts(win)
            e = dp / count
            acc = e
            k = 1
            while k < win:
                acc = acc + jnp.where(t < SEQ - k, pltpu.roll(acc, SEQ - k, 0), 0.0)
                k *= 2
            o_ref[:, cols] = (acc - dp).astype(BF16)

    return pl.pallas_call(
        body, name=f"pool_bwd_l{layer}",
        out_shape=SDS((SEQ, N_IN), BF16),
        grid=(1,),
        in_specs=[pl.BlockSpec((SEQ, WIDTH), lambda i: (0, 0)), ANY],
        out_specs=pl.BlockSpec((SEQ, WIDTH), lambda i: (0, 2)),
        input_output_aliases={1: 0},
        compiler_params=_cp(("arbitrary",)),
    )(dpooled, dproj)


def _proj_wgrad(layer, x, norm_g, dproj, prev, carry=None):
    tm = SEQ
    n_prev = 0 if prev is None else 1

    def body(*refs):
        x_ref, g_ref, dp_ref = refs[:3]
        gw_ref, gb_ref, ht_ref = refs[3 + n_prev:]
        n, t = pl.program_id(0), pl.program_id(1)

        @pl.when(t == 0)
        def _():
            gw_ref[...] = jnp.zeros_like(gw_ref)
            gb_ref[...] = jnp.zeros_like(gb_ref)

        @pl.when(n == 0)
        def _():
            for r in range(0, tm, TIME_BLK):
                _, xn = _rms(x_ref[r:r + TIME_BLK, :])
                ht_ref[t, :, r:r + TIME_BLK] = (xn * g_ref[layer:layer + 1, :]).T.astype(BF16)

        dp = dp_ref[...]
        gw_ref[...] += _dot(ht_ref[t], dp)
        gb_ref[...] += jnp.sum(dp.astype(F32), axis=0, keepdims=True)

    return _pcall(
        body, name=f"proj_wgrad_l{layer}",
        out_shape=(SDS(_big_shapes()["w_in"], F32), SDS((1, N_IN), F32)),
        grid=(N_DEV, SEQ // tm),
        in_specs=[pl.BlockSpec((tm, D_MODEL), lambda n, t: (jnp.where(n == 0, t, 0), 0)),
                  _const((DEPTH, D_MODEL)),
                  pl.BlockSpec((tm, WIDTH), lambda n, t: (t, n))] + [ANY] * n_prev,
        out_specs=(pl.BlockSpec((None, None, D_MODEL, WIDTH), lambda n, t: (layer, _slot(n), 0, 0)),
                   pl.BlockSpec((1, WIDTH), lambda n, t: (0, n))),
        scratch_shapes=[pltpu.VMEM((SEQ // tm, D_MODEL, tm), BF16)],
        args=[x, norm_g, dproj] + ([prev] if n_prev else []),
        aliases={3: 0} if n_prev else {}, sem=("arbitrary", "arbitrary"), limit=VMEM_LIMIT_BIG, carry=carry)


def _proj_dgrad(layer, dx_next, x, norm_g, dproj, wg_in, carry=None):
    n_w = len(wg_in)

    def body(dxn_ref, x_ref, g_ref, dp_ref, *refs):
        w_refs, (dx_ref, gg_ref) = refs[:n_w], refs[n_w:]

        @pl.when(pl.program_id(0) == 0)
        def _():
            gg_ref[...] = jnp.zeros_like(gg_ref)

        parts = []
        for w_ref in w_refs:
            part = jnp.zeros((TILE_M, w_ref.shape[1]), F32)
            for k in range(N_DEV):
                part = part + _dot_nt(dp_ref[:, k * WIDTH:(k + 1) * WIDTH], w_ref[k])
            parts.append(part)
        dh = parts[0] if n_w == 1 else jnp.concatenate(parts, axis=1)
        rs, xn = _rms(x_ref[...])
        gg_ref[...] += jnp.sum(dh * xn, axis=0, keepdims=True)
        dxn = dh * g_ref[layer:layer + 1, :]
        dx_ref[...] = dxn_ref[...] + rs * (dxn - xn * jnp.mean(dxn * xn, axis=-1, keepdims=True))

    return _pcall(
        body, name=f"proj_dgrad_l{layer}",
        out_shape=(SDS((SEQ, D_MODEL), F32), SDS((1, D_MODEL), F32)),
        grid=(SEQ // TILE_M,),
        in_specs=[pl.BlockSpec((TILE_M, D_MODEL), lambda i: (i, 0)),
                  pl.BlockSpec((TILE_M, D_MODEL), lambda i: (i, 0)),
                  _const((DEPTH, D_MODEL)),
                  pl.BlockSpec((TILE_M, N_IN), lambda i: (i, 0))] + [_const(w.shape) for w in wg_in],
        out_specs=(pl.BlockSpec((TILE_M, D_MODEL), lambda i: (i, 0)), _const((1, D_MODEL))),
        args=[dx_next, x, norm_g, dproj, *wg_in], sem=("arbitrary",), carry=carry)


def _my_place():
    return lax.axis_index("x"), lax.axis_index("y"), lax.axis_index("c")


def _gather_plan(shards, layer, by_columns=(), rows_of=None):
    n = len(shards)

    def parts(ins, outs, sems):
        send_sems, recv_sems, local_sems = sems
        x, y, c = _my_place()
        chips = [(1 - x, y), (x, 1 - y), (1 - x, 1 - y)]

        def source(t):
            return ins[t].at[layer] if rows_of is None else ins[t].at[layer, pl.ds(*rows_of)]

        def rows(t, place):
            px, py, pc = place
            index = 4 * px + 2 * py + pc
            if t in by_columns:
                width = shards[t].shape[2]
                return outs[t].at[:, pl.ds(pl.multiple_of(index * width, LANES), width)]
            return outs[t].at[index]

        def copy(t, k, block, to, from_src=False):
            return pltpu.make_async_remote_copy(
                src_ref=source(t) if from_src else rows(t, block), dst_ref=rows(t, block),
                send_sem=send_sems.at[7 * t + k], recv_sem=recv_sems.at[7 * t + k], device_id=to,
                device_id_type=MESH)

        def mine(t):
            return pltpu.make_async_copy(source(t), rows(t, (x, y, c)), local_sems.at[t])

        return (x, y, c), chips, copy, mine

    def start(ins, outs, sems):
        me, chips, copy, mine = parts(ins, outs, sems)
        x, y, c = me
        for t in range(n):
            mine(t).start()
            copy(t, 0, me, (x, y, 1 - c), from_src=True).start()
            for j, chip in enumerate(chips):
                copy(t, 1 + j, me, (*chip, c), from_src=True).start()

    def relay(ins, outs, sems):
        me, chips, copy, mine = parts(ins, outs, sems)
        x, y, c = me
        for t in range(n):
            for j, chip in enumerate(chips):
                copy(t, 1 + j, (*chip, c), me).wait_recv()
                copy(t, 4 + j, (*chip, c), (x, y, 1 - c)).start()

    def finish(ins, outs, sems):
        me, chips, copy, mine = parts(ins, outs, sems)
        x, y, c = me
        sibling = (x, y, 1 - c)
        for t in range(n):
            copy(t, 0, sibling, me).wait_recv()
            for j, chip in enumerate(chips):
                copy(t, 4 + j, (*chip, 1 - c), me).wait_recv()
            for k in range(7):
                copy(t, k, me, sibling, from_src=k < 4).wait_send()
            mine(t).wait()

    n_rows = lambda a: a.shape[1] if rows_of is None else rows_of[1]
    out_shape = [SDS((a.shape[1], N_DEV * a.shape[2]) if t in by_columns else (N_DEV, n_rows(a), a.shape[2]), a.dtype)
                 for t, a in enumerate(shards)]
    sems = [pltpu.SemaphoreType.DMA((7 * n,)), pltpu.SemaphoreType.DMA((7 * n,)), pltpu.SemaphoreType.DMA((n,))]
    return _Carried(shards, out_shape, sems, start, finish, relay)


class _Carried:
    def __init__(self, ins, out_shape, sems, start, finish, relay=None):
        self.ins, self.out_shape, self.sems = list(ins), list(out_shape), list(sems)
        self.start, self.finish = start, finish
        self.relay = relay or (lambda ins, outs, sems: None)


def _pcall(body, *, name, grid, in_specs, out_specs, out_shape, args, scratch_shapes=(), aliases=None,
           sem=None, limit=VMEM_LIMIT, carry=None):
    out_shape, out_specs, scratch_shapes = list(out_shape), list(out_specs), list(scratch_shapes)
    n_in, n_out, n_scr = len(args), len(out_shape), len(scratch_shapes)
    if carry is None:
        kern, c_ins, c_out, c_sems = body, [], [], []
    else:
        c_ins, c_out, c_sems = carry.ins, carry.out_shape, carry.sems
        ci, co = len(c_ins), len(c_out)
        steps = tuple(grid)

        def kern(*refs):
            o0 = n_in + ci
            s0 = o0 + n_out + co
            mine = refs[:n_in] + refs[o0:o0 + n_out] + refs[s0:s0 + n_scr]
            theirs = (refs[n_in:o0], refs[o0 + n_out:s0], refs[s0 + n_scr:])
            first = pl.program_id(0) == 0
            last = pl.program_id(0) == steps[0] - 1
            for a in range(1, len(steps)):
                first = jnp.logical_and(first, pl.program_id(a) == 0)
                last = jnp.logical_and(last, pl.program_id(a) == steps[a] - 1)

            @pl.when(first)
            def _():
                carry.start(*theirs)

            body(*mine)

            @pl.when(last)
            def _():
                carry.relay(*theirs)
                carry.finish(*theirs)

        sem = ("arbitrary",) * len(steps)
    res = pl.pallas_call(
        kern, name=name, grid=tuple(grid),
        in_specs=list(in_specs) + [ANY] * len(c_ins),
        out_specs=tuple(out_specs + [ANY] * len(c_out)),
        out_shape=tuple(out_shape + c_out),
        scratch_shapes=scratch_shapes + c_sems,
        input_output_aliases=aliases or {},
        compiler_params=_cp(sem, limit),
    )(*args, *c_ins)
    return res[:n_out], res[n_out:]


def _run_carried(name, carry):
    ci, co = len(carry.ins), len(carry.out_shape)

    def body(*refs):
        parts = (refs[:ci], refs[ci:ci + co], refs[ci + co:])
        carry.start(*parts)
        carry.relay(*parts)
        carry.finish(*parts)

    return pl.pallas_call(
        body, name=name, out_shape=tuple(carry.out_shape),
        in_specs=[ANY] * ci, out_specs=tuple([ANY] * co), scratch_shapes=carry.sems,
    )(*carry.ins)


def _sibling_plan(big, small):
    n = len(big)
    n_copies = 4 * n + len(small)

    def copies(ins, outs, sems):
        send_sems, recv_sems = sems
        x, y, c = _my_place()
        pairs = []
        for t, (_, layer) in enumerate(big):
            for s in range(4):
                pairs.append((ins[t].at[layer, pl.ds(4 * (1 - c) + s, 1)], outs[t].at[pl.ds(s, 1)]))
        pairs += list(zip(ins[n:], outs[n:]))
        return [pltpu.make_async_remote_copy(
            src_ref=src, dst_ref=dst, send_sem=send_sems.at[k], recv_sem=recv_sems.at[k],
            device_id=(x, y, 1 - c), device_id_type=MESH) for k, (src, dst) in enumerate(pairs)]

    def start(ins, outs, sems):
        for cp in copies(ins, outs, sems):
            cp.start()

    def finish(ins, outs, sems):
        for cp in copies(ins, outs, sems):
            cp.wait()

    out_shape = [SDS((4,) + a.shape[2:], a.dtype) for a, _ in big] + [SDS(a.shape, a.dtype) for a in small]
    sems = [pltpu.SemaphoreType.DMA((n_copies,)), pltpu.SemaphoreType.DMA((n_copies,))]
    return _Carried([a for a, _ in big] + list(small), out_shape, sems, start, finish)


def _chips_plan(big, small):
    n, n_small = len(big), len(small)
    max_rows = 512
    parts = [max(1, a.shape[1] // max_rows) for a in big]
    n_copies = 3 * (sum(parts) + n_small)

    def copies(ins, outs, sems, landing):
        send_sems, recv_sems, local_sems = sems
        x, y, c = _my_place()
        my_chip = 2 * x + y
        chips = [(1 - x, y), (x, 1 - y), (1 - x, 1 - y)]
        remote, local = [], []
        for chip in chips:
            to = 2 * chip[0] + chip[1]
            slot = to if landing else my_chip
            pairs = []
            for t in range(n):
                rows_per = big[t].shape[1] // parts[t]
                for p in range(parts[t]):
                    rows = pl.ds(p * rows_per, rows_per)
                    pairs.append((ins[t].at[to, rows], outs[t].at[slot, rows]))
            pairs += [(ins[t], outs[t].at[slot]) for t in range(n, n + n_small)]
            for src, dst in pairs:
                k = len(remote)
                remote.append(pltpu.make_async_remote_copy(
                    src_ref=src, dst_ref=dst, send_sem=send_sems.at[k], recv_sem=recv_sems.at[k],
                    device_id=(*chip, c), device_id_type=MESH))
        for t in range(n):
            local.append(pltpu.make_async_copy(ins[t].at[my_chip], outs[t].at[my_chip], local_sems.at[t]))
        for t in range(n, n + n_small):
            local.append(pltpu.make_async_copy(ins[t], outs[t].at[my_chip], local_sems.at[t]))
        return remote + local

    def start(ins, outs, sems):
        for cp in copies(ins, outs, sems, landing=False):
            cp.start()

    def finish(ins, outs, sems):
        for cp in copies(ins, outs, sems, landing=True):
            cp.wait()

    out_shape = [SDS(a.shape, a.dtype) for a in big] + [SDS((N_CHIP,) + a.shape, a.dtype) for a in small]
    sems = [pltpu.SemaphoreType.DMA((n_copies,)), pltpu.SemaphoreType.DMA((n_copies,)),
            pltpu.SemaphoreType.DMA((n + n_small,))]
    return _Carried(list(big) + list(small), out_shape, sems, start, finish)


def _all_plan(small):
    n = len(small)
    masks = [(m >> 2 & 1, m >> 1 & 1, m & 1) for m in range(1, N_DEV)]

    def copies(ins, outs, sems, landing):
        send_sems, recv_sems, local_sems = sems
        x, y, c = _my_place()
        me = 4 * x + 2 * y + c
        flip = lambda v, bit: 1 - v if bit else v
        remote = []
        for fx, fy, fc in masks:
            peer = (flip(x, fx), flip(y, fy), flip(c, fc))
            slot = 4 * peer[0] + 2 * peer[1] + peer[2] if landing else me
            for t in range(n):
                k = len(remote)
                remote.append(pltpu.make_async_remote_copy(
                    src_ref=ins[t], dst_ref=outs[t].at[slot], send_sem=send_sems.at[k], recv_sem=recv_sems.at[k],
                    device_id=peer, device_id_type=MESH))
        local = [pltpu.make_async_copy(ins[t], outs[t].at[me], local_sems.at[t]) for t in range(n)]
        return remote + local

    def start(ins, outs, sems):
        for cp in copies(ins, outs, sems, landing=False):
            cp.start()

    def finish(ins, outs, sems):
        for cp in copies(ins, outs, sems, landing=True):
            cp.wait()

    out_shape = [SDS((N_DEV,) + a.shape, a.dtype) for a in small]
    sems = [pltpu.SemaphoreType.DMA((7 * n,)), pltpu.SemaphoreType.DMA((7 * n,)), pltpu.SemaphoreType.DMA((n,))]
    return _Carried(list(small), out_shape, sems, start, finish)


def _join(*plans):
    plans = [p for p in plans if p is not None]
    if len(plans) <= 1:
        return plans[0] if plans else None

    def each(fn_name, ins, outs, sems):
        i = o = s = 0
        for p in plans:
            ni, no, ns = len(p.ins), len(p.out_shape), len(p.sems)
            getattr(p, fn_name)(ins[i:i + ni], outs[o:o + no], sems[s:s + ns])
            i, o, s = i + ni, o + no, s + ns

    return _Carried(sum((p.ins for p in plans), []), sum((p.out_shape for p in plans), []),
                    sum((p.sems for p in plans), []),
                    lambda i, o, s: each("start", i, o, s), lambda i, o, s: each("finish", i, o, s),
                    lambda i, o, s: each("relay", i, o, s))


def _row_block(rows, most=256):
    return min(rows, most)


def _add_own(tag, core, gs, layer, gots):
    n = len(gs)

    def body(core_ref, *refs):
        for a_ref, b_ref, o_ref in zip(refs[:n], refs[n:2 * n], refs[2 * n:]):
            o_ref[...] = (a_ref[...] + b_ref[...]).astype(o_ref.dtype)

    mine = lambda a: pl.BlockSpec((None, None) + a.shape[1:], lambda s, core: (layer, 4 * core[0] + s, 0, 0))
    theirs = lambda a: pl.BlockSpec((None,) + a.shape[1:], lambda s, core: (s, 0, 0))
    return pl.pallas_call(
        body, name=f"add_{tag}", out_shape=tuple(SDS(a.shape, BF16) for a in gots),
        grid_spec=pltpu.PrefetchScalarGridSpec(
            num_scalar_prefetch=1, grid=(4,),
            in_specs=[mine(a) for a in gots] + [theirs(a) for a in gots],
            out_specs=tuple(theirs(a) for a in gots)),
        compiler_params=_cp(("parallel",)),
    )(core, *gs, *gots)


def _add_lists(tag, own, got, grid=None, specs=None, dtype=F32):
    n = len(own)

    def body(*refs):
        for a, b, o in zip(refs[:n], refs[n:2 * n], refs[2 * n:]):
            o[...] = (a[...] + b[...]).astype(o.dtype)

    kw = {}
    if grid is not None:
        kw = dict(grid=grid, in_specs=list(specs) * 2, out_specs=tuple(specs),
                  compiler_params=_cp(("parallel",) * len(grid)))
    return pl.pallas_call(
        body, name=f"add_{tag}", out_shape=tuple(SDS(a.shape, dtype) for a in own), **kw)(*own, *got)


def _adamw_math(w, g, m, v):
    m = ADAM_B1 * m + (1.0 - ADAM_B1) * g
    v = ADAM_B2 * v + (1.0 - ADAM_B2) * (g * g)
    m_hat = m / (1.0 - ADAM_B1 ** ADAM_STEP)
    v_hat = v / (1.0 - ADAM_B2 ** ADAM_STEP)
    delta = -ADAM_LR * (m_hat / (jnp.sqrt(v_hat) + ADAM_EPS) + ADAM_WD * w)
    return delta, m, v


def _sum_slots_adamw(tag, slots, w, m, v):
    _, r, c = slots[0].shape
    rb = _row_block(r, most=512)

    def body(s0_ref, s1_ref, w_ref, m_ref, v_ref, g_ref, d_ref, nm_ref, nv_ref):
        first = pl.program_id(1) == 0
        g = _pair_sum([jnp.where(first, s0_ref[k], s1_ref[k]).astype(F32) for k in range(N_CHIP)])
        delta, nm, nv = _adamw_math(w_ref[...], g, m_ref[...], v_ref[...])
        g_ref[...] = g
        d_ref[...] = delta
        nm_ref[...] = nm
        nv_ref[...] = nv

    spec = pl.BlockSpec((None, rb, c), lambda j, l: (l, j, 0))
    sspec = pl.BlockSpec((N_CHIP, rb, c), lambda j, l: (0, j, 0))
    s = SDS((DEPTH, r, c), F32)
    return pl.pallas_call(
        body, name=f"adamw_{tag}", out_shape=(s, s, s, s),
        grid=(r // rb, DEPTH), in_specs=[sspec, sspec, spec, spec, spec], out_specs=(spec, spec, spec, spec),
        compiler_params=_cp(("parallel", "arbitrary")),
    )(*slots, w, m, v)


def _adamw_small(tag, entries, grid=None, sums=()):
    flat_in, in_specs, out_shape, out_specs, layout = [], [], [], [], []
    for slots, w, m, v, slot_spec, w_spec in entries:
        per_layer = isinstance(slots, (list, tuple))
        n_slot = len(slots) if per_layer else 1
        flat_in += (list(slots) if per_layer else [slots]) + [w, m, v]
        in_specs += [slot_spec] * n_slot + [w_spec] * 3
        out_shape += [SDS(w.shape, F32)] * 4
        out_specs += [w_spec] * 4
        layout.append((per_layer, n_slot))
    n_entry_in = len(flat_in)
    flat_in += list(sums)
    out_shape += [SDS(s.shape[1:], F32) for s in sums]
    n_in = len(flat_in)

    def body(*refs):
        for s_ref, o_ref in zip(refs[n_entry_in:n_in], refs[len(refs) - len(sums):]):
            o_ref[...] = _sum_slots(s_ref)
        i, o = 0, n_in
        for per_layer, n_slot in layout:
            s_refs = refs[i:i + n_slot]
            w_ref, m_ref, v_ref = refs[i + n_slot:i + n_slot + 3]
            outs = refs[o:o + 4]
            if per_layer:
                for l, s_ref in enumerate(s_refs):
                    at = (slice(l, l + 1),) if len(w_ref.shape) == 2 else (l,)
                    g = _sum_slots(s_ref)
                    res = (g,) + _adamw_math(w_ref[at], g, m_ref[at], v_ref[at])
                    for o_ref, val in zip(outs, res):
                        o_ref[at] = val
            else:
                g = _sum_slots(s_refs[0])
                res = (g,) + _adamw_math(w_ref[...], g, m_ref[...], v_ref[...])
                for o_ref, val in zip(outs, res):
                    o_ref[...] = val
            i += n_slot + 3
            o += 4

    kw = {}
    if grid is not None:
        kw = dict(grid=grid, in_specs=in_specs, out_specs=tuple(out_specs),
                  compiler_params=_cp(("parallel",) * len(grid)))
    res = pl.pallas_call(body, name=f"adamw_{tag}", out_shape=tuple(out_shape), **kw)(*flat_in)
    return [tuple(res[4 * e:4 * e + 4]) for e in range(len(entries))], res[4 * len(entries):]


def kernel(x, norm_g, w_in, b_in, ssm_log_dt, ssm_lam_re, ssm_lam_im, ssm_b_re, ssm_b_im, ssm_c_re, ssm_c_im, ssm_d, ssm_w_glu, ssm_b_glu, pool_w, pool_scale, w_branch_a, w_branch_b, w_out, final_norm_g, loss_target, m_norm_g, m_w_in, m_b_in, m_ssm_log_dt, m_ssm_lam_re, m_ssm_lam_im, m_ssm_b_re, m_ssm_b_im, m_ssm_c_re, m_ssm_c_im, m_ssm_d, m_ssm_w_glu, m_ssm_b_glu, m_pool_w, m_pool_scale, m_w_branch_a, m_w_branch_b, m_w_out, m_final_norm_g, v_norm_g, v_w_in, v_b_in, v_ssm_log_dt, v_ssm_lam_re, v_ssm_lam_im, v_ssm_b_re, v_ssm_b_im, v_ssm_c_re, v_ssm_c_im, v_ssm_d, v_ssm_w_glu, v_ssm_b_glu, v_pool_w, v_pool_scale, v_w_branch_a, v_w_branch_b, v_w_out, v_final_norm_g):
    weights = dict(norm_g=norm_g, w_in=w_in, b_in=b_in, ssm_log_dt=ssm_log_dt, ssm_lam_re=ssm_lam_re,
                   ssm_lam_im=ssm_lam_im, ssm_b_re=ssm_b_re, ssm_b_im=ssm_b_im, ssm_c_re=ssm_c_re,
                   ssm_c_im=ssm_c_im, ssm_d=ssm_d, ssm_w_glu=ssm_w_glu, ssm_b_glu=ssm_b_glu, pool_w=pool_w,
                   pool_scale=pool_scale, w_branch_a=w_branch_a, w_branch_b=w_branch_b, w_out=w_out,
                   final_norm_g=final_norm_g.reshape(1, D_MODEL))
    mom_m = dict(norm_g=m_norm_g, w_in=m_w_in, b_in=m_b_in, ssm_log_dt=m_ssm_log_dt, ssm_lam_re=m_ssm_lam_re,
                 ssm_lam_im=m_ssm_lam_im, ssm_b_re=m_ssm_b_re, ssm_b_im=m_ssm_b_im, ssm_c_re=m_ssm_c_re,
                 ssm_c_im=m_ssm_c_im, ssm_d=m_ssm_d, ssm_w_glu=m_ssm_w_glu, ssm_b_glu=m_ssm_b_glu,
                 pool_w=m_pool_w, pool_scale=m_pool_scale, w_branch_a=m_w_branch_a, w_branch_b=m_w_branch_b,
                 w_out=m_w_out, final_norm_g=m_final_norm_g.reshape(1, D_MODEL))
    mom_v = dict(norm_g=v_norm_g, w_in=v_w_in, b_in=v_b_in, ssm_log_dt=v_ssm_log_dt, ssm_lam_re=v_ssm_lam_re,
                 ssm_lam_im=v_ssm_lam_im, ssm_b_re=v_ssm_b_re, ssm_b_im=v_ssm_b_im, ssm_c_re=v_ssm_c_re,
                 ssm_c_im=v_ssm_c_im, ssm_d=v_ssm_d, ssm_w_glu=v_ssm_w_glu, ssm_b_glu=v_ssm_b_glu,
                 pool_w=v_pool_w, pool_scale=v_pool_scale, w_branch_a=v_w_branch_a, w_branch_b=v_w_branch_b,
                 w_out=v_w_out, final_norm_g=v_final_norm_g.reshape(1, D_MODEL))
    order = ["norm_g", "w_in", "b_in", "ssm_log_dt", "ssm_lam_re", "ssm_lam_im", "ssm_b_re", "ssm_b_im",
             "ssm_c_re", "ssm_c_im", "ssm_d", "ssm_w_glu", "ssm_b_glu", "pool_w", "pool_scale", "w_branch_a",
             "w_branch_b", "w_out", "final_norm_g"]
    big_names = ["w_in", "ssm_w_glu", "w_branch_a", "w_branch_b", "w_out"]

    log_dt3 = ssm_log_dt.reshape(DEPTH, N_GROUP, 1)
    b_t = lambda a: a.transpose(0, 1, 3, 2)
    for d in (weights, mom_m, mom_v):
        d["ssm_b_re"], d["ssm_b_im"] = b_t(d["ssm_b_re"]), b_t(d["ssm_b_im"])
    bt_re, bt_im = weights["ssm_b_re"], weights["ssm_b_im"]
    abar_re, abar_im, bbt_re, bbt_im = _s5_params(log_dt3, ssm_lam_re, ssm_lam_im, bt_re, bt_im)
    s5_args = (bbt_re, bbt_im, ssm_c_re, ssm_c_im, abar_re, abar_im, ssm_d)

    w16 = {n: weights[n].astype(BF16) for n in big_names}
    rest = [w16[n] for n in big_names[1:]]
    half = D_MODEL // 2
    wg_in = [None, [None, None]]
    wg_rest = [None, None]
    wg_in[0] = list(_run_carried("gather_w_in_l0", _gather_plan([w16["w_in"]], 0)))
    xs = [x.reshape(SEQ, D_MODEL)]
    saved = []
    for l in range(DEPTH):
        proj, moved = _norm_proj(l, xs[l], norm_g, wg_in[l], b_in,
                                 carry=_gather_plan([w16["w_in"]], 1, rows_of=(0, half)) if l == 0 else None)
        if l == 0:
            (wg_in[1][0],) = moved
        (states, y0), wg_rest[l] = _s5_scan_fwd(l, proj, *s5_args, carry=_gather_plan(rest, l, by_columns=(1, 2)))
        pooled = _pool_fwd(l, proj)
        wg_glu, wg_a, wg_b, wg_out = wg_rest[l]
        last = l == DEPTH - 1
        res, moved = _mix_fwd(
            l, xs[l], proj, y0, pooled, wg_glu, ssm_b_glu, pool_w, pool_scale, wg_a, wg_b, wg_out,
            carry=_gather_plan([w16["w_in"]], 1, rows_of=(half, half)) if l == 0 else None,
            head=(loss_target.reshape(SEQ, D_MODEL), weights["final_norm_g"]) if last else None)
        if l == 0:
            (wg_in[1][1],) = moved
        if last:
            dx, loss_part, g_final = res
        else:
            xs.append(res[0])
        saved.append((proj, states, y0, pooled))

    core = lax.axis_index("c").astype(jnp.int32).reshape(1)
    vec_names = ["norm_g", "b_in", "ssm_d", "ssm_b_glu", "pool_scale", "ssm_log_dt"]
    s5_names = ["ssm_log_dt", "ssm_lam_re", "ssm_lam_im", "ssm_b_re", "ssm_b_im"]
    mat_names = ["pool_w", "ssm_c_re", "ssm_c_im", "ssm_b_re", "ssm_b_im"]
    lane_sparse = ("ssm_c_re", "ssm_c_im", "ssm_b_re", "ssm_b_im")

    def dense(key, a):
        return a.reshape(-1, LANES) if key[0] in lane_sparse else a

    def undense(key, slots):
        return slots.reshape((N_CHIP, N_GROUP, GROUP_W, STATE)) if key[0] in lane_sparse else slots

    def add_small(tag, keys, own, got):
        out = [None] * len(keys)
        whole = [i for i, k in enumerate(keys) if k[0] not in mat_names]
        tiled = [i for i, k in enumerate(keys) if k[0] in mat_names]
        if whole:
            for i, r in zip(whole, _add_lists(f"{tag}_a", [own[i] for i in whole], [got[i] for i in whole])):
                out[i] = r
        if tiled:
            specs = [pl.BlockSpec((1, POOL_GROUP, POOL_GROUP), lambda j: (j, 0, 0)) if keys[i][0] == "pool_w"
                     else pl.BlockSpec((own[i].shape[0] // N_CHUNK, LANES), lambda j: (j, 0)) for i in tiled]
            for i, r in zip(tiled, _add_lists(f"{tag}_b", [own[i] for i in tiled], [got[i] for i in tiled],
                                              grid=(N_CHUNK,), specs=specs, dtype=BF16)):
                out[i] = r
        return out

    sm = {("final_norm_g", None): g_final, ("loss", None): loss_part}
    slots = {}
    grads = dict.fromkeys(big_names)

    class Wave:
        def __init__(self, tag, layer, big, keys):
            self.tag, self.layer, self.big, self.keys = tag, layer, big, keys

        def to_sibling(self):
            self.own = [dense(k, sm[k]) for k in self.keys]
            return _sibling_plan([(grads[n], self.layer) for n in self.big], self.own)

        def add(self, moved):
            nb = len(self.big)
            self.chip_big = list(_add_own(self.tag, core, [grads[n] for n in self.big], self.layer, moved[:nb])
                                 ) if nb else []
            self.chip_small = add_small(self.tag, self.keys, self.own, moved[nb:])

        def to_chips(self, big=None, small=True):
            self.sent = list(self.big if big is None else big), small
            return _chips_plan([self.chip_big[self.big.index(n)] for n in self.sent[0]],
                               self.chip_small if small else [])

        def landed(self, moved):
            names, small = self.sent
            for n, s in zip(names, moved[:len(names)]):
                slots[(n, self.layer)] = s
            if small:
                for k, s in zip(self.keys, moved[len(names):]):
                    slots[k] = undense(k, s)
            return moved[len(names) + (len(self.keys) if small else 0):]

    def s5_param_grads(l, g_abar_re, g_abar_im, g_bbt_re, g_bbt_im):
        g = _s5_params_bwd(l, log_dt3, ssm_lam_re, ssm_lam_im, bt_re, bt_im, g_abar_re, g_abar_im, g_bbt_re, g_bbt_im)
        sm[("ssm_log_dt", l)] = g[0].reshape(1, N_GROUP)
        for n, a in zip(s5_names[1:], g[1:]):
            sm[(n, l)] = a

    small1 = ["b_in", "ssm_d", "ssm_b_glu", "pool_scale", "pool_w", "ssm_c_re", "ssm_c_im"] + s5_names
    w1 = Wave("chip1", 1, list(big_names), [(n, 1) for n in small1] + [("final_norm_g", None), ("loss", None)])
    early = Wave("chip0e", 0, big_names[1:], [("pool_w", 0), ("pool_scale", 0), ("ssm_b_glu", 0)])
    mid = Wave("chip0m", 0, [], [(n, 0) for n in ["ssm_c_re", "ssm_c_im", "ssm_d"] + s5_names] + [("norm_g", 1)])
    late = Wave("chip0l", 0, ["w_in"], [("b_in", 0)])

    mix_prev, gw_in = None, None
    for l in reversed(range(DEPTH)):
        proj, states, y0, pooled = saved[l]
        wg_glu, wg_a, wg_b, wg_out = wg_rest[l]
        res, moved = _mix_bwd(l, dx, proj, y0, pooled, wg_glu, ssm_b_glu, pool_w, pool_scale, wg_a, wg_b, wg_out,
                              mix_prev, carry=None if l == 1 else w1.to_chips(big=["w_in"], small=False))
        if l == 0:
            w1.landed(moved)
        dproj, dy0, dpooled = res[:3]
        mix_prev = list(res[3:7])
        grads["w_out"], grads["w_branch_a"], grads["w_branch_b"], grads["ssm_w_glu"] = mix_prev
        sm[("pool_w", l)], sm[("pool_scale", l)], sm[("ssm_b_glu", l)] = res[7:]
        dproj = _pool_bwd(l, dpooled, dproj)
        carry = None if l == 1 else _join(w1.to_chips(big=big_names[1:]), early.to_sibling())
        res, moved = _s5_scan_bwd(l, dy0, proj, states, *s5_args, dproj, carry=carry)
        if l == 0:
            early.add(w1.landed(moved))
        dproj, g_bbt_re, g_bbt_im, sm[("ssm_c_re", l)], sm[("ssm_c_im", l)], g_abar_re, g_abar_im, sm[("ssm_d", l)] = res
        s5_param_grads(l, g_abar_re, g_abar_im, g_bbt_re, g_bbt_im)
        carry = None if l == 1 else _join(early.to_chips(), mid.to_sibling())
        (gw_in, sm[("b_in", l)]), moved = _proj_wgrad(l, xs[l], norm_g, dproj, gw_in, carry=carry)
        grads["w_in"] = gw_in
        if l == 0:
            mid.add(early.landed(moved))
        carry = w1.to_sibling() if l == 1 else _join(mid.to_chips(), late.to_sibling())
        (dx, sm[("norm_g", l)]), moved = _proj_dgrad(l, dx, xs[l], norm_g, dproj, wg_in[l], carry=carry)
        if l == 1:
            w1.add(moved)
        else:
            late.add(mid.landed(moved))
    grad_x = dx.reshape(1, SEQ, D_MODEL)
    moved = late.landed(_run_carried("exchange_last", _join(late.to_chips(), _all_plan([sm[("norm_g", 0)]]))))
    slots[("norm_g", 0)] = moved[0]

    res = {}
    for n in big_names:
        res[n] = _sum_slots_adamw(n, [slots[(n, l)] for l in range(DEPTH)], weights[n], mom_m[n], mom_v[n])
    per_layer = lambda n: [slots[(n, l)] for l in range(DEPTH)]
    names_a = vec_names + ["ssm_lam_re", "ssm_lam_im"]
    entries_a = [(per_layer(n), weights[n], mom_m[n], mom_v[n], None, None) for n in names_a]
    n = "final_norm_g"
    entries_a.append((slots[(n, None)], weights[n], mom_m[n], mom_v[n], None, None))
    out_a, (loss,) = _adamw_small("small_a", entries_a, sums=[slots[("loss", None)]])
    loss = loss.reshape(())
    for n, r in zip(names_a + ["final_norm_g"], out_a):
        res[n] = r
    res["final_norm_g"] = tuple(a.reshape(D_MODEL) for a in res["final_norm_g"])
    pw_s = pl.BlockSpec((N_CHIP, 1, POOL_GROUP, POOL_GROUP), lambda j: (0, j, 0, 0))
    pw_w = pl.BlockSpec((DEPTH, 1, POOL_GROUP, POOL_GROUP), lambda j: (0, j, 0, 0))
    c_s = pl.BlockSpec((N_CHIP, CH_G, GROUP_W, STATE), lambda j: (0, j, 0, 0))
    c_w = pl.BlockSpec((DEPTH, CH_G, GROUP_W, STATE), lambda j: (0, j, 0, 0))
    entries_b = [(per_layer(n), weights[n], mom_m[n], mom_v[n], pw_s if n == "pool_w" else c_s,
                  pw_w if n == "pool_w" else c_w) for n in mat_names]
    out_b, _ = _adamw_small("small_b", entries_b, grid=(N_CHUNK,))
    for n, r in zip(mat_names, out_b):
        res[n] = tuple(b_t(a) for a in r) if n in ("ssm_b_re", "ssm_b_im") else r

    outs = [loss, grad_x]
    for i in range(4):
        outs += [res[n][i] for n in order]
    return tuple(outs)
```

```python
import math

import jax
import jax.numpy as jnp
from jax import lax
from jax.experimental import pallas as pl
from jax.experimental.pallas import tpu as pltpu

F32 = jnp.float32
BF16 = jnp.bfloat16

SEQ = 2048
D_MODEL = 1024
N_IN = 4096
WIDTH = 512
N_GROUP = 32
GROUP_W = 16
STATE = 64
N_STATE = N_GROUP * STATE
N_CHUNK = 4
CH_G = N_GROUP // N_CHUNK
CH_W = WIDTH // N_CHUNK
CH_S = N_STATE // N_CHUNK
N_DEV = 8
N_CHIP = 4
POOL_WINDOWS = (2, 4, 8, 16)
POOL_GROUP = 128
EPS = 1e-6
DEPTH = 2

ADAM_LR = 0.001
ADAM_B1 = 0.9
ADAM_B2 = 0.999
ADAM_EPS = 1e-08
ADAM_WD = 0.01
ADAM_STEP = 10

LANES = 128
SUBLANES = 8
TILE_M = 256
VMEM_LIMIT = 48 * 1024 * 1024
VMEM_LIMIT_BIG = 60 * 1024 * 1024
MESH = pl.DeviceIdType.MESH
ANY = pl.BlockSpec(memory_space=pl.ANY)

GELU_C = math.sqrt(2.0 / math.pi)
GELU_A = 0.044715

SDS = jax.ShapeDtypeStruct


def _cp(sem=None, limit=VMEM_LIMIT):
    return pltpu.CompilerParams(dimension_semantics=sem, vmem_limit_bytes=limit)


def _dot(a, b):
    return jnp.dot(a, b, preferred_element_type=F32)


def _dot_nt(a, b):
    return lax.dot_general(a, b, (((1,), (1,)), ((), ())), preferred_element_type=F32)


def _dot_tn(a, b):
    return lax.dot_general(a, b, (((0,), (0,)), ((), ())), preferred_element_type=F32)


def _sig(x):
    return jax.nn.sigmoid(x)


def _rms(x):
    rs = lax.rsqrt(jnp.mean(x * x, axis=-1, keepdims=True) + EPS)
    return rs, x * rs


def _slot(n):
    return 4 * (n % 2) + n // 2


def _const(shape):
    n = len(shape)
    return pl.BlockSpec(shape, lambda *_: (0,) * n)


def _pair_sum(vals):
    while len(vals) > 1:
        vals = [vals[i] + vals[i + 1] for i in range(0, len(vals), 2)]
    return vals[0]


def _sum_slots(s_ref):
    return _pair_sum([s_ref[k].astype(F32) for k in range(s_ref.shape[0])])


def _s5_param_fn(log_dt, lam_re, lam_im, bt_re, bt_im):
    dt = jnp.exp(log_dt)
    mag = jnp.exp(lam_re * dt)
    ang = lam_im * dt
    abar_re = mag * jnp.cos(ang)
    abar_im = mag * jnp.sin(ang)
    num_re = abar_re - 1.0
    num_im = abar_im
    den = lam_re * lam_re + lam_im * lam_im
    coef_re = (num_re * lam_re + num_im * lam_im) / den
    coef_im = (num_im * lam_re - num_re * lam_im) / den
    bbar_re = coef_re[..., None, :] * bt_re - coef_im[..., None, :] * bt_im
    bbar_im = coef_re[..., None, :] * bt_im + coef_im[..., None, :] * bt_re
    return abar_re, abar_im, bbar_re, bbar_im


def _s5_params(log_dt, lam_re, lam_im, bt_re, bt_im):
    def body(ld, lr, li, br, bi, o_ar, o_ai, o_br, o_bi):
        ar, ai, bbr, bbi = _s5_param_fn(ld[...], lr[...], li[...], br[...], bi[...])
        o_ar[...] = ar
        o_ai[...] = ai
        o_br[...] = bbr
        o_bi[...] = bbi

    return pl.pallas_call(
        body, name="s5_params",
        out_shape=(SDS(lam_re.shape, F32), SDS(lam_re.shape, F32), SDS(bt_re.shape, F32), SDS(bt_re.shape, F32)),
    )(log_dt, lam_re, lam_im, bt_re, bt_im)


def _s5_params_bwd(layer, log_dt, lam_re, lam_im, bt_re, bt_im, g_ar, g_ai, g_br, g_bi):
    def body(ld, lr, li, br, bi, car, cai, cbr, cbi, o_ld, o_lr, o_li, o_br, o_bi):
        _, vjp = jax.vjp(_s5_param_fn, ld[...], lr[...], li[...], br[...], bi[...])
        d_ld, d_lr, d_li, d_br, d_bi = vjp((car[...], cai[...], cbr[...], cbi[...]))
        o_ld[...] = d_ld
        o_lr[...] = d_lr
        o_li[...] = d_li
        o_br[...] = d_br
        o_bi[...] = d_bi

    one = lambda shape: pl.BlockSpec((None,) + shape, lambda i: (layer,) + (0,) * len(shape))
    whole = lambda shape: _const(shape)
    vec, lam, mat = (N_GROUP, 1), (N_GROUP, STATE), (N_GROUP, GROUP_W, STATE)
    return pl.pallas_call(
        body, name=f"s5_params_bwd_l{layer}", grid=(1,),
        in_specs=[one(vec), one(lam), one(lam), one(mat), one(mat), whole(lam), whole(lam), whole(mat), whole(mat)],
        out_specs=(whole(vec), whole(lam), whole(lam), whole(mat), whole(mat)),
        out_shape=(SDS(vec, F32), SDS(lam, F32), SDS(lam, F32), SDS(mat, F32), SDS(mat, F32)),
    )(log_dt, lam_re, lam_im, bt_re, bt_im, g_ar, g_ai, g_br, g_bi)


def _norm_proj(layer, x, norm_g, wg_in, b_in, carry=None):
    n_w = len(wg_in)

    def body(x_ref, g_ref, b_ref, *refs):
        w_refs, o_ref = refs[:n_w], refs[n_w]
        _, xn = _rms(x_ref[...])
        h = (xn * g_ref[layer:layer + 1, :]).astype(BF16)
        for k in range(N_DEV):
            cols = slice(k * WIDTH, (k + 1) * WIDTH)
            acc = b_ref[layer:layer + 1, cols]
            row = 0
            for w_ref in w_refs:
                rows = w_ref.shape[1]
                acc = acc + _dot(h[:, row:row + rows], w_ref[k])
                row += rows
            o_ref[:, cols] = acc

    (proj,), moved = _pcall(
        body, name=f"norm_proj_l{layer}",
        out_shape=[SDS((SEQ, N_IN), F32)],
        grid=(SEQ // TILE_M,),
        in_specs=[pl.BlockSpec((TILE_M, D_MODEL), lambda i: (i, 0)),
                  _const((DEPTH, D_MODEL)),
                  _const((DEPTH, N_IN))] + [_const(w.shape) for w in wg_in],
        out_specs=[pl.BlockSpec((TILE_M, N_IN), lambda i: (i, 0))],
        args=[x, norm_g, b_in, *wg_in], sem=("parallel",), carry=carry)
    return proj, moved


TIME_BLK = 512
N_TBLK = SEQ // TIME_BLK
N_PANEL = CH_S // LANES
STATE_SHAPE = (N_PANEL, SEQ * SUBLANES, LANES)


def _s5_layer_specs(layer):
    mat = lambda: pl.BlockSpec((None, N_GROUP, GROUP_W, STATE), lambda i: (layer, 0, 0, 0))
    ab = lambda: pl.BlockSpec((None, N_GROUP, STATE), lambda i: (layer, 0, 0))
    return [mat(), mat(), mat(), mat(), ab(), ab(), _const((DEPTH, WIDTH))]


def _s5_layer_scratch():
    return [pltpu.VMEM((N_CHUNK, CH_W, CH_S), BF16)] * 4 + [pltpu.VMEM((8, CH_S), F32)] * 2


def _s5_layer_fill(btre_ref, btim_ref, cre_ref, cim_ref, are_ref, aim_ref, bdre, bdim, ctre, ctim, a1, a2):
    for m in (bdre, bdim, ctre, ctim):
        m[...] = jnp.zeros_like(m)
    for grp in range(N_GROUP):
        k, g = divmod(grp, CH_G)
        rows = slice(g * GROUP_W, (g + 1) * GROUP_W)
        cols = slice(g * STATE, (g + 1) * STATE)
        bdre[k, rows, cols] = btre_ref[grp].astype(BF16)
        bdim[k, rows, cols] = btim_ref[grp].astype(BF16)
        ctre[k, rows, cols] = cre_ref[grp].astype(BF16)
        ctim[k, rows, cols] = cim_ref[grp].astype(BF16)
        ar = are_ref[grp:grp + 1, :]
        ai = aim_ref[grp:grp + 1, :]
        a1[k:k + 1, cols] = ar
        a1[N_CHUNK + k:N_CHUNK + k + 1, cols] = ar
        a2[k:k + 1, cols] = -ai
        a2[N_CHUNK + k:N_CHUNK + k + 1, cols] = ai


SCAN_UNROLL = 16


def _panels(tile):
    return [tile[:, p * LANES:(p + 1) * LANES] for p in range(N_PANEL)]


def _rows_load(ref, row):
    return jnp.concatenate([ref[p, pl.ds(row, TIME_BLK, stride=SUBLANES), :] for p in range(N_PANEL)], axis=1)


def _rows_store(ref, row, val):
    for p in range(N_PANEL):
        ref[p, pl.ds(row, TIME_BLK, stride=SUBLANES), :] = val[:, p * LANES:(p + 1) * LANES]


def _s5_scan_fwd(layer, proj, bbt_re, bbt_im, c_re, c_im, abar_re, abar_im, d_skip, carry=None):
    def body(u_ref, btre_ref, btim_ref, cre_ref, cim_ref, are_ref, aim_ref, d_ref, s_ref, y_ref,
             bdre, bdim, ctre, ctim, a1, a2, state):
        @pl.when(pl.program_id(0) == 0)
        def _():
            _s5_layer_fill(btre_ref, btim_ref, cre_ref, cim_ref, are_ref, aim_ref, bdre, bdim, ctre, ctim, a1, a2)
            state[...] = jnp.zeros_like(state)

        for k in range(N_CHUNK):
            ub = u_ref[:, k * CH_W:(k + 1) * CH_W].astype(BF16)
            _rows_store(s_ref, k, _dot(ub, bdre[k]))
            _rows_store(s_ref, N_CHUNK + k, _dot(ub, bdim[k]))
        m1 = _panels(a1[...])
        m2 = _panels(a2[...])

        def steps(n, tile):
            for r in range(SCAN_UNROLL):
                rows = pl.ds(pl.multiple_of((n * SCAN_UNROLL + r) * 8, 8), 8)
                tile = [m1[p] * tile[p] + m2[p] * pltpu.roll(tile[p], N_CHUNK, 0) + s_ref[p, rows, :]
                        for p in range(N_PANEL)]
                for p in range(N_PANEL):
                    s_ref[p, rows, :] = tile[p]
            return tile

        tile = lax.fori_loop(0, TIME_BLK // SCAN_UNROLL, steps, _panels(state[...]))
        state[...] = jnp.concatenate(tile, axis=1)
        d = d_ref[layer:layer + 1, :]
        for k in range(N_CHUNK):
            cols = slice(k * CH_W, (k + 1) * CH_W)
            y = (_dot_nt(_rows_load(s_ref, k).astype(BF16), ctre[k])
                 - _dot_nt(_rows_load(s_ref, N_CHUNK + k).astype(BF16), ctim[k]))
            y_ref[:, cols] = y + d[:, cols] * u_ref[:, cols]

    return _pcall(
        body, name=f"s5_fwd_l{layer}",
        out_shape=(SDS(STATE_SHAPE, F32), SDS((SEQ, WIDTH), F32)),
        grid=(N_TBLK,),
        in_specs=[pl.BlockSpec((TIME_BLK, WIDTH), lambda i: (i, 0))] + _s5_layer_specs(layer),
        out_specs=(pl.BlockSpec((N_PANEL, TIME_BLK * SUBLANES, LANES), lambda i: (0, i, 0)),
                   pl.BlockSpec((TIME_BLK, WIDTH), lambda i: (i, 0))),
        scratch_shapes=_s5_layer_scratch() + [pltpu.VMEM((8, CH_S), F32)],
        args=[proj, bbt_re, bbt_im, c_re, c_im, abar_re, abar_im, d_skip], sem=("arbitrary",), carry=carry)


def _s5_scan_bwd(layer, dy0, proj, states, bbt_re, bbt_im, c_re, c_im, abar_re, abar_im, d_skip, dproj,
                 carry=None):
    def body(dy_ref, u_ref, s_ref, sprev_ref, btre_ref, btim_ref, cre_ref, cim_ref, are_ref, aim_ref, d_ref, _,
             du_ref, gbre_ref, gbim_ref, gcre_ref, gcim_ref, gare_ref, gaim_ref, gd_ref,
             lam_ref, bdre, bdim, ctre, ctim, a1, a2, state, acc1, acc2, gbre, gbim, gcre, gcim, gd):
        step_id = pl.program_id(0)

        @pl.when(step_id == 0)
        def _():
            _s5_layer_fill(btre_ref, btim_ref, cre_ref, cim_ref, are_ref, aim_ref, bdre, bdim, ctre, ctim, a1, a2)
            for r in (state, acc1, acc2, gbre, gbim, gcre, gcim, gd):
                r[...] = jnp.zeros_like(r)

        for k in range(N_CHUNK):
            dyb = dy_ref[:, k * CH_W:(k + 1) * CH_W].astype(BF16)
            _rows_store(lam_ref, k, _dot(dyb, ctre[k]))
            _rows_store(lam_ref, N_CHUNK + k, -_dot(dyb, ctim[k]))
            gcre[k] += _dot_tn(dyb, _rows_load(s_ref, k).astype(BF16))
            gcim[k] -= _dot_tn(dyb, _rows_load(s_ref, N_CHUNK + k).astype(BF16))

        m1 = _panels(a1[...])
        m2 = _panels(-a2[...])
        has_before = (step_id < N_TBLK - 1).astype(F32)

        def one(t8, c, first_token):
            tile, swapped, p1, p2 = c
            rows = pl.ds(t8, 8)
            tile = [m1[p] * tile[p] + m2[p] * swapped[p] + lam_ref[p, rows, :] for p in range(N_PANEL)]
            swapped = [pltpu.roll(tile[p], N_CHUNK, 0) for p in range(N_PANEL)]
            for p in range(N_PANEL):
                lam_ref[p, rows, :] = tile[p]
            if first_token:
                before = [sprev_ref[p] * has_before for p in range(N_PANEL)]
            else:
                before = [s_ref[p, pl.ds(t8 - 8, 8), :] for p in range(N_PANEL)]
            p1 = [p1[p] + tile[p] * before[p] for p in range(N_PANEL)]
            p2 = [p2[p] + swapped[p] * before[p] for p in range(N_PANEL)]
            return tile, swapped, p1, p2

        def steps(n, c):
            for r in range(SCAN_UNROLL):
                t8 = pl.multiple_of((TIME_BLK - 1 - (n * SCAN_UNROLL + r)) * 8, 8)
                c = one(t8, c, False)
            return c

        tile0 = _panels(state[...])
        c = (tile0, [pltpu.roll(t, N_CHUNK, 0) for t in tile0], _panels(acc1[...]), _panels(acc2[...]))
        c = lax.fori_loop(0, TIME_BLK // SCAN_UNROLL - 1, steps, c)
        for r in range(SCAN_UNROLL - 1, -1, -1):
            c = one(r * 8, c, r == 0)
        state[...] = jnp.concatenate(c[0], axis=1)
        acc1[...] = jnp.concatenate(c[2], axis=1)
        acc2[...] = jnp.concatenate(c[3], axis=1)

        d = d_ref[layer:layer + 1, :]
        for k in range(N_CHUNK):
            cols = slice(k * CH_W, (k + 1) * CH_W)
            lrb = _rows_load(lam_ref, k).astype(BF16)
            lib = _rows_load(lam_ref, N_CHUNK + k).astype(BF16)
            u = u_ref[:, cols]
            ub = u.astype(BF16)
            dy = dy_ref[:, cols]
            du = dy * d[:, cols] + _dot_nt(lrb, bdre[k]) + _dot_nt(lib, bdim[k])
            du_ref[:, cols] = du.astype(BF16)
            gbre[k] += _dot_tn(ub, lrb)
            gbim[k] += _dot_tn(ub, lib)
        gd[...] += jnp.sum(dy_ref[...] * u_ref[...], axis=0, keepdims=True)

        @pl.when(step_id == N_TBLK - 1)
        def _():
            gd_ref[...] = gd[...]
            ga_re = acc1[0:N_CHUNK, :] + acc1[N_CHUNK:, :]
            ga_im = acc2[0:N_CHUNK, :] - acc2[N_CHUNK:, :]
            for grp in range(N_GROUP):
                k, g = divmod(grp, CH_G)
                rows = slice(g * GROUP_W, (g + 1) * GROUP_W)
                cols = slice(g * STATE, (g + 1) * STATE)
                gcre_ref[grp] = gcre[k, rows, cols]
                gcim_ref[grp] = gcim[k, rows, cols]
                gbre_ref[grp] = gbre[k, rows, cols]
                gbim_ref[grp] = gbim[k, rows, cols]
                gare_ref[grp:grp + 1, :] = ga_re[k:k + 1, cols]
                gaim_ref[grp:grp + 1, :] = ga_im[k:k + 1, cols]

    back = lambda i: N_TBLK - 1 - i
    tok = lambda: pl.BlockSpec((TIME_BLK, WIDTH), lambda i: (back(i), 0))
    mat = lambda: _const((N_GROUP, GROUP_W, STATE))
    acc_mat = pltpu.VMEM((N_CHUNK, CH_W, CH_S), F32)
    return _pcall(
        body, name=f"s5_bwd_l{layer}",
        out_shape=(SDS((SEQ, N_IN), BF16), SDS((N_GROUP, GROUP_W, STATE), F32), SDS((N_GROUP, GROUP_W, STATE), F32),
                   SDS((N_GROUP, GROUP_W, STATE), F32), SDS((N_GROUP, GROUP_W, STATE), F32),
                   SDS((N_GROUP, STATE), F32), SDS((N_GROUP, STATE), F32), SDS((1, WIDTH), F32)),
        grid=(N_TBLK,),
        in_specs=[tok(), tok(),
                  pl.BlockSpec((N_PANEL, TIME_BLK * SUBLANES, LANES), lambda i: (0, back(i), 0)),
                  pl.BlockSpec((N_PANEL, SUBLANES, LANES), lambda i: (0, jnp.maximum(back(i) * TIME_BLK - 1, 0), 0))]
        + _s5_layer_specs(layer) + [ANY],
        out_specs=(tok(), mat(), mat(), mat(), mat(), _const((N_GROUP, STATE)), _const((N_GROUP, STATE)),
                   _const((1, WIDTH))),
        scratch_shapes=[pltpu.VMEM((N_PANEL, TIME_BLK * SUBLANES, LANES), F32)] + _s5_layer_scratch()
        + [pltpu.VMEM((8, CH_S), F32)] * 3 + [acc_mat] * 4 + [pltpu.VMEM((1, WIDTH), F32)],
        args=[dy0, proj, states, states, bbt_re, bbt_im, c_re, c_im, abar_re, abar_im, d_skip, dproj],
        aliases={11: 0}, sem=("arbitrary",), limit=VMEM_LIMIT_BIG, carry=carry)


def _pool_counts(win):
    t = lax.broadcasted_iota(jnp.int32, (SEQ, POOL_GROUP), 0)
    return t, jnp.minimum(t + 1, win).astype(F32)


def _pool_fwd(layer, proj):
    def body(u_ref, o_ref):
        for gi, win in enumerate(POOL_WINDOWS):
            cols = slice(gi * POOL_GROUP, (gi + 1) * POOL_GROUP)
            u = u_ref[:, cols]
            t, count = _pool_counts(win)
            acc = u
            k = 1
            while k < win:
                acc = acc + jnp.where(t >= k, pltpu.roll(acc, k, 0), 0.0)
                k *= 2
            o_ref[:, cols] = acc / count - u

    return pl.pallas_call(
        body, name=f"pool_fwd_l{layer}",
        out_shape=SDS((SEQ, WIDTH), F32),
        grid=(1,),
        in_specs=[pl.BlockSpec((SEQ, WIDTH), lambda i: (0, 2))],
        out_specs=pl.BlockSpec((SEQ, WIDTH), lambda i: (0, 0)),
        compiler_params=_cp(("arbitrary",)),
    )(proj)


def _gelu_parts(y0):
    t = jnp.tanh(GELU_C * (y0 + GELU_A * (y0 * y0 * y0)))
    return t, 0.5 * y0 * (1.0 + t)


def _mix_forward(layer, p_ref, y0_ref, pooled_ref, wglu_ref, bglu_ref, pw_ref, scale_ref, wa_ref, wb_ref):
    za = p_ref[:, WIDTH:2 * WIDTH]
    zb = p_ref[:, 3 * WIDTH:4 * WIDTH]
    ga = p_ref[:, 4 * WIDTH:4 * WIDTH + D_MODEL]
    gb = p_ref[:, 4 * WIDTH + D_MODEL:]
    y0 = y0_ref[...]
    t, y1 = _gelu_parts(y0)
    y1b = y1.astype(BF16)
    q = _dot(y1b, wglu_ref[...].reshape(WIDTH, WIDTH)) + bglu_ref[layer:layer + 1, :]
    sq = _sig(q)
    y2 = y1 * sq
    sza = _sig(za)
    silu_za = za * sza
    ya = y2 * silu_za
    pooled = pooled_ref[...]
    mixed = jnp.concatenate(
        [_dot(pooled[:, g * POOL_GROUP:(g + 1) * POOL_GROUP].astype(BF16), pw_ref[g].astype(BF16))
         for g in range(len(POOL_WINDOWS))], axis=1)
    szb = _sig(zb)
    silu_zb = zb * szb
    scale = scale_ref[layer:layer + 1, :]
    ms = mixed * scale
    yb = ms * silu_zb
    yab = ya.astype(BF16)
    ybb = yb.astype(BF16)
    ma = _dot(yab, wa_ref[...])
    mb = _dot(ybb, wb_ref[...])
    sga = _sig(ga)
    sgb = _sig(gb)
    merged = sga * ma + sgb * mb
    return dict(za=za, zb=zb, y0=y0, t=t, y1=y1, y1b=y1b, sq=sq, y2=y2, sza=sza, silu_za=silu_za,
                pooled=pooled, mixed=mixed, szb=szb, silu_zb=silu_zb, scale=scale, ms=ms, yab=yab, ybb=ybb,
                ma=ma, mb=mb, sga=sga, sgb=sgb, merged=merged)


def _mix_weight_specs(layer):
    return [_const((N_DEV, WIDTH // N_DEV, WIDTH)),
            _const((DEPTH, WIDTH)),
            pl.BlockSpec((None, 4, POOL_GROUP, POOL_GROUP), lambda i: (layer, 0, 0, 0)),
            _const((DEPTH, WIDTH)),
            _const((WIDTH, D_MODEL)),
            _const((WIDTH, D_MODEL)),
            _const((N_DEV, D_MODEL // N_DEV, D_MODEL))]


def _loss_head(x, t_ref, g_ref, dx_ref, loss_ref, gg_ref):
    @pl.when(pl.program_id(0) == 0)
    def _():
        loss_ref[...] = jnp.zeros_like(loss_ref)
        gg_ref[...] = jnp.zeros_like(gg_ref)

    g = g_ref[...]
    rs, xn = _rms(x)
    err = xn * g - t_ref[...]
    loss_ref[...] += 0.5 * jnp.sum(jnp.mean(err * err, axis=-1, keepdims=True), axis=0, keepdims=True)
    dy = err * (1.0 / D_MODEL)
    gg_ref[...] += jnp.sum(dy * xn, axis=0, keepdims=True)
    dxn = dy * g
    dx_ref[...] = rs * (dxn - xn * jnp.mean(dxn * xn, axis=-1, keepdims=True))


def _mix_fwd(layer, x, proj, y0, pooled, wg_glu, b_glu, pool_w, pool_scale, wg_a, wg_b, wg_out, carry=None,
             head=None):
    def body(x_ref, p_ref, y0_ref, pooled_ref, wglu_ref, bglu_ref, pw_ref, scale_ref, wa_ref, wb_ref,
             wout_ref, *rest):
        f = _mix_forward(layer, p_ref, y0_ref, pooled_ref, wglu_ref, bglu_ref, pw_ref, scale_ref, wa_ref, wb_ref)
        wout = wout_ref[...].reshape(D_MODEL, D_MODEL)
        x_next = x_ref[...] + _dot(f["merged"].astype(BF16), wout)
        if head is None:
            rest[0][...] = x_next
        else:
            _loss_head(x_next, *rest)

    tile = lambda: pl.BlockSpec((TILE_M, D_MODEL), lambda i: (i, 0))
    if head is None:
        extra, out_shape, out_specs = [], [SDS((SEQ, D_MODEL), F32)], [tile()]
    else:
        extra = list(head)
        out_shape = [SDS((SEQ, D_MODEL), F32), SDS((1, 1), F32), SDS((1, D_MODEL), F32)]
        out_specs = [tile(), _const((1, 1)), _const((1, D_MODEL))]
    return _pcall(
        body, name=f"mix_fwd_l{layer}",
        out_shape=out_shape,
        grid=(SEQ // TILE_M,),
        in_specs=[tile(),
                  pl.BlockSpec((TILE_M, N_IN), lambda i: (i, 0)),
                  pl.BlockSpec((TILE_M, WIDTH), lambda i: (i, 0)),
                  pl.BlockSpec((TILE_M, WIDTH), lambda i: (i, 0))] + _mix_weight_specs(layer)
        + ([tile(), _const((1, D_MODEL))] if head else []),
        out_specs=out_specs,
        args=[x, proj, y0, pooled, wg_glu, b_glu, pool_w, pool_scale, wg_a, wg_b, wg_out] + extra,
        sem=("parallel",) if head is None else ("arbitrary",), carry=carry)


def _big_shapes():
    return dict(w_out=(DEPTH, N_DEV, D_MODEL // N_DEV, D_MODEL), w_branch_a=(DEPTH, N_DEV, WIDTH, D_MODEL // N_DEV),
                w_branch_b=(DEPTH, N_DEV, WIDTH, D_MODEL // N_DEV), ssm_w_glu=(DEPTH, N_DEV, WIDTH // N_DEV, WIDTH),
                w_in=(DEPTH, N_DEV, D_MODEL, WIDTH))


def _mix_bwd(layer, dx_next, proj, y0, pooled, wg_glu, b_glu, pool_w, pool_scale, wg_a, wg_b, wg_out, prev,
             carry=None):
    n_k = N_DEV
    n_prev = 0 if prev is None else len(prev)

    def body(*refs):
        (dx_ref, p_ref, y0_ref, pooled_ref, wglu_ref, bglu_ref, pw_ref, scale_ref, wa_ref, wb_ref,
         wout_ref) = refs[:11]
        (dproj_ref, dy0_ref, dpooled_ref, gwout_ref, gwa_ref, gwb_ref, gwglu_ref, gpw_ref,
         gscale_ref, gbglu_ref) = refs[11 + n_prev:]

        @pl.when(pl.program_id(0) == 0)
        def _():
            for r in (gwout_ref, gwa_ref, gwb_ref, gwglu_ref, gpw_ref, gscale_ref, gbglu_ref):
                r[...] = jnp.zeros_like(r)

        f = _mix_forward(layer, p_ref, y0_ref, pooled_ref, wglu_ref, bglu_ref, pw_ref, scale_ref, wa_ref, wb_ref)
        wglu = wglu_ref[...].reshape(WIDTH, WIDTH)
        wout = wout_ref[...].reshape(D_MODEL, D_MODEL)
        blk = D_MODEL // n_k
        dxb = dx_ref[...].astype(BF16)
        dmerged = _dot_nt(dxb, wout)
        gwout = _dot_tn(f["merged"].astype(BF16), dxb)
        for k in range(n_k):
            gwout_ref[_slot(k)] += gwout[k * blk:(k + 1) * blk, :]
        dma = dmerged * f["sga"]
        dmb = dmerged * f["sgb"]
        dga = dmerged * f["ma"] * f["sga"] * (1.0 - f["sga"])
        dgb = dmerged * f["mb"] * f["sgb"] * (1.0 - f["sgb"])
        dmab = dma.astype(BF16)
        dmbb = dmb.astype(BF16)
        dya = _dot_nt(dmab, wa_ref[...])
        dyb = _dot_nt(dmbb, wb_ref[...])
        gwa = _dot_tn(f["yab"], dmab)
        gwb = _dot_tn(f["ybb"], dmbb)
        for k in range(n_k):
            gwa_ref[_slot(k)] += gwa[:, k * blk:(k + 1) * blk]
            gwb_ref[_slot(k)] += gwb[:, k * blk:(k + 1) * blk]
        zb, szb = f["zb"], f["szb"]
        dzb = dyb * f["ms"] * (szb * (1.0 + zb * (1.0 - szb)))
        dms = dyb * f["silu_zb"]
        gscale_ref[...] += jnp.sum(dms * f["mixed"], axis=0, keepdims=True)
        dmixed = (dms * f["scale"]).astype(BF16)
        pooled = f["pooled"]
        for g in range(len(POOL_WINDOWS)):
            cols = slice(g * POOL_GROUP, (g + 1) * POOL_GROUP)
            dpooled_ref[:, cols] = _dot_nt(dmixed[:, cols], pw_ref[g].astype(BF16))
            gpw_ref[g] += _dot_tn(pooled[:, cols].astype(BF16), dmixed[:, cols])
        za, sza = f["za"], f["sza"]
        dza = dya * f["y2"] * (sza * (1.0 + za * (1.0 - sza)))
        dy2 = dya * f["silu_za"]
        sq = f["sq"]
        dq = dy2 * f["y1"] * sq * (1.0 - sq)
        dqb = dq.astype(BF16)
        dy1 = dy2 * sq + _dot_nt(dqb, wglu)
        gwglu = _dot_tn(f["y1b"], dqb)
        rblk = WIDTH // n_k
        for k in range(n_k):
            gwglu_ref[_slot(k)] += gwglu[k * rblk:(k + 1) * rblk, :]
        gbglu_ref[...] += jnp.sum(dq, axis=0, keepdims=True)
        y0, t = f["y0"], f["t"]
        dgelu = 0.5 * (1.0 + t) + 0.5 * y0 * (1.0 - t * t) * (GELU_C * (1.0 + 3.0 * GELU_A * y0 * y0))
        dy0_ref[...] = dy1 * dgelu
        zeros = jnp.zeros((TILE_M, WIDTH), BF16)
        dproj_ref[:, 0:WIDTH] = zeros
        dproj_ref[:, WIDTH:2 * WIDTH] = dza.astype(BF16)
        dproj_ref[:, 2 * WIDTH:3 * WIDTH] = zeros
        dproj_ref[:, 3 * WIDTH:4 * WIDTH] = dzb.astype(BF16)
        dproj_ref[:, 4 * WIDTH:4 * WIDTH + D_MODEL] = dga.astype(BF16)
        dproj_ref[:, 4 * WIDTH + D_MODEL:] = dgb.astype(BF16)

    tile = lambda w: pl.BlockSpec((TILE_M, w), lambda i: (i, 0))
    shapes = _big_shapes()
    big = ["w_out", "w_branch_a", "w_branch_b", "ssm_w_glu"]
    slab = lambda n: pl.BlockSpec((None,) + shapes[n][1:], lambda i: (layer, 0, 0, 0))
    args = [dx_next, proj, y0, pooled, wg_glu, b_glu, pool_w, pool_scale, wg_a, wg_b, wg_out]
    return _pcall(
        body, name=f"mix_bwd_l{layer}",
        out_shape=(SDS((SEQ, N_IN), BF16), SDS((SEQ, WIDTH), F32), SDS((SEQ, WIDTH), F32))
        + tuple(SDS(shapes[n], F32) for n in big)
        + (SDS((4, POOL_GROUP, POOL_GROUP), F32), SDS((1, WIDTH), F32), SDS((1, WIDTH), F32)),
        grid=(SEQ // TILE_M,),
        in_specs=[tile(D_MODEL), tile(N_IN), tile(WIDTH), tile(WIDTH)] + _mix_weight_specs(layer) + [ANY] * n_prev,
        out_specs=(tile(N_IN), tile(WIDTH), tile(WIDTH)) + tuple(slab(n) for n in big)
        + (_const((4, POOL_GROUP, POOL_GROUP)), _const((1, WIDTH)), _const((1, WIDTH))),
        args=args + list(prev or ()),
        aliases={len(args) + i: 3 + i for i in range(n_prev)},
        sem=("arbitrary",), limit=VMEM_LIMIT_BIG, carry=carry)


def _pool_bwd(layer, dpooled, dproj):
    def body(dp_ref, _, o_ref):
        for gi, win in enumerate(POOL_WINDOWS):
            cols = slice(gi * POOL_GROUP, (gi + 1) * POOL_GROUP)
            dp = dp_ref[:, cols]
            t, count = _pool_counts(win)
            e = dp / count
            acc = e
            k = 1
            while k < win:
                acc = acc + jnp.where(t < SEQ - k, pltpu.roll(acc, SEQ - k, 0), 0.0)
                k *= 2
            o_ref[:, cols] = (acc - dp).astype(BF16)

    return pl.pallas_call(
        body, name=f"pool_bwd_l{layer}",
        out_shape=SDS((SEQ, N_IN), BF16),
        grid=(1,),
        in_specs=[pl.BlockSpec((SEQ, WIDTH), lambda i: (0, 0)), ANY],
        out_specs=pl.BlockSpec((SEQ, WIDTH), lambda i: (0, 2)),
        input_output_aliases={1: 0},
        compiler_params=_cp(("arbitrary",)),
    )(dpooled, dproj)


def _proj_wgrad(layer, x, norm_g, dproj, prev, carry=None):
    tm = SEQ if carry is None else SEQ // 2
    n_prev = 0 if prev is None else 1

    def body(*refs):
        x_ref, g_ref, dp_ref = refs[:3]
        gw_ref, gb_ref, ht_ref = refs[3 + n_prev:]
        n, t = pl.program_id(0), pl.program_id(1)

        @pl.when(t == 0)
        def _():
            gw_ref[...] = jnp.zeros_like(gw_ref)
            gb_ref[...] = jnp.zeros_like(gb_ref)

        @pl.when(n == 0)
        def _():
            for r in range(0, tm, TIME_BLK):
                _, xn = _rms(x_ref[r:r + TIME_BLK, :])
                ht_ref[t, :, r:r + TIME_BLK] = (xn * g_ref[layer:layer + 1, :]).T.astype(BF16)

        dp = dp_ref[...]
        gw_ref[...] += _dot(ht_ref[t], dp)
        gb_ref[...] += jnp.sum(dp.astype(F32), axis=0, keepdims=True)

    return _pcall(
        body, name=f"proj_wgrad_l{layer}",
        out_shape=(SDS(_big_shapes()["w_in"], F32), SDS((1, N_IN), F32)),
        grid=(N_DEV, SEQ // tm),
        in_specs=[pl.BlockSpec((tm, D_MODEL), lambda n, t: (jnp.where(n == 0, t, 0), 0)),
                  _const((DEPTH, D_MODEL)),
                  pl.BlockSpec((tm, WIDTH), lambda n, t: (t, n))] + [ANY] * n_prev,
        out_specs=(pl.BlockSpec((None, None, D_MODEL, WIDTH), lambda n, t: (layer, _slot(n), 0, 0)),
                   pl.BlockSpec((1, WIDTH), lambda n, t: (0, n))),
        scratch_shapes=[pltpu.VMEM((SEQ // tm, D_MODEL, tm), BF16)],
        args=[x, norm_g, dproj] + ([prev] if n_prev else []),
        aliases={3: 0} if n_prev else {}, sem=("arbitrary", "arbitrary"),
        limit=VMEM_LIMIT_BIG if tm == SEQ else VMEM_LIMIT, carry=carry)


def _proj_dgrad(layer, dx_next, x, norm_g, dproj, wg_in, carry=None):
    n_w = len(wg_in)

    def body(dxn_ref, x_ref, g_ref, dp_ref, *refs):
        w_refs, (dx_ref, gg_ref) = refs[:n_w], refs[n_w:]

        @pl.when(pl.program_id(0) == 0)
        def _():
            gg_ref[...] = jnp.zeros_like(gg_ref)

        parts = []
        for w_ref in w_refs:
            part = jnp.zeros((TILE_M, w_ref.shape[1]), F32)
            for k in range(N_DEV):
                part = part + _dot_nt(dp_ref[:, k * WIDTH:(k + 1) * WIDTH], w_ref[k])
            parts.append(part)
        dh = parts[0] if n_w == 1 else jnp.concatenate(parts, axis=1)
        rs, xn = _rms(x_ref[...])
        gg_ref[...] += jnp.sum(dh * xn, axis=0, keepdims=True)
        dxn = dh * g_ref[layer:layer + 1, :]
        dx_ref[...] = dxn_ref[...] + rs * (dxn - xn * jnp.mean(dxn * xn, axis=-1, keepdims=True))

    return _pcall(
        body, name=f"proj_dgrad_l{layer}",
        out_shape=(SDS((SEQ, D_MODEL), F32), SDS((1, D_MODEL), F32)),
        grid=(SEQ // TILE_M,),
        in_specs=[pl.BlockSpec((TILE_M, D_MODEL), lambda i: (i, 0)),
                  pl.BlockSpec((TILE_M, D_MODEL), lambda i: (i, 0)),
                  _const((DEPTH, D_MODEL)),
                  pl.BlockSpec((TILE_M, N_IN), lambda i: (i, 0))] + [_const(w.shape) for w in wg_in],
        out_specs=(pl.BlockSpec((TILE_M, D_MODEL), lambda i: (i, 0)), _const((1, D_MODEL))),
        args=[dx_next, x, norm_g, dproj, *wg_in], sem=("arbitrary",), carry=carry)


def _my_place():
    return lax.axis_index("x"), lax.axis_index("y"), lax.axis_index("c")


def _gather_plan(shards, layer, by_columns=(), rows_of=None):
    n = len(shards)

    def parts(ins, outs, sems):
        send_sems, recv_sems, local_sems = sems
        x, y, c = _my_place()
        chips = [(1 - x, y), (x, 1 - y), (1 - x, 1 - y)]

        def source(t):
            return ins[t].at[layer] if rows_of is None else ins[t].at[layer, pl.ds(*rows_of)]

        def rows(t, place):
            px, py, pc = place
            index = 4 * px + 2 * py + pc
            if t in by_columns:
                width = shards[t].shape[2]
                return outs[t].at[:, pl.ds(pl.multiple_of(index * width, LANES), width)]
            return outs[t].at[index]

        def copy(t, k, block, to, from_src=False):
            return pltpu.make_async_remote_copy(
                src_ref=source(t) if from_src else rows(t, block), dst_ref=rows(t, block),
                send_sem=send_sems.at[7 * t + k], recv_sem=recv_sems.at[7 * t + k], device_id=to,
                device_id_type=MESH)

        def mine(t):
            return pltpu.make_async_copy(source(t), rows(t, (x, y, c)), local_sems.at[t])

        return (x, y, c), chips, copy, mine

    def start(ins, outs, sems):
        me, chips, copy, mine = parts(ins, outs, sems)
        x, y, c = me
        for t in range(n):
            mine(t).start()
            copy(t, 0, me, (x, y, 1 - c), from_src=True).start()
            for j, chip in enumerate(chips):
                copy(t, 1 + j, me, (*chip, c), from_src=True).start()

    def relay(ins, outs, sems):
        me, chips, copy, mine = parts(ins, outs, sems)
        x, y, c = me
        for t in range(n):
            for j, chip in enumerate(chips):
                copy(t, 1 + j, (*chip, c), me).wait_recv()
                copy(t, 4 + j, (*chip, c), (x, y, 1 - c)).start()

    def finish(ins, outs, sems):
        me, chips, copy, mine = parts(ins, outs, sems)
        x, y, c = me
        sibling = (x, y, 1 - c)
        for t in range(n):
            copy(t, 0, sibling, me).wait_recv()
            for j, chip in enumerate(chips):
                copy(t, 4 + j, (*chip, 1 - c), me).wait_recv()
            for k in range(7):
                copy(t, k, me, sibling, from_src=k < 4).wait_send()
            mine(t).wait()

    n_rows = lambda a: a.shape[1] if rows_of is None else rows_of[1]
    out_shape = [SDS((a.shape[1], N_DEV * a.shape[2]) if t in by_columns else (N_DEV, n_rows(a), a.shape[2]), a.dtype)
                 for t, a in enumerate(shards)]
    sems = [pltpu.SemaphoreType.DMA((7 * n,)), pltpu.SemaphoreType.DMA((7 * n,)), pltpu.SemaphoreType.DMA((n,))]
    return _Carried(shards, out_shape, sems, start, finish, relay)


class _Carried:
    def __init__(self, ins, out_shape, sems, start, finish, relay=None):
        self.ins, self.out_shape, self.sems = list(ins), list(out_shape), list(sems)
        self.start, self.finish = start, finish
        self.relay = relay or (lambda ins, outs, sems: None)


def _pcall(body, *, name, grid, in_specs, out_specs, out_shape, args, scratch_shapes=(), aliases=None,
           sem=None, limit=VMEM_LIMIT, carry=None):
    out_shape, out_specs, scratch_shapes = list(out_shape), list(out_specs), list(scratch_shapes)
    n_in, n_out, n_scr = len(args), len(out_shape), len(scratch_shapes)
    if carry is None:
        kern, c_ins, c_out, c_sems = body, [], [], []
    else:
        c_ins, c_out, c_sems = carry.ins, carry.out_shape, carry.sems
        ci, co = len(c_ins), len(c_out)
        steps = tuple(grid)

        def kern(*refs):
            o0 = n_in + ci
            s0 = o0 + n_out + co
            mine = refs[:n_in] + refs[o0:o0 + n_out] + refs[s0:s0 + n_scr]
            theirs = (refs[n_in:o0], refs[o0 + n_out:s0], refs[s0 + n_scr:])
            first = pl.program_id(0) == 0
            last = pl.program_id(0) == steps[0] - 1
            for a in range(1, len(steps)):
                first = jnp.logical_and(first, pl.program_id(a) == 0)
                last = jnp.logical_and(last, pl.program_id(a) == steps[a] - 1)

            @pl.when(first)
            def _():
                carry.start(*theirs)

            body(*mine)

            @pl.when(last)
            def _():
                carry.relay(*theirs)
                carry.finish(*theirs)

        sem = ("arbitrary",) * len(steps)
    res = pl.pallas_call(
        kern, name=name, grid=tuple(grid),
        in_specs=list(in_specs) + [ANY] * len(c_ins),
        out_specs=tuple(out_specs + [ANY] * len(c_out)),
        out_shape=tuple(out_shape + c_out),
        scratch_shapes=scratch_shapes + c_sems,
        input_output_aliases=aliases or {},
        compiler_params=_cp(sem, limit),
    )(*args, *c_ins)
    return res[:n_out], res[n_out:]


def _run_carried(name, carry):
    ci, co = len(carry.ins), len(carry.out_shape)

    def body(*refs):
        parts = (refs[:ci], refs[ci:ci + co], refs[ci + co:])
        carry.start(*parts)
        carry.relay(*parts)
        carry.finish(*parts)

    return pl.pallas_call(
        body, name=name, out_shape=tuple(carry.out_shape),
        in_specs=[ANY] * ci, out_specs=tuple([ANY] * co), scratch_shapes=carry.sems,
    )(*carry.ins)


def _sibling_plan(big, small):
    n = len(big)
    n_copies = 4 * n + len(small)

    def copies(ins, outs, sems):
        send_sems, recv_sems = sems
        x, y, c = _my_place()
        pairs = []
        for t, (_, layer) in enumerate(big):
            for s in range(4):
                pairs.append((ins[t].at[layer, pl.ds(4 * (1 - c) + s, 1)], outs[t].at[pl.ds(s, 1)]))
        pairs += list(zip(ins[n:], outs[n:]))
        return [pltpu.make_async_remote_copy(
            src_ref=src, dst_ref=dst, send_sem=send_sems.at[k], recv_sem=recv_sems.at[k],
            device_id=(x, y, 1 - c), device_id_type=MESH) for k, (src, dst) in enumerate(pairs)]

    def start(ins, outs, sems):
        for cp in copies(ins, outs, sems):
            cp.start()

    def finish(ins, outs, sems):
        for cp in copies(ins, outs, sems):
            cp.wait()

    out_shape = [SDS((4,) + a.shape[2:], a.dtype) for a, _ in big] + [SDS(a.shape, a.dtype) for a in small]
    sems = [pltpu.SemaphoreType.DMA((n_copies,)), pltpu.SemaphoreType.DMA((n_copies,))]
    return _Carried([a for a, _ in big] + list(small), out_shape, sems, start, finish)


def _chips_plan(big, small):
    n, n_small = len(big), len(small)
    max_rows = 512
    parts = [max(1, a.shape[1] // max_rows) for a in big]
    n_copies = 3 * (sum(parts) + n_small)

    def copies(ins, outs, sems, landing):
        send_sems, recv_sems, local_sems = sems
        x, y, c = _my_place()
        my_chip = 2 * x + y
        chips = [(1 - x, y), (x, 1 - y), (1 - x, 1 - y)]
        remote, local = [], []
        for chip in chips:
            to = 2 * chip[0] + chip[1]
            slot = to if landing else my_chip
            pairs = []
            for t in range(n):
                rows_per = big[t].shape[1] // parts[t]
                for p in range(parts[t]):
                    rows = pl.ds(p * rows_per, rows_per)
                    pairs.append((ins[t].at[to, rows], outs[t].at[slot, rows]))
            pairs += [(ins[t], outs[t].at[slot]) for t in range(n, n + n_small)]
            for src, dst in pairs:
                k = len(remote)
                remote.append(pltpu.make_async_remote_copy(
                    src_ref=src, dst_ref=dst, send_sem=send_sems.at[k], recv_sem=recv_sems.at[k],
                    device_id=(*chip, c), device_id_type=MESH))
        for t in range(n):
            local.append(pltpu.make_async_copy(ins[t].at[my_chip], outs[t].at[my_chip], local_sems.at[t]))
        for t in range(n, n + n_small):
            local.append(pltpu.make_async_copy(ins[t], outs[t].at[my_chip], local_sems.at[t]))
        return remote + local

    def start(ins, outs, sems):
        for cp in copies(ins, outs, sems, landing=False):
            cp.start()

    def finish(ins, outs, sems):
        for cp in copies(ins, outs, sems, landing=True):
            cp.wait()

    out_shape = [SDS(a.shape, a.dtype) for a in big] + [SDS((N_CHIP,) + a.shape, a.dtype) for a in small]
    sems = [pltpu.SemaphoreType.DMA((n_copies,)), pltpu.SemaphoreType.DMA((n_copies,)),
            pltpu.SemaphoreType.DMA((n + n_small,))]
    return _Carried(list(big) + list(small), out_shape, sems, start, finish)


def _all_plan(small):
    n = len(small)
    masks = [(m >> 2 & 1, m >> 1 & 1, m & 1) for m in range(1, N_DEV)]

    def copies(ins, outs, sems, landing):
        send_sems, recv_sems, local_sems = sems
        x, y, c = _my_place()
        me = 4 * x + 2 * y + c
        flip = lambda v, bit: 1 - v if bit else v
        remote = []
        for fx, fy, fc in masks:
            peer = (flip(x, fx), flip(y, fy), flip(c, fc))
            slot = 4 * peer[0] + 2 * peer[1] + peer[2] if landing else me
            for t in range(n):
                k = len(remote)
                remote.append(pltpu.make_async_remote_copy(
                    src_ref=ins[t], dst_ref=outs[t].at[slot], send_sem=send_sems.at[k], recv_sem=recv_sems.at[k],
                    device_id=peer, device_id_type=MESH))
        local = [pltpu.make_async_copy(ins[t], outs[t].at[me], local_sems.at[t]) for t in range(n)]
        return remote + local

    def start(ins, outs, sems):
        for cp in copies(ins, outs, sems, landing=False):
            cp.start()

    def finish(ins, outs, sems):
        for cp in copies(ins, outs, sems, landing=True):
            cp.wait()

    out_shape = [SDS((N_DEV,) + a.shape, a.dtype) for a in small]
    sems = [pltpu.SemaphoreType.DMA((7 * n,)), pltpu.SemaphoreType.DMA((7 * n,)), pltpu.SemaphoreType.DMA((n,))]
    return _Carried(list(small), out_shape, sems, start, finish)


def _join(*plans):
    plans = [p for p in plans if p is not None]
    if len(plans) <= 1:
        return plans[0] if plans else None

    def each(fn_name, ins, outs, sems):
        i = o = s = 0
        for p in plans:
            ni, no, ns = len(p.ins), len(p.out_shape), len(p.sems)
            getattr(p, fn_name)(ins[i:i + ni], outs[o:o + no], sems[s:s + ns])
            i, o, s = i + ni, o + no, s + ns

    return _Carried(sum((p.ins for p in plans), []), sum((p.out_shape for p in plans), []),
                    sum((p.sems for p in plans), []),
                    lambda i, o, s: each("start", i, o, s), lambda i, o, s: each("finish", i, o, s),
                    lambda i, o, s: each("relay", i, o, s))


def _row_block(rows, most=256):
    return min(rows, most)


def _add_own(tag, core, gs, layer, gots):
    n = len(gs)

    def body(core_ref, *refs):
        for a_ref, b_ref, o_ref in zip(refs[:n], refs[n:2 * n], refs[2 * n:]):
            o_ref[...] = (a_ref[...] + b_ref[...]).astype(o_ref.dtype)

    mine = lambda a: pl.BlockSpec((None, None) + a.shape[1:], lambda s, core: (layer, 4 * core[0] + s, 0, 0))
    theirs = lambda a: pl.BlockSpec((None,) + a.shape[1:], lambda s, core: (s, 0, 0))
    return pl.pallas_call(
        body, name=f"add_{tag}", out_shape=tuple(SDS(a.shape, BF16) for a in gots),
        grid_spec=pltpu.PrefetchScalarGridSpec(
            num_scalar_prefetch=1, grid=(4,),
            in_specs=[mine(a) for a in gots] + [theirs(a) for a in gots],
            out_specs=tuple(theirs(a) for a in gots)),
        compiler_params=_cp(("parallel",)),
    )(core, *gs, *gots)


def _add_lists(tag, own, got, grid=None, specs=None, dtype=F32):
    n = len(own)

    def body(*refs):
        for a, b, o in zip(refs[:n], refs[n:2 * n], refs[2 * n:]):
            o[...] = (a[...] + b[...]).astype(o.dtype)

    kw = {}
    if grid is not None:
        kw = dict(grid=grid, in_specs=list(specs) * 2, out_specs=tuple(specs),
                  compiler_params=_cp(("parallel",) * len(grid)))
    return pl.pallas_call(
        body, name=f"add_{tag}", out_shape=tuple(SDS(a.shape, dtype) for a in own), **kw)(*own, *got)


def _adamw_math(w, g, m, v):
    m = ADAM_B1 * m + (1.0 - ADAM_B1) * g
    v = ADAM_B2 * v + (1.0 - ADAM_B2) * (g * g)
    m_hat = m / (1.0 - ADAM_B1 ** ADAM_STEP)
    v_hat = v / (1.0 - ADAM_B2 ** ADAM_STEP)
    delta = -ADAM_LR * (m_hat / (jnp.sqrt(v_hat) + ADAM_EPS) + ADAM_WD * w)
    return delta, m, v


def _sum_slots_adamw(tag, slots, w, m, v):
    _, r, c = slots[0].shape
    rb = _row_block(r, most=512)

    def body(s0_ref, s1_ref, w_ref, m_ref, v_ref, g_ref, d_ref, nm_ref, nv_ref):
        first = pl.program_id(1) == 0
        g = _pair_sum([jnp.where(first, s0_ref[k], s1_ref[k]).astype(F32) for k in range(N_CHIP)])
        delta, nm, nv = _adamw_math(w_ref[...], g, m_ref[...], v_ref[...])
        g_ref[...] = g
        d_ref[...] = delta
        nm_ref[...] = nm
        nv_ref[...] = nv

    spec = pl.BlockSpec((None, rb, c), lambda j, l: (l, j, 0))
    sspec = pl.BlockSpec((N_CHIP, rb, c), lambda j, l: (0, j, 0))
    s = SDS((DEPTH, r, c), F32)
    return pl.pallas_call(
        body, name=f"adamw_{tag}", out_shape=(s, s, s, s),
        grid=(r // rb, DEPTH), in_specs=[sspec, sspec, spec, spec, spec], out_specs=(spec, spec, spec, spec),
        compiler_params=_cp(("parallel", "arbitrary")),
    )(*slots, w, m, v)


def _adamw_small(tag, entries, grid=None, sums=()):
    flat_in, in_specs, out_shape, out_specs, layout = [], [], [], [], []
    for slots, w, m, v, slot_spec, w_spec in entries:
        per_layer = isinstance(slots, (list, tuple))
        n_slot = len(slots) if per_layer else 1
        flat_in += (list(slots) if per_layer else [slots]) + [w, m, v]
        in_specs += [slot_spec] * n_slot + [w_spec] * 3
        out_shape += [SDS(w.shape, F32)] * 4
        out_specs += [w_spec] * 4
        layout.append((per_layer, n_slot))
    n_entry_in = len(flat_in)
    flat_in += list(sums)
    out_shape += [SDS(s.shape[1:], F32) for s in sums]
    n_in = len(flat_in)

    def body(*refs):
        for s_ref, o_ref in zip(refs[n_entry_in:n_in], refs[len(refs) - len(sums):]):
            o_ref[...] = _sum_slots(s_ref)
        i, o = 0, n_in
        for per_layer, n_slot in layout:
            s_refs = refs[i:i + n_slot]
            w_ref, m_ref, v_ref = refs[i + n_slot:i + n_slot + 3]
            outs = refs[o:o + 4]
            if per_layer:
                for l, s_ref in enumerate(s_refs):
                    at = (slice(l, l + 1),) if len(w_ref.shape) == 2 else (l,)
                    g = _sum_slots(s_ref)
                    res = (g,) + _adamw_math(w_ref[at], g, m_ref[at], v_ref[at])
                    for o_ref, val in zip(outs, res):
                        o_ref[at] = val
            else:
                g = _sum_slots(s_refs[0])
                res = (g,) + _adamw_math(w_ref[...], g, m_ref[...], v_ref[...])
                for o_ref, val in zip(outs, res):
                    o_ref[...] = val
            i += n_slot + 3
            o += 4

    kw = {}
    if grid is not None:
        kw = dict(grid=grid, in_specs=in_specs, out_specs=tuple(out_specs),
                  compiler_params=_cp(("parallel",) * len(grid)))
    res = pl.pallas_call(body, name=f"adamw_{tag}", out_shape=tuple(out_shape), **kw)(*flat_in)
    return [tuple(res[4 * e:4 * e + 4]) for e in range(len(entries))], res[4 * len(entries):]


def kernel(x, norm_g, w_in, b_in, ssm_log_dt, ssm_lam_re, ssm_lam_im, ssm_b_re, ssm_b_im, ssm_c_re, ssm_c_im, ssm_d, ssm_w_glu, ssm_b_glu, pool_w, pool_scale, w_branch_a, w_branch_b, w_out, final_norm_g, loss_target, m_norm_g, m_w_in, m_b_in, m_ssm_log_dt, m_ssm_lam_re, m_ssm_lam_im, m_ssm_b_re, m_ssm_b_im, m_ssm_c_re, m_ssm_c_im, m_ssm_d, m_ssm_w_glu, m_ssm_b_glu, m_pool_w, m_pool_scale, m_w_branch_a, m_w_branch_b, m_w_out, m_final_norm_g, v_norm_g, v_w_in, v_b_in, v_ssm_log_dt, v_ssm_lam_re, v_ssm_lam_im, v_ssm_b_re, v_ssm_b_im, v_ssm_c_re, v_ssm_c_im, v_ssm_d, v_ssm_w_glu, v_ssm_b_glu, v_pool_w, v_pool_scale, v_w_branch_a, v_w_branch_b, v_w_out, v_final_norm_g):
    weights = dict(norm_g=norm_g, w_in=w_in, b_in=b_in, ssm_log_dt=ssm_log_dt, ssm_lam_re=ssm_lam_re,
                   ssm_lam_im=ssm_lam_im, ssm_b_re=ssm_b_re, ssm_b_im=ssm_b_im, ssm_c_re=ssm_c_re,
                   ssm_c_im=ssm_c_im, ssm_d=ssm_d, ssm_w_glu=ssm_w_glu, ssm_b_glu=ssm_b_glu, pool_w=pool_w,
                   pool_scale=pool_scale, w_branch_a=w_branch_a, w_branch_b=w_branch_b, w_out=w_out,
                   final_norm_g=final_norm_g.reshape(1, D_MODEL))
    mom_m = dict(norm_g=m_norm_g, w_in=m_w_in, b_in=m_b_in, ssm_log_dt=m_ssm_log_dt, ssm_lam_re=m_ssm_lam_re,
                 ssm_lam_im=m_ssm_lam_im, ssm_b_re=m_ssm_b_re, ssm_b_im=m_ssm_b_im, ssm_c_re=m_ssm_c_re,
                 ssm_c_im=m_ssm_c_im, ssm_d=m_ssm_d, ssm_w_glu=m_ssm_w_glu, ssm_b_glu=m_ssm_b_glu,
                 pool_w=m_pool_w, pool_scale=m_pool_scale, w_branch_a=m_w_branch_a, w_branch_b=m_w_branch_b,
                 w_out=m_w_out, final_norm_g=m_final_norm_g.reshape(1, D_MODEL))
    mom_v = dict(norm_g=v_norm_g, w_in=v_w_in, b_in=v_b_in, ssm_log_dt=v_ssm_log_dt, ssm_lam_re=v_ssm_lam_re,
                 ssm_lam_im=v_ssm_lam_im, ssm_b_re=v_ssm_b_re, ssm_b_im=v_ssm_b_im, ssm_c_re=v_ssm_c_re,
                 ssm_c_im=v_ssm_c_im, ssm_d=v_ssm_d, ssm_w_glu=v_ssm_w_glu, ssm_b_glu=v_ssm_b_glu,
                 pool_w=v_pool_w, pool_scale=v_pool_scale, w_branch_a=v_w_branch_a, w_branch_b=v_w_branch_b,
                 w_out=v_w_out, final_norm_g=v_final_norm_g.reshape(1, D_MODEL))
    order = ["norm_g", "w_in", "b_in", "ssm_log_dt", "ssm_lam_re", "ssm_lam_im", "ssm_b_re", "ssm_b_im",
             "ssm_c_re", "ssm_c_im", "ssm_d", "ssm_w_glu", "ssm_b_glu", "pool_w", "pool_scale", "w_branch_a",
             "w_branch_b", "w_out", "final_norm_g"]
    big_names = ["w_in", "ssm_w_glu", "w_branch_a", "w_branch_b", "w_out"]

    log_dt3 = ssm_log_dt.reshape(DEPTH, N_GROUP, 1)
    b_t = lambda a: a.transpose(0, 1, 3, 2)
    for d in (weights, mom_m, mom_v):
        d["ssm_b_re"], d["ssm_b_im"] = b_t(d["ssm_b_re"]), b_t(d["ssm_b_im"])
    bt_re, bt_im = weights["ssm_b_re"], weights["ssm_b_im"]
    abar_re, abar_im, bbt_re, bbt_im = _s5_params(log_dt3, ssm_lam_re, ssm_lam_im, bt_re, bt_im)
    s5_args = (bbt_re, bbt_im, ssm_c_re, ssm_c_im, abar_re, abar_im, ssm_d)

    w16 = {n: weights[n].astype(BF16) for n in big_names}
    rest = [w16[n] for n in big_names[1:]]
    half = D_MODEL // 2
    wg_in = [None, [None, None]]
    wg_rest = [None, None]
    wg_in[0] = list(_run_carried("gather_w_in_l0", _gather_plan([w16["w_in"]], 0)))
    xs = [x.reshape(SEQ, D_MODEL)]
    saved = []
    for l in range(DEPTH):
        proj, moved = _norm_proj(l, xs[l], norm_g, wg_in[l], b_in,
                                 carry=_gather_plan([w16["w_in"]], 1, rows_of=(0, half)) if l == 0 else None)
        if l == 0:
            (wg_in[1][0],) = moved
        (states, y0), wg_rest[l] = _s5_scan_fwd(l, proj, *s5_args, carry=_gather_plan(rest, l, by_columns=(1, 2)))
        pooled = _pool_fwd(l, proj)
        wg_glu, wg_a, wg_b, wg_out = wg_rest[l]
        last = l == DEPTH - 1
        res, moved = _mix_fwd(
            l, xs[l], proj, y0, pooled, wg_glu, ssm_b_glu, pool_w, pool_scale, wg_a, wg_b, wg_out,
            carry=_gather_plan([w16["w_in"]], 1, rows_of=(half, half)) if l == 0 else None,
            head=(loss_target.reshape(SEQ, D_MODEL), weights["final_norm_g"]) if last else None)
        if l == 0:
            (wg_in[1][1],) = moved
        if last:
            dx, loss_part, g_final = res
        else:
            xs.append(res[0])
        saved.append((proj, states, y0, pooled))

    core = lax.axis_index("c").astype(jnp.int32).reshape(1)
    vec_names = ["norm_g", "b_in", "ssm_d", "ssm_b_glu", "pool_scale", "ssm_log_dt"]
    s5_names = ["ssm_log_dt", "ssm_lam_re", "ssm_lam_im", "ssm_b_re", "ssm_b_im"]
    mat_names = ["pool_w", "ssm_c_re", "ssm_c_im", "ssm_b_re", "ssm_b_im"]
    lane_sparse = ("ssm_c_re", "ssm_c_im", "ssm_b_re", "ssm_b_im")

    def dense(key, a):
        return a.reshape(-1, LANES) if key[0] in lane_sparse else a

    def undense(key, slots):
        return slots.reshape((N_CHIP, N_GROUP, GROUP_W, STATE)) if key[0] in lane_sparse else slots

    def add_small(tag, keys, own, got):
        out = [None] * len(keys)
        whole = [i for i, k in enumerate(keys) if k[0] not in mat_names]
        tiled = [i for i, k in enumerate(keys) if k[0] in mat_names]
        if whole:
            for i, r in zip(whole, _add_lists(f"{tag}_a", [own[i] for i in whole], [got[i] for i in whole])):
                out[i] = r
        if tiled:
            specs = [pl.BlockSpec((1, POOL_GROUP, POOL_GROUP), lambda j: (j, 0, 0)) if keys[i][0] == "pool_w"
                     else pl.BlockSpec((own[i].shape[0] // N_CHUNK, LANES), lambda j: (j, 0)) for i in tiled]
            for i, r in zip(tiled, _add_lists(f"{tag}_b", [own[i] for i in tiled], [got[i] for i in tiled],
                                              grid=(N_CHUNK,), specs=specs, dtype=BF16)):
                out[i] = r
        return out

    sm = {("final_norm_g", None): g_final, ("loss", None): loss_part}
    slots = {}
    grads = dict.fromkeys(big_names)

    class Wave:
        def __init__(self, tag, layer, big, keys):
            self.tag, self.layer, self.big, self.keys = tag, layer, big, keys

        def to_sibling(self):
            self.own = [dense(k, sm[k]) for k in self.keys]
            return _sibling_plan([(grads[n], self.layer) for n in self.big], self.own)

        def add(self, moved):
            nb = len(self.big)
            self.chip_big = list(_add_own(self.tag, core, [grads[n] for n in self.big], self.layer, moved[:nb])
                                 ) if nb else []
            self.chip_small = add_small(self.tag, self.keys, self.own, moved[nb:])

        def to_chips(self, big=None, small=True):
            self.sent = list(self.big if big is None else big), small
            return _chips_plan([self.chip_big[self.big.index(n)] for n in self.sent[0]],
                               self.chip_small if small else [])

        def landed(self, moved):
            names, small = self.sent
            for n, s in zip(names, moved[:len(names)]):
                slots[(n, self.layer)] = s
            if small:
                for k, s in zip(self.keys, moved[len(names):]):
                    slots[k] = undense(k, s)
            return moved[len(names) + (len(self.keys) if small else 0):]

    def s5_param_grads(l, g_abar_re, g_abar_im, g_bbt_re, g_bbt_im):
        g = _s5_params_bwd(l, log_dt3, ssm_lam_re, ssm_lam_im, bt_re, bt_im, g_abar_re, g_abar_im, g_bbt_re, g_bbt_im)
        sm[("ssm_log_dt", l)] = g[0].reshape(1, N_GROUP)
        for n, a in zip(s5_names[1:], g[1:]):
            sm[(n, l)] = a

    small1 = ["b_in", "ssm_d", "ssm_b_glu", "pool_scale", "pool_w", "ssm_c_re", "ssm_c_im"] + s5_names
    w1 = Wave("chip1", 1, list(big_names), [(n, 1) for n in small1] + [("final_norm_g", None), ("loss", None)])
    early = Wave("chip0e", 0, big_names[1:], [("pool_w", 0), ("pool_scale", 0), ("ssm_b_glu", 0)])
    mid = Wave("chip0m", 0, [], [(n, 0) for n in ["ssm_c_re", "ssm_c_im", "ssm_d"] + s5_names] + [("norm_g", 1)])
    late = Wave("chip0l", 0, ["w_in"], [("b_in", 0)])

    mix_prev, gw_in = None, None
    for l in reversed(range(DEPTH)):
        proj, states, y0, pooled = saved[l]
        wg_glu, wg_a, wg_b, wg_out = wg_rest[l]
        res, moved = _mix_bwd(l, dx, proj, y0, pooled, wg_glu, ssm_b_glu, pool_w, pool_scale, wg_a, wg_b, wg_out,
                              mix_prev, carry=None if l == 1 else w1.to_chips(big=["w_in"], small=False))
        if l == 0:
            w1.landed(moved)
        dproj, dy0, dpooled = res[:3]
        mix_prev = list(res[3:7])
        grads["w_out"], grads["w_branch_a"], grads["w_branch_b"], grads["ssm_w_glu"] = mix_prev
        sm[("pool_w", l)], sm[("pool_scale", l)], sm[("ssm_b_glu", l)] = res[7:]
        dproj = _pool_bwd(l, dpooled, dproj)
        carry = None if l == 1 else _join(w1.to_chips(big=big_names[1:]), early.to_sibling())
        res, moved = _s5_scan_bwd(l, dy0, proj, states, *s5_args, dproj, carry=carry)
        if l == 0:
            early.add(w1.landed(moved))
        dproj, g_bbt_re, g_bbt_im, sm[("ssm_c_re", l)], sm[("ssm_c_im", l)], g_abar_re, g_abar_im, sm[("ssm_d", l)] = res
        s5_param_grads(l, g_abar_re, g_abar_im, g_bbt_re, g_bbt_im)
        carry = None if l == 1 else _join(early.to_chips(), mid.to_sibling())
        (gw_in, sm[("b_in", l)]), moved = _proj_wgrad(l, xs[l], norm_g, dproj, gw_in, carry=carry)
        grads["w_in"] = gw_in
        if l == 0:
            mid.add(early.landed(moved))
        carry = w1.to_sibling() if l == 1 else _join(mid.to_chips(), late.to_sibling())
        (dx, sm[("norm_g", l)]), moved = _proj_dgrad(l, dx, xs[l], norm_g, dproj, wg_in[l], carry=carry)
        if l == 1:
            w1.add(moved)
        else:
            late.add(mid.landed(moved))
    grad_x = dx.reshape(1, SEQ, D_MODEL)
    moved = late.landed(_run_carried("exchange_last", _join(late.to_chips(), _all_plan([sm[("norm_g", 0)]]))))
    slots[("norm_g", 0)] = moved[0]

    res = {}
    for n in big_names:
        res[n] = _sum_slots_adamw(n, [slots[(n, l)] for l in range(DEPTH)], weights[n], mom_m[n], mom_v[n])
    per_layer = lambda n: [slots[(n, l)] for l in range(DEPTH)]
    names_a = vec_names + ["ssm_lam_re", "ssm_lam_im"]
    entries_a = [(per_layer(n), weights[n], mom_m[n], mom_v[n], None, None) for n in names_a]
    n = "final_norm_g"
    entries_a.append((slots[(n, None)], weights[n], mom_m[n], mom_v[n], None, None))
    out_a, (loss,) = _adamw_small("small_a", entries_a, sums=[slots[("loss", None)]])
    loss = loss.reshape(())
    for n, r in zip(names_a + ["final_norm_g"], out_a):
        res[n] = r
    res["final_norm_g"] = tuple(a.reshape(D_MODEL) for a in res["final_norm_g"])
    pw_s = pl.BlockSpec((N_CHIP, 1, POOL_GROUP, POOL_GROUP), lambda j: (0, j, 0, 0))
    pw_w = pl.BlockSpec((DEPTH, 1, POOL_GROUP, POOL_GROUP), lambda j: (0, j, 0, 0))
    c_s = pl.BlockSpec((N_CHIP, CH_G, GROUP_W, STATE), lambda j: (0, j, 0, 0))
    c_w = pl.BlockSpec((DEPTH, CH_G, GROUP_W, STATE), lambda j: (0, j, 0, 0))
    entries_b = [(per_layer(n), weights[n], mom_m[n], mom_v[n], pw_s if n == "pool_w" else c_s,
                  pw_w if n == "pool_w" else c_w) for n in mat_names]
    out_b, _ = _adamw_small("small_b", entries_b, grid=(N_CHUNK,))
    for n, r in zip(mat_names, out_b):
        res[n] = tuple(b_t(a) for a in r) if n in ("ssm_b_re", "ssm_b_im") else r

    outs = [loss, grad_x]
    for i in range(4):
        outs += [res[n][i] for n in order]
    return tuple(outs)
```

```python
import math

import jax
import jax.numpy as jnp
from jax import lax
from jax.experimental import pallas as pl
from jax.experimental.pallas import tpu as pltpu

F32 = jnp.float32
BF16 = jnp.bfloat16

SEQ = 2048
D_MODEL = 1024
N_IN = 4096
WIDTH = 512
N_GROUP = 32
GROUP_W = 16
STATE = 64
N_STATE = N_GROUP * STATE
N_CHUNK = 4
CH_G = N_GROUP // N_CHUNK
CH_W = WIDTH // N_CHUNK
CH_S = N_STATE // N_CHUNK
N_DEV = 8
N_CHIP = 4
POOL_WINDOWS = (2, 4, 8, 16)
POOL_GROUP = 128
EPS = 1e-6
DEPTH = 2

ADAM_LR = 0.001
ADAM_B1 = 0.9
ADAM_B2 = 0.999
ADAM_EPS = 1e-08
ADAM_WD = 0.01
ADAM_STEP = 10

LANES = 128
SUBLANES = 8
TILE_M = 256
VMEM_LIMIT = 48 * 1024 * 1024
VMEM_LIMIT_BIG = 60 * 1024 * 1024
MESH = pl.DeviceIdType.MESH
ANY = pl.BlockSpec(memory_space=pl.ANY)

GELU_C = math.sqrt(2.0 / math.pi)
GELU_A = 0.044715

SDS = jax.ShapeDtypeStruct


def _cp(sem=None, limit=VMEM_LIMIT):
    return pltpu.CompilerParams(dimension_semantics=sem, vmem_limit_bytes=limit)


def _dot(a, b):
    return jnp.dot(a, b, preferred_element_type=F32)


def _dot_nt(a, b):
    return lax.dot_general(a, b, (((1,), (1,)), ((), ())), preferred_element_type=F32)


def _dot_tn(a, b):
    return lax.dot_general(a, b, (((0,), (0,)), ((), ())), preferred_element_type=F32)


def _sig(x):
    return jax.nn.sigmoid(x)


def _rms(x):
    rs = lax.rsqrt(jnp.mean(x * x, axis=-1, keepdims=True) + EPS)
    return rs, x * rs


def _slot(n):
    return 4 * (n % 2) + n // 2


def _const(shape):
    n = len(shape)
    return pl.BlockSpec(shape, lambda *_: (0,) * n)


def _pair_sum(vals):
    while len(vals) > 1:
        vals = [vals[i] + vals[i + 1] for i in range(0, len(vals), 2)]
    return vals[0]


def _sum_slots(s_ref):
    return _pair_sum([s_ref[k].astype(F32) for k in range(s_ref.shape[0])])


def _s5_param_fn(log_dt, lam_re, lam_im, bt_re, bt_im):
    dt = jnp.exp(log_dt)
    mag = jnp.exp(lam_re * dt)
    ang = lam_im * dt
    abar_re = mag * jnp.cos(ang)
    abar_im = mag * jnp.sin(ang)
    num_re = abar_re - 1.0
    num_im = abar_im
    den = lam_re * lam_re + lam_im * lam_im
    coef_re = (num_re * lam_re + num_im * lam_im) / den
    coef_im = (num_im * lam_re - num_re * lam_im) / den
    bbar_re = coef_re[..., None, :] * bt_re - coef_im[..., None, :] * bt_im
    bbar_im = coef_re[..., None, :] * bt_im + coef_im[..., None, :] * bt_re
    return abar_re, abar_im, bbar_re, bbar_im


def _s5_params(log_dt, lam_re, lam_im, bt_re, bt_im, carry=None):
    def body(ld, lr, li, br, bi, o_ar, o_ai, o_br, o_bi):
        ar, ai, bbr, bbi = _s5_param_fn(ld[...], lr[...], li[...], br[...], bi[...])
        o_ar[...] = ar
        o_ai[...] = ai
        o_br[...] = bbr
        o_bi[...] = bbi

    args = [log_dt, lam_re, lam_im, bt_re, bt_im]
    out_shape = (SDS(lam_re.shape, F32), SDS(lam_re.shape, F32), SDS(bt_re.shape, F32), SDS(bt_re.shape, F32))
    return _pcall(
        body, name="s5_params", grid=(1,),
        in_specs=[_const(a.shape) for a in args], out_specs=[_const(s.shape) for s in out_shape],
        out_shape=out_shape, args=args, sem=("arbitrary",), carry=carry)


def _s5_params_bwd(layer, log_dt, lam_re, lam_im, bt_re, bt_im, g_ar, g_ai, g_br, g_bi):
    def body(ld, lr, li, br, bi, car, cai, cbr, cbi, o_ld, o_lr, o_li, o_br, o_bi):
        _, vjp = jax.vjp(_s5_param_fn, ld[...], lr[...], li[...], br[...], bi[...])
        d_ld, d_lr, d_li, d_br, d_bi = vjp((car[...], cai[...], cbr[...], cbi[...]))
        o_ld[...] = d_ld
        o_lr[...] = d_lr
        o_li[...] = d_li
        o_br[...] = d_br
        o_bi[...] = d_bi

    one = lambda shape: pl.BlockSpec((None,) + shape, lambda i: (layer,) + (0,) * len(shape))
    whole = lambda shape: _const(shape)
    vec, lam, mat = (N_GROUP, 1), (N_GROUP, STATE), (N_GROUP, GROUP_W, STATE)
    return pl.pallas_call(
        body, name=f"s5_params_bwd_l{layer}", grid=(1,),
        in_specs=[one(vec), one(lam), one(lam), one(mat), one(mat), whole(lam), whole(lam), whole(mat), whole(mat)],
        out_specs=(whole(vec), whole(lam), whole(lam), whole(mat), whole(mat)),
        out_shape=(SDS(vec, F32), SDS(lam, F32), SDS(lam, F32), SDS(mat, F32), SDS(mat, F32)),
    )(log_dt, lam_re, lam_im, bt_re, bt_im, g_ar, g_ai, g_br, g_bi)


def _norm_proj(layer, x, norm_g, wg_in, b_in, carry=None):
    n_w = len(wg_in)

    def body(x_ref, g_ref, b_ref, *refs):
        w_refs, o_ref = refs[:n_w], refs[n_w]
        _, xn = _rms(x_ref[...])
        h = (xn * g_ref[layer:layer + 1, :]).astype(BF16)
        for k in range(N_DEV):
            cols = slice(k * WIDTH, (k + 1) * WIDTH)
            acc = b_ref[layer:layer + 1, cols]
            row = 0
            for w_ref in w_refs:
                rows = w_ref.shape[1]
                acc = acc + _dot(h[:, row:row + rows], w_ref[k])
                row += rows
            o_ref[:, cols] = acc

    (proj,), moved = _pcall(
        body, name=f"norm_proj_l{layer}",
        out_shape=[SDS((SEQ, N_IN), F32)],
        grid=(SEQ // TILE_M,),
        in_specs=[pl.BlockSpec((TILE_M, D_MODEL), lambda i: (i, 0)),
                  _const((DEPTH, D_MODEL)),
                  _const((DEPTH, N_IN))] + [_const(w.shape) for w in wg_in],
        out_specs=[pl.BlockSpec((TILE_M, N_IN), lambda i: (i, 0))],
        args=[x, norm_g, b_in, *wg_in], sem=("parallel",), carry=carry)
    return proj, moved


TIME_BLK = 512
N_TBLK = SEQ // TIME_BLK
N_PANEL = CH_S // LANES
STATE_SHAPE = (N_PANEL, SEQ * SUBLANES, LANES)


def _s5_layer_specs(layer):
    mat = lambda: pl.BlockSpec((None, N_GROUP, GROUP_W, STATE), lambda i: (layer, 0, 0, 0))
    ab = lambda: pl.BlockSpec((None, N_GROUP, STATE), lambda i: (layer, 0, 0))
    return [mat(), mat(), mat(), mat(), ab(), ab(), _const((DEPTH, WIDTH))]


def _s5_layer_scratch():
    return [pltpu.VMEM((N_CHUNK, CH_W, CH_S), BF16)] * 4 + [pltpu.VMEM((8, CH_S), F32)] * 2


def _s5_layer_fill(btre_ref, btim_ref, cre_ref, cim_ref, are_ref, aim_ref, bdre, bdim, ctre, ctim, a1, a2):
    for m in (bdre, bdim, ctre, ctim):
        m[...] = jnp.zeros_like(m)
    for grp in range(N_GROUP):
        k, g = divmod(grp, CH_G)
        rows = slice(g * GROUP_W, (g + 1) * GROUP_W)
        cols = slice(g * STATE, (g + 1) * STATE)
        bdre[k, rows, cols] = btre_ref[grp].astype(BF16)
        bdim[k, rows, cols] = btim_ref[grp].astype(BF16)
        ctre[k, rows, cols] = cre_ref[grp].astype(BF16)
        ctim[k, rows, cols] = cim_ref[grp].astype(BF16)
        ar = are_ref[grp:grp + 1, :]
        ai = aim_ref[grp:grp + 1, :]
        a1[k:k + 1, cols] = ar
        a1[N_CHUNK + k:N_CHUNK + k + 1, cols] = ar
        a2[k:k + 1, cols] = -ai
        a2[N_CHUNK + k:N_CHUNK + k + 1, cols] = ai


SCAN_UNROLL = 16


def _panels(tile):
    return [tile[:, p * LANES:(p + 1) * LANES] for p in range(N_PANEL)]


def _rows_load(ref, row):
    return jnp.concatenate([ref[p, pl.ds(row, TIME_BLK, stride=SUBLANES), :] for p in range(N_PANEL)], axis=1)


def _rows_store(ref, row, val):
    for p in range(N_PANEL):
        ref[p, pl.ds(row, TIME_BLK, stride=SUBLANES), :] = val[:, p * LANES:(p + 1) * LANES]


def _s5_scan_fwd(layer, proj, bbt_re, bbt_im, c_re, c_im, abar_re, abar_im, d_skip, carry=None):
    def body(u_ref, btre_ref, btim_ref, cre_ref, cim_ref, are_ref, aim_ref, d_ref, s_ref, y_ref,
             bdre, bdim, ctre, ctim, a1, a2, state):
        @pl.when(pl.program_id(0) == 0)
        def _():
            _s5_layer_fill(btre_ref, btim_ref, cre_ref, cim_ref, are_ref, aim_ref, bdre, bdim, ctre, ctim, a1, a2)
            state[...] = jnp.zeros_like(state)

        for k in range(N_CHUNK):
            ub = u_ref[:, k * CH_W:(k + 1) * CH_W].astype(BF16)
            _rows_store(s_ref, k, _dot(ub, bdre[k]))
            _rows_store(s_ref, N_CHUNK + k, _dot(ub, bdim[k]))
        m1 = _panels(a1[...])
        m2 = _panels(a2[...])

        def steps(n, tile):
            for r in range(SCAN_UNROLL):
                rows = pl.ds(pl.multiple_of((n * SCAN_UNROLL + r) * 8, 8), 8)
                tile = [m1[p] * tile[p] + m2[p] * pltpu.roll(tile[p], N_CHUNK, 0) + s_ref[p, rows, :]
                        for p in range(N_PANEL)]
                for p in range(N_PANEL):
                    s_ref[p, rows, :] = tile[p]
            return tile

        tile = lax.fori_loop(0, TIME_BLK // SCAN_UNROLL, steps, _panels(state[...]))
        state[...] = jnp.concatenate(tile, axis=1)
        d = d_ref[layer:layer + 1, :]
        for k in range(N_CHUNK):
            cols = slice(k * CH_W, (k + 1) * CH_W)
            y = (_dot_nt(_rows_load(s_ref, k).astype(BF16), ctre[k])
                 - _dot_nt(_rows_load(s_ref, N_CHUNK + k).astype(BF16), ctim[k]))
            y_ref[:, cols] = y + d[:, cols] * u_ref[:, cols]

    return _pcall(
        body, name=f"s5_fwd_l{layer}",
        out_shape=(SDS(STATE_SHAPE, F32), SDS((SEQ, WIDTH), F32)),
        grid=(N_TBLK,),
        in_specs=[pl.BlockSpec((TIME_BLK, WIDTH), lambda i: (i, 0))] + _s5_layer_specs(layer),
        out_specs=(pl.BlockSpec((N_PANEL, TIME_BLK * SUBLANES, LANES), lambda i: (0, i, 0)),
                   pl.BlockSpec((TIME_BLK, WIDTH), lambda i: (i, 0))),
        scratch_shapes=_s5_layer_scratch() + [pltpu.VMEM((8, CH_S), F32)],
        args=[proj, bbt_re, bbt_im, c_re, c_im, abar_re, abar_im, d_skip], sem=("arbitrary",), carry=carry)


def _s5_scan_bwd(layer, dy0, proj, states, bbt_re, bbt_im, c_re, c_im, abar_re, abar_im, d_skip, dproj,
                 carry=None):
    def body(dy_ref, u_ref, s_ref, sprev_ref, btre_ref, btim_ref, cre_ref, cim_ref, are_ref, aim_ref, d_ref, _,
             du_ref, gbre_ref, gbim_ref, gcre_ref, gcim_ref, gare_ref, gaim_ref, gd_ref,
             lam_ref, bdre, bdim, ctre, ctim, a1, a2, state, acc1, acc2, gbre, gbim, gcre, gcim, gd):
        step_id = pl.program_id(0)

        @pl.when(step_id == 0)
        def _():
            _s5_layer_fill(btre_ref, btim_ref, cre_ref, cim_ref, are_ref, aim_ref, bdre, bdim, ctre, ctim, a1, a2)
            for r in (state, acc1, acc2, gbre, gbim, gcre, gcim, gd):
                r[...] = jnp.zeros_like(r)

        for k in range(N_CHUNK):
            dyb = dy_ref[:, k * CH_W:(k + 1) * CH_W].astype(BF16)
            _rows_store(lam_ref, k, _dot(dyb, ctre[k]))
            _rows_store(lam_ref, N_CHUNK + k, -_dot(dyb, ctim[k]))
            gcre[k] += _dot_tn(dyb, _rows_load(s_ref, k).astype(BF16))
            gcim[k] -= _dot_tn(dyb, _rows_load(s_ref, N_CHUNK + k).astype(BF16))

        m1 = _panels(a1[...])
        m2 = _panels(-a2[...])
        has_before = (step_id < N_TBLK - 1).astype(F32)

        def one(t8, c, first_token):
            tile, swapped, p1, p2 = c
            rows = pl.ds(t8, 8)
            tile = [m1[p] * tile[p] + m2[p] * swapped[p] + lam_ref[p, rows, :] for p in range(N_PANEL)]
            swapped = [pltpu.roll(tile[p], N_CHUNK, 0) for p in range(N_PANEL)]
            for p in range(N_PANEL):
                lam_ref[p, rows, :] = tile[p]
            if first_token:
                before = [sprev_ref[p] * has_before for p in range(N_PANEL)]
            else:
                before = [s_ref[p, pl.ds(t8 - 8, 8), :] for p in range(N_PANEL)]
            p1 = [p1[p] + tile[p] * before[p] for p in range(N_PANEL)]
            p2 = [p2[p] + swapped[p] * before[p] for p in range(N_PANEL)]
            return tile, swapped, p1, p2

        def steps(n, c):
            for r in range(SCAN_UNROLL):
                t8 = pl.multiple_of((TIME_BLK - 1 - (n * SCAN_UNROLL + r)) * 8, 8)
                c = one(t8, c, False)
            return c

        tile0 = _panels(state[...])
        c = (tile0, [pltpu.roll(t, N_CHUNK, 0) for t in tile0], _panels(acc1[...]), _panels(acc2[...]))
        c = lax.fori_loop(0, TIME_BLK // SCAN_UNROLL - 1, steps, c)
        for r in range(SCAN_UNROLL - 1, -1, -1):
            c = one(r * 8, c, r == 0)
        state[...] = jnp.concatenate(c[0], axis=1)
        acc1[...] = jnp.concatenate(c[2], axis=1)
        acc2[...] = jnp.concatenate(c[3], axis=1)

        d = d_ref[layer:layer + 1, :]
        for k in range(N_CHUNK):
            cols = slice(k * CH_W, (k + 1) * CH_W)
            lrb = _rows_load(lam_ref, k).astype(BF16)
            lib = _rows_load(lam_ref, N_CHUNK + k).astype(BF16)
            u = u_ref[:, cols]
            ub = u.astype(BF16)
            dy = dy_ref[:, cols]
            du = dy * d[:, cols] + _dot_nt(lrb, bdre[k]) + _dot_nt(lib, bdim[k])
            du_ref[:, cols] = du.astype(BF16)
            gbre[k] += _dot_tn(ub, lrb)
            gbim[k] += _dot_tn(ub, lib)
        gd[...] += jnp.sum(dy_ref[...] * u_ref[...], axis=0, keepdims=True)

        @pl.when(step_id == N_TBLK - 1)
        def _():
            gd_ref[...] = gd[...]
            ga_re = acc1[0:N_CHUNK, :] + acc1[N_CHUNK:, :]
            ga_im = acc2[0:N_CHUNK, :] - acc2[N_CHUNK:, :]
            for grp in range(N_GROUP):
                k, g = divmod(grp, CH_G)
                rows = slice(g * GROUP_W, (g + 1) * GROUP_W)
                cols = slice(g * STATE, (g + 1) * STATE)
                gcre_ref[grp] = gcre[k, rows, cols]
                gcim_ref[grp] = gcim[k, rows, cols]
                gbre_ref[grp] = gbre[k, rows, cols]
                gbim_ref[grp] = gbim[k, rows, cols]
                gare_ref[grp:grp + 1, :] = ga_re[k:k + 1, cols]
                gaim_ref[grp:grp + 1, :] = ga_im[k:k + 1, cols]

    back = lambda i: N_TBLK - 1 - i
    tok = lambda: pl.BlockSpec((TIME_BLK, WIDTH), lambda i: (back(i), 0))
    mat = lambda: _const((N_GROUP, GROUP_W, STATE))
    acc_mat = pltpu.VMEM((N_CHUNK, CH_W, CH_S), F32)
    return _pcall(
        body, name=f"s5_bwd_l{layer}",
        out_shape=(SDS((SEQ, N_IN), BF16), SDS((N_GROUP, GROUP_W, STATE), F32), SDS((N_GROUP, GROUP_W, STATE), F32),
                   SDS((N_GROUP, GROUP_W, STATE), F32), SDS((N_GROUP, GROUP_W, STATE), F32),
                   SDS((N_GROUP, STATE), F32), SDS((N_GROUP, STATE), F32), SDS((1, WIDTH), F32)),
        grid=(N_TBLK,),
        in_specs=[tok(), tok(),
                  pl.BlockSpec((N_PANEL, TIME_BLK * SUBLANES, LANES), lambda i: (0, back(i), 0)),
                  pl.BlockSpec((N_PANEL, SUBLANES, LANES), lambda i: (0, jnp.maximum(back(i) * TIME_BLK - 1, 0), 0))]
        + _s5_layer_specs(layer) + [ANY],
        out_specs=(tok(), mat(), mat(), mat(), mat(), _const((N_GROUP, STATE)), _const((N_GROUP, STATE)),
                   _const((1, WIDTH))),
        scratch_shapes=[pltpu.VMEM((N_PANEL, TIME_BLK * SUBLANES, LANES), F32)] + _s5_layer_scratch()
        + [pltpu.VMEM((8, CH_S), F32)] * 3 + [acc_mat] * 4 + [pltpu.VMEM((1, WIDTH), F32)],
        args=[dy0, proj, states, states, bbt_re, bbt_im, c_re, c_im, abar_re, abar_im, d_skip, dproj],
        aliases={11: 0}, sem=("arbitrary",), limit=VMEM_LIMIT_BIG, carry=carry)


def _pool_counts(win):
    t = lax.broadcasted_iota(jnp.int32, (SEQ, POOL_GROUP), 0)
    return t, jnp.minimum(t + 1, win).astype(F32)


def _pool_fwd(layer, proj):
    def body(u_ref, o_ref):
        for gi, win in enumerate(POOL_WINDOWS):
            cols = slice(gi * POOL_GROUP, (gi + 1) * POOL_GROUP)
            u = u_ref[:, cols]
            t, count = _pool_counts(win)
            acc = u
            k = 1
            while k < win:
                acc = acc + jnp.where(t >= k, pltpu.roll(acc, k, 0), 0.0)
                k *= 2
            o_ref[:, cols] = acc / count - u

    return pl.pallas_call(
        body, name=f"pool_fwd_l{layer}",
        out_shape=SDS((SEQ, WIDTH), F32),
        grid=(1,),
        in_specs=[pl.BlockSpec((SEQ, WIDTH), lambda i: (0, 2))],
        out_specs=pl.BlockSpec((SEQ, WIDTH), lambda i: (0, 0)),
        compiler_params=_cp(("arbitrary",)),
    )(proj)


def _gelu_parts(y0):
    t = jnp.tanh(GELU_C * (y0 + GELU_A * (y0 * y0 * y0)))
    return t, 0.5 * y0 * (1.0 + t)


def _mix_forward(layer, p_ref, y0_ref, pooled_ref, wglu_ref, bglu_ref, pw_ref, scale_ref, wa_ref, wb_ref):
    za = p_ref[:, WIDTH:2 * WIDTH]
    zb = p_ref[:, 3 * WIDTH:4 * WIDTH]
    ga = p_ref[:, 4 * WIDTH:4 * WIDTH + D_MODEL]
    gb = p_ref[:, 4 * WIDTH + D_MODEL:]
    y0 = y0_ref[...]
    t, y1 = _gelu_parts(y0)
    y1b = y1.astype(BF16)
    q = _dot(y1b, wglu_ref[...].reshape(WIDTH, WIDTH)) + bglu_ref[layer:layer + 1, :]
    sq = _sig(q)
    y2 = y1 * sq
    sza = _sig(za)
    silu_za = za * sza
    ya = y2 * silu_za
    pooled = pooled_ref[...]
    mixed = jnp.concatenate(
        [_dot(pooled[:, g * POOL_GROUP:(g + 1) * POOL_GROUP].astype(BF16), pw_ref[g].astype(BF16))
         for g in range(len(POOL_WINDOWS))], axis=1)
    szb = _sig(zb)
    silu_zb = zb * szb
    scale = scale_ref[layer:layer + 1, :]
    ms = mixed * scale
    yb = ms * silu_zb
    yab = ya.astype(BF16)
    ybb = yb.astype(BF16)
    ma = _dot(yab, wa_ref[...])
    mb = _dot(ybb, wb_ref[...])
    sga = _sig(ga)
    sgb = _sig(gb)
    merged = sga * ma + sgb * mb
    return dict(za=za, zb=zb, y0=y0, t=t, y1=y1, y1b=y1b, sq=sq, y2=y2, sza=sza, silu_za=silu_za,
                pooled=pooled, mixed=mixed, szb=szb, silu_zb=silu_zb, scale=scale, ms=ms, yab=yab, ybb=ybb,
                ma=ma, mb=mb, sga=sga, sgb=sgb, merged=merged)


def _mix_weight_specs(layer):
    return [_const((N_DEV, WIDTH // N_DEV, WIDTH)),
            _const((DEPTH, WIDTH)),
            pl.BlockSpec((None, 4, POOL_GROUP, POOL_GROUP), lambda i: (layer, 0, 0, 0)),
            _const((DEPTH, WIDTH)),
            _const((WIDTH, D_MODEL)),
            _const((WIDTH, D_MODEL)),
            _const((N_DEV, D_MODEL // N_DEV, D_MODEL))]


def _loss_head(x, t_ref, g_ref, dx_ref, loss_ref, gg_ref):
    @pl.when(pl.program_id(0) == 0)
    def _():
        loss_ref[...] = jnp.zeros_like(loss_ref)
        gg_ref[...] = jnp.zeros_like(gg_ref)

    g = g_ref[...]
    rs, xn = _rms(x)
    err = xn * g - t_ref[...]
    loss_ref[...] += 0.5 * jnp.sum(jnp.mean(err * err, axis=-1, keepdims=True), axis=0, keepdims=True)
    dy = err * (1.0 / D_MODEL)
    gg_ref[...] += jnp.sum(dy * xn, axis=0, keepdims=True)
    dxn = dy * g
    dx_ref[...] = rs * (dxn - xn * jnp.mean(dxn * xn, axis=-1, keepdims=True))


def _mix_fwd(layer, x, proj, y0, pooled, wg_glu, b_glu, pool_w, pool_scale, wg_a, wg_b, wg_out, carry=None,
             head=None):
    def body(x_ref, p_ref, y0_ref, pooled_ref, wglu_ref, bglu_ref, pw_ref, scale_ref, wa_ref, wb_ref,
             wout_ref, *rest):
        f = _mix_forward(layer, p_ref, y0_ref, pooled_ref, wglu_ref, bglu_ref, pw_ref, scale_ref, wa_ref, wb_ref)
        wout = wout_ref[...].reshape(D_MODEL, D_MODEL)
        x_next = x_ref[...] + _dot(f["merged"].astype(BF16), wout)
        if head is None:
            rest[0][...] = x_next
        else:
            _loss_head(x_next, *rest)

    tile = lambda: pl.BlockSpec((TILE_M, D_MODEL), lambda i: (i, 0))
    if head is None:
        extra, out_shape, out_specs = [], [SDS((SEQ, D_MODEL), F32)], [tile()]
    else:
        extra = list(head)
        out_shape = [SDS((SEQ, D_MODEL), F32), SDS((1, 1), F32), SDS((1, D_MODEL), F32)]
        out_specs = [tile(), _const((1, 1)), _const((1, D_MODEL))]
    return _pcall(
        body, name=f"mix_fwd_l{layer}",
        out_shape=out_shape,
        grid=(SEQ // TILE_M,),
        in_specs=[tile(),
                  pl.BlockSpec((TILE_M, N_IN), lambda i: (i, 0)),
                  pl.BlockSpec((TILE_M, WIDTH), lambda i: (i, 0)),
                  pl.BlockSpec((TILE_M, WIDTH), lambda i: (i, 0))] + _mix_weight_specs(layer)
        + ([tile(), _const((1, D_MODEL))] if head else []),
        out_specs=out_specs,
        args=[x, proj, y0, pooled, wg_glu, b_glu, pool_w, pool_scale, wg_a, wg_b, wg_out] + extra,
        sem=("parallel",) if head is None else ("arbitrary",), carry=carry)


def _big_shapes():
    return dict(w_out=(DEPTH, N_DEV, D_MODEL // N_DEV, D_MODEL), w_branch_a=(DEPTH, N_DEV, WIDTH, D_MODEL // N_DEV),
                w_branch_b=(DEPTH, N_DEV, WIDTH, D_MODEL // N_DEV), ssm_w_glu=(DEPTH, N_DEV, WIDTH // N_DEV, WIDTH),
                w_in=(DEPTH, N_DEV, D_MODEL, WIDTH))


def _mix_bwd(layer, dx_next, proj, y0, pooled, wg_glu, b_glu, pool_w, pool_scale, wg_a, wg_b, wg_out, prev,
             carry=None):
    n_k = N_DEV
    n_prev = 0 if prev is None else len(prev)

    def body(*refs):
        (dx_ref, p_ref, y0_ref, pooled_ref, wglu_ref, bglu_ref, pw_ref, scale_ref, wa_ref, wb_ref,
         wout_ref) = refs[:11]
        (dproj_ref, dy0_ref, dpooled_ref, gwout_ref, gwa_ref, gwb_ref, gwglu_ref, gpw_ref,
         gscale_ref, gbglu_ref) = refs[11 + n_prev:]

        @pl.when(pl.program_id(0) == 0)
        def _():
            for r in (gwout_ref, gwa_ref, gwb_ref, gwglu_ref, gpw_ref, gscale_ref, gbglu_ref):
                r[...] = jnp.zeros_like(r)

        f = _mix_forward(layer, p_ref, y0_ref, pooled_ref, wglu_ref, bglu_ref, pw_ref, scale_ref, wa_ref, wb_ref)
        wglu = wglu_ref[...].reshape(WIDTH, WIDTH)
        wout = wout_ref[...].reshape(D_MODEL, D_MODEL)
        blk = D_MODEL // n_k
        dxb = dx_ref[...].astype(BF16)
        dmerged = _dot_nt(dxb, wout)
        gwout = _dot_tn(f["merged"].astype(BF16), dxb)
        for k in range(n_k):
            gwout_ref[_slot(k)] += gwout[k * blk:(k + 1) * blk, :]
        dma = dmerged * f["sga"]
        dmb = dmerged * f["sgb"]
        dga = dmerged * f["ma"] * f["sga"] * (1.0 - f["sga"])
        dgb = dmerged * f["mb"] * f["sgb"] * (1.0 - f["sgb"])
        dmab = dma.astype(BF16)
        dmbb = dmb.astype(BF16)
        dya = _dot_nt(dmab, wa_ref[...])
        dyb = _dot_nt(dmbb, wb_ref[...])
        gwa = _dot_tn(f["yab"], dmab)
        gwb = _dot_tn(f["ybb"], dmbb)
        for k in range(n_k):
            gwa_ref[_slot(k)] += gwa[:, k * blk:(k + 1) * blk]
            gwb_ref[_slot(k)] += gwb[:, k * blk:(k + 1) * blk]
        zb, szb = f["zb"], f["szb"]
        dzb = dyb * f["ms"] * (szb * (1.0 + zb * (1.0 - szb)))
        dms = dyb * f["silu_zb"]
        gscale_ref[...] += jnp.sum(dms * f["mixed"], axis=0, keepdims=True)
        dmixed = (dms * f["scale"]).astype(BF16)
        pooled = f["pooled"]
        for g in range(len(POOL_WINDOWS)):
            cols = slice(g * POOL_GROUP, (g + 1) * POOL_GROUP)
            dpooled_ref[:, cols] = _dot_nt(dmixed[:, cols], pw_ref[g].astype(BF16))
            gpw_ref[g] += _dot_tn(pooled[:, cols].astype(BF16), dmixed[:, cols])
        za, sza = f["za"], f["sza"]
        dza = dya * f["y2"] * (sza * (1.0 + za * (1.0 - sza)))
        dy2 = dya * f["silu_za"]
        sq = f["sq"]
        dq = dy2 * f["y1"] * sq * (1.0 - sq)
        dqb = dq.astype(BF16)
        dy1 = dy2 * sq + _dot_nt(dqb, wglu)
        gwglu = _dot_tn(f["y1b"], dqb)
        rblk = WIDTH // n_k
        for k in range(n_k):
            gwglu_ref[_slot(k)] += gwglu[k * rblk:(k + 1) * rblk, :]
        gbglu_ref[...] += jnp.sum(dq, axis=0, keepdims=True)
        y0, t = f["y0"], f["t"]
        dgelu = 0.5 * (1.0 + t) + 0.5 * y0 * (1.0 - t * t) * (GELU_C * (1.0 + 3.0 * GELU_A * y0 * y0))
        dy0_ref[...] = dy1 * dgelu
        zeros = jnp.zeros((TILE_M, WIDTH), BF16)
        dproj_ref[:, 0:WIDTH] = zeros
        dproj_ref[:, WIDTH:2 * WIDTH] = dza.astype(BF16)
        dproj_ref[:, 2 * WIDTH:3 * WIDTH] = zeros
        dproj_ref[:, 3 * WIDTH:4 * WIDTH] = dzb.astype(BF16)
        dproj_ref[:, 4 * WIDTH:4 * WIDTH + D_MODEL] = dga.astype(BF16)
        dproj_ref[:, 4 * WIDTH + D_MODEL:] = dgb.astype(BF16)

    tile = lambda w: pl.BlockSpec((TILE_M, w), lambda i: (i, 0))
    shapes = _big_shapes()
    big = ["w_out", "w_branch_a", "w_branch_b", "ssm_w_glu"]
    slab = lambda n: pl.BlockSpec((None,) + shapes[n][1:], lambda i: (layer, 0, 0, 0))
    args = [dx_next, proj, y0, pooled, wg_glu, b_glu, pool_w, pool_scale, wg_a, wg_b, wg_out]
    return _pcall(
        body, name=f"mix_bwd_l{layer}",
        out_shape=(SDS((SEQ, N_IN), BF16), SDS((SEQ, WIDTH), F32), SDS((SEQ, WIDTH), F32))
        + tuple(SDS(shapes[n], F32) for n in big)
        + (SDS((4, POOL_GROUP, POOL_GROUP), F32), SDS((1, WIDTH), F32), SDS((1, WIDTH), F32)),
        grid=(SEQ // TILE_M,),
        in_specs=[tile(D_MODEL), tile(N_IN), tile(WIDTH), tile(WIDTH)] + _mix_weight_specs(layer) + [ANY] * n_prev,
        out_specs=(tile(N_IN), tile(WIDTH), tile(WIDTH)) + tuple(slab(n) for n in big)
        + (_const((4, POOL_GROUP, POOL_GROUP)), _const((1, WIDTH)), _const((1, WIDTH))),
        args=args + list(prev or ()),
        aliases={len(args) + i: 3 + i for i in range(n_prev)},
        sem=("arbitrary",), limit=VMEM_LIMIT_BIG, carry=carry)


def _pool_bwd(layer, dpooled, dproj):
    def body(dp_ref, _, o_ref):
        for gi, win in enumerate(POOL_WINDOWS):
            cols = slice(gi * POOL_GROUP, (gi + 1) * POOL_GROUP)
            dp = dp_ref[:, cols]
            t, count = _pool_counts(win)
            e = dp / count
            acc = e
            k = 1
            while k < win:
                acc = acc + jnp.where(t < SEQ - k, pltpu.roll(acc, SEQ - k, 0), 0.0)
                k *= 2
            o_ref[:, cols] = (acc - dp).astype(BF16)

    return pl.pallas_call(
        body, name=f"pool_bwd_l{layer}",
        out_shape=SDS((SEQ, N_IN), BF16),
        grid=(1,),
        in_specs=[pl.BlockSpec((SEQ, WIDTH), lambda i: (0, 0)), ANY],
        out_specs=pl.BlockSpec((SEQ, WIDTH), lambda i: (0, 2)),
        input_output_aliases={1: 0},
        compiler_params=_cp(("arbitrary",)),
    )(dpooled, dproj)


def _proj_wgrad(layer, x, norm_g, dproj, prev, carry=None):
    tm = SEQ if carry is None else SEQ // 2
    n_prev = 0 if prev is None else 1

    def body(*refs):
        x_ref, g_ref, dp_ref = refs[:3]
        gw_ref, gb_ref, ht_ref = refs[3 + n_prev:]
        n, t = pl.program_id(0), pl.program_id(1)

        @pl.when(t == 0)
        def _():
            gw_ref[...] = jnp.zeros_like(gw_ref)
            gb_ref[...] = jnp.zeros_like(gb_ref)

        @pl.when(n == 0)
        def _():
            for r in range(0, tm, TIME_BLK):
                _, xn = _rms(x_ref[r:r + TIME_BLK, :])
                ht_ref[t, :, r:r + TIME_BLK] = (xn * g_ref[layer:layer + 1, :]).T.astype(BF16)

        dp = dp_ref[...]
        gw_ref[...] += _dot(ht_ref[t], dp)
        gb_ref[...] += jnp.sum(dp.astype(F32), axis=0, keepdims=True)

    return _pcall(
        body, name=f"proj_wgrad_l{layer}",
        out_shape=(SDS(_big_shapes()["w_in"], F32), SDS((1, N_IN), F32)),
        grid=(N_DEV, SEQ // tm),
        in_specs=[pl.BlockSpec((tm, D_MODEL), lambda n, t: (jnp.where(n == 0, t, 0), 0)),
                  _const((DEPTH, D_MODEL)),
                  pl.BlockSpec((tm, WIDTH), lambda n, t: (t, n))] + [ANY] * n_prev,
        out_specs=(pl.BlockSpec((None, None, D_MODEL, WIDTH), lambda n, t: (layer, _slot(n), 0, 0)),
                   pl.BlockSpec((1, WIDTH), lambda n, t: (0, n))),
        scratch_shapes=[pltpu.VMEM((SEQ // tm, D_MODEL, tm), BF16)],
        args=[x, norm_g, dproj] + ([prev] if n_prev else []),
        aliases={3: 0} if n_prev else {}, sem=("arbitrary", "arbitrary"),
        limit=VMEM_LIMIT_BIG if tm == SEQ else VMEM_LIMIT, carry=carry)


def _proj_dgrad(layer, dx_next, x, norm_g, dproj, wg_in, carry=None):
    n_w = len(wg_in)

    def body(dxn_ref, x_ref, g_ref, dp_ref, *refs):
        w_refs, (dx_ref, gg_ref) = refs[:n_w], refs[n_w:]

        @pl.when(pl.program_id(0) == 0)
        def _():
            gg_ref[...] = jnp.zeros_like(gg_ref)

        parts = []
        for w_ref in w_refs:
            part = jnp.zeros((TILE_M, w_ref.shape[1]), F32)
            for k in range(N_DEV):
                part = part + _dot_nt(dp_ref[:, k * WIDTH:(k + 1) * WIDTH], w_ref[k])
            parts.append(part)
        dh = parts[0] if n_w == 1 else jnp.concatenate(parts, axis=1)
        rs, xn = _rms(x_ref[...])
        gg_ref[...] += jnp.sum(dh * xn, axis=0, keepdims=True)
        dxn = dh * g_ref[layer:layer + 1, :]
        dx_ref[...] = dxn_ref[...] + rs * (dxn - xn * jnp.mean(dxn * xn, axis=-1, keepdims=True))

    return _pcall(
        body, name=f"proj_dgrad_l{layer}",
        out_shape=(SDS((SEQ, D_MODEL), F32), SDS((1, D_MODEL), F32)),
        grid=(SEQ // TILE_M,),
        in_specs=[pl.BlockSpec((TILE_M, D_MODEL), lambda i: (i, 0)),
                  pl.BlockSpec((TILE_M, D_MODEL), lambda i: (i, 0)),
                  _const((DEPTH, D_MODEL)),
                  pl.BlockSpec((TILE_M, N_IN), lambda i: (i, 0))] + [_const(w.shape) for w in wg_in],
        out_specs=(pl.BlockSpec((TILE_M, D_MODEL), lambda i: (i, 0)), _const((1, D_MODEL))),
        args=[dx_next, x, norm_g, dproj, *wg_in], sem=("arbitrary",), carry=carry)


def _my_place():
    return lax.axis_index("x"), lax.axis_index("y"), lax.axis_index("c")


def _gather_plan(shards, layer, by_columns=(), rows_of=None):
    n = len(shards)

    def parts(ins, outs, sems):
        send_sems, recv_sems, local_sems = sems
        x, y, c = _my_place()
        chips = [(1 - x, y), (x, 1 - y), (1 - x, 1 - y)]

        def source(t):
            return ins[t].at[layer] if rows_of is None else ins[t].at[layer, pl.ds(*rows_of)]

        def rows(t, place):
            px, py, pc = place
            index = 4 * px + 2 * py + pc
            if t in by_columns:
                width = shards[t].shape[2]
                return outs[t].at[:, pl.ds(pl.multiple_of(index * width, LANES), width)]
            return outs[t].at[index]

        def copy(t, k, block, to, from_src=False):
            return pltpu.make_async_remote_copy(
                src_ref=source(t) if from_src else rows(t, block), dst_ref=rows(t, block),
                send_sem=send_sems.at[7 * t + k], recv_sem=recv_sems.at[7 * t + k], device_id=to,
                device_id_type=MESH)

        def mine(t):
            return pltpu.make_async_copy(source(t), rows(t, (x, y, c)), local_sems.at[t])

        return (x, y, c), chips, copy, mine

    def start(ins, outs, sems):
        me, chips, copy, mine = parts(ins, outs, sems)
        x, y, c = me
        for t in range(n):
            mine(t).start()
            copy(t, 0, me, (x, y, 1 - c), from_src=True).start()
            for j, chip in enumerate(chips):
                copy(t, 1 + j, me, (*chip, c), from_src=True).start()

    def relay(ins, outs, sems):
        me, chips, copy, mine = parts(ins, outs, sems)
        x, y, c = me
        for t in range(n):
            for j, chip in enumerate(chips):
                copy(t, 1 + j, (*chip, c), me).wait_recv()
                copy(t, 4 + j, (*chip, c), (x, y, 1 - c)).start()

    def finish(ins, outs, sems):
        me, chips, copy, mine = parts(ins, outs, sems)
        x, y, c = me
        sibling = (x, y, 1 - c)
        for t in range(n):
            copy(t, 0, sibling, me).wait_recv()
            for j, chip in enumerate(chips):
                copy(t, 4 + j, (*chip, 1 - c), me).wait_recv()
            for k in range(7):
                copy(t, k, me, sibling, from_src=k < 4).wait_send()
            mine(t).wait()

    n_rows = lambda a: a.shape[1] if rows_of is None else rows_of[1]
    out_shape = [SDS((a.shape[1], N_DEV * a.shape[2]) if t in by_columns else (N_DEV, n_rows(a), a.shape[2]), a.dtype)
                 for t, a in enumerate(shards)]
    sems = [pltpu.SemaphoreType.DMA((7 * n,)), pltpu.SemaphoreType.DMA((7 * n,)), pltpu.SemaphoreType.DMA((n,))]
    return _Carried(shards, out_shape, sems, start, finish, relay)


class _Carried:
    def __init__(self, ins, out_shape, sems, start, finish, relay=None):
        self.ins, self.out_shape, self.sems = list(ins), list(out_shape), list(sems)
        self.start, self.finish = start, finish
        self.relay = relay or (lambda ins, outs, sems: None)


def _pcall(body, *, name, grid, in_specs, out_specs, out_shape, args, scratch_shapes=(), aliases=None,
           sem=None, limit=VMEM_LIMIT, carry=None):
    out_shape, out_specs, scratch_shapes = list(out_shape), list(out_specs), list(scratch_shapes)
    n_in, n_out, n_scr = len(args), len(out_shape), len(scratch_shapes)
    if carry is None:
        kern, c_ins, c_out, c_sems = body, [], [], []
    else:
        c_ins, c_out, c_sems = carry.ins, carry.out_shape, carry.sems
        ci, co = len(c_ins), len(c_out)
        steps = tuple(grid)

        def kern(*refs):
            o0 = n_in + ci
            s0 = o0 + n_out + co
            mine = refs[:n_in] + refs[o0:o0 + n_out] + refs[s0:s0 + n_scr]
            theirs = (refs[n_in:o0], refs[o0 + n_out:s0], refs[s0 + n_scr:])
            first = pl.program_id(0) == 0
            last = pl.program_id(0) == steps[0] - 1
            for a in range(1, len(steps)):
                first = jnp.logical_and(first, pl.program_id(a) == 0)
                last = jnp.logical_and(last, pl.program_id(a) == steps[a] - 1)

            @pl.when(first)
            def _():
                carry.start(*theirs)

            body(*mine)

            @pl.when(last)
            def _():
                carry.relay(*theirs)
                carry.finish(*theirs)

        sem = ("arbitrary",) * len(steps)
    res = pl.pallas_call(
        kern, name=name, grid=tuple(grid),
        in_specs=list(in_specs) + [ANY] * len(c_ins),
        out_specs=tuple(out_specs + [ANY] * len(c_out)),
        out_shape=tuple(out_shape + c_out),
        scratch_shapes=scratch_shapes + c_sems,
        input_output_aliases=aliases or {},
        compiler_params=_cp(sem, limit),
    )(*args, *c_ins)
    return res[:n_out], res[n_out:]


def _run_carried(name, carry):
    ci, co = len(carry.ins), len(carry.out_shape)

    def body(*refs):
        parts = (refs[:ci], refs[ci:ci + co], refs[ci + co:])
        carry.start(*parts)
        carry.relay(*parts)
        carry.finish(*parts)

    return pl.pallas_call(
        body, name=name, out_shape=tuple(carry.out_shape),
        in_specs=[ANY] * ci, out_specs=tuple([ANY] * co), scratch_shapes=carry.sems,
    )(*carry.ins)


def _sibling_plan(big, small):
    n = len(big)
    n_copies = 4 * n + len(small)

    def copies(ins, outs, sems):
        send_sems, recv_sems = sems
        x, y, c = _my_place()
        pairs = []
        for t, (_, layer) in enumerate(big):
            for s in range(4):
                pairs.append((ins[t].at[layer, pl.ds(4 * (1 - c) + s, 1)], outs[t].at[pl.ds(s, 1)]))
        pairs += list(zip(ins[n:], outs[n:]))
        return [pltpu.make_async_remote_copy(
            src_ref=src, dst_ref=dst, send_sem=send_sems.at[k], recv_sem=recv_sems.at[k],
            device_id=(x, y, 1 - c), device_id_type=MESH) for k, (src, dst) in enumerate(pairs)]

    def start(ins, outs, sems):
        for cp in copies(ins, outs, sems):
            cp.start()

    def finish(ins, outs, sems):
        for cp in copies(ins, outs, sems):
            cp.wait()

    out_shape = [SDS((4,) + a.shape[2:], a.dtype) for a, _ in big] + [SDS(a.shape, a.dtype) for a in small]
    sems = [pltpu.SemaphoreType.DMA((n_copies,)), pltpu.SemaphoreType.DMA((n_copies,))]
    return _Carried([a for a, _ in big] + list(small), out_shape, sems, start, finish)


def _chips_plan(big, small):
    n, n_small = len(big), len(small)
    max_rows = 512
    parts = [max(1, a.shape[1] // max_rows) for a in big]
    n_copies = 3 * (sum(parts) + n_small)

    def copies(ins, outs, sems, landing):
        send_sems, recv_sems, local_sems = sems
        x, y, c = _my_place()
        my_chip = 2 * x + y
        chips = [(1 - x, y), (x, 1 - y), (1 - x, 1 - y)]
        remote, local = [], []
        for chip in chips:
            to = 2 * chip[0] + chip[1]
            slot = to if landing else my_chip
            pairs = []
            for t in range(n):
                rows_per = big[t].shape[1] // parts[t]
                for p in range(parts[t]):
                    rows = pl.ds(p * rows_per, rows_per)
                    pairs.append((ins[t].at[to, rows], outs[t].at[slot, rows]))
            pairs += [(ins[t], outs[t].at[slot]) for t in range(n, n + n_small)]
            for src, dst in pairs:
                k = len(remote)
                remote.append(pltpu.make_async_remote_copy(
                    src_ref=src, dst_ref=dst, send_sem=send_sems.at[k], recv_sem=recv_sems.at[k],
                    device_id=(*chip, c), device_id_type=MESH))
        for t in range(n):
            local.append(pltpu.make_async_copy(ins[t].at[my_chip], outs[t].at[my_chip], local_sems.at[t]))
        for t in range(n, n + n_small):
            local.append(pltpu.make_async_copy(ins[t], outs[t].at[my_chip], local_sems.at[t]))
        return remote + local

    def start(ins, outs, sems):
        for cp in copies(ins, outs, sems, landing=False):
            cp.start()

    def finish(ins, outs, sems):
        for cp in copies(ins, outs, sems, landing=True):
            cp.wait()

    out_shape = [SDS(a.shape, a.dtype) for a in big] + [SDS((N_CHIP,) + a.shape, a.dtype) for a in small]
    sems = [pltpu.SemaphoreType.DMA((n_copies,)), pltpu.SemaphoreType.DMA((n_copies,)),
            pltpu.SemaphoreType.DMA((n + n_small,))]
    return _Carried(list(big) + list(small), out_shape, sems, start, finish)


def _all_plan(small):
    n = len(small)
    masks = [(m >> 2 & 1, m >> 1 & 1, m & 1) for m in range(1, N_DEV)]

    def copies(ins, outs, sems, landing):
        send_sems, recv_sems, local_sems = sems
        x, y, c = _my_place()
        me = 4 * x + 2 * y + c
        flip = lambda v, bit: 1 - v if bit else v
        remote = []
        for fx, fy, fc in masks:
            peer = (flip(x, fx), flip(y, fy), flip(c, fc))
            slot = 4 * peer[0] + 2 * peer[1] + peer[2] if landing else me
            for t in range(n):
                k = len(remote)
                remote.append(pltpu.make_async_remote_copy(
                    src_ref=ins[t], dst_ref=outs[t].at[slot], send_sem=send_sems.at[k], recv_sem=recv_sems.at[k],
                    device_id=peer, device_id_type=MESH))
        local = [pltpu.make_async_copy(ins[t], outs[t].at[me], local_sems.at[t]) for t in range(n)]
        return remote + local

    def start(ins, outs, sems):
        for cp in copies(ins, outs, sems, landing=False):
            cp.start()

    def finish(ins, outs, sems):
        for cp in copies(ins, outs, sems, landing=True):
            cp.wait()

    out_shape = [SDS((N_DEV,) + a.shape, a.dtype) for a in small]
    sems = [pltpu.SemaphoreType.DMA((7 * n,)), pltpu.SemaphoreType.DMA((7 * n,)), pltpu.SemaphoreType.DMA((n,))]
    return _Carried(list(small), out_shape, sems, start, finish)


def _join(*plans):
    plans = [p for p in plans if p is not None]
    if len(plans) <= 1:
        return plans[0] if plans else None

    def each(fn_name, ins, outs, sems):
        i = o = s = 0
        for p in plans:
            ni, no, ns = len(p.ins), len(p.out_shape), len(p.sems)
            getattr(p, fn_name)(ins[i:i + ni], outs[o:o + no], sems[s:s + ns])
            i, o, s = i + ni, o + no, s + ns

    return _Carried(sum((p.ins for p in plans), []), sum((p.out_shape for p in plans), []),
                    sum((p.sems for p in plans), []),
                    lambda i, o, s: each("start", i, o, s), lambda i, o, s: each("finish", i, o, s),
                    lambda i, o, s: each("relay", i, o, s))


def _row_block(rows, most=256):
    return min(rows, most)


def _add_own(tag, core, gs, layer, gots):
    n = len(gs)

    def body(core_ref, *refs):
        for a_ref, b_ref, o_ref in zip(refs[:n], refs[n:2 * n], refs[2 * n:]):
            o_ref[...] = (a_ref[...] + b_ref[...]).astype(o_ref.dtype)

    mine = lambda a: pl.BlockSpec((None, None) + a.shape[1:], lambda s, core: (layer, 4 * core[0] + s, 0, 0))
    theirs = lambda a: pl.BlockSpec((None,) + a.shape[1:], lambda s, core: (s, 0, 0))
    return pl.pallas_call(
        body, name=f"add_{tag}", out_shape=tuple(SDS(a.shape, BF16) for a in gots),
        grid_spec=pltpu.PrefetchScalarGridSpec(
            num_scalar_prefetch=1, grid=(4,),
            in_specs=[mine(a) for a in gots] + [theirs(a) for a in gots],
            out_specs=tuple(theirs(a) for a in gots)),
        compiler_params=_cp(("parallel",)),
    )(core, *gs, *gots)


def _add_lists(tag, own, got, grid=None, specs=None, dtype=F32):
    n = len(own)

    def body(*refs):
        for a, b, o in zip(refs[:n], refs[n:2 * n], refs[2 * n:]):
            o[...] = (a[...] + b[...]).astype(o.dtype)

    kw = {}
    if grid is not None:
        kw = dict(grid=grid, in_specs=list(specs) * 2, out_specs=tuple(specs),
                  compiler_params=_cp(("parallel",) * len(grid)))
    return pl.pallas_call(
        body, name=f"add_{tag}", out_shape=tuple(SDS(a.shape, dtype) for a in own), **kw)(*own, *got)


def _adamw_math(w, g, m, v):
    m = ADAM_B1 * m + (1.0 - ADAM_B1) * g
    v = ADAM_B2 * v + (1.0 - ADAM_B2) * (g * g)
    m_hat = m / (1.0 - ADAM_B1 ** ADAM_STEP)
    v_hat = v / (1.0 - ADAM_B2 ** ADAM_STEP)
    delta = -ADAM_LR * (m_hat / (jnp.sqrt(v_hat) + ADAM_EPS) + ADAM_WD * w)
    return delta, m, v


def _sum_slots_adamw(tag, slots, w, m, v):
    _, r, c = slots[0].shape
    rb = _row_block(r, most=512)

    def body(s0_ref, s1_ref, w_ref, m_ref, v_ref, g_ref, d_ref, nm_ref, nv_ref):
        first = pl.program_id(1) == 0
        g = _pair_sum([jnp.where(first, s0_ref[k], s1_ref[k]).astype(F32) for k in range(N_CHIP)])
        delta, nm, nv = _adamw_math(w_ref[...], g, m_ref[...], v_ref[...])
        g_ref[...] = g
        d_ref[...] = delta
        nm_ref[...] = nm
        nv_ref[...] = nv

    spec = pl.BlockSpec((None, rb, c), lambda j, l: (l, j, 0))
    sspec = pl.BlockSpec((N_CHIP, rb, c), lambda j, l: (0, j, 0))
    s = SDS((DEPTH, r, c), F32)
    return pl.pallas_call(
        body, name=f"adamw_{tag}", out_shape=(s, s, s, s),
        grid=(r // rb, DEPTH), in_specs=[sspec, sspec, spec, spec, spec], out_specs=(spec, spec, spec, spec),
        compiler_params=_cp(("parallel", "arbitrary")),
    )(*slots, w, m, v)


def _adamw_small(tag, entries, grid=None, sums=()):
    flat_in, in_specs, out_shape, out_specs, layout = [], [], [], [], []
    for slots, w, m, v, slot_spec, w_spec in entries:
        per_layer = isinstance(slots, (list, tuple))
        n_slot = len(slots) if per_layer else 1
        flat_in += (list(slots) if per_layer else [slots]) + [w, m, v]
        in_specs += [slot_spec] * n_slot + [w_spec] * 3
        out_shape += [SDS(w.shape, F32)] * 4
        out_specs += [w_spec] * 4
        layout.append((per_layer, n_slot))
    n_entry_in = len(flat_in)
    flat_in += list(sums)
    out_shape += [SDS(s.shape[1:], F32) for s in sums]
    n_in = len(flat_in)

    def body(*refs):
        for s_ref, o_ref in zip(refs[n_entry_in:n_in], refs[len(refs) - len(sums):]):
            o_ref[...] = _sum_slots(s_ref)
        i, o = 0, n_in
        for per_layer, n_slot in layout:
            s_refs = refs[i:i + n_slot]
            w_ref, m_ref, v_ref = refs[i + n_slot:i + n_slot + 3]
            outs = refs[o:o + 4]
            if per_layer:
                for l, s_ref in enumerate(s_refs):
                    at = (slice(l, l + 1),) if len(w_ref.shape) == 2 else (l,)
                    g = _sum_slots(s_ref)
                    res = (g,) + _adamw_math(w_ref[at], g, m_ref[at], v_ref[at])
                    for o_ref, val in zip(outs, res):
                        o_ref[at] = val
            else:
                g = _sum_slots(s_refs[0])
                res = (g,) + _adamw_math(w_ref[...], g, m_ref[...], v_ref[...])
                for o_ref, val in zip(outs, res):
                    o_ref[...] = val
            i += n_slot + 3
            o += 4

    kw = {}
    if grid is not None:
        kw = dict(grid=grid, in_specs=in_specs, out_specs=tuple(out_specs),
                  compiler_params=_cp(("parallel",) * len(grid)))
    res = pl.pallas_call(body, name=f"adamw_{tag}", out_shape=tuple(out_shape), **kw)(*flat_in)
    return [tuple(res[4 * e:4 * e + 4]) for e in range(len(entries))], res[4 * len(entries):]


def kernel(x, norm_g, w_in, b_in, ssm_log_dt, ssm_lam_re, ssm_lam_im, ssm_b_re, ssm_b_im, ssm_c_re, ssm_c_im, ssm_d, ssm_w_glu, ssm_b_glu, pool_w, pool_scale, w_branch_a, w_branch_b, w_out, final_norm_g, loss_target, m_norm_g, m_w_in, m_b_in, m_ssm_log_dt, m_ssm_lam_re, m_ssm_lam_im, m_ssm_b_re, m_ssm_b_im, m_ssm_c_re, m_ssm_c_im, m_ssm_d, m_ssm_w_glu, m_ssm_b_glu, m_pool_w, m_pool_scale, m_w_branch_a, m_w_branch_b, m_w_out, m_final_norm_g, v_norm_g, v_w_in, v_b_in, v_ssm_log_dt, v_ssm_lam_re, v_ssm_lam_im, v_ssm_b_re, v_ssm_b_im, v_ssm_c_re, v_ssm_c_im, v_ssm_d, v_ssm_w_glu, v_ssm_b_glu, v_pool_w, v_pool_scale, v_w_branch_a, v_w_branch_b, v_w_out, v_final_norm_g):
    weights = dict(norm_g=norm_g, w_in=w_in, b_in=b_in, ssm_log_dt=ssm_log_dt, ssm_lam_re=ssm_lam_re,
                   ssm_lam_im=ssm_lam_im, ssm_b_re=ssm_b_re, ssm_b_im=ssm_b_im, ssm_c_re=ssm_c_re,
                   ssm_c_im=ssm_c_im, ssm_d=ssm_d, ssm_w_glu=ssm_w_glu, ssm_b_glu=ssm_b_glu, pool_w=pool_w,
                   pool_scale=pool_scale, w_branch_a=w_branch_a, w_branch_b=w_branch_b, w_out=w_out,
                   final_norm_g=final_norm_g.reshape(1, D_MODEL))
    mom_m = dict(norm_g=m_norm_g, w_in=m_w_in, b_in=m_b_in, ssm_log_dt=m_ssm_log_dt, ssm_lam_re=m_ssm_lam_re,
                 ssm_lam_im=m_ssm_lam_im, ssm_b_re=m_ssm_b_re, ssm_b_im=m_ssm_b_im, ssm_c_re=m_ssm_c_re,
                 ssm_c_im=m_ssm_c_im, ssm_d=m_ssm_d, ssm_w_glu=m_ssm_w_glu, ssm_b_glu=m_ssm_b_glu,
                 pool_w=m_pool_w, pool_scale=m_pool_scale, w_branch_a=m_w_branch_a, w_branch_b=m_w_branch_b,
                 w_out=m_w_out, final_norm_g=m_final_norm_g.reshape(1, D_MODEL))
    mom_v = dict(norm_g=v_norm_g, w_in=v_w_in, b_in=v_b_in, ssm_log_dt=v_ssm_log_dt, ssm_lam_re=v_ssm_lam_re,
                 ssm_lam_im=v_ssm_lam_im, ssm_b_re=v_ssm_b_re, ssm_b_im=v_ssm_b_im, ssm_c_re=v_ssm_c_re,
                 ssm_c_im=v_ssm_c_im, ssm_d=v_ssm_d, ssm_w_glu=v_ssm_w_glu, ssm_b_glu=v_ssm_b_glu,
                 pool_w=v_pool_w, pool_scale=v_pool_scale, w_branch_a=v_w_branch_a, w_branch_b=v_w_branch_b,
                 w_out=v_w_out, final_norm_g=v_final_norm_g.reshape(1, D_MODEL))
    order = ["norm_g", "w_in", "b_in", "ssm_log_dt", "ssm_lam_re", "ssm_lam_im", "ssm_b_re", "ssm_b_im",
             "ssm_c_re", "ssm_c_im", "ssm_d", "ssm_w_glu", "ssm_b_glu", "pool_w", "pool_scale", "w_branch_a",
             "w_branch_b", "w_out", "final_norm_g"]
    big_names = ["w_in", "ssm_w_glu", "w_branch_a", "w_branch_b", "w_out"]

    log_dt3 = ssm_log_dt.reshape(DEPTH, N_GROUP, 1)
    b_t = lambda a: a.transpose(0, 1, 3, 2)
    for d in (weights, mom_m, mom_v):
        d["ssm_b_re"], d["ssm_b_im"] = b_t(d["ssm_b_re"]), b_t(d["ssm_b_im"])
    bt_re, bt_im = weights["ssm_b_re"], weights["ssm_b_im"]
    w16 = {n: weights[n].astype(BF16) for n in big_names}
    rest = [w16[n] for n in big_names[1:]]
    half = D_MODEL // 2
    wg_in = [None, [None, None]]
    wg_rest = [None, None]
    (abar_re, abar_im, bbt_re, bbt_im), moved = _s5_params(log_dt3, ssm_lam_re, ssm_lam_im, bt_re, bt_im,
                                                           carry=_gather_plan([w16["w_in"]], 0))
    s5_args = (bbt_re, bbt_im, ssm_c_re, ssm_c_im, abar_re, abar_im, ssm_d)
    wg_in[0] = list(moved)
    xs = [x.reshape(SEQ, D_MODEL)]
    saved = []
    for l in range(DEPTH):
        proj, moved = _norm_proj(l, xs[l], norm_g, wg_in[l], b_in,
                                 carry=_gather_plan([w16["w_in"]], 1, rows_of=(0, half)) if l == 0 else None)
        if l == 0:
            (wg_in[1][0],) = moved
        (states, y0), wg_rest[l] = _s5_scan_fwd(l, proj, *s5_args, carry=_gather_plan(rest, l, by_columns=(1, 2)))
        pooled = _pool_fwd(l, proj)
        wg_glu, wg_a, wg_b, wg_out = wg_rest[l]
        last = l == DEPTH - 1
        res, moved = _mix_fwd(
            l, xs[l], proj, y0, pooled, wg_glu, ssm_b_glu, pool_w, pool_scale, wg_a, wg_b, wg_out,
            carry=_gather_plan([w16["w_in"]], 1, rows_of=(half, half)) if l == 0 else None,
            head=(loss_target.reshape(SEQ, D_MODEL), weights["final_norm_g"]) if last else None)
        if l == 0:
            (wg_in[1][1],) = moved
        if last:
            dx, loss_part, g_final = res
        else:
            xs.append(res[0])
        saved.append((proj, states, y0, pooled))

    core = lax.axis_index("c").astype(jnp.int32).reshape(1)
    vec_names = ["norm_g", "b_in", "ssm_d", "ssm_b_glu", "pool_scale", "ssm_log_dt"]
    s5_names = ["ssm_log_dt", "ssm_lam_re", "ssm_lam_im", "ssm_b_re", "ssm_b_im"]
    mat_names = ["pool_w", "ssm_c_re", "ssm_c_im", "ssm_b_re", "ssm_b_im"]
    lane_sparse = ("ssm_c_re", "ssm_c_im", "ssm_b_re", "ssm_b_im")

    def dense(key, a):
        return a.reshape(-1, LANES) if key[0] in lane_sparse else a

    def undense(key, slots):
        return slots.reshape((N_CHIP, N_GROUP, GROUP_W, STATE)) if key[0] in lane_sparse else slots

    def add_small(tag, keys, own, got):
        out = [None] * len(keys)
        whole = [i for i, k in enumerate(keys) if k[0] not in mat_names]
        tiled = [i for i, k in enumerate(keys) if k[0] in mat_names]
        if whole:
            for i, r in zip(whole, _add_lists(f"{tag}_a", [own[i] for i in whole], [got[i] for i in whole])):
                out[i] = r
        if tiled:
            specs = [pl.BlockSpec((1, POOL_GROUP, POOL_GROUP), lambda j: (j, 0, 0)) if keys[i][0] == "pool_w"
                     else pl.BlockSpec((own[i].shape[0] // N_CHUNK, LANES), lambda j: (j, 0)) for i in tiled]
            for i, r in zip(tiled, _add_lists(f"{tag}_b", [own[i] for i in tiled], [got[i] for i in tiled],
                                              grid=(N_CHUNK,), specs=specs, dtype=BF16)):
                out[i] = r
        return out

    sm = {("final_norm_g", None): g_final, ("loss", None): loss_part}
    slots = {}
    grads = dict.fromkeys(big_names)

    class Wave:
        def __init__(self, tag, layer, big, keys):
            self.tag, self.layer, self.big, self.keys = tag, layer, big, keys

        def to_sibling(self):
            self.own = [dense(k, sm[k]) for k in self.keys]
            return _sibling_plan([(grads[n], self.layer) for n in self.big], self.own)

        def add(self, moved):
            nb = len(self.big)
            self.chip_big = list(_add_own(self.tag, core, [grads[n] for n in self.big], self.layer, moved[:nb])
                                 ) if nb else []
            self.chip_small = add_small(self.tag, self.keys, self.own, moved[nb:])

        def to_chips(self, big=None, small=True):
            self.sent = list(self.big if big is None else big), small
            return _chips_plan([self.chip_big[self.big.index(n)] for n in self.sent[0]],
                               self.chip_small if small else [])

        def landed(self, moved):
            names, small = self.sent
            for n, s in zip(names, moved[:len(names)]):
                slots[(n, self.layer)] = s
            if small:
                for k, s in zip(self.keys, moved[len(names):]):
                    slots[k] = undense(k, s)
            return moved[len(names) + (len(self.keys) if small else 0):]

    def s5_param_grads(l, g_abar_re, g_abar_im, g_bbt_re, g_bbt_im):
        g = _s5_params_bwd(l, log_dt3, ssm_lam_re, ssm_lam_im, bt_re, bt_im, g_abar_re, g_abar_im, g_bbt_re, g_bbt_im)
        sm[("ssm_log_dt", l)] = g[0].reshape(1, N_GROUP)
        for n, a in zip(s5_names[1:], g[1:]):
            sm[(n, l)] = a

    small1 = ["b_in", "ssm_d", "ssm_b_glu", "pool_scale", "pool_w", "ssm_c_re", "ssm_c_im"] + s5_names
    w1 = Wave("chip1", 1, list(big_names), [(n, 1) for n in small1] + [("final_norm_g", None), ("loss", None)])
    early = Wave("chip0e", 0, big_names[1:], [("pool_w", 0), ("pool_scale", 0), ("ssm_b_glu", 0)])
    mid = Wave("chip0m", 0, [], [(n, 0) for n in ["ssm_c_re", "ssm_c_im", "ssm_d"] + s5_names] + [("norm_g", 1)])
    late = Wave("chip0l", 0, ["w_in"], [("b_in", 0)])

    mix_prev, gw_in = None, None
    for l in reversed(range(DEPTH)):
        proj, states, y0, pooled = saved[l]
        wg_glu, wg_a, wg_b, wg_out = wg_rest[l]
        res, moved = _mix_bwd(l, dx, proj, y0, pooled, wg_glu, ssm_b_glu, pool_w, pool_scale, wg_a, wg_b, wg_out,
                              mix_prev, carry=None if l == 1 else w1.to_chips(big=["w_in"], small=False))
        if l == 0:
            w1.landed(moved)
        dproj, dy0, dpooled = res[:3]
        mix_prev = list(res[3:7])
        grads["w_out"], grads["w_branch_a"], grads["w_branch_b"], grads["ssm_w_glu"] = mix_prev
        sm[("pool_w", l)], sm[("pool_scale", l)], sm[("ssm_b_glu", l)] = res[7:]
        dproj = _pool_bwd(l, dpooled, dproj)
        carry = None if l == 1 else _join(w1.to_chips(big=big_names[1:]), early.to_sibling())
        res, moved = _s5_scan_bwd(l, dy0, proj, states, *s5_args, dproj, carry=carry)
        if l == 0:
            early.add(w1.landed(moved))
        dproj, g_bbt_re, g_bbt_im, sm[("ssm_c_re", l)], sm[("ssm_c_im", l)], g_abar_re, g_abar_im, sm[("ssm_d", l)] = res
        s5_param_grads(l, g_abar_re, g_abar_im, g_bbt_re, g_bbt_im)
        carry = None if l == 1 else _join(early.to_chips(), mid.to_sibling())
        (gw_in, sm[("b_in", l)]), moved = _proj_wgrad(l, xs[l], norm_g, dproj, gw_in, carry=carry)
        grads["w_in"] = gw_in
        if l == 0:
            mid.add(early.landed(moved))
        carry = w1.to_sibling() if l == 1 else _join(mid.to_chips(), late.to_sibling())
        (dx, sm[("norm_g", l)]), moved = _proj_dgrad(l, dx, xs[l], norm_g, dproj, wg_in[l], carry=carry)
        if l == 1:
            w1.add(moved)
        else:
            late.add(mid.landed(moved))
    grad_x = dx.reshape(1, SEQ, D_MODEL)
    moved = late.landed(_run_carried("exchange_last", _join(late.to_chips(), _all_plan([sm[("norm_g", 0)]]))))
    slots[("norm_g", 0)] = moved[0]

    res = {}
    for n in big_names:
        res[n] = _sum_slots_adamw(n, [slots[(n, l)] for l in range(DEPTH)], weights[n], mom_m[n], mom_v[n])
    per_layer = lambda n: [slots[(n, l)] for l in range(DEPTH)]
    names_a = vec_names + ["ssm_lam_re", "ssm_lam_im"]
    entries_a = [(per_layer(n), weights[n], mom_m[n], mom_v[n], None, None) for n in names_a]
    n = "final_norm_g"
    entries_a.append((slots[(n, None)], weights[n], mom_m[n], mom_v[n], None, None))
    out_a, (loss,) = _adamw_small("small_a", entries_a, sums=[slots[("loss", None)]])
    loss = loss.reshape(())
    for n, r in zip(names_a + ["final_norm_g"], out_a):
        res[n] = r
    res["final_norm_g"] = tuple(a.reshape(D_MODEL) for a in res["final_norm_g"])
    pw_s = pl.BlockSpec((N_CHIP, 1, POOL_GROUP, POOL_GROUP), lambda j: (0, j, 0, 0))
    pw_w = pl.BlockSpec((DEPTH, 1, POOL_GROUP, POOL_GROUP), lambda j: (0, j, 0, 0))
    c_s = pl.BlockSpec((N_CHIP, CH_G, GROUP_W, STATE), lambda j: (0, j, 0, 0))
    c_w = pl.BlockSpec((DEPTH, CH_G, GROUP_W, STATE), lambda j: (0, j, 0, 0))
    entries_b = [(per_layer(n), weights[n], mom_m[n], mom_v[n], pw_s if n == "pool_w" else c_s,
                  pw_w if n == "pool_w" else c_w) for n in mat_names]
    out_b, _ = _adamw_small("small_b", entries_b, grid=(N_CHUNK,))
    for n, r in zip(mat_names, out_b):
        res[n] = tuple(b_t(a) for a in r) if n in ("ssm_b_re", "ssm_b_im") else r

    outs = [loss, grad_x]
    for i in range(4):
        outs += [res[n][i] for n in order]
    return tuple(outs)
```
